```python
import jax, jax.numpy as jnp
from jax import lax
import numpy as np

D_MODEL = 2048
BATCH = 8
SEQ = 2048
DEPTH = 1

CHUNK = 64
D_CONV = D_MODEL // 2
CONV_K = 3
D_POOL = D_MODEL // 2
POOL_WINDOWS = (2, 4, 8, 16)
N_POOL_GROUPS = len(POOL_WINDOWS)
POOL_GROUP_W = D_POOL // N_POOL_GROUPS
POOL_GROUP_OUT = D_MODEL // N_POOL_GROUPS
D_FF = ((8 * D_MODEL // 3 + 255) // 256) * 256
D_IN = 3 * D_CONV + D_POOL + 2 * D_MODEL
EPS = 1e-6

kernel_name = "hybrid_shortconv_multipool_gated_block"


def rmsnorm(x, g):
    xf = x.astype(jnp.float32)
    y = xf * lax.rsqrt(jnp.mean(xf * xf, axis=-1, keepdims=True) + EPS)
    return (y * g.astype(jnp.float32)).astype(x.dtype)


def causal_depthwise_conv(u, w, b):
    S = u.shape[1]
    up = jnp.pad(u, ((0, 0), (CONV_K - 1, 0), (0, 0)))
    out = b
    for k in range(CONV_K):
        out = out + w[k] * up[:, k:k + S]
    return out


def multiscale_pool(v):
    B, S, _ = v.shape
    vg = v.reshape(B, S, N_POOL_GROUPS, POOL_GROUP_W)
    vf = vg.astype(jnp.float32)
    cs = jnp.cumsum(vf, axis=1)
    cs = jnp.concatenate([jnp.zeros_like(cs[:, :1]), cs], axis=1)
    t = jnp.arange(S, dtype=jnp.int32)[:, None]
    win = jnp.asarray(POOL_WINDOWS, dtype=jnp.int32)[None, :]
    lo = jnp.maximum(t + 1 - win, 0)
    cnt = (t + 1 - lo).astype(jnp.float32)
    g_idx = jnp.arange(N_POOL_GROUPS, dtype=jnp.int32)[None, :]
    window_sum = cs[:, 1:] - cs[:, lo, g_idx, :]
    mean = window_sum / cnt[None, :, :, None]
    return (mean - vf).astype(v.dtype)


def swiglu(h, w_gate, w_up, w_down):
    return (jax.nn.silu(h @ w_gate) * (h @ w_up)) @ w_down


def _fwd_setup_inputs(seed: int = 0) -> dict:
    key = jax.random.key(seed)
    ks = jax.random.split(key, 16)
    f32 = jnp.float32
    L = DEPTH
    nrm = lambda k, shape, fan_in: jax.random.normal(k, shape, f32) * (fan_in ** -0.5)
    return {
        "x": jax.random.normal(ks[0], (BATCH, SEQ, D_MODEL), f32),
        "norm1_g": 1.0 + 0.02 * jax.random.normal(ks[1], (L, D_MODEL), f32),
        "w_in": nrm(ks[2], (L, D_MODEL, D_IN), D_MODEL),
        "b_gate": 0.01 * jax.random.normal(ks[3], (L, 2 * D_MODEL), f32),
        "conv_w": nrm(ks[4], (L, CONV_K, D_CONV), CONV_K),
        "conv_b": 0.01 * jax.random.normal(ks[5], (L, D_CONV), f32),
        "w_a_out": nrm(ks[6], (L, D_CONV, D_MODEL), D_CONV),
        "w_pool": nrm(ks[7], (L, N_POOL_GROUPS, POOL_GROUP_W, POOL_GROUP_OUT), POOL_GROUP_W),
        "pool_scale": 1.0 + 0.02 * jax.random.normal(ks[8], (L, D_MODEL), f32),
        "w_o": nrm(ks[9], (L, D_MODEL, D_MODEL), D_MODEL),
        "norm2_g": 1.0 + 0.02 * jax.random.normal(ks[10], (L, D_MODEL), f32),
        "w_ffn_gate": nrm(ks[11], (L, D_MODEL, D_FF), D_MODEL),
        "w_ffn_up": nrm(ks[12], (L, D_MODEL, D_FF), D_MODEL),
        "w_ffn_down": nrm(ks[13], (L, D_FF, D_MODEL), D_FF),
        "final_g": 1.0 + 0.02 * jax.random.normal(ks[14], (D_MODEL,), f32),
    }


def _fwd_reference(x, norm1_g, w_in, b_gate, conv_w, conv_b, w_a_out, w_pool, pool_scale,
              w_o, norm2_g, w_ffn_gate, w_ffn_up, w_ffn_down, final_g):
    B, S, _ = x.shape
    splits = np.cumsum([D_CONV, D_CONV, D_CONV, D_POOL, D_MODEL]).tolist()
    for l in range(DEPTH):
        h = rmsnorm(x, norm1_g[l])
        proj = h @ w_in[l]
        b_a, c_a, v_a, v_b, g_a, g_b = jnp.split(proj, splits, axis=-1)

        u = causal_depthwise_conv(c_a * v_a, conv_w[l], conv_b[l])
        y_a = (b_a * u) @ w_a_out[l]

        p = multiscale_pool(v_b)
        y_b = jnp.einsum("bsgc,gcd->bsgd", p, w_pool[l]).reshape(B, S, D_MODEL)
        y_b = y_b * pool_scale[l]

        gb = b_gate[l]
        merged = jax.nn.sigmoid(g_a + gb[:D_MODEL]) * y_a + jax.nn.sigmoid(g_b + gb[D_MODEL:]) * y_b
        x = x + merged @ w_o[l]

        h2 = rmsnorm(x, norm2_g[l])
        x = x + swiglu(h2, w_ffn_gate[l], w_ffn_up[l], w_ffn_down[l])
    return rmsnorm(x, final_g)


import jax as _jax
import jax.numpy as _jnp

TWIN_FORMAT = 'train_step'
FWD_PARAMS = ['x', 'norm1_g', 'w_in', 'b_gate', 'conv_w', 'conv_b', 'w_a_out', 'w_pool', 'pool_scale', 'w_o', 'norm2_g', 'w_ffn_gate', 'w_ffn_up', 'w_ffn_down', 'final_g']
TWIN_WEIGHTS = ['norm1_g', 'w_in', 'b_gate', 'conv_w', 'conv_b', 'w_a_out', 'w_pool', 'pool_scale', 'w_o', 'norm2_g', 'w_ffn_gate', 'w_ffn_up', 'w_ffn_down', 'final_g']
TWIN_DIFF_INPUT = 'x'
TWIN_INPUTS = ['x', 'norm1_g', 'w_in', 'b_gate', 'conv_w', 'conv_b', 'w_a_out', 'w_pool', 'pool_scale', 'w_o', 'norm2_g', 'w_ffn_gate', 'w_ffn_up', 'w_ffn_down', 'final_g', 'loss_target', 'm_norm1_g', 'm_w_in', 'm_b_gate', 'm_conv_w', 'm_conv_b', 'm_w_a_out', 'm_w_pool', 'm_pool_scale', 'm_w_o', 'm_norm2_g', 'm_w_ffn_gate', 'm_w_ffn_up', 'm_w_ffn_down', 'm_final_g', 'v_norm1_g', 'v_w_in', 'v_b_gate', 'v_conv_w', 'v_conv_b', 'v_w_a_out', 'v_w_pool', 'v_pool_scale', 'v_w_o', 'v_norm2_g', 'v_w_ffn_gate', 'v_w_ffn_up', 'v_w_ffn_down', 'v_final_g']
TWIN_OUTPUTS = ['loss', 'grad_x', 'grad_norm1_g', 'grad_w_in', 'grad_b_gate', 'grad_conv_w', 'grad_conv_b', 'grad_w_a_out', 'grad_w_pool', 'grad_pool_scale', 'grad_w_o', 'grad_norm2_g', 'grad_w_ffn_gate', 'grad_w_ffn_up', 'grad_w_ffn_down', 'grad_final_g', 'delta_norm1_g', 'delta_w_in', 'delta_b_gate', 'delta_conv_w', 'delta_conv_b', 'delta_w_a_out', 'delta_w_pool', 'delta_pool_scale', 'delta_w_o', 'delta_norm2_g', 'delta_w_ffn_gate', 'delta_w_ffn_up', 'delta_w_ffn_down', 'delta_final_g', 'new_m_norm1_g', 'new_m_w_in', 'new_m_b_gate', 'new_m_conv_w', 'new_m_conv_b', 'new_m_w_a_out', 'new_m_w_pool', 'new_m_pool_scale', 'new_m_w_o', 'new_m_norm2_g', 'new_m_w_ffn_gate', 'new_m_w_ffn_up', 'new_m_w_ffn_down', 'new_m_final_g', 'new_v_norm1_g', 'new_v_w_in', 'new_v_b_gate', 'new_v_conv_w', 'new_v_conv_b', 'new_v_w_a_out', 'new_v_w_pool', 'new_v_pool_scale', 'new_v_w_o', 'new_v_norm2_g', 'new_v_w_ffn_gate', 'new_v_w_ffn_up', 'new_v_w_ffn_down', 'new_v_final_g']
TWIN_LEAF_KINDS = {'loss': 'loss', 'grad_x': 'grad_x', 'grad_norm1_g': 'grad_w', 'grad_w_in': 'grad_w', 'grad_b_gate': 'grad_w', 'grad_conv_w': 'grad_w', 'grad_conv_b': 'grad_w', 'grad_w_a_out': 'grad_w', 'grad_w_pool': 'grad_w', 'grad_pool_scale': 'grad_w', 'grad_w_o': 'grad_w', 'grad_norm2_g': 'grad_w', 'grad_w_ffn_gate': 'grad_w', 'grad_w_ffn_up': 'grad_w', 'grad_w_ffn_down': 'grad_w', 'grad_final_g': 'grad_w', 'delta_norm1_g': 'delta_w', 'delta_w_in': 'delta_w', 'delta_b_gate': 'delta_w', 'delta_conv_w': 'delta_w', 'delta_conv_b': 'delta_w', 'delta_w_a_out': 'delta_w', 'delta_w_pool': 'delta_w', 'delta_pool_scale': 'delta_w', 'delta_w_o': 'delta_w', 'delta_norm2_g': 'delta_w', 'delta_w_ffn_gate': 'delta_w', 'delta_w_ffn_up': 'delta_w', 'delta_w_ffn_down': 'delta_w', 'delta_final_g': 'delta_w', 'new_m_norm1_g': 'new_m', 'new_m_w_in': 'new_m', 'new_m_b_gate': 'new_m', 'new_m_conv_w': 'new_m', 'new_m_conv_b': 'new_m', 'new_m_w_a_out': 'new_m', 'new_m_w_pool': 'new_m', 'new_m_pool_scale': 'new_m', 'new_m_w_o': 'new_m', 'new_m_norm2_g': 'new_m', 'new_m_w_ffn_gate': 'new_m', 'new_m_w_ffn_up': 'new_m', 'new_m_w_ffn_down': 'new_m', 'new_m_final_g': 'new_m', 'new_v_norm1_g': 'new_v', 'new_v_w_in': 'new_v', 'new_v_b_gate': 'new_v', 'new_v_conv_w': 'new_v', 'new_v_conv_b': 'new_v', 'new_v_w_a_out': 'new_v', 'new_v_w_pool': 'new_v', 'new_v_pool_scale': 'new_v', 'new_v_w_o': 'new_v', 'new_v_norm2_g': 'new_v', 'new_v_w_ffn_gate': 'new_v', 'new_v_w_ffn_up': 'new_v', 'new_v_w_ffn_down': 'new_v', 'new_v_final_g': 'new_v'}


def _forward(args):
    return _fwd_reference(*[args[k] for k in FWD_PARAMS])


def _output_shape():
    out = _jax.eval_shape(lambda: _forward(_fwd_setup_inputs(0)))
    return out.shape, out.dtype

N_MICROBATCH = 1
ADAM_LR = 0.001
ADAM_B1 = 0.9
ADAM_B2 = 0.999
ADAM_EPS = 1e-08
ADAM_WD = 0.01
ADAM_STEP = 10
PER_EXAMPLE_BATCH_AXIS = {'x': 0, 'loss_target': 0}
SHARED_INPUTS = []
_WEIGHT_DTYPES = {'norm1_g': _jnp.float32, 'w_in': _jnp.float32, 'b_gate': _jnp.float32, 'conv_w': _jnp.float32, 'conv_b': _jnp.float32, 'w_a_out': _jnp.float32, 'w_pool': _jnp.float32, 'pool_scale': _jnp.float32, 'w_o': _jnp.float32, 'norm2_g': _jnp.float32, 'w_ffn_gate': _jnp.float32, 'w_ffn_up': _jnp.float32, 'w_ffn_down': _jnp.float32, 'final_g': _jnp.float32}
MOMENT_SCALE = {'norm1_g': 6.122432e-02, 'w_in': 3.032267e-02, 'b_gate': 1.119372e-02, 'conv_w': 4.354813e-02, 'conv_b': 4.293404e-02, 'w_a_out': 3.036398e-02, 'w_pool': 2.642535e-02, 'pool_scale': 2.657408e-02, 'w_o': 4.032277e-02, 'norm2_g': 3.971730e-02, 'w_ffn_gate': 1.699461e-02, 'w_ffn_up': 1.644981e-02, 'w_ffn_down': 2.724476e-02, 'final_g': 7.991512e+00}


def _to_microbatches(a, axis):
    t = _jnp.moveaxis(a, axis, 0)
    t = t.reshape((N_MICROBATCH, t.shape[0] // N_MICROBATCH) + t.shape[1:])
    return _jnp.moveaxis(t, 1, axis + 1)


def setup_inputs(seed: int = 0) -> dict:
    inp = _fwd_setup_inputs(seed)
    key = _jax.random.fold_in(_jax.random.key(seed), 7919)
    shape, _ = _output_shape()
    out = dict(inp)
    out["loss_target"] = _jax.random.normal(_jax.random.fold_in(key, 0), shape, _jnp.float32)
    for i, name in enumerate(TWIN_WEIGHTS):
        w = inp[name].astype(_jnp.float32)
        if MOMENT_SCALE is None:
            s = _jnp.sqrt(_jnp.mean(_jnp.square(w)) + 1e-30)
        else:
            s = MOMENT_SCALE[name]
        km, kv = _jax.random.split(_jax.random.fold_in(key, i + 1))
        out[name] = w
        out["m_" + name] = s * _jax.random.normal(km, w.shape, _jnp.float32)
        out["v_" + name] = (s * s) * _jax.random.uniform(kv, w.shape, _jnp.float32, 0.5, 1.5)
    if N_MICROBATCH > 1:
        for name, axis in PER_EXAMPLE_BATCH_AXIS.items():
            out[name] = _to_microbatches(out[name], axis)
    return {'x': out['x'], 'norm1_g': out['norm1_g'], 'w_in': out['w_in'], 'b_gate': out['b_gate'], 'conv_w': out['conv_w'], 'conv_b': out['conv_b'], 'w_a_out': out['w_a_out'], 'w_pool': out['w_pool'], 'pool_scale': out['pool_scale'], 'w_o': out['w_o'], 'norm2_g': out['norm2_g'], 'w_ffn_gate': out['w_ffn_gate'], 'w_ffn_up': out['w_ffn_up'], 'w_ffn_down': out['w_ffn_down'], 'final_g': out['final_g'], 'loss_target': out['loss_target'], 'm_norm1_g': out['m_norm1_g'], 'm_w_in': out['m_w_in'], 'm_b_gate': out['m_b_gate'], 'm_conv_w': out['m_conv_w'], 'm_conv_b': out['m_conv_b'], 'm_w_a_out': out['m_w_a_out'], 'm_w_pool': out['m_w_pool'], 'm_pool_scale': out['m_pool_scale'], 'm_w_o': out['m_w_o'], 'm_norm2_g': out['m_norm2_g'], 'm_w_ffn_gate': out['m_w_ffn_gate'], 'm_w_ffn_up': out['m_w_ffn_up'], 'm_w_ffn_down': out['m_w_ffn_down'], 'm_final_g': out['m_final_g'], 'v_norm1_g': out['v_norm1_g'], 'v_w_in': out['v_w_in'], 'v_b_gate': out['v_b_gate'], 'v_conv_w': out['v_conv_w'], 'v_conv_b': out['v_conv_b'], 'v_w_a_out': out['v_w_a_out'], 'v_w_pool': out['v_w_pool'], 'v_pool_scale': out['v_pool_scale'], 'v_w_o': out['v_w_o'], 'v_norm2_g': out['v_norm2_g'], 'v_w_ffn_gate': out['v_w_ffn_gate'], 'v_w_ffn_up': out['v_w_ffn_up'], 'v_w_ffn_down': out['v_w_ffn_down'], 'v_final_g': out['v_final_g']}


def _loss(weights, diff, rest, loss_target):
    with _jax.named_scope("forward"):
        args = {**rest, TWIN_DIFF_INPUT: diff, **{k: w.astype(_WEIGHT_DTYPES[k]) for k, w in weights.items()}}
        y = _forward(args)
    with _jax.named_scope("loss_head"):
        err = _jnp.square(y.astype(_jnp.float32) - loss_target)
        return 0.5 * _jnp.sum(_jnp.mean(err, axis=-1)) if err.ndim else 0.5 * err


def _adamw(w, g, m, v):
    m = ADAM_B1 * m + (1.0 - ADAM_B1) * g
    v = ADAM_B2 * v + (1.0 - ADAM_B2) * _jnp.square(g)
    m_hat = m / (1.0 - ADAM_B1 ** ADAM_STEP)
    v_hat = v / (1.0 - ADAM_B2 ** ADAM_STEP)
    delta = -ADAM_LR * (m_hat / (_jnp.sqrt(v_hat) + ADAM_EPS) + ADAM_WD * w)
    return delta, m, v


def reference(x, norm1_g, w_in, b_gate, conv_w, conv_b, w_a_out, w_pool, pool_scale, w_o, norm2_g, w_ffn_gate, w_ffn_up, w_ffn_down, final_g, loss_target, m_norm1_g, m_w_in, m_b_gate, m_conv_w, m_conv_b, m_w_a_out, m_w_pool, m_pool_scale, m_w_o, m_norm2_g, m_w_ffn_gate, m_w_ffn_up, m_w_ffn_down, m_final_g, v_norm1_g, v_w_in, v_b_gate, v_conv_w, v_conv_b, v_w_a_out, v_w_pool, v_pool_scale, v_w_o, v_norm2_g, v_w_ffn_gate, v_w_ffn_up, v_w_ffn_down, v_final_g):
    given = dict(x=x, norm1_g=norm1_g, w_in=w_in, b_gate=b_gate, conv_w=conv_w, conv_b=conv_b, w_a_out=w_a_out, w_pool=w_pool, pool_scale=pool_scale, w_o=w_o, norm2_g=norm2_g, w_ffn_gate=w_ffn_gate, w_ffn_up=w_ffn_up, w_ffn_down=w_ffn_down, final_g=final_g, loss_target=loss_target, m_norm1_g=m_norm1_g, m_w_in=m_w_in, m_b_gate=m_b_gate, m_conv_w=m_conv_w, m_conv_b=m_conv_b, m_w_a_out=m_w_a_out, m_w_pool=m_w_pool, m_pool_scale=m_pool_scale, m_w_o=m_w_o, m_norm2_g=m_norm2_g, m_w_ffn_gate=m_w_ffn_gate, m_w_ffn_up=m_w_ffn_up, m_w_ffn_down=m_w_ffn_down, m_final_g=m_final_g, v_norm1_g=v_norm1_g, v_w_in=v_w_in, v_b_gate=v_b_gate, v_conv_w=v_conv_w, v_conv_b=v_conv_b, v_w_a_out=v_w_a_out, v_w_pool=v_w_pool, v_pool_scale=v_pool_scale, v_w_o=v_w_o, v_norm2_g=v_norm2_g, v_w_ffn_gate=v_w_ffn_gate, v_w_ffn_up=v_w_ffn_up, v_w_ffn_down=v_w_ffn_down, v_final_g=v_final_g)
    weights = {n: given[n] for n in TWIN_WEIGHTS}
    shared = {n: given[n] for n in SHARED_INPUTS}
    per_example = {n: given[n] for n in ['x']}
    grad_fn = _jax.value_and_grad(_loss, argnums=(0, 1))

    def one_microbatch(ex, loss_target):
        ex = dict(ex)
        diff = ex.pop(TWIN_DIFF_INPUT)
        return grad_fn(weights, diff, {**shared, **ex}, loss_target)

    if N_MICROBATCH == 1:
        loss, (grad_w, grad_x) = one_microbatch(per_example, given["loss_target"])
    else:
        def body(carry, xs):
            loss_sum, grad_sum = carry
            l_k, (gw_k, gx_k) = one_microbatch(xs[0], xs[1])
            with _jax.named_scope("update"):
                return (loss_sum + l_k, _jax.tree.map(_jnp.add, grad_sum, gw_k)), gx_k

        init = (_jnp.zeros((), _jnp.float32), _jax.tree.map(_jnp.zeros_like, weights))
        (loss, grad_w), grad_x = _jax.lax.scan(body, init, (per_example, given["loss_target"]))
    with _jax.named_scope("update"):
        delta_w, new_m, new_v = {}, {}, {}
        for n in TWIN_WEIGHTS:
            delta_w[n], new_m[n], new_v[n] = _adamw(weights[n], grad_w[n], given["m_" + n], given["v_" + n])
    return (loss, grad_x, *[grad_w[n] for n in TWIN_WEIGHTS], *[delta_w[n] for n in TWIN_WEIGHTS],
            *[new_m[n] for n in TWIN_WEIGHTS], *[new_v[n] for n in TWIN_WEIGHTS])
```

```python
import functools

import jax
import jax.numpy as jnp
from jax import lax
from jax.experimental import pallas as pl
from jax.experimental.pallas import tpu as pltpu

F32, BF16 = jnp.float32, jnp.bfloat16
MESH = pl.DeviceIdType.MESH
ANY = pl.BlockSpec(memory_space=pl.ANY)
VMEM = pl.BlockSpec(memory_space=pltpu.VMEM)

EPS = 1e-6
POOL_WINDOWS = (2, 4, 8, 16)
ADAM_LR, ADAM_B1, ADAM_B2, ADAM_EPS, ADAM_WD, ADAM_STEP = 0.001, 0.9, 0.999, 1e-08, 0.01, 10

V7X_VMEM_BYTES = 64 * 1024 * 1024
VMEM_LIMIT = V7X_VMEM_BYTES * 3 // 4
LANES = 128
N_CHIPS = 4
N_DEV = 8

_DIMS = {
    "nn": (((1,), (0,)), ((), ())),
    "nt": (((1,), (1,)), ((), ())),
    "tn": (((0,), (0,)), ((), ())),
}


def _cp(sem):
    return pltpu.CompilerParams(dimension_semantics=sem, vmem_limit_bytes=VMEM_LIMIT)


def _mesh_pos():
    return lax.axis_index("x"), lax.axis_index("y"), lax.axis_index("c")


def _mm(name, pairs, *, mode, grid, out_shape, o_spec, nk=1, kaxis=None, add=None):
    npair = len(pairs)
    has_add = add is not None

    def body(*refs):
        ab = refs[: 2 * npair]
        pos = 2 * npair
        add_ref = refs[pos] if has_add else None
        pos += int(has_add)
        o_ref = refs[pos]
        acc_ref = refs[pos + 1] if nk > 1 else None
        d = None
        for p in range(npair):
            t = lax.dot_general(ab[2 * p][...], ab[2 * p + 1][...], _DIMS[mode], preferred_element_type=F32)
            d = t if d is None else d + t
        if nk == 1:
            if has_add:
                d = d + add_ref[...].astype(F32)
            o_ref[...] = d.astype(o_ref.dtype)
        else:
            k = pl.program_id(kaxis)

            @pl.when(k == 0)
            def _():
                acc_ref[...] = d

            @pl.when(k > 0)
            def _():
                acc_ref[...] += d

            @pl.when(k == nk - 1)
            def _():
                r = acc_ref[...]
                if has_add:
                    r = r + add_ref[...].astype(F32)
                o_ref[...] = r.astype(o_ref.dtype)

    args, specs = [], []
    for a, a_spec, b, b_spec in pairs:
        args += [a, b]
        specs += [a_spec, b_spec]
    if has_add:
        args.append(add[0])
        specs.append(add[1])
    scratch = []
    if nk > 1:
        blk = [d for d in o_spec.block_shape if d is not None]
        scratch = [pltpu.VMEM(tuple(blk), F32)]
    sem = tuple("arbitrary" if (nk > 1 and ax == kaxis) else "parallel" for ax in range(len(grid)))
    return pl.pallas_call(
        body, name=name, grid=grid, in_specs=specs, out_specs=o_spec, out_shape=out_shape,
        scratch_shapes=scratch, compiler_params=_cp(sem),
    )(*args)


def _tile(n, pref):
    if n <= pref:
        return n
    for t in range(pref, 0, -LANES):
        if t % LANES == 0 and n % t == 0:
            return t
    raise ValueError(f"no tile for {n}")


def _mm_nn(name, a, b, out_dtype, add=None, tk=None):
    m, kk = a.shape
    n = b.shape[1]
    tm, tn = _tile(m, 1024), _tile(n, 512)
    out_shape = jax.ShapeDtypeStruct((m, n), out_dtype)
    if tk is None or tk == kk:
        grid = (m // tm, n // tn)
        pairs = [(a, pl.BlockSpec((tm, kk), lambda i, j: (i, 0)), b, pl.BlockSpec((kk, tn), lambda i, j: (0, j)))]
        o_spec = pl.BlockSpec((tm, tn), lambda i, j: (i, j))
        add_ = None if add is None else (add, pl.BlockSpec((tm, tn), lambda i, j: (i, j)))
        return _mm(name, pairs, mode="nn", grid=grid, out_shape=out_shape, o_spec=o_spec, add=add_)
    tn = _tile(n, 1024)
    nk = kk // tk
    grid = (m // tm, n // tn, nk)
    pairs = [(a, pl.BlockSpec((tm, tk), lambda i, j, k: (i, k)), b, pl.BlockSpec((tk, tn), lambda i, j, k: (k, j)))]
    o_spec = pl.BlockSpec((tm, tn), lambda i, j, k: (i, j))
    add_ = None if add is None else (add, pl.BlockSpec((tm, tn), lambda i, j, k: (i, j)))
    return _mm(name, pairs, mode="nn", grid=grid, out_shape=out_shape, o_spec=o_spec, nk=nk, kaxis=2, add=add_)


def _mm_nt(name, abs_, out_dtype, tk):
    m, kk = abs_[0][0].shape
    n = abs_[0][1].shape[0]
    tm = _tile(m, 1024)
    nk = kk // tk
    tn = _tile(n, 512 if nk == 1 else 1024)
    out_shape = jax.ShapeDtypeStruct((m, n), out_dtype)
    if nk == 1:
        grid = (m // tm, n // tn)
        pairs = [(a, pl.BlockSpec((tm, kk), lambda i, j: (i, 0)), b, pl.BlockSpec((tn, kk), lambda i, j: (j, 0)))
                 for a, b in abs_]
        o_spec = pl.BlockSpec((tm, tn), lambda i, j: (i, j))
        return _mm(name, pairs, mode="nt", grid=grid, out_shape=out_shape, o_spec=o_spec)
    grid = (m // tm, n // tn, nk)
    pairs = [(a, pl.BlockSpec((tm, tk), lambda i, j, k: (i, k)), b, pl.BlockSpec((tn, tk), lambda i, j, k: (j, k)))
             for a, b in abs_]
    o_spec = pl.BlockSpec((tm, tn), lambda i, j, k: (i, j))
    return _mm(name, pairs, mode="nt", grid=grid, out_shape=out_shape, o_spec=o_spec, nk=nk, kaxis=2)


def _mm_tn(name, a, b, out_dtype):
    t, m = a.shape
    n = b.shape[1]
    tm, tn = _tile(m, 512), _tile(n, 2048)
    if n > m:
        grid = (n // tn, m // tm)
        a_map, b_map, o_map = (lambda j, i: (0, i)), (lambda j, i: (0, j)), (lambda j, i: (i, j))
    else:
        grid = (m // tm, n // tn)
        a_map, b_map, o_map = (lambda i, j: (0, i)), (lambda i, j: (0, j)), (lambda i, j: (i, j))
    pairs = [(a, pl.BlockSpec((t, tm), a_map), b, pl.BlockSpec((t, tn), b_map))]
    o_spec = pl.BlockSpec((tm, tn), o_map)
    return _mm(name, pairs, mode="tn", grid=grid, out_shape=jax.ShapeDtypeStruct((m, n), out_dtype), o_spec=o_spec)


def _gmm_nn(name, p, w, out_dtype):
    t = p.shape[0]
    g, cg, dg = w.shape
    tm = _tile(t, 1024)
    pairs = [(p, pl.BlockSpec((tm, cg), lambda i, j: (i, j)), w, pl.BlockSpec((None, cg, dg), lambda i, j: (j, 0, 0)))]
    o_spec = pl.BlockSpec((tm, dg), lambda i, j: (i, j))
    return _mm(name, pairs, mode="nn", grid=(t // tm, g), out_shape=jax.ShapeDtypeStruct((t, g * dg), out_dtype),
               o_spec=o_spec)


def _gmm_nt(name, dy, w, out_dtype):
    t = dy.shape[0]
    g, cg, dg = w.shape
    tm = _tile(t, 1024)
    pairs = [(dy, pl.BlockSpec((tm, dg), lambda i, j: (i, j)), w, pl.BlockSpec((None, cg, dg), lambda i, j: (j, 0, 0)))]
    o_spec = pl.BlockSpec((tm, cg), lambda i, j: (i, j))
    return _mm(name, pairs, mode="nt", grid=(t // tm, g), out_shape=jax.ShapeDtypeStruct((t, g * cg), out_dtype),
               o_spec=o_spec)


def _gmm_tn(name, p, dy, g, out_dtype):
    t = p.shape[0]
    cg, dg = p.shape[1] // g, dy.shape[1] // g
    pairs = [(p, pl.BlockSpec((t, cg), lambda j: (0, j)), dy, pl.BlockSpec((t, dg), lambda j: (0, j)))]
    o_spec = pl.BlockSpec((None, cg, dg), lambda j: (j, 0, 0))
    return _mm(name, pairs, mode="tn", grid=(g,), out_shape=jax.ShapeDtypeStruct((g, cg, dg), out_dtype), o_spec=o_spec)


ROW_TILE = 256


def _rows(t):
    return _tile8(t, ROW_TILE)


def _tile8(n, pref):
    if n <= pref:
        return n
    for t in range(pref, 0, -8):
        if n % t == 0:
            return t
    raise ValueError(f"no row tile for {n}")


def _cast_bf16(name, w):
    _, r, c = w.shape
    tr = _tile8(r, 512)
    spec = pl.BlockSpec((None, tr, c), lambda h, i: (h, i, 0))

    def body(w_ref, o_ref):
        o_ref[...] = w_ref[...].astype(BF16)

    return pl.pallas_call(body, name=name, grid=(2, r // tr), in_specs=[spec], out_specs=spec,
                          out_shape=jax.ShapeDtypeStruct(w.shape, BF16), compiler_params=_cp(("parallel", "parallel")))(w)


def _rms_fwd(name, x, g):
    t, d = x.shape
    tm = _rows(t)

    def body(x_ref, g_ref, h_ref):
        xf = x_ref[...]
        r = lax.rsqrt(jnp.mean(xf * xf, axis=-1, keepdims=True) + EPS)
        h_ref[...] = (xf * r * g_ref[...]).astype(BF16)

    return pl.pallas_call(
        body, name=name, grid=(t // tm,),
        in_specs=[pl.BlockSpec((tm, d), lambda i: (i, 0)), pl.BlockSpec((1, d), lambda i: (0, 0))],
        out_specs=pl.BlockSpec((tm, d), lambda i: (i, 0)), out_shape=jax.ShapeDtypeStruct((t, d), BF16),
        compiler_params=_cp(("parallel",)),
    )(x, g)


def _rms_bwd(name, x, g, dh, dres, want_bf16):
    t, d = x.shape
    tm = _rows(t)

    def body(x_ref, g_ref, dh_ref, dres_ref, dx_ref, *rest):
        dg_ref = rest[-1]
        xf = x_ref[...]
        r = lax.rsqrt(jnp.mean(xf * xf, axis=-1, keepdims=True) + EPS)
        xh = xf * r
        dhf = dh_ref[...]
        dxh = dhf * g_ref[...]
        m = jnp.mean(dxh * xh, axis=-1, keepdims=True)
        dx = dres_ref[...] + r * (dxh - xh * m)
        dx_ref[...] = dx
        if want_bf16:
            rest[0][...] = dx.astype(BF16)

        @pl.when(pl.program_id(0) == 0)
        def _():
            dg_ref[...] = jnp.zeros_like(dg_ref)

        dg_ref[...] += jnp.sum(dhf * xh, axis=0, keepdims=True)

    row = pl.BlockSpec((tm, d), lambda i: (i, 0))
    vec = pl.BlockSpec((1, d), lambda i: (0, 0))
    out_specs = [row] + ([row] if want_bf16 else []) + [vec]
    out_shape = ([jax.ShapeDtypeStruct((t, d), F32)] + ([jax.ShapeDtypeStruct((t, d), BF16)] if want_bf16 else [])
                 + [jax.ShapeDtypeStruct((1, d), F32)])
    return pl.pallas_call(body, name=name, grid=(t // tm,), in_specs=[row, vec, row, row], out_specs=out_specs,
                          out_shape=out_shape, compiler_params=_cp(("arbitrary",)))(x, g, dh, dres)


def _final_bwd(name, x3, gf, tgt):
    t, d = x3.shape
    tm = _rows(t)

    def body(x_ref, g_ref, t_ref, dx_ref, dxb_ref, dg_ref, lc_ref):
        xf = x_ref[...]
        g = g_ref[...]
        r = lax.rsqrt(jnp.mean(xf * xf, axis=-1, keepdims=True) + EPS)
        xh = xf * r
        diff = xh * g - t_ref[...]
        dy = diff * (1.0 / d)
        dxh = dy * g
        m = jnp.mean(dxh * xh, axis=-1, keepdims=True)
        dx = r * (dxh - xh * m)
        dx_ref[...] = dx
        dxb_ref[...] = dx.astype(BF16)

        @pl.when(pl.program_id(0) == 0)
        def _():
            dg_ref[...] = jnp.zeros_like(dg_ref)
            lc_ref[...] = jnp.zeros_like(lc_ref)

        dg_ref[...] += jnp.sum(dy * xh, axis=0, keepdims=True)
        lc_ref[...] += jnp.sum(diff * diff, axis=0, keepdims=True) * (0.5 / d)

    row = pl.BlockSpec((tm, d), lambda i: (i, 0))
    vec = pl.BlockSpec((1, d), lambda i: (0, 0))
    return pl.pallas_call(
        body, name=name, grid=(t // tm,), in_specs=[row, vec, row], out_specs=[row, row, vec, vec],
        out_shape=[jax.ShapeDtypeStruct((t, d), F32), jax.ShapeDtypeStruct((t, d), BF16),
                   jax.ShapeDtypeStruct((1, d), F32), jax.ShapeDtypeStruct((1, d), F32)],
        compiler_params=_cp(("arbitrary",)),
    )(x3, gf, tgt)


def _shift_down(v, k, t_idx):
    return jnp.where(t_idx >= k, pltpu.roll(v, k, 0), 0.0)


def _shift_up(v, k, t_idx):
    n = v.shape[0]
    return jnp.where(t_idx < n - k, pltpu.roll(v, n - k, 0), 0.0)


def _window_sums(v, shift, t_idx, grp):
    s = v + shift(v, 1, t_idx)
    out = s
    for lvl in range(1, len(POOL_WINDOWS)):
        s = s + shift(s, 1 << lvl, t_idx)
        out = jnp.where(grp >= lvl, s, out)
    return out


def _window_count(t_idx, grp):
    return jnp.minimum(t_idx + 1, jnp.left_shift(2, grp)).astype(F32)


MIX_COLS = 128


def _mixer_fwd(name, proj, cw, cb, n_conv, n_groups):
    t = proj.shape[0]
    nb = n_conv // MIX_COLS
    per_group = n_conv // n_groups // MIX_COLS

    def body(ba_ref, ca_ref, va_ref, vb_ref, cw_ref, cb_ref, z_ref, p_ref):
        t_idx = lax.broadcasted_iota(jnp.int32, (t, MIX_COLS), 0)
        q = ca_ref[...].astype(F32) * va_ref[...].astype(F32)
        w = cw_ref[...]
        u = cb_ref[...] + w[0:1] * _shift_down(q, 2, t_idx) + w[1:2] * _shift_down(q, 1, t_idx) + w[2:3] * q
        z_ref[...] = (ba_ref[...].astype(F32) * u).astype(BF16)
        grp = pl.program_id(0) // per_group
        v = vb_ref[...].astype(F32)
        p_ref[...] = (_window_sums(v, _shift_down, t_idx, grp) / _window_count(t_idx, grp) - v).astype(BF16)

    col = lambda s: pl.BlockSpec((t, MIX_COLS), lambda j: (0, s * nb + j))
    return pl.pallas_call(
        body, name=name, grid=(nb,),
        in_specs=[col(0), col(1), col(2), col(3), pl.BlockSpec((3, MIX_COLS), lambda j: (0, j)),
                  pl.BlockSpec((1, MIX_COLS), lambda j: (0, j))],
        out_specs=[col(0), col(0)],
        out_shape=[jax.ShapeDtypeStruct((t, n_conv), BF16), jax.ShapeDtypeStruct((t, n_conv), BF16)],
        compiler_params=_cp(("parallel",)),
    )(proj, proj, proj, proj, cw, cb)


def _mixer_bwd(name, dz, dp, proj, cw, cb, dproj, n_conv, n_groups):
    t = proj.shape[0]
    nb = n_conv // MIX_COLS
    per_group = n_conv // n_groups // MIX_COLS

    def body(dz_ref, dp_ref, ba_ref, ca_ref, va_ref, cw_ref, cb_ref, _, o_ref, dcw_ref, dcb_ref, scr):
        s = pl.program_id(1)

        @pl.when(s == 0)
        def _():
            t_idx = lax.broadcasted_iota(jnp.int32, (t, MIX_COLS), 0)
            ca, va = ca_ref[...].astype(F32), va_ref[...].astype(F32)
            q = ca * va
            q1, q2 = _shift_down(q, 1, t_idx), _shift_down(q, 2, t_idx)
            w = cw_ref[...]
            u = cb_ref[...] + w[0:1] * q2 + w[1:2] * q1 + w[2:3] * q
            dzf = dz_ref[...].astype(F32)
            du = dzf * ba_ref[...].astype(F32)
            scr[0] = (dzf * u).astype(BF16)
            dq = w[2:3] * du + w[1:2] * _shift_up(du, 1, t_idx) + w[0:1] * _shift_up(du, 2, t_idx)
            scr[1] = (dq * va).astype(BF16)
            scr[2] = (dq * ca).astype(BF16)
            dcb_ref[...] = jnp.sum(du, axis=0, keepdims=True)
            dcw_ref[0:1, :] = jnp.sum(du * q2, axis=0, keepdims=True)
            dcw_ref[1:2, :] = jnp.sum(du * q1, axis=0, keepdims=True)
            dcw_ref[2:3, :] = jnp.sum(du * q, axis=0, keepdims=True)
            grp = pl.program_id(0) // per_group
            dpf = dp_ref[...].astype(F32)
            e = dpf / _window_count(t_idx, grp)
            scr[3] = (_window_sums(e, _shift_up, t_idx, grp) - dpf).astype(BF16)

        o_ref[...] = scr[s]

    col = lambda c: pl.BlockSpec((t, MIX_COLS), lambda j, s: (0, c * nb + j))
    own = pl.BlockSpec((t, MIX_COLS), lambda j, s: (0, j))
    return pl.pallas_call(
        body, name=name, grid=(nb, 4),
        in_specs=[own, own, col(0), col(1), col(2), pl.BlockSpec((3, MIX_COLS), lambda j, s: (0, j)),
                  pl.BlockSpec((1, MIX_COLS), lambda j, s: (0, j)), ANY],
        out_specs=[pl.BlockSpec((t, MIX_COLS), lambda j, s: (0, s * nb + j)),
                   pl.BlockSpec((3, MIX_COLS), lambda j, s: (0, j)), pl.BlockSpec((1, MIX_COLS), lambda j, s: (0, j))],
        out_shape=[jax.ShapeDtypeStruct(dproj.shape, BF16), jax.ShapeDtypeStruct((3, n_conv), F32),
                   jax.ShapeDtypeStruct((1, n_conv), F32)],
        scratch_shapes=[pltpu.VMEM((4, t, MIX_COLS), BF16)],
        input_output_aliases={7: 0},
        compiler_params=_cp(("arbitrary", "arbitrary")),
    )(dz, dp, proj, proj, proj, cw, cb, dproj)


def _merge_fwd(name, proj, bg, ya, yb, ps):
    t, d = ya.shape
    tm = _rows(t)

    def body(gab_ref, bg_ref, ya_ref, yb_ref, ps_ref, o_ref):
        gab = gab_ref[...].astype(F32) + bg_ref[...]
        sa, sb = jax.nn.sigmoid(gab[:, :d]), jax.nn.sigmoid(gab[:, d:])
        o_ref[...] = (sa * ya_ref[...].astype(F32) + sb * (yb_ref[...].astype(F32) * ps_ref[...])).astype(BF16)

    row = pl.BlockSpec((tm, d), lambda i: (i, 0))
    return pl.pallas_call(
        body, name=name, grid=(t // tm,),
        in_specs=[pl.BlockSpec((tm, 2 * d), lambda i: (i, 1)), pl.BlockSpec((1, 2 * d), lambda i: (0, 0)), row, row,
                  pl.BlockSpec((1, d), lambda i: (0, 0))],
        out_specs=row, out_shape=jax.ShapeDtypeStruct((t, d), BF16), compiler_params=_cp(("parallel",)),
    )(proj, bg, ya, yb, ps)


def _merge_bwd(name, dm, proj, bg, ya, yb, ps):
    t, d = ya.shape
    tm = _rows(t)

    def body(dm_ref, gab_ref, bg_ref, ya_ref, yb_ref, ps_ref, dya_ref, dyb_ref, dg_ref, dba_ref, dbb_ref, dps_ref):
        gab = gab_ref[...].astype(F32) + bg_ref[...]
        sa, sb = jax.nn.sigmoid(gab[:, :d]), jax.nn.sigmoid(gab[:, d:])
        dmf = dm_ref[...].astype(F32)
        ybf, ps_ = yb_ref[...].astype(F32), ps_ref[...]
        dya_ref[...] = (dmf * sa).astype(BF16)
        dyb = dmf * sb
        dyb_ref[...] = (dyb * ps_).astype(BF16)
        dga = dmf * ya_ref[...].astype(F32) * sa * (1.0 - sa)
        dgb = dmf * (ybf * ps_) * sb * (1.0 - sb)
        dg_ref[:, :d] = dga.astype(BF16)
        dg_ref[:, d:] = dgb.astype(BF16)

        @pl.when(pl.program_id(0) == 0)
        def _():
            dba_ref[...] = jnp.zeros_like(dba_ref)
            dbb_ref[...] = jnp.zeros_like(dbb_ref)
            dps_ref[...] = jnp.zeros_like(dps_ref)

        dba_ref[...] += jnp.sum(dga, axis=0, keepdims=True)
        dbb_ref[...] += jnp.sum(dgb, axis=0, keepdims=True)
        dps_ref[...] += jnp.sum(dyb * ybf, axis=0, keepdims=True)

    row = pl.BlockSpec((tm, d), lambda i: (i, 0))
    vec = pl.BlockSpec((1, d), lambda i: (0, 0))
    gates = pl.BlockSpec((tm, 2 * d), lambda i: (i, 1))
    return pl.pallas_call(
        body, name=name, grid=(t // tm,),
        in_specs=[row, gates, pl.BlockSpec((1, 2 * d), lambda i: (0, 0)), row, row, vec],
        out_specs=[row, row, gates, vec, vec, vec],
        out_shape=[jax.ShapeDtypeStruct((t, d), BF16), jax.ShapeDtypeStruct((t, d), BF16),
                   jax.ShapeDtypeStruct(proj.shape, BF16), jax.ShapeDtypeStruct((1, d), F32),
                   jax.ShapeDtypeStruct((1, d), F32), jax.ShapeDtypeStruct((1, d), F32)],
        compiler_params=_cp(("arbitrary",)),
    )(dm, proj, bg, ya, yb, ps)


def _ffn_act(name, gate, up):
    t, f = gate.shape
    tm, tf = _rows(t), _tile(f, 2048)

    def body(g_ref, u_ref, o_ref):
        g = g_ref[...].astype(F32)
        o_ref[...] = (g * jax.nn.sigmoid(g) * u_ref[...].astype(F32)).astype(BF16)

    blk = pl.BlockSpec((tm, tf), lambda i, j: (i, j))
    return pl.pallas_call(body, name=name, grid=(t // tm, f // tf), in_specs=[blk, blk], out_specs=blk,
                          out_shape=jax.ShapeDtypeStruct((t, f), BF16), compiler_params=_cp(("parallel", "parallel")))(gate, up)


def _ffn_bwd(name, dact, gate, up):
    t, f = gate.shape
    tm, tf = _rows(t), _tile(f, 2048)

    def body(da_ref, g_ref, u_ref, dg_ref, du_ref):
        g, da = g_ref[...].astype(F32), da_ref[...].astype(F32)
        s = jax.nn.sigmoid(g)
        du_ref[...] = (da * (g * s)).astype(BF16)
        dg_ref[...] = (da * u_ref[...].astype(F32) * (s * (1.0 + g * (1.0 - s)))).astype(BF16)

    blk = pl.BlockSpec((tm, tf), lambda i, j: (i, j))
    shp = jax.ShapeDtypeStruct((t, f), BF16)
    return pl.pallas_call(body, name=name, grid=(t // tm, f // tf), in_specs=[blk, blk, blk], out_specs=[blk, blk],
                          out_shape=[shp, shp], compiler_params=_cp(("parallel", "parallel")))(dact, gate, up)


def _adamw_math(w, g, m, v):
    m = ADAM_B1 * m + (1.0 - ADAM_B1) * g
    v = ADAM_B2 * v + (1.0 - ADAM_B2) * (g * g)
    m_hat = m / (1.0 - ADAM_B1 ** ADAM_STEP)
    v_hat = v / (1.0 - ADAM_B2 ** ADAM_STEP)
    delta = -ADAM_LR * (m_hat / (jnp.sqrt(v_hat) + ADAM_EPS) + ADAM_WD * w)
    return delta, m, v


def _adamw(name, w, g, m, v):
    r, c = w.shape
    tr = _tile8(r, 512 if c <= 1024 else 256)

    def body(w_ref, g_ref, m_ref, v_ref, d_ref, nm_ref, nv_ref):
        d_ref[...], nm_ref[...], nv_ref[...] = _adamw_math(w_ref[...], g_ref[...], m_ref[...], v_ref[...])

    blk = pl.BlockSpec((tr, c), lambda i: (i, 0))
    shp = jax.ShapeDtypeStruct((r, c), F32)
    return pl.pallas_call(body, name=name, grid=(r // tr,), in_specs=[blk] * 4, out_specs=[blk] * 3,
                          out_shape=[shp] * 3, compiler_params=_cp(("parallel",)))(w, g, m, v)


class _Weight:
    def __init__(self, name, rows, cols, colshard):
        self.name, self.colshard = name, colshard
        self.R, self.nn = rows // 2, cols
        self.P = 1 if colshard else N_CHIPS
        self.N = N_CHIPS * cols if colshard else cols

    def cols(self, k):
        return pl.ds(pl.multiple_of(k * self.nn, LANES), self.nn)

    def shard(self, ref, k):
        return ref.at[0, :, :, self.cols(k)] if self.colshard else ref.at[k]

    def half(self, ref, k, h):
        return ref.at[0, h, :, self.cols(k)] if self.colshard else ref.at[k, h]

    def part(self, ref, k):
        return ref.at[0, :, self.cols(k)] if self.colshard else ref.at[k]


def _remote(src, dst, ssem, rsem, dev):
    return pltpu.make_async_remote_copy(src_ref=src, dst_ref=dst, send_sem=ssem, recv_sem=rsem, device_id=dev,
                                        device_id_type=MESH)


def _other_chips(x, y):
    chips = [(1 - x, y), (x, 1 - y), (1 - x, 1 - y)]
    return chips, [2 * cx + cy for cx, cy in chips]


def _all_gather_weights(ws, locs, cw):
    nw = len(ws)
    ncw = cw.shape[1]

    def body(*refs):
        loc, cw_in = refs[:nw], refs[nw]
        out, cw_out = refs[nw + 1:2 * nw + 1], refs[2 * nw + 1]
        ssem, rsem, lsem = refs[2 * nw + 2:]
        x, y, c = _mesh_pos()
        k_me = 2 * x + y
        chips, ks = _other_chips(x, y)
        sib = (x, y, 1 - c)
        cw_cols = lambda k: cw_out.at[:, pl.ds(pl.multiple_of(k * ncw, LANES), ncw)]
        n_ici = 3 * (nw + 1)

        local = [pltpu.make_async_copy(loc[i], w.shard(out[i], k_me), lsem.at[i]) for i, w in enumerate(ws)]
        local.append(pltpu.make_async_copy(cw_in, cw_cols(k_me), lsem.at[nw]))
        for cp in local:
            cp.start()
        sends = [_remote(cw_in, cw_cols(k_me), ssem.at[j], rsem.at[j], (*chip, c)) for j, chip in enumerate(chips)]
        for i, w in enumerate(ws):
            sends += [_remote(loc[i].at[c], w.half(out[i], k_me, c), ssem.at[3 + 3 * i + j], rsem.at[3 + 3 * i + j],
                              (*chip, c)) for j, chip in enumerate(chips)]
        for cp in sends:
            cp.start()
        for j in range(3):
            _remote(cw_in, cw_cols(ks[j]), ssem.at[j], rsem.at[j], sib).wait_recv()
        passed = []
        for i, w in enumerate(ws):
            for j in range(3):
                got = w.half(out[i], ks[j], c)
                s = 3 + 3 * i + j
                _remote(got, got, ssem.at[s], rsem.at[s], sib).wait_recv()
                cp = _remote(got, got, ssem.at[n_ici + 3 * i + j], rsem.at[n_ici + 3 * i + j], sib)
                cp.start()
                passed.append(cp)
        for i, w in enumerate(ws):
            for j in range(3):
                got = w.half(out[i], ks[j], 1 - c)
                s = n_ici + 3 * i + j
                _remote(got, got, ssem.at[s], rsem.at[s], sib).wait_recv()
        for cp in sends + passed:
            cp.wait_send()
        for cp in local:
            cp.wait()

    n_sem = 3 * (nw + 1) + 3 * nw
    return pl.pallas_call(
        body, name="all_gather_weights",
        in_specs=[ANY] * (nw + 1), out_specs=[ANY] * (nw + 1),
        out_shape=[jax.ShapeDtypeStruct((w.P, 2, w.R, w.N), BF16) for w in ws]
        + [jax.ShapeDtypeStruct((3, N_CHIPS * ncw), F32)],
        scratch_shapes=[pltpu.SemaphoreType.DMA((n_sem,)), pltpu.SemaphoreType.DMA((n_sem,)),
                        pltpu.SemaphoreType.DMA((nw + 1,))],
    )(*locs, cw)


def _pair_exchange(ws, grads):
    nw = len(ws)

    def body(*refs):
        g, out = refs[:nw], refs[nw:2 * nw]
        ssem, rsem = refs[2 * nw:]
        x, y, c = _mesh_pos()
        sib = (x, y, 1 - c)
        cps = [_remote(g[i].at[:, 1 - c], out[i], ssem.at[i], rsem.at[i], sib) for i in range(nw)]
        for cp in cps:
            cp.start()
        for cp in cps:
            cp.wait()

    return pl.pallas_call(
        body, name="grad_pair_exchange", in_specs=[ANY] * nw, out_specs=[ANY] * nw,
        out_shape=[jax.ShapeDtypeStruct((w.P, w.R, w.N), BF16) for w in ws],
        scratch_shapes=[pltpu.SemaphoreType.DMA((nw,)), pltpu.SemaphoreType.DMA((nw,))],
    )(*grads)


def _grad_tiles(w, n):
    return _tile8(w.R, 512) if w.R <= 512 else w.R // 2, _tile(n, 2048)


def _pair_sum(name, w, pos, grad, got):
    tr, tn = _grad_tiles(w, w.N)

    def body(pos_ref, g_ref, r_ref, o_ref):
        o_ref[...] = (g_ref[...].astype(F32) + r_ref[...].astype(F32)).astype(BF16)

    blk = pl.BlockSpec((None, tr, tn), lambda p, i, j, pos: (p, i, j))
    grid_spec = pltpu.PrefetchScalarGridSpec(
        num_scalar_prefetch=1, grid=(w.P, w.R // tr, w.N // tn),
        in_specs=[pl.BlockSpec((None, None, tr, tn), lambda p, i, j, pos: (p, pos[0], i, j)), blk], out_specs=blk)
    return pl.pallas_call(body, name=name, grid_spec=grid_spec, out_shape=jax.ShapeDtypeStruct((w.P, w.R, w.N), BF16),
                          compiler_params=_cp(("parallel",) * 3))(pos, grad, got)


def _reduce_scatter(ws, pairs):
    nw = len(ws)

    def body(*refs):
        pr, out = refs[:nw], refs[nw:2 * nw]
        ssem, rsem = refs[2 * nw:]
        x, y, c = _mesh_pos()
        chips, ks = _other_chips(x, y)
        cps = []
        for i, w in enumerate(ws):
            cps += [_remote(w.part(pr[i], ks[j]), out[i].at[j], ssem.at[3 * i + j], rsem.at[3 * i + j], (*chip, c))
                    for j, chip in enumerate(chips)]
        for cp in cps:
            cp.start()
        for cp in cps:
            cp.wait()

    return pl.pallas_call(
        body, name="grad_reduce_scatter", in_specs=[ANY] * nw, out_specs=[ANY] * nw,
        out_shape=[jax.ShapeDtypeStruct((3, w.R, w.nn), BF16) for w in ws],
        scratch_shapes=[pltpu.SemaphoreType.DMA((3 * nw,)), pltpu.SemaphoreType.DMA((3 * nw,))],
    )(*pairs)


def _final_sum(name, w, pos, grad, got, parts):
    tr, tn = _grad_tiles(w, w.nn)
    nbc = w.nn // tn

    def body(pos_ref, g_ref, r_ref, p_ref, o_ref):
        acc = g_ref[...].astype(F32) + r_ref[...].astype(F32)
        for j in range(3):
            acc = acc + p_ref[j].astype(F32)
        o_ref[...] = acc

    if w.colshard:
        g_spec = pl.BlockSpec((None, None, tr, tn), lambda i, j, pos: (0, pos[0], i, pos[1] * nbc + j))
        r_spec = pl.BlockSpec((None, tr, tn), lambda i, j, pos: (0, i, pos[1] * nbc + j))
    else:
        g_spec = pl.BlockSpec((None, None, tr, tn), lambda i, j, pos: (pos[1], pos[0], i, j))
        r_spec = pl.BlockSpec((None, tr, tn), lambda i, j, pos: (pos[1], i, j))
    grid_spec = pltpu.PrefetchScalarGridSpec(
        num_scalar_prefetch=1, grid=(w.R // tr, nbc),
        in_specs=[g_spec, r_spec, pl.BlockSpec((3, tr, tn), lambda i, j, pos: (0, i, j))],
        out_specs=pl.BlockSpec((None, tr, tn), lambda i, j, pos: (pos[0], i, j)))
    return pl.pallas_call(body, name=name, grid_spec=grid_spec, out_shape=jax.ShapeDtypeStruct((2, w.R, w.nn), F32),
                          compiler_params=_cp(("parallel",) * 2))(pos, grad, got, parts)


def _share_halves(ws, halves):
    nw = len(ws)

    def body(*refs):
        out = refs[nw:2 * nw]
        ssem, rsem = refs[2 * nw:]
        x, y, c = _mesh_pos()
        sib = (x, y, 1 - c)
        cps = [_remote(out[i].at[c], out[i].at[c], ssem.at[i], rsem.at[i], sib) for i in range(nw)]
        for cp in cps:
            cp.start()
        for i, cp in enumerate(cps):
            cp.wait_send()
            _remote(out[i].at[1 - c], out[i].at[1 - c], ssem.at[i], rsem.at[i], sib).wait_recv()

    return pl.pallas_call(
        body, name="grad_share_halves", in_specs=[ANY] * nw, out_specs=[ANY] * nw,
        out_shape=[jax.ShapeDtypeStruct(h.shape, F32) for h in halves],
        scratch_shapes=[pltpu.SemaphoreType.DMA((nw,)), pltpu.SemaphoreType.DMA((nw,))],
        input_output_aliases={i: i for i in range(nw)},
    )(*halves)


VEC_ROWS = 16


def _vector_step(d, n_conv, parts, params):
    ncw = params[2][0].shape[1]
    n_par = len(params)

    def body(*refs):
        dg1, dba, dbb, dcw, dcb, dps, dg2, dgf, lc = refs[:9]
        wmv = refs[9:9 + 3 * n_par]
        outs = refs[9 + 3 * n_par:9 + 7 * n_par]
        loss_ref = refs[9 + 7 * n_par]
        snd, got, ssem, rsem = refs[9 + 7 * n_par + 1:]
        x, y, c = _mesh_pos()
        me = 4 * x + 2 * y + c
        snd[...] = jnp.zeros_like(snd)
        for row, ref in ((0, dg1), (1, dba), (2, dbb), (3, dps), (4, dg2), (5, dgf), (6, lc)):
            snd[row:row + 1, :] = ref[...]
        snd[7:8, :n_conv] = dcb[...]
        snd[8:11, :n_conv] = dcw[...]
        cps = []
        for r in range(1, N_DEV):
            peer = tuple(1 - p if (r >> b) & 1 else p for p, b in ((x, 2), (y, 1), (c, 0)))
            cps.append(_remote(snd, got.at[me], ssem.at[r - 1], rsem.at[r - 1], peer))
        for cp in cps:
            cp.start()
        got[me] = snd[...]
        for r in range(1, N_DEV):
            peer = tuple(1 - p if (r >> b) & 1 else p for p, b in ((x, 2), (y, 1), (c, 0)))
            _remote(snd, got.at[4 * peer[0] + 2 * peer[1] + peer[2]], ssem.at[r - 1], rsem.at[r - 1], peer).wait_recv()
        for cp in cps:
            cp.wait_send()
        tot = got[0]
        for dev in range(1, N_DEV):
            tot = tot + got[dev]
        loss_ref[...] = jnp.sum(tot[6:7, :], axis=1, keepdims=True)
        k_me = 2 * x + y
        g_cw = jnp.zeros((3, ncw), F32)
        for k in range(N_CHIPS):
            g_cw = g_cw + jnp.where(k_me == k, tot[8:11, k * ncw:(k + 1) * ncw], 0.0)
        grads = [tot[0:1, :], jnp.concatenate([tot[1:2, :], tot[2:3, :]], axis=1), g_cw, tot[7:8, :n_conv],
                 tot[3:4, :], tot[4:5, :], tot[5:6, :]]
        for i, g in enumerate(grads):
            w_ref, m_ref, v_ref = wmv[3 * i:3 * i + 3]
            delta, nm, nv = _adamw_math(w_ref[...], g, m_ref[...], v_ref[...])
            outs[4 * i][...] = g
            outs[4 * i + 1][...] = delta
            outs[4 * i + 2][...] = nm
            outs[4 * i + 3][...] = nv

    args = list(parts)
    out_shape = []
    for w, m, v in params:
        args += [w, m, v]
        out_shape += [jax.ShapeDtypeStruct(w.shape, F32)] * 4
    out_shape.append(jax.ShapeDtypeStruct((1, 1), F32))
    return pl.pallas_call(
        body, name="vector_params_step", in_specs=[VMEM] * len(args), out_specs=[VMEM] * len(out_shape),
        out_shape=out_shape,
        scratch_shapes=[pltpu.VMEM((VEC_ROWS, d), F32), pltpu.VMEM((N_DEV, VEC_ROWS, d), F32),
                        pltpu.SemaphoreType.DMA((N_DEV - 1,)), pltpu.SemaphoreType.DMA((N_DEV - 1,))],
        compiler_params=pltpu.CompilerParams(vmem_limit_bytes=VMEM_LIMIT),
    )(*args)


def kernel(x, norm1_g, w_in, b_gate, conv_w, conv_b, w_a_out, w_pool, pool_scale, w_o, norm2_g, w_ffn_gate, w_ffn_up, w_ffn_down, final_g, loss_target, m_norm1_g, m_w_in, m_b_gate, m_conv_w, m_conv_b, m_w_a_out, m_w_pool, m_pool_scale, m_w_o, m_norm2_g, m_w_ffn_gate, m_w_ffn_up, m_w_ffn_down, m_final_g, v_norm1_g, v_w_in, v_b_gate, v_conv_w, v_conv_b, v_w_a_out, v_w_pool, v_pool_scale, v_w_o, v_norm2_g, v_w_ffn_gate, v_w_ffn_up, v_w_ffn_down, v_final_g):
    t, d = x.shape[1], x.shape[2]
    n_conv = conv_b.shape[1]
    n_groups, pool_cg, pool_dg = w_pool.shape[1], w_pool.shape[2], N_CHIPS * w_pool.shape[3]
    d_ff = N_CHIPS * w_ffn_gate.shape[2]
    assert n_conv // n_groups == pool_cg and n_conv % (n_groups * MIX_COLS) == 0 and n_groups == len(POOL_WINDOWS)

    big = {"w_in": (w_in, m_w_in, v_w_in), "w_a_out": (w_a_out, m_w_a_out, v_w_a_out), "w_pool": (w_pool, m_w_pool, v_w_pool),
           "w_o": (w_o, m_w_o, v_w_o), "w_ffn_gate": (w_ffn_gate, m_w_ffn_gate, v_w_ffn_gate),
           "w_ffn_up": (w_ffn_up, m_w_ffn_up, v_w_ffn_up), "w_ffn_down": (w_ffn_down, m_w_ffn_down, v_w_ffn_down)}
    colshard = {"w_in": True, "w_a_out": True, "w_pool": True, "w_o": False, "w_ffn_gate": True, "w_ffn_up": True,
                "w_ffn_down": False}
    names = list(big)
    shard2d = {n: big[n][0].reshape(-1, big[n][0].shape[-1]) for n in names}
    ws = [_Weight(n, *shard2d[n].shape, colshard[n]) for n in names]

    xs, tgt = x[0], loss_target[0]
    cw_loc = conv_w[0]
    pos = jnp.stack([lax.axis_index("c"), 2 * lax.axis_index("x") + lax.axis_index("y")]).astype(jnp.int32)

    locs = [_cast_bf16(f"cast_{w.name}", shard2d[w.name].reshape(2, w.R, w.nn)) for w in ws]
    gathered = _all_gather_weights(ws, locs, cw_loc)
    full = {w.name: g.reshape(w.P * 2 * w.R, w.N) for w, g in zip(ws, gathered[:-1])}
    cw_full = gathered[-1]
    wp_full = full["w_pool"].reshape(n_groups, pool_cg, pool_dg)

    h1 = _rms_fwd("norm1_fwd", xs, norm1_g)
    proj = _mm_nn("proj_in", h1, full["w_in"], BF16)
    z, p = _mixer_fwd("mixer_fwd", proj, cw_full, conv_b, n_conv, n_groups)
    ya = _mm_nn("conv_out", z, full["w_a_out"], BF16)
    yb = _gmm_nn("pool_out", p, wp_full, BF16)
    merged = _merge_fwd("merge_fwd", proj, b_gate, ya, yb, pool_scale)
    x2 = _mm_nn("mix_out", merged, full["w_o"], F32, add=xs)
    h2 = _rms_fwd("norm2_fwd", x2, norm2_g)
    gate = _mm_nn("ffn_gate", h2, full["w_ffn_gate"], BF16)
    up = _mm_nn("ffn_up", h2, full["w_ffn_up"], BF16)
    act = _ffn_act("ffn_act", gate, up)
    x3 = _mm_nn("ffn_down", act, full["w_ffn_down"], F32, add=x2, tk=d_ff // 4)

    dx3, dx3b, d_gf, loss_cols = _final_bwd("final_bwd", x3, final_g.reshape(1, d), tgt)
    dact = _mm_nt("d_act", [(dx3b, full["w_ffn_down"])], BF16, tk=d)
    dgate, dup = _ffn_bwd("ffn_bwd", dact, gate, up)
    grads = {"w_ffn_down": _mm_tn("dw_ffn_down", act, dx3b, BF16)}
    dh2 = _mm_nt("d_h2", [(dgate, full["w_ffn_gate"]), (dup, full["w_ffn_up"])], F32, tk=d_ff // 4)
    grads["w_ffn_gate"] = _mm_tn("dw_ffn_gate", h2, dgate, BF16)
    grads["w_ffn_up"] = _mm_tn("dw_ffn_up", h2, dup, BF16)
    dx2, dx2b, d_g2 = _rms_bwd("norm2_bwd", x2, norm2_g, dh2, dx3, True)
    dmerged = _mm_nt("d_merged", [(dx2b, full["w_o"])], BF16, tk=d)
    grads["w_o"] = _mm_tn("dw_o", merged, dx2b, BF16)
    dya, dyb, dproj, d_bga, d_bgb, d_ps = _merge_bwd("merge_bwd", dmerged, proj, b_gate, ya, yb, pool_scale)
    dz = _mm_nt("d_z", [(dya, full["w_a_out"])], BF16, tk=d)
    grads["w_a_out"] = _mm_tn("dw_a_out", z, dya, BF16)
    dp = _gmm_nt("d_pool", dyb, wp_full, BF16)
    grads["w_pool"] = _gmm_tn("dw_pool", p, dyb, n_groups, BF16)
    dproj, d_cw, d_cb = _mixer_bwd("mixer_bwd", dz, dp, proj, cw_full, conv_b, dproj, n_conv, n_groups)
    dh1 = _mm_nt("d_h1", [(dproj, full["w_in"])], F32, tk=proj.shape[1] // 4)
    grads["w_in"] = _mm_tn("dw_in", h1, dproj, BF16)
    grad_x, d_g1 = _rms_bwd("norm1_bwd", xs, norm1_g, dh1, dx2, False)

    gcan = [grads[w.name].reshape(w.P, 2, w.R, w.N) for w in ws]
    sib = _pair_exchange(ws, gcan)
    pairs = [_pair_sum(f"pair_sum_{w.name}", w, pos, g, s) for w, g, s in zip(ws, gcan, sib)]
    parts = _reduce_scatter(ws, pairs)
    halves = [_final_sum(f"final_sum_{w.name}", w, pos, g, s, q) for w, g, s, q in zip(ws, gcan, sib, parts)]
    gsh = _share_halves(ws, halves)

    g_big, d_big, m_big, v_big = {}, {}, {}, {}
    for w, g in zip(ws, gsh):
        wt, mt, vt = big[w.name]
        g2 = g.reshape(2 * w.R, w.nn)
        dl, nm, nv = _adamw(f"adamw_{w.name}", shard2d[w.name], g2, mt.reshape(g2.shape), vt.reshape(g2.shape))
        g_big[w.name], d_big[w.name], m_big[w.name], v_big[w.name] = (a.reshape(wt.shape) for a in (g2, dl, nm, nv))

    vec_names = ["norm1_g", "b_gate", "conv_w", "conv_b", "pool_scale", "norm2_g", "final_g"]
    vec = {"norm1_g": (norm1_g, m_norm1_g, v_norm1_g), "b_gate": (b_gate, m_b_gate, v_b_gate),
           "conv_w": (cw_loc, m_conv_w[0], v_conv_w[0]), "conv_b": (conv_b, m_conv_b, v_conv_b),
           "pool_scale": (pool_scale, m_pool_scale, v_pool_scale), "norm2_g": (norm2_g, m_norm2_g, v_norm2_g),
           "final_g": tuple(a.reshape(1, d) for a in (final_g, m_final_g, v_final_g))}
    vout = _vector_step(d, n_conv, [d_g1, d_bga, d_bgb, d_cw, d_cb, d_ps, d_g2, d_gf, loss_cols],
                        [vec[n] for n in vec_names])
    shapes = {"conv_w": conv_w.shape, "final_g": final_g.shape}
    g_vec, d_vec, m_vec, v_vec = ({n: vout[4 * i + q].reshape(shapes.get(n, vec[n][0].shape)) for i, n in enumerate(vec_names)}
                                  for q in range(4))
    loss = vout[-1].reshape(())

    order = ["norm1_g", "w_in", "b_gate", "conv_w", "conv_b", "w_a_out", "w_pool", "pool_scale", "w_o", "norm2_g",
             "w_ffn_gate", "w_ffn_up", "w_ffn_down", "final_g"]
    pick = lambda vecs, bigs: [vecs[n] if n in vecs else bigs[n] for n in order]
    return (loss, grad_x.reshape(x.shape), *pick(g_vec, g_big), *pick(d_vec, d_big), *pick(m_vec, m_big),
            *pick(v_vec, v_big))
```

```python
import functools

import jax
import jax.numpy as jnp
from jax import lax
from jax.experimental import pallas as pl
from jax.experimental.pallas import tpu as pltpu

F32, BF16 = jnp.float32, jnp.bfloat16
MESH = pl.DeviceIdType.MESH
ANY = pl.BlockSpec(memory_space=pl.ANY)
VMEM = pl.BlockSpec(memory_space=pltpu.VMEM)
HBM = pl.BlockSpec(memory_space=pltpu.HBM)
SEM = pl.BlockSpec(memory_space=pltpu.SEMAPHORE)
EFFECT = pltpu.SideEffectType.DATAFLOW_SIDE_EFFECTING

EPS = 1e-6
POOL_WINDOWS = (2, 4, 8, 16)
ADAM_LR, ADAM_B1, ADAM_B2, ADAM_EPS, ADAM_WD, ADAM_STEP = 0.001, 0.9, 0.999, 1e-08, 0.01, 10

V7X_VMEM_BYTES = 64 * 1024 * 1024
VMEM_LIMIT = V7X_VMEM_BYTES * 3 // 4
LANES = 128
N_CHIPS = 4
N_DEV = 8

_DIMS = {
    "nn": (((1,), (0,)), ((), ())),
    "nt": (((1,), (1,)), ((), ())),
    "tn": (((0,), (0,)), ((), ())),
}


def _cp(sem):
    return pltpu.CompilerParams(dimension_semantics=sem, vmem_limit_bytes=VMEM_LIMIT)


def _mesh_pos():
    return lax.axis_index("x"), lax.axis_index("y"), lax.axis_index("c")


def _mm(name, pairs, *, mode, grid, out_shape, o_spec, nk=1, kaxis=None, add=None, deps=()):
    npair = len(pairs)
    has_add = add is not None

    def body(*refs):
        ab = refs[: 2 * npair]
        pos = 2 * npair
        add_ref = refs[pos] if has_add else None
        pos += int(has_add) + len(deps)
        o_ref = refs[pos]
        acc_ref = refs[pos + 1] if nk > 1 else None
        d = None
        for p in range(npair):
            t = lax.dot_general(ab[2 * p][...], ab[2 * p + 1][...], _DIMS[mode], preferred_element_type=F32)
            d = t if d is None else d + t
        if nk == 1:
            if has_add:
                d = d + add_ref[...].astype(F32)
            o_ref[...] = d.astype(o_ref.dtype)
        else:
            k = pl.program_id(kaxis)

            @pl.when(k == 0)
            def _():
                acc_ref[...] = d

            @pl.when(k > 0)
            def _():
                acc_ref[...] += d

            @pl.when(k == nk - 1)
            def _():
                r = acc_ref[...]
                if has_add:
                    r = r + add_ref[...].astype(F32)
                o_ref[...] = r.astype(o_ref.dtype)

    args, specs = [], []
    for a, a_spec, b, b_spec in pairs:
        args += [a, b]
        specs += [a_spec, b_spec]
    if has_add:
        args.append(add[0])
        specs.append(add[1])
    args += list(deps)
    specs += [ANY] * len(deps)
    scratch = []
    if nk > 1:
        blk = [d for d in o_spec.block_shape if d is not None]
        scratch = [pltpu.VMEM(tuple(blk), F32)]
    sem = tuple("arbitrary" if (nk > 1 and ax == kaxis) else "parallel" for ax in range(len(grid)))
    return pl.pallas_call(
        body, name=name, grid=grid, in_specs=specs, out_specs=o_spec, out_shape=out_shape,
        scratch_shapes=scratch, compiler_params=_cp(sem),
    )(*args)


def _tile(n, pref):
    if n <= pref:
        return n
    for t in range(pref, 0, -LANES):
        if t % LANES == 0 and n % t == 0:
            return t
    raise ValueError(f"no tile for {n}")


def _mm_nn(name, a, b, out_dtype, add=None, tk=None):
    m, kk = a.shape
    n = b.shape[1]
    tm, tn = _tile(m, 1024), _tile(n, 512)
    out_shape = jax.ShapeDtypeStruct((m, n), out_dtype)
    if tk is None or tk == kk:
        grid = (m // tm, n // tn)
        pairs = [(a, pl.BlockSpec((tm, kk), lambda i, j: (i, 0)), b, pl.BlockSpec((kk, tn), lambda i, j: (0, j)))]
        o_spec = pl.BlockSpec((tm, tn), lambda i, j: (i, j))
        add_ = None if add is None else (add, pl.BlockSpec((tm, tn), lambda i, j: (i, j)))
        return _mm(name, pairs, mode="nn", grid=grid, out_shape=out_shape, o_spec=o_spec, add=add_)
    tn = _tile(n, 1024)
    nk = kk // tk
    grid = (m // tm, n // tn, nk)
    pairs = [(a, pl.BlockSpec((tm, tk), lambda i, j, k: (i, k)), b, pl.BlockSpec((tk, tn), lambda i, j, k: (k, j)))]
    o_spec = pl.BlockSpec((tm, tn), lambda i, j, k: (i, j))
    add_ = None if add is None else (add, pl.BlockSpec((tm, tn), lambda i, j, k: (i, j)))
    return _mm(name, pairs, mode="nn", grid=grid, out_shape=out_shape, o_spec=o_spec, nk=nk, kaxis=2, add=add_)


def _mm_nt(name, abs_, out_dtype, tk, deps=()):
    m, kk = abs_[0][0].shape
    n = abs_[0][1].shape[0]
    tm = _tile(m, 1024)
    nk = kk // tk
    tn = _tile(n, 512 if nk == 1 else 1024)
    out_shape = jax.ShapeDtypeStruct((m, n), out_dtype)
    if nk == 1:
        grid = (m // tm, n // tn)
        pairs = [(a, pl.BlockSpec((tm, kk), lambda i, j: (i, 0)), b, pl.BlockSpec((tn, kk), lambda i, j: (j, 0)))
                 for a, b in abs_]
        o_spec = pl.BlockSpec((tm, tn), lambda i, j: (i, j))
        return _mm(name, pairs, mode="nt", grid=grid, out_shape=out_shape, o_spec=o_spec, deps=deps)
    grid = (m // tm, n // tn, nk)
    pairs = [(a, pl.BlockSpec((tm, tk), lambda i, j, k: (i, k)), b, pl.BlockSpec((tn, tk), lambda i, j, k: (j, k)))
             for a, b in abs_]
    o_spec = pl.BlockSpec((tm, tn), lambda i, j, k: (i, j))
    return _mm(name, pairs, mode="nt", grid=grid, out_shape=out_shape, o_spec=o_spec, nk=nk, kaxis=2, deps=deps)


def _mm_tn(name, a, b, out_dtype):
    t, m = a.shape
    n = b.shape[1]
    tm, tn = _tile(m, 512), _tile(n, 2048)
    if n > m:
        grid = (n // tn, m // tm)
        a_map, b_map, o_map = (lambda j, i: (0, i)), (lambda j, i: (0, j)), (lambda j, i: (i, j))
    else:
        grid = (m // tm, n // tn)
        a_map, b_map, o_map = (lambda i, j: (0, i)), (lambda i, j: (0, j)), (lambda i, j: (i, j))
    pairs = [(a, pl.BlockSpec((t, tm), a_map), b, pl.BlockSpec((t, tn), b_map))]
    o_spec = pl.BlockSpec((tm, tn), o_map)
    return _mm(name, pairs, mode="tn", grid=grid, out_shape=jax.ShapeDtypeStruct((m, n), out_dtype), o_spec=o_spec)


def _gmm_nn(name, p, w, out_dtype):
    t = p.shape[0]
    g, cg, dg = w.shape
    tm = _tile(t, 1024)
    pairs = [(p, pl.BlockSpec((tm, cg), lambda i, j: (i, j)), w, pl.BlockSpec((None, cg, dg), lambda i, j: (j, 0, 0)))]
    o_spec = pl.BlockSpec((tm, dg), lambda i, j: (i, j))
    return _mm(name, pairs, mode="nn", grid=(t // tm, g), out_shape=jax.ShapeDtypeStruct((t, g * dg), out_dtype),
               o_spec=o_spec)


def _gmm_nt(name, dy, w, out_dtype):
    t = dy.shape[0]
    g, cg, dg = w.shape
    tm = _tile(t, 1024)
    pairs = [(dy, pl.BlockSpec((tm, dg), lambda i, j: (i, j)), w, pl.BlockSpec((None, cg, dg), lambda i, j: (j, 0, 0)))]
    o_spec = pl.BlockSpec((tm, cg), lambda i, j: (i, j))
    return _mm(name, pairs, mode="nt", grid=(t // tm, g), out_shape=jax.ShapeDtypeStruct((t, g * cg), out_dtype),
               o_spec=o_spec)


def _gmm_tn(name, p, dy, g, out_dtype):
    t = p.shape[0]
    cg, dg = p.shape[1] // g, dy.shape[1] // g
    pairs = [(p, pl.BlockSpec((t, cg), lambda j: (0, j)), dy, pl.BlockSpec((t, dg), lambda j: (0, j)))]
    o_spec = pl.BlockSpec((None, cg, dg), lambda j: (j, 0, 0))
    return _mm(name, pairs, mode="tn", grid=(g,), out_shape=jax.ShapeDtypeStruct((g, cg, dg), out_dtype), o_spec=o_spec)


ROW_TILE = 256


def _rows(t):
    return _tile8(t, ROW_TILE)


def _tile8(n, pref):
    if n <= pref:
        return n
    for t in range(pref, 0, -8):
        if n % t == 0:
            return t
    raise ValueError(f"no row tile for {n}")


def _cast_place(name, w, pos, shard):
    tr = _tile8(w.R, 512)
    if w.colshard:
        o_map = lambda h, i, pos: (0, h, i, pos[1])
    else:
        o_map = lambda h, i, pos: (pos[1], h, i, 0)

    def body(pos_ref, w_ref, o_ref):
        o_ref[...] = w_ref[...].astype(BF16)

    grid_spec = pltpu.PrefetchScalarGridSpec(
        num_scalar_prefetch=1, grid=(2, w.R // tr),
        in_specs=[pl.BlockSpec((None, tr, w.nn), lambda h, i, pos: (h, i, 0))],
        out_specs=pl.BlockSpec((None, None, tr, w.nn), o_map))
    return pl.pallas_call(body, name=name, grid_spec=grid_spec, out_shape=jax.ShapeDtypeStruct((w.P, 2, w.R, w.N), BF16),
                          compiler_params=_cp(("parallel", "parallel")))(pos, shard)


def _rms_fwd(name, x, g):
    t, d = x.shape
    tm = _rows(t)

    def body(x_ref, g_ref, h_ref):
        xf = x_ref[...]
        r = lax.rsqrt(jnp.mean(xf * xf, axis=-1, keepdims=True) + EPS)
        h_ref[...] = (xf * r * g_ref[...]).astype(BF16)

    return pl.pallas_call(
        body, name=name, grid=(t // tm,),
        in_specs=[pl.BlockSpec((tm, d), lambda i: (i, 0)), pl.BlockSpec((1, d), lambda i: (0, 0))],
        out_specs=pl.BlockSpec((tm, d), lambda i: (i, 0)), out_shape=jax.ShapeDtypeStruct((t, d), BF16),
        compiler_params=_cp(("parallel",)),
    )(x, g)


def _rms_bwd(name, x, g, dh, dres, want_bf16, deps=()):
    t, d = x.shape
    tm = _rows(t)

    def body(x_ref, g_ref, dh_ref, dres_ref, *rest):
        rest = rest[len(deps):]
        dx_ref, rest = rest[0], rest[1:]
        dg_ref = rest[-1]
        xf = x_ref[...]
        r = lax.rsqrt(jnp.mean(xf * xf, axis=-1, keepdims=True) + EPS)
        xh = xf * r
        dhf = dh_ref[...]
        dxh = dhf * g_ref[...]
        m = jnp.mean(dxh * xh, axis=-1, keepdims=True)
        dx = dres_ref[...] + r * (dxh - xh * m)
        dx_ref[...] = dx
        if want_bf16:
            rest[0][...] = dx.astype(BF16)

        @pl.when(pl.program_id(0) == 0)
        def _():
            dg_ref[...] = jnp.zeros_like(dg_ref)

        dg_ref[...] += jnp.sum(dhf * xh, axis=0, keepdims=True)

    row = pl.BlockSpec((tm, d), lambda i: (i, 0))
    vec = pl.BlockSpec((1, d), lambda i: (0, 0))
    out_specs = [row] + ([row] if want_bf16 else []) + [vec]
    out_shape = ([jax.ShapeDtypeStruct((t, d), F32)] + ([jax.ShapeDtypeStruct((t, d), BF16)] if want_bf16 else [])
                 + [jax.ShapeDtypeStruct((1, d), F32)])
    return pl.pallas_call(body, name=name, grid=(t // tm,), in_specs=[row, vec, row, row] + [ANY] * len(deps),
                          out_specs=out_specs, out_shape=out_shape, compiler_params=_cp(("arbitrary",)))(x, g, dh, dres, *deps)


def _final_bwd(name, x3, gf, tgt):
    t, d = x3.shape
    tm = _rows(t)

    def body(x_ref, g_ref, t_ref, dx_ref, dxb_ref, dg_ref, lc_ref):
        xf = x_ref[...]
        g = g_ref[...]
        r = lax.rsqrt(jnp.mean(xf * xf, axis=-1, keepdims=True) + EPS)
        xh = xf * r
        diff = xh * g - t_ref[...]
        dy = diff * (1.0 / d)
        dxh = dy * g
        m = jnp.mean(dxh * xh, axis=-1, keepdims=True)
        dx = r * (dxh - xh * m)
        dx_ref[...] = dx
        dxb_ref[...] = dx.astype(BF16)

        @pl.when(pl.program_id(0) == 0)
        def _():
            dg_ref[...] = jnp.zeros_like(dg_ref)
            lc_ref[...] = jnp.zeros_like(lc_ref)

        dg_ref[...] += jnp.sum(dy * xh, axis=0, keepdims=True)
        lc_ref[...] += jnp.sum(diff * diff, axis=0, keepdims=True) * (0.5 / d)

    row = pl.BlockSpec((tm, d), lambda i: (i, 0))
    vec = pl.BlockSpec((1, d), lambda i: (0, 0))
    return pl.pallas_call(
        body, name=name, grid=(t // tm,), in_specs=[row, vec, row], out_specs=[row, row, vec, vec],
        out_shape=[jax.ShapeDtypeStruct((t, d), F32), jax.ShapeDtypeStruct((t, d), BF16),
                   jax.ShapeDtypeStruct((1, d), F32), jax.ShapeDtypeStruct((1, d), F32)],
        compiler_params=_cp(("arbitrary",)),
    )(x3, gf, tgt)


def _shift_down(v, k, t_idx):
    return jnp.where(t_idx >= k, pltpu.roll(v, k, 0), 0.0)


def _shift_up(v, k, t_idx):
    n = v.shape[0]
    return jnp.where(t_idx < n - k, pltpu.roll(v, n - k, 0), 0.0)


def _window_sums(v, shift, t_idx, grp):
    s = v + shift(v, 1, t_idx)
    out = s
    for lvl in range(1, len(POOL_WINDOWS)):
        s = s + shift(s, 1 << lvl, t_idx)
        out = jnp.where(grp >= lvl, s, out)
    return out


def _window_count(t_idx, grp):
    return jnp.minimum(t_idx + 1, jnp.left_shift(2, grp)).astype(F32)


MIX_COLS = 128


def _mixer_fwd(name, proj, cw, cb, n_conv, n_groups):
    t = proj.shape[0]
    nb = n_conv // MIX_COLS
    per_group = n_conv // n_groups // MIX_COLS

    def body(ba_ref, ca_ref, va_ref, vb_ref, cw_ref, cb_ref, z_ref, p_ref):
        t_idx = lax.broadcasted_iota(jnp.int32, (t, MIX_COLS), 0)
        q = ca_ref[...].astype(F32) * va_ref[...].astype(F32)
        w = cw_ref[...]
        u = cb_ref[...] + w[0:1] * _shift_down(q, 2, t_idx) + w[1:2] * _shift_down(q, 1, t_idx) + w[2:3] * q
        z_ref[...] = (ba_ref[...].astype(F32) * u).astype(BF16)
        grp = pl.program_id(0) // per_group
        v = vb_ref[...].astype(F32)
        p_ref[...] = (_window_sums(v, _shift_down, t_idx, grp) / _window_count(t_idx, grp) - v).astype(BF16)

    col = lambda s: pl.BlockSpec((t, MIX_COLS), lambda j: (0, s * nb + j))
    return pl.pallas_call(
        body, name=name, grid=(nb,),
        in_specs=[col(0), col(1), col(2), col(3), pl.BlockSpec((3, MIX_COLS), lambda j: (0, j)),
                  pl.BlockSpec((1, MIX_COLS), lambda j: (0, j))],
        out_specs=[col(0), col(0)],
        out_shape=[jax.ShapeDtypeStruct((t, n_conv), BF16), jax.ShapeDtypeStruct((t, n_conv), BF16)],
        compiler_params=_cp(("parallel",)),
    )(proj, proj, proj, proj, cw, cb)


def _mixer_bwd(name, dz, dp, proj, cw, cb, dproj, n_conv, n_groups, deps=()):
    t = proj.shape[0]
    nb = n_conv // MIX_COLS
    per_group = n_conv // n_groups // MIX_COLS

    def body(dz_ref, dp_ref, ba_ref, ca_ref, va_ref, cw_ref, cb_ref, _, *rest):
        o_ref, dcw_ref, dcb_ref, scr = rest[len(deps):]
        s = pl.program_id(1)

        @pl.when(s == 0)
        def _():
            t_idx = lax.broadcasted_iota(jnp.int32, (t, MIX_COLS), 0)
            ca, va = ca_ref[...].astype(F32), va_ref[...].astype(F32)
            q = ca * va
            q1, q2 = _shift_down(q, 1, t_idx), _shift_down(q, 2, t_idx)
            w = cw_ref[...]
            u = cb_ref[...] + w[0:1] * q2 + w[1:2] * q1 + w[2:3] * q
            dzf = dz_ref[...].astype(F32)
            du = dzf * ba_ref[...].astype(F32)
            scr[0] = (dzf * u).astype(BF16)
            dq = w[2:3] * du + w[1:2] * _shift_up(du, 1, t_idx) + w[0:1] * _shift_up(du, 2, t_idx)
            scr[1] = (dq * va).astype(BF16)
            scr[2] = (dq * ca).astype(BF16)
            dcb_ref[...] = jnp.sum(du, axis=0, keepdims=True)
            dcw_ref[0:1, :] = jnp.sum(du * q2, axis=0, keepdims=True)
            dcw_ref[1:2, :] = jnp.sum(du * q1, axis=0, keepdims=True)
            dcw_ref[2:3, :] = jnp.sum(du * q, axis=0, keepdims=True)
            grp = pl.program_id(0) // per_group
            dpf = dp_ref[...].astype(F32)
            e = dpf / _window_count(t_idx, grp)
            scr[3] = (_window_sums(e, _shift_up, t_idx, grp) - dpf).astype(BF16)

        o_ref[...] = scr[s]

    col = lambda c: pl.BlockSpec((t, MIX_COLS), lambda j, s: (0, c * nb + j))
    own = pl.BlockSpec((t, MIX_COLS), lambda j, s: (0, j))
    return pl.pallas_call(
        body, name=name, grid=(nb, 4),
        in_specs=[own, own, col(0), col(1), col(2), pl.BlockSpec((3, MIX_COLS), lambda j, s: (0, j)),
                  pl.BlockSpec((1, MIX_COLS), lambda j, s: (0, j)), ANY] + [ANY] * len(deps),
        out_specs=[pl.BlockSpec((t, MIX_COLS), lambda j, s: (0, s * nb + j)),
                   pl.BlockSpec((3, MIX_COLS), lambda j, s: (0, j)), pl.BlockSpec((1, MIX_COLS), lambda j, s: (0, j))],
        out_shape=[jax.ShapeDtypeStruct(dproj.shape, BF16), jax.ShapeDtypeStruct((3, n_conv), F32),
                   jax.ShapeDtypeStruct((1, n_conv), F32)],
        scratch_shapes=[pltpu.VMEM((4, t, MIX_COLS), BF16)],
        input_output_aliases={7: 0},
        compiler_params=_cp(("arbitrary", "arbitrary")),
    )(dz, dp, proj, proj, proj, cw, cb, dproj, *deps)


def _merge_fwd(name, proj, bg, ya, yb, ps):
    t, d = ya.shape
    tm = _rows(t)

    def body(gab_ref, bg_ref, ya_ref, yb_ref, ps_ref, o_ref):
        gab = gab_ref[...].astype(F32) + bg_ref[...]
        sa, sb = jax.nn.sigmoid(gab[:, :d]), jax.nn.sigmoid(gab[:, d:])
        o_ref[...] = (sa * ya_ref[...].astype(F32) + sb * (yb_ref[...].astype(F32) * ps_ref[...])).astype(BF16)

    row = pl.BlockSpec((tm, d), lambda i: (i, 0))
    return pl.pallas_call(
        body, name=name, grid=(t // tm,),
        in_specs=[pl.BlockSpec((tm, 2 * d), lambda i: (i, 1)), pl.BlockSpec((1, 2 * d), lambda i: (0, 0)), row, row,
                  pl.BlockSpec((1, d), lambda i: (0, 0))],
        out_specs=row, out_shape=jax.ShapeDtypeStruct((t, d), BF16), compiler_params=_cp(("parallel",)),
    )(proj, bg, ya, yb, ps)


def _merge_bwd(name, dm, proj, bg, ya, yb, ps):
    t, d = ya.shape
    tm = _rows(t)

    def body(dm_ref, gab_ref, bg_ref, ya_ref, yb_ref, ps_ref, dya_ref, dyb_ref, dg_ref, dba_ref, dbb_ref, dps_ref):
        gab = gab_ref[...].astype(F32) + bg_ref[...]
        sa, sb = jax.nn.sigmoid(gab[:, :d]), jax.nn.sigmoid(gab[:, d:])
        dmf = dm_ref[...].astype(F32)
        ybf, ps_ = yb_ref[...].astype(F32), ps_ref[...]
        dya_ref[...] = (dmf * sa).astype(BF16)
        dyb = dmf * sb
        dyb_ref[...] = (dyb * ps_).astype(BF16)
        dga = dmf * ya_ref[...].astype(F32) * sa * (1.0 - sa)
        dgb = dmf * (ybf * ps_) * sb * (1.0 - sb)
        dg_ref[:, :d] = dga.astype(BF16)
        dg_ref[:, d:] = dgb.astype(BF16)

        @pl.when(pl.program_id(0) == 0)
        def _():
            dba_ref[...] = jnp.zeros_like(dba_ref)
            dbb_ref[...] = jnp.zeros_like(dbb_ref)
            dps_ref[...] = jnp.zeros_like(dps_ref)

        dba_ref[...] += jnp.sum(dga, axis=0, keepdims=True)
        dbb_ref[...] += jnp.sum(dgb, axis=0, keepdims=True)
        dps_ref[...] += jnp.sum(dyb * ybf, axis=0, keepdims=True)

    row = pl.BlockSpec((tm, d), lambda i: (i, 0))
    vec = pl.BlockSpec((1, d), lambda i: (0, 0))
    gates = pl.BlockSpec((tm, 2 * d), lambda i: (i, 1))
    return pl.pallas_call(
        body, name=name, grid=(t // tm,),
        in_specs=[row, gates, pl.BlockSpec((1, 2 * d), lambda i: (0, 0)), row, row, vec],
        out_specs=[row, row, gates, vec, vec, vec],
        out_shape=[jax.ShapeDtypeStruct((t, d), BF16), jax.ShapeDtypeStruct((t, d), BF16),
                   jax.ShapeDtypeStruct(proj.shape, BF16), jax.ShapeDtypeStruct((1, d), F32),
                   jax.ShapeDtypeStruct((1, d), F32), jax.ShapeDtypeStruct((1, d), F32)],
        compiler_params=_cp(("arbitrary",)),
    )(dm, proj, bg, ya, yb, ps)


def _ffn_act(name, gate, up):
    t, f = gate.shape
    tm, tf = _rows(t), _tile(f, 2048)

    def body(g_ref, u_ref, o_ref):
        g = g_ref[...].astype(F32)
        o_ref[...] = (g * jax.nn.sigmoid(g) * u_ref[...].astype(F32)).astype(BF16)

    blk = pl.BlockSpec((tm, tf), lambda i, j: (i, j))
    return pl.pallas_call(body, name=name, grid=(t // tm, f // tf), in_specs=[blk, blk], out_specs=blk,
                          out_shape=jax.ShapeDtypeStruct((t, f), BF16), compiler_params=_cp(("parallel", "parallel")))(gate, up)


def _ffn_bwd(name, dact, gate, up):
    t, f = gate.shape
    tm, tf = _rows(t), _tile(f, 2048)

    def body(da_ref, g_ref, u_ref, dg_ref, du_ref):
        g, da = g_ref[...].astype(F32), da_ref[...].astype(F32)
        s = jax.nn.sigmoid(g)
        du_ref[...] = (da * (g * s)).astype(BF16)
        dg_ref[...] = (da * u_ref[...].astype(F32) * (s * (1.0 + g * (1.0 - s)))).astype(BF16)

    blk = pl.BlockSpec((tm, tf), lambda i, j: (i, j))
    shp = jax.ShapeDtypeStruct((t, f), BF16)
    return pl.pallas_call(body, name=name, grid=(t // tm, f // tf), in_specs=[blk, blk, blk], out_specs=[blk, blk],
                          out_shape=[shp, shp], compiler_params=_cp(("parallel", "parallel")))(dact, gate, up)


def _adamw_math(w, g, m, v):
    m = ADAM_B1 * m + (1.0 - ADAM_B1) * g
    v = ADAM_B2 * v + (1.0 - ADAM_B2) * (g * g)
    m_hat = m / (1.0 - ADAM_B1 ** ADAM_STEP)
    v_hat = v / (1.0 - ADAM_B2 ** ADAM_STEP)
    delta = -ADAM_LR * (m_hat / (jnp.sqrt(v_hat) + ADAM_EPS) + ADAM_WD * w)
    return delta, m, v


def _adamw(name, w, g, m, v):
    r, c = w.shape
    tr = _tile8(r, 512 if c <= 1024 else 256)

    def body(w_ref, g_ref, m_ref, v_ref, d_ref, nm_ref, nv_ref):
        d_ref[...], nm_ref[...], nv_ref[...] = _adamw_math(w_ref[...], g_ref[...], m_ref[...], v_ref[...])

    blk = pl.BlockSpec((tr, c), lambda i: (i, 0))
    shp = jax.ShapeDtypeStruct((r, c), F32)
    return pl.pallas_call(body, name=name, grid=(r // tr,), in_specs=[blk] * 4, out_specs=[blk] * 3,
                          out_shape=[shp] * 3, compiler_params=_cp(("parallel",)))(w, g, m, v)


class _Weight:
    def __init__(self, name, rows, cols, colshard):
        self.name, self.colshard = name, colshard
        self.R, self.nn = rows // 2, cols
        self.P = 1 if colshard else N_CHIPS
        self.N = N_CHIPS * cols if colshard else cols

    def cols(self, k):
        return pl.ds(pl.multiple_of(k * self.nn, LANES), self.nn)

    def shard(self, ref, k):
        return ref.at[0, :, :, self.cols(k)] if self.colshard else ref.at[k]

    def half(self, ref, k, h):
        return ref.at[0, h, :, self.cols(k)] if self.colshard else ref.at[k, h]

    def part(self, ref, k):
        return ref.at[0, :, self.cols(k)] if self.colshard else ref.at[k]


def _remote(src, dst, ssem, rsem, dev):
    return pltpu.make_async_remote_copy(src_ref=src, dst_ref=dst, send_sem=ssem, recv_sem=rsem, device_id=dev,
                                        device_id_type=MESH)


def _other_chips(x, y):
    chips = [(1 - x, y), (x, 1 - y), (1 - x, 1 - y)]
    return chips, [2 * cx + cy for cx, cy in chips]


def _hbm(a):
    return pltpu.with_memory_space_constraint(a, pltpu.HBM)


def _gather_start(groups, lands):
    flat = [w for grp in groups for w in grp]
    nw, ng = len(flat), len(groups)

    def body(*refs):
        land = refs[:nw]
        sems = refs[nw:nw + 2 * ng]
        token = refs[2 * nw + 2 * ng]
        x, y, c = _mesh_pos()
        k_me = 2 * x + y
        chips, _ = _other_chips(x, y)
        i = 0
        for g, grp in enumerate(groups):
            for wi, w in enumerate(grp):
                mine = w.half(land[i], k_me, c)
                for j, chip in enumerate(chips):
                    _remote(mine, mine, sems[2 * g].at[3 * wi + j], sems[2 * g + 1].at[3 * wi + j], (*chip, c)).start()
                i += 1
        token[...] = jnp.zeros_like(token)

    sem_shapes = []
    for grp in groups:
        sem_shapes += [pltpu.SemaphoreType.DMA((3 * len(grp),))] * 2
    out = pl.pallas_call(
        body, name="gather_start", in_specs=[HBM] * nw,
        out_specs=[SEM] * (2 * ng) + [HBM] * nw + [VMEM],
        out_shape=sem_shapes + [pltpu.HBM(a.shape, a.dtype) for a in lands] + [jax.ShapeDtypeStruct((8, LANES), F32)],
        input_output_aliases={i: 2 * ng + i for i in range(nw)},
        compiler_params=pltpu.CompilerParams(has_side_effects=EFFECT),
    )(*[_hbm(a) for a in lands])
    sems = [(out[2 * g], out[2 * g + 1]) for g in range(ng)]
    return sems, list(out[2 * ng:2 * ng + nw]), out[-1]


def _gather_wait(name, grp, lands, ssem, rsem, after):
    n = len(grp)

    def body(*refs):
        land, ssem_ref, rsem_ref = refs[:n], refs[n], refs[n + 1]
        x, y, c = _mesh_pos()
        k_me = 2 * x + y
        chips, ks = _other_chips(x, y)
        for wi, w in enumerate(grp):
            for j, chip in enumerate(chips):
                cp = _remote(w.half(land[wi], k_me, c), w.half(land[wi], ks[j], c), ssem_ref.at[3 * wi + j],
                             rsem_ref.at[3 * wi + j], (*chip, c))
                cp.wait_send()
                cp.wait_recv()

    return pl.pallas_call(
        body, name=name, in_specs=[HBM] * n + [SEM, SEM, ANY], out_specs=[HBM] * n,
        out_shape=[pltpu.HBM(a.shape, a.dtype) for a in lands], input_output_aliases={i: i for i in range(n)},
        compiler_params=pltpu.CompilerParams(has_side_effects=EFFECT),
    )(*lands, ssem, rsem, after)


def _gather_pass(name, grp, lands):
    n = len(grp)

    def body(*refs):
        out = refs[n:2 * n]
        ssem, rsem = refs[2 * n:]
        x, y, c = _mesh_pos()
        _, ks = _other_chips(x, y)
        sib = (x, y, 1 - c)
        cps = []
        for wi, w in enumerate(grp):
            for j in range(3):
                got = w.half(out[wi], ks[j], c)
                cps.append(_remote(got, got, ssem.at[3 * wi + j], rsem.at[3 * wi + j], sib))
        for cp in cps:
            cp.start()
        for wi, w in enumerate(grp):
            for j in range(3):
                theirs = w.half(out[wi], ks[j], 1 - c)
                _remote(theirs, theirs, ssem.at[3 * wi + j], rsem.at[3 * wi + j], sib).wait_recv()
        for cp in cps:
            cp.wait_send()

    return pl.pallas_call(
        body, name=name, in_specs=[ANY] * n, out_specs=[ANY] * n,
        out_shape=[jax.ShapeDtypeStruct(a.shape, a.dtype) for a in lands],
        scratch_shapes=[pltpu.SemaphoreType.DMA((3 * n,)), pltpu.SemaphoreType.DMA((3 * n,))],
        input_output_aliases={i: i for i in range(n)},
    )(*lands)


def _gather_conv_w(cw):
    ncw = cw.shape[1]

    def body(cw_ref, out_ref, ssem, rsem):
        x, y, c = _mesh_pos()
        k_me = 2 * x + y
        chips, ks = _other_chips(x, y)
        cols = lambda k: out_ref.at[:, pl.ds(pl.multiple_of(k * ncw, LANES), ncw)]
        cps = [_remote(cw_ref, cols(k_me), ssem.at[j], rsem.at[j], (*chip, c)) for j, chip in enumerate(chips)]
        for cp in cps:
            cp.start()
        for k in range(N_CHIPS):
            @pl.when(k_me == k)
            def _():
                out_ref[:, k * ncw:(k + 1) * ncw] = cw_ref[...]
        for j in range(3):
            _remote(cw_ref, cols(ks[j]), ssem.at[j], rsem.at[j], (*chips[j], c)).wait_recv()
        for cp in cps:
            cp.wait_send()

    return pl.pallas_call(
        body, name="gather_conv_w", in_specs=[VMEM], out_specs=VMEM,
        out_shape=jax.ShapeDtypeStruct((3, N_CHIPS * ncw), F32),
        scratch_shapes=[pltpu.SemaphoreType.DMA((3,)), pltpu.SemaphoreType.DMA((3,))],
    )(cw)


def _pair_exchange(name, ws, grads):
    nw = len(ws)

    def body(*refs):
        g, out = refs[:nw], refs[nw:2 * nw]
        ssem, rsem = refs[2 * nw:]
        x, y, c = _mesh_pos()
        sib = (x, y, 1 - c)
        cps = [_remote(g[i].at[:, 1 - c], out[i], ssem.at[i], rsem.at[i], sib) for i in range(nw)]
        for cp in cps:
            cp.start()
        for cp in cps:
            cp.wait()

    return pl.pallas_call(
        body, name=name, in_specs=[ANY] * nw, out_specs=[ANY] * nw,
        out_shape=[jax.ShapeDtypeStruct((w.P, w.R, w.N), BF16) for w in ws],
        scratch_shapes=[pltpu.SemaphoreType.DMA((nw,)), pltpu.SemaphoreType.DMA((nw,))],
    )(*grads)


def _grad_tiles(w, n):
    return _tile8(w.R, 512) if w.R <= 512 else w.R // 2, _tile(n, 2048)


def _pair_sum(name, w, pos, grad, got):
    tr, tn = _grad_tiles(w, w.N)

    def body(pos_ref, g_ref, r_ref, o_ref):
        o_ref[...] = (g_ref[...].astype(F32) + r_ref[...].astype(F32)).astype(BF16)

    blk = pl.BlockSpec((None, tr, tn), lambda p, i, j, pos: (p, i, j))
    grid_spec = pltpu.PrefetchScalarGridSpec(
        num_scalar_prefetch=1, grid=(w.P, w.R // tr, w.N // tn),
        in_specs=[pl.BlockSpec((None, None, tr, tn), lambda p, i, j, pos: (p, pos[0], i, j)), blk], out_specs=blk)
    return pl.pallas_call(body, name=name, grid_spec=grid_spec, out_shape=jax.ShapeDtypeStruct((w.P, w.R, w.N), BF16),
                          compiler_params=_cp(("parallel",) * 3))(pos, grad, got)


def _scatter_start(name, ws, pairs):
    nw = len(ws)

    def body(*refs):
        pr, land = refs[:nw], refs[nw:2 * nw]
        ssem, rsem = refs[2 * nw], refs[2 * nw + 1]
        token = refs[4 * nw + 2]
        x, y, c = _mesh_pos()
        chips, ks = _other_chips(x, y)
        for i, w in enumerate(ws):
            for j, chip in enumerate(chips):
                _remote(w.part(pr[i], ks[j]), land[i].at[j], ssem.at[3 * i + j], rsem.at[3 * i + j], (*chip, c)).start()
        token[...] = jnp.zeros_like(token)

    lands = [lax.empty((3, w.R, w.nn), BF16) for w in ws]
    out = pl.pallas_call(
        body, name=name, in_specs=[HBM] * (2 * nw),
        out_specs=[SEM, SEM] + [HBM] * (2 * nw) + [VMEM],
        out_shape=[pltpu.SemaphoreType.DMA((3 * nw,))] * 2 + [pltpu.HBM(a.shape, a.dtype) for a in pairs + lands]
        + [jax.ShapeDtypeStruct((8, LANES), F32)],
        input_output_aliases={i: 2 + i for i in range(2 * nw)},
        compiler_params=pltpu.CompilerParams(has_side_effects=EFFECT),
    )(*[_hbm(a) for a in pairs + lands])
    return out[0], out[1], list(out[2:2 + nw]), list(out[2 + nw:2 + 2 * nw]), out[-1]


def _scatter_wait(name, ws, pairs, lands, ssem, rsem, after):
    nw = len(ws)

    def body(*refs):
        pr, land = refs[:nw], refs[nw:2 * nw]
        ssem_ref, rsem_ref = refs[2 * nw], refs[2 * nw + 1]
        x, y, c = _mesh_pos()
        chips, ks = _other_chips(x, y)
        for i, w in enumerate(ws):
            for j, chip in enumerate(chips):
                cp = _remote(w.part(pr[i], ks[j]), land[i].at[j], ssem_ref.at[3 * i + j], rsem_ref.at[3 * i + j], (*chip, c))
                cp.wait_send()
                cp.wait_recv()

    out = pl.pallas_call(
        body, name=name, in_specs=[HBM] * (2 * nw) + [SEM, SEM, ANY], out_specs=[HBM] * (2 * nw),
        out_shape=[pltpu.HBM(a.shape, a.dtype) for a in pairs + lands],
        input_output_aliases={i: i for i in range(2 * nw)},
        compiler_params=pltpu.CompilerParams(has_side_effects=EFFECT),
    )(*pairs, *lands, ssem, rsem, after)
    return list(out[nw:])


def _final_sum(name, w, pos, grad, got, parts):
    tr, tn = _grad_tiles(w, w.nn)
    nbc = w.nn // tn

    def body(pos_ref, g_ref, r_ref, p_ref, o_ref):
        acc = g_ref[...].astype(F32) + r_ref[...].astype(F32)
        for j in range(3):
            acc = acc + p_ref[j].astype(F32)
        o_ref[...] = acc

    if w.colshard:
        g_spec = pl.BlockSpec((None, None, tr, tn), lambda i, j, pos: (0, pos[0], i, pos[1] * nbc + j))
        r_spec = pl.BlockSpec((None, tr, tn), lambda i, j, pos: (0, i, pos[1] * nbc + j))
    else:
        g_spec = pl.BlockSpec((None, None, tr, tn), lambda i, j, pos: (pos[1], pos[0], i, j))
        r_spec = pl.BlockSpec((None, tr, tn), lambda i, j, pos: (pos[1], i, j))
    grid_spec = pltpu.PrefetchScalarGridSpec(
        num_scalar_prefetch=1, grid=(w.R // tr, nbc),
        in_specs=[g_spec, r_spec, pl.BlockSpec((3, tr, tn), lambda i, j, pos: (0, i, j))],
        out_specs=pl.BlockSpec((None, tr, tn), lambda i, j, pos: (pos[0], i, j)))
    return pl.pallas_call(body, name=name, grid_spec=grid_spec, out_shape=jax.ShapeDtypeStruct((2, w.R, w.nn), F32),
                          compiler_params=_cp(("parallel",) * 2))(pos, grad, got, parts)


def _share_halves(name, ws, halves):
    nw = len(ws)

    def body(*refs):
        out = refs[nw:2 * nw]
        ssem, rsem = refs[2 * nw:]
        x, y, c = _mesh_pos()
        sib = (x, y, 1 - c)
        cps = [_remote(out[i].at[c], out[i].at[c], ssem.at[i], rsem.at[i], sib) for i in range(nw)]
        for cp in cps:
            cp.start()
        for i, cp in enumerate(cps):
            cp.wait_send()
            _remote(out[i].at[1 - c], out[i].at[1 - c], ssem.at[i], rsem.at[i], sib).wait_recv()

    return pl.pallas_call(
        body, name=name, in_specs=[ANY] * nw, out_specs=[ANY] * nw,
        out_shape=[jax.ShapeDtypeStruct(h.shape, F32) for h in halves],
        scratch_shapes=[pltpu.SemaphoreType.DMA((nw,)), pltpu.SemaphoreType.DMA((nw,))],
        input_output_aliases={i: i for i in range(nw)},
    )(*halves)


VEC_ROWS = 16


def _vector_step(d, n_conv, parts, params):
    ncw = params[2][0].shape[1]
    n_par = len(params)

    def body(*refs):
        dg1, dba, dbb, dcw, dcb, dps, dg2, dgf, lc = refs[:9]
        wmv = refs[9:9 + 3 * n_par]
        outs = refs[9 + 3 * n_par:9 + 7 * n_par]
        loss_ref = refs[9 + 7 * n_par]
        snd, got, ssem, rsem = refs[9 + 7 * n_par + 1:]
        x, y, c = _mesh_pos()
        me = 4 * x + 2 * y + c
        snd[...] = jnp.zeros_like(snd)
        for row, ref in ((0, dg1), (1, dba), (2, dbb), (3, dps), (4, dg2), (5, dgf), (6, lc)):
            snd[row:row + 1, :] = ref[...]
        snd[7:8, :n_conv] = dcb[...]
        snd[8:11, :n_conv] = dcw[...]
        cps = []
        for r in range(1, N_DEV):
            peer = tuple(1 - p if (r >> b) & 1 else p for p, b in ((x, 2), (y, 1), (c, 0)))
            cps.append(_remote(snd, got.at[me], ssem.at[r - 1], rsem.at[r - 1], peer))
        for cp in cps:
            cp.start()
        got[me] = snd[...]
        for r in range(1, N_DEV):
            peer = tuple(1 - p if (r >> b) & 1 else p for p, b in ((x, 2), (y, 1), (c, 0)))
            _remote(snd, got.at[4 * peer[0] + 2 * peer[1] + peer[2]], ssem.at[r - 1], rsem.at[r - 1], peer).wait_recv()
        for cp in cps:
            cp.wait_send()
        tot = got[0]
        for dev in range(1, N_DEV):
            tot = tot + got[dev]
        loss_ref[...] = jnp.sum(tot[6:7, :], axis=1, keepdims=True)
        k_me = 2 * x + y
        g_cw = jnp.zeros((3, ncw), F32)
        for k in range(N_CHIPS):
            g_cw = g_cw + jnp.where(k_me == k, tot[8:11, k * ncw:(k + 1) * ncw], 0.0)
        grads = [tot[0:1, :], jnp.concatenate([tot[1:2, :], tot[2:3, :]], axis=1), g_cw, tot[7:8, :n_conv],
                 tot[3:4, :], tot[4:5, :], tot[5:6, :]]
        for i, g in enumerate(grads):
            w_ref, m_ref, v_ref = wmv[3 * i:3 * i + 3]
            delta, nm, nv = _adamw_math(w_ref[...], g, m_ref[...], v_ref[...])
            outs[4 * i][...] = g
            outs[4 * i + 1][...] = delta
            outs[4 * i + 2][...] = nm
            outs[4 * i + 3][...] = nv

    args = list(parts)
    out_shape = []
    for w, m, v in params:
        args += [w, m, v]
        out_shape += [jax.ShapeDtypeStruct(w.shape, F32)] * 4
    out_shape.append(jax.ShapeDtypeStruct((1, 1), F32))
    return pl.pallas_call(
        body, name="vector_params_step", in_specs=[VMEM] * len(args), out_specs=[VMEM] * len(out_shape),
        out_shape=out_shape,
        scratch_shapes=[pltpu.VMEM((VEC_ROWS, d), F32), pltpu.VMEM((N_DEV, VEC_ROWS, d), F32),
                        pltpu.SemaphoreType.DMA((N_DEV - 1,)), pltpu.SemaphoreType.DMA((N_DEV - 1,))],
        compiler_params=pltpu.CompilerParams(vmem_limit_bytes=VMEM_LIMIT),
    )(*args)


def kernel(x, norm1_g, w_in, b_gate, conv_w, conv_b, w_a_out, w_pool, pool_scale, w_o, norm2_g, w_ffn_gate, w_ffn_up, w_ffn_down, final_g, loss_target, m_norm1_g, m_w_in, m_b_gate, m_conv_w, m_conv_b, m_w_a_out, m_w_pool, m_pool_scale, m_w_o, m_norm2_g, m_w_ffn_gate, m_w_ffn_up, m_w_ffn_down, m_final_g, v_norm1_g, v_w_in, v_b_gate, v_conv_w, v_conv_b, v_w_a_out, v_w_pool, v_pool_scale, v_w_o, v_norm2_g, v_w_ffn_gate, v_w_ffn_up, v_w_ffn_down, v_final_g):
    t, d = x.shape[1], x.shape[2]
    n_conv = conv_b.shape[1]
    n_groups, pool_cg, pool_dg = w_pool.shape[1], w_pool.shape[2], N_CHIPS * w_pool.shape[3]
    d_ff = N_CHIPS * w_ffn_gate.shape[2]
    assert n_conv // n_groups == pool_cg and n_conv % (n_groups * MIX_COLS) == 0 and n_groups == len(POOL_WINDOWS)

    big = {"w_in": (w_in, m_w_in, v_w_in), "w_a_out": (w_a_out, m_w_a_out, v_w_a_out), "w_pool": (w_pool, m_w_pool, v_w_pool),
           "w_o": (w_o, m_w_o, v_w_o), "w_ffn_gate": (w_ffn_gate, m_w_ffn_gate, v_w_ffn_gate),
           "w_ffn_up": (w_ffn_up, m_w_ffn_up, v_w_ffn_up), "w_ffn_down": (w_ffn_down, m_w_ffn_down, v_w_ffn_down)}
    colshard = {"w_in": True, "w_a_out": True, "w_pool": True, "w_o": False, "w_ffn_gate": True, "w_ffn_up": True,
                "w_ffn_down": False}
    names = list(big)
    shard2d = {n: big[n][0].reshape(-1, big[n][0].shape[-1]) for n in names}
    ws = [_Weight(n, *shard2d[n].shape, colshard[n]) for n in names]

    xs, tgt = x[0], loss_target[0]
    cw_loc = conv_w[0]
    pos = jnp.stack([lax.axis_index("c"), 2 * lax.axis_index("x") + lax.axis_index("y")]).astype(jnp.int32)
    by_name = {w.name: w for w in ws}
    groups = [[by_name[n] for n in g] for g in (["w_in"], ["w_a_out", "w_pool", "w_o"], ["w_ffn_gate", "w_ffn_up"],
                                                 ["w_ffn_down"])]
    first = [sum(len(g) for g in groups[:i]) for i in range(len(groups))]

    lands = [_cast_place(f"cast_{w.name}", w, pos, shard2d[w.name].reshape(2, w.R, w.nn)) for grp in groups for w in grp]
    gsems, lands, _ = _gather_start(groups, lands)
    cw_full = _gather_conv_w(cw_loc)
    full = {}

    def arrive(g, after):
        grp = groups[g]
        got = _gather_wait(f"gather_wait_{g}", grp, lands[first[g]:first[g] + len(grp)], *gsems[g], after)
        got = _gather_pass(f"gather_pass_{g}", grp, got)
        full.update({w.name: a.reshape(w.P * 2 * w.R, w.N) for w, a in zip(grp, got)})

    h1 = _rms_fwd("norm1_fwd", xs, norm1_g)
    arrive(0, h1)
    proj = _mm_nn("proj_in", h1, full["w_in"], BF16)
    z, p = _mixer_fwd("mixer_fwd", proj, cw_full, conv_b, n_conv, n_groups)
    arrive(1, z)
    wp_full = full["w_pool"].reshape(n_groups, pool_cg, pool_dg)
    ya = _mm_nn("conv_out", z, full["w_a_out"], BF16)
    yb = _gmm_nn("pool_out", p, wp_full, BF16)
    merged = _merge_fwd("merge_fwd", proj, b_gate, ya, yb, pool_scale)
    x2 = _mm_nn("mix_out", merged, full["w_o"], F32, add=xs)
    h2 = _rms_fwd("norm2_fwd", x2, norm2_g)
    arrive(2, h2)
    gate = _mm_nn("ffn_gate", h2, full["w_ffn_gate"], BF16)
    up = _mm_nn("ffn_up", h2, full["w_ffn_up"], BF16)
    act = _ffn_act("ffn_act", gate, up)
    arrive(3, act)
    x3 = _mm_nn("ffn_down", act, full["w_ffn_down"], F32, add=x2, tk=d_ff // 4)

    pending = {}

    def reduce_start(g, grads):
        grp = groups[g]
        gcan = [grads[w.name].reshape(w.P, 2, w.R, w.N) for w in grp]
        sib = _pair_exchange(f"pair_exchange_{g}", grp, gcan)
        pairs = [_pair_sum(f"pair_sum_{w.name}", w, pos, a, s) for w, a, s in zip(grp, gcan, sib)]
        ssem, rsem, pairs, slots, token = _scatter_start(f"scatter_start_{g}", grp, pairs)
        pending[g] = (gcan, sib, pairs, slots, ssem, rsem)
        return token

    def reduce_finish(g, after):
        grp = groups[g]
        gcan, sib, pairs, slots, ssem, rsem = pending[g]
        parts = _scatter_wait(f"scatter_wait_{g}", grp, pairs, slots, ssem, rsem, after)
        return [_final_sum(f"final_sum_{w.name}", w, pos, a, s, q) for w, a, s, q in zip(grp, gcan, sib, parts)]

    grads = {}
    dx3, dx3b, d_gf, loss_cols = _final_bwd("final_bwd", x3, final_g.reshape(1, d), tgt)
    dact = _mm_nt("d_act", [(dx3b, full["w_ffn_down"])], BF16, tk=d)
    dgate, dup = _ffn_bwd("ffn_bwd", dact, gate, up)
    grads["w_ffn_down"] = _mm_tn("dw_ffn_down", act, dx3b, BF16)
    tok = reduce_start(3, grads)
    dh2 = _mm_nt("d_h2", [(dgate, full["w_ffn_gate"]), (dup, full["w_ffn_up"])], F32, tk=d_ff // 4, deps=[tok])
    grads["w_ffn_gate"] = _mm_tn("dw_ffn_gate", h2, dgate, BF16)
    grads["w_ffn_up"] = _mm_tn("dw_ffn_up", h2, dup, BF16)
    tok = reduce_start(2, grads)
    dx2, dx2b, d_g2 = _rms_bwd("norm2_bwd", x2, norm2_g, dh2, dx3, True, deps=[tok])
    dmerged = _mm_nt("d_merged", [(dx2b, full["w_o"])], BF16, tk=d)
    grads["w_o"] = _mm_tn("dw_o", merged, dx2b, BF16)
    dya, dyb, dproj, d_bga, d_bgb, d_ps = _merge_bwd("merge_bwd", dmerged, proj, b_gate, ya, yb, pool_scale)
    dz = _mm_nt("d_z", [(dya, full["w_a_out"])], BF16, tk=d)
    grads["w_a_out"] = _mm_tn("dw_a_out", z, dya, BF16)
    dp = _gmm_nt("d_pool", dyb, wp_full, BF16)
    grads["w_pool"] = _gmm_tn("dw_pool", p, dyb, n_groups, BF16)
    tok = reduce_start(1, grads)
    dproj, d_cw, d_cb = _mixer_bwd("mixer_bwd", dz, dp, proj, cw_full, conv_b, dproj, n_conv, n_groups, deps=[tok])
    dh1 = _mm_nt("d_h1", [(dproj, full["w_in"])], F32, tk=proj.shape[1] // 4)
    grads["w_in"] = _mm_tn("dw_in", h1, dproj, BF16)
    tok = reduce_start(0, grads)
    grad_x, d_g1 = _rms_bwd("norm1_bwd", xs, norm1_g, dh1, dx2, False, deps=[tok])

    g_big, d_big, m_big, v_big = {}, {}, {}, {}

    def update(name, wsub, halves):
        out = None
        for w, g in zip(wsub, _share_halves(name, wsub, halves)):
            wt, mt, vt = big[w.name]
            g2 = g.reshape(2 * w.R, w.nn)
            dl, nm, nv = _adamw(f"adamw_{w.name}", shard2d[w.name], g2, mt.reshape(g2.shape), vt.reshape(g2.shape))
            g_big[w.name], d_big[w.name], m_big[w.name], v_big[w.name] = (a.reshape(wt.shape) for a in (g2, dl, nm, nv))
            out = nv
        return out

    after, early, early_halves = grad_x, [], []
    for g in (3, 2, 1):
        halves = reduce_finish(g, after)
        early += groups[g]
        early_halves += halves
        after = halves[-1]
    after = update("share_halves_early", early, early_halves)
    update("share_halves_w_in", groups[0], reduce_finish(0, after))

    vec_names = ["norm1_g", "b_gate", "conv_w", "conv_b", "pool_scale", "norm2_g", "final_g"]
    vec = {"norm1_g": (norm1_g, m_norm1_g, v_norm1_g), "b_gate": (b_gate, m_b_gate, v_b_gate),
           "conv_w": (cw_loc, m_conv_w[0], v_conv_w[0]), "conv_b": (conv_b, m_conv_b, v_conv_b),
           "pool_scale": (pool_scale, m_pool_scale, v_pool_scale), "norm2_g": (norm2_g, m_norm2_g, v_norm2_g),
           "final_g": tuple(a.reshape(1, d) for a in (final_g, m_final_g, v_final_g))}
    vout = _vector_step(d, n_conv, [d_g1, d_bga, d_bgb, d_cw, d_cb, d_ps, d_g2, d_gf, loss_cols],
                        [vec[n] for n in vec_names])
    shapes = {"conv_w": conv_w.shape, "final_g": final_g.shape}
    g_vec, d_vec, m_vec, v_vec = ({n: vout[4 * i + q].reshape(shapes.get(n, vec[n][0].shape)) for i, n in enumerate(vec_names)}
                                  for q in range(4))
    loss = vout[-1].reshape(())

    order = ["norm1_g", "w_in", "b_gate", "conv_w", "conv_b", "w_a_out", "w_pool", "pool_scale", "w_o", "norm2_g",
             "w_ffn_gate", "w_ffn_up", "w_ffn_down", "final_g"]
    pick = lambda vecs, bigs: [vecs[n] if n in vecs else bigs[n] for n in order]
    return (loss, grad_x.reshape(x.shape), *pick(g_vec, g_big), *pick(d_vec, d_big), *pick(m_vec, m_big),
            *pick(v_vec, v_big))
```

```python
import functools

import jax
import jax.numpy as jnp
from jax import lax
from jax.experimental import pallas as pl
from jax.experimental.pallas import tpu as pltpu

F32, BF16 = jnp.float32, jnp.bfloat16
MESH = pl.DeviceIdType.MESH
ANY = pl.BlockSpec(memory_space=pl.ANY)
VMEM = pl.BlockSpec(memory_space=pltpu.VMEM)
HBM = pl.BlockSpec(memory_space=pltpu.HBM)
SEM = pl.BlockSpec(memory_space=pltpu.SEMAPHORE)
EFFECT = pltpu.SideEffectType.DATAFLOW_SIDE_EFFECTING

EPS = 1e-6
POOL_WINDOWS = (2, 4, 8, 16)
ADAM_LR, ADAM_B1, ADAM_B2, ADAM_EPS, ADAM_WD, ADAM_STEP = 0.001, 0.9, 0.999, 1e-08, 0.01, 10

V7X_VMEM_BYTES = 64 * 1024 * 1024
VMEM_LIMIT = V7X_VMEM_BYTES * 3 // 4
LANES = 128
N_CHIPS = 4
N_DEV = 8

_DIMS = {
    "nn": (((1,), (0,)), ((), ())),
    "nt": (((1,), (1,)), ((), ())),
    "tn": (((0,), (0,)), ((), ())),
}


def _cp(sem):
    return pltpu.CompilerParams(dimension_semantics=sem, vmem_limit_bytes=VMEM_LIMIT)


def _mesh_pos():
    return lax.axis_index("x"), lax.axis_index("y"), lax.axis_index("c")


def _mm(name, pairs, *, mode, grid, out_shape, o_spec, nk=1, kaxis=None, add=None, deps=()):
    npair = len(pairs)
    has_add = add is not None

    def body(*refs):
        ab = refs[: 2 * npair]
        pos = 2 * npair
        add_ref = refs[pos] if has_add else None
        pos += int(has_add) + len(deps)
        o_ref = refs[pos]
        acc_ref = refs[pos + 1] if nk > 1 else None
        d = None
        for p in range(npair):
            t = lax.dot_general(ab[2 * p][...], ab[2 * p + 1][...], _DIMS[mode], preferred_element_type=F32)
            d = t if d is None else d + t
        if nk == 1:
            if has_add:
                d = d + add_ref[...].astype(F32)
            o_ref[...] = d.astype(o_ref.dtype)
        else:
            k = pl.program_id(kaxis)

            @pl.when(k == 0)
            def _():
                acc_ref[...] = d

            @pl.when(k > 0)
            def _():
                acc_ref[...] += d

            @pl.when(k == nk - 1)
            def _():
                r = acc_ref[...]
                if has_add:
                    r = r + add_ref[...].astype(F32)
                o_ref[...] = r.astype(o_ref.dtype)

    args, specs = [], []
    for a, a_spec, b, b_spec in pairs:
        args += [a, b]
        specs += [a_spec, b_spec]
    if has_add:
        args.append(add[0])
        specs.append(add[1])
    args += list(deps)
    specs += [ANY] * len(deps)
    scratch = []
    if nk > 1:
        blk = [d for d in o_spec.block_shape if d is not None]
        scratch = [pltpu.VMEM(tuple(blk), F32)]
    sem = tuple("arbitrary" if (nk > 1 and ax == kaxis) else "parallel" for ax in range(len(grid)))
    return pl.pallas_call(
        body, name=name, grid=grid, in_specs=specs, out_specs=o_spec, out_shape=out_shape,
        scratch_shapes=scratch, compiler_params=_cp(sem),
    )(*args)


def _tile(n, pref):
    if n <= pref:
        return n
    for t in range(pref, 0, -LANES):
        if t % LANES == 0 and n % t == 0:
            return t
    raise ValueError(f"no tile for {n}")


def _mm_nn(name, a, b, out_dtype, add=None, tk=None):
    m, kk = a.shape
    n = b.shape[1]
    tm, tn = _tile(m, 1024), _tile(n, 512)
    out_shape = jax.ShapeDtypeStruct((m, n), out_dtype)
    if tk is None or tk == kk:
        grid = (m // tm, n // tn)
        pairs = [(a, pl.BlockSpec((tm, kk), lambda i, j: (i, 0)), b, pl.BlockSpec((kk, tn), lambda i, j: (0, j)))]
        o_spec = pl.BlockSpec((tm, tn), lambda i, j: (i, j))
        add_ = None if add is None else (add, pl.BlockSpec((tm, tn), lambda i, j: (i, j)))
        return _mm(name, pairs, mode="nn", grid=grid, out_shape=out_shape, o_spec=o_spec, add=add_)
    tn = _tile(n, 1024)
    nk = kk // tk
    grid = (m // tm, n // tn, nk)
    pairs = [(a, pl.BlockSpec((tm, tk), lambda i, j, k: (i, k)), b, pl.BlockSpec((tk, tn), lambda i, j, k: (k, j)))]
    o_spec = pl.BlockSpec((tm, tn), lambda i, j, k: (i, j))
    add_ = None if add is None else (add, pl.BlockSpec((tm, tn), lambda i, j, k: (i, j)))
    return _mm(name, pairs, mode="nn", grid=grid, out_shape=out_shape, o_spec=o_spec, nk=nk, kaxis=2, add=add_)


def _mm_nt(name, abs_, out_dtype, tk, deps=()):
    m, kk = abs_[0][0].shape
    n = abs_[0][1].shape[0]
    tm = _tile(m, 1024)
    nk = kk // tk
    tn = _tile(n, 512 if nk == 1 else 1024)
    out_shape = jax.ShapeDtypeStruct((m, n), out_dtype)
    if nk == 1:
        grid = (m // tm, n // tn)
        pairs = [(a, pl.BlockSpec((tm, kk), lambda i, j: (i, 0)), b, pl.BlockSpec((tn, kk), lambda i, j: (j, 0)))
                 for a, b in abs_]
        o_spec = pl.BlockSpec((tm, tn), lambda i, j: (i, j))
        return _mm(name, pairs, mode="nt", grid=grid, out_shape=out_shape, o_spec=o_spec, deps=deps)
    grid = (m // tm, n // tn, nk)
    pairs = [(a, pl.BlockSpec((tm, tk), lambda i, j, k: (i, k)), b, pl.BlockSpec((tn, tk), lambda i, j, k: (j, k)))
             for a, b in abs_]
    o_spec = pl.BlockSpec((tm, tn), lambda i, j, k: (i, j))
    return _mm(name, pairs, mode="nt", grid=grid, out_shape=out_shape, o_spec=o_spec, nk=nk, kaxis=2, deps=deps)


def _mm_tn(name, a, b, out_dtype, deps=()):
    t, m = a.shape
    n = b.shape[1]
    tm, tn = _tile(m, 512), _tile(n, 2048)
    if n > m:
        grid = (n // tn, m // tm)
        a_map, b_map, o_map = (lambda j, i: (0, i)), (lambda j, i: (0, j)), (lambda j, i: (i, j))
    else:
        grid = (m // tm, n // tn)
        a_map, b_map, o_map = (lambda i, j: (0, i)), (lambda i, j: (0, j)), (lambda i, j: (i, j))
    pairs = [(a, pl.BlockSpec((t, tm), a_map), b, pl.BlockSpec((t, tn), b_map))]
    o_spec = pl.BlockSpec((tm, tn), o_map)
    return _mm(name, pairs, mode="tn", grid=grid, out_shape=jax.ShapeDtypeStruct((m, n), out_dtype), o_spec=o_spec,
               deps=deps)


def _gmm_nn(name, p, w, out_dtype):
    t = p.shape[0]
    g, cg, dg = w.shape
    tm = _tile(t, 1024)
    pairs = [(p, pl.BlockSpec((tm, cg), lambda i, j: (i, j)), w, pl.BlockSpec((None, cg, dg), lambda i, j: (j, 0, 0)))]
    o_spec = pl.BlockSpec((tm, dg), lambda i, j: (i, j))
    return _mm(name, pairs, mode="nn", grid=(t // tm, g), out_shape=jax.ShapeDtypeStruct((t, g * dg), out_dtype),
               o_spec=o_spec)


def _gmm_nt(name, dy, w, out_dtype):
    t = dy.shape[0]
    g, cg, dg = w.shape
    tm = _tile(t, 1024)
    pairs = [(dy, pl.BlockSpec((tm, dg), lambda i, j: (i, j)), w, pl.BlockSpec((None, cg, dg), lambda i, j: (j, 0, 0)))]
    o_spec = pl.BlockSpec((tm, cg), lambda i, j: (i, j))
    return _mm(name, pairs, mode="nt", grid=(t // tm, g), out_shape=jax.ShapeDtypeStruct((t, g * cg), out_dtype),
               o_spec=o_spec)


def _gmm_tn(name, p, dy, g, out_dtype):
    t = p.shape[0]
    cg, dg = p.shape[1] // g, dy.shape[1] // g
    pairs = [(p, pl.BlockSpec((t, cg), lambda j: (0, j)), dy, pl.BlockSpec((t, dg), lambda j: (0, j)))]
    o_spec = pl.BlockSpec((None, cg, dg), lambda j: (j, 0, 0))
    return _mm(name, pairs, mode="tn", grid=(g,), out_shape=jax.ShapeDtypeStruct((g, cg, dg), out_dtype), o_spec=o_spec)


ROW_TILE = 256


def _rows(t):
    return _tile8(t, ROW_TILE)


def _tile8(n, pref):
    if n <= pref:
        return n
    for t in range(pref, 0, -8):
        if n % t == 0:
            return t
    raise ValueError(f"no row tile for {n}")


def _cast_place(name, w, pos, shard, deps=()):
    tr = _tile8(w.R, 512)
    if w.colshard:
        o_map = lambda h, i, pos: (0, h, i, pos[1])
    else:
        o_map = lambda h, i, pos: (pos[1], h, i, 0)

    def body(pos_ref, w_ref, *rest):
        rest[-1][...] = w_ref[...].astype(BF16)

    grid_spec = pltpu.PrefetchScalarGridSpec(
        num_scalar_prefetch=1, grid=(2, w.R // tr),
        in_specs=[pl.BlockSpec((None, tr, w.nn), lambda h, i, pos: (h, i, 0))] + [ANY] * len(deps),
        out_specs=pl.BlockSpec((None, None, tr, w.nn), o_map))
    return pl.pallas_call(body, name=name, grid_spec=grid_spec, out_shape=jax.ShapeDtypeStruct((w.P, 2, w.R, w.N), BF16),
                          compiler_params=_cp(("parallel", "parallel")))(pos, shard, *deps)


def _rms_fwd(name, x, g):
    t, d = x.shape
    tm = _rows(t)

    def body(x_ref, g_ref, h_ref):
        xf = x_ref[...]
        r = lax.rsqrt(jnp.mean(xf * xf, axis=-1, keepdims=True) + EPS)
        h_ref[...] = (xf * r * g_ref[...]).astype(BF16)

    return pl.pallas_call(
        body, name=name, grid=(t // tm,),
        in_specs=[pl.BlockSpec((tm, d), lambda i: (i, 0)), pl.BlockSpec((1, d), lambda i: (0, 0))],
        out_specs=pl.BlockSpec((tm, d), lambda i: (i, 0)), out_shape=jax.ShapeDtypeStruct((t, d), BF16),
        compiler_params=_cp(("parallel",)),
    )(x, g)


def _rms_bwd(name, x, g, dh, dres, want_bf16, deps=()):
    t, d = x.shape
    tm = _rows(t)

    def body(x_ref, g_ref, dh_ref, dres_ref, *rest):
        rest = rest[len(deps):]
        dx_ref, rest = rest[0], rest[1:]
        dg_ref = rest[-1]
        xf = x_ref[...]
        r = lax.rsqrt(jnp.mean(xf * xf, axis=-1, keepdims=True) + EPS)
        xh = xf * r
        dhf = dh_ref[...]
        dxh = dhf * g_ref[...]
        m = jnp.mean(dxh * xh, axis=-1, keepdims=True)
        dx = dres_ref[...] + r * (dxh - xh * m)
        dx_ref[...] = dx
        if want_bf16:
            rest[0][...] = dx.astype(BF16)

        @pl.when(pl.program_id(0) == 0)
        def _():
            dg_ref[...] = jnp.zeros_like(dg_ref)

        dg_ref[...] += jnp.sum(dhf * xh, axis=0, keepdims=True)

    row = pl.BlockSpec((tm, d), lambda i: (i, 0))
    vec = pl.BlockSpec((1, d), lambda i: (0, 0))
    out_specs = [row] + ([row] if want_bf16 else []) + [vec]
    out_shape = ([jax.ShapeDtypeStruct((t, d), F32)] + ([jax.ShapeDtypeStruct((t, d), BF16)] if want_bf16 else [])
                 + [jax.ShapeDtypeStruct((1, d), F32)])
    return pl.pallas_call(body, name=name, grid=(t // tm,), in_specs=[row, vec, row, row] + [ANY] * len(deps),
                          out_specs=out_specs, out_shape=out_shape, compiler_params=_cp(("arbitrary",)))(x, g, dh, dres, *deps)


def _final_bwd(name, x3, gf, tgt):
    t, d = x3.shape
    tm = _rows(t)

    def body(x_ref, g_ref, t_ref, dx_ref, dxb_ref, dg_ref, lc_ref):
        xf = x_ref[...]
        g = g_ref[...]
        r = lax.rsqrt(jnp.mean(xf * xf, axis=-1, keepdims=True) + EPS)
        xh = xf * r
        diff = xh * g - t_ref[...]
        dy = diff * (1.0 / d)
        dxh = dy * g
        m = jnp.mean(dxh * xh, axis=-1, keepdims=True)
        dx = r * (dxh - xh * m)
        dx_ref[...] = dx
        dxb_ref[...] = dx.astype(BF16)

        @pl.when(pl.program_id(0) == 0)
        def _():
            dg_ref[...] = jnp.zeros_like(dg_ref)
            lc_ref[...] = jnp.zeros_like(lc_ref)

        dg_ref[...] += jnp.sum(dy * xh, axis=0, keepdims=True)
        lc_ref[...] += jnp.sum(diff * diff, axis=0, keepdims=True) * (0.5 / d)

    row = pl.BlockSpec((tm, d), lambda i: (i, 0))
    vec = pl.BlockSpec((1, d), lambda i: (0, 0))
    return pl.pallas_call(
        body, name=name, grid=(t // tm,), in_specs=[row, vec, row], out_specs=[row, row, vec, vec],
        out_shape=[jax.ShapeDtypeStruct((t, d), F32), jax.ShapeDtypeStruct((t, d), BF16),
                   jax.ShapeDtypeStruct((1, d), F32), jax.ShapeDtypeStruct((1, d), F32)],
        compiler_params=_cp(("arbitrary",)),
    )(x3, gf, tgt)


def _shift_down(v, k, t_idx):
    return jnp.where(t_idx >= k, pltpu.roll(v, k, 0), 0.0)


def _shift_up(v, k, t_idx):
    n = v.shape[0]
    return jnp.where(t_idx < n - k, pltpu.roll(v, n - k, 0), 0.0)


def _window_sums(v, shift, t_idx, grp):
    s = v + shift(v, 1, t_idx)
    out = s
    for lvl in range(1, len(POOL_WINDOWS)):
        s = s + shift(s, 1 << lvl, t_idx)
        out = jnp.where(grp >= lvl, s, out)
    return out


def _window_count(t_idx, grp):
    return jnp.minimum(t_idx + 1, jnp.left_shift(2, grp)).astype(F32)


MIX_COLS = 128


def _mixer_fwd(name, proj, cw, cb, n_conv, n_groups):
    t = proj.shape[0]
    nb = n_conv // MIX_COLS
    per_group = n_conv // n_groups // MIX_COLS

    def body(ba_ref, ca_ref, va_ref, vb_ref, cw_ref, cb_ref, z_ref, p_ref):
        t_idx = lax.broadcasted_iota(jnp.int32, (t, MIX_COLS), 0)
        q = ca_ref[...].astype(F32) * va_ref[...].astype(F32)
        w = cw_ref[...]
        u = cb_ref[...] + w[0:1] * _shift_down(q, 2, t_idx) + w[1:2] * _shift_down(q, 1, t_idx) + w[2:3] * q
        z_ref[...] = (ba_ref[...].astype(F32) * u).astype(BF16)
        grp = pl.program_id(0) // per_group
        v = vb_ref[...].astype(F32)
        p_ref[...] = (_window_sums(v, _shift_down, t_idx, grp) / _window_count(t_idx, grp) - v).astype(BF16)

    col = lambda s: pl.BlockSpec((t, MIX_COLS), lambda j: (0, s * nb + j))
    return pl.pallas_call(
        body, name=name, grid=(nb,),
        in_specs=[col(0), col(1), col(2), col(3), pl.BlockSpec((3, MIX_COLS), lambda j: (0, j)),
                  pl.BlockSpec((1, MIX_COLS), lambda j: (0, j))],
        out_specs=[col(0), col(0)],
        out_shape=[jax.ShapeDtypeStruct((t, n_conv), BF16), jax.ShapeDtypeStruct((t, n_conv), BF16)],
        compiler_params=_cp(("parallel",)),
    )(proj, proj, proj, proj, cw, cb)


def _mixer_bwd(name, dz, dp, proj, cw, cb, dproj, n_conv, n_groups, deps=()):
    t = proj.shape[0]
    nb = n_conv // MIX_COLS
    per_group = n_conv // n_groups // MIX_COLS

    def body(dz_ref, dp_ref, ba_ref, ca_ref, va_ref, cw_ref, cb_ref, _, *rest):
        o_ref, dcw_ref, dcb_ref, scr = rest[len(deps):]
        s = pl.program_id(1)

        @pl.when(s == 0)
        def _():
            t_idx = lax.broadcasted_iota(jnp.int32, (t, MIX_COLS), 0)
            ca, va = ca_ref[...].astype(F32), va_ref[...].astype(F32)
            q = ca * va
            q1, q2 = _shift_down(q, 1, t_idx), _shift_down(q, 2, t_idx)
            w = cw_ref[...]
            u = cb_ref[...] + w[0:1] * q2 + w[1:2] * q1 + w[2:3] * q
            dzf = dz_ref[...].astype(F32)
            du = dzf * ba_ref[...].astype(F32)
            scr[0] = (dzf * u).astype(BF16)
            dq = w[2:3] * du + w[1:2] * _shift_up(du, 1, t_idx) + w[0:1] * _shift_up(du, 2, t_idx)
            scr[1] = (dq * va).astype(BF16)
            scr[2] = (dq * ca).astype(BF16)
            dcb_ref[...] = jnp.sum(du, axis=0, keepdims=True)
            dcw_ref[0:1, :] = jnp.sum(du * q2, axis=0, keepdims=True)
            dcw_ref[1:2, :] = jnp.sum(du * q1, axis=0, keepdims=True)
            dcw_ref[2:3, :] = jnp.sum(du * q, axis=0, keepdims=True)
            grp = pl.program_id(0) // per_group
            dpf = dp_ref[...].astype(F32)
            e = dpf / _window_count(t_idx, grp)
            scr[3] = (_window_sums(e, _shift_up, t_idx, grp) - dpf).astype(BF16)

        o_ref[...] = scr[s]

    col = lambda c: pl.BlockSpec((t, MIX_COLS), lambda j, s: (0, c * nb + j))
    own = pl.BlockSpec((t, MIX_COLS), lambda j, s: (0, j))
    return pl.pallas_call(
        body, name=name, grid=(nb, 4),
        in_specs=[own, own, col(0), col(1), col(2), pl.BlockSpec((3, MIX_COLS), lambda j, s: (0, j)),
                  pl.BlockSpec((1, MIX_COLS), lambda j, s: (0, j)), ANY] + [ANY] * len(deps),
        out_specs=[pl.BlockSpec((t, MIX_COLS), lambda j, s: (0, s * nb + j)),
                   pl.BlockSpec((3, MIX_COLS), lambda j, s: (0, j)), pl.BlockSpec((1, MIX_COLS), lambda j, s: (0, j))],
        out_shape=[jax.ShapeDtypeStruct(dproj.shape, BF16), jax.ShapeDtypeStruct((3, n_conv), F32),
                   jax.ShapeDtypeStruct((1, n_conv), F32)],
        scratch_shapes=[pltpu.VMEM((4, t, MIX_COLS), BF16)],
        input_output_aliases={7: 0},
        compiler_params=_cp(("arbitrary", "arbitrary")),
    )(dz, dp, proj, proj, proj, cw, cb, dproj, *deps)


def _merge_fwd(name, proj, bg, ya, yb, ps):
    t, d = ya.shape
    tm = _rows(t)

    def body(gab_ref, bg_ref, ya_ref, yb_ref, ps_ref, o_ref):
        gab = gab_ref[...].astype(F32) + bg_ref[...]
        sa, sb = jax.nn.sigmoid(gab[:, :d]), jax.nn.sigmoid(gab[:, d:])
        o_ref[...] = (sa * ya_ref[...].astype(F32) + sb * (yb_ref[...].astype(F32) * ps_ref[...])).astype(BF16)

    row = pl.BlockSpec((tm, d), lambda i: (i, 0))
    return pl.pallas_call(
        body, name=name, grid=(t // tm,),
        in_specs=[pl.BlockSpec((tm, 2 * d), lambda i: (i, 1)), pl.BlockSpec((1, 2 * d), lambda i: (0, 0)), row, row,
                  pl.BlockSpec((1, d), lambda i: (0, 0))],
        out_specs=row, out_shape=jax.ShapeDtypeStruct((t, d), BF16), compiler_params=_cp(("parallel",)),
    )(proj, bg, ya, yb, ps)


def _merge_bwd(name, dm, proj, bg, ya, yb, ps):
    t, d = ya.shape
    tm = _rows(t)

    def body(dm_ref, gab_ref, bg_ref, ya_ref, yb_ref, ps_ref, dya_ref, dyb_ref, dg_ref, dba_ref, dbb_ref, dps_ref):
        gab = gab_ref[...].astype(F32) + bg_ref[...]
        sa, sb = jax.nn.sigmoid(gab[:, :d]), jax.nn.sigmoid(gab[:, d:])
        dmf = dm_ref[...].astype(F32)
        ybf, ps_ = yb_ref[...].astype(F32), ps_ref[...]
        dya_ref[...] = (dmf * sa).astype(BF16)
        dyb = dmf * sb
        dyb_ref[...] = (dyb * ps_).astype(BF16)
        dga = dmf * ya_ref[...].astype(F32) * sa * (1.0 - sa)
        dgb = dmf * (ybf * ps_) * sb * (1.0 - sb)
        dg_ref[:, :d] = dga.astype(BF16)
        dg_ref[:, d:] = dgb.astype(BF16)

        @pl.when(pl.program_id(0) == 0)
        def _():
            dba_ref[...] = jnp.zeros_like(dba_ref)
            dbb_ref[...] = jnp.zeros_like(dbb_ref)
            dps_ref[...] = jnp.zeros_like(dps_ref)

        dba_ref[...] += jnp.sum(dga, axis=0, keepdims=True)
        dbb_ref[...] += jnp.sum(dgb, axis=0, keepdims=True)
        dps_ref[...] += jnp.sum(dyb * ybf, axis=0, keepdims=True)

    row = pl.BlockSpec((tm, d), lambda i: (i, 0))
    vec = pl.BlockSpec((1, d), lambda i: (0, 0))
    gates = pl.BlockSpec((tm, 2 * d), lambda i: (i, 1))
    return pl.pallas_call(
        body, name=name, grid=(t // tm,),
        in_specs=[row, gates, pl.BlockSpec((1, 2 * d), lambda i: (0, 0)), row, row, vec],
        out_specs=[row, row, gates, vec, vec, vec],
        out_shape=[jax.ShapeDtypeStruct((t, d), BF16), jax.ShapeDtypeStruct((t, d), BF16),
                   jax.ShapeDtypeStruct(proj.shape, BF16), jax.ShapeDtypeStruct((1, d), F32),
                   jax.ShapeDtypeStruct((1, d), F32), jax.ShapeDtypeStruct((1, d), F32)],
        compiler_params=_cp(("arbitrary",)),
    )(dm, proj, bg, ya, yb, ps)


def _ffn_act(name, gate, up):
    t, f = gate.shape
    tm, tf = _rows(t), _tile(f, 2048)

    def body(g_ref, u_ref, o_ref):
        g = g_ref[...].astype(F32)
        o_ref[...] = (g * jax.nn.sigmoid(g) * u_ref[...].astype(F32)).astype(BF16)

    blk = pl.BlockSpec((tm, tf), lambda i, j: (i, j))
    return pl.pallas_call(body, name=name, grid=(t // tm, f // tf), in_specs=[blk, blk], out_specs=blk,
                          out_shape=jax.ShapeDtypeStruct((t, f), BF16), compiler_params=_cp(("parallel", "parallel")))(gate, up)


def _ffn_bwd(name, dact, gate, up):
    t, f = gate.shape
    tm, tf = _rows(t), _tile(f, 2048)

    def body(da_ref, g_ref, u_ref, dg_ref, du_ref):
        g, da = g_ref[...].astype(F32), da_ref[...].astype(F32)
        s = jax.nn.sigmoid(g)
        du_ref[...] = (da * (g * s)).astype(BF16)
        dg_ref[...] = (da * u_ref[...].astype(F32) * (s * (1.0 + g * (1.0 - s)))).astype(BF16)

    blk = pl.BlockSpec((tm, tf), lambda i, j: (i, j))
    shp = jax.ShapeDtypeStruct((t, f), BF16)
    return pl.pallas_call(body, name=name, grid=(t // tm, f // tf), in_specs=[blk, blk, blk], out_specs=[blk, blk],
                          out_shape=[shp, shp], compiler_params=_cp(("parallel", "parallel")))(dact, gate, up)


def _adamw_math(w, g, m, v):
    m = ADAM_B1 * m + (1.0 - ADAM_B1) * g
    v = ADAM_B2 * v + (1.0 - ADAM_B2) * (g * g)
    m_hat = m / (1.0 - ADAM_B1 ** ADAM_STEP)
    v_hat = v / (1.0 - ADAM_B2 ** ADAM_STEP)
    delta = -ADAM_LR * (m_hat / (jnp.sqrt(v_hat) + ADAM_EPS) + ADAM_WD * w)
    return delta, m, v


def _adamw(name, w, g, m, v):
    r, c = w.shape
    tr = _tile8(r, 512 if c <= 1024 else 256)

    def body(w_ref, g_ref, m_ref, v_ref, go_ref, d_ref, nm_ref, nv_ref):
        g = g_ref[...]
        go_ref[...] = g
        d_ref[...], nm_ref[...], nv_ref[...] = _adamw_math(w_ref[...], g, m_ref[...], v_ref[...])

    blk = pl.BlockSpec((tr, c), lambda i: (i, 0))
    shp = jax.ShapeDtypeStruct((r, c), F32)
    return pl.pallas_call(body, name=name, grid=(r // tr,), in_specs=[blk] * 4, out_specs=[blk] * 4,
                          out_shape=[shp] * 4, compiler_params=_cp(("parallel",)))(w, g, m, v)


class _Weight:
    def __init__(self, name, rows, cols, colshard):
        self.name, self.colshard = name, colshard
        self.R, self.nn = rows // 2, cols
        self.P = 1 if colshard else N_CHIPS
        self.N = N_CHIPS * cols if colshard else cols

    def cols(self, k):
        return pl.ds(pl.multiple_of(k * self.nn, LANES), self.nn)

    def shard(self, ref, k):
        return ref.at[0, :, :, self.cols(k)] if self.colshard else ref.at[k]

    def half(self, ref, k, h):
        return ref.at[0, h, :, self.cols(k)] if self.colshard else ref.at[k, h]

    def part(self, ref, k):
        return ref.at[0, :, self.cols(k)] if self.colshard else ref.at[k]


def _remote(src, dst, ssem, rsem, dev):
    return pltpu.make_async_remote_copy(src_ref=src, dst_ref=dst, send_sem=ssem, recv_sem=rsem, device_id=dev,
                                        device_id_type=MESH)


def _other_chips(x, y):
    chips = [(1 - x, y), (x, 1 - y), (1 - x, 1 - y)]
    return chips, [2 * cx + cy for cx, cy in chips]


def _hbm(a):
    return pltpu.with_memory_space_constraint(a, pltpu.HBM)


def _gather_start(name, groups, lands):
    flat = [w for grp in groups for w in grp]
    nw, ng = len(flat), len(groups)

    def body(*refs):
        land = refs[:nw]
        sems = refs[nw:nw + 2 * ng]
        token = refs[2 * nw + 2 * ng]
        x, y, c = _mesh_pos()
        k_me = 2 * x + y
        chips, _ = _other_chips(x, y)
        i = 0
        for g, grp in enumerate(groups):
            for wi, w in enumerate(grp):
                mine = w.half(land[i], k_me, c)
                for j, chip in enumerate(chips):
                    _remote(mine, mine, sems[2 * g].at[3 * wi + j], sems[2 * g + 1].at[3 * wi + j], (*chip, c)).start()
                i += 1
        token[...] = jnp.zeros_like(token)

    sem_shapes = []
    for grp in groups:
        sem_shapes += [pltpu.SemaphoreType.DMA((3 * len(grp),))] * 2
    out = pl.pallas_call(
        body, name=name, in_specs=[HBM] * nw,
        out_specs=[SEM] * (2 * ng) + [HBM] * nw + [VMEM],
        out_shape=sem_shapes + [pltpu.HBM(a.shape, a.dtype) for a in lands] + [jax.ShapeDtypeStruct((8, LANES), F32)],
        input_output_aliases={i: 2 * ng + i for i in range(nw)},
        compiler_params=pltpu.CompilerParams(has_side_effects=EFFECT),
    )(*[_hbm(a) for a in lands])
    sems = [(out[2 * g], out[2 * g + 1]) for g in range(ng)]
    return sems, list(out[2 * ng:2 * ng + nw]), out[-1]


def _gather_wait(name, grp, lands, ssem, rsem, after):
    n = len(grp)

    def body(*refs):
        land, ssem_ref, rsem_ref = refs[:n], refs[n], refs[n + 1]
        x, y, c = _mesh_pos()
        k_me = 2 * x + y
        chips, ks = _other_chips(x, y)
        for wi, w in enumerate(grp):
            for j, chip in enumerate(chips):
                cp = _remote(w.half(land[wi], k_me, c), w.half(land[wi], ks[j], c), ssem_ref.at[3 * wi + j],
                             rsem_ref.at[3 * wi + j], (*chip, c))
                cp.wait_send()
                cp.wait_recv()

    return pl.pallas_call(
        body, name=name, in_specs=[HBM] * n + [SEM, SEM, ANY], out_specs=[HBM] * n,
        out_shape=[pltpu.HBM(a.shape, a.dtype) for a in lands], input_output_aliases={i: i for i in range(n)},
        compiler_params=pltpu.CompilerParams(has_side_effects=EFFECT),
    )(*lands, ssem, rsem, after)


def _gather_pass(name, grp, lands):
    n = len(grp)

    def body(*refs):
        out = refs[n:2 * n]
        ssem, rsem = refs[2 * n:]
        x, y, c = _mesh_pos()
        _, ks = _other_chips(x, y)
        sib = (x, y, 1 - c)
        cps = []
        for wi, w in enumerate(grp):
            for j in range(3):
                got = w.half(out[wi], ks[j], c)
                cps.append(_remote(got, got, ssem.at[3 * wi + j], rsem.at[3 * wi + j], sib))
        for cp in cps:
            cp.start()
        for wi, w in enumerate(grp):
            for j in range(3):
                theirs = w.half(out[wi], ks[j], 1 - c)
                _remote(theirs, theirs, ssem.at[3 * wi + j], rsem.at[3 * wi + j], sib).wait_recv()
        for cp in cps:
            cp.wait_send()

    return pl.pallas_call(
        body, name=name, in_specs=[ANY] * n, out_specs=[ANY] * n,
        out_shape=[jax.ShapeDtypeStruct(a.shape, a.dtype) for a in lands],
        scratch_shapes=[pltpu.SemaphoreType.DMA((3 * n,)), pltpu.SemaphoreType.DMA((3 * n,))],
        input_output_aliases={i: i for i in range(n)},
    )(*lands)


def _gather_conv_w(cw):
    ncw = cw.shape[1]

    def body(cw_ref, out_ref, ssem, rsem):
        x, y, c = _mesh_pos()
        k_me = 2 * x + y
        chips, ks = _other_chips(x, y)
        cols = lambda k: out_ref.at[:, pl.ds(pl.multiple_of(k * ncw, LANES), ncw)]
        cps = [_remote(cw_ref, cols(k_me), ssem.at[j], rsem.at[j], (*chip, c)) for j, chip in enumerate(chips)]
        for cp in cps:
            cp.start()
        for k in range(N_CHIPS):
            @pl.when(k_me == k)
            def _():
                out_ref[:, k * ncw:(k + 1) * ncw] = cw_ref[...]
        for j in range(3):
            _remote(cw_ref, cols(ks[j]), ssem.at[j], rsem.at[j], (*chips[j], c)).wait_recv()
        for cp in cps:
            cp.wait_send()

    return pl.pallas_call(
        body, name="gather_conv_w", in_specs=[VMEM], out_specs=VMEM,
        out_shape=jax.ShapeDtypeStruct((3, N_CHIPS * ncw), F32),
        scratch_shapes=[pltpu.SemaphoreType.DMA((3,)), pltpu.SemaphoreType.DMA((3,))],
    )(cw)


def _pair_exchange(name, ws, grads):
    nw = len(ws)

    def body(*refs):
        g, out = refs[:nw], refs[nw:2 * nw]
        ssem, rsem = refs[2 * nw:]
        x, y, c = _mesh_pos()
        sib = (x, y, 1 - c)
        cps = [_remote(g[i].at[:, 1 - c], out[i], ssem.at[i], rsem.at[i], sib) for i in range(nw)]
        for cp in cps:
            cp.start()
        for cp in cps:
            cp.wait()

    return pl.pallas_call(
        body, name=name, in_specs=[ANY] * nw, out_specs=[ANY] * nw,
        out_shape=[jax.ShapeDtypeStruct((w.P, w.R, w.N), BF16) for w in ws],
        scratch_shapes=[pltpu.SemaphoreType.DMA((nw,)), pltpu.SemaphoreType.DMA((nw,))],
    )(*grads)


def _grad_tiles(w, n):
    return _tile8(w.R, 512) if w.R <= 512 else w.R // 2, _tile(n, 2048)


def _pair_sum(name, w, pos, grad, got):
    tr, tn = _grad_tiles(w, w.N)

    def body(pos_ref, g_ref, r_ref, o_ref):
        o_ref[...] = (g_ref[...].astype(F32) + r_ref[...].astype(F32)).astype(BF16)

    blk = pl.BlockSpec((None, tr, tn), lambda p, i, j, pos: (p, i, j))
    grid_spec = pltpu.PrefetchScalarGridSpec(
        num_scalar_prefetch=1, grid=(w.P, w.R // tr, w.N // tn),
        in_specs=[pl.BlockSpec((None, None, tr, tn), lambda p, i, j, pos: (p, pos[0], i, j)), blk], out_specs=blk)
    return pl.pallas_call(body, name=name, grid_spec=grid_spec, out_shape=jax.ShapeDtypeStruct((w.P, w.R, w.N), BF16),
                          compiler_params=_cp(("parallel",) * 3))(pos, grad, got)


def _scatter_start(name, ws, pairs):
    nw = len(ws)

    def body(*refs):
        pr, land = refs[:nw], refs[nw:2 * nw]
        ssem, rsem = refs[2 * nw], refs[2 * nw + 1]
        token = refs[4 * nw + 2]
        x, y, c = _mesh_pos()
        chips, ks = _other_chips(x, y)
        for i, w in enumerate(ws):
            for j, chip in enumerate(chips):
                _remote(w.part(pr[i], ks[j]), land[i].at[j], ssem.at[3 * i + j], rsem.at[3 * i + j], (*chip, c)).start()
        token[...] = jnp.zeros_like(token)

    lands = [lax.empty((3, w.R, w.nn), BF16) for w in ws]
    out = pl.pallas_call(
        body, name=name, in_specs=[HBM] * (2 * nw),
        out_specs=[SEM, SEM] + [HBM] * (2 * nw) + [VMEM],
        out_shape=[pltpu.SemaphoreType.DMA((3 * nw,))] * 2 + [pltpu.HBM(a.shape, a.dtype) for a in pairs + lands]
        + [jax.ShapeDtypeStruct((8, LANES), F32)],
        input_output_aliases={i: 2 + i for i in range(2 * nw)},
        compiler_params=pltpu.CompilerParams(has_side_effects=EFFECT),
    )(*[_hbm(a) for a in pairs + lands])
    return out[0], out[1], list(out[2:2 + nw]), list(out[2 + nw:2 + 2 * nw]), out[-1]


def _scatter_wait(name, ws, pairs, lands, ssem, rsem, after):
    nw = len(ws)

    def body(*refs):
        pr, land = refs[:nw], refs[nw:2 * nw]
        ssem_ref, rsem_ref = refs[2 * nw], refs[2 * nw + 1]
        x, y, c = _mesh_pos()
        chips, ks = _other_chips(x, y)
        for i, w in enumerate(ws):
            for j, chip in enumerate(chips):
                cp = _remote(w.part(pr[i], ks[j]), land[i].at[j], ssem_ref.at[3 * i + j], rsem_ref.at[3 * i + j], (*chip, c))
                cp.wait_send()
                cp.wait_recv()

    out = pl.pallas_call(
        body, name=name, in_specs=[HBM] * (2 * nw) + [SEM, SEM, ANY], out_specs=[HBM] * (2 * nw),
        out_shape=[pltpu.HBM(a.shape, a.dtype) for a in pairs + lands],
        input_output_aliases={i: i for i in range(2 * nw)},
        compiler_params=pltpu.CompilerParams(has_side_effects=EFFECT),
    )(*pairs, *lands, ssem, rsem, after)
    return list(out[nw:])


def _final_sum(name, w, pos, grad, got, parts):
    tr, tn = _grad_tiles(w, w.nn)
    nbc = w.nn // tn

    def body(pos_ref, g_ref, r_ref, p_ref, o_ref):
        acc = g_ref[...].astype(F32) + r_ref[...].astype(F32)
        for j in range(3):
            acc = acc + p_ref[j].astype(F32)
        o_ref[...] = acc

    if w.colshard:
        g_spec = pl.BlockSpec((None, None, tr, tn), lambda i, j, pos: (0, pos[0], i, pos[1] * nbc + j))
        r_spec = pl.BlockSpec((None, tr, tn), lambda i, j, pos: (0, i, pos[1] * nbc + j))
    else:
        g_spec = pl.BlockSpec((None, None, tr, tn), lambda i, j, pos: (pos[1], pos[0], i, j))
        r_spec = pl.BlockSpec((None, tr, tn), lambda i, j, pos: (pos[1], i, j))
    grid_spec = pltpu.PrefetchScalarGridSpec(
        num_scalar_prefetch=1, grid=(w.R // tr, nbc),
        in_specs=[g_spec, r_spec, pl.BlockSpec((3, tr, tn), lambda i, j, pos: (0, i, j))],
        out_specs=pl.BlockSpec((None, tr, tn), lambda i, j, pos: (pos[0], i, j)))
    return pl.pallas_call(body, name=name, grid_spec=grid_spec, out_shape=jax.ShapeDtypeStruct((2, w.R, w.nn), F32),
                          compiler_params=_cp(("parallel",) * 2))(pos, grad, got, parts)


def _share_halves(name, ws, halves):
    nw = len(ws)

    def body(*refs):
        out = refs[nw:2 * nw]
        ssem, rsem = refs[2 * nw:]
        x, y, c = _mesh_pos()
        sib = (x, y, 1 - c)
        cps = [_remote(out[i].at[c], out[i].at[c], ssem.at[i], rsem.at[i], sib) for i in range(nw)]
        for cp in cps:
            cp.start()
        for i, cp in enumerate(cps):
            cp.wait_send()
            _remote(out[i].at[1 - c], out[i].at[1 - c], ssem.at[i], rsem.at[i], sib).wait_recv()

    return pl.pallas_call(
        body, name=name, in_specs=[ANY] * nw, out_specs=[ANY] * nw,
        out_shape=[jax.ShapeDtypeStruct(h.shape, F32) for h in halves],
        scratch_shapes=[pltpu.SemaphoreType.DMA((nw,)), pltpu.SemaphoreType.DMA((nw,))],
        input_output_aliases={i: i for i in range(nw)},
    )(*halves)


VEC_ROWS = 16


def _vector_step(d, n_conv, parts, params):
    ncw = params[2][0].shape[1]
    n_par = len(params)

    def body(*refs):
        dg1, dba, dbb, dcw, dcb, dps, dg2, dgf, lc = refs[:9]
        wmv = refs[9:9 + 3 * n_par]
        outs = refs[9 + 3 * n_par:9 + 7 * n_par]
        loss_ref = refs[9 + 7 * n_par]
        snd, got, ssem, rsem = refs[9 + 7 * n_par + 1:]
        x, y, c = _mesh_pos()
        me = 4 * x + 2 * y + c
        snd[...] = jnp.zeros_like(snd)
        for row, ref in ((0, dg1), (1, dba), (2, dbb), (3, dps), (4, dg2), (5, dgf), (6, lc)):
            snd[row:row + 1, :] = ref[...]
        snd[7:8, :n_conv] = dcb[...]
        snd[8:11, :n_conv] = dcw[...]
        cps = []
        for r in range(1, N_DEV):
            peer = tuple(1 - p if (r >> b) & 1 else p for p, b in ((x, 2), (y, 1), (c, 0)))
            cps.append(_remote(snd, got.at[me], ssem.at[r - 1], rsem.at[r - 1], peer))
        for cp in cps:
            cp.start()
        got[me] = snd[...]
        for r in range(1, N_DEV):
            peer = tuple(1 - p if (r >> b) & 1 else p for p, b in ((x, 2), (y, 1), (c, 0)))
            _remote(snd, got.at[4 * peer[0] + 2 * peer[1] + peer[2]], ssem.at[r - 1], rsem.at[r - 1], peer).wait_recv()
        for cp in cps:
            cp.wait_send()
        tot = got[0]
        for dev in range(1, N_DEV):
            tot = tot + got[dev]
        loss_ref[...] = jnp.sum(tot[6:7, :], axis=1, keepdims=True)
        k_me = 2 * x + y
        g_cw = jnp.zeros((3, ncw), F32)
        for k in range(N_CHIPS):
            g_cw = g_cw + jnp.where(k_me == k, tot[8:11, k * ncw:(k + 1) * ncw], 0.0)
        grads = [tot[0:1, :], jnp.concatenate([tot[1:2, :], tot[2:3, :]], axis=1), g_cw, tot[7:8, :n_conv],
                 tot[3:4, :], tot[4:5, :], tot[5:6, :]]
        for i, g in enumerate(grads):
            w_ref, m_ref, v_ref = wmv[3 * i:3 * i + 3]
            delta, nm, nv = _adamw_math(w_ref[...], g, m_ref[...], v_ref[...])
            outs[4 * i][...] = g
            outs[4 * i + 1][...] = delta
            outs[4 * i + 2][...] = nm
            outs[4 * i + 3][...] = nv

    args = list(parts)
    out_shape = []
    for w, m, v in params:
        args += [w, m, v]
        out_shape += [jax.ShapeDtypeStruct(w.shape, F32)] * 4
    out_shape.append(jax.ShapeDtypeStruct((1, 1), F32))
    return pl.pallas_call(
        body, name="vector_params_step", in_specs=[VMEM] * len(args), out_specs=[VMEM] * len(out_shape),
        out_shape=out_shape,
        scratch_shapes=[pltpu.VMEM((VEC_ROWS, d), F32), pltpu.VMEM((N_DEV, VEC_ROWS, d), F32),
                        pltpu.SemaphoreType.DMA((N_DEV - 1,)), pltpu.SemaphoreType.DMA((N_DEV - 1,))],
        compiler_params=pltpu.CompilerParams(vmem_limit_bytes=VMEM_LIMIT),
    )(*args)


def kernel(x, norm1_g, w_in, b_gate, conv_w, conv_b, w_a_out, w_pool, pool_scale, w_o, norm2_g, w_ffn_gate, w_ffn_up, w_ffn_down, final_g, loss_target, m_norm1_g, m_w_in, m_b_gate, m_conv_w, m_conv_b, m_w_a_out, m_w_pool, m_pool_scale, m_w_o, m_norm2_g, m_w_ffn_gate, m_w_ffn_up, m_w_ffn_down, m_final_g, v_norm1_g, v_w_in, v_b_gate, v_conv_w, v_conv_b, v_w_a_out, v_w_pool, v_pool_scale, v_w_o, v_norm2_g, v_w_ffn_gate, v_w_ffn_up, v_w_ffn_down, v_final_g):
    t, d = x.shape[1], x.shape[2]
    n_conv = conv_b.shape[1]
    n_groups, pool_cg, pool_dg = w_pool.shape[1], w_pool.shape[2], N_CHIPS * w_pool.shape[3]
    d_ff = N_CHIPS * w_ffn_gate.shape[2]
    assert n_conv // n_groups == pool_cg and n_conv % (n_groups * MIX_COLS) == 0 and n_groups == len(POOL_WINDOWS)

    big = {"w_in": (w_in, m_w_in, v_w_in), "w_a_out": (w_a_out, m_w_a_out, v_w_a_out), "w_pool": (w_pool, m_w_pool, v_w_pool),
           "w_o": (w_o, m_w_o, v_w_o), "w_ffn_gate": (w_ffn_gate, m_w_ffn_gate, v_w_ffn_gate),
           "w_ffn_up": (w_ffn_up, m_w_ffn_up, v_w_ffn_up), "w_ffn_down": (w_ffn_down, m_w_ffn_down, v_w_ffn_down)}
    colshard = {"w_in": True, "w_a_out": True, "w_pool": True, "w_o": False, "w_ffn_gate": True, "w_ffn_up": True,
                "w_ffn_down": False}
    names = list(big)
    shard2d = {n: big[n][0].reshape(-1, big[n][0].shape[-1]) for n in names}
    ws = [_Weight(n, *shard2d[n].shape, colshard[n]) for n in names]

    xs, tgt = x[0], loss_target[0]
    cw_loc = conv_w[0]
    pos = jnp.stack([lax.axis_index("c"), 2 * lax.axis_index("x") + lax.axis_index("y")]).astype(jnp.int32)
    by_name = {w.name: w for w in ws}
    groups = [[by_name[n] for n in g] for g in (["w_in"], ["w_a_out", "w_pool", "w_o"], ["w_ffn_gate", "w_ffn_up"],
                                                 ["w_ffn_down"])]
    first = [sum(len(g) for g in groups[:i]) for i in range(len(groups))]

    cw_full = _gather_conv_w(cw_loc)
    cast = lambda w: _cast_place(f"cast_{w.name}", w, pos, shard2d[w.name].reshape(2, w.R, w.nn), deps=[cw_full])
    sems_a, lands_a, _ = _gather_start("gather_start_a", groups[:1], [cast(w) for w in groups[0]])
    sems_b, lands_b, _ = _gather_start("gather_start_b", groups[1:], [cast(w) for grp in groups[1:] for w in grp])
    gsems, lands = sems_a + sems_b, lands_a + lands_b
    full = {}

    def arrive(g, after):
        grp = groups[g]
        got = _gather_wait(f"gather_wait_{g}", grp, lands[first[g]:first[g] + len(grp)], *gsems[g], after)
        got = _gather_pass(f"gather_pass_{g}", grp, got)
        full.update({w.name: a.reshape(w.P * 2 * w.R, w.N) for w, a in zip(grp, got)})

    h1 = _rms_fwd("norm1_fwd", xs, norm1_g)
    arrive(0, h1)
    proj = _mm_nn("proj_in", h1, full["w_in"], BF16)
    z, p = _mixer_fwd("mixer_fwd", proj, cw_full, conv_b, n_conv, n_groups)
    arrive(1, z)
    wp_full = full["w_pool"].reshape(n_groups, pool_cg, pool_dg)
    ya = _mm_nn("conv_out", z, full["w_a_out"], BF16)
    yb = _gmm_nn("pool_out", p, wp_full, BF16)
    merged = _merge_fwd("merge_fwd", proj, b_gate, ya, yb, pool_scale)
    x2 = _mm_nn("mix_out", merged, full["w_o"], F32, add=xs)
    h2 = _rms_fwd("norm2_fwd", x2, norm2_g)
    arrive(2, h2)
    gate = _mm_nn("ffn_gate", h2, full["w_ffn_gate"], BF16)
    up = _mm_nn("ffn_up", h2, full["w_ffn_up"], BF16)
    act = _ffn_act("ffn_act", gate, up)
    arrive(3, act)
    x3 = _mm_nn("ffn_down", act, full["w_ffn_down"], F32, add=x2, tk=d_ff // 4)

    pending = {}

    def reduce_start(g, grads):
        grp = groups[g]
        gcan = [grads[w.name].reshape(w.P, 2, w.R, w.N) for w in grp]
        sib = _pair_exchange(f"pair_exchange_{g}", grp, gcan)
        pairs = [_pair_sum(f"pair_sum_{w.name}", w, pos, a, s) for w, a, s in zip(grp, gcan, sib)]
        ssem, rsem, pairs, slots, token = _scatter_start(f"scatter_start_{g}", grp, pairs)
        pending[g] = (gcan, sib, pairs, slots, ssem, rsem)
        return token

    def reduce_finish(g, after):
        grp = groups[g]
        gcan, sib, pairs, slots, ssem, rsem = pending[g]
        parts = _scatter_wait(f"scatter_wait_{g}", grp, pairs, slots, ssem, rsem, after)
        return [_final_sum(f"final_sum_{w.name}", w, pos, a, s, q) for w, a, s, q in zip(grp, gcan, sib, parts)]

    grads = {}
    dx3, dx3b, d_gf, loss_cols = _final_bwd("final_bwd", x3, final_g.reshape(1, d), tgt)
    dact = _mm_nt("d_act", [(dx3b, full["w_ffn_down"])], BF16, tk=d)
    dgate, dup = _ffn_bwd("ffn_bwd", dact, gate, up)
    grads["w_ffn_down"] = _mm_tn("dw_ffn_down", act, dx3b, BF16)
    tok = reduce_start(3, grads)
    dh2 = _mm_nt("d_h2", [(dgate, full["w_ffn_gate"]), (dup, full["w_ffn_up"])], F32, tk=d_ff // 4, deps=[tok])
    grads["w_ffn_gate"] = _mm_tn("dw_ffn_gate", h2, dgate, BF16)
    grads["w_ffn_up"] = _mm_tn("dw_ffn_up", h2, dup, BF16)
    tok = reduce_start(2, grads)
    dx2, dx2b, d_g2 = _rms_bwd("norm2_bwd", x2, norm2_g, dh2, dx3, True, deps=[tok])
    dmerged = _mm_nt("d_merged", [(dx2b, full["w_o"])], BF16, tk=d)
    grads["w_o"] = _mm_tn("dw_o", merged, dx2b, BF16)
    dya, dyb, dproj, d_bga, d_bgb, d_ps = _merge_bwd("merge_bwd", dmerged, proj, b_gate, ya, yb, pool_scale)
    dz = _mm_nt("d_z", [(dya, full["w_a_out"])], BF16, tk=d)
    grads["w_a_out"] = _mm_tn("dw_a_out", z, dya, BF16)
    dp = _gmm_nt("d_pool", dyb, wp_full, BF16)
    grads["w_pool"] = _gmm_tn("dw_pool", p, dyb, n_groups, BF16)
    tok = reduce_start(1, grads)
    dproj, d_cw, d_cb = _mixer_bwd("mixer_bwd", dz, dp, proj, cw_full, conv_b, dproj, n_conv, n_groups, deps=[tok])
    dh1 = _mm_nt("d_h1", [(dproj, full["w_in"])], F32, tk=proj.shape[1] // 4)
    grad_x, d_g1 = _rms_bwd("norm1_bwd", xs, norm1_g, dh1, dx2, False)

    vec_names = ["norm1_g", "b_gate", "conv_w", "conv_b", "pool_scale", "norm2_g", "final_g"]
    vec = {"norm1_g": (norm1_g, m_norm1_g, v_norm1_g), "b_gate": (b_gate, m_b_gate, v_b_gate),
           "conv_w": (cw_loc, m_conv_w[0], v_conv_w[0]), "conv_b": (conv_b, m_conv_b, v_conv_b),
           "pool_scale": (pool_scale, m_pool_scale, v_pool_scale), "norm2_g": (norm2_g, m_norm2_g, v_norm2_g),
           "final_g": tuple(a.reshape(1, d) for a in (final_g, m_final_g, v_final_g))}
    vout = _vector_step(d, n_conv, [d_g1, d_bga, d_bgb, d_cw, d_cb, d_ps, d_g2, d_gf, loss_cols],
                        [vec[n] for n in vec_names])

    grads["w_in"] = _mm_tn("dw_in", h1, dproj, BF16, deps=[vout[-1]])
    tok = reduce_start(0, grads)

    g_big, d_big, m_big, v_big = {}, {}, {}, {}

    def update(name, wsub, halves):
        out = None
        for w, g in zip(wsub, _share_halves(name, wsub, halves)):
            wt, mt, vt = big[w.name]
            g2 = g.reshape(2 * w.R, w.nn)
            go, dl, nm, nv = _adamw(f"adamw_{w.name}", shard2d[w.name], g2, mt.reshape(g2.shape), vt.reshape(g2.shape))
            g_big[w.name], d_big[w.name], m_big[w.name], v_big[w.name] = (a.reshape(wt.shape) for a in (go, dl, nm, nv))
            out = nv
        return out

    after, early, early_halves = tok, [], []
    for g in (3, 2, 1):
        halves = reduce_finish(g, after)
        early += groups[g]
        early_halves += halves
        after = halves[-1]
    after = update("share_halves_early", early, early_halves)
    update("share_halves_w_in", groups[0], reduce_finish(0, after))

    shapes = {"conv_w": conv_w.shape, "final_g": final_g.shape}
    g_vec, d_vec, m_vec, v_vec = ({n: vout[4 * i + q].reshape(shapes.get(n, vec[n][0].shape)) for i, n in enumerate(vec_names)}
                                  for q in range(4))
    loss = vout[-1].reshape(())

    order = ["norm1_g", "w_in", "b_gate", "conv_w", "conv_b", "w_a_out", "w_pool", "pool_scale", "w_o", "norm2_g",
             "w_ffn_gate", "w_ffn_up", "w_ffn_down", "final_g"]
    pick = lambda vecs, bigs: [vecs[n] if n in vecs else bigs[n] for n in order]
    return (loss, grad_x.reshape(x.shape), *pick(g_vec, g_big), *pick(d_vec, d_big), *pick(m_vec, m_big),
            *pick(v_vec, v_big))
```

```python
import functools

import jax
import jax.numpy as jnp
from jax import lax
from jax.experimental import pallas as pl
from jax.experimental.pallas import tpu as pltpu

F32, BF16 = jnp.float32, jnp.bfloat16
MESH = pl.DeviceIdType.MESH
ANY = pl.BlockSpec(memory_space=pl.ANY)
VMEM = pl.BlockSpec(memory_space=pltpu.VMEM)
HBM = pl.BlockSpec(memory_space=pltpu.HBM)
SEM = pl.BlockSpec(memory_space=pltpu.SEMAPHORE)
EFFECT = pltpu.SideEffectType.DATAFLOW_SIDE_EFFECTING

EPS = 1e-6
POOL_WINDOWS = (2, 4, 8, 16)
ADAM_LR, ADAM_B1, ADAM_B2, ADAM_EPS, ADAM_WD, ADAM_STEP = 0.001, 0.9, 0.999, 1e-08, 0.01, 10

V7X_VMEM_BYTES = 64 * 1024 * 1024
VMEM_LIMIT = V7X_VMEM_BYTES * 3 // 4
LANES = 128
N_CHIPS = 4
N_DEV = 8

_DIMS = {
    "nn": (((1,), (0,)), ((), ())),
    "nt": (((1,), (1,)), ((), ())),
    "tn": (((0,), (0,)), ((), ())),
}


def _cp(sem):
    return pltpu.CompilerParams(dimension_semantics=sem, vmem_limit_bytes=VMEM_LIMIT)


def _mesh_pos():
    return lax.axis_index("x"), lax.axis_index("y"), lax.axis_index("c")


def _mm(name, pairs, *, mode, grid, out_shape, o_spec, nk=1, kaxis=None, add=None, deps=()):
    npair = len(pairs)
    has_add = add is not None

    def body(*refs):
        ab = refs[: 2 * npair]
        pos = 2 * npair
        add_ref = refs[pos] if has_add else None
        pos += int(has_add) + len(deps)
        o_ref = refs[pos]
        acc_ref = refs[pos + 1] if nk > 1 else None
        d = None
        for p in range(npair):
            t = lax.dot_general(ab[2 * p][...], ab[2 * p + 1][...], _DIMS[mode], preferred_element_type=F32)
            d = t if d is None else d + t
        if nk == 1:
            if has_add:
                d = d + add_ref[...].astype(F32)
            o_ref[...] = d.astype(o_ref.dtype)
        else:
            k = pl.program_id(kaxis)

            @pl.when(k == 0)
            def _():
                acc_ref[...] = d

            @pl.when(k > 0)
            def _():
                acc_ref[...] += d

            @pl.when(k == nk - 1)
            def _():
                r = acc_ref[...]
                if has_add:
                    r = r + add_ref[...].astype(F32)
                o_ref[...] = r.astype(o_ref.dtype)

    args, specs = [], []
    for a, a_spec, b, b_spec in pairs:
        args += [a, b]
        specs += [a_spec, b_spec]
    if has_add:
        args.append(add[0])
        specs.append(add[1])
    args += list(deps)
    specs += [ANY] * len(deps)
    scratch = []
    if nk > 1:
        blk = [d for d in o_spec.block_shape if d is not None]
        scratch = [pltpu.VMEM(tuple(blk), F32)]
    sem = tuple("arbitrary" if (nk > 1 and ax == kaxis) else "parallel" for ax in range(len(grid)))
    return pl.pallas_call(
        body, name=name, grid=grid, in_specs=specs, out_specs=o_spec, out_shape=out_shape,
        scratch_shapes=scratch, compiler_params=_cp(sem),
    )(*args)


def _tile(n, pref):
    if n <= pref:
        return n
    for t in range(pref, 0, -LANES):
        if t % LANES == 0 and n % t == 0:
            return t
    raise ValueError(f"no tile for {n}")


def _mm_nn(name, a, b, out_dtype, add=None, tk=None, deps=()):
    m, kk = a.shape
    n = b.shape[1]
    tm, tn = _tile(m, 1024), _tile(n, 512)
    out_shape = jax.ShapeDtypeStruct((m, n), out_dtype)
    if tk is None or tk == kk:
        grid = (m // tm, n // tn)
        pairs = [(a, pl.BlockSpec((tm, kk), lambda i, j: (i, 0)), b, pl.BlockSpec((kk, tn), lambda i, j: (0, j)))]
        o_spec = pl.BlockSpec((tm, tn), lambda i, j: (i, j))
        add_ = None if add is None else (add, pl.BlockSpec((tm, tn), lambda i, j: (i, j)))
        return _mm(name, pairs, mode="nn", grid=grid, out_shape=out_shape, o_spec=o_spec, add=add_, deps=deps)
    tn = _tile(n, 1024)
    nk = kk // tk
    grid = (m // tm, n // tn, nk)
    pairs = [(a, pl.BlockSpec((tm, tk), lambda i, j, k: (i, k)), b, pl.BlockSpec((tk, tn), lambda i, j, k: (k, j)))]
    o_spec = pl.BlockSpec((tm, tn), lambda i, j, k: (i, j))
    add_ = None if add is None else (add, pl.BlockSpec((tm, tn), lambda i, j, k: (i, j)))
    return _mm(name, pairs, mode="nn", grid=grid, out_shape=out_shape, o_spec=o_spec, nk=nk, kaxis=2, add=add_, deps=deps)


def _mm_nt(name, abs_, out_dtype, tk, deps=()):
    m, kk = abs_[0][0].shape
    n = abs_[0][1].shape[0]
    tm = _tile(m, 1024)
    nk = kk // tk
    tn = _tile(n, 512 if nk == 1 else 1024)
    out_shape = jax.ShapeDtypeStruct((m, n), out_dtype)
    if nk == 1:
        grid = (m // tm, n // tn)
        pairs = [(a, pl.BlockSpec((tm, kk), lambda i, j: (i, 0)), b, pl.BlockSpec((tn, kk), lambda i, j: (j, 0)))
                 for a, b in abs_]
        o_spec = pl.BlockSpec((tm, tn), lambda i, j: (i, j))
        return _mm(name, pairs, mode="nt", grid=grid, out_shape=out_shape, o_spec=o_spec, deps=deps)
    grid = (m // tm, n // tn, nk)
    pairs = [(a, pl.BlockSpec((tm, tk), lambda i, j, k: (i, k)), b, pl.BlockSpec((tn, tk), lambda i, j, k: (j, k)))
             for a, b in abs_]
    o_spec = pl.BlockSpec((tm, tn), lambda i, j, k: (i, j))
    return _mm(name, pairs, mode="nt", grid=grid, out_shape=out_shape, o_spec=o_spec, nk=nk, kaxis=2, deps=deps)


def _mm_tn(name, a, b, out_dtype, deps=()):
    t, m = a.shape
    n = b.shape[1]
    tm, tn = _tile(m, 512), _tile(n, 2048)
    if n > m:
        grid = (n // tn, m // tm)
        a_map, b_map, o_map = (lambda j, i: (0, i)), (lambda j, i: (0, j)), (lambda j, i: (i, j))
    else:
        grid = (m // tm, n // tn)
        a_map, b_map, o_map = (lambda i, j: (0, i)), (lambda i, j: (0, j)), (lambda i, j: (i, j))
    pairs = [(a, pl.BlockSpec((t, tm), a_map), b, pl.BlockSpec((t, tn), b_map))]
    o_spec = pl.BlockSpec((tm, tn), o_map)
    return _mm(name, pairs, mode="tn", grid=grid, out_shape=jax.ShapeDtypeStruct((m, n), out_dtype), o_spec=o_spec,
               deps=deps)


def _gmm_nn(name, p, w, out_dtype):
    t = p.shape[0]
    g, cg, dg = w.shape
    tm = _tile(t, 1024)
    pairs = [(p, pl.BlockSpec((tm, cg), lambda i, j: (i, j)), w, pl.BlockSpec((None, cg, dg), lambda i, j: (j, 0, 0)))]
    o_spec = pl.BlockSpec((tm, dg), lambda i, j: (i, j))
    return _mm(name, pairs, mode="nn", grid=(t // tm, g), out_shape=jax.ShapeDtypeStruct((t, g * dg), out_dtype),
               o_spec=o_spec)


def _gmm_nt(name, dy, w, out_dtype):
    t = dy.shape[0]
    g, cg, dg = w.shape
    tm = _tile(t, 1024)
    pairs = [(dy, pl.BlockSpec((tm, dg), lambda i, j: (i, j)), w, pl.BlockSpec((None, cg, dg), lambda i, j: (j, 0, 0)))]
    o_spec = pl.BlockSpec((tm, cg), lambda i, j: (i, j))
    return _mm(name, pairs, mode="nt", grid=(t // tm, g), out_shape=jax.ShapeDtypeStruct((t, g * cg), out_dtype),
               o_spec=o_spec)


def _gmm_tn(name, p, dy, g, out_dtype):
    t = p.shape[0]
    cg, dg = p.shape[1] // g, dy.shape[1] // g
    pairs = [(p, pl.BlockSpec((t, cg), lambda j: (0, j)), dy, pl.BlockSpec((t, dg), lambda j: (0, j)))]
    o_spec = pl.BlockSpec((None, cg, dg), lambda j: (j, 0, 0))
    return _mm(name, pairs, mode="tn", grid=(g,), out_shape=jax.ShapeDtypeStruct((g, cg, dg), out_dtype), o_spec=o_spec)


ROW_TILE = 256


def _rows(t):
    return _tile8(t, ROW_TILE)


def _tile8(n, pref):
    if n <= pref:
        return n
    for t in range(pref, 0, -8):
        if n % t == 0:
            return t
    raise ValueError(f"no row tile for {n}")


def _cast_place(name, w, pos, shard, deps=()):
    tr = _tile8(w.R, 512)
    if w.colshard:
        o_map = lambda h, i, pos: (0, h, i, pos[1])
    else:
        o_map = lambda h, i, pos: (pos[1], h, i, 0)

    def body(pos_ref, w_ref, *rest):
        rest[-1][...] = w_ref[...].astype(BF16)

    grid_spec = pltpu.PrefetchScalarGridSpec(
        num_scalar_prefetch=1, grid=(2, w.R // tr),
        in_specs=[pl.BlockSpec((None, tr, w.nn), lambda h, i, pos: (h, i, 0))] + [ANY] * len(deps),
        out_specs=pl.BlockSpec((None, None, tr, w.nn), o_map))
    return pl.pallas_call(body, name=name, grid_spec=grid_spec, out_shape=jax.ShapeDtypeStruct((w.P, 2, w.R, w.N), BF16),
                          compiler_params=_cp(("parallel", "parallel")))(pos, shard, *deps)


def _rms_fwd(name, x, g):
    t, d = x.shape
    tm = _rows(t)

    def body(x_ref, g_ref, h_ref):
        xf = x_ref[...]
        r = lax.rsqrt(jnp.mean(xf * xf, axis=-1, keepdims=True) + EPS)
        h_ref[...] = (xf * r * g_ref[...]).astype(BF16)

    return pl.pallas_call(
        body, name=name, grid=(t // tm,),
        in_specs=[pl.BlockSpec((tm, d), lambda i: (i, 0)), pl.BlockSpec((1, d), lambda i: (0, 0))],
        out_specs=pl.BlockSpec((tm, d), lambda i: (i, 0)), out_shape=jax.ShapeDtypeStruct((t, d), BF16),
        compiler_params=_cp(("parallel",)),
    )(x, g)


def _rms_bwd(name, x, g, dh, dres, want_bf16, deps=()):
    t, d = x.shape
    tm = _rows(t)

    def body(x_ref, g_ref, dh_ref, dres_ref, *rest):
        rest = rest[len(deps):]
        dx_ref, rest = rest[0], rest[1:]
        dg_ref = rest[-1]
        xf = x_ref[...]
        r = lax.rsqrt(jnp.mean(xf * xf, axis=-1, keepdims=True) + EPS)
        xh = xf * r
        dhf = dh_ref[...]
        dxh = dhf * g_ref[...]
        m = jnp.mean(dxh * xh, axis=-1, keepdims=True)
        dx = dres_ref[...] + r * (dxh - xh * m)
        dx_ref[...] = dx
        if want_bf16:
            rest[0][...] = dx.astype(BF16)

        @pl.when(pl.program_id(0) == 0)
        def _():
            dg_ref[...] = jnp.zeros_like(dg_ref)

        dg_ref[...] += jnp.sum(dhf * xh, axis=0, keepdims=True)

    row = pl.BlockSpec((tm, d), lambda i: (i, 0))
    vec = pl.BlockSpec((1, d), lambda i: (0, 0))
    out_specs = [row] + ([row] if want_bf16 else []) + [vec]
    out_shape = ([jax.ShapeDtypeStruct((t, d), F32)] + ([jax.ShapeDtypeStruct((t, d), BF16)] if want_bf16 else [])
                 + [jax.ShapeDtypeStruct((1, d), F32)])
    return pl.pallas_call(body, name=name, grid=(t // tm,), in_specs=[row, vec, row, row] + [ANY] * len(deps),
                          out_specs=out_specs, out_shape=out_shape, compiler_params=_cp(("arbitrary",)))(x, g, dh, dres, *deps)


def _final_bwd(name, x3, gf, tgt):
    t, d = x3.shape
    tm = _rows(t)

    def body(x_ref, g_ref, t_ref, dx_ref, dxb_ref, dg_ref, lc_ref):
        xf = x_ref[...]
        g = g_ref[...]
        r = lax.rsqrt(jnp.mean(xf * xf, axis=-1, keepdims=True) + EPS)
        xh = xf * r
        diff = xh * g - t_ref[...]
        dy = diff * (1.0 / d)
        dxh = dy * g
        m = jnp.mean(dxh * xh, axis=-1, keepdims=True)
        dx = r * (dxh - xh * m)
        dx_ref[...] = dx
        dxb_ref[...] = dx.astype(BF16)

        @pl.when(pl.program_id(0) == 0)
        def _():
            dg_ref[...] = jnp.zeros_like(dg_ref)
            lc_ref[...] = jnp.zeros_like(lc_ref)

        dg_ref[...] += jnp.sum(dy * xh, axis=0, keepdims=True)
        lc_ref[...] += jnp.sum(diff * diff, axis=0, keepdims=True) * (0.5 / d)

    row = pl.BlockSpec((tm, d), lambda i: (i, 0))
    vec = pl.BlockSpec((1, d), lambda i: (0, 0))
    return pl.pallas_call(
        body, name=name, grid=(t // tm,), in_specs=[row, vec, row], out_specs=[row, row, vec, vec],
        out_shape=[jax.ShapeDtypeStruct((t, d), F32), jax.ShapeDtypeStruct((t, d), BF16),
                   jax.ShapeDtypeStruct((1, d), F32), jax.ShapeDtypeStruct((1, d), F32)],
        compiler_params=_cp(("arbitrary",)),
    )(x3, gf, tgt)


def _shift_down(v, k, t_idx):
    return jnp.where(t_idx >= k, pltpu.roll(v, k, 0), 0.0)


def _shift_up(v, k, t_idx):
    n = v.shape[0]
    return jnp.where(t_idx < n - k, pltpu.roll(v, n - k, 0), 0.0)


def _window_sums(v, shift, t_idx, grp):
    s = v + shift(v, 1, t_idx)
    out = s
    for lvl in range(1, len(POOL_WINDOWS)):
        s = s + shift(s, 1 << lvl, t_idx)
        out = jnp.where(grp >= lvl, s, out)
    return out


def _window_count(t_idx, grp):
    return jnp.minimum(t_idx + 1, jnp.left_shift(2, grp)).astype(F32)


MIX_COLS = 128


def _mixer_fwd(name, proj, cw, cb, n_conv, n_groups, deps=()):
    t = proj.shape[0]
    nb = n_conv // MIX_COLS
    per_group = n_conv // n_groups // MIX_COLS

    def body(ba_ref, ca_ref, va_ref, vb_ref, cw_ref, cb_ref, *rest):
        z_ref, p_ref = rest[len(deps):]
        t_idx = lax.broadcasted_iota(jnp.int32, (t, MIX_COLS), 0)
        q = ca_ref[...].astype(F32) * va_ref[...].astype(F32)
        w = cw_ref[...]
        u = cb_ref[...] + w[0:1] * _shift_down(q, 2, t_idx) + w[1:2] * _shift_down(q, 1, t_idx) + w[2:3] * q
        z_ref[...] = (ba_ref[...].astype(F32) * u).astype(BF16)
        grp = pl.program_id(0) // per_group
        v = vb_ref[...].astype(F32)
        p_ref[...] = (_window_sums(v, _shift_down, t_idx, grp) / _window_count(t_idx, grp) - v).astype(BF16)

    col = lambda s: pl.BlockSpec((t, MIX_COLS), lambda j: (0, s * nb + j))
    return pl.pallas_call(
        body, name=name, grid=(nb,),
        in_specs=[col(0), col(1), col(2), col(3), pl.BlockSpec((3, MIX_COLS), lambda j: (0, j)),
                  pl.BlockSpec((1, MIX_COLS), lambda j: (0, j))] + [ANY] * len(deps),
        out_specs=[col(0), col(0)],
        out_shape=[jax.ShapeDtypeStruct((t, n_conv), BF16), jax.ShapeDtypeStruct((t, n_conv), BF16)],
        compiler_params=_cp(("parallel",)),
    )(proj, proj, proj, proj, cw, cb, *deps)


def _mixer_bwd(name, dz, dp, proj, cw, cb, dproj, n_conv, n_groups, deps=()):
    t = proj.shape[0]
    nb = n_conv // MIX_COLS
    per_group = n_conv // n_groups // MIX_COLS

    def body(dz_ref, dp_ref, ba_ref, ca_ref, va_ref, cw_ref, cb_ref, _, *rest):
        o_ref, dcw_ref, dcb_ref, scr = rest[len(deps):]
        s = pl.program_id(1)

        @pl.when(s == 0)
        def _():
            t_idx = lax.broadcasted_iota(jnp.int32, (t, MIX_COLS), 0)
            ca, va = ca_ref[...].astype(F32), va_ref[...].astype(F32)
            q = ca * va
            q1, q2 = _shift_down(q, 1, t_idx), _shift_down(q, 2, t_idx)
            w = cw_ref[...]
            u = cb_ref[...] + w[0:1] * q2 + w[1:2] * q1 + w[2:3] * q
            dzf = dz_ref[...].astype(F32)
            du = dzf * ba_ref[...].astype(F32)
            scr[0] = (dzf * u).astype(BF16)
            dq = w[2:3] * du + w[1:2] * _shift_up(du, 1, t_idx) + w[0:1] * _shift_up(du, 2, t_idx)
            scr[1] = (dq * va).astype(BF16)
            scr[2] = (dq * ca).astype(BF16)
            dcb_ref[...] = jnp.sum(du, axis=0, keepdims=True)
            dcw_ref[0:1, :] = jnp.sum(du * q2, axis=0, keepdims=True)
            dcw_ref[1:2, :] = jnp.sum(du * q1, axis=0, keepdims=True)
            dcw_ref[2:3, :] = jnp.sum(du * q, axis=0, keepdims=True)
            grp = pl.program_id(0) // per_group
            dpf = dp_ref[...].astype(F32)
            e = dpf / _window_count(t_idx, grp)
            scr[3] = (_window_sums(e, _shift_up, t_idx, grp) - dpf).astype(BF16)

        o_ref[...] = scr[s]

    col = lambda c: pl.BlockSpec((t, MIX_COLS), lambda j, s: (0, c * nb + j))
    own = pl.BlockSpec((t, MIX_COLS), lambda j, s: (0, j))
    return pl.pallas_call(
        body, name=name, grid=(nb, 4),
        in_specs=[own, own, col(0), col(1), col(2), pl.BlockSpec((3, MIX_COLS), lambda j, s: (0, j)),
                  pl.BlockSpec((1, MIX_COLS), lambda j, s: (0, j)), ANY] + [ANY] * len(deps),
        out_specs=[pl.BlockSpec((t, MIX_COLS), lambda j, s: (0, s * nb + j)),
                   pl.BlockSpec((3, MIX_COLS), lambda j, s: (0, j)), pl.BlockSpec((1, MIX_COLS), lambda j, s: (0, j))],
        out_shape=[jax.ShapeDtypeStruct(dproj.shape, BF16), jax.ShapeDtypeStruct((3, n_conv), F32),
                   jax.ShapeDtypeStruct((1, n_conv), F32)],
        scratch_shapes=[pltpu.VMEM((4, t, MIX_COLS), BF16)],
        input_output_aliases={7: 0},
        compiler_params=_cp(("arbitrary", "arbitrary")),
    )(dz, dp, proj, proj, proj, cw, cb, dproj, *deps)


def _merge_fwd(name, proj, bg, ya, yb, ps):
    t, d = ya.shape
    tm = _rows(t)

    def body(gab_ref, bg_ref, ya_ref, yb_ref, ps_ref, o_ref):
        gab = gab_ref[...].astype(F32) + bg_ref[...]
        sa, sb = jax.nn.sigmoid(gab[:, :d]), jax.nn.sigmoid(gab[:, d:])
        o_ref[...] = (sa * ya_ref[...].astype(F32) + sb * (yb_ref[...].astype(F32) * ps_ref[...])).astype(BF16)

    row = pl.BlockSpec((tm, d), lambda i: (i, 0))
    return pl.pallas_call(
        body, name=name, grid=(t // tm,),
        in_specs=[pl.BlockSpec((tm, 2 * d), lambda i: (i, 1)), pl.BlockSpec((1, 2 * d), lambda i: (0, 0)), row, row,
                  pl.BlockSpec((1, d), lambda i: (0, 0))],
        out_specs=row, out_shape=jax.ShapeDtypeStruct((t, d), BF16), compiler_params=_cp(("parallel",)),
    )(proj, bg, ya, yb, ps)


def _merge_bwd(name, dm, proj, bg, ya, yb, ps, deps=()):
    t, d = ya.shape
    tm = _rows(t)

    def body(dm_ref, gab_ref, bg_ref, ya_ref, yb_ref, ps_ref, *rest):
        dya_ref, dyb_ref, dg_ref, dba_ref, dbb_ref, dps_ref = rest[len(deps):]
        gab = gab_ref[...].astype(F32) + bg_ref[...]
        sa, sb = jax.nn.sigmoid(gab[:, :d]), jax.nn.sigmoid(gab[:, d:])
        dmf = dm_ref[...].astype(F32)
        ybf, ps_ = yb_ref[...].astype(F32), ps_ref[...]
        dya_ref[...] = (dmf * sa).astype(BF16)
        dyb = dmf * sb
        dyb_ref[...] = (dyb * ps_).astype(BF16)
        dga = dmf * ya_ref[...].astype(F32) * sa * (1.0 - sa)
        dgb = dmf * (ybf * ps_) * sb * (1.0 - sb)
        dg_ref[:, :d] = dga.astype(BF16)
        dg_ref[:, d:] = dgb.astype(BF16)

        @pl.when(pl.program_id(0) == 0)
        def _():
            dba_ref[...] = jnp.zeros_like(dba_ref)
            dbb_ref[...] = jnp.zeros_like(dbb_ref)
            dps_ref[...] = jnp.zeros_like(dps_ref)

        dba_ref[...] += jnp.sum(dga, axis=0, keepdims=True)
        dbb_ref[...] += jnp.sum(dgb, axis=0, keepdims=True)
        dps_ref[...] += jnp.sum(dyb * ybf, axis=0, keepdims=True)

    row = pl.BlockSpec((tm, d), lambda i: (i, 0))
    vec = pl.BlockSpec((1, d), lambda i: (0, 0))
    gates = pl.BlockSpec((tm, 2 * d), lambda i: (i, 1))
    return pl.pallas_call(
        body, name=name, grid=(t // tm,),
        in_specs=[row, gates, pl.BlockSpec((1, 2 * d), lambda i: (0, 0)), row, row, vec] + [ANY] * len(deps),
        out_specs=[row, row, gates, vec, vec, vec],
        out_shape=[jax.ShapeDtypeStruct((t, d), BF16), jax.ShapeDtypeStruct((t, d), BF16),
                   jax.ShapeDtypeStruct(proj.shape, BF16), jax.ShapeDtypeStruct((1, d), F32),
                   jax.ShapeDtypeStruct((1, d), F32), jax.ShapeDtypeStruct((1, d), F32)],
        compiler_params=_cp(("arbitrary",)),
    )(dm, proj, bg, ya, yb, ps, *deps)


def _ffn_act(name, gate, up, deps=()):
    t, f = gate.shape
    tm, tf = _rows(t), _tile(f, 2048)

    def body(g_ref, u_ref, *rest):
        g = g_ref[...].astype(F32)
        rest[-1][...] = (g * jax.nn.sigmoid(g) * u_ref[...].astype(F32)).astype(BF16)

    blk = pl.BlockSpec((tm, tf), lambda i, j: (i, j))
    return pl.pallas_call(body, name=name, grid=(t // tm, f // tf), in_specs=[blk, blk] + [ANY] * len(deps), out_specs=blk,
                          out_shape=jax.ShapeDtypeStruct((t, f), BF16),
                          compiler_params=_cp(("parallel", "parallel")))(gate, up, *deps)


def _ffn_bwd(name, dact, gate, up):
    t, f = gate.shape
    tm, tf = _rows(t), _tile(f, 2048)

    def body(da_ref, g_ref, u_ref, dg_ref, du_ref):
        g, da = g_ref[...].astype(F32), da_ref[...].astype(F32)
        s = jax.nn.sigmoid(g)
        du_ref[...] = (da * (g * s)).astype(BF16)
        dg_ref[...] = (da * u_ref[...].astype(F32) * (s * (1.0 + g * (1.0 - s)))).astype(BF16)

    blk = pl.BlockSpec((tm, tf), lambda i, j: (i, j))
    shp = jax.ShapeDtypeStruct((t, f), BF16)
    return pl.pallas_call(body, name=name, grid=(t // tm, f // tf), in_specs=[blk, blk, blk], out_specs=[blk, blk],
                          out_shape=[shp, shp], compiler_params=_cp(("parallel", "parallel")))(dact, gate, up)


def _adamw_math(w, g, m, v):
    m = ADAM_B1 * m + (1.0 - ADAM_B1) * g
    v = ADAM_B2 * v + (1.0 - ADAM_B2) * (g * g)
    m_hat = m / (1.0 - ADAM_B1 ** ADAM_STEP)
    v_hat = v / (1.0 - ADAM_B2 ** ADAM_STEP)
    delta = -ADAM_LR * (m_hat / (jnp.sqrt(v_hat) + ADAM_EPS) + ADAM_WD * w)
    return delta, m, v


def _adamw(name, w, g, m, v):
    r, c = w.shape
    tr = _tile8(r, 512 if c <= 1024 else 256)

    def body(w_ref, g_ref, m_ref, v_ref, go_ref, d_ref, nm_ref, nv_ref):
        g = g_ref[...]
        go_ref[...] = g
        d_ref[...], nm_ref[...], nv_ref[...] = _adamw_math(w_ref[...], g, m_ref[...], v_ref[...])

    blk = pl.BlockSpec((tr, c), lambda i: (i, 0))
    shp = jax.ShapeDtypeStruct((r, c), F32)
    return pl.pallas_call(body, name=name, grid=(r // tr,), in_specs=[blk] * 4, out_specs=[blk] * 4,
                          out_shape=[shp] * 4, compiler_params=_cp(("parallel",)))(w, g, m, v)


class _Weight:
    def __init__(self, name, rows, cols, colshard):
        self.name, self.colshard = name, colshard
        self.R, self.nn = rows // 2, cols
        self.P = 1 if colshard else N_CHIPS
        self.N = N_CHIPS * cols if colshard else cols

    def cols(self, k):
        return pl.ds(pl.multiple_of(k * self.nn, LANES), self.nn)

    def shard(self, ref, k):
        return ref.at[0, :, :, self.cols(k)] if self.colshard else ref.at[k]

    def half(self, ref, k, h):
        return ref.at[0, h, :, self.cols(k)] if self.colshard else ref.at[k, h]

    def part(self, ref, k):
        return ref.at[0, :, self.cols(k)] if self.colshard else ref.at[k]


def _remote(src, dst, ssem, rsem, dev):
    return pltpu.make_async_remote_copy(src_ref=src, dst_ref=dst, send_sem=ssem, recv_sem=rsem, device_id=dev,
                                        device_id_type=MESH)


def _other_chips(x, y):
    chips = [(1 - x, y), (x, 1 - y), (1 - x, 1 - y)]
    return chips, [2 * cx + cy for cx, cy in chips]


def _hbm(a):
    return pltpu.with_memory_space_constraint(a, pltpu.HBM)


def _gather_start(name, groups, lands):
    flat = [w for grp in groups for w in grp]
    nw, ng = len(flat), len(groups)

    def body(*refs):
        land = refs[:nw]
        sems = refs[nw:nw + 2 * ng]
        token = refs[2 * nw + 2 * ng]
        x, y, c = _mesh_pos()
        k_me = 2 * x + y
        chips, _ = _other_chips(x, y)
        i = 0
        for g, grp in enumerate(groups):
            for wi, w in enumerate(grp):
                mine = w.half(land[i], k_me, c)
                for j, chip in enumerate(chips):
                    _remote(mine, mine, sems[2 * g].at[3 * wi + j], sems[2 * g + 1].at[3 * wi + j], (*chip, c)).start()
                i += 1
        token[...] = jnp.zeros_like(token)

    sem_shapes = []
    for grp in groups:
        sem_shapes += [pltpu.SemaphoreType.DMA((3 * len(grp),))] * 2
    out = pl.pallas_call(
        body, name=name, in_specs=[HBM] * nw,
        out_specs=[SEM] * (2 * ng) + [HBM] * nw + [VMEM],
        out_shape=sem_shapes + [pltpu.HBM(a.shape, a.dtype) for a in lands] + [jax.ShapeDtypeStruct((8, LANES), F32)],
        input_output_aliases={i: 2 * ng + i for i in range(nw)},
        compiler_params=pltpu.CompilerParams(has_side_effects=EFFECT),
    )(*[_hbm(a) for a in lands])
    sems = [(out[2 * g], out[2 * g + 1]) for g in range(ng)]
    return sems, list(out[2 * ng:2 * ng + nw]), out[-1]


def _gather_wait(name, grp, lands, ssem, rsem, after):
    n = len(grp)

    def body(*refs):
        land, ssem_ref, rsem_ref = refs[:n], refs[n], refs[n + 1]
        x, y, c = _mesh_pos()
        k_me = 2 * x + y
        chips, ks = _other_chips(x, y)
        for wi, w in enumerate(grp):
            for j, chip in enumerate(chips):
                cp = _remote(w.half(land[wi], k_me, c), w.half(land[wi], ks[j], c), ssem_ref.at[3 * wi + j],
                             rsem_ref.at[3 * wi + j], (*chip, c))
                cp.wait_send()
                cp.wait_recv()

    return pl.pallas_call(
        body, name=name, in_specs=[HBM] * n + [SEM, SEM, ANY], out_specs=[HBM] * n,
        out_shape=[pltpu.HBM(a.shape, a.dtype) for a in lands], input_output_aliases={i: i for i in range(n)},
        compiler_params=pltpu.CompilerParams(has_side_effects=EFFECT),
    )(*lands, ssem, rsem, after)


def _split_start(name, arrays, n, copies):
    na = len(arrays)

    def body(*refs):
        ssem, rsem, token = refs[na], refs[na + 1], refs[2 * na + 2]
        for i, (src, dst, dev, _) in enumerate(copies(refs[:na], *_mesh_pos())):
            _remote(src, dst, ssem.at[i], rsem.at[i], dev).start()
        token[...] = jnp.zeros_like(token)

    out = pl.pallas_call(
        body, name=name, in_specs=[HBM] * na, out_specs=[SEM, SEM] + [HBM] * na + [VMEM],
        out_shape=[pltpu.SemaphoreType.DMA((n,))] * 2 + [pltpu.HBM(a.shape, a.dtype) for a in arrays]
        + [jax.ShapeDtypeStruct((8, LANES), F32)],
        input_output_aliases={i: 2 + i for i in range(na)},
        compiler_params=pltpu.CompilerParams(has_side_effects=EFFECT),
    )(*[_hbm(a) for a in arrays])
    return out[0], out[1], list(out[2:2 + na]), out[-1]


def _split_wait(name, arrays, ssem, rsem, copies, after):
    na = len(arrays)

    def body(*refs):
        for i, (src, _, dev, dst) in enumerate(copies(refs[:na], *_mesh_pos())):
            cp = _remote(src, dst, refs[na].at[i], refs[na + 1].at[i], dev)
            cp.wait_send()
            cp.wait_recv()

    return list(pl.pallas_call(
        body, name=name, in_specs=[HBM] * na + [SEM, SEM] + [ANY] * len(after), out_specs=[HBM] * na,
        out_shape=[pltpu.HBM(a.shape, a.dtype) for a in arrays], input_output_aliases={i: i for i in range(na)},
        compiler_params=pltpu.CompilerParams(has_side_effects=EFFECT),
    )(*arrays, ssem, rsem, *after))


def _pass_copies(grp):
    def copies(land, x, y, c):
        _, ks = _other_chips(x, y)
        return [(w.half(land[wi], ks[j], c), w.half(land[wi], ks[j], c), (x, y, 1 - c), w.half(land[wi], ks[j], 1 - c))
                for wi, w in enumerate(grp) for j in range(3)]
    return copies


def _pair_copies(n):
    def copies(refs, x, y, c):
        return [(refs[i].at[:, 1 - c], refs[n + i], (x, y, 1 - c), refs[n + i]) for i in range(n)]
    return copies


def _gather_pass(name, grp, lands):
    n = len(grp)

    def body(*refs):
        out = refs[n:2 * n]
        ssem, rsem = refs[2 * n:]
        x, y, c = _mesh_pos()
        _, ks = _other_chips(x, y)
        sib = (x, y, 1 - c)
        cps = []
        for wi, w in enumerate(grp):
            for j in range(3):
                got = w.half(out[wi], ks[j], c)
                cps.append(_remote(got, got, ssem.at[3 * wi + j], rsem.at[3 * wi + j], sib))
        for cp in cps:
            cp.start()
        for wi, w in enumerate(grp):
            for j in range(3):
                theirs = w.half(out[wi], ks[j], 1 - c)
                _remote(theirs, theirs, ssem.at[3 * wi + j], rsem.at[3 * wi + j], sib).wait_recv()
        for cp in cps:
            cp.wait_send()

    return pl.pallas_call(
        body, name=name, in_specs=[ANY] * n, out_specs=[ANY] * n,
        out_shape=[jax.ShapeDtypeStruct(a.shape, a.dtype) for a in lands],
        scratch_shapes=[pltpu.SemaphoreType.DMA((3 * n,)), pltpu.SemaphoreType.DMA((3 * n,))],
        input_output_aliases={i: i for i in range(n)},
    )(*lands)


def _gather_conv_w(cw):
    ncw = cw.shape[1]

    def body(cw_ref, out_ref, ssem, rsem):
        x, y, c = _mesh_pos()
        k_me = 2 * x + y
        chips, ks = _other_chips(x, y)
        cols = lambda k: out_ref.at[:, pl.ds(pl.multiple_of(k * ncw, LANES), ncw)]
        cps = [_remote(cw_ref, cols(k_me), ssem.at[j], rsem.at[j], (*chip, c)) for j, chip in enumerate(chips)]
        for cp in cps:
            cp.start()
        for k in range(N_CHIPS):
            @pl.when(k_me == k)
            def _():
                out_ref[:, k * ncw:(k + 1) * ncw] = cw_ref[...]
        for j in range(3):
            _remote(cw_ref, cols(ks[j]), ssem.at[j], rsem.at[j], (*chips[j], c)).wait_recv()
        for cp in cps:
            cp.wait_send()

    return pl.pallas_call(
        body, name="gather_conv_w", in_specs=[VMEM], out_specs=VMEM,
        out_shape=jax.ShapeDtypeStruct((3, N_CHIPS * ncw), F32),
        scratch_shapes=[pltpu.SemaphoreType.DMA((3,)), pltpu.SemaphoreType.DMA((3,))],
    )(cw)


def _grad_tiles(w, n):
    return _tile8(w.R, 512) if w.R <= 512 else w.R // 2, _tile(n, 2048)


def _pair_sum(name, w, pos, grad, got):
    tr, tn = _grad_tiles(w, w.N)

    def body(pos_ref, g_ref, r_ref, o_ref):
        o_ref[...] = (g_ref[...].astype(F32) + r_ref[...].astype(F32)).astype(BF16)

    blk = pl.BlockSpec((None, tr, tn), lambda p, i, j, pos: (p, i, j))
    grid_spec = pltpu.PrefetchScalarGridSpec(
        num_scalar_prefetch=1, grid=(w.P, w.R // tr, w.N // tn),
        in_specs=[pl.BlockSpec((None, None, tr, tn), lambda p, i, j, pos: (p, pos[0], i, j)), blk], out_specs=blk)
    return pl.pallas_call(body, name=name, grid_spec=grid_spec, out_shape=jax.ShapeDtypeStruct((w.P, w.R, w.N), BF16),
                          compiler_params=_cp(("parallel",) * 3))(pos, grad, got)


def _scatter_start(name, ws, pairs):
    nw = len(ws)

    def body(*refs):
        pr, land = refs[:nw], refs[nw:2 * nw]
        ssem, rsem = refs[2 * nw], refs[2 * nw + 1]
        token = refs[4 * nw + 2]
        x, y, c = _mesh_pos()
        chips, ks = _other_chips(x, y)
        for i, w in enumerate(ws):
            for j, chip in enumerate(chips):
                _remote(w.part(pr[i], ks[j]), land[i].at[j], ssem.at[3 * i + j], rsem.at[3 * i + j], (*chip, c)).start()
        token[...] = jnp.zeros_like(token)

    lands = [lax.empty((3, w.R, w.nn), BF16) for w in ws]
    out = pl.pallas_call(
        body, name=name, in_specs=[HBM] * (2 * nw),
        out_specs=[SEM, SEM] + [HBM] * (2 * nw) + [VMEM],
        out_shape=[pltpu.SemaphoreType.DMA((3 * nw,))] * 2 + [pltpu.HBM(a.shape, a.dtype) for a in pairs + lands]
        + [jax.ShapeDtypeStruct((8, LANES), F32)],
        input_output_aliases={i: 2 + i for i in range(2 * nw)},
        compiler_params=pltpu.CompilerParams(has_side_effects=EFFECT),
    )(*[_hbm(a) for a in pairs + lands])
    return out[0], out[1], list(out[2:2 + nw]), list(out[2 + nw:2 + 2 * nw]), out[-1]


def _scatter_wait(name, ws, pairs, lands, ssem, rsem, after):
    nw = len(ws)

    def body(*refs):
        pr, land = refs[:nw], refs[nw:2 * nw]
        ssem_ref, rsem_ref = refs[2 * nw], refs[2 * nw + 1]
        x, y, c = _mesh_pos()
        chips, ks = _other_chips(x, y)
        for i, w in enumerate(ws):
            for j, chip in enumerate(chips):
                cp = _remote(w.part(pr[i], ks[j]), land[i].at[j], ssem_ref.at[3 * i + j], rsem_ref.at[3 * i + j], (*chip, c))
                cp.wait_send()
                cp.wait_recv()

    out = pl.pallas_call(
        body, name=name, in_specs=[HBM] * (2 * nw) + [SEM, SEM] + [ANY] * len(after), out_specs=[HBM] * (2 * nw),
        out_shape=[pltpu.HBM(a.shape, a.dtype) for a in pairs + lands],
        input_output_aliases={i: i for i in range(2 * nw)},
        compiler_params=pltpu.CompilerParams(has_side_effects=EFFECT),
    )(*pairs, *lands, ssem, rsem, *after)
    return list(out[nw:])


def _final_sum(name, w, pos, grad, got, parts):
    tr, tn = _grad_tiles(w, w.nn)
    nbc = w.nn // tn

    def body(pos_ref, g_ref, r_ref, p_ref, o_ref):
        acc = g_ref[...].astype(F32) + r_ref[...].astype(F32)
        for j in range(3):
            acc = acc + p_ref[j].astype(F32)
        o_ref[...] = acc

    if w.colshard:
        g_spec = pl.BlockSpec((None, None, tr, tn), lambda i, j, pos: (0, pos[0], i, pos[1] * nbc + j))
        r_spec = pl.BlockSpec((None, tr, tn), lambda i, j, pos: (0, i, pos[1] * nbc + j))
    else:
        g_spec = pl.BlockSpec((None, None, tr, tn), lambda i, j, pos: (pos[1], pos[0], i, j))
        r_spec = pl.BlockSpec((None, tr, tn), lambda i, j, pos: (pos[1], i, j))
    grid_spec = pltpu.PrefetchScalarGridSpec(
        num_scalar_prefetch=1, grid=(w.R // tr, nbc),
        in_specs=[g_spec, r_spec, pl.BlockSpec((3, tr, tn), lambda i, j, pos: (0, i, j))],
        out_specs=pl.BlockSpec((None, tr, tn), lambda i, j, pos: (pos[0], i, j)))
    return pl.pallas_call(body, name=name, grid_spec=grid_spec, out_shape=jax.ShapeDtypeStruct((2, w.R, w.nn), F32),
                          compiler_params=_cp(("parallel",) * 2))(pos, grad, got, parts)


def _share_halves(name, ws, halves, deps=()):
    nw = len(ws)

    def body(*refs):
        out = refs[nw + len(deps):2 * nw + len(deps)]
        ssem, rsem = refs[2 * nw + len(deps):]
        x, y, c = _mesh_pos()
        sib = (x, y, 1 - c)
        cps = [_remote(out[i].at[c], out[i].at[c], ssem.at[i], rsem.at[i], sib) for i in range(nw)]
        for cp in cps:
            cp.start()
        for i, cp in enumerate(cps):
            cp.wait_send()
            _remote(out[i].at[1 - c], out[i].at[1 - c], ssem.at[i], rsem.at[i], sib).wait_recv()

    return pl.pallas_call(
        body, name=name, in_specs=[ANY] * (nw + len(deps)), out_specs=[ANY] * nw,
        out_shape=[jax.ShapeDtypeStruct(h.shape, F32) for h in halves],
        scratch_shapes=[pltpu.SemaphoreType.DMA((nw,)), pltpu.SemaphoreType.DMA((nw,))],
        input_output_aliases={i: i for i in range(nw)},
    )(*halves, *deps)


VEC_ROWS = 16


def _vector_step(d, n_conv, parts, params):
    ncw = params[2][0].shape[1]
    n_par = len(params)

    def body(*refs):
        dg1, dba, dbb, dcw, dcb, dps, dg2, dgf, lc = refs[:9]
        wmv = refs[9:9 + 3 * n_par]
        outs = refs[9 + 3 * n_par:9 + 7 * n_par]
        loss_ref = refs[9 + 7 * n_par]
        snd, got, ssem, rsem = refs[9 + 7 * n_par + 1:]
        x, y, c = _mesh_pos()
        me = 4 * x + 2 * y + c
        snd[...] = jnp.zeros_like(snd)
        for row, ref in ((0, dg1), (1, dba), (2, dbb), (3, dps), (4, dg2), (5, dgf), (6, lc)):
            snd[row:row + 1, :] = ref[...]
        snd[7:8, :n_conv] = dcb[...]
        snd[8:11, :n_conv] = dcw[...]
        cps = []
        for r in range(1, N_DEV):
            peer = tuple(1 - p if (r >> b) & 1 else p for p, b in ((x, 2), (y, 1), (c, 0)))
            cps.append(_remote(snd, got.at[me], ssem.at[r - 1], rsem.at[r - 1], peer))
        for cp in cps:
            cp.start()
        got[me] = snd[...]
        for r in range(1, N_DEV):
            peer = tuple(1 - p if (r >> b) & 1 else p for p, b in ((x, 2), (y, 1), (c, 0)))
            _remote(snd, got.at[4 * peer[0] + 2 * peer[1] + peer[2]], ssem.at[r - 1], rsem.at[r - 1], peer).wait_recv()
        for cp in cps:
            cp.wait_send()
        tot = got[0]
        for dev in range(1, N_DEV):
            tot = tot + got[dev]
        loss_ref[...] = jnp.sum(tot[6:7, :], axis=1, keepdims=True)
        k_me = 2 * x + y
        g_cw = jnp.zeros((3, ncw), F32)
        for k in range(N_CHIPS):
            g_cw = g_cw + jnp.where(k_me == k, tot[8:11, k * ncw:(k + 1) * ncw], 0.0)
        grads = [tot[0:1, :], jnp.concatenate([tot[1:2, :], tot[2:3, :]], axis=1), g_cw, tot[7:8, :n_conv],
                 tot[3:4, :], tot[4:5, :], tot[5:6, :]]
        for i, g in enumerate(grads):
            w_ref, m_ref, v_ref = wmv[3 * i:3 * i + 3]
            delta, nm, nv = _adamw_math(w_ref[...], g, m_ref[...], v_ref[...])
            outs[4 * i][...] = g
            outs[4 * i + 1][...] = delta
            outs[4 * i + 2][...] = nm
            outs[4 * i + 3][...] = nv

    args = list(parts)
    out_shape = []
    for w, m, v in params:
        args += [w, m, v]
        out_shape += [jax.ShapeDtypeStruct(w.shape, F32)] * 4
    out_shape.append(jax.ShapeDtypeStruct((1, 1), F32))
    return pl.pallas_call(
        body, name="vector_params_step", in_specs=[VMEM] * len(args), out_specs=[VMEM] * len(out_shape),
        out_shape=out_shape,
        scratch_shapes=[pltpu.VMEM((VEC_ROWS, d), F32), pltpu.VMEM((N_DEV, VEC_ROWS, d), F32),
                        pltpu.SemaphoreType.DMA((N_DEV - 1,)), pltpu.SemaphoreType.DMA((N_DEV - 1,))],
        compiler_params=pltpu.CompilerParams(vmem_limit_bytes=VMEM_LIMIT),
    )(*args)


def kernel(x, norm1_g, w_in, b_gate, conv_w, conv_b, w_a_out, w_pool, pool_scale, w_o, norm2_g, w_ffn_gate, w_ffn_up, w_ffn_down, final_g, loss_target, m_norm1_g, m_w_in, m_b_gate, m_conv_w, m_conv_b, m_w_a_out, m_w_pool, m_pool_scale, m_w_o, m_norm2_g, m_w_ffn_gate, m_w_ffn_up, m_w_ffn_down, m_final_g, v_norm1_g, v_w_in, v_b_gate, v_conv_w, v_conv_b, v_w_a_out, v_w_pool, v_pool_scale, v_w_o, v_norm2_g, v_w_ffn_gate, v_w_ffn_up, v_w_ffn_down, v_final_g):
    t, d = x.shape[1], x.shape[2]
    n_conv = conv_b.shape[1]
    n_groups, pool_cg, pool_dg = w_pool.shape[1], w_pool.shape[2], N_CHIPS * w_pool.shape[3]
    d_ff = N_CHIPS * w_ffn_gate.shape[2]
    assert n_conv // n_groups == pool_cg and n_conv % (n_groups * MIX_COLS) == 0 and n_groups == len(POOL_WINDOWS)

    big = {"w_in": (w_in, m_w_in, v_w_in), "w_a_out": (w_a_out, m_w_a_out, v_w_a_out), "w_pool": (w_pool, m_w_pool, v_w_pool),
           "w_o": (w_o, m_w_o, v_w_o), "w_ffn_gate": (w_ffn_gate, m_w_ffn_gate, v_w_ffn_gate),
           "w_ffn_up": (w_ffn_up, m_w_ffn_up, v_w_ffn_up), "w_ffn_down": (w_ffn_down, m_w_ffn_down, v_w_ffn_down)}
    colshard = {"w_in": True, "w_a_out": True, "w_pool": True, "w_o": False, "w_ffn_gate": True, "w_ffn_up": True,
                "w_ffn_down": False}
    names = list(big)
    shard2d = {n: big[n][0].reshape(-1, big[n][0].shape[-1]) for n in names}
    ws = [_Weight(n, *shard2d[n].shape, colshard[n]) for n in names]

    xs, tgt = x[0], loss_target[0]
    cw_loc = conv_w[0]
    pos = jnp.stack([lax.axis_index("c"), 2 * lax.axis_index("x") + lax.axis_index("y")]).astype(jnp.int32)
    by_name = {w.name: w for w in ws}
    groups = [[by_name[n] for n in g] for g in (["w_in"], ["w_a_out", "w_pool", "w_o"], ["w_ffn_gate"], ["w_ffn_up"],
                                                 ["w_ffn_down"])]
    first = [sum(len(g) for g in groups[:i]) for i in range(len(groups))]
    rgroups = [groups[0], groups[1], groups[2] + groups[3], groups[4]]

    cw_full = _gather_conv_w(cw_loc)
    cast = lambda w: _cast_place(f"cast_{w.name}", w, pos, shard2d[w.name].reshape(2, w.R, w.nn), deps=[cw_full])
    sems_a, lands_a, _ = _gather_start("gather_start_a", groups[:1], [cast(w) for w in groups[0]])
    sems_b, lands_b, tok_b = _gather_start("gather_start_b", groups[1:], [cast(w) for grp in groups[1:] for w in grp])
    gsems, lands = sems_a + sems_b, lands_a + lands_b
    full = {}

    def landed(g, after):
        return _gather_wait(f"gather_wait_{g}", groups[g], lands[first[g]:first[g] + len(groups[g])], *gsems[g], after)

    def pass_start(g, got):
        return _split_start(f"pass_start_{g}", got, 3 * len(got), _pass_copies(groups[g]))

    def pass_wait(g, started, after):
        ssem, rsem, got, _ = started
        got = _split_wait(f"pass_wait_{g}", got, ssem, rsem, _pass_copies(groups[g]), after)
        full.update({w.name: a.reshape(w.P * 2 * w.R, w.N) for w, a in zip(groups[g], got)})

    h1 = _rms_fwd("norm1_fwd", xs, norm1_g)
    got = _gather_pass("gather_pass_0", groups[0], landed(0, tok_b))
    w_in_full = got[0].reshape(-1, groups[0][0].N)
    proj = _mm_nn("proj_in", h1, w_in_full, BF16)
    st = pass_start(1, landed(1, proj))
    z, p = _mixer_fwd("mixer_fwd", proj, cw_full, conv_b, n_conv, n_groups, deps=[st[3]])
    pass_wait(1, st, [z])
    wp_full = full["w_pool"].reshape(n_groups, pool_cg, pool_dg)
    ya = _mm_nn("conv_out", z, full["w_a_out"], BF16)
    yb = _gmm_nn("pool_out", p, wp_full, BF16)
    merged = _merge_fwd("merge_fwd", proj, b_gate, ya, yb, pool_scale)
    x2 = _mm_nn("mix_out", merged, full["w_o"], F32, add=xs)
    h2 = _rms_fwd("norm2_fwd", x2, norm2_g)
    st_g = pass_start(2, landed(2, h2))
    got_u = landed(3, st_g[3])
    pass_wait(2, st_g, got_u[:1])
    st_u = pass_start(3, got_u)
    gate = _mm_nn("ffn_gate", h2, full["w_ffn_gate"], BF16, deps=[st_u[3]])
    pass_wait(3, st_u, [gate])
    up = _mm_nn("ffn_up", h2, full["w_ffn_up"], BF16)
    st_d = pass_start(4, landed(4, up))
    act = _ffn_act("ffn_act", gate, up, deps=[st_d[3]])
    pass_wait(4, st_d, [act])
    x3 = _mm_nn("ffn_down", act, full["w_ffn_down"], F32, add=x2, tk=d_ff // 4)

    pending = {}

    def pair_start(g, grads):
        grp = rgroups[g]
        gcan = [grads[w.name].reshape(w.P, 2, w.R, w.N) for w in grp]
        slots = [lax.empty((w.P, w.R, w.N), BF16) for w in grp]
        pending[g] = _split_start(f"pair_start_{g}", gcan + slots, len(grp), _pair_copies(len(grp)))
        return pending[g][3]

    def scatter_start(g, after):
        grp = rgroups[g]
        n = len(grp)
        ssem, rsem, arrs, _ = pending[g]
        arrs = _split_wait(f"pair_wait_{g}", arrs, ssem, rsem, _pair_copies(n), after)
        gcan, sib = arrs[:n], arrs[n:]
        pairs = [_pair_sum(f"pair_sum_{w.name}", w, pos, a, s) for w, a, s in zip(grp, gcan, sib)]
        ssem, rsem, pairs, slots, token = _scatter_start(f"scatter_start_{g}", grp, pairs)
        pending[g] = (gcan, sib, pairs, slots, ssem, rsem)
        return token

    def reduce_finish(g, after):
        grp = rgroups[g]
        gcan, sib, pairs, slots, ssem, rsem = pending[g]
        parts = _scatter_wait(f"scatter_wait_{g}", grp, pairs, slots, ssem, rsem, after)
        return [_final_sum(f"final_sum_{w.name}", w, pos, a, s, q) for w, a, s, q in zip(grp, gcan, sib, parts)]

    grads = {}
    dx3, dx3b, d_gf, loss_cols = _final_bwd("final_bwd", x3, final_g.reshape(1, d), tgt)
    dact = _mm_nt("d_act", [(dx3b, full["w_ffn_down"])], BF16, tk=d)
    dgate, dup = _ffn_bwd("ffn_bwd", dact, gate, up)
    grads["w_ffn_down"] = _mm_tn("dw_ffn_down", act, dx3b, BF16)
    tok = pair_start(3, grads)
    dh2 = _mm_nt("d_h2", [(dgate, full["w_ffn_gate"]), (dup, full["w_ffn_up"])], F32, tk=d_ff // 4, deps=[tok])
    tok = scatter_start(3, [dh2])
    grads["w_ffn_gate"] = _mm_tn("dw_ffn_gate", h2, dgate, BF16, deps=[tok])
    grads["w_ffn_up"] = _mm_tn("dw_ffn_up", h2, dup, BF16)
    tok = pair_start(2, grads)
    dx2, dx2b, d_g2 = _rms_bwd("norm2_bwd", x2, norm2_g, dh2, dx3, True, deps=[tok])
    dmerged = _mm_nt("d_merged", [(dx2b, full["w_o"])], BF16, tk=d)
    grads["w_o"] = _mm_tn("dw_o", merged, dx2b, BF16)
    tok = scatter_start(2, [grads["w_o"]])
    dya, dyb, dproj, d_bga, d_bgb, d_ps = _merge_bwd("merge_bwd", dmerged, proj, b_gate, ya, yb, pool_scale, deps=[tok])
    dz = _mm_nt("d_z", [(dya, full["w_a_out"])], BF16, tk=d)
    grads["w_a_out"] = _mm_tn("dw_a_out", z, dya, BF16)
    dp = _gmm_nt("d_pool", dyb, wp_full, BF16)
    grads["w_pool"] = _gmm_tn("dw_pool", p, dyb, n_groups, BF16)
    tok = pair_start(1, grads)
    dproj, d_cw, d_cb = _mixer_bwd("mixer_bwd", dz, dp, proj, cw_full, conv_b, dproj, n_conv, n_groups, deps=[tok])
    tok = scatter_start(1, [dproj])
    dh1 = _mm_nt("d_h1", [(dproj, w_in_full)], F32, tk=proj.shape[1] // 4, deps=[tok])
    grad_x, d_g1 = _rms_bwd("norm1_bwd", xs, norm1_g, dh1, dx2, False)

    vec_names = ["norm1_g", "b_gate", "conv_w", "conv_b", "pool_scale", "norm2_g", "final_g"]
    vec = {"norm1_g": (norm1_g, m_norm1_g, v_norm1_g), "b_gate": (b_gate, m_b_gate, v_b_gate),
           "conv_w": (cw_loc, m_conv_w[0], v_conv_w[0]), "conv_b": (conv_b, m_conv_b, v_conv_b),
           "pool_scale": (pool_scale, m_pool_scale, v_pool_scale), "norm2_g": (norm2_g, m_norm2_g, v_norm2_g),
           "final_g": tuple(a.reshape(1, d) for a in (final_g, m_final_g, v_final_g))}
    vout = _vector_step(d, n_conv, [d_g1, d_bga, d_bgb, d_cw, d_cb, d_ps, d_g2, d_gf, loss_cols],
                        [vec[n] for n in vec_names])

    grads["w_in"] = _mm_tn("dw_in", h1, dproj, BF16, deps=[vout[-1]])
    tok = pair_start(0, grads)

    g_big, d_big, m_big, v_big = {}, {}, {}, {}

    def update(name, wsub, halves, deps=()):
        out = []
        for w, g in zip(wsub, _share_halves(name, wsub, halves, deps)):
            wt, mt, vt = big[w.name]
            g2 = g.reshape(2 * w.R, w.nn)
            go, dl, nm, nv = _adamw(f"adamw_{w.name}", shard2d[w.name], g2, mt.reshape(g2.shape), vt.reshape(g2.shape))
            g_big[w.name], d_big[w.name], m_big[w.name], v_big[w.name] = (a.reshape(wt.shape) for a in (go, dl, nm, nv))
            out.append(nv)
        return out

    after, early, early_halves = [tok], [], []
    for g in (3, 2, 1):
        halves = reduce_finish(g, after)
        early += rgroups[g]
        early_halves += halves
        after = halves[-1:]
    tok = scatter_start(0, after)
    after = update("share_halves_early", early, early_halves, deps=[tok])
    update("share_halves_w_in", rgroups[0], reduce_finish(0, after))

    shapes = {"conv_w": conv_w.shape, "final_g": final_g.shape}
    g_vec, d_vec, m_vec, v_vec = ({n: vout[4 * i + q].reshape(shapes.get(n, vec[n][0].shape)) for i, n in enumerate(vec_names)}
                                  for q in range(4))
    loss = vout[-1].reshape(())

    order = ["norm1_g", "w_in", "b_gate", "conv_w", "conv_b", "w_a_out", "w_pool", "pool_scale", "w_o", "norm2_g",
             "w_ffn_gate", "w_ffn_up", "w_ffn_down", "final_g"]
    pick = lambda vecs, bigs: [vecs[n] if n in vecs else bigs[n] for n in order]
    return (loss, grad_x.reshape(x.shape), *pick(g_vec, g_big), *pick(d_vec, d_big), *pick(m_vec, m_big),
            *pick(v_vec, v_big))
```

```python
import functools

import jax
import jax.numpy as jnp
from jax import lax
from jax.experimental import pallas as pl
from jax.experimental.pallas import tpu as pltpu

F32, BF16 = jnp.float32, jnp.bfloat16
MESH = pl.DeviceIdType.MESH
ANY = pl.BlockSpec(memory_space=pl.ANY)
VMEM = pl.BlockSpec(memory_space=pltpu.VMEM)
HBM = pl.BlockSpec(memory_space=pltpu.HBM)
SEM = pl.BlockSpec(memory_space=pltpu.SEMAPHORE)
EFFECT = pltpu.SideEffectType.DATAFLOW_SIDE_EFFECTING

EPS = 1e-6
POOL_WINDOWS = (2, 4, 8, 16)
ADAM_LR, ADAM_B1, ADAM_B2, ADAM_EPS, ADAM_WD, ADAM_STEP = 0.001, 0.9, 0.999, 1e-08, 0.01, 10

V7X_VMEM_BYTES = 64 * 1024 * 1024
VMEM_LIMIT = V7X_VMEM_BYTES * 3 // 4
LANES = 128
N_CHIPS = 4
N_DEV = 8

_DIMS = {
    "nn": (((1,), (0,)), ((), ())),
    "nt": (((1,), (1,)), ((), ())),
    "tn": (((0,), (0,)), ((), ())),
}


def _cp(sem):
    return pltpu.CompilerParams(dimension_semantics=sem, vmem_limit_bytes=VMEM_LIMIT)


def _mesh_pos():
    return lax.axis_index("x"), lax.axis_index("y"), lax.axis_index("c")


def _mm(name, pairs, *, mode, grid, out_shape, o_spec, nk=1, kaxis=None, add=None, deps=()):
    npair = len(pairs)
    has_add = add is not None

    def body(*refs):
        ab = refs[: 2 * npair]
        pos = 2 * npair
        add_ref = refs[pos] if has_add else None
        pos += int(has_add) + len(deps)
        o_ref = refs[pos]
        acc_ref = refs[pos + 1] if nk > 1 else None
        d = None
        for p in range(npair):
            t = lax.dot_general(ab[2 * p][...], ab[2 * p + 1][...], _DIMS[mode], preferred_element_type=F32)
            d = t if d is None else d + t
        if nk == 1:
            if has_add:
                d = d + add_ref[...].astype(F32)
            o_ref[...] = d.astype(o_ref.dtype)
        else:
            k = pl.program_id(kaxis)

            @pl.when(k == 0)
            def _():
                acc_ref[...] = d

            @pl.when(k > 0)
            def _():
                acc_ref[...] += d

            @pl.when(k == nk - 1)
            def _():
                r = acc_ref[...]
                if has_add:
                    r = r + add_ref[...].astype(F32)
                o_ref[...] = r.astype(o_ref.dtype)

    args, specs = [], []
    for a, a_spec, b, b_spec in pairs:
        args += [a, b]
        specs += [a_spec, b_spec]
    if has_add:
        args.append(add[0])
        specs.append(add[1])
    args += list(deps)
    specs += [ANY] * len(deps)
    scratch = []
    if nk > 1:
        blk = [d for d in o_spec.block_shape if d is not None]
        scratch = [pltpu.VMEM(tuple(blk), F32)]
    sem = tuple("arbitrary" if (nk > 1 and ax == kaxis) else "parallel" for ax in range(len(grid)))
    return pl.pallas_call(
        body, name=name, grid=grid, in_specs=specs, out_specs=o_spec, out_shape=out_shape,
        scratch_shapes=scratch, compiler_params=_cp(sem),
    )(*args)


def _tile(n, pref):
    if n <= pref:
        return n
    for t in range(pref, 0, -LANES):
        if t % LANES == 0 and n % t == 0:
            return t
    raise ValueError(f"no tile for {n}")


def _mm_nn(name, a, b, out_dtype, add=None, tk=None, deps=()):
    m, kk = a.shape
    n = b.shape[1]
    tm, tn = _tile(m, 1024), _tile(n, 512)
    out_shape = jax.ShapeDtypeStruct((m, n), out_dtype)
    if tk is None or tk == kk:
        grid = (m // tm, n // tn)
        pairs = [(a, pl.BlockSpec((tm, kk), lambda i, j: (i, 0)), b, pl.BlockSpec((kk, tn), lambda i, j: (0, j)))]
        o_spec = pl.BlockSpec((tm, tn), lambda i, j: (i, j))
        add_ = None if add is None else (add, pl.BlockSpec((tm, tn), lambda i, j: (i, j)))
        return _mm(name, pairs, mode="nn", grid=grid, out_shape=out_shape, o_spec=o_spec, add=add_, deps=deps)
    tn = _tile(n, 1024)
    nk = kk // tk
    grid = (m // tm, n // tn, nk)
    pairs = [(a, pl.BlockSpec((tm, tk), lambda i, j, k: (i, k)), b, pl.BlockSpec((tk, tn), lambda i, j, k: (k, j)))]
    o_spec = pl.BlockSpec((tm, tn), lambda i, j, k: (i, j))
    add_ = None if add is None else (add, pl.BlockSpec((tm, tn), lambda i, j, k: (i, j)))
    return _mm(name, pairs, mode="nn", grid=grid, out_shape=out_shape, o_spec=o_spec, nk=nk, kaxis=2, add=add_, deps=deps)


def _mm_nt(name, abs_, out_dtype, tk, deps=()):
    m, kk = abs_[0][0].shape
    n = abs_[0][1].shape[0]
    tm = _tile(m, 1024)
    nk = kk // tk
    tn = _tile(n, 512 if nk == 1 else 1024)
    out_shape = jax.ShapeDtypeStruct((m, n), out_dtype)
    if nk == 1:
        grid = (m // tm, n // tn)
        pairs = [(a, pl.BlockSpec((tm, kk), lambda i, j: (i, 0)), b, pl.BlockSpec((tn, kk), lambda i, j: (j, 0)))
                 for a, b in abs_]
        o_spec = pl.BlockSpec((tm, tn), lambda i, j: (i, j))
        return _mm(name, pairs, mode="nt", grid=grid, out_shape=out_shape, o_spec=o_spec, deps=deps)
    grid = (m // tm, n // tn, nk)
    pairs = [(a, pl.BlockSpec((tm, tk), lambda i, j, k: (i, k)), b, pl.BlockSpec((tn, tk), lambda i, j, k: (j, k)))
             for a, b in abs_]
    o_spec = pl.BlockSpec((tm, tn), lambda i, j, k: (i, j))
    return _mm(name, pairs, mode="nt", grid=grid, out_shape=out_shape, o_spec=o_spec, nk=nk, kaxis=2, deps=deps)


def _mm_tn(name, a, b, out_dtype, deps=()):
    t, m = a.shape
    n = b.shape[1]
    tm, tn = _tile(m, 512), _tile(n, 2048)
    if n > m:
        grid = (n // tn, m // tm)
        a_map, b_map, o_map = (lambda j, i: (0, i)), (lambda j, i: (0, j)), (lambda j, i: (i, j))
    else:
        grid = (m // tm, n // tn)
        a_map, b_map, o_map = (lambda i, j: (0, i)), (lambda i, j: (0, j)), (lambda i, j: (i, j))
    pairs = [(a, pl.BlockSpec((t, tm), a_map), b, pl.BlockSpec((t, tn), b_map))]
    o_spec = pl.BlockSpec((tm, tn), o_map)
    return _mm(name, pairs, mode="tn", grid=grid, out_shape=jax.ShapeDtypeStruct((m, n), out_dtype), o_spec=o_spec,
               deps=deps)


def _gmm_nn(name, p, w, out_dtype):
    t = p.shape[0]
    g, cg, dg = w.shape
    tm = _tile(t, 1024)
    pairs = [(p, pl.BlockSpec((tm, cg), lambda i, j: (i, j)), w, pl.BlockSpec((None, cg, dg), lambda i, j: (j, 0, 0)))]
    o_spec = pl.BlockSpec((tm, dg), lambda i, j: (i, j))
    return _mm(name, pairs, mode="nn", grid=(t // tm, g), out_shape=jax.ShapeDtypeStruct((t, g * dg), out_dtype),
               o_spec=o_spec)


def _gmm_nt(name, dy, w, out_dtype):
    t = dy.shape[0]
    g, cg, dg = w.shape
    tm = _tile(t, 1024)
    pairs = [(dy, pl.BlockSpec((tm, dg), lambda i, j: (i, j)), w, pl.BlockSpec((None, cg, dg), lambda i, j: (j, 0, 0)))]
    o_spec = pl.BlockSpec((tm, cg), lambda i, j: (i, j))
    return _mm(name, pairs, mode="nt", grid=(t // tm, g), out_shape=jax.ShapeDtypeStruct((t, g * cg), out_dtype),
               o_spec=o_spec)


def _gmm_tn(name, p, dy, g, out_dtype):
    t = p.shape[0]
    cg, dg = p.shape[1] // g, dy.shape[1] // g
    pairs = [(p, pl.BlockSpec((t, cg), lambda j: (0, j)), dy, pl.BlockSpec((t, dg), lambda j: (0, j)))]
    o_spec = pl.BlockSpec((None, cg, dg), lambda j: (j, 0, 0))
    return _mm(name, pairs, mode="tn", grid=(g,), out_shape=jax.ShapeDtypeStruct((g, cg, dg), out_dtype), o_spec=o_spec)


ROW_TILE = 256


def _rows(t):
    return _tile8(t, ROW_TILE)


def _tile8(n, pref):
    if n <= pref:
        return n
    for t in range(pref, 0, -8):
        if n % t == 0:
            return t
    raise ValueError(f"no row tile for {n}")


def _cast_place(name, w, pos, shard, deps=()):
    tr = _tile8(w.R, 512)
    if w.colshard:
        o_map = lambda h, i, pos: (0, h, i, pos[1])
    else:
        o_map = lambda h, i, pos: (pos[1], h, i, 0)

    def body(pos_ref, w_ref, *rest):
        rest[-1][...] = w_ref[...].astype(BF16)

    grid_spec = pltpu.PrefetchScalarGridSpec(
        num_scalar_prefetch=1, grid=(2, w.R // tr),
        in_specs=[pl.BlockSpec((None, tr, w.nn), lambda h, i, pos: (h, i, 0))] + [ANY] * len(deps),
        out_specs=pl.BlockSpec((None, None, tr, w.nn), o_map))
    return pl.pallas_call(body, name=name, grid_spec=grid_spec, out_shape=jax.ShapeDtypeStruct((w.P, 2, w.R, w.N), BF16),
                          compiler_params=_cp(("parallel", "parallel")))(pos, shard, *deps)


def _rms_fwd(name, x, g, deps=()):
    t, d = x.shape
    tm = _rows(t)

    def body(x_ref, g_ref, *rest):
        xf = x_ref[...]
        r = lax.rsqrt(jnp.mean(xf * xf, axis=-1, keepdims=True) + EPS)
        rest[-1][...] = (xf * r * g_ref[...]).astype(BF16)

    return pl.pallas_call(
        body, name=name, grid=(t // tm,),
        in_specs=[pl.BlockSpec((tm, d), lambda i: (i, 0)), pl.BlockSpec((1, d), lambda i: (0, 0))] + [ANY] * len(deps),
        out_specs=pl.BlockSpec((tm, d), lambda i: (i, 0)), out_shape=jax.ShapeDtypeStruct((t, d), BF16),
        compiler_params=_cp(("parallel",)),
    )(x, g, *deps)


def _rms_bwd(name, x, g, dh, dres, want_bf16, deps=()):
    t, d = x.shape
    tm = _rows(t)

    def body(x_ref, g_ref, dh_ref, dres_ref, *rest):
        rest = rest[len(deps):]
        dx_ref, rest = rest[0], rest[1:]
        dg_ref = rest[-1]
        xf = x_ref[...]
        r = lax.rsqrt(jnp.mean(xf * xf, axis=-1, keepdims=True) + EPS)
        xh = xf * r
        dhf = dh_ref[...]
        dxh = dhf * g_ref[...]
        m = jnp.mean(dxh * xh, axis=-1, keepdims=True)
        dx = dres_ref[...] + r * (dxh - xh * m)
        dx_ref[...] = dx
        if want_bf16:
            rest[0][...] = dx.astype(BF16)

        @pl.when(pl.program_id(0) == 0)
        def _():
            dg_ref[...] = jnp.zeros_like(dg_ref)

        dg_ref[...] += jnp.sum(dhf * xh, axis=0, keepdims=True)

    row = pl.BlockSpec((tm, d), lambda i: (i, 0))
    vec = pl.BlockSpec((1, d), lambda i: (0, 0))
    out_specs = [row] + ([row] if want_bf16 else []) + [vec]
    out_shape = ([jax.ShapeDtypeStruct((t, d), F32)] + ([jax.ShapeDtypeStruct((t, d), BF16)] if want_bf16 else [])
                 + [jax.ShapeDtypeStruct((1, d), F32)])
    return pl.pallas_call(body, name=name, grid=(t // tm,), in_specs=[row, vec, row, row] + [ANY] * len(deps),
                          out_specs=out_specs, out_shape=out_shape, compiler_params=_cp(("arbitrary",)))(x, g, dh, dres, *deps)


def _final_bwd(name, x3, gf, tgt):
    t, d = x3.shape
    tm = _rows(t)

    def body(x_ref, g_ref, t_ref, dx_ref, dxb_ref, dg_ref, lc_ref):
        xf = x_ref[...]
        g = g_ref[...]
        r = lax.rsqrt(jnp.mean(xf * xf, axis=-1, keepdims=True) + EPS)
        xh = xf * r
        diff = xh * g - t_ref[...]
        dy = diff * (1.0 / d)
        dxh = dy * g
        m = jnp.mean(dxh * xh, axis=-1, keepdims=True)
        dx = r * (dxh - xh * m)
        dx_ref[...] = dx
        dxb_ref[...] = dx.astype(BF16)

        @pl.when(pl.program_id(0) == 0)
        def _():
            dg_ref[...] = jnp.zeros_like(dg_ref)
            lc_ref[...] = jnp.zeros_like(lc_ref)

        dg_ref[...] += jnp.sum(dy * xh, axis=0, keepdims=True)
        lc_ref[...] += jnp.sum(diff * diff, axis=0, keepdims=True) * (0.5 / d)

    row = pl.BlockSpec((tm, d), lambda i: (i, 0))
    vec = pl.BlockSpec((1, d), lambda i: (0, 0))
    return pl.pallas_call(
        body, name=name, grid=(t // tm,), in_specs=[row, vec, row], out_specs=[row, row, vec, vec],
        out_shape=[jax.ShapeDtypeStruct((t, d), F32), jax.ShapeDtypeStruct((t, d), BF16),
                   jax.ShapeDtypeStruct((1, d), F32), jax.ShapeDtypeStruct((1, d), F32)],
        compiler_params=_cp(("arbitrary",)),
    )(x3, gf, tgt)


def _shift_down(v, k, t_idx):
    return jnp.where(t_idx >= k, pltpu.roll(v, k, 0), 0.0)


def _shift_up(v, k, t_idx):
    n = v.shape[0]
    return jnp.where(t_idx < n - k, pltpu.roll(v, n - k, 0), 0.0)


def _window_sums(v, shift, t_idx, grp):
    s = v + shift(v, 1, t_idx)
    out = s
    for lvl in range(1, len(POOL_WINDOWS)):
        s = s + shift(s, 1 << lvl, t_idx)
        out = jnp.where(grp >= lvl, s, out)
    return out


def _window_count(t_idx, grp):
    return jnp.minimum(t_idx + 1, jnp.left_shift(2, grp)).astype(F32)


MIX_COLS = 128


def _mixer_fwd(name, proj, cw, cb, n_conv, n_groups, deps=()):
    t = proj.shape[0]
    nb = n_conv // MIX_COLS
    per_group = n_conv // n_groups // MIX_COLS

    def body(ba_ref, ca_ref, va_ref, vb_ref, cw_ref, cb_ref, *rest):
        z_ref, p_ref = rest[len(deps):]
        t_idx = lax.broadcasted_iota(jnp.int32, (t, MIX_COLS), 0)
        q = ca_ref[...].astype(F32) * va_ref[...].astype(F32)
        w = cw_ref[...]
        u = cb_ref[...] + w[0:1] * _shift_down(q, 2, t_idx) + w[1:2] * _shift_down(q, 1, t_idx) + w[2:3] * q
        z_ref[...] = (ba_ref[...].astype(F32) * u).astype(BF16)
        grp = pl.program_id(0) // per_group
        v = vb_ref[...].astype(F32)
        p_ref[...] = (_window_sums(v, _shift_down, t_idx, grp) / _window_count(t_idx, grp) - v).astype(BF16)

    col = lambda s: pl.BlockSpec((t, MIX_COLS), lambda j: (0, s * nb + j))
    return pl.pallas_call(
        body, name=name, grid=(nb,),
        in_specs=[col(0), col(1), col(2), col(3), pl.BlockSpec((3, MIX_COLS), lambda j: (0, j)),
                  pl.BlockSpec((1, MIX_COLS), lambda j: (0, j))] + [ANY] * len(deps),
        out_specs=[col(0), col(0)],
        out_shape=[jax.ShapeDtypeStruct((t, n_conv), BF16), jax.ShapeDtypeStruct((t, n_conv), BF16)],
        compiler_params=_cp(("parallel",)),
    )(proj, proj, proj, proj, cw, cb, *deps)


def _mixer_bwd(name, dz, dp, proj, cw, cb, dproj, n_conv, n_groups, deps=()):
    t = proj.shape[0]
    nb = n_conv // MIX_COLS
    per_group = n_conv // n_groups // MIX_COLS

    def body(dz_ref, dp_ref, ba_ref, ca_ref, va_ref, cw_ref, cb_ref, _, *rest):
        o_ref, dcw_ref, dcb_ref, scr = rest[len(deps):]
        s = pl.program_id(1)

        @pl.when(s == 0)
        def _():
            t_idx = lax.broadcasted_iota(jnp.int32, (t, MIX_COLS), 0)
            ca, va = ca_ref[...].astype(F32), va_ref[...].astype(F32)
            q = ca * va
            q1, q2 = _shift_down(q, 1, t_idx), _shift_down(q, 2, t_idx)
            w = cw_ref[...]
            u = cb_ref[...] + w[0:1] * q2 + w[1:2] * q1 + w[2:3] * q
            dzf = dz_ref[...].astype(F32)
            du = dzf * ba_ref[...].astype(F32)
            scr[0] = (dzf * u).astype(BF16)
            dq = w[2:3] * du + w[1:2] * _shift_up(du, 1, t_idx) + w[0:1] * _shift_up(du, 2, t_idx)
            scr[1] = (dq * va).astype(BF16)
            scr[2] = (dq * ca).astype(BF16)
            dcb_ref[...] = jnp.sum(du, axis=0, keepdims=True)
            dcw_ref[0:1, :] = jnp.sum(du * q2, axis=0, keepdims=True)
            dcw_ref[1:2, :] = jnp.sum(du * q1, axis=0, keepdims=True)
            dcw_ref[2:3, :] = jnp.sum(du * q, axis=0, keepdims=True)
            grp = pl.program_id(0) // per_group
            dpf = dp_ref[...].astype(F32)
            e = dpf / _window_count(t_idx, grp)
            scr[3] = (_window_sums(e, _shift_up, t_idx, grp) - dpf).astype(BF16)

        o_ref[...] = scr[s]

    col = lambda c: pl.BlockSpec((t, MIX_COLS), lambda j, s: (0, c * nb + j))
    own = pl.BlockSpec((t, MIX_COLS), lambda j, s: (0, j))
    return pl.pallas_call(
        body, name=name, grid=(nb, 4),
        in_specs=[own, own, col(0), col(1), col(2), pl.BlockSpec((3, MIX_COLS), lambda j, s: (0, j)),
                  pl.BlockSpec((1, MIX_COLS), lambda j, s: (0, j)), ANY] + [ANY] * len(deps),
        out_specs=[pl.BlockSpec((t, MIX_COLS), lambda j, s: (0, s * nb + j)),
                   pl.BlockSpec((3, MIX_COLS), lambda j, s: (0, j)), pl.BlockSpec((1, MIX_COLS), lambda j, s: (0, j))],
        out_shape=[jax.ShapeDtypeStruct(dproj.shape, BF16), jax.ShapeDtypeStruct((3, n_conv), F32),
                   jax.ShapeDtypeStruct((1, n_conv), F32)],
        scratch_shapes=[pltpu.VMEM((4, t, MIX_COLS), BF16)],
        input_output_aliases={7: 0},
        compiler_params=_cp(("arbitrary", "arbitrary")),
    )(dz, dp, proj, proj, proj, cw, cb, dproj, *deps)


def _merge_fwd(name, proj, bg, ya, yb, ps):
    t, d = ya.shape
    tm = _rows(t)

    def body(gab_ref, bg_ref, ya_ref, yb_ref, ps_ref, o_ref):
        gab = gab_ref[...].astype(F32) + bg_ref[...]
        sa, sb = jax.nn.sigmoid(gab[:, :d]), jax.nn.sigmoid(gab[:, d:])
        o_ref[...] = (sa * ya_ref[...].astype(F32) + sb * (yb_ref[...].astype(F32) * ps_ref[...])).astype(BF16)

    row = pl.BlockSpec((tm, d), lambda i: (i, 0))
    return pl.pallas_call(
        body, name=name, grid=(t // tm,),
        in_specs=[pl.BlockSpec((tm, 2 * d), lambda i: (i, 1)), pl.BlockSpec((1, 2 * d), lambda i: (0, 0)), row, row,
                  pl.BlockSpec((1, d), lambda i: (0, 0))],
        out_specs=row, out_shape=jax.ShapeDtypeStruct((t, d), BF16), compiler_params=_cp(("parallel",)),
    )(proj, bg, ya, yb, ps)


def _merge_bwd(name, dm, proj, bg, ya, yb, ps, deps=()):
    t, d = ya.shape
    tm = _rows(t)

    def body(dm_ref, gab_ref, bg_ref, ya_ref, yb_ref, ps_ref, *rest):
        dya_ref, dyb_ref, dg_ref, dba_ref, dbb_ref, dps_ref = rest[len(deps):]
        gab = gab_ref[...].astype(F32) + bg_ref[...]
        sa, sb = jax.nn.sigmoid(gab[:, :d]), jax.nn.sigmoid(gab[:, d:])
        dmf = dm_ref[...].astype(F32)
        ybf, ps_ = yb_ref[...].astype(F32), ps_ref[...]
        dya_ref[...] = (dmf * sa).astype(BF16)
        dyb = dmf * sb
        dyb_ref[...] = (dyb * ps_).astype(BF16)
        dga = dmf * ya_ref[...].astype(F32) * sa * (1.0 - sa)
        dgb = dmf * (ybf * ps_) * sb * (1.0 - sb)
        dg_ref[:, :d] = dga.astype(BF16)
        dg_ref[:, d:] = dgb.astype(BF16)

        @pl.when(pl.program_id(0) == 0)
        def _():
            dba_ref[...] = jnp.zeros_like(dba_ref)
            dbb_ref[...] = jnp.zeros_like(dbb_ref)
            dps_ref[...] = jnp.zeros_like(dps_ref)

        dba_ref[...] += jnp.sum(dga, axis=0, keepdims=True)
        dbb_ref[...] += jnp.sum(dgb, axis=0, keepdims=True)
        dps_ref[...] += jnp.sum(dyb * ybf, axis=0, keepdims=True)

    row = pl.BlockSpec((tm, d), lambda i: (i, 0))
    vec = pl.BlockSpec((1, d), lambda i: (0, 0))
    gates = pl.BlockSpec((tm, 2 * d), lambda i: (i, 1))
    return pl.pallas_call(
        body, name=name, grid=(t // tm,),
        in_specs=[row, gates, pl.BlockSpec((1, 2 * d), lambda i: (0, 0)), row, row, vec] + [ANY] * len(deps),
        out_specs=[row, row, gates, vec, vec, vec],
        out_shape=[jax.ShapeDtypeStruct((t, d), BF16), jax.ShapeDtypeStruct((t, d), BF16),
                   jax.ShapeDtypeStruct(proj.shape, BF16), jax.ShapeDtypeStruct((1, d), F32),
                   jax.ShapeDtypeStruct((1, d), F32), jax.ShapeDtypeStruct((1, d), F32)],
        compiler_params=_cp(("arbitrary",)),
    )(dm, proj, bg, ya, yb, ps, *deps)


def _ffn_act(name, gate, up, deps=()):
    t, f = gate.shape
    tm, tf = _rows(t), _tile(f, 2048)

    def body(g_ref, u_ref, *rest):
        g = g_ref[...].astype(F32)
        rest[-1][...] = (g * jax.nn.sigmoid(g) * u_ref[...].astype(F32)).astype(BF16)

    blk = pl.BlockSpec((tm, tf), lambda i, j: (i, j))
    return pl.pallas_call(body, name=name, grid=(t // tm, f // tf), in_specs=[blk, blk] + [ANY] * len(deps), out_specs=blk,
                          out_shape=jax.ShapeDtypeStruct((t, f), BF16),
                          compiler_params=_cp(("parallel", "parallel")))(gate, up, *deps)


def _ffn_bwd(name, dact, gate, up):
    t, f = gate.shape
    tm, tf = _rows(t), _tile(f, 2048)

    def body(da_ref, g_ref, u_ref, dg_ref, du_ref):
        g, da = g_ref[...].astype(F32), da_ref[...].astype(F32)
        s = jax.nn.sigmoid(g)
        du_ref[...] = (da * (g * s)).astype(BF16)
        dg_ref[...] = (da * u_ref[...].astype(F32) * (s * (1.0 + g * (1.0 - s)))).astype(BF16)

    blk = pl.BlockSpec((tm, tf), lambda i, j: (i, j))
    shp = jax.ShapeDtypeStruct((t, f), BF16)
    return pl.pallas_call(body, name=name, grid=(t // tm, f // tf), in_specs=[blk, blk, blk], out_specs=[blk, blk],
                          out_shape=[shp, shp], compiler_params=_cp(("parallel", "parallel")))(dact, gate, up)


def _adamw_math(w, g, m, v):
    m = ADAM_B1 * m + (1.0 - ADAM_B1) * g
    v = ADAM_B2 * v + (1.0 - ADAM_B2) * (g * g)
    m_hat = m / (1.0 - ADAM_B1 ** ADAM_STEP)
    v_hat = v / (1.0 - ADAM_B2 ** ADAM_STEP)
    delta = -ADAM_LR * (m_hat / (jnp.sqrt(v_hat) + ADAM_EPS) + ADAM_WD * w)
    return delta, m, v


def _adamw(name, w, g, m, v):
    r, c = w.shape
    tr = _tile8(r, 512 if c <= 1024 else 256)

    def body(w_ref, g_ref, m_ref, v_ref, go_ref, d_ref, nm_ref, nv_ref):
        g = g_ref[...]
        go_ref[...] = g
        d_ref[...], nm_ref[...], nv_ref[...] = _adamw_math(w_ref[...], g, m_ref[...], v_ref[...])

    blk = pl.BlockSpec((tr, c), lambda i: (i, 0))
    shp = jax.ShapeDtypeStruct((r, c), F32)
    return pl.pallas_call(body, name=name, grid=(r // tr,), in_specs=[blk] * 4, out_specs=[blk] * 4,
                          out_shape=[shp] * 4, compiler_params=_cp(("parallel",)))(w, g, m, v)


class _Weight:
    def __init__(self, name, rows, cols, colshard):
        self.name, self.colshard = name, colshard
        self.R, self.nn = rows // 2, cols
        self.P = 1 if colshard else N_CHIPS
        self.N = N_CHIPS * cols if colshard else cols

    def cols(self, k):
        return pl.ds(pl.multiple_of(k * self.nn, LANES), self.nn)

    def shard(self, ref, k):
        return ref.at[0, :, :, self.cols(k)] if self.colshard else ref.at[k]

    def half(self, ref, k, h):
        return ref.at[0, h, :, self.cols(k)] if self.colshard else ref.at[k, h]

    def part(self, ref, k):
        return ref.at[0, :, self.cols(k)] if self.colshard else ref.at[k]


def _remote(src, dst, ssem, rsem, dev):
    return pltpu.make_async_remote_copy(src_ref=src, dst_ref=dst, send_sem=ssem, recv_sem=rsem, device_id=dev,
                                        device_id_type=MESH)


def _other_chips(x, y):
    chips = [(1 - x, y), (x, 1 - y), (1 - x, 1 - y)]
    return chips, [2 * cx + cy for cx, cy in chips]


def _hbm(a):
    return pltpu.with_memory_space_constraint(a, pltpu.HBM)


def _gather_start(name, groups, lands, after=()):
    flat = [w for grp in groups for w in grp]
    nw, ng = len(flat), len(groups)

    def body(*refs):
        land = refs[:nw]
        sems = refs[nw + len(after):nw + len(after) + 2 * ng]
        token = refs[2 * nw + len(after) + 2 * ng]
        x, y, c = _mesh_pos()
        k_me = 2 * x + y
        chips, _ = _other_chips(x, y)
        i = 0
        for g, grp in enumerate(groups):
            for wi, w in enumerate(grp):
                mine = w.half(land[i], k_me, c)
                for j, chip in enumerate(chips):
                    _remote(mine, mine, sems[2 * g].at[3 * wi + j], sems[2 * g + 1].at[3 * wi + j], (*chip, c)).start()
                i += 1
        token[...] = jnp.zeros_like(token)

    sem_shapes = []
    for grp in groups:
        sem_shapes += [pltpu.SemaphoreType.DMA((3 * len(grp),))] * 2
    out = pl.pallas_call(
        body, name=name, in_specs=[HBM] * nw + [ANY] * len(after),
        out_specs=[SEM] * (2 * ng) + [HBM] * nw + [VMEM],
        out_shape=sem_shapes + [pltpu.HBM(a.shape, a.dtype) for a in lands] + [jax.ShapeDtypeStruct((8, LANES), F32)],
        input_output_aliases={i: 2 * ng + i for i in range(nw)},
        compiler_params=pltpu.CompilerParams(has_side_effects=EFFECT),
    )(*[_hbm(a) for a in lands], *after)
    sems = [(out[2 * g], out[2 * g + 1]) for g in range(ng)]
    return sems, list(out[2 * ng:2 * ng + nw]), out[-1]


def _gather_wait(name, grp, lands, ssem, rsem, after):
    n = len(grp)

    def body(*refs):
        land, ssem_ref, rsem_ref = refs[:n], refs[n], refs[n + 1]
        x, y, c = _mesh_pos()
        k_me = 2 * x + y
        chips, ks = _other_chips(x, y)
        for wi, w in enumerate(grp):
            for j, chip in enumerate(chips):
                cp = _remote(w.half(land[wi], k_me, c), w.half(land[wi], ks[j], c), ssem_ref.at[3 * wi + j],
                             rsem_ref.at[3 * wi + j], (*chip, c))
                cp.wait_send()
                cp.wait_recv()

    return pl.pallas_call(
        body, name=name, in_specs=[HBM] * n + [SEM, SEM, ANY], out_specs=[HBM] * n,
        out_shape=[pltpu.HBM(a.shape, a.dtype) for a in lands], input_output_aliases={i: i for i in range(n)},
        compiler_params=pltpu.CompilerParams(has_side_effects=EFFECT),
    )(*lands, ssem, rsem, after)


def _split_start(name, arrays, n, copies, after=()):
    na = len(arrays)

    def body(*refs):
        ssem, rsem, token = refs[na + len(after):][0], refs[na + len(after):][1], refs[2 * na + len(after) + 2]
        for i, (src, dst, dev, _) in enumerate(copies(refs[:na], *_mesh_pos())):
            _remote(src, dst, ssem.at[i], rsem.at[i], dev).start()
        token[...] = jnp.zeros_like(token)

    out = pl.pallas_call(
        body, name=name, in_specs=[HBM] * na + [ANY] * len(after), out_specs=[SEM, SEM] + [HBM] * na + [VMEM],
        out_shape=[pltpu.SemaphoreType.DMA((n,))] * 2 + [pltpu.HBM(a.shape, a.dtype) for a in arrays]
        + [jax.ShapeDtypeStruct((8, LANES), F32)],
        input_output_aliases={i: 2 + i for i in range(na)},
        compiler_params=pltpu.CompilerParams(has_side_effects=EFFECT),
    )(*[_hbm(a) for a in arrays], *after)
    return out[0], out[1], list(out[2:2 + na]), out[-1]


def _split_wait(name, arrays, ssem, rsem, copies, after):
    na = len(arrays)

    def body(*refs):
        for i, (src, _, dev, dst) in enumerate(copies(refs[:na], *_mesh_pos())):
            cp = _remote(src, dst, refs[na].at[i], refs[na + 1].at[i], dev)
            cp.wait_send()
            cp.wait_recv()

    return list(pl.pallas_call(
        body, name=name, in_specs=[HBM] * na + [SEM, SEM] + [ANY] * len(after), out_specs=[HBM] * na,
        out_shape=[pltpu.HBM(a.shape, a.dtype) for a in arrays], input_output_aliases={i: i for i in range(na)},
        compiler_params=pltpu.CompilerParams(has_side_effects=EFFECT),
    )(*arrays, ssem, rsem, *after))


def _pass_copies(grp):
    def copies(land, x, y, c):
        _, ks = _other_chips(x, y)
        return [(w.half(land[wi], ks[j], c), w.half(land[wi], ks[j], c), (x, y, 1 - c), w.half(land[wi], ks[j], 1 - c))
                for wi, w in enumerate(grp) for j in range(3)]
    return copies


def _pair_copies(n):
    def copies(refs, x, y, c):
        return [(refs[i].at[:, 1 - c], refs[n + i], (x, y, 1 - c), refs[n + i]) for i in range(n)]
    return copies


def _gather_conv_w(cw):
    ncw = cw.shape[1]

    def body(cw_ref, out_ref, ssem, rsem):
        x, y, c = _mesh_pos()
        k_me = 2 * x + y
        chips, ks = _other_chips(x, y)
        cols = lambda k: out_ref.at[:, pl.ds(pl.multiple_of(k * ncw, LANES), ncw)]
        cps = [_remote(cw_ref, cols(k_me), ssem.at[j], rsem.at[j], (*chip, c)) for j, chip in enumerate(chips)]
        for cp in cps:
            cp.start()
        for k in range(N_CHIPS):
            @pl.when(k_me == k)
            def _():
                out_ref[:, k * ncw:(k + 1) * ncw] = cw_ref[...]
        for j in range(3):
            _remote(cw_ref, cols(ks[j]), ssem.at[j], rsem.at[j], (*chips[j], c)).wait_recv()
        for cp in cps:
            cp.wait_send()

    return pl.pallas_call(
        body, name="gather_conv_w", in_specs=[VMEM], out_specs=VMEM,
        out_shape=jax.ShapeDtypeStruct((3, N_CHIPS * ncw), F32),
        scratch_shapes=[pltpu.SemaphoreType.DMA((3,)), pltpu.SemaphoreType.DMA((3,))],
    )(cw)


def _grad_tiles(w, n):
    return _tile8(w.R, 512) if w.R <= 512 else w.R // 2, _tile(n, 2048)


def _pair_sum(name, w, pos, grad, got):
    tr, tn = _grad_tiles(w, w.N)

    def body(pos_ref, g_ref, r_ref, o_ref):
        o_ref[...] = (g_ref[...].astype(F32) + r_ref[...].astype(F32)).astype(BF16)

    blk = pl.BlockSpec((None, tr, tn), lambda p, i, j, pos: (p, i, j))
    grid_spec = pltpu.PrefetchScalarGridSpec(
        num_scalar_prefetch=1, grid=(w.P, w.R // tr, w.N // tn),
        in_specs=[pl.BlockSpec((None, None, tr, tn), lambda p, i, j, pos: (p, pos[0], i, j)), blk], out_specs=blk)
    return pl.pallas_call(body, name=name, grid_spec=grid_spec, out_shape=jax.ShapeDtypeStruct((w.P, w.R, w.N), BF16),
                          compiler_params=_cp(("parallel",) * 3))(pos, grad, got)


def _scatter_start(name, ws, pairs):
    nw = len(ws)

    def body(*refs):
        pr, land = refs[:nw], refs[nw:2 * nw]
        ssem, rsem = refs[2 * nw], refs[2 * nw + 1]
        token = refs[4 * nw + 2]
        x, y, c = _mesh_pos()
        chips, ks = _other_chips(x, y)
        for i, w in enumerate(ws):
            for j, chip in enumerate(chips):
                _remote(w.part(pr[i], ks[j]), land[i].at[j], ssem.at[3 * i + j], rsem.at[3 * i + j], (*chip, c)).start()
        token[...] = jnp.zeros_like(token)

    lands = [lax.empty((3, w.R, w.nn), BF16) for w in ws]
    out = pl.pallas_call(
        body, name=name, in_specs=[HBM] * (2 * nw),
        out_specs=[SEM, SEM] + [HBM] * (2 * nw) + [VMEM],
        out_shape=[pltpu.SemaphoreType.DMA((3 * nw,))] * 2 + [pltpu.HBM(a.shape, a.dtype) for a in pairs + lands]
        + [jax.ShapeDtypeStruct((8, LANES), F32)],
        input_output_aliases={i: 2 + i for i in range(2 * nw)},
        compiler_params=pltpu.CompilerParams(has_side_effects=EFFECT),
    )(*[_hbm(a) for a in pairs + lands])
    return out[0], out[1], list(out[2:2 + nw]), list(out[2 + nw:2 + 2 * nw]), out[-1]


def _scatter_wait(name, ws, pairs, lands, ssem, rsem, after):
    nw = len(ws)

    def body(*refs):
        pr, land = refs[:nw], refs[nw:2 * nw]
        ssem_ref, rsem_ref = refs[2 * nw], refs[2 * nw + 1]
        x, y, c = _mesh_pos()
        chips, ks = _other_chips(x, y)
        for i, w in enumerate(ws):
            for j, chip in enumerate(chips):
                cp = _remote(w.part(pr[i], ks[j]), land[i].at[j], ssem_ref.at[3 * i + j], rsem_ref.at[3 * i + j], (*chip, c))
                cp.wait_send()
                cp.wait_recv()

    out = pl.pallas_call(
        body, name=name, in_specs=[HBM] * (2 * nw) + [SEM, SEM] + [ANY] * len(after), out_specs=[HBM] * (2 * nw),
        out_shape=[pltpu.HBM(a.shape, a.dtype) for a in pairs + lands],
        input_output_aliases={i: i for i in range(2 * nw)},
        compiler_params=pltpu.CompilerParams(has_side_effects=EFFECT),
    )(*pairs, *lands, ssem, rsem, *after)
    return list(out[nw:])


def _final_sum(name, w, pos, grad, got, parts):
    tr, tn = _grad_tiles(w, w.nn)
    nbc = w.nn // tn

    def body(pos_ref, g_ref, r_ref, p_ref, o_ref):
        acc = g_ref[...].astype(F32) + r_ref[...].astype(F32)
        for j in range(3):
            acc = acc + p_ref[j].astype(F32)
        o_ref[...] = acc

    if w.colshard:
        g_spec = pl.BlockSpec((None, None, tr, tn), lambda i, j, pos: (0, pos[0], i, pos[1] * nbc + j))
        r_spec = pl.BlockSpec((None, tr, tn), lambda i, j, pos: (0, i, pos[1] * nbc + j))
    else:
        g_spec = pl.BlockSpec((None, None, tr, tn), lambda i, j, pos: (pos[1], pos[0], i, j))
        r_spec = pl.BlockSpec((None, tr, tn), lambda i, j, pos: (pos[1], i, j))
    grid_spec = pltpu.PrefetchScalarGridSpec(
        num_scalar_prefetch=1, grid=(w.R // tr, nbc),
        in_specs=[g_spec, r_spec, pl.BlockSpec((3, tr, tn), lambda i, j, pos: (0, i, j))],
        out_specs=pl.BlockSpec((None, tr, tn), lambda i, j, pos: (pos[0], i, j)))
    return pl.pallas_call(body, name=name, grid_spec=grid_spec, out_shape=jax.ShapeDtypeStruct((2, w.R, w.nn), F32),
                          compiler_params=_cp(("parallel",) * 2))(pos, grad, got, parts)


def _share_halves(name, ws, halves, deps=()):
    nw = len(ws)

    def body(*refs):
        out = refs[nw + len(deps):2 * nw + len(deps)]
        ssem, rsem = refs[2 * nw + len(deps):]
        x, y, c = _mesh_pos()
        sib = (x, y, 1 - c)
        cps = [_remote(out[i].at[c], out[i].at[c], ssem.at[i], rsem.at[i], sib) for i in range(nw)]
        for cp in cps:
            cp.start()
        for i, cp in enumerate(cps):
            cp.wait_send()
            _remote(out[i].at[1 - c], out[i].at[1 - c], ssem.at[i], rsem.at[i], sib).wait_recv()

    return pl.pallas_call(
        body, name=name, in_specs=[ANY] * (nw + len(deps)), out_specs=[ANY] * nw,
        out_shape=[jax.ShapeDtypeStruct(h.shape, F32) for h in halves],
        scratch_shapes=[pltpu.SemaphoreType.DMA((nw,)), pltpu.SemaphoreType.DMA((nw,))],
        input_output_aliases={i: i for i in range(nw)},
    )(*halves, *deps)


VEC_ROWS = 16


def _vector_step(d, n_conv, parts, params):
    ncw = params[2][0].shape[1]
    n_par = len(params)

    def body(*refs):
        dg1, dba, dbb, dcw, dcb, dps, dg2, dgf, lc = refs[:9]
        wmv = refs[9:9 + 3 * n_par]
        outs = refs[9 + 3 * n_par:9 + 7 * n_par]
        loss_ref = refs[9 + 7 * n_par]
        snd, got, ssem, rsem = refs[9 + 7 * n_par + 1:]
        x, y, c = _mesh_pos()
        me = 4 * x + 2 * y + c
        snd[...] = jnp.zeros_like(snd)
        for row, ref in ((0, dg1), (1, dba), (2, dbb), (3, dps), (4, dg2), (5, dgf), (6, lc)):
            snd[row:row + 1, :] = ref[...]
        snd[7:8, :n_conv] = dcb[...]
        snd[8:11, :n_conv] = dcw[...]
        cps = []
        for r in range(1, N_DEV):
            peer = tuple(1 - p if (r >> b) & 1 else p for p, b in ((x, 2), (y, 1), (c, 0)))
            cps.append(_remote(snd, got.at[me], ssem.at[r - 1], rsem.at[r - 1], peer))
        for cp in cps:
            cp.start()
        got[me] = snd[...]
        for r in range(1, N_DEV):
            peer = tuple(1 - p if (r >> b) & 1 else p for p, b in ((x, 2), (y, 1), (c, 0)))
            _remote(snd, got.at[4 * peer[0] + 2 * peer[1] + peer[2]], ssem.at[r - 1], rsem.at[r - 1], peer).wait_recv()
        for cp in cps:
            cp.wait_send()
        tot = got[0]
        for dev in range(1, N_DEV):
            tot = tot + got[dev]
        loss_ref[...] = jnp.sum(tot[6:7, :], axis=1, keepdims=True)
        k_me = 2 * x + y
        g_cw = jnp.zeros((3, ncw), F32)
        for k in range(N_CHIPS):
            g_cw = g_cw + jnp.where(k_me == k, tot[8:11, k * ncw:(k + 1) * ncw], 0.0)
        grads = [tot[0:1, :], jnp.concatenate([tot[1:2, :], tot[2:3, :]], axis=1), g_cw, tot[7:8, :n_conv],
                 tot[3:4, :], tot[4:5, :], tot[5:6, :]]
        for i, g in enumerate(grads):
            w_ref, m_ref, v_ref = wmv[3 * i:3 * i + 3]
            delta, nm, nv = _adamw_math(w_ref[...], g, m_ref[...], v_ref[...])
            outs[4 * i][...] = g
            outs[4 * i + 1][...] = delta
            outs[4 * i + 2][...] = nm
            outs[4 * i + 3][...] = nv

    args = list(parts)
    out_shape = []
    for w, m, v in params:
        args += [w, m, v]
        out_shape += [jax.ShapeDtypeStruct(w.shape, F32)] * 4
    out_shape.append(jax.ShapeDtypeStruct((1, 1), F32))
    return pl.pallas_call(
        body, name="vector_params_step", in_specs=[VMEM] * len(args), out_specs=[VMEM] * len(out_shape),
        out_shape=out_shape,
        scratch_shapes=[pltpu.VMEM((VEC_ROWS, d), F32), pltpu.VMEM((N_DEV, VEC_ROWS, d), F32),
                        pltpu.SemaphoreType.DMA((N_DEV - 1,)), pltpu.SemaphoreType.DMA((N_DEV - 1,))],
        compiler_params=pltpu.CompilerParams(vmem_limit_bytes=VMEM_LIMIT),
    )(*args)


def kernel(x, norm1_g, w_in, b_gate, conv_w, conv_b, w_a_out, w_pool, pool_scale, w_o, norm2_g, w_ffn_gate, w_ffn_up, w_ffn_down, final_g, loss_target, m_norm1_g, m_w_in, m_b_gate, m_conv_w, m_conv_b, m_w_a_out, m_w_pool, m_pool_scale, m_w_o, m_norm2_g, m_w_ffn_gate, m_w_ffn_up, m_w_ffn_down, m_final_g, v_norm1_g, v_w_in, v_b_gate, v_conv_w, v_conv_b, v_w_a_out, v_w_pool, v_pool_scale, v_w_o, v_norm2_g, v_w_ffn_gate, v_w_ffn_up, v_w_ffn_down, v_final_g):
    t, d = x.shape[1], x.shape[2]
    n_conv = conv_b.shape[1]
    n_groups, pool_cg, pool_dg = w_pool.shape[1], w_pool.shape[2], N_CHIPS * w_pool.shape[3]
    d_ff = N_CHIPS * w_ffn_gate.shape[2]
    assert n_conv // n_groups == pool_cg and n_conv % (n_groups * MIX_COLS) == 0 and n_groups == len(POOL_WINDOWS)

    big = {"w_in": (w_in, m_w_in, v_w_in), "w_a_out": (w_a_out, m_w_a_out, v_w_a_out), "w_pool": (w_pool, m_w_pool, v_w_pool),
           "w_o": (w_o, m_w_o, v_w_o), "w_ffn_gate": (w_ffn_gate, m_w_ffn_gate, v_w_ffn_gate),
           "w_ffn_up": (w_ffn_up, m_w_ffn_up, v_w_ffn_up), "w_ffn_down": (w_ffn_down, m_w_ffn_down, v_w_ffn_down)}
    colshard = {"w_in": True, "w_a_out": True, "w_pool": True, "w_o": False, "w_ffn_gate": True, "w_ffn_up": True,
                "w_ffn_down": False}
    names = list(big)
    shard2d = {n: big[n][0].reshape(-1, big[n][0].shape[-1]) for n in names}
    ws = [_Weight(n, *shard2d[n].shape, colshard[n]) for n in names]

    xs, tgt = x[0], loss_target[0]
    cw_loc = conv_w[0]
    pos = jnp.stack([lax.axis_index("c"), 2 * lax.axis_index("x") + lax.axis_index("y")]).astype(jnp.int32)
    by_name = {w.name: w for w in ws}
    groups = [[by_name[n] for n in g] for g in (["w_in"], ["w_a_out", "w_pool", "w_o"], ["w_ffn_gate"], ["w_ffn_up"],
                                                 ["w_ffn_down"])]
    first = [sum(len(g) for g in groups[:i]) for i in range(len(groups))]
    rgroups = [groups[0], groups[1], groups[2] + groups[3], groups[4]]

    cw_full = _gather_conv_w(cw_loc)
    cast = lambda w, dep: _cast_place(f"cast_{w.name}", w, pos, shard2d[w.name].reshape(2, w.R, w.nn), deps=[dep])
    sems_a, lands_a, tok_a = _gather_start("gather_start_a", groups[:1], [cast(w, cw_full) for w in groups[0]])
    rest = [cast(w, tok_a) for grp in groups[1:] for w in grp]
    full = {}

    def landed(g, after):
        return _gather_wait(f"gather_wait_{g}", groups[g], lands[first[g]:first[g] + len(groups[g])], *gsems[g], after)

    def pass_start(g, got, after=()):
        return _split_start(f"pass_start_{g}", got, 3 * len(got), _pass_copies(groups[g]), after)

    def pass_wait(g, started, after):
        ssem, rsem, got, _ = started
        got = _split_wait(f"pass_wait_{g}", got, ssem, rsem, _pass_copies(groups[g]), after)
        full.update({w.name: a.reshape(w.P * 2 * w.R, w.N) for w, a in zip(groups[g], got)})

    got = _gather_wait("gather_wait_0", groups[0], lands_a, *sems_a[0], rest[-1])
    sems_b, lands_b, tok_b = _gather_start("gather_start_b", groups[1:], rest, after=got)
    gsems, lands = sems_a + sems_b, lands_a + lands_b
    st = pass_start(0, got, after=[tok_b])
    h1 = _rms_fwd("norm1_fwd", xs, norm1_g, deps=[st[3]])
    pass_wait(0, st, [h1])
    w_in_full = full["w_in"]
    proj = _mm_nn("proj_in", h1, w_in_full, BF16)
    st = pass_start(1, landed(1, proj))
    z, p = _mixer_fwd("mixer_fwd", proj, cw_full, conv_b, n_conv, n_groups, deps=[st[3]])
    pass_wait(1, st, [z])
    wp_full = full["w_pool"].reshape(n_groups, pool_cg, pool_dg)
    ya = _mm_nn("conv_out", z, full["w_a_out"], BF16)
    yb = _gmm_nn("pool_out", p, wp_full, BF16)
    merged = _merge_fwd("merge_fwd", proj, b_gate, ya, yb, pool_scale)
    x2 = _mm_nn("mix_out", merged, full["w_o"], F32, add=xs)
    h2 = _rms_fwd("norm2_fwd", x2, norm2_g)
    st_g = pass_start(2, landed(2, h2))
    got_u = landed(3, st_g[3])
    pass_wait(2, st_g, got_u[:1])
    st_u = pass_start(3, got_u)
    gate = _mm_nn("ffn_gate", h2, full["w_ffn_gate"], BF16, deps=[st_u[3]])
    pass_wait(3, st_u, [gate])
    up = _mm_nn("ffn_up", h2, full["w_ffn_up"], BF16)
    st_d = pass_start(4, landed(4, up))
    act = _ffn_act("ffn_act", gate, up, deps=[st_d[3]])
    pass_wait(4, st_d, [act])
    x3 = _mm_nn("ffn_down", act, full["w_ffn_down"], F32, add=x2, tk=d_ff // 4)

    pending = {}

    def pair_start(g, grads):
        grp = rgroups[g]
        gcan = [grads[w.name].reshape(w.P, 2, w.R, w.N) for w in grp]
        slots = [lax.empty((w.P, w.R, w.N), BF16) for w in grp]
        pending[g] = _split_start(f"pair_start_{g}", gcan + slots, len(grp), _pair_copies(len(grp)))
        return pending[g][3]

    def scatter_start(g, after):
        grp = rgroups[g]
        n = len(grp)
        ssem, rsem, arrs, _ = pending[g]
        arrs = _split_wait(f"pair_wait_{g}", arrs, ssem, rsem, _pair_copies(n), after)
        gcan, sib = arrs[:n], arrs[n:]
        pairs = [_pair_sum(f"pair_sum_{w.name}", w, pos, a, s) for w, a, s in zip(grp, gcan, sib)]
        ssem, rsem, pairs, slots, token = _scatter_start(f"scatter_start_{g}", grp, pairs)
        pending[g] = (gcan, sib, pairs, slots, ssem, rsem)
        return token

    def reduce_finish(g, after):
        grp = rgroups[g]
        gcan, sib, pairs, slots, ssem, rsem = pending[g]
        parts = _scatter_wait(f"scatter_wait_{g}", grp, pairs, slots, ssem, rsem, after)
        return [_final_sum(f"final_sum_{w.name}", w, pos, a, s, q) for w, a, s, q in zip(grp, gcan, sib, parts)]

    grads = {}
    dx3, dx3b, d_gf, loss_cols = _final_bwd("final_bwd", x3, final_g.reshape(1, d), tgt)
    dact = _mm_nt("d_act", [(dx3b, full["w_ffn_down"])], BF16, tk=d)
    dgate, dup = _ffn_bwd("ffn_bwd", dact, gate, up)
    grads["w_ffn_down"] = _mm_tn("dw_ffn_down", act, dx3b, BF16)
    tok = pair_start(3, grads)
    dh2 = _mm_nt("d_h2", [(dgate, full["w_ffn_gate"]), (dup, full["w_ffn_up"])], F32, tk=d_ff // 4, deps=[tok])
    tok = scatter_start(3, [dh2])
    grads["w_ffn_gate"] = _mm_tn("dw_ffn_gate", h2, dgate, BF16, deps=[tok])
    grads["w_ffn_up"] = _mm_tn("dw_ffn_up", h2, dup, BF16)
    tok = pair_start(2, grads)
    dx2, dx2b, d_g2 = _rms_bwd("norm2_bwd", x2, norm2_g, dh2, dx3, True, deps=[tok])
    dmerged = _mm_nt("d_merged", [(dx2b, full["w_o"])], BF16, tk=d)
    grads["w_o"] = _mm_tn("dw_o", merged, dx2b, BF16)
    tok = scatter_start(2, [grads["w_o"]])
    dya, dyb, dproj, d_bga, d_bgb, d_ps = _merge_bwd("merge_bwd", dmerged, proj, b_gate, ya, yb, pool_scale, deps=[tok])
    dz = _mm_nt("d_z", [(dya, full["w_a_out"])], BF16, tk=d)
    grads["w_a_out"] = _mm_tn("dw_a_out", z, dya, BF16)
    dp = _gmm_nt("d_pool", dyb, wp_full, BF16)
    grads["w_pool"] = _gmm_tn("dw_pool", p, dyb, n_groups, BF16)
    tok = pair_start(1, grads)
    dproj, d_cw, d_cb = _mixer_bwd("mixer_bwd", dz, dp, proj, cw_full, conv_b, dproj, n_conv, n_groups, deps=[tok])
    tok = scatter_start(1, [dproj])
    dh1 = _mm_nt("d_h1", [(dproj, w_in_full)], F32, tk=proj.shape[1] // 4, deps=[tok])
    grad_x, d_g1 = _rms_bwd("norm1_bwd", xs, norm1_g, dh1, dx2, False)

    vec_names = ["norm1_g", "b_gate", "conv_w", "conv_b", "pool_scale", "norm2_g", "final_g"]
    vec = {"norm1_g": (norm1_g, m_norm1_g, v_norm1_g), "b_gate": (b_gate, m_b_gate, v_b_gate),
           "conv_w": (cw_loc, m_conv_w[0], v_conv_w[0]), "conv_b": (conv_b, m_conv_b, v_conv_b),
           "pool_scale": (pool_scale, m_pool_scale, v_pool_scale), "norm2_g": (norm2_g, m_norm2_g, v_norm2_g),
           "final_g": tuple(a.reshape(1, d) for a in (final_g, m_final_g, v_final_g))}
    vout = _vector_step(d, n_conv, [d_g1, d_bga, d_bgb, d_cw, d_cb, d_ps, d_g2, d_gf, loss_cols],
                        [vec[n] for n in vec_names])

    grads["w_in"] = _mm_tn("dw_in", h1, dproj, BF16, deps=[vout[-1]])
    tok = pair_start(0, grads)

    g_big, d_big, m_big, v_big = {}, {}, {}, {}

    def update(name, wsub, halves, deps=()):
        out = []
        for w, g in zip(wsub, _share_halves(name, wsub, halves, deps)):
            wt, mt, vt = big[w.name]
            g2 = g.reshape(2 * w.R, w.nn)
            go, dl, nm, nv = _adamw(f"adamw_{w.name}", shard2d[w.name], g2, mt.reshape(g2.shape), vt.reshape(g2.shape))
            g_big[w.name], d_big[w.name], m_big[w.name], v_big[w.name] = (a.reshape(wt.shape) for a in (go, dl, nm, nv))
            out.append(nv)
        return out

    after, early, early_halves = [tok], [], []
    for g in (3, 2, 1):
        halves = reduce_finish(g, after)
        early += rgroups[g]
        early_halves += halves
        after = halves[-1:]
    tok = scatter_start(0, after)
    after = update("share_halves_early", early, early_halves, deps=[tok])
    update("share_halves_w_in", rgroups[0], reduce_finish(0, after))

    shapes = {"conv_w": conv_w.shape, "final_g": final_g.shape}
    g_vec, d_vec, m_vec, v_vec = ({n: vout[4 * i + q].reshape(shapes.get(n, vec[n][0].shape)) for i, n in enumerate(vec_names)}
                                  for q in range(4))
    loss = vout[-1].reshape(())

    order = ["norm1_g", "w_in", "b_gate", "conv_w", "conv_b", "w_a_out", "w_pool", "pool_scale", "w_o", "norm2_g",
             "w_ffn_gate", "w_ffn_up", "w_ffn_down", "final_g"]
    pick = lambda vecs, bigs: [vecs[n] if n in vecs else bigs[n] for n in order]
    return (loss, grad_x.reshape(x.shape), *pick(g_vec, g_big), *pick(d_vec, d_big), *pick(m_vec, m_big),
            *pick(v_vec, v_big))
```

```python
import functools

import jax
import jax.numpy as jnp
from jax import lax
from jax.experimental import pallas as pl
from jax.experimental.pallas import tpu as pltpu

F32, BF16 = jnp.float32, jnp.bfloat16
MESH = pl.DeviceIdType.MESH
ANY = pl.BlockSpec(memory_space=pl.ANY)
VMEM = pl.BlockSpec(memory_space=pltpu.VMEM)
HBM = pl.BlockSpec(memory_space=pltpu.HBM)
SEM = pl.BlockSpec(memory_space=pltpu.SEMAPHORE)
EFFECT = pltpu.SideEffectType.DATAFLOW_SIDE_EFFECTING

EPS = 1e-6
POOL_WINDOWS = (2, 4, 8, 16)
ADAM_LR, ADAM_B1, ADAM_B2, ADAM_EPS, ADAM_WD, ADAM_STEP = 0.001, 0.9, 0.999, 1e-08, 0.01, 10

V7X_VMEM_BYTES = 64 * 1024 * 1024
VMEM_LIMIT = V7X_VMEM_BYTES * 3 // 4
LANES = 128
N_CHIPS = 4
N_DEV = 8

_DIMS = {
    "nn": (((1,), (0,)), ((), ())),
    "nt": (((1,), (1,)), ((), ())),
    "tn": (((0,), (0,)), ((), ())),
}


def _cp(sem):
    return pltpu.CompilerParams(dimension_semantics=sem, vmem_limit_bytes=VMEM_LIMIT)


def _mesh_pos():
    return lax.axis_index("x"), lax.axis_index("y"), lax.axis_index("c")


def _mm(name, pairs, *, mode, grid, out_shape, o_spec, nk=1, kaxis=None, add=None, deps=()):
    npair = len(pairs)
    has_add = add is not None

    def body(*refs):
        ab = refs[: 2 * npair]
        pos = 2 * npair
        add_ref = refs[pos] if has_add else None
        pos += int(has_add) + len(deps)
        o_ref = refs[pos]
        acc_ref = refs[pos + 1] if nk > 1 else None
        d = None
        for p in range(npair):
            t = lax.dot_general(ab[2 * p][...], ab[2 * p + 1][...], _DIMS[mode], preferred_element_type=F32)
            d = t if d is None else d + t
        if nk == 1:
            if has_add:
                d = d + add_ref[...].astype(F32)
            o_ref[...] = d.astype(o_ref.dtype)
        else:
            k = pl.program_id(kaxis)

            @pl.when(k == 0)
            def _():
                acc_ref[...] = d

            @pl.when(k > 0)
            def _():
                acc_ref[...] += d

            @pl.when(k == nk - 1)
            def _():
                r = acc_ref[...]
                if has_add:
                    r = r + add_ref[...].astype(F32)
                o_ref[...] = r.astype(o_ref.dtype)

    args, specs = [], []
    for a, a_spec, b, b_spec in pairs:
        args += [a, b]
        specs += [a_spec, b_spec]
    if has_add:
        args.append(add[0])
        specs.append(add[1])
    args += list(deps)
    specs += [ANY] * len(deps)
    scratch = []
    if nk > 1:
        blk = [d for d in o_spec.block_shape if d is not None]
        scratch = [pltpu.VMEM(tuple(blk), F32)]
    sem = tuple("arbitrary" if (nk > 1 and ax == kaxis) else "parallel" for ax in range(len(grid)))
    return pl.pallas_call(
        body, name=name, grid=grid, in_specs=specs, out_specs=o_spec, out_shape=out_shape,
        scratch_shapes=scratch, compiler_params=_cp(sem),
    )(*args)


def _tile(n, pref):
    if n <= pref:
        return n
    for t in range(pref, 0, -LANES):
        if t % LANES == 0 and n % t == 0:
            return t
    raise ValueError(f"no tile for {n}")


def _mm_nn(name, a, b, out_dtype, add=None, tk=None, deps=()):
    m, kk = a.shape
    n = b.shape[1]
    tm, tn = _tile(m, 1024), _tile(n, 512)
    out_shape = jax.ShapeDtypeStruct((m, n), out_dtype)
    if tk is None or tk == kk:
        grid = (m // tm, n // tn)
        pairs = [(a, pl.BlockSpec((tm, kk), lambda i, j: (i, 0)), b, pl.BlockSpec((kk, tn), lambda i, j: (0, j)))]
        o_spec = pl.BlockSpec((tm, tn), lambda i, j: (i, j))
        add_ = None if add is None else (add, pl.BlockSpec((tm, tn), lambda i, j: (i, j)))
        return _mm(name, pairs, mode="nn", grid=grid, out_shape=out_shape, o_spec=o_spec, add=add_, deps=deps)
    tn = _tile(n, 1024)
    nk = kk // tk
    grid = (m // tm, n // tn, nk)
    pairs = [(a, pl.BlockSpec((tm, tk), lambda i, j, k: (i, k)), b, pl.BlockSpec((tk, tn), lambda i, j, k: (k, j)))]
    o_spec = pl.BlockSpec((tm, tn), lambda i, j, k: (i, j))
    add_ = None if add is None else (add, pl.BlockSpec((tm, tn), lambda i, j, k: (i, j)))
    return _mm(name, pairs, mode="nn", grid=grid, out_shape=out_shape, o_spec=o_spec, nk=nk, kaxis=2, add=add_, deps=deps)


def _mm_nt(name, abs_, out_dtype, tk, deps=()):
    m, kk = abs_[0][0].shape
    n = abs_[0][1].shape[0]
    tm = _tile(m, 1024)
    nk = kk // tk
    tn = _tile(n, 512 if nk == 1 else 1024)
    out_shape = jax.ShapeDtypeStruct((m, n), out_dtype)
    if nk == 1:
        grid = (m // tm, n // tn)
        pairs = [(a, pl.BlockSpec((tm, kk), lambda i, j: (i, 0)), b, pl.BlockSpec((tn, kk), lambda i, j: (j, 0)))
                 for a, b in abs_]
        o_spec = pl.BlockSpec((tm, tn), lambda i, j: (i, j))
        return _mm(name, pairs, mode="nt", grid=grid, out_shape=out_shape, o_spec=o_spec, deps=deps)
    grid = (m // tm, n // tn, nk)
    pairs = [(a, pl.BlockSpec((tm, tk), lambda i, j, k: (i, k)), b, pl.BlockSpec((tn, tk), lambda i, j, k: (j, k)))
             for a, b in abs_]
    o_spec = pl.BlockSpec((tm, tn), lambda i, j, k: (i, j))
    return _mm(name, pairs, mode="nt", grid=grid, out_shape=out_shape, o_spec=o_spec, nk=nk, kaxis=2, deps=deps)


def _mm_tn(name, a, b, out_dtype, deps=()):
    t, m = a.shape
    n = b.shape[1]
    tm, tn = _tile(m, 512), _tile(n, 2048)
    if n > m:
        grid = (n // tn, m // tm)
        a_map, b_map, o_map = (lambda j, i: (0, i)), (lambda j, i: (0, j)), (lambda j, i: (i, j))
    else:
        grid = (m // tm, n // tn)
        a_map, b_map, o_map = (lambda i, j: (0, i)), (lambda i, j: (0, j)), (lambda i, j: (i, j))
    pairs = [(a, pl.BlockSpec((t, tm), a_map), b, pl.BlockSpec((t, tn), b_map))]
    o_spec = pl.BlockSpec((tm, tn), o_map)
    return _mm(name, pairs, mode="tn", grid=grid, out_shape=jax.ShapeDtypeStruct((m, n), out_dtype), o_spec=o_spec,
               deps=deps)


def _gmm_nn(name, p, w, out_dtype):
    t = p.shape[0]
    g, cg, dg = w.shape
    tm = _tile(t, 1024)
    pairs = [(p, pl.BlockSpec((tm, cg), lambda i, j: (i, j)), w, pl.BlockSpec((None, cg, dg), lambda i, j: (j, 0, 0)))]
    o_spec = pl.BlockSpec((tm, dg), lambda i, j: (i, j))
    return _mm(name, pairs, mode="nn", grid=(t // tm, g), out_shape=jax.ShapeDtypeStruct((t, g * dg), out_dtype),
               o_spec=o_spec)


def _gmm_nt(name, dy, w, out_dtype):
    t = dy.shape[0]
    g, cg, dg = w.shape
    tm = _tile(t, 1024)
    pairs = [(dy, pl.BlockSpec((tm, dg), lambda i, j: (i, j)), w, pl.BlockSpec((None, cg, dg), lambda i, j: (j, 0, 0)))]
    o_spec = pl.BlockSpec((tm, cg), lambda i, j: (i, j))
    return _mm(name, pairs, mode="nt", grid=(t // tm, g), out_shape=jax.ShapeDtypeStruct((t, g * cg), out_dtype),
               o_spec=o_spec)


def _gmm_tn(name, p, dy, g, out_dtype):
    t = p.shape[0]
    cg, dg = p.shape[1] // g, dy.shape[1] // g
    pairs = [(p, pl.BlockSpec((t, cg), lambda j: (0, j)), dy, pl.BlockSpec((t, dg), lambda j: (0, j)))]
    o_spec = pl.BlockSpec((None, cg, dg), lambda j: (j, 0, 0))
    return _mm(name, pairs, mode="tn", grid=(g,), out_shape=jax.ShapeDtypeStruct((g, cg, dg), out_dtype), o_spec=o_spec)


ROW_TILE = 256


def _rows(t):
    return _tile8(t, ROW_TILE)


def _tile8(n, pref):
    if n <= pref:
        return n
    for t in range(pref, 0, -8):
        if n % t == 0:
            return t
    raise ValueError(f"no row tile for {n}")


def _cast_place(name, w, pos, shard, deps=()):
    tr = _tile8(w.R, 512)
    if w.colshard:
        o_map = lambda h, i, pos: (0, h, i, pos[1])
    else:
        o_map = lambda h, i, pos: (pos[1], h, i, 0)

    def body(pos_ref, w_ref, *rest):
        rest[-1][...] = w_ref[...].astype(BF16)

    grid_spec = pltpu.PrefetchScalarGridSpec(
        num_scalar_prefetch=1, grid=(2, w.R // tr),
        in_specs=[pl.BlockSpec((None, tr, w.nn), lambda h, i, pos: (h, i, 0))] + [ANY] * len(deps),
        out_specs=pl.BlockSpec((None, None, tr, w.nn), o_map))
    return pl.pallas_call(body, name=name, grid_spec=grid_spec, out_shape=jax.ShapeDtypeStruct((w.P, 2, w.R, w.N), BF16),
                          compiler_params=_cp(("parallel", "parallel")))(pos, shard, *deps)


def _rms_fwd(name, x, g, deps=()):
    t, d = x.shape
    tm = _rows(t)

    def body(x_ref, g_ref, *rest):
        xf = x_ref[...]
        r = lax.rsqrt(jnp.mean(xf * xf, axis=-1, keepdims=True) + EPS)
        rest[-1][...] = (xf * r * g_ref[...]).astype(BF16)

    return pl.pallas_call(
        body, name=name, grid=(t // tm,),
        in_specs=[pl.BlockSpec((tm, d), lambda i: (i, 0)), pl.BlockSpec((1, d), lambda i: (0, 0))] + [ANY] * len(deps),
        out_specs=pl.BlockSpec((tm, d), lambda i: (i, 0)), out_shape=jax.ShapeDtypeStruct((t, d), BF16),
        compiler_params=_cp(("parallel",)),
    )(x, g, *deps)


def _rms_bwd(name, x, g, dh, dres, want_bf16, deps=()):
    t, d = x.shape
    tm = _rows(t)

    def body(x_ref, g_ref, dh_ref, dres_ref, *rest):
        rest = rest[len(deps):]
        dx_ref, rest = rest[0], rest[1:]
        dg_ref = rest[-1]
        xf = x_ref[...]
        r = lax.rsqrt(jnp.mean(xf * xf, axis=-1, keepdims=True) + EPS)
        xh = xf * r
        dhf = dh_ref[...]
        dxh = dhf * g_ref[...]
        m = jnp.mean(dxh * xh, axis=-1, keepdims=True)
        dx = dres_ref[...] + r * (dxh - xh * m)
        dx_ref[...] = dx
        if want_bf16:
            rest[0][...] = dx.astype(BF16)

        @pl.when(pl.program_id(0) == 0)
        def _():
            dg_ref[...] = jnp.zeros_like(dg_ref)

        dg_ref[...] += jnp.sum(dhf * xh, axis=0, keepdims=True)

    row = pl.BlockSpec((tm, d), lambda i: (i, 0))
    vec = pl.BlockSpec((1, d), lambda i: (0, 0))
    out_specs = [row] + ([row] if want_bf16 else []) + [vec]
    out_shape = ([jax.ShapeDtypeStruct((t, d), F32)] + ([jax.ShapeDtypeStruct((t, d), BF16)] if want_bf16 else [])
                 + [jax.ShapeDtypeStruct((1, d), F32)])
    return pl.pallas_call(body, name=name, grid=(t // tm,), in_specs=[row, vec, row, row] + [ANY] * len(deps),
                          out_specs=out_specs, out_shape=out_shape, compiler_params=_cp(("arbitrary",)))(x, g, dh, dres, *deps)


def _final_bwd(name, x3, gf, tgt):
    t, d = x3.shape
    tm = _rows(t)

    def body(x_ref, g_ref, t_ref, dx_ref, dxb_ref, dg_ref, lc_ref):
        xf = x_ref[...]
        g = g_ref[...]
        r = lax.rsqrt(jnp.mean(xf * xf, axis=-1, keepdims=True) + EPS)
        xh = xf * r
        diff = xh * g - t_ref[...]
        dy = diff * (1.0 / d)
        dxh = dy * g
        m = jnp.mean(dxh * xh, axis=-1, keepdims=True)
        dx = r * (dxh - xh * m)
        dx_ref[...] = dx
        dxb_ref[...] = dx.astype(BF16)

        @pl.when(pl.program_id(0) == 0)
        def _():
            dg_ref[...] = jnp.zeros_like(dg_ref)
            lc_ref[...] = jnp.zeros_like(lc_ref)

        dg_ref[...] += jnp.sum(dy * xh, axis=0, keepdims=True)
        lc_ref[...] += jnp.sum(diff * diff, axis=0, keepdims=True) * (0.5 / d)

    row = pl.BlockSpec((tm, d), lambda i: (i, 0))
    vec = pl.BlockSpec((1, d), lambda i: (0, 0))
    return pl.pallas_call(
        body, name=name, grid=(t // tm,), in_specs=[row, vec, row], out_specs=[row, row, vec, vec],
        out_shape=[jax.ShapeDtypeStruct((t, d), F32), jax.ShapeDtypeStruct((t, d), BF16),
                   jax.ShapeDtypeStruct((1, d), F32), jax.ShapeDtypeStruct((1, d), F32)],
        compiler_params=_cp(("arbitrary",)),
    )(x3, gf, tgt)


def _shift_down(v, k, t_idx):
    return jnp.where(t_idx >= k, pltpu.roll(v, k, 0), 0.0)


def _shift_up(v, k, t_idx):
    n = v.shape[0]
    return jnp.where(t_idx < n - k, pltpu.roll(v, n - k, 0), 0.0)


def _window_sums(v, shift, t_idx, grp):
    s = v + shift(v, 1, t_idx)
    out = s
    for lvl in range(1, len(POOL_WINDOWS)):
        s = s + shift(s, 1 << lvl, t_idx)
        out = jnp.where(grp >= lvl, s, out)
    return out


def _window_count(t_idx, grp):
    return jnp.minimum(t_idx + 1, jnp.left_shift(2, grp)).astype(F32)


MIX_COLS = 128


def _mixer_fwd(name, proj, cw, cb, n_conv, n_groups, deps=()):
    t = proj.shape[0]
    nb = n_conv // MIX_COLS
    per_group = n_conv // n_groups // MIX_COLS

    def body(ba_ref, ca_ref, va_ref, vb_ref, cw_ref, cb_ref, *rest):
        z_ref, p_ref = rest[len(deps):]
        t_idx = lax.broadcasted_iota(jnp.int32, (t, MIX_COLS), 0)
        q = ca_ref[...].astype(F32) * va_ref[...].astype(F32)
        w = cw_ref[...]
        u = cb_ref[...] + w[0:1] * _shift_down(q, 2, t_idx) + w[1:2] * _shift_down(q, 1, t_idx) + w[2:3] * q
        z_ref[...] = (ba_ref[...].astype(F32) * u).astype(BF16)
        grp = pl.program_id(0) // per_group
        v = vb_ref[...].astype(F32)
        p_ref[...] = (_window_sums(v, _shift_down, t_idx, grp) / _window_count(t_idx, grp) - v).astype(BF16)

    col = lambda s: pl.BlockSpec((t, MIX_COLS), lambda j: (0, s * nb + j))
    return pl.pallas_call(
        body, name=name, grid=(nb,),
        in_specs=[col(0), col(1), col(2), col(3), pl.BlockSpec((3, MIX_COLS), lambda j: (0, j)),
                  pl.BlockSpec((1, MIX_COLS), lambda j: (0, j))] + [ANY] * len(deps),
        out_specs=[col(0), col(0)],
        out_shape=[jax.ShapeDtypeStruct((t, n_conv), BF16), jax.ShapeDtypeStruct((t, n_conv), BF16)],
        compiler_params=_cp(("parallel",)),
    )(proj, proj, proj, proj, cw, cb, *deps)


def _mixer_bwd(name, dz, dp, proj, cw, cb, dproj, n_conv, n_groups, deps=()):
    t = proj.shape[0]
    nb = n_conv // MIX_COLS
    per_group = n_conv // n_groups // MIX_COLS

    def body(dz_ref, dp_ref, ba_ref, ca_ref, va_ref, cw_ref, cb_ref, _, *rest):
        o_ref, dcw_ref, dcb_ref, scr = rest[len(deps):]
        s = pl.program_id(1)

        @pl.when(s == 0)
        def _():
            t_idx = lax.broadcasted_iota(jnp.int32, (t, MIX_COLS), 0)
            ca, va = ca_ref[...].astype(F32), va_ref[...].astype(F32)
            q = ca * va
            q1, q2 = _shift_down(q, 1, t_idx), _shift_down(q, 2, t_idx)
            w = cw_ref[...]
            u = cb_ref[...] + w[0:1] * q2 + w[1:2] * q1 + w[2:3] * q
            dzf = dz_ref[...].astype(F32)
            du = dzf * ba_ref[...].astype(F32)
            scr[0] = (dzf * u).astype(BF16)
            dq = w[2:3] * du + w[1:2] * _shift_up(du, 1, t_idx) + w[0:1] * _shift_up(du, 2, t_idx)
            scr[1] = (dq * va).astype(BF16)
            scr[2] = (dq * ca).astype(BF16)
            dcb_ref[...] = jnp.sum(du, axis=0, keepdims=True)
            dcw_ref[0:1, :] = jnp.sum(du * q2, axis=0, keepdims=True)
            dcw_ref[1:2, :] = jnp.sum(du * q1, axis=0, keepdims=True)
            dcw_ref[2:3, :] = jnp.sum(du * q, axis=0, keepdims=True)
            grp = pl.program_id(0) // per_group
            dpf = dp_ref[...].astype(F32)
            e = dpf / _window_count(t_idx, grp)
            scr[3] = (_window_sums(e, _shift_up, t_idx, grp) - dpf).astype(BF16)

        o_ref[...] = scr[s]

    col = lambda c: pl.BlockSpec((t, MIX_COLS), lambda j, s: (0, c * nb + j))
    own = pl.BlockSpec((t, MIX_COLS), lambda j, s: (0, j))
    return pl.pallas_call(
        body, name=name, grid=(nb, 4),
        in_specs=[own, own, col(0), col(1), col(2), pl.BlockSpec((3, MIX_COLS), lambda j, s: (0, j)),
                  pl.BlockSpec((1, MIX_COLS), lambda j, s: (0, j)), ANY] + [ANY] * len(deps),
        out_specs=[pl.BlockSpec((t, MIX_COLS), lambda j, s: (0, s * nb + j)),
                   pl.BlockSpec((3, MIX_COLS), lambda j, s: (0, j)), pl.BlockSpec((1, MIX_COLS), lambda j, s: (0, j))],
        out_shape=[jax.ShapeDtypeStruct(dproj.shape, BF16), jax.ShapeDtypeStruct((3, n_conv), F32),
                   jax.ShapeDtypeStruct((1, n_conv), F32)],
        scratch_shapes=[pltpu.VMEM((4, t, MIX_COLS), BF16)],
        input_output_aliases={7: 0},
        compiler_params=_cp(("arbitrary", "arbitrary")),
    )(dz, dp, proj, proj, proj, cw, cb, dproj, *deps)


def _merge_fwd(name, proj, bg, ya, yb, ps):
    t, d = ya.shape
    tm = _rows(t)

    def body(gab_ref, bg_ref, ya_ref, yb_ref, ps_ref, o_ref):
        gab = gab_ref[...].astype(F32) + bg_ref[...]
        sa, sb = jax.nn.sigmoid(gab[:, :d]), jax.nn.sigmoid(gab[:, d:])
        o_ref[...] = (sa * ya_ref[...].astype(F32) + sb * (yb_ref[...].astype(F32) * ps_ref[...])).astype(BF16)

    row = pl.BlockSpec((tm, d), lambda i: (i, 0))
    return pl.pallas_call(
        body, name=name, grid=(t // tm,),
        in_specs=[pl.BlockSpec((tm, 2 * d), lambda i: (i, 1)), pl.BlockSpec((1, 2 * d), lambda i: (0, 0)), row, row,
                  pl.BlockSpec((1, d), lambda i: (0, 0))],
        out_specs=row, out_shape=jax.ShapeDtypeStruct((t, d), BF16), compiler_params=_cp(("parallel",)),
    )(proj, bg, ya, yb, ps)


def _merge_bwd(name, dm, proj, bg, ya, yb, ps, deps=()):
    t, d = ya.shape
    tm = _rows(t)

    def body(dm_ref, gab_ref, bg_ref, ya_ref, yb_ref, ps_ref, *rest):
        dya_ref, dyb_ref, dg_ref, dba_ref, dbb_ref, dps_ref = rest[len(deps):]
        gab = gab_ref[...].astype(F32) + bg_ref[...]
        sa, sb = jax.nn.sigmoid(gab[:, :d]), jax.nn.sigmoid(gab[:, d:])
        dmf = dm_ref[...].astype(F32)
        ybf, ps_ = yb_ref[...].astype(F32), ps_ref[...]
        dya_ref[...] = (dmf * sa).astype(BF16)
        dyb = dmf * sb
        dyb_ref[...] = (dyb * ps_).astype(BF16)
        dga = dmf * ya_ref[...].astype(F32) * sa * (1.0 - sa)
        dgb = dmf * (ybf * ps_) * sb * (1.0 - sb)
        dg_ref[:, :d] = dga.astype(BF16)
        dg_ref[:, d:] = dgb.astype(BF16)

        @pl.when(pl.program_id(0) == 0)
        def _():
            dba_ref[...] = jnp.zeros_like(dba_ref)
            dbb_ref[...] = jnp.zeros_like(dbb_ref)
            dps_ref[...] = jnp.zeros_like(dps_ref)

        dba_ref[...] += jnp.sum(dga, axis=0, keepdims=True)
        dbb_ref[...] += jnp.sum(dgb, axis=0, keepdims=True)
        dps_ref[...] += jnp.sum(dyb * ybf, axis=0, keepdims=True)

    row = pl.BlockSpec((tm, d), lambda i: (i, 0))
    vec = pl.BlockSpec((1, d), lambda i: (0, 0))
    gates = pl.BlockSpec((tm, 2 * d), lambda i: (i, 1))
    return pl.pallas_call(
        body, name=name, grid=(t // tm,),
        in_specs=[row, gates, pl.BlockSpec((1, 2 * d), lambda i: (0, 0)), row, row, vec] + [ANY] * len(deps),
        out_specs=[row, row, gates, vec, vec, vec],
        out_shape=[jax.ShapeDtypeStruct((t, d), BF16), jax.ShapeDtypeStruct((t, d), BF16),
                   jax.ShapeDtypeStruct(proj.shape, BF16), jax.ShapeDtypeStruct((1, d), F32),
                   jax.ShapeDtypeStruct((1, d), F32), jax.ShapeDtypeStruct((1, d), F32)],
        compiler_params=_cp(("arbitrary",)),
    )(dm, proj, bg, ya, yb, ps, *deps)


def _ffn_act(name, gate, up, deps=()):
    t, f = gate.shape
    tm, tf = _rows(t), _tile(f, 2048)

    def body(g_ref, u_ref, *rest):
        g = g_ref[...].astype(F32)
        rest[-1][...] = (g * jax.nn.sigmoid(g) * u_ref[...].astype(F32)).astype(BF16)

    blk = pl.BlockSpec((tm, tf), lambda i, j: (i, j))
    return pl.pallas_call(body, name=name, grid=(t // tm, f // tf), in_specs=[blk, blk] + [ANY] * len(deps), out_specs=blk,
                          out_shape=jax.ShapeDtypeStruct((t, f), BF16),
                          compiler_params=_cp(("parallel", "parallel")))(gate, up, *deps)


def _ffn_bwd(name, dy, w_down, gate, up):
    t, d = dy.shape
    f = w_down.shape[0]
    tm, tf = _tile(t, 1024), _tile(f, 512)

    def body(dy_ref, w_ref, g_ref, u_ref, dg_ref, du_ref):
        da = lax.dot_general(dy_ref[...], w_ref[...], _DIMS["nt"], preferred_element_type=F32)
        g = g_ref[...].astype(F32)
        s = jax.nn.sigmoid(g)
        du_ref[...] = (da * (g * s)).astype(BF16)
        dg_ref[...] = (da * u_ref[...].astype(F32) * (s * (1.0 + g * (1.0 - s)))).astype(BF16)

    blk = pl.BlockSpec((tm, tf), lambda i, j: (i, j))
    shp = jax.ShapeDtypeStruct((t, f), BF16)
    return pl.pallas_call(
        body, name=name, grid=(t // tm, f // tf),
        in_specs=[pl.BlockSpec((tm, d), lambda i, j: (i, 0)), pl.BlockSpec((tf, d), lambda i, j: (j, 0)), blk, blk],
        out_specs=[blk, blk], out_shape=[shp, shp], compiler_params=_cp(("parallel", "parallel")))(dy, w_down, gate, up)


def _adamw_math(w, g, m, v):
    m = ADAM_B1 * m + (1.0 - ADAM_B1) * g
    v = ADAM_B2 * v + (1.0 - ADAM_B2) * (g * g)
    m_hat = m / (1.0 - ADAM_B1 ** ADAM_STEP)
    v_hat = v / (1.0 - ADAM_B2 ** ADAM_STEP)
    delta = -ADAM_LR * (m_hat / (jnp.sqrt(v_hat) + ADAM_EPS) + ADAM_WD * w)
    return delta, m, v


def _adamw(name, w, g, m, v):
    r, c = w.shape
    tr = _tile8(r, 512 if c <= 1024 else 256)

    def body(w_ref, g_ref, m_ref, v_ref, go_ref, d_ref, nm_ref, nv_ref):
        g = g_ref[...]
        go_ref[...] = g
        d_ref[...], nm_ref[...], nv_ref[...] = _adamw_math(w_ref[...], g, m_ref[...], v_ref[...])

    blk = pl.BlockSpec((tr, c), lambda i: (i, 0))
    shp = jax.ShapeDtypeStruct((r, c), F32)
    return pl.pallas_call(body, name=name, grid=(r // tr,), in_specs=[blk] * 4, out_specs=[blk] * 4,
                          out_shape=[shp] * 4, compiler_params=_cp(("parallel",)))(w, g, m, v)


class _Weight:
    def __init__(self, name, rows, cols, colshard):
        self.name, self.colshard = name, colshard
        self.R, self.nn = rows // 2, cols
        self.P = 1 if colshard else N_CHIPS
        self.N = N_CHIPS * cols if colshard else cols

    def cols(self, k):
        return pl.ds(pl.multiple_of(k * self.nn, LANES), self.nn)

    def shard(self, ref, k):
        return ref.at[0, :, :, self.cols(k)] if self.colshard else ref.at[k]

    def half(self, ref, k, h):
        return ref.at[0, h, :, self.cols(k)] if self.colshard else ref.at[k, h]

    def part(self, ref, k):
        return ref.at[0, :, self.cols(k)] if self.colshard else ref.at[k]


def _remote(src, dst, ssem, rsem, dev):
    return pltpu.make_async_remote_copy(src_ref=src, dst_ref=dst, send_sem=ssem, recv_sem=rsem, device_id=dev,
                                        device_id_type=MESH)


def _other_chips(x, y):
    chips = [(1 - x, y), (x, 1 - y), (1 - x, 1 - y)]
    return chips, [2 * cx + cy for cx, cy in chips]


def _hbm(a):
    return pltpu.with_memory_space_constraint(a, pltpu.HBM)


def _gather_start(name, groups, lands, after=()):
    flat = [w for grp in groups for w in grp]
    nw, ng = len(flat), len(groups)

    def body(*refs):
        land = refs[:nw]
        sems = refs[nw + len(after):nw + len(after) + 2 * ng]
        token = refs[2 * nw + len(after) + 2 * ng]
        x, y, c = _mesh_pos()
        k_me = 2 * x + y
        chips, _ = _other_chips(x, y)
        i = 0
        for g, grp in enumerate(groups):
            for wi, w in enumerate(grp):
                mine = w.half(land[i], k_me, c)
                for j, chip in enumerate(chips):
                    _remote(mine, mine, sems[2 * g].at[3 * wi + j], sems[2 * g + 1].at[3 * wi + j], (*chip, c)).start()
                i += 1
        token[...] = jnp.zeros_like(token)

    sem_shapes = []
    for grp in groups:
        sem_shapes += [pltpu.SemaphoreType.DMA((3 * len(grp),))] * 2
    out = pl.pallas_call(
        body, name=name, in_specs=[HBM] * nw + [ANY] * len(after),
        out_specs=[SEM] * (2 * ng) + [HBM] * nw + [VMEM],
        out_shape=sem_shapes + [pltpu.HBM(a.shape, a.dtype) for a in lands] + [jax.ShapeDtypeStruct((8, LANES), F32)],
        input_output_aliases={i: 2 * ng + i for i in range(nw)},
        compiler_params=pltpu.CompilerParams(has_side_effects=EFFECT),
    )(*[_hbm(a) for a in lands], *after)
    sems = [(out[2 * g], out[2 * g + 1]) for g in range(ng)]
    return sems, list(out[2 * ng:2 * ng + nw]), out[-1]


def _gather_wait(name, grp, lands, ssem, rsem, after):
    n = len(grp)

    def body(*refs):
        land, ssem_ref, rsem_ref = refs[:n], refs[n], refs[n + 1]
        x, y, c = _mesh_pos()
        k_me = 2 * x + y
        chips, ks = _other_chips(x, y)
        for wi, w in enumerate(grp):
            for j, chip in enumerate(chips):
                cp = _remote(w.half(land[wi], k_me, c), w.half(land[wi], ks[j], c), ssem_ref.at[3 * wi + j],
                             rsem_ref.at[3 * wi + j], (*chip, c))
                cp.wait_send()
                cp.wait_recv()

    return pl.pallas_call(
        body, name=name, in_specs=[HBM] * n + [SEM, SEM, ANY], out_specs=[HBM] * n,
        out_shape=[pltpu.HBM(a.shape, a.dtype) for a in lands], input_output_aliases={i: i for i in range(n)},
        compiler_params=pltpu.CompilerParams(has_side_effects=EFFECT),
    )(*lands, ssem, rsem, after)


def _split_start(name, arrays, n, copies, after=()):
    na = len(arrays)

    def body(*refs):
        ssem, rsem, token = refs[na + len(after):][0], refs[na + len(after):][1], refs[2 * na + len(after) + 2]
        for i, (src, dst, dev, _) in enumerate(copies(refs[:na], *_mesh_pos())):
            _remote(src, dst, ssem.at[i], rsem.at[i], dev).start()
        token[...] = jnp.zeros_like(token)

    out = pl.pallas_call(
        body, name=name, in_specs=[HBM] * na + [ANY] * len(after), out_specs=[SEM, SEM] + [HBM] * na + [VMEM],
        out_shape=[pltpu.SemaphoreType.DMA((n,))] * 2 + [pltpu.HBM(a.shape, a.dtype) for a in arrays]
        + [jax.ShapeDtypeStruct((8, LANES), F32)],
        input_output_aliases={i: 2 + i for i in range(na)},
        compiler_params=pltpu.CompilerParams(has_side_effects=EFFECT),
    )(*[_hbm(a) for a in arrays], *after)
    return out[0], out[1], list(out[2:2 + na]), out[-1]


def _split_wait(name, arrays, ssem, rsem, copies, after):
    na = len(arrays)

    def body(*refs):
        for i, (src, _, dev, dst) in enumerate(copies(refs[:na], *_mesh_pos())):
            cp = _remote(src, dst, refs[na].at[i], refs[na + 1].at[i], dev)
            cp.wait_send()
            cp.wait_recv()

    return list(pl.pallas_call(
        body, name=name, in_specs=[HBM] * na + [SEM, SEM] + [ANY] * len(after), out_specs=[HBM] * na,
        out_shape=[pltpu.HBM(a.shape, a.dtype) for a in arrays], input_output_aliases={i: i for i in range(na)},
        compiler_params=pltpu.CompilerParams(has_side_effects=EFFECT),
    )(*arrays, ssem, rsem, *after))


def _pass_copies(grp):
    def copies(land, x, y, c):
        _, ks = _other_chips(x, y)
        return [(w.half(land[wi], ks[j], c), w.half(land[wi], ks[j], c), (x, y, 1 - c), w.half(land[wi], ks[j], 1 - c))
                for wi, w in enumerate(grp) for j in range(3)]
    return copies


def _pair_copies(n):
    def copies(refs, x, y, c):
        return [(refs[i].at[:, 1 - c], refs[n + i], (x, y, 1 - c), refs[n + i]) for i in range(n)]
    return copies


def _gather_conv_w(cw):
    ncw = cw.shape[1]

    def body(cw_ref, out_ref, ssem, rsem):
        x, y, c = _mesh_pos()
        k_me = 2 * x + y
        chips, ks = _other_chips(x, y)
        cols = lambda k: out_ref.at[:, pl.ds(pl.multiple_of(k * ncw, LANES), ncw)]
        cps = [_remote(cw_ref, cols(k_me), ssem.at[j], rsem.at[j], (*chip, c)) for j, chip in enumerate(chips)]
        for cp in cps:
            cp.start()
        for k in range(N_CHIPS):
            @pl.when(k_me == k)
            def _():
                out_ref[:, k * ncw:(k + 1) * ncw] = cw_ref[...]
        for j in range(3):
            _remote(cw_ref, cols(ks[j]), ssem.at[j], rsem.at[j], (*chips[j], c)).wait_recv()
        for cp in cps:
            cp.wait_send()

    return pl.pallas_call(
        body, name="gather_conv_w", in_specs=[VMEM], out_specs=VMEM,
        out_shape=jax.ShapeDtypeStruct((3, N_CHIPS * ncw), F32),
        scratch_shapes=[pltpu.SemaphoreType.DMA((3,)), pltpu.SemaphoreType.DMA((3,))],
    )(cw)


def _grad_tiles(w, n):
    return _tile8(w.R, 512) if w.R <= 512 else w.R // 2, _tile(n, 2048)


def _pair_sum(name, w, pos, grad, got):
    tr, tn = _grad_tiles(w, w.N)

    def body(pos_ref, g_ref, r_ref, o_ref):
        o_ref[...] = (g_ref[...].astype(F32) + r_ref[...].astype(F32)).astype(BF16)

    blk = pl.BlockSpec((None, tr, tn), lambda p, i, j, pos: (p, i, j))
    grid_spec = pltpu.PrefetchScalarGridSpec(
        num_scalar_prefetch=1, grid=(w.P, w.R // tr, w.N // tn),
        in_specs=[pl.BlockSpec((None, None, tr, tn), lambda p, i, j, pos: (p, pos[0], i, j)), blk], out_specs=blk)
    return pl.pallas_call(body, name=name, grid_spec=grid_spec, out_shape=jax.ShapeDtypeStruct((w.P, w.R, w.N), BF16),
                          compiler_params=_cp(("parallel",) * 3))(pos, grad, got)


def _scatter_start(name, ws, pairs):
    nw = len(ws)

    def body(*refs):
        pr, land = refs[:nw], refs[nw:2 * nw]
        ssem, rsem = refs[2 * nw], refs[2 * nw + 1]
        token = refs[4 * nw + 2]
        x, y, c = _mesh_pos()
        chips, ks = _other_chips(x, y)
        for i, w in enumerate(ws):
            for j, chip in enumerate(chips):
                _remote(w.part(pr[i], ks[j]), land[i].at[j], ssem.at[3 * i + j], rsem.at[3 * i + j], (*chip, c)).start()
        token[...] = jnp.zeros_like(token)

    lands = [lax.empty((3, w.R, w.nn), BF16) for w in ws]
    out = pl.pallas_call(
        body, name=name, in_specs=[HBM] * (2 * nw),
        out_specs=[SEM, SEM] + [HBM] * (2 * nw) + [VMEM],
        out_shape=[pltpu.SemaphoreType.DMA((3 * nw,))] * 2 + [pltpu.HBM(a.shape, a.dtype) for a in pairs + lands]
        + [jax.ShapeDtypeStruct((8, LANES), F32)],
        input_output_aliases={i: 2 + i for i in range(2 * nw)},
        compiler_params=pltpu.CompilerParams(has_side_effects=EFFECT),
    )(*[_hbm(a) for a in pairs + lands])
    return out[0], out[1], list(out[2:2 + nw]), list(out[2 + nw:2 + 2 * nw]), out[-1]


def _scatter_wait(name, ws, pairs, lands, ssem, rsem, after):
    nw = len(ws)

    def body(*refs):
        pr, land = refs[:nw], refs[nw:2 * nw]
        ssem_ref, rsem_ref = refs[2 * nw], refs[2 * nw + 1]
        x, y, c = _mesh_pos()
        chips, ks = _other_chips(x, y)
        for i, w in enumerate(ws):
            for j, chip in enumerate(chips):
                cp = _remote(w.part(pr[i], ks[j]), land[i].at[j], ssem_ref.at[3 * i + j], rsem_ref.at[3 * i + j], (*chip, c))
                cp.wait_send()
                cp.wait_recv()

    out = pl.pallas_call(
        body, name=name, in_specs=[HBM] * (2 * nw) + [SEM, SEM] + [ANY] * len(after), out_specs=[HBM] * (2 * nw),
        out_shape=[pltpu.HBM(a.shape, a.dtype) for a in pairs + lands],
        input_output_aliases={i: i for i in range(2 * nw)},
        compiler_params=pltpu.CompilerParams(has_side_effects=EFFECT),
    )(*pairs, *lands, ssem, rsem, *after)
    return list(out[nw:])


def _final_sum(name, w, pos, grad, got, parts):
    tr, tn = _grad_tiles(w, w.nn)
    nbc = w.nn // tn

    def body(pos_ref, g_ref, r_ref, p_ref, o_ref):
        acc = g_ref[...].astype(F32) + r_ref[...].astype(F32)
        for j in range(3):
            acc = acc + p_ref[j].astype(F32)
        o_ref[...] = acc

    if w.colshard:
        g_spec = pl.BlockSpec((None, None, tr, tn), lambda i, j, pos: (0, pos[0], i, pos[1] * nbc + j))
        r_spec = pl.BlockSpec((None, tr, tn), lambda i, j, pos: (0, i, pos[1] * nbc + j))
    else:
        g_spec = pl.BlockSpec((None, None, tr, tn), lambda i, j, pos: (pos[1], pos[0], i, j))
        r_spec = pl.BlockSpec((None, tr, tn), lambda i, j, pos: (pos[1], i, j))
    grid_spec = pltpu.PrefetchScalarGridSpec(
        num_scalar_prefetch=1, grid=(w.R // tr, nbc),
        in_specs=[g_spec, r_spec, pl.BlockSpec((3, tr, tn), lambda i, j, pos: (0, i, j))],
        out_specs=pl.BlockSpec((None, tr, tn), lambda i, j, pos: (pos[0], i, j)))
    return pl.pallas_call(body, name=name, grid_spec=grid_spec, out_shape=jax.ShapeDtypeStruct((2, w.R, w.nn), F32),
                          compiler_params=_cp(("parallel",) * 2))(pos, grad, got, parts)


def _share_halves(name, ws, halves, deps=()):
    nw = len(ws)

    def body(*refs):
        out = refs[nw + len(deps):2 * nw + len(deps)]
        ssem, rsem = refs[2 * nw + len(deps):]
        x, y, c = _mesh_pos()
        sib = (x, y, 1 - c)
        cps = [_remote(out[i].at[c], out[i].at[c], ssem.at[i], rsem.at[i], sib) for i in range(nw)]
        for cp in cps:
            cp.start()
        for i, cp in enumerate(cps):
            cp.wait_send()
            _remote(out[i].at[1 - c], out[i].at[1 - c], ssem.at[i], rsem.at[i], sib).wait_recv()

    return pl.pallas_call(
        body, name=name, in_specs=[ANY] * (nw + len(deps)), out_specs=[ANY] * nw,
        out_shape=[jax.ShapeDtypeStruct(h.shape, F32) for h in halves],
        scratch_shapes=[pltpu.SemaphoreType.DMA((nw,)), pltpu.SemaphoreType.DMA((nw,))],
        input_output_aliases={i: i for i in range(nw)},
    )(*halves, *deps)


VEC_ROWS = 16


def _vector_step(d, n_conv, parts, params):
    ncw = params[2][0].shape[1]
    n_par = len(params)

    def body(*refs):
        dg1, dba, dbb, dcw, dcb, dps, dg2, dgf, lc = refs[:9]
        wmv = refs[9:9 + 3 * n_par]
        outs = refs[9 + 3 * n_par:9 + 7 * n_par]
        loss_ref = refs[9 + 7 * n_par]
        snd, got, ssem, rsem = refs[9 + 7 * n_par + 1:]
        x, y, c = _mesh_pos()
        me = 4 * x + 2 * y + c
        snd[...] = jnp.zeros_like(snd)
        for row, ref in ((0, dg1), (1, dba), (2, dbb), (3, dps), (4, dg2), (5, dgf), (6, lc)):
            snd[row:row + 1, :] = ref[...]
        snd[7:8, :n_conv] = dcb[...]
        snd[8:11, :n_conv] = dcw[...]
        cps = []
        for r in range(1, N_DEV):
            peer = tuple(1 - p if (r >> b) & 1 else p for p, b in ((x, 2), (y, 1), (c, 0)))
            cps.append(_remote(snd, got.at[me], ssem.at[r - 1], rsem.at[r - 1], peer))
        for cp in cps:
            cp.start()
        got[me] = snd[...]
        for r in range(1, N_DEV):
            peer = tuple(1 - p if (r >> b) & 1 else p for p, b in ((x, 2), (y, 1), (c, 0)))
            _remote(snd, got.at[4 * peer[0] + 2 * peer[1] + peer[2]], ssem.at[r - 1], rsem.at[r - 1], peer).wait_recv()
        for cp in cps:
            cp.wait_send()
        tot = got[0]
        for dev in range(1, N_DEV):
            tot = tot + got[dev]
        loss_ref[...] = jnp.sum(tot[6:7, :], axis=1, keepdims=True)
        k_me = 2 * x + y
        g_cw = jnp.zeros((3, ncw), F32)
        for k in range(N_CHIPS):
            g_cw = g_cw + jnp.where(k_me == k, tot[8:11, k * ncw:(k + 1) * ncw], 0.0)
        grads = [tot[0:1, :], jnp.concatenate([tot[1:2, :], tot[2:3, :]], axis=1), g_cw, tot[7:8, :n_conv],
                 tot[3:4, :], tot[4:5, :], tot[5:6, :]]
        for i, g in enumerate(grads):
            w_ref, m_ref, v_ref = wmv[3 * i:3 * i + 3]
            delta, nm, nv = _adamw_math(w_ref[...], g, m_ref[...], v_ref[...])
            outs[4 * i][...] = g
            outs[4 * i + 1][...] = delta
            outs[4 * i + 2][...] = nm
            outs[4 * i + 3][...] = nv

    args = list(parts)
    out_shape = []
    for w, m, v in params:
        args += [w, m, v]
        out_shape += [jax.ShapeDtypeStruct(w.shape, F32)] * 4
    out_shape.append(jax.ShapeDtypeStruct((1, 1), F32))
    return pl.pallas_call(
        body, name="vector_params_step", in_specs=[VMEM] * len(args), out_specs=[VMEM] * len(out_shape),
        out_shape=out_shape,
        scratch_shapes=[pltpu.VMEM((VEC_ROWS, d), F32), pltpu.VMEM((N_DEV, VEC_ROWS, d), F32),
                        pltpu.SemaphoreType.DMA((N_DEV - 1,)), pltpu.SemaphoreType.DMA((N_DEV - 1,))],
        compiler_params=pltpu.CompilerParams(vmem_limit_bytes=VMEM_LIMIT),
    )(*args)


def kernel(x, norm1_g, w_in, b_gate, conv_w, conv_b, w_a_out, w_pool, pool_scale, w_o, norm2_g, w_ffn_gate, w_ffn_up, w_ffn_down, final_g, loss_target, m_norm1_g, m_w_in, m_b_gate, m_conv_w, m_conv_b, m_w_a_out, m_w_pool, m_pool_scale, m_w_o, m_norm2_g, m_w_ffn_gate, m_w_ffn_up, m_w_ffn_down, m_final_g, v_norm1_g, v_w_in, v_b_gate, v_conv_w, v_conv_b, v_w_a_out, v_w_pool, v_pool_scale, v_w_o, v_norm2_g, v_w_ffn_gate, v_w_ffn_up, v_w_ffn_down, v_final_g):
    t, d = x.shape[1], x.shape[2]
    n_conv = conv_b.shape[1]
    n_groups, pool_cg, pool_dg = w_pool.shape[1], w_pool.shape[2], N_CHIPS * w_pool.shape[3]
    d_ff = N_CHIPS * w_ffn_gate.shape[2]
    assert n_conv // n_groups == pool_cg and n_conv % (n_groups * MIX_COLS) == 0 and n_groups == len(POOL_WINDOWS)

    big = {"w_in": (w_in, m_w_in, v_w_in), "w_a_out": (w_a_out, m_w_a_out, v_w_a_out), "w_pool": (w_pool, m_w_pool, v_w_pool),
           "w_o": (w_o, m_w_o, v_w_o), "w_ffn_gate": (w_ffn_gate, m_w_ffn_gate, v_w_ffn_gate),
           "w_ffn_up": (w_ffn_up, m_w_ffn_up, v_w_ffn_up), "w_ffn_down": (w_ffn_down, m_w_ffn_down, v_w_ffn_down)}
    colshard = {"w_in": True, "w_a_out": True, "w_pool": True, "w_o": False, "w_ffn_gate": True, "w_ffn_up": True,
                "w_ffn_down": False}
    names = list(big)
    shard2d = {n: big[n][0].reshape(-1, big[n][0].shape[-1]) for n in names}
    ws = [_Weight(n, *shard2d[n].shape, colshard[n]) for n in names]

    xs, tgt = x[0], loss_target[0]
    cw_loc = conv_w[0]
    pos = jnp.stack([lax.axis_index("c"), 2 * lax.axis_index("x") + lax.axis_index("y")]).astype(jnp.int32)
    by_name = {w.name: w for w in ws}
    groups = [[by_name[n] for n in g] for g in (["w_in"], ["w_a_out", "w_pool", "w_o"], ["w_ffn_gate"], ["w_ffn_up"],
                                                 ["w_ffn_down"])]
    first = [sum(len(g) for g in groups[:i]) for i in range(len(groups))]
    rgroups = [groups[0], groups[1], groups[2] + groups[3], groups[4]]

    cw_full = _gather_conv_w(cw_loc)
    cast = lambda w, dep: _cast_place(f"cast_{w.name}", w, pos, shard2d[w.name].reshape(2, w.R, w.nn), deps=[dep])
    sems_a, lands_a, tok_a = _gather_start("gather_start_a", groups[:1], [cast(w, cw_full) for w in groups[0]])
    rest = [cast(w, tok_a) for grp in groups[1:] for w in grp]
    full = {}

    def landed(g, after):
        return _gather_wait(f"gather_wait_{g}", groups[g], lands[first[g]:first[g] + len(groups[g])], *gsems[g], after)

    def pass_start(g, got, after=()):
        return _split_start(f"pass_start_{g}", got, 3 * len(got), _pass_copies(groups[g]), after)

    def pass_wait(g, started, after):
        ssem, rsem, got, _ = started
        got = _split_wait(f"pass_wait_{g}", got, ssem, rsem, _pass_copies(groups[g]), after)
        full.update({w.name: a.reshape(w.P * 2 * w.R, w.N) for w, a in zip(groups[g], got)})

    got = _gather_wait("gather_wait_0", groups[0], lands_a, *sems_a[0], rest[-1])
    sems_b, lands_b, tok_b = _gather_start("gather_start_b", groups[1:3], rest[:first[3] - 1], after=got)
    gsems, lands = sems_a + sems_b, lands_a + lands_b
    st = pass_start(0, got, after=[tok_b])
    h1 = _rms_fwd("norm1_fwd", xs, norm1_g, deps=[st[3]])
    pass_wait(0, st, [h1])
    w_in_full = full["w_in"]
    proj = _mm_nn("proj_in", h1, w_in_full, BF16)
    st = pass_start(1, landed(1, proj))
    z, p = _mixer_fwd("mixer_fwd", proj, cw_full, conv_b, n_conv, n_groups, deps=[st[3]])
    pass_wait(1, st, [z])
    wp_full = full["w_pool"].reshape(n_groups, pool_cg, pool_dg)
    ya = _mm_nn("conv_out", z, full["w_a_out"], BF16)
    yb = _gmm_nn("pool_out", p, wp_full, BF16)
    merged = _merge_fwd("merge_fwd", proj, b_gate, ya, yb, pool_scale)
    x2 = _mm_nn("mix_out", merged, full["w_o"], F32, add=xs)
    got_g = landed(2, x2)
    sems_c, lands_c, tok_c = _gather_start("gather_start_c", groups[3:], rest[first[3] - 1:], after=got_g)
    gsems, lands = gsems + sems_c, lands + lands_c
    st_g = pass_start(2, got_g, after=[tok_c])
    h2 = _rms_fwd("norm2_fwd", x2, norm2_g, deps=[st_g[3]])
    pass_wait(2, st_g, [h2])
    gate = _mm_nn("ffn_gate", h2, full["w_ffn_gate"], BF16)
    st_u = pass_start(3, landed(3, gate))
    pass_wait(3, st_u, [st_u[3]])
    up = _mm_nn("ffn_up", h2, full["w_ffn_up"], BF16)
    st_d = pass_start(4, landed(4, up))
    act = _ffn_act("ffn_act", gate, up, deps=[st_d[3]])
    pass_wait(4, st_d, [act])
    x3 = _mm_nn("ffn_down", act, full["w_ffn_down"], F32, add=x2, tk=d_ff // 4)

    pending = {}

    def pair_start(g, grads):
        grp = rgroups[g]
        gcan = [grads[w.name].reshape(w.P, 2, w.R, w.N) for w in grp]
        slots = [lax.empty((w.P, w.R, w.N), BF16) for w in grp]
        pending[g] = _split_start(f"pair_start_{g}", gcan + slots, len(grp), _pair_copies(len(grp)))
        return pending[g][3]

    def scatter_start(g, after):
        grp = rgroups[g]
        n = len(grp)
        ssem, rsem, arrs, _ = pending[g]
        arrs = _split_wait(f"pair_wait_{g}", arrs, ssem, rsem, _pair_copies(n), after)
        gcan, sib = arrs[:n], arrs[n:]
        pairs = [_pair_sum(f"pair_sum_{w.name}", w, pos, a, s) for w, a, s in zip(grp, gcan, sib)]
        ssem, rsem, pairs, slots, token = _scatter_start(f"scatter_start_{g}", grp, pairs)
        pending[g] = (gcan, sib, pairs, slots, ssem, rsem)
        return token

    def reduce_finish(g, after):
        grp = rgroups[g]
        gcan, sib, pairs, slots, ssem, rsem = pending[g]
        parts = _scatter_wait(f"scatter_wait_{g}", grp, pairs, slots, ssem, rsem, after)
        return [_final_sum(f"final_sum_{w.name}", w, pos, a, s, q) for w, a, s, q in zip(grp, gcan, sib, parts)]

    grads = {}
    dx3, dx3b, d_gf, loss_cols = _final_bwd("final_bwd", x3, final_g.reshape(1, d), tgt)
    dgate, dup = _ffn_bwd("ffn_bwd", dx3b, full["w_ffn_down"], gate, up)
    grads["w_ffn_down"] = _mm_tn("dw_ffn_down", act, dx3b, BF16)
    tok = pair_start(3, grads)
    dh2 = _mm_nt("d_h2", [(dgate, full["w_ffn_gate"]), (dup, full["w_ffn_up"])], F32, tk=d_ff // 4, deps=[tok])
    tok = scatter_start(3, [dh2])
    grads["w_ffn_gate"] = _mm_tn("dw_ffn_gate", h2, dgate, BF16, deps=[tok])
    grads["w_ffn_up"] = _mm_tn("dw_ffn_up", h2, dup, BF16)
    tok = pair_start(2, grads)
    dx2, dx2b, d_g2 = _rms_bwd("norm2_bwd", x2, norm2_g, dh2, dx3, True, deps=[tok])
    dmerged = _mm_nt("d_merged", [(dx2b, full["w_o"])], BF16, tk=d)
    grads["w_o"] = _mm_tn("dw_o", merged, dx2b, BF16)
    tok = scatter_start(2, [grads["w_o"]])
    dya, dyb, dproj, d_bga, d_bgb, d_ps = _merge_bwd("merge_bwd", dmerged, proj, b_gate, ya, yb, pool_scale, deps=[tok])
    dz = _mm_nt("d_z", [(dya, full["w_a_out"])], BF16, tk=d)
    grads["w_a_out"] = _mm_tn("dw_a_out", z, dya, BF16)
    dp = _gmm_nt("d_pool", dyb, wp_full, BF16)
    grads["w_pool"] = _gmm_tn("dw_pool", p, dyb, n_groups, BF16)
    tok = pair_start(1, grads)
    dproj, d_cw, d_cb = _mixer_bwd("mixer_bwd", dz, dp, proj, cw_full, conv_b, dproj, n_conv, n_groups, deps=[tok])
    tok = scatter_start(1, [dproj])
    dh1 = _mm_nt("d_h1", [(dproj, w_in_full)], F32, tk=proj.shape[1] // 4, deps=[tok])
    grad_x, d_g1 = _rms_bwd("norm1_bwd", xs, norm1_g, dh1, dx2, False)

    vec_names = ["norm1_g", "b_gate", "conv_w", "conv_b", "pool_scale", "norm2_g", "final_g"]
    vec = {"norm1_g": (norm1_g, m_norm1_g, v_norm1_g), "b_gate": (b_gate, m_b_gate, v_b_gate),
           "conv_w": (cw_loc, m_conv_w[0], v_conv_w[0]), "conv_b": (conv_b, m_conv_b, v_conv_b),
           "pool_scale": (pool_scale, m_pool_scale, v_pool_scale), "norm2_g": (norm2_g, m_norm2_g, v_norm2_g),
           "final_g": tuple(a.reshape(1, d) for a in (final_g, m_final_g, v_final_g))}
    vout = _vector_step(d, n_conv, [d_g1, d_bga, d_bgb, d_cw, d_cb, d_ps, d_g2, d_gf, loss_cols],
                        [vec[n] for n in vec_names])

    grads["w_in"] = _mm_tn("dw_in", h1, dproj, BF16, deps=[vout[-1]])
    tok = pair_start(0, grads)

    g_big, d_big, m_big, v_big = {}, {}, {}, {}

    def update(name, wsub, halves, deps=()):
        out = []
        for w, g in zip(wsub, _share_halves(name, wsub, halves, deps)):
            wt, mt, vt = big[w.name]
            g2 = g.reshape(2 * w.R, w.nn)
            go, dl, nm, nv = _adamw(f"adamw_{w.name}", shard2d[w.name], g2, mt.reshape(g2.shape), vt.reshape(g2.shape))
            g_big[w.name], d_big[w.name], m_big[w.name], v_big[w.name] = (a.reshape(wt.shape) for a in (go, dl, nm, nv))
            out.append(nv)
        return out

    after, early, early_halves = [tok], [], []
    for g in (3, 2, 1):
        halves = reduce_finish(g, after)
        early += rgroups[g]
        early_halves += halves
        after = halves[-1:]
    tok = scatter_start(0, after)
    after = update("share_halves_early", early, early_halves, deps=[tok])
    update("share_halves_w_in", rgroups[0], reduce_finish(0, after))

    shapes = {"conv_w": conv_w.shape, "final_g": final_g.shape}
    g_vec, d_vec, m_vec, v_vec = ({n: vout[4 * i + q].reshape(shapes.get(n, vec[n][0].shape)) for i, n in enumerate(vec_names)}
                                  for q in range(4))
    loss = vout[-1].reshape(())

    order = ["norm1_g", "w_in", "b_gate", "conv_w", "conv_b", "w_a_out", "w_pool", "pool_scale", "w_o", "norm2_g",
             "w_ffn_gate", "w_ffn_up", "w_ffn_down", "final_g"]
    pick = lambda vecs, bigs: [vecs[n] if n in vecs else bigs[n] for n in order]
    return (loss, grad_x.reshape(x.shape), *pick(g_vec, g_big), *pick(d_vec, d_big), *pick(m_vec, m_big),
            *pick(v_vec, v_big))
```

```python
import functools

import jax
import jax.numpy as jnp
from jax import lax
from jax.experimental import pallas as pl
from jax.experimental.pallas import tpu as pltpu

F32, BF16 = jnp.float32, jnp.bfloat16
MESH = pl.DeviceIdType.MESH
ANY = pl.BlockSpec(memory_space=pl.ANY)
VMEM = pl.BlockSpec(memory_space=pltpu.VMEM)
HBM = pl.BlockSpec(memory_space=pltpu.HBM)
SEM = pl.BlockSpec(memory_space=pltpu.SEMAPHORE)
EFFECT = pltpu.SideEffectType.DATAFLOW_SIDE_EFFECTING

EPS = 1e-6
POOL_WINDOWS = (2, 4, 8, 16)
ADAM_LR, ADAM_B1, ADAM_B2, ADAM_EPS, ADAM_WD, ADAM_STEP = 0.001, 0.9, 0.999, 1e-08, 0.01, 10

V7X_VMEM_BYTES = 64 * 1024 * 1024
VMEM_LIMIT = V7X_VMEM_BYTES * 3 // 4
LANES = 128
N_CHIPS = 4
N_DEV = 8

_DIMS = {
    "nn": (((1,), (0,)), ((), ())),
    "nt": (((1,), (1,)), ((), ())),
    "tn": (((0,), (0,)), ((), ())),
}


def _cp(sem):
    return pltpu.CompilerParams(dimension_semantics=sem, vmem_limit_bytes=VMEM_LIMIT)


def _mesh_pos():
    return lax.axis_index("x"), lax.axis_index("y"), lax.axis_index("c")


def _mm(name, pairs, *, mode, grid, out_shape, o_spec, nk=1, kaxis=None, add=None, deps=()):
    npair = len(pairs)
    has_add = add is not None

    def body(*refs):
        ab = refs[: 2 * npair]
        pos = 2 * npair
        add_ref = refs[pos] if has_add else None
        pos += int(has_add) + len(deps)
        o_ref = refs[pos]
        acc_ref = refs[pos + 1] if nk > 1 else None
        d = None
        for p in range(npair):
            t = lax.dot_general(ab[2 * p][...], ab[2 * p + 1][...], _DIMS[mode], preferred_element_type=F32)
            d = t if d is None else d + t
        if nk == 1:
            if has_add:
                d = d + add_ref[...].astype(F32)
            o_ref[...] = d.astype(o_ref.dtype)
        else:
            k = pl.program_id(kaxis)

            @pl.when(k == 0)
            def _():
                acc_ref[...] = d

            @pl.when(k > 0)
            def _():
                acc_ref[...] += d

            @pl.when(k == nk - 1)
            def _():
                r = acc_ref[...]
                if has_add:
                    r = r + add_ref[...].astype(F32)
                o_ref[...] = r.astype(o_ref.dtype)

    args, specs = [], []
    for a, a_spec, b, b_spec in pairs:
        args += [a, b]
        specs += [a_spec, b_spec]
    if has_add:
        args.append(add[0])
        specs.append(add[1])
    args += list(deps)
    specs += [ANY] * len(deps)
    scratch = []
    if nk > 1:
        blk = [d for d in o_spec.block_shape if d is not None]
        scratch = [pltpu.VMEM(tuple(blk), F32)]
    sem = tuple("arbitrary" if (nk > 1 and ax == kaxis) else "parallel" for ax in range(len(grid)))
    return pl.pallas_call(
        body, name=name, grid=grid, in_specs=specs, out_specs=o_spec, out_shape=out_shape,
        scratch_shapes=scratch, compiler_params=_cp(sem),
    )(*args)


def _tile(n, pref):
    if n <= pref:
        return n
    for t in range(pref, 0, -LANES):
        if t % LANES == 0 and n % t == 0:
            return t
    raise ValueError(f"no tile for {n}")


def _mm_nn(name, a, b, out_dtype, add=None, tk=None, deps=()):
    m, kk = a.shape
    n = b.shape[1]
    tm, tn = _tile(m, 1024), _tile(n, 512)
    out_shape = jax.ShapeDtypeStruct((m, n), out_dtype)
    if tk is None or tk == kk:
        grid = (m // tm, n // tn)
        pairs = [(a, pl.BlockSpec((tm, kk), lambda i, j: (i, 0)), b, pl.BlockSpec((kk, tn), lambda i, j: (0, j)))]
        o_spec = pl.BlockSpec((tm, tn), lambda i, j: (i, j))
        add_ = None if add is None else (add, pl.BlockSpec((tm, tn), lambda i, j: (i, j)))
        return _mm(name, pairs, mode="nn", grid=grid, out_shape=out_shape, o_spec=o_spec, add=add_, deps=deps)
    tn = _tile(n, 1024)
    nk = kk // tk
    grid = (m // tm, n // tn, nk)
    pairs = [(a, pl.BlockSpec((tm, tk), lambda i, j, k: (i, k)), b, pl.BlockSpec((tk, tn), lambda i, j, k: (k, j)))]
    o_spec = pl.BlockSpec((tm, tn), lambda i, j, k: (i, j))
    add_ = None if add is None else (add, pl.BlockSpec((tm, tn), lambda i, j, k: (i, j)))
    return _mm(name, pairs, mode="nn", grid=grid, out_shape=out_shape, o_spec=o_spec, nk=nk, kaxis=2, add=add_, deps=deps)


def _mm_nt(name, abs_, out_dtype, tk, deps=()):
    m, kk = abs_[0][0].shape
    n = abs_[0][1].shape[0]
    tm = _tile(m, 1024)
    nk = kk // tk
    tn = _tile(n, 512 if nk == 1 else 1024)
    out_shape = jax.ShapeDtypeStruct((m, n), out_dtype)
    if nk == 1:
        grid = (m // tm, n // tn)
        pairs = [(a, pl.BlockSpec((tm, kk), lambda i, j: (i, 0)), b, pl.BlockSpec((tn, kk), lambda i, j: (j, 0)))
                 for a, b in abs_]
        o_spec = pl.BlockSpec((tm, tn), lambda i, j: (i, j))
        return _mm(name, pairs, mode="nt", grid=grid, out_shape=out_shape, o_spec=o_spec, deps=deps)
    grid = (m // tm, n // tn, nk)
    pairs = [(a, pl.BlockSpec((tm, tk), lambda i, j, k: (i, k)), b, pl.BlockSpec((tn, tk), lambda i, j, k: (j, k)))
             for a, b in abs_]
    o_spec = pl.BlockSpec((tm, tn), lambda i, j, k: (i, j))
    return _mm(name, pairs, mode="nt", grid=grid, out_shape=out_shape, o_spec=o_spec, nk=nk, kaxis=2, deps=deps)


def _mm_tn(name, a, b, out_dtype, deps=()):
    t, m = a.shape
    n = b.shape[1]
    tm, tn = _tile(m, 512), _tile(n, 2048)
    if n > m:
        grid = (n // tn, m // tm)
        a_map, b_map, o_map = (lambda j, i: (0, i)), (lambda j, i: (0, j)), (lambda j, i: (i, j))
    else:
        grid = (m // tm, n // tn)
        a_map, b_map, o_map = (lambda i, j: (0, i)), (lambda i, j: (0, j)), (lambda i, j: (i, j))
    pairs = [(a, pl.BlockSpec((t, tm), a_map), b, pl.BlockSpec((t, tn), b_map))]
    o_spec = pl.BlockSpec((tm, tn), o_map)
    return _mm(name, pairs, mode="tn", grid=grid, out_shape=jax.ShapeDtypeStruct((m, n), out_dtype), o_spec=o_spec,
               deps=deps)


def _gmm_nn(name, p, w, out_dtype):
    t = p.shape[0]
    g, cg, dg = w.shape
    tm = _tile(t, 1024)
    pairs = [(p, pl.BlockSpec((tm, cg), lambda i, j: (i, j)), w, pl.BlockSpec((None, cg, dg), lambda i, j: (j, 0, 0)))]
    o_spec = pl.BlockSpec((tm, dg), lambda i, j: (i, j))
    return _mm(name, pairs, mode="nn", grid=(t // tm, g), out_shape=jax.ShapeDtypeStruct((t, g * dg), out_dtype),
               o_spec=o_spec)


def _gmm_nt(name, dy, w, out_dtype):
    t = dy.shape[0]
    g, cg, dg = w.shape
    tm = _tile(t, 1024)
    pairs = [(dy, pl.BlockSpec((tm, dg), lambda i, j: (i, j)), w, pl.BlockSpec((None, cg, dg), lambda i, j: (j, 0, 0)))]
    o_spec = pl.BlockSpec((tm, cg), lambda i, j: (i, j))
    return _mm(name, pairs, mode="nt", grid=(t // tm, g), out_shape=jax.ShapeDtypeStruct((t, g * cg), out_dtype),
               o_spec=o_spec)


def _gmm_tn(name, p, dy, g, out_dtype):
    t = p.shape[0]
    cg, dg = p.shape[1] // g, dy.shape[1] // g
    pairs = [(p, pl.BlockSpec((t, cg), lambda j: (0, j)), dy, pl.BlockSpec((t, dg), lambda j: (0, j)))]
    o_spec = pl.BlockSpec((None, cg, dg), lambda j: (j, 0, 0))
    return _mm(name, pairs, mode="tn", grid=(g,), out_shape=jax.ShapeDtypeStruct((g, cg, dg), out_dtype), o_spec=o_spec)


ROW_TILE = 256


def _rows(t):
    return _tile8(t, ROW_TILE)


def _tile8(n, pref):
    if n <= pref:
        return n
    for t in range(pref, 0, -8):
        if n % t == 0:
            return t
    raise ValueError(f"no row tile for {n}")


def _cast_place(name, w, pos, shard, deps=()):
    tr = _tile8(w.R, 512)
    if w.colshard:
        o_map = lambda h, i, pos: (0, h, i, pos[1])
    else:
        o_map = lambda h, i, pos: (pos[1], h, i, 0)

    def body(pos_ref, w_ref, *rest):
        rest[-1][...] = w_ref[...].astype(BF16)

    grid_spec = pltpu.PrefetchScalarGridSpec(
        num_scalar_prefetch=1, grid=(2, w.R // tr),
        in_specs=[pl.BlockSpec((None, tr, w.nn), lambda h, i, pos: (h, i, 0))] + [ANY] * len(deps),
        out_specs=pl.BlockSpec((None, None, tr, w.nn), o_map))
    return pl.pallas_call(body, name=name, grid_spec=grid_spec, out_shape=jax.ShapeDtypeStruct((w.P, 2, w.R, w.N), BF16),
                          compiler_params=_cp(("parallel", "parallel")))(pos, shard, *deps)


def _rms_fwd(name, x, g, deps=()):
    t, d = x.shape
    tm = _rows(t)

    def body(x_ref, g_ref, *rest):
        xf = x_ref[...]
        r = lax.rsqrt(jnp.mean(xf * xf, axis=-1, keepdims=True) + EPS)
        rest[-1][...] = (xf * r * g_ref[...]).astype(BF16)

    return pl.pallas_call(
        body, name=name, grid=(t // tm,),
        in_specs=[pl.BlockSpec((tm, d), lambda i: (i, 0)), pl.BlockSpec((1, d), lambda i: (0, 0))] + [ANY] * len(deps),
        out_specs=pl.BlockSpec((tm, d), lambda i: (i, 0)), out_shape=jax.ShapeDtypeStruct((t, d), BF16),
        compiler_params=_cp(("parallel",)),
    )(x, g, *deps)


def _rms_bwd(name, x, g, dh, dres, want_bf16, deps=()):
    t, d = x.shape
    tm = _rows(t)

    def body(x_ref, g_ref, dh_ref, dres_ref, *rest):
        rest = rest[len(deps):]
        dx_ref, rest = rest[0], rest[1:]
        dg_ref = rest[-1]
        xf = x_ref[...]
        r = lax.rsqrt(jnp.mean(xf * xf, axis=-1, keepdims=True) + EPS)
        xh = xf * r
        dhf = dh_ref[...]
        dxh = dhf * g_ref[...]
        m = jnp.mean(dxh * xh, axis=-1, keepdims=True)
        dx = dres_ref[...] + r * (dxh - xh * m)
        dx_ref[...] = dx
        if want_bf16:
            rest[0][...] = dx.astype(BF16)

        @pl.when(pl.program_id(0) == 0)
        def _():
            dg_ref[...] = jnp.zeros_like(dg_ref)

        dg_ref[...] += jnp.sum(dhf * xh, axis=0, keepdims=True)

    row = pl.BlockSpec((tm, d), lambda i: (i, 0))
    vec = pl.BlockSpec((1, d), lambda i: (0, 0))
    out_specs = [row] + ([row] if want_bf16 else []) + [vec]
    out_shape = ([jax.ShapeDtypeStruct((t, d), F32)] + ([jax.ShapeDtypeStruct((t, d), BF16)] if want_bf16 else [])
                 + [jax.ShapeDtypeStruct((1, d), F32)])
    return pl.pallas_call(body, name=name, grid=(t // tm,), in_specs=[row, vec, row, row] + [ANY] * len(deps),
                          out_specs=out_specs, out_shape=out_shape, compiler_params=_cp(("arbitrary",)))(x, g, dh, dres, *deps)


def _final_bwd(name, x3, gf, tgt):
    t, d = x3.shape
    tm = _rows(t)

    def body(x_ref, g_ref, t_ref, dx_ref, dxb_ref, dg_ref, lc_ref):
        xf = x_ref[...]
        g = g_ref[...]
        r = lax.rsqrt(jnp.mean(xf * xf, axis=-1, keepdims=True) + EPS)
        xh = xf * r
        diff = xh * g - t_ref[...]
        dy = diff * (1.0 / d)
        dxh = dy * g
        m = jnp.mean(dxh * xh, axis=-1, keepdims=True)
        dx = r * (dxh - xh * m)
        dx_ref[...] = dx
        dxb_ref[...] = dx.astype(BF16)

        @pl.when(pl.program_id(0) == 0)
        def _():
            dg_ref[...] = jnp.zeros_like(dg_ref)
            lc_ref[...] = jnp.zeros_like(lc_ref)

        dg_ref[...] += jnp.sum(dy * xh, axis=0, keepdims=True)
        lc_ref[...] += jnp.sum(diff * diff, axis=0, keepdims=True) * (0.5 / d)

    row = pl.BlockSpec((tm, d), lambda i: (i, 0))
    vec = pl.BlockSpec((1, d), lambda i: (0, 0))
    return pl.pallas_call(
        body, name=name, grid=(t // tm,), in_specs=[row, vec, row], out_specs=[row, row, vec, vec],
        out_shape=[jax.ShapeDtypeStruct((t, d), F32), jax.ShapeDtypeStruct((t, d), BF16),
                   jax.ShapeDtypeStruct((1, d), F32), jax.ShapeDtypeStruct((1, d), F32)],
        compiler_params=_cp(("arbitrary",)),
    )(x3, gf, tgt)


def _shift_down(v, k, t_idx):
    return jnp.where(t_idx >= k, pltpu.roll(v, k, 0), 0.0)


def _shift_up(v, k, t_idx):
    n = v.shape[0]
    return jnp.where(t_idx < n - k, pltpu.roll(v, n - k, 0), 0.0)


def _window_sums(v, shift, t_idx, grp):
    s = v + shift(v, 1, t_idx)
    out = s
    for lvl in range(1, len(POOL_WINDOWS)):
        s = s + shift(s, 1 << lvl, t_idx)
        out = jnp.where(grp >= lvl, s, out)
    return out


def _window_count(t_idx, grp):
    return jnp.minimum(t_idx + 1, jnp.left_shift(2, grp)).astype(F32)


MIX_COLS = 128


def _mixer_fwd(name, proj, cw, cb, n_conv, n_groups, deps=()):
    t = proj.shape[0]
    nb = n_conv // MIX_COLS
    per_group = n_conv // n_groups // MIX_COLS

    def body(ba_ref, ca_ref, va_ref, vb_ref, cw_ref, cb_ref, *rest):
        z_ref, p_ref = rest[len(deps):]
        t_idx = lax.broadcasted_iota(jnp.int32, (t, MIX_COLS), 0)
        q = ca_ref[...].astype(F32) * va_ref[...].astype(F32)
        w = cw_ref[...]
        u = cb_ref[...] + w[0:1] * _shift_down(q, 2, t_idx) + w[1:2] * _shift_down(q, 1, t_idx) + w[2:3] * q
        z_ref[...] = (ba_ref[...].astype(F32) * u).astype(BF16)
        grp = pl.program_id(0) // per_group
        v = vb_ref[...].astype(F32)
        p_ref[...] = (_window_sums(v, _shift_down, t_idx, grp) / _window_count(t_idx, grp) - v).astype(BF16)

    col = lambda s: pl.BlockSpec((t, MIX_COLS), lambda j: (0, s * nb + j))
    return pl.pallas_call(
        body, name=name, grid=(nb,),
        in_specs=[col(0), col(1), col(2), col(3), pl.BlockSpec((3, MIX_COLS), lambda j: (0, j)),
                  pl.BlockSpec((1, MIX_COLS), lambda j: (0, j))] + [ANY] * len(deps),
        out_specs=[col(0), col(0)],
        out_shape=[jax.ShapeDtypeStruct((t, n_conv), BF16), jax.ShapeDtypeStruct((t, n_conv), BF16)],
        compiler_params=_cp(("parallel",)),
    )(proj, proj, proj, proj, cw, cb, *deps)


def _mixer_bwd(name, dz, dp, proj, cw, cb, dproj, n_conv, n_groups, deps=()):
    t = proj.shape[0]
    nb = n_conv // MIX_COLS
    per_group = n_conv // n_groups // MIX_COLS

    def body(dz_ref, dp_ref, ba_ref, ca_ref, va_ref, cw_ref, cb_ref, _, *rest):
        o_ref, dcw_ref, dcb_ref, scr = rest[len(deps):]
        s = pl.program_id(1)

        @pl.when(s == 0)
        def _():
            t_idx = lax.broadcasted_iota(jnp.int32, (t, MIX_COLS), 0)
            ca, va = ca_ref[...].astype(F32), va_ref[...].astype(F32)
            q = ca * va
            q1, q2 = _shift_down(q, 1, t_idx), _shift_down(q, 2, t_idx)
            w = cw_ref[...]
            u = cb_ref[...] + w[0:1] * q2 + w[1:2] * q1 + w[2:3] * q
            dzf = dz_ref[...].astype(F32)
            du = dzf * ba_ref[...].astype(F32)
            scr[0] = (dzf * u).astype(BF16)
            dq = w[2:3] * du + w[1:2] * _shift_up(du, 1, t_idx) + w[0:1] * _shift_up(du, 2, t_idx)
            scr[1] = (dq * va).astype(BF16)
            scr[2] = (dq * ca).astype(BF16)
            dcb_ref[...] = jnp.sum(du, axis=0, keepdims=True)
            dcw_ref[0:1, :] = jnp.sum(du * q2, axis=0, keepdims=True)
            dcw_ref[1:2, :] = jnp.sum(du * q1, axis=0, keepdims=True)
            dcw_ref[2:3, :] = jnp.sum(du * q, axis=0, keepdims=True)
            grp = pl.program_id(0) // per_group
            dpf = dp_ref[...].astype(F32)
            e = dpf / _window_count(t_idx, grp)
            scr[3] = (_window_sums(e, _shift_up, t_idx, grp) - dpf).astype(BF16)

        o_ref[...] = scr[s]

    col = lambda c: pl.BlockSpec((t, MIX_COLS), lambda j, s: (0, c * nb + j))
    own = pl.BlockSpec((t, MIX_COLS), lambda j, s: (0, j))
    return pl.pallas_call(
        body, name=name, grid=(nb, 4),
        in_specs=[own, own, col(0), col(1), col(2), pl.BlockSpec((3, MIX_COLS), lambda j, s: (0, j)),
                  pl.BlockSpec((1, MIX_COLS), lambda j, s: (0, j)), ANY] + [ANY] * len(deps),
        out_specs=[pl.BlockSpec((t, MIX_COLS), lambda j, s: (0, s * nb + j)),
                   pl.BlockSpec((3, MIX_COLS), lambda j, s: (0, j)), pl.BlockSpec((1, MIX_COLS), lambda j, s: (0, j))],
        out_shape=[jax.ShapeDtypeStruct(dproj.shape, BF16), jax.ShapeDtypeStruct((3, n_conv), F32),
                   jax.ShapeDtypeStruct((1, n_conv), F32)],
        scratch_shapes=[pltpu.VMEM((4, t, MIX_COLS), BF16)],
        input_output_aliases={7: 0},
        compiler_params=_cp(("arbitrary", "arbitrary")),
    )(dz, dp, proj, proj, proj, cw, cb, dproj, *deps)


def _merge_fwd(name, proj, bg, ya, yb, ps):
    t, d = ya.shape
    tm = _rows(t)

    def body(gab_ref, bg_ref, ya_ref, yb_ref, ps_ref, o_ref):
        gab = gab_ref[...].astype(F32) + bg_ref[...]
        sa, sb = jax.nn.sigmoid(gab[:, :d]), jax.nn.sigmoid(gab[:, d:])
        o_ref[...] = (sa * ya_ref[...].astype(F32) + sb * (yb_ref[...].astype(F32) * ps_ref[...])).astype(BF16)

    row = pl.BlockSpec((tm, d), lambda i: (i, 0))
    return pl.pallas_call(
        body, name=name, grid=(t // tm,),
        in_specs=[pl.BlockSpec((tm, 2 * d), lambda i: (i, 1)), pl.BlockSpec((1, 2 * d), lambda i: (0, 0)), row, row,
                  pl.BlockSpec((1, d), lambda i: (0, 0))],
        out_specs=row, out_shape=jax.ShapeDtypeStruct((t, d), BF16), compiler_params=_cp(("parallel",)),
    )(proj, bg, ya, yb, ps)


def _merge_bwd(name, dm, proj, bg, ya, yb, ps, deps=()):
    t, d = ya.shape
    tm = _rows(t)

    def body(dm_ref, gab_ref, bg_ref, ya_ref, yb_ref, ps_ref, *rest):
        dya_ref, dyb_ref, dg_ref, dba_ref, dbb_ref, dps_ref = rest[len(deps):]
        gab = gab_ref[...].astype(F32) + bg_ref[...]
        sa, sb = jax.nn.sigmoid(gab[:, :d]), jax.nn.sigmoid(gab[:, d:])
        dmf = dm_ref[...].astype(F32)
        ybf, ps_ = yb_ref[...].astype(F32), ps_ref[...]
        dya_ref[...] = (dmf * sa).astype(BF16)
        dyb = dmf * sb
        dyb_ref[...] = (dyb * ps_).astype(BF16)
        dga = dmf * ya_ref[...].astype(F32) * sa * (1.0 - sa)
        dgb = dmf * (ybf * ps_) * sb * (1.0 - sb)
        dg_ref[:, :d] = dga.astype(BF16)
        dg_ref[:, d:] = dgb.astype(BF16)

        @pl.when(pl.program_id(0) == 0)
        def _():
            dba_ref[...] = jnp.zeros_like(dba_ref)
            dbb_ref[...] = jnp.zeros_like(dbb_ref)
            dps_ref[...] = jnp.zeros_like(dps_ref)

        dba_ref[...] += jnp.sum(dga, axis=0, keepdims=True)
        dbb_ref[...] += jnp.sum(dgb, axis=0, keepdims=True)
        dps_ref[...] += jnp.sum(dyb * ybf, axis=0, keepdims=True)

    row = pl.BlockSpec((tm, d), lambda i: (i, 0))
    vec = pl.BlockSpec((1, d), lambda i: (0, 0))
    gates = pl.BlockSpec((tm, 2 * d), lambda i: (i, 1))
    return pl.pallas_call(
        body, name=name, grid=(t // tm,),
        in_specs=[row, gates, pl.BlockSpec((1, 2 * d), lambda i: (0, 0)), row, row, vec] + [ANY] * len(deps),
        out_specs=[row, row, gates, vec, vec, vec],
        out_shape=[jax.ShapeDtypeStruct((t, d), BF16), jax.ShapeDtypeStruct((t, d), BF16),
                   jax.ShapeDtypeStruct(proj.shape, BF16), jax.ShapeDtypeStruct((1, d), F32),
                   jax.ShapeDtypeStruct((1, d), F32), jax.ShapeDtypeStruct((1, d), F32)],
        compiler_params=_cp(("arbitrary",)),
    )(dm, proj, bg, ya, yb, ps, *deps)


def _ffn_up_act(name, h, w_up, gate):
    t, d = h.shape
    f = w_up.shape[1]
    tm, tf = _tile(t, 1024), _tile(f, 512)

    def body(h_ref, w_ref, g_ref, u_ref, a_ref):
        u = lax.dot_general(h_ref[...], w_ref[...], _DIMS["nn"], preferred_element_type=F32)
        g = g_ref[...].astype(F32)
        u_ref[...] = u.astype(BF16)
        a_ref[...] = (g * jax.nn.sigmoid(g) * u).astype(BF16)

    blk = pl.BlockSpec((tm, tf), lambda i, j: (i, j))
    shp = jax.ShapeDtypeStruct((t, f), BF16)
    return pl.pallas_call(
        body, name=name, grid=(t // tm, f // tf),
        in_specs=[pl.BlockSpec((tm, d), lambda i, j: (i, 0)), pl.BlockSpec((d, tf), lambda i, j: (0, j)), blk],
        out_specs=[blk, blk], out_shape=[shp, shp], compiler_params=_cp(("parallel", "parallel")))(h, w_up, gate)


def _ffn_bwd(name, dy, w_down, gate, up):
    t, d = dy.shape
    f = w_down.shape[0]
    tm, tf = _tile(t, 1024), _tile(f, 512)

    def body(dy_ref, w_ref, g_ref, u_ref, dg_ref, du_ref):
        da = lax.dot_general(dy_ref[...], w_ref[...], _DIMS["nt"], preferred_element_type=F32)
        g = g_ref[...].astype(F32)
        s = jax.nn.sigmoid(g)
        du_ref[...] = (da * (g * s)).astype(BF16)
        dg_ref[...] = (da * u_ref[...].astype(F32) * (s * (1.0 + g * (1.0 - s)))).astype(BF16)

    blk = pl.BlockSpec((tm, tf), lambda i, j: (i, j))
    shp = jax.ShapeDtypeStruct((t, f), BF16)
    return pl.pallas_call(
        body, name=name, grid=(t // tm, f // tf),
        in_specs=[pl.BlockSpec((tm, d), lambda i, j: (i, 0)), pl.BlockSpec((tf, d), lambda i, j: (j, 0)), blk, blk],
        out_specs=[blk, blk], out_shape=[shp, shp], compiler_params=_cp(("parallel", "parallel")))(dy, w_down, gate, up)


def _adamw_math(w, g, m, v):
    m = ADAM_B1 * m + (1.0 - ADAM_B1) * g
    v = ADAM_B2 * v + (1.0 - ADAM_B2) * (g * g)
    m_hat = m / (1.0 - ADAM_B1 ** ADAM_STEP)
    v_hat = v / (1.0 - ADAM_B2 ** ADAM_STEP)
    delta = -ADAM_LR * (m_hat / (jnp.sqrt(v_hat) + ADAM_EPS) + ADAM_WD * w)
    return delta, m, v


def _adamw(name, w, g, m, v):
    r, c = w.shape
    tr = _tile8(r, 512 if c <= 1024 else 256)

    def body(w_ref, g_ref, m_ref, v_ref, go_ref, d_ref, nm_ref, nv_ref):
        g = g_ref[...]
        go_ref[...] = g
        d_ref[...], nm_ref[...], nv_ref[...] = _adamw_math(w_ref[...], g, m_ref[...], v_ref[...])

    blk = pl.BlockSpec((tr, c), lambda i: (i, 0))
    shp = jax.ShapeDtypeStruct((r, c), F32)
    return pl.pallas_call(body, name=name, grid=(r // tr,), in_specs=[blk] * 4, out_specs=[blk] * 4,
                          out_shape=[shp] * 4, compiler_params=_cp(("parallel",)))(w, g, m, v)


class _Weight:
    def __init__(self, name, rows, cols, colshard):
        self.name, self.colshard = name, colshard
        self.R, self.nn = rows // 2, cols
        self.P = 1 if colshard else N_CHIPS
        self.N = N_CHIPS * cols if colshard else cols

    def cols(self, k):
        return pl.ds(pl.multiple_of(k * self.nn, LANES), self.nn)

    def shard(self, ref, k):
        return ref.at[0, :, :, self.cols(k)] if self.colshard else ref.at[k]

    def half(self, ref, k, h):
        return ref.at[0, h, :, self.cols(k)] if self.colshard else ref.at[k, h]

    def part(self, ref, k):
        return ref.at[0, :, self.cols(k)] if self.colshard else ref.at[k]


def _remote(src, dst, ssem, rsem, dev):
    return pltpu.make_async_remote_copy(src_ref=src, dst_ref=dst, send_sem=ssem, recv_sem=rsem, device_id=dev,
                                        device_id_type=MESH)


def _other_chips(x, y):
    chips = [(1 - x, y), (x, 1 - y), (1 - x, 1 - y)]
    return chips, [2 * cx + cy for cx, cy in chips]


def _hbm(a):
    return pltpu.with_memory_space_constraint(a, pltpu.HBM)


def _gather_start(name, groups, lands, after=()):
    flat = [w for grp in groups for w in grp]
    nw, ng = len(flat), len(groups)

    def body(*refs):
        land = refs[:nw]
        sems = refs[nw + len(after):nw + len(after) + 2 * ng]
        token = refs[2 * nw + len(after) + 2 * ng]
        x, y, c = _mesh_pos()
        k_me = 2 * x + y
        chips, _ = _other_chips(x, y)
        i = 0
        for g, grp in enumerate(groups):
            for wi, w in enumerate(grp):
                mine = w.half(land[i], k_me, c)
                for j, chip in enumerate(chips):
                    _remote(mine, mine, sems[2 * g].at[3 * wi + j], sems[2 * g + 1].at[3 * wi + j], (*chip, c)).start()
                i += 1
        token[...] = jnp.zeros_like(token)

    sem_shapes = []
    for grp in groups:
        sem_shapes += [pltpu.SemaphoreType.DMA((3 * len(grp),))] * 2
    out = pl.pallas_call(
        body, name=name, in_specs=[HBM] * nw + [ANY] * len(after),
        out_specs=[SEM] * (2 * ng) + [HBM] * nw + [VMEM],
        out_shape=sem_shapes + [pltpu.HBM(a.shape, a.dtype) for a in lands] + [jax.ShapeDtypeStruct((8, LANES), F32)],
        input_output_aliases={i: 2 * ng + i for i in range(nw)},
        compiler_params=pltpu.CompilerParams(has_side_effects=EFFECT),
    )(*[_hbm(a) for a in lands], *after)
    sems = [(out[2 * g], out[2 * g + 1]) for g in range(ng)]
    return sems, list(out[2 * ng:2 * ng + nw]), out[-1]


def _gather_wait(name, grp, lands, ssem, rsem, after):
    n = len(grp)

    def body(*refs):
        land, ssem_ref, rsem_ref = refs[:n], refs[n], refs[n + 1]
        x, y, c = _mesh_pos()
        k_me = 2 * x + y
        chips, ks = _other_chips(x, y)
        for wi, w in enumerate(grp):
            for j, chip in enumerate(chips):
                cp = _remote(w.half(land[wi], k_me, c), w.half(land[wi], ks[j], c), ssem_ref.at[3 * wi + j],
                             rsem_ref.at[3 * wi + j], (*chip, c))
                cp.wait_send()
                cp.wait_recv()

    return pl.pallas_call(
        body, name=name, in_specs=[HBM] * n + [SEM, SEM, ANY], out_specs=[HBM] * n,
        out_shape=[pltpu.HBM(a.shape, a.dtype) for a in lands], input_output_aliases={i: i for i in range(n)},
        compiler_params=pltpu.CompilerParams(has_side_effects=EFFECT),
    )(*lands, ssem, rsem, after)


def _split_start(name, arrays, n, copies, after=()):
    na = len(arrays)

    def body(*refs):
        ssem, rsem, token = refs[na + len(after):][0], refs[na + len(after):][1], refs[2 * na + len(after) + 2]
        for i, (src, dst, dev, _) in enumerate(copies(refs[:na], *_mesh_pos())):
            _remote(src, dst, ssem.at[i], rsem.at[i], dev).start()
        token[...] = jnp.zeros_like(token)

    out = pl.pallas_call(
        body, name=name, in_specs=[HBM] * na + [ANY] * len(after), out_specs=[SEM, SEM] + [HBM] * na + [VMEM],
        out_shape=[pltpu.SemaphoreType.DMA((n,))] * 2 + [pltpu.HBM(a.shape, a.dtype) for a in arrays]
        + [jax.ShapeDtypeStruct((8, LANES), F32)],
        input_output_aliases={i: 2 + i for i in range(na)},
        compiler_params=pltpu.CompilerParams(has_side_effects=EFFECT),
    )(*[_hbm(a) for a in arrays], *after)
    return out[0], out[1], list(out[2:2 + na]), out[-1]


def _split_wait(name, arrays, ssem, rsem, copies, after):
    na = len(arrays)

    def body(*refs):
        for i, (src, _, dev, dst) in enumerate(copies(refs[:na], *_mesh_pos())):
            cp = _remote(src, dst, refs[na].at[i], refs[na + 1].at[i], dev)
            cp.wait_send()
            cp.wait_recv()

    return list(pl.pallas_call(
        body, name=name, in_specs=[HBM] * na + [SEM, SEM] + [ANY] * len(after), out_specs=[HBM] * na,
        out_shape=[pltpu.HBM(a.shape, a.dtype) for a in arrays], input_output_aliases={i: i for i in range(na)},
        compiler_params=pltpu.CompilerParams(has_side_effects=EFFECT),
    )(*arrays, ssem, rsem, *after))


def _pass_copies(grp):
    def copies(land, x, y, c):
        _, ks = _other_chips(x, y)
        return [(w.half(land[wi], ks[j], c), w.half(land[wi], ks[j], c), (x, y, 1 - c), w.half(land[wi], ks[j], 1 - c))
                for wi, w in enumerate(grp) for j in range(3)]
    return copies


def _pair_copies(n):
    def copies(refs, x, y, c):
        return [(refs[i].at[:, 1 - c], refs[n + i], (x, y, 1 - c), refs[n + i]) for i in range(n)]
    return copies


def _share_copies(n):
    def copies(refs, x, y, c):
        return [(refs[i].at[c], refs[i].at[c], (x, y, 1 - c), refs[i].at[1 - c]) for i in range(n)]
    return copies


def _gather_conv_w(cw):
    ncw = cw.shape[1]

    def body(cw_ref, out_ref, ssem, rsem):
        x, y, c = _mesh_pos()
        k_me = 2 * x + y
        chips, ks = _other_chips(x, y)
        cols = lambda k: out_ref.at[:, pl.ds(pl.multiple_of(k * ncw, LANES), ncw)]
        cps = [_remote(cw_ref, cols(k_me), ssem.at[j], rsem.at[j], (*chip, c)) for j, chip in enumerate(chips)]
        for cp in cps:
            cp.start()
        for k in range(N_CHIPS):
            @pl.when(k_me == k)
            def _():
                out_ref[:, k * ncw:(k + 1) * ncw] = cw_ref[...]
        for j in range(3):
            _remote(cw_ref, cols(ks[j]), ssem.at[j], rsem.at[j], (*chips[j], c)).wait_recv()
        for cp in cps:
            cp.wait_send()

    return pl.pallas_call(
        body, name="gather_conv_w", in_specs=[VMEM], out_specs=VMEM,
        out_shape=jax.ShapeDtypeStruct((3, N_CHIPS * ncw), F32),
        scratch_shapes=[pltpu.SemaphoreType.DMA((3,)), pltpu.SemaphoreType.DMA((3,))],
    )(cw)


def _grad_tiles(w, n):
    return _tile8(w.R, 512) if w.R <= 512 else w.R // 2, _tile(n, 2048)


def _pair_sum(name, w, pos, grad, got):
    tr, tn = _grad_tiles(w, w.N)

    def body(pos_ref, g_ref, r_ref, o_ref):
        o_ref[...] = (g_ref[...].astype(F32) + r_ref[...].astype(F32)).astype(BF16)

    blk = pl.BlockSpec((None, tr, tn), lambda p, i, j, pos: (p, i, j))
    grid_spec = pltpu.PrefetchScalarGridSpec(
        num_scalar_prefetch=1, grid=(w.P, w.R // tr, w.N // tn),
        in_specs=[pl.BlockSpec((None, None, tr, tn), lambda p, i, j, pos: (p, pos[0], i, j)), blk], out_specs=blk)
    return pl.pallas_call(body, name=name, grid_spec=grid_spec, out_shape=jax.ShapeDtypeStruct((w.P, w.R, w.N), BF16),
                          compiler_params=_cp(("parallel",) * 3))(pos, grad, got)


def _scatter_start(name, ws, pairs):
    nw = len(ws)

    def body(*refs):
        pr, land = refs[:nw], refs[nw:2 * nw]
        ssem, rsem = refs[2 * nw], refs[2 * nw + 1]
        token = refs[4 * nw + 2]
        x, y, c = _mesh_pos()
        chips, ks = _other_chips(x, y)
        for i, w in enumerate(ws):
            for j, chip in enumerate(chips):
                _remote(w.part(pr[i], ks[j]), land[i].at[j], ssem.at[3 * i + j], rsem.at[3 * i + j], (*chip, c)).start()
        token[...] = jnp.zeros_like(token)

    lands = [lax.empty((3, w.R, w.nn), BF16) for w in ws]
    out = pl.pallas_call(
        body, name=name, in_specs=[HBM] * (2 * nw),
        out_specs=[SEM, SEM] + [HBM] * (2 * nw) + [VMEM],
        out_shape=[pltpu.SemaphoreType.DMA((3 * nw,))] * 2 + [pltpu.HBM(a.shape, a.dtype) for a in pairs + lands]
        + [jax.ShapeDtypeStruct((8, LANES), F32)],
        input_output_aliases={i: 2 + i for i in range(2 * nw)},
        compiler_params=pltpu.CompilerParams(has_side_effects=EFFECT),
    )(*[_hbm(a) for a in pairs + lands])
    return out[0], out[1], list(out[2:2 + nw]), list(out[2 + nw:2 + 2 * nw]), out[-1]


def _scatter_wait(name, ws, pairs, lands, ssem, rsem, after):
    nw = len(ws)

    def body(*refs):
        pr, land = refs[:nw], refs[nw:2 * nw]
        ssem_ref, rsem_ref = refs[2 * nw], refs[2 * nw + 1]
        x, y, c = _mesh_pos()
        chips, ks = _other_chips(x, y)
        for i, w in enumerate(ws):
            for j, chip in enumerate(chips):
                cp = _remote(w.part(pr[i], ks[j]), land[i].at[j], ssem_ref.at[3 * i + j], rsem_ref.at[3 * i + j], (*chip, c))
                cp.wait_send()
                cp.wait_recv()

    out = pl.pallas_call(
        body, name=name, in_specs=[HBM] * (2 * nw) + [SEM, SEM] + [ANY] * len(after), out_specs=[HBM] * (2 * nw),
        out_shape=[pltpu.HBM(a.shape, a.dtype) for a in pairs + lands],
        input_output_aliases={i: i for i in range(2 * nw)},
        compiler_params=pltpu.CompilerParams(has_side_effects=EFFECT),
    )(*pairs, *lands, ssem, rsem, *after)
    return list(out[nw:])


def _final_sum(name, w, pos, grad, got, parts):
    tr, tn = _grad_tiles(w, w.nn)
    nbc = w.nn // tn

    def body(pos_ref, g_ref, r_ref, p_ref, o_ref):
        acc = g_ref[...].astype(F32) + r_ref[...].astype(F32)
        for j in range(3):
            acc = acc + p_ref[j].astype(F32)
        o_ref[...] = acc

    if w.colshard:
        g_spec = pl.BlockSpec((None, None, tr, tn), lambda i, j, pos: (0, pos[0], i, pos[1] * nbc + j))
        r_spec = pl.BlockSpec((None, tr, tn), lambda i, j, pos: (0, i, pos[1] * nbc + j))
    else:
        g_spec = pl.BlockSpec((None, None, tr, tn), lambda i, j, pos: (pos[1], pos[0], i, j))
        r_spec = pl.BlockSpec((None, tr, tn), lambda i, j, pos: (pos[1], i, j))
    grid_spec = pltpu.PrefetchScalarGridSpec(
        num_scalar_prefetch=1, grid=(w.R // tr, nbc),
        in_specs=[g_spec, r_spec, pl.BlockSpec((3, tr, tn), lambda i, j, pos: (0, i, j))],
        out_specs=pl.BlockSpec((None, tr, tn), lambda i, j, pos: (pos[0], i, j)))
    return pl.pallas_call(body, name=name, grid_spec=grid_spec, out_shape=jax.ShapeDtypeStruct((2, w.R, w.nn), F32),
                          compiler_params=_cp(("parallel",) * 2))(pos, grad, got, parts)


def _share_halves(name, ws, halves, deps=()):
    nw = len(ws)

    def body(*refs):
        out = refs[nw + len(deps):2 * nw + len(deps)]
        ssem, rsem = refs[2 * nw + len(deps):]
        x, y, c = _mesh_pos()
        sib = (x, y, 1 - c)
        cps = [_remote(out[i].at[c], out[i].at[c], ssem.at[i], rsem.at[i], sib) for i in range(nw)]
        for cp in cps:
            cp.start()
        for i, cp in enumerate(cps):
            cp.wait_send()
            _remote(out[i].at[1 - c], out[i].at[1 - c], ssem.at[i], rsem.at[i], sib).wait_recv()

    return pl.pallas_call(
        body, name=name, in_specs=[ANY] * (nw + len(deps)), out_specs=[ANY] * nw,
        out_shape=[jax.ShapeDtypeStruct(h.shape, F32) for h in halves],
        scratch_shapes=[pltpu.SemaphoreType.DMA((nw,)), pltpu.SemaphoreType.DMA((nw,))],
        input_output_aliases={i: i for i in range(nw)},
    )(*halves, *deps)


VEC_ROWS = 16


def _vector_step(d, n_conv, parts, params):
    ncw = params[2][0].shape[1]
    n_par = len(params)

    def body(*refs):
        dg1, dba, dbb, dcw, dcb, dps, dg2, dgf, lc = refs[:9]
        wmv = refs[9:9 + 3 * n_par]
        outs = refs[9 + 3 * n_par:9 + 7 * n_par]
        loss_ref = refs[9 + 7 * n_par]
        snd, got, ssem, rsem = refs[9 + 7 * n_par + 1:]
        x, y, c = _mesh_pos()
        me = 4 * x + 2 * y + c
        snd[...] = jnp.zeros_like(snd)
        for row, ref in ((0, dg1), (1, dba), (2, dbb), (3, dps), (4, dg2), (5, dgf), (6, lc)):
            snd[row:row + 1, :] = ref[...]
        snd[7:8, :n_conv] = dcb[...]
        snd[8:11, :n_conv] = dcw[...]
        cps = []
        for r in range(1, N_DEV):
            peer = tuple(1 - p if (r >> b) & 1 else p for p, b in ((x, 2), (y, 1), (c, 0)))
            cps.append(_remote(snd, got.at[me], ssem.at[r - 1], rsem.at[r - 1], peer))
        for cp in cps:
            cp.start()
        got[me] = snd[...]
        for r in range(1, N_DEV):
            peer = tuple(1 - p if (r >> b) & 1 else p for p, b in ((x, 2), (y, 1), (c, 0)))
            _remote(snd, got.at[4 * peer[0] + 2 * peer[1] + peer[2]], ssem.at[r - 1], rsem.at[r - 1], peer).wait_recv()
        for cp in cps:
            cp.wait_send()
        tot = got[0]
        for dev in range(1, N_DEV):
            tot = tot + got[dev]
        loss_ref[...] = jnp.sum(tot[6:7, :], axis=1, keepdims=True)
        k_me = 2 * x + y
        g_cw = jnp.zeros((3, ncw), F32)
        for k in range(N_CHIPS):
            g_cw = g_cw + jnp.where(k_me == k, tot[8:11, k * ncw:(k + 1) * ncw], 0.0)
        grads = [tot[0:1, :], jnp.concatenate([tot[1:2, :], tot[2:3, :]], axis=1), g_cw, tot[7:8, :n_conv],
                 tot[3:4, :], tot[4:5, :], tot[5:6, :]]
        for i, g in enumerate(grads):
            w_ref, m_ref, v_ref = wmv[3 * i:3 * i + 3]
            delta, nm, nv = _adamw_math(w_ref[...], g, m_ref[...], v_ref[...])
            outs[4 * i][...] = g
            outs[4 * i + 1][...] = delta
            outs[4 * i + 2][...] = nm
            outs[4 * i + 3][...] = nv

    args = list(parts)
    out_shape = []
    for w, m, v in params:
        args += [w, m, v]
        out_shape += [jax.ShapeDtypeStruct(w.shape, F32)] * 4
    out_shape.append(jax.ShapeDtypeStruct((1, 1), F32))
    return pl.pallas_call(
        body, name="vector_params_step", in_specs=[VMEM] * len(args), out_specs=[VMEM] * len(out_shape),
        out_shape=out_shape,
        scratch_shapes=[pltpu.VMEM((VEC_ROWS, d), F32), pltpu.VMEM((N_DEV, VEC_ROWS, d), F32),
                        pltpu.SemaphoreType.DMA((N_DEV - 1,)), pltpu.SemaphoreType.DMA((N_DEV - 1,))],
        compiler_params=pltpu.CompilerParams(vmem_limit_bytes=VMEM_LIMIT),
    )(*args)


def kernel(x, norm1_g, w_in, b_gate, conv_w, conv_b, w_a_out, w_pool, pool_scale, w_o, norm2_g, w_ffn_gate, w_ffn_up, w_ffn_down, final_g, loss_target, m_norm1_g, m_w_in, m_b_gate, m_conv_w, m_conv_b, m_w_a_out, m_w_pool, m_pool_scale, m_w_o, m_norm2_g, m_w_ffn_gate, m_w_ffn_up, m_w_ffn_down, m_final_g, v_norm1_g, v_w_in, v_b_gate, v_conv_w, v_conv_b, v_w_a_out, v_w_pool, v_pool_scale, v_w_o, v_norm2_g, v_w_ffn_gate, v_w_ffn_up, v_w_ffn_down, v_final_g):
    t, d = x.shape[1], x.shape[2]
    n_conv = conv_b.shape[1]
    n_groups, pool_cg, pool_dg = w_pool.shape[1], w_pool.shape[2], N_CHIPS * w_pool.shape[3]
    d_ff = N_CHIPS * w_ffn_gate.shape[2]
    assert n_conv // n_groups == pool_cg and n_conv % (n_groups * MIX_COLS) == 0 and n_groups == len(POOL_WINDOWS)

    big = {"w_in": (w_in, m_w_in, v_w_in), "w_a_out": (w_a_out, m_w_a_out, v_w_a_out), "w_pool": (w_pool, m_w_pool, v_w_pool),
           "w_o": (w_o, m_w_o, v_w_o), "w_ffn_gate": (w_ffn_gate, m_w_ffn_gate, v_w_ffn_gate),
           "w_ffn_up": (w_ffn_up, m_w_ffn_up, v_w_ffn_up), "w_ffn_down": (w_ffn_down, m_w_ffn_down, v_w_ffn_down)}
    colshard = {"w_in": True, "w_a_out": True, "w_pool": True, "w_o": False, "w_ffn_gate": True, "w_ffn_up": True,
                "w_ffn_down": False}
    names = list(big)
    shard2d = {n: big[n][0].reshape(-1, big[n][0].shape[-1]) for n in names}
    ws = [_Weight(n, *shard2d[n].shape, colshard[n]) for n in names]

    xs, tgt = x[0], loss_target[0]
    cw_loc = conv_w[0]
    pos = jnp.stack([lax.axis_index("c"), 2 * lax.axis_index("x") + lax.axis_index("y")]).astype(jnp.int32)
    by_name = {w.name: w for w in ws}
    groups = [[by_name[n] for n in g] for g in (["w_in"], ["w_a_out", "w_pool", "w_o"], ["w_ffn_gate"], ["w_ffn_up"],
                                                 ["w_ffn_down"])]
    first = [sum(len(g) for g in groups[:i]) for i in range(len(groups))]
    rgroups = [groups[0], groups[1], groups[2] + groups[3], groups[4]]

    cw_full = _gather_conv_w(cw_loc)
    cast = lambda w, dep: _cast_place(f"cast_{w.name}", w, pos, shard2d[w.name].reshape(2, w.R, w.nn), deps=[dep])
    sems_a, lands_a, tok_a = _gather_start("gather_start_a", groups[:1], [cast(w, cw_full) for w in groups[0]])
    rest = [cast(w, tok_a) for grp in groups[1:] for w in grp]
    full = {}

    def landed(g, after):
        return _gather_wait(f"gather_wait_{g}", groups[g], lands[first[g]:first[g] + len(groups[g])], *gsems[g], after)

    def pass_start(g, got, after=()):
        return _split_start(f"pass_start_{g}", got, 3 * len(got), _pass_copies(groups[g]), after)

    def pass_wait(g, started, after):
        ssem, rsem, got, _ = started
        got = _split_wait(f"pass_wait_{g}", got, ssem, rsem, _pass_copies(groups[g]), after)
        full.update({w.name: a.reshape(w.P * 2 * w.R, w.N) for w, a in zip(groups[g], got)})

    got = _gather_wait("gather_wait_0", groups[0], lands_a, *sems_a[0], rest[-1])
    sems_b, lands_b, tok_b = _gather_start("gather_start_b", groups[1:3], rest[:first[3] - 1], after=got)
    gsems, lands = sems_a + sems_b, lands_a + lands_b
    st = pass_start(0, got, after=[tok_b])
    h1 = _rms_fwd("norm1_fwd", xs, norm1_g, deps=[st[3]])
    pass_wait(0, st, [h1])
    w_in_full = full["w_in"]
    proj = _mm_nn("proj_in", h1, w_in_full, BF16)
    st = pass_start(1, landed(1, proj))
    z, p = _mixer_fwd("mixer_fwd", proj, cw_full, conv_b, n_conv, n_groups, deps=[st[3]])
    pass_wait(1, st, [z])
    wp_full = full["w_pool"].reshape(n_groups, pool_cg, pool_dg)
    ya = _mm_nn("conv_out", z, full["w_a_out"], BF16)
    yb = _gmm_nn("pool_out", p, wp_full, BF16)
    merged = _merge_fwd("merge_fwd", proj, b_gate, ya, yb, pool_scale)
    got_g = landed(2, merged)
    sems_c, lands_c, tok_c = _gather_start("gather_start_c", groups[3:], rest[first[3] - 1:], after=got_g)
    gsems, lands = gsems + sems_c, lands + lands_c
    st_g = pass_start(2, got_g, after=[tok_c])
    x2 = _mm_nn("mix_out", merged, full["w_o"], F32, add=xs, deps=[st_g[3]])
    pass_wait(2, st_g, [x2])
    h2 = _rms_fwd("norm2_fwd", x2, norm2_g)
    gate = _mm_nn("ffn_gate", h2, full["w_ffn_gate"], BF16)
    st_u = pass_start(3, landed(3, gate))
    pass_wait(3, st_u, [st_u[3]])
    up, act = _ffn_up_act("ffn_up_act", h2, full["w_ffn_up"], gate)
    st_d = pass_start(4, landed(4, act))
    pass_wait(4, st_d, [st_d[3]])
    x3 = _mm_nn("ffn_down", act, full["w_ffn_down"], F32, add=x2, tk=d_ff // 4)

    pending = {}

    def pair_start(g, grads):
        grp = rgroups[g]
        gcan = [grads[w.name].reshape(w.P, 2, w.R, w.N) for w in grp]
        slots = [lax.empty((w.P, w.R, w.N), BF16) for w in grp]
        pending[g] = _split_start(f"pair_start_{g}", gcan + slots, len(grp), _pair_copies(len(grp)))
        return pending[g][3]

    def scatter_start(g, after):
        grp = rgroups[g]
        n = len(grp)
        ssem, rsem, arrs, _ = pending[g]
        arrs = _split_wait(f"pair_wait_{g}", arrs, ssem, rsem, _pair_copies(n), after)
        gcan, sib = arrs[:n], arrs[n:]
        pairs = [_pair_sum(f"pair_sum_{w.name}", w, pos, a, s) for w, a, s in zip(grp, gcan, sib)]
        ssem, rsem, pairs, slots, token = _scatter_start(f"scatter_start_{g}", grp, pairs)
        pending[g] = (gcan, sib, pairs, slots, ssem, rsem)
        return token

    def reduce_finish(g, after):
        grp = rgroups[g]
        gcan, sib, pairs, slots, ssem, rsem = pending[g]
        parts = _scatter_wait(f"scatter_wait_{g}", grp, pairs, slots, ssem, rsem, after)
        return [_final_sum(f"final_sum_{w.name}", w, pos, a, s, q) for w, a, s, q in zip(grp, gcan, sib, parts)]

    grads = {}
    dx3, dx3b, d_gf, loss_cols = _final_bwd("final_bwd", x3, final_g.reshape(1, d), tgt)
    dgate, dup = _ffn_bwd("ffn_bwd", dx3b, full["w_ffn_down"], gate, up)
    grads["w_ffn_down"] = _mm_tn("dw_ffn_down", act, dx3b, BF16)
    tok = pair_start(3, grads)
    dh2 = _mm_nt("d_h2", [(dgate, full["w_ffn_gate"]), (dup, full["w_ffn_up"])], F32, tk=d_ff // 4, deps=[tok])
    tok = scatter_start(3, [dh2])
    grads["w_ffn_gate"] = _mm_tn("dw_ffn_gate", h2, dgate, BF16, deps=[tok])
    grads["w_ffn_up"] = _mm_tn("dw_ffn_up", h2, dup, BF16)
    tok = pair_start(2, grads)
    dx2, dx2b, d_g2 = _rms_bwd("norm2_bwd", x2, norm2_g, dh2, dx3, True, deps=[tok])
    dmerged = _mm_nt("d_merged", [(dx2b, full["w_o"])], BF16, tk=d)
    grads["w_o"] = _mm_tn("dw_o", merged, dx2b, BF16)
    tok = scatter_start(2, [grads["w_o"]])
    dya, dyb, dproj, d_bga, d_bgb, d_ps = _merge_bwd("merge_bwd", dmerged, proj, b_gate, ya, yb, pool_scale, deps=[tok])
    dz = _mm_nt("d_z", [(dya, full["w_a_out"])], BF16, tk=d)
    grads["w_a_out"] = _mm_tn("dw_a_out", z, dya, BF16)
    dp = _gmm_nt("d_pool", dyb, wp_full, BF16)
    grads["w_pool"] = _gmm_tn("dw_pool", p, dyb, n_groups, BF16)
    tok = pair_start(1, grads)
    dproj, d_cw, d_cb = _mixer_bwd("mixer_bwd", dz, dp, proj, cw_full, conv_b, dproj, n_conv, n_groups, deps=[tok])
    tok = scatter_start(1, [dproj])
    dh1 = _mm_nt("d_h1", [(dproj, w_in_full)], F32, tk=proj.shape[1] // 4, deps=[tok])
    grad_x, d_g1 = _rms_bwd("norm1_bwd", xs, norm1_g, dh1, dx2, False)

    vec_names = ["norm1_g", "b_gate", "conv_w", "conv_b", "pool_scale", "norm2_g", "final_g"]
    vec = {"norm1_g": (norm1_g, m_norm1_g, v_norm1_g), "b_gate": (b_gate, m_b_gate, v_b_gate),
           "conv_w": (cw_loc, m_conv_w[0], v_conv_w[0]), "conv_b": (conv_b, m_conv_b, v_conv_b),
           "pool_scale": (pool_scale, m_pool_scale, v_pool_scale), "norm2_g": (norm2_g, m_norm2_g, v_norm2_g),
           "final_g": tuple(a.reshape(1, d) for a in (final_g, m_final_g, v_final_g))}
    vout = _vector_step(d, n_conv, [d_g1, d_bga, d_bgb, d_cw, d_cb, d_ps, d_g2, d_gf, loss_cols],
                        [vec[n] for n in vec_names])

    grads["w_in"] = _mm_tn("dw_in", h1, dproj, BF16, deps=[vout[-1]])
    tok = pair_start(0, grads)

    g_big, d_big, m_big, v_big = {}, {}, {}, {}

    def update(wsub, shared):
        out = []
        for w, g in zip(wsub, shared):
            wt, mt, vt = big[w.name]
            g2 = g.reshape(2 * w.R, w.nn)
            go, dl, nm, nv = _adamw(f"adamw_{w.name}", shard2d[w.name], g2, mt.reshape(g2.shape), vt.reshape(g2.shape))
            g_big[w.name], d_big[w.name], m_big[w.name], v_big[w.name] = (a.reshape(wt.shape) for a in (go, dl, nm, nv))
            out.append(nv)
        return out

    after, early, early_halves = [tok], [], []
    for g in (3, 2, 1):
        halves = reduce_finish(g, after)
        early += rgroups[g]
        early_halves += halves
        after = halves[-1:]
    share = _share_copies(len(early))
    ssem, rsem, early_halves, tok = _split_start("share_start_early", early_halves, len(early), share)
    tok = scatter_start(0, [tok])
    after = update(early, _split_wait("share_wait_early", early_halves, ssem, rsem, share, [tok]))
    update(rgroups[0], _share_halves("share_halves_w_in", rgroups[0], reduce_finish(0, after)))

    shapes = {"conv_w": conv_w.shape, "final_g": final_g.shape}
    g_vec, d_vec, m_vec, v_vec = ({n: vout[4 * i + q].reshape(shapes.get(n, vec[n][0].shape)) for i, n in enumerate(vec_names)}
                                  for q in range(4))
    loss = vout[-1].reshape(())

    order = ["norm1_g", "w_in", "b_gate", "conv_w", "conv_b", "w_a_out", "w_pool", "pool_scale", "w_o", "norm2_g",
             "w_ffn_gate", "w_ffn_up", "w_ffn_down", "final_g"]
    pick = lambda vecs, bigs: [vecs[n] if n in vecs else bigs[n] for n in order]
    return (loss, grad_x.reshape(x.shape), *pick(g_vec, g_big), *pick(d_vec, d_big), *pick(m_vec, m_big),
            *pick(v_vec, v_big))
```

```python
import functools

import jax
import jax.numpy as jnp
from jax import lax
from jax.experimental import pallas as pl
from jax.experimental.pallas import tpu as pltpu

F32, BF16 = jnp.float32, jnp.bfloat16
MESH = pl.DeviceIdType.MESH
ANY = pl.BlockSpec(memory_space=pl.ANY)
VMEM = pl.BlockSpec(memory_space=pltpu.VMEM)
HBM = pl.BlockSpec(memory_space=pltpu.HBM)
SEM = pl.BlockSpec(memory_space=pltpu.SEMAPHORE)
EFFECT = pltpu.SideEffectType.DATAFLOW_SIDE_EFFECTING

EPS = 1e-6
POOL_WINDOWS = (2, 4, 8, 16)
ADAM_LR, ADAM_B1, ADAM_B2, ADAM_EPS, ADAM_WD, ADAM_STEP = 0.001, 0.9, 0.999, 1e-08, 0.01, 10

V7X_VMEM_BYTES = 64 * 1024 * 1024
VMEM_LIMIT = V7X_VMEM_BYTES * 3 // 4
LANES = 128
N_CHIPS = 4
N_DEV = 8

_DIMS = {
    "nn": (((1,), (0,)), ((), ())),
    "nt": (((1,), (1,)), ((), ())),
    "tn": (((0,), (0,)), ((), ())),
}


def _cp(sem):
    return pltpu.CompilerParams(dimension_semantics=sem, vmem_limit_bytes=VMEM_LIMIT)


def _mesh_pos():
    return lax.axis_index("x"), lax.axis_index("y"), lax.axis_index("c")


def _mm(name, pairs, *, mode, grid, out_shape, o_spec, nk=1, kaxis=None, add=None, deps=()):
    npair = len(pairs)
    has_add = add is not None

    def body(*refs):
        ab = refs[: 2 * npair]
        pos = 2 * npair
        add_ref = refs[pos] if has_add else None
        pos += int(has_add) + len(deps)
        o_ref = refs[pos]
        acc_ref = refs[pos + 1] if nk > 1 else None
        d = None
        for p in range(npair):
            t = lax.dot_general(ab[2 * p][...], ab[2 * p + 1][...], _DIMS[mode], preferred_element_type=F32)
            d = t if d is None else d + t
        if nk == 1:
            if has_add:
                d = d + add_ref[...].astype(F32)
            o_ref[...] = d.astype(o_ref.dtype)
        else:
            k = pl.program_id(kaxis)

            @pl.when(k == 0)
            def _():
                acc_ref[...] = d

            @pl.when(k > 0)
            def _():
                acc_ref[...] += d

            @pl.when(k == nk - 1)
            def _():
                r = acc_ref[...]
                if has_add:
                    r = r + add_ref[...].astype(F32)
                o_ref[...] = r.astype(o_ref.dtype)

    args, specs = [], []
    for a, a_spec, b, b_spec in pairs:
        args += [a, b]
        specs += [a_spec, b_spec]
    if has_add:
        args.append(add[0])
        specs.append(add[1])
    args += list(deps)
    specs += [ANY] * len(deps)
    scratch = []
    if nk > 1:
        blk = [d for d in o_spec.block_shape if d is not None]
        scratch = [pltpu.VMEM(tuple(blk), F32)]
    sem = tuple("arbitrary" if (nk > 1 and ax == kaxis) else "parallel" for ax in range(len(grid)))
    return pl.pallas_call(
        body, name=name, grid=grid, in_specs=specs, out_specs=o_spec, out_shape=out_shape,
        scratch_shapes=scratch, compiler_params=_cp(sem),
    )(*args)


def _tile(n, pref):
    if n <= pref:
        return n
    for t in range(pref, 0, -LANES):
        if t % LANES == 0 and n % t == 0:
            return t
    raise ValueError(f"no tile for {n}")


def _mm_nn(name, a, b, out_dtype, add=None, tk=None, deps=()):
    m, kk = a.shape
    n = b.shape[1]
    tm, tn = _tile(m, 1024), _tile(n, 512)
    out_shape = jax.ShapeDtypeStruct((m, n), out_dtype)
    if tk is None or tk == kk:
        grid = (m // tm, n // tn)
        pairs = [(a, pl.BlockSpec((tm, kk), lambda i, j: (i, 0)), b, pl.BlockSpec((kk, tn), lambda i, j: (0, j)))]
        o_spec = pl.BlockSpec((tm, tn), lambda i, j: (i, j))
        add_ = None if add is None else (add, pl.BlockSpec((tm, tn), lambda i, j: (i, j)))
        return _mm(name, pairs, mode="nn", grid=grid, out_shape=out_shape, o_spec=o_spec, add=add_, deps=deps)
    tn = _tile(n, 1024)
    nk = kk // tk
    grid = (m // tm, n // tn, nk)
    pairs = [(a, pl.BlockSpec((tm, tk), lambda i, j, k: (i, k)), b, pl.BlockSpec((tk, tn), lambda i, j, k: (k, j)))]
    o_spec = pl.BlockSpec((tm, tn), lambda i, j, k: (i, j))
    add_ = None if add is None else (add, pl.BlockSpec((tm, tn), lambda i, j, k: (i, j)))
    return _mm(name, pairs, mode="nn", grid=grid, out_shape=out_shape, o_spec=o_spec, nk=nk, kaxis=2, add=add_, deps=deps)


def _mm_nt(name, abs_, out_dtype, tk, deps=()):
    m, kk = abs_[0][0].shape
    n = abs_[0][1].shape[0]
    tm = _tile(m, 1024)
    nk = kk // tk
    tn = _tile(n, 512 if nk == 1 else 1024)
    out_shape = jax.ShapeDtypeStruct((m, n), out_dtype)
    if nk == 1:
        grid = (m // tm, n // tn)
        pairs = [(a, pl.BlockSpec((tm, kk), lambda i, j: (i, 0)), b, pl.BlockSpec((tn, kk), lambda i, j: (j, 0)))
                 for a, b in abs_]
        o_spec = pl.BlockSpec((tm, tn), lambda i, j: (i, j))
        return _mm(name, pairs, mode="nt", grid=grid, out_shape=out_shape, o_spec=o_spec, deps=deps)
    grid = (m // tm, n // tn, nk)
    pairs = [(a, pl.BlockSpec((tm, tk), lambda i, j, k: (i, k)), b, pl.BlockSpec((tn, tk), lambda i, j, k: (j, k)))
             for a, b in abs_]
    o_spec = pl.BlockSpec((tm, tn), lambda i, j, k: (i, j))
    return _mm(name, pairs, mode="nt", grid=grid, out_shape=out_shape, o_spec=o_spec, nk=nk, kaxis=2, deps=deps)


def _mm_tn(name, a, b, out_dtype, deps=()):
    t, m = a.shape
    n = b.shape[1]
    tm, tn = _tile(m, 512), _tile(n, 2048)
    if n > m:
        grid = (n // tn, m // tm)
        a_map, b_map, o_map = (lambda j, i: (0, i)), (lambda j, i: (0, j)), (lambda j, i: (i, j))
    else:
        grid = (m // tm, n // tn)
        a_map, b_map, o_map = (lambda i, j: (0, i)), (lambda i, j: (0, j)), (lambda i, j: (i, j))
    pairs = [(a, pl.BlockSpec((t, tm), a_map), b, pl.BlockSpec((t, tn), b_map))]
    o_spec = pl.BlockSpec((tm, tn), o_map)
    return _mm(name, pairs, mode="tn", grid=grid, out_shape=jax.ShapeDtypeStruct((m, n), out_dtype), o_spec=o_spec,
               deps=deps)


def _gmm_nn(name, p, w, out_dtype):
    t = p.shape[0]
    g, cg, dg = w.shape
    tm = _tile(t, 1024)
    pairs = [(p, pl.BlockSpec((tm, cg), lambda i, j: (i, j)), w, pl.BlockSpec((None, cg, dg), lambda i, j: (j, 0, 0)))]
    o_spec = pl.BlockSpec((tm, dg), lambda i, j: (i, j))
    return _mm(name, pairs, mode="nn", grid=(t // tm, g), out_shape=jax.ShapeDtypeStruct((t, g * dg), out_dtype),
               o_spec=o_spec)


def _gmm_nt(name, dy, w, out_dtype):
    t = dy.shape[0]
    g, cg, dg = w.shape
    tm = _tile(t, 1024)
    pairs = [(dy, pl.BlockSpec((tm, dg), lambda i, j: (i, j)), w, pl.BlockSpec((None, cg, dg), lambda i, j: (j, 0, 0)))]
    o_spec = pl.BlockSpec((tm, cg), lambda i, j: (i, j))
    return _mm(name, pairs, mode="nt", grid=(t // tm, g), out_shape=jax.ShapeDtypeStruct((t, g * cg), out_dtype),
               o_spec=o_spec)


def _gmm_tn(name, p, dy, g, out_dtype):
    t = p.shape[0]
    cg, dg = p.shape[1] // g, dy.shape[1] // g
    pairs = [(p, pl.BlockSpec((t, cg), lambda j: (0, j)), dy, pl.BlockSpec((t, dg), lambda j: (0, j)))]
    o_spec = pl.BlockSpec((None, cg, dg), lambda j: (j, 0, 0))
    return _mm(name, pairs, mode="tn", grid=(g,), out_shape=jax.ShapeDtypeStruct((g, cg, dg), out_dtype), o_spec=o_spec)


ROW_TILE = 256


def _rows(t):
    return _tile8(t, ROW_TILE)


def _tile8(n, pref):
    if n <= pref:
        return n
    for t in range(pref, 0, -8):
        if n % t == 0:
            return t
    raise ValueError(f"no row tile for {n}")


def _cast_place(name, w, pos, shard, deps=()):
    tr = _tile8(w.R, 512)
    if w.colshard:
        o_map = lambda h, i, pos: (0, h, i, pos[1])
    else:
        o_map = lambda h, i, pos: (pos[1], h, i, 0)

    def body(pos_ref, w_ref, *rest):
        rest[-1][...] = w_ref[...].astype(BF16)

    grid_spec = pltpu.PrefetchScalarGridSpec(
        num_scalar_prefetch=1, grid=(2, w.R // tr),
        in_specs=[pl.BlockSpec((None, tr, w.nn), lambda h, i, pos: (h, i, 0))] + [ANY] * len(deps),
        out_specs=pl.BlockSpec((None, None, tr, w.nn), o_map))
    return pl.pallas_call(body, name=name, grid_spec=grid_spec, out_shape=jax.ShapeDtypeStruct((w.P, 2, w.R, w.N), BF16),
                          compiler_params=_cp(("parallel", "parallel")))(pos, shard, *deps)


def _rms_fwd(name, x, g, deps=()):
    t, d = x.shape
    tm = _rows(t)

    def body(x_ref, g_ref, *rest):
        xf = x_ref[...]
        r = lax.rsqrt(jnp.mean(xf * xf, axis=-1, keepdims=True) + EPS)
        rest[-1][...] = (xf * r * g_ref[...]).astype(BF16)

    return pl.pallas_call(
        body, name=name, grid=(t // tm,),
        in_specs=[pl.BlockSpec((tm, d), lambda i: (i, 0)), pl.BlockSpec((1, d), lambda i: (0, 0))] + [ANY] * len(deps),
        out_specs=pl.BlockSpec((tm, d), lambda i: (i, 0)), out_shape=jax.ShapeDtypeStruct((t, d), BF16),
        compiler_params=_cp(("parallel",)),
    )(x, g, *deps)


def _rms_bwd(name, x, g, dh, dres, want_bf16, deps=()):
    t, d = x.shape
    tm = _rows(t)

    def body(x_ref, g_ref, dh_ref, dres_ref, *rest):
        rest = rest[len(deps):]
        dx_ref, rest = rest[0], rest[1:]
        dg_ref = rest[-1]
        xf = x_ref[...]
        r = lax.rsqrt(jnp.mean(xf * xf, axis=-1, keepdims=True) + EPS)
        xh = xf * r
        dhf = dh_ref[...]
        dxh = dhf * g_ref[...]
        m = jnp.mean(dxh * xh, axis=-1, keepdims=True)
        dx = dres_ref[...] + r * (dxh - xh * m)
        dx_ref[...] = dx
        if want_bf16:
            rest[0][...] = dx.astype(BF16)

        @pl.when(pl.program_id(0) == 0)
        def _():
            dg_ref[...] = jnp.zeros_like(dg_ref)

        dg_ref[...] += jnp.sum(dhf * xh, axis=0, keepdims=True)

    row = pl.BlockSpec((tm, d), lambda i: (i, 0))
    vec = pl.BlockSpec((1, d), lambda i: (0, 0))
    out_specs = [row] + ([row] if want_bf16 else []) + [vec]
    out_shape = ([jax.ShapeDtypeStruct((t, d), F32)] + ([jax.ShapeDtypeStruct((t, d), BF16)] if want_bf16 else [])
                 + [jax.ShapeDtypeStruct((1, d), F32)])
    return pl.pallas_call(body, name=name, grid=(t // tm,), in_specs=[row, vec, row, row] + [ANY] * len(deps),
                          out_specs=out_specs, out_shape=out_shape, compiler_params=_cp(("arbitrary",)))(x, g, dh, dres, *deps)


def _final_bwd(name, x3, gf, tgt):
    t, d = x3.shape
    tm = _rows(t)

    def body(x_ref, g_ref, t_ref, dx_ref, dxb_ref, dg_ref, lc_ref):
        xf = x_ref[...]
        g = g_ref[...]
        r = lax.rsqrt(jnp.mean(xf * xf, axis=-1, keepdims=True) + EPS)
        xh = xf * r
        diff = xh * g - t_ref[...]
        dy = diff * (1.0 / d)
        dxh = dy * g
        m = jnp.mean(dxh * xh, axis=-1, keepdims=True)
        dx = r * (dxh - xh * m)
        dx_ref[...] = dx
        dxb_ref[...] = dx.astype(BF16)

        @pl.when(pl.program_id(0) == 0)
        def _():
            dg_ref[...] = jnp.zeros_like(dg_ref)
            lc_ref[...] = jnp.zeros_like(lc_ref)

        dg_ref[...] += jnp.sum(dy * xh, axis=0, keepdims=True)
        lc_ref[...] += jnp.sum(diff * diff, axis=0, keepdims=True) * (0.5 / d)

    row = pl.BlockSpec((tm, d), lambda i: (i, 0))
    vec = pl.BlockSpec((1, d), lambda i: (0, 0))
    return pl.pallas_call(
        body, name=name, grid=(t // tm,), in_specs=[row, vec, row], out_specs=[row, row, vec, vec],
        out_shape=[jax.ShapeDtypeStruct((t, d), F32), jax.ShapeDtypeStruct((t, d), BF16),
                   jax.ShapeDtypeStruct((1, d), F32), jax.ShapeDtypeStruct((1, d), F32)],
        compiler_params=_cp(("arbitrary",)),
    )(x3, gf, tgt)


def _shift_down(v, k, t_idx):
    return jnp.where(t_idx >= k, pltpu.roll(v, k, 0), 0.0)


def _shift_up(v, k, t_idx):
    n = v.shape[0]
    return jnp.where(t_idx < n - k, pltpu.roll(v, n - k, 0), 0.0)


def _window_sums(v, shift, t_idx, grp):
    s = v + shift(v, 1, t_idx)
    out = s
    for lvl in range(1, len(POOL_WINDOWS)):
        s = s + shift(s, 1 << lvl, t_idx)
        out = jnp.where(grp >= lvl, s, out)
    return out


def _window_count(t_idx, grp):
    return jnp.minimum(t_idx + 1, jnp.left_shift(2, grp)).astype(F32)


MIX_COLS = 128


def _mixer_fwd(name, proj, cw, cb, n_conv, n_groups, deps=()):
    t = proj.shape[0]
    nb = n_conv // MIX_COLS
    per_group = n_conv // n_groups // MIX_COLS

    def body(ba_ref, ca_ref, va_ref, vb_ref, cw_ref, cb_ref, *rest):
        z_ref, p_ref = rest[len(deps):]
        t_idx = lax.broadcasted_iota(jnp.int32, (t, MIX_COLS), 0)
        q = ca_ref[...].astype(F32) * va_ref[...].astype(F32)
        w = cw_ref[...]
        u = cb_ref[...] + w[0:1] * _shift_down(q, 2, t_idx) + w[1:2] * _shift_down(q, 1, t_idx) + w[2:3] * q
        z_ref[...] = (ba_ref[...].astype(F32) * u).astype(BF16)
        grp = pl.program_id(0) // per_group
        v = vb_ref[...].astype(F32)
        p_ref[...] = (_window_sums(v, _shift_down, t_idx, grp) / _window_count(t_idx, grp) - v).astype(BF16)

    col = lambda s: pl.BlockSpec((t, MIX_COLS), lambda j: (0, s * nb + j))
    return pl.pallas_call(
        body, name=name, grid=(nb,),
        in_specs=[col(0), col(1), col(2), col(3), pl.BlockSpec((3, MIX_COLS), lambda j: (0, j)),
                  pl.BlockSpec((1, MIX_COLS), lambda j: (0, j))] + [ANY] * len(deps),
        out_specs=[col(0), col(0)],
        out_shape=[jax.ShapeDtypeStruct((t, n_conv), BF16), jax.ShapeDtypeStruct((t, n_conv), BF16)],
        compiler_params=_cp(("parallel",)),
    )(proj, proj, proj, proj, cw, cb, *deps)


def _mixer_bwd(name, dz, dp, proj, cw, cb, dproj, n_conv, n_groups, deps=()):
    t = proj.shape[0]
    nb = n_conv // MIX_COLS
    per_group = n_conv // n_groups // MIX_COLS

    def body(dz_ref, dp_ref, ba_ref, ca_ref, va_ref, cw_ref, cb_ref, _, *rest):
        o_ref, dcw_ref, dcb_ref, scr = rest[len(deps):]
        s = pl.program_id(1)

        @pl.when(s == 0)
        def _():
            t_idx = lax.broadcasted_iota(jnp.int32, (t, MIX_COLS), 0)
            ca, va = ca_ref[...].astype(F32), va_ref[...].astype(F32)
            q = ca * va
            q1, q2 = _shift_down(q, 1, t_idx), _shift_down(q, 2, t_idx)
            w = cw_ref[...]
            u = cb_ref[...] + w[0:1] * q2 + w[1:2] * q1 + w[2:3] * q
            dzf = dz_ref[...].astype(F32)
            du = dzf * ba_ref[...].astype(F32)
            scr[0] = (dzf * u).astype(BF16)
            dq = w[2:3] * du + w[1:2] * _shift_up(du, 1, t_idx) + w[0:1] * _shift_up(du, 2, t_idx)
            scr[1] = (dq * va).astype(BF16)
            scr[2] = (dq * ca).astype(BF16)
            dcb_ref[...] = jnp.sum(du, axis=0, keepdims=True)
            dcw_ref[0:1, :] = jnp.sum(du * q2, axis=0, keepdims=True)
            dcw_ref[1:2, :] = jnp.sum(du * q1, axis=0, keepdims=True)
            dcw_ref[2:3, :] = jnp.sum(du * q, axis=0, keepdims=True)
            grp = pl.program_id(0) // per_group
            dpf = dp_ref[...].astype(F32)
            e = dpf / _window_count(t_idx, grp)
            scr[3] = (_window_sums(e, _shift_up, t_idx, grp) - dpf).astype(BF16)

        o_ref[...] = scr[s]

    col = lambda c: pl.BlockSpec((t, MIX_COLS), lambda j, s: (0, c * nb + j))
    own = pl.BlockSpec((t, MIX_COLS), lambda j, s: (0, j))
    return pl.pallas_call(
        body, name=name, grid=(nb, 4),
        in_specs=[own, own, col(0), col(1), col(2), pl.BlockSpec((3, MIX_COLS), lambda j, s: (0, j)),
                  pl.BlockSpec((1, MIX_COLS), lambda j, s: (0, j)), ANY] + [ANY] * len(deps),
        out_specs=[pl.BlockSpec((t, MIX_COLS), lambda j, s: (0, s * nb + j)),
                   pl.BlockSpec((3, MIX_COLS), lambda j, s: (0, j)), pl.BlockSpec((1, MIX_COLS), lambda j, s: (0, j))],
        out_shape=[jax.ShapeDtypeStruct(dproj.shape, BF16), jax.ShapeDtypeStruct((3, n_conv), F32),
                   jax.ShapeDtypeStruct((1, n_conv), F32)],
        scratch_shapes=[pltpu.VMEM((4, t, MIX_COLS), BF16)],
        input_output_aliases={7: 0},
        compiler_params=_cp(("arbitrary", "arbitrary")),
    )(dz, dp, proj, proj, proj, cw, cb, dproj, *deps)


def _merge_fwd(name, proj, bg, ya, yb, ps):
    t, d = ya.shape
    tm = _rows(t)

    def body(gab_ref, bg_ref, ya_ref, yb_ref, ps_ref, o_ref):
        gab = gab_ref[...].astype(F32) + bg_ref[...]
        sa, sb = jax.nn.sigmoid(gab[:, :d]), jax.nn.sigmoid(gab[:, d:])
        o_ref[...] = (sa * ya_ref[...].astype(F32) + sb * (yb_ref[...].astype(F32) * ps_ref[...])).astype(BF16)

    row = pl.BlockSpec((tm, d), lambda i: (i, 0))
    return pl.pallas_call(
        body, name=name, grid=(t // tm,),
        in_specs=[pl.BlockSpec((tm, 2 * d), lambda i: (i, 1)), pl.BlockSpec((1, 2 * d), lambda i: (0, 0)), row, row,
                  pl.BlockSpec((1, d), lambda i: (0, 0))],
        out_specs=row, out_shape=jax.ShapeDtypeStruct((t, d), BF16), compiler_params=_cp(("parallel",)),
    )(proj, bg, ya, yb, ps)


def _merge_bwd(name, dm, proj, bg, ya, yb, ps, deps=()):
    t, d = ya.shape
    tm = _rows(t)

    def body(dm_ref, gab_ref, bg_ref, ya_ref, yb_ref, ps_ref, *rest):
        dya_ref, dyb_ref, dg_ref, dba_ref, dbb_ref, dps_ref = rest[len(deps):]
        gab = gab_ref[...].astype(F32) + bg_ref[...]
        sa, sb = jax.nn.sigmoid(gab[:, :d]), jax.nn.sigmoid(gab[:, d:])
        dmf = dm_ref[...].astype(F32)
        ybf, ps_ = yb_ref[...].astype(F32), ps_ref[...]
        dya_ref[...] = (dmf * sa).astype(BF16)
        dyb = dmf * sb
        dyb_ref[...] = (dyb * ps_).astype(BF16)
        dga = dmf * ya_ref[...].astype(F32) * sa * (1.0 - sa)
        dgb = dmf * (ybf * ps_) * sb * (1.0 - sb)
        dg_ref[:, :d] = dga.astype(BF16)
        dg_ref[:, d:] = dgb.astype(BF16)

        @pl.when(pl.program_id(0) == 0)
        def _():
            dba_ref[...] = jnp.zeros_like(dba_ref)
            dbb_ref[...] = jnp.zeros_like(dbb_ref)
            dps_ref[...] = jnp.zeros_like(dps_ref)

        dba_ref[...] += jnp.sum(dga, axis=0, keepdims=True)
        dbb_ref[...] += jnp.sum(dgb, axis=0, keepdims=True)
        dps_ref[...] += jnp.sum(dyb * ybf, axis=0, keepdims=True)

    row = pl.BlockSpec((tm, d), lambda i: (i, 0))
    vec = pl.BlockSpec((1, d), lambda i: (0, 0))
    gates = pl.BlockSpec((tm, 2 * d), lambda i: (i, 1))
    return pl.pallas_call(
        body, name=name, grid=(t // tm,),
        in_specs=[row, gates, pl.BlockSpec((1, 2 * d), lambda i: (0, 0)), row, row, vec] + [ANY] * len(deps),
        out_specs=[row, row, gates, vec, vec, vec],
        out_shape=[jax.ShapeDtypeStruct((t, d), BF16), jax.ShapeDtypeStruct((t, d), BF16),
                   jax.ShapeDtypeStruct(proj.shape, BF16), jax.ShapeDtypeStruct((1, d), F32),
                   jax.ShapeDtypeStruct((1, d), F32), jax.ShapeDtypeStruct((1, d), F32)],
        compiler_params=_cp(("arbitrary",)),
    )(dm, proj, bg, ya, yb, ps, *deps)


def _ffn_up_act(name, h, w_up, gate):
    t, d = h.shape
    f = w_up.shape[1]
    tm, tf = _tile(t, 1024), _tile(f, 512)

    def body(h_ref, w_ref, g_ref, u_ref, a_ref):
        u = lax.dot_general(h_ref[...], w_ref[...], _DIMS["nn"], preferred_element_type=F32)
        g = g_ref[...].astype(F32)
        u_ref[...] = u.astype(BF16)
        a_ref[...] = (g * jax.nn.sigmoid(g) * u).astype(BF16)

    blk = pl.BlockSpec((tm, tf), lambda i, j: (i, j))
    shp = jax.ShapeDtypeStruct((t, f), BF16)
    return pl.pallas_call(
        body, name=name, grid=(t // tm, f // tf),
        in_specs=[pl.BlockSpec((tm, d), lambda i, j: (i, 0)), pl.BlockSpec((d, tf), lambda i, j: (0, j)), blk],
        out_specs=[blk, blk], out_shape=[shp, shp], compiler_params=_cp(("parallel", "parallel")))(h, w_up, gate)


def _ffn_bwd(name, dy, w_down, gate, up):
    t, d = dy.shape
    f = w_down.shape[0]
    tm, tf = _tile(t, 1024), _tile(f, 512)

    def body(dy_ref, w_ref, g_ref, u_ref, dg_ref, du_ref):
        da = lax.dot_general(dy_ref[...], w_ref[...], _DIMS["nt"], preferred_element_type=F32)
        g = g_ref[...].astype(F32)
        s = jax.nn.sigmoid(g)
        du_ref[...] = (da * (g * s)).astype(BF16)
        dg_ref[...] = (da * u_ref[...].astype(F32) * (s * (1.0 + g * (1.0 - s)))).astype(BF16)

    blk = pl.BlockSpec((tm, tf), lambda i, j: (i, j))
    shp = jax.ShapeDtypeStruct((t, f), BF16)
    return pl.pallas_call(
        body, name=name, grid=(t // tm, f // tf),
        in_specs=[pl.BlockSpec((tm, d), lambda i, j: (i, 0)), pl.BlockSpec((tf, d), lambda i, j: (j, 0)), blk, blk],
        out_specs=[blk, blk], out_shape=[shp, shp], compiler_params=_cp(("parallel", "parallel")))(dy, w_down, gate, up)


def _adamw_math(w, g, m, v):
    m = ADAM_B1 * m + (1.0 - ADAM_B1) * g
    v = ADAM_B2 * v + (1.0 - ADAM_B2) * (g * g)
    m_hat = m / (1.0 - ADAM_B1 ** ADAM_STEP)
    v_hat = v / (1.0 - ADAM_B2 ** ADAM_STEP)
    delta = -ADAM_LR * (m_hat / (jnp.sqrt(v_hat) + ADAM_EPS) + ADAM_WD * w)
    return delta, m, v


def _adamw(name, w, g, m, v):
    r, c = w.shape
    tr = _tile8(r, 512 if c <= 1024 else 256)

    def body(w_ref, g_ref, m_ref, v_ref, _, d_ref, nm_ref, nv_ref):
        d_ref[...], nm_ref[...], nv_ref[...] = _adamw_math(w_ref[...], g_ref[...], m_ref[...], v_ref[...])

    blk = pl.BlockSpec((tr, c), lambda i: (i, 0))
    shp = jax.ShapeDtypeStruct((r, c), F32)
    return pl.pallas_call(body, name=name, grid=(r // tr,), in_specs=[blk] * 4, out_specs=[ANY, blk, blk, blk],
                          out_shape=[shp] * 4, input_output_aliases={1: 0},
                          compiler_params=_cp(("parallel",)))(w, g, m, v)


class _Weight:
    def __init__(self, name, rows, cols, colshard):
        self.name, self.colshard = name, colshard
        self.R, self.nn = rows // 2, cols
        self.P = 1 if colshard else N_CHIPS
        self.N = N_CHIPS * cols if colshard else cols

    def cols(self, k):
        return pl.ds(pl.multiple_of(k * self.nn, LANES), self.nn)

    def shard(self, ref, k):
        return ref.at[0, :, :, self.cols(k)] if self.colshard else ref.at[k]

    def half(self, ref, k, h):
        return ref.at[0, h, :, self.cols(k)] if self.colshard else ref.at[k, h]

    def part(self, ref, k):
        return ref.at[0, :, self.cols(k)] if self.colshard else ref.at[k]


def _remote(src, dst, ssem, rsem, dev):
    return pltpu.make_async_remote_copy(src_ref=src, dst_ref=dst, send_sem=ssem, recv_sem=rsem, device_id=dev,
                                        device_id_type=MESH)


def _other_chips(x, y):
    chips = [(1 - x, y), (x, 1 - y), (1 - x, 1 - y)]
    return chips, [2 * cx + cy for cx, cy in chips]


def _hbm(a):
    return pltpu.with_memory_space_constraint(a, pltpu.HBM)


def _gather_start(name, groups, lands, after=()):
    flat = [w for grp in groups for w in grp]
    nw, ng = len(flat), len(groups)

    def body(*refs):
        land = refs[:nw]
        sems = refs[nw + len(after):nw + len(after) + 2 * ng]
        token = refs[2 * nw + len(after) + 2 * ng]
        x, y, c = _mesh_pos()
        k_me = 2 * x + y
        chips, _ = _other_chips(x, y)
        i = 0
        for g, grp in enumerate(groups):
            for wi, w in enumerate(grp):
                mine = w.half(land[i], k_me, c)
                for j, chip in enumerate(chips):
                    _remote(mine, mine, sems[2 * g].at[3 * wi + j], sems[2 * g + 1].at[3 * wi + j], (*chip, c)).start()
                i += 1
        token[...] = jnp.zeros_like(token)

    sem_shapes = []
    for grp in groups:
        sem_shapes += [pltpu.SemaphoreType.DMA((3 * len(grp),))] * 2
    out = pl.pallas_call(
        body, name=name, in_specs=[HBM] * nw + [ANY] * len(after),
        out_specs=[SEM] * (2 * ng) + [HBM] * nw + [VMEM],
        out_shape=sem_shapes + [pltpu.HBM(a.shape, a.dtype) for a in lands] + [jax.ShapeDtypeStruct((8, LANES), F32)],
        input_output_aliases={i: 2 * ng + i for i in range(nw)},
        compiler_params=pltpu.CompilerParams(has_side_effects=EFFECT),
    )(*[_hbm(a) for a in lands], *after)
    sems = [(out[2 * g], out[2 * g + 1]) for g in range(ng)]
    return sems, list(out[2 * ng:2 * ng + nw]), out[-1]


def _gather_wait(name, grp, lands, ssem, rsem, after):
    n = len(grp)

    def body(*refs):
        land, ssem_ref, rsem_ref = refs[:n], refs[n], refs[n + 1]
        x, y, c = _mesh_pos()
        k_me = 2 * x + y
        chips, ks = _other_chips(x, y)
        for wi, w in enumerate(grp):
            for j, chip in enumerate(chips):
                cp = _remote(w.half(land[wi], k_me, c), w.half(land[wi], ks[j], c), ssem_ref.at[3 * wi + j],
                             rsem_ref.at[3 * wi + j], (*chip, c))
                cp.wait_send()
                cp.wait_recv()

    return pl.pallas_call(
        body, name=name, in_specs=[HBM] * n + [SEM, SEM, ANY], out_specs=[HBM] * n,
        out_shape=[pltpu.HBM(a.shape, a.dtype) for a in lands], input_output_aliases={i: i for i in range(n)},
        compiler_params=pltpu.CompilerParams(has_side_effects=EFFECT),
    )(*lands, ssem, rsem, after)


def _split_start(name, arrays, n, copies, after=()):
    na = len(arrays)

    def body(*refs):
        ssem, rsem, token = refs[na + len(after):][0], refs[na + len(after):][1], refs[2 * na + len(after) + 2]
        for i, (src, dst, dev, _) in enumerate(copies(refs[:na], *_mesh_pos())):
            _remote(src, dst, ssem.at[i], rsem.at[i], dev).start()
        token[...] = jnp.zeros_like(token)

    out = pl.pallas_call(
        body, name=name, in_specs=[HBM] * na + [ANY] * len(after), out_specs=[SEM, SEM] + [HBM] * na + [VMEM],
        out_shape=[pltpu.SemaphoreType.DMA((n,))] * 2 + [pltpu.HBM(a.shape, a.dtype) for a in arrays]
        + [jax.ShapeDtypeStruct((8, LANES), F32)],
        input_output_aliases={i: 2 + i for i in range(na)},
        compiler_params=pltpu.CompilerParams(has_side_effects=EFFECT),
    )(*[_hbm(a) for a in arrays], *after)
    return out[0], out[1], list(out[2:2 + na]), out[-1]


def _split_wait(name, arrays, ssem, rsem, copies, after):
    na = len(arrays)

    def body(*refs):
        for i, (src, _, dev, dst) in enumerate(copies(refs[:na], *_mesh_pos())):
            cp = _remote(src, dst, refs[na].at[i], refs[na + 1].at[i], dev)
            cp.wait_send()
            cp.wait_recv()

    return list(pl.pallas_call(
        body, name=name, in_specs=[HBM] * na + [SEM, SEM] + [ANY] * len(after), out_specs=[HBM] * na,
        out_shape=[pltpu.HBM(a.shape, a.dtype) for a in arrays], input_output_aliases={i: i for i in range(na)},
        compiler_params=pltpu.CompilerParams(has_side_effects=EFFECT),
    )(*arrays, ssem, rsem, *after))


def _pass_copies(grp):
    def copies(land, x, y, c):
        _, ks = _other_chips(x, y)
        return [(w.half(land[wi], ks[j], c), w.half(land[wi], ks[j], c), (x, y, 1 - c), w.half(land[wi], ks[j], 1 - c))
                for wi, w in enumerate(grp) for j in range(3)]
    return copies


def _pair_copies(n):
    def copies(refs, x, y, c):
        return [(refs[i].at[:, 1 - c], refs[n + i], (x, y, 1 - c), refs[n + i]) for i in range(n)]
    return copies


def _share_copies(n):
    def copies(refs, x, y, c):
        return [(refs[i].at[c], refs[i].at[c], (x, y, 1 - c), refs[i].at[1 - c]) for i in range(n)]
    return copies


def _gather_conv_w(cw):
    ncw = cw.shape[1]

    def body(cw_ref, out_ref, ssem, rsem):
        x, y, c = _mesh_pos()
        k_me = 2 * x + y
        chips, ks = _other_chips(x, y)
        cols = lambda k: out_ref.at[:, pl.ds(pl.multiple_of(k * ncw, LANES), ncw)]
        cps = [_remote(cw_ref, cols(k_me), ssem.at[j], rsem.at[j], (*chip, c)) for j, chip in enumerate(chips)]
        for cp in cps:
            cp.start()
        for k in range(N_CHIPS):
            @pl.when(k_me == k)
            def _():
                out_ref[:, k * ncw:(k + 1) * ncw] = cw_ref[...]
        for j in range(3):
            _remote(cw_ref, cols(ks[j]), ssem.at[j], rsem.at[j], (*chips[j], c)).wait_recv()
        for cp in cps:
            cp.wait_send()

    return pl.pallas_call(
        body, name="gather_conv_w", in_specs=[VMEM], out_specs=VMEM,
        out_shape=jax.ShapeDtypeStruct((3, N_CHIPS * ncw), F32),
        scratch_shapes=[pltpu.SemaphoreType.DMA((3,)), pltpu.SemaphoreType.DMA((3,))],
    )(cw)


def _grad_tiles(w, n):
    return _tile8(w.R, 512) if w.R <= 512 else w.R // 2, _tile(n, 2048)


def _pair_sum(name, w, pos, grad, got):
    tr, tn = _grad_tiles(w, w.N)

    def body(pos_ref, g_ref, r_ref, o_ref):
        o_ref[...] = (g_ref[...].astype(F32) + r_ref[...].astype(F32)).astype(BF16)

    blk = pl.BlockSpec((None, tr, tn), lambda p, i, j, pos: (p, i, j))
    grid_spec = pltpu.PrefetchScalarGridSpec(
        num_scalar_prefetch=1, grid=(w.P, w.R // tr, w.N // tn),
        in_specs=[pl.BlockSpec((None, None, tr, tn), lambda p, i, j, pos: (p, pos[0], i, j)), blk], out_specs=blk)
    return pl.pallas_call(body, name=name, grid_spec=grid_spec, out_shape=jax.ShapeDtypeStruct((w.P, w.R, w.N), BF16),
                          compiler_params=_cp(("parallel",) * 3))(pos, grad, got)


def _scatter_start(name, ws, pairs):
    nw = len(ws)

    def body(*refs):
        pr, land = refs[:nw], refs[nw:2 * nw]
        ssem, rsem = refs[2 * nw], refs[2 * nw + 1]
        token = refs[4 * nw + 2]
        x, y, c = _mesh_pos()
        chips, ks = _other_chips(x, y)
        for i, w in enumerate(ws):
            for j, chip in enumerate(chips):
                _remote(w.part(pr[i], ks[j]), land[i].at[j], ssem.at[3 * i + j], rsem.at[3 * i + j], (*chip, c)).start()
        token[...] = jnp.zeros_like(token)

    lands = [lax.empty((3, w.R, w.nn), BF16) for w in ws]
    out = pl.pallas_call(
        body, name=name, in_specs=[HBM] * (2 * nw),
        out_specs=[SEM, SEM] + [HBM] * (2 * nw) + [VMEM],
        out_shape=[pltpu.SemaphoreType.DMA((3 * nw,))] * 2 + [pltpu.HBM(a.shape, a.dtype) for a in pairs + lands]
        + [jax.ShapeDtypeStruct((8, LANES), F32)],
        input_output_aliases={i: 2 + i for i in range(2 * nw)},
        compiler_params=pltpu.CompilerParams(has_side_effects=EFFECT),
    )(*[_hbm(a) for a in pairs + lands])
    return out[0], out[1], list(out[2:2 + nw]), list(out[2 + nw:2 + 2 * nw]), out[-1]


def _scatter_wait(name, ws, pairs, lands, ssem, rsem, after):
    nw = len(ws)

    def body(*refs):
        pr, land = refs[:nw], refs[nw:2 * nw]
        ssem_ref, rsem_ref = refs[2 * nw], refs[2 * nw + 1]
        x, y, c = _mesh_pos()
        chips, ks = _other_chips(x, y)
        for i, w in enumerate(ws):
            for j, chip in enumerate(chips):
                cp = _remote(w.part(pr[i], ks[j]), land[i].at[j], ssem_ref.at[3 * i + j], rsem_ref.at[3 * i + j], (*chip, c))
                cp.wait_send()
                cp.wait_recv()

    out = pl.pallas_call(
        body, name=name, in_specs=[HBM] * (2 * nw) + [SEM, SEM] + [ANY] * len(after), out_specs=[HBM] * (2 * nw),
        out_shape=[pltpu.HBM(a.shape, a.dtype) for a in pairs + lands],
        input_output_aliases={i: i for i in range(2 * nw)},
        compiler_params=pltpu.CompilerParams(has_side_effects=EFFECT),
    )(*pairs, *lands, ssem, rsem, *after)
    return list(out[nw:])


def _final_sum(name, w, pos, grad, got, parts):
    tr, tn = _grad_tiles(w, w.nn)
    nbc = w.nn // tn

    def body(pos_ref, g_ref, r_ref, p_ref, o_ref):
        acc = g_ref[...].astype(F32) + r_ref[...].astype(F32)
        for j in range(3):
            acc = acc + p_ref[j].astype(F32)
        o_ref[...] = acc

    if w.colshard:
        g_spec = pl.BlockSpec((None, None, tr, tn), lambda i, j, pos: (0, pos[0], i, pos[1] * nbc + j))
        r_spec = pl.BlockSpec((None, tr, tn), lambda i, j, pos: (0, i, pos[1] * nbc + j))
    else:
        g_spec = pl.BlockSpec((None, None, tr, tn), lambda i, j, pos: (pos[1], pos[0], i, j))
        r_spec = pl.BlockSpec((None, tr, tn), lambda i, j, pos: (pos[1], i, j))
    grid_spec = pltpu.PrefetchScalarGridSpec(
        num_scalar_prefetch=1, grid=(w.R // tr, nbc),
        in_specs=[g_spec, r_spec, pl.BlockSpec((3, tr, tn), lambda i, j, pos: (0, i, j))],
        out_specs=pl.BlockSpec((None, tr, tn), lambda i, j, pos: (pos[0], i, j)))
    return pl.pallas_call(body, name=name, grid_spec=grid_spec, out_shape=jax.ShapeDtypeStruct((2, w.R, w.nn), F32),
                          compiler_params=_cp(("parallel",) * 2))(pos, grad, got, parts)


def _share_halves(name, ws, halves, deps=()):
    nw = len(ws)

    def body(*refs):
        out = refs[nw + len(deps):2 * nw + len(deps)]
        ssem, rsem = refs[2 * nw + len(deps):]
        x, y, c = _mesh_pos()
        sib = (x, y, 1 - c)
        cps = [_remote(out[i].at[c], out[i].at[c], ssem.at[i], rsem.at[i], sib) for i in range(nw)]
        for cp in cps:
            cp.start()
        for i, cp in enumerate(cps):
            cp.wait_send()
            _remote(out[i].at[1 - c], out[i].at[1 - c], ssem.at[i], rsem.at[i], sib).wait_recv()

    return pl.pallas_call(
        body, name=name, in_specs=[ANY] * (nw + len(deps)), out_specs=[ANY] * nw,
        out_shape=[jax.ShapeDtypeStruct(h.shape, F32) for h in halves],
        scratch_shapes=[pltpu.SemaphoreType.DMA((nw,)), pltpu.SemaphoreType.DMA((nw,))],
        input_output_aliases={i: i for i in range(nw)},
    )(*halves, *deps)


VEC_ROWS = 16


def _vector_step(d, n_conv, parts, params, deps=()):
    ncw = params[2][0].shape[1]
    n_par = len(params)

    def body(*refs):
        dg1, dba, dbb, dcw, dcb, dps, dg2, dgf, lc = refs[:9]
        wmv = refs[9:9 + 3 * n_par]
        refs = refs[9 + 3 * n_par + len(deps):]
        outs = refs[:4 * n_par]
        loss_ref = refs[4 * n_par]
        snd, got, ssem, rsem = refs[4 * n_par + 1:]
        x, y, c = _mesh_pos()
        me = 4 * x + 2 * y + c
        snd[...] = jnp.zeros_like(snd)
        for row, ref in ((0, dg1), (1, dba), (2, dbb), (3, dps), (4, dg2), (5, dgf), (6, lc)):
            snd[row:row + 1, :] = ref[...]
        snd[7:8, :n_conv] = dcb[...]
        snd[8:11, :n_conv] = dcw[...]
        cps = []
        for r in range(1, N_DEV):
            peer = tuple(1 - p if (r >> b) & 1 else p for p, b in ((x, 2), (y, 1), (c, 0)))
            cps.append(_remote(snd, got.at[me], ssem.at[r - 1], rsem.at[r - 1], peer))
        for cp in cps:
            cp.start()
        got[me] = snd[...]
        for r in range(1, N_DEV):
            peer = tuple(1 - p if (r >> b) & 1 else p for p, b in ((x, 2), (y, 1), (c, 0)))
            _remote(snd, got.at[4 * peer[0] + 2 * peer[1] + peer[2]], ssem.at[r - 1], rsem.at[r - 1], peer).wait_recv()
        for cp in cps:
            cp.wait_send()
        tot = got[0]
        for dev in range(1, N_DEV):
            tot = tot + got[dev]
        loss_ref[...] = jnp.sum(tot[6:7, :], axis=1, keepdims=True)
        k_me = 2 * x + y
        g_cw = jnp.zeros((3, ncw), F32)
        for k in range(N_CHIPS):
            g_cw = g_cw + jnp.where(k_me == k, tot[8:11, k * ncw:(k + 1) * ncw], 0.0)
        grads = [tot[0:1, :], jnp.concatenate([tot[1:2, :], tot[2:3, :]], axis=1), g_cw, tot[7:8, :n_conv],
                 tot[3:4, :], tot[4:5, :], tot[5:6, :]]
        for i, g in enumerate(grads):
            w_ref, m_ref, v_ref = wmv[3 * i:3 * i + 3]
            delta, nm, nv = _adamw_math(w_ref[...], g, m_ref[...], v_ref[...])
            outs[4 * i][...] = g
            outs[4 * i + 1][...] = delta
            outs[4 * i + 2][...] = nm
            outs[4 * i + 3][...] = nv

    args = list(parts)
    out_shape = []
    for w, m, v in params:
        args += [w, m, v]
        out_shape += [jax.ShapeDtypeStruct(w.shape, F32)] * 4
    out_shape.append(jax.ShapeDtypeStruct((1, 1), F32))
    return pl.pallas_call(
        body, name="vector_params_step", in_specs=[VMEM] * len(args) + [ANY] * len(deps),
        out_specs=[VMEM] * len(out_shape), out_shape=out_shape,
        scratch_shapes=[pltpu.VMEM((VEC_ROWS, d), F32), pltpu.VMEM((N_DEV, VEC_ROWS, d), F32),
                        pltpu.SemaphoreType.DMA((N_DEV - 1,)), pltpu.SemaphoreType.DMA((N_DEV - 1,))],
        compiler_params=pltpu.CompilerParams(vmem_limit_bytes=VMEM_LIMIT),
    )(*args, *deps)


def kernel(x, norm1_g, w_in, b_gate, conv_w, conv_b, w_a_out, w_pool, pool_scale, w_o, norm2_g, w_ffn_gate, w_ffn_up, w_ffn_down, final_g, loss_target, m_norm1_g, m_w_in, m_b_gate, m_conv_w, m_conv_b, m_w_a_out, m_w_pool, m_pool_scale, m_w_o, m_norm2_g, m_w_ffn_gate, m_w_ffn_up, m_w_ffn_down, m_final_g, v_norm1_g, v_w_in, v_b_gate, v_conv_w, v_conv_b, v_w_a_out, v_w_pool, v_pool_scale, v_w_o, v_norm2_g, v_w_ffn_gate, v_w_ffn_up, v_w_ffn_down, v_final_g):
    t, d = x.shape[1], x.shape[2]
    n_conv = conv_b.shape[1]
    n_groups, pool_cg, pool_dg = w_pool.shape[1], w_pool.shape[2], N_CHIPS * w_pool.shape[3]
    d_ff = N_CHIPS * w_ffn_gate.shape[2]
    assert n_conv // n_groups == pool_cg and n_conv % (n_groups * MIX_COLS) == 0 and n_groups == len(POOL_WINDOWS)

    big = {"w_in": (w_in, m_w_in, v_w_in), "w_a_out": (w_a_out, m_w_a_out, v_w_a_out), "w_pool": (w_pool, m_w_pool, v_w_pool),
           "w_o": (w_o, m_w_o, v_w_o), "w_ffn_gate": (w_ffn_gate, m_w_ffn_gate, v_w_ffn_gate),
           "w_ffn_up": (w_ffn_up, m_w_ffn_up, v_w_ffn_up), "w_ffn_down": (w_ffn_down, m_w_ffn_down, v_w_ffn_down)}
    colshard = {"w_in": True, "w_a_out": True, "w_pool": True, "w_o": False, "w_ffn_gate": True, "w_ffn_up": True,
                "w_ffn_down": False}
    names = list(big)
    shard2d = {n: big[n][0].reshape(-1, big[n][0].shape[-1]) for n in names}
    ws = [_Weight(n, *shard2d[n].shape, colshard[n]) for n in names]

    xs, tgt = x[0], loss_target[0]
    cw_loc = conv_w[0]
    pos = jnp.stack([lax.axis_index("c"), 2 * lax.axis_index("x") + lax.axis_index("y")]).astype(jnp.int32)
    by_name = {w.name: w for w in ws}
    groups = [[by_name[n] for n in g] for g in (["w_in"], ["w_a_out", "w_pool", "w_o"], ["w_ffn_gate"], ["w_ffn_up"],
                                                 ["w_ffn_down"])]
    first = [sum(len(g) for g in groups[:i]) for i in range(len(groups))]
    rgroups = [groups[0], groups[1], groups[2] + groups[3], groups[4]]

    cw_full = _gather_conv_w(cw_loc)
    cast = lambda w, dep: _cast_place(f"cast_{w.name}", w, pos, shard2d[w.name].reshape(2, w.R, w.nn), deps=[dep])
    sems_a, lands_a, tok_a = _gather_start("gather_start_a", groups[:1], [cast(w, cw_full) for w in groups[0]])
    rest = [cast(w, tok_a) for grp in groups[1:] for w in grp]
    full = {}

    def landed(g, after):
        return _gather_wait(f"gather_wait_{g}", groups[g], lands[first[g]:first[g] + len(groups[g])], *gsems[g], after)

    def pass_start(g, got, after=()):
        return _split_start(f"pass_start_{g}", got, 3 * len(got), _pass_copies(groups[g]), after)

    def pass_wait(g, started, after):
        ssem, rsem, got, _ = started
        got = _split_wait(f"pass_wait_{g}", got, ssem, rsem, _pass_copies(groups[g]), after)
        full.update({w.name: a.reshape(w.P * 2 * w.R, w.N) for w, a in zip(groups[g], got)})

    got = _gather_wait("gather_wait_0", groups[0], lands_a, *sems_a[0], rest[-1])
    sems_b, lands_b, tok_b = _gather_start("gather_start_b", groups[1:3], rest[:first[3] - 1], after=got)
    gsems, lands = sems_a + sems_b, lands_a + lands_b
    st = pass_start(0, got, after=[tok_b])
    h1 = _rms_fwd("norm1_fwd", xs, norm1_g, deps=[st[3]])
    pass_wait(0, st, [h1])
    w_in_full = full["w_in"]
    proj = _mm_nn("proj_in", h1, w_in_full, BF16)
    st = pass_start(1, landed(1, proj))
    z, p = _mixer_fwd("mixer_fwd", proj, cw_full, conv_b, n_conv, n_groups, deps=[st[3]])
    pass_wait(1, st, [z])
    wp_full = full["w_pool"].reshape(n_groups, pool_cg, pool_dg)
    ya = _mm_nn("conv_out", z, full["w_a_out"], BF16)
    yb = _gmm_nn("pool_out", p, wp_full, BF16)
    merged = _merge_fwd("merge_fwd", proj, b_gate, ya, yb, pool_scale)
    got_g = landed(2, merged)
    sems_c, lands_c, tok_c = _gather_start("gather_start_c", groups[3:], rest[first[3] - 1:], after=got_g)
    gsems, lands = gsems + sems_c, lands + lands_c
    st_g = pass_start(2, got_g, after=[tok_c])
    x2 = _mm_nn("mix_out", merged, full["w_o"], F32, add=xs, deps=[st_g[3]])
    pass_wait(2, st_g, [x2])
    h2 = _rms_fwd("norm2_fwd", x2, norm2_g)
    gate = _mm_nn("ffn_gate", h2, full["w_ffn_gate"], BF16)
    st_u = pass_start(3, landed(3, gate))
    pass_wait(3, st_u, [st_u[3]])
    up, act = _ffn_up_act("ffn_up_act", h2, full["w_ffn_up"], gate)
    st_d = pass_start(4, landed(4, act))
    pass_wait(4, st_d, [st_d[3]])
    x3 = _mm_nn("ffn_down", act, full["w_ffn_down"], F32, add=x2, tk=d_ff // 4)

    pending = {}

    def pair_start(g, grads):
        grp = rgroups[g]
        gcan = [grads[w.name].reshape(w.P, 2, w.R, w.N) for w in grp]
        slots = [lax.empty((w.P, w.R, w.N), BF16) for w in grp]
        pending[g] = _split_start(f"pair_start_{g}", gcan + slots, len(grp), _pair_copies(len(grp)))
        return pending[g][3]

    def scatter_start(g, after):
        grp = rgroups[g]
        n = len(grp)
        ssem, rsem, arrs, _ = pending[g]
        arrs = _split_wait(f"pair_wait_{g}", arrs, ssem, rsem, _pair_copies(n), after)
        gcan, sib = arrs[:n], arrs[n:]
        pairs = [_pair_sum(f"pair_sum_{w.name}", w, pos, a, s) for w, a, s in zip(grp, gcan, sib)]
        ssem, rsem, pairs, slots, token = _scatter_start(f"scatter_start_{g}", grp, pairs)
        pending[g] = (gcan, sib, pairs, slots, ssem, rsem)
        return token

    def reduce_finish(g, after):
        grp = rgroups[g]
        gcan, sib, pairs, slots, ssem, rsem = pending[g]
        parts = _scatter_wait(f"scatter_wait_{g}", grp, pairs, slots, ssem, rsem, after)
        return [_final_sum(f"final_sum_{w.name}", w, pos, a, s, q) for w, a, s, q in zip(grp, gcan, sib, parts)]

    grads = {}
    dx3, dx3b, d_gf, loss_cols = _final_bwd("final_bwd", x3, final_g.reshape(1, d), tgt)
    dgate, dup = _ffn_bwd("ffn_bwd", dx3b, full["w_ffn_down"], gate, up)
    grads["w_ffn_down"] = _mm_tn("dw_ffn_down", act, dx3b, BF16)
    tok = pair_start(3, grads)
    dh2 = _mm_nt("d_h2", [(dgate, full["w_ffn_gate"]), (dup, full["w_ffn_up"])], F32, tk=d_ff // 4, deps=[tok])
    tok = scatter_start(3, [dh2])
    grads["w_ffn_gate"] = _mm_tn("dw_ffn_gate", h2, dgate, BF16, deps=[tok])
    grads["w_ffn_up"] = _mm_tn("dw_ffn_up", h2, dup, BF16)
    tok = pair_start(2, grads)
    dx2, dx2b, d_g2 = _rms_bwd("norm2_bwd", x2, norm2_g, dh2, dx3, True, deps=[tok])
    dmerged = _mm_nt("d_merged", [(dx2b, full["w_o"])], BF16, tk=d)
    grads["w_o"] = _mm_tn("dw_o", merged, dx2b, BF16)
    tok = scatter_start(2, [grads["w_o"]])
    dya, dyb, dproj, d_bga, d_bgb, d_ps = _merge_bwd("merge_bwd", dmerged, proj, b_gate, ya, yb, pool_scale, deps=[tok])
    dz = _mm_nt("d_z", [(dya, full["w_a_out"])], BF16, tk=d)
    grads["w_a_out"] = _mm_tn("dw_a_out", z, dya, BF16)
    dp = _gmm_nt("d_pool", dyb, wp_full, BF16)
    grads["w_pool"] = _gmm_tn("dw_pool", p, dyb, n_groups, BF16)
    tok = pair_start(1, grads)
    dproj, d_cw, d_cb = _mixer_bwd("mixer_bwd", dz, dp, proj, cw_full, conv_b, dproj, n_conv, n_groups, deps=[tok])
    tok = scatter_start(1, [dproj])
    grads["w_in"] = _mm_tn("dw_in", h1, dproj, BF16, deps=[tok])
    tok = pair_start(0, grads)
    dh1 = _mm_nt("d_h1", [(dproj, w_in_full)], F32, tk=proj.shape[1] // 4, deps=[tok])
    tok = scatter_start(0, [dh1])
    grad_x, d_g1 = _rms_bwd("norm1_bwd", xs, norm1_g, dh1, dx2, False, deps=[tok])

    g_big, d_big, m_big, v_big = {}, {}, {}, {}

    def update(wsub, shared):
        out = []
        for w, g in zip(wsub, shared):
            wt, mt, vt = big[w.name]
            g2 = g.reshape(2 * w.R, w.nn)
            go, dl, nm, nv = _adamw(f"adamw_{w.name}", shard2d[w.name], g2, mt.reshape(g2.shape), vt.reshape(g2.shape))
            g_big[w.name], d_big[w.name], m_big[w.name], v_big[w.name] = (a.reshape(wt.shape) for a in (go, dl, nm, nv))
            out.append(nv)
        return out

    after = [grad_x]
    started = []
    for g in (3, 2, 1):
        halves = reduce_finish(g, after)
        share = _share_copies(len(halves))
        ssem, rsem, halves, tok = _split_start(f"share_start_{g}", halves, len(halves), share)
        started.append((g, ssem, rsem, halves, share))
        after = [tok]
    for g, ssem, rsem, halves, share in started:
        after = update(rgroups[g], _split_wait(f"share_wait_{g}", halves, ssem, rsem, share, after))
    after = update(rgroups[0], _share_halves("share_halves_w_in", rgroups[0], reduce_finish(0, after)))

    vec_names = ["norm1_g", "b_gate", "conv_w", "conv_b", "pool_scale", "norm2_g", "final_g"]
    vec = {"norm1_g": (norm1_g, m_norm1_g, v_norm1_g), "b_gate": (b_gate, m_b_gate, v_b_gate),
           "conv_w": (cw_loc, m_conv_w[0], v_conv_w[0]), "conv_b": (conv_b, m_conv_b, v_conv_b),
           "pool_scale": (pool_scale, m_pool_scale, v_pool_scale), "norm2_g": (norm2_g, m_norm2_g, v_norm2_g),
           "final_g": tuple(a.reshape(1, d) for a in (final_g, m_final_g, v_final_g))}
    vout = _vector_step(d, n_conv, [d_g1, d_bga, d_bgb, d_cw, d_cb, d_ps, d_g2, d_gf, loss_cols],
                        [vec[n] for n in vec_names], deps=after)

    shapes = {"conv_w": conv_w.shape, "final_g": final_g.shape}
    g_vec, d_vec, m_vec, v_vec = ({n: vout[4 * i + q].reshape(shapes.get(n, vec[n][0].shape)) for i, n in enumerate(vec_names)}
                                  for q in range(4))
    loss = vout[-1].reshape(())

    order = ["norm1_g", "w_in", "b_gate", "conv_w", "conv_b", "w_a_out", "w_pool", "pool_scale", "w_o", "norm2_g",
             "w_ffn_gate", "w_ffn_up", "w_ffn_down", "final_g"]
    pick = lambda vecs, bigs: [vecs[n] if n in vecs else bigs[n] for n in order]
    return (loss, grad_x.reshape(x.shape), *pick(g_vec, g_big), *pick(d_vec, d_big), *pick(m_vec, m_big),
            *pick(v_vec, v_big))
```

```python
import functools

import jax
import jax.numpy as jnp
from jax import lax
from jax.experimental import pallas as pl
from jax.experimental.pallas import tpu as pltpu

F32, BF16 = jnp.float32, jnp.bfloat16
MESH = pl.DeviceIdType.MESH
ANY = pl.BlockSpec(memory_space=pl.ANY)
VMEM = pl.BlockSpec(memory_space=pltpu.VMEM)
HBM = pl.BlockSpec(memory_space=pltpu.HBM)
SEM = pl.BlockSpec(memory_space=pltpu.SEMAPHORE)
EFFECT = pltpu.SideEffectType.DATAFLOW_SIDE_EFFECTING

EPS = 1e-6
POOL_WINDOWS = (2, 4, 8, 16)
ADAM_LR, ADAM_B1, ADAM_B2, ADAM_EPS, ADAM_WD, ADAM_STEP = 0.001, 0.9, 0.999, 1e-08, 0.01, 10

V7X_VMEM_BYTES = 64 * 1024 * 1024
VMEM_LIMIT = V7X_VMEM_BYTES * 3 // 4
LANES = 128
N_CHIPS = 4
N_DEV = 8

_DIMS = {
    "nn": (((1,), (0,)), ((), ())),
    "nt": (((1,), (1,)), ((), ())),
    "tn": (((0,), (0,)), ((), ())),
}


def _cp(sem):
    return pltpu.CompilerParams(dimension_semantics=sem, vmem_limit_bytes=VMEM_LIMIT)


def _mesh_pos():
    return lax.axis_index("x"), lax.axis_index("y"), lax.axis_index("c")


def _mm(name, pairs, *, mode, grid, out_shape, o_spec, nk=1, kaxis=None, add=None, deps=()):
    npair = len(pairs)
    has_add = add is not None

    def body(*refs):
        ab = refs[: 2 * npair]
        pos = 2 * npair
        add_ref = refs[pos] if has_add else None
        pos += int(has_add) + len(deps)
        o_ref = refs[pos]
        acc_ref = refs[pos + 1] if nk > 1 else None
        d = None
        for p in range(npair):
            t = lax.dot_general(ab[2 * p][...], ab[2 * p + 1][...], _DIMS[mode], preferred_element_type=F32)
            d = t if d is None else d + t
        if nk == 1:
            if has_add:
                d = d + add_ref[...].astype(F32)
            o_ref[...] = d.astype(o_ref.dtype)
        else:
            k = pl.program_id(kaxis)

            @pl.when(k == 0)
            def _():
                acc_ref[...] = d

            @pl.when(k > 0)
            def _():
                acc_ref[...] += d

            @pl.when(k == nk - 1)
            def _():
                r = acc_ref[...]
                if has_add:
                    r = r + add_ref[...].astype(F32)
                o_ref[...] = r.astype(o_ref.dtype)

    args, specs = [], []
    for a, a_spec, b, b_spec in pairs:
        args += [a, b]
        specs += [a_spec, b_spec]
    if has_add:
        args.append(add[0])
        specs.append(add[1])
    args += list(deps)
    specs += [ANY] * len(deps)
    scratch = []
    if nk > 1:
        blk = [d for d in o_spec.block_shape if d is not None]
        scratch = [pltpu.VMEM(tuple(blk), F32)]
    sem = tuple("arbitrary" if (nk > 1 and ax == kaxis) else "parallel" for ax in range(len(grid)))
    return pl.pallas_call(
        body, name=name, grid=grid, in_specs=specs, out_specs=o_spec, out_shape=out_shape,
        scratch_shapes=scratch, compiler_params=_cp(sem),
    )(*args)


def _tile(n, pref):
    if n <= pref:
        return n
    for t in range(pref, 0, -LANES):
        if t % LANES == 0 and n % t == 0:
            return t
    raise ValueError(f"no tile for {n}")


def _mm_nn(name, a, b, out_dtype, add=None, tk=None, deps=()):
    m, kk = a.shape
    n = b.shape[1]
    tm, tn = _tile(m, 1024), _tile(n, 512)
    out_shape = jax.ShapeDtypeStruct((m, n), out_dtype)
    if tk is None or tk == kk:
        grid = (m // tm, n // tn)
        pairs = [(a, pl.BlockSpec((tm, kk), lambda i, j: (i, 0)), b, pl.BlockSpec((kk, tn), lambda i, j: (0, j)))]
        o_spec = pl.BlockSpec((tm, tn), lambda i, j: (i, j))
        add_ = None if add is None else (add, pl.BlockSpec((tm, tn), lambda i, j: (i, j)))
        return _mm(name, pairs, mode="nn", grid=grid, out_shape=out_shape, o_spec=o_spec, add=add_, deps=deps)
    tn = _tile(n, 1024)
    nk = kk // tk
    grid = (m // tm, n // tn, nk)
    pairs = [(a, pl.BlockSpec((tm, tk), lambda i, j, k: (i, k)), b, pl.BlockSpec((tk, tn), lambda i, j, k: (k, j)))]
    o_spec = pl.BlockSpec((tm, tn), lambda i, j, k: (i, j))
    add_ = None if add is None else (add, pl.BlockSpec((tm, tn), lambda i, j, k: (i, j)))
    return _mm(name, pairs, mode="nn", grid=grid, out_shape=out_shape, o_spec=o_spec, nk=nk, kaxis=2, add=add_, deps=deps)


def _mm_nt(name, abs_, out_dtype, tk, deps=()):
    m, kk = abs_[0][0].shape
    n = abs_[0][1].shape[0]
    tm = _tile(m, 1024)
    nk = kk // tk
    tn = _tile(n, 512 if nk == 1 else 1024)
    out_shape = jax.ShapeDtypeStruct((m, n), out_dtype)
    if nk == 1:
        grid = (m // tm, n // tn)
        pairs = [(a, pl.BlockSpec((tm, kk), lambda i, j: (i, 0)), b, pl.BlockSpec((tn, kk), lambda i, j: (j, 0)))
                 for a, b in abs_]
        o_spec = pl.BlockSpec((tm, tn), lambda i, j: (i, j))
        return _mm(name, pairs, mode="nt", grid=grid, out_shape=out_shape, o_spec=o_spec, deps=deps)
    grid = (m // tm, n // tn, nk)
    pairs = [(a, pl.BlockSpec((tm, tk), lambda i, j, k: (i, k)), b, pl.BlockSpec((tn, tk), lambda i, j, k: (j, k)))
             for a, b in abs_]
    o_spec = pl.BlockSpec((tm, tn), lambda i, j, k: (i, j))
    return _mm(name, pairs, mode="nt", grid=grid, out_shape=out_shape, o_spec=o_spec, nk=nk, kaxis=2, deps=deps)


def _mm_tn(name, a, b, out_dtype, deps=()):
    t, m = a.shape
    n = b.shape[1]
    tm, tn = _tile(m, 512), _tile(n, 2048)
    if n > m:
        grid = (n // tn, m // tm)
        a_map, b_map, o_map = (lambda j, i: (0, i)), (lambda j, i: (0, j)), (lambda j, i: (i, j))
    else:
        grid = (m // tm, n // tn)
        a_map, b_map, o_map = (lambda i, j: (0, i)), (lambda i, j: (0, j)), (lambda i, j: (i, j))
    pairs = [(a, pl.BlockSpec((t, tm), a_map), b, pl.BlockSpec((t, tn), b_map))]
    o_spec = pl.BlockSpec((tm, tn), o_map)
    return _mm(name, pairs, mode="tn", grid=grid, out_shape=jax.ShapeDtypeStruct((m, n), out_dtype), o_spec=o_spec,
               deps=deps)


def _gmm_nn(name, p, w, out_dtype):
    t = p.shape[0]
    g, cg, dg = w.shape
    tm = _tile(t, 1024)
    pairs = [(p, pl.BlockSpec((tm, cg), lambda i, j: (i, j)), w, pl.BlockSpec((None, cg, dg), lambda i, j: (j, 0, 0)))]
    o_spec = pl.BlockSpec((tm, dg), lambda i, j: (i, j))
    return _mm(name, pairs, mode="nn", grid=(t // tm, g), out_shape=jax.ShapeDtypeStruct((t, g * dg), out_dtype),
               o_spec=o_spec)


def _gmm_nt(name, dy, w, out_dtype):
    t = dy.shape[0]
    g, cg, dg = w.shape
    tm = _tile(t, 1024)
    pairs = [(dy, pl.BlockSpec((tm, dg), lambda i, j: (i, j)), w, pl.BlockSpec((None, cg, dg), lambda i, j: (j, 0, 0)))]
    o_spec = pl.BlockSpec((tm, cg), lambda i, j: (i, j))
    return _mm(name, pairs, mode="nt", grid=(t // tm, g), out_shape=jax.ShapeDtypeStruct((t, g * cg), out_dtype),
               o_spec=o_spec)


def _gmm_tn(name, p, dy, g, out_dtype):
    t = p.shape[0]
    cg, dg = p.shape[1] // g, dy.shape[1] // g
    pairs = [(p, pl.BlockSpec((t, cg), lambda j: (0, j)), dy, pl.BlockSpec((t, dg), lambda j: (0, j)))]
    o_spec = pl.BlockSpec((None, cg, dg), lambda j: (j, 0, 0))
    return _mm(name, pairs, mode="tn", grid=(g,), out_shape=jax.ShapeDtypeStruct((g, cg, dg), out_dtype), o_spec=o_spec)


ROW_TILE = 256


def _rows(t):
    return _tile8(t, ROW_TILE)


def _tile8(n, pref):
    if n <= pref:
        return n
    for t in range(pref, 0, -8):
        if n % t == 0:
            return t
    raise ValueError(f"no row tile for {n}")


def _cast_place(name, w, pos, shard, deps=()):
    tr = _tile8(w.R, 512)
    if w.colshard:
        o_map = lambda h, i, pos: (0, h, i, pos[1])
    else:
        o_map = lambda h, i, pos: (pos[1], h, i, 0)

    def body(pos_ref, w_ref, *rest):
        rest[-1][...] = w_ref[...].astype(BF16)

    grid_spec = pltpu.PrefetchScalarGridSpec(
        num_scalar_prefetch=1, grid=(2, w.R // tr),
        in_specs=[pl.BlockSpec((None, tr, w.nn), lambda h, i, pos: (h, i, 0))] + [ANY] * len(deps),
        out_specs=pl.BlockSpec((None, None, tr, w.nn), o_map))
    return pl.pallas_call(body, name=name, grid_spec=grid_spec, out_shape=jax.ShapeDtypeStruct((w.P, 2, w.R, w.N), BF16),
                          compiler_params=_cp(("parallel", "parallel")))(pos, shard, *deps)


def _rms_fwd(name, x, g, deps=()):
    t, d = x.shape
    tm = _rows(t)

    def body(x_ref, g_ref, *rest):
        xf = x_ref[...]
        r = lax.rsqrt(jnp.mean(xf * xf, axis=-1, keepdims=True) + EPS)
        rest[-1][...] = (xf * r * g_ref[...]).astype(BF16)

    return pl.pallas_call(
        body, name=name, grid=(t // tm,),
        in_specs=[pl.BlockSpec((tm, d), lambda i: (i, 0)), pl.BlockSpec((1, d), lambda i: (0, 0))] + [ANY] * len(deps),
        out_specs=pl.BlockSpec((tm, d), lambda i: (i, 0)), out_shape=jax.ShapeDtypeStruct((t, d), BF16),
        compiler_params=_cp(("parallel",)),
    )(x, g, *deps)


def _rms_bwd(name, x, g, dh, dres, want_bf16, deps=()):
    t, d = x.shape
    tm = _rows(t)

    def body(x_ref, g_ref, dh_ref, dres_ref, *rest):
        rest = rest[len(deps):]
        dx_ref, rest = rest[0], rest[1:]
        dg_ref = rest[-1]
        xf = x_ref[...]
        r = lax.rsqrt(jnp.mean(xf * xf, axis=-1, keepdims=True) + EPS)
        xh = xf * r
        dhf = dh_ref[...]
        dxh = dhf * g_ref[...]
        m = jnp.mean(dxh * xh, axis=-1, keepdims=True)
        dx = dres_ref[...] + r * (dxh - xh * m)
        dx_ref[...] = dx
        if want_bf16:
            rest[0][...] = dx.astype(BF16)

        @pl.when(pl.program_id(0) == 0)
        def _():
            dg_ref[...] = jnp.zeros_like(dg_ref)

        dg_ref[...] += jnp.sum(dhf * xh, axis=0, keepdims=True)

    row = pl.BlockSpec((tm, d), lambda i: (i, 0))
    vec = pl.BlockSpec((1, d), lambda i: (0, 0))
    out_specs = [row] + ([row] if want_bf16 else []) + [vec]
    out_shape = ([jax.ShapeDtypeStruct((t, d), F32)] + ([jax.ShapeDtypeStruct((t, d), BF16)] if want_bf16 else [])
                 + [jax.ShapeDtypeStruct((1, d), F32)])
    return pl.pallas_call(body, name=name, grid=(t // tm,), in_specs=[row, vec, row, row] + [ANY] * len(deps),
                          out_specs=out_specs, out_shape=out_shape, compiler_params=_cp(("arbitrary",)))(x, g, dh, dres, *deps)


def _final_bwd(name, x3, gf, tgt):
    t, d = x3.shape
    tm = _rows(t)

    def body(x_ref, g_ref, t_ref, dx_ref, dxb_ref, dg_ref, lc_ref):
        xf = x_ref[...]
        g = g_ref[...]
        r = lax.rsqrt(jnp.mean(xf * xf, axis=-1, keepdims=True) + EPS)
        xh = xf * r
        diff = xh * g - t_ref[...]
        dy = diff * (1.0 / d)
        dxh = dy * g
        m = jnp.mean(dxh * xh, axis=-1, keepdims=True)
        dx = r * (dxh - xh * m)
        dx_ref[...] = dx
        dxb_ref[...] = dx.astype(BF16)

        @pl.when(pl.program_id(0) == 0)
        def _():
            dg_ref[...] = jnp.zeros_like(dg_ref)
            lc_ref[...] = jnp.zeros_like(lc_ref)

        dg_ref[...] += jnp.sum(dy * xh, axis=0, keepdims=True)
        lc_ref[...] += jnp.sum(diff * diff, axis=0, keepdims=True) * (0.5 / d)

    row = pl.BlockSpec((tm, d), lambda i: (i, 0))
    vec = pl.BlockSpec((1, d), lambda i: (0, 0))
    return pl.pallas_call(
        body, name=name, grid=(t // tm,), in_specs=[row, vec, row], out_specs=[row, row, vec, vec],
        out_shape=[jax.ShapeDtypeStruct((t, d), F32), jax.ShapeDtypeStruct((t, d), BF16),
                   jax.ShapeDtypeStruct((1, d), F32), jax.ShapeDtypeStruct((1, d), F32)],
        compiler_params=_cp(("arbitrary",)),
    )(x3, gf, tgt)


def _shift_down(v, k, t_idx):
    return jnp.where(t_idx >= k, pltpu.roll(v, k, 0), 0.0)


def _shift_up(v, k, t_idx):
    n = v.shape[0]
    return jnp.where(t_idx < n - k, pltpu.roll(v, n - k, 0), 0.0)


def _window_sums(v, shift, t_idx, grp):
    s = v + shift(v, 1, t_idx)
    out = s
    for lvl in range(1, len(POOL_WINDOWS)):
        s = s + shift(s, 1 << lvl, t_idx)
        out = jnp.where(grp >= lvl, s, out)
    return out


def _window_count(t_idx, grp):
    return jnp.minimum(t_idx + 1, jnp.left_shift(2, grp)).astype(F32)


MIX_COLS = 128


def _mixer_fwd(name, proj, cw, cb, n_conv, n_groups, deps=()):
    t = proj.shape[0]
    nb = n_conv // MIX_COLS
    per_group = n_conv // n_groups // MIX_COLS

    def body(ba_ref, ca_ref, va_ref, vb_ref, cw_ref, cb_ref, *rest):
        z_ref, p_ref = rest[len(deps):]
        t_idx = lax.broadcasted_iota(jnp.int32, (t, MIX_COLS), 0)
        q = ca_ref[...].astype(F32) * va_ref[...].astype(F32)
        w = cw_ref[...]
        u = cb_ref[...] + w[0:1] * _shift_down(q, 2, t_idx) + w[1:2] * _shift_down(q, 1, t_idx) + w[2:3] * q
        z_ref[...] = (ba_ref[...].astype(F32) * u).astype(BF16)
        grp = pl.program_id(0) // per_group
        v = vb_ref[...].astype(F32)
        p_ref[...] = (_window_sums(v, _shift_down, t_idx, grp) / _window_count(t_idx, grp) - v).astype(BF16)

    col = lambda s: pl.BlockSpec((t, MIX_COLS), lambda j: (0, s * nb + j))
    return pl.pallas_call(
        body, name=name, grid=(nb,),
        in_specs=[col(0), col(1), col(2), col(3), pl.BlockSpec((3, MIX_COLS), lambda j: (0, j)),
                  pl.BlockSpec((1, MIX_COLS), lambda j: (0, j))] + [ANY] * len(deps),
        out_specs=[col(0), col(0)],
        out_shape=[jax.ShapeDtypeStruct((t, n_conv), BF16), jax.ShapeDtypeStruct((t, n_conv), BF16)],
        compiler_params=_cp(("parallel",)),
    )(proj, proj, proj, proj, cw, cb, *deps)


def _mixer_bwd(name, dz, dp, proj, cw, cb, dproj, n_conv, n_groups, deps=()):
    t = proj.shape[0]
    nb = n_conv // MIX_COLS
    per_group = n_conv // n_groups // MIX_COLS

    def body(dz_ref, dp_ref, ba_ref, ca_ref, va_ref, cw_ref, cb_ref, _, *rest):
        o_ref, dcw_ref, dcb_ref, scr = rest[len(deps):]
        s = pl.program_id(1)

        @pl.when(s == 0)
        def _():
            t_idx = lax.broadcasted_iota(jnp.int32, (t, MIX_COLS), 0)
            ca, va = ca_ref[...].astype(F32), va_ref[...].astype(F32)
            q = ca * va
            q1, q2 = _shift_down(q, 1, t_idx), _shift_down(q, 2, t_idx)
            w = cw_ref[...]
            u = cb_ref[...] + w[0:1] * q2 + w[1:2] * q1 + w[2:3] * q
            dzf = dz_ref[...].astype(F32)
            du = dzf * ba_ref[...].astype(F32)
            scr[0] = (dzf * u).astype(BF16)
            dq = w[2:3] * du + w[1:2] * _shift_up(du, 1, t_idx) + w[0:1] * _shift_up(du, 2, t_idx)
            scr[1] = (dq * va).astype(BF16)
            scr[2] = (dq * ca).astype(BF16)
            dcb_ref[...] = jnp.sum(du, axis=0, keepdims=True)
            dcw_ref[0:1, :] = jnp.sum(du * q2, axis=0, keepdims=True)
            dcw_ref[1:2, :] = jnp.sum(du * q1, axis=0, keepdims=True)
            dcw_ref[2:3, :] = jnp.sum(du * q, axis=0, keepdims=True)
            grp = pl.program_id(0) // per_group
            dpf = dp_ref[...].astype(F32)
            e = dpf / _window_count(t_idx, grp)
            scr[3] = (_window_sums(e, _shift_up, t_idx, grp) - dpf).astype(BF16)

        o_ref[...] = scr[s]

    col = lambda c: pl.BlockSpec((t, MIX_COLS), lambda j, s: (0, c * nb + j))
    own = pl.BlockSpec((t, MIX_COLS), lambda j, s: (0, j))
    return pl.pallas_call(
        body, name=name, grid=(nb, 4),
        in_specs=[own, own, col(0), col(1), col(2), pl.BlockSpec((3, MIX_COLS), lambda j, s: (0, j)),
                  pl.BlockSpec((1, MIX_COLS), lambda j, s: (0, j)), ANY] + [ANY] * len(deps),
        out_specs=[pl.BlockSpec((t, MIX_COLS), lambda j, s: (0, s * nb + j)),
                   pl.BlockSpec((3, MIX_COLS), lambda j, s: (0, j)), pl.BlockSpec((1, MIX_COLS), lambda j, s: (0, j))],
        out_shape=[jax.ShapeDtypeStruct(dproj.shape, BF16), jax.ShapeDtypeStruct((3, n_conv), F32),
                   jax.ShapeDtypeStruct((1, n_conv), F32)],
        scratch_shapes=[pltpu.VMEM((4, t, MIX_COLS), BF16)],
        input_output_aliases={7: 0},
        compiler_params=_cp(("arbitrary", "arbitrary")),
    )(dz, dp, proj, proj, proj, cw, cb, dproj, *deps)


def _merge_fwd(name, proj, bg, ya, yb, ps):
    t, d = ya.shape
    tm = _rows(t)

    def body(gab_ref, bg_ref, ya_ref, yb_ref, ps_ref, o_ref):
        gab = gab_ref[...].astype(F32) + bg_ref[...]
        sa, sb = jax.nn.sigmoid(gab[:, :d]), jax.nn.sigmoid(gab[:, d:])
        o_ref[...] = (sa * ya_ref[...].astype(F32) + sb * (yb_ref[...].astype(F32) * ps_ref[...])).astype(BF16)

    row = pl.BlockSpec((tm, d), lambda i: (i, 0))
    return pl.pallas_call(
        body, name=name, grid=(t // tm,),
        in_specs=[pl.BlockSpec((tm, 2 * d), lambda i: (i, 1)), pl.BlockSpec((1, 2 * d), lambda i: (0, 0)), row, row,
                  pl.BlockSpec((1, d), lambda i: (0, 0))],
        out_specs=row, out_shape=jax.ShapeDtypeStruct((t, d), BF16), compiler_params=_cp(("parallel",)),
    )(proj, bg, ya, yb, ps)


def _merge_bwd(name, dm, proj, bg, ya, yb, ps, deps=()):
    t, d = ya.shape
    tm = _rows(t)

    def body(dm_ref, gab_ref, bg_ref, ya_ref, yb_ref, ps_ref, *rest):
        dya_ref, dyb_ref, dg_ref, dba_ref, dbb_ref, dps_ref = rest[len(deps):]
        gab = gab_ref[...].astype(F32) + bg_ref[...]
        sa, sb = jax.nn.sigmoid(gab[:, :d]), jax.nn.sigmoid(gab[:, d:])
        dmf = dm_ref[...].astype(F32)
        ybf, ps_ = yb_ref[...].astype(F32), ps_ref[...]
        dya_ref[...] = (dmf * sa).astype(BF16)
        dyb = dmf * sb
        dyb_ref[...] = (dyb * ps_).astype(BF16)
        dga = dmf * ya_ref[...].astype(F32) * sa * (1.0 - sa)
        dgb = dmf * (ybf * ps_) * sb * (1.0 - sb)
        dg_ref[:, :d] = dga.astype(BF16)
        dg_ref[:, d:] = dgb.astype(BF16)

        @pl.when(pl.program_id(0) == 0)
        def _():
            dba_ref[...] = jnp.zeros_like(dba_ref)
            dbb_ref[...] = jnp.zeros_like(dbb_ref)
            dps_ref[...] = jnp.zeros_like(dps_ref)

        dba_ref[...] += jnp.sum(dga, axis=0, keepdims=True)
        dbb_ref[...] += jnp.sum(dgb, axis=0, keepdims=True)
        dps_ref[...] += jnp.sum(dyb * ybf, axis=0, keepdims=True)

    row = pl.BlockSpec((tm, d), lambda i: (i, 0))
    vec = pl.BlockSpec((1, d), lambda i: (0, 0))
    gates = pl.BlockSpec((tm, 2 * d), lambda i: (i, 1))
    return pl.pallas_call(
        body, name=name, grid=(t // tm,),
        in_specs=[row, gates, pl.BlockSpec((1, 2 * d), lambda i: (0, 0)), row, row, vec] + [ANY] * len(deps),
        out_specs=[row, row, gates, vec, vec, vec],
        out_shape=[jax.ShapeDtypeStruct((t, d), BF16), jax.ShapeDtypeStruct((t, d), BF16),
                   jax.ShapeDtypeStruct(proj.shape, BF16), jax.ShapeDtypeStruct((1, d), F32),
                   jax.ShapeDtypeStruct((1, d), F32), jax.ShapeDtypeStruct((1, d), F32)],
        compiler_params=_cp(("arbitrary",)),
    )(dm, proj, bg, ya, yb, ps, *deps)


def _ffn_up_act(name, h, w_up, gate):
    t, d = h.shape
    f = w_up.shape[1]
    tm, tf = _tile(t, 1024), _tile(f, 512)

    def body(h_ref, w_ref, g_ref, u_ref, a_ref):
        u = lax.dot_general(h_ref[...], w_ref[...], _DIMS["nn"], preferred_element_type=F32)
        g = g_ref[...].astype(F32)
        u_ref[...] = u.astype(BF16)
        a_ref[...] = (g * jax.nn.sigmoid(g) * u).astype(BF16)

    blk = pl.BlockSpec((tm, tf), lambda i, j: (i, j))
    shp = jax.ShapeDtypeStruct((t, f), BF16)
    return pl.pallas_call(
        body, name=name, grid=(t // tm, f // tf),
        in_specs=[pl.BlockSpec((tm, d), lambda i, j: (i, 0)), pl.BlockSpec((d, tf), lambda i, j: (0, j)), blk],
        out_specs=[blk, blk], out_shape=[shp, shp], compiler_params=_cp(("parallel", "parallel")))(h, w_up, gate)


def _ffn_bwd(name, dy, w_down, gate, up):
    t, d = dy.shape
    f = w_down.shape[0]
    tm, tf = _tile(t, 1024), _tile(f, 512)

    def body(dy_ref, w_ref, g_ref, u_ref, dg_ref, du_ref):
        da = lax.dot_general(dy_ref[...], w_ref[...], _DIMS["nt"], preferred_element_type=F32)
        g = g_ref[...].astype(F32)
        s = jax.nn.sigmoid(g)
        du_ref[...] = (da * (g * s)).astype(BF16)
        dg_ref[...] = (da * u_ref[...].astype(F32) * (s * (1.0 + g * (1.0 - s)))).astype(BF16)

    blk = pl.BlockSpec((tm, tf), lambda i, j: (i, j))
    shp = jax.ShapeDtypeStruct((t, f), BF16)
    return pl.pallas_call(
        body, name=name, grid=(t // tm, f // tf),
        in_specs=[pl.BlockSpec((tm, d), lambda i, j: (i, 0)), pl.BlockSpec((tf, d), lambda i, j: (j, 0)), blk, blk],
        out_specs=[blk, blk], out_shape=[shp, shp], compiler_params=_cp(("parallel", "parallel")))(dy, w_down, gate, up)


def _adamw_math(w, g, m, v):
    m = ADAM_B1 * m + (1.0 - ADAM_B1) * g
    v = ADAM_B2 * v + (1.0 - ADAM_B2) * (g * g)
    m_hat = m / (1.0 - ADAM_B1 ** ADAM_STEP)
    v_hat = v / (1.0 - ADAM_B2 ** ADAM_STEP)
    delta = -ADAM_LR * (m_hat / (jnp.sqrt(v_hat) + ADAM_EPS) + ADAM_WD * w)
    return delta, m, v


def _adamw(name, w, g, m, v):
    r, c = w.shape
    tr = _tile8(r, 512 if c <= 1024 else 256)

    def body(w_ref, g_ref, m_ref, v_ref, go_ref, d_ref, nm_ref, nv_ref):
        g = g_ref[...]
        go_ref[...] = g
        d_ref[...], nm_ref[...], nv_ref[...] = _adamw_math(w_ref[...], g, m_ref[...], v_ref[...])

    blk = pl.BlockSpec((tr, c), lambda i: (i, 0))
    shp = jax.ShapeDtypeStruct((r, c), F32)
    return pl.pallas_call(body, name=name, grid=(r // tr,), in_specs=[blk] * 4, out_specs=[blk] * 4,
                          out_shape=[shp] * 4, compiler_params=_cp(("parallel",)))(w, g, m, v)


class _Weight:
    def __init__(self, name, rows, cols, colshard):
        self.name, self.colshard = name, colshard
        self.R, self.nn = rows // 2, cols
        self.P = 1 if colshard else N_CHIPS
        self.N = N_CHIPS * cols if colshard else cols

    def cols(self, k):
        return pl.ds(pl.multiple_of(k * self.nn, LANES), self.nn)

    def shard(self, ref, k):
        return ref.at[0, :, :, self.cols(k)] if self.colshard else ref.at[k]

    def half(self, ref, k, h):
        return ref.at[0, h, :, self.cols(k)] if self.colshard else ref.at[k, h]

    def part(self, ref, k):
        return ref.at[0, :, self.cols(k)] if self.colshard else ref.at[k]


def _remote(src, dst, ssem, rsem, dev):
    return pltpu.make_async_remote_copy(src_ref=src, dst_ref=dst, send_sem=ssem, recv_sem=rsem, device_id=dev,
                                        device_id_type=MESH)


def _other_chips(x, y):
    chips = [(1 - x, y), (x, 1 - y), (1 - x, 1 - y)]
    return chips, [2 * cx + cy for cx, cy in chips]


def _hbm(a):
    return pltpu.with_memory_space_constraint(a, pltpu.HBM)


def _gather_start(name, groups, lands, after=()):
    flat = [w for grp in groups for w in grp]
    nw, ng = len(flat), len(groups)

    def body(*refs):
        land = refs[:nw]
        sems = refs[nw + len(after):nw + len(after) + 2 * ng]
        token = refs[2 * nw + len(after) + 2 * ng]
        x, y, c = _mesh_pos()
        k_me = 2 * x + y
        chips, _ = _other_chips(x, y)
        i = 0
        for g, grp in enumerate(groups):
            for wi, w in enumerate(grp):
                mine = w.half(land[i], k_me, c)
                for j, chip in enumerate(chips):
                    _remote(mine, mine, sems[2 * g].at[3 * wi + j], sems[2 * g + 1].at[3 * wi + j], (*chip, c)).start()
                i += 1
        token[...] = jnp.zeros_like(token)

    sem_shapes = []
    for grp in groups:
        sem_shapes += [pltpu.SemaphoreType.DMA((3 * len(grp),))] * 2
    out = pl.pallas_call(
        body, name=name, in_specs=[HBM] * nw + [ANY] * len(after),
        out_specs=[SEM] * (2 * ng) + [HBM] * nw + [VMEM],
        out_shape=sem_shapes + [pltpu.HBM(a.shape, a.dtype) for a in lands] + [jax.ShapeDtypeStruct((8, LANES), F32)],
        input_output_aliases={i: 2 * ng + i for i in range(nw)},
        compiler_params=pltpu.CompilerParams(has_side_effects=EFFECT),
    )(*[_hbm(a) for a in lands], *after)
    sems = [(out[2 * g], out[2 * g + 1]) for g in range(ng)]
    return sems, list(out[2 * ng:2 * ng + nw]), out[-1]


def _gather_wait(name, grp, lands, ssem, rsem, after):
    n = len(grp)

    def body(*refs):
        land, ssem_ref, rsem_ref = refs[:n], refs[n], refs[n + 1]
        x, y, c = _mesh_pos()
        k_me = 2 * x + y
        chips, ks = _other_chips(x, y)
        for wi, w in enumerate(grp):
            for j, chip in enumerate(chips):
                cp = _remote(w.half(land[wi], k_me, c), w.half(land[wi], ks[j], c), ssem_ref.at[3 * wi + j],
                             rsem_ref.at[3 * wi + j], (*chip, c))
                cp.wait_send()
                cp.wait_recv()

    return pl.pallas_call(
        body, name=name, in_specs=[HBM] * n + [SEM, SEM, ANY], out_specs=[HBM] * n,
        out_shape=[pltpu.HBM(a.shape, a.dtype) for a in lands], input_output_aliases={i: i for i in range(n)},
        compiler_params=pltpu.CompilerParams(has_side_effects=EFFECT),
    )(*lands, ssem, rsem, after)


def _split_start(name, arrays, n, copies, after=()):
    na = len(arrays)

    def body(*refs):
        ssem, rsem, token = refs[na + len(after):][0], refs[na + len(after):][1], refs[2 * na + len(after) + 2]
        for i, (src, dst, dev, _) in enumerate(copies(refs[:na], *_mesh_pos())):
            _remote(src, dst, ssem.at[i], rsem.at[i], dev).start()
        token[...] = jnp.zeros_like(token)

    out = pl.pallas_call(
        body, name=name, in_specs=[HBM] * na + [ANY] * len(after), out_specs=[SEM, SEM] + [HBM] * na + [VMEM],
        out_shape=[pltpu.SemaphoreType.DMA((n,))] * 2 + [pltpu.HBM(a.shape, a.dtype) for a in arrays]
        + [jax.ShapeDtypeStruct((8, LANES), F32)],
        input_output_aliases={i: 2 + i for i in range(na)},
        compiler_params=pltpu.CompilerParams(has_side_effects=EFFECT),
    )(*[_hbm(a) for a in arrays], *after)
    return out[0], out[1], list(out[2:2 + na]), out[-1]


def _split_wait(name, arrays, ssem, rsem, copies, after):
    na = len(arrays)

    def body(*refs):
        for i, (src, _, dev, dst) in enumerate(copies(refs[:na], *_mesh_pos())):
            cp = _remote(src, dst, refs[na].at[i], refs[na + 1].at[i], dev)
            cp.wait_send()
            cp.wait_recv()

    return list(pl.pallas_call(
        body, name=name, in_specs=[HBM] * na + [SEM, SEM] + [ANY] * len(after), out_specs=[HBM] * na,
        out_shape=[pltpu.HBM(a.shape, a.dtype) for a in arrays], input_output_aliases={i: i for i in range(na)},
        compiler_params=pltpu.CompilerParams(has_side_effects=EFFECT),
    )(*arrays, ssem, rsem, *after))


def _pass_copies(grp):
    def copies(land, x, y, c):
        _, ks = _other_chips(x, y)
        return [(w.half(land[wi], ks[j], c), w.half(land[wi], ks[j], c), (x, y, 1 - c), w.half(land[wi], ks[j], 1 - c))
                for wi, w in enumerate(grp) for j in range(3)]
    return copies


def _pair_copies(n):
    def copies(refs, x, y, c):
        return [(refs[i].at[:, 1 - c], refs[n + i], (x, y, 1 - c), refs[n + i]) for i in range(n)]
    return copies


def _share_copies(n):
    def copies(refs, x, y, c):
        return [(refs[i].at[c], refs[i].at[c], (x, y, 1 - c), refs[i].at[1 - c]) for i in range(n)]
    return copies


def _gather_conv_w(cw):
    ncw = cw.shape[1]

    def body(cw_ref, out_ref, ssem, rsem):
        x, y, c = _mesh_pos()
        k_me = 2 * x + y
        chips, ks = _other_chips(x, y)
        cols = lambda k: out_ref.at[:, pl.ds(pl.multiple_of(k * ncw, LANES), ncw)]
        cps = [_remote(cw_ref, cols(k_me), ssem.at[j], rsem.at[j], (*chip, c)) for j, chip in enumerate(chips)]
        for cp in cps:
            cp.start()
        for k in range(N_CHIPS):
            @pl.when(k_me == k)
            def _():
                out_ref[:, k * ncw:(k + 1) * ncw] = cw_ref[...]
        for j in range(3):
            _remote(cw_ref, cols(ks[j]), ssem.at[j], rsem.at[j], (*chips[j], c)).wait_recv()
        for cp in cps:
            cp.wait_send()

    return pl.pallas_call(
        body, name="gather_conv_w", in_specs=[VMEM], out_specs=VMEM,
        out_shape=jax.ShapeDtypeStruct((3, N_CHIPS * ncw), F32),
        scratch_shapes=[pltpu.SemaphoreType.DMA((3,)), pltpu.SemaphoreType.DMA((3,))],
    )(cw)


def _grad_tiles(w, n):
    return _tile8(w.R, 512) if w.R <= 512 else w.R // 2, _tile(n, 2048)


def _pair_sum(name, w, pos, grad, got):
    tr, tn = _grad_tiles(w, w.N)

    def body(pos_ref, g_ref, r_ref, o_ref):
        o_ref[...] = (g_ref[...].astype(F32) + r_ref[...].astype(F32)).astype(BF16)

    blk = pl.BlockSpec((None, tr, tn), lambda p, i, j, pos: (p, i, j))
    grid_spec = pltpu.PrefetchScalarGridSpec(
        num_scalar_prefetch=1, grid=(w.P, w.R // tr, w.N // tn),
        in_specs=[pl.BlockSpec((None, None, tr, tn), lambda p, i, j, pos: (p, pos[0], i, j)), blk], out_specs=blk)
    return pl.pallas_call(body, name=name, grid_spec=grid_spec, out_shape=jax.ShapeDtypeStruct((w.P, w.R, w.N), BF16),
                          compiler_params=_cp(("parallel",) * 3))(pos, grad, got)


def _scatter_start(name, ws, pairs):
    nw = len(ws)

    def body(*refs):
        pr, land = refs[:nw], refs[nw:2 * nw]
        ssem, rsem = refs[2 * nw], refs[2 * nw + 1]
        token = refs[4 * nw + 2]
        x, y, c = _mesh_pos()
        chips, ks = _other_chips(x, y)
        for i, w in enumerate(ws):
            for j, chip in enumerate(chips):
                _remote(w.part(pr[i], ks[j]), land[i].at[j], ssem.at[3 * i + j], rsem.at[3 * i + j], (*chip, c)).start()
        token[...] = jnp.zeros_like(token)

    lands = [lax.empty((3, w.R, w.nn), BF16) for w in ws]
    out = pl.pallas_call(
        body, name=name, in_specs=[HBM] * (2 * nw),
        out_specs=[SEM, SEM] + [HBM] * (2 * nw) + [VMEM],
        out_shape=[pltpu.SemaphoreType.DMA((3 * nw,))] * 2 + [pltpu.HBM(a.shape, a.dtype) for a in pairs + lands]
        + [jax.ShapeDtypeStruct((8, LANES), F32)],
        input_output_aliases={i: 2 + i for i in range(2 * nw)},
        compiler_params=pltpu.CompilerParams(has_side_effects=EFFECT),
    )(*[_hbm(a) for a in pairs + lands])
    return out[0], out[1], list(out[2:2 + nw]), list(out[2 + nw:2 + 2 * nw]), out[-1]


def _scatter_wait(name, ws, pairs, lands, ssem, rsem, after):
    nw = len(ws)

    def body(*refs):
        pr, land = refs[:nw], refs[nw:2 * nw]
        ssem_ref, rsem_ref = refs[2 * nw], refs[2 * nw + 1]
        x, y, c = _mesh_pos()
        chips, ks = _other_chips(x, y)
        for i, w in enumerate(ws):
            for j, chip in enumerate(chips):
                cp = _remote(w.part(pr[i], ks[j]), land[i].at[j], ssem_ref.at[3 * i + j], rsem_ref.at[3 * i + j], (*chip, c))
                cp.wait_send()
                cp.wait_recv()

    out = pl.pallas_call(
        body, name=name, in_specs=[HBM] * (2 * nw) + [SEM, SEM] + [ANY] * len(after), out_specs=[HBM] * (2 * nw),
        out_shape=[pltpu.HBM(a.shape, a.dtype) for a in pairs + lands],
        input_output_aliases={i: i for i in range(2 * nw)},
        compiler_params=pltpu.CompilerParams(has_side_effects=EFFECT),
    )(*pairs, *lands, ssem, rsem, *after)
    return list(out[nw:])


def _final_sum(name, w, pos, grad, got, parts):
    tr, tn = _grad_tiles(w, w.nn)
    nbc = w.nn // tn

    def body(pos_ref, g_ref, r_ref, p_ref, o_ref):
        acc = g_ref[...].astype(F32) + r_ref[...].astype(F32)
        for j in range(3):
            acc = acc + p_ref[j].astype(F32)
        o_ref[...] = acc

    if w.colshard:
        g_spec = pl.BlockSpec((None, None, tr, tn), lambda i, j, pos: (0, pos[0], i, pos[1] * nbc + j))
        r_spec = pl.BlockSpec((None, tr, tn), lambda i, j, pos: (0, i, pos[1] * nbc + j))
    else:
        g_spec = pl.BlockSpec((None, None, tr, tn), lambda i, j, pos: (pos[1], pos[0], i, j))
        r_spec = pl.BlockSpec((None, tr, tn), lambda i, j, pos: (pos[1], i, j))
    grid_spec = pltpu.PrefetchScalarGridSpec(
        num_scalar_prefetch=1, grid=(w.R // tr, nbc),
        in_specs=[g_spec, r_spec, pl.BlockSpec((3, tr, tn), lambda i, j, pos: (0, i, j))],
        out_specs=pl.BlockSpec((None, tr, tn), lambda i, j, pos: (pos[0], i, j)))
    return pl.pallas_call(body, name=name, grid_spec=grid_spec, out_shape=jax.ShapeDtypeStruct((2, w.R, w.nn), F32),
                          compiler_params=_cp(("parallel",) * 2))(pos, grad, got, parts)


def _share_halves(name, ws, halves, deps=()):
    nw = len(ws)

    def body(*refs):
        out = refs[nw + len(deps):2 * nw + len(deps)]
        ssem, rsem = refs[2 * nw + len(deps):]
        x, y, c = _mesh_pos()
        sib = (x, y, 1 - c)
        cps = [_remote(out[i].at[c], out[i].at[c], ssem.at[i], rsem.at[i], sib) for i in range(nw)]
        for cp in cps:
            cp.start()
        for i, cp in enumerate(cps):
            cp.wait_send()
            _remote(out[i].at[1 - c], out[i].at[1 - c], ssem.at[i], rsem.at[i], sib).wait_recv()

    return pl.pallas_call(
        body, name=name, in_specs=[ANY] * (nw + len(deps)), out_specs=[ANY] * nw,
        out_shape=[jax.ShapeDtypeStruct(h.shape, F32) for h in halves],
        scratch_shapes=[pltpu.SemaphoreType.DMA((nw,)), pltpu.SemaphoreType.DMA((nw,))],
        input_output_aliases={i: i for i in range(nw)},
    )(*halves, *deps)


VEC_ROWS = 16


def _vector_step(d, n_conv, parts, params, deps=()):
    ncw = params[2][0].shape[1]
    n_par = len(params)

    def body(*refs):
        dg1, dba, dbb, dcw, dcb, dps, dg2, dgf, lc = refs[:9]
        wmv = refs[9:9 + 3 * n_par]
        refs = refs[9 + 3 * n_par + len(deps):]
        outs = refs[:4 * n_par]
        loss_ref = refs[4 * n_par]
        snd, got, ssem, rsem = refs[4 * n_par + 1:]
        x, y, c = _mesh_pos()
        me = 4 * x + 2 * y + c
        snd[...] = jnp.zeros_like(snd)
        for row, ref in ((0, dg1), (1, dba), (2, dbb), (3, dps), (4, dg2), (5, dgf), (6, lc)):
            snd[row:row + 1, :] = ref[...]
        snd[7:8, :n_conv] = dcb[...]
        snd[8:11, :n_conv] = dcw[...]
        cps = []
        for r in range(1, N_DEV):
            peer = tuple(1 - p if (r >> b) & 1 else p for p, b in ((x, 2), (y, 1), (c, 0)))
            cps.append(_remote(snd, got.at[me], ssem.at[r - 1], rsem.at[r - 1], peer))
        for cp in cps:
            cp.start()
        got[me] = snd[...]
        for r in range(1, N_DEV):
            peer = tuple(1 - p if (r >> b) & 1 else p for p, b in ((x, 2), (y, 1), (c, 0)))
            _remote(snd, got.at[4 * peer[0] + 2 * peer[1] + peer[2]], ssem.at[r - 1], rsem.at[r - 1], peer).wait_recv()
        for cp in cps:
            cp.wait_send()
        tot = got[0]
        for dev in range(1, N_DEV):
            tot = tot + got[dev]
        loss_ref[...] = jnp.sum(tot[6:7, :], axis=1, keepdims=True)
        k_me = 2 * x + y
        g_cw = jnp.zeros((3, ncw), F32)
        for k in range(N_CHIPS):
            g_cw = g_cw + jnp.where(k_me == k, tot[8:11, k * ncw:(k + 1) * ncw], 0.0)
        grads = [tot[0:1, :], jnp.concatenate([tot[1:2, :], tot[2:3, :]], axis=1), g_cw, tot[7:8, :n_conv],
                 tot[3:4, :], tot[4:5, :], tot[5:6, :]]
        for i, g in enumerate(grads):
            w_ref, m_ref, v_ref = wmv[3 * i:3 * i + 3]
            delta, nm, nv = _adamw_math(w_ref[...], g, m_ref[...], v_ref[...])
            outs[4 * i][...] = g
            outs[4 * i + 1][...] = delta
            outs[4 * i + 2][...] = nm
            outs[4 * i + 3][...] = nv

    args = list(parts)
    out_shape = []
    for w, m, v in params:
        args += [w, m, v]
        out_shape += [jax.ShapeDtypeStruct(w.shape, F32)] * 4
    out_shape.append(jax.ShapeDtypeStruct((1, 1), F32))
    return pl.pallas_call(
        body, name="vector_params_step", in_specs=[VMEM] * len(args) + [ANY] * len(deps),
        out_specs=[VMEM] * len(out_shape), out_shape=out_shape,
        scratch_shapes=[pltpu.VMEM((VEC_ROWS, d), F32), pltpu.VMEM((N_DEV, VEC_ROWS, d), F32),
                        pltpu.SemaphoreType.DMA((N_DEV - 1,)), pltpu.SemaphoreType.DMA((N_DEV - 1,))],
        compiler_params=pltpu.CompilerParams(vmem_limit_bytes=VMEM_LIMIT),
    )(*args, *deps)


def kernel(x, norm1_g, w_in, b_gate, conv_w, conv_b, w_a_out, w_pool, pool_scale, w_o, norm2_g, w_ffn_gate, w_ffn_up, w_ffn_down, final_g, loss_target, m_norm1_g, m_w_in, m_b_gate, m_conv_w, m_conv_b, m_w_a_out, m_w_pool, m_pool_scale, m_w_o, m_norm2_g, m_w_ffn_gate, m_w_ffn_up, m_w_ffn_down, m_final_g, v_norm1_g, v_w_in, v_b_gate, v_conv_w, v_conv_b, v_w_a_out, v_w_pool, v_pool_scale, v_w_o, v_norm2_g, v_w_ffn_gate, v_w_ffn_up, v_w_ffn_down, v_final_g):
    t, d = x.shape[1], x.shape[2]
    n_conv = conv_b.shape[1]
    n_groups, pool_cg, pool_dg = w_pool.shape[1], w_pool.shape[2], N_CHIPS * w_pool.shape[3]
    d_ff = N_CHIPS * w_ffn_gate.shape[2]
    assert n_conv // n_groups == pool_cg and n_conv % (n_groups * MIX_COLS) == 0 and n_groups == len(POOL_WINDOWS)

    big = {"w_in": (w_in, m_w_in, v_w_in), "w_a_out": (w_a_out, m_w_a_out, v_w_a_out), "w_pool": (w_pool, m_w_pool, v_w_pool),
           "w_o": (w_o, m_w_o, v_w_o), "w_ffn_gate": (w_ffn_gate, m_w_ffn_gate, v_w_ffn_gate),
           "w_ffn_up": (w_ffn_up, m_w_ffn_up, v_w_ffn_up), "w_ffn_down": (w_ffn_down, m_w_ffn_down, v_w_ffn_down)}
    colshard = {"w_in": True, "w_a_out": True, "w_pool": True, "w_o": False, "w_ffn_gate": True, "w_ffn_up": True,
                "w_ffn_down": False}
    names = list(big)
    shard2d = {n: big[n][0].reshape(-1, big[n][0].shape[-1]) for n in names}
    ws = [_Weight(n, *shard2d[n].shape, colshard[n]) for n in names]

    xs, tgt = x[0], loss_target[0]
    cw_loc = conv_w[0]
    pos = jnp.stack([lax.axis_index("c"), 2 * lax.axis_index("x") + lax.axis_index("y")]).astype(jnp.int32)
    by_name = {w.name: w for w in ws}
    groups = [[by_name[n] for n in g] for g in (["w_in"], ["w_a_out", "w_pool", "w_o"], ["w_ffn_gate"], ["w_ffn_up"],
                                                 ["w_ffn_down"])]
    first = [sum(len(g) for g in groups[:i]) for i in range(len(groups))]
    rgroups = [groups[0], groups[1], groups[2] + groups[3], groups[4]]

    cw_full = _gather_conv_w(cw_loc)
    cast = lambda w, dep: _cast_place(f"cast_{w.name}", w, pos, shard2d[w.name].reshape(2, w.R, w.nn), deps=[dep])
    sems_a, lands_a, tok_a = _gather_start("gather_start_a", groups[:1], [cast(w, cw_full) for w in groups[0]])
    rest = [cast(w, tok_a) for grp in groups[1:] for w in grp]
    full = {}

    def landed(g, after):
        return _gather_wait(f"gather_wait_{g}", groups[g], lands[first[g]:first[g] + len(groups[g])], *gsems[g], after)

    def pass_start(g, got, after=()):
        return _split_start(f"pass_start_{g}", got, 3 * len(got), _pass_copies(groups[g]), after)

    def pass_wait(g, started, after):
        ssem, rsem, got, _ = started
        got = _split_wait(f"pass_wait_{g}", got, ssem, rsem, _pass_copies(groups[g]), after)
        full.update({w.name: a.reshape(w.P * 2 * w.R, w.N) for w, a in zip(groups[g], got)})

    got = _gather_wait("gather_wait_0", groups[0], lands_a, *sems_a[0], rest[-1])
    sems_b, lands_b, tok_b = _gather_start("gather_start_b", groups[1:3], rest[:first[3] - 1], after=got)
    gsems, lands = sems_a + sems_b, lands_a + lands_b
    st = pass_start(0, got, after=[tok_b])
    h1 = _rms_fwd("norm1_fwd", xs, norm1_g, deps=[st[3]])
    pass_wait(0, st, [h1])
    w_in_full = full["w_in"]
    proj = _mm_nn("proj_in", h1, w_in_full, BF16)
    st = pass_start(1, landed(1, proj))
    z, p = _mixer_fwd("mixer_fwd", proj, cw_full, conv_b, n_conv, n_groups, deps=[st[3]])
    pass_wait(1, st, [z])
    wp_full = full["w_pool"].reshape(n_groups, pool_cg, pool_dg)
    ya = _mm_nn("conv_out", z, full["w_a_out"], BF16)
    yb = _gmm_nn("pool_out", p, wp_full, BF16)
    merged = _merge_fwd("merge_fwd", proj, b_gate, ya, yb, pool_scale)
    got_g = landed(2, merged)
    sems_c, lands_c, tok_c = _gather_start("gather_start_c", groups[3:], rest[first[3] - 1:], after=got_g)
    gsems, lands = gsems + sems_c, lands + lands_c
    st_g = pass_start(2, got_g, after=[tok_c])
    x2 = _mm_nn("mix_out", merged, full["w_o"], F32, add=xs, deps=[st_g[3]])
    pass_wait(2, st_g, [x2])
    h2 = _rms_fwd("norm2_fwd", x2, norm2_g)
    gate = _mm_nn("ffn_gate", h2, full["w_ffn_gate"], BF16)
    st_u = pass_start(3, landed(3, gate))
    pass_wait(3, st_u, [st_u[3]])
    up, act = _ffn_up_act("ffn_up_act", h2, full["w_ffn_up"], gate)
    st_d = pass_start(4, landed(4, act))
    pass_wait(4, st_d, [st_d[3]])
    x3 = _mm_nn("ffn_down", act, full["w_ffn_down"], F32, add=x2, tk=d_ff // 4)

    pending = {}

    def pair_start(g, grads):
        grp = rgroups[g]
        gcan = [grads[w.name].reshape(w.P, 2, w.R, w.N) for w in grp]
        slots = [lax.empty((w.P, w.R, w.N), BF16) for w in grp]
        pending[g] = _split_start(f"pair_start_{g}", gcan + slots, len(grp), _pair_copies(len(grp)))
        return pending[g][3]

    def scatter_start(g, after):
        grp = rgroups[g]
        n = len(grp)
        ssem, rsem, arrs, _ = pending[g]
        arrs = _split_wait(f"pair_wait_{g}", arrs, ssem, rsem, _pair_copies(n), after)
        gcan, sib = arrs[:n], arrs[n:]
        pairs = [_pair_sum(f"pair_sum_{w.name}", w, pos, a, s) for w, a, s in zip(grp, gcan, sib)]
        ssem, rsem, pairs, slots, token = _scatter_start(f"scatter_start_{g}", grp, pairs)
        pending[g] = (gcan, sib, pairs, slots, ssem, rsem)
        return token

    def reduce_finish(g, after):
        grp = rgroups[g]
        gcan, sib, pairs, slots, ssem, rsem = pending[g]
        parts = _scatter_wait(f"scatter_wait_{g}", grp, pairs, slots, ssem, rsem, after)
        return [_final_sum(f"final_sum_{w.name}", w, pos, a, s, q) for w, a, s, q in zip(grp, gcan, sib, parts)]

    grads = {}
    dx3, dx3b, d_gf, loss_cols = _final_bwd("final_bwd", x3, final_g.reshape(1, d), tgt)
    dgate, dup = _ffn_bwd("ffn_bwd", dx3b, full["w_ffn_down"], gate, up)
    grads["w_ffn_down"] = _mm_tn("dw_ffn_down", act, dx3b, BF16)
    tok = pair_start(3, grads)
    dh2 = _mm_nt("d_h2", [(dgate, full["w_ffn_gate"]), (dup, full["w_ffn_up"])], F32, tk=d_ff // 4, deps=[tok])
    tok = scatter_start(3, [dh2])
    grads["w_ffn_gate"] = _mm_tn("dw_ffn_gate", h2, dgate, BF16, deps=[tok])
    grads["w_ffn_up"] = _mm_tn("dw_ffn_up", h2, dup, BF16)
    tok = pair_start(2, grads)
    dx2, dx2b, d_g2 = _rms_bwd("norm2_bwd", x2, norm2_g, dh2, dx3, True, deps=[tok])
    dmerged = _mm_nt("d_merged", [(dx2b, full["w_o"])], BF16, tk=d)
    grads["w_o"] = _mm_tn("dw_o", merged, dx2b, BF16)
    tok = scatter_start(2, [grads["w_o"]])
    dya, dyb, dproj, d_bga, d_bgb, d_ps = _merge_bwd("merge_bwd", dmerged, proj, b_gate, ya, yb, pool_scale, deps=[tok])
    dz = _mm_nt("d_z", [(dya, full["w_a_out"])], BF16, tk=d)
    grads["w_a_out"] = _mm_tn("dw_a_out", z, dya, BF16)
    dp = _gmm_nt("d_pool", dyb, wp_full, BF16)
    grads["w_pool"] = _gmm_tn("dw_pool", p, dyb, n_groups, BF16)
    tok = pair_start(1, grads)
    dproj, d_cw, d_cb = _mixer_bwd("mixer_bwd", dz, dp, proj, cw_full, conv_b, dproj, n_conv, n_groups, deps=[tok])
    tok = scatter_start(1, [dproj])
    grads["w_in"] = _mm_tn("dw_in", h1, dproj, BF16, deps=[tok])
    tok = pair_start(0, grads)
    dh1 = _mm_nt("d_h1", [(dproj, w_in_full)], F32, tk=proj.shape[1] // 4, deps=[tok])
    tok = scatter_start(0, [dh1])
    grad_x, d_g1 = _rms_bwd("norm1_bwd", xs, norm1_g, dh1, dx2, False, deps=[tok])

    g_big, d_big, m_big, v_big = {}, {}, {}, {}

    def update(wsub, shared):
        out = []
        for w, g in zip(wsub, shared):
            wt, mt, vt = big[w.name]
            g2 = g.reshape(2 * w.R, w.nn)
            go, dl, nm, nv = _adamw(f"adamw_{w.name}", shard2d[w.name], g2, mt.reshape(g2.shape), vt.reshape(g2.shape))
            g_big[w.name], d_big[w.name], m_big[w.name], v_big[w.name] = (a.reshape(wt.shape) for a in (go, dl, nm, nv))
            out.append(nv)
        return out

    after = [grad_x]
    started = []
    for g in (3, 2, 1):
        halves = reduce_finish(g, after)
        share = _share_copies(len(halves))
        ssem, rsem, halves, tok = _split_start(f"share_start_{g}", halves, len(halves), share)
        started.append((g, ssem, rsem, halves, share))
        after = [tok]
    for g, ssem, rsem, halves, share in started:
        after = update(rgroups[g], _split_wait(f"share_wait_{g}", halves, ssem, rsem, share, after))
    after = update(rgroups[0], _share_halves("share_halves_w_in", rgroups[0], reduce_finish(0, after)))

    vec_names = ["norm1_g", "b_gate", "conv_w", "conv_b", "pool_scale", "norm2_g", "final_g"]
    vec = {"norm1_g": (norm1_g, m_norm1_g, v_norm1_g), "b_gate": (b_gate, m_b_gate, v_b_gate),
           "conv_w": (cw_loc, m_conv_w[0], v_conv_w[0]), "conv_b": (conv_b, m_conv_b, v_conv_b),
           "pool_scale": (pool_scale, m_pool_scale, v_pool_scale), "norm2_g": (norm2_g, m_norm2_g, v_norm2_g),
           "final_g": tuple(a.reshape(1, d) for a in (final_g, m_final_g, v_final_g))}
    vout = _vector_step(d, n_conv, [d_g1, d_bga, d_bgb, d_cw, d_cb, d_ps, d_g2, d_gf, loss_cols],
                        [vec[n] for n in vec_names], deps=after)

    shapes = {"conv_w": conv_w.shape, "final_g": final_g.shape}
    g_vec, d_vec, m_vec, v_vec = ({n: vout[4 * i + q].reshape(shapes.get(n, vec[n][0].shape)) for i, n in enumerate(vec_names)}
                                  for q in range(4))
    loss = vout[-1].reshape(())

    order = ["norm1_g", "w_in", "b_gate", "conv_w", "conv_b", "w_a_out", "w_pool", "pool_scale", "w_o", "norm2_g",
             "w_ffn_gate", "w_ffn_up", "w_ffn_down", "final_g"]
    pick = lambda vecs, bigs: [vecs[n] if n in vecs else bigs[n] for n in order]
    return (loss, grad_x.reshape(x.shape), *pick(g_vec, g_big), *pick(d_vec, d_big), *pick(m_vec, m_big),
            *pick(v_vec, v_big))
```

```python
import functools

import jax
import jax.numpy as jnp
from jax import lax
from jax.experimental import pallas as pl
from jax.experimental.pallas import tpu as pltpu

F32, BF16 = jnp.float32, jnp.bfloat16
MESH = pl.DeviceIdType.MESH
ANY = pl.BlockSpec(memory_space=pl.ANY)
VMEM = pl.BlockSpec(memory_space=pltpu.VMEM)
HBM = pl.BlockSpec(memory_space=pltpu.HBM)
SEM = pl.BlockSpec(memory_space=pltpu.SEMAPHORE)
EFFECT = pltpu.SideEffectType.DATAFLOW_SIDE_EFFECTING

EPS = 1e-6
POOL_WINDOWS = (2, 4, 8, 16)
ADAM_LR, ADAM_B1, ADAM_B2, ADAM_EPS, ADAM_WD, ADAM_STEP = 0.001, 0.9, 0.999, 1e-08, 0.01, 10

V7X_VMEM_BYTES = 64 * 1024 * 1024
VMEM_LIMIT = V7X_VMEM_BYTES * 3 // 4
LANES = 128
N_CHIPS = 4
N_DEV = 8

_DIMS = {
    "nn": (((1,), (0,)), ((), ())),
    "nt": (((1,), (1,)), ((), ())),
    "tn": (((0,), (0,)), ((), ())),
}


def _cp(sem):
    return pltpu.CompilerParams(dimension_semantics=sem, vmem_limit_bytes=VMEM_LIMIT)


def _mesh_pos():
    return lax.axis_index("x"), lax.axis_index("y"), lax.axis_index("c")


def _mm(name, pairs, *, mode, grid, out_shape, o_spec, nk=1, kaxis=None, add=None, deps=()):
    npair = len(pairs)
    has_add = add is not None

    def body(*refs):
        ab = refs[: 2 * npair]
        pos = 2 * npair
        add_ref = refs[pos] if has_add else None
        pos += int(has_add) + len(deps)
        o_ref = refs[pos]
        acc_ref = refs[pos + 1] if nk > 1 else None
        d = None
        for p in range(npair):
            t = lax.dot_general(ab[2 * p][...], ab[2 * p + 1][...], _DIMS[mode], preferred_element_type=F32)
            d = t if d is None else d + t
        if nk == 1:
            if has_add:
                d = d + add_ref[...].astype(F32)
            o_ref[...] = d.astype(o_ref.dtype)
        else:
            k = pl.program_id(kaxis)

            @pl.when(k == 0)
            def _():
                acc_ref[...] = d

            @pl.when(k > 0)
            def _():
                acc_ref[...] += d

            @pl.when(k == nk - 1)
            def _():
                r = acc_ref[...]
                if has_add:
                    r = r + add_ref[...].astype(F32)
                o_ref[...] = r.astype(o_ref.dtype)

    args, specs = [], []
    for a, a_spec, b, b_spec in pairs:
        args += [a, b]
        specs += [a_spec, b_spec]
    if has_add:
        args.append(add[0])
        specs.append(add[1])
    args += list(deps)
    specs += [ANY] * len(deps)
    scratch = []
    if nk > 1:
        blk = [d for d in o_spec.block_shape if d is not None]
        scratch = [pltpu.VMEM(tuple(blk), F32)]
    sem = tuple("arbitrary" if (nk > 1 and ax == kaxis) else "parallel" for ax in range(len(grid)))
    return pl.pallas_call(
        body, name=name, grid=grid, in_specs=specs, out_specs=o_spec, out_shape=out_shape,
        scratch_shapes=scratch, compiler_params=_cp(sem),
    )(*args)


def _tile(n, pref):
    if n <= pref:
        return n
    for t in range(pref, 0, -LANES):
        if t % LANES == 0 and n % t == 0:
            return t
    raise ValueError(f"no tile for {n}")


def _mm_nn(name, a, b, out_dtype, add=None, tk=None, deps=()):
    m, kk = a.shape
    n = b.shape[1]
    tm, tn = _tile(m, 1024), _tile(n, 512)
    out_shape = jax.ShapeDtypeStruct((m, n), out_dtype)
    if tk is None or tk == kk:
        grid = (m // tm, n // tn)
        pairs = [(a, pl.BlockSpec((tm, kk), lambda i, j: (i, 0)), b, pl.BlockSpec((kk, tn), lambda i, j: (0, j)))]
        o_spec = pl.BlockSpec((tm, tn), lambda i, j: (i, j))
        add_ = None if add is None else (add, pl.BlockSpec((tm, tn), lambda i, j: (i, j)))
        return _mm(name, pairs, mode="nn", grid=grid, out_shape=out_shape, o_spec=o_spec, add=add_, deps=deps)
    tn = _tile(n, 1024)
    nk = kk // tk
    grid = (m // tm, n // tn, nk)
    pairs = [(a, pl.BlockSpec((tm, tk), lambda i, j, k: (i, k)), b, pl.BlockSpec((tk, tn), lambda i, j, k: (k, j)))]
    o_spec = pl.BlockSpec((tm, tn), lambda i, j, k: (i, j))
    add_ = None if add is None else (add, pl.BlockSpec((tm, tn), lambda i, j, k: (i, j)))
    return _mm(name, pairs, mode="nn", grid=grid, out_shape=out_shape, o_spec=o_spec, nk=nk, kaxis=2, add=add_, deps=deps)


def _mm_nt(name, abs_, out_dtype, tk, deps=()):
    m, kk = abs_[0][0].shape
    n = abs_[0][1].shape[0]
    tm = _tile(m, 1024)
    nk = kk // tk
    tn = _tile(n, 512 if nk == 1 else 1024)
    out_shape = jax.ShapeDtypeStruct((m, n), out_dtype)
    if nk == 1:
        grid = (m // tm, n // tn)
        pairs = [(a, pl.BlockSpec((tm, kk), lambda i, j: (i, 0)), b, pl.BlockSpec((tn, kk), lambda i, j: (j, 0)))
                 for a, b in abs_]
        o_spec = pl.BlockSpec((tm, tn), lambda i, j: (i, j))
        return _mm(name, pairs, mode="nt", grid=grid, out_shape=out_shape, o_spec=o_spec, deps=deps)
    grid = (m // tm, n // tn, nk)
    pairs = [(a, pl.BlockSpec((tm, tk), lambda i, j, k: (i, k)), b, pl.BlockSpec((tn, tk), lambda i, j, k: (j, k)))
             for a, b in abs_]
    o_spec = pl.BlockSpec((tm, tn), lambda i, j, k: (i, j))
    return _mm(name, pairs, mode="nt", grid=grid, out_shape=out_shape, o_spec=o_spec, nk=nk, kaxis=2, deps=deps)


def _mm_tn(name, a, b, out_dtype, deps=()):
    t, m = a.shape
    n = b.shape[1]
    tm, tn = _tile(m, 512), _tile(n, 2048)
    if n > m:
        grid = (n // tn, m // tm)
        a_map, b_map, o_map = (lambda j, i: (0, i)), (lambda j, i: (0, j)), (lambda j, i: (i, j))
    else:
        grid = (m // tm, n // tn)
        a_map, b_map, o_map = (lambda i, j: (0, i)), (lambda i, j: (0, j)), (lambda i, j: (i, j))
    pairs = [(a, pl.BlockSpec((t, tm), a_map), b, pl.BlockSpec((t, tn), b_map))]
    o_spec = pl.BlockSpec((tm, tn), o_map)
    return _mm(name, pairs, mode="tn", grid=grid, out_shape=jax.ShapeDtypeStruct((m, n), out_dtype), o_spec=o_spec,
               deps=deps)


def _proj_piece(name, h, w, prev, kvec, base, count, deps=()):
    t, kk = h.shape
    own = w.dtype == F32
    nn = w.shape[1] if own else w.shape[1] // N_CHIPS
    tm, tn = _tile(t, 1024), _tile(nn, 512)
    nb = nn // tn

    def body(kv_ref, h_ref, w_ref, *rest):
        rest[-1][...] = lax.dot_general(h_ref[...], w_ref[...].astype(BF16), _DIMS["nn"],
                                        preferred_element_type=F32).astype(BF16)

    cols = lambda s, i, j, kv: (0, j) if own else (0, kv[base + s] * nb + j)
    extra = ([] if prev is None else [prev]) + list(deps)
    grid_spec = pltpu.PrefetchScalarGridSpec(
        num_scalar_prefetch=1, grid=(count, t // tm, nb),
        in_specs=[pl.BlockSpec((tm, kk), lambda s, i, j, kv: (i, 0)), pl.BlockSpec((kk, tn), cols)] + [ANY] * len(extra),
        out_specs=pl.BlockSpec((tm, tn), lambda s, i, j, kv: (i, kv[base + s] * nb + j)))
    return pl.pallas_call(body, name=name, grid_spec=grid_spec, out_shape=jax.ShapeDtypeStruct((t, N_CHIPS * nn), BF16),
                          input_output_aliases={} if prev is None else {3: 0},
                          compiler_params=_cp(("parallel",) * 3))(kvec, h, w, *extra)


def _gmm_nn(name, p, w, out_dtype):
    t = p.shape[0]
    g, cg, dg = w.shape
    tm = _tile(t, 1024)
    pairs = [(p, pl.BlockSpec((tm, cg), lambda i, j: (i, j)), w, pl.BlockSpec((None, cg, dg), lambda i, j: (j, 0, 0)))]
    o_spec = pl.BlockSpec((tm, dg), lambda i, j: (i, j))
    return _mm(name, pairs, mode="nn", grid=(t // tm, g), out_shape=jax.ShapeDtypeStruct((t, g * dg), out_dtype),
               o_spec=o_spec)


def _gmm_nt(name, dy, w, out_dtype):
    t = dy.shape[0]
    g, cg, dg = w.shape
    tm = _tile(t, 1024)
    pairs = [(dy, pl.BlockSpec((tm, dg), lambda i, j: (i, j)), w, pl.BlockSpec((None, cg, dg), lambda i, j: (j, 0, 0)))]
    o_spec = pl.BlockSpec((tm, cg), lambda i, j: (i, j))
    return _mm(name, pairs, mode="nt", grid=(t // tm, g), out_shape=jax.ShapeDtypeStruct((t, g * cg), out_dtype),
               o_spec=o_spec)


def _gmm_tn(name, p, dy, g, out_dtype):
    t = p.shape[0]
    cg, dg = p.shape[1] // g, dy.shape[1] // g
    pairs = [(p, pl.BlockSpec((t, cg), lambda j: (0, j)), dy, pl.BlockSpec((t, dg), lambda j: (0, j)))]
    o_spec = pl.BlockSpec((None, cg, dg), lambda j: (j, 0, 0))
    return _mm(name, pairs, mode="tn", grid=(g,), out_shape=jax.ShapeDtypeStruct((g, cg, dg), out_dtype), o_spec=o_spec)


ROW_TILE = 256


def _rows(t):
    return _tile8(t, ROW_TILE)


def _tile8(n, pref):
    if n <= pref:
        return n
    for t in range(pref, 0, -8):
        if n % t == 0:
            return t
    raise ValueError(f"no row tile for {n}")


def _cast_place(name, w, pos, shard, deps=()):
    tr = _tile8(w.R, 512)
    if w.colshard:
        o_map = lambda h, i, pos: (0, h, i, pos[1])
    else:
        o_map = lambda h, i, pos: (pos[1], h, i, 0)

    def body(pos_ref, w_ref, *rest):
        rest[-1][...] = w_ref[...].astype(BF16)

    grid_spec = pltpu.PrefetchScalarGridSpec(
        num_scalar_prefetch=1, grid=(2, w.R // tr),
        in_specs=[pl.BlockSpec((None, tr, w.nn), lambda h, i, pos: (h, i, 0))] + [ANY] * len(deps),
        out_specs=pl.BlockSpec((None, None, tr, w.nn), o_map))
    return pl.pallas_call(body, name=name, grid_spec=grid_spec, out_shape=jax.ShapeDtypeStruct((w.P, 2, w.R, w.N), BF16),
                          compiler_params=_cp(("parallel", "parallel")))(pos, shard, *deps)


def _rms_fwd(name, x, g, deps=()):
    t, d = x.shape
    tm = _rows(t)

    def body(x_ref, g_ref, *rest):
        xf = x_ref[...]
        r = lax.rsqrt(jnp.mean(xf * xf, axis=-1, keepdims=True) + EPS)
        rest[-1][...] = (xf * r * g_ref[...]).astype(BF16)

    return pl.pallas_call(
        body, name=name, grid=(t // tm,),
        in_specs=[pl.BlockSpec((tm, d), lambda i: (i, 0)), pl.BlockSpec((1, d), lambda i: (0, 0))] + [ANY] * len(deps),
        out_specs=pl.BlockSpec((tm, d), lambda i: (i, 0)), out_shape=jax.ShapeDtypeStruct((t, d), BF16),
        compiler_params=_cp(("parallel",)),
    )(x, g, *deps)


def _rms_bwd(name, x, g, dh, dres, want_bf16, deps=()):
    t, d = x.shape
    tm = _rows(t)

    def body(x_ref, g_ref, dh_ref, dres_ref, *rest):
        rest = rest[len(deps):]
        dx_ref, rest = rest[0], rest[1:]
        dg_ref = rest[-1]
        xf = x_ref[...]
        r = lax.rsqrt(jnp.mean(xf * xf, axis=-1, keepdims=True) + EPS)
        xh = xf * r
        dhf = dh_ref[...]
        dxh = dhf * g_ref[...]
        m = jnp.mean(dxh * xh, axis=-1, keepdims=True)
        dx = dres_ref[...] + r * (dxh - xh * m)
        dx_ref[...] = dx
        if want_bf16:
            rest[0][...] = dx.astype(BF16)

        @pl.when(pl.program_id(0) == 0)
        def _():
            dg_ref[...] = jnp.zeros_like(dg_ref)

        dg_ref[...] += jnp.sum(dhf * xh, axis=0, keepdims=True)

    row = pl.BlockSpec((tm, d), lambda i: (i, 0))
    vec = pl.BlockSpec((1, d), lambda i: (0, 0))
    out_specs = [row] + ([row] if want_bf16 else []) + [vec]
    out_shape = ([jax.ShapeDtypeStruct((t, d), F32)] + ([jax.ShapeDtypeStruct((t, d), BF16)] if want_bf16 else [])
                 + [jax.ShapeDtypeStruct((1, d), F32)])
    return pl.pallas_call(body, name=name, grid=(t // tm,), in_specs=[row, vec, row, row] + [ANY] * len(deps),
                          out_specs=out_specs, out_shape=out_shape, compiler_params=_cp(("arbitrary",)))(x, g, dh, dres, *deps)


def _final_bwd(name, x3, gf, tgt):
    t, d = x3.shape
    tm = _rows(t)

    def body(x_ref, g_ref, t_ref, dx_ref, dxb_ref, dg_ref, lc_ref):
        xf = x_ref[...]
        g = g_ref[...]
        r = lax.rsqrt(jnp.mean(xf * xf, axis=-1, keepdims=True) + EPS)
        xh = xf * r
        diff = xh * g - t_ref[...]
        dy = diff * (1.0 / d)
        dxh = dy * g
        m = jnp.mean(dxh * xh, axis=-1, keepdims=True)
        dx = r * (dxh - xh * m)
        dx_ref[...] = dx
        dxb_ref[...] = dx.astype(BF16)

        @pl.when(pl.program_id(0) == 0)
        def _():
            dg_ref[...] = jnp.zeros_like(dg_ref)
            lc_ref[...] = jnp.zeros_like(lc_ref)

        dg_ref[...] += jnp.sum(dy * xh, axis=0, keepdims=True)
        lc_ref[...] += jnp.sum(diff * diff, axis=0, keepdims=True) * (0.5 / d)

    row = pl.BlockSpec((tm, d), lambda i: (i, 0))
    vec = pl.BlockSpec((1, d), lambda i: (0, 0))
    return pl.pallas_call(
        body, name=name, grid=(t // tm,), in_specs=[row, vec, row], out_specs=[row, row, vec, vec],
        out_shape=[jax.ShapeDtypeStruct((t, d), F32), jax.ShapeDtypeStruct((t, d), BF16),
                   jax.ShapeDtypeStruct((1, d), F32), jax.ShapeDtypeStruct((1, d), F32)],
        compiler_params=_cp(("arbitrary",)),
    )(x3, gf, tgt)


def _shift_down(v, k, t_idx):
    return jnp.where(t_idx >= k, pltpu.roll(v, k, 0), 0.0)


def _shift_up(v, k, t_idx):
    n = v.shape[0]
    return jnp.where(t_idx < n - k, pltpu.roll(v, n - k, 0), 0.0)


def _window_sums(v, shift, t_idx, grp):
    s = v + shift(v, 1, t_idx)
    out = s
    for lvl in range(1, len(POOL_WINDOWS)):
        s = s + shift(s, 1 << lvl, t_idx)
        out = jnp.where(grp >= lvl, s, out)
    return out


def _window_count(t_idx, grp):
    return jnp.minimum(t_idx + 1, jnp.left_shift(2, grp)).astype(F32)


MIX_COLS = 128


def _mixer_fwd(name, proj, cw, cb, n_conv, n_groups, deps=()):
    t = proj.shape[0]
    nb = n_conv // MIX_COLS
    per_group = n_conv // n_groups // MIX_COLS

    def body(ba_ref, ca_ref, va_ref, vb_ref, cw_ref, cb_ref, *rest):
        z_ref, p_ref = rest[len(deps):]
        t_idx = lax.broadcasted_iota(jnp.int32, (t, MIX_COLS), 0)
        q = ca_ref[...].astype(F32) * va_ref[...].astype(F32)
        w = cw_ref[...]
        u = cb_ref[...] + w[0:1] * _shift_down(q, 2, t_idx) + w[1:2] * _shift_down(q, 1, t_idx) + w[2:3] * q
        z_ref[...] = (ba_ref[...].astype(F32) * u).astype(BF16)
        grp = pl.program_id(0) // per_group
        v = vb_ref[...].astype(F32)
        p_ref[...] = (_window_sums(v, _shift_down, t_idx, grp) / _window_count(t_idx, grp) - v).astype(BF16)

    col = lambda s: pl.BlockSpec((t, MIX_COLS), lambda j: (0, s * nb + j))
    return pl.pallas_call(
        body, name=name, grid=(nb,),
        in_specs=[col(0), col(1), col(2), col(3), pl.BlockSpec((3, MIX_COLS), lambda j: (0, j)),
                  pl.BlockSpec((1, MIX_COLS), lambda j: (0, j))] + [ANY] * len(deps),
        out_specs=[col(0), col(0)],
        out_shape=[jax.ShapeDtypeStruct((t, n_conv), BF16), jax.ShapeDtypeStruct((t, n_conv), BF16)],
        compiler_params=_cp(("parallel",)),
    )(proj, proj, proj, proj, cw, cb, *deps)


def _mixer_bwd(name, dz, dp, proj, cw, cb, dproj, n_conv, n_groups, deps=()):
    t = proj.shape[0]
    nb = n_conv // MIX_COLS
    per_group = n_conv // n_groups // MIX_COLS

    def body(dz_ref, dp_ref, ba_ref, ca_ref, va_ref, cw_ref, cb_ref, _, *rest):
        o_ref, dcw_ref, dcb_ref, scr = rest[len(deps):]
        s = pl.program_id(1)

        @pl.when(s == 0)
        def _():
            t_idx = lax.broadcasted_iota(jnp.int32, (t, MIX_COLS), 0)
            ca, va = ca_ref[...].astype(F32), va_ref[...].astype(F32)
            q = ca * va
            q1, q2 = _shift_down(q, 1, t_idx), _shift_down(q, 2, t_idx)
            w = cw_ref[...]
            u = cb_ref[...] + w[0:1] * q2 + w[1:2] * q1 + w[2:3] * q
            dzf = dz_ref[...].astype(F32)
            du = dzf * ba_ref[...].astype(F32)
            scr[0] = (dzf * u).astype(BF16)
            dq = w[2:3] * du + w[1:2] * _shift_up(du, 1, t_idx) + w[0:1] * _shift_up(du, 2, t_idx)
            scr[1] = (dq * va).astype(BF16)
            scr[2] = (dq * ca).astype(BF16)
            dcb_ref[...] = jnp.sum(du, axis=0, keepdims=True)
            dcw_ref[0:1, :] = jnp.sum(du * q2, axis=0, keepdims=True)
            dcw_ref[1:2, :] = jnp.sum(du * q1, axis=0, keepdims=True)
            dcw_ref[2:3, :] = jnp.sum(du * q, axis=0, keepdims=True)
            grp = pl.program_id(0) // per_group
            dpf = dp_ref[...].astype(F32)
            e = dpf / _window_count(t_idx, grp)
            scr[3] = (_window_sums(e, _shift_up, t_idx, grp) - dpf).astype(BF16)

        o_ref[...] = scr[s]

    col = lambda c: pl.BlockSpec((t, MIX_COLS), lambda j, s: (0, c * nb + j))
    own = pl.BlockSpec((t, MIX_COLS), lambda j, s: (0, j))
    return pl.pallas_call(
        body, name=name, grid=(nb, 4),
        in_specs=[own, own, col(0), col(1), col(2), pl.BlockSpec((3, MIX_COLS), lambda j, s: (0, j)),
                  pl.BlockSpec((1, MIX_COLS), lambda j, s: (0, j)), ANY] + [ANY] * len(deps),
        out_specs=[pl.BlockSpec((t, MIX_COLS), lambda j, s: (0, s * nb + j)),
                   pl.BlockSpec((3, MIX_COLS), lambda j, s: (0, j)), pl.BlockSpec((1, MIX_COLS), lambda j, s: (0, j))],
        out_shape=[jax.ShapeDtypeStruct(dproj.shape, BF16), jax.ShapeDtypeStruct((3, n_conv), F32),
                   jax.ShapeDtypeStruct((1, n_conv), F32)],
        scratch_shapes=[pltpu.VMEM((4, t, MIX_COLS), BF16)],
        input_output_aliases={7: 0},
        compiler_params=_cp(("arbitrary", "arbitrary")),
    )(dz, dp, proj, proj, proj, cw, cb, dproj, *deps)


def _merge_fwd(name, proj, bg, ya, yb, ps):
    t, d = ya.shape
    tm = _rows(t)

    def body(gab_ref, bg_ref, ya_ref, yb_ref, ps_ref, o_ref):
        gab = gab_ref[...].astype(F32) + bg_ref[...]
        sa, sb = jax.nn.sigmoid(gab[:, :d]), jax.nn.sigmoid(gab[:, d:])
        o_ref[...] = (sa * ya_ref[...].astype(F32) + sb * (yb_ref[...].astype(F32) * ps_ref[...])).astype(BF16)

    row = pl.BlockSpec((tm, d), lambda i: (i, 0))
    return pl.pallas_call(
        body, name=name, grid=(t // tm,),
        in_specs=[pl.BlockSpec((tm, 2 * d), lambda i: (i, 1)), pl.BlockSpec((1, 2 * d), lambda i: (0, 0)), row, row,
                  pl.BlockSpec((1, d), lambda i: (0, 0))],
        out_specs=row, out_shape=jax.ShapeDtypeStruct((t, d), BF16), compiler_params=_cp(("parallel",)),
    )(proj, bg, ya, yb, ps)


def _merge_bwd(name, dm, proj, bg, ya, yb, ps, deps=()):
    t, d = ya.shape
    tm = _rows(t)

    def body(dm_ref, gab_ref, bg_ref, ya_ref, yb_ref, ps_ref, *rest):
        dya_ref, dyb_ref, dg_ref, dba_ref, dbb_ref, dps_ref = rest[len(deps):]
        gab = gab_ref[...].astype(F32) + bg_ref[...]
        sa, sb = jax.nn.sigmoid(gab[:, :d]), jax.nn.sigmoid(gab[:, d:])
        dmf = dm_ref[...].astype(F32)
        ybf, ps_ = yb_ref[...].astype(F32), ps_ref[...]
        dya_ref[...] = (dmf * sa).astype(BF16)
        dyb = dmf * sb
        dyb_ref[...] = (dyb * ps_).astype(BF16)
        dga = dmf * ya_ref[...].astype(F32) * sa * (1.0 - sa)
        dgb = dmf * (ybf * ps_) * sb * (1.0 - sb)
        dg_ref[:, :d] = dga.astype(BF16)
        dg_ref[:, d:] = dgb.astype(BF16)

        @pl.when(pl.program_id(0) == 0)
        def _():
            dba_ref[...] = jnp.zeros_like(dba_ref)
            dbb_ref[...] = jnp.zeros_like(dbb_ref)
            dps_ref[...] = jnp.zeros_like(dps_ref)

        dba_ref[...] += jnp.sum(dga, axis=0, keepdims=True)
        dbb_ref[...] += jnp.sum(dgb, axis=0, keepdims=True)
        dps_ref[...] += jnp.sum(dyb * ybf, axis=0, keepdims=True)

    row = pl.BlockSpec((tm, d), lambda i: (i, 0))
    vec = pl.BlockSpec((1, d), lambda i: (0, 0))
    gates = pl.BlockSpec((tm, 2 * d), lambda i: (i, 1))
    return pl.pallas_call(
        body, name=name, grid=(t // tm,),
        in_specs=[row, gates, pl.BlockSpec((1, 2 * d), lambda i: (0, 0)), row, row, vec] + [ANY] * len(deps),
        out_specs=[row, row, gates, vec, vec, vec],
        out_shape=[jax.ShapeDtypeStruct((t, d), BF16), jax.ShapeDtypeStruct((t, d), BF16),
                   jax.ShapeDtypeStruct(proj.shape, BF16), jax.ShapeDtypeStruct((1, d), F32),
                   jax.ShapeDtypeStruct((1, d), F32), jax.ShapeDtypeStruct((1, d), F32)],
        compiler_params=_cp(("arbitrary",)),
    )(dm, proj, bg, ya, yb, ps, *deps)


def _ffn_up_act(name, h, w_up, gate):
    t, d = h.shape
    f = w_up.shape[1]
    tm, tf = _tile(t, 1024), _tile(f, 512)

    def body(h_ref, w_ref, g_ref, u_ref, a_ref):
        u = lax.dot_general(h_ref[...], w_ref[...], _DIMS["nn"], preferred_element_type=F32)
        g = g_ref[...].astype(F32)
        u_ref[...] = u.astype(BF16)
        a_ref[...] = (g * jax.nn.sigmoid(g) * u).astype(BF16)

    blk = pl.BlockSpec((tm, tf), lambda i, j: (i, j))
    shp = jax.ShapeDtypeStruct((t, f), BF16)
    return pl.pallas_call(
        body, name=name, grid=(t // tm, f // tf),
        in_specs=[pl.BlockSpec((tm, d), lambda i, j: (i, 0)), pl.BlockSpec((d, tf), lambda i, j: (0, j)), blk],
        out_specs=[blk, blk], out_shape=[shp, shp], compiler_params=_cp(("parallel", "parallel")))(h, w_up, gate)


def _ffn_bwd(name, dy, w_down, gate, up):
    t, d = dy.shape
    f = w_down.shape[0]
    tm, tf = _tile(t, 1024), _tile(f, 512)

    def body(dy_ref, w_ref, g_ref, u_ref, dg_ref, du_ref):
        da = lax.dot_general(dy_ref[...], w_ref[...], _DIMS["nt"], preferred_element_type=F32)
        g = g_ref[...].astype(F32)
        s = jax.nn.sigmoid(g)
        du_ref[...] = (da * (g * s)).astype(BF16)
        dg_ref[...] = (da * u_ref[...].astype(F32) * (s * (1.0 + g * (1.0 - s)))).astype(BF16)

    blk = pl.BlockSpec((tm, tf), lambda i, j: (i, j))
    shp = jax.ShapeDtypeStruct((t, f), BF16)
    return pl.pallas_call(
        body, name=name, grid=(t // tm, f // tf),
        in_specs=[pl.BlockSpec((tm, d), lambda i, j: (i, 0)), pl.BlockSpec((tf, d), lambda i, j: (j, 0)), blk, blk],
        out_specs=[blk, blk], out_shape=[shp, shp], compiler_params=_cp(("parallel", "parallel")))(dy, w_down, gate, up)


def _adamw_math(w, g, m, v):
    m = ADAM_B1 * m + (1.0 - ADAM_B1) * g
    v = ADAM_B2 * v + (1.0 - ADAM_B2) * (g * g)
    m_hat = m / (1.0 - ADAM_B1 ** ADAM_STEP)
    v_hat = v / (1.0 - ADAM_B2 ** ADAM_STEP)
    delta = -ADAM_LR * (m_hat / (jnp.sqrt(v_hat) + ADAM_EPS) + ADAM_WD * w)
    return delta, m, v


def _adamw(name, w, g, m, v):
    r, c = w.shape
    tr = _tile8(r, 512 if c <= 1024 else 256)

    def body(w_ref, g_ref, m_ref, v_ref, go_ref, d_ref, nm_ref, nv_ref):
        g = g_ref[...]
        go_ref[...] = g
        d_ref[...], nm_ref[...], nv_ref[...] = _adamw_math(w_ref[...], g, m_ref[...], v_ref[...])

    blk = pl.BlockSpec((tr, c), lambda i: (i, 0))
    shp = jax.ShapeDtypeStruct((r, c), F32)
    return pl.pallas_call(body, name=name, grid=(r // tr,), in_specs=[blk] * 4, out_specs=[blk] * 4,
                          out_shape=[shp] * 4, compiler_params=_cp(("parallel",)))(w, g, m, v)


class _Weight:
    def __init__(self, name, rows, cols, colshard):
        self.name, self.colshard = name, colshard
        self.R, self.nn = rows // 2, cols
        self.P = 1 if colshard else N_CHIPS
        self.N = N_CHIPS * cols if colshard else cols

    def cols(self, k):
        return pl.ds(pl.multiple_of(k * self.nn, LANES), self.nn)

    def shard(self, ref, k):
        return ref.at[0, :, :, self.cols(k)] if self.colshard else ref.at[k]

    def half(self, ref, k, h):
        return ref.at[0, h, :, self.cols(k)] if self.colshard else ref.at[k, h]

    def quarter(self, ref, k, h, q):
        return self.half(ref, k, h).at[pl.ds(q * (self.R // 2), self.R // 2), :]

    def part(self, ref, k):
        return ref.at[0, :, self.cols(k)] if self.colshard else ref.at[k]


def _remote(src, dst, ssem, rsem, dev):
    return pltpu.make_async_remote_copy(src_ref=src, dst_ref=dst, send_sem=ssem, recv_sem=rsem, device_id=dev,
                                        device_id_type=MESH)


def _other_chips(x, y):
    chips = [(1 - x, y), (x, 1 - y), (1 - x, 1 - y)]
    return chips, [2 * cx + cy for cx, cy in chips]


def _hbm(a):
    return pltpu.with_memory_space_constraint(a, pltpu.HBM)


def _gather_start(name, groups, lands, after=()):
    flat = [w for grp in groups for w in grp]
    nw, ng = len(flat), len(groups)

    def body(*refs):
        land = refs[:nw]
        sems = refs[nw + len(after):nw + len(after) + 2 * ng]
        token = refs[2 * nw + len(after) + 2 * ng]
        x, y, c = _mesh_pos()
        k_me = 2 * x + y
        chips, _ = _other_chips(x, y)
        i = 0
        for g, grp in enumerate(groups):
            for wi, w in enumerate(grp):
                mine = w.half(land[i], k_me, c)
                for j, chip in enumerate(chips):
                    _remote(mine, mine, sems[2 * g].at[3 * wi + j], sems[2 * g + 1].at[3 * wi + j], (*chip, c)).start()
                i += 1
        token[...] = jnp.zeros_like(token)

    sem_shapes = []
    for grp in groups:
        sem_shapes += [pltpu.SemaphoreType.DMA((3 * len(grp),))] * 2
    out = pl.pallas_call(
        body, name=name, in_specs=[HBM] * nw + [ANY] * len(after),
        out_specs=[SEM] * (2 * ng) + [HBM] * nw + [VMEM],
        out_shape=sem_shapes + [pltpu.HBM(a.shape, a.dtype) for a in lands] + [jax.ShapeDtypeStruct((8, LANES), F32)],
        input_output_aliases={i: 2 * ng + i for i in range(nw)},
        compiler_params=pltpu.CompilerParams(has_side_effects=EFFECT),
    )(*[_hbm(a) for a in lands], *after)
    sems = [(out[2 * g], out[2 * g + 1]) for g in range(ng)]
    return sems, list(out[2 * ng:2 * ng + nw]), out[-1]


def _gather_wait(name, grp, lands, ssem, rsem, after):
    n = len(grp)

    def body(*refs):
        land, ssem_ref, rsem_ref = refs[:n], refs[n], refs[n + 1]
        x, y, c = _mesh_pos()
        k_me = 2 * x + y
        chips, ks = _other_chips(x, y)
        for wi, w in enumerate(grp):
            for j, chip in enumerate(chips):
                cp = _remote(w.half(land[wi], k_me, c), w.half(land[wi], ks[j], c), ssem_ref.at[3 * wi + j],
                             rsem_ref.at[3 * wi + j], (*chip, c))
                cp.wait_send()
                cp.wait_recv()

    return pl.pallas_call(
        body, name=name, in_specs=[HBM] * n + [SEM, SEM, ANY], out_specs=[HBM] * n,
        out_shape=[pltpu.HBM(a.shape, a.dtype) for a in lands], input_output_aliases={i: i for i in range(n)},
        compiler_params=pltpu.CompilerParams(has_side_effects=EFFECT),
    )(*lands, ssem, rsem, after)


def _split_start(name, arrays, n, copies, after=()):
    na = len(arrays)

    def body(*refs):
        ssem, rsem, token = refs[na + len(after):][0], refs[na + len(after):][1], refs[2 * na + len(after) + 2]
        for i, (src, dst, dev, _) in enumerate(copies(refs[:na], *_mesh_pos())):
            _remote(src, dst, ssem.at[i], rsem.at[i], dev).start()
        token[...] = jnp.zeros_like(token)

    out = pl.pallas_call(
        body, name=name, in_specs=[HBM] * na + [ANY] * len(after), out_specs=[SEM, SEM] + [HBM] * na + [VMEM],
        out_shape=[pltpu.SemaphoreType.DMA((n,))] * 2 + [pltpu.HBM(a.shape, a.dtype) for a in arrays]
        + [jax.ShapeDtypeStruct((8, LANES), F32)],
        input_output_aliases={i: 2 + i for i in range(na)},
        compiler_params=pltpu.CompilerParams(has_side_effects=EFFECT),
    )(*[_hbm(a) for a in arrays], *after)
    return out[0], out[1], list(out[2:2 + na]), out[-1]


def _split_wait(name, arrays, ssem, rsem, copies, after):
    na = len(arrays)

    def body(*refs):
        for i, (src, _, dev, dst) in enumerate(copies(refs[:na], *_mesh_pos())):
            cp = _remote(src, dst, refs[na].at[i], refs[na + 1].at[i], dev)
            cp.wait_send()
            cp.wait_recv()

    return list(pl.pallas_call(
        body, name=name, in_specs=[HBM] * na + [SEM, SEM] + [ANY] * len(after), out_specs=[HBM] * na,
        out_shape=[pltpu.HBM(a.shape, a.dtype) for a in arrays], input_output_aliases={i: i for i in range(na)},
        compiler_params=pltpu.CompilerParams(has_side_effects=EFFECT),
    )(*arrays, ssem, rsem, *after))


def _pass_copies(grp, rels=(0, 1, 2)):
    def copies(land, x, y, c):
        _, ks = _other_chips(x, y)
        return [(w.half(land[wi], ks[j], c), w.half(land[wi], ks[j], c), (x, y, 1 - c), w.half(land[wi], ks[j], 1 - c))
                for wi, w in enumerate(grp) for j in rels]
    copies.n = len(grp) * len(rels)
    return copies


def _near_copies(grp):
    def copies(land, x, y, c):
        chips, ks = _other_chips(x, y)
        out = []
        for wi, w in enumerate(grp):
            mine = w.half(land[wi], 2 * x + y, c)
            out += [(mine, mine, (*chips[j], c), w.half(land[wi], ks[j], c)) for j in (0, 1)]
        return out
    copies.n = 2 * len(grp)
    return copies


def _far_copies(grp):
    def copies(land, x, y, c):
        chips, ks = _other_chips(x, y)
        out = []
        for wi, w in enumerate(grp):
            for j in (0, 1):
                q = w.quarter(land[wi], ks[j], c, j)
                out.append((q, q, (*chips[1 - j], c), w.quarter(land[wi], ks[2], c, j)))
        return out
    copies.n = 2 * len(grp)
    return copies


def _pair_copies(n):
    def copies(refs, x, y, c):
        return [(refs[i].at[:, 1 - c], refs[n + i], (x, y, 1 - c), refs[n + i]) for i in range(n)]
    return copies


def _share_copies(n):
    def copies(refs, x, y, c):
        return [(refs[i].at[c], refs[i].at[c], (x, y, 1 - c), refs[i].at[1 - c]) for i in range(n)]
    return copies


def _gather_conv_w(cw):
    ncw = cw.shape[1]

    def body(cw_ref, out_ref, ssem, rsem):
        x, y, c = _mesh_pos()
        k_me = 2 * x + y
        chips, ks = _other_chips(x, y)
        cols = lambda k: out_ref.at[:, pl.ds(pl.multiple_of(k * ncw, LANES), ncw)]
        cps = [_remote(cw_ref, cols(k_me), ssem.at[j], rsem.at[j], (*chip, c)) for j, chip in enumerate(chips)]
        for cp in cps:
            cp.start()
        for k in range(N_CHIPS):
            @pl.when(k_me == k)
            def _():
                out_ref[:, k * ncw:(k + 1) * ncw] = cw_ref[...]
        for j in range(3):
            _remote(cw_ref, cols(ks[j]), ssem.at[j], rsem.at[j], (*chips[j], c)).wait_recv()
        for cp in cps:
            cp.wait_send()

    return pl.pallas_call(
        body, name="gather_conv_w", in_specs=[VMEM], out_specs=VMEM,
        out_shape=jax.ShapeDtypeStruct((3, N_CHIPS * ncw), F32),
        scratch_shapes=[pltpu.SemaphoreType.DMA((3,)), pltpu.SemaphoreType.DMA((3,))],
    )(cw)


def _grad_tiles(w, n):
    return _tile8(w.R, 512) if w.R <= 512 else w.R // 2, _tile(n, 2048)


def _pair_sum(name, w, pos, grad, got):
    tr, tn = _grad_tiles(w, w.N)

    def body(pos_ref, g_ref, r_ref, o_ref):
        o_ref[...] = (g_ref[...].astype(F32) + r_ref[...].astype(F32)).astype(BF16)

    blk = pl.BlockSpec((None, tr, tn), lambda p, i, j, pos: (p, i, j))
    grid_spec = pltpu.PrefetchScalarGridSpec(
        num_scalar_prefetch=1, grid=(w.P, w.R // tr, w.N // tn),
        in_specs=[pl.BlockSpec((None, None, tr, tn), lambda p, i, j, pos: (p, pos[0], i, j)), blk], out_specs=blk)
    return pl.pallas_call(body, name=name, grid_spec=grid_spec, out_shape=jax.ShapeDtypeStruct((w.P, w.R, w.N), BF16),
                          compiler_params=_cp(("parallel",) * 3))(pos, grad, got)


def _scatter_start(name, ws, pairs):
    nw = len(ws)

    def body(*refs):
        pr, land = refs[:nw], refs[nw:2 * nw]
        ssem, rsem = refs[2 * nw], refs[2 * nw + 1]
        token = refs[4 * nw + 2]
        x, y, c = _mesh_pos()
        chips, ks = _other_chips(x, y)
        for i, w in enumerate(ws):
            for j, chip in enumerate(chips):
                _remote(w.part(pr[i], ks[j]), land[i].at[j], ssem.at[3 * i + j], rsem.at[3 * i + j], (*chip, c)).start()
        token[...] = jnp.zeros_like(token)

    lands = [lax.empty((3, w.R, w.nn), BF16) for w in ws]
    out = pl.pallas_call(
        body, name=name, in_specs=[HBM] * (2 * nw),
        out_specs=[SEM, SEM] + [HBM] * (2 * nw) + [VMEM],
        out_shape=[pltpu.SemaphoreType.DMA((3 * nw,))] * 2 + [pltpu.HBM(a.shape, a.dtype) for a in pairs + lands]
        + [jax.ShapeDtypeStruct((8, LANES), F32)],
        input_output_aliases={i: 2 + i for i in range(2 * nw)},
        compiler_params=pltpu.CompilerParams(has_side_effects=EFFECT),
    )(*[_hbm(a) for a in pairs + lands])
    return out[0], out[1], list(out[2:2 + nw]), list(out[2 + nw:2 + 2 * nw]), out[-1]


def _scatter_wait(name, ws, pairs, lands, ssem, rsem, after):
    nw = len(ws)

    def body(*refs):
        pr, land = refs[:nw], refs[nw:2 * nw]
        ssem_ref, rsem_ref = refs[2 * nw], refs[2 * nw + 1]
        x, y, c = _mesh_pos()
        chips, ks = _other_chips(x, y)
        for i, w in enumerate(ws):
            for j, chip in enumerate(chips):
                cp = _remote(w.part(pr[i], ks[j]), land[i].at[j], ssem_ref.at[3 * i + j], rsem_ref.at[3 * i + j], (*chip, c))
                cp.wait_send()
                cp.wait_recv()

    out = pl.pallas_call(
        body, name=name, in_specs=[HBM] * (2 * nw) + [SEM, SEM] + [ANY] * len(after), out_specs=[HBM] * (2 * nw),
        out_shape=[pltpu.HBM(a.shape, a.dtype) for a in pairs + lands],
        input_output_aliases={i: i for i in range(2 * nw)},
        compiler_params=pltpu.CompilerParams(has_side_effects=EFFECT),
    )(*pairs, *lands, ssem, rsem, *after)
    return list(out[nw:])


def _final_sum(name, w, pos, grad, got, parts):
    tr, tn = _grad_tiles(w, w.nn)
    nbc = w.nn // tn

    def body(pos_ref, g_ref, r_ref, p_ref, o_ref):
        acc = g_ref[...].astype(F32) + r_ref[...].astype(F32)
        for j in range(3):
            acc = acc + p_ref[j].astype(F32)
        o_ref[...] = acc

    if w.colshard:
        g_spec = pl.BlockSpec((None, None, tr, tn), lambda i, j, pos: (0, pos[0], i, pos[1] * nbc + j))
        r_spec = pl.BlockSpec((None, tr, tn), lambda i, j, pos: (0, i, pos[1] * nbc + j))
    else:
        g_spec = pl.BlockSpec((None, None, tr, tn), lambda i, j, pos: (pos[1], pos[0], i, j))
        r_spec = pl.BlockSpec((None, tr, tn), lambda i, j, pos: (pos[1], i, j))
    grid_spec = pltpu.PrefetchScalarGridSpec(
        num_scalar_prefetch=1, grid=(w.R // tr, nbc),
        in_specs=[g_spec, r_spec, pl.BlockSpec((3, tr, tn), lambda i, j, pos: (0, i, j))],
        out_specs=pl.BlockSpec((None, tr, tn), lambda i, j, pos: (pos[0], i, j)))
    return pl.pallas_call(body, name=name, grid_spec=grid_spec, out_shape=jax.ShapeDtypeStruct((2, w.R, w.nn), F32),
                          compiler_params=_cp(("parallel",) * 2))(pos, grad, got, parts)


def _share_halves(name, ws, halves, deps=()):
    nw = len(ws)

    def body(*refs):
        out = refs[nw + len(deps):2 * nw + len(deps)]
        ssem, rsem = refs[2 * nw + len(deps):]
        x, y, c = _mesh_pos()
        sib = (x, y, 1 - c)
        cps = [_remote(out[i].at[c], out[i].at[c], ssem.at[i], rsem.at[i], sib) for i in range(nw)]
        for cp in cps:
            cp.start()
        for i, cp in enumerate(cps):
            cp.wait_send()
            _remote(out[i].at[1 - c], out[i].at[1 - c], ssem.at[i], rsem.at[i], sib).wait_recv()

    return pl.pallas_call(
        body, name=name, in_specs=[ANY] * (nw + len(deps)), out_specs=[ANY] * nw,
        out_shape=[jax.ShapeDtypeStruct(h.shape, F32) for h in halves],
        scratch_shapes=[pltpu.SemaphoreType.DMA((nw,)), pltpu.SemaphoreType.DMA((nw,))],
        input_output_aliases={i: i for i in range(nw)},
    )(*halves, *deps)


VEC_ROWS = 16


def _vector_step(d, n_conv, parts, params, deps=()):
    ncw = params[2][0].shape[1]
    n_par = len(params)

    def body(*refs):
        dg1, dba, dbb, dcw, dcb, dps, dg2, dgf, lc = refs[:9]
        wmv = refs[9:9 + 3 * n_par]
        refs = refs[9 + 3 * n_par + len(deps):]
        outs = refs[:4 * n_par]
        loss_ref = refs[4 * n_par]
        snd, got, ssem, rsem = refs[4 * n_par + 1:]
        x, y, c = _mesh_pos()
        me = 4 * x + 2 * y + c
        snd[...] = jnp.zeros_like(snd)
        for row, ref in ((0, dg1), (1, dba), (2, dbb), (3, dps), (4, dg2), (5, dgf), (6, lc)):
            snd[row:row + 1, :] = ref[...]
        snd[7:8, :n_conv] = dcb[...]
        snd[8:11, :n_conv] = dcw[...]
        cps = []
        for r in range(1, N_DEV):
            peer = tuple(1 - p if (r >> b) & 1 else p for p, b in ((x, 2), (y, 1), (c, 0)))
            cps.append(_remote(snd, got.at[me], ssem.at[r - 1], rsem.at[r - 1], peer))
        for cp in cps:
            cp.start()
        got[me] = snd[...]
        for r in range(1, N_DEV):
            peer = tuple(1 - p if (r >> b) & 1 else p for p, b in ((x, 2), (y, 1), (c, 0)))
            _remote(snd, got.at[4 * peer[0] + 2 * peer[1] + peer[2]], ssem.at[r - 1], rsem.at[r - 1], peer).wait_recv()
        for cp in cps:
            cp.wait_send()
        tot = got[0]
        for dev in range(1, N_DEV):
            tot = tot + got[dev]
        loss_ref[...] = jnp.sum(tot[6:7, :], axis=1, keepdims=True)
        k_me = 2 * x + y
        g_cw = jnp.zeros((3, ncw), F32)
        for k in range(N_CHIPS):
            g_cw = g_cw + jnp.where(k_me == k, tot[8:11, k * ncw:(k + 1) * ncw], 0.0)
        grads = [tot[0:1, :], jnp.concatenate([tot[1:2, :], tot[2:3, :]], axis=1), g_cw, tot[7:8, :n_conv],
                 tot[3:4, :], tot[4:5, :], tot[5:6, :]]
        for i, g in enumerate(grads):
            w_ref, m_ref, v_ref = wmv[3 * i:3 * i + 3]
            delta, nm, nv = _adamw_math(w_ref[...], g, m_ref[...], v_ref[...])
            outs[4 * i][...] = g
            outs[4 * i + 1][...] = delta
            outs[4 * i + 2][...] = nm
            outs[4 * i + 3][...] = nv

    args = list(parts)
    out_shape = []
    for w, m, v in params:
        args += [w, m, v]
        out_shape += [jax.ShapeDtypeStruct(w.shape, F32)] * 4
    out_shape.append(jax.ShapeDtypeStruct((1, 1), F32))
    return pl.pallas_call(
        body, name="vector_params_step", in_specs=[VMEM] * len(args) + [ANY] * len(deps),
        out_specs=[VMEM] * len(out_shape), out_shape=out_shape,
        scratch_shapes=[pltpu.VMEM((VEC_ROWS, d), F32), pltpu.VMEM((N_DEV, VEC_ROWS, d), F32),
                        pltpu.SemaphoreType.DMA((N_DEV - 1,)), pltpu.SemaphoreType.DMA((N_DEV - 1,))],
        compiler_params=pltpu.CompilerParams(vmem_limit_bytes=VMEM_LIMIT),
    )(*args, *deps)


def kernel(x, norm1_g, w_in, b_gate, conv_w, conv_b, w_a_out, w_pool, pool_scale, w_o, norm2_g, w_ffn_gate, w_ffn_up, w_ffn_down, final_g, loss_target, m_norm1_g, m_w_in, m_b_gate, m_conv_w, m_conv_b, m_w_a_out, m_w_pool, m_pool_scale, m_w_o, m_norm2_g, m_w_ffn_gate, m_w_ffn_up, m_w_ffn_down, m_final_g, v_norm1_g, v_w_in, v_b_gate, v_conv_w, v_conv_b, v_w_a_out, v_w_pool, v_pool_scale, v_w_o, v_norm2_g, v_w_ffn_gate, v_w_ffn_up, v_w_ffn_down, v_final_g):
    t, d = x.shape[1], x.shape[2]
    n_conv = conv_b.shape[1]
    n_groups, pool_cg, pool_dg = w_pool.shape[1], w_pool.shape[2], N_CHIPS * w_pool.shape[3]
    d_ff = N_CHIPS * w_ffn_gate.shape[2]
    assert n_conv // n_groups == pool_cg and n_conv % (n_groups * MIX_COLS) == 0 and n_groups == len(POOL_WINDOWS)

    big = {"w_in": (w_in, m_w_in, v_w_in), "w_a_out": (w_a_out, m_w_a_out, v_w_a_out), "w_pool": (w_pool, m_w_pool, v_w_pool),
           "w_o": (w_o, m_w_o, v_w_o), "w_ffn_gate": (w_ffn_gate, m_w_ffn_gate, v_w_ffn_gate),
           "w_ffn_up": (w_ffn_up, m_w_ffn_up, v_w_ffn_up), "w_ffn_down": (w_ffn_down, m_w_ffn_down, v_w_ffn_down)}
    colshard = {"w_in": True, "w_a_out": True, "w_pool": True, "w_o": False, "w_ffn_gate": True, "w_ffn_up": True,
                "w_ffn_down": False}
    names = list(big)
    shard2d = {n: big[n][0].reshape(-1, big[n][0].shape[-1]) for n in names}
    ws = [_Weight(n, *shard2d[n].shape, colshard[n]) for n in names]

    xs, tgt = x[0], loss_target[0]
    cw_loc = conv_w[0]
    pos = jnp.stack([lax.axis_index("c"), 2 * lax.axis_index("x") + lax.axis_index("y")]).astype(jnp.int32)
    by_name = {w.name: w for w in ws}
    groups = [[by_name[n] for n in g] for g in (["w_in"], ["w_a_out", "w_pool", "w_o"], ["w_ffn_gate"], ["w_ffn_up"],
                                                 ["w_ffn_down"])]
    first = [sum(len(g) for g in groups[:i]) for i in range(len(groups))]
    rgroups = [groups[0], groups[1], groups[2] + groups[3], groups[4]]

    cw_full = _gather_conv_w(cw_loc)
    cast = lambda w, dep: _cast_place(f"cast_{w.name}", w, pos, shard2d[w.name].reshape(2, w.R, w.nn), deps=[dep])
    chips, ks = _other_chips(lax.axis_index("x"), lax.axis_index("y"))
    kvec = jnp.stack([pos[1], *ks]).astype(jnp.int32)
    full = {}

    def start(name, arrays, copies, after=()):
        ssem, rsem, arrays, token = _split_start(name, arrays, copies.n, copies, after)
        return name, arrays, ssem, rsem, copies, token

    def wait(started, after):
        name, arrays, ssem, rsem, copies, _ = started
        return _split_wait(name + "_wait", arrays, ssem, rsem, copies, after)

    def passed(g, got, after):
        st = start(f"pass_{g}", got, _pass_copies(groups[g]), after)
        got = wait(st, [st[5]])
        full.update({w.name: a.reshape(w.P * 2 * w.R, w.N) for w, a in zip(groups[g], got)})

    near = start("near_0", [cast(w, cw_full) for w in groups[0]], _near_copies(groups[0]))
    rest = [cast(w, near[5]) for grp in groups[1:] for w in grp]
    h1 = _rms_fwd("norm1_fwd", xs, norm1_g, deps=[rest[-1]])
    proj = _proj_piece("proj_own", h1, shard2d["w_in"], None, kvec, 0, 1)
    got = wait(near, [proj])
    far = start("far_0", got, _far_copies(groups[0]))
    st = start("pass_near_0", far[1], _pass_copies(groups[0], (0, 1)), [far[5]])
    got = wait(st, [st[5]])
    proj = _proj_piece("proj_near", h1, got[0].reshape(-1, groups[0][0].N), proj, kvec, 1, 2)
    got = wait((far[0], got) + far[2:], [proj])
    sems_b, lands_b, tok_b = _gather_start("gather_start_b", groups[1:2], rest[:3], after=got)
    st = start("pass_far_0", got, _pass_copies(groups[0], (2,)), [tok_b])
    got = wait(st, [st[5]])
    w_in_full = got[0].reshape(-1, groups[0][0].N)
    proj = _proj_piece("proj_far", h1, w_in_full, proj, kvec, 3, 1)
    z, p = _mixer_fwd("mixer_fwd", proj, cw_full, conv_b, n_conv, n_groups)
    got = _gather_wait("gather_wait_1", groups[1], lands_b, *sems_b[0], z)
    near_g = start("near_2", rest[3:4], _near_copies(groups[2]), got)
    passed(1, got, [near_g[5]])
    wp_full = full["w_pool"].reshape(n_groups, pool_cg, pool_dg)
    ya = _mm_nn("conv_out", z, full["w_a_out"], BF16)
    yb = _gmm_nn("pool_out", p, wp_full, BF16)
    merged = _merge_fwd("merge_fwd", proj, b_gate, ya, yb, pool_scale)
    far_g = start("far_2", wait(near_g, [merged]), _far_copies(groups[2]))
    near_u = start("near_3", rest[4:5], _near_copies(groups[3]), [far_g[5]])
    x2 = _mm_nn("mix_out", merged, full["w_o"], F32, add=xs, deps=[near_u[5]])
    h2 = _rms_fwd("norm2_fwd", x2, norm2_g)
    passed(2, wait(far_g, [h2]), [])
    gate = _mm_nn("ffn_gate", h2, full["w_ffn_gate"], BF16)
    far_u = start("far_3", wait(near_u, [gate]), _far_copies(groups[3]))
    near_d = start("near_4", rest[5:6], _near_copies(groups[4]), [far_u[5]])
    passed(3, wait(far_u, [near_d[5]]), [])
    up, act = _ffn_up_act("ffn_up_act", h2, full["w_ffn_up"], gate)
    far_d = start("far_4", wait(near_d, [act]), _far_copies(groups[4]))
    passed(4, wait(far_d, [far_d[5]]), [])
    x3 = _mm_nn("ffn_down", act, full["w_ffn_down"], F32, add=x2, tk=d_ff // 4)

    pending = {}

    def pair_start(g, grads):
        grp = rgroups[g]
        gcan = [grads[w.name].reshape(w.P, 2, w.R, w.N) for w in grp]
        slots = [lax.empty((w.P, w.R, w.N), BF16) for w in grp]
        pending[g] = _split_start(f"pair_start_{g}", gcan + slots, len(grp), _pair_copies(len(grp)))
        return pending[g][3]

    def scatter_start(g, after):
        grp = rgroups[g]
        n = len(grp)
        ssem, rsem, arrs, _ = pending[g]
        arrs = _split_wait(f"pair_wait_{g}", arrs, ssem, rsem, _pair_copies(n), after)
        gcan, sib = arrs[:n], arrs[n:]
        pairs = [_pair_sum(f"pair_sum_{w.name}", w, pos, a, s) for w, a, s in zip(grp, gcan, sib)]
        ssem, rsem, pairs, slots, token = _scatter_start(f"scatter_start_{g}", grp, pairs)
        pending[g] = (gcan, sib, pairs, slots, ssem, rsem)
        return token

    def reduce_finish(g, after):
        grp = rgroups[g]
        gcan, sib, pairs, slots, ssem, rsem = pending[g]
        parts = _scatter_wait(f"scatter_wait_{g}", grp, pairs, slots, ssem, rsem, after)
        return [_final_sum(f"final_sum_{w.name}", w, pos, a, s, q) for w, a, s, q in zip(grp, gcan, sib, parts)]

    grads = {}
    dx3, dx3b, d_gf, loss_cols = _final_bwd("final_bwd", x3, final_g.reshape(1, d), tgt)
    dgate, dup = _ffn_bwd("ffn_bwd", dx3b, full["w_ffn_down"], gate, up)
    grads["w_ffn_down"] = _mm_tn("dw_ffn_down", act, dx3b, BF16)
    tok = pair_start(3, grads)
    dh2 = _mm_nt("d_h2", [(dgate, full["w_ffn_gate"]), (dup, full["w_ffn_up"])], F32, tk=d_ff // 4, deps=[tok])
    tok = scatter_start(3, [dh2])
    grads["w_ffn_gate"] = _mm_tn("dw_ffn_gate", h2, dgate, BF16, deps=[tok])
    grads["w_ffn_up"] = _mm_tn("dw_ffn_up", h2, dup, BF16)
    tok = pair_start(2, grads)
    dx2, dx2b, d_g2 = _rms_bwd("norm2_bwd", x2, norm2_g, dh2, dx3, True, deps=[tok])
    dmerged = _mm_nt("d_merged", [(dx2b, full["w_o"])], BF16, tk=d)
    grads["w_o"] = _mm_tn("dw_o", merged, dx2b, BF16)
    tok = scatter_start(2, [grads["w_o"]])
    dya, dyb, dproj, d_bga, d_bgb, d_ps = _merge_bwd("merge_bwd", dmerged, proj, b_gate, ya, yb, pool_scale, deps=[tok])
    dz = _mm_nt("d_z", [(dya, full["w_a_out"])], BF16, tk=d)
    grads["w_a_out"] = _mm_tn("dw_a_out", z, dya, BF16)
    dp = _gmm_nt("d_pool", dyb, wp_full, BF16)
    grads["w_pool"] = _gmm_tn("dw_pool", p, dyb, n_groups, BF16)
    tok = pair_start(1, grads)
    dproj, d_cw, d_cb = _mixer_bwd("mixer_bwd", dz, dp, proj, cw_full, conv_b, dproj, n_conv, n_groups, deps=[tok])
    tok = scatter_start(1, [dproj])
    grads["w_in"] = _mm_tn("dw_in", h1, dproj, BF16, deps=[tok])
    tok = pair_start(0, grads)
    dh1 = _mm_nt("d_h1", [(dproj, w_in_full)], F32, tk=proj.shape[1] // 4, deps=[tok])
    tok = scatter_start(0, [dh1])
    grad_x, d_g1 = _rms_bwd("norm1_bwd", xs, norm1_g, dh1, dx2, False, deps=[tok])

    g_big, d_big, m_big, v_big = {}, {}, {}, {}

    def update(wsub, shared):
        out = []
        for w, g in zip(wsub, shared):
            wt, mt, vt = big[w.name]
            g2 = g.reshape(2 * w.R, w.nn)
            go, dl, nm, nv = _adamw(f"adamw_{w.name}", shard2d[w.name], g2, mt.reshape(g2.shape), vt.reshape(g2.shape))
            g_big[w.name], d_big[w.name], m_big[w.name], v_big[w.name] = (a.reshape(wt.shape) for a in (go, dl, nm, nv))
            out.append(nv)
        return out

    after = [grad_x]
    started = []
    for g in (3, 2, 1):
        halves = reduce_finish(g, after)
        share = _share_copies(len(halves))
        ssem, rsem, halves, tok = _split_start(f"share_start_{g}", halves, len(halves), share)
        started.append((g, ssem, rsem, halves, share))
        after = [tok]
    for g, ssem, rsem, halves, share in started:
        after = update(rgroups[g], _split_wait(f"share_wait_{g}", halves, ssem, rsem, share, after))
    after = update(rgroups[0], _share_halves("share_halves_w_in", rgroups[0], reduce_finish(0, after)))

    vec_names = ["norm1_g", "b_gate", "conv_w", "conv_b", "pool_scale", "norm2_g", "final_g"]
    vec = {"norm1_g": (norm1_g, m_norm1_g, v_norm1_g), "b_gate": (b_gate, m_b_gate, v_b_gate),
           "conv_w": (cw_loc, m_conv_w[0], v_conv_w[0]), "conv_b": (conv_b, m_conv_b, v_conv_b),
           "pool_scale": (pool_scale, m_pool_scale, v_pool_scale), "norm2_g": (norm2_g, m_norm2_g, v_norm2_g),
           "final_g": tuple(a.reshape(1, d) for a in (final_g, m_final_g, v_final_g))}
    vout = _vector_step(d, n_conv, [d_g1, d_bga, d_bgb, d_cw, d_cb, d_ps, d_g2, d_gf, loss_cols],
                        [vec[n] for n in vec_names], deps=after)

    shapes = {"conv_w": conv_w.shape, "final_g": final_g.shape}
    g_vec, d_vec, m_vec, v_vec = ({n: vout[4 * i + q].reshape(shapes.get(n, vec[n][0].shape)) for i, n in enumerate(vec_names)}
                                  for q in range(4))
    loss = vout[-1].reshape(())

    order = ["norm1_g", "w_in", "b_gate", "conv_w", "conv_b", "w_a_out", "w_pool", "pool_scale", "w_o", "norm2_g",
             "w_ffn_gate", "w_ffn_up", "w_ffn_down", "final_g"]
    pick = lambda vecs, bigs: [vecs[n] if n in vecs else bigs[n] for n in order]
    return (loss, grad_x.reshape(x.shape), *pick(g_vec, g_big), *pick(d_vec, d_big), *pick(m_vec, m_big),
            *pick(v_vec, v_big))
```

```python
import functools

import jax
import jax.numpy as jnp
from jax import lax
from jax.experimental import pallas as pl
from jax.experimental.pallas import tpu as pltpu

F32, BF16 = jnp.float32, jnp.bfloat16
MESH = pl.DeviceIdType.MESH
ANY = pl.BlockSpec(memory_space=pl.ANY)
VMEM = pl.BlockSpec(memory_space=pltpu.VMEM)
HBM = pl.BlockSpec(memory_space=pltpu.HBM)
SEM = pl.BlockSpec(memory_space=pltpu.SEMAPHORE)
EFFECT = pltpu.SideEffectType.DATAFLOW_SIDE_EFFECTING

EPS = 1e-6
POOL_WINDOWS = (2, 4, 8, 16)
ADAM_LR, ADAM_B1, ADAM_B2, ADAM_EPS, ADAM_WD, ADAM_STEP = 0.001, 0.9, 0.999, 1e-08, 0.01, 10

V7X_VMEM_BYTES = 64 * 1024 * 1024
VMEM_LIMIT = V7X_VMEM_BYTES * 3 // 4
LANES = 128
N_CHIPS = 4
N_DEV = 8

_DIMS = {
    "nn": (((1,), (0,)), ((), ())),
    "nt": (((1,), (1,)), ((), ())),
    "tn": (((0,), (0,)), ((), ())),
}


def _cp(sem):
    return pltpu.CompilerParams(dimension_semantics=sem, vmem_limit_bytes=VMEM_LIMIT)


def _mesh_pos():
    return lax.axis_index("x"), lax.axis_index("y"), lax.axis_index("c")


def _mm(name, pairs, *, mode, grid, out_shape, o_spec, nk=1, kaxis=None, add=None, deps=(), prev=None):
    npair = len(pairs)
    has_add = add is not None

    def body(*refs):
        ab = refs[: 2 * npair]
        pos = 2 * npair
        add_ref = refs[pos] if has_add else None
        pos += int(has_add) + len(deps) + (prev is not None)
        o_ref = refs[pos]
        acc_ref = refs[pos + 1] if nk > 1 else None
        d = None
        for p in range(npair):
            t = lax.dot_general(ab[2 * p][...], ab[2 * p + 1][...], _DIMS[mode], preferred_element_type=F32)
            d = t if d is None else d + t
        if nk == 1:
            if has_add:
                d = d + add_ref[...].astype(F32)
            o_ref[...] = d.astype(o_ref.dtype)
        else:
            k = pl.program_id(kaxis)

            @pl.when(k == 0)
            def _():
                acc_ref[...] = d

            @pl.when(k > 0)
            def _():
                acc_ref[...] += d

            @pl.when(k == nk - 1)
            def _():
                r = acc_ref[...]
                if has_add:
                    r = r + add_ref[...].astype(F32)
                o_ref[...] = r.astype(o_ref.dtype)

    args, specs = [], []
    for a, a_spec, b, b_spec in pairs:
        args += [a, b]
        specs += [a_spec, b_spec]
    if has_add:
        args.append(add[0])
        specs.append(add[1])
    args += list(deps)
    specs += [ANY] * len(deps)
    aliases = {}
    if prev is not None:
        aliases = {len(args): 0}
        args.append(prev)
        specs.append(ANY)
    scratch = []
    if nk > 1:
        blk = [d for d in o_spec.block_shape if d is not None]
        scratch = [pltpu.VMEM(tuple(blk), F32)]
    sem = tuple("arbitrary" if (nk > 1 and ax == kaxis) else "parallel" for ax in range(len(grid)))
    return pl.pallas_call(
        body, name=name, grid=grid, in_specs=specs, out_specs=o_spec, out_shape=out_shape,
        scratch_shapes=scratch, input_output_aliases=aliases, compiler_params=_cp(sem),
    )(*args)


def _tile_span(n_tiles, part):
    if part is None:
        return 0, n_tiles
    p, of = part
    return p * n_tiles // of, (p + 1) * n_tiles // of


def _tile(n, pref):
    if n <= pref:
        return n
    for t in range(pref, 0, -LANES):
        if t % LANES == 0 and n % t == 0:
            return t
    raise ValueError(f"no tile for {n}")


def _mm_nn(name, a, b, out_dtype, add=None, tk=None, deps=(), part=None, prev=None):
    m, kk = a.shape
    n = b.shape[1]
    tm, tn = _tile(m, 1024), _tile(n, 512)
    out_shape = jax.ShapeDtypeStruct((m, n), out_dtype)
    if tk is None or tk == kk:
        j0, j1 = _tile_span(n // tn, part)
        grid = (m // tm, j1 - j0)
        pairs = [(a, pl.BlockSpec((tm, kk), lambda i, j: (i, 0)), b, pl.BlockSpec((kk, tn), lambda i, j: (0, j0 + j)))]
        o_spec = pl.BlockSpec((tm, tn), lambda i, j: (i, j0 + j))
        add_ = None if add is None else (add, pl.BlockSpec((tm, tn), lambda i, j: (i, j0 + j)))
        return _mm(name, pairs, mode="nn", grid=grid, out_shape=out_shape, o_spec=o_spec, add=add_, deps=deps, prev=prev)
    tn = _tile(n, 1024)
    nk = kk // tk
    grid = (m // tm, n // tn, nk)
    pairs = [(a, pl.BlockSpec((tm, tk), lambda i, j, k: (i, k)), b, pl.BlockSpec((tk, tn), lambda i, j, k: (k, j)))]
    o_spec = pl.BlockSpec((tm, tn), lambda i, j, k: (i, j))
    add_ = None if add is None else (add, pl.BlockSpec((tm, tn), lambda i, j, k: (i, j)))
    return _mm(name, pairs, mode="nn", grid=grid, out_shape=out_shape, o_spec=o_spec, nk=nk, kaxis=2, add=add_, deps=deps)


def _mm_nt(name, abs_, out_dtype, tk, deps=()):
    m, kk = abs_[0][0].shape
    n = abs_[0][1].shape[0]
    tm = _tile(m, 1024)
    nk = kk // tk
    tn = _tile(n, 512 if nk == 1 else 1024)
    out_shape = jax.ShapeDtypeStruct((m, n), out_dtype)
    if nk == 1:
        grid = (m // tm, n // tn)
        pairs = [(a, pl.BlockSpec((tm, kk), lambda i, j: (i, 0)), b, pl.BlockSpec((tn, kk), lambda i, j: (j, 0)))
                 for a, b in abs_]
        o_spec = pl.BlockSpec((tm, tn), lambda i, j: (i, j))
        return _mm(name, pairs, mode="nt", grid=grid, out_shape=out_shape, o_spec=o_spec, deps=deps)
    grid = (m // tm, n // tn, nk)
    pairs = [(a, pl.BlockSpec((tm, tk), lambda i, j, k: (i, k)), b, pl.BlockSpec((tn, tk), lambda i, j, k: (j, k)))
             for a, b in abs_]
    o_spec = pl.BlockSpec((tm, tn), lambda i, j, k: (i, j))
    return _mm(name, pairs, mode="nt", grid=grid, out_shape=out_shape, o_spec=o_spec, nk=nk, kaxis=2, deps=deps)


def _mm_tn(name, a, b, out_dtype, deps=()):
    t, m = a.shape
    n = b.shape[1]
    tm, tn = _tile(m, 512), _tile(n, 2048)
    if n > m:
        grid = (n // tn, m // tm)
        a_map, b_map, o_map = (lambda j, i: (0, i)), (lambda j, i: (0, j)), (lambda j, i: (i, j))
    else:
        grid = (m // tm, n // tn)
        a_map, b_map, o_map = (lambda i, j: (0, i)), (lambda i, j: (0, j)), (lambda i, j: (i, j))
    pairs = [(a, pl.BlockSpec((t, tm), a_map), b, pl.BlockSpec((t, tn), b_map))]
    o_spec = pl.BlockSpec((tm, tn), o_map)
    return _mm(name, pairs, mode="tn", grid=grid, out_shape=jax.ShapeDtypeStruct((m, n), out_dtype), o_spec=o_spec,
               deps=deps)


def _proj_piece(name, h, w, prev, kvec, base, count, deps=()):
    t, kk = h.shape
    own = w.dtype == F32
    nn = w.shape[1] if own else w.shape[1] // N_CHIPS
    tm, tn = _tile(t, 1024), _tile(nn, 512)
    nb = nn // tn

    def body(kv_ref, h_ref, w_ref, *rest):
        rest[-1][...] = lax.dot_general(h_ref[...], w_ref[...].astype(BF16), _DIMS["nn"],
                                        preferred_element_type=F32).astype(BF16)

    cols = lambda s, i, j, kv: (0, j) if own else (0, kv[base + s] * nb + j)
    extra = ([] if prev is None else [prev]) + list(deps)
    grid_spec = pltpu.PrefetchScalarGridSpec(
        num_scalar_prefetch=1, grid=(count, t // tm, nb),
        in_specs=[pl.BlockSpec((tm, kk), lambda s, i, j, kv: (i, 0)), pl.BlockSpec((kk, tn), cols)] + [ANY] * len(extra),
        out_specs=pl.BlockSpec((tm, tn), lambda s, i, j, kv: (i, kv[base + s] * nb + j)))
    return pl.pallas_call(body, name=name, grid_spec=grid_spec, out_shape=jax.ShapeDtypeStruct((t, N_CHIPS * nn), BF16),
                          input_output_aliases={} if prev is None else {3: 0},
                          compiler_params=_cp(("parallel",) * 3))(kvec, h, w, *extra)


def _gmm_nn(name, p, w, out_dtype):
    t = p.shape[0]
    g, cg, dg = w.shape
    tm = _tile(t, 1024)
    pairs = [(p, pl.BlockSpec((tm, cg), lambda i, j: (i, j)), w, pl.BlockSpec((None, cg, dg), lambda i, j: (j, 0, 0)))]
    o_spec = pl.BlockSpec((tm, dg), lambda i, j: (i, j))
    return _mm(name, pairs, mode="nn", grid=(t // tm, g), out_shape=jax.ShapeDtypeStruct((t, g * dg), out_dtype),
               o_spec=o_spec)


def _gmm_nt(name, dy, w, out_dtype):
    t = dy.shape[0]
    g, cg, dg = w.shape
    tm = _tile(t, 1024)
    pairs = [(dy, pl.BlockSpec((tm, dg), lambda i, j: (i, j)), w, pl.BlockSpec((None, cg, dg), lambda i, j: (j, 0, 0)))]
    o_spec = pl.BlockSpec((tm, cg), lambda i, j: (i, j))
    return _mm(name, pairs, mode="nt", grid=(t // tm, g), out_shape=jax.ShapeDtypeStruct((t, g * cg), out_dtype),
               o_spec=o_spec)


def _gmm_tn(name, p, dy, g, out_dtype):
    t = p.shape[0]
    cg, dg = p.shape[1] // g, dy.shape[1] // g
    pairs = [(p, pl.BlockSpec((t, cg), lambda j: (0, j)), dy, pl.BlockSpec((t, dg), lambda j: (0, j)))]
    o_spec = pl.BlockSpec((None, cg, dg), lambda j: (j, 0, 0))
    return _mm(name, pairs, mode="tn", grid=(g,), out_shape=jax.ShapeDtypeStruct((g, cg, dg), out_dtype), o_spec=o_spec)


ROW_TILE = 256


def _rows(t):
    return _tile8(t, ROW_TILE)


def _tile8(n, pref):
    if n <= pref:
        return n
    for t in range(pref, 0, -8):
        if n % t == 0:
            return t
    raise ValueError(f"no row tile for {n}")


def _cast_place(name, w, pos, shard, deps=()):
    tr = _tile8(w.R, 512)
    if w.colshard:
        o_map = lambda h, i, pos: (0, h, i, pos[1])
    else:
        o_map = lambda h, i, pos: (pos[1], h, i, 0)

    def body(pos_ref, w_ref, *rest):
        rest[-1][...] = w_ref[...].astype(BF16)

    grid_spec = pltpu.PrefetchScalarGridSpec(
        num_scalar_prefetch=1, grid=(2, w.R // tr),
        in_specs=[pl.BlockSpec((None, tr, w.nn), lambda h, i, pos: (h, i, 0))] + [ANY] * len(deps),
        out_specs=pl.BlockSpec((None, None, tr, w.nn), o_map))
    return pl.pallas_call(body, name=name, grid_spec=grid_spec, out_shape=jax.ShapeDtypeStruct((w.P, 2, w.R, w.N), BF16),
                          compiler_params=_cp(("parallel", "parallel")))(pos, shard, *deps)


def _rms_fwd(name, x, g, deps=()):
    t, d = x.shape
    tm = _rows(t)

    def body(x_ref, g_ref, *rest):
        xf = x_ref[...]
        r = lax.rsqrt(jnp.mean(xf * xf, axis=-1, keepdims=True) + EPS)
        rest[-1][...] = (xf * r * g_ref[...]).astype(BF16)

    return pl.pallas_call(
        body, name=name, grid=(t // tm,),
        in_specs=[pl.BlockSpec((tm, d), lambda i: (i, 0)), pl.BlockSpec((1, d), lambda i: (0, 0))] + [ANY] * len(deps),
        out_specs=pl.BlockSpec((tm, d), lambda i: (i, 0)), out_shape=jax.ShapeDtypeStruct((t, d), BF16),
        compiler_params=_cp(("parallel",)),
    )(x, g, *deps)


def _rms_bwd(name, x, g, dh, dres, want_bf16, deps=()):
    t, d = x.shape
    tm = _rows(t)

    def body(x_ref, g_ref, dh_ref, dres_ref, *rest):
        rest = rest[len(deps):]
        dx_ref, rest = rest[0], rest[1:]
        dg_ref = rest[-1]
        xf = x_ref[...]
        r = lax.rsqrt(jnp.mean(xf * xf, axis=-1, keepdims=True) + EPS)
        xh = xf * r
        dhf = dh_ref[...]
        dxh = dhf * g_ref[...]
        m = jnp.mean(dxh * xh, axis=-1, keepdims=True)
        dx = dres_ref[...] + r * (dxh - xh * m)
        dx_ref[...] = dx
        if want_bf16:
            rest[0][...] = dx.astype(BF16)

        @pl.when(pl.program_id(0) == 0)
        def _():
            dg_ref[...] = jnp.zeros_like(dg_ref)

        dg_ref[...] += jnp.sum(dhf * xh, axis=0, keepdims=True)

    row = pl.BlockSpec((tm, d), lambda i: (i, 0))
    vec = pl.BlockSpec((1, d), lambda i: (0, 0))
    out_specs = [row] + ([row] if want_bf16 else []) + [vec]
    out_shape = ([jax.ShapeDtypeStruct((t, d), F32)] + ([jax.ShapeDtypeStruct((t, d), BF16)] if want_bf16 else [])
                 + [jax.ShapeDtypeStruct((1, d), F32)])
    return pl.pallas_call(body, name=name, grid=(t // tm,), in_specs=[row, vec, row, row] + [ANY] * len(deps),
                          out_specs=out_specs, out_shape=out_shape, compiler_params=_cp(("arbitrary",)))(x, g, dh, dres, *deps)


def _final_bwd(name, x3, gf, tgt):
    t, d = x3.shape
    tm = _rows(t)

    def body(x_ref, g_ref, t_ref, dx_ref, dxb_ref, dg_ref, lc_ref):
        xf = x_ref[...]
        g = g_ref[...]
        r = lax.rsqrt(jnp.mean(xf * xf, axis=-1, keepdims=True) + EPS)
        xh = xf * r
        diff = xh * g - t_ref[...]
        dy = diff * (1.0 / d)
        dxh = dy * g
        m = jnp.mean(dxh * xh, axis=-1, keepdims=True)
        dx = r * (dxh - xh * m)
        dx_ref[...] = dx
        dxb_ref[...] = dx.astype(BF16)

        @pl.when(pl.program_id(0) == 0)
        def _():
            dg_ref[...] = jnp.zeros_like(dg_ref)
            lc_ref[...] = jnp.zeros_like(lc_ref)

        dg_ref[...] += jnp.sum(dy * xh, axis=0, keepdims=True)
        lc_ref[...] += jnp.sum(diff * diff, axis=0, keepdims=True) * (0.5 / d)

    row = pl.BlockSpec((tm, d), lambda i: (i, 0))
    vec = pl.BlockSpec((1, d), lambda i: (0, 0))
    return pl.pallas_call(
        body, name=name, grid=(t // tm,), in_specs=[row, vec, row], out_specs=[row, row, vec, vec],
        out_shape=[jax.ShapeDtypeStruct((t, d), F32), jax.ShapeDtypeStruct((t, d), BF16),
                   jax.ShapeDtypeStruct((1, d), F32), jax.ShapeDtypeStruct((1, d), F32)],
        compiler_params=_cp(("arbitrary",)),
    )(x3, gf, tgt)


def _shift_down(v, k, t_idx):
    return jnp.where(t_idx >= k, pltpu.roll(v, k, 0), 0.0)


def _shift_up(v, k, t_idx):
    n = v.shape[0]
    return jnp.where(t_idx < n - k, pltpu.roll(v, n - k, 0), 0.0)


def _window_sums(v, shift, t_idx, grp):
    s = v + shift(v, 1, t_idx)
    out = s
    for lvl in range(1, len(POOL_WINDOWS)):
        s = s + shift(s, 1 << lvl, t_idx)
        out = jnp.where(grp >= lvl, s, out)
    return out


def _window_count(t_idx, grp):
    return jnp.minimum(t_idx + 1, jnp.left_shift(2, grp)).astype(F32)


MIX_COLS = 128


def _mixer_fwd(name, proj, cw, cb, n_conv, n_groups, deps=()):
    t = proj.shape[0]
    nb = n_conv // MIX_COLS
    per_group = n_conv // n_groups // MIX_COLS

    def body(ba_ref, ca_ref, va_ref, vb_ref, cw_ref, cb_ref, *rest):
        z_ref, p_ref = rest[len(deps):]
        t_idx = lax.broadcasted_iota(jnp.int32, (t, MIX_COLS), 0)
        q = ca_ref[...].astype(F32) * va_ref[...].astype(F32)
        w = cw_ref[...]
        u = cb_ref[...] + w[0:1] * _shift_down(q, 2, t_idx) + w[1:2] * _shift_down(q, 1, t_idx) + w[2:3] * q
        z_ref[...] = (ba_ref[...].astype(F32) * u).astype(BF16)
        grp = pl.program_id(0) // per_group
        v = vb_ref[...].astype(F32)
        p_ref[...] = (_window_sums(v, _shift_down, t_idx, grp) / _window_count(t_idx, grp) - v).astype(BF16)

    col = lambda s: pl.BlockSpec((t, MIX_COLS), lambda j: (0, s * nb + j))
    return pl.pallas_call(
        body, name=name, grid=(nb,),
        in_specs=[col(0), col(1), col(2), col(3), pl.BlockSpec((3, MIX_COLS), lambda j: (0, j)),
                  pl.BlockSpec((1, MIX_COLS), lambda j: (0, j))] + [ANY] * len(deps),
        out_specs=[col(0), col(0)],
        out_shape=[jax.ShapeDtypeStruct((t, n_conv), BF16), jax.ShapeDtypeStruct((t, n_conv), BF16)],
        compiler_params=_cp(("parallel",)),
    )(proj, proj, proj, proj, cw, cb, *deps)


def _mixer_bwd(name, dz, dp, proj, cw, cb, dproj, n_conv, n_groups, deps=()):
    t = proj.shape[0]
    nb = n_conv // MIX_COLS
    per_group = n_conv // n_groups // MIX_COLS

    def body(dz_ref, dp_ref, ba_ref, ca_ref, va_ref, cw_ref, cb_ref, _, *rest):
        o_ref, dcw_ref, dcb_ref, scr = rest[len(deps):]
        s = pl.program_id(1)

        @pl.when(s == 0)
        def _():
            t_idx = lax.broadcasted_iota(jnp.int32, (t, MIX_COLS), 0)
            ca, va = ca_ref[...].astype(F32), va_ref[...].astype(F32)
            q = ca * va
            q1, q2 = _shift_down(q, 1, t_idx), _shift_down(q, 2, t_idx)
            w = cw_ref[...]
            u = cb_ref[...] + w[0:1] * q2 + w[1:2] * q1 + w[2:3] * q
            dzf = dz_ref[...].astype(F32)
            du = dzf * ba_ref[...].astype(F32)
            scr[0] = (dzf * u).astype(BF16)
            dq = w[2:3] * du + w[1:2] * _shift_up(du, 1, t_idx) + w[0:1] * _shift_up(du, 2, t_idx)
            scr[1] = (dq * va).astype(BF16)
            scr[2] = (dq * ca).astype(BF16)
            dcb_ref[...] = jnp.sum(du, axis=0, keepdims=True)
            dcw_ref[0:1, :] = jnp.sum(du * q2, axis=0, keepdims=True)
            dcw_ref[1:2, :] = jnp.sum(du * q1, axis=0, keepdims=True)
            dcw_ref[2:3, :] = jnp.sum(du * q, axis=0, keepdims=True)
            grp = pl.program_id(0) // per_group
            dpf = dp_ref[...].astype(F32)
            e = dpf / _window_count(t_idx, grp)
            scr[3] = (_window_sums(e, _shift_up, t_idx, grp) - dpf).astype(BF16)

        o_ref[...] = scr[s]

    col = lambda c: pl.BlockSpec((t, MIX_COLS), lambda j, s: (0, c * nb + j))
    own = pl.BlockSpec((t, MIX_COLS), lambda j, s: (0, j))
    return pl.pallas_call(
        body, name=name, grid=(nb, 4),
        in_specs=[own, own, col(0), col(1), col(2), pl.BlockSpec((3, MIX_COLS), lambda j, s: (0, j)),
                  pl.BlockSpec((1, MIX_COLS), lambda j, s: (0, j)), ANY] + [ANY] * len(deps),
        out_specs=[pl.BlockSpec((t, MIX_COLS), lambda j, s: (0, s * nb + j)),
                   pl.BlockSpec((3, MIX_COLS), lambda j, s: (0, j)), pl.BlockSpec((1, MIX_COLS), lambda j, s: (0, j))],
        out_shape=[jax.ShapeDtypeStruct(dproj.shape, BF16), jax.ShapeDtypeStruct((3, n_conv), F32),
                   jax.ShapeDtypeStruct((1, n_conv), F32)],
        scratch_shapes=[pltpu.VMEM((4, t, MIX_COLS), BF16)],
        input_output_aliases={7: 0},
        compiler_params=_cp(("arbitrary", "arbitrary")),
    )(dz, dp, proj, proj, proj, cw, cb, dproj, *deps)


def _merge_fwd(name, proj, bg, ya, yb, ps):
    t, d = ya.shape
    tm = _rows(t)

    def body(gab_ref, bg_ref, ya_ref, yb_ref, ps_ref, o_ref):
        gab = gab_ref[...].astype(F32) + bg_ref[...]
        sa, sb = jax.nn.sigmoid(gab[:, :d]), jax.nn.sigmoid(gab[:, d:])
        o_ref[...] = (sa * ya_ref[...].astype(F32) + sb * (yb_ref[...].astype(F32) * ps_ref[...])).astype(BF16)

    row = pl.BlockSpec((tm, d), lambda i: (i, 0))
    return pl.pallas_call(
        body, name=name, grid=(t // tm,),
        in_specs=[pl.BlockSpec((tm, 2 * d), lambda i: (i, 1)), pl.BlockSpec((1, 2 * d), lambda i: (0, 0)), row, row,
                  pl.BlockSpec((1, d), lambda i: (0, 0))],
        out_specs=row, out_shape=jax.ShapeDtypeStruct((t, d), BF16), compiler_params=_cp(("parallel",)),
    )(proj, bg, ya, yb, ps)


def _merge_bwd(name, dm, proj, bg, ya, yb, ps, deps=()):
    t, d = ya.shape
    tm = _rows(t)

    def body(dm_ref, gab_ref, bg_ref, ya_ref, yb_ref, ps_ref, *rest):
        dya_ref, dyb_ref, dg_ref, dba_ref, dbb_ref, dps_ref = rest[len(deps):]
        gab = gab_ref[...].astype(F32) + bg_ref[...]
        sa, sb = jax.nn.sigmoid(gab[:, :d]), jax.nn.sigmoid(gab[:, d:])
        dmf = dm_ref[...].astype(F32)
        ybf, ps_ = yb_ref[...].astype(F32), ps_ref[...]
        dya_ref[...] = (dmf * sa).astype(BF16)
        dyb = dmf * sb
        dyb_ref[...] = (dyb * ps_).astype(BF16)
        dga = dmf * ya_ref[...].astype(F32) * sa * (1.0 - sa)
        dgb = dmf * (ybf * ps_) * sb * (1.0 - sb)
        dg_ref[:, :d] = dga.astype(BF16)
        dg_ref[:, d:] = dgb.astype(BF16)

        @pl.when(pl.program_id(0) == 0)
        def _():
            dba_ref[...] = jnp.zeros_like(dba_ref)
            dbb_ref[...] = jnp.zeros_like(dbb_ref)
            dps_ref[...] = jnp.zeros_like(dps_ref)

        dba_ref[...] += jnp.sum(dga, axis=0, keepdims=True)
        dbb_ref[...] += jnp.sum(dgb, axis=0, keepdims=True)
        dps_ref[...] += jnp.sum(dyb * ybf, axis=0, keepdims=True)

    row = pl.BlockSpec((tm, d), lambda i: (i, 0))
    vec = pl.BlockSpec((1, d), lambda i: (0, 0))
    gates = pl.BlockSpec((tm, 2 * d), lambda i: (i, 1))
    return pl.pallas_call(
        body, name=name, grid=(t // tm,),
        in_specs=[row, gates, pl.BlockSpec((1, 2 * d), lambda i: (0, 0)), row, row, vec] + [ANY] * len(deps),
        out_specs=[row, row, gates, vec, vec, vec],
        out_shape=[jax.ShapeDtypeStruct((t, d), BF16), jax.ShapeDtypeStruct((t, d), BF16),
                   jax.ShapeDtypeStruct(proj.shape, BF16), jax.ShapeDtypeStruct((1, d), F32),
                   jax.ShapeDtypeStruct((1, d), F32), jax.ShapeDtypeStruct((1, d), F32)],
        compiler_params=_cp(("arbitrary",)),
    )(dm, proj, bg, ya, yb, ps, *deps)


def _ffn_up_act(name, h, w_up, gate, part=None, prev=None, deps=()):
    t, d = h.shape
    f = w_up.shape[1]
    tm, tf = _tile(t, 1024), _tile(f, 512)
    j0, j1 = _tile_span(f // tf, part)
    n_prev = 0 if prev is None else 2
    extra = ([] if prev is None else list(prev)) + list(deps)

    def body(h_ref, w_ref, g_ref, *rest):
        u_ref, a_ref = rest[len(extra):]
        u = lax.dot_general(h_ref[...], w_ref[...], _DIMS["nn"], preferred_element_type=F32)
        g = g_ref[...].astype(F32)
        u_ref[...] = u.astype(BF16)
        a_ref[...] = (g * jax.nn.sigmoid(g) * u).astype(BF16)

    blk = pl.BlockSpec((tm, tf), lambda i, j: (i, j0 + j))
    shp = jax.ShapeDtypeStruct((t, f), BF16)
    return pl.pallas_call(
        body, name=name, grid=(t // tm, j1 - j0),
        in_specs=[pl.BlockSpec((tm, d), lambda i, j: (i, 0)), pl.BlockSpec((d, tf), lambda i, j: (0, j0 + j)), blk]
        + [ANY] * len(extra),
        out_specs=[blk, blk], out_shape=[shp, shp], input_output_aliases={3 + i: i for i in range(n_prev)},
        compiler_params=_cp(("parallel", "parallel")))(h, w_up, gate, *extra)


def _ffn_bwd(name, dy, w_down, gate, up):
    t, d = dy.shape
    f = w_down.shape[0]
    tm, tf = _tile(t, 1024), _tile(f, 512)

    def body(dy_ref, w_ref, g_ref, u_ref, dg_ref, du_ref):
        da = lax.dot_general(dy_ref[...], w_ref[...], _DIMS["nt"], preferred_element_type=F32)
        g = g_ref[...].astype(F32)
        s = jax.nn.sigmoid(g)
        du_ref[...] = (da * (g * s)).astype(BF16)
        dg_ref[...] = (da * u_ref[...].astype(F32) * (s * (1.0 + g * (1.0 - s)))).astype(BF16)

    blk = pl.BlockSpec((tm, tf), lambda i, j: (i, j))
    shp = jax.ShapeDtypeStruct((t, f), BF16)
    return pl.pallas_call(
        body, name=name, grid=(t // tm, f // tf),
        in_specs=[pl.BlockSpec((tm, d), lambda i, j: (i, 0)), pl.BlockSpec((tf, d), lambda i, j: (j, 0)), blk, blk],
        out_specs=[blk, blk], out_shape=[shp, shp], compiler_params=_cp(("parallel", "parallel")))(dy, w_down, gate, up)


def _adamw_math(w, g, m, v):
    m = ADAM_B1 * m + (1.0 - ADAM_B1) * g
    v = ADAM_B2 * v + (1.0 - ADAM_B2) * (g * g)
    m_hat = m / (1.0 - ADAM_B1 ** ADAM_STEP)
    v_hat = v / (1.0 - ADAM_B2 ** ADAM_STEP)
    delta = -ADAM_LR * (m_hat / (jnp.sqrt(v_hat) + ADAM_EPS) + ADAM_WD * w)
    return delta, m, v


def _adamw(name, w, g, m, v):
    r, c = w.shape
    tr = _tile8(r, 512 if c <= 1024 else 256)

    def body(w_ref, g_ref, m_ref, v_ref, go_ref, d_ref, nm_ref, nv_ref):
        g = g_ref[...]
        go_ref[...] = g
        d_ref[...], nm_ref[...], nv_ref[...] = _adamw_math(w_ref[...], g, m_ref[...], v_ref[...])

    blk = pl.BlockSpec((tr, c), lambda i: (i, 0))
    shp = jax.ShapeDtypeStruct((r, c), F32)
    return pl.pallas_call(body, name=name, grid=(r // tr,), in_specs=[blk] * 4, out_specs=[blk] * 4,
                          out_shape=[shp] * 4, compiler_params=_cp(("parallel",)))(w, g, m, v)


class _Weight:
    def __init__(self, name, rows, cols, colshard):
        self.name, self.colshard = name, colshard
        self.R, self.nn = rows // 2, cols
        self.P = 1 if colshard else N_CHIPS
        self.N = N_CHIPS * cols if colshard else cols

    def cols(self, k):
        return pl.ds(pl.multiple_of(k * self.nn, LANES), self.nn)

    def shard(self, ref, k):
        return ref.at[0, :, :, self.cols(k)] if self.colshard else ref.at[k]

    def half(self, ref, k, h):
        return ref.at[0, h, :, self.cols(k)] if self.colshard else ref.at[k, h]

    def quarter(self, ref, k, h, q):
        return self.half(ref, k, h).at[pl.ds(q * (self.R // 2), self.R // 2), :]

    def part(self, ref, k):
        return ref.at[0, :, self.cols(k)] if self.colshard else ref.at[k]


def _remote(src, dst, ssem, rsem, dev):
    return pltpu.make_async_remote_copy(src_ref=src, dst_ref=dst, send_sem=ssem, recv_sem=rsem, device_id=dev,
                                        device_id_type=MESH)


def _other_chips(x, y):
    chips = [(1 - x, y), (x, 1 - y), (1 - x, 1 - y)]
    return chips, [2 * cx + cy for cx, cy in chips]


def _hbm(a):
    return pltpu.with_memory_space_constraint(a, pltpu.HBM)


def _gather_start(name, groups, lands, after=()):
    flat = [w for grp in groups for w in grp]
    nw, ng = len(flat), len(groups)

    def body(*refs):
        land = refs[:nw]
        sems = refs[nw + len(after):nw + len(after) + 2 * ng]
        token = refs[2 * nw + len(after) + 2 * ng]
        x, y, c = _mesh_pos()
        k_me = 2 * x + y
        chips, _ = _other_chips(x, y)
        i = 0
        for g, grp in enumerate(groups):
            for wi, w in enumerate(grp):
                mine = w.half(land[i], k_me, c)
                for j, chip in enumerate(chips):
                    _remote(mine, mine, sems[2 * g].at[3 * wi + j], sems[2 * g + 1].at[3 * wi + j], (*chip, c)).start()
                i += 1
        token[...] = jnp.zeros_like(token)

    sem_shapes = []
    for grp in groups:
        sem_shapes += [pltpu.SemaphoreType.DMA((3 * len(grp),))] * 2
    out = pl.pallas_call(
        body, name=name, in_specs=[HBM] * nw + [ANY] * len(after),
        out_specs=[SEM] * (2 * ng) + [HBM] * nw + [VMEM],
        out_shape=sem_shapes + [pltpu.HBM(a.shape, a.dtype) for a in lands] + [jax.ShapeDtypeStruct((8, LANES), F32)],
        input_output_aliases={i: 2 * ng + i for i in range(nw)},
        compiler_params=pltpu.CompilerParams(has_side_effects=EFFECT),
    )(*[_hbm(a) for a in lands], *after)
    sems = [(out[2 * g], out[2 * g + 1]) for g in range(ng)]
    return sems, list(out[2 * ng:2 * ng + nw]), out[-1]


def _gather_wait(name, grp, lands, ssem, rsem, after):
    n = len(grp)

    def body(*refs):
        land, ssem_ref, rsem_ref = refs[:n], refs[n], refs[n + 1]
        x, y, c = _mesh_pos()
        k_me = 2 * x + y
        chips, ks = _other_chips(x, y)
        for wi, w in enumerate(grp):
            for j, chip in enumerate(chips):
                cp = _remote(w.half(land[wi], k_me, c), w.half(land[wi], ks[j], c), ssem_ref.at[3 * wi + j],
                             rsem_ref.at[3 * wi + j], (*chip, c))
                cp.wait_send()
                cp.wait_recv()

    return pl.pallas_call(
        body, name=name, in_specs=[HBM] * n + [SEM, SEM, ANY], out_specs=[HBM] * n,
        out_shape=[pltpu.HBM(a.shape, a.dtype) for a in lands], input_output_aliases={i: i for i in range(n)},
        compiler_params=pltpu.CompilerParams(has_side_effects=EFFECT),
    )(*lands, ssem, rsem, after)


def _split_start(name, arrays, n, copies, after=()):
    na = len(arrays)

    def body(*refs):
        ssem, rsem, token = refs[na + len(after):][0], refs[na + len(after):][1], refs[2 * na + len(after) + 2]
        for i, (src, dst, dev, _) in enumerate(copies(refs[:na], *_mesh_pos())):
            _remote(src, dst, ssem.at[i], rsem.at[i], dev).start()
        token[...] = jnp.zeros_like(token)

    out = pl.pallas_call(
        body, name=name, in_specs=[HBM] * na + [ANY] * len(after), out_specs=[SEM, SEM] + [HBM] * na + [VMEM],
        out_shape=[pltpu.SemaphoreType.DMA((n,))] * 2 + [pltpu.HBM(a.shape, a.dtype) for a in arrays]
        + [jax.ShapeDtypeStruct((8, LANES), F32)],
        input_output_aliases={i: 2 + i for i in range(na)},
        compiler_params=pltpu.CompilerParams(has_side_effects=EFFECT),
    )(*[_hbm(a) for a in arrays], *after)
    return out[0], out[1], list(out[2:2 + na]), out[-1]


def _split_wait(name, arrays, ssem, rsem, copies, after):
    na = len(arrays)

    def body(*refs):
        for i, (src, _, dev, dst) in enumerate(copies(refs[:na], *_mesh_pos())):
            cp = _remote(src, dst, refs[na].at[i], refs[na + 1].at[i], dev)
            cp.wait_send()
            cp.wait_recv()

    return list(pl.pallas_call(
        body, name=name, in_specs=[HBM] * na + [SEM, SEM] + [ANY] * len(after), out_specs=[HBM] * na,
        out_shape=[pltpu.HBM(a.shape, a.dtype) for a in arrays], input_output_aliases={i: i for i in range(na)},
        compiler_params=pltpu.CompilerParams(has_side_effects=EFFECT),
    )(*arrays, ssem, rsem, *after))


def _pass_copies(grp, rels=(0, 1, 2)):
    def copies(land, x, y, c):
        _, ks = _other_chips(x, y)
        return [(w.half(land[wi], ks[j], c), w.half(land[wi], ks[j], c), (x, y, 1 - c), w.half(land[wi], ks[j], 1 - c))
                for wi, w in enumerate(grp) for j in rels]
    copies.n = len(grp) * len(rels)
    return copies


def _near_copies(grp):
    def copies(land, x, y, c):
        chips, ks = _other_chips(x, y)
        out = []
        for wi, w in enumerate(grp):
            mine = w.half(land[wi], 2 * x + y, c)
            out += [(mine, mine, (*chips[j], c), w.half(land[wi], ks[j], c)) for j in (0, 1)]
        return out
    copies.n = 2 * len(grp)
    return copies


def _far_copies(grp):
    def copies(land, x, y, c):
        chips, ks = _other_chips(x, y)
        out = []
        for wi, w in enumerate(grp):
            for j in (0, 1):
                q = w.quarter(land[wi], ks[j], c, j)
                out.append((q, q, (*chips[1 - j], c), w.quarter(land[wi], ks[2], c, j)))
        return out
    copies.n = 2 * len(grp)
    return copies


def _pair_copies(n):
    def copies(refs, x, y, c):
        return [(refs[i].at[:, 1 - c], refs[n + i], (x, y, 1 - c), refs[n + i]) for i in range(n)]
    return copies


def _share_copies(n):
    def copies(refs, x, y, c):
        return [(refs[i].at[c], refs[i].at[c], (x, y, 1 - c), refs[i].at[1 - c]) for i in range(n)]
    return copies


def _gather_conv_w(cw):
    ncw = cw.shape[1]

    def body(cw_ref, out_ref, ssem, rsem):
        x, y, c = _mesh_pos()
        k_me = 2 * x + y
        chips, ks = _other_chips(x, y)
        cols = lambda k: out_ref.at[:, pl.ds(pl.multiple_of(k * ncw, LANES), ncw)]
        cps = [_remote(cw_ref, cols(k_me), ssem.at[j], rsem.at[j], (*chip, c)) for j, chip in enumerate(chips)]
        for cp in cps:
            cp.start()
        for k in range(N_CHIPS):
            @pl.when(k_me == k)
            def _():
                out_ref[:, k * ncw:(k + 1) * ncw] = cw_ref[...]
        for j in range(3):
            _remote(cw_ref, cols(ks[j]), ssem.at[j], rsem.at[j], (*chips[j], c)).wait_recv()
        for cp in cps:
            cp.wait_send()

    return pl.pallas_call(
        body, name="gather_conv_w", in_specs=[VMEM], out_specs=VMEM,
        out_shape=jax.ShapeDtypeStruct((3, N_CHIPS * ncw), F32),
        scratch_shapes=[pltpu.SemaphoreType.DMA((3,)), pltpu.SemaphoreType.DMA((3,))],
    )(cw)


def _grad_tiles(w, n):
    return _tile8(w.R, 512) if w.R <= 512 else w.R // 2, _tile(n, 2048)


def _pair_sum(name, w, pos, grad, got):
    tr, tn = _grad_tiles(w, w.N)

    def body(pos_ref, g_ref, r_ref, o_ref):
        o_ref[...] = (g_ref[...].astype(F32) + r_ref[...].astype(F32)).astype(BF16)

    blk = pl.BlockSpec((None, tr, tn), lambda p, i, j, pos: (p, i, j))
    grid_spec = pltpu.PrefetchScalarGridSpec(
        num_scalar_prefetch=1, grid=(w.P, w.R // tr, w.N // tn),
        in_specs=[pl.BlockSpec((None, None, tr, tn), lambda p, i, j, pos: (p, pos[0], i, j)), blk], out_specs=blk)
    return pl.pallas_call(body, name=name, grid_spec=grid_spec, out_shape=jax.ShapeDtypeStruct((w.P, w.R, w.N), BF16),
                          compiler_params=_cp(("parallel",) * 3))(pos, grad, got)


def _scatter_start(name, ws, pairs):
    nw = len(ws)

    def body(*refs):
        pr, land = refs[:nw], refs[nw:2 * nw]
        ssem, rsem = refs[2 * nw], refs[2 * nw + 1]
        token = refs[4 * nw + 2]
        x, y, c = _mesh_pos()
        chips, ks = _other_chips(x, y)
        for i, w in enumerate(ws):
            for j, chip in enumerate(chips):
                _remote(w.part(pr[i], ks[j]), land[i].at[j], ssem.at[3 * i + j], rsem.at[3 * i + j], (*chip, c)).start()
        token[...] = jnp.zeros_like(token)

    lands = [lax.empty((3, w.R, w.nn), BF16) for w in ws]
    out = pl.pallas_call(
        body, name=name, in_specs=[HBM] * (2 * nw),
        out_specs=[SEM, SEM] + [HBM] * (2 * nw) + [VMEM],
        out_shape=[pltpu.SemaphoreType.DMA((3 * nw,))] * 2 + [pltpu.HBM(a.shape, a.dtype) for a in pairs + lands]
        + [jax.ShapeDtypeStruct((8, LANES), F32)],
        input_output_aliases={i: 2 + i for i in range(2 * nw)},
        compiler_params=pltpu.CompilerParams(has_side_effects=EFFECT),
    )(*[_hbm(a) for a in pairs + lands])
    return out[0], out[1], list(out[2:2 + nw]), list(out[2 + nw:2 + 2 * nw]), out[-1]


def _scatter_wait(name, ws, pairs, lands, ssem, rsem, after):
    nw = len(ws)

    def body(*refs):
        pr, land = refs[:nw], refs[nw:2 * nw]
        ssem_ref, rsem_ref = refs[2 * nw], refs[2 * nw + 1]
        x, y, c = _mesh_pos()
        chips, ks = _other_chips(x, y)
        for i, w in enumerate(ws):
            for j, chip in enumerate(chips):
                cp = _remote(w.part(pr[i], ks[j]), land[i].at[j], ssem_ref.at[3 * i + j], rsem_ref.at[3 * i + j], (*chip, c))
                cp.wait_send()
                cp.wait_recv()

    out = pl.pallas_call(
        body, name=name, in_specs=[HBM] * (2 * nw) + [SEM, SEM] + [ANY] * len(after), out_specs=[HBM] * (2 * nw),
        out_shape=[pltpu.HBM(a.shape, a.dtype) for a in pairs + lands],
        input_output_aliases={i: i for i in range(2 * nw)},
        compiler_params=pltpu.CompilerParams(has_side_effects=EFFECT),
    )(*pairs, *lands, ssem, rsem, *after)
    return list(out[nw:])


def _final_sum(name, w, pos, grad, got, parts):
    tr, tn = _grad_tiles(w, w.nn)
    nbc = w.nn // tn

    def body(pos_ref, g_ref, r_ref, p_ref, o_ref):
        acc = g_ref[...].astype(F32) + r_ref[...].astype(F32)
        for j in range(3):
            acc = acc + p_ref[j].astype(F32)
        o_ref[...] = acc

    if w.colshard:
        g_spec = pl.BlockSpec((None, None, tr, tn), lambda i, j, pos: (0, pos[0], i, pos[1] * nbc + j))
        r_spec = pl.BlockSpec((None, tr, tn), lambda i, j, pos: (0, i, pos[1] * nbc + j))
    else:
        g_spec = pl.BlockSpec((None, None, tr, tn), lambda i, j, pos: (pos[1], pos[0], i, j))
        r_spec = pl.BlockSpec((None, tr, tn), lambda i, j, pos: (pos[1], i, j))
    grid_spec = pltpu.PrefetchScalarGridSpec(
        num_scalar_prefetch=1, grid=(w.R // tr, nbc),
        in_specs=[g_spec, r_spec, pl.BlockSpec((3, tr, tn), lambda i, j, pos: (0, i, j))],
        out_specs=pl.BlockSpec((None, tr, tn), lambda i, j, pos: (pos[0], i, j)))
    return pl.pallas_call(body, name=name, grid_spec=grid_spec, out_shape=jax.ShapeDtypeStruct((2, w.R, w.nn), F32),
                          compiler_params=_cp(("parallel",) * 2))(pos, grad, got, parts)


def _share_halves(name, ws, halves, deps=()):
    nw = len(ws)

    def body(*refs):
        out = refs[nw + len(deps):2 * nw + len(deps)]
        ssem, rsem = refs[2 * nw + len(deps):]
        x, y, c = _mesh_pos()
        sib = (x, y, 1 - c)
        cps = [_remote(out[i].at[c], out[i].at[c], ssem.at[i], rsem.at[i], sib) for i in range(nw)]
        for cp in cps:
            cp.start()
        for i, cp in enumerate(cps):
            cp.wait_send()
            _remote(out[i].at[1 - c], out[i].at[1 - c], ssem.at[i], rsem.at[i], sib).wait_recv()

    return pl.pallas_call(
        body, name=name, in_specs=[ANY] * (nw + len(deps)), out_specs=[ANY] * nw,
        out_shape=[jax.ShapeDtypeStruct(h.shape, F32) for h in halves],
        scratch_shapes=[pltpu.SemaphoreType.DMA((nw,)), pltpu.SemaphoreType.DMA((nw,))],
        input_output_aliases={i: i for i in range(nw)},
    )(*halves, *deps)


VEC_ROWS = 16


def _vector_step(d, n_conv, parts, params, deps=()):
    ncw = params[2][0].shape[1]
    n_par = len(params)

    def body(*refs):
        dg1, dba, dbb, dcw, dcb, dps, dg2, dgf, lc = refs[:9]
        wmv = refs[9:9 + 3 * n_par]
        refs = refs[9 + 3 * n_par + len(deps):]
        outs = refs[:4 * n_par]
        loss_ref = refs[4 * n_par]
        snd, got, ssem, rsem = refs[4 * n_par + 1:]
        x, y, c = _mesh_pos()
        me = 4 * x + 2 * y + c
        snd[...] = jnp.zeros_like(snd)
        for row, ref in ((0, dg1), (1, dba), (2, dbb), (3, dps), (4, dg2), (5, dgf), (6, lc)):
            snd[row:row + 1, :] = ref[...]
        snd[7:8, :n_conv] = dcb[...]
        snd[8:11, :n_conv] = dcw[...]
        cps = []
        for r in range(1, N_DEV):
            peer = tuple(1 - p if (r >> b) & 1 else p for p, b in ((x, 2), (y, 1), (c, 0)))
            cps.append(_remote(snd, got.at[me], ssem.at[r - 1], rsem.at[r - 1], peer))
        for cp in cps:
            cp.start()
        got[me] = snd[...]
        for r in range(1, N_DEV):
            peer = tuple(1 - p if (r >> b) & 1 else p for p, b in ((x, 2), (y, 1), (c, 0)))
            _remote(snd, got.at[4 * peer[0] + 2 * peer[1] + peer[2]], ssem.at[r - 1], rsem.at[r - 1], peer).wait_recv()
        for cp in cps:
            cp.wait_send()
        tot = got[0]
        for dev in range(1, N_DEV):
            tot = tot + got[dev]
        loss_ref[...] = jnp.sum(tot[6:7, :], axis=1, keepdims=True)
        k_me = 2 * x + y
        g_cw = jnp.zeros((3, ncw), F32)
        for k in range(N_CHIPS):
            g_cw = g_cw + jnp.where(k_me == k, tot[8:11, k * ncw:(k + 1) * ncw], 0.0)
        grads = [tot[0:1, :], jnp.concatenate([tot[1:2, :], tot[2:3, :]], axis=1), g_cw, tot[7:8, :n_conv],
                 tot[3:4, :], tot[4:5, :], tot[5:6, :]]
        for i, g in enumerate(grads):
            w_ref, m_ref, v_ref = wmv[3 * i:3 * i + 3]
            delta, nm, nv = _adamw_math(w_ref[...], g, m_ref[...], v_ref[...])
            outs[4 * i][...] = g
            outs[4 * i + 1][...] = delta
            outs[4 * i + 2][...] = nm
            outs[4 * i + 3][...] = nv

    args = list(parts)
    out_shape = []
    for w, m, v in params:
        args += [w, m, v]
        out_shape += [jax.ShapeDtypeStruct(w.shape, F32)] * 4
    out_shape.append(jax.ShapeDtypeStruct((1, 1), F32))
    return pl.pallas_call(
        body, name="vector_params_step", in_specs=[VMEM] * len(args) + [ANY] * len(deps),
        out_specs=[VMEM] * len(out_shape), out_shape=out_shape,
        scratch_shapes=[pltpu.VMEM((VEC_ROWS, d), F32), pltpu.VMEM((N_DEV, VEC_ROWS, d), F32),
                        pltpu.SemaphoreType.DMA((N_DEV - 1,)), pltpu.SemaphoreType.DMA((N_DEV - 1,))],
        compiler_params=pltpu.CompilerParams(vmem_limit_bytes=VMEM_LIMIT),
    )(*args, *deps)


def kernel(x, norm1_g, w_in, b_gate, conv_w, conv_b, w_a_out, w_pool, pool_scale, w_o, norm2_g, w_ffn_gate, w_ffn_up, w_ffn_down, final_g, loss_target, m_norm1_g, m_w_in, m_b_gate, m_conv_w, m_conv_b, m_w_a_out, m_w_pool, m_pool_scale, m_w_o, m_norm2_g, m_w_ffn_gate, m_w_ffn_up, m_w_ffn_down, m_final_g, v_norm1_g, v_w_in, v_b_gate, v_conv_w, v_conv_b, v_w_a_out, v_w_pool, v_pool_scale, v_w_o, v_norm2_g, v_w_ffn_gate, v_w_ffn_up, v_w_ffn_down, v_final_g):
    t, d = x.shape[1], x.shape[2]
    n_conv = conv_b.shape[1]
    n_groups, pool_cg, pool_dg = w_pool.shape[1], w_pool.shape[2], N_CHIPS * w_pool.shape[3]
    d_ff = N_CHIPS * w_ffn_gate.shape[2]
    assert n_conv // n_groups == pool_cg and n_conv % (n_groups * MIX_COLS) == 0 and n_groups == len(POOL_WINDOWS)

    big = {"w_in": (w_in, m_w_in, v_w_in), "w_a_out": (w_a_out, m_w_a_out, v_w_a_out), "w_pool": (w_pool, m_w_pool, v_w_pool),
           "w_o": (w_o, m_w_o, v_w_o), "w_ffn_gate": (w_ffn_gate, m_w_ffn_gate, v_w_ffn_gate),
           "w_ffn_up": (w_ffn_up, m_w_ffn_up, v_w_ffn_up), "w_ffn_down": (w_ffn_down, m_w_ffn_down, v_w_ffn_down)}
    colshard = {"w_in": True, "w_a_out": True, "w_pool": True, "w_o": False, "w_ffn_gate": True, "w_ffn_up": True,
                "w_ffn_down": False}
    names = list(big)
    shard2d = {n: big[n][0].reshape(-1, big[n][0].shape[-1]) for n in names}
    ws = [_Weight(n, *shard2d[n].shape, colshard[n]) for n in names]

    xs, tgt = x[0], loss_target[0]
    cw_loc = conv_w[0]
    pos = jnp.stack([lax.axis_index("c"), 2 * lax.axis_index("x") + lax.axis_index("y")]).astype(jnp.int32)
    by_name = {w.name: w for w in ws}
    groups = [[by_name[n] for n in g] for g in (["w_in"], ["w_a_out", "w_pool", "w_o"], ["w_ffn_gate"], ["w_ffn_up"],
                                                 ["w_ffn_down"])]
    first = [sum(len(g) for g in groups[:i]) for i in range(len(groups))]
    rgroups = [groups[0], groups[1], groups[2] + groups[3], groups[4]]

    cw_full = _gather_conv_w(cw_loc)
    cast = lambda w, dep: _cast_place(f"cast_{w.name}", w, pos, shard2d[w.name].reshape(2, w.R, w.nn), deps=[dep])
    chips, ks = _other_chips(lax.axis_index("x"), lax.axis_index("y"))
    kvec = jnp.stack([pos[1], *ks]).astype(jnp.int32)
    full = {}

    def start(name, arrays, copies, after=()):
        ssem, rsem, arrays, token = _split_start(name, arrays, copies.n, copies, after)
        return name, arrays, ssem, rsem, copies, token

    def wait(started, after):
        name, arrays, ssem, rsem, copies, _ = started
        return _split_wait(name + "_wait", arrays, ssem, rsem, copies, after)

    def passed(g, got, after):
        st = start(f"pass_{g}", got, _pass_copies(groups[g]), after)
        got = wait(st, [st[5]])
        full.update({w.name: a.reshape(w.P * 2 * w.R, w.N) for w, a in zip(groups[g], got)})

    near = start("near_0", [cast(w, cw_full) for w in groups[0]], _near_copies(groups[0]))
    rest = [cast(w, near[5]) for grp in groups[1:] for w in grp]
    h1 = _rms_fwd("norm1_fwd", xs, norm1_g, deps=[rest[-1]])
    proj = _proj_piece("proj_own", h1, shard2d["w_in"], None, kvec, 0, 1)
    got = wait(near, [proj])
    far = start("far_0", got, _far_copies(groups[0]))
    sems_b, lands_b, tok_b = _gather_start("gather_start_b", groups[1:2], rest[:3], after=[far[5]])
    st = start("pass_near_0", far[1], _pass_copies(groups[0], (0, 1)), [tok_b])
    got = wait(st, [st[5]])
    proj = _proj_piece("proj_near", h1, got[0].reshape(-1, groups[0][0].N), proj, kvec, 1, 2)
    st = start("pass_far_0", wait((far[0], got) + far[2:], [proj]), _pass_copies(groups[0], (2,)))
    got = wait(st, [st[5]])
    w_in_full = got[0].reshape(-1, groups[0][0].N)
    proj = _proj_piece("proj_far", h1, w_in_full, proj, kvec, 3, 1)
    z, p = _mixer_fwd("mixer_fwd", proj, cw_full, conv_b, n_conv, n_groups)
    got = _gather_wait("gather_wait_1", groups[1], lands_b, *sems_b[0], z)
    near_g = start("near_2", rest[3:4], _near_copies(groups[2]), got)
    passed(1, got, [near_g[5]])
    wp_full = full["w_pool"].reshape(n_groups, pool_cg, pool_dg)
    ya = _mm_nn("conv_out", z, full["w_a_out"], BF16)
    yb = _gmm_nn("pool_out", p, wp_full, BF16)
    merged = _merge_fwd("merge_fwd", proj, b_gate, ya, yb, pool_scale)
    far_g = start("far_2", wait(near_g, [merged]), _far_copies(groups[2]))
    near_u = start("near_3", rest[4:5], _near_copies(groups[3]), [far_g[5]])
    x2 = _mm_nn("mix_out", merged, full["w_o"], F32, add=xs, deps=[near_u[5]])
    h2 = _rms_fwd("norm2_fwd", x2, norm2_g)
    passed(2, wait(far_g, [h2]), [])
    gate = _mm_nn("ffn_gate_a", h2, full["w_ffn_gate"], BF16, part=(0, 2))
    far_u = start("far_3", wait(near_u, [gate]), _far_copies(groups[3]))
    near_d = start("near_4", rest[5:6], _near_copies(groups[4]), [far_u[5]])
    gate = _mm_nn("ffn_gate_b", h2, full["w_ffn_gate"], BF16, part=(1, 2), prev=gate, deps=[near_d[5]])
    passed(3, wait(far_u, [gate]), [])
    up_act = _ffn_up_act("ffn_up_act_a", h2, full["w_ffn_up"], gate, part=(0, 2))
    far_d = start("far_4", wait(near_d, [up_act[0]]), _far_copies(groups[4]))
    up, act = _ffn_up_act("ffn_up_act_b", h2, full["w_ffn_up"], gate, part=(1, 2), prev=up_act, deps=[far_d[5]])
    passed(4, wait(far_d, [act]), [])
    x3 = _mm_nn("ffn_down", act, full["w_ffn_down"], F32, add=x2, tk=d_ff // 4)

    pending = {}

    def pair_start(g, grads):
        grp = rgroups[g]
        gcan = [grads[w.name].reshape(w.P, 2, w.R, w.N) for w in grp]
        slots = [lax.empty((w.P, w.R, w.N), BF16) for w in grp]
        pending[g] = _split_start(f"pair_start_{g}", gcan + slots, len(grp), _pair_copies(len(grp)))
        return pending[g][3]

    def scatter_start(g, after):
        grp = rgroups[g]
        n = len(grp)
        ssem, rsem, arrs, _ = pending[g]
        arrs = _split_wait(f"pair_wait_{g}", arrs, ssem, rsem, _pair_copies(n), after)
        gcan, sib = arrs[:n], arrs[n:]
        pairs = [_pair_sum(f"pair_sum_{w.name}", w, pos, a, s) for w, a, s in zip(grp, gcan, sib)]
        ssem, rsem, pairs, slots, token = _scatter_start(f"scatter_start_{g}", grp, pairs)
        pending[g] = (gcan, sib, pairs, slots, ssem, rsem)
        return token

    def reduce_finish(g, after):
        grp = rgroups[g]
        gcan, sib, pairs, slots, ssem, rsem = pending[g]
        parts = _scatter_wait(f"scatter_wait_{g}", grp, pairs, slots, ssem, rsem, after)
        return [_final_sum(f"final_sum_{w.name}", w, pos, a, s, q) for w, a, s, q in zip(grp, gcan, sib, parts)]

    grads = {}
    dx3, dx3b, d_gf, loss_cols = _final_bwd("final_bwd", x3, final_g.reshape(1, d), tgt)
    dgate, dup = _ffn_bwd("ffn_bwd", dx3b, full["w_ffn_down"], gate, up)
    grads["w_ffn_down"] = _mm_tn("dw_ffn_down", act, dx3b, BF16)
    tok = pair_start(3, grads)
    dh2 = _mm_nt("d_h2", [(dgate, full["w_ffn_gate"]), (dup, full["w_ffn_up"])], F32, tk=d_ff // 4, deps=[tok])
    tok = scatter_start(3, [dh2])
    grads["w_ffn_gate"] = _mm_tn("dw_ffn_gate", h2, dgate, BF16, deps=[tok])
    grads["w_ffn_up"] = _mm_tn("dw_ffn_up", h2, dup, BF16)
    tok = pair_start(2, grads)
    dx2, dx2b, d_g2 = _rms_bwd("norm2_bwd", x2, norm2_g, dh2, dx3, True, deps=[tok])
    dmerged = _mm_nt("d_merged", [(dx2b, full["w_o"])], BF16, tk=d)
    grads["w_o"] = _mm_tn("dw_o", merged, dx2b, BF16)
    tok = scatter_start(2, [grads["w_o"]])
    dya, dyb, dproj, d_bga, d_bgb, d_ps = _merge_bwd("merge_bwd", dmerged, proj, b_gate, ya, yb, pool_scale, deps=[tok])
    dz = _mm_nt("d_z", [(dya, full["w_a_out"])], BF16, tk=d)
    grads["w_a_out"] = _mm_tn("dw_a_out", z, dya, BF16)
    dp = _gmm_nt("d_pool", dyb, wp_full, BF16)
    grads["w_pool"] = _gmm_tn("dw_pool", p, dyb, n_groups, BF16)
    tok = pair_start(1, grads)
    dproj, d_cw, d_cb = _mixer_bwd("mixer_bwd", dz, dp, proj, cw_full, conv_b, dproj, n_conv, n_groups, deps=[tok])
    tok = scatter_start(1, [dproj])
    grads["w_in"] = _mm_tn("dw_in", h1, dproj, BF16, deps=[tok])
    tok = pair_start(0, grads)
    dh1 = _mm_nt("d_h1", [(dproj, w_in_full)], F32, tk=proj.shape[1] // 4, deps=[tok])
    tok = scatter_start(0, [dh1])
    grad_x, d_g1 = _rms_bwd("norm1_bwd", xs, norm1_g, dh1, dx2, False, deps=[tok])

    g_big, d_big, m_big, v_big = {}, {}, {}, {}

    def update(wsub, shared):
        out = []
        for w, g in zip(wsub, shared):
            wt, mt, vt = big[w.name]
            g2 = g.reshape(2 * w.R, w.nn)
            go, dl, nm, nv = _adamw(f"adamw_{w.name}", shard2d[w.name], g2, mt.reshape(g2.shape), vt.reshape(g2.shape))
            g_big[w.name], d_big[w.name], m_big[w.name], v_big[w.name] = (a.reshape(wt.shape) for a in (go, dl, nm, nv))
            out.append(nv)
        return out

    after = [grad_x]
    started = []
    for g in (3, 2, 1):
        halves = reduce_finish(g, after)
        share = _share_copies(len(halves))
        ssem, rsem, halves, tok = _split_start(f"share_start_{g}", halves, len(halves), share)
        started.append((g, ssem, rsem, halves, share))
        after = [tok]
    for g, ssem, rsem, halves, share in started:
        after = update(rgroups[g], _split_wait(f"share_wait_{g}", halves, ssem, rsem, share, after))
    after = update(rgroups[0], _share_halves("share_halves_w_in", rgroups[0], reduce_finish(0, after)))

    vec_names = ["norm1_g", "b_gate", "conv_w", "conv_b", "pool_scale", "norm2_g", "final_g"]
    vec = {"norm1_g": (norm1_g, m_norm1_g, v_norm1_g), "b_gate": (b_gate, m_b_gate, v_b_gate),
           "conv_w": (cw_loc, m_conv_w[0], v_conv_w[0]), "conv_b": (conv_b, m_conv_b, v_conv_b),
           "pool_scale": (pool_scale, m_pool_scale, v_pool_scale), "norm2_g": (norm2_g, m_norm2_g, v_norm2_g),
           "final_g": tuple(a.reshape(1, d) for a in (final_g, m_final_g, v_final_g))}
    vout = _vector_step(d, n_conv, [d_g1, d_bga, d_bgb, d_cw, d_cb, d_ps, d_g2, d_gf, loss_cols],
                        [vec[n] for n in vec_names], deps=after)

    shapes = {"conv_w": conv_w.shape, "final_g": final_g.shape}
    g_vec, d_vec, m_vec, v_vec = ({n: vout[4 * i + q].reshape(shapes.get(n, vec[n][0].shape)) for i, n in enumerate(vec_names)}
                                  for q in range(4))
    loss = vout[-1].reshape(())

    order = ["norm1_g", "w_in", "b_gate", "conv_w", "conv_b", "w_a_out", "w_pool", "pool_scale", "w_o", "norm2_g",
             "w_ffn_gate", "w_ffn_up", "w_ffn_down", "final_g"]
    pick = lambda vecs, bigs: [vecs[n] if n in vecs else bigs[n] for n in order]
    return (loss, grad_x.reshape(x.shape), *pick(g_vec, g_big), *pick(d_vec, d_big), *pick(m_vec, m_big),
            *pick(v_vec, v_big))
```

```python
import functools

import jax
import jax.numpy as jnp
from jax import lax
from jax.experimental import pallas as pl
from jax.experimental.pallas import tpu as pltpu

F32, BF16 = jnp.float32, jnp.bfloat16
MESH = pl.DeviceIdType.MESH
ANY = pl.BlockSpec(memory_space=pl.ANY)
VMEM = pl.BlockSpec(memory_space=pltpu.VMEM)
HBM = pl.BlockSpec(memory_space=pltpu.HBM)
SEM = pl.BlockSpec(memory_space=pltpu.SEMAPHORE)
EFFECT = pltpu.SideEffectType.DATAFLOW_SIDE_EFFECTING

EPS = 1e-6
POOL_WINDOWS = (2, 4, 8, 16)
ADAM_LR, ADAM_B1, ADAM_B2, ADAM_EPS, ADAM_WD, ADAM_STEP = 0.001, 0.9, 0.999, 1e-08, 0.01, 10

V7X_VMEM_BYTES = 64 * 1024 * 1024
VMEM_LIMIT = V7X_VMEM_BYTES * 3 // 4
LANES = 128
N_CHIPS = 4
N_DEV = 8

_DIMS = {
    "nn": (((1,), (0,)), ((), ())),
    "nt": (((1,), (1,)), ((), ())),
    "tn": (((0,), (0,)), ((), ())),
}


def _cp(sem):
    return pltpu.CompilerParams(dimension_semantics=sem, vmem_limit_bytes=VMEM_LIMIT)


def _mesh_pos():
    return lax.axis_index("x"), lax.axis_index("y"), lax.axis_index("c")


def _mm(name, pairs, *, mode, grid, out_shape, o_spec, nk=1, kaxis=None, add=None, deps=(), prev=None):
    npair = len(pairs)
    has_add = add is not None

    def body(*refs):
        ab = refs[: 2 * npair]
        pos = 2 * npair
        add_ref = refs[pos] if has_add else None
        pos += int(has_add) + len(deps) + (prev is not None)
        o_ref = refs[pos]
        acc_ref = refs[pos + 1] if nk > 1 else None
        d = None
        for p in range(npair):
            t = lax.dot_general(ab[2 * p][...], ab[2 * p + 1][...], _DIMS[mode], preferred_element_type=F32)
            d = t if d is None else d + t
        if nk == 1:
            if has_add:
                d = d + add_ref[...].astype(F32)
            o_ref[...] = d.astype(o_ref.dtype)
        else:
            k = pl.program_id(kaxis)

            @pl.when(k == 0)
            def _():
                acc_ref[...] = d

            @pl.when(k > 0)
            def _():
                acc_ref[...] += d

            @pl.when(k == nk - 1)
            def _():
                r = acc_ref[...]
                if has_add:
                    r = r + add_ref[...].astype(F32)
                o_ref[...] = r.astype(o_ref.dtype)

    args, specs = [], []
    for a, a_spec, b, b_spec in pairs:
        args += [a, b]
        specs += [a_spec, b_spec]
    if has_add:
        args.append(add[0])
        specs.append(add[1])
    args += list(deps)
    specs += [ANY] * len(deps)
    aliases = {}
    if prev is not None:
        aliases = {len(args): 0}
        args.append(prev)
        specs.append(ANY)
    scratch = []
    if nk > 1:
        blk = [d for d in o_spec.block_shape if d is not None]
        scratch = [pltpu.VMEM(tuple(blk), F32)]
    sem = tuple("arbitrary" if (nk > 1 and ax == kaxis) else "parallel" for ax in range(len(grid)))
    return pl.pallas_call(
        body, name=name, grid=grid, in_specs=specs, out_specs=o_spec, out_shape=out_shape,
        scratch_shapes=scratch, input_output_aliases=aliases, compiler_params=_cp(sem),
    )(*args)


def _tile_span(n_tiles, part):
    if part is None:
        return 0, n_tiles
    p, of = part
    return p * n_tiles // of, (p + 1) * n_tiles // of


def _tile(n, pref):
    if n <= pref:
        return n
    for t in range(pref, 0, -LANES):
        if t % LANES == 0 and n % t == 0:
            return t
    raise ValueError(f"no tile for {n}")


def _mm_nn(name, a, b, out_dtype, add=None, tk=None, deps=(), part=None, prev=None):
    m, kk = a.shape
    n = b.shape[1]
    tm, tn = _tile(m, 1024), _tile(n, 512)
    out_shape = jax.ShapeDtypeStruct((m, n), out_dtype)
    if tk is None or tk == kk:
        j0, j1 = _tile_span(n // tn, part)
        grid = (m // tm, j1 - j0)
        pairs = [(a, pl.BlockSpec((tm, kk), lambda i, j: (i, 0)), b, pl.BlockSpec((kk, tn), lambda i, j: (0, j0 + j)))]
        o_spec = pl.BlockSpec((tm, tn), lambda i, j: (i, j0 + j))
        add_ = None if add is None else (add, pl.BlockSpec((tm, tn), lambda i, j: (i, j0 + j)))
        return _mm(name, pairs, mode="nn", grid=grid, out_shape=out_shape, o_spec=o_spec, add=add_, deps=deps, prev=prev)
    tn = _tile(n, 1024)
    nk = kk // tk
    grid = (m // tm, n // tn, nk)
    pairs = [(a, pl.BlockSpec((tm, tk), lambda i, j, k: (i, k)), b, pl.BlockSpec((tk, tn), lambda i, j, k: (k, j)))]
    o_spec = pl.BlockSpec((tm, tn), lambda i, j, k: (i, j))
    add_ = None if add is None else (add, pl.BlockSpec((tm, tn), lambda i, j, k: (i, j)))
    return _mm(name, pairs, mode="nn", grid=grid, out_shape=out_shape, o_spec=o_spec, nk=nk, kaxis=2, add=add_, deps=deps)


def _mm_nt(name, abs_, out_dtype, tk, deps=()):
    m, kk = abs_[0][0].shape
    n = abs_[0][1].shape[0]
    tm = _tile(m, 1024)
    nk = kk // tk
    tn = _tile(n, 512 if nk == 1 else 1024)
    out_shape = jax.ShapeDtypeStruct((m, n), out_dtype)
    if nk == 1:
        grid = (m // tm, n // tn)
        pairs = [(a, pl.BlockSpec((tm, kk), lambda i, j: (i, 0)), b, pl.BlockSpec((tn, kk), lambda i, j: (j, 0)))
                 for a, b in abs_]
        o_spec = pl.BlockSpec((tm, tn), lambda i, j: (i, j))
        return _mm(name, pairs, mode="nt", grid=grid, out_shape=out_shape, o_spec=o_spec, deps=deps)
    grid = (m // tm, n // tn, nk)
    pairs = [(a, pl.BlockSpec((tm, tk), lambda i, j, k: (i, k)), b, pl.BlockSpec((tn, tk), lambda i, j, k: (j, k)))
             for a, b in abs_]
    o_spec = pl.BlockSpec((tm, tn), lambda i, j, k: (i, j))
    return _mm(name, pairs, mode="nt", grid=grid, out_shape=out_shape, o_spec=o_spec, nk=nk, kaxis=2, deps=deps)


def _mm_tn(name, a, b, out_dtype, deps=()):
    t, m = a.shape
    n = b.shape[1]
    tm, tn = _tile(m, 512), _tile(n, 2048)
    if n > m:
        grid = (n // tn, m // tm)
        a_map, b_map, o_map = (lambda j, i: (0, i)), (lambda j, i: (0, j)), (lambda j, i: (i, j))
    else:
        grid = (m // tm, n // tn)
        a_map, b_map, o_map = (lambda i, j: (0, i)), (lambda i, j: (0, j)), (lambda i, j: (i, j))
    pairs = [(a, pl.BlockSpec((t, tm), a_map), b, pl.BlockSpec((t, tn), b_map))]
    o_spec = pl.BlockSpec((tm, tn), o_map)
    return _mm(name, pairs, mode="tn", grid=grid, out_shape=jax.ShapeDtypeStruct((m, n), out_dtype), o_spec=o_spec,
               deps=deps)


def _proj_piece(name, h, w, prev, kvec, base, count, deps=()):
    t, kk = h.shape
    own = w.dtype == F32
    nn = w.shape[1] if own else w.shape[1] // N_CHIPS
    tm, tn = _tile(t, 1024), _tile(nn, 512)
    nb = nn // tn

    def body(kv_ref, h_ref, w_ref, *rest):
        rest[-1][...] = lax.dot_general(h_ref[...], w_ref[...].astype(BF16), _DIMS["nn"],
                                        preferred_element_type=F32).astype(BF16)

    cols = lambda s, i, j, kv: (0, j) if own else (0, kv[base + s] * nb + j)
    extra = ([] if prev is None else [prev]) + list(deps)
    grid_spec = pltpu.PrefetchScalarGridSpec(
        num_scalar_prefetch=1, grid=(count, t // tm, nb),
        in_specs=[pl.BlockSpec((tm, kk), lambda s, i, j, kv: (i, 0)), pl.BlockSpec((kk, tn), cols)] + [ANY] * len(extra),
        out_specs=pl.BlockSpec((tm, tn), lambda s, i, j, kv: (i, kv[base + s] * nb + j)))
    return pl.pallas_call(body, name=name, grid_spec=grid_spec, out_shape=jax.ShapeDtypeStruct((t, N_CHIPS * nn), BF16),
                          input_output_aliases={} if prev is None else {3: 0},
                          compiler_params=_cp(("parallel",) * 3))(kvec, h, w, *extra)


def _gmm_nn(name, p, w, out_dtype):
    t = p.shape[0]
    g, cg, dg = w.shape
    tm = _tile(t, 1024)
    pairs = [(p, pl.BlockSpec((tm, cg), lambda i, j: (i, j)), w, pl.BlockSpec((None, cg, dg), lambda i, j: (j, 0, 0)))]
    o_spec = pl.BlockSpec((tm, dg), lambda i, j: (i, j))
    return _mm(name, pairs, mode="nn", grid=(t // tm, g), out_shape=jax.ShapeDtypeStruct((t, g * dg), out_dtype),
               o_spec=o_spec)


def _gmm_nt(name, dy, w, out_dtype):
    t = dy.shape[0]
    g, cg, dg = w.shape
    tm = _tile(t, 1024)
    pairs = [(dy, pl.BlockSpec((tm, dg), lambda i, j: (i, j)), w, pl.BlockSpec((None, cg, dg), lambda i, j: (j, 0, 0)))]
    o_spec = pl.BlockSpec((tm, cg), lambda i, j: (i, j))
    return _mm(name, pairs, mode="nt", grid=(t // tm, g), out_shape=jax.ShapeDtypeStruct((t, g * cg), out_dtype),
               o_spec=o_spec)


def _gmm_tn(name, p, dy, g, out_dtype):
    t = p.shape[0]
    cg, dg = p.shape[1] // g, dy.shape[1] // g
    pairs = [(p, pl.BlockSpec((t, cg), lambda j: (0, j)), dy, pl.BlockSpec((t, dg), lambda j: (0, j)))]
    o_spec = pl.BlockSpec((None, cg, dg), lambda j: (j, 0, 0))
    return _mm(name, pairs, mode="tn", grid=(g,), out_shape=jax.ShapeDtypeStruct((g, cg, dg), out_dtype), o_spec=o_spec)


ROW_TILE = 256


def _rows(t):
    return _tile8(t, ROW_TILE)


def _tile8(n, pref):
    if n <= pref:
        return n
    for t in range(pref, 0, -8):
        if n % t == 0:
            return t
    raise ValueError(f"no row tile for {n}")


def _cast_place(name, w, pos, shard, deps=()):
    tr = _tile8(w.R, 512)
    if w.colshard:
        o_map = lambda h, i, pos: (0, h, i, pos[1])
    else:
        o_map = lambda h, i, pos: (pos[1], h, i, 0)

    def body(pos_ref, w_ref, *rest):
        rest[-1][...] = w_ref[...].astype(BF16)

    grid_spec = pltpu.PrefetchScalarGridSpec(
        num_scalar_prefetch=1, grid=(2, w.R // tr),
        in_specs=[pl.BlockSpec((None, tr, w.nn), lambda h, i, pos: (h, i, 0))] + [ANY] * len(deps),
        out_specs=pl.BlockSpec((None, None, tr, w.nn), o_map))
    return pl.pallas_call(body, name=name, grid_spec=grid_spec, out_shape=jax.ShapeDtypeStruct((w.P, 2, w.R, w.N), BF16),
                          compiler_params=_cp(("parallel", "parallel")))(pos, shard, *deps)


def _rms_fwd(name, x, g, deps=()):
    t, d = x.shape
    tm = _rows(t)

    def body(x_ref, g_ref, *rest):
        xf = x_ref[...]
        r = lax.rsqrt(jnp.mean(xf * xf, axis=-1, keepdims=True) + EPS)
        rest[-1][...] = (xf * r * g_ref[...]).astype(BF16)

    return pl.pallas_call(
        body, name=name, grid=(t // tm,),
        in_specs=[pl.BlockSpec((tm, d), lambda i: (i, 0)), pl.BlockSpec((1, d), lambda i: (0, 0))] + [ANY] * len(deps),
        out_specs=pl.BlockSpec((tm, d), lambda i: (i, 0)), out_shape=jax.ShapeDtypeStruct((t, d), BF16),
        compiler_params=_cp(("parallel",)),
    )(x, g, *deps)


def _rms_bwd(name, x, g, dh, dres, want_bf16, deps=()):
    t, d = x.shape
    tm = _rows(t)

    def body(x_ref, g_ref, dh_ref, dres_ref, *rest):
        rest = rest[len(deps):]
        dx_ref, rest = rest[0], rest[1:]
        dg_ref = rest[-1]
        xf = x_ref[...]
        r = lax.rsqrt(jnp.mean(xf * xf, axis=-1, keepdims=True) + EPS)
        xh = xf * r
        dhf = dh_ref[...]
        dxh = dhf * g_ref[...]
        m = jnp.mean(dxh * xh, axis=-1, keepdims=True)
        dx = dres_ref[...] + r * (dxh - xh * m)
        dx_ref[...] = dx
        if want_bf16:
            rest[0][...] = dx.astype(BF16)

        @pl.when(pl.program_id(0) == 0)
        def _():
            dg_ref[...] = jnp.zeros_like(dg_ref)

        dg_ref[...] += jnp.sum(dhf * xh, axis=0, keepdims=True)

    row = pl.BlockSpec((tm, d), lambda i: (i, 0))
    vec = pl.BlockSpec((1, d), lambda i: (0, 0))
    out_specs = [row] + ([row] if want_bf16 else []) + [vec]
    out_shape = ([jax.ShapeDtypeStruct((t, d), F32)] + ([jax.ShapeDtypeStruct((t, d), BF16)] if want_bf16 else [])
                 + [jax.ShapeDtypeStruct((1, d), F32)])
    return pl.pallas_call(body, name=name, grid=(t // tm,), in_specs=[row, vec, row, row] + [ANY] * len(deps),
                          out_specs=out_specs, out_shape=out_shape, compiler_params=_cp(("arbitrary",)))(x, g, dh, dres, *deps)


def _final_bwd(name, x3, gf, tgt):
    t, d = x3.shape
    tm = _rows(t)

    def body(x_ref, g_ref, t_ref, dx_ref, dxb_ref, dg_ref, lc_ref):
        xf = x_ref[...]
        g = g_ref[...]
        r = lax.rsqrt(jnp.mean(xf * xf, axis=-1, keepdims=True) + EPS)
        xh = xf * r
        diff = xh * g - t_ref[...]
        dy = diff * (1.0 / d)
        dxh = dy * g
        m = jnp.mean(dxh * xh, axis=-1, keepdims=True)
        dx = r * (dxh - xh * m)
        dx_ref[...] = dx
        dxb_ref[...] = dx.astype(BF16)

        @pl.when(pl.program_id(0) == 0)
        def _():
            dg_ref[...] = jnp.zeros_like(dg_ref)
            lc_ref[...] = jnp.zeros_like(lc_ref)

        dg_ref[...] += jnp.sum(dy * xh, axis=0, keepdims=True)
        lc_ref[...] += jnp.sum(diff * diff, axis=0, keepdims=True) * (0.5 / d)

    row = pl.BlockSpec((tm, d), lambda i: (i, 0))
    vec = pl.BlockSpec((1, d), lambda i: (0, 0))
    return pl.pallas_call(
        body, name=name, grid=(t // tm,), in_specs=[row, vec, row], out_specs=[row, row, vec, vec],
        out_shape=[jax.ShapeDtypeStruct((t, d), F32), jax.ShapeDtypeStruct((t, d), BF16),
                   jax.ShapeDtypeStruct((1, d), F32), jax.ShapeDtypeStruct((1, d), F32)],
        compiler_params=_cp(("arbitrary",)),
    )(x3, gf, tgt)


def _shift_down(v, k, t_idx):
    return jnp.where(t_idx >= k, pltpu.roll(v, k, 0), 0.0)


def _shift_up(v, k, t_idx):
    n = v.shape[0]
    return jnp.where(t_idx < n - k, pltpu.roll(v, n - k, 0), 0.0)


def _window_sums(v, shift, t_idx, grp):
    s = v + shift(v, 1, t_idx)
    out = s
    for lvl in range(1, len(POOL_WINDOWS)):
        s = s + shift(s, 1 << lvl, t_idx)
        out = jnp.where(grp >= lvl, s, out)
    return out


def _window_count(t_idx, grp):
    return jnp.minimum(t_idx + 1, jnp.left_shift(2, grp)).astype(F32)


MIX_COLS = 128


def _mixer_fwd(name, proj, cw, cb, n_conv, n_groups, deps=()):
    t = proj.shape[0]
    nb = n_conv // MIX_COLS
    per_group = n_conv // n_groups // MIX_COLS

    def body(ba_ref, ca_ref, va_ref, vb_ref, cw_ref, cb_ref, *rest):
        z_ref, p_ref = rest[len(deps):]
        t_idx = lax.broadcasted_iota(jnp.int32, (t, MIX_COLS), 0)
        q = ca_ref[...].astype(F32) * va_ref[...].astype(F32)
        w = cw_ref[...]
        u = cb_ref[...] + w[0:1] * _shift_down(q, 2, t_idx) + w[1:2] * _shift_down(q, 1, t_idx) + w[2:3] * q
        z_ref[...] = (ba_ref[...].astype(F32) * u).astype(BF16)
        grp = pl.program_id(0) // per_group
        v = vb_ref[...].astype(F32)
        p_ref[...] = (_window_sums(v, _shift_down, t_idx, grp) / _window_count(t_idx, grp) - v).astype(BF16)

    col = lambda s: pl.BlockSpec((t, MIX_COLS), lambda j: (0, s * nb + j))
    return pl.pallas_call(
        body, name=name, grid=(nb,),
        in_specs=[col(0), col(1), col(2), col(3), pl.BlockSpec((3, MIX_COLS), lambda j: (0, j)),
                  pl.BlockSpec((1, MIX_COLS), lambda j: (0, j))] + [ANY] * len(deps),
        out_specs=[col(0), col(0)],
        out_shape=[jax.ShapeDtypeStruct((t, n_conv), BF16), jax.ShapeDtypeStruct((t, n_conv), BF16)],
        compiler_params=_cp(("parallel",)),
    )(proj, proj, proj, proj, cw, cb, *deps)


def _mixer_bwd(name, dz, dp, proj, cw, cb, dproj, n_conv, n_groups, deps=()):
    t = proj.shape[0]
    nb = n_conv // MIX_COLS
    per_group = n_conv // n_groups // MIX_COLS

    def body(dz_ref, dp_ref, ba_ref, ca_ref, va_ref, cw_ref, cb_ref, _, *rest):
        o_ref, dcw_ref, dcb_ref, scr = rest[len(deps):]
        s = pl.program_id(1)

        @pl.when(s == 0)
        def _():
            t_idx = lax.broadcasted_iota(jnp.int32, (t, MIX_COLS), 0)
            ca, va = ca_ref[...].astype(F32), va_ref[...].astype(F32)
            q = ca * va
            q1, q2 = _shift_down(q, 1, t_idx), _shift_down(q, 2, t_idx)
            w = cw_ref[...]
            u = cb_ref[...] + w[0:1] * q2 + w[1:2] * q1 + w[2:3] * q
            dzf = dz_ref[...].astype(F32)
            du = dzf * ba_ref[...].astype(F32)
            scr[0] = (dzf * u).astype(BF16)
            dq = w[2:3] * du + w[1:2] * _shift_up(du, 1, t_idx) + w[0:1] * _shift_up(du, 2, t_idx)
            scr[1] = (dq * va).astype(BF16)
            scr[2] = (dq * ca).astype(BF16)
            dcb_ref[...] = jnp.sum(du, axis=0, keepdims=True)
            dcw_ref[0:1, :] = jnp.sum(du * q2, axis=0, keepdims=True)
            dcw_ref[1:2, :] = jnp.sum(du * q1, axis=0, keepdims=True)
            dcw_ref[2:3, :] = jnp.sum(du * q, axis=0, keepdims=True)
            grp = pl.program_id(0) // per_group
            dpf = dp_ref[...].astype(F32)
            e = dpf / _window_count(t_idx, grp)
            scr[3] = (_window_sums(e, _shift_up, t_idx, grp) - dpf).astype(BF16)

        o_ref[...] = scr[s]

    col = lambda c: pl.BlockSpec((t, MIX_COLS), lambda j, s: (0, c * nb + j))
    own = pl.BlockSpec((t, MIX_COLS), lambda j, s: (0, j))
    return pl.pallas_call(
        body, name=name, grid=(nb, 4),
        in_specs=[own, own, col(0), col(1), col(2), pl.BlockSpec((3, MIX_COLS), lambda j, s: (0, j)),
                  pl.BlockSpec((1, MIX_COLS), lambda j, s: (0, j)), ANY] + [ANY] * len(deps),
        out_specs=[pl.BlockSpec((t, MIX_COLS), lambda j, s: (0, s * nb + j)),
                   pl.BlockSpec((3, MIX_COLS), lambda j, s: (0, j)), pl.BlockSpec((1, MIX_COLS), lambda j, s: (0, j))],
        out_shape=[jax.ShapeDtypeStruct(dproj.shape, BF16), jax.ShapeDtypeStruct((3, n_conv), F32),
                   jax.ShapeDtypeStruct((1, n_conv), F32)],
        scratch_shapes=[pltpu.VMEM((4, t, MIX_COLS), BF16)],
        input_output_aliases={7: 0},
        compiler_params=_cp(("arbitrary", "arbitrary")),
    )(dz, dp, proj, proj, proj, cw, cb, dproj, *deps)


def _merge_fwd(name, proj, bg, ya, yb, ps):
    t, d = ya.shape
    tm = _rows(t)

    def body(gab_ref, bg_ref, ya_ref, yb_ref, ps_ref, o_ref):
        gab = gab_ref[...].astype(F32) + bg_ref[...]
        sa, sb = jax.nn.sigmoid(gab[:, :d]), jax.nn.sigmoid(gab[:, d:])
        o_ref[...] = (sa * ya_ref[...].astype(F32) + sb * (yb_ref[...].astype(F32) * ps_ref[...])).astype(BF16)

    row = pl.BlockSpec((tm, d), lambda i: (i, 0))
    return pl.pallas_call(
        body, name=name, grid=(t // tm,),
        in_specs=[pl.BlockSpec((tm, 2 * d), lambda i: (i, 1)), pl.BlockSpec((1, 2 * d), lambda i: (0, 0)), row, row,
                  pl.BlockSpec((1, d), lambda i: (0, 0))],
        out_specs=row, out_shape=jax.ShapeDtypeStruct((t, d), BF16), compiler_params=_cp(("parallel",)),
    )(proj, bg, ya, yb, ps)


def _merge_bwd(name, dm, proj, bg, ya, yb, ps, deps=()):
    t, d = ya.shape
    tm = _rows(t)

    def body(dm_ref, gab_ref, bg_ref, ya_ref, yb_ref, ps_ref, *rest):
        dya_ref, dyb_ref, dg_ref, dba_ref, dbb_ref, dps_ref = rest[len(deps):]
        gab = gab_ref[...].astype(F32) + bg_ref[...]
        sa, sb = jax.nn.sigmoid(gab[:, :d]), jax.nn.sigmoid(gab[:, d:])
        dmf = dm_ref[...].astype(F32)
        ybf, ps_ = yb_ref[...].astype(F32), ps_ref[...]
        dya_ref[...] = (dmf * sa).astype(BF16)
        dyb = dmf * sb
        dyb_ref[...] = (dyb * ps_).astype(BF16)
        dga = dmf * ya_ref[...].astype(F32) * sa * (1.0 - sa)
        dgb = dmf * (ybf * ps_) * sb * (1.0 - sb)
        dg_ref[:, :d] = dga.astype(BF16)
        dg_ref[:, d:] = dgb.astype(BF16)

        @pl.when(pl.program_id(0) == 0)
        def _():
            dba_ref[...] = jnp.zeros_like(dba_ref)
            dbb_ref[...] = jnp.zeros_like(dbb_ref)
            dps_ref[...] = jnp.zeros_like(dps_ref)

        dba_ref[...] += jnp.sum(dga, axis=0, keepdims=True)
        dbb_ref[...] += jnp.sum(dgb, axis=0, keepdims=True)
        dps_ref[...] += jnp.sum(dyb * ybf, axis=0, keepdims=True)

    row = pl.BlockSpec((tm, d), lambda i: (i, 0))
    vec = pl.BlockSpec((1, d), lambda i: (0, 0))
    gates = pl.BlockSpec((tm, 2 * d), lambda i: (i, 1))
    return pl.pallas_call(
        body, name=name, grid=(t // tm,),
        in_specs=[row, gates, pl.BlockSpec((1, 2 * d), lambda i: (0, 0)), row, row, vec] + [ANY] * len(deps),
        out_specs=[row, row, gates, vec, vec, vec],
        out_shape=[jax.ShapeDtypeStruct((t, d), BF16), jax.ShapeDtypeStruct((t, d), BF16),
                   jax.ShapeDtypeStruct(proj.shape, BF16), jax.ShapeDtypeStruct((1, d), F32),
                   jax.ShapeDtypeStruct((1, d), F32), jax.ShapeDtypeStruct((1, d), F32)],
        compiler_params=_cp(("arbitrary",)),
    )(dm, proj, bg, ya, yb, ps, *deps)


def _ffn_up_act(name, h, w_up, gate, part=None, prev=None, deps=()):
    t, d = h.shape
    f = w_up.shape[1]
    tm, tf = _tile(t, 1024), _tile(f, 512)
    j0, j1 = _tile_span(f // tf, part)
    n_prev = 0 if prev is None else 2
    extra = ([] if prev is None else list(prev)) + list(deps)

    def body(h_ref, w_ref, g_ref, *rest):
        u_ref, a_ref = rest[len(extra):]
        u = lax.dot_general(h_ref[...], w_ref[...], _DIMS["nn"], preferred_element_type=F32)
        g = g_ref[...].astype(F32)
        u_ref[...] = u.astype(BF16)
        a_ref[...] = (g * jax.nn.sigmoid(g) * u).astype(BF16)

    blk = pl.BlockSpec((tm, tf), lambda i, j: (i, j0 + j))
    shp = jax.ShapeDtypeStruct((t, f), BF16)
    return pl.pallas_call(
        body, name=name, grid=(t // tm, j1 - j0),
        in_specs=[pl.BlockSpec((tm, d), lambda i, j: (i, 0)), pl.BlockSpec((d, tf), lambda i, j: (0, j0 + j)), blk]
        + [ANY] * len(extra),
        out_specs=[blk, blk], out_shape=[shp, shp], input_output_aliases={3 + i: i for i in range(n_prev)},
        compiler_params=_cp(("parallel", "parallel")))(h, w_up, gate, *extra)


def _ffn_bwd(name, dy, w_down, gate, up):
    t, d = dy.shape
    f = w_down.shape[0]
    tm, tf = _tile(t, 1024), _tile(f, 512)

    def body(dy_ref, w_ref, g_ref, u_ref, dg_ref, du_ref):
        da = lax.dot_general(dy_ref[...], w_ref[...], _DIMS["nt"], preferred_element_type=F32)
        g = g_ref[...].astype(F32)
        s = jax.nn.sigmoid(g)
        du_ref[...] = (da * (g * s)).astype(BF16)
        dg_ref[...] = (da * u_ref[...].astype(F32) * (s * (1.0 + g * (1.0 - s)))).astype(BF16)

    blk = pl.BlockSpec((tm, tf), lambda i, j: (i, j))
    shp = jax.ShapeDtypeStruct((t, f), BF16)
    return pl.pallas_call(
        body, name=name, grid=(t // tm, f // tf),
        in_specs=[pl.BlockSpec((tm, d), lambda i, j: (i, 0)), pl.BlockSpec((tf, d), lambda i, j: (j, 0)), blk, blk],
        out_specs=[blk, blk], out_shape=[shp, shp], compiler_params=_cp(("parallel", "parallel")))(dy, w_down, gate, up)


def _adamw_math(w, g, m, v):
    m = ADAM_B1 * m + (1.0 - ADAM_B1) * g
    v = ADAM_B2 * v + (1.0 - ADAM_B2) * (g * g)
    m_hat = m / (1.0 - ADAM_B1 ** ADAM_STEP)
    v_hat = v / (1.0 - ADAM_B2 ** ADAM_STEP)
    delta = -ADAM_LR * (m_hat / (jnp.sqrt(v_hat) + ADAM_EPS) + ADAM_WD * w)
    return delta, m, v


def _adamw(name, w, g, m, v):
    r, c = w.shape
    tr = _tile8(r, 512 if c <= 1024 else 256)

    def body(w_ref, g_ref, m_ref, v_ref, go_ref, d_ref, nm_ref, nv_ref):
        g = g_ref[...]
        go_ref[...] = g
        d_ref[...], nm_ref[...], nv_ref[...] = _adamw_math(w_ref[...], g, m_ref[...], v_ref[...])

    blk = pl.BlockSpec((tr, c), lambda i: (i, 0))
    shp = jax.ShapeDtypeStruct((r, c), F32)
    return pl.pallas_call(body, name=name, grid=(r // tr,), in_specs=[blk] * 4, out_specs=[blk] * 4,
                          out_shape=[shp] * 4, compiler_params=_cp(("parallel",)))(w, g, m, v)


class _Weight:
    def __init__(self, name, rows, cols, colshard):
        self.name, self.colshard = name, colshard
        self.R, self.nn = rows // 2, cols
        self.P = 1 if colshard else N_CHIPS
        self.N = N_CHIPS * cols if colshard else cols

    def cols(self, k):
        return pl.ds(pl.multiple_of(k * self.nn, LANES), self.nn)

    def shard(self, ref, k):
        return ref.at[0, :, :, self.cols(k)] if self.colshard else ref.at[k]

    def half(self, ref, k, h):
        return ref.at[0, h, :, self.cols(k)] if self.colshard else ref.at[k, h]

    def quarter(self, ref, k, h, q):
        return self.half(ref, k, h).at[pl.ds(q * (self.R // 2), self.R // 2), :]

    def part(self, ref, k):
        return ref.at[0, :, self.cols(k)] if self.colshard else ref.at[k]


def _remote(src, dst, ssem, rsem, dev):
    return pltpu.make_async_remote_copy(src_ref=src, dst_ref=dst, send_sem=ssem, recv_sem=rsem, device_id=dev,
                                        device_id_type=MESH)


def _other_chips(x, y):
    chips = [(1 - x, y), (x, 1 - y), (1 - x, 1 - y)]
    return chips, [2 * cx + cy for cx, cy in chips]


def _hbm(a):
    return pltpu.with_memory_space_constraint(a, pltpu.HBM)


def _gather_start(name, groups, lands, after=()):
    flat = [w for grp in groups for w in grp]
    nw, ng = len(flat), len(groups)

    def body(*refs):
        land = refs[:nw]
        sems = refs[nw + len(after):nw + len(after) + 2 * ng]
        token = refs[2 * nw + len(after) + 2 * ng]
        x, y, c = _mesh_pos()
        k_me = 2 * x + y
        chips, _ = _other_chips(x, y)
        i = 0
        for g, grp in enumerate(groups):
            for wi, w in enumerate(grp):
                mine = w.half(land[i], k_me, c)
                for j, chip in enumerate(chips):
                    _remote(mine, mine, sems[2 * g].at[3 * wi + j], sems[2 * g + 1].at[3 * wi + j], (*chip, c)).start()
                i += 1
        token[...] = jnp.zeros_like(token)

    sem_shapes = []
    for grp in groups:
        sem_shapes += [pltpu.SemaphoreType.DMA((3 * len(grp),))] * 2
    out = pl.pallas_call(
        body, name=name, in_specs=[HBM] * nw + [ANY] * len(after),
        out_specs=[SEM] * (2 * ng) + [HBM] * nw + [VMEM],
        out_shape=sem_shapes + [pltpu.HBM(a.shape, a.dtype) for a in lands] + [jax.ShapeDtypeStruct((8, LANES), F32)],
        input_output_aliases={i: 2 * ng + i for i in range(nw)},
        compiler_params=pltpu.CompilerParams(has_side_effects=EFFECT),
    )(*[_hbm(a) for a in lands], *after)
    sems = [(out[2 * g], out[2 * g + 1]) for g in range(ng)]
    return sems, list(out[2 * ng:2 * ng + nw]), out[-1]


def _gather_wait(name, grp, lands, ssem, rsem, after):
    n = len(grp)

    def body(*refs):
        land, ssem_ref, rsem_ref = refs[:n], refs[n], refs[n + 1]
        x, y, c = _mesh_pos()
        k_me = 2 * x + y
        chips, ks = _other_chips(x, y)
        for wi, w in enumerate(grp):
            for j, chip in enumerate(chips):
                cp = _remote(w.half(land[wi], k_me, c), w.half(land[wi], ks[j], c), ssem_ref.at[3 * wi + j],
                             rsem_ref.at[3 * wi + j], (*chip, c))
                cp.wait_send()
                cp.wait_recv()

    return pl.pallas_call(
        body, name=name, in_specs=[HBM] * n + [SEM, SEM, ANY], out_specs=[HBM] * n,
        out_shape=[pltpu.HBM(a.shape, a.dtype) for a in lands], input_output_aliases={i: i for i in range(n)},
        compiler_params=pltpu.CompilerParams(has_side_effects=EFFECT),
    )(*lands, ssem, rsem, after)


def _split_start(name, arrays, n, copies, after=()):
    na = len(arrays)

    def body(*refs):
        ssem, rsem, token = refs[na + len(after):][0], refs[na + len(after):][1], refs[2 * na + len(after) + 2]
        for i, (src, dst, dev, _) in enumerate(copies(refs[:na], *_mesh_pos())):
            _remote(src, dst, ssem.at[i], rsem.at[i], dev).start()
        token[...] = jnp.zeros_like(token)

    out = pl.pallas_call(
        body, name=name, in_specs=[HBM] * na + [ANY] * len(after), out_specs=[SEM, SEM] + [HBM] * na + [VMEM],
        out_shape=[pltpu.SemaphoreType.DMA((n,))] * 2 + [pltpu.HBM(a.shape, a.dtype) for a in arrays]
        + [jax.ShapeDtypeStruct((8, LANES), F32)],
        input_output_aliases={i: 2 + i for i in range(na)},
        compiler_params=pltpu.CompilerParams(has_side_effects=EFFECT),
    )(*[_hbm(a) for a in arrays], *after)
    return out[0], out[1], list(out[2:2 + na]), out[-1]


def _split_wait(name, arrays, ssem, rsem, copies, after):
    na = len(arrays)

    def body(*refs):
        for i, (src, _, dev, dst) in enumerate(copies(refs[:na], *_mesh_pos())):
            cp = _remote(src, dst, refs[na].at[i], refs[na + 1].at[i], dev)
            cp.wait_send()
            cp.wait_recv()

    return list(pl.pallas_call(
        body, name=name, in_specs=[HBM] * na + [SEM, SEM] + [ANY] * len(after), out_specs=[HBM] * na,
        out_shape=[pltpu.HBM(a.shape, a.dtype) for a in arrays], input_output_aliases={i: i for i in range(na)},
        compiler_params=pltpu.CompilerParams(has_side_effects=EFFECT),
    )(*arrays, ssem, rsem, *after))


def _pass_copies(grp, rels=(0, 1, 2)):
    def copies(land, x, y, c):
        _, ks = _other_chips(x, y)
        return [(w.half(land[wi], ks[j], c), w.half(land[wi], ks[j], c), (x, y, 1 - c), w.half(land[wi], ks[j], 1 - c))
                for wi, w in enumerate(grp) for j in rels]
    copies.n = len(grp) * len(rels)
    return copies


def _near_copies(grp):
    def copies(land, x, y, c):
        chips, ks = _other_chips(x, y)
        out = []
        for wi, w in enumerate(grp):
            mine = w.half(land[wi], 2 * x + y, c)
            out += [(mine, mine, (*chips[j], c), w.half(land[wi], ks[j], c)) for j in (0, 1)]
        return out
    copies.n = 2 * len(grp)
    return copies


def _far_copies(grp):
    def copies(land, x, y, c):
        chips, ks = _other_chips(x, y)
        out = []
        for wi, w in enumerate(grp):
            for j in (0, 1):
                q = w.quarter(land[wi], ks[j], c, j)
                out.append((q, q, (*chips[1 - j], c), w.quarter(land[wi], ks[2], c, j)))
        return out
    copies.n = 2 * len(grp)
    return copies


def _pair_copies(n):
    def copies(refs, x, y, c):
        return [(refs[i].at[:, 1 - c], refs[n + i], (x, y, 1 - c), refs[n + i]) for i in range(n)]
    return copies


def _share_copies(n):
    def copies(refs, x, y, c):
        return [(refs[i].at[c], refs[i].at[c], (x, y, 1 - c), refs[i].at[1 - c]) for i in range(n)]
    return copies


def _gather_conv_w(cw):
    ncw = cw.shape[1]

    def body(cw_ref, out_ref, ssem, rsem):
        x, y, c = _mesh_pos()
        k_me = 2 * x + y
        chips, ks = _other_chips(x, y)
        cols = lambda k: out_ref.at[:, pl.ds(pl.multiple_of(k * ncw, LANES), ncw)]
        cps = [_remote(cw_ref, cols(k_me), ssem.at[j], rsem.at[j], (*chip, c)) for j, chip in enumerate(chips)]
        for cp in cps:
            cp.start()
        for k in range(N_CHIPS):
            @pl.when(k_me == k)
            def _():
                out_ref[:, k * ncw:(k + 1) * ncw] = cw_ref[...]
        for j in range(3):
            _remote(cw_ref, cols(ks[j]), ssem.at[j], rsem.at[j], (*chips[j], c)).wait_recv()
        for cp in cps:
            cp.wait_send()

    return pl.pallas_call(
        body, name="gather_conv_w", in_specs=[VMEM], out_specs=VMEM,
        out_shape=jax.ShapeDtypeStruct((3, N_CHIPS * ncw), F32),
        scratch_shapes=[pltpu.SemaphoreType.DMA((3,)), pltpu.SemaphoreType.DMA((3,))],
    )(cw)


def _grad_tiles(w, n):
    return _tile8(w.R, 512) if w.R <= 512 else w.R // 2, _tile(n, 2048)


def _pair_sum(name, w, pos, grad, got):
    tr, tn = _grad_tiles(w, w.N)

    def body(pos_ref, g_ref, r_ref, o_ref):
        o_ref[...] = (g_ref[...].astype(F32) + r_ref[...].astype(F32)).astype(BF16)

    blk = pl.BlockSpec((None, tr, tn), lambda p, i, j, pos: (p, i, j))
    grid_spec = pltpu.PrefetchScalarGridSpec(
        num_scalar_prefetch=1, grid=(w.P, w.R // tr, w.N // tn),
        in_specs=[pl.BlockSpec((None, None, tr, tn), lambda p, i, j, pos: (p, pos[0], i, j)), blk], out_specs=blk)
    return pl.pallas_call(body, name=name, grid_spec=grid_spec, out_shape=jax.ShapeDtypeStruct((w.P, w.R, w.N), BF16),
                          compiler_params=_cp(("parallel",) * 3))(pos, grad, got)


def _scatter_start(name, ws, pairs):
    nw = len(ws)

    def body(*refs):
        pr, land = refs[:nw], refs[nw:2 * nw]
        ssem, rsem = refs[2 * nw], refs[2 * nw + 1]
        token = refs[4 * nw + 2]
        x, y, c = _mesh_pos()
        chips, ks = _other_chips(x, y)
        for i, w in enumerate(ws):
            for j, chip in enumerate(chips):
                _remote(w.part(pr[i], ks[j]), land[i].at[j], ssem.at[3 * i + j], rsem.at[3 * i + j], (*chip, c)).start()
        token[...] = jnp.zeros_like(token)

    lands = [lax.empty((3, w.R, w.nn), BF16) for w in ws]
    out = pl.pallas_call(
        body, name=name, in_specs=[HBM] * (2 * nw),
        out_specs=[SEM, SEM] + [HBM] * (2 * nw) + [VMEM],
        out_shape=[pltpu.SemaphoreType.DMA((3 * nw,))] * 2 + [pltpu.HBM(a.shape, a.dtype) for a in pairs + lands]
        + [jax.ShapeDtypeStruct((8, LANES), F32)],
        input_output_aliases={i: 2 + i for i in range(2 * nw)},
        compiler_params=pltpu.CompilerParams(has_side_effects=EFFECT),
    )(*[_hbm(a) for a in pairs + lands])
    return out[0], out[1], list(out[2:2 + nw]), list(out[2 + nw:2 + 2 * nw]), out[-1]


def _scatter_wait(name, ws, pairs, lands, ssem, rsem, after):
    nw = len(ws)

    def body(*refs):
        pr, land = refs[:nw], refs[nw:2 * nw]
        ssem_ref, rsem_ref = refs[2 * nw], refs[2 * nw + 1]
        x, y, c = _mesh_pos()
        chips, ks = _other_chips(x, y)
        for i, w in enumerate(ws):
            for j, chip in enumerate(chips):
                cp = _remote(w.part(pr[i], ks[j]), land[i].at[j], ssem_ref.at[3 * i + j], rsem_ref.at[3 * i + j], (*chip, c))
                cp.wait_send()
                cp.wait_recv()

    out = pl.pallas_call(
        body, name=name, in_specs=[HBM] * (2 * nw) + [SEM, SEM] + [ANY] * len(after), out_specs=[HBM] * (2 * nw),
        out_shape=[pltpu.HBM(a.shape, a.dtype) for a in pairs + lands],
        input_output_aliases={i: i for i in range(2 * nw)},
        compiler_params=pltpu.CompilerParams(has_side_effects=EFFECT),
    )(*pairs, *lands, ssem, rsem, *after)
    return list(out[nw:])


def _final_sum(name, w, pos, grad, got, parts):
    tr, tn = _grad_tiles(w, w.nn)
    nbc = w.nn // tn

    def body(pos_ref, g_ref, r_ref, p_ref, o_ref):
        acc = g_ref[...].astype(F32) + r_ref[...].astype(F32)
        for j in range(3):
            acc = acc + p_ref[j].astype(F32)
        o_ref[...] = acc

    if w.colshard:
        g_spec = pl.BlockSpec((None, None, tr, tn), lambda i, j, pos: (0, pos[0], i, pos[1] * nbc + j))
        r_spec = pl.BlockSpec((None, tr, tn), lambda i, j, pos: (0, i, pos[1] * nbc + j))
    else:
        g_spec = pl.BlockSpec((None, None, tr, tn), lambda i, j, pos: (pos[1], pos[0], i, j))
        r_spec = pl.BlockSpec((None, tr, tn), lambda i, j, pos: (pos[1], i, j))
    grid_spec = pltpu.PrefetchScalarGridSpec(
        num_scalar_prefetch=1, grid=(w.R // tr, nbc),
        in_specs=[g_spec, r_spec, pl.BlockSpec((3, tr, tn), lambda i, j, pos: (0, i, j))],
        out_specs=pl.BlockSpec((None, tr, tn), lambda i, j, pos: (pos[0], i, j)))
    return pl.pallas_call(body, name=name, grid_spec=grid_spec, out_shape=jax.ShapeDtypeStruct((2, w.R, w.nn), F32),
                          compiler_params=_cp(("parallel",) * 2))(pos, grad, got, parts)


def _share_halves(name, ws, halves, deps=()):
    nw = len(ws)

    def body(*refs):
        out = refs[nw + len(deps):2 * nw + len(deps)]
        ssem, rsem = refs[2 * nw + len(deps):]
        x, y, c = _mesh_pos()
        sib = (x, y, 1 - c)
        cps = [_remote(out[i].at[c], out[i].at[c], ssem.at[i], rsem.at[i], sib) for i in range(nw)]
        for cp in cps:
            cp.start()
        for i, cp in enumerate(cps):
            cp.wait_send()
            _remote(out[i].at[1 - c], out[i].at[1 - c], ssem.at[i], rsem.at[i], sib).wait_recv()

    return pl.pallas_call(
        body, name=name, in_specs=[ANY] * (nw + len(deps)), out_specs=[ANY] * nw,
        out_shape=[jax.ShapeDtypeStruct(h.shape, F32) for h in halves],
        scratch_shapes=[pltpu.SemaphoreType.DMA((nw,)), pltpu.SemaphoreType.DMA((nw,))],
        input_output_aliases={i: i for i in range(nw)},
    )(*halves, *deps)


VEC_ROWS = 16


def _vector_step(d, n_conv, parts, params, deps=()):
    ncw = params[2][0].shape[1]
    n_par = len(params)

    def body(*refs):
        dg1, dba, dbb, dcw, dcb, dps, dg2, dgf, lc = refs[:9]
        wmv = refs[9:9 + 3 * n_par]
        refs = refs[9 + 3 * n_par + len(deps):]
        outs = refs[:4 * n_par]
        loss_ref = refs[4 * n_par]
        snd, got, ssem, rsem = refs[4 * n_par + 1:]
        x, y, c = _mesh_pos()
        me = 4 * x + 2 * y + c
        snd[...] = jnp.zeros_like(snd)
        for row, ref in ((0, dg1), (1, dba), (2, dbb), (3, dps), (4, dg2), (5, dgf), (6, lc)):
            snd[row:row + 1, :] = ref[...]
        snd[7:8, :n_conv] = dcb[...]
        snd[8:11, :n_conv] = dcw[...]
        cps = []
        for r in range(1, N_DEV):
            peer = tuple(1 - p if (r >> b) & 1 else p for p, b in ((x, 2), (y, 1), (c, 0)))
            cps.append(_remote(snd, got.at[me], ssem.at[r - 1], rsem.at[r - 1], peer))
        for cp in cps:
            cp.start()
        got[me] = snd[...]
        for r in range(1, N_DEV):
            peer = tuple(1 - p if (r >> b) & 1 else p for p, b in ((x, 2), (y, 1), (c, 0)))
            _remote(snd, got.at[4 * peer[0] + 2 * peer[1] + peer[2]], ssem.at[r - 1], rsem.at[r - 1], peer).wait_recv()
        for cp in cps:
            cp.wait_send()
        tot = got[0]
        for dev in range(1, N_DEV):
            tot = tot + got[dev]
        loss_ref[...] = jnp.sum(tot[6:7, :], axis=1, keepdims=True)
        k_me = 2 * x + y
        g_cw = jnp.zeros((3, ncw), F32)
        for k in range(N_CHIPS):
            g_cw = g_cw + jnp.where(k_me == k, tot[8:11, k * ncw:(k + 1) * ncw], 0.0)
        grads = [tot[0:1, :], jnp.concatenate([tot[1:2, :], tot[2:3, :]], axis=1), g_cw, tot[7:8, :n_conv],
                 tot[3:4, :], tot[4:5, :], tot[5:6, :]]
        for i, g in enumerate(grads):
            w_ref, m_ref, v_ref = wmv[3 * i:3 * i + 3]
            delta, nm, nv = _adamw_math(w_ref[...], g, m_ref[...], v_ref[...])
            outs[4 * i][...] = g
            outs[4 * i + 1][...] = delta
            outs[4 * i + 2][...] = nm
            outs[4 * i + 3][...] = nv

    args = list(parts)
    out_shape = []
    for w, m, v in params:
        args += [w, m, v]
        out_shape += [jax.ShapeDtypeStruct(w.shape, F32)] * 4
    out_shape.append(jax.ShapeDtypeStruct((1, 1), F32))
    return pl.pallas_call(
        body, name="vector_params_step", in_specs=[VMEM] * len(args) + [ANY] * len(deps),
        out_specs=[VMEM] * len(out_shape), out_shape=out_shape,
        scratch_shapes=[pltpu.VMEM((VEC_ROWS, d), F32), pltpu.VMEM((N_DEV, VEC_ROWS, d), F32),
                        pltpu.SemaphoreType.DMA((N_DEV - 1,)), pltpu.SemaphoreType.DMA((N_DEV - 1,))],
        compiler_params=pltpu.CompilerParams(vmem_limit_bytes=VMEM_LIMIT),
    )(*args, *deps)


def kernel(x, norm1_g, w_in, b_gate, conv_w, conv_b, w_a_out, w_pool, pool_scale, w_o, norm2_g, w_ffn_gate, w_ffn_up, w_ffn_down, final_g, loss_target, m_norm1_g, m_w_in, m_b_gate, m_conv_w, m_conv_b, m_w_a_out, m_w_pool, m_pool_scale, m_w_o, m_norm2_g, m_w_ffn_gate, m_w_ffn_up, m_w_ffn_down, m_final_g, v_norm1_g, v_w_in, v_b_gate, v_conv_w, v_conv_b, v_w_a_out, v_w_pool, v_pool_scale, v_w_o, v_norm2_g, v_w_ffn_gate, v_w_ffn_up, v_w_ffn_down, v_final_g):
    t, d = x.shape[1], x.shape[2]
    n_conv = conv_b.shape[1]
    n_groups, pool_cg, pool_dg = w_pool.shape[1], w_pool.shape[2], N_CHIPS * w_pool.shape[3]
    d_ff = N_CHIPS * w_ffn_gate.shape[2]
    assert n_conv // n_groups == pool_cg and n_conv % (n_groups * MIX_COLS) == 0 and n_groups == len(POOL_WINDOWS)

    big = {"w_in": (w_in, m_w_in, v_w_in), "w_a_out": (w_a_out, m_w_a_out, v_w_a_out), "w_pool": (w_pool, m_w_pool, v_w_pool),
           "w_o": (w_o, m_w_o, v_w_o), "w_ffn_gate": (w_ffn_gate, m_w_ffn_gate, v_w_ffn_gate),
           "w_ffn_up": (w_ffn_up, m_w_ffn_up, v_w_ffn_up), "w_ffn_down": (w_ffn_down, m_w_ffn_down, v_w_ffn_down)}
    colshard = {"w_in": True, "w_a_out": True, "w_pool": True, "w_o": False, "w_ffn_gate": True, "w_ffn_up": True,
                "w_ffn_down": False}
    names = list(big)
    shard2d = {n: big[n][0].reshape(-1, big[n][0].shape[-1]) for n in names}
    ws = [_Weight(n, *shard2d[n].shape, colshard[n]) for n in names]

    xs, tgt = x[0], loss_target[0]
    cw_loc = conv_w[0]
    pos = jnp.stack([lax.axis_index("c"), 2 * lax.axis_index("x") + lax.axis_index("y")]).astype(jnp.int32)
    by_name = {w.name: w for w in ws}
    groups = [[by_name[n] for n in g] for g in (["w_in"], ["w_a_out", "w_pool", "w_o"], ["w_ffn_gate"], ["w_ffn_up"],
                                                 ["w_ffn_down"])]
    first = [sum(len(g) for g in groups[:i]) for i in range(len(groups))]
    rgroups = [groups[0], groups[1], groups[2] + groups[3], groups[4]]

    cw_full = _gather_conv_w(cw_loc)
    cast = lambda w, dep: _cast_place(f"cast_{w.name}", w, pos, shard2d[w.name].reshape(2, w.R, w.nn), deps=[dep])
    chips, ks = _other_chips(lax.axis_index("x"), lax.axis_index("y"))
    kvec = jnp.stack([pos[1], *ks]).astype(jnp.int32)
    full = {}

    def start(name, arrays, copies, after=()):
        ssem, rsem, arrays, token = _split_start(name, arrays, copies.n, copies, after)
        return name, arrays, ssem, rsem, copies, token

    def wait(started, after):
        name, arrays, ssem, rsem, copies, _ = started
        return _split_wait(name + "_wait", arrays, ssem, rsem, copies, after)

    def pass_on(g, got, after=()):
        return start(f"pass_{g}", got, _pass_copies(groups[g]), after)

    def passed(g, st, after=None):
        got = wait(st, [st[5]] if after is None else after)
        full.update({w.name: a.reshape(w.P * 2 * w.R, w.N) for w, a in zip(groups[g], got)})

    near = start("near_0", [cast(w, cw_full) for w in groups[0]], _near_copies(groups[0]))
    rest = [cast(w, near[5]) for grp in groups[1:] for w in grp]
    h1 = _rms_fwd("norm1_fwd", xs, norm1_g, deps=[near[5]])
    proj = _proj_piece("proj_own", h1, shard2d["w_in"], None, kvec, 0, 1, deps=rest)
    got = wait(near, [proj])
    far = start("far_0", got, _far_copies(groups[0]))
    sems_b, lands_b, tok_b = _gather_start("gather_start_b", groups[1:2], rest[:3], after=[far[5]])
    st = start("pass_near_0", far[1], _pass_copies(groups[0], (0, 1)), [tok_b])
    got = wait(st, [st[5]])
    proj = _proj_piece("proj_near", h1, got[0].reshape(-1, groups[0][0].N), proj, kvec, 1, 2)
    st = start("pass_far_0", wait((far[0], got) + far[2:], [proj]), _pass_copies(groups[0], (2,)))
    got = wait(st, [st[5]])
    w_in_full = got[0].reshape(-1, groups[0][0].N)
    proj = _proj_piece("proj_far", h1, w_in_full, proj, kvec, 3, 1)
    got = _gather_wait("gather_wait_1", groups[1], lands_b, *sems_b[0], proj)
    near_g = start("near_2", rest[3:4], _near_copies(groups[2]), got)
    st = pass_on(1, got, [near_g[5]])
    z, p = _mixer_fwd("mixer_fwd", proj, cw_full, conv_b, n_conv, n_groups, deps=[st[5]])
    passed(1, st, [z])
    wp_full = full["w_pool"].reshape(n_groups, pool_cg, pool_dg)
    ya = _mm_nn("conv_out", z, full["w_a_out"], BF16)
    yb = _gmm_nn("pool_out", p, wp_full, BF16)
    merged = _merge_fwd("merge_fwd", proj, b_gate, ya, yb, pool_scale)
    far_g = start("far_2", wait(near_g, [merged]), _far_copies(groups[2]))
    near_u = start("near_3", rest[4:5], _near_copies(groups[3]), [far_g[5]])
    x2 = _mm_nn("mix_out", merged, full["w_o"], F32, add=xs, deps=[near_u[5]])
    st = pass_on(2, wait(far_g, [x2]))
    h2 = _rms_fwd("norm2_fwd", x2, norm2_g, deps=[st[5]])
    passed(2, st, [h2])
    gate = _mm_nn("ffn_gate_a", h2, full["w_ffn_gate"], BF16, part=(0, 2))
    far_u = start("far_3", wait(near_u, [gate]), _far_copies(groups[3]))
    near_d = start("near_4", rest[5:6], _near_copies(groups[4]), [far_u[5]])
    gate = _mm_nn("ffn_gate_b", h2, full["w_ffn_gate"], BF16, part=(1, 2), prev=gate, deps=[near_d[5]])
    passed(3, pass_on(3, wait(far_u, [gate])))
    up_act = _ffn_up_act("ffn_up_act_a", h2, full["w_ffn_up"], gate, part=(0, 2))
    far_d = start("far_4", wait(near_d, [up_act[0]]), _far_copies(groups[4]))
    up, act = _ffn_up_act("ffn_up_act_b", h2, full["w_ffn_up"], gate, part=(1, 2), prev=up_act, deps=[far_d[5]])
    passed(4, pass_on(4, wait(far_d, [act])))
    x3 = _mm_nn("ffn_down", act, full["w_ffn_down"], F32, add=x2, tk=d_ff // 4)

    pending = {}

    def pair_start(g, grads):
        grp = rgroups[g]
        gcan = [grads[w.name].reshape(w.P, 2, w.R, w.N) for w in grp]
        slots = [lax.empty((w.P, w.R, w.N), BF16) for w in grp]
        pending[g] = _split_start(f"pair_start_{g}", gcan + slots, len(grp), _pair_copies(len(grp)))
        return pending[g][3]

    def scatter_start(g, after):
        grp = rgroups[g]
        n = len(grp)
        ssem, rsem, arrs, _ = pending[g]
        arrs = _split_wait(f"pair_wait_{g}", arrs, ssem, rsem, _pair_copies(n), after)
        gcan, sib = arrs[:n], arrs[n:]
        pairs = [_pair_sum(f"pair_sum_{w.name}", w, pos, a, s) for w, a, s in zip(grp, gcan, sib)]
        ssem, rsem, pairs, slots, token = _scatter_start(f"scatter_start_{g}", grp, pairs)
        pending[g] = (gcan, sib, pairs, slots, ssem, rsem)
        return token

    def reduce_finish(g, after):
        grp = rgroups[g]
        gcan, sib, pairs, slots, ssem, rsem = pending[g]
        parts = _scatter_wait(f"scatter_wait_{g}", grp, pairs, slots, ssem, rsem, after)
        return [_final_sum(f"final_sum_{w.name}", w, pos, a, s, q) for w, a, s, q in zip(grp, gcan, sib, parts)]

    grads = {}
    dx3, dx3b, d_gf, loss_cols = _final_bwd("final_bwd", x3, final_g.reshape(1, d), tgt)
    dgate, dup = _ffn_bwd("ffn_bwd", dx3b, full["w_ffn_down"], gate, up)
    grads["w_ffn_down"] = _mm_tn("dw_ffn_down", act, dx3b, BF16)
    tok = pair_start(3, grads)
    dh2 = _mm_nt("d_h2", [(dgate, full["w_ffn_gate"]), (dup, full["w_ffn_up"])], F32, tk=d_ff // 4, deps=[tok])
    tok = scatter_start(3, [dh2])
    grads["w_ffn_gate"] = _mm_tn("dw_ffn_gate", h2, dgate, BF16, deps=[tok])
    grads["w_ffn_up"] = _mm_tn("dw_ffn_up", h2, dup, BF16)
    tok = pair_start(2, grads)
    dx2, dx2b, d_g2 = _rms_bwd("norm2_bwd", x2, norm2_g, dh2, dx3, True, deps=[tok])
    dmerged = _mm_nt("d_merged", [(dx2b, full["w_o"])], BF16, tk=d)
    grads["w_o"] = _mm_tn("dw_o", merged, dx2b, BF16)
    tok = scatter_start(2, [grads["w_o"]])
    dya, dyb, dproj, d_bga, d_bgb, d_ps = _merge_bwd("merge_bwd", dmerged, proj, b_gate, ya, yb, pool_scale, deps=[tok])
    dz = _mm_nt("d_z", [(dya, full["w_a_out"])], BF16, tk=d)
    grads["w_a_out"] = _mm_tn("dw_a_out", z, dya, BF16)
    dp = _gmm_nt("d_pool", dyb, wp_full, BF16)
    grads["w_pool"] = _gmm_tn("dw_pool", p, dyb, n_groups, BF16)
    tok = pair_start(1, grads)
    dproj, d_cw, d_cb = _mixer_bwd("mixer_bwd", dz, dp, proj, cw_full, conv_b, dproj, n_conv, n_groups, deps=[tok])
    tok = scatter_start(1, [dproj])
    grads["w_in"] = _mm_tn("dw_in", h1, dproj, BF16, deps=[tok])
    tok = pair_start(0, grads)
    dh1 = _mm_nt("d_h1", [(dproj, w_in_full)], F32, tk=proj.shape[1] // 4, deps=[tok])
    tok = scatter_start(0, [dh1])
    grad_x, d_g1 = _rms_bwd("norm1_bwd", xs, norm1_g, dh1, dx2, False, deps=[tok])

    g_big, d_big, m_big, v_big = {}, {}, {}, {}

    def update(wsub, shared):
        out = []
        for w, g in zip(wsub, shared):
            wt, mt, vt = big[w.name]
            g2 = g.reshape(2 * w.R, w.nn)
            go, dl, nm, nv = _adamw(f"adamw_{w.name}", shard2d[w.name], g2, mt.reshape(g2.shape), vt.reshape(g2.shape))
            g_big[w.name], d_big[w.name], m_big[w.name], v_big[w.name] = (a.reshape(wt.shape) for a in (go, dl, nm, nv))
            out.append(nv)
        return out

    after = [grad_x]
    started = []
    for g in (3, 2, 1):
        halves = reduce_finish(g, after)
        share = _share_copies(len(halves))
        ssem, rsem, halves, tok = _split_start(f"share_start_{g}", halves, len(halves), share)
        started.append((g, ssem, rsem, halves, share))
        after = [tok]
    for g, ssem, rsem, halves, share in started:
        after = update(rgroups[g], _split_wait(f"share_wait_{g}", halves, ssem, rsem, share, after))
    after = update(rgroups[0], _share_halves("share_halves_w_in", rgroups[0], reduce_finish(0, after)))

    vec_names = ["norm1_g", "b_gate", "conv_w", "conv_b", "pool_scale", "norm2_g", "final_g"]
    vec = {"norm1_g": (norm1_g, m_norm1_g, v_norm1_g), "b_gate": (b_gate, m_b_gate, v_b_gate),
           "conv_w": (cw_loc, m_conv_w[0], v_conv_w[0]), "conv_b": (conv_b, m_conv_b, v_conv_b),
           "pool_scale": (pool_scale, m_pool_scale, v_pool_scale), "norm2_g": (norm2_g, m_norm2_g, v_norm2_g),
           "final_g": tuple(a.reshape(1, d) for a in (final_g, m_final_g, v_final_g))}
    vout = _vector_step(d, n_conv, [d_g1, d_bga, d_bgb, d_cw, d_cb, d_ps, d_g2, d_gf, loss_cols],
                        [vec[n] for n in vec_names], deps=after)

    shapes = {"conv_w": conv_w.shape, "final_g": final_g.shape}
    g_vec, d_vec, m_vec, v_vec = ({n: vout[4 * i + q].reshape(shapes.get(n, vec[n][0].shape)) for i, n in enumerate(vec_names)}
                                  for q in range(4))
    loss = vout[-1].reshape(())

    order = ["norm1_g", "w_in", "b_gate", "conv_w", "conv_b", "w_a_out", "w_pool", "pool_scale", "w_o", "norm2_g",
             "w_ffn_gate", "w_ffn_up", "w_ffn_down", "final_g"]
    pick = lambda vecs, bigs: [vecs[n] if n in vecs else bigs[n] for n in order]
    return (loss, grad_x.reshape(x.shape), *pick(g_vec, g_big), *pick(d_vec, d_big), *pick(m_vec, m_big),
            *pick(v_vec, v_big))
```

```python
import functools

import jax
import jax.numpy as jnp
from jax import lax
from jax.experimental import pallas as pl
from jax.experimental.pallas import tpu as pltpu

F32, BF16 = jnp.float32, jnp.bfloat16
MESH = pl.DeviceIdType.MESH
ANY = pl.BlockSpec(memory_space=pl.ANY)
VMEM = pl.BlockSpec(memory_space=pltpu.VMEM)
HBM = pl.BlockSpec(memory_space=pltpu.HBM)
SEM = pl.BlockSpec(memory_space=pltpu.SEMAPHORE)
EFFECT = pltpu.SideEffectType.DATAFLOW_SIDE_EFFECTING

EPS = 1e-6
POOL_WINDOWS = (2, 4, 8, 16)
ADAM_LR, ADAM_B1, ADAM_B2, ADAM_EPS, ADAM_WD, ADAM_STEP = 0.001, 0.9, 0.999, 1e-08, 0.01, 10

V7X_VMEM_BYTES = 64 * 1024 * 1024
VMEM_LIMIT = V7X_VMEM_BYTES * 3 // 4
LANES = 128
N_CHIPS = 4
N_DEV = 8

_DIMS = {
    "nn": (((1,), (0,)), ((), ())),
    "nt": (((1,), (1,)), ((), ())),
    "tn": (((0,), (0,)), ((), ())),
}


def _cp(sem):
    return pltpu.CompilerParams(dimension_semantics=sem, vmem_limit_bytes=VMEM_LIMIT)


def _mesh_pos():
    return lax.axis_index("x"), lax.axis_index("y"), lax.axis_index("c")


def _mm(name, pairs, *, mode, grid, out_shape, o_spec, nk=1, kaxis=None, add=None, deps=(), prev=None):
    npair = len(pairs)
    has_add = add is not None

    def body(*refs):
        ab = refs[: 2 * npair]
        pos = 2 * npair
        add_ref = refs[pos] if has_add else None
        pos += int(has_add) + len(deps) + (prev is not None)
        o_ref = refs[pos]
        acc_ref = refs[pos + 1] if nk > 1 else None
        d = None
        for p in range(npair):
            t = lax.dot_general(ab[2 * p][...], ab[2 * p + 1][...], _DIMS[mode], preferred_element_type=F32)
            d = t if d is None else d + t
        if nk == 1:
            if has_add:
                d = d + add_ref[...].astype(F32)
            o_ref[...] = d.astype(o_ref.dtype)
        else:
            k = pl.program_id(kaxis)

            @pl.when(k == 0)
            def _():
                acc_ref[...] = d

            @pl.when(k > 0)
            def _():
                acc_ref[...] += d

            @pl.when(k == nk - 1)
            def _():
                r = acc_ref[...]
                if has_add:
                    r = r + add_ref[...].astype(F32)
                o_ref[...] = r.astype(o_ref.dtype)

    args, specs = [], []
    for a, a_spec, b, b_spec in pairs:
        args += [a, b]
        specs += [a_spec, b_spec]
    if has_add:
        args.append(add[0])
        specs.append(add[1])
    args += list(deps)
    specs += [ANY] * len(deps)
    aliases = {}
    if prev is not None:
        aliases = {len(args): 0}
        args.append(prev)
        specs.append(ANY)
    scratch = []
    if nk > 1:
        blk = [d for d in o_spec.block_shape if d is not None]
        scratch = [pltpu.VMEM(tuple(blk), F32)]
    sem = tuple("arbitrary" if (nk > 1 and ax == kaxis) else "parallel" for ax in range(len(grid)))
    return pl.pallas_call(
        body, name=name, grid=grid, in_specs=specs, out_specs=o_spec, out_shape=out_shape,
        scratch_shapes=scratch, input_output_aliases=aliases, compiler_params=_cp(sem),
    )(*args)


def _tile_span(n_tiles, part):
    if part is None:
        return 0, n_tiles
    p, of = part
    return p * n_tiles // of, (p + 1) * n_tiles // of


def _tile(n, pref):
    if n <= pref:
        return n
    for t in range(pref, 0, -LANES):
        if t % LANES == 0 and n % t == 0:
            return t
    raise ValueError(f"no tile for {n}")


def _mm_nn(name, a, b, out_dtype, add=None, tk=None, deps=(), part=None, prev=None):
    m, kk = a.shape
    n = b.shape[1]
    tm, tn = _tile(m, 1024), _tile(n, 512)
    out_shape = jax.ShapeDtypeStruct((m, n), out_dtype)
    if tk is None or tk == kk:
        j0, j1 = _tile_span(n // tn, part)
        grid = (m // tm, j1 - j0)
        pairs = [(a, pl.BlockSpec((tm, kk), lambda i, j: (i, 0)), b, pl.BlockSpec((kk, tn), lambda i, j: (0, j0 + j)))]
        o_spec = pl.BlockSpec((tm, tn), lambda i, j: (i, j0 + j))
        add_ = None if add is None else (add, pl.BlockSpec((tm, tn), lambda i, j: (i, j0 + j)))
        return _mm(name, pairs, mode="nn", grid=grid, out_shape=out_shape, o_spec=o_spec, add=add_, deps=deps, prev=prev)
    tn = _tile(n, 1024)
    nk = kk // tk
    grid = (m // tm, n // tn, nk)
    pairs = [(a, pl.BlockSpec((tm, tk), lambda i, j, k: (i, k)), b, pl.BlockSpec((tk, tn), lambda i, j, k: (k, j)))]
    o_spec = pl.BlockSpec((tm, tn), lambda i, j, k: (i, j))
    add_ = None if add is None else (add, pl.BlockSpec((tm, tn), lambda i, j, k: (i, j)))
    return _mm(name, pairs, mode="nn", grid=grid, out_shape=out_shape, o_spec=o_spec, nk=nk, kaxis=2, add=add_, deps=deps)


def _mm_nt(name, abs_, out_dtype, tk, deps=()):
    m, kk = abs_[0][0].shape
    n = abs_[0][1].shape[0]
    tm = _tile(m, 1024)
    nk = kk // tk
    tn = _tile(n, 512 if nk == 1 else 1024)
    out_shape = jax.ShapeDtypeStruct((m, n), out_dtype)
    if nk == 1:
        grid = (m // tm, n // tn)
        pairs = [(a, pl.BlockSpec((tm, kk), lambda i, j: (i, 0)), b, pl.BlockSpec((tn, kk), lambda i, j: (j, 0)))
                 for a, b in abs_]
        o_spec = pl.BlockSpec((tm, tn), lambda i, j: (i, j))
        return _mm(name, pairs, mode="nt", grid=grid, out_shape=out_shape, o_spec=o_spec, deps=deps)
    grid = (m // tm, n // tn, nk)
    pairs = [(a, pl.BlockSpec((tm, tk), lambda i, j, k: (i, k)), b, pl.BlockSpec((tn, tk), lambda i, j, k: (j, k)))
             for a, b in abs_]
    o_spec = pl.BlockSpec((tm, tn), lambda i, j, k: (i, j))
    return _mm(name, pairs, mode="nt", grid=grid, out_shape=out_shape, o_spec=o_spec, nk=nk, kaxis=2, deps=deps)


def _mm_tn(name, a, b, out_dtype, deps=()):
    t, m = a.shape
    n = b.shape[1]
    tm, tn = _tile(m, 512), _tile(n, 2048)
    if n > m:
        grid = (n // tn, m // tm)
        a_map, b_map, o_map = (lambda j, i: (0, i)), (lambda j, i: (0, j)), (lambda j, i: (i, j))
    else:
        grid = (m // tm, n // tn)
        a_map, b_map, o_map = (lambda i, j: (0, i)), (lambda i, j: (0, j)), (lambda i, j: (i, j))
    pairs = [(a, pl.BlockSpec((t, tm), a_map), b, pl.BlockSpec((t, tn), b_map))]
    o_spec = pl.BlockSpec((tm, tn), o_map)
    return _mm(name, pairs, mode="tn", grid=grid, out_shape=jax.ShapeDtypeStruct((m, n), out_dtype), o_spec=o_spec,
               deps=deps)


def _mm_tn_half(name, a, b, pos, mine, add=None, deps=()):
    t, m = a.shape
    r, n = m // 2, b.shape[1]
    tm, tn = _tile(r, 512), _tile(n, 2048)
    nbi = r // tm
    half = (lambda pos: pos[0]) if mine else (lambda pos: 1 - pos[0])
    if n > r:
        grid, ij = (n // tn, nbi), (lambda g0, g1: (g1, g0))
    else:
        grid, ij = (nbi, n // tn), (lambda g0, g1: (g0, g1))
    has_add = add is not None

    def body(pos_ref, a_ref, b_ref, *rest):
        d = lax.dot_general(a_ref[...], b_ref[...], _DIMS["tn"], preferred_element_type=F32)
        if has_add:
            d = d + rest[0][...].astype(F32)
        rest[-1][...] = d.astype(BF16)

    o_spec = pl.BlockSpec((None, tm, tn), lambda g0, g1, pos: (0, *ij(g0, g1)))
    grid_spec = pltpu.PrefetchScalarGridSpec(
        num_scalar_prefetch=1, grid=grid,
        in_specs=[pl.BlockSpec((t, tm), lambda g0, g1, pos: (0, half(pos) * nbi + ij(g0, g1)[0])),
                  pl.BlockSpec((t, tn), lambda g0, g1, pos: (0, ij(g0, g1)[1]))]
        + ([o_spec] if has_add else []) + [ANY] * len(deps),
        out_specs=o_spec)
    return pl.pallas_call(body, name=name, grid_spec=grid_spec, out_shape=jax.ShapeDtypeStruct((1, r, n), BF16),
                          compiler_params=_cp(("parallel",) * 2))(pos, a, b, *([add] if has_add else []), *deps)


def _proj_piece(name, h, w, prev, kvec, base, count, deps=()):
    t, kk = h.shape
    own = w.dtype == F32
    nn = w.shape[1] if own else w.shape[1] // N_CHIPS
    tm, tn = _tile(t, 1024), _tile(nn, 512)
    nb = nn // tn

    def body(kv_ref, h_ref, w_ref, *rest):
        rest[-1][...] = lax.dot_general(h_ref[...], w_ref[...].astype(BF16), _DIMS["nn"],
                                        preferred_element_type=F32).astype(BF16)

    cols = lambda s, i, j, kv: (0, j) if own else (0, kv[base + s] * nb + j)
    extra = ([] if prev is None else [prev]) + list(deps)
    grid_spec = pltpu.PrefetchScalarGridSpec(
        num_scalar_prefetch=1, grid=(count, t // tm, nb),
        in_specs=[pl.BlockSpec((tm, kk), lambda s, i, j, kv: (i, 0)), pl.BlockSpec((kk, tn), cols)] + [ANY] * len(extra),
        out_specs=pl.BlockSpec((tm, tn), lambda s, i, j, kv: (i, kv[base + s] * nb + j)))
    return pl.pallas_call(body, name=name, grid_spec=grid_spec, out_shape=jax.ShapeDtypeStruct((t, N_CHIPS * nn), BF16),
                          input_output_aliases={} if prev is None else {3: 0},
                          compiler_params=_cp(("parallel",) * 3))(kvec, h, w, *extra)


def _gmm_nn(name, p, w, out_dtype):
    t = p.shape[0]
    g, cg, dg = w.shape
    tm = _tile(t, 1024)
    pairs = [(p, pl.BlockSpec((tm, cg), lambda i, j: (i, j)), w, pl.BlockSpec((None, cg, dg), lambda i, j: (j, 0, 0)))]
    o_spec = pl.BlockSpec((tm, dg), lambda i, j: (i, j))
    return _mm(name, pairs, mode="nn", grid=(t // tm, g), out_shape=jax.ShapeDtypeStruct((t, g * dg), out_dtype),
               o_spec=o_spec)


def _gmm_nt(name, dy, w, out_dtype):
    t = dy.shape[0]
    g, cg, dg = w.shape
    tm = _tile(t, 1024)
    pairs = [(dy, pl.BlockSpec((tm, dg), lambda i, j: (i, j)), w, pl.BlockSpec((None, cg, dg), lambda i, j: (j, 0, 0)))]
    o_spec = pl.BlockSpec((tm, cg), lambda i, j: (i, j))
    return _mm(name, pairs, mode="nt", grid=(t // tm, g), out_shape=jax.ShapeDtypeStruct((t, g * cg), out_dtype),
               o_spec=o_spec)


def _gmm_tn(name, p, dy, g, out_dtype):
    t = p.shape[0]
    cg, dg = p.shape[1] // g, dy.shape[1] // g
    pairs = [(p, pl.BlockSpec((t, cg), lambda j: (0, j)), dy, pl.BlockSpec((t, dg), lambda j: (0, j)))]
    o_spec = pl.BlockSpec((None, cg, dg), lambda j: (j, 0, 0))
    return _mm(name, pairs, mode="tn", grid=(g,), out_shape=jax.ShapeDtypeStruct((g, cg, dg), out_dtype), o_spec=o_spec)


ROW_TILE = 256


def _rows(t):
    return _tile8(t, ROW_TILE)


def _tile8(n, pref):
    if n <= pref:
        return n
    for t in range(pref, 0, -8):
        if n % t == 0:
            return t
    raise ValueError(f"no row tile for {n}")


def _cast_place(name, w, pos, shard, deps=()):
    tr = _tile8(w.R, 512)
    if w.colshard:
        o_map = lambda h, i, pos: (0, h, i, pos[1])
    else:
        o_map = lambda h, i, pos: (pos[1], h, i, 0)

    def body(pos_ref, w_ref, *rest):
        rest[-1][...] = w_ref[...].astype(BF16)

    grid_spec = pltpu.PrefetchScalarGridSpec(
        num_scalar_prefetch=1, grid=(2, w.R // tr),
        in_specs=[pl.BlockSpec((None, tr, w.nn), lambda h, i, pos: (h, i, 0))] + [ANY] * len(deps),
        out_specs=pl.BlockSpec((None, None, tr, w.nn), o_map))
    return pl.pallas_call(body, name=name, grid_spec=grid_spec, out_shape=jax.ShapeDtypeStruct((w.P, 2, w.R, w.N), BF16),
                          compiler_params=_cp(("parallel", "parallel")))(pos, shard, *deps)


def _rms_fwd(name, x, g, deps=()):
    t, d = x.shape
    tm = _rows(t)

    def body(x_ref, g_ref, *rest):
        xf = x_ref[...]
        r = lax.rsqrt(jnp.mean(xf * xf, axis=-1, keepdims=True) + EPS)
        rest[-1][...] = (xf * r * g_ref[...]).astype(BF16)

    return pl.pallas_call(
        body, name=name, grid=(t // tm,),
        in_specs=[pl.BlockSpec((tm, d), lambda i: (i, 0)), pl.BlockSpec((1, d), lambda i: (0, 0))] + [ANY] * len(deps),
        out_specs=pl.BlockSpec((tm, d), lambda i: (i, 0)), out_shape=jax.ShapeDtypeStruct((t, d), BF16),
        compiler_params=_cp(("parallel",)),
    )(x, g, *deps)


def _rms_bwd(name, x, g, dh, dres, want_bf16, deps=()):
    t, d = x.shape
    tm = _rows(t)

    def body(x_ref, g_ref, dh_ref, dres_ref, *rest):
        rest = rest[len(deps):]
        dx_ref, rest = rest[0], rest[1:]
        dg_ref = rest[-1]
        xf = x_ref[...]
        r = lax.rsqrt(jnp.mean(xf * xf, axis=-1, keepdims=True) + EPS)
        xh = xf * r
        dhf = dh_ref[...]
        dxh = dhf * g_ref[...]
        m = jnp.mean(dxh * xh, axis=-1, keepdims=True)
        dx = dres_ref[...] + r * (dxh - xh * m)
        dx_ref[...] = dx
        if want_bf16:
            rest[0][...] = dx.astype(BF16)

        @pl.when(pl.program_id(0) == 0)
        def _():
            dg_ref[...] = jnp.zeros_like(dg_ref)

        dg_ref[...] += jnp.sum(dhf * xh, axis=0, keepdims=True)

    row = pl.BlockSpec((tm, d), lambda i: (i, 0))
    vec = pl.BlockSpec((1, d), lambda i: (0, 0))
    out_specs = [row] + ([row] if want_bf16 else []) + [vec]
    out_shape = ([jax.ShapeDtypeStruct((t, d), F32)] + ([jax.ShapeDtypeStruct((t, d), BF16)] if want_bf16 else [])
                 + [jax.ShapeDtypeStruct((1, d), F32)])
    return pl.pallas_call(body, name=name, grid=(t // tm,), in_specs=[row, vec, row, row] + [ANY] * len(deps),
                          out_specs=out_specs, out_shape=out_shape, compiler_params=_cp(("arbitrary",)))(x, g, dh, dres, *deps)


def _final_bwd(name, x3, gf, tgt):
    t, d = x3.shape
    tm = _rows(t)

    def body(x_ref, g_ref, t_ref, dx_ref, dxb_ref, dg_ref, lc_ref):
        xf = x_ref[...]
        g = g_ref[...]
        r = lax.rsqrt(jnp.mean(xf * xf, axis=-1, keepdims=True) + EPS)
        xh = xf * r
        diff = xh * g - t_ref[...]
        dy = diff * (1.0 / d)
        dxh = dy * g
        m = jnp.mean(dxh * xh, axis=-1, keepdims=True)
        dx = r * (dxh - xh * m)
        dx_ref[...] = dx
        dxb_ref[...] = dx.astype(BF16)

        @pl.when(pl.program_id(0) == 0)
        def _():
            dg_ref[...] = jnp.zeros_like(dg_ref)
            lc_ref[...] = jnp.zeros_like(lc_ref)

        dg_ref[...] += jnp.sum(dy * xh, axis=0, keepdims=True)
        lc_ref[...] += jnp.sum(diff * diff, axis=0, keepdims=True) * (0.5 / d)

    row = pl.BlockSpec((tm, d), lambda i: (i, 0))
    vec = pl.BlockSpec((1, d), lambda i: (0, 0))
    return pl.pallas_call(
        body, name=name, grid=(t // tm,), in_specs=[row, vec, row], out_specs=[row, row, vec, vec],
        out_shape=[jax.ShapeDtypeStruct((t, d), F32), jax.ShapeDtypeStruct((t, d), BF16),
                   jax.ShapeDtypeStruct((1, d), F32), jax.ShapeDtypeStruct((1, d), F32)],
        compiler_params=_cp(("arbitrary",)),
    )(x3, gf, tgt)


def _shift_down(v, k, t_idx):
    return jnp.where(t_idx >= k, pltpu.roll(v, k, 0), 0.0)


def _shift_up(v, k, t_idx):
    n = v.shape[0]
    return jnp.where(t_idx < n - k, pltpu.roll(v, n - k, 0), 0.0)


def _window_sums(v, shift, t_idx, grp):
    s = v + shift(v, 1, t_idx)
    out = s
    for lvl in range(1, len(POOL_WINDOWS)):
        s = s + shift(s, 1 << lvl, t_idx)
        out = jnp.where(grp >= lvl, s, out)
    return out


def _window_count(t_idx, grp):
    return jnp.minimum(t_idx + 1, jnp.left_shift(2, grp)).astype(F32)


MIX_COLS = 128


def _mixer_fwd(name, proj, cw, cb, n_conv, n_groups, deps=()):
    t = proj.shape[0]
    nb = n_conv // MIX_COLS
    per_group = n_conv // n_groups // MIX_COLS

    def body(ba_ref, ca_ref, va_ref, vb_ref, cw_ref, cb_ref, *rest):
        z_ref, p_ref = rest[len(deps):]
        t_idx = lax.broadcasted_iota(jnp.int32, (t, MIX_COLS), 0)
        q = ca_ref[...].astype(F32) * va_ref[...].astype(F32)
        w = cw_ref[...]
        u = cb_ref[...] + w[0:1] * _shift_down(q, 2, t_idx) + w[1:2] * _shift_down(q, 1, t_idx) + w[2:3] * q
        z_ref[...] = (ba_ref[...].astype(F32) * u).astype(BF16)
        grp = pl.program_id(0) // per_group
        v = vb_ref[...].astype(F32)
        p_ref[...] = (_window_sums(v, _shift_down, t_idx, grp) / _window_count(t_idx, grp) - v).astype(BF16)

    col = lambda s: pl.BlockSpec((t, MIX_COLS), lambda j: (0, s * nb + j))
    return pl.pallas_call(
        body, name=name, grid=(nb,),
        in_specs=[col(0), col(1), col(2), col(3), pl.BlockSpec((3, MIX_COLS), lambda j: (0, j)),
                  pl.BlockSpec((1, MIX_COLS), lambda j: (0, j))] + [ANY] * len(deps),
        out_specs=[col(0), col(0)],
        out_shape=[jax.ShapeDtypeStruct((t, n_conv), BF16), jax.ShapeDtypeStruct((t, n_conv), BF16)],
        compiler_params=_cp(("parallel",)),
    )(proj, proj, proj, proj, cw, cb, *deps)


def _mixer_bwd(name, dz, dp, proj, cw, cb, dproj, n_conv, n_groups, deps=()):
    t = proj.shape[0]
    nb = n_conv // MIX_COLS
    per_group = n_conv // n_groups // MIX_COLS

    def body(dz_ref, dp_ref, ba_ref, ca_ref, va_ref, cw_ref, cb_ref, _, *rest):
        o_ref, dcw_ref, dcb_ref, scr = rest[len(deps):]
        s = pl.program_id(1)

        @pl.when(s == 0)
        def _():
            t_idx = lax.broadcasted_iota(jnp.int32, (t, MIX_COLS), 0)
            ca, va = ca_ref[...].astype(F32), va_ref[...].astype(F32)
            q = ca * va
            q1, q2 = _shift_down(q, 1, t_idx), _shift_down(q, 2, t_idx)
            w = cw_ref[...]
            u = cb_ref[...] + w[0:1] * q2 + w[1:2] * q1 + w[2:3] * q
            dzf = dz_ref[...].astype(F32)
            du = dzf * ba_ref[...].astype(F32)
            scr[0] = (dzf * u).astype(BF16)
            dq = w[2:3] * du + w[1:2] * _shift_up(du, 1, t_idx) + w[0:1] * _shift_up(du, 2, t_idx)
            scr[1] = (dq * va).astype(BF16)
            scr[2] = (dq * ca).astype(BF16)
            dcb_ref[...] = jnp.sum(du, axis=0, keepdims=True)
            dcw_ref[0:1, :] = jnp.sum(du * q2, axis=0, keepdims=True)
            dcw_ref[1:2, :] = jnp.sum(du * q1, axis=0, keepdims=True)
            dcw_ref[2:3, :] = jnp.sum(du * q, axis=0, keepdims=True)
            grp = pl.program_id(0) // per_group
            dpf = dp_ref[...].astype(F32)
            e = dpf / _window_count(t_idx, grp)
            scr[3] = (_window_sums(e, _shift_up, t_idx, grp) - dpf).astype(BF16)

        o_ref[...] = scr[s]

    col = lambda c: pl.BlockSpec((t, MIX_COLS), lambda j, s: (0, c * nb + j))
    own = pl.BlockSpec((t, MIX_COLS), lambda j, s: (0, j))
    return pl.pallas_call(
        body, name=name, grid=(nb, 4),
        in_specs=[own, own, col(0), col(1), col(2), pl.BlockSpec((3, MIX_COLS), lambda j, s: (0, j)),
                  pl.BlockSpec((1, MIX_COLS), lambda j, s: (0, j)), ANY] + [ANY] * len(deps),
        out_specs=[pl.BlockSpec((t, MIX_COLS), lambda j, s: (0, s * nb + j)),
                   pl.BlockSpec((3, MIX_COLS), lambda j, s: (0, j)), pl.BlockSpec((1, MIX_COLS), lambda j, s: (0, j))],
        out_shape=[jax.ShapeDtypeStruct(dproj.shape, BF16), jax.ShapeDtypeStruct((3, n_conv), F32),
                   jax.ShapeDtypeStruct((1, n_conv), F32)],
        scratch_shapes=[pltpu.VMEM((4, t, MIX_COLS), BF16)],
        input_output_aliases={7: 0},
        compiler_params=_cp(("arbitrary", "arbitrary")),
    )(dz, dp, proj, proj, proj, cw, cb, dproj, *deps)


def _merge_fwd(name, proj, bg, ya, yb, ps):
    t, d = ya.shape
    tm = _rows(t)

    def body(gab_ref, bg_ref, ya_ref, yb_ref, ps_ref, o_ref):
        gab = gab_ref[...].astype(F32) + bg_ref[...]
        sa, sb = jax.nn.sigmoid(gab[:, :d]), jax.nn.sigmoid(gab[:, d:])
        o_ref[...] = (sa * ya_ref[...].astype(F32) + sb * (yb_ref[...].astype(F32) * ps_ref[...])).astype(BF16)

    row = pl.BlockSpec((tm, d), lambda i: (i, 0))
    return pl.pallas_call(
        body, name=name, grid=(t // tm,),
        in_specs=[pl.BlockSpec((tm, 2 * d), lambda i: (i, 1)), pl.BlockSpec((1, 2 * d), lambda i: (0, 0)), row, row,
                  pl.BlockSpec((1, d), lambda i: (0, 0))],
        out_specs=row, out_shape=jax.ShapeDtypeStruct((t, d), BF16), compiler_params=_cp(("parallel",)),
    )(proj, bg, ya, yb, ps)


def _merge_bwd(name, dm, proj, bg, ya, yb, ps, deps=()):
    t, d = ya.shape
    tm = _rows(t)

    def body(dm_ref, gab_ref, bg_ref, ya_ref, yb_ref, ps_ref, *rest):
        dya_ref, dyb_ref, dg_ref, dba_ref, dbb_ref, dps_ref = rest[len(deps):]
        gab = gab_ref[...].astype(F32) + bg_ref[...]
        sa, sb = jax.nn.sigmoid(gab[:, :d]), jax.nn.sigmoid(gab[:, d:])
        dmf = dm_ref[...].astype(F32)
        ybf, ps_ = yb_ref[...].astype(F32), ps_ref[...]
        dya_ref[...] = (dmf * sa).astype(BF16)
        dyb = dmf * sb
        dyb_ref[...] = (dyb * ps_).astype(BF16)
        dga = dmf * ya_ref[...].astype(F32) * sa * (1.0 - sa)
        dgb = dmf * (ybf * ps_) * sb * (1.0 - sb)
        dg_ref[:, :d] = dga.astype(BF16)
        dg_ref[:, d:] = dgb.astype(BF16)

        @pl.when(pl.program_id(0) == 0)
        def _():
            dba_ref[...] = jnp.zeros_like(dba_ref)
            dbb_ref[...] = jnp.zeros_like(dbb_ref)
            dps_ref[...] = jnp.zeros_like(dps_ref)

        dba_ref[...] += jnp.sum(dga, axis=0, keepdims=True)
        dbb_ref[...] += jnp.sum(dgb, axis=0, keepdims=True)
        dps_ref[...] += jnp.sum(dyb * ybf, axis=0, keepdims=True)

    row = pl.BlockSpec((tm, d), lambda i: (i, 0))
    vec = pl.BlockSpec((1, d), lambda i: (0, 0))
    gates = pl.BlockSpec((tm, 2 * d), lambda i: (i, 1))
    return pl.pallas_call(
        body, name=name, grid=(t // tm,),
        in_specs=[row, gates, pl.BlockSpec((1, 2 * d), lambda i: (0, 0)), row, row, vec] + [ANY] * len(deps),
        out_specs=[row, row, gates, vec, vec, vec],
        out_shape=[jax.ShapeDtypeStruct((t, d), BF16), jax.ShapeDtypeStruct((t, d), BF16),
                   jax.ShapeDtypeStruct(proj.shape, BF16), jax.ShapeDtypeStruct((1, d), F32),
                   jax.ShapeDtypeStruct((1, d), F32), jax.ShapeDtypeStruct((1, d), F32)],
        compiler_params=_cp(("arbitrary",)),
    )(dm, proj, bg, ya, yb, ps, *deps)


def _ffn_up_act(name, h, w_up, gate, part=None, prev=None, deps=()):
    t, d = h.shape
    f = w_up.shape[1]
    tm, tf = _tile(t, 1024), _tile(f, 512)
    j0, j1 = _tile_span(f // tf, part)
    n_prev = 0 if prev is None else 2
    extra = ([] if prev is None else list(prev)) + list(deps)

    def body(h_ref, w_ref, g_ref, *rest):
        u_ref, a_ref = rest[len(extra):]
        u = lax.dot_general(h_ref[...], w_ref[...], _DIMS["nn"], preferred_element_type=F32)
        g = g_ref[...].astype(F32)
        u_ref[...] = u.astype(BF16)
        a_ref[...] = (g * jax.nn.sigmoid(g) * u).astype(BF16)

    blk = pl.BlockSpec((tm, tf), lambda i, j: (i, j0 + j))
    shp = jax.ShapeDtypeStruct((t, f), BF16)
    return pl.pallas_call(
        body, name=name, grid=(t // tm, j1 - j0),
        in_specs=[pl.BlockSpec((tm, d), lambda i, j: (i, 0)), pl.BlockSpec((d, tf), lambda i, j: (0, j0 + j)), blk]
        + [ANY] * len(extra),
        out_specs=[blk, blk], out_shape=[shp, shp], input_output_aliases={3 + i: i for i in range(n_prev)},
        compiler_params=_cp(("parallel", "parallel")))(h, w_up, gate, *extra)


def _ffn_bwd(name, dy, w_down, gate, up):
    t, d = dy.shape
    f = w_down.shape[0]
    tm, tf = _tile(t, 1024), _tile(f, 512)

    def body(dy_ref, w_ref, g_ref, u_ref, dg_ref, du_ref):
        da = lax.dot_general(dy_ref[...], w_ref[...], _DIMS["nt"], preferred_element_type=F32)
        g = g_ref[...].astype(F32)
        s = jax.nn.sigmoid(g)
        du_ref[...] = (da * (g * s)).astype(BF16)
        dg_ref[...] = (da * u_ref[...].astype(F32) * (s * (1.0 + g * (1.0 - s)))).astype(BF16)

    blk = pl.BlockSpec((tm, tf), lambda i, j: (i, j))
    shp = jax.ShapeDtypeStruct((t, f), BF16)
    return pl.pallas_call(
        body, name=name, grid=(t // tm, f // tf),
        in_specs=[pl.BlockSpec((tm, d), lambda i, j: (i, 0)), pl.BlockSpec((tf, d), lambda i, j: (j, 0)), blk, blk],
        out_specs=[blk, blk], out_shape=[shp, shp], compiler_params=_cp(("parallel", "parallel")))(dy, w_down, gate, up)


def _adamw_math(w, g, m, v):
    m = ADAM_B1 * m + (1.0 - ADAM_B1) * g
    v = ADAM_B2 * v + (1.0 - ADAM_B2) * (g * g)
    m_hat = m / (1.0 - ADAM_B1 ** ADAM_STEP)
    v_hat = v / (1.0 - ADAM_B2 ** ADAM_STEP)
    delta = -ADAM_LR * (m_hat / (jnp.sqrt(v_hat) + ADAM_EPS) + ADAM_WD * w)
    return delta, m, v


def _adamw(name, w, g, m, v):
    r, c = w.shape
    tr = _tile8(r, 512 if c <= 1024 else 256)

    def body(w_ref, g_ref, m_ref, v_ref, go_ref, d_ref, nm_ref, nv_ref):
        g = g_ref[...]
        go_ref[...] = g
        d_ref[...], nm_ref[...], nv_ref[...] = _adamw_math(w_ref[...], g, m_ref[...], v_ref[...])

    blk = pl.BlockSpec((tr, c), lambda i: (i, 0))
    shp = jax.ShapeDtypeStruct((r, c), F32)
    return pl.pallas_call(body, name=name, grid=(r // tr,), in_specs=[blk] * 4, out_specs=[blk] * 4,
                          out_shape=[shp] * 4, compiler_params=_cp(("parallel",)))(w, g, m, v)


class _Weight:
    def __init__(self, name, rows, cols, colshard):
        self.name, self.colshard = name, colshard
        self.R, self.nn = rows // 2, cols
        self.P = 1 if colshard else N_CHIPS
        self.N = N_CHIPS * cols if colshard else cols

    def cols(self, k):
        return pl.ds(pl.multiple_of(k * self.nn, LANES), self.nn)

    def shard(self, ref, k):
        return ref.at[0, :, :, self.cols(k)] if self.colshard else ref.at[k]

    def half(self, ref, k, h):
        return ref.at[0, h, :, self.cols(k)] if self.colshard else ref.at[k, h]

    def quarter(self, ref, k, h, q):
        return self.half(ref, k, h).at[pl.ds(q * (self.R // 2), self.R // 2), :]

    def part(self, ref, k):
        return ref.at[0, :, self.cols(k)] if self.colshard else ref.at[k]


def _remote(src, dst, ssem, rsem, dev):
    return pltpu.make_async_remote_copy(src_ref=src, dst_ref=dst, send_sem=ssem, recv_sem=rsem, device_id=dev,
                                        device_id_type=MESH)


def _other_chips(x, y):
    chips = [(1 - x, y), (x, 1 - y), (1 - x, 1 - y)]
    return chips, [2 * cx + cy for cx, cy in chips]


def _hbm(a):
    return pltpu.with_memory_space_constraint(a, pltpu.HBM)


def _gather_start(name, groups, lands, after=()):
    flat = [w for grp in groups for w in grp]
    nw, ng = len(flat), len(groups)

    def body(*refs):
        land = refs[:nw]
        sems = refs[nw + len(after):nw + len(after) + 2 * ng]
        token = refs[2 * nw + len(after) + 2 * ng]
        x, y, c = _mesh_pos()
        k_me = 2 * x + y
        chips, _ = _other_chips(x, y)
        i = 0
        for g, grp in enumerate(groups):
            for wi, w in enumerate(grp):
                mine = w.half(land[i], k_me, c)
                for j, chip in enumerate(chips):
                    _remote(mine, mine, sems[2 * g].at[3 * wi + j], sems[2 * g + 1].at[3 * wi + j], (*chip, c)).start()
                i += 1
        token[...] = jnp.zeros_like(token)

    sem_shapes = []
    for grp in groups:
        sem_shapes += [pltpu.SemaphoreType.DMA((3 * len(grp),))] * 2
    out = pl.pallas_call(
        body, name=name, in_specs=[HBM] * nw + [ANY] * len(after),
        out_specs=[SEM] * (2 * ng) + [HBM] * nw + [VMEM],
        out_shape=sem_shapes + [pltpu.HBM(a.shape, a.dtype) for a in lands] + [jax.ShapeDtypeStruct((8, LANES), F32)],
        input_output_aliases={i: 2 * ng + i for i in range(nw)},
        compiler_params=pltpu.CompilerParams(has_side_effects=EFFECT),
    )(*[_hbm(a) for a in lands], *after)
    sems = [(out[2 * g], out[2 * g + 1]) for g in range(ng)]
    return sems, list(out[2 * ng:2 * ng + nw]), out[-1]


def _gather_wait(name, grp, lands, ssem, rsem, after):
    n = len(grp)

    def body(*refs):
        land, ssem_ref, rsem_ref = refs[:n], refs[n], refs[n + 1]
        x, y, c = _mesh_pos()
        k_me = 2 * x + y
        chips, ks = _other_chips(x, y)
        for wi, w in enumerate(grp):
            for j, chip in enumerate(chips):
                cp = _remote(w.half(land[wi], k_me, c), w.half(land[wi], ks[j], c), ssem_ref.at[3 * wi + j],
                             rsem_ref.at[3 * wi + j], (*chip, c))
                cp.wait_send()
                cp.wait_recv()

    return pl.pallas_call(
        body, name=name, in_specs=[HBM] * n + [SEM, SEM, ANY], out_specs=[HBM] * n,
        out_shape=[pltpu.HBM(a.shape, a.dtype) for a in lands], input_output_aliases={i: i for i in range(n)},
        compiler_params=pltpu.CompilerParams(has_side_effects=EFFECT),
    )(*lands, ssem, rsem, after)


def _split_start(name, arrays, n, copies, after=()):
    na = len(arrays)

    def body(*refs):
        ssem, rsem, token = refs[na + len(after):][0], refs[na + len(after):][1], refs[2 * na + len(after) + 2]
        for i, (src, dst, dev, _) in enumerate(copies(refs[:na], *_mesh_pos())):
            _remote(src, dst, ssem.at[i], rsem.at[i], dev).start()
        token[...] = jnp.zeros_like(token)

    out = pl.pallas_call(
        body, name=name, in_specs=[HBM] * na + [ANY] * len(after), out_specs=[SEM, SEM] + [HBM] * na + [VMEM],
        out_shape=[pltpu.SemaphoreType.DMA((n,))] * 2 + [pltpu.HBM(a.shape, a.dtype) for a in arrays]
        + [jax.ShapeDtypeStruct((8, LANES), F32)],
        input_output_aliases={i: 2 + i for i in range(na)},
        compiler_params=pltpu.CompilerParams(has_side_effects=EFFECT),
    )(*[_hbm(a) for a in arrays], *after)
    return out[0], out[1], list(out[2:2 + na]), out[-1]


def _split_wait(name, arrays, ssem, rsem, copies, after):
    na = len(arrays)

    def body(*refs):
        for i, (src, _, dev, dst) in enumerate(copies(refs[:na], *_mesh_pos())):
            cp = _remote(src, dst, refs[na].at[i], refs[na + 1].at[i], dev)
            cp.wait_send()
            cp.wait_recv()

    return list(pl.pallas_call(
        body, name=name, in_specs=[HBM] * na + [SEM, SEM] + [ANY] * len(after), out_specs=[HBM] * na,
        out_shape=[pltpu.HBM(a.shape, a.dtype) for a in arrays], input_output_aliases={i: i for i in range(na)},
        compiler_params=pltpu.CompilerParams(has_side_effects=EFFECT),
    )(*arrays, ssem, rsem, *after))


def _pass_copies(grp, rels=(0, 1, 2)):
    def copies(land, x, y, c):
        _, ks = _other_chips(x, y)
        return [(w.half(land[wi], ks[j], c), w.half(land[wi], ks[j], c), (x, y, 1 - c), w.half(land[wi], ks[j], 1 - c))
                for wi, w in enumerate(grp) for j in rels]
    copies.n = len(grp) * len(rels)
    return copies


def _near_copies(grp):
    def copies(land, x, y, c):
        chips, ks = _other_chips(x, y)
        out = []
        for wi, w in enumerate(grp):
            mine = w.half(land[wi], 2 * x + y, c)
            out += [(mine, mine, (*chips[j], c), w.half(land[wi], ks[j], c)) for j in (0, 1)]
        return out
    copies.n = 2 * len(grp)
    return copies


def _far_copies(grp):
    def copies(land, x, y, c):
        chips, ks = _other_chips(x, y)
        out = []
        for wi, w in enumerate(grp):
            for j in (0, 1):
                q = w.quarter(land[wi], ks[j], c, j)
                out.append((q, q, (*chips[1 - j], c), w.quarter(land[wi], ks[2], c, j)))
        return out
    copies.n = 2 * len(grp)
    return copies


def _pair_copies(n, whole=False):
    def copies(refs, x, y, c):
        return [(refs[i] if whole else refs[i].at[:, 1 - c], refs[n + i], (x, y, 1 - c), refs[n + i]) for i in range(n)]
    return copies


def _share_copies(n):
    def copies(refs, x, y, c):
        return [(refs[i].at[c], refs[i].at[c], (x, y, 1 - c), refs[i].at[1 - c]) for i in range(n)]
    return copies


def _gather_conv_w(cw):
    ncw = cw.shape[1]

    def body(cw_ref, out_ref, ssem, rsem):
        x, y, c = _mesh_pos()
        k_me = 2 * x + y
        chips, ks = _other_chips(x, y)
        cols = lambda k: out_ref.at[:, pl.ds(pl.multiple_of(k * ncw, LANES), ncw)]
        cps = [_remote(cw_ref, cols(k_me), ssem.at[j], rsem.at[j], (*chip, c)) for j, chip in enumerate(chips)]
        for cp in cps:
            cp.start()
        for k in range(N_CHIPS):
            @pl.when(k_me == k)
            def _():
                out_ref[:, k * ncw:(k + 1) * ncw] = cw_ref[...]
        for j in range(3):
            _remote(cw_ref, cols(ks[j]), ssem.at[j], rsem.at[j], (*chips[j], c)).wait_recv()
        for cp in cps:
            cp.wait_send()

    return pl.pallas_call(
        body, name="gather_conv_w", in_specs=[VMEM], out_specs=VMEM,
        out_shape=jax.ShapeDtypeStruct((3, N_CHIPS * ncw), F32),
        scratch_shapes=[pltpu.SemaphoreType.DMA((3,)), pltpu.SemaphoreType.DMA((3,))],
    )(cw)


def _grad_tiles(w, n):
    return _tile8(w.R, 512) if w.R <= 512 else w.R // 2, _tile(n, 2048)


def _pair_sum(name, w, pos, grad, got):
    tr, tn = _grad_tiles(w, w.N)

    def body(pos_ref, g_ref, r_ref, o_ref):
        o_ref[...] = (g_ref[...].astype(F32) + r_ref[...].astype(F32)).astype(BF16)

    blk = pl.BlockSpec((None, tr, tn), lambda p, i, j, pos: (p, i, j))
    grid_spec = pltpu.PrefetchScalarGridSpec(
        num_scalar_prefetch=1, grid=(w.P, w.R // tr, w.N // tn),
        in_specs=[pl.BlockSpec((None, None, tr, tn), lambda p, i, j, pos: (p, pos[0], i, j)), blk], out_specs=blk)
    return pl.pallas_call(body, name=name, grid_spec=grid_spec, out_shape=jax.ShapeDtypeStruct((w.P, w.R, w.N), BF16),
                          compiler_params=_cp(("parallel",) * 3))(pos, grad, got)


def _scatter_start(name, ws, pairs):
    nw = len(ws)

    def body(*refs):
        pr, land = refs[:nw], refs[nw:2 * nw]
        ssem, rsem = refs[2 * nw], refs[2 * nw + 1]
        token = refs[4 * nw + 2]
        x, y, c = _mesh_pos()
        chips, ks = _other_chips(x, y)
        for i, w in enumerate(ws):
            for j, chip in enumerate(chips):
                _remote(w.part(pr[i], ks[j]), land[i].at[j], ssem.at[3 * i + j], rsem.at[3 * i + j], (*chip, c)).start()
        token[...] = jnp.zeros_like(token)

    lands = [lax.empty((3, w.R, w.nn), BF16) for w in ws]
    out = pl.pallas_call(
        body, name=name, in_specs=[HBM] * (2 * nw),
        out_specs=[SEM, SEM] + [HBM] * (2 * nw) + [VMEM],
        out_shape=[pltpu.SemaphoreType.DMA((3 * nw,))] * 2 + [pltpu.HBM(a.shape, a.dtype) for a in pairs + lands]
        + [jax.ShapeDtypeStruct((8, LANES), F32)],
        input_output_aliases={i: 2 + i for i in range(2 * nw)},
        compiler_params=pltpu.CompilerParams(has_side_effects=EFFECT),
    )(*[_hbm(a) for a in pairs + lands])
    return out[0], out[1], list(out[2:2 + nw]), list(out[2 + nw:2 + 2 * nw]), out[-1]


def _scatter_wait(name, ws, pairs, lands, ssem, rsem, after):
    nw = len(ws)

    def body(*refs):
        pr, land = refs[:nw], refs[nw:2 * nw]
        ssem_ref, rsem_ref = refs[2 * nw], refs[2 * nw + 1]
        x, y, c = _mesh_pos()
        chips, ks = _other_chips(x, y)
        for i, w in enumerate(ws):
            for j, chip in enumerate(chips):
                cp = _remote(w.part(pr[i], ks[j]), land[i].at[j], ssem_ref.at[3 * i + j], rsem_ref.at[3 * i + j], (*chip, c))
                cp.wait_send()
                cp.wait_recv()

    out = pl.pallas_call(
        body, name=name, in_specs=[HBM] * (2 * nw) + [SEM, SEM] + [ANY] * len(after), out_specs=[HBM] * (2 * nw),
        out_shape=[pltpu.HBM(a.shape, a.dtype) for a in pairs + lands],
        input_output_aliases={i: i for i in range(2 * nw)},
        compiler_params=pltpu.CompilerParams(has_side_effects=EFFECT),
    )(*pairs, *lands, ssem, rsem, *after)
    return list(out[:nw]), list(out[nw:])


def _final_sum(name, w, pos, grad, got, parts):
    tr, tn = _grad_tiles(w, w.nn)
    nbc = w.nn // tn
    if got is None:
        return _final_sum_pair(name, w, pos, grad, parts, tr, tn)

    def body(pos_ref, g_ref, r_ref, p_ref, o_ref):
        acc = g_ref[...].astype(F32) + r_ref[...].astype(F32)
        for j in range(3):
            acc = acc + p_ref[j].astype(F32)
        o_ref[...] = acc

    if w.colshard:
        g_spec = pl.BlockSpec((None, None, tr, tn), lambda i, j, pos: (0, pos[0], i, pos[1] * nbc + j))
        r_spec = pl.BlockSpec((None, tr, tn), lambda i, j, pos: (0, i, pos[1] * nbc + j))
    else:
        g_spec = pl.BlockSpec((None, None, tr, tn), lambda i, j, pos: (pos[1], pos[0], i, j))
        r_spec = pl.BlockSpec((None, tr, tn), lambda i, j, pos: (pos[1], i, j))
    grid_spec = pltpu.PrefetchScalarGridSpec(
        num_scalar_prefetch=1, grid=(w.R // tr, nbc),
        in_specs=[g_spec, r_spec, pl.BlockSpec((3, tr, tn), lambda i, j, pos: (0, i, j))],
        out_specs=pl.BlockSpec((None, tr, tn), lambda i, j, pos: (pos[0], i, j)))
    return pl.pallas_call(body, name=name, grid_spec=grid_spec, out_shape=jax.ShapeDtypeStruct((2, w.R, w.nn), F32),
                          compiler_params=_cp(("parallel",) * 2))(pos, grad, got, parts)


def _final_sum_pair(name, w, pos, pair, parts, tr, tn):
    nbc = w.nn // tn

    def body(pos_ref, g_ref, p_ref, o_ref):
        acc = g_ref[...].astype(F32)
        for j in range(3):
            acc = acc + p_ref[j].astype(F32)
        o_ref[...] = acc

    if w.colshard:
        g_spec = pl.BlockSpec((None, tr, tn), lambda i, j, pos: (0, i, pos[1] * nbc + j))
    else:
        g_spec = pl.BlockSpec((None, tr, tn), lambda i, j, pos: (pos[1], i, j))
    grid_spec = pltpu.PrefetchScalarGridSpec(
        num_scalar_prefetch=1, grid=(w.R // tr, nbc),
        in_specs=[g_spec, pl.BlockSpec((3, tr, tn), lambda i, j, pos: (0, i, j))],
        out_specs=pl.BlockSpec((None, tr, tn), lambda i, j, pos: (pos[0], i, j)))
    return pl.pallas_call(body, name=name, grid_spec=grid_spec, out_shape=jax.ShapeDtypeStruct((2, w.R, w.nn), F32),
                          compiler_params=_cp(("parallel",) * 2))(pos, pair, parts)


def _share_halves(name, ws, halves, deps=()):
    nw = len(ws)

    def body(*refs):
        out = refs[nw + len(deps):2 * nw + len(deps)]
        ssem, rsem = refs[2 * nw + len(deps):]
        x, y, c = _mesh_pos()
        sib = (x, y, 1 - c)
        cps = [_remote(out[i].at[c], out[i].at[c], ssem.at[i], rsem.at[i], sib) for i in range(nw)]
        for cp in cps:
            cp.start()
        for i, cp in enumerate(cps):
            cp.wait_send()
            _remote(out[i].at[1 - c], out[i].at[1 - c], ssem.at[i], rsem.at[i], sib).wait_recv()

    return pl.pallas_call(
        body, name=name, in_specs=[ANY] * (nw + len(deps)), out_specs=[ANY] * nw,
        out_shape=[jax.ShapeDtypeStruct(h.shape, F32) for h in halves],
        scratch_shapes=[pltpu.SemaphoreType.DMA((nw,)), pltpu.SemaphoreType.DMA((nw,))],
        input_output_aliases={i: i for i in range(nw)},
    )(*halves, *deps)


VEC_ROWS = 16


def _vector_step(d, n_conv, parts, params, deps=()):
    ncw = params[2][0].shape[1]
    n_par = len(params)

    def body(*refs):
        dg1, dba, dbb, dcw, dcb, dps, dg2, dgf, lc = refs[:9]
        wmv = refs[9:9 + 3 * n_par]
        refs = refs[9 + 3 * n_par + len(deps):]
        outs = refs[:4 * n_par]
        loss_ref = refs[4 * n_par]
        snd, got, ssem, rsem = refs[4 * n_par + 1:]
        x, y, c = _mesh_pos()
        me = 4 * x + 2 * y + c
        snd[...] = jnp.zeros_like(snd)
        for row, ref in ((0, dg1), (1, dba), (2, dbb), (3, dps), (4, dg2), (5, dgf), (6, lc)):
            snd[row:row + 1, :] = ref[...]
        snd[7:8, :n_conv] = dcb[...]
        snd[8:11, :n_conv] = dcw[...]
        cps = []
        for r in range(1, N_DEV):
            peer = tuple(1 - p if (r >> b) & 1 else p for p, b in ((x, 2), (y, 1), (c, 0)))
            cps.append(_remote(snd, got.at[me], ssem.at[r - 1], rsem.at[r - 1], peer))
        for cp in cps:
            cp.start()
        got[me] = snd[...]
        for r in range(1, N_DEV):
            peer = tuple(1 - p if (r >> b) & 1 else p for p, b in ((x, 2), (y, 1), (c, 0)))
            _remote(snd, got.at[4 * peer[0] + 2 * peer[1] + peer[2]], ssem.at[r - 1], rsem.at[r - 1], peer).wait_recv()
        for cp in cps:
            cp.wait_send()
        tot = got[0]
        for dev in range(1, N_DEV):
            tot = tot + got[dev]
        loss_ref[...] = jnp.sum(tot[6:7, :], axis=1, keepdims=True)
        k_me = 2 * x + y
        g_cw = jnp.zeros((3, ncw), F32)
        for k in range(N_CHIPS):
            g_cw = g_cw + jnp.where(k_me == k, tot[8:11, k * ncw:(k + 1) * ncw], 0.0)
        grads = [tot[0:1, :], jnp.concatenate([tot[1:2, :], tot[2:3, :]], axis=1), g_cw, tot[7:8, :n_conv],
                 tot[3:4, :], tot[4:5, :], tot[5:6, :]]
        for i, g in enumerate(grads):
            w_ref, m_ref, v_ref = wmv[3 * i:3 * i + 3]
            delta, nm, nv = _adamw_math(w_ref[...], g, m_ref[...], v_ref[...])
            outs[4 * i][...] = g
            outs[4 * i + 1][...] = delta
            outs[4 * i + 2][...] = nm
            outs[4 * i + 3][...] = nv

    args = list(parts)
    out_shape = []
    for w, m, v in params:
        args += [w, m, v]
        out_shape += [jax.ShapeDtypeStruct(w.shape, F32)] * 4
    out_shape.append(jax.ShapeDtypeStruct((1, 1), F32))
    return pl.pallas_call(
        body, name="vector_params_step", in_specs=[VMEM] * len(args) + [ANY] * len(deps),
        out_specs=[VMEM] * len(out_shape), out_shape=out_shape,
        scratch_shapes=[pltpu.VMEM((VEC_ROWS, d), F32), pltpu.VMEM((N_DEV, VEC_ROWS, d), F32),
                        pltpu.SemaphoreType.DMA((N_DEV - 1,)), pltpu.SemaphoreType.DMA((N_DEV - 1,))],
        compiler_params=pltpu.CompilerParams(vmem_limit_bytes=VMEM_LIMIT),
    )(*args, *deps)


def kernel(x, norm1_g, w_in, b_gate, conv_w, conv_b, w_a_out, w_pool, pool_scale, w_o, norm2_g, w_ffn_gate, w_ffn_up, w_ffn_down, final_g, loss_target, m_norm1_g, m_w_in, m_b_gate, m_conv_w, m_conv_b, m_w_a_out, m_w_pool, m_pool_scale, m_w_o, m_norm2_g, m_w_ffn_gate, m_w_ffn_up, m_w_ffn_down, m_final_g, v_norm1_g, v_w_in, v_b_gate, v_conv_w, v_conv_b, v_w_a_out, v_w_pool, v_pool_scale, v_w_o, v_norm2_g, v_w_ffn_gate, v_w_ffn_up, v_w_ffn_down, v_final_g):
    t, d = x.shape[1], x.shape[2]
    n_conv = conv_b.shape[1]
    n_groups, pool_cg, pool_dg = w_pool.shape[1], w_pool.shape[2], N_CHIPS * w_pool.shape[3]
    d_ff = N_CHIPS * w_ffn_gate.shape[2]
    assert n_conv // n_groups == pool_cg and n_conv % (n_groups * MIX_COLS) == 0 and n_groups == len(POOL_WINDOWS)

    big = {"w_in": (w_in, m_w_in, v_w_in), "w_a_out": (w_a_out, m_w_a_out, v_w_a_out), "w_pool": (w_pool, m_w_pool, v_w_pool),
           "w_o": (w_o, m_w_o, v_w_o), "w_ffn_gate": (w_ffn_gate, m_w_ffn_gate, v_w_ffn_gate),
           "w_ffn_up": (w_ffn_up, m_w_ffn_up, v_w_ffn_up), "w_ffn_down": (w_ffn_down, m_w_ffn_down, v_w_ffn_down)}
    colshard = {"w_in": True, "w_a_out": True, "w_pool": True, "w_o": False, "w_ffn_gate": True, "w_ffn_up": True,
                "w_ffn_down": False}
    names = list(big)
    shard2d = {n: big[n][0].reshape(-1, big[n][0].shape[-1]) for n in names}
    ws = [_Weight(n, *shard2d[n].shape, colshard[n]) for n in names]

    xs, tgt = x[0], loss_target[0]
    cw_loc = conv_w[0]
    pos = jnp.stack([lax.axis_index("c"), 2 * lax.axis_index("x") + lax.axis_index("y")]).astype(jnp.int32)
    by_name = {w.name: w for w in ws}
    groups = [[by_name[n] for n in g] for g in (["w_in"], ["w_a_out", "w_pool", "w_o"], ["w_ffn_gate"], ["w_ffn_up"],
                                                 ["w_ffn_down"])]
    first = [sum(len(g) for g in groups[:i]) for i in range(len(groups))]
    rgroups = [groups[0], groups[1], groups[2] + groups[3], groups[4]]

    cw_full = _gather_conv_w(cw_loc)
    cast = lambda w, dep: _cast_place(f"cast_{w.name}", w, pos, shard2d[w.name].reshape(2, w.R, w.nn), deps=[dep])
    chips, ks = _other_chips(lax.axis_index("x"), lax.axis_index("y"))
    kvec = jnp.stack([pos[1], *ks]).astype(jnp.int32)
    full = {}

    def start(name, arrays, copies, after=()):
        ssem, rsem, arrays, token = _split_start(name, arrays, copies.n, copies, after)
        return name, arrays, ssem, rsem, copies, token

    def wait(started, after):
        name, arrays, ssem, rsem, copies, _ = started
        return _split_wait(name + "_wait", arrays, ssem, rsem, copies, after)

    def pass_on(g, got, after=()):
        return start(f"pass_{g}", got, _pass_copies(groups[g]), after)

    def passed(g, st, after=None):
        got = wait(st, [st[5]] if after is None else after)
        full.update({w.name: a.reshape(w.P * 2 * w.R, w.N) for w, a in zip(groups[g], got)})

    near = start("near_0", [cast(w, cw_full) for w in groups[0]], _near_copies(groups[0]))
    rest = [cast(w, near[5]) for grp in groups[1:] for w in grp]
    h1 = _rms_fwd("norm1_fwd", xs, norm1_g, deps=[near[5]])
    proj = _proj_piece("proj_own", h1, shard2d["w_in"], None, kvec, 0, 1, deps=rest)
    got = wait(near, [proj])
    far = start("far_0", got, _far_copies(groups[0]))
    sems_b, lands_b, tok_b = _gather_start("gather_start_b", groups[1:2], rest[:3], after=[far[5]])
    st = start("pass_near_0", far[1], _pass_copies(groups[0], (0, 1)), [tok_b])
    got = wait(st, [st[5]])
    proj = _proj_piece("proj_near", h1, got[0].reshape(-1, groups[0][0].N), proj, kvec, 1, 2)
    st = start("pass_far_0", wait((far[0], got) + far[2:], [proj]), _pass_copies(groups[0], (2,)))
    got = wait(st, [st[5]])
    w_in_full = got[0].reshape(-1, groups[0][0].N)
    proj = _proj_piece("proj_far", h1, w_in_full, proj, kvec, 3, 1)
    got = _gather_wait("gather_wait_1", groups[1], lands_b, *sems_b[0], proj)
    near_g = start("near_2", rest[3:4], _near_copies(groups[2]), got)
    st = pass_on(1, got, [near_g[5]])
    z, p = _mixer_fwd("mixer_fwd", proj, cw_full, conv_b, n_conv, n_groups, deps=[st[5]])
    passed(1, st, [z])
    wp_full = full["w_pool"].reshape(n_groups, pool_cg, pool_dg)
    ya = _mm_nn("conv_out", z, full["w_a_out"], BF16)
    yb = _gmm_nn("pool_out", p, wp_full, BF16)
    merged = _merge_fwd("merge_fwd", proj, b_gate, ya, yb, pool_scale)
    far_g = start("far_2", wait(near_g, [merged]), _far_copies(groups[2]))
    near_u = start("near_3", rest[4:5], _near_copies(groups[3]), [far_g[5]])
    x2 = _mm_nn("mix_out", merged, full["w_o"], F32, add=xs, deps=[near_u[5]])
    st = pass_on(2, wait(far_g, [x2]))
    h2 = _rms_fwd("norm2_fwd", x2, norm2_g, deps=[st[5]])
    passed(2, st, [h2])
    gate = _mm_nn("ffn_gate_a", h2, full["w_ffn_gate"], BF16, part=(0, 2))
    far_u = start("far_3", wait(near_u, [gate]), _far_copies(groups[3]))
    near_d = start("near_4", rest[5:6], _near_copies(groups[4]), [far_u[5]])
    gate = _mm_nn("ffn_gate_b", h2, full["w_ffn_gate"], BF16, part=(1, 2), prev=gate, deps=[near_d[5]])
    passed(3, pass_on(3, wait(far_u, [gate])))
    up_act = _ffn_up_act("ffn_up_act_a", h2, full["w_ffn_up"], gate, part=(0, 2))
    far_d = start("far_4", wait(near_d, [up_act[0]]), _far_copies(groups[4]))
    up, act = _ffn_up_act("ffn_up_act_b", h2, full["w_ffn_up"], gate, part=(1, 2), prev=up_act, deps=[far_d[5]])
    passed(4, pass_on(4, wait(far_d, [act])))
    x3 = _mm_nn("ffn_down", act, full["w_ffn_down"], F32, add=x2, tk=d_ff // 4)

    pending = {}

    def pair_start(g, grads):
        grp = rgroups[g]
        gcan = [grads[w.name].reshape(w.P, 2, w.R, w.N) for w in grp]
        slots = [lax.empty((w.P, w.R, w.N), BF16) for w in grp]
        pending[g] = _split_start(f"pair_start_{g}", gcan + slots, len(grp), _pair_copies(len(grp)))
        return pending[g][3]

    def scatter_start(g, after):
        grp = rgroups[g]
        n = len(grp)
        ssem, rsem, arrs, _ = pending[g]
        arrs = _split_wait(f"pair_wait_{g}", arrs, ssem, rsem, _pair_copies(n), after)
        gcan, sib = arrs[:n], arrs[n:]
        pairs = [_pair_sum(f"pair_sum_{w.name}", w, pos, a, s) for w, a, s in zip(grp, gcan, sib)]
        ssem, rsem, pairs, slots, token = _scatter_start(f"scatter_start_{g}", grp, pairs)
        pending[g] = (gcan, sib, pairs, slots, ssem, rsem)
        return token

    def pair_start_halves(g, ab, deps):
        grp = rgroups[g]
        sent = [_mm_tn_half(f"d{w.name}_sib", a, b, pos, False, deps=deps if i == 0 else ()) for i, (w, (a, b)) in enumerate(zip(grp, ab))]
        slots = [lax.empty((1, w.R, w.N), BF16) for w in grp]
        pending[g] = _split_start(f"pair_start_{g}", sent + slots, len(grp), _pair_copies(len(grp), whole=True))
        return pending[g][3]

    def scatter_start_halves(g, ab, after):
        grp = rgroups[g]
        n = len(grp)
        ssem, rsem, arrs, _ = pending[g]
        arrs = _split_wait(f"pair_wait_{g}", arrs, ssem, rsem, _pair_copies(n, whole=True), after)
        pairs = [_mm_tn_half(f"d{w.name}_own", a, b, pos, True, add=s) for w, (a, b), s in zip(grp, ab, arrs[n:])]
        ssem, rsem, pairs, slots, token = _scatter_start(f"scatter_start_{g}", grp, pairs)
        pending[g] = (None, None, pairs, slots, ssem, rsem)
        return token

    def reduce_finish(g, after):
        grp = rgroups[g]
        gcan, sib, pairs, slots, ssem, rsem = pending[g]
        pairs, parts = _scatter_wait(f"scatter_wait_{g}", grp, pairs, slots, ssem, rsem, after)
        if gcan is None:
            return [_final_sum(f"final_sum_{w.name}", w, pos, a, None, q) for w, a, q in zip(grp, pairs, parts)]
        return [_final_sum(f"final_sum_{w.name}", w, pos, a, s, q) for w, a, s, q in zip(grp, gcan, sib, parts)]

    grads = {}
    dx3, dx3b, d_gf, loss_cols = _final_bwd("final_bwd", x3, final_g.reshape(1, d), tgt)
    dgate, dup = _ffn_bwd("ffn_bwd", dx3b, full["w_ffn_down"], gate, up)
    grads["w_ffn_down"] = _mm_tn("dw_ffn_down", act, dx3b, BF16)
    tok = pair_start(3, grads)
    dh2 = _mm_nt("d_h2", [(dgate, full["w_ffn_gate"]), (dup, full["w_ffn_up"])], F32, tk=d_ff // 4, deps=[tok])
    tok = scatter_start(3, [dh2])
    tok = pair_start_halves(2, [(h2, dgate), (h2, dup)], [tok])
    dx2, dx2b, d_g2 = _rms_bwd("norm2_bwd", x2, norm2_g, dh2, dx3, True, deps=[tok])
    dmerged = _mm_nt("d_merged", [(dx2b, full["w_o"])], BF16, tk=d)
    grads["w_o"] = _mm_tn("dw_o", merged, dx2b, BF16)
    tok = scatter_start_halves(2, [(h2, dgate), (h2, dup)], [grads["w_o"]])
    dya, dyb, dproj, d_bga, d_bgb, d_ps = _merge_bwd("merge_bwd", dmerged, proj, b_gate, ya, yb, pool_scale, deps=[tok])
    dz = _mm_nt("d_z", [(dya, full["w_a_out"])], BF16, tk=d)
    grads["w_a_out"] = _mm_tn("dw_a_out", z, dya, BF16)
    dp = _gmm_nt("d_pool", dyb, wp_full, BF16)
    grads["w_pool"] = _gmm_tn("dw_pool", p, dyb, n_groups, BF16)
    tok = pair_start(1, grads)
    dproj, d_cw, d_cb = _mixer_bwd("mixer_bwd", dz, dp, proj, cw_full, conv_b, dproj, n_conv, n_groups, deps=[tok])
    tok = scatter_start(1, [dproj])
    tok = pair_start_halves(0, [(h1, dproj)], [tok])
    dh1 = _mm_nt("d_h1", [(dproj, w_in_full)], F32, tk=proj.shape[1] // 4, deps=[tok])
    tok = scatter_start_halves(0, [(h1, dproj)], [dh1])
    grad_x, d_g1 = _rms_bwd("norm1_bwd", xs, norm1_g, dh1, dx2, False, deps=[tok])

    g_big, d_big, m_big, v_big = {}, {}, {}, {}

    def update(wsub, shared):
        out = []
        for w, g in zip(wsub, shared):
            wt, mt, vt = big[w.name]
            g2 = g.reshape(2 * w.R, w.nn)
            go, dl, nm, nv = _adamw(f"adamw_{w.name}", shard2d[w.name], g2, mt.reshape(g2.shape), vt.reshape(g2.shape))
            g_big[w.name], d_big[w.name], m_big[w.name], v_big[w.name] = (a.reshape(wt.shape) for a in (go, dl, nm, nv))
            out.append(nv)
        return out

    after = [grad_x]
    started = []
    for g in (3, 2, 1):
        halves = reduce_finish(g, after)
        share = _share_copies(len(halves))
        ssem, rsem, halves, tok = _split_start(f"share_start_{g}", halves, len(halves), share)
        started.append((g, ssem, rsem, halves, share))
        after = [tok]
    for g, ssem, rsem, halves, share in started:
        after = update(rgroups[g], _split_wait(f"share_wait_{g}", halves, ssem, rsem, share, after))
    after = update(rgroups[0], _share_halves("share_halves_w_in", rgroups[0], reduce_finish(0, after)))

    vec_names = ["norm1_g", "b_gate", "conv_w", "conv_b", "pool_scale", "norm2_g", "final_g"]
    vec = {"norm1_g": (norm1_g, m_norm1_g, v_norm1_g), "b_gate": (b_gate, m_b_gate, v_b_gate),
           "conv_w": (cw_loc, m_conv_w[0], v_conv_w[0]), "conv_b": (conv_b, m_conv_b, v_conv_b),
           "pool_scale": (pool_scale, m_pool_scale, v_pool_scale), "norm2_g": (norm2_g, m_norm2_g, v_norm2_g),
           "final_g": tuple(a.reshape(1, d) for a in (final_g, m_final_g, v_final_g))}
    vout = _vector_step(d, n_conv, [d_g1, d_bga, d_bgb, d_cw, d_cb, d_ps, d_g2, d_gf, loss_cols],
                        [vec[n] for n in vec_names], deps=after)

    shapes = {"conv_w": conv_w.shape, "final_g": final_g.shape}
    g_vec, d_vec, m_vec, v_vec = ({n: vout[4 * i + q].reshape(shapes.get(n, vec[n][0].shape)) for i, n in enumerate(vec_names)}
                                  for q in range(4))
    loss = vout[-1].reshape(())

    order = ["norm1_g", "w_in", "b_gate", "conv_w", "conv_b", "w_a_out", "w_pool", "pool_scale", "w_o", "norm2_g",
             "w_ffn_gate", "w_ffn_up", "w_ffn_down", "final_g"]
    pick = lambda vecs, bigs: [vecs[n] if n in vecs else bigs[n] for n in order]
    return (loss, grad_x.reshape(x.shape), *pick(g_vec, g_big), *pick(d_vec, d_big), *pick(m_vec, m_big),
            *pick(v_vec, v_big))
```

```python
import functools

import jax
import jax.numpy as jnp
from jax import lax
from jax.experimental import pallas as pl
from jax.experimental.pallas import tpu as pltpu

F32, BF16 = jnp.float32, jnp.bfloat16
MESH = pl.DeviceIdType.MESH
ANY = pl.BlockSpec(memory_space=pl.ANY)
VMEM = pl.BlockSpec(memory_space=pltpu.VMEM)
HBM = pl.BlockSpec(memory_space=pltpu.HBM)
SEM = pl.BlockSpec(memory_space=pltpu.SEMAPHORE)
EFFECT = pltpu.SideEffectType.DATAFLOW_SIDE_EFFECTING

EPS = 1e-6
POOL_WINDOWS = (2, 4, 8, 16)
ADAM_LR, ADAM_B1, ADAM_B2, ADAM_EPS, ADAM_WD, ADAM_STEP = 0.001, 0.9, 0.999, 1e-08, 0.01, 10

V7X_VMEM_BYTES = 64 * 1024 * 1024
VMEM_LIMIT = V7X_VMEM_BYTES * 3 // 4
LANES = 128
COL_TILE = 11 * LANES
N_CHIPS = 4
N_DEV = 8

_DIMS = {
    "nn": (((1,), (0,)), ((), ())),
    "nt": (((1,), (1,)), ((), ())),
    "tn": (((0,), (0,)), ((), ())),
}


def _cp(sem):
    return pltpu.CompilerParams(dimension_semantics=sem, vmem_limit_bytes=VMEM_LIMIT)


def _mesh_pos():
    return lax.axis_index("x"), lax.axis_index("y"), lax.axis_index("c")


def _mm(name, pairs, *, mode, grid, out_shape, o_spec, nk=1, kaxis=None, add=None, deps=(), prev=None):
    npair = len(pairs)
    has_add = add is not None

    def body(*refs):
        ab = refs[: 2 * npair]
        pos = 2 * npair
        add_ref = refs[pos] if has_add else None
        pos += int(has_add) + len(deps) + (prev is not None)
        o_ref = refs[pos]
        acc_ref = refs[pos + 1] if nk > 1 else None
        d = None
        for p in range(npair):
            t = lax.dot_general(ab[2 * p][...], ab[2 * p + 1][...], _DIMS[mode], preferred_element_type=F32)
            d = t if d is None else d + t
        if nk == 1:
            if has_add:
                d = d + add_ref[...].astype(F32)
            o_ref[...] = d.astype(o_ref.dtype)
        else:
            k = pl.program_id(kaxis)

            @pl.when(k == 0)
            def _():
                acc_ref[...] = d

            @pl.when(k > 0)
            def _():
                acc_ref[...] += d

            @pl.when(k == nk - 1)
            def _():
                r = acc_ref[...]
                if has_add:
                    r = r + add_ref[...].astype(F32)
                o_ref[...] = r.astype(o_ref.dtype)

    args, specs = [], []
    for a, a_spec, b, b_spec in pairs:
        args += [a, b]
        specs += [a_spec, b_spec]
    if has_add:
        args.append(add[0])
        specs.append(add[1])
    args += list(deps)
    specs += [ANY] * len(deps)
    aliases = {}
    if prev is not None:
        aliases = {len(args): 0}
        args.append(prev)
        specs.append(ANY)
    scratch = []
    if nk > 1:
        blk = [d for d in o_spec.block_shape if d is not None]
        scratch = [pltpu.VMEM(tuple(blk), F32)]
    sem = tuple("arbitrary" if (nk > 1 and ax == kaxis) else "parallel" for ax in range(len(grid)))
    return pl.pallas_call(
        body, name=name, grid=grid, in_specs=specs, out_specs=o_spec, out_shape=out_shape,
        scratch_shapes=scratch, input_output_aliases=aliases, compiler_params=_cp(sem),
    )(*args)


def _tile_span(n_tiles, part):
    if part is None:
        return 0, n_tiles
    p, of = part
    return p * n_tiles // of, (p + 1) * n_tiles // of


def _tile(n, pref):
    if n <= pref:
        return n
    for t in range(pref, 0, -LANES):
        if t % LANES == 0 and n % t == 0:
            return t
    raise ValueError(f"no tile for {n}")


def _mm_nn(name, a, b, out_dtype, add=None, tk=None, deps=(), part=None, prev=None):
    m, kk = a.shape
    n = b.shape[1]
    tm, tn = _tile(m, 1024), _tile(n, COL_TILE)
    out_shape = jax.ShapeDtypeStruct((m, n), out_dtype)
    if tk is None or tk == kk:
        j0, j1 = _tile_span(n // tn, part)
        grid = (m // tm, j1 - j0)
        pairs = [(a, pl.BlockSpec((tm, kk), lambda i, j: (i, 0)), b, pl.BlockSpec((kk, tn), lambda i, j: (0, j0 + j)))]
        o_spec = pl.BlockSpec((tm, tn), lambda i, j: (i, j0 + j))
        add_ = None if add is None else (add, pl.BlockSpec((tm, tn), lambda i, j: (i, j0 + j)))
        return _mm(name, pairs, mode="nn", grid=grid, out_shape=out_shape, o_spec=o_spec, add=add_, deps=deps, prev=prev)
    tn = _tile(n, 1024)
    nk = kk // tk
    grid = (m // tm, n // tn, nk)
    pairs = [(a, pl.BlockSpec((tm, tk), lambda i, j, k: (i, k)), b, pl.BlockSpec((tk, tn), lambda i, j, k: (k, j)))]
    o_spec = pl.BlockSpec((tm, tn), lambda i, j, k: (i, j))
    add_ = None if add is None else (add, pl.BlockSpec((tm, tn), lambda i, j, k: (i, j)))
    return _mm(name, pairs, mode="nn", grid=grid, out_shape=out_shape, o_spec=o_spec, nk=nk, kaxis=2, add=add_, deps=deps)


def _mm_nt(name, abs_, out_dtype, tk, deps=()):
    m, kk = abs_[0][0].shape
    n = abs_[0][1].shape[0]
    tm = _tile(m, 1024)
    nk = kk // tk
    tn = _tile(n, COL_TILE if nk == 1 else 1024)
    out_shape = jax.ShapeDtypeStruct((m, n), out_dtype)
    if nk == 1:
        grid = (m // tm, n // tn)
        pairs = [(a, pl.BlockSpec((tm, kk), lambda i, j: (i, 0)), b, pl.BlockSpec((tn, kk), lambda i, j: (j, 0)))
                 for a, b in abs_]
        o_spec = pl.BlockSpec((tm, tn), lambda i, j: (i, j))
        return _mm(name, pairs, mode="nt", grid=grid, out_shape=out_shape, o_spec=o_spec, deps=deps)
    grid = (m // tm, n // tn, nk)
    pairs = [(a, pl.BlockSpec((tm, tk), lambda i, j, k: (i, k)), b, pl.BlockSpec((tn, tk), lambda i, j, k: (j, k)))
             for a, b in abs_]
    o_spec = pl.BlockSpec((tm, tn), lambda i, j, k: (i, j))
    return _mm(name, pairs, mode="nt", grid=grid, out_shape=out_shape, o_spec=o_spec, nk=nk, kaxis=2, deps=deps)


def _mm_tn(name, a, b, out_dtype, deps=()):
    t, m = a.shape
    n = b.shape[1]
    tm, tn = _tile(m, 512), _tile(n, 2048)
    if n > m:
        grid = (n // tn, m // tm)
        a_map, b_map, o_map = (lambda j, i: (0, i)), (lambda j, i: (0, j)), (lambda j, i: (i, j))
    else:
        grid = (m // tm, n // tn)
        a_map, b_map, o_map = (lambda i, j: (0, i)), (lambda i, j: (0, j)), (lambda i, j: (i, j))
    pairs = [(a, pl.BlockSpec((t, tm), a_map), b, pl.BlockSpec((t, tn), b_map))]
    o_spec = pl.BlockSpec((tm, tn), o_map)
    return _mm(name, pairs, mode="tn", grid=grid, out_shape=jax.ShapeDtypeStruct((m, n), out_dtype), o_spec=o_spec,
               deps=deps)


def _mm_tn_half(name, a, b, pos, mine, add=None, deps=()):
    t, m = a.shape
    r, n = m // 2, b.shape[1]
    tm, tn = _tile(r, 512), _tile(n, 2048)
    nbi = r // tm
    half = (lambda pos: pos[0]) if mine else (lambda pos: 1 - pos[0])
    if n > r:
        grid, ij = (n // tn, nbi), (lambda g0, g1: (g1, g0))
    else:
        grid, ij = (nbi, n // tn), (lambda g0, g1: (g0, g1))
    has_add = add is not None

    def body(pos_ref, a_ref, b_ref, *rest):
        d = lax.dot_general(a_ref[...], b_ref[...], _DIMS["tn"], preferred_element_type=F32)
        if has_add:
            d = d + rest[0][...].astype(F32)
        rest[-1][...] = d.astype(BF16)

    o_spec = pl.BlockSpec((None, tm, tn), lambda g0, g1, pos: (0, *ij(g0, g1)))
    grid_spec = pltpu.PrefetchScalarGridSpec(
        num_scalar_prefetch=1, grid=grid,
        in_specs=[pl.BlockSpec((t, tm), lambda g0, g1, pos: (0, half(pos) * nbi + ij(g0, g1)[0])),
                  pl.BlockSpec((t, tn), lambda g0, g1, pos: (0, ij(g0, g1)[1]))]
        + ([o_spec] if has_add else []) + [ANY] * len(deps),
        out_specs=o_spec)
    return pl.pallas_call(body, name=name, grid_spec=grid_spec, out_shape=jax.ShapeDtypeStruct((1, r, n), BF16),
                          compiler_params=_cp(("parallel",) * 2))(pos, a, b, *([add] if has_add else []), *deps)


def _proj_piece(name, h, w, prev, kvec, base, count, deps=()):
    t, kk = h.shape
    own = w.dtype == F32
    nn = w.shape[1] if own else w.shape[1] // N_CHIPS
    tm, tn = _tile(t, 1024), _tile(nn, COL_TILE)
    nb = nn // tn

    def body(kv_ref, h_ref, w_ref, *rest):
        rest[-1][...] = lax.dot_general(h_ref[...], w_ref[...].astype(BF16), _DIMS["nn"],
                                        preferred_element_type=F32).astype(BF16)

    cols = lambda s, i, j, kv: (0, j) if own else (0, kv[base + s] * nb + j)
    extra = ([] if prev is None else [prev]) + list(deps)
    grid_spec = pltpu.PrefetchScalarGridSpec(
        num_scalar_prefetch=1, grid=(count, t // tm, nb),
        in_specs=[pl.BlockSpec((tm, kk), lambda s, i, j, kv: (i, 0)), pl.BlockSpec((kk, tn), cols)] + [ANY] * len(extra),
        out_specs=pl.BlockSpec((tm, tn), lambda s, i, j, kv: (i, kv[base + s] * nb + j)))
    return pl.pallas_call(body, name=name, grid_spec=grid_spec, out_shape=jax.ShapeDtypeStruct((t, N_CHIPS * nn), BF16),
                          input_output_aliases={} if prev is None else {3: 0},
                          compiler_params=_cp(("parallel",) * 3))(kvec, h, w, *extra)


def _gmm_nn(name, p, w, out_dtype):
    t = p.shape[0]
    g, cg, dg = w.shape
    tm = _tile(t, 1024)
    pairs = [(p, pl.BlockSpec((tm, cg), lambda i, j: (i, j)), w, pl.BlockSpec((None, cg, dg), lambda i, j: (j, 0, 0)))]
    o_spec = pl.BlockSpec((tm, dg), lambda i, j: (i, j))
    return _mm(name, pairs, mode="nn", grid=(t // tm, g), out_shape=jax.ShapeDtypeStruct((t, g * dg), out_dtype),
               o_spec=o_spec)


def _gmm_nt(name, dy, w, out_dtype):
    t = dy.shape[0]
    g, cg, dg = w.shape
    tm = _tile(t, 1024)
    pairs = [(dy, pl.BlockSpec((tm, dg), lambda i, j: (i, j)), w, pl.BlockSpec((None, cg, dg), lambda i, j: (j, 0, 0)))]
    o_spec = pl.BlockSpec((tm, cg), lambda i, j: (i, j))
    return _mm(name, pairs, mode="nt", grid=(t // tm, g), out_shape=jax.ShapeDtypeStruct((t, g * cg), out_dtype),
               o_spec=o_spec)


def _gmm_tn(name, p, dy, g, out_dtype):
    t = p.shape[0]
    cg, dg = p.shape[1] // g, dy.shape[1] // g
    pairs = [(p, pl.BlockSpec((t, cg), lambda j: (0, j)), dy, pl.BlockSpec((t, dg), lambda j: (0, j)))]
    o_spec = pl.BlockSpec((None, cg, dg), lambda j: (j, 0, 0))
    return _mm(name, pairs, mode="tn", grid=(g,), out_shape=jax.ShapeDtypeStruct((g, cg, dg), out_dtype), o_spec=o_spec)


ROW_TILE = 256


def _rows(t):
    return _tile8(t, ROW_TILE)


def _tile8(n, pref):
    if n <= pref:
        return n
    for t in range(pref, 0, -8):
        if n % t == 0:
            return t
    raise ValueError(f"no row tile for {n}")


def _cast_place(name, w, pos, shard, deps=()):
    tr = _tile8(w.R, 512)
    if w.colshard:
        o_map = lambda h, i, pos: (0, h, i, pos[1])
    else:
        o_map = lambda h, i, pos: (pos[1], h, i, 0)

    def body(pos_ref, w_ref, *rest):
        rest[-1][...] = w_ref[...].astype(BF16)

    grid_spec = pltpu.PrefetchScalarGridSpec(
        num_scalar_prefetch=1, grid=(2, w.R // tr),
        in_specs=[pl.BlockSpec((None, tr, w.nn), lambda h, i, pos: (h, i, 0))] + [ANY] * len(deps),
        out_specs=pl.BlockSpec((None, None, tr, w.nn), o_map))
    return pl.pallas_call(body, name=name, grid_spec=grid_spec, out_shape=jax.ShapeDtypeStruct((w.P, 2, w.R, w.N), BF16),
                          compiler_params=_cp(("parallel", "parallel")))(pos, shard, *deps)


def _rms_fwd(name, x, g, deps=()):
    t, d = x.shape
    tm = _rows(t)

    def body(x_ref, g_ref, *rest):
        xf = x_ref[...]
        r = lax.rsqrt(jnp.mean(xf * xf, axis=-1, keepdims=True) + EPS)
        rest[-1][...] = (xf * r * g_ref[...]).astype(BF16)

    return pl.pallas_call(
        body, name=name, grid=(t // tm,),
        in_specs=[pl.BlockSpec((tm, d), lambda i: (i, 0)), pl.BlockSpec((1, d), lambda i: (0, 0))] + [ANY] * len(deps),
        out_specs=pl.BlockSpec((tm, d), lambda i: (i, 0)), out_shape=jax.ShapeDtypeStruct((t, d), BF16),
        compiler_params=_cp(("parallel",)),
    )(x, g, *deps)


def _rms_bwd(name, x, g, dh, dres, want_bf16, deps=()):
    t, d = x.shape
    tm = _rows(t)

    def body(x_ref, g_ref, dh_ref, dres_ref, *rest):
        rest = rest[len(deps):]
        dx_ref, rest = rest[0], rest[1:]
        dg_ref = rest[-1]
        xf = x_ref[...]
        r = lax.rsqrt(jnp.mean(xf * xf, axis=-1, keepdims=True) + EPS)
        xh = xf * r
        dhf = dh_ref[...]
        dxh = dhf * g_ref[...]
        m = jnp.mean(dxh * xh, axis=-1, keepdims=True)
        dx = dres_ref[...] + r * (dxh - xh * m)
        dx_ref[...] = dx
        if want_bf16:
            rest[0][...] = dx.astype(BF16)

        @pl.when(pl.program_id(0) == 0)
        def _():
            dg_ref[...] = jnp.zeros_like(dg_ref)

        dg_ref[...] += jnp.sum(dhf * xh, axis=0, keepdims=True)

    row = pl.BlockSpec((tm, d), lambda i: (i, 0))
    vec = pl.BlockSpec((1, d), lambda i: (0, 0))
    out_specs = [row] + ([row] if want_bf16 else []) + [vec]
    out_shape = ([jax.ShapeDtypeStruct((t, d), F32)] + ([jax.ShapeDtypeStruct((t, d), BF16)] if want_bf16 else [])
                 + [jax.ShapeDtypeStruct((1, d), F32)])
    return pl.pallas_call(body, name=name, grid=(t // tm,), in_specs=[row, vec, row, row] + [ANY] * len(deps),
                          out_specs=out_specs, out_shape=out_shape, compiler_params=_cp(("arbitrary",)))(x, g, dh, dres, *deps)


def _final_bwd(name, x3, gf, tgt):
    t, d = x3.shape
    tm = _rows(t)

    def body(x_ref, g_ref, t_ref, dx_ref, dxb_ref, dg_ref, lc_ref):
        xf = x_ref[...]
        g = g_ref[...]
        r = lax.rsqrt(jnp.mean(xf * xf, axis=-1, keepdims=True) + EPS)
        xh = xf * r
        diff = xh * g - t_ref[...]
        dy = diff * (1.0 / d)
        dxh = dy * g
        m = jnp.mean(dxh * xh, axis=-1, keepdims=True)
        dx = r * (dxh - xh * m)
        dx_ref[...] = dx
        dxb_ref[...] = dx.astype(BF16)

        @pl.when(pl.program_id(0) == 0)
        def _():
            dg_ref[...] = jnp.zeros_like(dg_ref)
            lc_ref[...] = jnp.zeros_like(lc_ref)

        dg_ref[...] += jnp.sum(dy * xh, axis=0, keepdims=True)
        lc_ref[...] += jnp.sum(diff * diff, axis=0, keepdims=True) * (0.5 / d)

    row = pl.BlockSpec((tm, d), lambda i: (i, 0))
    vec = pl.BlockSpec((1, d), lambda i: (0, 0))
    return pl.pallas_call(
        body, name=name, grid=(t // tm,), in_specs=[row, vec, row], out_specs=[row, row, vec, vec],
        out_shape=[jax.ShapeDtypeStruct((t, d), F32), jax.ShapeDtypeStruct((t, d), BF16),
                   jax.ShapeDtypeStruct((1, d), F32), jax.ShapeDtypeStruct((1, d), F32)],
        compiler_params=_cp(("arbitrary",)),
    )(x3, gf, tgt)


def _shift_down(v, k, t_idx):
    return jnp.where(t_idx >= k, pltpu.roll(v, k, 0), 0.0)


def _shift_up(v, k, t_idx):
    n = v.shape[0]
    return jnp.where(t_idx < n - k, pltpu.roll(v, n - k, 0), 0.0)


def _window_sums(v, shift, t_idx, grp):
    s = v + shift(v, 1, t_idx)
    out = s
    for lvl in range(1, len(POOL_WINDOWS)):
        s = s + shift(s, 1 << lvl, t_idx)
        out = jnp.where(grp >= lvl, s, out)
    return out


def _window_count(t_idx, grp):
    return jnp.minimum(t_idx + 1, jnp.left_shift(2, grp)).astype(F32)


MIX_COLS = 128


def _mixer_fwd(name, proj, cw, cb, n_conv, n_groups, deps=()):
    t = proj.shape[0]
    nb = n_conv // MIX_COLS
    per_group = n_conv // n_groups // MIX_COLS

    def body(ba_ref, ca_ref, va_ref, vb_ref, cw_ref, cb_ref, *rest):
        z_ref, p_ref = rest[len(deps):]
        t_idx = lax.broadcasted_iota(jnp.int32, (t, MIX_COLS), 0)
        q = ca_ref[...].astype(F32) * va_ref[...].astype(F32)
        w = cw_ref[...]
        u = cb_ref[...] + w[0:1] * _shift_down(q, 2, t_idx) + w[1:2] * _shift_down(q, 1, t_idx) + w[2:3] * q
        z_ref[...] = (ba_ref[...].astype(F32) * u).astype(BF16)
        grp = pl.program_id(0) // per_group
        v = vb_ref[...].astype(F32)
        p_ref[...] = (_window_sums(v, _shift_down, t_idx, grp) / _window_count(t_idx, grp) - v).astype(BF16)

    col = lambda s: pl.BlockSpec((t, MIX_COLS), lambda j: (0, s * nb + j))
    return pl.pallas_call(
        body, name=name, grid=(nb,),
        in_specs=[col(0), col(1), col(2), col(3), pl.BlockSpec((3, MIX_COLS), lambda j: (0, j)),
                  pl.BlockSpec((1, MIX_COLS), lambda j: (0, j))] + [ANY] * len(deps),
        out_specs=[col(0), col(0)],
        out_shape=[jax.ShapeDtypeStruct((t, n_conv), BF16), jax.ShapeDtypeStruct((t, n_conv), BF16)],
        compiler_params=_cp(("parallel",)),
    )(proj, proj, proj, proj, cw, cb, *deps)


def _mixer_bwd(name, dz, dp, proj, cw, cb, dproj, n_conv, n_groups, deps=()):
    t = proj.shape[0]
    nb = n_conv // MIX_COLS
    per_group = n_conv // n_groups // MIX_COLS

    def body(dz_ref, dp_ref, ba_ref, ca_ref, va_ref, cw_ref, cb_ref, _, *rest):
        o_ref, dcw_ref, dcb_ref, scr = rest[len(deps):]
        s = pl.program_id(1)

        @pl.when(s == 0)
        def _():
            t_idx = lax.broadcasted_iota(jnp.int32, (t, MIX_COLS), 0)
            ca, va = ca_ref[...].astype(F32), va_ref[...].astype(F32)
            q = ca * va
            q1, q2 = _shift_down(q, 1, t_idx), _shift_down(q, 2, t_idx)
            w = cw_ref[...]
            u = cb_ref[...] + w[0:1] * q2 + w[1:2] * q1 + w[2:3] * q
            dzf = dz_ref[...].astype(F32)
            du = dzf * ba_ref[...].astype(F32)
            scr[0] = (dzf * u).astype(BF16)
            dq = w[2:3] * du + w[1:2] * _shift_up(du, 1, t_idx) + w[0:1] * _shift_up(du, 2, t_idx)
            scr[1] = (dq * va).astype(BF16)
            scr[2] = (dq * ca).astype(BF16)
            dcb_ref[...] = jnp.sum(du, axis=0, keepdims=True)
            dcw_ref[0:1, :] = jnp.sum(du * q2, axis=0, keepdims=True)
            dcw_ref[1:2, :] = jnp.sum(du * q1, axis=0, keepdims=True)
            dcw_ref[2:3, :] = jnp.sum(du * q, axis=0, keepdims=True)
            grp = pl.program_id(0) // per_group
            dpf = dp_ref[...].astype(F32)
            e = dpf / _window_count(t_idx, grp)
            scr[3] = (_window_sums(e, _shift_up, t_idx, grp) - dpf).astype(BF16)

        o_ref[...] = scr[s]

    col = lambda c: pl.BlockSpec((t, MIX_COLS), lambda j, s: (0, c * nb + j))
    own = pl.BlockSpec((t, MIX_COLS), lambda j, s: (0, j))
    return pl.pallas_call(
        body, name=name, grid=(nb, 4),
        in_specs=[own, own, col(0), col(1), col(2), pl.BlockSpec((3, MIX_COLS), lambda j, s: (0, j)),
                  pl.BlockSpec((1, MIX_COLS), lambda j, s: (0, j)), ANY] + [ANY] * len(deps),
        out_specs=[pl.BlockSpec((t, MIX_COLS), lambda j, s: (0, s * nb + j)),
                   pl.BlockSpec((3, MIX_COLS), lambda j, s: (0, j)), pl.BlockSpec((1, MIX_COLS), lambda j, s: (0, j))],
        out_shape=[jax.ShapeDtypeStruct(dproj.shape, BF16), jax.ShapeDtypeStruct((3, n_conv), F32),
                   jax.ShapeDtypeStruct((1, n_conv), F32)],
        scratch_shapes=[pltpu.VMEM((4, t, MIX_COLS), BF16)],
        input_output_aliases={7: 0},
        compiler_params=_cp(("arbitrary", "arbitrary")),
    )(dz, dp, proj, proj, proj, cw, cb, dproj, *deps)


def _merge_fwd(name, proj, bg, ya, yb, ps):
    t, d = ya.shape
    tm = _rows(t)

    def body(gab_ref, bg_ref, ya_ref, yb_ref, ps_ref, o_ref):
        gab = gab_ref[...].astype(F32) + bg_ref[...]
        sa, sb = jax.nn.sigmoid(gab[:, :d]), jax.nn.sigmoid(gab[:, d:])
        o_ref[...] = (sa * ya_ref[...].astype(F32) + sb * (yb_ref[...].astype(F32) * ps_ref[...])).astype(BF16)

    row = pl.BlockSpec((tm, d), lambda i: (i, 0))
    return pl.pallas_call(
        body, name=name, grid=(t // tm,),
        in_specs=[pl.BlockSpec((tm, 2 * d), lambda i: (i, 1)), pl.BlockSpec((1, 2 * d), lambda i: (0, 0)), row, row,
                  pl.BlockSpec((1, d), lambda i: (0, 0))],
        out_specs=row, out_shape=jax.ShapeDtypeStruct((t, d), BF16), compiler_params=_cp(("parallel",)),
    )(proj, bg, ya, yb, ps)


def _merge_bwd(name, dm, proj, bg, ya, yb, ps, deps=()):
    t, d = ya.shape
    tm = _rows(t)

    def body(dm_ref, gab_ref, bg_ref, ya_ref, yb_ref, ps_ref, *rest):
        dya_ref, dyb_ref, dg_ref, dba_ref, dbb_ref, dps_ref = rest[len(deps):]
        gab = gab_ref[...].astype(F32) + bg_ref[...]
        sa, sb = jax.nn.sigmoid(gab[:, :d]), jax.nn.sigmoid(gab[:, d:])
        dmf = dm_ref[...].astype(F32)
        ybf, ps_ = yb_ref[...].astype(F32), ps_ref[...]
        dya_ref[...] = (dmf * sa).astype(BF16)
        dyb = dmf * sb
        dyb_ref[...] = (dyb * ps_).astype(BF16)
        dga = dmf * ya_ref[...].astype(F32) * sa * (1.0 - sa)
        dgb = dmf * (ybf * ps_) * sb * (1.0 - sb)
        dg_ref[:, :d] = dga.astype(BF16)
        dg_ref[:, d:] = dgb.astype(BF16)

        @pl.when(pl.program_id(0) == 0)
        def _():
            dba_ref[...] = jnp.zeros_like(dba_ref)
            dbb_ref[...] = jnp.zeros_like(dbb_ref)
            dps_ref[...] = jnp.zeros_like(dps_ref)

        dba_ref[...] += jnp.sum(dga, axis=0, keepdims=True)
        dbb_ref[...] += jnp.sum(dgb, axis=0, keepdims=True)
        dps_ref[...] += jnp.sum(dyb * ybf, axis=0, keepdims=True)

    row = pl.BlockSpec((tm, d), lambda i: (i, 0))
    vec = pl.BlockSpec((1, d), lambda i: (0, 0))
    gates = pl.BlockSpec((tm, 2 * d), lambda i: (i, 1))
    return pl.pallas_call(
        body, name=name, grid=(t // tm,),
        in_specs=[row, gates, pl.BlockSpec((1, 2 * d), lambda i: (0, 0)), row, row, vec] + [ANY] * len(deps),
        out_specs=[row, row, gates, vec, vec, vec],
        out_shape=[jax.ShapeDtypeStruct((t, d), BF16), jax.ShapeDtypeStruct((t, d), BF16),
                   jax.ShapeDtypeStruct(proj.shape, BF16), jax.ShapeDtypeStruct((1, d), F32),
                   jax.ShapeDtypeStruct((1, d), F32), jax.ShapeDtypeStruct((1, d), F32)],
        compiler_params=_cp(("arbitrary",)),
    )(dm, proj, bg, ya, yb, ps, *deps)


def _ffn_up_act(name, h, w_up, gate, part=None, prev=None, deps=()):
    t, d = h.shape
    f = w_up.shape[1]
    tm, tf = _tile(t, 512), _tile(f, COL_TILE)
    j0, j1 = _tile_span(f // tf, part)
    n_prev = 0 if prev is None else 2
    extra = ([] if prev is None else list(prev)) + list(deps)

    def body(h_ref, w_ref, g_ref, *rest):
        u_ref, a_ref = rest[len(extra):]
        u = lax.dot_general(h_ref[...], w_ref[...], _DIMS["nn"], preferred_element_type=F32)
        g = g_ref[...].astype(F32)
        u_ref[...] = u.astype(BF16)
        a_ref[...] = (g * jax.nn.sigmoid(g) * u).astype(BF16)

    blk = pl.BlockSpec((tm, tf), lambda i, j: (i, j0 + j))
    shp = jax.ShapeDtypeStruct((t, f), BF16)
    return pl.pallas_call(
        body, name=name, grid=(t // tm, j1 - j0),
        in_specs=[pl.BlockSpec((tm, d), lambda i, j: (i, 0)), pl.BlockSpec((d, tf), lambda i, j: (0, j0 + j)), blk]
        + [ANY] * len(extra),
        out_specs=[blk, blk], out_shape=[shp, shp], input_output_aliases={3 + i: i for i in range(n_prev)},
        compiler_params=_cp(("parallel", "parallel")))(h, w_up, gate, *extra)


def _ffn_bwd(name, dy, w_down, gate, up):
    t, d = dy.shape
    f = w_down.shape[0]
    tm, tf = _tile(t, 512), _tile(f, COL_TILE)

    def body(dy_ref, w_ref, g_ref, u_ref, dg_ref, du_ref):
        da = lax.dot_general(dy_ref[...], w_ref[...], _DIMS["nt"], preferred_element_type=F32)
        g = g_ref[...].astype(F32)
        s = jax.nn.sigmoid(g)
        du_ref[...] = (da * (g * s)).astype(BF16)
        dg_ref[...] = (da * u_ref[...].astype(F32) * (s * (1.0 + g * (1.0 - s)))).astype(BF16)

    blk = pl.BlockSpec((tm, tf), lambda i, j: (i, j))
    shp = jax.ShapeDtypeStruct((t, f), BF16)
    return pl.pallas_call(
        body, name=name, grid=(t // tm, f // tf),
        in_specs=[pl.BlockSpec((tm, d), lambda i, j: (i, 0)), pl.BlockSpec((tf, d), lambda i, j: (j, 0)), blk, blk],
        out_specs=[blk, blk], out_shape=[shp, shp], compiler_params=_cp(("parallel", "parallel")))(dy, w_down, gate, up)


def _adamw_math(w, g, m, v):
    m = ADAM_B1 * m + (1.0 - ADAM_B1) * g
    v = ADAM_B2 * v + (1.0 - ADAM_B2) * (g * g)
    m_hat = m / (1.0 - ADAM_B1 ** ADAM_STEP)
    v_hat = v / (1.0 - ADAM_B2 ** ADAM_STEP)
    delta = -ADAM_LR * (m_hat / (jnp.sqrt(v_hat) + ADAM_EPS) + ADAM_WD * w)
    return delta, m, v


def _adamw(name, w, g, m, v):
    r, c = w.shape
    tr = _tile8(r, 512 if c <= 1024 else 256)

    def body(w_ref, g_ref, m_ref, v_ref, go_ref, d_ref, nm_ref, nv_ref):
        g = g_ref[...]
        go_ref[...] = g
        d_ref[...], nm_ref[...], nv_ref[...] = _adamw_math(w_ref[...], g, m_ref[...], v_ref[...])

    blk = pl.BlockSpec((tr, c), lambda i: (i, 0))
    shp = jax.ShapeDtypeStruct((r, c), F32)
    return pl.pallas_call(body, name=name, grid=(r // tr,), in_specs=[blk] * 4, out_specs=[blk] * 4,
                          out_shape=[shp] * 4, compiler_params=_cp(("parallel",)))(w, g, m, v)


class _Weight:
    def __init__(self, name, rows, cols, colshard):
        self.name, self.colshard = name, colshard
        self.R, self.nn = rows // 2, cols
        self.P = 1 if colshard else N_CHIPS
        self.N = N_CHIPS * cols if colshard else cols

    def cols(self, k):
        return pl.ds(pl.multiple_of(k * self.nn, LANES), self.nn)

    def shard(self, ref, k):
        return ref.at[0, :, :, self.cols(k)] if self.colshard else ref.at[k]

    def half(self, ref, k, h):
        return ref.at[0, h, :, self.cols(k)] if self.colshard else ref.at[k, h]

    def quarter(self, ref, k, h, q):
        return self.half(ref, k, h).at[pl.ds(q * (self.R // 2), self.R // 2), :]

    def part(self, ref, k):
        return ref.at[0, :, self.cols(k)] if self.colshard else ref.at[k]


def _remote(src, dst, ssem, rsem, dev):
    return pltpu.make_async_remote_copy(src_ref=src, dst_ref=dst, send_sem=ssem, recv_sem=rsem, device_id=dev,
                                        device_id_type=MESH)


def _other_chips(x, y):
    chips = [(1 - x, y), (x, 1 - y), (1 - x, 1 - y)]
    return chips, [2 * cx + cy for cx, cy in chips]


def _hbm(a):
    return pltpu.with_memory_space_constraint(a, pltpu.HBM)


def _gather_start(name, groups, lands, after=()):
    flat = [w for grp in groups for w in grp]
    nw, ng = len(flat), len(groups)

    def body(*refs):
        land = refs[:nw]
        sems = refs[nw + len(after):nw + len(after) + 2 * ng]
        token = refs[2 * nw + len(after) + 2 * ng]
        x, y, c = _mesh_pos()
        k_me = 2 * x + y
        chips, _ = _other_chips(x, y)
        i = 0
        for g, grp in enumerate(groups):
            for wi, w in enumerate(grp):
                mine = w.half(land[i], k_me, c)
                for j, chip in enumerate(chips):
                    _remote(mine, mine, sems[2 * g].at[3 * wi + j], sems[2 * g + 1].at[3 * wi + j], (*chip, c)).start()
                i += 1
        token[...] = jnp.zeros_like(token)

    sem_shapes = []
    for grp in groups:
        sem_shapes += [pltpu.SemaphoreType.DMA((3 * len(grp),))] * 2
    out = pl.pallas_call(
        body, name=name, in_specs=[HBM] * nw + [ANY] * len(after),
        out_specs=[SEM] * (2 * ng) + [HBM] * nw + [VMEM],
        out_shape=sem_shapes + [pltpu.HBM(a.shape, a.dtype) for a in lands] + [jax.ShapeDtypeStruct((8, LANES), F32)],
        input_output_aliases={i: 2 * ng + i for i in range(nw)},
        compiler_params=pltpu.CompilerParams(has_side_effects=EFFECT),
    )(*[_hbm(a) for a in lands], *after)
    sems = [(out[2 * g], out[2 * g + 1]) for g in range(ng)]
    return sems, list(out[2 * ng:2 * ng + nw]), out[-1]


def _gather_wait(name, grp, lands, ssem, rsem, after):
    n = len(grp)

    def body(*refs):
        land, ssem_ref, rsem_ref = refs[:n], refs[n], refs[n + 1]
        x, y, c = _mesh_pos()
        k_me = 2 * x + y
        chips, ks = _other_chips(x, y)
        for wi, w in enumerate(grp):
            for j, chip in enumerate(chips):
                cp = _remote(w.half(land[wi], k_me, c), w.half(land[wi], ks[j], c), ssem_ref.at[3 * wi + j],
                             rsem_ref.at[3 * wi + j], (*chip, c))
                cp.wait_send()
                cp.wait_recv()

    return pl.pallas_call(
        body, name=name, in_specs=[HBM] * n + [SEM, SEM, ANY], out_specs=[HBM] * n,
        out_shape=[pltpu.HBM(a.shape, a.dtype) for a in lands], input_output_aliases={i: i for i in range(n)},
        compiler_params=pltpu.CompilerParams(has_side_effects=EFFECT),
    )(*lands, ssem, rsem, after)


def _split_start(name, arrays, n, copies, after=()):
    na = len(arrays)

    def body(*refs):
        ssem, rsem, token = refs[na + len(after):][0], refs[na + len(after):][1], refs[2 * na + len(after) + 2]
        for i, (src, dst, dev, _) in enumerate(copies(refs[:na], *_mesh_pos())):
            _remote(src, dst, ssem.at[i], rsem.at[i], dev).start()
        token[...] = jnp.zeros_like(token)

    out = pl.pallas_call(
        body, name=name, in_specs=[HBM] * na + [ANY] * len(after), out_specs=[SEM, SEM] + [HBM] * na + [VMEM],
        out_shape=[pltpu.SemaphoreType.DMA((n,))] * 2 + [pltpu.HBM(a.shape, a.dtype) for a in arrays]
        + [jax.ShapeDtypeStruct((8, LANES), F32)],
        input_output_aliases={i: 2 + i for i in range(na)},
        compiler_params=pltpu.CompilerParams(has_side_effects=EFFECT),
    )(*[_hbm(a) for a in arrays], *after)
    return out[0], out[1], list(out[2:2 + na]), out[-1]


def _split_wait(name, arrays, ssem, rsem, copies, after):
    na = len(arrays)

    def body(*refs):
        for i, (src, _, dev, dst) in enumerate(copies(refs[:na], *_mesh_pos())):
            cp = _remote(src, dst, refs[na].at[i], refs[na + 1].at[i], dev)
            cp.wait_send()
            cp.wait_recv()

    return list(pl.pallas_call(
        body, name=name, in_specs=[HBM] * na + [SEM, SEM] + [ANY] * len(after), out_specs=[HBM] * na,
        out_shape=[pltpu.HBM(a.shape, a.dtype) for a in arrays], input_output_aliases={i: i for i in range(na)},
        compiler_params=pltpu.CompilerParams(has_side_effects=EFFECT),
    )(*arrays, ssem, rsem, *after))


def _pass_copies(grp, rels=(0, 1, 2)):
    def copies(land, x, y, c):
        _, ks = _other_chips(x, y)
        return [(w.half(land[wi], ks[j], c), w.half(land[wi], ks[j], c), (x, y, 1 - c), w.half(land[wi], ks[j], 1 - c))
                for wi, w in enumerate(grp) for j in rels]
    copies.n = len(grp) * len(rels)
    return copies


def _near_copies(grp):
    def copies(land, x, y, c):
        chips, ks = _other_chips(x, y)
        out = []
        for wi, w in enumerate(grp):
            mine = w.half(land[wi], 2 * x + y, c)
            out += [(mine, mine, (*chips[j], c), w.half(land[wi], ks[j], c)) for j in (0, 1)]
        return out
    copies.n = 2 * len(grp)
    return copies


def _far_copies(grp):
    def copies(land, x, y, c):
        chips, ks = _other_chips(x, y)
        out = []
        for wi, w in enumerate(grp):
            for j in (0, 1):
                q = w.quarter(land[wi], ks[j], c, j)
                out.append((q, q, (*chips[1 - j], c), w.quarter(land[wi], ks[2], c, j)))
        return out
    copies.n = 2 * len(grp)
    return copies


def _pair_copies(n, whole=False):
    def copies(refs, x, y, c):
        return [(refs[i] if whole else refs[i].at[:, 1 - c], refs[n + i], (x, y, 1 - c), refs[n + i]) for i in range(n)]
    return copies


def _share_copies(n):
    def copies(refs, x, y, c):
        return [(refs[i].at[c], refs[i].at[c], (x, y, 1 - c), refs[i].at[1 - c]) for i in range(n)]
    return copies


def _gather_conv_w(cw):
    ncw = cw.shape[1]

    def body(cw_ref, out_ref, ssem, rsem):
        x, y, c = _mesh_pos()
        k_me = 2 * x + y
        chips, ks = _other_chips(x, y)
        cols = lambda k: out_ref.at[:, pl.ds(pl.multiple_of(k * ncw, LANES), ncw)]
        cps = [_remote(cw_ref, cols(k_me), ssem.at[j], rsem.at[j], (*chip, c)) for j, chip in enumerate(chips)]
        for cp in cps:
            cp.start()
        for k in range(N_CHIPS):
            @pl.when(k_me == k)
            def _():
                out_ref[:, k * ncw:(k + 1) * ncw] = cw_ref[...]
        for j in range(3):
            _remote(cw_ref, cols(ks[j]), ssem.at[j], rsem.at[j], (*chips[j], c)).wait_recv()
        for cp in cps:
            cp.wait_send()

    return pl.pallas_call(
        body, name="gather_conv_w", in_specs=[VMEM], out_specs=VMEM,
        out_shape=jax.ShapeDtypeStruct((3, N_CHIPS * ncw), F32),
        scratch_shapes=[pltpu.SemaphoreType.DMA((3,)), pltpu.SemaphoreType.DMA((3,))],
    )(cw)


def _grad_tiles(w, n):
    return _tile8(w.R, 512) if w.R <= 512 else w.R // 2, _tile(n, 2048)


def _pair_sum(name, w, pos, grad, got):
    tr, tn = _grad_tiles(w, w.N)

    def body(pos_ref, g_ref, r_ref, o_ref):
        o_ref[...] = (g_ref[...].astype(F32) + r_ref[...].astype(F32)).astype(BF16)

    blk = pl.BlockSpec((None, tr, tn), lambda p, i, j, pos: (p, i, j))
    grid_spec = pltpu.PrefetchScalarGridSpec(
        num_scalar_prefetch=1, grid=(w.P, w.R // tr, w.N // tn),
        in_specs=[pl.BlockSpec((None, None, tr, tn), lambda p, i, j, pos: (p, pos[0], i, j)), blk], out_specs=blk)
    return pl.pallas_call(body, name=name, grid_spec=grid_spec, out_shape=jax.ShapeDtypeStruct((w.P, w.R, w.N), BF16),
                          compiler_params=_cp(("parallel",) * 3))(pos, grad, got)


def _scatter_start(name, ws, pairs):
    nw = len(ws)

    def body(*refs):
        pr, land = refs[:nw], refs[nw:2 * nw]
        ssem, rsem = refs[2 * nw], refs[2 * nw + 1]
        token = refs[4 * nw + 2]
        x, y, c = _mesh_pos()
        chips, ks = _other_chips(x, y)
        for i, w in enumerate(ws):
            for j, chip in enumerate(chips):
                _remote(w.part(pr[i], ks[j]), land[i].at[j], ssem.at[3 * i + j], rsem.at[3 * i + j], (*chip, c)).start()
        token[...] = jnp.zeros_like(token)

    lands = [lax.empty((3, w.R, w.nn), BF16) for w in ws]
    out = pl.pallas_call(
        body, name=name, in_specs=[HBM] * (2 * nw),
        out_specs=[SEM, SEM] + [HBM] * (2 * nw) + [VMEM],
        out_shape=[pltpu.SemaphoreType.DMA((3 * nw,))] * 2 + [pltpu.HBM(a.shape, a.dtype) for a in pairs + lands]
        + [jax.ShapeDtypeStruct((8, LANES), F32)],
        input_output_aliases={i: 2 + i for i in range(2 * nw)},
        compiler_params=pltpu.CompilerParams(has_side_effects=EFFECT),
    )(*[_hbm(a) for a in pairs + lands])
    return out[0], out[1], list(out[2:2 + nw]), list(out[2 + nw:2 + 2 * nw]), out[-1]


def _scatter_wait(name, ws, pairs, lands, ssem, rsem, after):
    nw = len(ws)

    def body(*refs):
        pr, land = refs[:nw], refs[nw:2 * nw]
        ssem_ref, rsem_ref = refs[2 * nw], refs[2 * nw + 1]
        x, y, c = _mesh_pos()
        chips, ks = _other_chips(x, y)
        for i, w in enumerate(ws):
            for j, chip in enumerate(chips):
                cp = _remote(w.part(pr[i], ks[j]), land[i].at[j], ssem_ref.at[3 * i + j], rsem_ref.at[3 * i + j], (*chip, c))
                cp.wait_send()
                cp.wait_recv()

    out = pl.pallas_call(
        body, name=name, in_specs=[HBM] * (2 * nw) + [SEM, SEM] + [ANY] * len(after), out_specs=[HBM] * (2 * nw),
        out_shape=[pltpu.HBM(a.shape, a.dtype) for a in pairs + lands],
        input_output_aliases={i: i for i in range(2 * nw)},
        compiler_params=pltpu.CompilerParams(has_side_effects=EFFECT),
    )(*pairs, *lands, ssem, rsem, *after)
    return list(out[:nw]), list(out[nw:])


def _final_sum(name, w, pos, grad, got, parts):
    tr, tn = _grad_tiles(w, w.nn)
    nbc = w.nn // tn
    if got is None:
        return _final_sum_pair(name, w, pos, grad, parts, tr, tn)

    def body(pos_ref, g_ref, r_ref, p_ref, o_ref):
        acc = g_ref[...].astype(F32) + r_ref[...].astype(F32)
        for j in range(3):
            acc = acc + p_ref[j].astype(F32)
        o_ref[...] = acc

    if w.colshard:
        g_spec = pl.BlockSpec((None, None, tr, tn), lambda i, j, pos: (0, pos[0], i, pos[1] * nbc + j))
        r_spec = pl.BlockSpec((None, tr, tn), lambda i, j, pos: (0, i, pos[1] * nbc + j))
    else:
        g_spec = pl.BlockSpec((None, None, tr, tn), lambda i, j, pos: (pos[1], pos[0], i, j))
        r_spec = pl.BlockSpec((None, tr, tn), lambda i, j, pos: (pos[1], i, j))
    grid_spec = pltpu.PrefetchScalarGridSpec(
        num_scalar_prefetch=1, grid=(w.R // tr, nbc),
        in_specs=[g_spec, r_spec, pl.BlockSpec((3, tr, tn), lambda i, j, pos: (0, i, j))],
        out_specs=pl.BlockSpec((None, tr, tn), lambda i, j, pos: (pos[0], i, j)))
    return pl.pallas_call(body, name=name, grid_spec=grid_spec, out_shape=jax.ShapeDtypeStruct((2, w.R, w.nn), F32),
                          compiler_params=_cp(("parallel",) * 2))(pos, grad, got, parts)


def _final_sum_pair(name, w, pos, pair, parts, tr, tn):
    nbc = w.nn // tn

    def body(pos_ref, g_ref, p_ref, o_ref):
        acc = g_ref[...].astype(F32)
        for j in range(3):
            acc = acc + p_ref[j].astype(F32)
        o_ref[...] = acc

    if w.colshard:
        g_spec = pl.BlockSpec((None, tr, tn), lambda i, j, pos: (0, i, pos[1] * nbc + j))
    else:
        g_spec = pl.BlockSpec((None, tr, tn), lambda i, j, pos: (pos[1], i, j))
    grid_spec = pltpu.PrefetchScalarGridSpec(
        num_scalar_prefetch=1, grid=(w.R // tr, nbc),
        in_specs=[g_spec, pl.BlockSpec((3, tr, tn), lambda i, j, pos: (0, i, j))],
        out_specs=pl.BlockSpec((None, tr, tn), lambda i, j, pos: (pos[0], i, j)))
    return pl.pallas_call(body, name=name, grid_spec=grid_spec, out_shape=jax.ShapeDtypeStruct((2, w.R, w.nn), F32),
                          compiler_params=_cp(("parallel",) * 2))(pos, pair, parts)


def _share_halves(name, ws, halves, deps=()):
    nw = len(ws)

    def body(*refs):
        out = refs[nw + len(deps):2 * nw + len(deps)]
        ssem, rsem = refs[2 * nw + len(deps):]
        x, y, c = _mesh_pos()
        sib = (x, y, 1 - c)
        cps = [_remote(out[i].at[c], out[i].at[c], ssem.at[i], rsem.at[i], sib) for i in range(nw)]
        for cp in cps:
            cp.start()
        for i, cp in enumerate(cps):
            cp.wait_send()
            _remote(out[i].at[1 - c], out[i].at[1 - c], ssem.at[i], rsem.at[i], sib).wait_recv()

    return pl.pallas_call(
        body, name=name, in_specs=[ANY] * (nw + len(deps)), out_specs=[ANY] * nw,
        out_shape=[jax.ShapeDtypeStruct(h.shape, F32) for h in halves],
        scratch_shapes=[pltpu.SemaphoreType.DMA((nw,)), pltpu.SemaphoreType.DMA((nw,))],
        input_output_aliases={i: i for i in range(nw)},
    )(*halves, *deps)


VEC_ROWS = 16


def _vector_step(d, n_conv, parts, params, deps=()):
    ncw = params[2][0].shape[1]
    n_par = len(params)

    def body(*refs):
        dg1, dba, dbb, dcw, dcb, dps, dg2, dgf, lc = refs[:9]
        wmv = refs[9:9 + 3 * n_par]
        refs = refs[9 + 3 * n_par + len(deps):]
        outs = refs[:4 * n_par]
        loss_ref = refs[4 * n_par]
        snd, got, ssem, rsem = refs[4 * n_par + 1:]
        x, y, c = _mesh_pos()
        me = 4 * x + 2 * y + c
        snd[...] = jnp.zeros_like(snd)
        for row, ref in ((0, dg1), (1, dba), (2, dbb), (3, dps), (4, dg2), (5, dgf), (6, lc)):
            snd[row:row + 1, :] = ref[...]
        snd[7:8, :n_conv] = dcb[...]
        snd[8:11, :n_conv] = dcw[...]
        cps = []
        for r in range(1, N_DEV):
            peer = tuple(1 - p if (r >> b) & 1 else p for p, b in ((x, 2), (y, 1), (c, 0)))
            cps.append(_remote(snd, got.at[me], ssem.at[r - 1], rsem.at[r - 1], peer))
        for cp in cps:
            cp.start()
        got[me] = snd[...]
        for r in range(1, N_DEV):
            peer = tuple(1 - p if (r >> b) & 1 else p for p, b in ((x, 2), (y, 1), (c, 0)))
            _remote(snd, got.at[4 * peer[0] + 2 * peer[1] + peer[2]], ssem.at[r - 1], rsem.at[r - 1], peer).wait_recv()
        for cp in cps:
            cp.wait_send()
        tot = got[0]
        for dev in range(1, N_DEV):
            tot = tot + got[dev]
        loss_ref[...] = jnp.sum(tot[6:7, :], axis=1, keepdims=True)
        k_me = 2 * x + y
        g_cw = jnp.zeros((3, ncw), F32)
        for k in range(N_CHIPS):
            g_cw = g_cw + jnp.where(k_me == k, tot[8:11, k * ncw:(k + 1) * ncw], 0.0)
        grads = [tot[0:1, :], jnp.concatenate([tot[1:2, :], tot[2:3, :]], axis=1), g_cw, tot[7:8, :n_conv],
                 tot[3:4, :], tot[4:5, :], tot[5:6, :]]
        for i, g in enumerate(grads):
            w_ref, m_ref, v_ref = wmv[3 * i:3 * i + 3]
            delta, nm, nv = _adamw_math(w_ref[...], g, m_ref[...], v_ref[...])
            outs[4 * i][...] = g
            outs[4 * i + 1][...] = delta
            outs[4 * i + 2][...] = nm
            outs[4 * i + 3][...] = nv

    args = list(parts)
    out_shape = []
    for w, m, v in params:
        args += [w, m, v]
        out_shape += [jax.ShapeDtypeStruct(w.shape, F32)] * 4
    out_shape.append(jax.ShapeDtypeStruct((1, 1), F32))
    return pl.pallas_call(
        body, name="vector_params_step", in_specs=[VMEM] * len(args) + [ANY] * len(deps),
        out_specs=[VMEM] * len(out_shape), out_shape=out_shape,
        scratch_shapes=[pltpu.VMEM((VEC_ROWS, d), F32), pltpu.VMEM((N_DEV, VEC_ROWS, d), F32),
                        pltpu.SemaphoreType.DMA((N_DEV - 1,)), pltpu.SemaphoreType.DMA((N_DEV - 1,))],
        compiler_params=pltpu.CompilerParams(vmem_limit_bytes=VMEM_LIMIT),
    )(*args, *deps)


def kernel(x, norm1_g, w_in, b_gate, conv_w, conv_b, w_a_out, w_pool, pool_scale, w_o, norm2_g, w_ffn_gate, w_ffn_up, w_ffn_down, final_g, loss_target, m_norm1_g, m_w_in, m_b_gate, m_conv_w, m_conv_b, m_w_a_out, m_w_pool, m_pool_scale, m_w_o, m_norm2_g, m_w_ffn_gate, m_w_ffn_up, m_w_ffn_down, m_final_g, v_norm1_g, v_w_in, v_b_gate, v_conv_w, v_conv_b, v_w_a_out, v_w_pool, v_pool_scale, v_w_o, v_norm2_g, v_w_ffn_gate, v_w_ffn_up, v_w_ffn_down, v_final_g):
    t, d = x.shape[1], x.shape[2]
    n_conv = conv_b.shape[1]
    n_groups, pool_cg, pool_dg = w_pool.shape[1], w_pool.shape[2], N_CHIPS * w_pool.shape[3]
    d_ff = N_CHIPS * w_ffn_gate.shape[2]
    assert n_conv // n_groups == pool_cg and n_conv % (n_groups * MIX_COLS) == 0 and n_groups == len(POOL_WINDOWS)

    big = {"w_in": (w_in, m_w_in, v_w_in), "w_a_out": (w_a_out, m_w_a_out, v_w_a_out), "w_pool": (w_pool, m_w_pool, v_w_pool),
           "w_o": (w_o, m_w_o, v_w_o), "w_ffn_gate": (w_ffn_gate, m_w_ffn_gate, v_w_ffn_gate),
           "w_ffn_up": (w_ffn_up, m_w_ffn_up, v_w_ffn_up), "w_ffn_down": (w_ffn_down, m_w_ffn_down, v_w_ffn_down)}
    colshard = {"w_in": True, "w_a_out": True, "w_pool": True, "w_o": False, "w_ffn_gate": True, "w_ffn_up": True,
                "w_ffn_down": False}
    names = list(big)
    shard2d = {n: big[n][0].reshape(-1, big[n][0].shape[-1]) for n in names}
    ws = [_Weight(n, *shard2d[n].shape, colshard[n]) for n in names]

    xs, tgt = x[0], loss_target[0]
    cw_loc = conv_w[0]
    pos = jnp.stack([lax.axis_index("c"), 2 * lax.axis_index("x") + lax.axis_index("y")]).astype(jnp.int32)
    by_name = {w.name: w for w in ws}
    groups = [[by_name[n] for n in g] for g in (["w_in"], ["w_a_out", "w_pool", "w_o"], ["w_ffn_gate"], ["w_ffn_up"],
                                                 ["w_ffn_down"])]
    first = [sum(len(g) for g in groups[:i]) for i in range(len(groups))]
    rgroups = [groups[0], groups[1], groups[2] + groups[3], groups[4]]

    cw_full = _gather_conv_w(cw_loc)
    cast = lambda w, dep: _cast_place(f"cast_{w.name}", w, pos, shard2d[w.name].reshape(2, w.R, w.nn), deps=[dep])
    chips, ks = _other_chips(lax.axis_index("x"), lax.axis_index("y"))
    kvec = jnp.stack([pos[1], *ks]).astype(jnp.int32)
    full = {}

    def start(name, arrays, copies, after=()):
        ssem, rsem, arrays, token = _split_start(name, arrays, copies.n, copies, after)
        return name, arrays, ssem, rsem, copies, token

    def wait(started, after):
        name, arrays, ssem, rsem, copies, _ = started
        return _split_wait(name + "_wait", arrays, ssem, rsem, copies, after)

    def pass_on(g, got, after=()):
        return start(f"pass_{g}", got, _pass_copies(groups[g]), after)

    def passed(g, st, after=None):
        got = wait(st, [st[5]] if after is None else after)
        full.update({w.name: a.reshape(w.P * 2 * w.R, w.N) for w, a in zip(groups[g], got)})

    near = start("near_0", [cast(w, cw_full) for w in groups[0]], _near_copies(groups[0]))
    rest = [cast(w, near[5]) for grp in groups[1:] for w in grp]
    h1 = _rms_fwd("norm1_fwd", xs, norm1_g, deps=[near[5]])
    proj = _proj_piece("proj_own", h1, shard2d["w_in"], None, kvec, 0, 1, deps=rest)
    got = wait(near, [proj])
    far = start("far_0", got, _far_copies(groups[0]))
    sems_b, lands_b, tok_b = _gather_start("gather_start_b", groups[1:2], rest[:3], after=[far[5]])
    st = start("pass_near_0", far[1], _pass_copies(groups[0], (0, 1)), [tok_b])
    got = wait(st, [st[5]])
    proj = _proj_piece("proj_near", h1, got[0].reshape(-1, groups[0][0].N), proj, kvec, 1, 2)
    st = start("pass_far_0", wait((far[0], got) + far[2:], [proj]), _pass_copies(groups[0], (2,)))
    got = wait(st, [st[5]])
    w_in_full = got[0].reshape(-1, groups[0][0].N)
    proj = _proj_piece("proj_far", h1, w_in_full, proj, kvec, 3, 1)
    got = _gather_wait("gather_wait_1", groups[1], lands_b, *sems_b[0], proj)
    near_g = start("near_2", rest[3:4], _near_copies(groups[2]), got)
    st = pass_on(1, got, [near_g[5]])
    z, p = _mixer_fwd("mixer_fwd", proj, cw_full, conv_b, n_conv, n_groups, deps=[st[5]])
    passed(1, st, [z])
    wp_full = full["w_pool"].reshape(n_groups, pool_cg, pool_dg)
    ya = _mm_nn("conv_out", z, full["w_a_out"], BF16)
    yb = _gmm_nn("pool_out", p, wp_full, BF16)
    merged = _merge_fwd("merge_fwd", proj, b_gate, ya, yb, pool_scale)
    far_g = start("far_2", wait(near_g, [merged]), _far_copies(groups[2]))
    near_u = start("near_3", rest[4:5], _near_copies(groups[3]), [far_g[5]])
    x2 = _mm_nn("mix_out", merged, full["w_o"], F32, add=xs, deps=[near_u[5]])
    st = pass_on(2, wait(far_g, [x2]))
    h2 = _rms_fwd("norm2_fwd", x2, norm2_g, deps=[st[5]])
    passed(2, st, [h2])
    gate = _mm_nn("ffn_gate_a", h2, full["w_ffn_gate"], BF16, part=(0, 2))
    far_u = start("far_3", wait(near_u, [gate]), _far_copies(groups[3]))
    near_d = start("near_4", rest[5:6], _near_copies(groups[4]), [far_u[5]])
    gate = _mm_nn("ffn_gate_b", h2, full["w_ffn_gate"], BF16, part=(1, 2), prev=gate, deps=[near_d[5]])
    passed(3, pass_on(3, wait(far_u, [gate])))
    up_act = _ffn_up_act("ffn_up_act_a", h2, full["w_ffn_up"], gate, part=(0, 2))
    far_d = start("far_4", wait(near_d, [up_act[0]]), _far_copies(groups[4]))
    up, act = _ffn_up_act("ffn_up_act_b", h2, full["w_ffn_up"], gate, part=(1, 2), prev=up_act, deps=[far_d[5]])
    passed(4, pass_on(4, wait(far_d, [act])))
    x3 = _mm_nn("ffn_down", act, full["w_ffn_down"], F32, add=x2, tk=d_ff // 4)

    pending = {}

    def pair_start(g, grads):
        grp = rgroups[g]
        gcan = [grads[w.name].reshape(w.P, 2, w.R, w.N) for w in grp]
        slots = [lax.empty((w.P, w.R, w.N), BF16) for w in grp]
        pending[g] = _split_start(f"pair_start_{g}", gcan + slots, len(grp), _pair_copies(len(grp)))
        return pending[g][3]

    def scatter_start(g, after):
        grp = rgroups[g]
        n = len(grp)
        ssem, rsem, arrs, _ = pending[g]
        arrs = _split_wait(f"pair_wait_{g}", arrs, ssem, rsem, _pair_copies(n), after)
        gcan, sib = arrs[:n], arrs[n:]
        pairs = [_pair_sum(f"pair_sum_{w.name}", w, pos, a, s) for w, a, s in zip(grp, gcan, sib)]
        ssem, rsem, pairs, slots, token = _scatter_start(f"scatter_start_{g}", grp, pairs)
        pending[g] = (gcan, sib, pairs, slots, ssem, rsem)
        return token

    def pair_start_halves(g, ab, deps):
        grp = rgroups[g]
        sent = [_mm_tn_half(f"d{w.name}_sib", a, b, pos, False, deps=deps if i == 0 else ()) for i, (w, (a, b)) in enumerate(zip(grp, ab))]
        slots = [lax.empty((1, w.R, w.N), BF16) for w in grp]
        pending[g] = _split_start(f"pair_start_{g}", sent + slots, len(grp), _pair_copies(len(grp), whole=True))
        return pending[g][3]

    def scatter_start_halves(g, ab, after):
        grp = rgroups[g]
        n = len(grp)
        ssem, rsem, arrs, _ = pending[g]
        arrs = _split_wait(f"pair_wait_{g}", arrs, ssem, rsem, _pair_copies(n, whole=True), after)
        pairs = [_mm_tn_half(f"d{w.name}_own", a, b, pos, True, add=s) for w, (a, b), s in zip(grp, ab, arrs[n:])]
        ssem, rsem, pairs, slots, token = _scatter_start(f"scatter_start_{g}", grp, pairs)
        pending[g] = (None, None, pairs, slots, ssem, rsem)
        return token

    def reduce_finish(g, after):
        grp = rgroups[g]
        gcan, sib, pairs, slots, ssem, rsem = pending[g]
        pairs, parts = _scatter_wait(f"scatter_wait_{g}", grp, pairs, slots, ssem, rsem, after)
        if gcan is None:
            return [_final_sum(f"final_sum_{w.name}", w, pos, a, None, q) for w, a, q in zip(grp, pairs, parts)]
        return [_final_sum(f"final_sum_{w.name}", w, pos, a, s, q) for w, a, s, q in zip(grp, gcan, sib, parts)]

    grads = {}
    dx3, dx3b, d_gf, loss_cols = _final_bwd("final_bwd", x3, final_g.reshape(1, d), tgt)
    dgate, dup = _ffn_bwd("ffn_bwd", dx3b, full["w_ffn_down"], gate, up)
    grads["w_ffn_down"] = _mm_tn("dw_ffn_down", act, dx3b, BF16)
    tok = pair_start(3, grads)
    dh2 = _mm_nt("d_h2", [(dgate, full["w_ffn_gate"]), (dup, full["w_ffn_up"])], F32, tk=d_ff // 4, deps=[tok])
    tok = scatter_start(3, [dh2])
    tok = pair_start_halves(2, [(h2, dgate), (h2, dup)], [tok])
    dx2, dx2b, d_g2 = _rms_bwd("norm2_bwd", x2, norm2_g, dh2, dx3, True, deps=[tok])
    dmerged = _mm_nt("d_merged", [(dx2b, full["w_o"])], BF16, tk=d)
    grads["w_o"] = _mm_tn("dw_o", merged, dx2b, BF16)
    tok = scatter_start_halves(2, [(h2, dgate), (h2, dup)], [grads["w_o"]])
    dya, dyb, dproj, d_bga, d_bgb, d_ps = _merge_bwd("merge_bwd", dmerged, proj, b_gate, ya, yb, pool_scale, deps=[tok])
    dz = _mm_nt("d_z", [(dya, full["w_a_out"])], BF16, tk=d)
    grads["w_a_out"] = _mm_tn("dw_a_out", z, dya, BF16)
    dp = _gmm_nt("d_pool", dyb, wp_full, BF16)
    grads["w_pool"] = _gmm_tn("dw_pool", p, dyb, n_groups, BF16)
    tok = pair_start(1, grads)
    dproj, d_cw, d_cb = _mixer_bwd("mixer_bwd", dz, dp, proj, cw_full, conv_b, dproj, n_conv, n_groups, deps=[tok])
    tok = scatter_start(1, [dproj])
    tok = pair_start_halves(0, [(h1, dproj)], [tok])
    dh1 = _mm_nt("d_h1", [(dproj, w_in_full)], F32, tk=proj.shape[1] // 4, deps=[tok])
    tok = scatter_start_halves(0, [(h1, dproj)], [dh1])
    grad_x, d_g1 = _rms_bwd("norm1_bwd", xs, norm1_g, dh1, dx2, False, deps=[tok])

    g_big, d_big, m_big, v_big = {}, {}, {}, {}

    def update(wsub, shared):
        out = []
        for w, g in zip(wsub, shared):
            wt, mt, vt = big[w.name]
            g2 = g.reshape(2 * w.R, w.nn)
            go, dl, nm, nv = _adamw(f"adamw_{w.name}", shard2d[w.name], g2, mt.reshape(g2.shape), vt.reshape(g2.shape))
            g_big[w.name], d_big[w.name], m_big[w.name], v_big[w.name] = (a.reshape(wt.shape) for a in (go, dl, nm, nv))
            out.append(nv)
        return out

    after = [grad_x]
    started = []
    for g in (3, 2, 1):
        halves = reduce_finish(g, after)
        share = _share_copies(len(halves))
        ssem, rsem, halves, tok = _split_start(f"share_start_{g}", halves, len(halves), share)
        started.append((g, ssem, rsem, halves, share))
        after = [tok]
    for g, ssem, rsem, halves, share in started:
        after = update(rgroups[g], _split_wait(f"share_wait_{g}", halves, ssem, rsem, share, after))
    after = update(rgroups[0], _share_halves("share_halves_w_in", rgroups[0], reduce_finish(0, after)))

    vec_names = ["norm1_g", "b_gate", "conv_w", "conv_b", "pool_scale", "norm2_g", "final_g"]
    vec = {"norm1_g": (norm1_g, m_norm1_g, v_norm1_g), "b_gate": (b_gate, m_b_gate, v_b_gate),
           "conv_w": (cw_loc, m_conv_w[0], v_conv_w[0]), "conv_b": (conv_b, m_conv_b, v_conv_b),
           "pool_scale": (pool_scale, m_pool_scale, v_pool_scale), "norm2_g": (norm2_g, m_norm2_g, v_norm2_g),
           "final_g": tuple(a.reshape(1, d) for a in (final_g, m_final_g, v_final_g))}
    vout = _vector_step(d, n_conv, [d_g1, d_bga, d_bgb, d_cw, d_cb, d_ps, d_g2, d_gf, loss_cols],
                        [vec[n] for n in vec_names], deps=after)

    shapes = {"conv_w": conv_w.shape, "final_g": final_g.shape}
    g_vec, d_vec, m_vec, v_vec = ({n: vout[4 * i + q].reshape(shapes.get(n, vec[n][0].shape)) for i, n in enumerate(vec_names)}
                                  for q in range(4))
    loss = vout[-1].reshape(())

    order = ["norm1_g", "w_in", "b_gate", "conv_w", "conv_b", "w_a_out", "w_pool", "pool_scale", "w_o", "norm2_g",
             "w_ffn_gate", "w_ffn_up", "w_ffn_down", "final_g"]
    pick = lambda vecs, bigs: [vecs[n] if n in vecs else bigs[n] for n in order]
    return (loss, grad_x.reshape(x.shape), *pick(g_vec, g_big), *pick(d_vec, d_big), *pick(m_vec, m_big),
            *pick(v_vec, v_big))
```

```python
import functools

import jax
import jax.numpy as jnp
from jax import lax
from jax.experimental import pallas as pl
from jax.experimental.pallas import tpu as pltpu

F32, BF16 = jnp.float32, jnp.bfloat16
MESH = pl.DeviceIdType.MESH
ANY = pl.BlockSpec(memory_space=pl.ANY)
VMEM = pl.BlockSpec(memory_space=pltpu.VMEM)
HBM = pl.BlockSpec(memory_space=pltpu.HBM)
SEM = pl.BlockSpec(memory_space=pltpu.SEMAPHORE)
EFFECT = pltpu.SideEffectType.DATAFLOW_SIDE_EFFECTING

EPS = 1e-6
POOL_WINDOWS = (2, 4, 8, 16)
ADAM_LR, ADAM_B1, ADAM_B2, ADAM_EPS, ADAM_WD, ADAM_STEP = 0.001, 0.9, 0.999, 1e-08, 0.01, 10

V7X_VMEM_BYTES = 64 * 1024 * 1024
VMEM_LIMIT = V7X_VMEM_BYTES * 3 // 4
LANES = 128
COL_TILE = 8 * LANES
N_CHIPS = 4
N_DEV = 8

_DIMS = {
    "nn": (((1,), (0,)), ((), ())),
    "nt": (((1,), (1,)), ((), ())),
    "tn": (((0,), (0,)), ((), ())),
}


def _cp(sem):
    return pltpu.CompilerParams(dimension_semantics=sem, vmem_limit_bytes=VMEM_LIMIT)


def _mesh_pos():
    return lax.axis_index("x"), lax.axis_index("y"), lax.axis_index("c")


def _mm(name, pairs, *, mode, grid, out_shape, o_spec, nk=1, kaxis=None, add=None, deps=(), prev=None):
    npair = len(pairs)
    has_add = add is not None

    def body(*refs):
        ab = refs[: 2 * npair]
        pos = 2 * npair
        add_ref = refs[pos] if has_add else None
        pos += int(has_add) + len(deps) + (prev is not None)
        o_ref = refs[pos]
        acc_ref = refs[pos + 1] if nk > 1 else None
        d = None
        for p in range(npair):
            t = lax.dot_general(ab[2 * p][...], ab[2 * p + 1][...], _DIMS[mode], preferred_element_type=F32)
            d = t if d is None else d + t
        if nk == 1:
            if has_add:
                d = d + add_ref[...].astype(F32)
            o_ref[...] = d.astype(o_ref.dtype)
        else:
            k = pl.program_id(kaxis)

            @pl.when(k == 0)
            def _():
                acc_ref[...] = d

            @pl.when(k > 0)
            def _():
                acc_ref[...] += d

            @pl.when(k == nk - 1)
            def _():
                r = acc_ref[...]
                if has_add:
                    r = r + add_ref[...].astype(F32)
                o_ref[...] = r.astype(o_ref.dtype)

    args, specs = [], []
    for a, a_spec, b, b_spec in pairs:
        args += [a, b]
        specs += [a_spec, b_spec]
    if has_add:
        args.append(add[0])
        specs.append(add[1])
    args += list(deps)
    specs += [ANY] * len(deps)
    aliases = {}
    if prev is not None:
        aliases = {len(args): 0}
        args.append(prev)
        specs.append(ANY)
    scratch = []
    if nk > 1:
        blk = [d for d in o_spec.block_shape if d is not None]
        scratch = [pltpu.VMEM(tuple(blk), F32)]
    sem = tuple("arbitrary" if (nk > 1 and ax == kaxis) else "parallel" for ax in range(len(grid)))
    return pl.pallas_call(
        body, name=name, grid=grid, in_specs=specs, out_specs=o_spec, out_shape=out_shape,
        scratch_shapes=scratch, input_output_aliases=aliases, compiler_params=_cp(sem),
    )(*args)


def _tile_span(n_tiles, part):
    if part is None:
        return 0, n_tiles
    p, of = part
    return p * n_tiles // of, (p + 1) * n_tiles // of


def _tile(n, pref):
    if n <= pref:
        return n
    for t in range(pref, 0, -LANES):
        if t % LANES == 0 and n % t == 0:
            return t
    raise ValueError(f"no tile for {n}")


def _mm_nn(name, a, b, out_dtype, add=None, tk=None, deps=(), part=None, prev=None):
    m, kk = a.shape
    n = b.shape[1]
    tm, tn = _tile(m, 1024), _tile(n, COL_TILE)
    out_shape = jax.ShapeDtypeStruct((m, n), out_dtype)
    if tk is None or tk == kk:
        j0, j1 = _tile_span(n // tn, part)
        grid = (m // tm, j1 - j0)
        pairs = [(a, pl.BlockSpec((tm, kk), lambda i, j: (i, 0)), b, pl.BlockSpec((kk, tn), lambda i, j: (0, j0 + j)))]
        o_spec = pl.BlockSpec((tm, tn), lambda i, j: (i, j0 + j))
        add_ = None if add is None else (add, pl.BlockSpec((tm, tn), lambda i, j: (i, j0 + j)))
        return _mm(name, pairs, mode="nn", grid=grid, out_shape=out_shape, o_spec=o_spec, add=add_, deps=deps, prev=prev)
    tn = _tile(n, 1024)
    nk = kk // tk
    grid = (m // tm, n // tn, nk)
    pairs = [(a, pl.BlockSpec((tm, tk), lambda i, j, k: (i, k)), b, pl.BlockSpec((tk, tn), lambda i, j, k: (k, j)))]
    o_spec = pl.BlockSpec((tm, tn), lambda i, j, k: (i, j))
    add_ = None if add is None else (add, pl.BlockSpec((tm, tn), lambda i, j, k: (i, j)))
    return _mm(name, pairs, mode="nn", grid=grid, out_shape=out_shape, o_spec=o_spec, nk=nk, kaxis=2, add=add_, deps=deps)


def _mm_nt(name, abs_, out_dtype, tk, deps=()):
    m, kk = abs_[0][0].shape
    n = abs_[0][1].shape[0]
    tm = _tile(m, 1024)
    nk = kk // tk
    tn = _tile(n, COL_TILE if nk == 1 else 1024)
    out_shape = jax.ShapeDtypeStruct((m, n), out_dtype)
    if nk == 1:
        grid = (m // tm, n // tn)
        pairs = [(a, pl.BlockSpec((tm, kk), lambda i, j: (i, 0)), b, pl.BlockSpec((tn, kk), lambda i, j: (j, 0)))
                 for a, b in abs_]
        o_spec = pl.BlockSpec((tm, tn), lambda i, j: (i, j))
        return _mm(name, pairs, mode="nt", grid=grid, out_shape=out_shape, o_spec=o_spec, deps=deps)
    grid = (m // tm, n // tn, nk)
    pairs = [(a, pl.BlockSpec((tm, tk), lambda i, j, k: (i, k)), b, pl.BlockSpec((tn, tk), lambda i, j, k: (j, k)))
             for a, b in abs_]
    o_spec = pl.BlockSpec((tm, tn), lambda i, j, k: (i, j))
    return _mm(name, pairs, mode="nt", grid=grid, out_shape=out_shape, o_spec=o_spec, nk=nk, kaxis=2, deps=deps)


def _mm_tn(name, a, b, out_dtype, deps=()):
    t, m = a.shape
    n = b.shape[1]
    tm, tn = _tile(m, 512), _tile(n, 2048)
    if n > m:
        grid = (n // tn, m // tm)
        a_map, b_map, o_map = (lambda j, i: (0, i)), (lambda j, i: (0, j)), (lambda j, i: (i, j))
    else:
        grid = (m // tm, n // tn)
        a_map, b_map, o_map = (lambda i, j: (0, i)), (lambda i, j: (0, j)), (lambda i, j: (i, j))
    pairs = [(a, pl.BlockSpec((t, tm), a_map), b, pl.BlockSpec((t, tn), b_map))]
    o_spec = pl.BlockSpec((tm, tn), o_map)
    return _mm(name, pairs, mode="tn", grid=grid, out_shape=jax.ShapeDtypeStruct((m, n), out_dtype), o_spec=o_spec,
               deps=deps)


def _mm_tn_half(name, a, b, pos, mine, add=None, deps=()):
    t, m = a.shape
    r, n = m // 2, b.shape[1]
    tm, tn = _tile(r, 512), _tile(n, 2048)
    nbi = r // tm
    half = (lambda pos: pos[0]) if mine else (lambda pos: 1 - pos[0])
    if n > r:
        grid, ij = (n // tn, nbi), (lambda g0, g1: (g1, g0))
    else:
        grid, ij = (nbi, n // tn), (lambda g0, g1: (g0, g1))
    has_add = add is not None

    def body(pos_ref, a_ref, b_ref, *rest):
        d = lax.dot_general(a_ref[...], b_ref[...], _DIMS["tn"], preferred_element_type=F32)
        if has_add:
            d = d + rest[0][...].astype(F32)
        rest[-1][...] = d.astype(BF16)

    o_spec = pl.BlockSpec((None, tm, tn), lambda g0, g1, pos: (0, *ij(g0, g1)))
    grid_spec = pltpu.PrefetchScalarGridSpec(
        num_scalar_prefetch=1, grid=grid,
        in_specs=[pl.BlockSpec((t, tm), lambda g0, g1, pos: (0, half(pos) * nbi + ij(g0, g1)[0])),
                  pl.BlockSpec((t, tn), lambda g0, g1, pos: (0, ij(g0, g1)[1]))]
        + ([o_spec] if has_add else []) + [ANY] * len(deps),
        out_specs=o_spec)
    return pl.pallas_call(body, name=name, grid_spec=grid_spec, out_shape=jax.ShapeDtypeStruct((1, r, n), BF16),
                          compiler_params=_cp(("parallel",) * 2))(pos, a, b, *([add] if has_add else []), *deps)


def _proj_piece(name, h, w, prev, kvec, base, count, deps=()):
    t, kk = h.shape
    own = w.dtype == F32
    nn = w.shape[1] if own else w.shape[1] // N_CHIPS
    tm, tn = _tile(t, 1024), _tile(nn, COL_TILE)
    nb = nn // tn

    def body(kv_ref, h_ref, w_ref, *rest):
        rest[-1][...] = lax.dot_general(h_ref[...], w_ref[...].astype(BF16), _DIMS["nn"],
                                        preferred_element_type=F32).astype(BF16)

    cols = lambda s, i, j, kv: (0, j) if own else (0, kv[base + s] * nb + j)
    extra = ([] if prev is None else [prev]) + list(deps)
    grid_spec = pltpu.PrefetchScalarGridSpec(
        num_scalar_prefetch=1, grid=(count, t // tm, nb),
        in_specs=[pl.BlockSpec((tm, kk), lambda s, i, j, kv: (i, 0)), pl.BlockSpec((kk, tn), cols)] + [ANY] * len(extra),
        out_specs=pl.BlockSpec((tm, tn), lambda s, i, j, kv: (i, kv[base + s] * nb + j)))
    return pl.pallas_call(body, name=name, grid_spec=grid_spec, out_shape=jax.ShapeDtypeStruct((t, N_CHIPS * nn), BF16),
                          input_output_aliases={} if prev is None else {3: 0},
                          compiler_params=_cp(("parallel",) * 3))(kvec, h, w, *extra)


def _gmm_nn(name, p, w, out_dtype):
    t = p.shape[0]
    g, cg, dg = w.shape
    tm = _tile(t, 1024)
    pairs = [(p, pl.BlockSpec((tm, cg), lambda i, j: (i, j)), w, pl.BlockSpec((None, cg, dg), lambda i, j: (j, 0, 0)))]
    o_spec = pl.BlockSpec((tm, dg), lambda i, j: (i, j))
    return _mm(name, pairs, mode="nn", grid=(t // tm, g), out_shape=jax.ShapeDtypeStruct((t, g * dg), out_dtype),
               o_spec=o_spec)


def _gmm_nt(name, dy, w, out_dtype):
    t = dy.shape[0]
    g, cg, dg = w.shape
    tm = _tile(t, 1024)
    pairs = [(dy, pl.BlockSpec((tm, dg), lambda i, j: (i, j)), w, pl.BlockSpec((None, cg, dg), lambda i, j: (j, 0, 0)))]
    o_spec = pl.BlockSpec((tm, cg), lambda i, j: (i, j))
    return _mm(name, pairs, mode="nt", grid=(t // tm, g), out_shape=jax.ShapeDtypeStruct((t, g * cg), out_dtype),
               o_spec=o_spec)


def _gmm_tn(name, p, dy, g, out_dtype):
    t = p.shape[0]
    cg, dg = p.shape[1] // g, dy.shape[1] // g
    pairs = [(p, pl.BlockSpec((t, cg), lambda j: (0, j)), dy, pl.BlockSpec((t, dg), lambda j: (0, j)))]
    o_spec = pl.BlockSpec((None, cg, dg), lambda j: (j, 0, 0))
    return _mm(name, pairs, mode="tn", grid=(g,), out_shape=jax.ShapeDtypeStruct((g, cg, dg), out_dtype), o_spec=o_spec)


ROW_TILE = 256


def _rows(t):
    return _tile8(t, ROW_TILE)


def _tile8(n, pref):
    if n <= pref:
        return n
    for t in range(pref, 0, -8):
        if n % t == 0:
            return t
    raise ValueError(f"no row tile for {n}")


def _cast_place(name, w, pos, shard, deps=()):
    tr = _tile8(w.R, 512)
    if w.colshard:
        o_map = lambda h, i, pos: (0, h, i, pos[1])
    else:
        o_map = lambda h, i, pos: (pos[1], h, i, 0)

    def body(pos_ref, w_ref, *rest):
        rest[-1][...] = w_ref[...].astype(BF16)

    grid_spec = pltpu.PrefetchScalarGridSpec(
        num_scalar_prefetch=1, grid=(2, w.R // tr),
        in_specs=[pl.BlockSpec((None, tr, w.nn), lambda h, i, pos: (h, i, 0))] + [ANY] * len(deps),
        out_specs=pl.BlockSpec((None, None, tr, w.nn), o_map))
    return pl.pallas_call(body, name=name, grid_spec=grid_spec, out_shape=jax.ShapeDtypeStruct((w.P, 2, w.R, w.N), BF16),
                          compiler_params=_cp(("parallel", "parallel")))(pos, shard, *deps)


def _rms_fwd(name, x, g, deps=()):
    t, d = x.shape
    tm = _rows(t)

    def body(x_ref, g_ref, *rest):
        xf = x_ref[...]
        r = lax.rsqrt(jnp.mean(xf * xf, axis=-1, keepdims=True) + EPS)
        rest[-1][...] = (xf * r * g_ref[...]).astype(BF16)

    return pl.pallas_call(
        body, name=name, grid=(t // tm,),
        in_specs=[pl.BlockSpec((tm, d), lambda i: (i, 0)), pl.BlockSpec((1, d), lambda i: (0, 0))] + [ANY] * len(deps),
        out_specs=pl.BlockSpec((tm, d), lambda i: (i, 0)), out_shape=jax.ShapeDtypeStruct((t, d), BF16),
        compiler_params=_cp(("parallel",)),
    )(x, g, *deps)


def _rms_bwd(name, x, g, dh, dres, want_bf16, deps=()):
    t, d = x.shape
    tm = _rows(t)

    def body(x_ref, g_ref, dh_ref, dres_ref, *rest):
        rest = rest[len(deps):]
        dx_ref, rest = rest[0], rest[1:]
        dg_ref = rest[-1]
        xf = x_ref[...]
        r = lax.rsqrt(jnp.mean(xf * xf, axis=-1, keepdims=True) + EPS)
        xh = xf * r
        dhf = dh_ref[...]
        dxh = dhf * g_ref[...]
        m = jnp.mean(dxh * xh, axis=-1, keepdims=True)
        dx = dres_ref[...] + r * (dxh - xh * m)
        dx_ref[...] = dx
        if want_bf16:
            rest[0][...] = dx.astype(BF16)

        @pl.when(pl.program_id(0) == 0)
        def _():
            dg_ref[...] = jnp.zeros_like(dg_ref)

        dg_ref[...] += jnp.sum(dhf * xh, axis=0, keepdims=True)

    row = pl.BlockSpec((tm, d), lambda i: (i, 0))
    vec = pl.BlockSpec((1, d), lambda i: (0, 0))
    out_specs = [row] + ([row] if want_bf16 else []) + [vec]
    out_shape = ([jax.ShapeDtypeStruct((t, d), F32)] + ([jax.ShapeDtypeStruct((t, d), BF16)] if want_bf16 else [])
                 + [jax.ShapeDtypeStruct((1, d), F32)])
    return pl.pallas_call(body, name=name, grid=(t // tm,), in_specs=[row, vec, row, row] + [ANY] * len(deps),
                          out_specs=out_specs, out_shape=out_shape, compiler_params=_cp(("arbitrary",)))(x, g, dh, dres, *deps)


def _final_bwd(name, x3, gf, tgt):
    t, d = x3.shape
    tm = _rows(t)

    def body(x_ref, g_ref, t_ref, dx_ref, dxb_ref, dg_ref, lc_ref):
        xf = x_ref[...]
        g = g_ref[...]
        r = lax.rsqrt(jnp.mean(xf * xf, axis=-1, keepdims=True) + EPS)
        xh = xf * r
        diff = xh * g - t_ref[...]
        dy = diff * (1.0 / d)
        dxh = dy * g
        m = jnp.mean(dxh * xh, axis=-1, keepdims=True)
        dx = r * (dxh - xh * m)
        dx_ref[...] = dx
        dxb_ref[...] = dx.astype(BF16)

        @pl.when(pl.program_id(0) == 0)
        def _():
            dg_ref[...] = jnp.zeros_like(dg_ref)
            lc_ref[...] = jnp.zeros_like(lc_ref)

        dg_ref[...] += jnp.sum(dy * xh, axis=0, keepdims=True)
        lc_ref[...] += jnp.sum(diff * diff, axis=0, keepdims=True) * (0.5 / d)

    row = pl.BlockSpec((tm, d), lambda i: (i, 0))
    vec = pl.BlockSpec((1, d), lambda i: (0, 0))
    return pl.pallas_call(
        body, name=name, grid=(t // tm,), in_specs=[row, vec, row], out_specs=[row, row, vec, vec],
        out_shape=[jax.ShapeDtypeStruct((t, d), F32), jax.ShapeDtypeStruct((t, d), BF16),
                   jax.ShapeDtypeStruct((1, d), F32), jax.ShapeDtypeStruct((1, d), F32)],
        compiler_params=_cp(("arbitrary",)),
    )(x3, gf, tgt)


def _shift_down(v, k, t_idx):
    return jnp.where(t_idx >= k, pltpu.roll(v, k, 0), 0.0)


def _shift_up(v, k, t_idx):
    n = v.shape[0]
    return jnp.where(t_idx < n - k, pltpu.roll(v, n - k, 0), 0.0)


def _window_sums(v, shift, t_idx, grp):
    s = v + shift(v, 1, t_idx)
    out = s
    for lvl in range(1, len(POOL_WINDOWS)):
        s = s + shift(s, 1 << lvl, t_idx)
        out = jnp.where(grp >= lvl, s, out)
    return out


def _window_weight(t_idx, grp):
    return 1.0 / jnp.minimum(t_idx[:, :1] + 1, jnp.left_shift(2, grp)).astype(F32)


MIX_COLS = 128


def _mixer_fwd(name, proj, cw, cb, n_conv, n_groups, deps=()):
    t = proj.shape[0]
    nb = n_conv // MIX_COLS
    per_group = n_conv // n_groups // MIX_COLS

    def body(ba_ref, ca_ref, va_ref, vb_ref, cw_ref, cb_ref, *rest):
        z_ref, p_ref = rest[len(deps):]
        t_idx = lax.broadcasted_iota(jnp.int32, (t, MIX_COLS), 0)
        q = ca_ref[...].astype(F32) * va_ref[...].astype(F32)
        w = cw_ref[...]
        u = cb_ref[...] + w[0:1] * _shift_down(q, 2, t_idx) + w[1:2] * _shift_down(q, 1, t_idx) + w[2:3] * q
        z_ref[...] = (ba_ref[...].astype(F32) * u).astype(BF16)
        grp = pl.program_id(0) // per_group
        v = vb_ref[...].astype(F32)
        p_ref[...] = (_window_sums(v, _shift_down, t_idx, grp) * _window_weight(t_idx, grp) - v).astype(BF16)

    col = lambda s: pl.BlockSpec((t, MIX_COLS), lambda j: (0, s * nb + j))
    return pl.pallas_call(
        body, name=name, grid=(nb,),
        in_specs=[col(0), col(1), col(2), col(3), pl.BlockSpec((3, MIX_COLS), lambda j: (0, j)),
                  pl.BlockSpec((1, MIX_COLS), lambda j: (0, j))] + [ANY] * len(deps),
        out_specs=[col(0), col(0)],
        out_shape=[jax.ShapeDtypeStruct((t, n_conv), BF16), jax.ShapeDtypeStruct((t, n_conv), BF16)],
        compiler_params=_cp(("parallel",)),
    )(proj, proj, proj, proj, cw, cb, *deps)


def _mixer_bwd(name, dz, dp, proj, cw, cb, dproj, n_conv, n_groups, deps=()):
    t = proj.shape[0]
    nb = n_conv // MIX_COLS
    per_group = n_conv // n_groups // MIX_COLS

    def body(dz_ref, dp_ref, ba_ref, ca_ref, va_ref, cw_ref, cb_ref, _, *rest):
        o_ref, dcw_ref, dcb_ref, scr = rest[len(deps):]
        s = pl.program_id(1)

        @pl.when(s == 0)
        def _():
            t_idx = lax.broadcasted_iota(jnp.int32, (t, MIX_COLS), 0)
            ca, va = ca_ref[...].astype(F32), va_ref[...].astype(F32)
            q = ca * va
            q1, q2 = _shift_down(q, 1, t_idx), _shift_down(q, 2, t_idx)
            w = cw_ref[...]
            u = cb_ref[...] + w[0:1] * q2 + w[1:2] * q1 + w[2:3] * q
            dzf = dz_ref[...].astype(F32)
            du = dzf * ba_ref[...].astype(F32)
            scr[0] = (dzf * u).astype(BF16)
            dq = w[2:3] * du + w[1:2] * _shift_up(du, 1, t_idx) + w[0:1] * _shift_up(du, 2, t_idx)
            scr[1] = (dq * va).astype(BF16)
            scr[2] = (dq * ca).astype(BF16)
            dcb_ref[...] = jnp.sum(du, axis=0, keepdims=True)
            dcw_ref[0:1, :] = jnp.sum(du * q2, axis=0, keepdims=True)
            dcw_ref[1:2, :] = jnp.sum(du * q1, axis=0, keepdims=True)
            dcw_ref[2:3, :] = jnp.sum(du * q, axis=0, keepdims=True)
            grp = pl.program_id(0) // per_group
            dpf = dp_ref[...].astype(F32)
            e = dpf * _window_weight(t_idx, grp)
            scr[3] = (_window_sums(e, _shift_up, t_idx, grp) - dpf).astype(BF16)

        o_ref[...] = scr[s]

    col = lambda c: pl.BlockSpec((t, MIX_COLS), lambda j, s: (0, c * nb + j))
    own = pl.BlockSpec((t, MIX_COLS), lambda j, s: (0, j))
    return pl.pallas_call(
        body, name=name, grid=(nb, 4),
        in_specs=[own, own, col(0), col(1), col(2), pl.BlockSpec((3, MIX_COLS), lambda j, s: (0, j)),
                  pl.BlockSpec((1, MIX_COLS), lambda j, s: (0, j)), ANY] + [ANY] * len(deps),
        out_specs=[pl.BlockSpec((t, MIX_COLS), lambda j, s: (0, s * nb + j)),
                   pl.BlockSpec((3, MIX_COLS), lambda j, s: (0, j)), pl.BlockSpec((1, MIX_COLS), lambda j, s: (0, j))],
        out_shape=[jax.ShapeDtypeStruct(dproj.shape, BF16), jax.ShapeDtypeStruct((3, n_conv), F32),
                   jax.ShapeDtypeStruct((1, n_conv), F32)],
        scratch_shapes=[pltpu.VMEM((4, t, MIX_COLS), BF16)],
        input_output_aliases={7: 0},
        compiler_params=_cp(("arbitrary", "arbitrary")),
    )(dz, dp, proj, proj, proj, cw, cb, dproj, *deps)


def _merge_fwd(name, proj, bg, ya, yb, ps):
    t, d = ya.shape
    tm = _rows(t)

    def body(gab_ref, bg_ref, ya_ref, yb_ref, ps_ref, o_ref):
        gab = gab_ref[...].astype(F32) + bg_ref[...]
        sa, sb = jax.nn.sigmoid(gab[:, :d]), jax.nn.sigmoid(gab[:, d:])
        o_ref[...] = (sa * ya_ref[...].astype(F32) + sb * (yb_ref[...].astype(F32) * ps_ref[...])).astype(BF16)

    row = pl.BlockSpec((tm, d), lambda i: (i, 0))
    return pl.pallas_call(
        body, name=name, grid=(t // tm,),
        in_specs=[pl.BlockSpec((tm, 2 * d), lambda i: (i, 1)), pl.BlockSpec((1, 2 * d), lambda i: (0, 0)), row, row,
                  pl.BlockSpec((1, d), lambda i: (0, 0))],
        out_specs=row, out_shape=jax.ShapeDtypeStruct((t, d), BF16), compiler_params=_cp(("parallel",)),
    )(proj, bg, ya, yb, ps)


def _merge_bwd(name, dm, proj, bg, ya, yb, ps, deps=()):
    t, d = ya.shape
    tm = _rows(t)

    def body(dm_ref, gab_ref, bg_ref, ya_ref, yb_ref, ps_ref, *rest):
        dya_ref, dyb_ref, dg_ref, dba_ref, dbb_ref, dps_ref = rest[len(deps):]
        gab = gab_ref[...].astype(F32) + bg_ref[...]
        sa, sb = jax.nn.sigmoid(gab[:, :d]), jax.nn.sigmoid(gab[:, d:])
        dmf = dm_ref[...].astype(F32)
        ybf, ps_ = yb_ref[...].astype(F32), ps_ref[...]
        dya_ref[...] = (dmf * sa).astype(BF16)
        dyb = dmf * sb
        dyb_ref[...] = (dyb * ps_).astype(BF16)
        dga = dmf * ya_ref[...].astype(F32) * sa * (1.0 - sa)
        dgb = dmf * (ybf * ps_) * sb * (1.0 - sb)
        dg_ref[:, :d] = dga.astype(BF16)
        dg_ref[:, d:] = dgb.astype(BF16)

        @pl.when(pl.program_id(0) == 0)
        def _():
            dba_ref[...] = jnp.zeros_like(dba_ref)
            dbb_ref[...] = jnp.zeros_like(dbb_ref)
            dps_ref[...] = jnp.zeros_like(dps_ref)

        dba_ref[...] += jnp.sum(dga, axis=0, keepdims=True)
        dbb_ref[...] += jnp.sum(dgb, axis=0, keepdims=True)
        dps_ref[...] += jnp.sum(dyb * ybf, axis=0, keepdims=True)

    row = pl.BlockSpec((tm, d), lambda i: (i, 0))
    vec = pl.BlockSpec((1, d), lambda i: (0, 0))
    gates = pl.BlockSpec((tm, 2 * d), lambda i: (i, 1))
    return pl.pallas_call(
        body, name=name, grid=(t // tm,),
        in_specs=[row, gates, pl.BlockSpec((1, 2 * d), lambda i: (0, 0)), row, row, vec] + [ANY] * len(deps),
        out_specs=[row, row, gates, vec, vec, vec],
        out_shape=[jax.ShapeDtypeStruct((t, d), BF16), jax.ShapeDtypeStruct((t, d), BF16),
                   jax.ShapeDtypeStruct(proj.shape, BF16), jax.ShapeDtypeStruct((1, d), F32),
                   jax.ShapeDtypeStruct((1, d), F32), jax.ShapeDtypeStruct((1, d), F32)],
        compiler_params=_cp(("arbitrary",)),
    )(dm, proj, bg, ya, yb, ps, *deps)


def _ffn_up_act(name, h, w_up, gate, part=None, prev=None, deps=()):
    t, d = h.shape
    f = w_up.shape[1]
    tm, tf = _tile(t, 1024), _tile(f, 512)
    j0, j1 = _tile_span(f // tf, part)
    n_prev = 0 if prev is None else 2
    extra = ([] if prev is None else list(prev)) + list(deps)

    def body(h_ref, w_ref, g_ref, *rest):
        u_ref, a_ref = rest[len(extra):]
        u = lax.dot_general(h_ref[...], w_ref[...], _DIMS["nn"], preferred_element_type=F32)
        g = g_ref[...].astype(F32)
        u_ref[...] = u.astype(BF16)
        a_ref[...] = (g * jax.nn.sigmoid(g) * u).astype(BF16)

    blk = pl.BlockSpec((tm, tf), lambda i, j: (i, j0 + j))
    shp = jax.ShapeDtypeStruct((t, f), BF16)
    return pl.pallas_call(
        body, name=name, grid=(t // tm, j1 - j0),
        in_specs=[pl.BlockSpec((tm, d), lambda i, j: (i, 0)), pl.BlockSpec((d, tf), lambda i, j: (0, j0 + j)), blk]
        + [ANY] * len(extra),
        out_specs=[blk, blk], out_shape=[shp, shp], input_output_aliases={3 + i: i for i in range(n_prev)},
        compiler_params=_cp(("parallel", "parallel")))(h, w_up, gate, *extra)


def _ffn_bwd(name, dy, w_down, gate, up):
    t, d = dy.shape
    f = w_down.shape[0]
    tm, tf = _tile(t, 1024), _tile(f, 512)

    def body(dy_ref, w_ref, g_ref, u_ref, dg_ref, du_ref):
        da = lax.dot_general(dy_ref[...], w_ref[...], _DIMS["nt"], preferred_element_type=F32)
        g = g_ref[...].astype(F32)
        s = jax.nn.sigmoid(g)
        du_ref[...] = (da * (g * s)).astype(BF16)
        dg_ref[...] = (da * u_ref[...].astype(F32) * (s * (1.0 + g * (1.0 - s)))).astype(BF16)

    blk = pl.BlockSpec((tm, tf), lambda i, j: (i, j))
    shp = jax.ShapeDtypeStruct((t, f), BF16)
    return pl.pallas_call(
        body, name=name, grid=(t // tm, f // tf),
        in_specs=[pl.BlockSpec((tm, d), lambda i, j: (i, 0)), pl.BlockSpec((tf, d), lambda i, j: (j, 0)), blk, blk],
        out_specs=[blk, blk], out_shape=[shp, shp], compiler_params=_cp(("parallel", "parallel")))(dy, w_down, gate, up)


def _adamw_math(w, g, m, v):
    m = ADAM_B1 * m + (1.0 - ADAM_B1) * g
    v = ADAM_B2 * v + (1.0 - ADAM_B2) * (g * g)
    m_hat = m / (1.0 - ADAM_B1 ** ADAM_STEP)
    v_hat = v / (1.0 - ADAM_B2 ** ADAM_STEP)
    delta = -ADAM_LR * (m_hat / (jnp.sqrt(v_hat) + ADAM_EPS) + ADAM_WD * w)
    return delta, m, v


def _adamw(name, w, g, m, v):
    r, c = w.shape
    tr = _tile8(r, 512 if c <= 1024 else 256)

    def body(w_ref, g_ref, m_ref, v_ref, go_ref, d_ref, nm_ref, nv_ref):
        g = g_ref[...]
        go_ref[...] = g
        d_ref[...], nm_ref[...], nv_ref[...] = _adamw_math(w_ref[...], g, m_ref[...], v_ref[...])

    blk = pl.BlockSpec((tr, c), lambda i: (i, 0))
    shp = jax.ShapeDtypeStruct((r, c), F32)
    return pl.pallas_call(body, name=name, grid=(r // tr,), in_specs=[blk] * 4, out_specs=[blk] * 4,
                          out_shape=[shp] * 4, compiler_params=_cp(("parallel",)))(w, g, m, v)


class _Weight:
    def __init__(self, name, rows, cols, colshard):
        self.name, self.colshard = name, colshard
        self.R, self.nn = rows // 2, cols
        self.P = 1 if colshard else N_CHIPS
        self.N = N_CHIPS * cols if colshard else cols

    def cols(self, k):
        return pl.ds(pl.multiple_of(k * self.nn, LANES), self.nn)

    def shard(self, ref, k):
        return ref.at[0, :, :, self.cols(k)] if self.colshard else ref.at[k]

    def half(self, ref, k, h):
        return ref.at[0, h, :, self.cols(k)] if self.colshard else ref.at[k, h]

    def quarter(self, ref, k, h, q):
        return self.half(ref, k, h).at[pl.ds(q * (self.R // 2), self.R // 2), :]

    def part(self, ref, k):
        return ref.at[0, :, self.cols(k)] if self.colshard else ref.at[k]


def _remote(src, dst, ssem, rsem, dev):
    return pltpu.make_async_remote_copy(src_ref=src, dst_ref=dst, send_sem=ssem, recv_sem=rsem, device_id=dev,
                                        device_id_type=MESH)


def _other_chips(x, y):
    chips = [(1 - x, y), (x, 1 - y), (1 - x, 1 - y)]
    return chips, [2 * cx + cy for cx, cy in chips]


def _hbm(a):
    return pltpu.with_memory_space_constraint(a, pltpu.HBM)


def _gather_start(name, groups, lands, after=()):
    flat = [w for grp in groups for w in grp]
    nw, ng = len(flat), len(groups)

    def body(*refs):
        land = refs[:nw]
        sems = refs[nw + len(after):nw + len(after) + 2 * ng]
        token = refs[2 * nw + len(after) + 2 * ng]
        x, y, c = _mesh_pos()
        k_me = 2 * x + y
        chips, _ = _other_chips(x, y)
        i = 0
        for g, grp in enumerate(groups):
            for wi, w in enumerate(grp):
                mine = w.half(land[i], k_me, c)
                for j, chip in enumerate(chips):
                    _remote(mine, mine, sems[2 * g].at[3 * wi + j], sems[2 * g + 1].at[3 * wi + j], (*chip, c)).start()
                i += 1
        token[...] = jnp.zeros_like(token)

    sem_shapes = []
    for grp in groups:
        sem_shapes += [pltpu.SemaphoreType.DMA((3 * len(grp),))] * 2
    out = pl.pallas_call(
        body, name=name, in_specs=[HBM] * nw + [ANY] * len(after),
        out_specs=[SEM] * (2 * ng) + [HBM] * nw + [VMEM],
        out_shape=sem_shapes + [pltpu.HBM(a.shape, a.dtype) for a in lands] + [jax.ShapeDtypeStruct((8, LANES), F32)],
        input_output_aliases={i: 2 * ng + i for i in range(nw)},
        compiler_params=pltpu.CompilerParams(has_side_effects=EFFECT),
    )(*[_hbm(a) for a in lands], *after)
    sems = [(out[2 * g], out[2 * g + 1]) for g in range(ng)]
    return sems, list(out[2 * ng:2 * ng + nw]), out[-1]


def _gather_wait(name, grp, lands, ssem, rsem, after):
    n = len(grp)

    def body(*refs):
        land, ssem_ref, rsem_ref = refs[:n], refs[n], refs[n + 1]
        x, y, c = _mesh_pos()
        k_me = 2 * x + y
        chips, ks = _other_chips(x, y)
        for wi, w in enumerate(grp):
            for j, chip in enumerate(chips):
                cp = _remote(w.half(land[wi], k_me, c), w.half(land[wi], ks[j], c), ssem_ref.at[3 * wi + j],
                             rsem_ref.at[3 * wi + j], (*chip, c))
                cp.wait_send()
                cp.wait_recv()

    return pl.pallas_call(
        body, name=name, in_specs=[HBM] * n + [SEM, SEM, ANY], out_specs=[HBM] * n,
        out_shape=[pltpu.HBM(a.shape, a.dtype) for a in lands], input_output_aliases={i: i for i in range(n)},
        compiler_params=pltpu.CompilerParams(has_side_effects=EFFECT),
    )(*lands, ssem, rsem, after)


def _split_start(name, arrays, n, copies, after=()):
    na = len(arrays)

    def body(*refs):
        ssem, rsem, token = refs[na + len(after):][0], refs[na + len(after):][1], refs[2 * na + len(after) + 2]
        for i, (src, dst, dev, _) in enumerate(copies(refs[:na], *_mesh_pos())):
            _remote(src, dst, ssem.at[i], rsem.at[i], dev).start()
        token[...] = jnp.zeros_like(token)

    out = pl.pallas_call(
        body, name=name, in_specs=[HBM] * na + [ANY] * len(after), out_specs=[SEM, SEM] + [HBM] * na + [VMEM],
        out_shape=[pltpu.SemaphoreType.DMA((n,))] * 2 + [pltpu.HBM(a.shape, a.dtype) for a in arrays]
        + [jax.ShapeDtypeStruct((8, LANES), F32)],
        input_output_aliases={i: 2 + i for i in range(na)},
        compiler_params=pltpu.CompilerParams(has_side_effects=EFFECT),
    )(*[_hbm(a) for a in arrays], *after)
    return out[0], out[1], list(out[2:2 + na]), out[-1]


def _split_wait(name, arrays, ssem, rsem, copies, after):
    na = len(arrays)

    def body(*refs):
        for i, (src, _, dev, dst) in enumerate(copies(refs[:na], *_mesh_pos())):
            cp = _remote(src, dst, refs[na].at[i], refs[na + 1].at[i], dev)
            cp.wait_send()
            cp.wait_recv()

    return list(pl.pallas_call(
        body, name=name, in_specs=[HBM] * na + [SEM, SEM] + [ANY] * len(after), out_specs=[HBM] * na,
        out_shape=[pltpu.HBM(a.shape, a.dtype) for a in arrays], input_output_aliases={i: i for i in range(na)},
        compiler_params=pltpu.CompilerParams(has_side_effects=EFFECT),
    )(*arrays, ssem, rsem, *after))


def _pass_copies(grp, rels=(0, 1, 2)):
    def copies(land, x, y, c):
        _, ks = _other_chips(x, y)
        return [(w.half(land[wi], ks[j], c), w.half(land[wi], ks[j], c), (x, y, 1 - c), w.half(land[wi], ks[j], 1 - c))
                for wi, w in enumerate(grp) for j in rels]
    copies.n = len(grp) * len(rels)
    return copies


def _near_copies(grp):
    def copies(land, x, y, c):
        chips, ks = _other_chips(x, y)
        out = []
        for wi, w in enumerate(grp):
            mine = w.half(land[wi], 2 * x + y, c)
            out += [(mine, mine, (*chips[j], c), w.half(land[wi], ks[j], c)) for j in (0, 1)]
        return out
    copies.n = 2 * len(grp)
    return copies


def _far_copies(grp):
    def copies(land, x, y, c):
        chips, ks = _other_chips(x, y)
        out = []
        for wi, w in enumerate(grp):
            for j in (0, 1):
                q = w.quarter(land[wi], ks[j], c, j)
                out.append((q, q, (*chips[1 - j], c), w.quarter(land[wi], ks[2], c, j)))
        return out
    copies.n = 2 * len(grp)
    return copies


def _pair_copies(n, whole=False):
    def copies(refs, x, y, c):
        return [(refs[i] if whole else refs[i].at[:, 1 - c], refs[n + i], (x, y, 1 - c), refs[n + i]) for i in range(n)]
    return copies


def _share_copies(n):
    def copies(refs, x, y, c):
        return [(refs[i].at[c], refs[i].at[c], (x, y, 1 - c), refs[i].at[1 - c]) for i in range(n)]
    return copies


def _gather_conv_w(cw):
    ncw = cw.shape[1]

    def body(cw_ref, out_ref, ssem, rsem):
        x, y, c = _mesh_pos()
        k_me = 2 * x + y
        chips, ks = _other_chips(x, y)
        cols = lambda k: out_ref.at[:, pl.ds(pl.multiple_of(k * ncw, LANES), ncw)]
        cps = [_remote(cw_ref, cols(k_me), ssem.at[j], rsem.at[j], (*chip, c)) for j, chip in enumerate(chips)]
        for cp in cps:
            cp.start()
        for k in range(N_CHIPS):
            @pl.when(k_me == k)
            def _():
                out_ref[:, k * ncw:(k + 1) * ncw] = cw_ref[...]
        for j in range(3):
            _remote(cw_ref, cols(ks[j]), ssem.at[j], rsem.at[j], (*chips[j], c)).wait_recv()
        for cp in cps:
            cp.wait_send()

    return pl.pallas_call(
        body, name="gather_conv_w", in_specs=[VMEM], out_specs=VMEM,
        out_shape=jax.ShapeDtypeStruct((3, N_CHIPS * ncw), F32),
        scratch_shapes=[pltpu.SemaphoreType.DMA((3,)), pltpu.SemaphoreType.DMA((3,))],
    )(cw)


def _grad_tiles(w, n):
    return _tile8(w.R, 512) if w.R <= 512 else w.R // 2, _tile(n, 2048)


def _pair_sum(name, w, pos, grad, got):
    tr, tn = _grad_tiles(w, w.N)

    def body(pos_ref, g_ref, r_ref, o_ref):
        o_ref[...] = (g_ref[...].astype(F32) + r_ref[...].astype(F32)).astype(BF16)

    blk = pl.BlockSpec((None, tr, tn), lambda p, i, j, pos: (p, i, j))
    grid_spec = pltpu.PrefetchScalarGridSpec(
        num_scalar_prefetch=1, grid=(w.P, w.R // tr, w.N // tn),
        in_specs=[pl.BlockSpec((None, None, tr, tn), lambda p, i, j, pos: (p, pos[0], i, j)), blk], out_specs=blk)
    return pl.pallas_call(body, name=name, grid_spec=grid_spec, out_shape=jax.ShapeDtypeStruct((w.P, w.R, w.N), BF16),
                          compiler_params=_cp(("parallel",) * 3))(pos, grad, got)


def _scatter_start(name, ws, pairs):
    nw = len(ws)

    def body(*refs):
        pr, land = refs[:nw], refs[nw:2 * nw]
        ssem, rsem = refs[2 * nw], refs[2 * nw + 1]
        token = refs[4 * nw + 2]
        x, y, c = _mesh_pos()
        chips, ks = _other_chips(x, y)
        for i, w in enumerate(ws):
            for j, chip in enumerate(chips):
                _remote(w.part(pr[i], ks[j]), land[i].at[j], ssem.at[3 * i + j], rsem.at[3 * i + j], (*chip, c)).start()
        token[...] = jnp.zeros_like(token)

    lands = [lax.empty((3, w.R, w.nn), BF16) for w in ws]
    out = pl.pallas_call(
        body, name=name, in_specs=[HBM] * (2 * nw),
        out_specs=[SEM, SEM] + [HBM] * (2 * nw) + [VMEM],
        out_shape=[pltpu.SemaphoreType.DMA((3 * nw,))] * 2 + [pltpu.HBM(a.shape, a.dtype) for a in pairs + lands]
        + [jax.ShapeDtypeStruct((8, LANES), F32)],
        input_output_aliases={i: 2 + i for i in range(2 * nw)},
        compiler_params=pltpu.CompilerParams(has_side_effects=EFFECT),
    )(*[_hbm(a) for a in pairs + lands])
    return out[0], out[1], list(out[2:2 + nw]), list(out[2 + nw:2 + 2 * nw]), out[-1]


def _scatter_wait(name, ws, pairs, lands, ssem, rsem, after):
    nw = len(ws)

    def body(*refs):
        pr, land = refs[:nw], refs[nw:2 * nw]
        ssem_ref, rsem_ref = refs[2 * nw], refs[2 * nw + 1]
        x, y, c = _mesh_pos()
        chips, ks = _other_chips(x, y)
        for i, w in enumerate(ws):
            for j, chip in enumerate(chips):
                cp = _remote(w.part(pr[i], ks[j]), land[i].at[j], ssem_ref.at[3 * i + j], rsem_ref.at[3 * i + j], (*chip, c))
                cp.wait_send()
                cp.wait_recv()

    out = pl.pallas_call(
        body, name=name, in_specs=[HBM] * (2 * nw) + [SEM, SEM] + [ANY] * len(after), out_specs=[HBM] * (2 * nw),
        out_shape=[pltpu.HBM(a.shape, a.dtype) for a in pairs + lands],
        input_output_aliases={i: i for i in range(2 * nw)},
        compiler_params=pltpu.CompilerParams(has_side_effects=EFFECT),
    )(*pairs, *lands, ssem, rsem, *after)
    return list(out[:nw]), list(out[nw:])


def _final_sum(name, w, pos, grad, got, parts):
    tr, tn = _grad_tiles(w, w.nn)
    nbc = w.nn // tn
    if got is None:
        return _final_sum_pair(name, w, pos, grad, parts, tr, tn)

    def body(pos_ref, g_ref, r_ref, p_ref, o_ref):
        acc = g_ref[...].astype(F32) + r_ref[...].astype(F32)
        for j in range(3):
            acc = acc + p_ref[j].astype(F32)
        o_ref[...] = acc

    if w.colshard:
        g_spec = pl.BlockSpec((None, None, tr, tn), lambda i, j, pos: (0, pos[0], i, pos[1] * nbc + j))
        r_spec = pl.BlockSpec((None, tr, tn), lambda i, j, pos: (0, i, pos[1] * nbc + j))
    else:
        g_spec = pl.BlockSpec((None, None, tr, tn), lambda i, j, pos: (pos[1], pos[0], i, j))
        r_spec = pl.BlockSpec((None, tr, tn), lambda i, j, pos: (pos[1], i, j))
    grid_spec = pltpu.PrefetchScalarGridSpec(
        num_scalar_prefetch=1, grid=(w.R // tr, nbc),
        in_specs=[g_spec, r_spec, pl.BlockSpec((3, tr, tn), lambda i, j, pos: (0, i, j))],
        out_specs=pl.BlockSpec((None, tr, tn), lambda i, j, pos: (pos[0], i, j)))
    return pl.pallas_call(body, name=name, grid_spec=grid_spec, out_shape=jax.ShapeDtypeStruct((2, w.R, w.nn), F32),
                          compiler_params=_cp(("parallel",) * 2))(pos, grad, got, parts)


def _final_sum_pair(name, w, pos, pair, parts, tr, tn):
    nbc = w.nn // tn

    def body(pos_ref, g_ref, p_ref, o_ref):
        acc = g_ref[...].astype(F32)
        for j in range(3):
            acc = acc + p_ref[j].astype(F32)
        o_ref[...] = acc

    if w.colshard:
        g_spec = pl.BlockSpec((None, tr, tn), lambda i, j, pos: (0, i, pos[1] * nbc + j))
    else:
        g_spec = pl.BlockSpec((None, tr, tn), lambda i, j, pos: (pos[1], i, j))
    grid_spec = pltpu.PrefetchScalarGridSpec(
        num_scalar_prefetch=1, grid=(w.R // tr, nbc),
        in_specs=[g_spec, pl.BlockSpec((3, tr, tn), lambda i, j, pos: (0, i, j))],
        out_specs=pl.BlockSpec((None, tr, tn), lambda i, j, pos: (pos[0], i, j)))
    return pl.pallas_call(body, name=name, grid_spec=grid_spec, out_shape=jax.ShapeDtypeStruct((2, w.R, w.nn), F32),
                          compiler_params=_cp(("parallel",) * 2))(pos, pair, parts)


def _share_halves(name, ws, halves, deps=()):
    nw = len(ws)

    def body(*refs):
        out = refs[nw + len(deps):2 * nw + len(deps)]
        ssem, rsem = refs[2 * nw + len(deps):]
        x, y, c = _mesh_pos()
        sib = (x, y, 1 - c)
        cps = [_remote(out[i].at[c], out[i].at[c], ssem.at[i], rsem.at[i], sib) for i in range(nw)]
        for cp in cps:
            cp.start()
        for i, cp in enumerate(cps):
            cp.wait_send()
            _remote(out[i].at[1 - c], out[i].at[1 - c], ssem.at[i], rsem.at[i], sib).wait_recv()

    return pl.pallas_call(
        body, name=name, in_specs=[ANY] * (nw + len(deps)), out_specs=[ANY] * nw,
        out_shape=[jax.ShapeDtypeStruct(h.shape, F32) for h in halves],
        scratch_shapes=[pltpu.SemaphoreType.DMA((nw,)), pltpu.SemaphoreType.DMA((nw,))],
        input_output_aliases={i: i for i in range(nw)},
    )(*halves, *deps)


VEC_ROWS = 16


def _vector_step(d, n_conv, parts, params, deps=()):
    ncw = params[2][0].shape[1]
    n_par = len(params)

    def body(*refs):
        dg1, dba, dbb, dcw, dcb, dps, dg2, dgf, lc = refs[:9]
        wmv = refs[9:9 + 3 * n_par]
        refs = refs[9 + 3 * n_par + len(deps):]
        outs = refs[:4 * n_par]
        loss_ref = refs[4 * n_par]
        snd, got, ssem, rsem = refs[4 * n_par + 1:]
        x, y, c = _mesh_pos()
        me = 4 * x + 2 * y + c
        snd[...] = jnp.zeros_like(snd)
        for row, ref in ((0, dg1), (1, dba), (2, dbb), (3, dps), (4, dg2), (5, dgf), (6, lc)):
            snd[row:row + 1, :] = ref[...]
        snd[7:8, :n_conv] = dcb[...]
        snd[8:11, :n_conv] = dcw[...]
        cps = []
        for r in range(1, N_DEV):
            peer = tuple(1 - p if (r >> b) & 1 else p for p, b in ((x, 2), (y, 1), (c, 0)))
            cps.append(_remote(snd, got.at[me], ssem.at[r - 1], rsem.at[r - 1], peer))
        for cp in cps:
            cp.start()
        got[me] = snd[...]
        for r in range(1, N_DEV):
            peer = tuple(1 - p if (r >> b) & 1 else p for p, b in ((x, 2), (y, 1), (c, 0)))
            _remote(snd, got.at[4 * peer[0] + 2 * peer[1] + peer[2]], ssem.at[r - 1], rsem.at[r - 1], peer).wait_recv()
        for cp in cps:
            cp.wait_send()
        tot = got[0]
        for dev in range(1, N_DEV):
            tot = tot + got[dev]
        loss_ref[...] = jnp.sum(tot[6:7, :], axis=1, keepdims=True)
        k_me = 2 * x + y
        g_cw = jnp.zeros((3, ncw), F32)
        for k in range(N_CHIPS):
            g_cw = g_cw + jnp.where(k_me == k, tot[8:11, k * ncw:(k + 1) * ncw], 0.0)
        grads = [tot[0:1, :], jnp.concatenate([tot[1:2, :], tot[2:3, :]], axis=1), g_cw, tot[7:8, :n_conv],
                 tot[3:4, :], tot[4:5, :], tot[5:6, :]]
        for i, g in enumerate(grads):
            w_ref, m_ref, v_ref = wmv[3 * i:3 * i + 3]
            delta, nm, nv = _adamw_math(w_ref[...], g, m_ref[...], v_ref[...])
            outs[4 * i][...] = g
            outs[4 * i + 1][...] = delta
            outs[4 * i + 2][...] = nm
            outs[4 * i + 3][...] = nv

    args = list(parts)
    out_shape = []
    for w, m, v in params:
        args += [w, m, v]
        out_shape += [jax.ShapeDtypeStruct(w.shape, F32)] * 4
    out_shape.append(jax.ShapeDtypeStruct((1, 1), F32))
    return pl.pallas_call(
        body, name="vector_params_step", in_specs=[VMEM] * len(args) + [ANY] * len(deps),
        out_specs=[VMEM] * len(out_shape), out_shape=out_shape,
        scratch_shapes=[pltpu.VMEM((VEC_ROWS, d), F32), pltpu.VMEM((N_DEV, VEC_ROWS, d), F32),
                        pltpu.SemaphoreType.DMA((N_DEV - 1,)), pltpu.SemaphoreType.DMA((N_DEV - 1,))],
        compiler_params=pltpu.CompilerParams(vmem_limit_bytes=VMEM_LIMIT),
    )(*args, *deps)


def kernel(x, norm1_g, w_in, b_gate, conv_w, conv_b, w_a_out, w_pool, pool_scale, w_o, norm2_g, w_ffn_gate, w_ffn_up, w_ffn_down, final_g, loss_target, m_norm1_g, m_w_in, m_b_gate, m_conv_w, m_conv_b, m_w_a_out, m_w_pool, m_pool_scale, m_w_o, m_norm2_g, m_w_ffn_gate, m_w_ffn_up, m_w_ffn_down, m_final_g, v_norm1_g, v_w_in, v_b_gate, v_conv_w, v_conv_b, v_w_a_out, v_w_pool, v_pool_scale, v_w_o, v_norm2_g, v_w_ffn_gate, v_w_ffn_up, v_w_ffn_down, v_final_g):
    t, d = x.shape[1], x.shape[2]
    n_conv = conv_b.shape[1]
    n_groups, pool_cg, pool_dg = w_pool.shape[1], w_pool.shape[2], N_CHIPS * w_pool.shape[3]
    d_ff = N_CHIPS * w_ffn_gate.shape[2]
    assert n_conv // n_groups == pool_cg and n_conv % (n_groups * MIX_COLS) == 0 and n_groups == len(POOL_WINDOWS)

    big = {"w_in": (w_in, m_w_in, v_w_in), "w_a_out": (w_a_out, m_w_a_out, v_w_a_out), "w_pool": (w_pool, m_w_pool, v_w_pool),
           "w_o": (w_o, m_w_o, v_w_o), "w_ffn_gate": (w_ffn_gate, m_w_ffn_gate, v_w_ffn_gate),
           "w_ffn_up": (w_ffn_up, m_w_ffn_up, v_w_ffn_up), "w_ffn_down": (w_ffn_down, m_w_ffn_down, v_w_ffn_down)}
    colshard = {"w_in": True, "w_a_out": True, "w_pool": True, "w_o": False, "w_ffn_gate": True, "w_ffn_up": True,
                "w_ffn_down": False}
    names = list(big)
    shard2d = {n: big[n][0].reshape(-1, big[n][0].shape[-1]) for n in names}
    ws = [_Weight(n, *shard2d[n].shape, colshard[n]) for n in names]

    xs, tgt = x[0], loss_target[0]
    cw_loc = conv_w[0]
    pos = jnp.stack([lax.axis_index("c"), 2 * lax.axis_index("x") + lax.axis_index("y")]).astype(jnp.int32)
    by_name = {w.name: w for w in ws}
    groups = [[by_name[n] for n in g] for g in (["w_in"], ["w_a_out", "w_pool", "w_o"], ["w_ffn_gate"], ["w_ffn_up"],
                                                 ["w_ffn_down"])]
    first = [sum(len(g) for g in groups[:i]) for i in range(len(groups))]
    rgroups = [groups[0], groups[1], groups[2] + groups[3], groups[4]]

    cw_full = _gather_conv_w(cw_loc)
    cast = lambda w, dep: _cast_place(f"cast_{w.name}", w, pos, shard2d[w.name].reshape(2, w.R, w.nn), deps=[dep])
    chips, ks = _other_chips(lax.axis_index("x"), lax.axis_index("y"))
    kvec = jnp.stack([pos[1], *ks]).astype(jnp.int32)
    full = {}

    def start(name, arrays, copies, after=()):
        ssem, rsem, arrays, token = _split_start(name, arrays, copies.n, copies, after)
        return name, arrays, ssem, rsem, copies, token

    def wait(started, after):
        name, arrays, ssem, rsem, copies, _ = started
        return _split_wait(name + "_wait", arrays, ssem, rsem, copies, after)

    def pass_on(g, got, after=()):
        return start(f"pass_{g}", got, _pass_copies(groups[g]), after)

    def passed(g, st, after=None):
        got = wait(st, [st[5]] if after is None else after)
        full.update({w.name: a.reshape(w.P * 2 * w.R, w.N) for w, a in zip(groups[g], got)})

    near = start("near_0", [cast(w, cw_full) for w in groups[0]], _near_copies(groups[0]))
    rest = [cast(w, near[5]) for grp in groups[1:] for w in grp]
    h1 = _rms_fwd("norm1_fwd", xs, norm1_g, deps=[near[5]])
    proj = _proj_piece("proj_own", h1, shard2d["w_in"], None, kvec, 0, 1, deps=rest)
    got = wait(near, [proj])
    far = start("far_0", got, _far_copies(groups[0]))
    sems_b, lands_b, tok_b = _gather_start("gather_start_b", groups[1:2], rest[:3], after=[far[5]])
    st = start("pass_near_0", far[1], _pass_copies(groups[0], (0, 1)), [tok_b])
    got = wait(st, [st[5]])
    proj = _proj_piece("proj_near", h1, got[0].reshape(-1, groups[0][0].N), proj, kvec, 1, 2)
    st = start("pass_far_0", wait((far[0], got) + far[2:], [proj]), _pass_copies(groups[0], (2,)))
    got = wait(st, [st[5]])
    w_in_full = got[0].reshape(-1, groups[0][0].N)
    proj = _proj_piece("proj_far", h1, w_in_full, proj, kvec, 3, 1)
    got = _gather_wait("gather_wait_1", groups[1], lands_b, *sems_b[0], proj)
    near_g = start("near_2", rest[3:4], _near_copies(groups[2]), got)
    st = pass_on(1, got, [near_g[5]])
    z, p = _mixer_fwd("mixer_fwd", proj, cw_full, conv_b, n_conv, n_groups, deps=[st[5]])
    passed(1, st, [z])
    wp_full = full["w_pool"].reshape(n_groups, pool_cg, pool_dg)
    ya = _mm_nn("conv_out", z, full["w_a_out"], BF16)
    yb = _gmm_nn("pool_out", p, wp_full, BF16)
    merged = _merge_fwd("merge_fwd", proj, b_gate, ya, yb, pool_scale)
    far_g = start("far_2", wait(near_g, [merged]), _far_copies(groups[2]))
    near_u = start("near_3", rest[4:5], _near_copies(groups[3]), [far_g[5]])
    x2 = _mm_nn("mix_out", merged, full["w_o"], F32, add=xs, deps=[near_u[5]])
    st = pass_on(2, wait(far_g, [x2]))
    h2 = _rms_fwd("norm2_fwd", x2, norm2_g, deps=[st[5]])
    passed(2, st, [h2])
    gate = _mm_nn("ffn_gate_a", h2, full["w_ffn_gate"], BF16, part=(0, 2))
    far_u = start("far_3", wait(near_u, [gate]), _far_copies(groups[3]))
    near_d = start("near_4", rest[5:6], _near_copies(groups[4]), [far_u[5]])
    gate = _mm_nn("ffn_gate_b", h2, full["w_ffn_gate"], BF16, part=(1, 2), prev=gate, deps=[near_d[5]])
    passed(3, pass_on(3, wait(far_u, [gate])))
    up_act = _ffn_up_act("ffn_up_act_a", h2, full["w_ffn_up"], gate, part=(0, 2))
    far_d = start("far_4", wait(near_d, [up_act[0]]), _far_copies(groups[4]))
    up, act = _ffn_up_act("ffn_up_act_b", h2, full["w_ffn_up"], gate, part=(1, 2), prev=up_act, deps=[far_d[5]])
    passed(4, pass_on(4, wait(far_d, [act])))
    x3 = _mm_nn("ffn_down", act, full["w_ffn_down"], F32, add=x2, tk=d_ff // 4)

    pending = {}

    def pair_start(g, grads):
        grp = rgroups[g]
        gcan = [grads[w.name].reshape(w.P, 2, w.R, w.N) for w in grp]
        slots = [lax.empty((w.P, w.R, w.N), BF16) for w in grp]
        pending[g] = _split_start(f"pair_start_{g}", gcan + slots, len(grp), _pair_copies(len(grp)))
        return pending[g][3]

    def scatter_start(g, after):
        grp = rgroups[g]
        n = len(grp)
        ssem, rsem, arrs, _ = pending[g]
        arrs = _split_wait(f"pair_wait_{g}", arrs, ssem, rsem, _pair_copies(n), after)
        gcan, sib = arrs[:n], arrs[n:]
        pairs = [_pair_sum(f"pair_sum_{w.name}", w, pos, a, s) for w, a, s in zip(grp, gcan, sib)]
        ssem, rsem, pairs, slots, token = _scatter_start(f"scatter_start_{g}", grp, pairs)
        pending[g] = (gcan, sib, pairs, slots, ssem, rsem)
        return token

    def pair_start_halves(g, ab, deps):
        grp = rgroups[g]
        sent = [_mm_tn_half(f"d{w.name}_sib", a, b, pos, False, deps=deps if i == 0 else ()) for i, (w, (a, b)) in enumerate(zip(grp, ab))]
        slots = [lax.empty((1, w.R, w.N), BF16) for w in grp]
        pending[g] = _split_start(f"pair_start_{g}", sent + slots, len(grp), _pair_copies(len(grp), whole=True))
        return pending[g][3]

    def scatter_start_halves(g, ab, after):
        grp = rgroups[g]
        n = len(grp)
        ssem, rsem, arrs, _ = pending[g]
        arrs = _split_wait(f"pair_wait_{g}", arrs, ssem, rsem, _pair_copies(n, whole=True), after)
        pairs = [_mm_tn_half(f"d{w.name}_own", a, b, pos, True, add=s) for w, (a, b), s in zip(grp, ab, arrs[n:])]
        ssem, rsem, pairs, slots, token = _scatter_start(f"scatter_start_{g}", grp, pairs)
        pending[g] = (None, None, pairs, slots, ssem, rsem)
        return token

    def reduce_finish(g, after):
        grp = rgroups[g]
        gcan, sib, pairs, slots, ssem, rsem = pending[g]
        pairs, parts = _scatter_wait(f"scatter_wait_{g}", grp, pairs, slots, ssem, rsem, after)
        if gcan is None:
            return [_final_sum(f"final_sum_{w.name}", w, pos, a, None, q) for w, a, q in zip(grp, pairs, parts)]
        return [_final_sum(f"final_sum_{w.name}", w, pos, a, s, q) for w, a, s, q in zip(grp, gcan, sib, parts)]

    grads = {}
    dx3, dx3b, d_gf, loss_cols = _final_bwd("final_bwd", x3, final_g.reshape(1, d), tgt)
    dgate, dup = _ffn_bwd("ffn_bwd", dx3b, full["w_ffn_down"], gate, up)
    grads["w_ffn_down"] = _mm_tn("dw_ffn_down", act, dx3b, BF16)
    tok = pair_start(3, grads)
    dh2 = _mm_nt("d_h2", [(dgate, full["w_ffn_gate"]), (dup, full["w_ffn_up"])], F32, tk=d_ff // 4, deps=[tok])
    tok = scatter_start(3, [dh2])
    tok = pair_start_halves(2, [(h2, dgate), (h2, dup)], [tok])
    dx2, dx2b, d_g2 = _rms_bwd("norm2_bwd", x2, norm2_g, dh2, dx3, True, deps=[tok])
    dmerged = _mm_nt("d_merged", [(dx2b, full["w_o"])], BF16, tk=d)
    grads["w_o"] = _mm_tn("dw_o", merged, dx2b, BF16)
    tok = scatter_start_halves(2, [(h2, dgate), (h2, dup)], [grads["w_o"]])
    dya, dyb, dproj, d_bga, d_bgb, d_ps = _merge_bwd("merge_bwd", dmerged, proj, b_gate, ya, yb, pool_scale, deps=[tok])
    dz = _mm_nt("d_z", [(dya, full["w_a_out"])], BF16, tk=d)
    grads["w_a_out"] = _mm_tn("dw_a_out", z, dya, BF16)
    dp = _gmm_nt("d_pool", dyb, wp_full, BF16)
    grads["w_pool"] = _gmm_tn("dw_pool", p, dyb, n_groups, BF16)
    tok = pair_start(1, grads)
    dproj, d_cw, d_cb = _mixer_bwd("mixer_bwd", dz, dp, proj, cw_full, conv_b, dproj, n_conv, n_groups, deps=[tok])
    tok = scatter_start(1, [dproj])
    tok = pair_start_halves(0, [(h1, dproj)], [tok])
    dh1 = _mm_nt("d_h1", [(dproj, w_in_full)], F32, tk=proj.shape[1] // 4, deps=[tok])
    tok = scatter_start_halves(0, [(h1, dproj)], [dh1])
    grad_x, d_g1 = _rms_bwd("norm1_bwd", xs, norm1_g, dh1, dx2, False, deps=[tok])

    g_big, d_big, m_big, v_big = {}, {}, {}, {}

    def update(wsub, shared):
        out = []
        for w, g in zip(wsub, shared):
            wt, mt, vt = big[w.name]
            g2 = g.reshape(2 * w.R, w.nn)
            go, dl, nm, nv = _adamw(f"adamw_{w.name}", shard2d[w.name], g2, mt.reshape(g2.shape), vt.reshape(g2.shape))
            g_big[w.name], d_big[w.name], m_big[w.name], v_big[w.name] = (a.reshape(wt.shape) for a in (go, dl, nm, nv))
            out.append(nv)
        return out

    after = [grad_x]
    started = []
    for g in (3, 2, 1):
        halves = reduce_finish(g, after)
        share = _share_copies(len(halves))
        ssem, rsem, halves, tok = _split_start(f"share_start_{g}", halves, len(halves), share)
        started.append((g, ssem, rsem, halves, share))
        after = [tok]
    for g, ssem, rsem, halves, share in started:
        after = update(rgroups[g], _split_wait(f"share_wait_{g}", halves, ssem, rsem, share, after))
    after = update(rgroups[0], _share_halves("share_halves_w_in", rgroups[0], reduce_finish(0, after)))

    vec_names = ["norm1_g", "b_gate", "conv_w", "conv_b", "pool_scale", "norm2_g", "final_g"]
    vec = {"norm1_g": (norm1_g, m_norm1_g, v_norm1_g), "b_gate": (b_gate, m_b_gate, v_b_gate),
           "conv_w": (cw_loc, m_conv_w[0], v_conv_w[0]), "conv_b": (conv_b, m_conv_b, v_conv_b),
           "pool_scale": (pool_scale, m_pool_scale, v_pool_scale), "norm2_g": (norm2_g, m_norm2_g, v_norm2_g),
           "final_g": tuple(a.reshape(1, d) for a in (final_g, m_final_g, v_final_g))}
    vout = _vector_step(d, n_conv, [d_g1, d_bga, d_bgb, d_cw, d_cb, d_ps, d_g2, d_gf, loss_cols],
                        [vec[n] for n in vec_names], deps=after)

    shapes = {"conv_w": conv_w.shape, "final_g": final_g.shape}
    g_vec, d_vec, m_vec, v_vec = ({n: vout[4 * i + q].reshape(shapes.get(n, vec[n][0].shape)) for i, n in enumerate(vec_names)}
                                  for q in range(4))
    loss = vout[-1].reshape(())

    order = ["norm1_g", "w_in", "b_gate", "conv_w", "conv_b", "w_a_out", "w_pool", "pool_scale", "w_o", "norm2_g",
             "w_ffn_gate", "w_ffn_up", "w_ffn_down", "final_g"]
    pick = lambda vecs, bigs: [vecs[n] if n in vecs else bigs[n] for n in order]
    return (loss, grad_x.reshape(x.shape), *pick(g_vec, g_big), *pick(d_vec, d_big), *pick(m_vec, m_big),
            *pick(v_vec, v_big))
```

```python
import functools

import jax
import jax.numpy as jnp
from jax import lax
from jax.experimental import pallas as pl
from jax.experimental.pallas import tpu as pltpu

F32, BF16 = jnp.float32, jnp.bfloat16
MESH = pl.DeviceIdType.MESH
ANY = pl.BlockSpec(memory_space=pl.ANY)
VMEM = pl.BlockSpec(memory_space=pltpu.VMEM)
HBM = pl.BlockSpec(memory_space=pltpu.HBM)
SEM = pl.BlockSpec(memory_space=pltpu.SEMAPHORE)
EFFECT = pltpu.SideEffectType.DATAFLOW_SIDE_EFFECTING

EPS = 1e-6
POOL_WINDOWS = (2, 4, 8, 16)
ADAM_LR, ADAM_B1, ADAM_B2, ADAM_EPS, ADAM_WD, ADAM_STEP = 0.001, 0.9, 0.999, 1e-08, 0.01, 10

V7X_VMEM_BYTES = 64 * 1024 * 1024
VMEM_LIMIT = V7X_VMEM_BYTES * 3 // 4
LANES = 128
COL_TILE = 8 * LANES
N_CHIPS = 4
N_DEV = 8

_DIMS = {
    "nn": (((1,), (0,)), ((), ())),
    "nt": (((1,), (1,)), ((), ())),
    "tn": (((0,), (0,)), ((), ())),
}


def _cp(sem):
    return pltpu.CompilerParams(dimension_semantics=sem, vmem_limit_bytes=VMEM_LIMIT)


def _mesh_pos():
    return lax.axis_index("x"), lax.axis_index("y"), lax.axis_index("c")


def _mm(name, pairs, *, mode, grid, out_shape, o_spec, nk=1, kaxis=None, add=None, deps=(), prev=None):
    npair = len(pairs)
    has_add = add is not None

    def body(*refs):
        ab = refs[: 2 * npair]
        pos = 2 * npair
        add_ref = refs[pos] if has_add else None
        pos += int(has_add) + len(deps) + (prev is not None)
        o_ref = refs[pos]
        acc_ref = refs[pos + 1] if nk > 1 else None
        d = None
        for p in range(npair):
            t = lax.dot_general(ab[2 * p][...], ab[2 * p + 1][...], _DIMS[mode], preferred_element_type=F32)
            d = t if d is None else d + t
        if nk == 1:
            if has_add:
                d = d + add_ref[...].astype(F32)
            o_ref[...] = d.astype(o_ref.dtype)
        else:
            k = pl.program_id(kaxis)

            @pl.when(k == 0)
            def _():
                acc_ref[...] = d

            @pl.when(k > 0)
            def _():
                acc_ref[...] += d

            @pl.when(k == nk - 1)
            def _():
                r = acc_ref[...]
                if has_add:
                    r = r + add_ref[...].astype(F32)
                o_ref[...] = r.astype(o_ref.dtype)

    args, specs = [], []
    for a, a_spec, b, b_spec in pairs:
        args += [a, b]
        specs += [a_spec, b_spec]
    if has_add:
        args.append(add[0])
        specs.append(add[1])
    args += list(deps)
    specs += [ANY] * len(deps)
    aliases = {}
    if prev is not None:
        aliases = {len(args): 0}
        args.append(prev)
        specs.append(ANY)
    scratch = []
    if nk > 1:
        blk = [d for d in o_spec.block_shape if d is not None]
        scratch = [pltpu.VMEM(tuple(blk), F32)]
    sem = tuple("arbitrary" if (nk > 1 and ax == kaxis) else "parallel" for ax in range(len(grid)))
    return pl.pallas_call(
        body, name=name, grid=grid, in_specs=specs, out_specs=o_spec, out_shape=out_shape,
        scratch_shapes=scratch, input_output_aliases=aliases, compiler_params=_cp(sem),
    )(*args)


def _tile_span(n_tiles, part):
    if part is None:
        return 0, n_tiles
    p, of = part
    return p * n_tiles // of, (p + 1) * n_tiles // of


def _tile(n, pref):
    if n <= pref:
        return n
    for t in range(pref, 0, -LANES):
        if t % LANES == 0 and n % t == 0:
            return t
    raise ValueError(f"no tile for {n}")


def _mm_nn(name, a, b, out_dtype, add=None, tk=None, deps=(), part=None, prev=None):
    m, kk = a.shape
    n = b.shape[1]
    tm, tn = _tile(m, 1024), _tile(n, COL_TILE)
    out_shape = jax.ShapeDtypeStruct((m, n), out_dtype)
    if tk is None or tk == kk:
        j0, j1 = _tile_span(n // tn, part)
        grid = (m // tm, j1 - j0)
        pairs = [(a, pl.BlockSpec((tm, kk), lambda i, j: (i, 0)), b, pl.BlockSpec((kk, tn), lambda i, j: (0, j0 + j)))]
        o_spec = pl.BlockSpec((tm, tn), lambda i, j: (i, j0 + j))
        add_ = None if add is None else (add, pl.BlockSpec((tm, tn), lambda i, j: (i, j0 + j)))
        return _mm(name, pairs, mode="nn", grid=grid, out_shape=out_shape, o_spec=o_spec, add=add_, deps=deps, prev=prev)
    tn = _tile(n, 1024)
    nk = kk // tk
    grid = (m // tm, n // tn, nk)
    pairs = [(a, pl.BlockSpec((tm, tk), lambda i, j, k: (i, k)), b, pl.BlockSpec((tk, tn), lambda i, j, k: (k, j)))]
    o_spec = pl.BlockSpec((tm, tn), lambda i, j, k: (i, j))
    add_ = None if add is None else (add, pl.BlockSpec((tm, tn), lambda i, j, k: (i, j)))
    return _mm(name, pairs, mode="nn", grid=grid, out_shape=out_shape, o_spec=o_spec, nk=nk, kaxis=2, add=add_, deps=deps)


def _mm_nt(name, abs_, out_dtype, tk, deps=()):
    m, kk = abs_[0][0].shape
    n = abs_[0][1].shape[0]
    tm = _tile(m, 1024)
    nk = kk // tk
    tn = _tile(n, COL_TILE if nk == 1 else 1024)
    out_shape = jax.ShapeDtypeStruct((m, n), out_dtype)
    if nk == 1:
        grid = (m // tm, n // tn)
        pairs = [(a, pl.BlockSpec((tm, kk), lambda i, j: (i, 0)), b, pl.BlockSpec((tn, kk), lambda i, j: (j, 0)))
                 for a, b in abs_]
        o_spec = pl.BlockSpec((tm, tn), lambda i, j: (i, j))
        return _mm(name, pairs, mode="nt", grid=grid, out_shape=out_shape, o_spec=o_spec, deps=deps)
    grid = (m // tm, n // tn, nk)
    pairs = [(a, pl.BlockSpec((tm, tk), lambda i, j, k: (i, k)), b, pl.BlockSpec((tn, tk), lambda i, j, k: (j, k)))
             for a, b in abs_]
    o_spec = pl.BlockSpec((tm, tn), lambda i, j, k: (i, j))
    return _mm(name, pairs, mode="nt", grid=grid, out_shape=out_shape, o_spec=o_spec, nk=nk, kaxis=2, deps=deps)


def _mm_tn(name, a, b, out_dtype, deps=()):
    t, m = a.shape
    n = b.shape[1]
    tm, tn = _tile(m, 512), _tile(n, 2048)
    if n > m:
        grid = (n // tn, m // tm)
        a_map, b_map, o_map = (lambda j, i: (0, i)), (lambda j, i: (0, j)), (lambda j, i: (i, j))
    else:
        grid = (m // tm, n // tn)
        a_map, b_map, o_map = (lambda i, j: (0, i)), (lambda i, j: (0, j)), (lambda i, j: (i, j))
    pairs = [(a, pl.BlockSpec((t, tm), a_map), b, pl.BlockSpec((t, tn), b_map))]
    o_spec = pl.BlockSpec((tm, tn), o_map)
    return _mm(name, pairs, mode="tn", grid=grid, out_shape=jax.ShapeDtypeStruct((m, n), out_dtype), o_spec=o_spec,
               deps=deps)


def _mm_tn_half(name, a, b, pos, mine, add=None, deps=()):
    t, m = a.shape
    r, n = m // 2, b.shape[1]
    tm, tn = _tile(r, 512), _tile(n, 2048)
    nbi = r // tm
    half = (lambda pos: pos[0]) if mine else (lambda pos: 1 - pos[0])
    if n > r:
        grid, ij = (n // tn, nbi), (lambda g0, g1: (g1, g0))
    else:
        grid, ij = (nbi, n // tn), (lambda g0, g1: (g0, g1))
    has_add = add is not None

    def body(pos_ref, a_ref, b_ref, *rest):
        d = lax.dot_general(a_ref[...], b_ref[...], _DIMS["tn"], preferred_element_type=F32)
        if has_add:
            d = d + rest[0][...].astype(F32)
        rest[-1][...] = d.astype(BF16)

    o_spec = pl.BlockSpec((None, tm, tn), lambda g0, g1, pos: (0, *ij(g0, g1)))
    grid_spec = pltpu.PrefetchScalarGridSpec(
        num_scalar_prefetch=1, grid=grid,
        in_specs=[pl.BlockSpec((t, tm), lambda g0, g1, pos: (0, half(pos) * nbi + ij(g0, g1)[0])),
                  pl.BlockSpec((t, tn), lambda g0, g1, pos: (0, ij(g0, g1)[1]))]
        + ([o_spec] if has_add else []) + [ANY] * len(deps),
        out_specs=o_spec)
    return pl.pallas_call(body, name=name, grid_spec=grid_spec, out_shape=jax.ShapeDtypeStruct((1, r, n), BF16),
                          compiler_params=_cp(("parallel",) * 2))(pos, a, b, *([add] if has_add else []), *deps)


def _proj_piece(name, h, w, prev, kvec, base, count, deps=()):
    t, kk = h.shape
    own = w.dtype == F32
    nn = w.shape[1] if own else w.shape[1] // N_CHIPS
    tm, tn = _tile(t, 1024), _tile(nn, COL_TILE)
    nb = nn // tn

    def body(kv_ref, h_ref, w_ref, *rest):
        rest[-1][...] = lax.dot_general(h_ref[...], w_ref[...].astype(BF16), _DIMS["nn"],
                                        preferred_element_type=F32).astype(BF16)

    cols = lambda s, i, j, kv: (0, j) if own else (0, kv[base + s] * nb + j)
    extra = ([] if prev is None else [prev]) + list(deps)
    grid_spec = pltpu.PrefetchScalarGridSpec(
        num_scalar_prefetch=1, grid=(count, t // tm, nb),
        in_specs=[pl.BlockSpec((tm, kk), lambda s, i, j, kv: (i, 0)), pl.BlockSpec((kk, tn), cols)] + [ANY] * len(extra),
        out_specs=pl.BlockSpec((tm, tn), lambda s, i, j, kv: (i, kv[base + s] * nb + j)))
    return pl.pallas_call(body, name=name, grid_spec=grid_spec, out_shape=jax.ShapeDtypeStruct((t, N_CHIPS * nn), BF16),
                          input_output_aliases={} if prev is None else {3: 0},
                          compiler_params=_cp(("parallel",) * 3))(kvec, h, w, *extra)


def _gmm_nn(name, p, w, out_dtype):
    t = p.shape[0]
    g, cg, dg = w.shape
    tm = _tile(t, 1024)
    pairs = [(p, pl.BlockSpec((tm, cg), lambda i, j: (i, j)), w, pl.BlockSpec((None, cg, dg), lambda i, j: (j, 0, 0)))]
    o_spec = pl.BlockSpec((tm, dg), lambda i, j: (i, j))
    return _mm(name, pairs, mode="nn", grid=(t // tm, g), out_shape=jax.ShapeDtypeStruct((t, g * dg), out_dtype),
               o_spec=o_spec)


def _gmm_nt(name, dy, w, out_dtype):
    t = dy.shape[0]
    g, cg, dg = w.shape
    tm = _tile(t, 1024)
    pairs = [(dy, pl.BlockSpec((tm, dg), lambda i, j: (i, j)), w, pl.BlockSpec((None, cg, dg), lambda i, j: (j, 0, 0)))]
    o_spec = pl.BlockSpec((tm, cg), lambda i, j: (i, j))
    return _mm(name, pairs, mode="nt", grid=(t // tm, g), out_shape=jax.ShapeDtypeStruct((t, g * cg), out_dtype),
               o_spec=o_spec)


def _gmm_tn(name, p, dy, g, out_dtype):
    t = p.shape[0]
    cg, dg = p.shape[1] // g, dy.shape[1] // g
    pairs = [(p, pl.BlockSpec((t, cg), lambda j: (0, j)), dy, pl.BlockSpec((t, dg), lambda j: (0, j)))]
    o_spec = pl.BlockSpec((None, cg, dg), lambda j: (j, 0, 0))
    return _mm(name, pairs, mode="tn", grid=(g,), out_shape=jax.ShapeDtypeStruct((g, cg, dg), out_dtype), o_spec=o_spec)


ROW_TILE = 256


def _rows(t):
    return _tile8(t, ROW_TILE)


def _tile8(n, pref):
    if n <= pref:
        return n
    for t in range(pref, 0, -8):
        if n % t == 0:
            return t
    raise ValueError(f"no row tile for {n}")


def _cast_place(name, w, pos, shard, deps=()):
    tr = _tile8(w.R, 512)
    if w.colshard:
        o_map = lambda h, i, pos: (0, h, i, pos[1])
    else:
        o_map = lambda h, i, pos: (pos[1], h, i, 0)

    def body(pos_ref, w_ref, *rest):
        rest[-1][...] = w_ref[...].astype(BF16)

    grid_spec = pltpu.PrefetchScalarGridSpec(
        num_scalar_prefetch=1, grid=(2, w.R // tr),
        in_specs=[pl.BlockSpec((None, tr, w.nn), lambda h, i, pos: (h, i, 0))] + [ANY] * len(deps),
        out_specs=pl.BlockSpec((None, None, tr, w.nn), o_map))
    return pl.pallas_call(body, name=name, grid_spec=grid_spec, out_shape=jax.ShapeDtypeStruct((w.P, 2, w.R, w.N), BF16),
                          compiler_params=_cp(("parallel", "parallel")))(pos, shard, *deps)


def _rms_fwd(name, x, g, deps=()):
    t, d = x.shape
    tm = _rows(t)

    def body(x_ref, g_ref, *rest):
        xf = x_ref[...]
        r = lax.rsqrt(jnp.mean(xf * xf, axis=-1, keepdims=True) + EPS)
        rest[-1][...] = (xf * r * g_ref[...]).astype(BF16)

    return pl.pallas_call(
        body, name=name, grid=(t // tm,),
        in_specs=[pl.BlockSpec((tm, d), lambda i: (i, 0)), pl.BlockSpec((1, d), lambda i: (0, 0))] + [ANY] * len(deps),
        out_specs=pl.BlockSpec((tm, d), lambda i: (i, 0)), out_shape=jax.ShapeDtypeStruct((t, d), BF16),
        compiler_params=_cp(("parallel",)),
    )(x, g, *deps)


def _rms_bwd(name, x, g, dh, dres, want_bf16, deps=()):
    t, d = x.shape
    tm = _rows(t)

    def body(x_ref, g_ref, dh_ref, dres_ref, *rest):
        rest = rest[len(deps):]
        dx_ref, rest = rest[0], rest[1:]
        dg_ref = rest[-1]
        xf = x_ref[...]
        r = lax.rsqrt(jnp.mean(xf * xf, axis=-1, keepdims=True) + EPS)
        xh = xf * r
        dhf = dh_ref[...].astype(F32)
        dxh = dhf * g_ref[...]
        m = jnp.mean(dxh * xh, axis=-1, keepdims=True)
        dx = dres_ref[...] + r * (dxh - xh * m)
        dx_ref[...] = dx
        if want_bf16:
            rest[0][...] = dx.astype(BF16)

        @pl.when(pl.program_id(0) == 0)
        def _():
            dg_ref[...] = jnp.zeros_like(dg_ref)

        dg_ref[...] += jnp.sum(dhf * xh, axis=0, keepdims=True)

    row = pl.BlockSpec((tm, d), lambda i: (i, 0))
    vec = pl.BlockSpec((1, d), lambda i: (0, 0))
    out_specs = [row] + ([row] if want_bf16 else []) + [vec]
    out_shape = ([jax.ShapeDtypeStruct((t, d), F32)] + ([jax.ShapeDtypeStruct((t, d), BF16)] if want_bf16 else [])
                 + [jax.ShapeDtypeStruct((1, d), F32)])
    return pl.pallas_call(body, name=name, grid=(t // tm,), in_specs=[row, vec, row, row] + [ANY] * len(deps),
                          out_specs=out_specs, out_shape=out_shape, compiler_params=_cp(("arbitrary",)))(x, g, dh, dres, *deps)


def _final_bwd(name, x3, gf, tgt):
    t, d = x3.shape
    tm = _rows(t)

    def body(x_ref, g_ref, t_ref, dx_ref, dxb_ref, dg_ref, lc_ref):
        xf = x_ref[...]
        g = g_ref[...]
        r = lax.rsqrt(jnp.mean(xf * xf, axis=-1, keepdims=True) + EPS)
        xh = xf * r
        diff = xh * g - t_ref[...]
        dy = diff * (1.0 / d)
        dxh = dy * g
        m = jnp.mean(dxh * xh, axis=-1, keepdims=True)
        dx = r * (dxh - xh * m)
        dx_ref[...] = dx
        dxb_ref[...] = dx.astype(BF16)

        @pl.when(pl.program_id(0) == 0)
        def _():
            dg_ref[...] = jnp.zeros_like(dg_ref)
            lc_ref[...] = jnp.zeros_like(lc_ref)

        dg_ref[...] += jnp.sum(dy * xh, axis=0, keepdims=True)
        lc_ref[...] += jnp.sum(diff * diff, axis=0, keepdims=True) * (0.5 / d)

    row = pl.BlockSpec((tm, d), lambda i: (i, 0))
    vec = pl.BlockSpec((1, d), lambda i: (0, 0))
    return pl.pallas_call(
        body, name=name, grid=(t // tm,), in_specs=[row, vec, row], out_specs=[row, row, vec, vec],
        out_shape=[jax.ShapeDtypeStruct((t, d), F32), jax.ShapeDtypeStruct((t, d), BF16),
                   jax.ShapeDtypeStruct((1, d), F32), jax.ShapeDtypeStruct((1, d), F32)],
        compiler_params=_cp(("arbitrary",)),
    )(x3, gf, tgt)


def _shift_down(v, k, t_idx):
    return jnp.where(t_idx >= k, pltpu.roll(v, k, 0), 0.0)


def _shift_up(v, k, t_idx):
    n = v.shape[0]
    return jnp.where(t_idx < n - k, pltpu.roll(v, n - k, 0), 0.0)


def _window_sums(v, shift, t_idx, grp):
    s = v + shift(v, 1, t_idx)
    out = s
    for lvl in range(1, len(POOL_WINDOWS)):
        s = s + shift(s, 1 << lvl, t_idx)
        out = jnp.where(grp >= lvl, s, out)
    return out


def _window_weight(t_idx, grp):
    return 1.0 / jnp.minimum(t_idx[:, :1] + 1, jnp.left_shift(2, grp)).astype(F32)


MIX_COLS = 128


def _mixer_fwd(name, proj, cw, cb, n_conv, n_groups, deps=()):
    t = proj.shape[0]
    nb = n_conv // MIX_COLS
    per_group = n_conv // n_groups // MIX_COLS

    def body(ba_ref, ca_ref, va_ref, vb_ref, cw_ref, cb_ref, *rest):
        z_ref, p_ref = rest[len(deps):]
        t_idx = lax.broadcasted_iota(jnp.int32, (t, MIX_COLS), 0)
        q = ca_ref[...].astype(F32) * va_ref[...].astype(F32)
        w = cw_ref[...]
        u = cb_ref[...] + w[0:1] * _shift_down(q, 2, t_idx) + w[1:2] * _shift_down(q, 1, t_idx) + w[2:3] * q
        z_ref[...] = (ba_ref[...].astype(F32) * u).astype(BF16)
        grp = pl.program_id(0) // per_group
        v = vb_ref[...].astype(F32)
        p_ref[...] = (_window_sums(v, _shift_down, t_idx, grp) * _window_weight(t_idx, grp) - v).astype(BF16)

    col = lambda s: pl.BlockSpec((t, MIX_COLS), lambda j: (0, s * nb + j))
    return pl.pallas_call(
        body, name=name, grid=(nb,),
        in_specs=[col(0), col(1), col(2), col(3), pl.BlockSpec((3, MIX_COLS), lambda j: (0, j)),
                  pl.BlockSpec((1, MIX_COLS), lambda j: (0, j))] + [ANY] * len(deps),
        out_specs=[col(0), col(0)],
        out_shape=[jax.ShapeDtypeStruct((t, n_conv), BF16), jax.ShapeDtypeStruct((t, n_conv), BF16)],
        compiler_params=_cp(("parallel",)),
    )(proj, proj, proj, proj, cw, cb, *deps)


def _mixer_bwd(name, dz, dp, proj, cw, cb, dproj, n_conv, n_groups, deps=()):
    t = proj.shape[0]
    nb = n_conv // MIX_COLS
    per_group = n_conv // n_groups // MIX_COLS

    def body(dz_ref, dp_ref, ba_ref, ca_ref, va_ref, cw_ref, cb_ref, _, *rest):
        o_ref, dcw_ref, dcb_ref, scr = rest[len(deps):]
        s = pl.program_id(1)

        @pl.when(s == 0)
        def _():
            t_idx = lax.broadcasted_iota(jnp.int32, (t, MIX_COLS), 0)
            ca, va = ca_ref[...].astype(F32), va_ref[...].astype(F32)
            q = ca * va
            q1, q2 = _shift_down(q, 1, t_idx), _shift_down(q, 2, t_idx)
            w = cw_ref[...]
            u = cb_ref[...] + w[0:1] * q2 + w[1:2] * q1 + w[2:3] * q
            dzf = dz_ref[...].astype(F32)
            du = dzf * ba_ref[...].astype(F32)
            scr[0] = (dzf * u).astype(BF16)
            dq = w[2:3] * du + w[1:2] * _shift_up(du, 1, t_idx) + w[0:1] * _shift_up(du, 2, t_idx)
            scr[1] = (dq * va).astype(BF16)
            scr[2] = (dq * ca).astype(BF16)
            dcb_ref[...] = jnp.sum(du, axis=0, keepdims=True)
            dcw_ref[0:1, :] = jnp.sum(du * q2, axis=0, keepdims=True)
            dcw_ref[1:2, :] = jnp.sum(du * q1, axis=0, keepdims=True)
            dcw_ref[2:3, :] = jnp.sum(du * q, axis=0, keepdims=True)
            grp = pl.program_id(0) // per_group
            dpf = dp_ref[...].astype(F32)
            e = dpf * _window_weight(t_idx, grp)
            scr[3] = (_window_sums(e, _shift_up, t_idx, grp) - dpf).astype(BF16)

        o_ref[...] = scr[s]

    col = lambda c: pl.BlockSpec((t, MIX_COLS), lambda j, s: (0, c * nb + j))
    own = pl.BlockSpec((t, MIX_COLS), lambda j, s: (0, j))
    return pl.pallas_call(
        body, name=name, grid=(nb, 4),
        in_specs=[own, own, col(0), col(1), col(2), pl.BlockSpec((3, MIX_COLS), lambda j, s: (0, j)),
                  pl.BlockSpec((1, MIX_COLS), lambda j, s: (0, j)), ANY] + [ANY] * len(deps),
        out_specs=[pl.BlockSpec((t, MIX_COLS), lambda j, s: (0, s * nb + j)),
                   pl.BlockSpec((3, MIX_COLS), lambda j, s: (0, j)), pl.BlockSpec((1, MIX_COLS), lambda j, s: (0, j))],
        out_shape=[jax.ShapeDtypeStruct(dproj.shape, BF16), jax.ShapeDtypeStruct((3, n_conv), F32),
                   jax.ShapeDtypeStruct((1, n_conv), F32)],
        scratch_shapes=[pltpu.VMEM((4, t, MIX_COLS), BF16)],
        input_output_aliases={7: 0},
        compiler_params=_cp(("arbitrary", "arbitrary")),
    )(dz, dp, proj, proj, proj, cw, cb, dproj, *deps)


def _merge_fwd(name, proj, bg, ya, yb, ps):
    t, d = ya.shape
    tm = _rows(t)

    def body(gab_ref, bg_ref, ya_ref, yb_ref, ps_ref, o_ref):
        gab = gab_ref[...].astype(F32) + bg_ref[...]
        sa, sb = jax.nn.sigmoid(gab[:, :d]), jax.nn.sigmoid(gab[:, d:])
        o_ref[...] = (sa * ya_ref[...].astype(F32) + sb * (yb_ref[...].astype(F32) * ps_ref[...])).astype(BF16)

    row = pl.BlockSpec((tm, d), lambda i: (i, 0))
    return pl.pallas_call(
        body, name=name, grid=(t // tm,),
        in_specs=[pl.BlockSpec((tm, 2 * d), lambda i: (i, 1)), pl.BlockSpec((1, 2 * d), lambda i: (0, 0)), row, row,
                  pl.BlockSpec((1, d), lambda i: (0, 0))],
        out_specs=row, out_shape=jax.ShapeDtypeStruct((t, d), BF16), compiler_params=_cp(("parallel",)),
    )(proj, bg, ya, yb, ps)


def _merge_bwd(name, dm, proj, bg, ya, yb, ps, deps=()):
    t, d = ya.shape
    tm = _rows(t)

    def body(dm_ref, gab_ref, bg_ref, ya_ref, yb_ref, ps_ref, *rest):
        dya_ref, dyb_ref, dg_ref, dba_ref, dbb_ref, dps_ref = rest[len(deps):]
        gab = gab_ref[...].astype(F32) + bg_ref[...]
        sa, sb = jax.nn.sigmoid(gab[:, :d]), jax.nn.sigmoid(gab[:, d:])
        dmf = dm_ref[...].astype(F32)
        ybf, ps_ = yb_ref[...].astype(F32), ps_ref[...]
        dya_ref[...] = (dmf * sa).astype(BF16)
        dyb = dmf * sb
        dyb_ref[...] = (dyb * ps_).astype(BF16)
        dga = dmf * ya_ref[...].astype(F32) * sa * (1.0 - sa)
        dgb = dmf * (ybf * ps_) * sb * (1.0 - sb)
        dg_ref[:, :d] = dga.astype(BF16)
        dg_ref[:, d:] = dgb.astype(BF16)

        @pl.when(pl.program_id(0) == 0)
        def _():
            dba_ref[...] = jnp.zeros_like(dba_ref)
            dbb_ref[...] = jnp.zeros_like(dbb_ref)
            dps_ref[...] = jnp.zeros_like(dps_ref)

        dba_ref[...] += jnp.sum(dga, axis=0, keepdims=True)
        dbb_ref[...] += jnp.sum(dgb, axis=0, keepdims=True)
        dps_ref[...] += jnp.sum(dyb * ybf, axis=0, keepdims=True)

    row = pl.BlockSpec((tm, d), lambda i: (i, 0))
    vec = pl.BlockSpec((1, d), lambda i: (0, 0))
    gates = pl.BlockSpec((tm, 2 * d), lambda i: (i, 1))
    return pl.pallas_call(
        body, name=name, grid=(t // tm,),
        in_specs=[row, gates, pl.BlockSpec((1, 2 * d), lambda i: (0, 0)), row, row, vec] + [ANY] * len(deps),
        out_specs=[row, row, gates, vec, vec, vec],
        out_shape=[jax.ShapeDtypeStruct((t, d), BF16), jax.ShapeDtypeStruct((t, d), BF16),
                   jax.ShapeDtypeStruct(proj.shape, BF16), jax.ShapeDtypeStruct((1, d), F32),
                   jax.ShapeDtypeStruct((1, d), F32), jax.ShapeDtypeStruct((1, d), F32)],
        compiler_params=_cp(("arbitrary",)),
    )(dm, proj, bg, ya, yb, ps, *deps)


def _ffn_up_act(name, h, w_up, gate, part=None, prev=None, deps=()):
    t, d = h.shape
    f = w_up.shape[1]
    tm, tf = _tile(t, 1024), _tile(f, 512)
    j0, j1 = _tile_span(f // tf, part)
    n_prev = 0 if prev is None else 2
    extra = ([] if prev is None else list(prev)) + list(deps)

    def body(h_ref, w_ref, g_ref, *rest):
        u_ref, a_ref = rest[len(extra):]
        u = lax.dot_general(h_ref[...], w_ref[...], _DIMS["nn"], preferred_element_type=F32)
        g = g_ref[...].astype(F32)
        u_ref[...] = u.astype(BF16)
        a_ref[...] = (g * jax.nn.sigmoid(g) * u).astype(BF16)

    blk = pl.BlockSpec((tm, tf), lambda i, j: (i, j0 + j))
    shp = jax.ShapeDtypeStruct((t, f), BF16)
    return pl.pallas_call(
        body, name=name, grid=(t // tm, j1 - j0),
        in_specs=[pl.BlockSpec((tm, d), lambda i, j: (i, 0)), pl.BlockSpec((d, tf), lambda i, j: (0, j0 + j)), blk]
        + [ANY] * len(extra),
        out_specs=[blk, blk], out_shape=[shp, shp], input_output_aliases={3 + i: i for i in range(n_prev)},
        compiler_params=_cp(("parallel", "parallel")))(h, w_up, gate, *extra)


def _ffn_bwd(name, dy, w_down, gate, up):
    t, d = dy.shape
    f = w_down.shape[0]
    tm, tf = _tile(t, 1024), _tile(f, 512)

    def body(dy_ref, w_ref, g_ref, u_ref, dg_ref, du_ref):
        da = lax.dot_general(dy_ref[...], w_ref[...], _DIMS["nt"], preferred_element_type=F32)
        g = g_ref[...].astype(F32)
        s = jax.nn.sigmoid(g)
        du_ref[...] = (da * (g * s)).astype(BF16)
        dg_ref[...] = (da * u_ref[...].astype(F32) * (s * (1.0 + g * (1.0 - s)))).astype(BF16)

    blk = pl.BlockSpec((tm, tf), lambda i, j: (i, j))
    shp = jax.ShapeDtypeStruct((t, f), BF16)
    return pl.pallas_call(
        body, name=name, grid=(t // tm, f // tf),
        in_specs=[pl.BlockSpec((tm, d), lambda i, j: (i, 0)), pl.BlockSpec((tf, d), lambda i, j: (j, 0)), blk, blk],
        out_specs=[blk, blk], out_shape=[shp, shp], compiler_params=_cp(("parallel", "parallel")))(dy, w_down, gate, up)


def _adamw_math(w, g, m, v):
    m = ADAM_B1 * m + (1.0 - ADAM_B1) * g
    v = ADAM_B2 * v + (1.0 - ADAM_B2) * (g * g)
    m_hat = m / (1.0 - ADAM_B1 ** ADAM_STEP)
    v_hat = v / (1.0 - ADAM_B2 ** ADAM_STEP)
    delta = -ADAM_LR * (m_hat / (jnp.sqrt(v_hat) + ADAM_EPS) + ADAM_WD * w)
    return delta, m, v


def _adamw(name, w, g, m, v):
    r, c = w.shape
    tr = _tile8(r, 512 if c <= 1024 else 256)

    def body(w_ref, g_ref, m_ref, v_ref, go_ref, d_ref, nm_ref, nv_ref):
        g = g_ref[...]
        go_ref[...] = g
        d_ref[...], nm_ref[...], nv_ref[...] = _adamw_math(w_ref[...], g, m_ref[...], v_ref[...])

    blk = pl.BlockSpec((tr, c), lambda i: (i, 0))
    shp = jax.ShapeDtypeStruct((r, c), F32)
    return pl.pallas_call(body, name=name, grid=(r // tr,), in_specs=[blk] * 4, out_specs=[blk] * 4,
                          out_shape=[shp] * 4, compiler_params=_cp(("parallel",)))(w, g, m, v)


class _Weight:
    def __init__(self, name, rows, cols, colshard):
        self.name, self.colshard = name, colshard
        self.R, self.nn = rows // 2, cols
        self.P = 1 if colshard else N_CHIPS
        self.N = N_CHIPS * cols if colshard else cols

    def cols(self, k):
        return pl.ds(pl.multiple_of(k * self.nn, LANES), self.nn)

    def shard(self, ref, k):
        return ref.at[0, :, :, self.cols(k)] if self.colshard else ref.at[k]

    def half(self, ref, k, h):
        return ref.at[0, h, :, self.cols(k)] if self.colshard else ref.at[k, h]

    def quarter(self, ref, k, h, q):
        return self.half(ref, k, h).at[pl.ds(q * (self.R // 2), self.R // 2), :]

    def part(self, ref, k):
        return ref.at[0, :, self.cols(k)] if self.colshard else ref.at[k]


def _remote(src, dst, ssem, rsem, dev):
    return pltpu.make_async_remote_copy(src_ref=src, dst_ref=dst, send_sem=ssem, recv_sem=rsem, device_id=dev,
                                        device_id_type=MESH)


def _other_chips(x, y):
    chips = [(1 - x, y), (x, 1 - y), (1 - x, 1 - y)]
    return chips, [2 * cx + cy for cx, cy in chips]


def _hbm(a):
    return pltpu.with_memory_space_constraint(a, pltpu.HBM)


def _gather_start(name, groups, lands, after=()):
    flat = [w for grp in groups for w in grp]
    nw, ng = len(flat), len(groups)

    def body(*refs):
        land = refs[:nw]
        sems = refs[nw + len(after):nw + len(after) + 2 * ng]
        token = refs[2 * nw + len(after) + 2 * ng]
        x, y, c = _mesh_pos()
        k_me = 2 * x + y
        chips, _ = _other_chips(x, y)
        i = 0
        for g, grp in enumerate(groups):
            for wi, w in enumerate(grp):
                mine = w.half(land[i], k_me, c)
                for j, chip in enumerate(chips):
                    _remote(mine, mine, sems[2 * g].at[3 * wi + j], sems[2 * g + 1].at[3 * wi + j], (*chip, c)).start()
                i += 1
        token[...] = jnp.zeros_like(token)

    sem_shapes = []
    for grp in groups:
        sem_shapes += [pltpu.SemaphoreType.DMA((3 * len(grp),))] * 2
    out = pl.pallas_call(
        body, name=name, in_specs=[HBM] * nw + [ANY] * len(after),
        out_specs=[SEM] * (2 * ng) + [HBM] * nw + [VMEM],
        out_shape=sem_shapes + [pltpu.HBM(a.shape, a.dtype) for a in lands] + [jax.ShapeDtypeStruct((8, LANES), F32)],
        input_output_aliases={i: 2 * ng + i for i in range(nw)},
        compiler_params=pltpu.CompilerParams(has_side_effects=EFFECT),
    )(*[_hbm(a) for a in lands], *after)
    sems = [(out[2 * g], out[2 * g + 1]) for g in range(ng)]
    return sems, list(out[2 * ng:2 * ng + nw]), out[-1]


def _gather_wait(name, grp, lands, ssem, rsem, after):
    n = len(grp)

    def body(*refs):
        land, ssem_ref, rsem_ref = refs[:n], refs[n], refs[n + 1]
        x, y, c = _mesh_pos()
        k_me = 2 * x + y
        chips, ks = _other_chips(x, y)
        for wi, w in enumerate(grp):
            for j, chip in enumerate(chips):
                cp = _remote(w.half(land[wi], k_me, c), w.half(land[wi], ks[j], c), ssem_ref.at[3 * wi + j],
                             rsem_ref.at[3 * wi + j], (*chip, c))
                cp.wait_send()
                cp.wait_recv()

    return pl.pallas_call(
        body, name=name, in_specs=[HBM] * n + [SEM, SEM, ANY], out_specs=[HBM] * n,
        out_shape=[pltpu.HBM(a.shape, a.dtype) for a in lands], input_output_aliases={i: i for i in range(n)},
        compiler_params=pltpu.CompilerParams(has_side_effects=EFFECT),
    )(*lands, ssem, rsem, after)


def _split_start(name, arrays, n, copies, after=()):
    na = len(arrays)

    def body(*refs):
        ssem, rsem, token = refs[na + len(after):][0], refs[na + len(after):][1], refs[2 * na + len(after) + 2]
        for i, (src, dst, dev, _) in enumerate(copies(refs[:na], *_mesh_pos())):
            _remote(src, dst, ssem.at[i], rsem.at[i], dev).start()
        token[...] = jnp.zeros_like(token)

    out = pl.pallas_call(
        body, name=name, in_specs=[HBM] * na + [ANY] * len(after), out_specs=[SEM, SEM] + [HBM] * na + [VMEM],
        out_shape=[pltpu.SemaphoreType.DMA((n,))] * 2 + [pltpu.HBM(a.shape, a.dtype) for a in arrays]
        + [jax.ShapeDtypeStruct((8, LANES), F32)],
        input_output_aliases={i: 2 + i for i in range(na)},
        compiler_params=pltpu.CompilerParams(has_side_effects=EFFECT),
    )(*[_hbm(a) for a in arrays], *after)
    return out[0], out[1], list(out[2:2 + na]), out[-1]


def _split_wait(name, arrays, ssem, rsem, copies, after):
    na = len(arrays)

    def body(*refs):
        for i, (src, _, dev, dst) in enumerate(copies(refs[:na], *_mesh_pos())):
            cp = _remote(src, dst, refs[na].at[i], refs[na + 1].at[i], dev)
            cp.wait_send()
            cp.wait_recv()

    return list(pl.pallas_call(
        body, name=name, in_specs=[HBM] * na + [SEM, SEM] + [ANY] * len(after), out_specs=[HBM] * na,
        out_shape=[pltpu.HBM(a.shape, a.dtype) for a in arrays], input_output_aliases={i: i for i in range(na)},
        compiler_params=pltpu.CompilerParams(has_side_effects=EFFECT),
    )(*arrays, ssem, rsem, *after))


def _pass_copies(grp, rels=(0, 1, 2)):
    def copies(land, x, y, c):
        _, ks = _other_chips(x, y)
        return [(w.half(land[wi], ks[j], c), w.half(land[wi], ks[j], c), (x, y, 1 - c), w.half(land[wi], ks[j], 1 - c))
                for wi, w in enumerate(grp) for j in rels]
    copies.n = len(grp) * len(rels)
    return copies


def _near_copies(grp):
    def copies(land, x, y, c):
        chips, ks = _other_chips(x, y)
        out = []
        for wi, w in enumerate(grp):
            mine = w.half(land[wi], 2 * x + y, c)
            out += [(mine, mine, (*chips[j], c), w.half(land[wi], ks[j], c)) for j in (0, 1)]
        return out
    copies.n = 2 * len(grp)
    return copies


def _far_copies(grp):
    def copies(land, x, y, c):
        chips, ks = _other_chips(x, y)
        out = []
        for wi, w in enumerate(grp):
            for j in (0, 1):
                q = w.quarter(land[wi], ks[j], c, j)
                out.append((q, q, (*chips[1 - j], c), w.quarter(land[wi], ks[2], c, j)))
        return out
    copies.n = 2 * len(grp)
    return copies


def _pair_copies(n, whole=False):
    def copies(refs, x, y, c):
        return [(refs[i] if whole else refs[i].at[:, 1 - c], refs[n + i], (x, y, 1 - c), refs[n + i]) for i in range(n)]
    return copies


def _share_copies(n):
    def copies(refs, x, y, c):
        return [(refs[i].at[c], refs[i].at[c], (x, y, 1 - c), refs[i].at[1 - c]) for i in range(n)]
    return copies


def _gather_conv_w(cw):
    ncw = cw.shape[1]

    def body(cw_ref, out_ref, ssem, rsem):
        x, y, c = _mesh_pos()
        k_me = 2 * x + y
        chips, ks = _other_chips(x, y)
        cols = lambda k: out_ref.at[:, pl.ds(pl.multiple_of(k * ncw, LANES), ncw)]
        cps = [_remote(cw_ref, cols(k_me), ssem.at[j], rsem.at[j], (*chip, c)) for j, chip in enumerate(chips)]
        for cp in cps:
            cp.start()
        for k in range(N_CHIPS):
            @pl.when(k_me == k)
            def _():
                out_ref[:, k * ncw:(k + 1) * ncw] = cw_ref[...]
        for j in range(3):
            _remote(cw_ref, cols(ks[j]), ssem.at[j], rsem.at[j], (*chips[j], c)).wait_recv()
        for cp in cps:
            cp.wait_send()

    return pl.pallas_call(
        body, name="gather_conv_w", in_specs=[VMEM], out_specs=VMEM,
        out_shape=jax.ShapeDtypeStruct((3, N_CHIPS * ncw), F32),
        scratch_shapes=[pltpu.SemaphoreType.DMA((3,)), pltpu.SemaphoreType.DMA((3,))],
    )(cw)


def _grad_tiles(w, n):
    return _tile8(w.R, 512) if w.R <= 512 else w.R // 2, _tile(n, 2048)


def _pair_sum(name, w, pos, grad, got):
    tr, tn = _grad_tiles(w, w.N)

    def body(pos_ref, g_ref, r_ref, o_ref):
        o_ref[...] = (g_ref[...].astype(F32) + r_ref[...].astype(F32)).astype(BF16)

    blk = pl.BlockSpec((None, tr, tn), lambda p, i, j, pos: (p, i, j))
    grid_spec = pltpu.PrefetchScalarGridSpec(
        num_scalar_prefetch=1, grid=(w.P, w.R // tr, w.N // tn),
        in_specs=[pl.BlockSpec((None, None, tr, tn), lambda p, i, j, pos: (p, pos[0], i, j)), blk], out_specs=blk)
    return pl.pallas_call(body, name=name, grid_spec=grid_spec, out_shape=jax.ShapeDtypeStruct((w.P, w.R, w.N), BF16),
                          compiler_params=_cp(("parallel",) * 3))(pos, grad, got)


def _scatter_start(name, ws, pairs):
    nw = len(ws)

    def body(*refs):
        pr, land = refs[:nw], refs[nw:2 * nw]
        ssem, rsem = refs[2 * nw], refs[2 * nw + 1]
        token = refs[4 * nw + 2]
        x, y, c = _mesh_pos()
        chips, ks = _other_chips(x, y)
        for i, w in enumerate(ws):
            for j, chip in enumerate(chips):
                _remote(w.part(pr[i], ks[j]), land[i].at[j], ssem.at[3 * i + j], rsem.at[3 * i + j], (*chip, c)).start()
        token[...] = jnp.zeros_like(token)

    lands = [lax.empty((3, w.R, w.nn), BF16) for w in ws]
    out = pl.pallas_call(
        body, name=name, in_specs=[HBM] * (2 * nw),
        out_specs=[SEM, SEM] + [HBM] * (2 * nw) + [VMEM],
        out_shape=[pltpu.SemaphoreType.DMA((3 * nw,))] * 2 + [pltpu.HBM(a.shape, a.dtype) for a in pairs + lands]
        + [jax.ShapeDtypeStruct((8, LANES), F32)],
        input_output_aliases={i: 2 + i for i in range(2 * nw)},
        compiler_params=pltpu.CompilerParams(has_side_effects=EFFECT),
    )(*[_hbm(a) for a in pairs + lands])
    return out[0], out[1], list(out[2:2 + nw]), list(out[2 + nw:2 + 2 * nw]), out[-1]


def _scatter_wait(name, ws, pairs, lands, ssem, rsem, after):
    nw = len(ws)

    def body(*refs):
        pr, land = refs[:nw], refs[nw:2 * nw]
        ssem_ref, rsem_ref = refs[2 * nw], refs[2 * nw + 1]
        x, y, c = _mesh_pos()
        chips, ks = _other_chips(x, y)
        for i, w in enumerate(ws):
            for j, chip in enumerate(chips):
                cp = _remote(w.part(pr[i], ks[j]), land[i].at[j], ssem_ref.at[3 * i + j], rsem_ref.at[3 * i + j], (*chip, c))
                cp.wait_send()
                cp.wait_recv()

    out = pl.pallas_call(
        body, name=name, in_specs=[HBM] * (2 * nw) + [SEM, SEM] + [ANY] * len(after), out_specs=[HBM] * (2 * nw),
        out_shape=[pltpu.HBM(a.shape, a.dtype) for a in pairs + lands],
        input_output_aliases={i: i for i in range(2 * nw)},
        compiler_params=pltpu.CompilerParams(has_side_effects=EFFECT),
    )(*pairs, *lands, ssem, rsem, *after)
    return list(out[:nw]), list(out[nw:])


def _final_sum(name, w, pos, grad, got, parts):
    tr, tn = _grad_tiles(w, w.nn)
    nbc = w.nn // tn
    if got is None:
        return _final_sum_pair(name, w, pos, grad, parts, tr, tn)

    def body(pos_ref, g_ref, r_ref, p_ref, o_ref):
        acc = g_ref[...].astype(F32) + r_ref[...].astype(F32)
        for j in range(3):
            acc = acc + p_ref[j].astype(F32)
        o_ref[...] = acc

    if w.colshard:
        g_spec = pl.BlockSpec((None, None, tr, tn), lambda i, j, pos: (0, pos[0], i, pos[1] * nbc + j))
        r_spec = pl.BlockSpec((None, tr, tn), lambda i, j, pos: (0, i, pos[1] * nbc + j))
    else:
        g_spec = pl.BlockSpec((None, None, tr, tn), lambda i, j, pos: (pos[1], pos[0], i, j))
        r_spec = pl.BlockSpec((None, tr, tn), lambda i, j, pos: (pos[1], i, j))
    grid_spec = pltpu.PrefetchScalarGridSpec(
        num_scalar_prefetch=1, grid=(w.R // tr, nbc),
        in_specs=[g_spec, r_spec, pl.BlockSpec((3, tr, tn), lambda i, j, pos: (0, i, j))],
        out_specs=pl.BlockSpec((None, tr, tn), lambda i, j, pos: (pos[0], i, j)))
    return pl.pallas_call(body, name=name, grid_spec=grid_spec, out_shape=jax.ShapeDtypeStruct((2, w.R, w.nn), F32),
                          compiler_params=_cp(("parallel",) * 2))(pos, grad, got, parts)


def _final_sum_pair(name, w, pos, pair, parts, tr, tn):
    nbc = w.nn // tn

    def body(pos_ref, g_ref, p_ref, o_ref):
        acc = g_ref[...].astype(F32)
        for j in range(3):
            acc = acc + p_ref[j].astype(F32)
        o_ref[...] = acc

    if w.colshard:
        g_spec = pl.BlockSpec((None, tr, tn), lambda i, j, pos: (0, i, pos[1] * nbc + j))
    else:
        g_spec = pl.BlockSpec((None, tr, tn), lambda i, j, pos: (pos[1], i, j))
    grid_spec = pltpu.PrefetchScalarGridSpec(
        num_scalar_prefetch=1, grid=(w.R // tr, nbc),
        in_specs=[g_spec, pl.BlockSpec((3, tr, tn), lambda i, j, pos: (0, i, j))],
        out_specs=pl.BlockSpec((None, tr, tn), lambda i, j, pos: (pos[0], i, j)))
    return pl.pallas_call(body, name=name, grid_spec=grid_spec, out_shape=jax.ShapeDtypeStruct((2, w.R, w.nn), F32),
                          compiler_params=_cp(("parallel",) * 2))(pos, pair, parts)


VEC_ROWS = 16


def _vector_step(d, n_conv, parts, params, deps=()):
    ncw = params[2][0].shape[1]
    n_par = len(params)

    def body(*refs):
        dg1, dba, dbb, dcw, dcb, dps, dg2, dgf, lc = refs[:9]
        wmv = refs[9:9 + 3 * n_par]
        refs = refs[9 + 3 * n_par + len(deps):]
        outs = refs[:4 * n_par]
        loss_ref = refs[4 * n_par]
        snd, got, ssem, rsem = refs[4 * n_par + 1:]
        x, y, c = _mesh_pos()
        me = 4 * x + 2 * y + c
        snd[...] = jnp.zeros_like(snd)
        for row, ref in ((0, dg1), (1, dba), (2, dbb), (3, dps), (4, dg2), (5, dgf), (6, lc)):
            snd[row:row + 1, :] = ref[...]
        snd[7:8, :n_conv] = dcb[...]
        snd[8:11, :n_conv] = dcw[...]
        cps = []
        for r in range(1, N_DEV):
            peer = tuple(1 - p if (r >> b) & 1 else p for p, b in ((x, 2), (y, 1), (c, 0)))
            cps.append(_remote(snd, got.at[me], ssem.at[r - 1], rsem.at[r - 1], peer))
        for cp in cps:
            cp.start()
        got[me] = snd[...]
        for r in range(1, N_DEV):
            peer = tuple(1 - p if (r >> b) & 1 else p for p, b in ((x, 2), (y, 1), (c, 0)))
            _remote(snd, got.at[4 * peer[0] + 2 * peer[1] + peer[2]], ssem.at[r - 1], rsem.at[r - 1], peer).wait_recv()
        for cp in cps:
            cp.wait_send()
        tot = got[0]
        for dev in range(1, N_DEV):
            tot = tot + got[dev]
        loss_ref[...] = jnp.sum(tot[6:7, :], axis=1, keepdims=True)
        k_me = 2 * x + y
        g_cw = jnp.zeros((3, ncw), F32)
        for k in range(N_CHIPS):
            g_cw = g_cw + jnp.where(k_me == k, tot[8:11, k * ncw:(k + 1) * ncw], 0.0)
        grads = [tot[0:1, :], jnp.concatenate([tot[1:2, :], tot[2:3, :]], axis=1), g_cw, tot[7:8, :n_conv],
                 tot[3:4, :], tot[4:5, :], tot[5:6, :]]
        for i, g in enumerate(grads):
            w_ref, m_ref, v_ref = wmv[3 * i:3 * i + 3]
            delta, nm, nv = _adamw_math(w_ref[...], g, m_ref[...], v_ref[...])
            outs[4 * i][...] = g
            outs[4 * i + 1][...] = delta
            outs[4 * i + 2][...] = nm
            outs[4 * i + 3][...] = nv

    args = list(parts)
    out_shape = []
    for w, m, v in params:
        args += [w, m, v]
        out_shape += [jax.ShapeDtypeStruct(w.shape, F32)] * 4
    out_shape.append(jax.ShapeDtypeStruct((1, 1), F32))
    return pl.pallas_call(
        body, name="vector_params_step", in_specs=[VMEM] * len(args) + [ANY] * len(deps),
        out_specs=[VMEM] * len(out_shape), out_shape=out_shape,
        scratch_shapes=[pltpu.VMEM((VEC_ROWS, d), F32), pltpu.VMEM((N_DEV, VEC_ROWS, d), F32),
                        pltpu.SemaphoreType.DMA((N_DEV - 1,)), pltpu.SemaphoreType.DMA((N_DEV - 1,))],
        compiler_params=pltpu.CompilerParams(vmem_limit_bytes=VMEM_LIMIT),
    )(*args, *deps)


def kernel(x, norm1_g, w_in, b_gate, conv_w, conv_b, w_a_out, w_pool, pool_scale, w_o, norm2_g, w_ffn_gate, w_ffn_up, w_ffn_down, final_g, loss_target, m_norm1_g, m_w_in, m_b_gate, m_conv_w, m_conv_b, m_w_a_out, m_w_pool, m_pool_scale, m_w_o, m_norm2_g, m_w_ffn_gate, m_w_ffn_up, m_w_ffn_down, m_final_g, v_norm1_g, v_w_in, v_b_gate, v_conv_w, v_conv_b, v_w_a_out, v_w_pool, v_pool_scale, v_w_o, v_norm2_g, v_w_ffn_gate, v_w_ffn_up, v_w_ffn_down, v_final_g):
    t, d = x.shape[1], x.shape[2]
    n_conv = conv_b.shape[1]
    n_groups, pool_cg, pool_dg = w_pool.shape[1], w_pool.shape[2], N_CHIPS * w_pool.shape[3]
    d_ff = N_CHIPS * w_ffn_gate.shape[2]
    assert n_conv // n_groups == pool_cg and n_conv % (n_groups * MIX_COLS) == 0 and n_groups == len(POOL_WINDOWS)

    big = {"w_in": (w_in, m_w_in, v_w_in), "w_a_out": (w_a_out, m_w_a_out, v_w_a_out), "w_pool": (w_pool, m_w_pool, v_w_pool),
           "w_o": (w_o, m_w_o, v_w_o), "w_ffn_gate": (w_ffn_gate, m_w_ffn_gate, v_w_ffn_gate),
           "w_ffn_up": (w_ffn_up, m_w_ffn_up, v_w_ffn_up), "w_ffn_down": (w_ffn_down, m_w_ffn_down, v_w_ffn_down)}
    colshard = {"w_in": True, "w_a_out": True, "w_pool": True, "w_o": False, "w_ffn_gate": True, "w_ffn_up": True,
                "w_ffn_down": False}
    names = list(big)
    shard2d = {n: big[n][0].reshape(-1, big[n][0].shape[-1]) for n in names}
    ws = [_Weight(n, *shard2d[n].shape, colshard[n]) for n in names]

    xs, tgt = x[0], loss_target[0]
    cw_loc = conv_w[0]
    pos = jnp.stack([lax.axis_index("c"), 2 * lax.axis_index("x") + lax.axis_index("y")]).astype(jnp.int32)
    by_name = {w.name: w for w in ws}
    groups = [[by_name[n] for n in g] for g in (["w_in"], ["w_a_out", "w_pool", "w_o"], ["w_ffn_gate"], ["w_ffn_up"],
                                                 ["w_ffn_down"])]
    first = [sum(len(g) for g in groups[:i]) for i in range(len(groups))]
    rgroups = [groups[0], groups[1], groups[2] + groups[3], groups[4]]

    cw_full = _gather_conv_w(cw_loc)
    cast = lambda w, dep: _cast_place(f"cast_{w.name}", w, pos, shard2d[w.name].reshape(2, w.R, w.nn), deps=[dep])
    chips, ks = _other_chips(lax.axis_index("x"), lax.axis_index("y"))
    kvec = jnp.stack([pos[1], *ks]).astype(jnp.int32)
    full = {}

    def start(name, arrays, copies, after=()):
        ssem, rsem, arrays, token = _split_start(name, arrays, copies.n, copies, after)
        return name, arrays, ssem, rsem, copies, token

    def wait(started, after):
        name, arrays, ssem, rsem, copies, _ = started
        return _split_wait(name + "_wait", arrays, ssem, rsem, copies, after)

    def pass_on(g, got, after=()):
        return start(f"pass_{g}", got, _pass_copies(groups[g]), after)

    def passed(g, st, after=None):
        got = wait(st, [st[5]] if after is None else after)
        full.update({w.name: a.reshape(w.P * 2 * w.R, w.N) for w, a in zip(groups[g], got)})

    near = start("near_0", [cast(w, cw_full) for w in groups[0]], _near_copies(groups[0]))
    rest = [cast(w, near[5]) for grp in groups[1:] for w in grp]
    h1 = _rms_fwd("norm1_fwd", xs, norm1_g, deps=[near[5]])
    proj = _proj_piece("proj_own", h1, shard2d["w_in"], None, kvec, 0, 1, deps=rest)
    got = wait(near, [proj])
    far = start("far_0", got, _far_copies(groups[0]))
    sems_b, lands_b, tok_b = _gather_start("gather_start_b", groups[1:2], rest[:3], after=[far[5]])
    st = start("pass_near_0", far[1], _pass_copies(groups[0], (0, 1)), [tok_b])
    got = wait(st, [st[5]])
    proj = _proj_piece("proj_near", h1, got[0].reshape(-1, groups[0][0].N), proj, kvec, 1, 2)
    st = start("pass_far_0", wait((far[0], got) + far[2:], [proj]), _pass_copies(groups[0], (2,)))
    got = wait(st, [st[5]])
    w_in_full = got[0].reshape(-1, groups[0][0].N)
    proj = _proj_piece("proj_far", h1, w_in_full, proj, kvec, 3, 1)
    got = _gather_wait("gather_wait_1", groups[1], lands_b, *sems_b[0], proj)
    near_g = start("near_2", rest[3:4], _near_copies(groups[2]), got)
    st = pass_on(1, got, [near_g[5]])
    z, p = _mixer_fwd("mixer_fwd", proj, cw_full, conv_b, n_conv, n_groups, deps=[st[5]])
    passed(1, st, [z])
    wp_full = full["w_pool"].reshape(n_groups, pool_cg, pool_dg)
    ya = _mm_nn("conv_out", z, full["w_a_out"], BF16)
    yb = _gmm_nn("pool_out", p, wp_full, BF16)
    merged = _merge_fwd("merge_fwd", proj, b_gate, ya, yb, pool_scale)
    far_g = start("far_2", wait(near_g, [merged]), _far_copies(groups[2]))
    near_u = start("near_3", rest[4:5], _near_copies(groups[3]), [far_g[5]])
    x2 = _mm_nn("mix_out", merged, full["w_o"], F32, add=xs, deps=[near_u[5]])
    st = pass_on(2, wait(far_g, [x2]))
    h2 = _rms_fwd("norm2_fwd", x2, norm2_g, deps=[st[5]])
    passed(2, st, [h2])
    gate = _mm_nn("ffn_gate_a", h2, full["w_ffn_gate"], BF16, part=(0, 2))
    far_u = start("far_3", wait(near_u, [gate]), _far_copies(groups[3]))
    near_d = start("near_4", rest[5:6], _near_copies(groups[4]), [far_u[5]])
    gate = _mm_nn("ffn_gate_b", h2, full["w_ffn_gate"], BF16, part=(1, 2), prev=gate, deps=[near_d[5]])
    passed(3, pass_on(3, wait(far_u, [gate])))
    up_act = _ffn_up_act("ffn_up_act_a", h2, full["w_ffn_up"], gate, part=(0, 2))
    far_d = start("far_4", wait(near_d, [up_act[0]]), _far_copies(groups[4]))
    up, act = _ffn_up_act("ffn_up_act_b", h2, full["w_ffn_up"], gate, part=(1, 2), prev=up_act, deps=[far_d[5]])
    passed(4, pass_on(4, wait(far_d, [act])))
    x3 = _mm_nn("ffn_down", act, full["w_ffn_down"], F32, add=x2, tk=d_ff // 4)

    pending = {}

    def pair_start(g, grads):
        grp = rgroups[g]
        gcan = [grads[w.name].reshape(w.P, 2, w.R, w.N) for w in grp]
        slots = [lax.empty((w.P, w.R, w.N), BF16) for w in grp]
        pending[g] = _split_start(f"pair_start_{g}", gcan + slots, len(grp), _pair_copies(len(grp)))
        return pending[g][3]

    def scatter_start(g, after):
        grp = rgroups[g]
        n = len(grp)
        ssem, rsem, arrs, _ = pending[g]
        arrs = _split_wait(f"pair_wait_{g}", arrs, ssem, rsem, _pair_copies(n), after)
        gcan, sib = arrs[:n], arrs[n:]
        pairs = [_pair_sum(f"pair_sum_{w.name}", w, pos, a, s) for w, a, s in zip(grp, gcan, sib)]
        ssem, rsem, pairs, slots, token = _scatter_start(f"scatter_start_{g}", grp, pairs)
        pending[g] = (gcan, sib, pairs, slots, ssem, rsem)
        return token

    def pair_start_halves(g, ab, deps):
        grp = rgroups[g]
        sent = [_mm_tn_half(f"d{w.name}_sib", a, b, pos, False, deps=deps if i == 0 else ()) for i, (w, (a, b)) in enumerate(zip(grp, ab))]
        slots = [lax.empty((1, w.R, w.N), BF16) for w in grp]
        pending[g] = _split_start(f"pair_start_{g}", sent + slots, len(grp), _pair_copies(len(grp), whole=True))
        return pending[g][3]

    def scatter_start_halves(g, ab, after):
        grp = rgroups[g]
        n = len(grp)
        ssem, rsem, arrs, _ = pending[g]
        arrs = _split_wait(f"pair_wait_{g}", arrs, ssem, rsem, _pair_copies(n, whole=True), after)
        pairs = [_mm_tn_half(f"d{w.name}_own", a, b, pos, True, add=s) for w, (a, b), s in zip(grp, ab, arrs[n:])]
        ssem, rsem, pairs, slots, token = _scatter_start(f"scatter_start_{g}", grp, pairs)
        pending[g] = (None, None, pairs, slots, ssem, rsem)
        return token

    def reduce_finish(g, after):
        grp = rgroups[g]
        gcan, sib, pairs, slots, ssem, rsem = pending[g]
        pairs, parts = _scatter_wait(f"scatter_wait_{g}", grp, pairs, slots, ssem, rsem, after)
        if gcan is None:
            return [_final_sum(f"final_sum_{w.name}", w, pos, a, None, q) for w, a, q in zip(grp, pairs, parts)]
        return [_final_sum(f"final_sum_{w.name}", w, pos, a, s, q) for w, a, s, q in zip(grp, gcan, sib, parts)]

    grads = {}
    dx3, dx3b, d_gf, loss_cols = _final_bwd("final_bwd", x3, final_g.reshape(1, d), tgt)
    dgate, dup = _ffn_bwd("ffn_bwd", dx3b, full["w_ffn_down"], gate, up)
    grads["w_ffn_down"] = _mm_tn("dw_ffn_down", act, dx3b, BF16)
    tok = pair_start(3, grads)
    dh2 = _mm_nt("d_h2", [(dgate, full["w_ffn_gate"]), (dup, full["w_ffn_up"])], BF16, tk=d_ff // 4, deps=[tok])
    tok = scatter_start(3, [dh2])
    tok = pair_start_halves(2, [(h2, dgate), (h2, dup)], [tok])
    dx2, dx2b, d_g2 = _rms_bwd("norm2_bwd", x2, norm2_g, dh2, dx3, True, deps=[tok])
    dmerged = _mm_nt("d_merged", [(dx2b, full["w_o"])], BF16, tk=d)
    grads["w_o"] = _mm_tn("dw_o", merged, dx2b, BF16)
    tok = scatter_start_halves(2, [(h2, dgate), (h2, dup)], [grads["w_o"]])
    dya, dyb, dproj, d_bga, d_bgb, d_ps = _merge_bwd("merge_bwd", dmerged, proj, b_gate, ya, yb, pool_scale, deps=[tok])
    dz = _mm_nt("d_z", [(dya, full["w_a_out"])], BF16, tk=d)
    grads["w_a_out"] = _mm_tn("dw_a_out", z, dya, BF16)
    dp = _gmm_nt("d_pool", dyb, wp_full, BF16)
    grads["w_pool"] = _gmm_tn("dw_pool", p, dyb, n_groups, BF16)
    tok = pair_start(1, grads)
    dproj, d_cw, d_cb = _mixer_bwd("mixer_bwd", dz, dp, proj, cw_full, conv_b, dproj, n_conv, n_groups, deps=[tok])
    tok = scatter_start(1, [dproj])
    tok = pair_start_halves(0, [(h1, dproj)], [tok])
    dh1 = _mm_nt("d_h1", [(dproj, w_in_full)], BF16, tk=proj.shape[1] // 4, deps=[tok])
    tok = scatter_start_halves(0, [(h1, dproj)], [dh1])
    grad_x, d_g1 = _rms_bwd("norm1_bwd", xs, norm1_g, dh1, dx2, False, deps=[tok])

    g_big, d_big, m_big, v_big = {}, {}, {}, {}

    def update(wsub, shared):
        out = []
        for w, g in zip(wsub, shared):
            wt, mt, vt = big[w.name]
            g2 = g.reshape(2 * w.R, w.nn)
            go, dl, nm, nv = _adamw(f"adamw_{w.name}", shard2d[w.name], g2, mt.reshape(g2.shape), vt.reshape(g2.shape))
            g_big[w.name], d_big[w.name], m_big[w.name], v_big[w.name] = (a.reshape(wt.shape) for a in (go, dl, nm, nv))
            out.append(nv)
        return out

    after = [grad_x]
    started = []
    for g in (3, 2, 1):
        halves = reduce_finish(g, after)
        share = _share_copies(len(halves))
        ssem, rsem, halves, tok = _split_start(f"share_start_{g}", halves, len(halves), share)
        started.append((g, ssem, rsem, halves, share))
        after = [tok]
    for g, ssem, rsem, halves, share in started:
        after = update(rgroups[g], _split_wait(f"share_wait_{g}", halves, ssem, rsem, share, after))
    share = _share_copies(1)
    ssem, rsem, halves, tok = _split_start("share_start_0", reduce_finish(0, after), 1, share)

    vec_names = ["norm1_g", "b_gate", "conv_w", "conv_b", "pool_scale", "norm2_g", "final_g"]
    vec = {"norm1_g": (norm1_g, m_norm1_g, v_norm1_g), "b_gate": (b_gate, m_b_gate, v_b_gate),
           "conv_w": (cw_loc, m_conv_w[0], v_conv_w[0]), "conv_b": (conv_b, m_conv_b, v_conv_b),
           "pool_scale": (pool_scale, m_pool_scale, v_pool_scale), "norm2_g": (norm2_g, m_norm2_g, v_norm2_g),
           "final_g": tuple(a.reshape(1, d) for a in (final_g, m_final_g, v_final_g))}
    vout = _vector_step(d, n_conv, [d_g1, d_bga, d_bgb, d_cw, d_cb, d_ps, d_g2, d_gf, loss_cols],
                        [vec[n] for n in vec_names], deps=halves)
    update(rgroups[0], _split_wait("share_wait_0", halves, ssem, rsem, share, []))

    shapes = {"conv_w": conv_w.shape, "final_g": final_g.shape}
    g_vec, d_vec, m_vec, v_vec = ({n: vout[4 * i + q].reshape(shapes.get(n, vec[n][0].shape)) for i, n in enumerate(vec_names)}
                                  for q in range(4))
    loss = vout[-1].reshape(())

    order = ["norm1_g", "w_in", "b_gate", "conv_w", "conv_b", "w_a_out", "w_pool", "pool_scale", "w_o", "norm2_g",
             "w_ffn_gate", "w_ffn_up", "w_ffn_down", "final_g"]
    pick = lambda vecs, bigs: [vecs[n] if n in vecs else bigs[n] for n in order]
    return (loss, grad_x.reshape(x.shape), *pick(g_vec, g_big), *pick(d_vec, d_big), *pick(m_vec, m_big),
            *pick(v_vec, v_big))
```

```python
import functools

import jax
import jax.numpy as jnp
from jax import lax
from jax.experimental import pallas as pl
from jax.experimental.pallas import tpu as pltpu

F32, BF16 = jnp.float32, jnp.bfloat16
MESH = pl.DeviceIdType.MESH
ANY = pl.BlockSpec(memory_space=pl.ANY)
VMEM = pl.BlockSpec(memory_space=pltpu.VMEM)
HBM = pl.BlockSpec(memory_space=pltpu.HBM)
SEM = pl.BlockSpec(memory_space=pltpu.SEMAPHORE)
EFFECT = pltpu.SideEffectType.DATAFLOW_SIDE_EFFECTING

EPS = 1e-6
POOL_WINDOWS = (2, 4, 8, 16)
ADAM_LR, ADAM_B1, ADAM_B2, ADAM_EPS, ADAM_WD, ADAM_STEP = 0.001, 0.9, 0.999, 1e-08, 0.01, 10

V7X_VMEM_BYTES = 64 * 1024 * 1024
VMEM_LIMIT = V7X_VMEM_BYTES * 3 // 4
LANES = 128
COL_TILE = 8 * LANES
N_CHIPS = 4
N_DEV = 8

_DIMS = {
    "nn": (((1,), (0,)), ((), ())),
    "nt": (((1,), (1,)), ((), ())),
    "tn": (((0,), (0,)), ((), ())),
}


def _cp(sem):
    return pltpu.CompilerParams(dimension_semantics=sem, vmem_limit_bytes=VMEM_LIMIT)


def _mesh_pos():
    return lax.axis_index("x"), lax.axis_index("y"), lax.axis_index("c")


def _mm(name, pairs, *, mode, grid, out_shape, o_spec, nk=1, kaxis=None, add=None, deps=(), prev=None):
    npair = len(pairs)
    has_add = add is not None

    def body(*refs):
        ab = refs[: 2 * npair]
        pos = 2 * npair
        add_ref = refs[pos] if has_add else None
        pos += int(has_add) + len(deps) + (prev is not None)
        o_ref = refs[pos]
        acc_ref = refs[pos + 1] if nk > 1 else None
        d = None
        for p in range(npair):
            t = lax.dot_general(ab[2 * p][...], ab[2 * p + 1][...], _DIMS[mode], preferred_element_type=F32)
            d = t if d is None else d + t
        if nk == 1:
            if has_add:
                d = d + add_ref[...].astype(F32)
            o_ref[...] = d.astype(o_ref.dtype)
        else:
            k = pl.program_id(kaxis)

            @pl.when(k == 0)
            def _():
                acc_ref[...] = d

            @pl.when(k > 0)
            def _():
                acc_ref[...] += d

            @pl.when(k == nk - 1)
            def _():
                r = acc_ref[...]
                if has_add:
                    r = r + add_ref[...].astype(F32)
                o_ref[...] = r.astype(o_ref.dtype)

    args, specs = [], []
    for a, a_spec, b, b_spec in pairs:
        args += [a, b]
        specs += [a_spec, b_spec]
    if has_add:
        args.append(add[0])
        specs.append(add[1])
    args += list(deps)
    specs += [ANY] * len(deps)
    aliases = {}
    if prev is not None:
        aliases = {len(args): 0}
        args.append(prev)
        specs.append(ANY)
    scratch = []
    if nk > 1:
        blk = [d for d in o_spec.block_shape if d is not None]
        scratch = [pltpu.VMEM(tuple(blk), F32)]
    sem = tuple("arbitrary" if (nk > 1 and ax == kaxis) else "parallel" for ax in range(len(grid)))
    return pl.pallas_call(
        body, name=name, grid=grid, in_specs=specs, out_specs=o_spec, out_shape=out_shape,
        scratch_shapes=scratch, input_output_aliases=aliases, compiler_params=_cp(sem),
    )(*args)


def _tile_span(n_tiles, part):
    if part is None:
        return 0, n_tiles
    p, of = part
    return p * n_tiles // of, (p + 1) * n_tiles // of


def _tile(n, pref):
    if n <= pref:
        return n
    for t in range(pref, 0, -LANES):
        if t % LANES == 0 and n % t == 0:
            return t
    raise ValueError(f"no tile for {n}")


def _mm_nn(name, a, b, out_dtype, add=None, tk=None, deps=(), part=None, prev=None):
    m, kk = a.shape
    n = b.shape[1]
    tm, tn = _tile(m, 1024), _tile(n, COL_TILE)
    out_shape = jax.ShapeDtypeStruct((m, n), out_dtype)
    if tk is None or tk == kk:
        j0, j1 = _tile_span(n // tn, part)
        grid = (m // tm, j1 - j0)
        pairs = [(a, pl.BlockSpec((tm, kk), lambda i, j: (i, 0)), b, pl.BlockSpec((kk, tn), lambda i, j: (0, j0 + j)))]
        o_spec = pl.BlockSpec((tm, tn), lambda i, j: (i, j0 + j))
        add_ = None if add is None else (add, pl.BlockSpec((tm, tn), lambda i, j: (i, j0 + j)))
        return _mm(name, pairs, mode="nn", grid=grid, out_shape=out_shape, o_spec=o_spec, add=add_, deps=deps, prev=prev)
    tn = _tile(n, 1024)
    nk = kk // tk
    grid = (m // tm, n // tn, nk)
    pairs = [(a, pl.BlockSpec((tm, tk), lambda i, j, k: (i, k)), b, pl.BlockSpec((tk, tn), lambda i, j, k: (k, j)))]
    o_spec = pl.BlockSpec((tm, tn), lambda i, j, k: (i, j))
    add_ = None if add is None else (add, pl.BlockSpec((tm, tn), lambda i, j, k: (i, j)))
    return _mm(name, pairs, mode="nn", grid=grid, out_shape=out_shape, o_spec=o_spec, nk=nk, kaxis=2, add=add_, deps=deps)


def _mm_nt(name, abs_, out_dtype, tk, deps=()):
    m, kk = abs_[0][0].shape
    n = abs_[0][1].shape[0]
    tm = _tile(m, 1024)
    nk = kk // tk
    tn = _tile(n, COL_TILE if nk == 1 else 1024)
    out_shape = jax.ShapeDtypeStruct((m, n), out_dtype)
    if nk == 1:
        grid = (m // tm, n // tn)
        pairs = [(a, pl.BlockSpec((tm, kk), lambda i, j: (i, 0)), b, pl.BlockSpec((tn, kk), lambda i, j: (j, 0)))
                 for a, b in abs_]
        o_spec = pl.BlockSpec((tm, tn), lambda i, j: (i, j))
        return _mm(name, pairs, mode="nt", grid=grid, out_shape=out_shape, o_spec=o_spec, deps=deps)
    grid = (m // tm, n // tn, nk)
    pairs = [(a, pl.BlockSpec((tm, tk), lambda i, j, k: (i, k)), b, pl.BlockSpec((tn, tk), lambda i, j, k: (j, k)))
             for a, b in abs_]
    o_spec = pl.BlockSpec((tm, tn), lambda i, j, k: (i, j))
    return _mm(name, pairs, mode="nt", grid=grid, out_shape=out_shape, o_spec=o_spec, nk=nk, kaxis=2, deps=deps)


def _mm_tn(name, a, b, out_dtype, deps=()):
    t, m = a.shape
    n = b.shape[1]
    tm, tn = _tile(m, 512), _tile(n, 2048)
    if n > m:
        grid = (n // tn, m // tm)
        a_map, b_map, o_map = (lambda j, i: (0, i)), (lambda j, i: (0, j)), (lambda j, i: (i, j))
    else:
        grid = (m // tm, n // tn)
        a_map, b_map, o_map = (lambda i, j: (0, i)), (lambda i, j: (0, j)), (lambda i, j: (i, j))
    pairs = [(a, pl.BlockSpec((t, tm), a_map), b, pl.BlockSpec((t, tn), b_map))]
    o_spec = pl.BlockSpec((tm, tn), o_map)
    return _mm(name, pairs, mode="tn", grid=grid, out_shape=jax.ShapeDtypeStruct((m, n), out_dtype), o_spec=o_spec,
               deps=deps)


def _mm_tn_half(name, a, b, pos, mine, add=None, deps=()):
    t, m = a.shape
    r, n = m // 2, b.shape[1]
    tm, tn = _tile(r, 512), _tile(n, 2048)
    nbi = r // tm
    half = (lambda pos: pos[0]) if mine else (lambda pos: 1 - pos[0])
    if n > r:
        grid, ij = (n // tn, nbi), (lambda g0, g1: (g1, g0))
    else:
        grid, ij = (nbi, n // tn), (lambda g0, g1: (g0, g1))
    has_add = add is not None

    def body(pos_ref, a_ref, b_ref, *rest):
        d = lax.dot_general(a_ref[...], b_ref[...], _DIMS["tn"], preferred_element_type=F32)
        if has_add:
            d = d + rest[0][...].astype(F32)
        rest[-1][...] = d.astype(BF16)

    o_spec = pl.BlockSpec((None, tm, tn), lambda g0, g1, pos: (0, *ij(g0, g1)))
    grid_spec = pltpu.PrefetchScalarGridSpec(
        num_scalar_prefetch=1, grid=grid,
        in_specs=[pl.BlockSpec((t, tm), lambda g0, g1, pos: (0, half(pos) * nbi + ij(g0, g1)[0])),
                  pl.BlockSpec((t, tn), lambda g0, g1, pos: (0, ij(g0, g1)[1]))]
        + ([o_spec] if has_add else []) + [ANY] * len(deps),
        out_specs=o_spec)
    return pl.pallas_call(body, name=name, grid_spec=grid_spec, out_shape=jax.ShapeDtypeStruct((1, r, n), BF16),
                          compiler_params=_cp(("parallel",) * 2))(pos, a, b, *([add] if has_add else []), *deps)


def _proj_piece(name, h, w, prev, kvec, base, count, deps=()):
    t, kk = h.shape
    own = w.dtype == F32
    nn = w.shape[1] if own else w.shape[1] // N_CHIPS
    tm, tn = _tile(t, 1024), _tile(nn, COL_TILE)
    nb = nn // tn

    def body(kv_ref, h_ref, w_ref, *rest):
        rest[-1][...] = lax.dot_general(h_ref[...], w_ref[...].astype(BF16), _DIMS["nn"],
                                        preferred_element_type=F32).astype(BF16)

    cols = lambda s, i, j, kv: (0, j) if own else (0, kv[base + s] * nb + j)
    extra = ([] if prev is None else [prev]) + list(deps)
    grid_spec = pltpu.PrefetchScalarGridSpec(
        num_scalar_prefetch=1, grid=(count, t // tm, nb),
        in_specs=[pl.BlockSpec((tm, kk), lambda s, i, j, kv: (i, 0)), pl.BlockSpec((kk, tn), cols)] + [ANY] * len(extra),
        out_specs=pl.BlockSpec((tm, tn), lambda s, i, j, kv: (i, kv[base + s] * nb + j)))
    return pl.pallas_call(body, name=name, grid_spec=grid_spec, out_shape=jax.ShapeDtypeStruct((t, N_CHIPS * nn), BF16),
                          input_output_aliases={} if prev is None else {3: 0},
                          compiler_params=_cp(("parallel",) * 3))(kvec, h, w, *extra)


def _gmm_nn(name, p, w, out_dtype):
    t = p.shape[0]
    g, cg, dg = w.shape
    tm = _tile(t, 1024)
    pairs = [(p, pl.BlockSpec((tm, cg), lambda i, j: (i, j)), w, pl.BlockSpec((None, cg, dg), lambda i, j: (j, 0, 0)))]
    o_spec = pl.BlockSpec((tm, dg), lambda i, j: (i, j))
    return _mm(name, pairs, mode="nn", grid=(t // tm, g), out_shape=jax.ShapeDtypeStruct((t, g * dg), out_dtype),
               o_spec=o_spec)


def _gmm_nt(name, dy, w, out_dtype):
    t = dy.shape[0]
    g, cg, dg = w.shape
    tm = _tile(t, 1024)
    pairs = [(dy, pl.BlockSpec((tm, dg), lambda i, j: (i, j)), w, pl.BlockSpec((None, cg, dg), lambda i, j: (j, 0, 0)))]
    o_spec = pl.BlockSpec((tm, cg), lambda i, j: (i, j))
    return _mm(name, pairs, mode="nt", grid=(t // tm, g), out_shape=jax.ShapeDtypeStruct((t, g * cg), out_dtype),
               o_spec=o_spec)


def _gmm_tn(name, p, dy, g, out_dtype):
    t = p.shape[0]
    cg, dg = p.shape[1] // g, dy.shape[1] // g
    pairs = [(p, pl.BlockSpec((t, cg), lambda j: (0, j)), dy, pl.BlockSpec((t, dg), lambda j: (0, j)))]
    o_spec = pl.BlockSpec((None, cg, dg), lambda j: (j, 0, 0))
    return _mm(name, pairs, mode="tn", grid=(g,), out_shape=jax.ShapeDtypeStruct((g, cg, dg), out_dtype), o_spec=o_spec)


ROW_TILE = 256


def _rows(t):
    return _tile8(t, ROW_TILE)


def _tile8(n, pref):
    if n <= pref:
        return n
    for t in range(pref, 0, -8):
        if n % t == 0:
            return t
    raise ValueError(f"no row tile for {n}")


def _cast_place(name, w, pos, shard, deps=()):
    tr = _tile8(w.R, 512)
    if w.colshard:
        o_map = lambda h, i, pos: (0, h, i, pos[1])
    else:
        o_map = lambda h, i, pos: (pos[1], h, i, 0)

    def body(pos_ref, w_ref, *rest):
        rest[-1][...] = w_ref[...].astype(BF16)

    grid_spec = pltpu.PrefetchScalarGridSpec(
        num_scalar_prefetch=1, grid=(2, w.R // tr),
        in_specs=[pl.BlockSpec((None, tr, w.nn), lambda h, i, pos: (h, i, 0))] + [ANY] * len(deps),
        out_specs=pl.BlockSpec((None, None, tr, w.nn), o_map))
    return pl.pallas_call(body, name=name, grid_spec=grid_spec, out_shape=jax.ShapeDtypeStruct((w.P, 2, w.R, w.N), BF16),
                          compiler_params=_cp(("parallel", "parallel")))(pos, shard, *deps)


def _rms_fwd(name, x, g, deps=()):
    t, d = x.shape
    tm = _rows(t)

    def body(x_ref, g_ref, *rest):
        xf = x_ref[...]
        r = lax.rsqrt(jnp.mean(xf * xf, axis=-1, keepdims=True) + EPS)
        rest[-1][...] = (xf * r * g_ref[...]).astype(BF16)

    return pl.pallas_call(
        body, name=name, grid=(t // tm,),
        in_specs=[pl.BlockSpec((tm, d), lambda i: (i, 0)), pl.BlockSpec((1, d), lambda i: (0, 0))] + [ANY] * len(deps),
        out_specs=pl.BlockSpec((tm, d), lambda i: (i, 0)), out_shape=jax.ShapeDtypeStruct((t, d), BF16),
        compiler_params=_cp(("parallel",)),
    )(x, g, *deps)


def _rms_bwd(name, x, g, dh, dres, want_bf16, deps=()):
    t, d = x.shape
    tm = _rows(t)

    def body(x_ref, g_ref, dh_ref, dres_ref, *rest):
        rest = rest[len(deps):]
        dx_ref, rest = rest[0], rest[1:]
        dg_ref = rest[-1]
        xf = x_ref[...]
        r = lax.rsqrt(jnp.mean(xf * xf, axis=-1, keepdims=True) + EPS)
        xh = xf * r
        dhf = dh_ref[...].astype(F32)
        dxh = dhf * g_ref[...]
        m = jnp.mean(dxh * xh, axis=-1, keepdims=True)
        dx = dres_ref[...] + r * (dxh - xh * m)
        dx_ref[...] = dx
        if want_bf16:
            rest[0][...] = dx.astype(BF16)

        @pl.when(pl.program_id(0) == 0)
        def _():
            dg_ref[...] = jnp.zeros_like(dg_ref)

        dg_ref[...] += jnp.sum(dhf * xh, axis=0, keepdims=True)

    row = pl.BlockSpec((tm, d), lambda i: (i, 0))
    vec = pl.BlockSpec((1, d), lambda i: (0, 0))
    out_specs = [row] + ([row] if want_bf16 else []) + [vec]
    out_shape = ([jax.ShapeDtypeStruct((t, d), F32)] + ([jax.ShapeDtypeStruct((t, d), BF16)] if want_bf16 else [])
                 + [jax.ShapeDtypeStruct((1, d), F32)])
    return pl.pallas_call(body, name=name, grid=(t // tm,), in_specs=[row, vec, row, row] + [ANY] * len(deps),
                          out_specs=out_specs, out_shape=out_shape, compiler_params=_cp(("arbitrary",)))(x, g, dh, dres, *deps)


def _final_bwd(name, x3, gf, tgt):
    t, d = x3.shape
    tm = _rows(t)

    def body(x_ref, g_ref, t_ref, dx_ref, dxb_ref, dg_ref, lc_ref):
        xf = x_ref[...]
        g = g_ref[...]
        r = lax.rsqrt(jnp.mean(xf * xf, axis=-1, keepdims=True) + EPS)
        xh = xf * r
        diff = xh * g - t_ref[...]
        dy = diff * (1.0 / d)
        dxh = dy * g
        m = jnp.mean(dxh * xh, axis=-1, keepdims=True)
        dx = r * (dxh - xh * m)
        dx_ref[...] = dx
        dxb_ref[...] = dx.astype(BF16)

        @pl.when(pl.program_id(0) == 0)
        def _():
            dg_ref[...] = jnp.zeros_like(dg_ref)
            lc_ref[...] = jnp.zeros_like(lc_ref)

        dg_ref[...] += jnp.sum(dy * xh, axis=0, keepdims=True)
        lc_ref[...] += jnp.sum(diff * diff, axis=0, keepdims=True) * (0.5 / d)

    row = pl.BlockSpec((tm, d), lambda i: (i, 0))
    vec = pl.BlockSpec((1, d), lambda i: (0, 0))
    return pl.pallas_call(
        body, name=name, grid=(t // tm,), in_specs=[row, vec, row], out_specs=[row, row, vec, vec],
        out_shape=[jax.ShapeDtypeStruct((t, d), F32), jax.ShapeDtypeStruct((t, d), BF16),
                   jax.ShapeDtypeStruct((1, d), F32), jax.ShapeDtypeStruct((1, d), F32)],
        compiler_params=_cp(("arbitrary",)),
    )(x3, gf, tgt)


def _shift_down(v, k, t_idx):
    return jnp.where(t_idx >= k, pltpu.roll(v, k, 0), 0.0)


def _shift_up(v, k, t_idx):
    n = v.shape[0]
    return jnp.where(t_idx < n - k, pltpu.roll(v, n - k, 0), 0.0)


def _window_sums(v, shift, t_idx, grp):
    s = v + shift(v, 1, t_idx)
    out = s
    for lvl in range(1, len(POOL_WINDOWS)):
        s = s + shift(s, 1 << lvl, t_idx)
        out = jnp.where(grp >= lvl, s, out)
    return out


def _window_weight(t_idx, grp):
    return 1.0 / jnp.minimum(t_idx[:, :1] + 1, jnp.left_shift(2, grp)).astype(F32)


MIX_COLS = 128


def _mixer_fwd(name, proj, cw, cb, n_conv, n_groups, deps=()):
    t = proj.shape[0]
    nb = n_conv // MIX_COLS
    per_group = n_conv // n_groups // MIX_COLS

    def body(ba_ref, ca_ref, va_ref, vb_ref, cw_ref, cb_ref, *rest):
        z_ref, p_ref = rest[len(deps):]
        t_idx = lax.broadcasted_iota(jnp.int32, (t, MIX_COLS), 0)
        q = ca_ref[...].astype(F32) * va_ref[...].astype(F32)
        w = cw_ref[...]
        u = cb_ref[...] + w[0:1] * _shift_down(q, 2, t_idx) + w[1:2] * _shift_down(q, 1, t_idx) + w[2:3] * q
        z_ref[...] = (ba_ref[...].astype(F32) * u).astype(BF16)
        grp = pl.program_id(0) // per_group
        v = vb_ref[...].astype(F32)
        p_ref[...] = (_window_sums(v, _shift_down, t_idx, grp) * _window_weight(t_idx, grp) - v).astype(BF16)

    col = lambda s: pl.BlockSpec((t, MIX_COLS), lambda j: (0, s * nb + j))
    return pl.pallas_call(
        body, name=name, grid=(nb,),
        in_specs=[col(0), col(1), col(2), col(3), pl.BlockSpec((3, MIX_COLS), lambda j: (0, j)),
                  pl.BlockSpec((1, MIX_COLS), lambda j: (0, j))] + [ANY] * len(deps),
        out_specs=[col(0), col(0)],
        out_shape=[jax.ShapeDtypeStruct((t, n_conv), BF16), jax.ShapeDtypeStruct((t, n_conv), BF16)],
        compiler_params=_cp(("parallel",)),
    )(proj, proj, proj, proj, cw, cb, *deps)


def _mixer_bwd(name, dz, dp, proj, cw, cb, dproj, n_conv, n_groups, deps=()):
    t = proj.shape[0]
    nb = n_conv // MIX_COLS
    per_group = n_conv // n_groups // MIX_COLS

    def body(dz_ref, dp_ref, ba_ref, ca_ref, va_ref, cw_ref, cb_ref, _, *rest):
        o_ref, dcw_ref, dcb_ref, scr = rest[len(deps):]
        s = pl.program_id(1)

        @pl.when(s == 0)
        def _():
            t_idx = lax.broadcasted_iota(jnp.int32, (t, MIX_COLS), 0)
            ca, va = ca_ref[...].astype(F32), va_ref[...].astype(F32)
            q = ca * va
            q1, q2 = _shift_down(q, 1, t_idx), _shift_down(q, 2, t_idx)
            w = cw_ref[...]
            u = cb_ref[...] + w[0:1] * q2 + w[1:2] * q1 + w[2:3] * q
            dzf = dz_ref[...].astype(F32)
            du = dzf * ba_ref[...].astype(F32)
            scr[0] = (dzf * u).astype(BF16)
            dq = w[2:3] * du + w[1:2] * _shift_up(du, 1, t_idx) + w[0:1] * _shift_up(du, 2, t_idx)
            scr[1] = (dq * va).astype(BF16)
            scr[2] = (dq * ca).astype(BF16)
            dcb_ref[...] = jnp.sum(du, axis=0, keepdims=True)
            dcw_ref[0:1, :] = jnp.sum(du * q2, axis=0, keepdims=True)
            dcw_ref[1:2, :] = jnp.sum(du * q1, axis=0, keepdims=True)
            dcw_ref[2:3, :] = jnp.sum(du * q, axis=0, keepdims=True)
            grp = pl.program_id(0) // per_group
            dpf = dp_ref[...].astype(F32)
            e = dpf * _window_weight(t_idx, grp)
            scr[3] = (_window_sums(e, _shift_up, t_idx, grp) - dpf).astype(BF16)

        o_ref[...] = scr[s]

    col = lambda c: pl.BlockSpec((t, MIX_COLS), lambda j, s: (0, c * nb + j))
    own = pl.BlockSpec((t, MIX_COLS), lambda j, s: (0, j))
    return pl.pallas_call(
        body, name=name, grid=(nb, 4),
        in_specs=[own, own, col(0), col(1), col(2), pl.BlockSpec((3, MIX_COLS), lambda j, s: (0, j)),
                  pl.BlockSpec((1, MIX_COLS), lambda j, s: (0, j)), ANY] + [ANY] * len(deps),
        out_specs=[pl.BlockSpec((t, MIX_COLS), lambda j, s: (0, s * nb + j)),
                   pl.BlockSpec((3, MIX_COLS), lambda j, s: (0, j)), pl.BlockSpec((1, MIX_COLS), lambda j, s: (0, j))],
        out_shape=[jax.ShapeDtypeStruct(dproj.shape, BF16), jax.ShapeDtypeStruct((3, n_conv), F32),
                   jax.ShapeDtypeStruct((1, n_conv), F32)],
        scratch_shapes=[pltpu.VMEM((4, t, MIX_COLS), BF16)],
        input_output_aliases={7: 0},
        compiler_params=_cp(("arbitrary", "arbitrary")),
    )(dz, dp, proj, proj, proj, cw, cb, dproj, *deps)


def _merge_fwd(name, proj, bg, ya, yb, ps):
    t, d = ya.shape
    tm = _rows(t)

    def body(gab_ref, bg_ref, ya_ref, yb_ref, ps_ref, o_ref):
        gab = gab_ref[...].astype(F32) + bg_ref[...]
        sa, sb = jax.nn.sigmoid(gab[:, :d]), jax.nn.sigmoid(gab[:, d:])
        o_ref[...] = (sa * ya_ref[...].astype(F32) + sb * (yb_ref[...].astype(F32) * ps_ref[...])).astype(BF16)

    row = pl.BlockSpec((tm, d), lambda i: (i, 0))
    return pl.pallas_call(
        body, name=name, grid=(t // tm,),
        in_specs=[pl.BlockSpec((tm, 2 * d), lambda i: (i, 1)), pl.BlockSpec((1, 2 * d), lambda i: (0, 0)), row, row,
                  pl.BlockSpec((1, d), lambda i: (0, 0))],
        out_specs=row, out_shape=jax.ShapeDtypeStruct((t, d), BF16), compiler_params=_cp(("parallel",)),
    )(proj, bg, ya, yb, ps)


def _merge_bwd(name, dm, proj, bg, ya, yb, ps, deps=()):
    t, d = ya.shape
    tm = _rows(t)

    def body(dm_ref, gab_ref, bg_ref, ya_ref, yb_ref, ps_ref, *rest):
        dya_ref, dyb_ref, dg_ref, dba_ref, dbb_ref, dps_ref = rest[len(deps):]
        gab = gab_ref[...].astype(F32) + bg_ref[...]
        sa, sb = jax.nn.sigmoid(gab[:, :d]), jax.nn.sigmoid(gab[:, d:])
        dmf = dm_ref[...].astype(F32)
        ybf, ps_ = yb_ref[...].astype(F32), ps_ref[...]
        dya_ref[...] = (dmf * sa).astype(BF16)
        dyb = dmf * sb
        dyb_ref[...] = (dyb * ps_).astype(BF16)
        dga = dmf * ya_ref[...].astype(F32) * sa * (1.0 - sa)
        dgb = dmf * (ybf * ps_) * sb * (1.0 - sb)
        dg_ref[:, :d] = dga.astype(BF16)
        dg_ref[:, d:] = dgb.astype(BF16)

        @pl.when(pl.program_id(0) == 0)
        def _():
            dba_ref[...] = jnp.zeros_like(dba_ref)
            dbb_ref[...] = jnp.zeros_like(dbb_ref)
            dps_ref[...] = jnp.zeros_like(dps_ref)

        dba_ref[...] += jnp.sum(dga, axis=0, keepdims=True)
        dbb_ref[...] += jnp.sum(dgb, axis=0, keepdims=True)
        dps_ref[...] += jnp.sum(dyb * ybf, axis=0, keepdims=True)

    row = pl.BlockSpec((tm, d), lambda i: (i, 0))
    vec = pl.BlockSpec((1, d), lambda i: (0, 0))
    gates = pl.BlockSpec((tm, 2 * d), lambda i: (i, 1))
    return pl.pallas_call(
        body, name=name, grid=(t // tm,),
        in_specs=[row, gates, pl.BlockSpec((1, 2 * d), lambda i: (0, 0)), row, row, vec] + [ANY] * len(deps),
        out_specs=[row, row, gates, vec, vec, vec],
        out_shape=[jax.ShapeDtypeStruct((t, d), BF16), jax.ShapeDtypeStruct((t, d), BF16),
                   jax.ShapeDtypeStruct(proj.shape, BF16), jax.ShapeDtypeStruct((1, d), F32),
                   jax.ShapeDtypeStruct((1, d), F32), jax.ShapeDtypeStruct((1, d), F32)],
        compiler_params=_cp(("arbitrary",)),
    )(dm, proj, bg, ya, yb, ps, *deps)


def _row_chunks(tm, rows=256):
    return [pl.ds(r, min(rows, tm)) for r in range(0, tm, rows)]


def _ffn_up_act(name, h, w_up, gate, part=None, prev=None, deps=()):
    t, d = h.shape
    f = w_up.shape[1]
    tm, tf = _tile(t, 1024), _tile(f, 512)
    j0, j1 = _tile_span(f // tf, part)
    n_prev = 0 if prev is None else 2
    extra = ([] if prev is None else list(prev)) + list(deps)

    def body(h_ref, w_ref, g_ref, *rest):
        u_ref, a_ref = rest[len(extra):]
        for rows in _row_chunks(tm):
            u = lax.dot_general(h_ref[rows, :], w_ref[...], _DIMS["nn"], preferred_element_type=F32)
            g = g_ref[rows, :].astype(F32)
            u_ref[rows, :] = u.astype(BF16)
            a_ref[rows, :] = (g * jax.nn.sigmoid(g) * u).astype(BF16)

    blk = pl.BlockSpec((tm, tf), lambda i, j: (i, j0 + j))
    shp = jax.ShapeDtypeStruct((t, f), BF16)
    return pl.pallas_call(
        body, name=name, grid=(t // tm, j1 - j0),
        in_specs=[pl.BlockSpec((tm, d), lambda i, j: (i, 0)), pl.BlockSpec((d, tf), lambda i, j: (0, j0 + j)), blk]
        + [ANY] * len(extra),
        out_specs=[blk, blk], out_shape=[shp, shp], input_output_aliases={3 + i: i for i in range(n_prev)},
        compiler_params=_cp(("parallel", "parallel")))(h, w_up, gate, *extra)


def _ffn_bwd(name, dy, w_down, gate, up):
    t, d = dy.shape
    f = w_down.shape[0]
    tm, tf = _tile(t, 1024), _tile(f, 512)

    def body(dy_ref, w_ref, g_ref, u_ref, dg_ref, du_ref):
        for rows in _row_chunks(tm):
            da = lax.dot_general(dy_ref[rows, :], w_ref[...], _DIMS["nt"], preferred_element_type=F32)
            g = g_ref[rows, :].astype(F32)
            s = jax.nn.sigmoid(g)
            du_ref[rows, :] = (da * (g * s)).astype(BF16)
            dg_ref[rows, :] = (da * u_ref[rows, :].astype(F32) * (s * (1.0 + g * (1.0 - s)))).astype(BF16)

    blk = pl.BlockSpec((tm, tf), lambda i, j: (i, j))
    shp = jax.ShapeDtypeStruct((t, f), BF16)
    return pl.pallas_call(
        body, name=name, grid=(t // tm, f // tf),
        in_specs=[pl.BlockSpec((tm, d), lambda i, j: (i, 0)), pl.BlockSpec((tf, d), lambda i, j: (j, 0)), blk, blk],
        out_specs=[blk, blk], out_shape=[shp, shp], compiler_params=_cp(("parallel", "parallel")))(dy, w_down, gate, up)


def _adamw_math(w, g, m, v):
    m = ADAM_B1 * m + (1.0 - ADAM_B1) * g
    v = ADAM_B2 * v + (1.0 - ADAM_B2) * (g * g)
    m_hat = m / (1.0 - ADAM_B1 ** ADAM_STEP)
    v_hat = v / (1.0 - ADAM_B2 ** ADAM_STEP)
    delta = -ADAM_LR * (m_hat / (jnp.sqrt(v_hat) + ADAM_EPS) + ADAM_WD * w)
    return delta, m, v


def _adamw(name, w, g, m, v):
    r, c = w.shape
    tr = _tile8(r, 512 if c <= 1024 else 256)

    def body(w_ref, g_ref, m_ref, v_ref, go_ref, d_ref, nm_ref, nv_ref):
        g = g_ref[...]
        go_ref[...] = g
        d_ref[...], nm_ref[...], nv_ref[...] = _adamw_math(w_ref[...], g, m_ref[...], v_ref[...])

    blk = pl.BlockSpec((tr, c), lambda i: (i, 0))
    shp = jax.ShapeDtypeStruct((r, c), F32)
    return pl.pallas_call(body, name=name, grid=(r // tr,), in_specs=[blk] * 4, out_specs=[blk] * 4,
                          out_shape=[shp] * 4, compiler_params=_cp(("parallel",)))(w, g, m, v)


class _Weight:
    def __init__(self, name, rows, cols, colshard):
        self.name, self.colshard = name, colshard
        self.R, self.nn = rows // 2, cols
        self.P = 1 if colshard else N_CHIPS
        self.N = N_CHIPS * cols if colshard else cols

    def cols(self, k):
        return pl.ds(pl.multiple_of(k * self.nn, LANES), self.nn)

    def shard(self, ref, k):
        return ref.at[0, :, :, self.cols(k)] if self.colshard else ref.at[k]

    def half(self, ref, k, h):
        return ref.at[0, h, :, self.cols(k)] if self.colshard else ref.at[k, h]

    def quarter(self, ref, k, h, q):
        return self.half(ref, k, h).at[pl.ds(q * (self.R // 2), self.R // 2), :]

    def part(self, ref, k):
        return ref.at[0, :, self.cols(k)] if self.colshard else ref.at[k]


def _remote(src, dst, ssem, rsem, dev):
    return pltpu.make_async_remote_copy(src_ref=src, dst_ref=dst, send_sem=ssem, recv_sem=rsem, device_id=dev,
                                        device_id_type=MESH)


def _other_chips(x, y):
    chips = [(1 - x, y), (x, 1 - y), (1 - x, 1 - y)]
    return chips, [2 * cx + cy for cx, cy in chips]


def _hbm(a):
    return pltpu.with_memory_space_constraint(a, pltpu.HBM)


def _gather_start(name, groups, lands, after=()):
    flat = [w for grp in groups for w in grp]
    nw, ng = len(flat), len(groups)

    def body(*refs):
        land = refs[:nw]
        sems = refs[nw + len(after):nw + len(after) + 2 * ng]
        token = refs[2 * nw + len(after) + 2 * ng]
        x, y, c = _mesh_pos()
        k_me = 2 * x + y
        chips, _ = _other_chips(x, y)
        i = 0
        for g, grp in enumerate(groups):
            for wi, w in enumerate(grp):
                mine = w.half(land[i], k_me, c)
                for j, chip in enumerate(chips):
                    _remote(mine, mine, sems[2 * g].at[3 * wi + j], sems[2 * g + 1].at[3 * wi + j], (*chip, c)).start()
                i += 1
        token[...] = jnp.zeros_like(token)

    sem_shapes = []
    for grp in groups:
        sem_shapes += [pltpu.SemaphoreType.DMA((3 * len(grp),))] * 2
    out = pl.pallas_call(
        body, name=name, in_specs=[HBM] * nw + [ANY] * len(after),
        out_specs=[SEM] * (2 * ng) + [HBM] * nw + [VMEM],
        out_shape=sem_shapes + [pltpu.HBM(a.shape, a.dtype) for a in lands] + [jax.ShapeDtypeStruct((8, LANES), F32)],
        input_output_aliases={i: 2 * ng + i for i in range(nw)},
        compiler_params=pltpu.CompilerParams(has_side_effects=EFFECT),
    )(*[_hbm(a) for a in lands], *after)
    sems = [(out[2 * g], out[2 * g + 1]) for g in range(ng)]
    return sems, list(out[2 * ng:2 * ng + nw]), out[-1]


def _gather_wait(name, grp, lands, ssem, rsem, after):
    n = len(grp)

    def body(*refs):
        land, ssem_ref, rsem_ref = refs[:n], refs[n], refs[n + 1]
        x, y, c = _mesh_pos()
        k_me = 2 * x + y
        chips, ks = _other_chips(x, y)
        for wi, w in enumerate(grp):
            for j, chip in enumerate(chips):
                cp = _remote(w.half(land[wi], k_me, c), w.half(land[wi], ks[j], c), ssem_ref.at[3 * wi + j],
                             rsem_ref.at[3 * wi + j], (*chip, c))
                cp.wait_send()
                cp.wait_recv()

    return pl.pallas_call(
        body, name=name, in_specs=[HBM] * n + [SEM, SEM, ANY], out_specs=[HBM] * n,
        out_shape=[pltpu.HBM(a.shape, a.dtype) for a in lands], input_output_aliases={i: i for i in range(n)},
        compiler_params=pltpu.CompilerParams(has_side_effects=EFFECT),
    )(*lands, ssem, rsem, after)


def _split_start(name, arrays, n, copies, after=()):
    na = len(arrays)

    def body(*refs):
        ssem, rsem, token = refs[na + len(after):][0], refs[na + len(after):][1], refs[2 * na + len(after) + 2]
        for i, (src, dst, dev, _) in enumerate(copies(refs[:na], *_mesh_pos())):
            _remote(src, dst, ssem.at[i], rsem.at[i], dev).start()
        token[...] = jnp.zeros_like(token)

    out = pl.pallas_call(
        body, name=name, in_specs=[HBM] * na + [ANY] * len(after), out_specs=[SEM, SEM] + [HBM] * na + [VMEM],
        out_shape=[pltpu.SemaphoreType.DMA((n,))] * 2 + [pltpu.HBM(a.shape, a.dtype) for a in arrays]
        + [jax.ShapeDtypeStruct((8, LANES), F32)],
        input_output_aliases={i: 2 + i for i in range(na)},
        compiler_params=pltpu.CompilerParams(has_side_effects=EFFECT),
    )(*[_hbm(a) for a in arrays], *after)
    return out[0], out[1], list(out[2:2 + na]), out[-1]


def _split_wait(name, arrays, ssem, rsem, copies, after):
    na = len(arrays)

    def body(*refs):
        for i, (src, _, dev, dst) in enumerate(copies(refs[:na], *_mesh_pos())):
            cp = _remote(src, dst, refs[na].at[i], refs[na + 1].at[i], dev)
            cp.wait_send()
            cp.wait_recv()

    return list(pl.pallas_call(
        body, name=name, in_specs=[HBM] * na + [SEM, SEM] + [ANY] * len(after), out_specs=[HBM] * na,
        out_shape=[pltpu.HBM(a.shape, a.dtype) for a in arrays], input_output_aliases={i: i for i in range(na)},
        compiler_params=pltpu.CompilerParams(has_side_effects=EFFECT),
    )(*arrays, ssem, rsem, *after))


def _pass_copies(grp, rels=(0, 1, 2)):
    def copies(land, x, y, c):
        _, ks = _other_chips(x, y)
        return [(w.half(land[wi], ks[j], c), w.half(land[wi], ks[j], c), (x, y, 1 - c), w.half(land[wi], ks[j], 1 - c))
                for wi, w in enumerate(grp) for j in rels]
    copies.n = len(grp) * len(rels)
    return copies


def _near_copies(grp):
    def copies(land, x, y, c):
        chips, ks = _other_chips(x, y)
        out = []
        for wi, w in enumerate(grp):
            mine = w.half(land[wi], 2 * x + y, c)
            out += [(mine, mine, (*chips[j], c), w.half(land[wi], ks[j], c)) for j in (0, 1)]
        return out
    copies.n = 2 * len(grp)
    return copies


def _far_copies(grp):
    def copies(land, x, y, c):
        chips, ks = _other_chips(x, y)
        out = []
        for wi, w in enumerate(grp):
            for j in (0, 1):
                q = w.quarter(land[wi], ks[j], c, j)
                out.append((q, q, (*chips[1 - j], c), w.quarter(land[wi], ks[2], c, j)))
        return out
    copies.n = 2 * len(grp)
    return copies


def _pair_copies(n, whole=False):
    def copies(refs, x, y, c):
        return [(refs[i] if whole else refs[i].at[:, 1 - c], refs[n + i], (x, y, 1 - c), refs[n + i]) for i in range(n)]
    return copies


def _share_copies(n):
    def copies(refs, x, y, c):
        return [(refs[i].at[c], refs[i].at[c], (x, y, 1 - c), refs[i].at[1 - c]) for i in range(n)]
    return copies


def _gather_conv_w(cw):
    ncw = cw.shape[1]

    def body(cw_ref, out_ref, ssem, rsem):
        x, y, c = _mesh_pos()
        k_me = 2 * x + y
        chips, ks = _other_chips(x, y)
        cols = lambda k: out_ref.at[:, pl.ds(pl.multiple_of(k * ncw, LANES), ncw)]
        cps = [_remote(cw_ref, cols(k_me), ssem.at[j], rsem.at[j], (*chip, c)) for j, chip in enumerate(chips)]
        for cp in cps:
            cp.start()
        for k in range(N_CHIPS):
            @pl.when(k_me == k)
            def _():
                out_ref[:, k * ncw:(k + 1) * ncw] = cw_ref[...]
        for j in range(3):
            _remote(cw_ref, cols(ks[j]), ssem.at[j], rsem.at[j], (*chips[j], c)).wait_recv()
        for cp in cps:
            cp.wait_send()

    return pl.pallas_call(
        body, name="gather_conv_w", in_specs=[VMEM], out_specs=VMEM,
        out_shape=jax.ShapeDtypeStruct((3, N_CHIPS * ncw), F32),
        scratch_shapes=[pltpu.SemaphoreType.DMA((3,)), pltpu.SemaphoreType.DMA((3,))],
    )(cw)


def _grad_tiles(w, n):
    return _tile8(w.R, 512) if w.R <= 512 else w.R // 2, _tile(n, 2048)


def _pair_sum(name, w, pos, grad, got):
    tr, tn = _grad_tiles(w, w.N)

    def body(pos_ref, g_ref, r_ref, o_ref):
        o_ref[...] = (g_ref[...].astype(F32) + r_ref[...].astype(F32)).astype(BF16)

    blk = pl.BlockSpec((None, tr, tn), lambda p, i, j, pos: (p, i, j))
    grid_spec = pltpu.PrefetchScalarGridSpec(
        num_scalar_prefetch=1, grid=(w.P, w.R // tr, w.N // tn),
        in_specs=[pl.BlockSpec((None, None, tr, tn), lambda p, i, j, pos: (p, pos[0], i, j)), blk], out_specs=blk)
    return pl.pallas_call(body, name=name, grid_spec=grid_spec, out_shape=jax.ShapeDtypeStruct((w.P, w.R, w.N), BF16),
                          compiler_params=_cp(("parallel",) * 3))(pos, grad, got)


def _scatter_start(name, ws, pairs):
    nw = len(ws)

    def body(*refs):
        pr, land = refs[:nw], refs[nw:2 * nw]
        ssem, rsem = refs[2 * nw], refs[2 * nw + 1]
        token = refs[4 * nw + 2]
        x, y, c = _mesh_pos()
        chips, ks = _other_chips(x, y)
        for i, w in enumerate(ws):
            for j, chip in enumerate(chips):
                _remote(w.part(pr[i], ks[j]), land[i].at[j], ssem.at[3 * i + j], rsem.at[3 * i + j], (*chip, c)).start()
        token[...] = jnp.zeros_like(token)

    lands = [lax.empty((3, w.R, w.nn), BF16) for w in ws]
    out = pl.pallas_call(
        body, name=name, in_specs=[HBM] * (2 * nw),
        out_specs=[SEM, SEM] + [HBM] * (2 * nw) + [VMEM],
        out_shape=[pltpu.SemaphoreType.DMA((3 * nw,))] * 2 + [pltpu.HBM(a.shape, a.dtype) for a in pairs + lands]
        + [jax.ShapeDtypeStruct((8, LANES), F32)],
        input_output_aliases={i: 2 + i for i in range(2 * nw)},
        compiler_params=pltpu.CompilerParams(has_side_effects=EFFECT),
    )(*[_hbm(a) for a in pairs + lands])
    return out[0], out[1], list(out[2:2 + nw]), list(out[2 + nw:2 + 2 * nw]), out[-1]


def _scatter_wait(name, ws, pairs, lands, ssem, rsem, after):
    nw = len(ws)

    def body(*refs):
        pr, land = refs[:nw], refs[nw:2 * nw]
        ssem_ref, rsem_ref = refs[2 * nw], refs[2 * nw + 1]
        x, y, c = _mesh_pos()
        chips, ks = _other_chips(x, y)
        for i, w in enumerate(ws):
            for j, chip in enumerate(chips):
                cp = _remote(w.part(pr[i], ks[j]), land[i].at[j], ssem_ref.at[3 * i + j], rsem_ref.at[3 * i + j], (*chip, c))
                cp.wait_send()
                cp.wait_recv()

    out = pl.pallas_call(
        body, name=name, in_specs=[HBM] * (2 * nw) + [SEM, SEM] + [ANY] * len(after), out_specs=[HBM] * (2 * nw),
        out_shape=[pltpu.HBM(a.shape, a.dtype) for a in pairs + lands],
        input_output_aliases={i: i for i in range(2 * nw)},
        compiler_params=pltpu.CompilerParams(has_side_effects=EFFECT),
    )(*pairs, *lands, ssem, rsem, *after)
    return list(out[:nw]), list(out[nw:])


def _final_sum(name, w, pos, grad, got, parts):
    tr, tn = _grad_tiles(w, w.nn)
    nbc = w.nn // tn
    if got is None:
        return _final_sum_pair(name, w, pos, grad, parts, tr, tn)

    def body(pos_ref, g_ref, r_ref, p_ref, o_ref):
        acc = g_ref[...].astype(F32) + r_ref[...].astype(F32)
        for j in range(3):
            acc = acc + p_ref[j].astype(F32)
        o_ref[...] = acc

    if w.colshard:
        g_spec = pl.BlockSpec((None, None, tr, tn), lambda i, j, pos: (0, pos[0], i, pos[1] * nbc + j))
        r_spec = pl.BlockSpec((None, tr, tn), lambda i, j, pos: (0, i, pos[1] * nbc + j))
    else:
        g_spec = pl.BlockSpec((None, None, tr, tn), lambda i, j, pos: (pos[1], pos[0], i, j))
        r_spec = pl.BlockSpec((None, tr, tn), lambda i, j, pos: (pos[1], i, j))
    grid_spec = pltpu.PrefetchScalarGridSpec(
        num_scalar_prefetch=1, grid=(w.R // tr, nbc),
        in_specs=[g_spec, r_spec, pl.BlockSpec((3, tr, tn), lambda i, j, pos: (0, i, j))],
        out_specs=pl.BlockSpec((None, tr, tn), lambda i, j, pos: (pos[0], i, j)))
    return pl.pallas_call(body, name=name, grid_spec=grid_spec, out_shape=jax.ShapeDtypeStruct((2, w.R, w.nn), F32),
                          compiler_params=_cp(("parallel",) * 2))(pos, grad, got, parts)


def _final_sum_pair(name, w, pos, pair, parts, tr, tn):
    nbc = w.nn // tn

    def body(pos_ref, g_ref, p_ref, o_ref):
        acc = g_ref[...].astype(F32)
        for j in range(3):
            acc = acc + p_ref[j].astype(F32)
        o_ref[...] = acc

    if w.colshard:
        g_spec = pl.BlockSpec((None, tr, tn), lambda i, j, pos: (0, i, pos[1] * nbc + j))
    else:
        g_spec = pl.BlockSpec((None, tr, tn), lambda i, j, pos: (pos[1], i, j))
    grid_spec = pltpu.PrefetchScalarGridSpec(
        num_scalar_prefetch=1, grid=(w.R // tr, nbc),
        in_specs=[g_spec, pl.BlockSpec((3, tr, tn), lambda i, j, pos: (0, i, j))],
        out_specs=pl.BlockSpec((None, tr, tn), lambda i, j, pos: (pos[0], i, j)))
    return pl.pallas_call(body, name=name, grid_spec=grid_spec, out_shape=jax.ShapeDtypeStruct((2, w.R, w.nn), F32),
                          compiler_params=_cp(("parallel",) * 2))(pos, pair, parts)


VEC_ROWS = 16


def _vector_step(d, n_conv, parts, params, deps=()):
    ncw = params[2][0].shape[1]
    n_par = len(params)

    def body(*refs):
        dg1, dba, dbb, dcw, dcb, dps, dg2, dgf, lc = refs[:9]
        wmv = refs[9:9 + 3 * n_par]
        refs = refs[9 + 3 * n_par + len(deps):]
        outs = refs[:4 * n_par]
        loss_ref = refs[4 * n_par]
        snd, got, ssem, rsem = refs[4 * n_par + 1:]
        x, y, c = _mesh_pos()
        me = 4 * x + 2 * y + c
        snd[...] = jnp.zeros_like(snd)
        for row, ref in ((0, dg1), (1, dba), (2, dbb), (3, dps), (4, dg2), (5, dgf), (6, lc)):
            snd[row:row + 1, :] = ref[...]
        snd[7:8, :n_conv] = dcb[...]
        snd[8:11, :n_conv] = dcw[...]
        cps = []
        for r in range(1, N_DEV):
            peer = tuple(1 - p if (r >> b) & 1 else p for p, b in ((x, 2), (y, 1), (c, 0)))
            cps.append(_remote(snd, got.at[me], ssem.at[r - 1], rsem.at[r - 1], peer))
        for cp in cps:
            cp.start()
        got[me] = snd[...]
        for r in range(1, N_DEV):
            peer = tuple(1 - p if (r >> b) & 1 else p for p, b in ((x, 2), (y, 1), (c, 0)))
            _remote(snd, got.at[4 * peer[0] + 2 * peer[1] + peer[2]], ssem.at[r - 1], rsem.at[r - 1], peer).wait_recv()
        for cp in cps:
            cp.wait_send()
        tot = got[0]
        for dev in range(1, N_DEV):
            tot = tot + got[dev]
        loss_ref[...] = jnp.sum(tot[6:7, :], axis=1, keepdims=True)
        k_me = 2 * x + y
        g_cw = jnp.zeros((3, ncw), F32)
        for k in range(N_CHIPS):
            g_cw = g_cw + jnp.where(k_me == k, tot[8:11, k * ncw:(k + 1) * ncw], 0.0)
        grads = [tot[0:1, :], jnp.concatenate([tot[1:2, :], tot[2:3, :]], axis=1), g_cw, tot[7:8, :n_conv],
                 tot[3:4, :], tot[4:5, :], tot[5:6, :]]
        for i, g in enumerate(grads):
            w_ref, m_ref, v_ref = wmv[3 * i:3 * i + 3]
            delta, nm, nv = _adamw_math(w_ref[...], g, m_ref[...], v_ref[...])
            outs[4 * i][...] = g
            outs[4 * i + 1][...] = delta
            outs[4 * i + 2][...] = nm
            outs[4 * i + 3][...] = nv

    args = list(parts)
    out_shape = []
    for w, m, v in params:
        args += [w, m, v]
        out_shape += [jax.ShapeDtypeStruct(w.shape, F32)] * 4
    out_shape.append(jax.ShapeDtypeStruct((1, 1), F32))
    return pl.pallas_call(
        body, name="vector_params_step", in_specs=[VMEM] * len(args) + [ANY] * len(deps),
        out_specs=[VMEM] * len(out_shape), out_shape=out_shape,
        scratch_shapes=[pltpu.VMEM((VEC_ROWS, d), F32), pltpu.VMEM((N_DEV, VEC_ROWS, d), F32),
                        pltpu.SemaphoreType.DMA((N_DEV - 1,)), pltpu.SemaphoreType.DMA((N_DEV - 1,))],
        compiler_params=pltpu.CompilerParams(vmem_limit_bytes=VMEM_LIMIT),
    )(*args, *deps)


def kernel(x, norm1_g, w_in, b_gate, conv_w, conv_b, w_a_out, w_pool, pool_scale, w_o, norm2_g, w_ffn_gate, w_ffn_up, w_ffn_down, final_g, loss_target, m_norm1_g, m_w_in, m_b_gate, m_conv_w, m_conv_b, m_w_a_out, m_w_pool, m_pool_scale, m_w_o, m_norm2_g, m_w_ffn_gate, m_w_ffn_up, m_w_ffn_down, m_final_g, v_norm1_g, v_w_in, v_b_gate, v_conv_w, v_conv_b, v_w_a_out, v_w_pool, v_pool_scale, v_w_o, v_norm2_g, v_w_ffn_gate, v_w_ffn_up, v_w_ffn_down, v_final_g):
    t, d = x.shape[1], x.shape[2]
    n_conv = conv_b.shape[1]
    n_groups, pool_cg, pool_dg = w_pool.shape[1], w_pool.shape[2], N_CHIPS * w_pool.shape[3]
    d_ff = N_CHIPS * w_ffn_gate.shape[2]
    assert n_conv // n_groups == pool_cg and n_conv % (n_groups * MIX_COLS) == 0 and n_groups == len(POOL_WINDOWS)

    big = {"w_in": (w_in, m_w_in, v_w_in), "w_a_out": (w_a_out, m_w_a_out, v_w_a_out), "w_pool": (w_pool, m_w_pool, v_w_pool),
           "w_o": (w_o, m_w_o, v_w_o), "w_ffn_gate": (w_ffn_gate, m_w_ffn_gate, v_w_ffn_gate),
           "w_ffn_up": (w_ffn_up, m_w_ffn_up, v_w_ffn_up), "w_ffn_down": (w_ffn_down, m_w_ffn_down, v_w_ffn_down)}
    colshard = {"w_in": True, "w_a_out": True, "w_pool": True, "w_o": False, "w_ffn_gate": True, "w_ffn_up": True,
                "w_ffn_down": False}
    names = list(big)
    shard2d = {n: big[n][0].reshape(-1, big[n][0].shape[-1]) for n in names}
    ws = [_Weight(n, *shard2d[n].shape, colshard[n]) for n in names]

    xs, tgt = x[0], loss_target[0]
    cw_loc = conv_w[0]
    pos = jnp.stack([lax.axis_index("c"), 2 * lax.axis_index("x") + lax.axis_index("y")]).astype(jnp.int32)
    by_name = {w.name: w for w in ws}
    groups = [[by_name[n] for n in g] for g in (["w_in"], ["w_a_out", "w_pool", "w_o"], ["w_ffn_gate"], ["w_ffn_up"],
                                                 ["w_ffn_down"])]
    first = [sum(len(g) for g in groups[:i]) for i in range(len(groups))]
    rgroups = [groups[0], groups[1], groups[2] + groups[3], groups[4]]

    cw_full = _gather_conv_w(cw_loc)
    cast = lambda w, dep: _cast_place(f"cast_{w.name}", w, pos, shard2d[w.name].reshape(2, w.R, w.nn), deps=[dep])
    chips, ks = _other_chips(lax.axis_index("x"), lax.axis_index("y"))
    kvec = jnp.stack([pos[1], *ks]).astype(jnp.int32)
    full = {}

    def start(name, arrays, copies, after=()):
        ssem, rsem, arrays, token = _split_start(name, arrays, copies.n, copies, after)
        return name, arrays, ssem, rsem, copies, token

    def wait(started, after):
        name, arrays, ssem, rsem, copies, _ = started
        return _split_wait(name + "_wait", arrays, ssem, rsem, copies, after)

    def pass_on(g, got, after=()):
        return start(f"pass_{g}", got, _pass_copies(groups[g]), after)

    def passed(g, st, after=None):
        got = wait(st, [st[5]] if after is None else after)
        full.update({w.name: a.reshape(w.P * 2 * w.R, w.N) for w, a in zip(groups[g], got)})

    near = start("near_0", [cast(w, cw_full) for w in groups[0]], _near_copies(groups[0]))
    rest = [cast(w, near[5]) for grp in groups[1:] for w in grp]
    h1 = _rms_fwd("norm1_fwd", xs, norm1_g, deps=[near[5]])
    proj = _proj_piece("proj_own", h1, shard2d["w_in"], None, kvec, 0, 1, deps=rest)
    got = wait(near, [proj])
    far = start("far_0", got, _far_copies(groups[0]))
    sems_b, lands_b, tok_b = _gather_start("gather_start_b", groups[1:2], rest[:3], after=[far[5]])
    st = start("pass_near_0", far[1], _pass_copies(groups[0], (0, 1)), [tok_b])
    got = wait(st, [st[5]])
    proj = _proj_piece("proj_near", h1, got[0].reshape(-1, groups[0][0].N), proj, kvec, 1, 2)
    st = start("pass_far_0", wait((far[0], got) + far[2:], [proj]), _pass_copies(groups[0], (2,)))
    got = wait(st, [st[5]])
    w_in_full = got[0].reshape(-1, groups[0][0].N)
    proj = _proj_piece("proj_far", h1, w_in_full, proj, kvec, 3, 1)
    got = _gather_wait("gather_wait_1", groups[1], lands_b, *sems_b[0], proj)
    near_g = start("near_2", rest[3:4], _near_copies(groups[2]), got)
    st = pass_on(1, got, [near_g[5]])
    z, p = _mixer_fwd("mixer_fwd", proj, cw_full, conv_b, n_conv, n_groups, deps=[st[5]])
    passed(1, st, [z])
    wp_full = full["w_pool"].reshape(n_groups, pool_cg, pool_dg)
    ya = _mm_nn("conv_out", z, full["w_a_out"], BF16)
    yb = _gmm_nn("pool_out", p, wp_full, BF16)
    merged = _merge_fwd("merge_fwd", proj, b_gate, ya, yb, pool_scale)
    far_g = start("far_2", wait(near_g, [merged]), _far_copies(groups[2]))
    near_u = start("near_3", rest[4:5], _near_copies(groups[3]), [far_g[5]])
    x2 = _mm_nn("mix_out", merged, full["w_o"], F32, add=xs, deps=[near_u[5]])
    st = pass_on(2, wait(far_g, [x2]))
    h2 = _rms_fwd("norm2_fwd", x2, norm2_g, deps=[st[5]])
    passed(2, st, [h2])
    gate = _mm_nn("ffn_gate_a", h2, full["w_ffn_gate"], BF16, part=(0, 2))
    far_u = start("far_3", wait(near_u, [gate]), _far_copies(groups[3]))
    near_d = start("near_4", rest[5:6], _near_copies(groups[4]), [far_u[5]])
    gate = _mm_nn("ffn_gate_b", h2, full["w_ffn_gate"], BF16, part=(1, 2), prev=gate, deps=[near_d[5]])
    passed(3, pass_on(3, wait(far_u, [gate])))
    up_act = _ffn_up_act("ffn_up_act_a", h2, full["w_ffn_up"], gate, part=(0, 2))
    far_d = start("far_4", wait(near_d, [up_act[0]]), _far_copies(groups[4]))
    up, act = _ffn_up_act("ffn_up_act_b", h2, full["w_ffn_up"], gate, part=(1, 2), prev=up_act, deps=[far_d[5]])
    passed(4, pass_on(4, wait(far_d, [act])))
    x3 = _mm_nn("ffn_down", act, full["w_ffn_down"], F32, add=x2, tk=d_ff // 4)

    pending = {}

    def pair_start(g, grads):
        grp = rgroups[g]
        gcan = [grads[w.name].reshape(w.P, 2, w.R, w.N) for w in grp]
        slots = [lax.empty((w.P, w.R, w.N), BF16) for w in grp]
        pending[g] = _split_start(f"pair_start_{g}", gcan + slots, len(grp), _pair_copies(len(grp)))
        return pending[g][3]

    def scatter_start(g, after):
        grp = rgroups[g]
        n = len(grp)
        ssem, rsem, arrs, _ = pending[g]
        arrs = _split_wait(f"pair_wait_{g}", arrs, ssem, rsem, _pair_copies(n), after)
        gcan, sib = arrs[:n], arrs[n:]
        pairs = [_pair_sum(f"pair_sum_{w.name}", w, pos, a, s) for w, a, s in zip(grp, gcan, sib)]
        ssem, rsem, pairs, slots, token = _scatter_start(f"scatter_start_{g}", grp, pairs)
        pending[g] = (gcan, sib, pairs, slots, ssem, rsem)
        return token

    def pair_start_halves(g, ab, deps):
        grp = rgroups[g]
        sent = [_mm_tn_half(f"d{w.name}_sib", a, b, pos, False, deps=deps if i == 0 else ()) for i, (w, (a, b)) in enumerate(zip(grp, ab))]
        slots = [lax.empty((1, w.R, w.N), BF16) for w in grp]
        pending[g] = _split_start(f"pair_start_{g}", sent + slots, len(grp), _pair_copies(len(grp), whole=True))
        return pending[g][3]

    def scatter_start_halves(g, ab, after):
        grp = rgroups[g]
        n = len(grp)
        ssem, rsem, arrs, _ = pending[g]
        arrs = _split_wait(f"pair_wait_{g}", arrs, ssem, rsem, _pair_copies(n, whole=True), after)
        pairs = [_mm_tn_half(f"d{w.name}_own", a, b, pos, True, add=s) for w, (a, b), s in zip(grp, ab, arrs[n:])]
        ssem, rsem, pairs, slots, token = _scatter_start(f"scatter_start_{g}", grp, pairs)
        pending[g] = (None, None, pairs, slots, ssem, rsem)
        return token

    def reduce_finish(g, after):
        grp = rgroups[g]
        gcan, sib, pairs, slots, ssem, rsem = pending[g]
        pairs, parts = _scatter_wait(f"scatter_wait_{g}", grp, pairs, slots, ssem, rsem, after)
        if gcan is None:
            return [_final_sum(f"final_sum_{w.name}", w, pos, a, None, q) for w, a, q in zip(grp, pairs, parts)]
        return [_final_sum(f"final_sum_{w.name}", w, pos, a, s, q) for w, a, s, q in zip(grp, gcan, sib, parts)]

    grads = {}
    dx3, dx3b, d_gf, loss_cols = _final_bwd("final_bwd", x3, final_g.reshape(1, d), tgt)
    dgate, dup = _ffn_bwd("ffn_bwd", dx3b, full["w_ffn_down"], gate, up)
    grads["w_ffn_down"] = _mm_tn("dw_ffn_down", act, dx3b, BF16)
    tok = pair_start(3, grads)
    dh2 = _mm_nt("d_h2", [(dgate, full["w_ffn_gate"]), (dup, full["w_ffn_up"])], BF16, tk=d_ff // 4, deps=[tok])
    tok = scatter_start(3, [dh2])
    tok = pair_start_halves(2, [(h2, dgate), (h2, dup)], [tok])
    dx2, dx2b, d_g2 = _rms_bwd("norm2_bwd", x2, norm2_g, dh2, dx3, True, deps=[tok])
    dmerged = _mm_nt("d_merged", [(dx2b, full["w_o"])], BF16, tk=d)
    grads["w_o"] = _mm_tn("dw_o", merged, dx2b, BF16)
    tok = scatter_start_halves(2, [(h2, dgate), (h2, dup)], [grads["w_o"]])
    dya, dyb, dproj, d_bga, d_bgb, d_ps = _merge_bwd("merge_bwd", dmerged, proj, b_gate, ya, yb, pool_scale, deps=[tok])
    dz = _mm_nt("d_z", [(dya, full["w_a_out"])], BF16, tk=d)
    grads["w_a_out"] = _mm_tn("dw_a_out", z, dya, BF16)
    dp = _gmm_nt("d_pool", dyb, wp_full, BF16)
    grads["w_pool"] = _gmm_tn("dw_pool", p, dyb, n_groups, BF16)
    tok = pair_start(1, grads)
    dproj, d_cw, d_cb = _mixer_bwd("mixer_bwd", dz, dp, proj, cw_full, conv_b, dproj, n_conv, n_groups, deps=[tok])
    tok = scatter_start(1, [dproj])
    tok = pair_start_halves(0, [(h1, dproj)], [tok])
    dh1 = _mm_nt("d_h1", [(dproj, w_in_full)], BF16, tk=proj.shape[1] // 4, deps=[tok])
    tok = scatter_start_halves(0, [(h1, dproj)], [dh1])
    grad_x, d_g1 = _rms_bwd("norm1_bwd", xs, norm1_g, dh1, dx2, False, deps=[tok])

    g_big, d_big, m_big, v_big = {}, {}, {}, {}

    def update(wsub, shared):
        out = []
        for w, g in zip(wsub, shared):
            wt, mt, vt = big[w.name]
            g2 = g.reshape(2 * w.R, w.nn)
            go, dl, nm, nv = _adamw(f"adamw_{w.name}", shard2d[w.name], g2, mt.reshape(g2.shape), vt.reshape(g2.shape))
            g_big[w.name], d_big[w.name], m_big[w.name], v_big[w.name] = (a.reshape(wt.shape) for a in (go, dl, nm, nv))
            out.append(nv)
        return out

    after = [grad_x]
    started = []
    for g in (3, 2, 1):
        halves = reduce_finish(g, after)
        share = _share_copies(len(halves))
        ssem, rsem, halves, tok = _split_start(f"share_start_{g}", halves, len(halves), share)
        started.append((g, ssem, rsem, halves, share))
        after = [tok]
    for g, ssem, rsem, halves, share in started:
        after = update(rgroups[g], _split_wait(f"share_wait_{g}", halves, ssem, rsem, share, after))
    share = _share_copies(1)
    ssem, rsem, halves, tok = _split_start("share_start_0", reduce_finish(0, after), 1, share)

    vec_names = ["norm1_g", "b_gate", "conv_w", "conv_b", "pool_scale", "norm2_g", "final_g"]
    vec = {"norm1_g": (norm1_g, m_norm1_g, v_norm1_g), "b_gate": (b_gate, m_b_gate, v_b_gate),
           "conv_w": (cw_loc, m_conv_w[0], v_conv_w[0]), "conv_b": (conv_b, m_conv_b, v_conv_b),
           "pool_scale": (pool_scale, m_pool_scale, v_pool_scale), "norm2_g": (norm2_g, m_norm2_g, v_norm2_g),
           "final_g": tuple(a.reshape(1, d) for a in (final_g, m_final_g, v_final_g))}
    vout = _vector_step(d, n_conv, [d_g1, d_bga, d_bgb, d_cw, d_cb, d_ps, d_g2, d_gf, loss_cols],
                        [vec[n] for n in vec_names], deps=halves)
    update(rgroups[0], _split_wait("share_wait_0", halves, ssem, rsem, share, []))

    shapes = {"conv_w": conv_w.shape, "final_g": final_g.shape}
    g_vec, d_vec, m_vec, v_vec = ({n: vout[4 * i + q].reshape(shapes.get(n, vec[n][0].shape)) for i, n in enumerate(vec_names)}
                                  for q in range(4))
    loss = vout[-1].reshape(())

    order = ["norm1_g", "w_in", "b_gate", "conv_w", "conv_b", "w_a_out", "w_pool", "pool_scale", "w_o", "norm2_g",
             "w_ffn_gate", "w_ffn_up", "w_ffn_down", "final_g"]
    pick = lambda vecs, bigs: [vecs[n] if n in vecs else bigs[n] for n in order]
    return (loss, grad_x.reshape(x.shape), *pick(g_vec, g_big), *pick(d_vec, d_big), *pick(m_vec, m_big),
            *pick(v_vec, v_big))
```

```python
import functools

import jax
import jax.numpy as jnp
from jax import lax
from jax.experimental import pallas as pl
from jax.experimental.pallas import tpu as pltpu

F32, BF16 = jnp.float32, jnp.bfloat16
MESH = pl.DeviceIdType.MESH
ANY = pl.BlockSpec(memory_space=pl.ANY)
VMEM = pl.BlockSpec(memory_space=pltpu.VMEM)
HBM = pl.BlockSpec(memory_space=pltpu.HBM)
SEM = pl.BlockSpec(memory_space=pltpu.SEMAPHORE)
EFFECT = pltpu.SideEffectType.DATAFLOW_SIDE_EFFECTING

EPS = 1e-6
POOL_WINDOWS = (2, 4, 8, 16)
ADAM_LR, ADAM_B1, ADAM_B2, ADAM_EPS, ADAM_WD, ADAM_STEP = 0.001, 0.9, 0.999, 1e-08, 0.01, 10

V7X_VMEM_BYTES = 64 * 1024 * 1024
VMEM_LIMIT = V7X_VMEM_BYTES * 3 // 4
LANES = 128
COL_TILE = 8 * LANES
N_CHIPS = 4
N_DEV = 8

_DIMS = {
    "nn": (((1,), (0,)), ((), ())),
    "nt": (((1,), (1,)), ((), ())),
    "tn": (((0,), (0,)), ((), ())),
}


def _cp(sem):
    return pltpu.CompilerParams(dimension_semantics=sem, vmem_limit_bytes=VMEM_LIMIT)


def _mesh_pos():
    return lax.axis_index("x"), lax.axis_index("y"), lax.axis_index("c")


def _mm(name, pairs, *, mode, grid, out_shape, o_spec, nk=1, kaxis=None, add=None, deps=(), prev=None):
    npair = len(pairs)
    has_add = add is not None

    def body(*refs):
        ab = refs[: 2 * npair]
        pos = 2 * npair
        add_ref = refs[pos] if has_add else None
        pos += int(has_add) + len(deps) + (prev is not None)
        o_ref = refs[pos]
        acc_ref = refs[pos + 1] if nk > 1 else None
        d = None
        for p in range(npair):
            t = lax.dot_general(ab[2 * p][...], ab[2 * p + 1][...], _DIMS[mode], preferred_element_type=F32)
            d = t if d is None else d + t
        if nk == 1:
            if has_add:
                d = d + add_ref[...].astype(F32)
            o_ref[...] = d.astype(o_ref.dtype)
        else:
            k = pl.program_id(kaxis)

            @pl.when(k == 0)
            def _():
                acc_ref[...] = d

            @pl.when(k > 0)
            def _():
                acc_ref[...] += d

            @pl.when(k == nk - 1)
            def _():
                r = acc_ref[...]
                if has_add:
                    r = r + add_ref[...].astype(F32)
                o_ref[...] = r.astype(o_ref.dtype)

    args, specs = [], []
    for a, a_spec, b, b_spec in pairs:
        args += [a, b]
        specs += [a_spec, b_spec]
    if has_add:
        args.append(add[0])
        specs.append(add[1])
    args += list(deps)
    specs += [ANY] * len(deps)
    aliases = {}
    if prev is not None:
        aliases = {len(args): 0}
        args.append(prev)
        specs.append(ANY)
    scratch = []
    if nk > 1:
        blk = [d for d in o_spec.block_shape if d is not None]
        scratch = [pltpu.VMEM(tuple(blk), F32)]
    sem = tuple("arbitrary" if (nk > 1 and ax == kaxis) else "parallel" for ax in range(len(grid)))
    return pl.pallas_call(
        body, name=name, grid=grid, in_specs=specs, out_specs=o_spec, out_shape=out_shape,
        scratch_shapes=scratch, input_output_aliases=aliases, compiler_params=_cp(sem),
    )(*args)


def _tile_span(n_tiles, part):
    if part is None:
        return 0, n_tiles
    p, of = part
    return p * n_tiles // of, (p + 1) * n_tiles // of


def _tile(n, pref):
    if n <= pref:
        return n
    for t in range(pref, 0, -LANES):
        if t % LANES == 0 and n % t == 0:
            return t
    raise ValueError(f"no tile for {n}")


def _mm_nn(name, a, b, out_dtype, add=None, tk=None, deps=(), part=None, prev=None):
    m, kk = a.shape
    n = b.shape[1]
    tm, tn = _tile(m, 1024), _tile(n, COL_TILE)
    out_shape = jax.ShapeDtypeStruct((m, n), out_dtype)
    if tk is None or tk == kk:
        j0, j1 = _tile_span(n // tn, part)
        grid = (m // tm, j1 - j0)
        pairs = [(a, pl.BlockSpec((tm, kk), lambda i, j: (i, 0)), b, pl.BlockSpec((kk, tn), lambda i, j: (0, j0 + j)))]
        o_spec = pl.BlockSpec((tm, tn), lambda i, j: (i, j0 + j))
        add_ = None if add is None else (add, pl.BlockSpec((tm, tn), lambda i, j: (i, j0 + j)))
        return _mm(name, pairs, mode="nn", grid=grid, out_shape=out_shape, o_spec=o_spec, add=add_, deps=deps, prev=prev)
    tn = _tile(n, 1024)
    nk = kk // tk
    grid = (m // tm, n // tn, nk)
    pairs = [(a, pl.BlockSpec((tm, tk), lambda i, j, k: (i, k)), b, pl.BlockSpec((tk, tn), lambda i, j, k: (k, j)))]
    o_spec = pl.BlockSpec((tm, tn), lambda i, j, k: (i, j))
    add_ = None if add is None else (add, pl.BlockSpec((tm, tn), lambda i, j, k: (i, j)))
    return _mm(name, pairs, mode="nn", grid=grid, out_shape=out_shape, o_spec=o_spec, nk=nk, kaxis=2, add=add_, deps=deps)


def _mm_nt(name, abs_, out_dtype, tk, deps=()):
    m, kk = abs_[0][0].shape
    n = abs_[0][1].shape[0]
    tm = _tile(m, 1024)
    nk = kk // tk
    tn = _tile(n, COL_TILE if nk == 1 else 1024)
    out_shape = jax.ShapeDtypeStruct((m, n), out_dtype)
    if nk == 1:
        grid = (m // tm, n // tn)
        pairs = [(a, pl.BlockSpec((tm, kk), lambda i, j: (i, 0)), b, pl.BlockSpec((tn, kk), lambda i, j: (j, 0)))
                 for a, b in abs_]
        o_spec = pl.BlockSpec((tm, tn), lambda i, j: (i, j))
        return _mm(name, pairs, mode="nt", grid=grid, out_shape=out_shape, o_spec=o_spec, deps=deps)
    grid = (m // tm, n // tn, nk)
    pairs = [(a, pl.BlockSpec((tm, tk), lambda i, j, k: (i, k)), b, pl.BlockSpec((tn, tk), lambda i, j, k: (j, k)))
             for a, b in abs_]
    o_spec = pl.BlockSpec((tm, tn), lambda i, j, k: (i, j))
    return _mm(name, pairs, mode="nt", grid=grid, out_shape=out_shape, o_spec=o_spec, nk=nk, kaxis=2, deps=deps)


def _mm_tn(name, a, b, out_dtype, deps=()):
    t, m = a.shape
    n = b.shape[1]
    tm, tn = _tile(m, 512), _tile(n, 2048)
    if n > m:
        grid = (n // tn, m // tm)
        a_map, b_map, o_map = (lambda j, i: (0, i)), (lambda j, i: (0, j)), (lambda j, i: (i, j))
    else:
        grid = (m // tm, n // tn)
        a_map, b_map, o_map = (lambda i, j: (0, i)), (lambda i, j: (0, j)), (lambda i, j: (i, j))
    pairs = [(a, pl.BlockSpec((t, tm), a_map), b, pl.BlockSpec((t, tn), b_map))]
    o_spec = pl.BlockSpec((tm, tn), o_map)
    return _mm(name, pairs, mode="tn", grid=grid, out_shape=jax.ShapeDtypeStruct((m, n), out_dtype), o_spec=o_spec,
               deps=deps)


def _mm_tn_half(name, a, b, pos, mine, add=None, deps=()):
    t, m = a.shape
    r, n = m // 2, b.shape[1]
    tm, tn = _tile(r, 512), _tile(n, 2048)
    nbi = r // tm
    half = (lambda pos: pos[0]) if mine else (lambda pos: 1 - pos[0])
    if n > r:
        grid, ij = (n // tn, nbi), (lambda g0, g1: (g1, g0))
    else:
        grid, ij = (nbi, n // tn), (lambda g0, g1: (g0, g1))
    has_add = add is not None

    def body(pos_ref, a_ref, b_ref, *rest):
        d = lax.dot_general(a_ref[...], b_ref[...], _DIMS["tn"], preferred_element_type=F32)
        if has_add:
            d = d + rest[0][...].astype(F32)
        rest[-1][...] = d.astype(BF16)

    o_spec = pl.BlockSpec((None, tm, tn), lambda g0, g1, pos: (0, *ij(g0, g1)))
    grid_spec = pltpu.PrefetchScalarGridSpec(
        num_scalar_prefetch=1, grid=grid,
        in_specs=[pl.BlockSpec((t, tm), lambda g0, g1, pos: (0, half(pos) * nbi + ij(g0, g1)[0])),
                  pl.BlockSpec((t, tn), lambda g0, g1, pos: (0, ij(g0, g1)[1]))]
        + ([o_spec] if has_add else []) + [ANY] * len(deps),
        out_specs=o_spec)
    return pl.pallas_call(body, name=name, grid_spec=grid_spec, out_shape=jax.ShapeDtypeStruct((1, r, n), BF16),
                          compiler_params=_cp(("parallel",) * 2))(pos, a, b, *([add] if has_add else []), *deps)


def _proj_piece(name, h, w, prev, kvec, base, count, deps=()):
    t, kk = h.shape
    own = w.dtype == F32
    nn = w.shape[1] if own else w.shape[1] // N_CHIPS
    tm, tn = _tile(t, 1024), _tile(nn, COL_TILE)
    nb = nn // tn

    def body(kv_ref, h_ref, w_ref, *rest):
        rest[-1][...] = lax.dot_general(h_ref[...], w_ref[...].astype(BF16), _DIMS["nn"],
                                        preferred_element_type=F32).astype(BF16)

    cols = lambda s, i, j, kv: (0, j) if own else (0, kv[base + s] * nb + j)
    extra = ([] if prev is None else [prev]) + list(deps)
    grid_spec = pltpu.PrefetchScalarGridSpec(
        num_scalar_prefetch=1, grid=(count, t // tm, nb),
        in_specs=[pl.BlockSpec((tm, kk), lambda s, i, j, kv: (i, 0)), pl.BlockSpec((kk, tn), cols)] + [ANY] * len(extra),
        out_specs=pl.BlockSpec((tm, tn), lambda s, i, j, kv: (i, kv[base + s] * nb + j)))
    return pl.pallas_call(body, name=name, grid_spec=grid_spec, out_shape=jax.ShapeDtypeStruct((t, N_CHIPS * nn), BF16),
                          input_output_aliases={} if prev is None else {3: 0},
                          compiler_params=_cp(("parallel",) * 3))(kvec, h, w, *extra)


def _gmm_nn(name, p, w, out_dtype):
    t = p.shape[0]
    g, cg, dg = w.shape
    tm = _tile(t, 1024)
    pairs = [(p, pl.BlockSpec((tm, cg), lambda i, j: (i, j)), w, pl.BlockSpec((None, cg, dg), lambda i, j: (j, 0, 0)))]
    o_spec = pl.BlockSpec((tm, dg), lambda i, j: (i, j))
    return _mm(name, pairs, mode="nn", grid=(t // tm, g), out_shape=jax.ShapeDtypeStruct((t, g * dg), out_dtype),
               o_spec=o_spec)


def _gmm_nt(name, dy, w, out_dtype):
    t = dy.shape[0]
    g, cg, dg = w.shape
    tm = _tile(t, 1024)
    pairs = [(dy, pl.BlockSpec((tm, dg), lambda i, j: (i, j)), w, pl.BlockSpec((None, cg, dg), lambda i, j: (j, 0, 0)))]
    o_spec = pl.BlockSpec((tm, cg), lambda i, j: (i, j))
    return _mm(name, pairs, mode="nt", grid=(t // tm, g), out_shape=jax.ShapeDtypeStruct((t, g * cg), out_dtype),
               o_spec=o_spec)


def _gmm_tn(name, p, dy, g, out_dtype):
    t = p.shape[0]
    cg, dg = p.shape[1] // g, dy.shape[1] // g
    pairs = [(p, pl.BlockSpec((t, cg), lambda j: (0, j)), dy, pl.BlockSpec((t, dg), lambda j: (0, j)))]
    o_spec = pl.BlockSpec((None, cg, dg), lambda j: (j, 0, 0))
    return _mm(name, pairs, mode="tn", grid=(g,), out_shape=jax.ShapeDtypeStruct((g, cg, dg), out_dtype), o_spec=o_spec)


ROW_TILE = 256


def _rows(t):
    return _tile8(t, ROW_TILE)


def _tile8(n, pref):
    if n <= pref:
        return n
    for t in range(pref, 0, -8):
        if n % t == 0:
            return t
    raise ValueError(f"no row tile for {n}")


def _cast_place(name, w, pos, shard, deps=()):
    tr = _tile8(w.R, 512)
    if w.colshard:
        o_map = lambda h, i, pos: (0, h, i, pos[1])
    else:
        o_map = lambda h, i, pos: (pos[1], h, i, 0)

    def body(pos_ref, w_ref, *rest):
        rest[-1][...] = w_ref[...].astype(BF16)

    grid_spec = pltpu.PrefetchScalarGridSpec(
        num_scalar_prefetch=1, grid=(2, w.R // tr),
        in_specs=[pl.BlockSpec((None, tr, w.nn), lambda h, i, pos: (h, i, 0))] + [ANY] * len(deps),
        out_specs=pl.BlockSpec((None, None, tr, w.nn), o_map))
    return pl.pallas_call(body, name=name, grid_spec=grid_spec, out_shape=jax.ShapeDtypeStruct((w.P, 2, w.R, w.N), BF16),
                          compiler_params=_cp(("parallel", "parallel")))(pos, shard, *deps)


def _rms_fwd(name, x, g, deps=()):
    t, d = x.shape
    tm = _rows(t)

    def body(x_ref, g_ref, *rest):
        xf = x_ref[...]
        r = lax.rsqrt(jnp.mean(xf * xf, axis=-1, keepdims=True) + EPS)
        rest[-1][...] = (xf * r * g_ref[...]).astype(BF16)

    return pl.pallas_call(
        body, name=name, grid=(t // tm,),
        in_specs=[pl.BlockSpec((tm, d), lambda i: (i, 0)), pl.BlockSpec((1, d), lambda i: (0, 0))] + [ANY] * len(deps),
        out_specs=pl.BlockSpec((tm, d), lambda i: (i, 0)), out_shape=jax.ShapeDtypeStruct((t, d), BF16),
        compiler_params=_cp(("parallel",)),
    )(x, g, *deps)


def _rms_bwd(name, x, g, dh, dres, want_bf16, deps=()):
    t, d = x.shape
    tm = _rows(t)

    def body(x_ref, g_ref, dh_ref, dres_ref, *rest):
        rest = rest[len(deps):]
        dx_ref, rest = rest[0], rest[1:]
        dg_ref = rest[-1]
        xf = x_ref[...]
        r = lax.rsqrt(jnp.mean(xf * xf, axis=-1, keepdims=True) + EPS)
        xh = xf * r
        dhf = dh_ref[...].astype(F32)
        dxh = dhf * g_ref[...]
        m = jnp.mean(dxh * xh, axis=-1, keepdims=True)
        dx = dres_ref[...] + r * (dxh - xh * m)
        dx_ref[...] = dx
        if want_bf16:
            rest[0][...] = dx.astype(BF16)

        @pl.when(pl.program_id(0) == 0)
        def _():
            dg_ref[...] = jnp.zeros_like(dg_ref)

        dg_ref[...] += jnp.sum(dhf * xh, axis=0, keepdims=True)

    row = pl.BlockSpec((tm, d), lambda i: (i, 0))
    vec = pl.BlockSpec((1, d), lambda i: (0, 0))
    out_specs = [row] + ([row] if want_bf16 else []) + [vec]
    out_shape = ([jax.ShapeDtypeStruct((t, d), F32)] + ([jax.ShapeDtypeStruct((t, d), BF16)] if want_bf16 else [])
                 + [jax.ShapeDtypeStruct((1, d), F32)])
    return pl.pallas_call(body, name=name, grid=(t // tm,), in_specs=[row, vec, row, row] + [ANY] * len(deps),
                          out_specs=out_specs, out_shape=out_shape, compiler_params=_cp(("arbitrary",)))(x, g, dh, dres, *deps)


def _final_bwd(name, x3, gf, tgt):
    t, d = x3.shape
    tm = _rows(t)

    def body(x_ref, g_ref, t_ref, dx_ref, dxb_ref, dg_ref, lc_ref):
        xf = x_ref[...]
        g = g_ref[...]
        r = lax.rsqrt(jnp.mean(xf * xf, axis=-1, keepdims=True) + EPS)
        xh = xf * r
        diff = xh * g - t_ref[...]
        dy = diff * (1.0 / d)
        dxh = dy * g
        m = jnp.mean(dxh * xh, axis=-1, keepdims=True)
        dx = r * (dxh - xh * m)
        dx_ref[...] = dx
        dxb_ref[...] = dx.astype(BF16)

        @pl.when(pl.program_id(0) == 0)
        def _():
            dg_ref[...] = jnp.zeros_like(dg_ref)
            lc_ref[...] = jnp.zeros_like(lc_ref)

        dg_ref[...] += jnp.sum(dy * xh, axis=0, keepdims=True)
        lc_ref[...] += jnp.sum(diff * diff, axis=0, keepdims=True) * (0.5 / d)

    row = pl.BlockSpec((tm, d), lambda i: (i, 0))
    vec = pl.BlockSpec((1, d), lambda i: (0, 0))
    return pl.pallas_call(
        body, name=name, grid=(t // tm,), in_specs=[row, vec, row], out_specs=[row, row, vec, vec],
        out_shape=[jax.ShapeDtypeStruct((t, d), F32), jax.ShapeDtypeStruct((t, d), BF16),
                   jax.ShapeDtypeStruct((1, d), F32), jax.ShapeDtypeStruct((1, d), F32)],
        compiler_params=_cp(("arbitrary",)),
    )(x3, gf, tgt)


def _shift_down(v, k, t_idx):
    return jnp.where(t_idx >= k, pltpu.roll(v, k, 0), 0.0)


def _shift_up(v, k, t_idx):
    n = v.shape[0]
    return jnp.where(t_idx < n - k, pltpu.roll(v, n - k, 0), 0.0)


def _window_sums(v, shift, t_idx, grp):
    s = v + shift(v, 1, t_idx)
    out = s
    for lvl in range(1, len(POOL_WINDOWS)):
        s = s + shift(s, 1 << lvl, t_idx)
        out = jnp.where(grp >= lvl, s, out)
    return out


def _window_weight(t_idx, grp):
    return 1.0 / jnp.minimum(t_idx[:, :1] + 1, jnp.left_shift(2, grp)).astype(F32)


MIX_COLS = 256


def _mixer_fwd(name, proj, cw, cb, n_conv, n_groups, deps=()):
    t = proj.shape[0]
    nb = n_conv // MIX_COLS
    per_group = n_conv // n_groups // MIX_COLS

    def body(ba_ref, ca_ref, va_ref, vb_ref, cw_ref, cb_ref, *rest):
        z_ref, p_ref = rest[len(deps):]
        t_idx = lax.broadcasted_iota(jnp.int32, (t, MIX_COLS), 0)
        q = ca_ref[...].astype(F32) * va_ref[...].astype(F32)
        w = cw_ref[...]
        u = cb_ref[...] + w[0:1] * _shift_down(q, 2, t_idx) + w[1:2] * _shift_down(q, 1, t_idx) + w[2:3] * q
        z_ref[...] = (ba_ref[...].astype(F32) * u).astype(BF16)
        grp = pl.program_id(0) // per_group
        v = vb_ref[...].astype(F32)
        p_ref[...] = (_window_sums(v, _shift_down, t_idx, grp) * _window_weight(t_idx, grp) - v).astype(BF16)

    col = lambda s: pl.BlockSpec((t, MIX_COLS), lambda j: (0, s * nb + j))
    return pl.pallas_call(
        body, name=name, grid=(nb,),
        in_specs=[col(0), col(1), col(2), col(3), pl.BlockSpec((3, MIX_COLS), lambda j: (0, j)),
                  pl.BlockSpec((1, MIX_COLS), lambda j: (0, j))] + [ANY] * len(deps),
        out_specs=[col(0), col(0)],
        out_shape=[jax.ShapeDtypeStruct((t, n_conv), BF16), jax.ShapeDtypeStruct((t, n_conv), BF16)],
        compiler_params=_cp(("parallel",)),
    )(proj, proj, proj, proj, cw, cb, *deps)


def _mixer_bwd(name, dz, dp, proj, cw, cb, dproj, n_conv, n_groups, deps=()):
    t = proj.shape[0]
    nb = n_conv // MIX_COLS
    per_group = n_conv // n_groups // MIX_COLS

    def body(dz_ref, dp_ref, ba_ref, ca_ref, va_ref, cw_ref, cb_ref, _, *rest):
        o_ref, dcw_ref, dcb_ref, scr = rest[len(deps):]
        s = pl.program_id(1)

        @pl.when(s == 0)
        def _():
            t_idx = lax.broadcasted_iota(jnp.int32, (t, MIX_COLS), 0)
            ca, va = ca_ref[...].astype(F32), va_ref[...].astype(F32)
            q = ca * va
            q1, q2 = _shift_down(q, 1, t_idx), _shift_down(q, 2, t_idx)
            w = cw_ref[...]
            u = cb_ref[...] + w[0:1] * q2 + w[1:2] * q1 + w[2:3] * q
            dzf = dz_ref[...].astype(F32)
            du = dzf * ba_ref[...].astype(F32)
            scr[0] = (dzf * u).astype(BF16)
            dq = w[2:3] * du + w[1:2] * _shift_up(du, 1, t_idx) + w[0:1] * _shift_up(du, 2, t_idx)
            scr[1] = (dq * va).astype(BF16)
            scr[2] = (dq * ca).astype(BF16)
            dcb_ref[...] = jnp.sum(du, axis=0, keepdims=True)
            dcw_ref[0:1, :] = jnp.sum(du * q2, axis=0, keepdims=True)
            dcw_ref[1:2, :] = jnp.sum(du * q1, axis=0, keepdims=True)
            dcw_ref[2:3, :] = jnp.sum(du * q, axis=0, keepdims=True)
            grp = pl.program_id(0) // per_group
            dpf = dp_ref[...].astype(F32)
            e = dpf * _window_weight(t_idx, grp)
            scr[3] = (_window_sums(e, _shift_up, t_idx, grp) - dpf).astype(BF16)

        o_ref[...] = scr[s]

    col = lambda c: pl.BlockSpec((t, MIX_COLS), lambda j, s: (0, c * nb + j))
    own = pl.BlockSpec((t, MIX_COLS), lambda j, s: (0, j))
    return pl.pallas_call(
        body, name=name, grid=(nb, 4),
        in_specs=[own, own, col(0), col(1), col(2), pl.BlockSpec((3, MIX_COLS), lambda j, s: (0, j)),
                  pl.BlockSpec((1, MIX_COLS), lambda j, s: (0, j)), ANY] + [ANY] * len(deps),
        out_specs=[pl.BlockSpec((t, MIX_COLS), lambda j, s: (0, s * nb + j)),
                   pl.BlockSpec((3, MIX_COLS), lambda j, s: (0, j)), pl.BlockSpec((1, MIX_COLS), lambda j, s: (0, j))],
        out_shape=[jax.ShapeDtypeStruct(dproj.shape, BF16), jax.ShapeDtypeStruct((3, n_conv), F32),
                   jax.ShapeDtypeStruct((1, n_conv), F32)],
        scratch_shapes=[pltpu.VMEM((4, t, MIX_COLS), BF16)],
        input_output_aliases={7: 0},
        compiler_params=_cp(("arbitrary", "arbitrary")),
    )(dz, dp, proj, proj, proj, cw, cb, dproj, *deps)


def _merge_fwd(name, proj, bg, ya, yb, ps):
    t, d = ya.shape
    tm = _rows(t)

    def body(gab_ref, bg_ref, ya_ref, yb_ref, ps_ref, o_ref):
        gab = gab_ref[...].astype(F32) + bg_ref[...]
        sa, sb = jax.nn.sigmoid(gab[:, :d]), jax.nn.sigmoid(gab[:, d:])
        o_ref[...] = (sa * ya_ref[...].astype(F32) + sb * (yb_ref[...].astype(F32) * ps_ref[...])).astype(BF16)

    row = pl.BlockSpec((tm, d), lambda i: (i, 0))
    return pl.pallas_call(
        body, name=name, grid=(t // tm,),
        in_specs=[pl.BlockSpec((tm, 2 * d), lambda i: (i, 1)), pl.BlockSpec((1, 2 * d), lambda i: (0, 0)), row, row,
                  pl.BlockSpec((1, d), lambda i: (0, 0))],
        out_specs=row, out_shape=jax.ShapeDtypeStruct((t, d), BF16), compiler_params=_cp(("parallel",)),
    )(proj, bg, ya, yb, ps)


def _merge_bwd(name, dm, proj, bg, ya, yb, ps, deps=()):
    t, d = ya.shape
    tm = _rows(t)

    def body(dm_ref, gab_ref, bg_ref, ya_ref, yb_ref, ps_ref, *rest):
        dya_ref, dyb_ref, dg_ref, dba_ref, dbb_ref, dps_ref = rest[len(deps):]
        gab = gab_ref[...].astype(F32) + bg_ref[...]
        sa, sb = jax.nn.sigmoid(gab[:, :d]), jax.nn.sigmoid(gab[:, d:])
        dmf = dm_ref[...].astype(F32)
        ybf, ps_ = yb_ref[...].astype(F32), ps_ref[...]
        dya_ref[...] = (dmf * sa).astype(BF16)
        dyb = dmf * sb
        dyb_ref[...] = (dyb * ps_).astype(BF16)
        dga = dmf * ya_ref[...].astype(F32) * sa * (1.0 - sa)
        dgb = dmf * (ybf * ps_) * sb * (1.0 - sb)
        dg_ref[:, :d] = dga.astype(BF16)
        dg_ref[:, d:] = dgb.astype(BF16)

        @pl.when(pl.program_id(0) == 0)
        def _():
            dba_ref[...] = jnp.zeros_like(dba_ref)
            dbb_ref[...] = jnp.zeros_like(dbb_ref)
            dps_ref[...] = jnp.zeros_like(dps_ref)

        dba_ref[...] += jnp.sum(dga, axis=0, keepdims=True)
        dbb_ref[...] += jnp.sum(dgb, axis=0, keepdims=True)
        dps_ref[...] += jnp.sum(dyb * ybf, axis=0, keepdims=True)

    row = pl.BlockSpec((tm, d), lambda i: (i, 0))
    vec = pl.BlockSpec((1, d), lambda i: (0, 0))
    gates = pl.BlockSpec((tm, 2 * d), lambda i: (i, 1))
    return pl.pallas_call(
        body, name=name, grid=(t // tm,),
        in_specs=[row, gates, pl.BlockSpec((1, 2 * d), lambda i: (0, 0)), row, row, vec] + [ANY] * len(deps),
        out_specs=[row, row, gates, vec, vec, vec],
        out_shape=[jax.ShapeDtypeStruct((t, d), BF16), jax.ShapeDtypeStruct((t, d), BF16),
                   jax.ShapeDtypeStruct(proj.shape, BF16), jax.ShapeDtypeStruct((1, d), F32),
                   jax.ShapeDtypeStruct((1, d), F32), jax.ShapeDtypeStruct((1, d), F32)],
        compiler_params=_cp(("arbitrary",)),
    )(dm, proj, bg, ya, yb, ps, *deps)


def _ffn_up_act(name, h, w_up, gate, part=None, prev=None, deps=()):
    t, d = h.shape
    f = w_up.shape[1]
    tm, tf = _tile(t, 1024), _tile(f, 512)
    j0, j1 = _tile_span(f // tf, part)
    n_prev = 0 if prev is None else 2
    extra = ([] if prev is None else list(prev)) + list(deps)

    def body(h_ref, w_ref, g_ref, *rest):
        u_ref, a_ref = rest[len(extra):]
        u = lax.dot_general(h_ref[...], w_ref[...], _DIMS["nn"], preferred_element_type=F32)
        g = g_ref[...].astype(F32)
        u_ref[...] = u.astype(BF16)
        a_ref[...] = (g * jax.nn.sigmoid(g) * u).astype(BF16)

    blk = pl.BlockSpec((tm, tf), lambda i, j: (i, j0 + j))
    shp = jax.ShapeDtypeStruct((t, f), BF16)
    return pl.pallas_call(
        body, name=name, grid=(t // tm, j1 - j0),
        in_specs=[pl.BlockSpec((tm, d), lambda i, j: (i, 0)), pl.BlockSpec((d, tf), lambda i, j: (0, j0 + j)), blk]
        + [ANY] * len(extra),
        out_specs=[blk, blk], out_shape=[shp, shp], input_output_aliases={3 + i: i for i in range(n_prev)},
        compiler_params=_cp(("parallel", "parallel")))(h, w_up, gate, *extra)


def _ffn_bwd(name, dy, w_down, gate, up):
    t, d = dy.shape
    f = w_down.shape[0]
    tm, tf = _tile(t, 1024), _tile(f, 512)

    def body(dy_ref, w_ref, g_ref, u_ref, dg_ref, du_ref):
        da = lax.dot_general(dy_ref[...], w_ref[...], _DIMS["nt"], preferred_element_type=F32)
        g = g_ref[...].astype(F32)
        s = jax.nn.sigmoid(g)
        du_ref[...] = (da * (g * s)).astype(BF16)
        dg_ref[...] = (da * u_ref[...].astype(F32) * (s * (1.0 + g * (1.0 - s)))).astype(BF16)

    blk = pl.BlockSpec((tm, tf), lambda i, j: (i, j))
    shp = jax.ShapeDtypeStruct((t, f), BF16)
    return pl.pallas_call(
        body, name=name, grid=(t // tm, f // tf),
        in_specs=[pl.BlockSpec((tm, d), lambda i, j: (i, 0)), pl.BlockSpec((tf, d), lambda i, j: (j, 0)), blk, blk],
        out_specs=[blk, blk], out_shape=[shp, shp], compiler_params=_cp(("parallel", "parallel")))(dy, w_down, gate, up)


def _adamw_math(w, g, m, v):
    m = ADAM_B1 * m + (1.0 - ADAM_B1) * g
    v = ADAM_B2 * v + (1.0 - ADAM_B2) * (g * g)
    m_hat = m / (1.0 - ADAM_B1 ** ADAM_STEP)
    v_hat = v / (1.0 - ADAM_B2 ** ADAM_STEP)
    delta = -ADAM_LR * (m_hat / (jnp.sqrt(v_hat) + ADAM_EPS) + ADAM_WD * w)
    return delta, m, v


def _adamw(name, w, g, m, v):
    r, c = w.shape
    tr = _tile8(r, 512 if c <= 1024 else 256)

    def body(w_ref, g_ref, m_ref, v_ref, go_ref, d_ref, nm_ref, nv_ref):
        g = g_ref[...]
        go_ref[...] = g
        d_ref[...], nm_ref[...], nv_ref[...] = _adamw_math(w_ref[...], g, m_ref[...], v_ref[...])

    blk = pl.BlockSpec((tr, c), lambda i: (i, 0))
    shp = jax.ShapeDtypeStruct((r, c), F32)
    return pl.pallas_call(body, name=name, grid=(r // tr,), in_specs=[blk] * 4, out_specs=[blk] * 4,
                          out_shape=[shp] * 4, compiler_params=_cp(("parallel",)))(w, g, m, v)


class _Weight:
    def __init__(self, name, rows, cols, colshard):
        self.name, self.colshard = name, colshard
        self.R, self.nn = rows // 2, cols
        self.P = 1 if colshard else N_CHIPS
        self.N = N_CHIPS * cols if colshard else cols

    def cols(self, k):
        return pl.ds(pl.multiple_of(k * self.nn, LANES), self.nn)

    def shard(self, ref, k):
        return ref.at[0, :, :, self.cols(k)] if self.colshard else ref.at[k]

    def half(self, ref, k, h):
        return ref.at[0, h, :, self.cols(k)] if self.colshard else ref.at[k, h]

    def quarter(self, ref, k, h, q):
        return self.half(ref, k, h).at[pl.ds(q * (self.R // 2), self.R // 2), :]

    def part(self, ref, k):
        return ref.at[0, :, self.cols(k)] if self.colshard else ref.at[k]


def _remote(src, dst, ssem, rsem, dev):
    return pltpu.make_async_remote_copy(src_ref=src, dst_ref=dst, send_sem=ssem, recv_sem=rsem, device_id=dev,
                                        device_id_type=MESH)


def _other_chips(x, y):
    chips = [(1 - x, y), (x, 1 - y), (1 - x, 1 - y)]
    return chips, [2 * cx + cy for cx, cy in chips]


def _hbm(a):
    return pltpu.with_memory_space_constraint(a, pltpu.HBM)


def _gather_start(name, groups, lands, after=()):
    flat = [w for grp in groups for w in grp]
    nw, ng = len(flat), len(groups)

    def body(*refs):
        land = refs[:nw]
        sems = refs[nw + len(after):nw + len(after) + 2 * ng]
        token = refs[2 * nw + len(after) + 2 * ng]
        x, y, c = _mesh_pos()
        k_me = 2 * x + y
        chips, _ = _other_chips(x, y)
        i = 0
        for g, grp in enumerate(groups):
            for wi, w in enumerate(grp):
                mine = w.half(land[i], k_me, c)
                for j, chip in enumerate(chips):
                    _remote(mine, mine, sems[2 * g].at[3 * wi + j], sems[2 * g + 1].at[3 * wi + j], (*chip, c)).start()
                i += 1
        token[...] = jnp.zeros_like(token)

    sem_shapes = []
    for grp in groups:
        sem_shapes += [pltpu.SemaphoreType.DMA((3 * len(grp),))] * 2
    out = pl.pallas_call(
        body, name=name, in_specs=[HBM] * nw + [ANY] * len(after),
        out_specs=[SEM] * (2 * ng) + [HBM] * nw + [VMEM],
        out_shape=sem_shapes + [pltpu.HBM(a.shape, a.dtype) for a in lands] + [jax.ShapeDtypeStruct((8, LANES), F32)],
        input_output_aliases={i: 2 * ng + i for i in range(nw)},
        compiler_params=pltpu.CompilerParams(has_side_effects=EFFECT),
    )(*[_hbm(a) for a in lands], *after)
    sems = [(out[2 * g], out[2 * g + 1]) for g in range(ng)]
    return sems, list(out[2 * ng:2 * ng + nw]), out[-1]


def _gather_wait(name, grp, lands, ssem, rsem, after):
    n = len(grp)

    def body(*refs):
        land, ssem_ref, rsem_ref = refs[:n], refs[n], refs[n + 1]
        x, y, c = _mesh_pos()
        k_me = 2 * x + y
        chips, ks = _other_chips(x, y)
        for wi, w in enumerate(grp):
            for j, chip in enumerate(chips):
                cp = _remote(w.half(land[wi], k_me, c), w.half(land[wi], ks[j], c), ssem_ref.at[3 * wi + j],
                             rsem_ref.at[3 * wi + j], (*chip, c))
                cp.wait_send()
                cp.wait_recv()

    return pl.pallas_call(
        body, name=name, in_specs=[HBM] * n + [SEM, SEM, ANY], out_specs=[HBM] * n,
        out_shape=[pltpu.HBM(a.shape, a.dtype) for a in lands], input_output_aliases={i: i for i in range(n)},
        compiler_params=pltpu.CompilerParams(has_side_effects=EFFECT),
    )(*lands, ssem, rsem, after)


def _split_start(name, arrays, n, copies, after=()):
    na = len(arrays)

    def body(*refs):
        ssem, rsem, token = refs[na + len(after):][0], refs[na + len(after):][1], refs[2 * na + len(after) + 2]
        for i, (src, dst, dev, _) in enumerate(copies(refs[:na], *_mesh_pos())):
            _remote(src, dst, ssem.at[i], rsem.at[i], dev).start()
        token[...] = jnp.zeros_like(token)

    out = pl.pallas_call(
        body, name=name, in_specs=[HBM] * na + [ANY] * len(after), out_specs=[SEM, SEM] + [HBM] * na + [VMEM],
        out_shape=[pltpu.SemaphoreType.DMA((n,))] * 2 + [pltpu.HBM(a.shape, a.dtype) for a in arrays]
        + [jax.ShapeDtypeStruct((8, LANES), F32)],
        input_output_aliases={i: 2 + i for i in range(na)},
        compiler_params=pltpu.CompilerParams(has_side_effects=EFFECT),
    )(*[_hbm(a) for a in arrays], *after)
    return out[0], out[1], list(out[2:2 + na]), out[-1]


def _split_wait(name, arrays, ssem, rsem, copies, after):
    na = len(arrays)

    def body(*refs):
        for i, (src, _, dev, dst) in enumerate(copies(refs[:na], *_mesh_pos())):
            cp = _remote(src, dst, refs[na].at[i], refs[na + 1].at[i], dev)
            cp.wait_send()
            cp.wait_recv()

    return list(pl.pallas_call(
        body, name=name, in_specs=[HBM] * na + [SEM, SEM] + [ANY] * len(after), out_specs=[HBM] * na,
        out_shape=[pltpu.HBM(a.shape, a.dtype) for a in arrays], input_output_aliases={i: i for i in range(na)},
        compiler_params=pltpu.CompilerParams(has_side_effects=EFFECT),
    )(*arrays, ssem, rsem, *after))


def _pass_copies(grp, rels=(0, 1, 2)):
    def copies(land, x, y, c):
        _, ks = _other_chips(x, y)
        return [(w.half(land[wi], ks[j], c), w.half(land[wi], ks[j], c), (x, y, 1 - c), w.half(land[wi], ks[j], 1 - c))
                for wi, w in enumerate(grp) for j in rels]
    copies.n = len(grp) * len(rels)
    return copies


def _near_copies(grp):
    def copies(land, x, y, c):
        chips, ks = _other_chips(x, y)
        out = []
        for wi, w in enumerate(grp):
            mine = w.half(land[wi], 2 * x + y, c)
            out += [(mine, mine, (*chips[j], c), w.half(land[wi], ks[j], c)) for j in (0, 1)]
        return out
    copies.n = 2 * len(grp)
    return copies


def _far_copies(grp):
    def copies(land, x, y, c):
        chips, ks = _other_chips(x, y)
        out = []
        for wi, w in enumerate(grp):
            for j in (0, 1):
                q = w.quarter(land[wi], ks[j], c, j)
                out.append((q, q, (*chips[1 - j], c), w.quarter(land[wi], ks[2], c, j)))
        return out
    copies.n = 2 * len(grp)
    return copies


def _pair_copies(n, whole=False):
    def copies(refs, x, y, c):
        return [(refs[i] if whole else refs[i].at[:, 1 - c], refs[n + i], (x, y, 1 - c), refs[n + i]) for i in range(n)]
    return copies


def _share_copies(n):
    def copies(refs, x, y, c):
        return [(refs[i].at[c], refs[i].at[c], (x, y, 1 - c), refs[i].at[1 - c]) for i in range(n)]
    return copies


def _gather_conv_w(cw):
    ncw = cw.shape[1]

    def body(cw_ref, out_ref, ssem, rsem):
        x, y, c = _mesh_pos()
        k_me = 2 * x + y
        chips, ks = _other_chips(x, y)
        cols = lambda k: out_ref.at[:, pl.ds(pl.multiple_of(k * ncw, LANES), ncw)]
        cps = [_remote(cw_ref, cols(k_me), ssem.at[j], rsem.at[j], (*chip, c)) for j, chip in enumerate(chips)]
        for cp in cps:
            cp.start()
        for k in range(N_CHIPS):
            @pl.when(k_me == k)
            def _():
                out_ref[:, k * ncw:(k + 1) * ncw] = cw_ref[...]
        for j in range(3):
            _remote(cw_ref, cols(ks[j]), ssem.at[j], rsem.at[j], (*chips[j], c)).wait_recv()
        for cp in cps:
            cp.wait_send()

    return pl.pallas_call(
        body, name="gather_conv_w", in_specs=[VMEM], out_specs=VMEM,
        out_shape=jax.ShapeDtypeStruct((3, N_CHIPS * ncw), F32),
        scratch_shapes=[pltpu.SemaphoreType.DMA((3,)), pltpu.SemaphoreType.DMA((3,))],
    )(cw)


def _grad_tiles(w, n):
    return _tile8(w.R, 512) if w.R <= 512 else w.R // 2, _tile(n, 2048)


def _pair_sum(name, w, pos, grad, got):
    tr, tn = _grad_tiles(w, w.N)

    def body(pos_ref, g_ref, r_ref, o_ref):
        o_ref[...] = (g_ref[...].astype(F32) + r_ref[...].astype(F32)).astype(BF16)

    blk = pl.BlockSpec((None, tr, tn), lambda p, i, j, pos: (p, i, j))
    grid_spec = pltpu.PrefetchScalarGridSpec(
        num_scalar_prefetch=1, grid=(w.P, w.R // tr, w.N // tn),
        in_specs=[pl.BlockSpec((None, None, tr, tn), lambda p, i, j, pos: (p, pos[0], i, j)), blk], out_specs=blk)
    return pl.pallas_call(body, name=name, grid_spec=grid_spec, out_shape=jax.ShapeDtypeStruct((w.P, w.R, w.N), BF16),
                          compiler_params=_cp(("parallel",) * 3))(pos, grad, got)


def _scatter_start(name, ws, pairs):
    nw = len(ws)

    def body(*refs):
        pr, land = refs[:nw], refs[nw:2 * nw]
        ssem, rsem = refs[2 * nw], refs[2 * nw + 1]
        token = refs[4 * nw + 2]
        x, y, c = _mesh_pos()
        chips, ks = _other_chips(x, y)
        for i, w in enumerate(ws):
            for j, chip in enumerate(chips):
                _remote(w.part(pr[i], ks[j]), land[i].at[j], ssem.at[3 * i + j], rsem.at[3 * i + j], (*chip, c)).start()
        token[...] = jnp.zeros_like(token)

    lands = [lax.empty((3, w.R, w.nn), BF16) for w in ws]
    out = pl.pallas_call(
        body, name=name, in_specs=[HBM] * (2 * nw),
        out_specs=[SEM, SEM] + [HBM] * (2 * nw) + [VMEM],
        out_shape=[pltpu.SemaphoreType.DMA((3 * nw,))] * 2 + [pltpu.HBM(a.shape, a.dtype) for a in pairs + lands]
        + [jax.ShapeDtypeStruct((8, LANES), F32)],
        input_output_aliases={i: 2 + i for i in range(2 * nw)},
        compiler_params=pltpu.CompilerParams(has_side_effects=EFFECT),
    )(*[_hbm(a) for a in pairs + lands])
    return out[0], out[1], list(out[2:2 + nw]), list(out[2 + nw:2 + 2 * nw]), out[-1]


def _scatter_wait(name, ws, pairs, lands, ssem, rsem, after):
    nw = len(ws)

    def body(*refs):
        pr, land = refs[:nw], refs[nw:2 * nw]
        ssem_ref, rsem_ref = refs[2 * nw], refs[2 * nw + 1]
        x, y, c = _mesh_pos()
        chips, ks = _other_chips(x, y)
        for i, w in enumerate(ws):
            for j, chip in enumerate(chips):
                cp = _remote(w.part(pr[i], ks[j]), land[i].at[j], ssem_ref.at[3 * i + j], rsem_ref.at[3 * i + j], (*chip, c))
                cp.wait_send()
                cp.wait_recv()

    out = pl.pallas_call(
        body, name=name, in_specs=[HBM] * (2 * nw) + [SEM, SEM] + [ANY] * len(after), out_specs=[HBM] * (2 * nw),
        out_shape=[pltpu.HBM(a.shape, a.dtype) for a in pairs + lands],
        input_output_aliases={i: i for i in range(2 * nw)},
        compiler_params=pltpu.CompilerParams(has_side_effects=EFFECT),
    )(*pairs, *lands, ssem, rsem, *after)
    return list(out[:nw]), list(out[nw:])


def _final_sum(name, w, pos, grad, got, parts):
    tr, tn = _grad_tiles(w, w.nn)
    nbc = w.nn // tn
    if got is None:
        return _final_sum_pair(name, w, pos, grad, parts, tr, tn)

    def body(pos_ref, g_ref, r_ref, p_ref, o_ref):
        acc = g_ref[...].astype(F32) + r_ref[...].astype(F32)
        for j in range(3):
            acc = acc + p_ref[j].astype(F32)
        o_ref[...] = acc

    if w.colshard:
        g_spec = pl.BlockSpec((None, None, tr, tn), lambda i, j, pos: (0, pos[0], i, pos[1] * nbc + j))
        r_spec = pl.BlockSpec((None, tr, tn), lambda i, j, pos: (0, i, pos[1] * nbc + j))
    else:
        g_spec = pl.BlockSpec((None, None, tr, tn), lambda i, j, pos: (pos[1], pos[0], i, j))
        r_spec = pl.BlockSpec((None, tr, tn), lambda i, j, pos: (pos[1], i, j))
    grid_spec = pltpu.PrefetchScalarGridSpec(
        num_scalar_prefetch=1, grid=(w.R // tr, nbc),
        in_specs=[g_spec, r_spec, pl.BlockSpec((3, tr, tn), lambda i, j, pos: (0, i, j))],
        out_specs=pl.BlockSpec((None, tr, tn), lambda i, j, pos: (pos[0], i, j)))
    return pl.pallas_call(body, name=name, grid_spec=grid_spec, out_shape=jax.ShapeDtypeStruct((2, w.R, w.nn), F32),
                          compiler_params=_cp(("parallel",) * 2))(pos, grad, got, parts)


def _final_sum_pair(name, w, pos, pair, parts, tr, tn):
    nbc = w.nn // tn

    def body(pos_ref, g_ref, p_ref, o_ref):
        acc = g_ref[...].astype(F32)
        for j in range(3):
            acc = acc + p_ref[j].astype(F32)
        o_ref[...] = acc

    if w.colshard:
        g_spec = pl.BlockSpec((None, tr, tn), lambda i, j, pos: (0, i, pos[1] * nbc + j))
    else:
        g_spec = pl.BlockSpec((None, tr, tn), lambda i, j, pos: (pos[1], i, j))
    grid_spec = pltpu.PrefetchScalarGridSpec(
        num_scalar_prefetch=1, grid=(w.R // tr, nbc),
        in_specs=[g_spec, pl.BlockSpec((3, tr, tn), lambda i, j, pos: (0, i, j))],
        out_specs=pl.BlockSpec((None, tr, tn), lambda i, j, pos: (pos[0], i, j)))
    return pl.pallas_call(body, name=name, grid_spec=grid_spec, out_shape=jax.ShapeDtypeStruct((2, w.R, w.nn), F32),
                          compiler_params=_cp(("parallel",) * 2))(pos, pair, parts)


VEC_ROWS = 16


def _vector_step(d, n_conv, parts, params, deps=()):
    ncw = params[2][0].shape[1]
    n_par = len(params)

    def body(*refs):
        dg1, dba, dbb, dcw, dcb, dps, dg2, dgf, lc = refs[:9]
        wmv = refs[9:9 + 3 * n_par]
        refs = refs[9 + 3 * n_par + len(deps):]
        outs = refs[:4 * n_par]
        loss_ref = refs[4 * n_par]
        snd, got, ssem, rsem = refs[4 * n_par + 1:]
        x, y, c = _mesh_pos()
        me = 4 * x + 2 * y + c
        snd[...] = jnp.zeros_like(snd)
        for row, ref in ((0, dg1), (1, dba), (2, dbb), (3, dps), (4, dg2), (5, dgf), (6, lc)):
            snd[row:row + 1, :] = ref[...]
        snd[7:8, :n_conv] = dcb[...]
        snd[8:11, :n_conv] = dcw[...]
        cps = []
        for r in range(1, N_DEV):
            peer = tuple(1 - p if (r >> b) & 1 else p for p, b in ((x, 2), (y, 1), (c, 0)))
            cps.append(_remote(snd, got.at[me], ssem.at[r - 1], rsem.at[r - 1], peer))
        for cp in cps:
            cp.start()
        got[me] = snd[...]
        for r in range(1, N_DEV):
            peer = tuple(1 - p if (r >> b) & 1 else p for p, b in ((x, 2), (y, 1), (c, 0)))
            _remote(snd, got.at[4 * peer[0] + 2 * peer[1] + peer[2]], ssem.at[r - 1], rsem.at[r - 1], peer).wait_recv()
        for cp in cps:
            cp.wait_send()
        tot = got[0]
        for dev in range(1, N_DEV):
            tot = tot + got[dev]
        loss_ref[...] = jnp.sum(tot[6:7, :], axis=1, keepdims=True)
        k_me = 2 * x + y
        g_cw = jnp.zeros((3, ncw), F32)
        for k in range(N_CHIPS):
            g_cw = g_cw + jnp.where(k_me == k, tot[8:11, k * ncw:(k + 1) * ncw], 0.0)
        grads = [tot[0:1, :], jnp.concatenate([tot[1:2, :], tot[2:3, :]], axis=1), g_cw, tot[7:8, :n_conv],
                 tot[3:4, :], tot[4:5, :], tot[5:6, :]]
        for i, g in enumerate(grads):
            w_ref, m_ref, v_ref = wmv[3 * i:3 * i + 3]
            delta, nm, nv = _adamw_math(w_ref[...], g, m_ref[...], v_ref[...])
            outs[4 * i][...] = g
            outs[4 * i + 1][...] = delta
            outs[4 * i + 2][...] = nm
            outs[4 * i + 3][...] = nv

    args = list(parts)
    out_shape = []
    for w, m, v in params:
        args += [w, m, v]
        out_shape += [jax.ShapeDtypeStruct(w.shape, F32)] * 4
    out_shape.append(jax.ShapeDtypeStruct((1, 1), F32))
    return pl.pallas_call(
        body, name="vector_params_step", in_specs=[VMEM] * len(args) + [ANY] * len(deps),
        out_specs=[VMEM] * len(out_shape), out_shape=out_shape,
        scratch_shapes=[pltpu.VMEM((VEC_ROWS, d), F32), pltpu.VMEM((N_DEV, VEC_ROWS, d), F32),
                        pltpu.SemaphoreType.DMA((N_DEV - 1,)), pltpu.SemaphoreType.DMA((N_DEV - 1,))],
        compiler_params=pltpu.CompilerParams(vmem_limit_bytes=VMEM_LIMIT),
    )(*args, *deps)


def kernel(x, norm1_g, w_in, b_gate, conv_w, conv_b, w_a_out, w_pool, pool_scale, w_o, norm2_g, w_ffn_gate, w_ffn_up, w_ffn_down, final_g, loss_target, m_norm1_g, m_w_in, m_b_gate, m_conv_w, m_conv_b, m_w_a_out, m_w_pool, m_pool_scale, m_w_o, m_norm2_g, m_w_ffn_gate, m_w_ffn_up, m_w_ffn_down, m_final_g, v_norm1_g, v_w_in, v_b_gate, v_conv_w, v_conv_b, v_w_a_out, v_w_pool, v_pool_scale, v_w_o, v_norm2_g, v_w_ffn_gate, v_w_ffn_up, v_w_ffn_down, v_final_g):
    t, d = x.shape[1], x.shape[2]
    n_conv = conv_b.shape[1]
    n_groups, pool_cg, pool_dg = w_pool.shape[1], w_pool.shape[2], N_CHIPS * w_pool.shape[3]
    d_ff = N_CHIPS * w_ffn_gate.shape[2]
    assert n_conv // n_groups == pool_cg and n_conv % (n_groups * MIX_COLS) == 0 and n_groups == len(POOL_WINDOWS)

    big = {"w_in": (w_in, m_w_in, v_w_in), "w_a_out": (w_a_out, m_w_a_out, v_w_a_out), "w_pool": (w_pool, m_w_pool, v_w_pool),
           "w_o": (w_o, m_w_o, v_w_o), "w_ffn_gate": (w_ffn_gate, m_w_ffn_gate, v_w_ffn_gate),
           "w_ffn_up": (w_ffn_up, m_w_ffn_up, v_w_ffn_up), "w_ffn_down": (w_ffn_down, m_w_ffn_down, v_w_ffn_down)}
    colshard = {"w_in": True, "w_a_out": True, "w_pool": True, "w_o": False, "w_ffn_gate": True, "w_ffn_up": True,
                "w_ffn_down": False}
    names = list(big)
    shard2d = {n: big[n][0].reshape(-1, big[n][0].shape[-1]) for n in names}
    ws = [_Weight(n, *shard2d[n].shape, colshard[n]) for n in names]

    xs, tgt = x[0], loss_target[0]
    cw_loc = conv_w[0]
    pos = jnp.stack([lax.axis_index("c"), 2 * lax.axis_index("x") + lax.axis_index("y")]).astype(jnp.int32)
    by_name = {w.name: w for w in ws}
    groups = [[by_name[n] for n in g] for g in (["w_in"], ["w_a_out", "w_pool", "w_o"], ["w_ffn_gate"], ["w_ffn_up"],
                                                 ["w_ffn_down"])]
    first = [sum(len(g) for g in groups[:i]) for i in range(len(groups))]
    rgroups = [groups[0], groups[1], groups[2] + groups[3], groups[4]]

    cw_full = _gather_conv_w(cw_loc)
    cast = lambda w, dep: _cast_place(f"cast_{w.name}", w, pos, shard2d[w.name].reshape(2, w.R, w.nn), deps=[dep])
    chips, ks = _other_chips(lax.axis_index("x"), lax.axis_index("y"))
    kvec = jnp.stack([pos[1], *ks]).astype(jnp.int32)
    full = {}

    def start(name, arrays, copies, after=()):
        ssem, rsem, arrays, token = _split_start(name, arrays, copies.n, copies, after)
        return name, arrays, ssem, rsem, copies, token

    def wait(started, after):
        name, arrays, ssem, rsem, copies, _ = started
        return _split_wait(name + "_wait", arrays, ssem, rsem, copies, after)

    def pass_on(g, got, after=()):
        return start(f"pass_{g}", got, _pass_copies(groups[g]), after)

    def passed(g, st, after=None):
        got = wait(st, [st[5]] if after is None else after)
        full.update({w.name: a.reshape(w.P * 2 * w.R, w.N) for w, a in zip(groups[g], got)})

    near = start("near_0", [cast(w, cw_full) for w in groups[0]], _near_copies(groups[0]))
    rest = [cast(w, near[5]) for grp in groups[1:] for w in grp]
    h1 = _rms_fwd("norm1_fwd", xs, norm1_g, deps=[near[5]])
    proj = _proj_piece("proj_own", h1, shard2d["w_in"], None, kvec, 0, 1, deps=rest)
    got = wait(near, [proj])
    far = start("far_0", got, _far_copies(groups[0]))
    sems_b, lands_b, tok_b = _gather_start("gather_start_b", groups[1:2], rest[:3], after=[far[5]])
    st = start("pass_near_0", far[1], _pass_copies(groups[0], (0, 1)), [tok_b])
    got = wait(st, [st[5]])
    proj = _proj_piece("proj_near", h1, got[0].reshape(-1, groups[0][0].N), proj, kvec, 1, 2)
    st = start("pass_far_0", wait((far[0], got) + far[2:], [proj]), _pass_copies(groups[0], (2,)))
    got = wait(st, [st[5]])
    w_in_full = got[0].reshape(-1, groups[0][0].N)
    proj = _proj_piece("proj_far", h1, w_in_full, proj, kvec, 3, 1)
    got = _gather_wait("gather_wait_1", groups[1], lands_b, *sems_b[0], proj)
    near_g = start("near_2", rest[3:4], _near_copies(groups[2]), got)
    st = pass_on(1, got, [near_g[5]])
    z, p = _mixer_fwd("mixer_fwd", proj, cw_full, conv_b, n_conv, n_groups, deps=[st[5]])
    passed(1, st, [z])
    wp_full = full["w_pool"].reshape(n_groups, pool_cg, pool_dg)
    ya = _mm_nn("conv_out", z, full["w_a_out"], BF16)
    yb = _gmm_nn("pool_out", p, wp_full, BF16)
    merged = _merge_fwd("merge_fwd", proj, b_gate, ya, yb, pool_scale)
    far_g = start("far_2", wait(near_g, [merged]), _far_copies(groups[2]))
    near_u = start("near_3", rest[4:5], _near_copies(groups[3]), [far_g[5]])
    x2 = _mm_nn("mix_out", merged, full["w_o"], F32, add=xs, deps=[near_u[5]])
    st = pass_on(2, wait(far_g, [x2]))
    h2 = _rms_fwd("norm2_fwd", x2, norm2_g, deps=[st[5]])
    passed(2, st, [h2])
    gate = _mm_nn("ffn_gate_a", h2, full["w_ffn_gate"], BF16, part=(0, 2))
    far_u = start("far_3", wait(near_u, [gate]), _far_copies(groups[3]))
    near_d = start("near_4", rest[5:6], _near_copies(groups[4]), [far_u[5]])
    gate = _mm_nn("ffn_gate_b", h2, full["w_ffn_gate"], BF16, part=(1, 2), prev=gate, deps=[near_d[5]])
    passed(3, pass_on(3, wait(far_u, [gate])))
    up_act = _ffn_up_act("ffn_up_act_a", h2, full["w_ffn_up"], gate, part=(0, 2))
    far_d = start("far_4", wait(near_d, [up_act[0]]), _far_copies(groups[4]))
    up, act = _ffn_up_act("ffn_up_act_b", h2, full["w_ffn_up"], gate, part=(1, 2), prev=up_act, deps=[far_d[5]])
    passed(4, pass_on(4, wait(far_d, [act])))
    x3 = _mm_nn("ffn_down", act, full["w_ffn_down"], F32, add=x2, tk=d_ff // 4)

    pending = {}

    def pair_start(g, grads):
        grp = rgroups[g]
        gcan = [grads[w.name].reshape(w.P, 2, w.R, w.N) for w in grp]
        slots = [lax.empty((w.P, w.R, w.N), BF16) for w in grp]
        pending[g] = _split_start(f"pair_start_{g}", gcan + slots, len(grp), _pair_copies(len(grp)))
        return pending[g][3]

    def scatter_start(g, after):
        grp = rgroups[g]
        n = len(grp)
        ssem, rsem, arrs, _ = pending[g]
        arrs = _split_wait(f"pair_wait_{g}", arrs, ssem, rsem, _pair_copies(n), after)
        gcan, sib = arrs[:n], arrs[n:]
        pairs = [_pair_sum(f"pair_sum_{w.name}", w, pos, a, s) for w, a, s in zip(grp, gcan, sib)]
        ssem, rsem, pairs, slots, token = _scatter_start(f"scatter_start_{g}", grp, pairs)
        pending[g] = (gcan, sib, pairs, slots, ssem, rsem)
        return token

    def pair_start_halves(g, ab, deps):
        grp = rgroups[g]
        sent = [_mm_tn_half(f"d{w.name}_sib", a, b, pos, False, deps=deps if i == 0 else ()) for i, (w, (a, b)) in enumerate(zip(grp, ab))]
        slots = [lax.empty((1, w.R, w.N), BF16) for w in grp]
        pending[g] = _split_start(f"pair_start_{g}", sent + slots, len(grp), _pair_copies(len(grp), whole=True))
        return pending[g][3]

    def scatter_start_halves(g, ab, after):
        grp = rgroups[g]
        n = len(grp)
        ssem, rsem, arrs, _ = pending[g]
        arrs = _split_wait(f"pair_wait_{g}", arrs, ssem, rsem, _pair_copies(n, whole=True), after)
        pairs = [_mm_tn_half(f"d{w.name}_own", a, b, pos, True, add=s) for w, (a, b), s in zip(grp, ab, arrs[n:])]
        ssem, rsem, pairs, slots, token = _scatter_start(f"scatter_start_{g}", grp, pairs)
        pending[g] = (None, None, pairs, slots, ssem, rsem)
        return token

    def reduce_finish(g, after):
        grp = rgroups[g]
        gcan, sib, pairs, slots, ssem, rsem = pending[g]
        pairs, parts = _scatter_wait(f"scatter_wait_{g}", grp, pairs, slots, ssem, rsem, after)
        if gcan is None:
            return [_final_sum(f"final_sum_{w.name}", w, pos, a, None, q) for w, a, q in zip(grp, pairs, parts)]
        return [_final_sum(f"final_sum_{w.name}", w, pos, a, s, q) for w, a, s, q in zip(grp, gcan, sib, parts)]

    grads = {}
    dx3, dx3b, d_gf, loss_cols = _final_bwd("final_bwd", x3, final_g.reshape(1, d), tgt)
    dgate, dup = _ffn_bwd("ffn_bwd", dx3b, full["w_ffn_down"], gate, up)
    grads["w_ffn_down"] = _mm_tn("dw_ffn_down", act, dx3b, BF16)
    tok = pair_start(3, grads)
    dh2 = _mm_nt("d_h2", [(dgate, full["w_ffn_gate"]), (dup, full["w_ffn_up"])], BF16, tk=d_ff // 4, deps=[tok])
    tok = scatter_start(3, [dh2])
    tok = pair_start_halves(2, [(h2, dgate), (h2, dup)], [tok])
    dx2, dx2b, d_g2 = _rms_bwd("norm2_bwd", x2, norm2_g, dh2, dx3, True, deps=[tok])
    dmerged = _mm_nt("d_merged", [(dx2b, full["w_o"])], BF16, tk=d)
    grads["w_o"] = _mm_tn("dw_o", merged, dx2b, BF16)
    tok = scatter_start_halves(2, [(h2, dgate), (h2, dup)], [grads["w_o"]])
    dya, dyb, dproj, d_bga, d_bgb, d_ps = _merge_bwd("merge_bwd", dmerged, proj, b_gate, ya, yb, pool_scale, deps=[tok])
    dz = _mm_nt("d_z", [(dya, full["w_a_out"])], BF16, tk=d)
    grads["w_a_out"] = _mm_tn("dw_a_out", z, dya, BF16)
    dp = _gmm_nt("d_pool", dyb, wp_full, BF16)
    grads["w_pool"] = _gmm_tn("dw_pool", p, dyb, n_groups, BF16)
    tok = pair_start(1, grads)
    dproj, d_cw, d_cb = _mixer_bwd("mixer_bwd", dz, dp, proj, cw_full, conv_b, dproj, n_conv, n_groups, deps=[tok])
    tok = scatter_start(1, [dproj])
    tok = pair_start_halves(0, [(h1, dproj)], [tok])
    dh1 = _mm_nt("d_h1", [(dproj, w_in_full)], BF16, tk=proj.shape[1] // 4, deps=[tok])
    tok = scatter_start_halves(0, [(h1, dproj)], [dh1])
    grad_x, d_g1 = _rms_bwd("norm1_bwd", xs, norm1_g, dh1, dx2, False, deps=[tok])

    g_big, d_big, m_big, v_big = {}, {}, {}, {}

    def update(wsub, shared):
        out = []
        for w, g in zip(wsub, shared):
            wt, mt, vt = big[w.name]
            g2 = g.reshape(2 * w.R, w.nn)
            go, dl, nm, nv = _adamw(f"adamw_{w.name}", shard2d[w.name], g2, mt.reshape(g2.shape), vt.reshape(g2.shape))
            g_big[w.name], d_big[w.name], m_big[w.name], v_big[w.name] = (a.reshape(wt.shape) for a in (go, dl, nm, nv))
            out.append(nv)
        return out

    after = [grad_x]
    started = []
    for g in (3, 2, 1):
        halves = reduce_finish(g, after)
        share = _share_copies(len(halves))
        ssem, rsem, halves, tok = _split_start(f"share_start_{g}", halves, len(halves), share)
        started.append((g, ssem, rsem, halves, share))
        after = [tok]
    for g, ssem, rsem, halves, share in started:
        after = update(rgroups[g], _split_wait(f"share_wait_{g}", halves, ssem, rsem, share, after))
    share = _share_copies(1)
    ssem, rsem, halves, tok = _split_start("share_start_0", reduce_finish(0, after), 1, share)

    vec_names = ["norm1_g", "b_gate", "conv_w", "conv_b", "pool_scale", "norm2_g", "final_g"]
    vec = {"norm1_g": (norm1_g, m_norm1_g, v_norm1_g), "b_gate": (b_gate, m_b_gate, v_b_gate),
           "conv_w": (cw_loc, m_conv_w[0], v_conv_w[0]), "conv_b": (conv_b, m_conv_b, v_conv_b),
           "pool_scale": (pool_scale, m_pool_scale, v_pool_scale), "norm2_g": (norm2_g, m_norm2_g, v_norm2_g),
           "final_g": tuple(a.reshape(1, d) for a in (final_g, m_final_g, v_final_g))}
    vout = _vector_step(d, n_conv, [d_g1, d_bga, d_bgb, d_cw, d_cb, d_ps, d_g2, d_gf, loss_cols],
                        [vec[n] for n in vec_names], deps=halves)
    update(rgroups[0], _split_wait("share_wait_0", halves, ssem, rsem, share, []))

    shapes = {"conv_w": conv_w.shape, "final_g": final_g.shape}
    g_vec, d_vec, m_vec, v_vec = ({n: vout[4 * i + q].reshape(shapes.get(n, vec[n][0].shape)) for i, n in enumerate(vec_names)}
                                  for q in range(4))
    loss = vout[-1].reshape(())

    order = ["norm1_g", "w_in", "b_gate", "conv_w", "conv_b", "w_a_out", "w_pool", "pool_scale", "w_o", "norm2_g",
             "w_ffn_gate", "w_ffn_up", "w_ffn_down", "final_g"]
    pick = lambda vecs, bigs: [vecs[n] if n in vecs else bigs[n] for n in order]
    return (loss, grad_x.reshape(x.shape), *pick(g_vec, g_big), *pick(d_vec, d_big), *pick(m_vec, m_big),
            *pick(v_vec, v_big))
```

```python
import functools

import jax
import jax.numpy as jnp
from jax import lax
from jax.experimental import pallas as pl
from jax.experimental.pallas import tpu as pltpu

F32, BF16 = jnp.float32, jnp.bfloat16
MESH = pl.DeviceIdType.MESH
ANY = pl.BlockSpec(memory_space=pl.ANY)
VMEM = pl.BlockSpec(memory_space=pltpu.VMEM)
HBM = pl.BlockSpec(memory_space=pltpu.HBM)
SEM = pl.BlockSpec(memory_space=pltpu.SEMAPHORE)
EFFECT = pltpu.SideEffectType.DATAFLOW_SIDE_EFFECTING

EPS = 1e-6
POOL_WINDOWS = (2, 4, 8, 16)
ADAM_LR, ADAM_B1, ADAM_B2, ADAM_EPS, ADAM_WD, ADAM_STEP = 0.001, 0.9, 0.999, 1e-08, 0.01, 10

V7X_VMEM_BYTES = 64 * 1024 * 1024
VMEM_LIMIT = V7X_VMEM_BYTES * 3 // 4
LANES = 128
COL_TILE = 8 * LANES
N_CHIPS = 4
N_DEV = 8

_DIMS = {
    "nn": (((1,), (0,)), ((), ())),
    "nt": (((1,), (1,)), ((), ())),
    "tn": (((0,), (0,)), ((), ())),
}


def _cp(sem):
    return pltpu.CompilerParams(dimension_semantics=sem, vmem_limit_bytes=VMEM_LIMIT)


def _mesh_pos():
    return lax.axis_index("x"), lax.axis_index("y"), lax.axis_index("c")


def _mm(name, pairs, *, mode, grid, out_shape, o_spec, nk=1, kaxis=None, add=None, deps=(), prev=None):
    npair = len(pairs)
    has_add = add is not None

    def body(*refs):
        ab = refs[: 2 * npair]
        pos = 2 * npair
        add_ref = refs[pos] if has_add else None
        pos += int(has_add) + len(deps) + (prev is not None)
        o_ref = refs[pos]
        acc_ref = refs[pos + 1] if nk > 1 else None
        d = None
        for p in range(npair):
            t = lax.dot_general(ab[2 * p][...], ab[2 * p + 1][...], _DIMS[mode], preferred_element_type=F32)
            d = t if d is None else d + t
        if nk == 1:
            if has_add:
                d = d + add_ref[...].astype(F32)
            o_ref[...] = d.astype(o_ref.dtype)
        else:
            k = pl.program_id(kaxis)

            @pl.when(k == 0)
            def _():
                acc_ref[...] = d

            @pl.when(k > 0)
            def _():
                acc_ref[...] += d

            @pl.when(k == nk - 1)
            def _():
                r = acc_ref[...]
                if has_add:
                    r = r + add_ref[...].astype(F32)
                o_ref[...] = r.astype(o_ref.dtype)

    args, specs = [], []
    for a, a_spec, b, b_spec in pairs:
        args += [a, b]
        specs += [a_spec, b_spec]
    if has_add:
        args.append(add[0])
        specs.append(add[1])
    args += list(deps)
    specs += [ANY] * len(deps)
    aliases = {}
    if prev is not None:
        aliases = {len(args): 0}
        args.append(prev)
        specs.append(ANY)
    scratch = []
    if nk > 1:
        blk = [d for d in o_spec.block_shape if d is not None]
        scratch = [pltpu.VMEM(tuple(blk), F32)]
    sem = tuple("arbitrary" if (nk > 1 and ax == kaxis) else "parallel" for ax in range(len(grid)))
    return pl.pallas_call(
        body, name=name, grid=grid, in_specs=specs, out_specs=o_spec, out_shape=out_shape,
        scratch_shapes=scratch, input_output_aliases=aliases, compiler_params=_cp(sem),
    )(*args)


def _tile_span(n_tiles, part):
    if part is None:
        return 0, n_tiles
    p, of = part
    return p * n_tiles // of, (p + 1) * n_tiles // of


def _tile(n, pref):
    if n <= pref:
        return n
    for t in range(pref, 0, -LANES):
        if t % LANES == 0 and n % t == 0:
            return t
    raise ValueError(f"no tile for {n}")


def _mm_nn(name, a, b, out_dtype, add=None, tk=None, deps=(), part=None, prev=None, tiles=None):
    m, kk = a.shape
    n = b.shape[1]
    tm, tn = _tile(m, 1024), _tile(n, COL_TILE)
    if tiles is not None:
        tm, tn = _tile(m, tiles[0]), _tile(n, tiles[1])
    out_shape = jax.ShapeDtypeStruct((m, n), out_dtype)
    if tk is None or tk == kk:
        j0, j1 = _tile_span(n // tn, part)
        grid = (m // tm, j1 - j0)
        pairs = [(a, pl.BlockSpec((tm, kk), lambda i, j: (i, 0)), b, pl.BlockSpec((kk, tn), lambda i, j: (0, j0 + j)))]
        o_spec = pl.BlockSpec((tm, tn), lambda i, j: (i, j0 + j))
        add_ = None if add is None else (add, pl.BlockSpec((tm, tn), lambda i, j: (i, j0 + j)))
        return _mm(name, pairs, mode="nn", grid=grid, out_shape=out_shape, o_spec=o_spec, add=add_, deps=deps, prev=prev)
    tn = _tile(n, 1024)
    nk = kk // tk
    grid = (m // tm, n // tn, nk)
    pairs = [(a, pl.BlockSpec((tm, tk), lambda i, j, k: (i, k)), b, pl.BlockSpec((tk, tn), lambda i, j, k: (k, j)))]
    o_spec = pl.BlockSpec((tm, tn), lambda i, j, k: (i, j))
    add_ = None if add is None else (add, pl.BlockSpec((tm, tn), lambda i, j, k: (i, j)))
    return _mm(name, pairs, mode="nn", grid=grid, out_shape=out_shape, o_spec=o_spec, nk=nk, kaxis=2, add=add_, deps=deps)


def _mm_nt(name, abs_, out_dtype, tk, deps=(), tiles=None):
    m, kk = abs_[0][0].shape
    n = abs_[0][1].shape[0]
    tm = _tile(m, 1024)
    nk = kk // tk
    tn = _tile(n, COL_TILE if nk == 1 else 1024)
    if tiles is not None:
        tm, tn = _tile(m, tiles[0]), _tile(n, tiles[1])
    out_shape = jax.ShapeDtypeStruct((m, n), out_dtype)
    if nk == 1:
        grid = (m // tm, n // tn)
        pairs = [(a, pl.BlockSpec((tm, kk), lambda i, j: (i, 0)), b, pl.BlockSpec((tn, kk), lambda i, j: (j, 0)))
                 for a, b in abs_]
        o_spec = pl.BlockSpec((tm, tn), lambda i, j: (i, j))
        return _mm(name, pairs, mode="nt", grid=grid, out_shape=out_shape, o_spec=o_spec, deps=deps)
    grid = (m // tm, n // tn, nk)
    pairs = [(a, pl.BlockSpec((tm, tk), lambda i, j, k: (i, k)), b, pl.BlockSpec((tn, tk), lambda i, j, k: (j, k)))
             for a, b in abs_]
    o_spec = pl.BlockSpec((tm, tn), lambda i, j, k: (i, j))
    return _mm(name, pairs, mode="nt", grid=grid, out_shape=out_shape, o_spec=o_spec, nk=nk, kaxis=2, deps=deps)


def _mm_tn(name, a, b, out_dtype, deps=()):
    t, m = a.shape
    n = b.shape[1]
    tm, tn = _tile(m, 512), _tile(n, 2048)
    if n > m:
        grid = (n // tn, m // tm)
        a_map, b_map, o_map = (lambda j, i: (0, i)), (lambda j, i: (0, j)), (lambda j, i: (i, j))
    else:
        grid = (m // tm, n // tn)
        a_map, b_map, o_map = (lambda i, j: (0, i)), (lambda i, j: (0, j)), (lambda i, j: (i, j))
    pairs = [(a, pl.BlockSpec((t, tm), a_map), b, pl.BlockSpec((t, tn), b_map))]
    o_spec = pl.BlockSpec((tm, tn), o_map)
    return _mm(name, pairs, mode="tn", grid=grid, out_shape=jax.ShapeDtypeStruct((m, n), out_dtype), o_spec=o_spec,
               deps=deps)


def _mm_tn_half(name, a, b, pos, mine, add=None, deps=()):
    t, m = a.shape
    r, n = m // 2, b.shape[1]
    tm, tn = _tile(r, 512), _tile(n, 2048)
    nbi = r // tm
    half = (lambda pos: pos[0]) if mine else (lambda pos: 1 - pos[0])
    if n > r:
        grid, ij = (n // tn, nbi), (lambda g0, g1: (g1, g0))
    else:
        grid, ij = (nbi, n // tn), (lambda g0, g1: (g0, g1))
    has_add = add is not None

    def body(pos_ref, a_ref, b_ref, *rest):
        d = lax.dot_general(a_ref[...], b_ref[...], _DIMS["tn"], preferred_element_type=F32)
        if has_add:
            d = d + rest[0][...].astype(F32)
        rest[-1][...] = d.astype(BF16)

    o_spec = pl.BlockSpec((None, tm, tn), lambda g0, g1, pos: (0, *ij(g0, g1)))
    grid_spec = pltpu.PrefetchScalarGridSpec(
        num_scalar_prefetch=1, grid=grid,
        in_specs=[pl.BlockSpec((t, tm), lambda g0, g1, pos: (0, half(pos) * nbi + ij(g0, g1)[0])),
                  pl.BlockSpec((t, tn), lambda g0, g1, pos: (0, ij(g0, g1)[1]))]
        + ([o_spec] if has_add else []) + [ANY] * len(deps),
        out_specs=o_spec)
    return pl.pallas_call(body, name=name, grid_spec=grid_spec, out_shape=jax.ShapeDtypeStruct((1, r, n), BF16),
                          compiler_params=_cp(("parallel",) * 2))(pos, a, b, *([add] if has_add else []), *deps)


def _proj_piece(name, h, w, prev, kvec, base, count, deps=()):
    t, kk = h.shape
    own = w.dtype == F32
    nn = w.shape[1] if own else w.shape[1] // N_CHIPS
    tm, tn = _tile(t, 1024), _tile(nn, COL_TILE)
    nb = nn // tn

    def body(kv_ref, h_ref, w_ref, *rest):
        rest[-1][...] = lax.dot_general(h_ref[...], w_ref[...].astype(BF16), _DIMS["nn"],
                                        preferred_element_type=F32).astype(BF16)

    cols = lambda s, i, j, kv: (0, j) if own else (0, kv[base + s] * nb + j)
    extra = ([] if prev is None else [prev]) + list(deps)
    grid_spec = pltpu.PrefetchScalarGridSpec(
        num_scalar_prefetch=1, grid=(count, t // tm, nb),
        in_specs=[pl.BlockSpec((tm, kk), lambda s, i, j, kv: (i, 0)), pl.BlockSpec((kk, tn), cols)] + [ANY] * len(extra),
        out_specs=pl.BlockSpec((tm, tn), lambda s, i, j, kv: (i, kv[base + s] * nb + j)))
    return pl.pallas_call(body, name=name, grid_spec=grid_spec, out_shape=jax.ShapeDtypeStruct((t, N_CHIPS * nn), BF16),
                          input_output_aliases={} if prev is None else {3: 0},
                          compiler_params=_cp(("parallel",) * 3))(kvec, h, w, *extra)


def _gmm_nn(name, p, w, out_dtype):
    t = p.shape[0]
    g, cg, dg = w.shape
    tm = _tile(t, 1024)
    pairs = [(p, pl.BlockSpec((tm, cg), lambda i, j: (i, j)), w, pl.BlockSpec((None, cg, dg), lambda i, j: (j, 0, 0)))]
    o_spec = pl.BlockSpec((tm, dg), lambda i, j: (i, j))
    return _mm(name, pairs, mode="nn", grid=(t // tm, g), out_shape=jax.ShapeDtypeStruct((t, g * dg), out_dtype),
               o_spec=o_spec)


def _gmm_nt(name, dy, w, out_dtype):
    t = dy.shape[0]
    g, cg, dg = w.shape
    tm = _tile(t, 1024)
    pairs = [(dy, pl.BlockSpec((tm, dg), lambda i, j: (i, j)), w, pl.BlockSpec((None, cg, dg), lambda i, j: (j, 0, 0)))]
    o_spec = pl.BlockSpec((tm, cg), lambda i, j: (i, j))
    return _mm(name, pairs, mode="nt", grid=(t // tm, g), out_shape=jax.ShapeDtypeStruct((t, g * cg), out_dtype),
               o_spec=o_spec)


def _gmm_tn(name, p, dy, g, out_dtype):
    t = p.shape[0]
    cg, dg = p.shape[1] // g, dy.shape[1] // g
    pairs = [(p, pl.BlockSpec((t, cg), lambda j: (0, j)), dy, pl.BlockSpec((t, dg), lambda j: (0, j)))]
    o_spec = pl.BlockSpec((None, cg, dg), lambda j: (j, 0, 0))
    return _mm(name, pairs, mode="tn", grid=(g,), out_shape=jax.ShapeDtypeStruct((g, cg, dg), out_dtype), o_spec=o_spec)


ROW_TILE = 256


def _rows(t):
    return _tile8(t, ROW_TILE)


def _tile8(n, pref):
    if n <= pref:
        return n
    for t in range(pref, 0, -8):
        if n % t == 0:
            return t
    raise ValueError(f"no row tile for {n}")


def _cast_place(name, w, pos, shard, deps=()):
    tr = _tile8(w.R, 512)
    if w.colshard:
        o_map = lambda h, i, pos: (0, h, i, pos[1])
    else:
        o_map = lambda h, i, pos: (pos[1], h, i, 0)

    def body(pos_ref, w_ref, *rest):
        rest[-1][...] = w_ref[...].astype(BF16)

    grid_spec = pltpu.PrefetchScalarGridSpec(
        num_scalar_prefetch=1, grid=(2, w.R // tr),
        in_specs=[pl.BlockSpec((None, tr, w.nn), lambda h, i, pos: (h, i, 0))] + [ANY] * len(deps),
        out_specs=pl.BlockSpec((None, None, tr, w.nn), o_map))
    return pl.pallas_call(body, name=name, grid_spec=grid_spec, out_shape=jax.ShapeDtypeStruct((w.P, 2, w.R, w.N), BF16),
                          compiler_params=_cp(("parallel", "parallel")))(pos, shard, *deps)


def _rms_fwd(name, x, g, deps=()):
    t, d = x.shape
    tm = _rows(t)

    def body(x_ref, g_ref, *rest):
        xf = x_ref[...]
        r = lax.rsqrt(jnp.mean(xf * xf, axis=-1, keepdims=True) + EPS)
        rest[-1][...] = (xf * r * g_ref[...]).astype(BF16)

    return pl.pallas_call(
        body, name=name, grid=(t // tm,),
        in_specs=[pl.BlockSpec((tm, d), lambda i: (i, 0)), pl.BlockSpec((1, d), lambda i: (0, 0))] + [ANY] * len(deps),
        out_specs=pl.BlockSpec((tm, d), lambda i: (i, 0)), out_shape=jax.ShapeDtypeStruct((t, d), BF16),
        compiler_params=_cp(("parallel",)),
    )(x, g, *deps)


def _rms_bwd(name, x, g, dh, dres, want_bf16, deps=()):
    t, d = x.shape
    tm = _rows(t)

    def body(x_ref, g_ref, dh_ref, dres_ref, *rest):
        rest = rest[len(deps):]
        dx_ref, rest = rest[0], rest[1:]
        dg_ref = rest[-1]
        xf = x_ref[...]
        r = lax.rsqrt(jnp.mean(xf * xf, axis=-1, keepdims=True) + EPS)
        xh = xf * r
        dhf = dh_ref[...].astype(F32)
        dxh = dhf * g_ref[...]
        m = jnp.mean(dxh * xh, axis=-1, keepdims=True)
        dx = dres_ref[...] + r * (dxh - xh * m)
        dx_ref[...] = dx
        if want_bf16:
            rest[0][...] = dx.astype(BF16)

        @pl.when(pl.program_id(0) == 0)
        def _():
            dg_ref[...] = jnp.zeros_like(dg_ref)

        dg_ref[...] += jnp.sum(dhf * xh, axis=0, keepdims=True)

    row = pl.BlockSpec((tm, d), lambda i: (i, 0))
    vec = pl.BlockSpec((1, d), lambda i: (0, 0))
    out_specs = [row] + ([row] if want_bf16 else []) + [vec]
    out_shape = ([jax.ShapeDtypeStruct((t, d), F32)] + ([jax.ShapeDtypeStruct((t, d), BF16)] if want_bf16 else [])
                 + [jax.ShapeDtypeStruct((1, d), F32)])
    return pl.pallas_call(body, name=name, grid=(t // tm,), in_specs=[row, vec, row, row] + [ANY] * len(deps),
                          out_specs=out_specs, out_shape=out_shape, compiler_params=_cp(("arbitrary",)))(x, g, dh, dres, *deps)


def _final_bwd(name, x3, gf, tgt):
    t, d = x3.shape
    tm = _rows(t)

    def body(x_ref, g_ref, t_ref, dx_ref, dxb_ref, dg_ref, lc_ref):
        xf = x_ref[...]
        g = g_ref[...]
        r = lax.rsqrt(jnp.mean(xf * xf, axis=-1, keepdims=True) + EPS)
        xh = xf * r
        diff = xh * g - t_ref[...]
        dy = diff * (1.0 / d)
        dxh = dy * g
        m = jnp.mean(dxh * xh, axis=-1, keepdims=True)
        dx = r * (dxh - xh * m)
        dx_ref[...] = dx
        dxb_ref[...] = dx.astype(BF16)

        @pl.when(pl.program_id(0) == 0)
        def _():
            dg_ref[...] = jnp.zeros_like(dg_ref)
            lc_ref[...] = jnp.zeros_like(lc_ref)

        dg_ref[...] += jnp.sum(dy * xh, axis=0, keepdims=True)
        lc_ref[...] += jnp.sum(diff * diff, axis=0, keepdims=True) * (0.5 / d)

    row = pl.BlockSpec((tm, d), lambda i: (i, 0))
    vec = pl.BlockSpec((1, d), lambda i: (0, 0))
    return pl.pallas_call(
        body, name=name, grid=(t // tm,), in_specs=[row, vec, row], out_specs=[row, row, vec, vec],
        out_shape=[jax.ShapeDtypeStruct((t, d), F32), jax.ShapeDtypeStruct((t, d), BF16),
                   jax.ShapeDtypeStruct((1, d), F32), jax.ShapeDtypeStruct((1, d), F32)],
        compiler_params=_cp(("arbitrary",)),
    )(x3, gf, tgt)


def _shift_down(v, k, t_idx):
    return jnp.where(t_idx >= k, pltpu.roll(v, k, 0), 0.0)


def _shift_up(v, k, t_idx):
    n = v.shape[0]
    return jnp.where(t_idx < n - k, pltpu.roll(v, n - k, 0), 0.0)


def _window_sums(v, shift, t_idx, grp):
    s = v + shift(v, 1, t_idx)
    out = s
    for lvl in range(1, len(POOL_WINDOWS)):
        s = s + shift(s, 1 << lvl, t_idx)
        out = jnp.where(grp >= lvl, s, out)
    return out


def _window_weight(t_idx, grp):
    return 1.0 / jnp.minimum(t_idx[:, :1] + 1, jnp.left_shift(2, grp)).astype(F32)


MIX_COLS = 256


def _mixer_fwd(name, proj, cw, cb, n_conv, n_groups, deps=()):
    t = proj.shape[0]
    nb = n_conv // MIX_COLS
    per_group = n_conv // n_groups // MIX_COLS

    def body(ba_ref, ca_ref, va_ref, vb_ref, cw_ref, cb_ref, *rest):
        z_ref, p_ref = rest[len(deps):]
        t_idx = lax.broadcasted_iota(jnp.int32, (t, MIX_COLS), 0)
        q = ca_ref[...].astype(F32) * va_ref[...].astype(F32)
        w = cw_ref[...]
        u = cb_ref[...] + w[0:1] * _shift_down(q, 2, t_idx) + w[1:2] * _shift_down(q, 1, t_idx) + w[2:3] * q
        z_ref[...] = (ba_ref[...].astype(F32) * u).astype(BF16)
        grp = pl.program_id(0) // per_group
        v = vb_ref[...].astype(F32)
        p_ref[...] = (_window_sums(v, _shift_down, t_idx, grp) * _window_weight(t_idx, grp) - v).astype(BF16)

    col = lambda s: pl.BlockSpec((t, MIX_COLS), lambda j: (0, s * nb + j))
    return pl.pallas_call(
        body, name=name, grid=(nb,),
        in_specs=[col(0), col(1), col(2), col(3), pl.BlockSpec((3, MIX_COLS), lambda j: (0, j)),
                  pl.BlockSpec((1, MIX_COLS), lambda j: (0, j))] + [ANY] * len(deps),
        out_specs=[col(0), col(0)],
        out_shape=[jax.ShapeDtypeStruct((t, n_conv), BF16), jax.ShapeDtypeStruct((t, n_conv), BF16)],
        compiler_params=_cp(("parallel",)),
    )(proj, proj, proj, proj, cw, cb, *deps)


def _mixer_bwd(name, dz, dp, proj, cw, cb, dproj, n_conv, n_groups, deps=()):
    t = proj.shape[0]
    nb = n_conv // MIX_COLS
    per_group = n_conv // n_groups // MIX_COLS

    def body(dz_ref, dp_ref, ba_ref, ca_ref, va_ref, cw_ref, cb_ref, _, *rest):
        o_ref, dcw_ref, dcb_ref, scr = rest[len(deps):]
        s = pl.program_id(1)

        @pl.when(s == 0)
        def _():
            t_idx = lax.broadcasted_iota(jnp.int32, (t, MIX_COLS), 0)
            ca, va = ca_ref[...].astype(F32), va_ref[...].astype(F32)
            q = ca * va
            q1, q2 = _shift_down(q, 1, t_idx), _shift_down(q, 2, t_idx)
            w = cw_ref[...]
            u = cb_ref[...] + w[0:1] * q2 + w[1:2] * q1 + w[2:3] * q
            dzf = dz_ref[...].astype(F32)
            du = dzf * ba_ref[...].astype(F32)
            scr[0] = (dzf * u).astype(BF16)
            dq = w[2:3] * du + w[1:2] * _shift_up(du, 1, t_idx) + w[0:1] * _shift_up(du, 2, t_idx)
            scr[1] = (dq * va).astype(BF16)
            scr[2] = (dq * ca).astype(BF16)
            dcb_ref[...] = jnp.sum(du, axis=0, keepdims=True)
            dcw_ref[0:1, :] = jnp.sum(du * q2, axis=0, keepdims=True)
            dcw_ref[1:2, :] = jnp.sum(du * q1, axis=0, keepdims=True)
            dcw_ref[2:3, :] = jnp.sum(du * q, axis=0, keepdims=True)
            grp = pl.program_id(0) // per_group
            dpf = dp_ref[...].astype(F32)
            e = dpf * _window_weight(t_idx, grp)
            scr[3] = (_window_sums(e, _shift_up, t_idx, grp) - dpf).astype(BF16)

        o_ref[...] = scr[s]

    col = lambda c: pl.BlockSpec((t, MIX_COLS), lambda j, s: (0, c * nb + j))
    own = pl.BlockSpec((t, MIX_COLS), lambda j, s: (0, j))
    return pl.pallas_call(
        body, name=name, grid=(nb, 4),
        in_specs=[own, own, col(0), col(1), col(2), pl.BlockSpec((3, MIX_COLS), lambda j, s: (0, j)),
                  pl.BlockSpec((1, MIX_COLS), lambda j, s: (0, j)), ANY] + [ANY] * len(deps),
        out_specs=[pl.BlockSpec((t, MIX_COLS), lambda j, s: (0, s * nb + j)),
                   pl.BlockSpec((3, MIX_COLS), lambda j, s: (0, j)), pl.BlockSpec((1, MIX_COLS), lambda j, s: (0, j))],
        out_shape=[jax.ShapeDtypeStruct(dproj.shape, BF16), jax.ShapeDtypeStruct((3, n_conv), F32),
                   jax.ShapeDtypeStruct((1, n_conv), F32)],
        scratch_shapes=[pltpu.VMEM((4, t, MIX_COLS), BF16)],
        input_output_aliases={7: 0},
        compiler_params=_cp(("arbitrary", "arbitrary")),
    )(dz, dp, proj, proj, proj, cw, cb, dproj, *deps)


def _merge_fwd(name, proj, bg, ya, yb, ps):
    t, d = ya.shape
    tm = _rows(t)

    def body(gab_ref, bg_ref, ya_ref, yb_ref, ps_ref, o_ref):
        gab = gab_ref[...].astype(F32) + bg_ref[...]
        sa, sb = jax.nn.sigmoid(gab[:, :d]), jax.nn.sigmoid(gab[:, d:])
        o_ref[...] = (sa * ya_ref[...].astype(F32) + sb * (yb_ref[...].astype(F32) * ps_ref[...])).astype(BF16)

    row = pl.BlockSpec((tm, d), lambda i: (i, 0))
    return pl.pallas_call(
        body, name=name, grid=(t // tm,),
        in_specs=[pl.BlockSpec((tm, 2 * d), lambda i: (i, 1)), pl.BlockSpec((1, 2 * d), lambda i: (0, 0)), row, row,
                  pl.BlockSpec((1, d), lambda i: (0, 0))],
        out_specs=row, out_shape=jax.ShapeDtypeStruct((t, d), BF16), compiler_params=_cp(("parallel",)),
    )(proj, bg, ya, yb, ps)


def _merge_bwd(name, dm, proj, bg, ya, yb, ps, deps=()):
    t, d = ya.shape
    tm = _rows(t)

    def body(dm_ref, gab_ref, bg_ref, ya_ref, yb_ref, ps_ref, *rest):
        dya_ref, dyb_ref, dg_ref, dba_ref, dbb_ref, dps_ref = rest[len(deps):]
        gab = gab_ref[...].astype(F32) + bg_ref[...]
        sa, sb = jax.nn.sigmoid(gab[:, :d]), jax.nn.sigmoid(gab[:, d:])
        dmf = dm_ref[...].astype(F32)
        ybf, ps_ = yb_ref[...].astype(F32), ps_ref[...]
        dya_ref[...] = (dmf * sa).astype(BF16)
        dyb = dmf * sb
        dyb_ref[...] = (dyb * ps_).astype(BF16)
        dga = dmf * ya_ref[...].astype(F32) * sa * (1.0 - sa)
        dgb = dmf * (ybf * ps_) * sb * (1.0 - sb)
        dg_ref[:, :d] = dga.astype(BF16)
        dg_ref[:, d:] = dgb.astype(BF16)

        @pl.when(pl.program_id(0) == 0)
        def _():
            dba_ref[...] = jnp.zeros_like(dba_ref)
            dbb_ref[...] = jnp.zeros_like(dbb_ref)
            dps_ref[...] = jnp.zeros_like(dps_ref)

        dba_ref[...] += jnp.sum(dga, axis=0, keepdims=True)
        dbb_ref[...] += jnp.sum(dgb, axis=0, keepdims=True)
        dps_ref[...] += jnp.sum(dyb * ybf, axis=0, keepdims=True)

    row = pl.BlockSpec((tm, d), lambda i: (i, 0))
    vec = pl.BlockSpec((1, d), lambda i: (0, 0))
    gates = pl.BlockSpec((tm, 2 * d), lambda i: (i, 1))
    return pl.pallas_call(
        body, name=name, grid=(t // tm,),
        in_specs=[row, gates, pl.BlockSpec((1, 2 * d), lambda i: (0, 0)), row, row, vec] + [ANY] * len(deps),
        out_specs=[row, row, gates, vec, vec, vec],
        out_shape=[jax.ShapeDtypeStruct((t, d), BF16), jax.ShapeDtypeStruct((t, d), BF16),
                   jax.ShapeDtypeStruct(proj.shape, BF16), jax.ShapeDtypeStruct((1, d), F32),
                   jax.ShapeDtypeStruct((1, d), F32), jax.ShapeDtypeStruct((1, d), F32)],
        compiler_params=_cp(("arbitrary",)),
    )(dm, proj, bg, ya, yb, ps, *deps)


def _ffn_up_act(name, h, w_up, gate, part=None, prev=None, deps=()):
    t, d = h.shape
    f = w_up.shape[1]
    tm, tf = _tile(t, 1024), _tile(f, 512)
    j0, j1 = _tile_span(f // tf, part)
    n_prev = 0 if prev is None else 2
    extra = ([] if prev is None else list(prev)) + list(deps)

    def body(h_ref, w_ref, g_ref, *rest):
        u_ref, a_ref = rest[len(extra):]
        u = lax.dot_general(h_ref[...], w_ref[...], _DIMS["nn"], preferred_element_type=F32)
        g = g_ref[...].astype(F32)
        u_ref[...] = u.astype(BF16)
        a_ref[...] = (g * jax.nn.sigmoid(g) * u).astype(BF16)

    blk = pl.BlockSpec((tm, tf), lambda i, j: (i, j0 + j))
    shp = jax.ShapeDtypeStruct((t, f), BF16)
    return pl.pallas_call(
        body, name=name, grid=(t // tm, j1 - j0),
        in_specs=[pl.BlockSpec((tm, d), lambda i, j: (i, 0)), pl.BlockSpec((d, tf), lambda i, j: (0, j0 + j)), blk]
        + [ANY] * len(extra),
        out_specs=[blk, blk], out_shape=[shp, shp], input_output_aliases={3 + i: i for i in range(n_prev)},
        compiler_params=_cp(("parallel", "parallel")))(h, w_up, gate, *extra)


def _ffn_bwd(name, dy, w_down, gate, up):
    t, d = dy.shape
    f = w_down.shape[0]
    tm, tf = _tile(t, 1024), _tile(f, 512)

    def body(dy_ref, w_ref, g_ref, u_ref, dg_ref, du_ref):
        da = lax.dot_general(dy_ref[...], w_ref[...], _DIMS["nt"], preferred_element_type=F32)
        g = g_ref[...].astype(F32)
        s = jax.nn.sigmoid(g)
        du_ref[...] = (da * (g * s)).astype(BF16)
        dg_ref[...] = (da * u_ref[...].astype(F32) * (s * (1.0 + g * (1.0 - s)))).astype(BF16)

    blk = pl.BlockSpec((tm, tf), lambda i, j: (i, j))
    shp = jax.ShapeDtypeStruct((t, f), BF16)
    return pl.pallas_call(
        body, name=name, grid=(t // tm, f // tf),
        in_specs=[pl.BlockSpec((tm, d), lambda i, j: (i, 0)), pl.BlockSpec((tf, d), lambda i, j: (j, 0)), blk, blk],
        out_specs=[blk, blk], out_shape=[shp, shp], compiler_params=_cp(("parallel", "parallel")))(dy, w_down, gate, up)


def _adamw_math(w, g, m, v):
    m = ADAM_B1 * m + (1.0 - ADAM_B1) * g
    v = ADAM_B2 * v + (1.0 - ADAM_B2) * (g * g)
    m_hat = m / (1.0 - ADAM_B1 ** ADAM_STEP)
    v_hat = v / (1.0 - ADAM_B2 ** ADAM_STEP)
    delta = -ADAM_LR * (m_hat / (jnp.sqrt(v_hat) + ADAM_EPS) + ADAM_WD * w)
    return delta, m, v


def _adamw(name, w, g, m, v):
    r, c = w.shape
    tr = _tile8(r, 512 if c <= 1024 else 256)

    def body(w_ref, g_ref, m_ref, v_ref, go_ref, d_ref, nm_ref, nv_ref):
        g = g_ref[...]
        go_ref[...] = g
        d_ref[...], nm_ref[...], nv_ref[...] = _adamw_math(w_ref[...], g, m_ref[...], v_ref[...])

    blk = pl.BlockSpec((tr, c), lambda i: (i, 0))
    shp = jax.ShapeDtypeStruct((r, c), F32)
    return pl.pallas_call(body, name=name, grid=(r // tr,), in_specs=[blk] * 4, out_specs=[blk] * 4,
                          out_shape=[shp] * 4, compiler_params=_cp(("parallel",)))(w, g, m, v)


class _Weight:
    def __init__(self, name, rows, cols, colshard):
        self.name, self.colshard = name, colshard
        self.R, self.nn = rows // 2, cols
        self.P = 1 if colshard else N_CHIPS
        self.N = N_CHIPS * cols if colshard else cols

    def cols(self, k):
        return pl.ds(pl.multiple_of(k * self.nn, LANES), self.nn)

    def shard(self, ref, k):
        return ref.at[0, :, :, self.cols(k)] if self.colshard else ref.at[k]

    def half(self, ref, k, h):
        return ref.at[0, h, :, self.cols(k)] if self.colshard else ref.at[k, h]

    def quarter(self, ref, k, h, q):
        return self.half(ref, k, h).at[pl.ds(q * (self.R // 2), self.R // 2), :]

    def part(self, ref, k):
        return ref.at[0, :, self.cols(k)] if self.colshard else ref.at[k]


def _remote(src, dst, ssem, rsem, dev):
    return pltpu.make_async_remote_copy(src_ref=src, dst_ref=dst, send_sem=ssem, recv_sem=rsem, device_id=dev,
                                        device_id_type=MESH)


def _other_chips(x, y):
    chips = [(1 - x, y), (x, 1 - y), (1 - x, 1 - y)]
    return chips, [2 * cx + cy for cx, cy in chips]


def _hbm(a):
    return pltpu.with_memory_space_constraint(a, pltpu.HBM)


def _gather_start(name, groups, lands, after=()):
    flat = [w for grp in groups for w in grp]
    nw, ng = len(flat), len(groups)

    def body(*refs):
        land = refs[:nw]
        sems = refs[nw + len(after):nw + len(after) + 2 * ng]
        token = refs[2 * nw + len(after) + 2 * ng]
        x, y, c = _mesh_pos()
        k_me = 2 * x + y
        chips, _ = _other_chips(x, y)
        i = 0
        for g, grp in enumerate(groups):
            for wi, w in enumerate(grp):
                mine = w.half(land[i], k_me, c)
                for j, chip in enumerate(chips):
                    _remote(mine, mine, sems[2 * g].at[3 * wi + j], sems[2 * g + 1].at[3 * wi + j], (*chip, c)).start()
                i += 1
        token[...] = jnp.zeros_like(token)

    sem_shapes = []
    for grp in groups:
        sem_shapes += [pltpu.SemaphoreType.DMA((3 * len(grp),))] * 2
    out = pl.pallas_call(
        body, name=name, in_specs=[HBM] * nw + [ANY] * len(after),
        out_specs=[SEM] * (2 * ng) + [HBM] * nw + [VMEM],
        out_shape=sem_shapes + [pltpu.HBM(a.shape, a.dtype) for a in lands] + [jax.ShapeDtypeStruct((8, LANES), F32)],
        input_output_aliases={i: 2 * ng + i for i in range(nw)},
        compiler_params=pltpu.CompilerParams(has_side_effects=EFFECT),
    )(*[_hbm(a) for a in lands], *after)
    sems = [(out[2 * g], out[2 * g + 1]) for g in range(ng)]
    return sems, list(out[2 * ng:2 * ng + nw]), out[-1]


def _gather_wait(name, grp, lands, ssem, rsem, after):
    n = len(grp)

    def body(*refs):
        land, ssem_ref, rsem_ref = refs[:n], refs[n], refs[n + 1]
        x, y, c = _mesh_pos()
        k_me = 2 * x + y
        chips, ks = _other_chips(x, y)
        for wi, w in enumerate(grp):
            for j, chip in enumerate(chips):
                cp = _remote(w.half(land[wi], k_me, c), w.half(land[wi], ks[j], c), ssem_ref.at[3 * wi + j],
                             rsem_ref.at[3 * wi + j], (*chip, c))
                cp.wait_send()
                cp.wait_recv()

    return pl.pallas_call(
        body, name=name, in_specs=[HBM] * n + [SEM, SEM, ANY], out_specs=[HBM] * n,
        out_shape=[pltpu.HBM(a.shape, a.dtype) for a in lands], input_output_aliases={i: i for i in range(n)},
        compiler_params=pltpu.CompilerParams(has_side_effects=EFFECT),
    )(*lands, ssem, rsem, after)


def _split_start(name, arrays, n, copies, after=()):
    na = len(arrays)

    def body(*refs):
        ssem, rsem, token = refs[na + len(after):][0], refs[na + len(after):][1], refs[2 * na + len(after) + 2]
        for i, (src, dst, dev, _) in enumerate(copies(refs[:na], *_mesh_pos())):
            _remote(src, dst, ssem.at[i], rsem.at[i], dev).start()
        token[...] = jnp.zeros_like(token)

    out = pl.pallas_call(
        body, name=name, in_specs=[HBM] * na + [ANY] * len(after), out_specs=[SEM, SEM] + [HBM] * na + [VMEM],
        out_shape=[pltpu.SemaphoreType.DMA((n,))] * 2 + [pltpu.HBM(a.shape, a.dtype) for a in arrays]
        + [jax.ShapeDtypeStruct((8, LANES), F32)],
        input_output_aliases={i: 2 + i for i in range(na)},
        compiler_params=pltpu.CompilerParams(has_side_effects=EFFECT),
    )(*[_hbm(a) for a in arrays], *after)
    return out[0], out[1], list(out[2:2 + na]), out[-1]


def _split_wait(name, arrays, ssem, rsem, copies, after):
    na = len(arrays)

    def body(*refs):
        for i, (src, _, dev, dst) in enumerate(copies(refs[:na], *_mesh_pos())):
            cp = _remote(src, dst, refs[na].at[i], refs[na + 1].at[i], dev)
            cp.wait_send()
            cp.wait_recv()

    return list(pl.pallas_call(
        body, name=name, in_specs=[HBM] * na + [SEM, SEM] + [ANY] * len(after), out_specs=[HBM] * na,
        out_shape=[pltpu.HBM(a.shape, a.dtype) for a in arrays], input_output_aliases={i: i for i in range(na)},
        compiler_params=pltpu.CompilerParams(has_side_effects=EFFECT),
    )(*arrays, ssem, rsem, *after))


def _pass_copies(grp, rels=(0, 1, 2)):
    def copies(land, x, y, c):
        _, ks = _other_chips(x, y)
        return [(w.half(land[wi], ks[j], c), w.half(land[wi], ks[j], c), (x, y, 1 - c), w.half(land[wi], ks[j], 1 - c))
                for wi, w in enumerate(grp) for j in rels]
    copies.n = len(grp) * len(rels)
    return copies


def _near_copies(grp):
    def copies(land, x, y, c):
        chips, ks = _other_chips(x, y)
        out = []
        for wi, w in enumerate(grp):
            mine = w.half(land[wi], 2 * x + y, c)
            out += [(mine, mine, (*chips[j], c), w.half(land[wi], ks[j], c)) for j in (0, 1)]
        return out
    copies.n = 2 * len(grp)
    return copies


def _far_copies(grp):
    def copies(land, x, y, c):
        chips, ks = _other_chips(x, y)
        out = []
        for wi, w in enumerate(grp):
            for j in (0, 1):
                q = w.quarter(land[wi], ks[j], c, j)
                out.append((q, q, (*chips[1 - j], c), w.quarter(land[wi], ks[2], c, j)))
        return out
    copies.n = 2 * len(grp)
    return copies


def _pair_copies(n, whole=False):
    def copies(refs, x, y, c):
        return [(refs[i] if whole else refs[i].at[:, 1 - c], refs[n + i], (x, y, 1 - c), refs[n + i]) for i in range(n)]
    return copies


def _share_copies(n):
    def copies(refs, x, y, c):
        return [(refs[i].at[c], refs[i].at[c], (x, y, 1 - c), refs[i].at[1 - c]) for i in range(n)]
    return copies


def _gather_conv_w(cw):
    ncw = cw.shape[1]

    def body(cw_ref, out_ref, ssem, rsem):
        x, y, c = _mesh_pos()
        k_me = 2 * x + y
        chips, ks = _other_chips(x, y)
        cols = lambda k: out_ref.at[:, pl.ds(pl.multiple_of(k * ncw, LANES), ncw)]
        cps = [_remote(cw_ref, cols(k_me), ssem.at[j], rsem.at[j], (*chip, c)) for j, chip in enumerate(chips)]
        for cp in cps:
            cp.start()
        for k in range(N_CHIPS):
            @pl.when(k_me == k)
            def _():
                out_ref[:, k * ncw:(k + 1) * ncw] = cw_ref[...]
        for j in range(3):
            _remote(cw_ref, cols(ks[j]), ssem.at[j], rsem.at[j], (*chips[j], c)).wait_recv()
        for cp in cps:
            cp.wait_send()

    return pl.pallas_call(
        body, name="gather_conv_w", in_specs=[VMEM], out_specs=VMEM,
        out_shape=jax.ShapeDtypeStruct((3, N_CHIPS * ncw), F32),
        scratch_shapes=[pltpu.SemaphoreType.DMA((3,)), pltpu.SemaphoreType.DMA((3,))],
    )(cw)


def _grad_tiles(w, n):
    return _tile8(w.R, 512) if w.R <= 512 else w.R // 2, _tile(n, 2048)


def _pair_sum(name, w, pos, grad, got):
    tr, tn = _grad_tiles(w, w.N)

    def body(pos_ref, g_ref, r_ref, o_ref):
        o_ref[...] = (g_ref[...].astype(F32) + r_ref[...].astype(F32)).astype(BF16)

    blk = pl.BlockSpec((None, tr, tn), lambda p, i, j, pos: (p, i, j))
    grid_spec = pltpu.PrefetchScalarGridSpec(
        num_scalar_prefetch=1, grid=(w.P, w.R // tr, w.N // tn),
        in_specs=[pl.BlockSpec((None, None, tr, tn), lambda p, i, j, pos: (p, pos[0], i, j)), blk], out_specs=blk)
    return pl.pallas_call(body, name=name, grid_spec=grid_spec, out_shape=jax.ShapeDtypeStruct((w.P, w.R, w.N), BF16),
                          compiler_params=_cp(("parallel",) * 3))(pos, grad, got)


def _scatter_start(name, ws, pairs):
    nw = len(ws)

    def body(*refs):
        pr, land = refs[:nw], refs[nw:2 * nw]
        ssem, rsem = refs[2 * nw], refs[2 * nw + 1]
        token = refs[4 * nw + 2]
        x, y, c = _mesh_pos()
        chips, ks = _other_chips(x, y)
        for i, w in enumerate(ws):
            for j, chip in enumerate(chips):
                _remote(w.part(pr[i], ks[j]), land[i].at[j], ssem.at[3 * i + j], rsem.at[3 * i + j], (*chip, c)).start()
        token[...] = jnp.zeros_like(token)

    lands = [lax.empty((3, w.R, w.nn), BF16) for w in ws]
    out = pl.pallas_call(
        body, name=name, in_specs=[HBM] * (2 * nw),
        out_specs=[SEM, SEM] + [HBM] * (2 * nw) + [VMEM],
        out_shape=[pltpu.SemaphoreType.DMA((3 * nw,))] * 2 + [pltpu.HBM(a.shape, a.dtype) for a in pairs + lands]
        + [jax.ShapeDtypeStruct((8, LANES), F32)],
        input_output_aliases={i: 2 + i for i in range(2 * nw)},
        compiler_params=pltpu.CompilerParams(has_side_effects=EFFECT),
    )(*[_hbm(a) for a in pairs + lands])
    return out[0], out[1], list(out[2:2 + nw]), list(out[2 + nw:2 + 2 * nw]), out[-1]


def _scatter_wait(name, ws, pairs, lands, ssem, rsem, after):
    nw = len(ws)

    def body(*refs):
        pr, land = refs[:nw], refs[nw:2 * nw]
        ssem_ref, rsem_ref = refs[2 * nw], refs[2 * nw + 1]
        x, y, c = _mesh_pos()
        chips, ks = _other_chips(x, y)
        for i, w in enumerate(ws):
            for j, chip in enumerate(chips):
                cp = _remote(w.part(pr[i], ks[j]), land[i].at[j], ssem_ref.at[3 * i + j], rsem_ref.at[3 * i + j], (*chip, c))
                cp.wait_send()
                cp.wait_recv()

    out = pl.pallas_call(
        body, name=name, in_specs=[HBM] * (2 * nw) + [SEM, SEM] + [ANY] * len(after), out_specs=[HBM] * (2 * nw),
        out_shape=[pltpu.HBM(a.shape, a.dtype) for a in pairs + lands],
        input_output_aliases={i: i for i in range(2 * nw)},
        compiler_params=pltpu.CompilerParams(has_side_effects=EFFECT),
    )(*pairs, *lands, ssem, rsem, *after)
    return list(out[:nw]), list(out[nw:])


def _final_sum(name, w, pos, grad, got, parts):
    tr, tn = _grad_tiles(w, w.nn)
    nbc = w.nn // tn
    if got is None:
        return _final_sum_pair(name, w, pos, grad, parts, tr, tn)

    def body(pos_ref, g_ref, r_ref, p_ref, o_ref):
        acc = g_ref[...].astype(F32) + r_ref[...].astype(F32)
        for j in range(3):
            acc = acc + p_ref[j].astype(F32)
        o_ref[...] = acc

    if w.colshard:
        g_spec = pl.BlockSpec((None, None, tr, tn), lambda i, j, pos: (0, pos[0], i, pos[1] * nbc + j))
        r_spec = pl.BlockSpec((None, tr, tn), lambda i, j, pos: (0, i, pos[1] * nbc + j))
    else:
        g_spec = pl.BlockSpec((None, None, tr, tn), lambda i, j, pos: (pos[1], pos[0], i, j))
        r_spec = pl.BlockSpec((None, tr, tn), lambda i, j, pos: (pos[1], i, j))
    grid_spec = pltpu.PrefetchScalarGridSpec(
        num_scalar_prefetch=1, grid=(w.R // tr, nbc),
        in_specs=[g_spec, r_spec, pl.BlockSpec((3, tr, tn), lambda i, j, pos: (0, i, j))],
        out_specs=pl.BlockSpec((None, tr, tn), lambda i, j, pos: (pos[0], i, j)))
    return pl.pallas_call(body, name=name, grid_spec=grid_spec, out_shape=jax.ShapeDtypeStruct((2, w.R, w.nn), F32),
                          compiler_params=_cp(("parallel",) * 2))(pos, grad, got, parts)


def _final_sum_pair(name, w, pos, pair, parts, tr, tn):
    nbc = w.nn // tn

    def body(pos_ref, g_ref, p_ref, o_ref):
        acc = g_ref[...].astype(F32)
        for j in range(3):
            acc = acc + p_ref[j].astype(F32)
        o_ref[...] = acc

    if w.colshard:
        g_spec = pl.BlockSpec((None, tr, tn), lambda i, j, pos: (0, i, pos[1] * nbc + j))
    else:
        g_spec = pl.BlockSpec((None, tr, tn), lambda i, j, pos: (pos[1], i, j))
    grid_spec = pltpu.PrefetchScalarGridSpec(
        num_scalar_prefetch=1, grid=(w.R // tr, nbc),
        in_specs=[g_spec, pl.BlockSpec((3, tr, tn), lambda i, j, pos: (0, i, j))],
        out_specs=pl.BlockSpec((None, tr, tn), lambda i, j, pos: (pos[0], i, j)))
    return pl.pallas_call(body, name=name, grid_spec=grid_spec, out_shape=jax.ShapeDtypeStruct((2, w.R, w.nn), F32),
                          compiler_params=_cp(("parallel",) * 2))(pos, pair, parts)


VEC_ROWS = 16


def _vector_step(d, n_conv, parts, params, deps=()):
    ncw = params[2][0].shape[1]
    n_par = len(params)

    def body(*refs):
        dg1, dba, dbb, dcw, dcb, dps, dg2, dgf, lc = refs[:9]
        wmv = refs[9:9 + 3 * n_par]
        refs = refs[9 + 3 * n_par + len(deps):]
        outs = refs[:4 * n_par]
        loss_ref = refs[4 * n_par]
        snd, got, ssem, rsem = refs[4 * n_par + 1:]
        x, y, c = _mesh_pos()
        me = 4 * x + 2 * y + c
        snd[...] = jnp.zeros_like(snd)
        for row, ref in ((0, dg1), (1, dba), (2, dbb), (3, dps), (4, dg2), (5, dgf), (6, lc)):
            snd[row:row + 1, :] = ref[...]
        snd[7:8, :n_conv] = dcb[...]
        snd[8:11, :n_conv] = dcw[...]
        cps = []
        for r in range(1, N_DEV):
            peer = tuple(1 - p if (r >> b) & 1 else p for p, b in ((x, 2), (y, 1), (c, 0)))
            cps.append(_remote(snd, got.at[me], ssem.at[r - 1], rsem.at[r - 1], peer))
        for cp in cps:
            cp.start()
        got[me] = snd[...]
        for r in range(1, N_DEV):
            peer = tuple(1 - p if (r >> b) & 1 else p for p, b in ((x, 2), (y, 1), (c, 0)))
            _remote(snd, got.at[4 * peer[0] + 2 * peer[1] + peer[2]], ssem.at[r - 1], rsem.at[r - 1], peer).wait_recv()
        for cp in cps:
            cp.wait_send()
        tot = got[0]
        for dev in range(1, N_DEV):
            tot = tot + got[dev]
        loss_ref[...] = jnp.sum(tot[6:7, :], axis=1, keepdims=True)
        k_me = 2 * x + y
        g_cw = jnp.zeros((3, ncw), F32)
        for k in range(N_CHIPS):
            g_cw = g_cw + jnp.where(k_me == k, tot[8:11, k * ncw:(k + 1) * ncw], 0.0)
        grads = [tot[0:1, :], jnp.concatenate([tot[1:2, :], tot[2:3, :]], axis=1), g_cw, tot[7:8, :n_conv],
                 tot[3:4, :], tot[4:5, :], tot[5:6, :]]
        for i, g in enumerate(grads):
            w_ref, m_ref, v_ref = wmv[3 * i:3 * i + 3]
            delta, nm, nv = _adamw_math(w_ref[...], g, m_ref[...], v_ref[...])
            outs[4 * i][...] = g
            outs[4 * i + 1][...] = delta
            outs[4 * i + 2][...] = nm
            outs[4 * i + 3][...] = nv

    args = list(parts)
    out_shape = []
    for w, m, v in params:
        args += [w, m, v]
        out_shape += [jax.ShapeDtypeStruct(w.shape, F32)] * 4
    out_shape.append(jax.ShapeDtypeStruct((1, 1), F32))
    return pl.pallas_call(
        body, name="vector_params_step", in_specs=[VMEM] * len(args) + [ANY] * len(deps),
        out_specs=[VMEM] * len(out_shape), out_shape=out_shape,
        scratch_shapes=[pltpu.VMEM((VEC_ROWS, d), F32), pltpu.VMEM((N_DEV, VEC_ROWS, d), F32),
                        pltpu.SemaphoreType.DMA((N_DEV - 1,)), pltpu.SemaphoreType.DMA((N_DEV - 1,))],
        compiler_params=pltpu.CompilerParams(vmem_limit_bytes=VMEM_LIMIT),
    )(*args, *deps)


def kernel(x, norm1_g, w_in, b_gate, conv_w, conv_b, w_a_out, w_pool, pool_scale, w_o, norm2_g, w_ffn_gate, w_ffn_up, w_ffn_down, final_g, loss_target, m_norm1_g, m_w_in, m_b_gate, m_conv_w, m_conv_b, m_w_a_out, m_w_pool, m_pool_scale, m_w_o, m_norm2_g, m_w_ffn_gate, m_w_ffn_up, m_w_ffn_down, m_final_g, v_norm1_g, v_w_in, v_b_gate, v_conv_w, v_conv_b, v_w_a_out, v_w_pool, v_pool_scale, v_w_o, v_norm2_g, v_w_ffn_gate, v_w_ffn_up, v_w_ffn_down, v_final_g):
    t, d = x.shape[1], x.shape[2]
    n_conv = conv_b.shape[1]
    n_groups, pool_cg, pool_dg = w_pool.shape[1], w_pool.shape[2], N_CHIPS * w_pool.shape[3]
    d_ff = N_CHIPS * w_ffn_gate.shape[2]
    assert n_conv // n_groups == pool_cg and n_conv % (n_groups * MIX_COLS) == 0 and n_groups == len(POOL_WINDOWS)

    big = {"w_in": (w_in, m_w_in, v_w_in), "w_a_out": (w_a_out, m_w_a_out, v_w_a_out), "w_pool": (w_pool, m_w_pool, v_w_pool),
           "w_o": (w_o, m_w_o, v_w_o), "w_ffn_gate": (w_ffn_gate, m_w_ffn_gate, v_w_ffn_gate),
           "w_ffn_up": (w_ffn_up, m_w_ffn_up, v_w_ffn_up), "w_ffn_down": (w_ffn_down, m_w_ffn_down, v_w_ffn_down)}
    colshard = {"w_in": True, "w_a_out": True, "w_pool": True, "w_o": False, "w_ffn_gate": True, "w_ffn_up": True,
                "w_ffn_down": False}
    names = list(big)
    shard2d = {n: big[n][0].reshape(-1, big[n][0].shape[-1]) for n in names}
    ws = [_Weight(n, *shard2d[n].shape, colshard[n]) for n in names]

    xs, tgt = x[0], loss_target[0]
    cw_loc = conv_w[0]
    pos = jnp.stack([lax.axis_index("c"), 2 * lax.axis_index("x") + lax.axis_index("y")]).astype(jnp.int32)
    by_name = {w.name: w for w in ws}
    groups = [[by_name[n] for n in g] for g in (["w_in"], ["w_a_out", "w_pool", "w_o"], ["w_ffn_gate"], ["w_ffn_up"],
                                                 ["w_ffn_down"])]
    first = [sum(len(g) for g in groups[:i]) for i in range(len(groups))]
    rgroups = [groups[0], groups[1], groups[2] + groups[3], groups[4]]

    cw_full = _gather_conv_w(cw_loc)
    cast = lambda w, dep: _cast_place(f"cast_{w.name}", w, pos, shard2d[w.name].reshape(2, w.R, w.nn), deps=[dep])
    chips, ks = _other_chips(lax.axis_index("x"), lax.axis_index("y"))
    kvec = jnp.stack([pos[1], *ks]).astype(jnp.int32)
    full = {}

    def start(name, arrays, copies, after=()):
        ssem, rsem, arrays, token = _split_start(name, arrays, copies.n, copies, after)
        return name, arrays, ssem, rsem, copies, token

    def wait(started, after):
        name, arrays, ssem, rsem, copies, _ = started
        return _split_wait(name + "_wait", arrays, ssem, rsem, copies, after)

    def pass_on(g, got, after=()):
        return start(f"pass_{g}", got, _pass_copies(groups[g]), after)

    def passed(g, st, after=None):
        got = wait(st, [st[5]] if after is None else after)
        full.update({w.name: a.reshape(w.P * 2 * w.R, w.N) for w, a in zip(groups[g], got)})

    near = start("near_0", [cast(w, cw_full) for w in groups[0]], _near_copies(groups[0]))
    rest = [cast(w, near[5]) for grp in groups[1:] for w in grp]
    h1 = _rms_fwd("norm1_fwd", xs, norm1_g, deps=[near[5]])
    proj = _proj_piece("proj_own", h1, shard2d["w_in"], None, kvec, 0, 1, deps=rest)
    got = wait(near, [proj])
    far = start("far_0", got, _far_copies(groups[0]))
    sems_b, lands_b, tok_b = _gather_start("gather_start_b", groups[1:2], rest[:3], after=[far[5]])
    st = start("pass_near_0", far[1], _pass_copies(groups[0], (0, 1)), [tok_b])
    got = wait(st, [st[5]])
    proj = _proj_piece("proj_near", h1, got[0].reshape(-1, groups[0][0].N), proj, kvec, 1, 2)
    st = start("pass_far_0", wait((far[0], got) + far[2:], [proj]), _pass_copies(groups[0], (2,)))
    got = wait(st, [st[5]])
    w_in_full = got[0].reshape(-1, groups[0][0].N)
    proj = _proj_piece("proj_far", h1, w_in_full, proj, kvec, 3, 1)
    got = _gather_wait("gather_wait_1", groups[1], lands_b, *sems_b[0], proj)
    near_g = start("near_2", rest[3:4], _near_copies(groups[2]), got)
    st = pass_on(1, got, [near_g[5]])
    z, p = _mixer_fwd("mixer_fwd", proj, cw_full, conv_b, n_conv, n_groups, deps=[st[5]])
    passed(1, st, [z])
    wp_full = full["w_pool"].reshape(n_groups, pool_cg, pool_dg)
    ya = _mm_nn("conv_out", z, full["w_a_out"], BF16)
    yb = _gmm_nn("pool_out", p, wp_full, BF16)
    merged = _merge_fwd("merge_fwd", proj, b_gate, ya, yb, pool_scale)
    far_g = start("far_2", wait(near_g, [merged]), _far_copies(groups[2]))
    near_u = start("near_3", rest[4:5], _near_copies(groups[3]), [far_g[5]])
    x2 = _mm_nn("mix_out", merged, full["w_o"], F32, add=xs, deps=[near_u[5]])
    st = pass_on(2, wait(far_g, [x2]))
    h2 = _rms_fwd("norm2_fwd", x2, norm2_g, deps=[st[5]])
    passed(2, st, [h2])
    gate = _mm_nn("ffn_gate_a", h2, full["w_ffn_gate"], BF16, part=(0, 2))
    far_u = start("far_3", wait(near_u, [gate]), _far_copies(groups[3]))
    near_d = start("near_4", rest[5:6], _near_copies(groups[4]), [far_u[5]])
    gate = _mm_nn("ffn_gate_b", h2, full["w_ffn_gate"], BF16, part=(1, 2), prev=gate, deps=[near_d[5]])
    passed(3, pass_on(3, wait(far_u, [gate])))
    up_act = _ffn_up_act("ffn_up_act_a", h2, full["w_ffn_up"], gate, part=(0, 2))
    far_d = start("far_4", wait(near_d, [up_act[0]]), _far_copies(groups[4]))
    up, act = _ffn_up_act("ffn_up_act_b", h2, full["w_ffn_up"], gate, part=(1, 2), prev=up_act, deps=[far_d[5]])
    passed(4, pass_on(4, wait(far_d, [act])))
    x3 = _mm_nn("ffn_down", act, full["w_ffn_down"], F32, add=x2, tiles=(512, 512))

    pending = {}

    def pair_start(g, grads):
        grp = rgroups[g]
        gcan = [grads[w.name].reshape(w.P, 2, w.R, w.N) for w in grp]
        slots = [lax.empty((w.P, w.R, w.N), BF16) for w in grp]
        pending[g] = _split_start(f"pair_start_{g}", gcan + slots, len(grp), _pair_copies(len(grp)))
        return pending[g][3]

    def scatter_start(g, after):
        grp = rgroups[g]
        n = len(grp)
        ssem, rsem, arrs, _ = pending[g]
        arrs = _split_wait(f"pair_wait_{g}", arrs, ssem, rsem, _pair_copies(n), after)
        gcan, sib = arrs[:n], arrs[n:]
        pairs = [_pair_sum(f"pair_sum_{w.name}", w, pos, a, s) for w, a, s in zip(grp, gcan, sib)]
        ssem, rsem, pairs, slots, token = _scatter_start(f"scatter_start_{g}", grp, pairs)
        pending[g] = (gcan, sib, pairs, slots, ssem, rsem)
        return token

    def pair_start_halves(g, ab, deps):
        grp = rgroups[g]
        sent = [_mm_tn_half(f"d{w.name}_sib", a, b, pos, False, deps=deps if i == 0 else ()) for i, (w, (a, b)) in enumerate(zip(grp, ab))]
        slots = [lax.empty((1, w.R, w.N), BF16) for w in grp]
        pending[g] = _split_start(f"pair_start_{g}", sent + slots, len(grp), _pair_copies(len(grp), whole=True))
        return pending[g][3]

    def scatter_start_halves(g, ab, after):
        grp = rgroups[g]
        n = len(grp)
        ssem, rsem, arrs, _ = pending[g]
        arrs = _split_wait(f"pair_wait_{g}", arrs, ssem, rsem, _pair_copies(n, whole=True), after)
        pairs = [_mm_tn_half(f"d{w.name}_own", a, b, pos, True, add=s) for w, (a, b), s in zip(grp, ab, arrs[n:])]
        ssem, rsem, pairs, slots, token = _scatter_start(f"scatter_start_{g}", grp, pairs)
        pending[g] = (None, None, pairs, slots, ssem, rsem)
        return token

    def reduce_finish(g, after):
        grp = rgroups[g]
        gcan, sib, pairs, slots, ssem, rsem = pending[g]
        pairs, parts = _scatter_wait(f"scatter_wait_{g}", grp, pairs, slots, ssem, rsem, after)
        if gcan is None:
            return [_final_sum(f"final_sum_{w.name}", w, pos, a, None, q) for w, a, q in zip(grp, pairs, parts)]
        return [_final_sum(f"final_sum_{w.name}", w, pos, a, s, q) for w, a, s, q in zip(grp, gcan, sib, parts)]

    grads = {}
    dx3, dx3b, d_gf, loss_cols = _final_bwd("final_bwd", x3, final_g.reshape(1, d), tgt)
    dgate, dup = _ffn_bwd("ffn_bwd", dx3b, full["w_ffn_down"], gate, up)
    grads["w_ffn_down"] = _mm_tn("dw_ffn_down", act, dx3b, BF16)
    tok = pair_start(3, grads)
    dh2 = _mm_nt("d_h2", [(dgate, full["w_ffn_gate"]), (dup, full["w_ffn_up"])], BF16, tk=d_ff, deps=[tok],
                 tiles=(512, 256))
    tok = scatter_start(3, [dh2])
    tok = pair_start_halves(2, [(h2, dgate), (h2, dup)], [tok])
    dx2, dx2b, d_g2 = _rms_bwd("norm2_bwd", x2, norm2_g, dh2, dx3, True, deps=[tok])
    dmerged = _mm_nt("d_merged", [(dx2b, full["w_o"])], BF16, tk=d)
    grads["w_o"] = _mm_tn("dw_o", merged, dx2b, BF16)
    tok = scatter_start_halves(2, [(h2, dgate), (h2, dup)], [grads["w_o"]])
    dya, dyb, dproj, d_bga, d_bgb, d_ps = _merge_bwd("merge_bwd", dmerged, proj, b_gate, ya, yb, pool_scale, deps=[tok])
    dz = _mm_nt("d_z", [(dya, full["w_a_out"])], BF16, tk=d)
    grads["w_a_out"] = _mm_tn("dw_a_out", z, dya, BF16)
    dp = _gmm_nt("d_pool", dyb, wp_full, BF16)
    grads["w_pool"] = _gmm_tn("dw_pool", p, dyb, n_groups, BF16)
    tok = pair_start(1, grads)
    dproj, d_cw, d_cb = _mixer_bwd("mixer_bwd", dz, dp, proj, cw_full, conv_b, dproj, n_conv, n_groups, deps=[tok])
    tok = scatter_start(1, [dproj])
    tok = pair_start_halves(0, [(h1, dproj)], [tok])
    dh1 = _mm_nt("d_h1", [(dproj, w_in_full)], BF16, tk=proj.shape[1], deps=[tok], tiles=(512, 512))
    tok = scatter_start_halves(0, [(h1, dproj)], [dh1])
    grad_x, d_g1 = _rms_bwd("norm1_bwd", xs, norm1_g, dh1, dx2, False, deps=[tok])

    g_big, d_big, m_big, v_big = {}, {}, {}, {}

    def update(wsub, shared):
        out = []
        for w, g in zip(wsub, shared):
            wt, mt, vt = big[w.name]
            g2 = g.reshape(2 * w.R, w.nn)
            go, dl, nm, nv = _adamw(f"adamw_{w.name}", shard2d[w.name], g2, mt.reshape(g2.shape), vt.reshape(g2.shape))
            g_big[w.name], d_big[w.name], m_big[w.name], v_big[w.name] = (a.reshape(wt.shape) for a in (go, dl, nm, nv))
            out.append(nv)
        return out

    after = [grad_x]
    started = []
    for g in (3, 2, 1):
        halves = reduce_finish(g, after)
        share = _share_copies(len(halves))
        ssem, rsem, halves, tok = _split_start(f"share_start_{g}", halves, len(halves), share)
        started.append((g, ssem, rsem, halves, share))
        after = [tok]
    for g, ssem, rsem, halves, share in started:
        after = update(rgroups[g], _split_wait(f"share_wait_{g}", halves, ssem, rsem, share, after))
    share = _share_copies(1)
    ssem, rsem, halves, tok = _split_start("share_start_0", reduce_finish(0, after), 1, share)

    vec_names = ["norm1_g", "b_gate", "conv_w", "conv_b", "pool_scale", "norm2_g", "final_g"]
    vec = {"norm1_g": (norm1_g, m_norm1_g, v_norm1_g), "b_gate": (b_gate, m_b_gate, v_b_gate),
           "conv_w": (cw_loc, m_conv_w[0], v_conv_w[0]), "conv_b": (conv_b, m_conv_b, v_conv_b),
           "pool_scale": (pool_scale, m_pool_scale, v_pool_scale), "norm2_g": (norm2_g, m_norm2_g, v_norm2_g),
           "final_g": tuple(a.reshape(1, d) for a in (final_g, m_final_g, v_final_g))}
    vout = _vector_step(d, n_conv, [d_g1, d_bga, d_bgb, d_cw, d_cb, d_ps, d_g2, d_gf, loss_cols],
                        [vec[n] for n in vec_names], deps=halves)
    update(rgroups[0], _split_wait("share_wait_0", halves, ssem, rsem, share, []))

    shapes = {"conv_w": conv_w.shape, "final_g": final_g.shape}
    g_vec, d_vec, m_vec, v_vec = ({n: vout[4 * i + q].reshape(shapes.get(n, vec[n][0].shape)) for i, n in enumerate(vec_names)}
                                  for q in range(4))
    loss = vout[-1].reshape(())

    order = ["norm1_g", "w_in", "b_gate", "conv_w", "conv_b", "w_a_out", "w_pool", "pool_scale", "w_o", "norm2_g",
             "w_ffn_gate", "w_ffn_up", "w_ffn_down", "final_g"]
    pick = lambda vecs, bigs: [vecs[n] if n in vecs else bigs[n] for n in order]
    return (loss, grad_x.reshape(x.shape), *pick(g_vec, g_big), *pick(d_vec, d_big), *pick(m_vec, m_big),
            *pick(v_vec, v_big))
```

```python
import functools

import jax
import jax.numpy as jnp
from jax import lax
from jax.experimental import pallas as pl
from jax.experimental.pallas import tpu as pltpu

F32, BF16 = jnp.float32, jnp.bfloat16
MESH = pl.DeviceIdType.MESH
ANY = pl.BlockSpec(memory_space=pl.ANY)
VMEM = pl.BlockSpec(memory_space=pltpu.VMEM)
HBM = pl.BlockSpec(memory_space=pltpu.HBM)
SEM = pl.BlockSpec(memory_space=pltpu.SEMAPHORE)
EFFECT = pltpu.SideEffectType.DATAFLOW_SIDE_EFFECTING

EPS = 1e-6
POOL_WINDOWS = (2, 4, 8, 16)
ADAM_LR, ADAM_B1, ADAM_B2, ADAM_EPS, ADAM_WD, ADAM_STEP = 0.001, 0.9, 0.999, 1e-08, 0.01, 10

V7X_VMEM_BYTES = 64 * 1024 * 1024
VMEM_LIMIT = V7X_VMEM_BYTES * 3 // 4
LANES = 128
COL_TILE = 8 * LANES
N_CHIPS = 4
N_DEV = 8

_DIMS = {
    "nn": (((1,), (0,)), ((), ())),
    "nt": (((1,), (1,)), ((), ())),
    "tn": (((0,), (0,)), ((), ())),
}


def _cp(sem):
    return pltpu.CompilerParams(dimension_semantics=sem, vmem_limit_bytes=VMEM_LIMIT)


def _mesh_pos():
    return lax.axis_index("x"), lax.axis_index("y"), lax.axis_index("c")


def _mm(name, pairs, *, mode, grid, out_shape, o_spec, nk=1, kaxis=None, add=None, deps=(), prev=None):
    npair = len(pairs)
    has_add = add is not None

    def body(*refs):
        ab = refs[: 2 * npair]
        pos = 2 * npair
        add_ref = refs[pos] if has_add else None
        pos += int(has_add) + len(deps) + (prev is not None)
        o_ref = refs[pos]
        acc_ref = refs[pos + 1] if nk > 1 else None
        d = None
        for p in range(npair):
            t = lax.dot_general(ab[2 * p][...], ab[2 * p + 1][...], _DIMS[mode], preferred_element_type=F32)
            d = t if d is None else d + t
        if nk == 1:
            if has_add:
                d = d + add_ref[...].astype(F32)
            o_ref[...] = d.astype(o_ref.dtype)
        else:
            k = pl.program_id(kaxis)

            @pl.when(k == 0)
            def _():
                acc_ref[...] = d

            @pl.when(k > 0)
            def _():
                acc_ref[...] += d

            @pl.when(k == nk - 1)
            def _():
                r = acc_ref[...]
                if has_add:
                    r = r + add_ref[...].astype(F32)
                o_ref[...] = r.astype(o_ref.dtype)

    args, specs = [], []
    for a, a_spec, b, b_spec in pairs:
        args += [a, b]
        specs += [a_spec, b_spec]
    if has_add:
        args.append(add[0])
        specs.append(add[1])
    args += list(deps)
    specs += [ANY] * len(deps)
    aliases = {}
    if prev is not None:
        aliases = {len(args): 0}
        args.append(prev)
        specs.append(ANY)
    scratch = []
    if nk > 1:
        blk = [d for d in o_spec.block_shape if d is not None]
        scratch = [pltpu.VMEM(tuple(blk), F32)]
    sem = tuple("arbitrary" if (nk > 1 and ax == kaxis) else "parallel" for ax in range(len(grid)))
    return pl.pallas_call(
        body, name=name, grid=grid, in_specs=specs, out_specs=o_spec, out_shape=out_shape,
        scratch_shapes=scratch, input_output_aliases=aliases, compiler_params=_cp(sem),
    )(*args)


def _tile_span(n_tiles, part):
    if part is None:
        return 0, n_tiles
    p, of = part
    return p * n_tiles // of, (p + 1) * n_tiles // of


def _tile(n, pref):
    if n <= pref:
        return n
    for t in range(pref, 0, -LANES):
        if t % LANES == 0 and n % t == 0:
            return t
    raise ValueError(f"no tile for {n}")


def _mm_nn(name, a, b, out_dtype, add=None, tk=None, deps=(), part=None, prev=None, tiles=None):
    m, kk = a.shape
    n = b.shape[1]
    tm, tn = _tile(m, 1024), _tile(n, COL_TILE)
    if tiles is not None:
        tm, tn = _tile(m, tiles[0]), _tile(n, tiles[1])
    out_shape = jax.ShapeDtypeStruct((m, n), out_dtype)
    if tk is None or tk == kk:
        j0, j1 = _tile_span(n // tn, part)
        grid = (m // tm, j1 - j0)
        pairs = [(a, pl.BlockSpec((tm, kk), lambda i, j: (i, 0)), b, pl.BlockSpec((kk, tn), lambda i, j: (0, j0 + j)))]
        o_spec = pl.BlockSpec((tm, tn), lambda i, j: (i, j0 + j))
        add_ = None if add is None else (add, pl.BlockSpec((tm, tn), lambda i, j: (i, j0 + j)))
        return _mm(name, pairs, mode="nn", grid=grid, out_shape=out_shape, o_spec=o_spec, add=add_, deps=deps, prev=prev)
    tn = _tile(n, 1024)
    nk = kk // tk
    grid = (m // tm, n // tn, nk)
    pairs = [(a, pl.BlockSpec((tm, tk), lambda i, j, k: (i, k)), b, pl.BlockSpec((tk, tn), lambda i, j, k: (k, j)))]
    o_spec = pl.BlockSpec((tm, tn), lambda i, j, k: (i, j))
    add_ = None if add is None else (add, pl.BlockSpec((tm, tn), lambda i, j, k: (i, j)))
    return _mm(name, pairs, mode="nn", grid=grid, out_shape=out_shape, o_spec=o_spec, nk=nk, kaxis=2, add=add_, deps=deps)


def _mm_nt(name, abs_, out_dtype, tk, deps=()):
    m, kk = abs_[0][0].shape
    n = abs_[0][1].shape[0]
    tm = _tile(m, 1024)
    nk = kk // tk
    tn = _tile(n, COL_TILE if nk == 1 else 1024)
    out_shape = jax.ShapeDtypeStruct((m, n), out_dtype)
    if nk == 1:
        grid = (m // tm, n // tn)
        pairs = [(a, pl.BlockSpec((tm, kk), lambda i, j: (i, 0)), b, pl.BlockSpec((tn, kk), lambda i, j: (j, 0)))
                 for a, b in abs_]
        o_spec = pl.BlockSpec((tm, tn), lambda i, j: (i, j))
        return _mm(name, pairs, mode="nt", grid=grid, out_shape=out_shape, o_spec=o_spec, deps=deps)
    grid = (m // tm, n // tn, nk)
    pairs = [(a, pl.BlockSpec((tm, tk), lambda i, j, k: (i, k)), b, pl.BlockSpec((tn, tk), lambda i, j, k: (j, k)))
             for a, b in abs_]
    o_spec = pl.BlockSpec((tm, tn), lambda i, j, k: (i, j))
    return _mm(name, pairs, mode="nt", grid=grid, out_shape=out_shape, o_spec=o_spec, nk=nk, kaxis=2, deps=deps)


def _mm_tn(name, a, b, out_dtype, deps=()):
    t, m = a.shape
    n = b.shape[1]
    tm, tn = _tile(m, 512), _tile(n, 2048)
    if n > m:
        grid = (n // tn, m // tm)
        a_map, b_map, o_map = (lambda j, i: (0, i)), (lambda j, i: (0, j)), (lambda j, i: (i, j))
    else:
        grid = (m // tm, n // tn)
        a_map, b_map, o_map = (lambda i, j: (0, i)), (lambda i, j: (0, j)), (lambda i, j: (i, j))
    pairs = [(a, pl.BlockSpec((t, tm), a_map), b, pl.BlockSpec((t, tn), b_map))]
    o_spec = pl.BlockSpec((tm, tn), o_map)
    return _mm(name, pairs, mode="tn", grid=grid, out_shape=jax.ShapeDtypeStruct((m, n), out_dtype), o_spec=o_spec,
               deps=deps)


def _mm_tn_half(name, a, b, pos, mine, add=None, deps=()):
    t, m = a.shape
    r, n = m // 2, b.shape[1]
    tm, tn = _tile(r, 512), _tile(n, 2048)
    nbi = r // tm
    half = (lambda pos: pos[0]) if mine else (lambda pos: 1 - pos[0])
    if n > r:
        grid, ij = (n // tn, nbi), (lambda g0, g1: (g1, g0))
    else:
        grid, ij = (nbi, n // tn), (lambda g0, g1: (g0, g1))
    has_add = add is not None

    def body(pos_ref, a_ref, b_ref, *rest):
        d = lax.dot_general(a_ref[...], b_ref[...], _DIMS["tn"], preferred_element_type=F32)
        if has_add:
            d = d + rest[0][...].astype(F32)
        rest[-1][...] = d.astype(BF16)

    o_spec = pl.BlockSpec((None, tm, tn), lambda g0, g1, pos: (0, *ij(g0, g1)))
    grid_spec = pltpu.PrefetchScalarGridSpec(
        num_scalar_prefetch=1, grid=grid,
        in_specs=[pl.BlockSpec((t, tm), lambda g0, g1, pos: (0, half(pos) * nbi + ij(g0, g1)[0])),
                  pl.BlockSpec((t, tn), lambda g0, g1, pos: (0, ij(g0, g1)[1]))]
        + ([o_spec] if has_add else []) + [ANY] * len(deps),
        out_specs=o_spec)
    return pl.pallas_call(body, name=name, grid_spec=grid_spec, out_shape=jax.ShapeDtypeStruct((1, r, n), BF16),
                          compiler_params=_cp(("parallel",) * 2))(pos, a, b, *([add] if has_add else []), *deps)


def _proj_piece(name, h, w, prev, kvec, base, count, deps=()):
    t, kk = h.shape
    own = w.dtype == F32
    nn = w.shape[1] if own else w.shape[1] // N_CHIPS
    tm, tn = _tile(t, 1024), _tile(nn, COL_TILE)
    nb = nn // tn

    def body(kv_ref, h_ref, w_ref, *rest):
        rest[-1][...] = lax.dot_general(h_ref[...], w_ref[...].astype(BF16), _DIMS["nn"],
                                        preferred_element_type=F32).astype(BF16)

    cols = lambda s, i, j, kv: (0, j) if own else (0, kv[base + s] * nb + j)
    extra = ([] if prev is None else [prev]) + list(deps)
    grid_spec = pltpu.PrefetchScalarGridSpec(
        num_scalar_prefetch=1, grid=(count, t // tm, nb),
        in_specs=[pl.BlockSpec((tm, kk), lambda s, i, j, kv: (i, 0)), pl.BlockSpec((kk, tn), cols)] + [ANY] * len(extra),
        out_specs=pl.BlockSpec((tm, tn), lambda s, i, j, kv: (i, kv[base + s] * nb + j)))
    return pl.pallas_call(body, name=name, grid_spec=grid_spec, out_shape=jax.ShapeDtypeStruct((t, N_CHIPS * nn), BF16),
                          input_output_aliases={} if prev is None else {3: 0},
                          compiler_params=_cp(("parallel",) * 3))(kvec, h, w, *extra)


def _gmm_nn(name, p, w, out_dtype):
    t = p.shape[0]
    g, cg, dg = w.shape
    tm = _tile(t, 1024)
    pairs = [(p, pl.BlockSpec((tm, cg), lambda i, j: (i, j)), w, pl.BlockSpec((None, cg, dg), lambda i, j: (j, 0, 0)))]
    o_spec = pl.BlockSpec((tm, dg), lambda i, j: (i, j))
    return _mm(name, pairs, mode="nn", grid=(t // tm, g), out_shape=jax.ShapeDtypeStruct((t, g * dg), out_dtype),
               o_spec=o_spec)


def _gmm_nt(name, dy, w, out_dtype):
    t = dy.shape[0]
    g, cg, dg = w.shape
    tm = _tile(t, 1024)
    pairs = [(dy, pl.BlockSpec((tm, dg), lambda i, j: (i, j)), w, pl.BlockSpec((None, cg, dg), lambda i, j: (j, 0, 0)))]
    o_spec = pl.BlockSpec((tm, cg), lambda i, j: (i, j))
    return _mm(name, pairs, mode="nt", grid=(t // tm, g), out_shape=jax.ShapeDtypeStruct((t, g * cg), out_dtype),
               o_spec=o_spec)


def _gmm_tn(name, p, dy, g, out_dtype):
    t = p.shape[0]
    cg, dg = p.shape[1] // g, dy.shape[1] // g
    pairs = [(p, pl.BlockSpec((t, cg), lambda j: (0, j)), dy, pl.BlockSpec((t, dg), lambda j: (0, j)))]
    o_spec = pl.BlockSpec((None, cg, dg), lambda j: (j, 0, 0))
    return _mm(name, pairs, mode="tn", grid=(g,), out_shape=jax.ShapeDtypeStruct((g, cg, dg), out_dtype), o_spec=o_spec)


ROW_TILE = 256


def _rows(t):
    return _tile8(t, ROW_TILE)


def _tile8(n, pref):
    if n <= pref:
        return n
    for t in range(pref, 0, -8):
        if n % t == 0:
            return t
    raise ValueError(f"no row tile for {n}")


def _cast_place(name, w, pos, shard, deps=()):
    tr = _tile8(w.R, 512)
    if w.colshard:
        o_map = lambda h, i, pos: (0, h, i, pos[1])
    else:
        o_map = lambda h, i, pos: (pos[1], h, i, 0)

    def body(pos_ref, w_ref, *rest):
        rest[-1][...] = w_ref[...].astype(BF16)

    grid_spec = pltpu.PrefetchScalarGridSpec(
        num_scalar_prefetch=1, grid=(2, w.R // tr),
        in_specs=[pl.BlockSpec((None, tr, w.nn), lambda h, i, pos: (h, i, 0))] + [ANY] * len(deps),
        out_specs=pl.BlockSpec((None, None, tr, w.nn), o_map))
    return pl.pallas_call(body, name=name, grid_spec=grid_spec, out_shape=jax.ShapeDtypeStruct((w.P, 2, w.R, w.N), BF16),
                          compiler_params=_cp(("parallel", "parallel")))(pos, shard, *deps)


def _rms_fwd(name, x, g, deps=()):
    t, d = x.shape
    tm = _rows(t)

    def body(x_ref, g_ref, *rest):
        xf = x_ref[...]
        r = lax.rsqrt(jnp.mean(xf * xf, axis=-1, keepdims=True) + EPS)
        rest[-1][...] = (xf * r * g_ref[...]).astype(BF16)

    return pl.pallas_call(
        body, name=name, grid=(t // tm,),
        in_specs=[pl.BlockSpec((tm, d), lambda i: (i, 0)), pl.BlockSpec((1, d), lambda i: (0, 0))] + [ANY] * len(deps),
        out_specs=pl.BlockSpec((tm, d), lambda i: (i, 0)), out_shape=jax.ShapeDtypeStruct((t, d), BF16),
        compiler_params=_cp(("parallel",)),
    )(x, g, *deps)


def _rms_bwd(name, x, g, dh, dres, want_bf16, deps=()):
    t, d = x.shape
    tm = _rows(t)

    def body(x_ref, g_ref, dh_ref, dres_ref, *rest):
        rest = rest[len(deps):]
        dx_ref, rest = rest[0], rest[1:]
        dg_ref = rest[-1]
        xf = x_ref[...]
        r = lax.rsqrt(jnp.mean(xf * xf, axis=-1, keepdims=True) + EPS)
        xh = xf * r
        dhf = dh_ref[...].astype(F32)
        dxh = dhf * g_ref[...]
        m = jnp.mean(dxh * xh, axis=-1, keepdims=True)
        dx = dres_ref[...] + r * (dxh - xh * m)
        dx_ref[...] = dx
        if want_bf16:
            rest[0][...] = dx.astype(BF16)

        @pl.when(pl.program_id(0) == 0)
        def _():
            dg_ref[...] = jnp.zeros_like(dg_ref)

        dg_ref[...] += jnp.sum(dhf * xh, axis=0, keepdims=True)

    row = pl.BlockSpec((tm, d), lambda i: (i, 0))
    vec = pl.BlockSpec((1, d), lambda i: (0, 0))
    out_specs = [row] + ([row] if want_bf16 else []) + [vec]
    out_shape = ([jax.ShapeDtypeStruct((t, d), F32)] + ([jax.ShapeDtypeStruct((t, d), BF16)] if want_bf16 else [])
                 + [jax.ShapeDtypeStruct((1, d), F32)])
    return pl.pallas_call(body, name=name, grid=(t // tm,), in_specs=[row, vec, row, row] + [ANY] * len(deps),
                          out_specs=out_specs, out_shape=out_shape, compiler_params=_cp(("arbitrary",)))(x, g, dh, dres, *deps)


def _final_bwd(name, x3, gf, tgt):
    t, d = x3.shape
    tm = _rows(t)

    def body(x_ref, g_ref, t_ref, dx_ref, dxb_ref, dg_ref, lc_ref):
        xf = x_ref[...]
        g = g_ref[...]
        r = lax.rsqrt(jnp.mean(xf * xf, axis=-1, keepdims=True) + EPS)
        xh = xf * r
        diff = xh * g - t_ref[...]
        dy = diff * (1.0 / d)
        dxh = dy * g
        m = jnp.mean(dxh * xh, axis=-1, keepdims=True)
        dx = r * (dxh - xh * m)
        dx_ref[...] = dx
        dxb_ref[...] = dx.astype(BF16)

        @pl.when(pl.program_id(0) == 0)
        def _():
            dg_ref[...] = jnp.zeros_like(dg_ref)
            lc_ref[...] = jnp.zeros_like(lc_ref)

        dg_ref[...] += jnp.sum(dy * xh, axis=0, keepdims=True)
        lc_ref[...] += jnp.sum(diff * diff, axis=0, keepdims=True) * (0.5 / d)

    row = pl.BlockSpec((tm, d), lambda i: (i, 0))
    vec = pl.BlockSpec((1, d), lambda i: (0, 0))
    return pl.pallas_call(
        body, name=name, grid=(t // tm,), in_specs=[row, vec, row], out_specs=[row, row, vec, vec],
        out_shape=[jax.ShapeDtypeStruct((t, d), F32), jax.ShapeDtypeStruct((t, d), BF16),
                   jax.ShapeDtypeStruct((1, d), F32), jax.ShapeDtypeStruct((1, d), F32)],
        compiler_params=_cp(("arbitrary",)),
    )(x3, gf, tgt)


def _shift_down(v, k, t_idx):
    return jnp.where(t_idx >= k, pltpu.roll(v, k, 0), 0.0)


def _shift_up(v, k, t_idx):
    n = v.shape[0]
    return jnp.where(t_idx < n - k, pltpu.roll(v, n - k, 0), 0.0)


def _window_sums(v, shift, t_idx, grp):
    s = v + shift(v, 1, t_idx)
    out = s
    for lvl in range(1, len(POOL_WINDOWS)):
        s = s + shift(s, 1 << lvl, t_idx)
        out = jnp.where(grp >= lvl, s, out)
    return out


def _window_weight(t_idx, grp):
    return 1.0 / jnp.minimum(t_idx[:, :1] + 1, jnp.left_shift(2, grp)).astype(F32)


MIX_COLS = 256


def _mixer_fwd(name, proj, cw, cb, n_conv, n_groups, deps=()):
    t = proj.shape[0]
    nb = n_conv // MIX_COLS
    per_group = n_conv // n_groups // MIX_COLS

    def body(ba_ref, ca_ref, va_ref, vb_ref, cw_ref, cb_ref, *rest):
        z_ref, p_ref = rest[len(deps):]
        t_idx = lax.broadcasted_iota(jnp.int32, (t, MIX_COLS), 0)
        q = ca_ref[...].astype(F32) * va_ref[...].astype(F32)
        w = cw_ref[...]
        u = cb_ref[...] + w[0:1] * _shift_down(q, 2, t_idx) + w[1:2] * _shift_down(q, 1, t_idx) + w[2:3] * q
        z_ref[...] = (ba_ref[...].astype(F32) * u).astype(BF16)
        grp = pl.program_id(0) // per_group
        v = vb_ref[...].astype(F32)
        p_ref[...] = (_window_sums(v, _shift_down, t_idx, grp) * _window_weight(t_idx, grp) - v).astype(BF16)

    col = lambda s: pl.BlockSpec((t, MIX_COLS), lambda j: (0, s * nb + j))
    return pl.pallas_call(
        body, name=name, grid=(nb,),
        in_specs=[col(0), col(1), col(2), col(3), pl.BlockSpec((3, MIX_COLS), lambda j: (0, j)),
                  pl.BlockSpec((1, MIX_COLS), lambda j: (0, j))] + [ANY] * len(deps),
        out_specs=[col(0), col(0)],
        out_shape=[jax.ShapeDtypeStruct((t, n_conv), BF16), jax.ShapeDtypeStruct((t, n_conv), BF16)],
        compiler_params=_cp(("parallel",)),
    )(proj, proj, proj, proj, cw, cb, *deps)


def _mixer_bwd(name, dz, dp, proj, cw, cb, dproj, n_conv, n_groups, deps=()):
    t = proj.shape[0]
    nb = n_conv // MIX_COLS
    per_group = n_conv // n_groups // MIX_COLS

    def body(dz_ref, dp_ref, ba_ref, ca_ref, va_ref, cw_ref, cb_ref, _, *rest):
        o_ref, dcw_ref, dcb_ref, scr = rest[len(deps):]
        s = pl.program_id(1)

        @pl.when(s == 0)
        def _():
            t_idx = lax.broadcasted_iota(jnp.int32, (t, MIX_COLS), 0)
            ca, va = ca_ref[...].astype(F32), va_ref[...].astype(F32)
            q = ca * va
            q1, q2 = _shift_down(q, 1, t_idx), _shift_down(q, 2, t_idx)
            w = cw_ref[...]
            u = cb_ref[...] + w[0:1] * q2 + w[1:2] * q1 + w[2:3] * q
            dzf = dz_ref[...].astype(F32)
            du = dzf * ba_ref[...].astype(F32)
            scr[0] = (dzf * u).astype(BF16)
            dq = w[2:3] * du + w[1:2] * _shift_up(du, 1, t_idx) + w[0:1] * _shift_up(du, 2, t_idx)
            scr[1] = (dq * va).astype(BF16)
            scr[2] = (dq * ca).astype(BF16)
            dcb_ref[...] = jnp.sum(du, axis=0, keepdims=True)
            dcw_ref[0:1, :] = jnp.sum(du * q2, axis=0, keepdims=True)
            dcw_ref[1:2, :] = jnp.sum(du * q1, axis=0, keepdims=True)
            dcw_ref[2:3, :] = jnp.sum(du * q, axis=0, keepdims=True)
            grp = pl.program_id(0) // per_group
            dpf = dp_ref[...].astype(F32)
            e = dpf * _window_weight(t_idx, grp)
            scr[3] = (_window_sums(e, _shift_up, t_idx, grp) - dpf).astype(BF16)

        o_ref[...] = scr[s]

    col = lambda c: pl.BlockSpec((t, MIX_COLS), lambda j, s: (0, c * nb + j))
    own = pl.BlockSpec((t, MIX_COLS), lambda j, s: (0, j))
    return pl.pallas_call(
        body, name=name, grid=(nb, 4),
        in_specs=[own, own, col(0), col(1), col(2), pl.BlockSpec((3, MIX_COLS), lambda j, s: (0, j)),
                  pl.BlockSpec((1, MIX_COLS), lambda j, s: (0, j)), ANY] + [ANY] * len(deps),
        out_specs=[pl.BlockSpec((t, MIX_COLS), lambda j, s: (0, s * nb + j)),
                   pl.BlockSpec((3, MIX_COLS), lambda j, s: (0, j)), pl.BlockSpec((1, MIX_COLS), lambda j, s: (0, j))],
        out_shape=[jax.ShapeDtypeStruct(dproj.shape, BF16), jax.ShapeDtypeStruct((3, n_conv), F32),
                   jax.ShapeDtypeStruct((1, n_conv), F32)],
        scratch_shapes=[pltpu.VMEM((4, t, MIX_COLS), BF16)],
        input_output_aliases={7: 0},
        compiler_params=_cp(("arbitrary", "arbitrary")),
    )(dz, dp, proj, proj, proj, cw, cb, dproj, *deps)


def _merge_fwd(name, proj, bg, ya, yb, ps):
    t, d = ya.shape
    tm = _rows(t)

    def body(gab_ref, bg_ref, ya_ref, yb_ref, ps_ref, o_ref):
        gab = gab_ref[...].astype(F32) + bg_ref[...]
        sa, sb = jax.nn.sigmoid(gab[:, :d]), jax.nn.sigmoid(gab[:, d:])
        o_ref[...] = (sa * ya_ref[...].astype(F32) + sb * (yb_ref[...].astype(F32) * ps_ref[...])).astype(BF16)

    row = pl.BlockSpec((tm, d), lambda i: (i, 0))
    return pl.pallas_call(
        body, name=name, grid=(t // tm,),
        in_specs=[pl.BlockSpec((tm, 2 * d), lambda i: (i, 1)), pl.BlockSpec((1, 2 * d), lambda i: (0, 0)), row, row,
                  pl.BlockSpec((1, d), lambda i: (0, 0))],
        out_specs=row, out_shape=jax.ShapeDtypeStruct((t, d), BF16), compiler_params=_cp(("parallel",)),
    )(proj, bg, ya, yb, ps)


def _merge_bwd(name, dm, proj, bg, ya, yb, ps, deps=()):
    t, d = ya.shape
    tm = _rows(t)

    def body(dm_ref, gab_ref, bg_ref, ya_ref, yb_ref, ps_ref, *rest):
        dya_ref, dyb_ref, dg_ref, dba_ref, dbb_ref, dps_ref = rest[len(deps):]
        gab = gab_ref[...].astype(F32) + bg_ref[...]
        sa, sb = jax.nn.sigmoid(gab[:, :d]), jax.nn.sigmoid(gab[:, d:])
        dmf = dm_ref[...].astype(F32)
        ybf, ps_ = yb_ref[...].astype(F32), ps_ref[...]
        dya_ref[...] = (dmf * sa).astype(BF16)
        dyb = dmf * sb
        dyb_ref[...] = (dyb * ps_).astype(BF16)
        dga = dmf * ya_ref[...].astype(F32) * sa * (1.0 - sa)
        dgb = dmf * (ybf * ps_) * sb * (1.0 - sb)
        dg_ref[:, :d] = dga.astype(BF16)
        dg_ref[:, d:] = dgb.astype(BF16)

        @pl.when(pl.program_id(0) == 0)
        def _():
            dba_ref[...] = jnp.zeros_like(dba_ref)
            dbb_ref[...] = jnp.zeros_like(dbb_ref)
            dps_ref[...] = jnp.zeros_like(dps_ref)

        dba_ref[...] += jnp.sum(dga, axis=0, keepdims=True)
        dbb_ref[...] += jnp.sum(dgb, axis=0, keepdims=True)
        dps_ref[...] += jnp.sum(dyb * ybf, axis=0, keepdims=True)

    row = pl.BlockSpec((tm, d), lambda i: (i, 0))
    vec = pl.BlockSpec((1, d), lambda i: (0, 0))
    gates = pl.BlockSpec((tm, 2 * d), lambda i: (i, 1))
    return pl.pallas_call(
        body, name=name, grid=(t // tm,),
        in_specs=[row, gates, pl.BlockSpec((1, 2 * d), lambda i: (0, 0)), row, row, vec] + [ANY] * len(deps),
        out_specs=[row, row, gates, vec, vec, vec],
        out_shape=[jax.ShapeDtypeStruct((t, d), BF16), jax.ShapeDtypeStruct((t, d), BF16),
                   jax.ShapeDtypeStruct(proj.shape, BF16), jax.ShapeDtypeStruct((1, d), F32),
                   jax.ShapeDtypeStruct((1, d), F32), jax.ShapeDtypeStruct((1, d), F32)],
        compiler_params=_cp(("arbitrary",)),
    )(dm, proj, bg, ya, yb, ps, *deps)


def _ffn_up_act(name, h, w_up, gate, part=None, prev=None, deps=()):
    t, d = h.shape
    f = w_up.shape[1]
    tm, tf = _tile(t, 1024), _tile(f, 512)
    j0, j1 = _tile_span(f // tf, part)
    n_prev = 0 if prev is None else 2
    extra = ([] if prev is None else list(prev)) + list(deps)

    def body(h_ref, w_ref, g_ref, *rest):
        u_ref, a_ref = rest[len(extra):]
        u = lax.dot_general(h_ref[...], w_ref[...], _DIMS["nn"], preferred_element_type=F32)
        g = g_ref[...].astype(F32)
        u_ref[...] = u.astype(BF16)
        a_ref[...] = (g * jax.nn.sigmoid(g) * u).astype(BF16)

    blk = pl.BlockSpec((tm, tf), lambda i, j: (i, j0 + j))
    shp = jax.ShapeDtypeStruct((t, f), BF16)
    return pl.pallas_call(
        body, name=name, grid=(t // tm, j1 - j0),
        in_specs=[pl.BlockSpec((tm, d), lambda i, j: (i, 0)), pl.BlockSpec((d, tf), lambda i, j: (0, j0 + j)), blk]
        + [ANY] * len(extra),
        out_specs=[blk, blk], out_shape=[shp, shp], input_output_aliases={3 + i: i for i in range(n_prev)},
        compiler_params=_cp(("parallel", "parallel")))(h, w_up, gate, *extra)


def _ffn_bwd(name, dy, w_down, gate, up):
    t, d = dy.shape
    f = w_down.shape[0]
    tm, tf = _tile(t, 1024), _tile(f, 512)

    def body(dy_ref, w_ref, g_ref, u_ref, dg_ref, du_ref):
        da = lax.dot_general(dy_ref[...], w_ref[...], _DIMS["nt"], preferred_element_type=F32)
        g = g_ref[...].astype(F32)
        s = jax.nn.sigmoid(g)
        du_ref[...] = (da * (g * s)).astype(BF16)
        dg_ref[...] = (da * u_ref[...].astype(F32) * (s * (1.0 + g * (1.0 - s)))).astype(BF16)

    blk = pl.BlockSpec((tm, tf), lambda i, j: (i, j))
    shp = jax.ShapeDtypeStruct((t, f), BF16)
    return pl.pallas_call(
        body, name=name, grid=(t // tm, f // tf),
        in_specs=[pl.BlockSpec((tm, d), lambda i, j: (i, 0)), pl.BlockSpec((tf, d), lambda i, j: (j, 0)), blk, blk],
        out_specs=[blk, blk], out_shape=[shp, shp], compiler_params=_cp(("parallel", "parallel")))(dy, w_down, gate, up)


def _adamw_math(w, g, m, v):
    m = ADAM_B1 * m + (1.0 - ADAM_B1) * g
    v = ADAM_B2 * v + (1.0 - ADAM_B2) * (g * g)
    m_hat = m / (1.0 - ADAM_B1 ** ADAM_STEP)
    v_hat = v / (1.0 - ADAM_B2 ** ADAM_STEP)
    delta = -ADAM_LR * (m_hat / (jnp.sqrt(v_hat) + ADAM_EPS) + ADAM_WD * w)
    return delta, m, v


def _adamw(name, w, g, m, v):
    r, c = w.shape
    tr = _tile8(r, 512 if c <= 1024 else 256)

    def body(w_ref, g_ref, m_ref, v_ref, go_ref, d_ref, nm_ref, nv_ref):
        g = g_ref[...]
        go_ref[...] = g
        d_ref[...], nm_ref[...], nv_ref[...] = _adamw_math(w_ref[...], g, m_ref[...], v_ref[...])

    blk = pl.BlockSpec((tr, c), lambda i: (i, 0))
    shp = jax.ShapeDtypeStruct((r, c), F32)
    return pl.pallas_call(body, name=name, grid=(r // tr,), in_specs=[blk] * 4, out_specs=[blk] * 4,
                          out_shape=[shp] * 4, compiler_params=_cp(("parallel",)))(w, g, m, v)


class _Weight:
    def __init__(self, name, rows, cols, colshard):
        self.name, self.colshard = name, colshard
        self.R, self.nn = rows // 2, cols
        self.P = 1 if colshard else N_CHIPS
        self.N = N_CHIPS * cols if colshard else cols

    def cols(self, k):
        return pl.ds(pl.multiple_of(k * self.nn, LANES), self.nn)

    def shard(self, ref, k):
        return ref.at[0, :, :, self.cols(k)] if self.colshard else ref.at[k]

    def half(self, ref, k, h):
        return ref.at[0, h, :, self.cols(k)] if self.colshard else ref.at[k, h]

    def quarter(self, ref, k, h, q):
        return self.half(ref, k, h).at[pl.ds(q * (self.R // 2), self.R // 2), :]

    def part(self, ref, k):
        return ref.at[0, :, self.cols(k)] if self.colshard else ref.at[k]


def _remote(src, dst, ssem, rsem, dev):
    return pltpu.make_async_remote_copy(src_ref=src, dst_ref=dst, send_sem=ssem, recv_sem=rsem, device_id=dev,
                                        device_id_type=MESH)


def _other_chips(x, y):
    chips = [(1 - x, y), (x, 1 - y), (1 - x, 1 - y)]
    return chips, [2 * cx + cy for cx, cy in chips]


def _hbm(a):
    return pltpu.with_memory_space_constraint(a, pltpu.HBM)


def _gather_start(name, groups, lands, after=()):
    flat = [w for grp in groups for w in grp]
    nw, ng = len(flat), len(groups)

    def body(*refs):
        land = refs[:nw]
        sems = refs[nw + len(after):nw + len(after) + 2 * ng]
        token = refs[2 * nw + len(after) + 2 * ng]
        x, y, c = _mesh_pos()
        k_me = 2 * x + y
        chips, _ = _other_chips(x, y)
        i = 0
        for g, grp in enumerate(groups):
            for wi, w in enumerate(grp):
                mine = w.half(land[i], k_me, c)
                for j, chip in enumerate(chips):
                    _remote(mine, mine, sems[2 * g].at[3 * wi + j], sems[2 * g + 1].at[3 * wi + j], (*chip, c)).start()
                i += 1
        token[...] = jnp.zeros_like(token)

    sem_shapes = []
    for grp in groups:
        sem_shapes += [pltpu.SemaphoreType.DMA((3 * len(grp),))] * 2
    out = pl.pallas_call(
        body, name=name, in_specs=[HBM] * nw + [ANY] * len(after),
        out_specs=[SEM] * (2 * ng) + [HBM] * nw + [VMEM],
        out_shape=sem_shapes + [pltpu.HBM(a.shape, a.dtype) for a in lands] + [jax.ShapeDtypeStruct((8, LANES), F32)],
        input_output_aliases={i: 2 * ng + i for i in range(nw)},
        compiler_params=pltpu.CompilerParams(has_side_effects=EFFECT),
    )(*[_hbm(a) for a in lands], *after)
    sems = [(out[2 * g], out[2 * g + 1]) for g in range(ng)]
    return sems, list(out[2 * ng:2 * ng + nw]), out[-1]


def _gather_wait(name, grp, lands, ssem, rsem, after):
    n = len(grp)

    def body(*refs):
        land, ssem_ref, rsem_ref = refs[:n], refs[n], refs[n + 1]
        x, y, c = _mesh_pos()
        k_me = 2 * x + y
        chips, ks = _other_chips(x, y)
        for wi, w in enumerate(grp):
            for j, chip in enumerate(chips):
                cp = _remote(w.half(land[wi], k_me, c), w.half(land[wi], ks[j], c), ssem_ref.at[3 * wi + j],
                             rsem_ref.at[3 * wi + j], (*chip, c))
                cp.wait_send()
                cp.wait_recv()

    return pl.pallas_call(
        body, name=name, in_specs=[HBM] * n + [SEM, SEM, ANY], out_specs=[HBM] * n,
        out_shape=[pltpu.HBM(a.shape, a.dtype) for a in lands], input_output_aliases={i: i for i in range(n)},
        compiler_params=pltpu.CompilerParams(has_side_effects=EFFECT),
    )(*lands, ssem, rsem, after)


def _split_start(name, arrays, n, copies, after=()):
    na = len(arrays)

    def body(*refs):
        ssem, rsem, token = refs[na + len(after):][0], refs[na + len(after):][1], refs[2 * na + len(after) + 2]
        for i, (src, dst, dev, _) in enumerate(copies(refs[:na], *_mesh_pos())):
            _remote(src, dst, ssem.at[i], rsem.at[i], dev).start()
        token[...] = jnp.zeros_like(token)

    out = pl.pallas_call(
        body, name=name, in_specs=[HBM] * na + [ANY] * len(after), out_specs=[SEM, SEM] + [HBM] * na + [VMEM],
        out_shape=[pltpu.SemaphoreType.DMA((n,))] * 2 + [pltpu.HBM(a.shape, a.dtype) for a in arrays]
        + [jax.ShapeDtypeStruct((8, LANES), F32)],
        input_output_aliases={i: 2 + i for i in range(na)},
        compiler_params=pltpu.CompilerParams(has_side_effects=EFFECT),
    )(*[_hbm(a) for a in arrays], *after)
    return out[0], out[1], list(out[2:2 + na]), out[-1]


def _split_wait(name, arrays, ssem, rsem, copies, after):
    na = len(arrays)

    def body(*refs):
        for i, (src, _, dev, dst) in enumerate(copies(refs[:na], *_mesh_pos())):
            cp = _remote(src, dst, refs[na].at[i], refs[na + 1].at[i], dev)
            cp.wait_send()
            cp.wait_recv()

    return list(pl.pallas_call(
        body, name=name, in_specs=[HBM] * na + [SEM, SEM] + [ANY] * len(after), out_specs=[HBM] * na,
        out_shape=[pltpu.HBM(a.shape, a.dtype) for a in arrays], input_output_aliases={i: i for i in range(na)},
        compiler_params=pltpu.CompilerParams(has_side_effects=EFFECT),
    )(*arrays, ssem, rsem, *after))


def _pass_copies(grp, rels=(0, 1, 2)):
    def copies(land, x, y, c):
        _, ks = _other_chips(x, y)
        return [(w.half(land[wi], ks[j], c), w.half(land[wi], ks[j], c), (x, y, 1 - c), w.half(land[wi], ks[j], 1 - c))
                for wi, w in enumerate(grp) for j in rels]
    copies.n = len(grp) * len(rels)
    return copies


def _near_copies(grp):
    def copies(land, x, y, c):
        chips, ks = _other_chips(x, y)
        out = []
        for wi, w in enumerate(grp):
            mine = w.half(land[wi], 2 * x + y, c)
            out += [(mine, mine, (*chips[j], c), w.half(land[wi], ks[j], c)) for j in (0, 1)]
        return out
    copies.n = 2 * len(grp)
    return copies


def _far_copies(grp):
    def copies(land, x, y, c):
        chips, ks = _other_chips(x, y)
        out = []
        for wi, w in enumerate(grp):
            for j in (0, 1):
                q = w.quarter(land[wi], ks[j], c, j)
                out.append((q, q, (*chips[1 - j], c), w.quarter(land[wi], ks[2], c, j)))
        return out
    copies.n = 2 * len(grp)
    return copies


def _pair_copies(n, whole=False):
    def copies(refs, x, y, c):
        return [(refs[i] if whole else refs[i].at[:, 1 - c], refs[n + i], (x, y, 1 - c), refs[n + i]) for i in range(n)]
    return copies


def _share_copies(n):
    def copies(refs, x, y, c):
        return [(refs[i].at[c], refs[i].at[c], (x, y, 1 - c), refs[i].at[1 - c]) for i in range(n)]
    return copies


def _gather_conv_w(cw):
    ncw = cw.shape[1]

    def body(cw_ref, out_ref, ssem, rsem):
        x, y, c = _mesh_pos()
        k_me = 2 * x + y
        chips, ks = _other_chips(x, y)
        cols = lambda k: out_ref.at[:, pl.ds(pl.multiple_of(k * ncw, LANES), ncw)]
        cps = [_remote(cw_ref, cols(k_me), ssem.at[j], rsem.at[j], (*chip, c)) for j, chip in enumerate(chips)]
        for cp in cps:
            cp.start()
        for k in range(N_CHIPS):
            @pl.when(k_me == k)
            def _():
                out_ref[:, k * ncw:(k + 1) * ncw] = cw_ref[...]
        for j in range(3):
            _remote(cw_ref, cols(ks[j]), ssem.at[j], rsem.at[j], (*chips[j], c)).wait_recv()
        for cp in cps:
            cp.wait_send()

    return pl.pallas_call(
        body, name="gather_conv_w", in_specs=[VMEM], out_specs=VMEM,
        out_shape=jax.ShapeDtypeStruct((3, N_CHIPS * ncw), F32),
        scratch_shapes=[pltpu.SemaphoreType.DMA((3,)), pltpu.SemaphoreType.DMA((3,))],
    )(cw)


def _grad_tiles(w, n):
    return _tile8(w.R, 512) if w.R <= 512 else w.R // 2, _tile(n, 2048)


def _pair_sum(name, w, pos, grad, got):
    tr, tn = _grad_tiles(w, w.N)

    def body(pos_ref, g_ref, r_ref, o_ref):
        o_ref[...] = (g_ref[...].astype(F32) + r_ref[...].astype(F32)).astype(BF16)

    blk = pl.BlockSpec((None, tr, tn), lambda p, i, j, pos: (p, i, j))
    grid_spec = pltpu.PrefetchScalarGridSpec(
        num_scalar_prefetch=1, grid=(w.P, w.R // tr, w.N // tn),
        in_specs=[pl.BlockSpec((None, None, tr, tn), lambda p, i, j, pos: (p, pos[0], i, j)), blk], out_specs=blk)
    return pl.pallas_call(body, name=name, grid_spec=grid_spec, out_shape=jax.ShapeDtypeStruct((w.P, w.R, w.N), BF16),
                          compiler_params=_cp(("parallel",) * 3))(pos, grad, got)


def _scatter_start(name, ws, pairs):
    nw = len(ws)

    def body(*refs):
        pr, land = refs[:nw], refs[nw:2 * nw]
        ssem, rsem = refs[2 * nw], refs[2 * nw + 1]
        token = refs[4 * nw + 2]
        x, y, c = _mesh_pos()
        chips, ks = _other_chips(x, y)
        for i, w in enumerate(ws):
            for j, chip in enumerate(chips):
                _remote(w.part(pr[i], ks[j]), land[i].at[j], ssem.at[3 * i + j], rsem.at[3 * i + j], (*chip, c)).start()
        token[...] = jnp.zeros_like(token)

    lands = [lax.empty((3, w.R, w.nn), BF16) for w in ws]
    out = pl.pallas_call(
        body, name=name, in_specs=[HBM] * (2 * nw),
        out_specs=[SEM, SEM] + [HBM] * (2 * nw) + [VMEM],
        out_shape=[pltpu.SemaphoreType.DMA((3 * nw,))] * 2 + [pltpu.HBM(a.shape, a.dtype) for a in pairs + lands]
        + [jax.ShapeDtypeStruct((8, LANES), F32)],
        input_output_aliases={i: 2 + i for i in range(2 * nw)},
        compiler_params=pltpu.CompilerParams(has_side_effects=EFFECT),
    )(*[_hbm(a) for a in pairs + lands])
    return out[0], out[1], list(out[2:2 + nw]), list(out[2 + nw:2 + 2 * nw]), out[-1]


def _scatter_wait(name, ws, pairs, lands, ssem, rsem, after):
    nw = len(ws)

    def body(*refs):
        pr, land = refs[:nw], refs[nw:2 * nw]
        ssem_ref, rsem_ref = refs[2 * nw], refs[2 * nw + 1]
        x, y, c = _mesh_pos()
        chips, ks = _other_chips(x, y)
        for i, w in enumerate(ws):
            for j, chip in enumerate(chips):
                cp = _remote(w.part(pr[i], ks[j]), land[i].at[j], ssem_ref.at[3 * i + j], rsem_ref.at[3 * i + j], (*chip, c))
                cp.wait_send()
                cp.wait_recv()

    out = pl.pallas_call(
        body, name=name, in_specs=[HBM] * (2 * nw) + [SEM, SEM] + [ANY] * len(after), out_specs=[HBM] * (2 * nw),
        out_shape=[pltpu.HBM(a.shape, a.dtype) for a in pairs + lands],
        input_output_aliases={i: i for i in range(2 * nw)},
        compiler_params=pltpu.CompilerParams(has_side_effects=EFFECT),
    )(*pairs, *lands, ssem, rsem, *after)
    return list(out[:nw]), list(out[nw:])


def _final_sum(name, w, pos, grad, got, parts):
    tr, tn = _grad_tiles(w, w.nn)
    nbc = w.nn // tn
    if got is None:
        return _final_sum_pair(name, w, pos, grad, parts, tr, tn)

    def body(pos_ref, g_ref, r_ref, p_ref, o_ref):
        acc = g_ref[...].astype(F32) + r_ref[...].astype(F32)
        for j in range(3):
            acc = acc + p_ref[j].astype(F32)
        o_ref[...] = acc

    if w.colshard:
        g_spec = pl.BlockSpec((None, None, tr, tn), lambda i, j, pos: (0, pos[0], i, pos[1] * nbc + j))
        r_spec = pl.BlockSpec((None, tr, tn), lambda i, j, pos: (0, i, pos[1] * nbc + j))
    else:
        g_spec = pl.BlockSpec((None, None, tr, tn), lambda i, j, pos: (pos[1], pos[0], i, j))
        r_spec = pl.BlockSpec((None, tr, tn), lambda i, j, pos: (pos[1], i, j))
    grid_spec = pltpu.PrefetchScalarGridSpec(
        num_scalar_prefetch=1, grid=(w.R // tr, nbc),
        in_specs=[g_spec, r_spec, pl.BlockSpec((3, tr, tn), lambda i, j, pos: (0, i, j))],
        out_specs=pl.BlockSpec((None, tr, tn), lambda i, j, pos: (pos[0], i, j)))
    return pl.pallas_call(body, name=name, grid_spec=grid_spec, out_shape=jax.ShapeDtypeStruct((2, w.R, w.nn), F32),
                          compiler_params=_cp(("parallel",) * 2))(pos, grad, got, parts)


def _final_sum_pair(name, w, pos, pair, parts, tr, tn):
    nbc = w.nn // tn

    def body(pos_ref, g_ref, p_ref, o_ref):
        acc = g_ref[...].astype(F32)
        for j in range(3):
            acc = acc + p_ref[j].astype(F32)
        o_ref[...] = acc

    if w.colshard:
        g_spec = pl.BlockSpec((None, tr, tn), lambda i, j, pos: (0, i, pos[1] * nbc + j))
    else:
        g_spec = pl.BlockSpec((None, tr, tn), lambda i, j, pos: (pos[1], i, j))
    grid_spec = pltpu.PrefetchScalarGridSpec(
        num_scalar_prefetch=1, grid=(w.R // tr, nbc),
        in_specs=[g_spec, pl.BlockSpec((3, tr, tn), lambda i, j, pos: (0, i, j))],
        out_specs=pl.BlockSpec((None, tr, tn), lambda i, j, pos: (pos[0], i, j)))
    return pl.pallas_call(body, name=name, grid_spec=grid_spec, out_shape=jax.ShapeDtypeStruct((2, w.R, w.nn), F32),
                          compiler_params=_cp(("parallel",) * 2))(pos, pair, parts)


VEC_ROWS = 16


def _vector_step(d, n_conv, parts, params, deps=()):
    ncw = params[2][0].shape[1]
    n_par = len(params)

    def body(*refs):
        dg1, dba, dbb, dcw, dcb, dps, dg2, dgf, lc = refs[:9]
        wmv = refs[9:9 + 3 * n_par]
        refs = refs[9 + 3 * n_par + len(deps):]
        outs = refs[:4 * n_par]
        loss_ref = refs[4 * n_par]
        snd, got, ssem, rsem = refs[4 * n_par + 1:]
        x, y, c = _mesh_pos()
        me = 4 * x + 2 * y + c
        snd[...] = jnp.zeros_like(snd)
        for row, ref in ((0, dg1), (1, dba), (2, dbb), (3, dps), (4, dg2), (5, dgf), (6, lc)):
            snd[row:row + 1, :] = ref[...]
        snd[7:8, :n_conv] = dcb[...]
        snd[8:11, :n_conv] = dcw[...]
        cps = []
        for r in range(1, N_DEV):
            peer = tuple(1 - p if (r >> b) & 1 else p for p, b in ((x, 2), (y, 1), (c, 0)))
            cps.append(_remote(snd, got.at[me], ssem.at[r - 1], rsem.at[r - 1], peer))
        for cp in cps:
            cp.start()
        got[me] = snd[...]
        for r in range(1, N_DEV):
            peer = tuple(1 - p if (r >> b) & 1 else p for p, b in ((x, 2), (y, 1), (c, 0)))
            _remote(snd, got.at[4 * peer[0] + 2 * peer[1] + peer[2]], ssem.at[r - 1], rsem.at[r - 1], peer).wait_recv()
        for cp in cps:
            cp.wait_send()
        tot = got[0]
        for dev in range(1, N_DEV):
            tot = tot + got[dev]
        loss_ref[...] = jnp.sum(tot[6:7, :], axis=1, keepdims=True)
        k_me = 2 * x + y
        g_cw = jnp.zeros((3, ncw), F32)
        for k in range(N_CHIPS):
            g_cw = g_cw + jnp.where(k_me == k, tot[8:11, k * ncw:(k + 1) * ncw], 0.0)
        grads = [tot[0:1, :], jnp.concatenate([tot[1:2, :], tot[2:3, :]], axis=1), g_cw, tot[7:8, :n_conv],
                 tot[3:4, :], tot[4:5, :], tot[5:6, :]]
        for i, g in enumerate(grads):
            w_ref, m_ref, v_ref = wmv[3 * i:3 * i + 3]
            delta, nm, nv = _adamw_math(w_ref[...], g, m_ref[...], v_ref[...])
            outs[4 * i][...] = g
            outs[4 * i + 1][...] = delta
            outs[4 * i + 2][...] = nm
            outs[4 * i + 3][...] = nv

    args = list(parts)
    out_shape = []
    for w, m, v in params:
        args += [w, m, v]
        out_shape += [jax.ShapeDtypeStruct(w.shape, F32)] * 4
    out_shape.append(jax.ShapeDtypeStruct((1, 1), F32))
    return pl.pallas_call(
        body, name="vector_params_step", in_specs=[VMEM] * len(args) + [ANY] * len(deps),
        out_specs=[VMEM] * len(out_shape), out_shape=out_shape,
        scratch_shapes=[pltpu.VMEM((VEC_ROWS, d), F32), pltpu.VMEM((N_DEV, VEC_ROWS, d), F32),
                        pltpu.SemaphoreType.DMA((N_DEV - 1,)), pltpu.SemaphoreType.DMA((N_DEV - 1,))],
        compiler_params=pltpu.CompilerParams(vmem_limit_bytes=VMEM_LIMIT),
    )(*args, *deps)


def kernel(x, norm1_g, w_in, b_gate, conv_w, conv_b, w_a_out, w_pool, pool_scale, w_o, norm2_g, w_ffn_gate, w_ffn_up, w_ffn_down, final_g, loss_target, m_norm1_g, m_w_in, m_b_gate, m_conv_w, m_conv_b, m_w_a_out, m_w_pool, m_pool_scale, m_w_o, m_norm2_g, m_w_ffn_gate, m_w_ffn_up, m_w_ffn_down, m_final_g, v_norm1_g, v_w_in, v_b_gate, v_conv_w, v_conv_b, v_w_a_out, v_w_pool, v_pool_scale, v_w_o, v_norm2_g, v_w_ffn_gate, v_w_ffn_up, v_w_ffn_down, v_final_g):
    t, d = x.shape[1], x.shape[2]
    n_conv = conv_b.shape[1]
    n_groups, pool_cg, pool_dg = w_pool.shape[1], w_pool.shape[2], N_CHIPS * w_pool.shape[3]
    d_ff = N_CHIPS * w_ffn_gate.shape[2]
    assert n_conv // n_groups == pool_cg and n_conv % (n_groups * MIX_COLS) == 0 and n_groups == len(POOL_WINDOWS)

    big = {"w_in": (w_in, m_w_in, v_w_in), "w_a_out": (w_a_out, m_w_a_out, v_w_a_out), "w_pool": (w_pool, m_w_pool, v_w_pool),
           "w_o": (w_o, m_w_o, v_w_o), "w_ffn_gate": (w_ffn_gate, m_w_ffn_gate, v_w_ffn_gate),
           "w_ffn_up": (w_ffn_up, m_w_ffn_up, v_w_ffn_up), "w_ffn_down": (w_ffn_down, m_w_ffn_down, v_w_ffn_down)}
    colshard = {"w_in": True, "w_a_out": True, "w_pool": True, "w_o": False, "w_ffn_gate": True, "w_ffn_up": True,
                "w_ffn_down": False}
    names = list(big)
    shard2d = {n: big[n][0].reshape(-1, big[n][0].shape[-1]) for n in names}
    ws = [_Weight(n, *shard2d[n].shape, colshard[n]) for n in names]

    xs, tgt = x[0], loss_target[0]
    cw_loc = conv_w[0]
    pos = jnp.stack([lax.axis_index("c"), 2 * lax.axis_index("x") + lax.axis_index("y")]).astype(jnp.int32)
    by_name = {w.name: w for w in ws}
    groups = [[by_name[n] for n in g] for g in (["w_in"], ["w_a_out", "w_pool", "w_o"], ["w_ffn_gate"], ["w_ffn_up"],
                                                 ["w_ffn_down"])]
    first = [sum(len(g) for g in groups[:i]) for i in range(len(groups))]
    rgroups = [groups[0], groups[1], groups[2] + groups[3], groups[4]]

    cw_full = _gather_conv_w(cw_loc)
    cast = lambda w, dep: _cast_place(f"cast_{w.name}", w, pos, shard2d[w.name].reshape(2, w.R, w.nn), deps=[dep])
    chips, ks = _other_chips(lax.axis_index("x"), lax.axis_index("y"))
    kvec = jnp.stack([pos[1], *ks]).astype(jnp.int32)
    full = {}

    def start(name, arrays, copies, after=()):
        ssem, rsem, arrays, token = _split_start(name, arrays, copies.n, copies, after)
        return name, arrays, ssem, rsem, copies, token

    def wait(started, after):
        name, arrays, ssem, rsem, copies, _ = started
        return _split_wait(name + "_wait", arrays, ssem, rsem, copies, after)

    def pass_on(g, got, after=()):
        return start(f"pass_{g}", got, _pass_copies(groups[g]), after)

    def passed(g, st, after=None):
        got = wait(st, [st[5]] if after is None else after)
        full.update({w.name: a.reshape(w.P * 2 * w.R, w.N) for w, a in zip(groups[g], got)})

    near = start("near_0", [cast(w, cw_full) for w in groups[0]], _near_copies(groups[0]))
    rest = [cast(w, near[5]) for grp in groups[1:] for w in grp]
    h1 = _rms_fwd("norm1_fwd", xs, norm1_g, deps=[near[5]])
    proj = _proj_piece("proj_own", h1, shard2d["w_in"], None, kvec, 0, 1, deps=rest)
    got = wait(near, [proj])
    far = start("far_0", got, _far_copies(groups[0]))
    sems_b, lands_b, tok_b = _gather_start("gather_start_b", groups[1:2], rest[:3], after=[far[5]])
    st = start("pass_near_0", far[1], _pass_copies(groups[0], (0, 1)), [tok_b])
    got = wait(st, [st[5]])
    proj = _proj_piece("proj_near", h1, got[0].reshape(-1, groups[0][0].N), proj, kvec, 1, 2)
    st = start("pass_far_0", wait((far[0], got) + far[2:], [proj]), _pass_copies(groups[0], (2,)))
    got = wait(st, [st[5]])
    w_in_full = got[0].reshape(-1, groups[0][0].N)
    proj = _proj_piece("proj_far", h1, w_in_full, proj, kvec, 3, 1)
    got = _gather_wait("gather_wait_1", groups[1], lands_b, *sems_b[0], proj)
    near_g = start("near_2", rest[3:4], _near_copies(groups[2]), got)
    st = pass_on(1, got, [near_g[5]])
    z, p = _mixer_fwd("mixer_fwd", proj, cw_full, conv_b, n_conv, n_groups, deps=[st[5]])
    passed(1, st, [z])
    wp_full = full["w_pool"].reshape(n_groups, pool_cg, pool_dg)
    ya = _mm_nn("conv_out", z, full["w_a_out"], BF16)
    yb = _gmm_nn("pool_out", p, wp_full, BF16)
    merged = _merge_fwd("merge_fwd", proj, b_gate, ya, yb, pool_scale)
    far_g = start("far_2", wait(near_g, [merged]), _far_copies(groups[2]))
    near_u = start("near_3", rest[4:5], _near_copies(groups[3]), [far_g[5]])
    x2 = _mm_nn("mix_out", merged, full["w_o"], F32, add=xs, deps=[near_u[5]])
    st = pass_on(2, wait(far_g, [x2]))
    h2 = _rms_fwd("norm2_fwd", x2, norm2_g, deps=[st[5]])
    passed(2, st, [h2])
    gate = _mm_nn("ffn_gate_a", h2, full["w_ffn_gate"], BF16, part=(0, 2))
    far_u = start("far_3", wait(near_u, [gate]), _far_copies(groups[3]))
    near_d = start("near_4", rest[5:6], _near_copies(groups[4]), [far_u[5]])
    gate = _mm_nn("ffn_gate_b", h2, full["w_ffn_gate"], BF16, part=(1, 2), prev=gate, deps=[near_d[5]])
    passed(3, pass_on(3, wait(far_u, [gate])))
    up_act = _ffn_up_act("ffn_up_act_a", h2, full["w_ffn_up"], gate, part=(0, 2))
    far_d = start("far_4", wait(near_d, [up_act[0]]), _far_copies(groups[4]))
    up, act = _ffn_up_act("ffn_up_act_b", h2, full["w_ffn_up"], gate, part=(1, 2), prev=up_act, deps=[far_d[5]])
    passed(4, pass_on(4, wait(far_d, [act])))
    x3 = _mm_nn("ffn_down", act, full["w_ffn_down"], F32, add=x2, tiles=(512, 512))

    pending = {}

    def pair_start(g, grads):
        grp = rgroups[g]
        gcan = [grads[w.name].reshape(w.P, 2, w.R, w.N) for w in grp]
        slots = [lax.empty((w.P, w.R, w.N), BF16) for w in grp]
        pending[g] = _split_start(f"pair_start_{g}", gcan + slots, len(grp), _pair_copies(len(grp)))
        return pending[g][3]

    def scatter_start(g, after):
        grp = rgroups[g]
        n = len(grp)
        ssem, rsem, arrs, _ = pending[g]
        arrs = _split_wait(f"pair_wait_{g}", arrs, ssem, rsem, _pair_copies(n), after)
        gcan, sib = arrs[:n], arrs[n:]
        pairs = [_pair_sum(f"pair_sum_{w.name}", w, pos, a, s) for w, a, s in zip(grp, gcan, sib)]
        ssem, rsem, pairs, slots, token = _scatter_start(f"scatter_start_{g}", grp, pairs)
        pending[g] = (gcan, sib, pairs, slots, ssem, rsem)
        return token

    def pair_start_halves(g, ab, deps):
        grp = rgroups[g]
        sent = [_mm_tn_half(f"d{w.name}_sib", a, b, pos, False, deps=deps if i == 0 else ()) for i, (w, (a, b)) in enumerate(zip(grp, ab))]
        slots = [lax.empty((1, w.R, w.N), BF16) for w in grp]
        pending[g] = _split_start(f"pair_start_{g}", sent + slots, len(grp), _pair_copies(len(grp), whole=True))
        return pending[g][3]

    def scatter_start_halves(g, ab, after):
        grp = rgroups[g]
        n = len(grp)
        ssem, rsem, arrs, _ = pending[g]
        arrs = _split_wait(f"pair_wait_{g}", arrs, ssem, rsem, _pair_copies(n, whole=True), after)
        pairs = [_mm_tn_half(f"d{w.name}_own", a, b, pos, True, add=s) for w, (a, b), s in zip(grp, ab, arrs[n:])]
        ssem, rsem, pairs, slots, token = _scatter_start(f"scatter_start_{g}", grp, pairs)
        pending[g] = (None, None, pairs, slots, ssem, rsem)
        return token

    def reduce_finish(g, after):
        grp = rgroups[g]
        gcan, sib, pairs, slots, ssem, rsem = pending[g]
        pairs, parts = _scatter_wait(f"scatter_wait_{g}", grp, pairs, slots, ssem, rsem, after)
        if gcan is None:
            return [_final_sum(f"final_sum_{w.name}", w, pos, a, None, q) for w, a, q in zip(grp, pairs, parts)]
        return [_final_sum(f"final_sum_{w.name}", w, pos, a, s, q) for w, a, s, q in zip(grp, gcan, sib, parts)]

    grads = {}
    dx3, dx3b, d_gf, loss_cols = _final_bwd("final_bwd", x3, final_g.reshape(1, d), tgt)
    dgate, dup = _ffn_bwd("ffn_bwd", dx3b, full["w_ffn_down"], gate, up)
    grads["w_ffn_down"] = _mm_tn("dw_ffn_down", act, dx3b, BF16)
    tok = pair_start(3, grads)
    dh2 = _mm_nt("d_h2", [(dgate, full["w_ffn_gate"]), (dup, full["w_ffn_up"])], BF16, tk=d_ff // 4, deps=[tok])
    tok = scatter_start(3, [dh2])
    tok = pair_start_halves(2, [(h2, dgate), (h2, dup)], [tok])
    dx2, dx2b, d_g2 = _rms_bwd("norm2_bwd", x2, norm2_g, dh2, dx3, True, deps=[tok])
    dmerged = _mm_nt("d_merged", [(dx2b, full["w_o"])], BF16, tk=d)
    grads["w_o"] = _mm_tn("dw_o", merged, dx2b, BF16)
    tok = scatter_start_halves(2, [(h2, dgate), (h2, dup)], [grads["w_o"]])
    dya, dyb, dproj, d_bga, d_bgb, d_ps = _merge_bwd("merge_bwd", dmerged, proj, b_gate, ya, yb, pool_scale, deps=[tok])
    dz = _mm_nt("d_z", [(dya, full["w_a_out"])], BF16, tk=d)
    grads["w_a_out"] = _mm_tn("dw_a_out", z, dya, BF16)
    dp = _gmm_nt("d_pool", dyb, wp_full, BF16)
    grads["w_pool"] = _gmm_tn("dw_pool", p, dyb, n_groups, BF16)
    tok = pair_start(1, grads)
    dproj, d_cw, d_cb = _mixer_bwd("mixer_bwd", dz, dp, proj, cw_full, conv_b, dproj, n_conv, n_groups, deps=[tok])
    tok = scatter_start(1, [dproj])
    tok = pair_start_halves(0, [(h1, dproj)], [tok])
    dh1 = _mm_nt("d_h1", [(dproj, w_in_full)], BF16, tk=proj.shape[1] // 4, deps=[tok])
    tok = scatter_start_halves(0, [(h1, dproj)], [dh1])
    grad_x, d_g1 = _rms_bwd("norm1_bwd", xs, norm1_g, dh1, dx2, False, deps=[tok])

    g_big, d_big, m_big, v_big = {}, {}, {}, {}

    def update(wsub, shared):
        out = []
        for w, g in zip(wsub, shared):
            wt, mt, vt = big[w.name]
            g2 = g.reshape(2 * w.R, w.nn)
            go, dl, nm, nv = _adamw(f"adamw_{w.name}", shard2d[w.name], g2, mt.reshape(g2.shape), vt.reshape(g2.shape))
            g_big[w.name], d_big[w.name], m_big[w.name], v_big[w.name] = (a.reshape(wt.shape) for a in (go, dl, nm, nv))
            out.append(nv)
        return out

    after = [grad_x]
    started = []
    for g in (3, 2, 1):
        halves = reduce_finish(g, after)
        share = _share_copies(len(halves))
        ssem, rsem, halves, tok = _split_start(f"share_start_{g}", halves, len(halves), share)
        started.append((g, ssem, rsem, halves, share))
        after = [tok]
    for g, ssem, rsem, halves, share in started:
        after = update(rgroups[g], _split_wait(f"share_wait_{g}", halves, ssem, rsem, share, after))
    share = _share_copies(1)
    ssem, rsem, halves, tok = _split_start("share_start_0", reduce_finish(0, after), 1, share)

    vec_names = ["norm1_g", "b_gate", "conv_w", "conv_b", "pool_scale", "norm2_g", "final_g"]
    vec = {"norm1_g": (norm1_g, m_norm1_g, v_norm1_g), "b_gate": (b_gate, m_b_gate, v_b_gate),
           "conv_w": (cw_loc, m_conv_w[0], v_conv_w[0]), "conv_b": (conv_b, m_conv_b, v_conv_b),
           "pool_scale": (pool_scale, m_pool_scale, v_pool_scale), "norm2_g": (norm2_g, m_norm2_g, v_norm2_g),
           "final_g": tuple(a.reshape(1, d) for a in (final_g, m_final_g, v_final_g))}
    vout = _vector_step(d, n_conv, [d_g1, d_bga, d_bgb, d_cw, d_cb, d_ps, d_g2, d_gf, loss_cols],
                        [vec[n] for n in vec_names], deps=halves)
    update(rgroups[0], _split_wait("share_wait_0", halves, ssem, rsem, share, []))

    shapes = {"conv_w": conv_w.shape, "final_g": final_g.shape}
    g_vec, d_vec, m_vec, v_vec = ({n: vout[4 * i + q].reshape(shapes.get(n, vec[n][0].shape)) for i, n in enumerate(vec_names)}
                                  for q in range(4))
    loss = vout[-1].reshape(())

    order = ["norm1_g", "w_in", "b_gate", "conv_w", "conv_b", "w_a_out", "w_pool", "pool_scale", "w_o", "norm2_g",
             "w_ffn_gate", "w_ffn_up", "w_ffn_down", "final_g"]
    pick = lambda vecs, bigs: [vecs[n] if n in vecs else bigs[n] for n in order]
    return (loss, grad_x.reshape(x.shape), *pick(g_vec, g_big), *pick(d_vec, d_big), *pick(m_vec, m_big),
            *pick(v_vec, v_big))
```

```python
import functools

import jax
import jax.numpy as jnp
from jax import lax
from jax.experimental import pallas as pl
from jax.experimental.pallas import tpu as pltpu

F32, BF16 = jnp.float32, jnp.bfloat16
MESH = pl.DeviceIdType.MESH
ANY = pl.BlockSpec(memory_space=pl.ANY)
VMEM = pl.BlockSpec(memory_space=pltpu.VMEM)
HBM = pl.BlockSpec(memory_space=pltpu.HBM)
SEM = pl.BlockSpec(memory_space=pltpu.SEMAPHORE)
EFFECT = pltpu.SideEffectType.DATAFLOW_SIDE_EFFECTING

EPS = 1e-6
POOL_WINDOWS = (2, 4, 8, 16)
ADAM_LR, ADAM_B1, ADAM_B2, ADAM_EPS, ADAM_WD, ADAM_STEP = 0.001, 0.9, 0.999, 1e-08, 0.01, 10

V7X_VMEM_BYTES = 64 * 1024 * 1024
VMEM_LIMIT = V7X_VMEM_BYTES * 3 // 4
LANES = 128
COL_TILE = 8 * LANES
N_CHIPS = 4
N_DEV = 8

_DIMS = {
    "nn": (((1,), (0,)), ((), ())),
    "nt": (((1,), (1,)), ((), ())),
    "tn": (((0,), (0,)), ((), ())),
}


def _cp(sem):
    return pltpu.CompilerParams(dimension_semantics=sem, vmem_limit_bytes=VMEM_LIMIT)


def _mesh_pos():
    return lax.axis_index("x"), lax.axis_index("y"), lax.axis_index("c")


def _mm(name, pairs, *, mode, grid, out_shape, o_spec, nk=1, kaxis=None, add=None, deps=(), prev=None):
    npair = len(pairs)
    has_add = add is not None

    def body(*refs):
        ab = refs[: 2 * npair]
        pos = 2 * npair
        add_ref = refs[pos] if has_add else None
        pos += int(has_add) + len(deps) + (prev is not None)
        o_ref = refs[pos]
        acc_ref = refs[pos + 1] if nk > 1 else None
        d = None
        for p in range(npair):
            t = lax.dot_general(ab[2 * p][...], ab[2 * p + 1][...], _DIMS[mode], preferred_element_type=F32)
            d = t if d is None else d + t
        if nk == 1:
            if has_add:
                d = d + add_ref[...].astype(F32)
            o_ref[...] = d.astype(o_ref.dtype)
        else:
            k = pl.program_id(kaxis)

            @pl.when(k == 0)
            def _():
                acc_ref[...] = d

            @pl.when(k > 0)
            def _():
                acc_ref[...] += d

            @pl.when(k == nk - 1)
            def _():
                r = acc_ref[...]
                if has_add:
                    r = r + add_ref[...].astype(F32)
                o_ref[...] = r.astype(o_ref.dtype)

    args, specs = [], []
    for a, a_spec, b, b_spec in pairs:
        args += [a, b]
        specs += [a_spec, b_spec]
    if has_add:
        args.append(add[0])
        specs.append(add[1])
    args += list(deps)
    specs += [ANY] * len(deps)
    aliases = {}
    if prev is not None:
        aliases = {len(args): 0}
        args.append(prev)
        specs.append(ANY)
    scratch = []
    if nk > 1:
        blk = [d for d in o_spec.block_shape if d is not None]
        scratch = [pltpu.VMEM(tuple(blk), F32)]
    sem = tuple("arbitrary" if (nk > 1 and ax == kaxis) else "parallel" for ax in range(len(grid)))
    return pl.pallas_call(
        body, name=name, grid=grid, in_specs=specs, out_specs=o_spec, out_shape=out_shape,
        scratch_shapes=scratch, input_output_aliases=aliases, compiler_params=_cp(sem),
    )(*args)


def _tile_span(n_tiles, part):
    if part is None:
        return 0, n_tiles
    p, of = part
    return p * n_tiles // of, (p + 1) * n_tiles // of


def _tile(n, pref):
    if n <= pref:
        return n
    for t in range(pref, 0, -LANES):
        if t % LANES == 0 and n % t == 0:
            return t
    raise ValueError(f"no tile for {n}")


def _mm_nn(name, a, b, out_dtype, add=None, tk=None, deps=(), part=None, prev=None, tiles=None):
    m, kk = a.shape
    n = b.shape[1]
    tm, tn = _tile(m, 1024), _tile(n, COL_TILE)
    if tiles is not None:
        tm, tn = _tile(m, tiles[0]), _tile(n, tiles[1])
    out_shape = jax.ShapeDtypeStruct((m, n), out_dtype)
    if tk is None or tk == kk:
        j0, j1 = _tile_span(n // tn, part)
        grid = (m // tm, j1 - j0)
        pairs = [(a, pl.BlockSpec((tm, kk), lambda i, j: (i, 0)), b, pl.BlockSpec((kk, tn), lambda i, j: (0, j0 + j)))]
        o_spec = pl.BlockSpec((tm, tn), lambda i, j: (i, j0 + j))
        add_ = None if add is None else (add, pl.BlockSpec((tm, tn), lambda i, j: (i, j0 + j)))
        return _mm(name, pairs, mode="nn", grid=grid, out_shape=out_shape, o_spec=o_spec, add=add_, deps=deps, prev=prev)
    tn = _tile(n, 1024)
    nk = kk // tk
    grid = (m // tm, n // tn, nk)
    pairs = [(a, pl.BlockSpec((tm, tk), lambda i, j, k: (i, k)), b, pl.BlockSpec((tk, tn), lambda i, j, k: (k, j)))]
    o_spec = pl.BlockSpec((tm, tn), lambda i, j, k: (i, j))
    add_ = None if add is None else (add, pl.BlockSpec((tm, tn), lambda i, j, k: (i, j)))
    return _mm(name, pairs, mode="nn", grid=grid, out_shape=out_shape, o_spec=o_spec, nk=nk, kaxis=2, add=add_, deps=deps)


def _mm_nt(name, abs_, out_dtype, tk, deps=()):
    m, kk = abs_[0][0].shape
    n = abs_[0][1].shape[0]
    tm = _tile(m, 1024)
    nk = kk // tk
    tn = _tile(n, COL_TILE if nk == 1 else 1024)
    out_shape = jax.ShapeDtypeStruct((m, n), out_dtype)
    if nk == 1:
        grid = (m // tm, n // tn)
        pairs = [(a, pl.BlockSpec((tm, kk), lambda i, j: (i, 0)), b, pl.BlockSpec((tn, kk), lambda i, j: (j, 0)))
                 for a, b in abs_]
        o_spec = pl.BlockSpec((tm, tn), lambda i, j: (i, j))
        return _mm(name, pairs, mode="nt", grid=grid, out_shape=out_shape, o_spec=o_spec, deps=deps)
    grid = (m // tm, n // tn, nk)
    pairs = [(a, pl.BlockSpec((tm, tk), lambda i, j, k: (i, k)), b, pl.BlockSpec((tn, tk), lambda i, j, k: (j, k)))
             for a, b in abs_]
    o_spec = pl.BlockSpec((tm, tn), lambda i, j, k: (i, j))
    return _mm(name, pairs, mode="nt", grid=grid, out_shape=out_shape, o_spec=o_spec, nk=nk, kaxis=2, deps=deps)


def _mm_tn(name, a, b, out_dtype, deps=()):
    t, m = a.shape
    n = b.shape[1]
    tm, tn = _tile(m, 512), _tile(n, 2048)
    if n > m:
        grid = (n // tn, m // tm)
        a_map, b_map, o_map = (lambda j, i: (0, i)), (lambda j, i: (0, j)), (lambda j, i: (i, j))
    else:
        grid = (m // tm, n // tn)
        a_map, b_map, o_map = (lambda i, j: (0, i)), (lambda i, j: (0, j)), (lambda i, j: (i, j))
    pairs = [(a, pl.BlockSpec((t, tm), a_map), b, pl.BlockSpec((t, tn), b_map))]
    o_spec = pl.BlockSpec((tm, tn), o_map)
    return _mm(name, pairs, mode="tn", grid=grid, out_shape=jax.ShapeDtypeStruct((m, n), out_dtype), o_spec=o_spec,
               deps=deps)


def _mm_tn_half(name, a, b, pos, mine, add=None, deps=()):
    t, m = a.shape
    r, n = m // 2, b.shape[1]
    tm, tn = _tile(r, 512), _tile(n, 2048)
    nbi = r // tm
    half = (lambda pos: pos[0]) if mine else (lambda pos: 1 - pos[0])
    if n > r:
        grid, ij = (n // tn, nbi), (lambda g0, g1: (g1, g0))
    else:
        grid, ij = (nbi, n // tn), (lambda g0, g1: (g0, g1))
    has_add = add is not None

    def body(pos_ref, a_ref, b_ref, *rest):
        d = lax.dot_general(a_ref[...], b_ref[...], _DIMS["tn"], preferred_element_type=F32)
        if has_add:
            d = d + rest[0][...].astype(F32)
        rest[-1][...] = d.astype(BF16)

    o_spec = pl.BlockSpec((None, tm, tn), lambda g0, g1, pos: (0, *ij(g0, g1)))
    grid_spec = pltpu.PrefetchScalarGridSpec(
        num_scalar_prefetch=1, grid=grid,
        in_specs=[pl.BlockSpec((t, tm), lambda g0, g1, pos: (0, half(pos) * nbi + ij(g0, g1)[0])),
                  pl.BlockSpec((t, tn), lambda g0, g1, pos: (0, ij(g0, g1)[1]))]
        + ([o_spec] if has_add else []) + [ANY] * len(deps),
        out_specs=o_spec)
    return pl.pallas_call(body, name=name, grid_spec=grid_spec, out_shape=jax.ShapeDtypeStruct((1, r, n), BF16),
                          compiler_params=_cp(("parallel",) * 2))(pos, a, b, *([add] if has_add else []), *deps)


def _proj_piece(name, h, w, prev, kvec, base, count, deps=()):
    t, kk = h.shape
    own = w.dtype == F32
    nn = w.shape[1] if own else w.shape[1] // N_CHIPS
    tm, tn = _tile(t, 1024), _tile(nn, COL_TILE)
    nb = nn // tn

    def body(kv_ref, h_ref, w_ref, *rest):
        rest[-1][...] = lax.dot_general(h_ref[...], w_ref[...].astype(BF16), _DIMS["nn"],
                                        preferred_element_type=F32).astype(BF16)

    cols = lambda s, i, j, kv: (0, j) if own else (0, kv[base + s] * nb + j)
    extra = ([] if prev is None else [prev]) + list(deps)
    grid_spec = pltpu.PrefetchScalarGridSpec(
        num_scalar_prefetch=1, grid=(count, t // tm, nb),
        in_specs=[pl.BlockSpec((tm, kk), lambda s, i, j, kv: (i, 0)), pl.BlockSpec((kk, tn), cols)] + [ANY] * len(extra),
        out_specs=pl.BlockSpec((tm, tn), lambda s, i, j, kv: (i, kv[base + s] * nb + j)))
    return pl.pallas_call(body, name=name, grid_spec=grid_spec, out_shape=jax.ShapeDtypeStruct((t, N_CHIPS * nn), BF16),
                          input_output_aliases={} if prev is None else {3: 0},
                          compiler_params=_cp(("parallel",) * 3))(kvec, h, w, *extra)


def _gmm_nn(name, p, w, out_dtype):
    t = p.shape[0]
    g, cg, dg = w.shape
    tm = _tile(t, 1024)
    pairs = [(p, pl.BlockSpec((tm, cg), lambda i, j: (i, j)), w, pl.BlockSpec((None, cg, dg), lambda i, j: (j, 0, 0)))]
    o_spec = pl.BlockSpec((tm, dg), lambda i, j: (i, j))
    return _mm(name, pairs, mode="nn", grid=(t // tm, g), out_shape=jax.ShapeDtypeStruct((t, g * dg), out_dtype),
               o_spec=o_spec)


def _gmm_nt(name, dy, w, out_dtype):
    t = dy.shape[0]
    g, cg, dg = w.shape
    tm = _tile(t, 1024)
    pairs = [(dy, pl.BlockSpec((tm, dg), lambda i, j: (i, j)), w, pl.BlockSpec((None, cg, dg), lambda i, j: (j, 0, 0)))]
    o_spec = pl.BlockSpec((tm, cg), lambda i, j: (i, j))
    return _mm(name, pairs, mode="nt", grid=(t // tm, g), out_shape=jax.ShapeDtypeStruct((t, g * cg), out_dtype),
               o_spec=o_spec)


def _gmm_tn(name, p, dy, g, out_dtype):
    t = p.shape[0]
    cg, dg = p.shape[1] // g, dy.shape[1] // g
    pairs = [(p, pl.BlockSpec((t, cg), lambda j: (0, j)), dy, pl.BlockSpec((t, dg), lambda j: (0, j)))]
    o_spec = pl.BlockSpec((None, cg, dg), lambda j: (j, 0, 0))
    return _mm(name, pairs, mode="tn", grid=(g,), out_shape=jax.ShapeDtypeStruct((g, cg, dg), out_dtype), o_spec=o_spec)


ROW_TILE = 256


def _rows(t):
    return _tile8(t, ROW_TILE)


def _tile8(n, pref):
    if n <= pref:
        return n
    for t in range(pref, 0, -8):
        if n % t == 0:
            return t
    raise ValueError(f"no row tile for {n}")


def _cast_place(name, w, pos, shard, deps=()):
    tr = _tile8(w.R, 512)
    if w.colshard:
        o_map = lambda h, i, pos: (0, h, i, pos[1])
    else:
        o_map = lambda h, i, pos: (pos[1], h, i, 0)

    def body(pos_ref, w_ref, *rest):
        rest[-1][...] = w_ref[...].astype(BF16)

    grid_spec = pltpu.PrefetchScalarGridSpec(
        num_scalar_prefetch=1, grid=(2, w.R // tr),
        in_specs=[pl.BlockSpec((None, tr, w.nn), lambda h, i, pos: (h, i, 0))] + [ANY] * len(deps),
        out_specs=pl.BlockSpec((None, None, tr, w.nn), o_map))
    return pl.pallas_call(body, name=name, grid_spec=grid_spec, out_shape=jax.ShapeDtypeStruct((w.P, 2, w.R, w.N), BF16),
                          compiler_params=_cp(("parallel", "parallel")))(pos, shard, *deps)


def _rms_fwd(name, x, g, deps=()):
    t, d = x.shape
    tm = _rows(t)

    def body(x_ref, g_ref, *rest):
        xf = x_ref[...]
        r = lax.rsqrt(jnp.mean(xf * xf, axis=-1, keepdims=True) + EPS)
        rest[-1][...] = (xf * r * g_ref[...]).astype(BF16)

    return pl.pallas_call(
        body, name=name, grid=(t // tm,),
        in_specs=[pl.BlockSpec((tm, d), lambda i: (i, 0)), pl.BlockSpec((1, d), lambda i: (0, 0))] + [ANY] * len(deps),
        out_specs=pl.BlockSpec((tm, d), lambda i: (i, 0)), out_shape=jax.ShapeDtypeStruct((t, d), BF16),
        compiler_params=_cp(("parallel",)),
    )(x, g, *deps)


def _rms_bwd(name, x, g, dh, dres, want_bf16, deps=()):
    t, d = x.shape
    tm = _rows(t)

    def body(x_ref, g_ref, dh_ref, dres_ref, *rest):
        rest = rest[len(deps):]
        dx_ref, rest = rest[0], rest[1:]
        dg_ref = rest[-1]
        xf = x_ref[...]
        r = lax.rsqrt(jnp.mean(xf * xf, axis=-1, keepdims=True) + EPS)
        xh = xf * r
        dhf = dh_ref[...].astype(F32)
        dxh = dhf * g_ref[...]
        m = jnp.mean(dxh * xh, axis=-1, keepdims=True)
        dx = dres_ref[...] + r * (dxh - xh * m)
        dx_ref[...] = dx
        if want_bf16:
            rest[0][...] = dx.astype(BF16)

        @pl.when(pl.program_id(0) == 0)
        def _():
            dg_ref[...] = jnp.zeros_like(dg_ref)

        dg_ref[...] += jnp.sum(dhf * xh, axis=0, keepdims=True)

    row = pl.BlockSpec((tm, d), lambda i: (i, 0))
    vec = pl.BlockSpec((1, d), lambda i: (0, 0))
    out_specs = [row] + ([row] if want_bf16 else []) + [vec]
    out_shape = ([jax.ShapeDtypeStruct((t, d), F32)] + ([jax.ShapeDtypeStruct((t, d), BF16)] if want_bf16 else [])
                 + [jax.ShapeDtypeStruct((1, d), F32)])
    return pl.pallas_call(body, name=name, grid=(t // tm,), in_specs=[row, vec, row, row] + [ANY] * len(deps),
                          out_specs=out_specs, out_shape=out_shape, compiler_params=_cp(("arbitrary",)))(x, g, dh, dres, *deps)


def _final_bwd(name, x3, gf, tgt):
    t, d = x3.shape
    tm = _rows(t)

    def body(x_ref, g_ref, t_ref, dx_ref, dxb_ref, dg_ref, lc_ref):
        xf = x_ref[...]
        g = g_ref[...]
        r = lax.rsqrt(jnp.mean(xf * xf, axis=-1, keepdims=True) + EPS)
        xh = xf * r
        diff = xh * g - t_ref[...]
        dy = diff * (1.0 / d)
        dxh = dy * g
        m = jnp.mean(dxh * xh, axis=-1, keepdims=True)
        dx = r * (dxh - xh * m)
        dx_ref[...] = dx
        dxb_ref[...] = dx.astype(BF16)

        @pl.when(pl.program_id(0) == 0)
        def _():
            dg_ref[...] = jnp.zeros_like(dg_ref)
            lc_ref[...] = jnp.zeros_like(lc_ref)

        dg_ref[...] += jnp.sum(dy * xh, axis=0, keepdims=True)
        lc_ref[...] += jnp.sum(diff * diff, axis=0, keepdims=True) * (0.5 / d)

    row = pl.BlockSpec((tm, d), lambda i: (i, 0))
    vec = pl.BlockSpec((1, d), lambda i: (0, 0))
    return pl.pallas_call(
        body, name=name, grid=(t // tm,), in_specs=[row, vec, row], out_specs=[row, row, vec, vec],
        out_shape=[jax.ShapeDtypeStruct((t, d), F32), jax.ShapeDtypeStruct((t, d), BF16),
                   jax.ShapeDtypeStruct((1, d), F32), jax.ShapeDtypeStruct((1, d), F32)],
        compiler_params=_cp(("arbitrary",)),
    )(x3, gf, tgt)


def _shift_down(v, k, t_idx):
    return jnp.where(t_idx >= k, pltpu.roll(v, k, 0), 0.0)


def _shift_up(v, k, t_idx):
    n = v.shape[0]
    return jnp.where(t_idx < n - k, pltpu.roll(v, n - k, 0), 0.0)


def _window_sums(v, shift, t_idx, grp):
    s = v + shift(v, 1, t_idx)
    out = s
    for lvl in range(1, len(POOL_WINDOWS)):
        s = s + shift(s, 1 << lvl, t_idx)
        out = jnp.where(grp >= lvl, s, out)
    return out


def _window_weight(t_idx, grp):
    return 1.0 / jnp.minimum(t_idx[:, :1] + 1, jnp.left_shift(2, grp)).astype(F32)


MIX_COLS = 256


def _mixer_fwd(name, proj, cw, cb, n_conv, n_groups, deps=()):
    t = proj.shape[0]
    nb = n_conv // MIX_COLS
    per_group = n_conv // n_groups // MIX_COLS

    def body(ba_ref, ca_ref, va_ref, vb_ref, cw_ref, cb_ref, *rest):
        z_ref, p_ref = rest[len(deps):]
        t_idx = lax.broadcasted_iota(jnp.int32, (t, MIX_COLS), 0)
        q = ca_ref[...].astype(F32) * va_ref[...].astype(F32)
        w = cw_ref[...]
        u = cb_ref[...] + w[0:1] * _shift_down(q, 2, t_idx) + w[1:2] * _shift_down(q, 1, t_idx) + w[2:3] * q
        z_ref[...] = (ba_ref[...].astype(F32) * u).astype(BF16)
        grp = pl.program_id(0) // per_group
        v = vb_ref[...].astype(F32)
        p_ref[...] = (_window_sums(v, _shift_down, t_idx, grp) * _window_weight(t_idx, grp) - v).astype(BF16)

    col = lambda s: pl.BlockSpec((t, MIX_COLS), lambda j: (0, s * nb + j))
    return pl.pallas_call(
        body, name=name, grid=(nb,),
        in_specs=[col(0), col(1), col(2), col(3), pl.BlockSpec((3, MIX_COLS), lambda j: (0, j)),
                  pl.BlockSpec((1, MIX_COLS), lambda j: (0, j))] + [ANY] * len(deps),
        out_specs=[col(0), col(0)],
        out_shape=[jax.ShapeDtypeStruct((t, n_conv), BF16), jax.ShapeDtypeStruct((t, n_conv), BF16)],
        compiler_params=_cp(("parallel",)),
    )(proj, proj, proj, proj, cw, cb, *deps)


def _mixer_bwd(name, dz, dp, proj, cw, cb, dproj, n_conv, n_groups, deps=()):
    t = proj.shape[0]
    nb = n_conv // MIX_COLS
    per_group = n_conv // n_groups // MIX_COLS

    def body(dz_ref, dp_ref, ba_ref, ca_ref, va_ref, cw_ref, cb_ref, _, *rest):
        o_ref, dcw_ref, dcb_ref, scr = rest[len(deps):]
        s = pl.program_id(1)

        @pl.when(s == 0)
        def _():
            t_idx = lax.broadcasted_iota(jnp.int32, (t, MIX_COLS), 0)
            ca, va = ca_ref[...].astype(F32), va_ref[...].astype(F32)
            q = ca * va
            q1, q2 = _shift_down(q, 1, t_idx), _shift_down(q, 2, t_idx)
            w = cw_ref[...]
            u = cb_ref[...] + w[0:1] * q2 + w[1:2] * q1 + w[2:3] * q
            dzf = dz_ref[...].astype(F32)
            du = dzf * ba_ref[...].astype(F32)
            scr[0] = (dzf * u).astype(BF16)
            dq = w[2:3] * du + w[1:2] * _shift_up(du, 1, t_idx) + w[0:1] * _shift_up(du, 2, t_idx)
            scr[1] = (dq * va).astype(BF16)
            scr[2] = (dq * ca).astype(BF16)
            dcb_ref[...] = jnp.sum(du, axis=0, keepdims=True)
            dcw_ref[0:1, :] = jnp.sum(du * q2, axis=0, keepdims=True)
            dcw_ref[1:2, :] = jnp.sum(du * q1, axis=0, keepdims=True)
            dcw_ref[2:3, :] = jnp.sum(du * q, axis=0, keepdims=True)
            grp = pl.program_id(0) // per_group
            dpf = dp_ref[...].astype(F32)
            e = dpf * _window_weight(t_idx, grp)
            scr[3] = (_window_sums(e, _shift_up, t_idx, grp) - dpf).astype(BF16)

        o_ref[...] = scr[s]

    col = lambda c: pl.BlockSpec((t, MIX_COLS), lambda j, s: (0, c * nb + j))
    own = pl.BlockSpec((t, MIX_COLS), lambda j, s: (0, j))
    return pl.pallas_call(
        body, name=name, grid=(nb, 4),
        in_specs=[own, own, col(0), col(1), col(2), pl.BlockSpec((3, MIX_COLS), lambda j, s: (0, j)),
                  pl.BlockSpec((1, MIX_COLS), lambda j, s: (0, j)), ANY] + [ANY] * len(deps),
        out_specs=[pl.BlockSpec((t, MIX_COLS), lambda j, s: (0, s * nb + j)),
                   pl.BlockSpec((3, MIX_COLS), lambda j, s: (0, j)), pl.BlockSpec((1, MIX_COLS), lambda j, s: (0, j))],
        out_shape=[jax.ShapeDtypeStruct(dproj.shape, BF16), jax.ShapeDtypeStruct((3, n_conv), F32),
                   jax.ShapeDtypeStruct((1, n_conv), F32)],
        scratch_shapes=[pltpu.VMEM((4, t, MIX_COLS), BF16)],
        input_output_aliases={7: 0},
        compiler_params=_cp(("arbitrary", "arbitrary")),
    )(dz, dp, proj, proj, proj, cw, cb, dproj, *deps)


def _merge_fwd(name, proj, bg, ya, yb, ps):
    t, d = ya.shape
    tm = _rows(t)

    def body(gab_ref, bg_ref, ya_ref, yb_ref, ps_ref, o_ref):
        gab = gab_ref[...].astype(F32) + bg_ref[...]
        sa, sb = jax.nn.sigmoid(gab[:, :d]), jax.nn.sigmoid(gab[:, d:])
        o_ref[...] = (sa * ya_ref[...].astype(F32) + sb * (yb_ref[...].astype(F32) * ps_ref[...])).astype(BF16)

    row = pl.BlockSpec((tm, d), lambda i: (i, 0))
    return pl.pallas_call(
        body, name=name, grid=(t // tm,),
        in_specs=[pl.BlockSpec((tm, 2 * d), lambda i: (i, 1)), pl.BlockSpec((1, 2 * d), lambda i: (0, 0)), row, row,
                  pl.BlockSpec((1, d), lambda i: (0, 0))],
        out_specs=row, out_shape=jax.ShapeDtypeStruct((t, d), BF16), compiler_params=_cp(("parallel",)),
    )(proj, bg, ya, yb, ps)


def _merge_bwd(name, dm, proj, bg, ya, yb, ps, deps=()):
    t, d = ya.shape
    tm = _rows(t)

    def body(dm_ref, gab_ref, bg_ref, ya_ref, yb_ref, ps_ref, *rest):
        dya_ref, dyb_ref, dg_ref, dba_ref, dbb_ref, dps_ref = rest[len(deps):]
        gab = gab_ref[...].astype(F32) + bg_ref[...]
        sa, sb = jax.nn.sigmoid(gab[:, :d]), jax.nn.sigmoid(gab[:, d:])
        dmf = dm_ref[...].astype(F32)
        ybf, ps_ = yb_ref[...].astype(F32), ps_ref[...]
        dya_ref[...] = (dmf * sa).astype(BF16)
        dyb = dmf * sb
        dyb_ref[...] = (dyb * ps_).astype(BF16)
        dga = dmf * ya_ref[...].astype(F32) * sa * (1.0 - sa)
        dgb = dmf * (ybf * ps_) * sb * (1.0 - sb)
        dg_ref[:, :d] = dga.astype(BF16)
        dg_ref[:, d:] = dgb.astype(BF16)

        @pl.when(pl.program_id(0) == 0)
        def _():
            dba_ref[...] = jnp.zeros_like(dba_ref)
            dbb_ref[...] = jnp.zeros_like(dbb_ref)
            dps_ref[...] = jnp.zeros_like(dps_ref)

        dba_ref[...] += jnp.sum(dga, axis=0, keepdims=True)
        dbb_ref[...] += jnp.sum(dgb, axis=0, keepdims=True)
        dps_ref[...] += jnp.sum(dyb * ybf, axis=0, keepdims=True)

    row = pl.BlockSpec((tm, d), lambda i: (i, 0))
    vec = pl.BlockSpec((1, d), lambda i: (0, 0))
    gates = pl.BlockSpec((tm, 2 * d), lambda i: (i, 1))
    return pl.pallas_call(
        body, name=name, grid=(t // tm,),
        in_specs=[row, gates, pl.BlockSpec((1, 2 * d), lambda i: (0, 0)), row, row, vec] + [ANY] * len(deps),
        out_specs=[row, row, gates, vec, vec, vec],
        out_shape=[jax.ShapeDtypeStruct((t, d), BF16), jax.ShapeDtypeStruct((t, d), BF16),
                   jax.ShapeDtypeStruct(proj.shape, BF16), jax.ShapeDtypeStruct((1, d), F32),
                   jax.ShapeDtypeStruct((1, d), F32), jax.ShapeDtypeStruct((1, d), F32)],
        compiler_params=_cp(("arbitrary",)),
    )(dm, proj, bg, ya, yb, ps, *deps)


def _ffn_up_act(name, h, w_up, gate, part=None, prev=None, deps=()):
    t, d = h.shape
    f = w_up.shape[1]
    tm, tf = _tile(t, 1024), _tile(f, 512)
    j0, j1 = _tile_span(f // tf, part)
    n_prev = 0 if prev is None else 2
    extra = ([] if prev is None else list(prev)) + list(deps)

    def body(h_ref, w_ref, g_ref, *rest):
        u_ref, a_ref = rest[len(extra):]
        u = lax.dot_general(h_ref[...], w_ref[...], _DIMS["nn"], preferred_element_type=F32)
        g = g_ref[...].astype(F32)
        u_ref[...] = u.astype(BF16)
        a_ref[...] = (g * jax.nn.sigmoid(g) * u).astype(BF16)

    blk = pl.BlockSpec((tm, tf), lambda i, j: (i, j0 + j))
    shp = jax.ShapeDtypeStruct((t, f), BF16)
    return pl.pallas_call(
        body, name=name, grid=(t // tm, j1 - j0),
        in_specs=[pl.BlockSpec((tm, d), lambda i, j: (i, 0)), pl.BlockSpec((d, tf), lambda i, j: (0, j0 + j)), blk]
        + [ANY] * len(extra),
        out_specs=[blk, blk], out_shape=[shp, shp], input_output_aliases={3 + i: i for i in range(n_prev)},
        compiler_params=_cp(("parallel", "parallel")))(h, w_up, gate, *extra)


def _ffn_bwd(name, dy, w_down, gate, up):
    t, d = dy.shape
    f = w_down.shape[0]
    tm, tf = _tile(t, 1024), _tile(f, 512)

    def body(dy_ref, w_ref, g_ref, u_ref, dg_ref, du_ref):
        da = lax.dot_general(dy_ref[...], w_ref[...], _DIMS["nt"], preferred_element_type=F32)
        g = g_ref[...].astype(F32)
        s = jax.nn.sigmoid(g)
        du_ref[...] = (da * (g * s)).astype(BF16)
        dg_ref[...] = (da * u_ref[...].astype(F32) * (s * (1.0 + g * (1.0 - s)))).astype(BF16)

    blk = pl.BlockSpec((tm, tf), lambda i, j: (i, j))
    shp = jax.ShapeDtypeStruct((t, f), BF16)
    return pl.pallas_call(
        body, name=name, grid=(t // tm, f // tf),
        in_specs=[pl.BlockSpec((tm, d), lambda i, j: (i, 0)), pl.BlockSpec((tf, d), lambda i, j: (j, 0)), blk, blk],
        out_specs=[blk, blk], out_shape=[shp, shp], compiler_params=_cp(("parallel", "parallel")))(dy, w_down, gate, up)


def _adamw_math(w, g, m, v):
    m = ADAM_B1 * m + (1.0 - ADAM_B1) * g
    v = ADAM_B2 * v + (1.0 - ADAM_B2) * (g * g)
    m_hat = m / (1.0 - ADAM_B1 ** ADAM_STEP)
    v_hat = v / (1.0 - ADAM_B2 ** ADAM_STEP)
    delta = -ADAM_LR * (m_hat / (jnp.sqrt(v_hat) + ADAM_EPS) + ADAM_WD * w)
    return delta, m, v


def _adamw(name, w, g, m, v):
    r, c = w.shape
    tr = _tile8(r, 512 if c <= 1024 else 256)

    def body(w_ref, g_ref, m_ref, v_ref, go_ref, d_ref, nm_ref, nv_ref):
        g = g_ref[...]
        go_ref[...] = g
        d_ref[...], nm_ref[...], nv_ref[...] = _adamw_math(w_ref[...], g, m_ref[...], v_ref[...])

    blk = pl.BlockSpec((tr, c), lambda i: (i, 0))
    shp = jax.ShapeDtypeStruct((r, c), F32)
    return pl.pallas_call(body, name=name, grid=(r // tr,), in_specs=[blk] * 4, out_specs=[blk] * 4,
                          out_shape=[shp] * 4, compiler_params=_cp(("parallel",)))(w, g, m, v)


class _Weight:
    def __init__(self, name, rows, cols, colshard):
        self.name, self.colshard = name, colshard
        self.R, self.nn = rows // 2, cols
        self.P = 1 if colshard else N_CHIPS
        self.N = N_CHIPS * cols if colshard else cols

    def cols(self, k):
        return pl.ds(pl.multiple_of(k * self.nn, LANES), self.nn)

    def shard(self, ref, k):
        return ref.at[0, :, :, self.cols(k)] if self.colshard else ref.at[k]

    def half(self, ref, k, h):
        return ref.at[0, h, :, self.cols(k)] if self.colshard else ref.at[k, h]

    def quarter(self, ref, k, h, q):
        return self.half(ref, k, h).at[pl.ds(q * (self.R // 2), self.R // 2), :]

    def part(self, ref, k):
        return ref.at[0, :, self.cols(k)] if self.colshard else ref.at[k]


def _remote(src, dst, ssem, rsem, dev):
    return pltpu.make_async_remote_copy(src_ref=src, dst_ref=dst, send_sem=ssem, recv_sem=rsem, device_id=dev,
                                        device_id_type=MESH)


def _other_chips(x, y):
    chips = [(1 - x, y), (x, 1 - y), (1 - x, 1 - y)]
    return chips, [2 * cx + cy for cx, cy in chips]


def _hbm(a):
    return pltpu.with_memory_space_constraint(a, pltpu.HBM)


def _gather_start(name, groups, lands, after=()):
    flat = [w for grp in groups for w in grp]
    nw, ng = len(flat), len(groups)

    def body(*refs):
        land = refs[:nw]
        sems = refs[nw + len(after):nw + len(after) + 2 * ng]
        token = refs[2 * nw + len(after) + 2 * ng]
        x, y, c = _mesh_pos()
        k_me = 2 * x + y
        chips, _ = _other_chips(x, y)
        i = 0
        for g, grp in enumerate(groups):
            for wi, w in enumerate(grp):
                mine = w.half(land[i], k_me, c)
                for j, chip in enumerate(chips):
                    _remote(mine, mine, sems[2 * g].at[3 * wi + j], sems[2 * g + 1].at[3 * wi + j], (*chip, c)).start()
                i += 1
        token[...] = jnp.zeros_like(token)

    sem_shapes = []
    for grp in groups:
        sem_shapes += [pltpu.SemaphoreType.DMA((3 * len(grp),))] * 2
    out = pl.pallas_call(
        body, name=name, in_specs=[HBM] * nw + [ANY] * len(after),
        out_specs=[SEM] * (2 * ng) + [HBM] * nw + [VMEM],
        out_shape=sem_shapes + [pltpu.HBM(a.shape, a.dtype) for a in lands] + [jax.ShapeDtypeStruct((8, LANES), F32)],
        input_output_aliases={i: 2 * ng + i for i in range(nw)},
        compiler_params=pltpu.CompilerParams(has_side_effects=EFFECT),
    )(*[_hbm(a) for a in lands], *after)
    sems = [(out[2 * g], out[2 * g + 1]) for g in range(ng)]
    return sems, list(out[2 * ng:2 * ng + nw]), out[-1]


def _gather_wait(name, grp, lands, ssem, rsem, after):
    n = len(grp)

    def body(*refs):
        land, ssem_ref, rsem_ref = refs[:n], refs[n], refs[n + 1]
        x, y, c = _mesh_pos()
        k_me = 2 * x + y
        chips, ks = _other_chips(x, y)
        for wi, w in enumerate(grp):
            for j, chip in enumerate(chips):
                cp = _remote(w.half(land[wi], k_me, c), w.half(land[wi], ks[j], c), ssem_ref.at[3 * wi + j],
                             rsem_ref.at[3 * wi + j], (*chip, c))
                cp.wait_send()
                cp.wait_recv()

    return pl.pallas_call(
        body, name=name, in_specs=[HBM] * n + [SEM, SEM, ANY], out_specs=[HBM] * n,
        out_shape=[pltpu.HBM(a.shape, a.dtype) for a in lands], input_output_aliases={i: i for i in range(n)},
        compiler_params=pltpu.CompilerParams(has_side_effects=EFFECT),
    )(*lands, ssem, rsem, after)


def _split_start(name, arrays, n, copies, after=()):
    na = len(arrays)

    def body(*refs):
        ssem, rsem, token = refs[na + len(after):][0], refs[na + len(after):][1], refs[2 * na + len(after) + 2]
        for i, (src, dst, dev, _) in enumerate(copies(refs[:na], *_mesh_pos())):
            _remote(src, dst, ssem.at[i], rsem.at[i], dev).start()
        token[...] = jnp.zeros_like(token)

    out = pl.pallas_call(
        body, name=name, in_specs=[HBM] * na + [ANY] * len(after), out_specs=[SEM, SEM] + [HBM] * na + [VMEM],
        out_shape=[pltpu.SemaphoreType.DMA((n,))] * 2 + [pltpu.HBM(a.shape, a.dtype) for a in arrays]
        + [jax.ShapeDtypeStruct((8, LANES), F32)],
        input_output_aliases={i: 2 + i for i in range(na)},
        compiler_params=pltpu.CompilerParams(has_side_effects=EFFECT),
    )(*[_hbm(a) for a in arrays], *after)
    return out[0], out[1], list(out[2:2 + na]), out[-1]


def _split_wait(name, arrays, ssem, rsem, copies, after):
    na = len(arrays)

    def body(*refs):
        for i, (src, _, dev, dst) in enumerate(copies(refs[:na], *_mesh_pos())):
            cp = _remote(src, dst, refs[na].at[i], refs[na + 1].at[i], dev)
            cp.wait_send()
            cp.wait_recv()

    return list(pl.pallas_call(
        body, name=name, in_specs=[HBM] * na + [SEM, SEM] + [ANY] * len(after), out_specs=[HBM] * na,
        out_shape=[pltpu.HBM(a.shape, a.dtype) for a in arrays], input_output_aliases={i: i for i in range(na)},
        compiler_params=pltpu.CompilerParams(has_side_effects=EFFECT),
    )(*arrays, ssem, rsem, *after))


def _pass_copies(grp, rels=(0, 1, 2)):
    def copies(land, x, y, c):
        _, ks = _other_chips(x, y)
        return [(w.half(land[wi], ks[j], c), w.half(land[wi], ks[j], c), (x, y, 1 - c), w.half(land[wi], ks[j], 1 - c))
                for wi, w in enumerate(grp) for j in rels]
    copies.n = len(grp) * len(rels)
    return copies


def _near_copies(grp):
    def copies(land, x, y, c):
        chips, ks = _other_chips(x, y)
        out = []
        for wi, w in enumerate(grp):
            mine = w.half(land[wi], 2 * x + y, c)
            out += [(mine, mine, (*chips[j], c), w.half(land[wi], ks[j], c)) for j in (0, 1)]
        return out
    copies.n = 2 * len(grp)
    return copies


def _far_copies(grp):
    def copies(land, x, y, c):
        chips, ks = _other_chips(x, y)
        out = []
        for wi, w in enumerate(grp):
            for j in (0, 1):
                q = w.quarter(land[wi], ks[j], c, j)
                out.append((q, q, (*chips[1 - j], c), w.quarter(land[wi], ks[2], c, j)))
        return out
    copies.n = 2 * len(grp)
    return copies


def _pair_copies(n, whole=False):
    def copies(refs, x, y, c):
        return [(refs[i] if whole else refs[i].at[:, 1 - c], refs[n + i], (x, y, 1 - c), refs[n + i]) for i in range(n)]
    return copies


def _share_copies(n):
    def copies(refs, x, y, c):
        return [(refs[i].at[c], refs[i].at[c], (x, y, 1 - c), refs[i].at[1 - c]) for i in range(n)]
    return copies


def _gather_conv_w(cw, thru):
    ncw = cw.shape[1]

    def body(cw_ref, _, out_ref, __, ssem, rsem):
        x, y, c = _mesh_pos()
        k_me = 2 * x + y
        chips, ks = _other_chips(x, y)
        cols = lambda k: out_ref.at[:, pl.ds(pl.multiple_of(k * ncw, LANES), ncw)]
        cps = [_remote(cw_ref, cols(k_me), ssem.at[j], rsem.at[j], (*chip, c)) for j, chip in enumerate(chips)]
        for cp in cps:
            cp.start()
        for k in range(N_CHIPS):
            @pl.when(k_me == k)
            def _():
                out_ref[:, k * ncw:(k + 1) * ncw] = cw_ref[...]
        for j in range(3):
            _remote(cw_ref, cols(ks[j]), ssem.at[j], rsem.at[j], (*chips[j], c)).wait_recv()
        for cp in cps:
            cp.wait_send()

    return pl.pallas_call(
        body, name="gather_conv_w", in_specs=[VMEM, ANY], out_specs=[VMEM, ANY],
        out_shape=[jax.ShapeDtypeStruct((3, N_CHIPS * ncw), F32), jax.ShapeDtypeStruct(thru.shape, thru.dtype)],
        scratch_shapes=[pltpu.SemaphoreType.DMA((3,)), pltpu.SemaphoreType.DMA((3,))],
        input_output_aliases={1: 1},
    )(cw, thru)


def _grad_tiles(w, n):
    return _tile8(w.R, 512) if w.R <= 512 else w.R // 2, _tile(n, 2048)


def _pair_sum(name, w, pos, grad, got):
    tr, tn = _grad_tiles(w, w.N)

    def body(pos_ref, g_ref, r_ref, o_ref):
        o_ref[...] = (g_ref[...].astype(F32) + r_ref[...].astype(F32)).astype(BF16)

    blk = pl.BlockSpec((None, tr, tn), lambda p, i, j, pos: (p, i, j))
    grid_spec = pltpu.PrefetchScalarGridSpec(
        num_scalar_prefetch=1, grid=(w.P, w.R // tr, w.N // tn),
        in_specs=[pl.BlockSpec((None, None, tr, tn), lambda p, i, j, pos: (p, pos[0], i, j)), blk], out_specs=blk)
    return pl.pallas_call(body, name=name, grid_spec=grid_spec, out_shape=jax.ShapeDtypeStruct((w.P, w.R, w.N), BF16),
                          compiler_params=_cp(("parallel",) * 3))(pos, grad, got)


def _scatter_start(name, ws, pairs):
    nw = len(ws)

    def body(*refs):
        pr, land = refs[:nw], refs[nw:2 * nw]
        ssem, rsem = refs[2 * nw], refs[2 * nw + 1]
        token = refs[4 * nw + 2]
        x, y, c = _mesh_pos()
        chips, ks = _other_chips(x, y)
        for i, w in enumerate(ws):
            for j, chip in enumerate(chips):
                _remote(w.part(pr[i], ks[j]), land[i].at[j], ssem.at[3 * i + j], rsem.at[3 * i + j], (*chip, c)).start()
        token[...] = jnp.zeros_like(token)

    lands = [lax.empty((3, w.R, w.nn), BF16) for w in ws]
    out = pl.pallas_call(
        body, name=name, in_specs=[HBM] * (2 * nw),
        out_specs=[SEM, SEM] + [HBM] * (2 * nw) + [VMEM],
        out_shape=[pltpu.SemaphoreType.DMA((3 * nw,))] * 2 + [pltpu.HBM(a.shape, a.dtype) for a in pairs + lands]
        + [jax.ShapeDtypeStruct((8, LANES), F32)],
        input_output_aliases={i: 2 + i for i in range(2 * nw)},
        compiler_params=pltpu.CompilerParams(has_side_effects=EFFECT),
    )(*[_hbm(a) for a in pairs + lands])
    return out[0], out[1], list(out[2:2 + nw]), list(out[2 + nw:2 + 2 * nw]), out[-1]


def _scatter_wait(name, ws, pairs, lands, ssem, rsem, after):
    nw = len(ws)

    def body(*refs):
        pr, land = refs[:nw], refs[nw:2 * nw]
        ssem_ref, rsem_ref = refs[2 * nw], refs[2 * nw + 1]
        x, y, c = _mesh_pos()
        chips, ks = _other_chips(x, y)
        for i, w in enumerate(ws):
            for j, chip in enumerate(chips):
                cp = _remote(w.part(pr[i], ks[j]), land[i].at[j], ssem_ref.at[3 * i + j], rsem_ref.at[3 * i + j], (*chip, c))
                cp.wait_send()
                cp.wait_recv()

    out = pl.pallas_call(
        body, name=name, in_specs=[HBM] * (2 * nw) + [SEM, SEM] + [ANY] * len(after), out_specs=[HBM] * (2 * nw),
        out_shape=[pltpu.HBM(a.shape, a.dtype) for a in pairs + lands],
        input_output_aliases={i: i for i in range(2 * nw)},
        compiler_params=pltpu.CompilerParams(has_side_effects=EFFECT),
    )(*pairs, *lands, ssem, rsem, *after)
    return list(out[:nw]), list(out[nw:])


def _final_sum(name, w, pos, grad, got, parts):
    tr, tn = _grad_tiles(w, w.nn)
    nbc = w.nn // tn
    if got is None:
        return _final_sum_pair(name, w, pos, grad, parts, tr, tn)

    def body(pos_ref, g_ref, r_ref, p_ref, o_ref):
        acc = g_ref[...].astype(F32) + r_ref[...].astype(F32)
        for j in range(3):
            acc = acc + p_ref[j].astype(F32)
        o_ref[...] = acc

    if w.colshard:
        g_spec = pl.BlockSpec((None, None, tr, tn), lambda i, j, pos: (0, pos[0], i, pos[1] * nbc + j))
        r_spec = pl.BlockSpec((None, tr, tn), lambda i, j, pos: (0, i, pos[1] * nbc + j))
    else:
        g_spec = pl.BlockSpec((None, None, tr, tn), lambda i, j, pos: (pos[1], pos[0], i, j))
        r_spec = pl.BlockSpec((None, tr, tn), lambda i, j, pos: (pos[1], i, j))
    grid_spec = pltpu.PrefetchScalarGridSpec(
        num_scalar_prefetch=1, grid=(w.R // tr, nbc),
        in_specs=[g_spec, r_spec, pl.BlockSpec((3, tr, tn), lambda i, j, pos: (0, i, j))],
        out_specs=pl.BlockSpec((None, tr, tn), lambda i, j, pos: (pos[0], i, j)))
    return pl.pallas_call(body, name=name, grid_spec=grid_spec, out_shape=jax.ShapeDtypeStruct((2, w.R, w.nn), F32),
                          compiler_params=_cp(("parallel",) * 2))(pos, grad, got, parts)


def _final_sum_pair(name, w, pos, pair, parts, tr, tn):
    nbc = w.nn // tn

    def body(pos_ref, g_ref, p_ref, o_ref):
        acc = g_ref[...].astype(F32)
        for j in range(3):
            acc = acc + p_ref[j].astype(F32)
        o_ref[...] = acc

    if w.colshard:
        g_spec = pl.BlockSpec((None, tr, tn), lambda i, j, pos: (0, i, pos[1] * nbc + j))
    else:
        g_spec = pl.BlockSpec((None, tr, tn), lambda i, j, pos: (pos[1], i, j))
    grid_spec = pltpu.PrefetchScalarGridSpec(
        num_scalar_prefetch=1, grid=(w.R // tr, nbc),
        in_specs=[g_spec, pl.BlockSpec((3, tr, tn), lambda i, j, pos: (0, i, j))],
        out_specs=pl.BlockSpec((None, tr, tn), lambda i, j, pos: (pos[0], i, j)))
    return pl.pallas_call(body, name=name, grid_spec=grid_spec, out_shape=jax.ShapeDtypeStruct((2, w.R, w.nn), F32),
                          compiler_params=_cp(("parallel",) * 2))(pos, pair, parts)


VEC_ROWS = 16


def _vector_step(d, n_conv, parts, params, deps=()):
    ncw = params[2][0].shape[1]
    n_par = len(params)

    def body(*refs):
        dg1, dba, dbb, dcw, dcb, dps, dg2, dgf, lc = refs[:9]
        wmv = refs[9:9 + 3 * n_par]
        refs = refs[9 + 3 * n_par + len(deps):]
        outs = refs[:4 * n_par]
        loss_ref = refs[4 * n_par]
        snd, got, ssem, rsem = refs[4 * n_par + 1:]
        x, y, c = _mesh_pos()
        me = 4 * x + 2 * y + c
        snd[...] = jnp.zeros_like(snd)
        for row, ref in ((0, dg1), (1, dba), (2, dbb), (3, dps), (4, dg2), (5, dgf), (6, lc)):
            snd[row:row + 1, :] = ref[...]
        snd[7:8, :n_conv] = dcb[...]
        snd[8:11, :n_conv] = dcw[...]
        cps = []
        for r in range(1, N_DEV):
            peer = tuple(1 - p if (r >> b) & 1 else p for p, b in ((x, 2), (y, 1), (c, 0)))
            cps.append(_remote(snd, got.at[me], ssem.at[r - 1], rsem.at[r - 1], peer))
        for cp in cps:
            cp.start()
        got[me] = snd[...]
        for r in range(1, N_DEV):
            peer = tuple(1 - p if (r >> b) & 1 else p for p, b in ((x, 2), (y, 1), (c, 0)))
            _remote(snd, got.at[4 * peer[0] + 2 * peer[1] + peer[2]], ssem.at[r - 1], rsem.at[r - 1], peer).wait_recv()
        for cp in cps:
            cp.wait_send()
        tot = got[0]
        for dev in range(1, N_DEV):
            tot = tot + got[dev]
        loss_ref[...] = jnp.sum(tot[6:7, :], axis=1, keepdims=True)
        k_me = 2 * x + y
        g_cw = jnp.zeros((3, ncw), F32)
        for k in range(N_CHIPS):
            g_cw = g_cw + jnp.where(k_me == k, tot[8:11, k * ncw:(k + 1) * ncw], 0.0)
        grads = [tot[0:1, :], jnp.concatenate([tot[1:2, :], tot[2:3, :]], axis=1), g_cw, tot[7:8, :n_conv],
                 tot[3:4, :], tot[4:5, :], tot[5:6, :]]
        for i, g in enumerate(grads):
            w_ref, m_ref, v_ref = wmv[3 * i:3 * i + 3]
            delta, nm, nv = _adamw_math(w_ref[...], g, m_ref[...], v_ref[...])
            outs[4 * i][...] = g
            outs[4 * i + 1][...] = delta
            outs[4 * i + 2][...] = nm
            outs[4 * i + 3][...] = nv

    args = list(parts)
    out_shape = []
    for w, m, v in params:
        args += [w, m, v]
        out_shape += [jax.ShapeDtypeStruct(w.shape, F32)] * 4
    out_shape.append(jax.ShapeDtypeStruct((1, 1), F32))
    return pl.pallas_call(
        body, name="vector_params_step", in_specs=[VMEM] * len(args) + [ANY] * len(deps),
        out_specs=[VMEM] * len(out_shape), out_shape=out_shape,
        scratch_shapes=[pltpu.VMEM((VEC_ROWS, d), F32), pltpu.VMEM((N_DEV, VEC_ROWS, d), F32),
                        pltpu.SemaphoreType.DMA((N_DEV - 1,)), pltpu.SemaphoreType.DMA((N_DEV - 1,))],
        compiler_params=pltpu.CompilerParams(vmem_limit_bytes=VMEM_LIMIT),
    )(*args, *deps)


def kernel(x, norm1_g, w_in, b_gate, conv_w, conv_b, w_a_out, w_pool, pool_scale, w_o, norm2_g, w_ffn_gate, w_ffn_up, w_ffn_down, final_g, loss_target, m_norm1_g, m_w_in, m_b_gate, m_conv_w, m_conv_b, m_w_a_out, m_w_pool, m_pool_scale, m_w_o, m_norm2_g, m_w_ffn_gate, m_w_ffn_up, m_w_ffn_down, m_final_g, v_norm1_g, v_w_in, v_b_gate, v_conv_w, v_conv_b, v_w_a_out, v_w_pool, v_pool_scale, v_w_o, v_norm2_g, v_w_ffn_gate, v_w_ffn_up, v_w_ffn_down, v_final_g):
    t, d = x.shape[1], x.shape[2]
    n_conv = conv_b.shape[1]
    n_groups, pool_cg, pool_dg = w_pool.shape[1], w_pool.shape[2], N_CHIPS * w_pool.shape[3]
    d_ff = N_CHIPS * w_ffn_gate.shape[2]
    assert n_conv // n_groups == pool_cg and n_conv % (n_groups * MIX_COLS) == 0 and n_groups == len(POOL_WINDOWS)

    big = {"w_in": (w_in, m_w_in, v_w_in), "w_a_out": (w_a_out, m_w_a_out, v_w_a_out), "w_pool": (w_pool, m_w_pool, v_w_pool),
           "w_o": (w_o, m_w_o, v_w_o), "w_ffn_gate": (w_ffn_gate, m_w_ffn_gate, v_w_ffn_gate),
           "w_ffn_up": (w_ffn_up, m_w_ffn_up, v_w_ffn_up), "w_ffn_down": (w_ffn_down, m_w_ffn_down, v_w_ffn_down)}
    colshard = {"w_in": True, "w_a_out": True, "w_pool": True, "w_o": False, "w_ffn_gate": True, "w_ffn_up": True,
                "w_ffn_down": False}
    names = list(big)
    shard2d = {n: big[n][0].reshape(-1, big[n][0].shape[-1]) for n in names}
    ws = [_Weight(n, *shard2d[n].shape, colshard[n]) for n in names]

    xs, tgt = x[0], loss_target[0]
    cw_loc = conv_w[0]
    pos = jnp.stack([lax.axis_index("c"), 2 * lax.axis_index("x") + lax.axis_index("y")]).astype(jnp.int32)
    by_name = {w.name: w for w in ws}
    groups = [[by_name[n] for n in g] for g in (["w_in"], ["w_a_out", "w_pool", "w_o"], ["w_ffn_gate"], ["w_ffn_up"],
                                                 ["w_ffn_down"])]
    first = [sum(len(g) for g in groups[:i]) for i in range(len(groups))]
    rgroups = [groups[0], groups[1], groups[2] + groups[3], groups[4]]

    cast = lambda w, dep: _cast_place(f"cast_{w.name}", w, pos, shard2d[w.name].reshape(2, w.R, w.nn), deps=dep)
    chips, ks = _other_chips(lax.axis_index("x"), lax.axis_index("y"))
    kvec = jnp.stack([pos[1], *ks]).astype(jnp.int32)
    full = {}

    def start(name, arrays, copies, after=()):
        ssem, rsem, arrays, token = _split_start(name, arrays, copies.n, copies, after)
        return name, arrays, ssem, rsem, copies, token

    def wait(started, after):
        name, arrays, ssem, rsem, copies, _ = started
        return _split_wait(name + "_wait", arrays, ssem, rsem, copies, after)

    def pass_on(g, got, after=()):
        return start(f"pass_{g}", got, _pass_copies(groups[g]), after)

    def passed(g, st, after=None):
        got = wait(st, [st[5]] if after is None else after)
        full.update({w.name: a.reshape(w.P * 2 * w.R, w.N) for w, a in zip(groups[g], got)})

    near = start("near_0", [cast(w, []) for w in groups[0]], _near_copies(groups[0]))
    rest = [cast(w, [near[5]]) for grp in groups[1:] for w in grp]
    h1 = _rms_fwd("norm1_fwd", xs, norm1_g, deps=[near[5]])
    proj = _proj_piece("proj_own", h1, shard2d["w_in"], None, kvec, 0, 1, deps=rest)
    got = wait(near, [proj])
    far = start("far_0", got, _far_copies(groups[0]))
    sems_b, lands_b, tok_b = _gather_start("gather_start_b", groups[1:2], rest[:3], after=[far[5]])
    st = start("pass_near_0", far[1], _pass_copies(groups[0], (0, 1)), [tok_b])
    got = wait(st, [st[5]])
    proj = _proj_piece("proj_near", h1, got[0].reshape(-1, groups[0][0].N), proj, kvec, 1, 2)
    st = start("pass_far_0", wait((far[0], got) + far[2:], [proj]), _pass_copies(groups[0], (2,)))
    got = wait(st, [st[5]])
    w_in_full = got[0].reshape(-1, groups[0][0].N)
    proj = _proj_piece("proj_far", h1, w_in_full, proj, kvec, 3, 1)
    cw_full, proj = _gather_conv_w(cw_loc, proj)
    got = _gather_wait("gather_wait_1", groups[1], lands_b, *sems_b[0], proj)
    near_g = start("near_2", rest[3:4], _near_copies(groups[2]), got)
    st = pass_on(1, got, [near_g[5]])
    z, p = _mixer_fwd("mixer_fwd", proj, cw_full, conv_b, n_conv, n_groups, deps=[st[5]])
    passed(1, st, [z])
    wp_full = full["w_pool"].reshape(n_groups, pool_cg, pool_dg)
    ya = _mm_nn("conv_out", z, full["w_a_out"], BF16)
    yb = _gmm_nn("pool_out", p, wp_full, BF16)
    merged = _merge_fwd("merge_fwd", proj, b_gate, ya, yb, pool_scale)
    far_g = start("far_2", wait(near_g, [merged]), _far_copies(groups[2]))
    near_u = start("near_3", rest[4:5], _near_copies(groups[3]), [far_g[5]])
    x2 = _mm_nn("mix_out", merged, full["w_o"], F32, add=xs, deps=[near_u[5]])
    st = pass_on(2, wait(far_g, [x2]))
    h2 = _rms_fwd("norm2_fwd", x2, norm2_g, deps=[st[5]])
    passed(2, st, [h2])
    gate = _mm_nn("ffn_gate_a", h2, full["w_ffn_gate"], BF16, part=(0, 2))
    far_u = start("far_3", wait(near_u, [gate]), _far_copies(groups[3]))
    near_d = start("near_4", rest[5:6], _near_copies(groups[4]), [far_u[5]])
    gate = _mm_nn("ffn_gate_b", h2, full["w_ffn_gate"], BF16, part=(1, 2), prev=gate, deps=[near_d[5]])
    passed(3, pass_on(3, wait(far_u, [gate])))
    up_act = _ffn_up_act("ffn_up_act_a", h2, full["w_ffn_up"], gate, part=(0, 2))
    far_d = start("far_4", wait(near_d, [up_act[0]]), _far_copies(groups[4]))
    up, act = _ffn_up_act("ffn_up_act_b", h2, full["w_ffn_up"], gate, part=(1, 2), prev=up_act, deps=[far_d[5]])
    passed(4, pass_on(4, wait(far_d, [act])))
    x3 = _mm_nn("ffn_down", act, full["w_ffn_down"], F32, add=x2, tiles=(512, 512))

    pending = {}

    def pair_start(g, grads):
        grp = rgroups[g]
        gcan = [grads[w.name].reshape(w.P, 2, w.R, w.N) for w in grp]
        slots = [lax.empty((w.P, w.R, w.N), BF16) for w in grp]
        pending[g] = _split_start(f"pair_start_{g}", gcan + slots, len(grp), _pair_copies(len(grp)))
        return pending[g][3]

    def scatter_start(g, after):
        grp = rgroups[g]
        n = len(grp)
        ssem, rsem, arrs, _ = pending[g]
        arrs = _split_wait(f"pair_wait_{g}", arrs, ssem, rsem, _pair_copies(n), after)
        gcan, sib = arrs[:n], arrs[n:]
        pairs = [_pair_sum(f"pair_sum_{w.name}", w, pos, a, s) for w, a, s in zip(grp, gcan, sib)]
        ssem, rsem, pairs, slots, token = _scatter_start(f"scatter_start_{g}", grp, pairs)
        pending[g] = (gcan, sib, pairs, slots, ssem, rsem)
        return token

    def pair_start_halves(g, ab, deps):
        grp = rgroups[g]
        sent = [_mm_tn_half(f"d{w.name}_sib", a, b, pos, False, deps=deps if i == 0 else ()) for i, (w, (a, b)) in enumerate(zip(grp, ab))]
        slots = [lax.empty((1, w.R, w.N), BF16) for w in grp]
        pending[g] = _split_start(f"pair_start_{g}", sent + slots, len(grp), _pair_copies(len(grp), whole=True))
        return pending[g][3]

    def scatter_start_halves(g, ab, after):
        grp = rgroups[g]
        n = len(grp)
        ssem, rsem, arrs, _ = pending[g]
        arrs = _split_wait(f"pair_wait_{g}", arrs, ssem, rsem, _pair_copies(n, whole=True), after)
        pairs = [_mm_tn_half(f"d{w.name}_own", a, b, pos, True, add=s) for w, (a, b), s in zip(grp, ab, arrs[n:])]
        ssem, rsem, pairs, slots, token = _scatter_start(f"scatter_start_{g}", grp, pairs)
        pending[g] = (None, None, pairs, slots, ssem, rsem)
        return token

    def reduce_finish(g, after):
        grp = rgroups[g]
        gcan, sib, pairs, slots, ssem, rsem = pending[g]
        pairs, parts = _scatter_wait(f"scatter_wait_{g}", grp, pairs, slots, ssem, rsem, after)
        if gcan is None:
            return [_final_sum(f"final_sum_{w.name}", w, pos, a, None, q) for w, a, q in zip(grp, pairs, parts)]
        return [_final_sum(f"final_sum_{w.name}", w, pos, a, s, q) for w, a, s, q in zip(grp, gcan, sib, parts)]

    grads = {}
    dx3, dx3b, d_gf, loss_cols = _final_bwd("final_bwd", x3, final_g.reshape(1, d), tgt)
    dgate, dup = _ffn_bwd("ffn_bwd", dx3b, full["w_ffn_down"], gate, up)
    grads["w_ffn_down"] = _mm_tn("dw_ffn_down", act, dx3b, BF16)
    tok = pair_start(3, grads)
    dh2 = _mm_nt("d_h2", [(dgate, full["w_ffn_gate"]), (dup, full["w_ffn_up"])], BF16, tk=d_ff // 4, deps=[tok])
    tok = scatter_start(3, [dh2])
    tok = pair_start_halves(2, [(h2, dgate), (h2, dup)], [tok])
    dx2, dx2b, d_g2 = _rms_bwd("norm2_bwd", x2, norm2_g, dh2, dx3, True, deps=[tok])
    dmerged = _mm_nt("d_merged", [(dx2b, full["w_o"])], BF16, tk=d)
    grads["w_o"] = _mm_tn("dw_o", merged, dx2b, BF16)
    tok = scatter_start_halves(2, [(h2, dgate), (h2, dup)], [grads["w_o"]])
    dya, dyb, dproj, d_bga, d_bgb, d_ps = _merge_bwd("merge_bwd", dmerged, proj, b_gate, ya, yb, pool_scale, deps=[tok])
    dz = _mm_nt("d_z", [(dya, full["w_a_out"])], BF16, tk=d)
    grads["w_a_out"] = _mm_tn("dw_a_out", z, dya, BF16)
    dp = _gmm_nt("d_pool", dyb, wp_full, BF16)
    grads["w_pool"] = _gmm_tn("dw_pool", p, dyb, n_groups, BF16)
    tok = pair_start(1, grads)
    dproj, d_cw, d_cb = _mixer_bwd("mixer_bwd", dz, dp, proj, cw_full, conv_b, dproj, n_conv, n_groups, deps=[tok])
    tok = scatter_start(1, [dproj])
    tok = pair_start_halves(0, [(h1, dproj)], [tok])
    dh1 = _mm_nt("d_h1", [(dproj, w_in_full)], BF16, tk=proj.shape[1] // 4, deps=[tok])
    tok = scatter_start_halves(0, [(h1, dproj)], [dh1])
    grad_x, d_g1 = _rms_bwd("norm1_bwd", xs, norm1_g, dh1, dx2, False, deps=[tok])

    g_big, d_big, m_big, v_big = {}, {}, {}, {}

    def update(wsub, shared):
        out = []
        for w, g in zip(wsub, shared):
            wt, mt, vt = big[w.name]
            g2 = g.reshape(2 * w.R, w.nn)
            go, dl, nm, nv = _adamw(f"adamw_{w.name}", shard2d[w.name], g2, mt.reshape(g2.shape), vt.reshape(g2.shape))
            g_big[w.name], d_big[w.name], m_big[w.name], v_big[w.name] = (a.reshape(wt.shape) for a in (go, dl, nm, nv))
            out.append(nv)
        return out

    after = [grad_x]
    started = []
    for g in (3, 2, 1):
        halves = reduce_finish(g, after)
        share = _share_copies(len(halves))
        ssem, rsem, halves, tok = _split_start(f"share_start_{g}", halves, len(halves), share)
        started.append((g, ssem, rsem, halves, share))
        after = [tok]
    for g, ssem, rsem, halves, share in started:
        after = update(rgroups[g], _split_wait(f"share_wait_{g}", halves, ssem, rsem, share, after))
    share = _share_copies(1)
    ssem, rsem, halves, tok = _split_start("share_start_0", reduce_finish(0, after), 1, share)

    vec_names = ["norm1_g", "b_gate", "conv_w", "conv_b", "pool_scale", "norm2_g", "final_g"]
    vec = {"norm1_g": (norm1_g, m_norm1_g, v_norm1_g), "b_gate": (b_gate, m_b_gate, v_b_gate),
           "conv_w": (cw_loc, m_conv_w[0], v_conv_w[0]), "conv_b": (conv_b, m_conv_b, v_conv_b),
           "pool_scale": (pool_scale, m_pool_scale, v_pool_scale), "norm2_g": (norm2_g, m_norm2_g, v_norm2_g),
           "final_g": tuple(a.reshape(1, d) for a in (final_g, m_final_g, v_final_g))}
    vout = _vector_step(d, n_conv, [d_g1, d_bga, d_bgb, d_cw, d_cb, d_ps, d_g2, d_gf, loss_cols],
                        [vec[n] for n in vec_names], deps=halves)
    update(rgroups[0], _split_wait("share_wait_0", halves, ssem, rsem, share, []))

    shapes = {"conv_w": conv_w.shape, "final_g": final_g.shape}
    g_vec, d_vec, m_vec, v_vec = ({n: vout[4 * i + q].reshape(shapes.get(n, vec[n][0].shape)) for i, n in enumerate(vec_names)}
                                  for q in range(4))
    loss = vout[-1].reshape(())

    order = ["norm1_g", "w_in", "b_gate", "conv_w", "conv_b", "w_a_out", "w_pool", "pool_scale", "w_o", "norm2_g",
             "w_ffn_gate", "w_ffn_up", "w_ffn_down", "final_g"]
    pick = lambda vecs, bigs: [vecs[n] if n in vecs else bigs[n] for n in order]
    return (loss, grad_x.reshape(x.shape), *pick(g_vec, g_big), *pick(d_vec, d_big), *pick(m_vec, m_big),
            *pick(v_vec, v_big))
```

```python
import functools

import jax
import jax.numpy as jnp
from jax import lax
from jax.experimental import pallas as pl
from jax.experimental.pallas import tpu as pltpu

F32, BF16 = jnp.float32, jnp.bfloat16
MESH = pl.DeviceIdType.MESH
ANY = pl.BlockSpec(memory_space=pl.ANY)
VMEM = pl.BlockSpec(memory_space=pltpu.VMEM)
HBM = pl.BlockSpec(memory_space=pltpu.HBM)
SEM = pl.BlockSpec(memory_space=pltpu.SEMAPHORE)
EFFECT = pltpu.SideEffectType.DATAFLOW_SIDE_EFFECTING

EPS = 1e-6
POOL_WINDOWS = (2, 4, 8, 16)
ADAM_LR, ADAM_B1, ADAM_B2, ADAM_EPS, ADAM_WD, ADAM_STEP = 0.001, 0.9, 0.999, 1e-08, 0.01, 10

V7X_VMEM_BYTES = 64 * 1024 * 1024
VMEM_LIMIT = V7X_VMEM_BYTES * 3 // 4
LANES = 128
COL_TILE = 8 * LANES
N_CHIPS = 4
N_DEV = 8

_DIMS = {
    "nn": (((1,), (0,)), ((), ())),
    "nt": (((1,), (1,)), ((), ())),
    "tn": (((0,), (0,)), ((), ())),
}


def _cp(sem):
    return pltpu.CompilerParams(dimension_semantics=sem, vmem_limit_bytes=VMEM_LIMIT)


def _mesh_pos():
    return lax.axis_index("x"), lax.axis_index("y"), lax.axis_index("c")


def _mm(name, pairs, *, mode, grid, out_shape, o_spec, nk=1, kaxis=None, add=None, deps=(), prev=None):
    npair = len(pairs)
    has_add = add is not None

    def body(*refs):
        ab = refs[: 2 * npair]
        pos = 2 * npair
        add_ref = refs[pos] if has_add else None
        pos += int(has_add) + len(deps) + (prev is not None)
        o_ref = refs[pos]
        acc_ref = refs[pos + 1] if nk > 1 else None
        d = None
        for p in range(npair):
            t = lax.dot_general(ab[2 * p][...], ab[2 * p + 1][...], _DIMS[mode], preferred_element_type=F32)
            d = t if d is None else d + t
        if nk == 1:
            if has_add:
                d = d + add_ref[...].astype(F32)
            o_ref[...] = d.astype(o_ref.dtype)
        else:
            k = pl.program_id(kaxis)

            @pl.when(k == 0)
            def _():
                acc_ref[...] = d

            @pl.when(k > 0)
            def _():
                acc_ref[...] += d

            @pl.when(k == nk - 1)
            def _():
                r = acc_ref[...]
                if has_add:
                    r = r + add_ref[...].astype(F32)
                o_ref[...] = r.astype(o_ref.dtype)

    args, specs = [], []
    for a, a_spec, b, b_spec in pairs:
        args += [a, b]
        specs += [a_spec, b_spec]
    if has_add:
        args.append(add[0])
        specs.append(add[1])
    args += list(deps)
    specs += [ANY] * len(deps)
    aliases = {}
    if prev is not None:
        aliases = {len(args): 0}
        args.append(prev)
        specs.append(ANY)
    scratch = []
    if nk > 1:
        blk = [d for d in o_spec.block_shape if d is not None]
        scratch = [pltpu.VMEM(tuple(blk), F32)]
    sem = tuple("arbitrary" if (nk > 1 and ax == kaxis) else "parallel" for ax in range(len(grid)))
    return pl.pallas_call(
        body, name=name, grid=grid, in_specs=specs, out_specs=o_spec, out_shape=out_shape,
        scratch_shapes=scratch, input_output_aliases=aliases, compiler_params=_cp(sem),
    )(*args)


def _tile_span(n_tiles, part):
    if part is None:
        return 0, n_tiles
    p, of = part
    return p * n_tiles // of, (p + 1) * n_tiles // of


def _tile(n, pref):
    if n <= pref:
        return n
    for t in range(pref, 0, -LANES):
        if t % LANES == 0 and n % t == 0:
            return t
    raise ValueError(f"no tile for {n}")


def _mm_nn(name, a, b, out_dtype, add=None, tk=None, deps=(), part=None, prev=None, tiles=None):
    m, kk = a.shape
    n = b.shape[1]
    tm, tn = _tile(m, 1024), _tile(n, COL_TILE)
    if tiles is not None:
        tm, tn = _tile(m, tiles[0]), _tile(n, tiles[1])
    out_shape = jax.ShapeDtypeStruct((m, n), out_dtype)
    if tk is None or tk == kk:
        j0, j1 = _tile_span(n // tn, part)
        grid = (m // tm, j1 - j0)
        pairs = [(a, pl.BlockSpec((tm, kk), lambda i, j: (i, 0)), b, pl.BlockSpec((kk, tn), lambda i, j: (0, j0 + j)))]
        o_spec = pl.BlockSpec((tm, tn), lambda i, j: (i, j0 + j))
        add_ = None if add is None else (add, pl.BlockSpec((tm, tn), lambda i, j: (i, j0 + j)))
        return _mm(name, pairs, mode="nn", grid=grid, out_shape=out_shape, o_spec=o_spec, add=add_, deps=deps, prev=prev)
    tn = _tile(n, 1024)
    nk = kk // tk
    grid = (m // tm, n // tn, nk)
    pairs = [(a, pl.BlockSpec((tm, tk), lambda i, j, k: (i, k)), b, pl.BlockSpec((tk, tn), lambda i, j, k: (k, j)))]
    o_spec = pl.BlockSpec((tm, tn), lambda i, j, k: (i, j))
    add_ = None if add is None else (add, pl.BlockSpec((tm, tn), lambda i, j, k: (i, j)))
    return _mm(name, pairs, mode="nn", grid=grid, out_shape=out_shape, o_spec=o_spec, nk=nk, kaxis=2, add=add_, deps=deps)


def _mm_nt(name, abs_, out_dtype, tk, deps=()):
    m, kk = abs_[0][0].shape
    n = abs_[0][1].shape[0]
    tm = _tile(m, 1024)
    nk = kk // tk
    tn = _tile(n, COL_TILE if nk == 1 else 1024)
    out_shape = jax.ShapeDtypeStruct((m, n), out_dtype)
    if nk == 1:
        grid = (m // tm, n // tn)
        pairs = [(a, pl.BlockSpec((tm, kk), lambda i, j: (i, 0)), b, pl.BlockSpec((tn, kk), lambda i, j: (j, 0)))
                 for a, b in abs_]
        o_spec = pl.BlockSpec((tm, tn), lambda i, j: (i, j))
        return _mm(name, pairs, mode="nt", grid=grid, out_shape=out_shape, o_spec=o_spec, deps=deps)
    grid = (m // tm, n // tn, nk)
    pairs = [(a, pl.BlockSpec((tm, tk), lambda i, j, k: (i, k)), b, pl.BlockSpec((tn, tk), lambda i, j, k: (j, k)))
             for a, b in abs_]
    o_spec = pl.BlockSpec((tm, tn), lambda i, j, k: (i, j))
    return _mm(name, pairs, mode="nt", grid=grid, out_shape=out_shape, o_spec=o_spec, nk=nk, kaxis=2, deps=deps)


def _mm_tn(name, a, b, out_dtype, deps=()):
    t, m = a.shape
    n = b.shape[1]
    tm, tn = _tile(m, 512), _tile(n, 2048)
    if n > m:
        grid = (n // tn, m // tm)
        a_map, b_map, o_map = (lambda j, i: (0, i)), (lambda j, i: (0, j)), (lambda j, i: (i, j))
    else:
        grid = (m // tm, n // tn)
        a_map, b_map, o_map = (lambda i, j: (0, i)), (lambda i, j: (0, j)), (lambda i, j: (i, j))
    pairs = [(a, pl.BlockSpec((t, tm), a_map), b, pl.BlockSpec((t, tn), b_map))]
    o_spec = pl.BlockSpec((tm, tn), o_map)
    return _mm(name, pairs, mode="tn", grid=grid, out_shape=jax.ShapeDtypeStruct((m, n), out_dtype), o_spec=o_spec,
               deps=deps)


def _mm_tn_half(name, a, b, pos, mine, add=None, deps=()):
    t, m = a.shape
    r, n = m // 2, b.shape[1]
    tm, tn = _tile(r, 512), _tile(n, 2048)
    nbi = r // tm
    half = (lambda pos: pos[0]) if mine else (lambda pos: 1 - pos[0])
    if n > r:
        grid, ij = (n // tn, nbi), (lambda g0, g1: (g1, g0))
    else:
        grid, ij = (nbi, n // tn), (lambda g0, g1: (g0, g1))
    has_add = add is not None

    def body(pos_ref, a_ref, b_ref, *rest):
        d = lax.dot_general(a_ref[...], b_ref[...], _DIMS["tn"], preferred_element_type=F32)
        if has_add:
            d = d + rest[0][...].astype(F32)
        rest[-1][...] = d.astype(BF16)

    o_spec = pl.BlockSpec((None, tm, tn), lambda g0, g1, pos: (0, *ij(g0, g1)))
    grid_spec = pltpu.PrefetchScalarGridSpec(
        num_scalar_prefetch=1, grid=grid,
        in_specs=[pl.BlockSpec((t, tm), lambda g0, g1, pos: (0, half(pos) * nbi + ij(g0, g1)[0])),
                  pl.BlockSpec((t, tn), lambda g0, g1, pos: (0, ij(g0, g1)[1]))]
        + ([o_spec] if has_add else []) + [ANY] * len(deps),
        out_specs=o_spec)
    return pl.pallas_call(body, name=name, grid_spec=grid_spec, out_shape=jax.ShapeDtypeStruct((1, r, n), BF16),
                          compiler_params=_cp(("parallel",) * 2))(pos, a, b, *([add] if has_add else []), *deps)


def _proj_piece(name, h, w, prev, kvec, base, count, deps=()):
    t, kk = h.shape
    own = w.dtype == F32
    nn = w.shape[1] if own else w.shape[1] // N_CHIPS
    tm, tn = _tile(t, 1024), _tile(nn, COL_TILE)
    nb = nn // tn

    def body(kv_ref, h_ref, w_ref, *rest):
        rest[-1][...] = lax.dot_general(h_ref[...], w_ref[...].astype(BF16), _DIMS["nn"],
                                        preferred_element_type=F32).astype(BF16)

    cols = lambda s, i, j, kv: (0, j) if own else (0, kv[base + s] * nb + j)
    extra = ([] if prev is None else [prev]) + list(deps)
    grid_spec = pltpu.PrefetchScalarGridSpec(
        num_scalar_prefetch=1, grid=(count, t // tm, nb),
        in_specs=[pl.BlockSpec((tm, kk), lambda s, i, j, kv: (i, 0)), pl.BlockSpec((kk, tn), cols)] + [ANY] * len(extra),
        out_specs=pl.BlockSpec((tm, tn), lambda s, i, j, kv: (i, kv[base + s] * nb + j)))
    return pl.pallas_call(body, name=name, grid_spec=grid_spec, out_shape=jax.ShapeDtypeStruct((t, N_CHIPS * nn), BF16),
                          input_output_aliases={} if prev is None else {3: 0},
                          compiler_params=_cp(("parallel",) * 3))(kvec, h, w, *extra)


def _gmm_nn(name, p, w, out_dtype):
    t = p.shape[0]
    g, cg, dg = w.shape
    tm = _tile(t, 1024)
    pairs = [(p, pl.BlockSpec((tm, cg), lambda i, j: (i, j)), w, pl.BlockSpec((None, cg, dg), lambda i, j: (j, 0, 0)))]
    o_spec = pl.BlockSpec((tm, dg), lambda i, j: (i, j))
    return _mm(name, pairs, mode="nn", grid=(t // tm, g), out_shape=jax.ShapeDtypeStruct((t, g * dg), out_dtype),
               o_spec=o_spec)


def _gmm_nt(name, dy, w, out_dtype):
    t = dy.shape[0]
    g, cg, dg = w.shape
    tm = _tile(t, 1024)
    pairs = [(dy, pl.BlockSpec((tm, dg), lambda i, j: (i, j)), w, pl.BlockSpec((None, cg, dg), lambda i, j: (j, 0, 0)))]
    o_spec = pl.BlockSpec((tm, cg), lambda i, j: (i, j))
    return _mm(name, pairs, mode="nt", grid=(t // tm, g), out_shape=jax.ShapeDtypeStruct((t, g * cg), out_dtype),
               o_spec=o_spec)


def _gmm_tn(name, p, dy, g, out_dtype):
    t = p.shape[0]
    cg, dg = p.shape[1] // g, dy.shape[1] // g
    pairs = [(p, pl.BlockSpec((t, cg), lambda j: (0, j)), dy, pl.BlockSpec((t, dg), lambda j: (0, j)))]
    o_spec = pl.BlockSpec((None, cg, dg), lambda j: (j, 0, 0))
    return _mm(name, pairs, mode="tn", grid=(g,), out_shape=jax.ShapeDtypeStruct((g, cg, dg), out_dtype), o_spec=o_spec)


ROW_TILE = 256


def _rows(t):
    return _tile8(t, ROW_TILE)


def _tile8(n, pref):
    if n <= pref:
        return n
    for t in range(pref, 0, -8):
        if n % t == 0:
            return t
    raise ValueError(f"no row tile for {n}")


def _cast_place(name, w, pos, shard, deps=()):
    tr = _tile8(w.R, 512)
    if w.colshard:
        o_map = lambda h, i, pos: (0, h, i, pos[1])
    else:
        o_map = lambda h, i, pos: (pos[1], h, i, 0)

    def body(pos_ref, w_ref, *rest):
        rest[-1][...] = w_ref[...].astype(BF16)

    grid_spec = pltpu.PrefetchScalarGridSpec(
        num_scalar_prefetch=1, grid=(2, w.R // tr),
        in_specs=[pl.BlockSpec((None, tr, w.nn), lambda h, i, pos: (h, i, 0))] + [ANY] * len(deps),
        out_specs=pl.BlockSpec((None, None, tr, w.nn), o_map))
    return pl.pallas_call(body, name=name, grid_spec=grid_spec, out_shape=jax.ShapeDtypeStruct((w.P, 2, w.R, w.N), BF16),
                          compiler_params=_cp(("parallel", "parallel")))(pos, shard, *deps)


def _rms_fwd(name, x, g, deps=()):
    t, d = x.shape
    tm = _rows(t)

    def body(x_ref, g_ref, *rest):
        xf = x_ref[...]
        r = lax.rsqrt(jnp.mean(xf * xf, axis=-1, keepdims=True) + EPS)
        rest[-1][...] = (xf * r * g_ref[...]).astype(BF16)

    return pl.pallas_call(
        body, name=name, grid=(t // tm,),
        in_specs=[pl.BlockSpec((tm, d), lambda i: (i, 0)), pl.BlockSpec((1, d), lambda i: (0, 0))] + [ANY] * len(deps),
        out_specs=pl.BlockSpec((tm, d), lambda i: (i, 0)), out_shape=jax.ShapeDtypeStruct((t, d), BF16),
        compiler_params=_cp(("parallel",)),
    )(x, g, *deps)


def _rms_bwd(name, x, g, dh, dres, want_bf16, deps=()):
    t, d = x.shape
    tm = _rows(t)

    def body(x_ref, g_ref, dh_ref, dres_ref, *rest):
        rest = rest[len(deps):]
        dx_ref, rest = rest[0], rest[1:]
        dg_ref = rest[-1]
        xf = x_ref[...]
        r = lax.rsqrt(jnp.mean(xf * xf, axis=-1, keepdims=True) + EPS)
        xh = xf * r
        dhf = dh_ref[...].astype(F32)
        dxh = dhf * g_ref[...]
        m = jnp.mean(dxh * xh, axis=-1, keepdims=True)
        dx = dres_ref[...] + r * (dxh - xh * m)
        dx_ref[...] = dx
        if want_bf16:
            rest[0][...] = dx.astype(BF16)

        @pl.when(pl.program_id(0) == 0)
        def _():
            dg_ref[...] = jnp.zeros_like(dg_ref)

        dg_ref[...] += jnp.sum(dhf * xh, axis=0, keepdims=True)

    row = pl.BlockSpec((tm, d), lambda i: (i, 0))
    vec = pl.BlockSpec((1, d), lambda i: (0, 0))
    out_specs = [row] + ([row] if want_bf16 else []) + [vec]
    out_shape = ([jax.ShapeDtypeStruct((t, d), F32)] + ([jax.ShapeDtypeStruct((t, d), BF16)] if want_bf16 else [])
                 + [jax.ShapeDtypeStruct((1, d), F32)])
    return pl.pallas_call(body, name=name, grid=(t // tm,), in_specs=[row, vec, row, row] + [ANY] * len(deps),
                          out_specs=out_specs, out_shape=out_shape, compiler_params=_cp(("arbitrary",)))(x, g, dh, dres, *deps)


def _final_bwd(name, x3, gf, tgt):
    t, d = x3.shape
    tm = _rows(t)

    def body(x_ref, g_ref, t_ref, dx_ref, dxb_ref, dg_ref, lc_ref):
        xf = x_ref[...]
        g = g_ref[...]
        r = lax.rsqrt(jnp.mean(xf * xf, axis=-1, keepdims=True) + EPS)
        xh = xf * r
        diff = xh * g - t_ref[...]
        dy = diff * (1.0 / d)
        dxh = dy * g
        m = jnp.mean(dxh * xh, axis=-1, keepdims=True)
        dx = r * (dxh - xh * m)
        dx_ref[...] = dx
        dxb_ref[...] = dx.astype(BF16)

        @pl.when(pl.program_id(0) == 0)
        def _():
            dg_ref[...] = jnp.zeros_like(dg_ref)
            lc_ref[...] = jnp.zeros_like(lc_ref)

        dg_ref[...] += jnp.sum(dy * xh, axis=0, keepdims=True)
        lc_ref[...] += jnp.sum(diff * diff, axis=0, keepdims=True) * (0.5 / d)

    row = pl.BlockSpec((tm, d), lambda i: (i, 0))
    vec = pl.BlockSpec((1, d), lambda i: (0, 0))
    return pl.pallas_call(
        body, name=name, grid=(t // tm,), in_specs=[row, vec, row], out_specs=[row, row, vec, vec],
        out_shape=[jax.ShapeDtypeStruct((t, d), F32), jax.ShapeDtypeStruct((t, d), BF16),
                   jax.ShapeDtypeStruct((1, d), F32), jax.ShapeDtypeStruct((1, d), F32)],
        compiler_params=_cp(("arbitrary",)),
    )(x3, gf, tgt)


def _shift_down(v, k, t_idx):
    return jnp.where(t_idx >= k, pltpu.roll(v, k, 0), 0.0)


def _shift_up(v, k, t_idx):
    n = v.shape[0]
    return jnp.where(t_idx < n - k, pltpu.roll(v, n - k, 0), 0.0)


def _window_sums(v, shift, t_idx, grp):
    s = v + shift(v, 1, t_idx)
    out = s
    for lvl in range(1, len(POOL_WINDOWS)):
        s = s + shift(s, 1 << lvl, t_idx)
        out = jnp.where(grp >= lvl, s, out)
    return out


def _window_weight(t_idx, grp):
    return 1.0 / jnp.minimum(t_idx[:, :1] + 1, jnp.left_shift(2, grp)).astype(F32)


MIX_COLS = 256


def _mixer_fwd(name, proj, cw, cb, n_conv, n_groups, deps=()):
    t = proj.shape[0]
    nb = n_conv // MIX_COLS
    per_group = n_conv // n_groups // MIX_COLS

    def body(ba_ref, ca_ref, va_ref, vb_ref, cw_ref, cb_ref, *rest):
        z_ref, p_ref = rest[len(deps):]
        t_idx = lax.broadcasted_iota(jnp.int32, (t, MIX_COLS), 0)
        q = ca_ref[...].astype(F32) * va_ref[...].astype(F32)
        w = cw_ref[...]
        u = cb_ref[...] + w[0:1] * _shift_down(q, 2, t_idx) + w[1:2] * _shift_down(q, 1, t_idx) + w[2:3] * q
        z_ref[...] = (ba_ref[...].astype(F32) * u).astype(BF16)
        grp = pl.program_id(0) // per_group
        v = vb_ref[...].astype(F32)
        p_ref[...] = (_window_sums(v, _shift_down, t_idx, grp) * _window_weight(t_idx, grp) - v).astype(BF16)

    col = lambda s: pl.BlockSpec((t, MIX_COLS), lambda j: (0, s * nb + j))
    return pl.pallas_call(
        body, name=name, grid=(nb,),
        in_specs=[col(0), col(1), col(2), col(3), pl.BlockSpec((3, MIX_COLS), lambda j: (0, j)),
                  pl.BlockSpec((1, MIX_COLS), lambda j: (0, j))] + [ANY] * len(deps),
        out_specs=[col(0), col(0)],
        out_shape=[jax.ShapeDtypeStruct((t, n_conv), BF16), jax.ShapeDtypeStruct((t, n_conv), BF16)],
        compiler_params=_cp(("parallel",)),
    )(proj, proj, proj, proj, cw, cb, *deps)


def _mixer_bwd(name, dz, dp, proj, cw, cb, dproj, n_conv, n_groups, deps=()):
    t = proj.shape[0]
    nb = n_conv // MIX_COLS
    per_group = n_conv // n_groups // MIX_COLS

    def body(dz_ref, dp_ref, ba_ref, ca_ref, va_ref, cw_ref, cb_ref, _, *rest):
        o_ref, dcw_ref, dcb_ref, scr = rest[len(deps):]
        s = pl.program_id(1)

        @pl.when(s == 0)
        def _():
            t_idx = lax.broadcasted_iota(jnp.int32, (t, MIX_COLS), 0)
            ca, va = ca_ref[...].astype(F32), va_ref[...].astype(F32)
            q = ca * va
            q1, q2 = _shift_down(q, 1, t_idx), _shift_down(q, 2, t_idx)
            w = cw_ref[...]
            u = cb_ref[...] + w[0:1] * q2 + w[1:2] * q1 + w[2:3] * q
            dzf = dz_ref[...].astype(F32)
            du = dzf * ba_ref[...].astype(F32)
            scr[0] = (dzf * u).astype(BF16)
            dq = w[2:3] * du + w[1:2] * _shift_up(du, 1, t_idx) + w[0:1] * _shift_up(du, 2, t_idx)
            scr[1] = (dq * va).astype(BF16)
            scr[2] = (dq * ca).astype(BF16)
            dcb_ref[...] = jnp.sum(du, axis=0, keepdims=True)
            dcw_ref[0:1, :] = jnp.sum(du * q2, axis=0, keepdims=True)
            dcw_ref[1:2, :] = jnp.sum(du * q1, axis=0, keepdims=True)
            dcw_ref[2:3, :] = jnp.sum(du * q, axis=0, keepdims=True)
            grp = pl.program_id(0) // per_group
            dpf = dp_ref[...].astype(F32)
            e = dpf * _window_weight(t_idx, grp)
            scr[3] = (_window_sums(e, _shift_up, t_idx, grp) - dpf).astype(BF16)

        o_ref[...] = scr[s]

    col = lambda c: pl.BlockSpec((t, MIX_COLS), lambda j, s: (0, c * nb + j))
    own = pl.BlockSpec((t, MIX_COLS), lambda j, s: (0, j))
    return pl.pallas_call(
        body, name=name, grid=(nb, 4),
        in_specs=[own, own, col(0), col(1), col(2), pl.BlockSpec((3, MIX_COLS), lambda j, s: (0, j)),
                  pl.BlockSpec((1, MIX_COLS), lambda j, s: (0, j)), ANY] + [ANY] * len(deps),
        out_specs=[pl.BlockSpec((t, MIX_COLS), lambda j, s: (0, s * nb + j)),
                   pl.BlockSpec((3, MIX_COLS), lambda j, s: (0, j)), pl.BlockSpec((1, MIX_COLS), lambda j, s: (0, j))],
        out_shape=[jax.ShapeDtypeStruct(dproj.shape, BF16), jax.ShapeDtypeStruct((3, n_conv), F32),
                   jax.ShapeDtypeStruct((1, n_conv), F32)],
        scratch_shapes=[pltpu.VMEM((4, t, MIX_COLS), BF16)],
        input_output_aliases={7: 0},
        compiler_params=_cp(("arbitrary", "arbitrary")),
    )(dz, dp, proj, proj, proj, cw, cb, dproj, *deps)


def _merge_fwd(name, proj, bg, ya, yb, ps):
    t, d = ya.shape
    tm = _rows(t)

    def body(gab_ref, bg_ref, ya_ref, yb_ref, ps_ref, o_ref):
        gab = gab_ref[...].astype(F32) + bg_ref[...]
        sa, sb = jax.nn.sigmoid(gab[:, :d]), jax.nn.sigmoid(gab[:, d:])
        o_ref[...] = (sa * ya_ref[...].astype(F32) + sb * (yb_ref[...].astype(F32) * ps_ref[...])).astype(BF16)

    row = pl.BlockSpec((tm, d), lambda i: (i, 0))
    return pl.pallas_call(
        body, name=name, grid=(t // tm,),
        in_specs=[pl.BlockSpec((tm, 2 * d), lambda i: (i, 1)), pl.BlockSpec((1, 2 * d), lambda i: (0, 0)), row, row,
                  pl.BlockSpec((1, d), lambda i: (0, 0))],
        out_specs=row, out_shape=jax.ShapeDtypeStruct((t, d), BF16), compiler_params=_cp(("parallel",)),
    )(proj, bg, ya, yb, ps)


def _merge_bwd(name, dm, proj, bg, ya, yb, ps, deps=()):
    t, d = ya.shape
    tm = _rows(t)

    def body(dm_ref, gab_ref, bg_ref, ya_ref, yb_ref, ps_ref, *rest):
        dya_ref, dyb_ref, dg_ref, dba_ref, dbb_ref, dps_ref = rest[len(deps):]
        gab = gab_ref[...].astype(F32) + bg_ref[...]
        sa, sb = jax.nn.sigmoid(gab[:, :d]), jax.nn.sigmoid(gab[:, d:])
        dmf = dm_ref[...].astype(F32)
        ybf, ps_ = yb_ref[...].astype(F32), ps_ref[...]
        dya_ref[...] = (dmf * sa).astype(BF16)
        dyb = dmf * sb
        dyb_ref[...] = (dyb * ps_).astype(BF16)
        dga = dmf * ya_ref[...].astype(F32) * sa * (1.0 - sa)
        dgb = dmf * (ybf * ps_) * sb * (1.0 - sb)
        dg_ref[:, :d] = dga.astype(BF16)
        dg_ref[:, d:] = dgb.astype(BF16)

        @pl.when(pl.program_id(0) == 0)
        def _():
            dba_ref[...] = jnp.zeros_like(dba_ref)
            dbb_ref[...] = jnp.zeros_like(dbb_ref)
            dps_ref[...] = jnp.zeros_like(dps_ref)

        dba_ref[...] += jnp.sum(dga, axis=0, keepdims=True)
        dbb_ref[...] += jnp.sum(dgb, axis=0, keepdims=True)
        dps_ref[...] += jnp.sum(dyb * ybf, axis=0, keepdims=True)

    row = pl.BlockSpec((tm, d), lambda i: (i, 0))
    vec = pl.BlockSpec((1, d), lambda i: (0, 0))
    gates = pl.BlockSpec((tm, 2 * d), lambda i: (i, 1))
    return pl.pallas_call(
        body, name=name, grid=(t // tm,),
        in_specs=[row, gates, pl.BlockSpec((1, 2 * d), lambda i: (0, 0)), row, row, vec] + [ANY] * len(deps),
        out_specs=[row, row, gates, vec, vec, vec],
        out_shape=[jax.ShapeDtypeStruct((t, d), BF16), jax.ShapeDtypeStruct((t, d), BF16),
                   jax.ShapeDtypeStruct(proj.shape, BF16), jax.ShapeDtypeStruct((1, d), F32),
                   jax.ShapeDtypeStruct((1, d), F32), jax.ShapeDtypeStruct((1, d), F32)],
        compiler_params=_cp(("arbitrary",)),
    )(dm, proj, bg, ya, yb, ps, *deps)


def _ffn_up_act(name, h, w_up, gate, part=None, prev=None, deps=()):
    t, d = h.shape
    f = w_up.shape[1]
    tm, tf = _tile(t, 1024), _tile(f, 512)
    j0, j1 = _tile_span(f // tf, part)
    n_prev = 0 if prev is None else 2
    extra = ([] if prev is None else list(prev)) + list(deps)

    def body(h_ref, w_ref, g_ref, *rest):
        u_ref, a_ref = rest[len(extra):]
        u = lax.dot_general(h_ref[...], w_ref[...], _DIMS["nn"], preferred_element_type=F32)
        g = g_ref[...].astype(F32)
        u_ref[...] = u.astype(BF16)
        a_ref[...] = (g * jax.nn.sigmoid(g) * u).astype(BF16)

    blk = pl.BlockSpec((tm, tf), lambda i, j: (i, j0 + j))
    shp = jax.ShapeDtypeStruct((t, f), BF16)
    return pl.pallas_call(
        body, name=name, grid=(t // tm, j1 - j0),
        in_specs=[pl.BlockSpec((tm, d), lambda i, j: (i, 0)), pl.BlockSpec((d, tf), lambda i, j: (0, j0 + j)), blk]
        + [ANY] * len(extra),
        out_specs=[blk, blk], out_shape=[shp, shp], input_output_aliases={3 + i: i for i in range(n_prev)},
        compiler_params=_cp(("parallel", "parallel")))(h, w_up, gate, *extra)


def _ffn_bwd(name, dy, w_down, gate, up):
    t, d = dy.shape
    f = w_down.shape[0]
    tm, tf = _tile(t, 1024), _tile(f, 512)

    def body(dy_ref, w_ref, g_ref, u_ref, dg_ref, du_ref):
        da = lax.dot_general(dy_ref[...], w_ref[...], _DIMS["nt"], preferred_element_type=F32)
        g = g_ref[...].astype(F32)
        s = jax.nn.sigmoid(g)
        du_ref[...] = (da * (g * s)).astype(BF16)
        dg_ref[...] = (da * u_ref[...].astype(F32) * (s * (1.0 + g * (1.0 - s)))).astype(BF16)

    blk = pl.BlockSpec((tm, tf), lambda i, j: (i, j))
    shp = jax.ShapeDtypeStruct((t, f), BF16)
    return pl.pallas_call(
        body, name=name, grid=(t // tm, f // tf),
        in_specs=[pl.BlockSpec((tm, d), lambda i, j: (i, 0)), pl.BlockSpec((tf, d), lambda i, j: (j, 0)), blk, blk],
        out_specs=[blk, blk], out_shape=[shp, shp], compiler_params=_cp(("parallel", "parallel")))(dy, w_down, gate, up)


def _adamw_math(w, g, m, v):
    m = ADAM_B1 * m + (1.0 - ADAM_B1) * g
    v = ADAM_B2 * v + (1.0 - ADAM_B2) * (g * g)
    m_hat = m / (1.0 - ADAM_B1 ** ADAM_STEP)
    v_hat = v / (1.0 - ADAM_B2 ** ADAM_STEP)
    delta = -ADAM_LR * (m_hat / (jnp.sqrt(v_hat) + ADAM_EPS) + ADAM_WD * w)
    return delta, m, v


def _adamw(name, w, g, m, v):
    r, c = w.shape
    tr = _tile8(r, 512 if c <= 1024 else 256)

    def body(w_ref, g_ref, m_ref, v_ref, go_ref, d_ref, nm_ref, nv_ref):
        g = g_ref[...]
        go_ref[...] = g
        d_ref[...], nm_ref[...], nv_ref[...] = _adamw_math(w_ref[...], g, m_ref[...], v_ref[...])

    blk = pl.BlockSpec((tr, c), lambda i: (i, 0))
    shp = jax.ShapeDtypeStruct((r, c), F32)
    return pl.pallas_call(body, name=name, grid=(r // tr,), in_specs=[blk] * 4, out_specs=[blk] * 4,
                          out_shape=[shp] * 4, compiler_params=_cp(("parallel",)))(w, g, m, v)


class _Weight:
    def __init__(self, name, rows, cols, colshard):
        self.name, self.colshard = name, colshard
        self.R, self.nn = rows // 2, cols
        self.P = 1 if colshard else N_CHIPS
        self.N = N_CHIPS * cols if colshard else cols

    def cols(self, k):
        return pl.ds(pl.multiple_of(k * self.nn, LANES), self.nn)

    def shard(self, ref, k):
        return ref.at[0, :, :, self.cols(k)] if self.colshard else ref.at[k]

    def half(self, ref, k, h):
        return ref.at[0, h, :, self.cols(k)] if self.colshard else ref.at[k, h]

    def quarter(self, ref, k, h, q):
        return self.half(ref, k, h).at[pl.ds(q * (self.R // 2), self.R // 2), :]

    def part(self, ref, k):
        return ref.at[0, :, self.cols(k)] if self.colshard else ref.at[k]


def _remote(src, dst, ssem, rsem, dev):
    return pltpu.make_async_remote_copy(src_ref=src, dst_ref=dst, send_sem=ssem, recv_sem=rsem, device_id=dev,
                                        device_id_type=MESH)


def _other_chips(x, y):
    chips = [(1 - x, y), (x, 1 - y), (1 - x, 1 - y)]
    return chips, [2 * cx + cy for cx, cy in chips]


def _hbm(a):
    return pltpu.with_memory_space_constraint(a, pltpu.HBM)


def _gather_start(name, groups, lands, after=()):
    flat = [w for grp in groups for w in grp]
    nw, ng = len(flat), len(groups)

    def body(*refs):
        land = refs[:nw]
        sems = refs[nw + len(after):nw + len(after) + 2 * ng]
        token = refs[2 * nw + len(after) + 2 * ng]
        x, y, c = _mesh_pos()
        k_me = 2 * x + y
        chips, _ = _other_chips(x, y)
        i = 0
        for g, grp in enumerate(groups):
            for wi, w in enumerate(grp):
                mine = w.half(land[i], k_me, c)
                for j, chip in enumerate(chips):
                    _remote(mine, mine, sems[2 * g].at[3 * wi + j], sems[2 * g + 1].at[3 * wi + j], (*chip, c)).start()
                i += 1
        token[...] = jnp.zeros_like(token)

    sem_shapes = []
    for grp in groups:
        sem_shapes += [pltpu.SemaphoreType.DMA((3 * len(grp),))] * 2
    out = pl.pallas_call(
        body, name=name, in_specs=[HBM] * nw + [ANY] * len(after),
        out_specs=[SEM] * (2 * ng) + [HBM] * nw + [VMEM],
        out_shape=sem_shapes + [pltpu.HBM(a.shape, a.dtype) for a in lands] + [jax.ShapeDtypeStruct((8, LANES), F32)],
        input_output_aliases={i: 2 * ng + i for i in range(nw)},
        compiler_params=pltpu.CompilerParams(has_side_effects=EFFECT),
    )(*[_hbm(a) for a in lands], *after)
    sems = [(out[2 * g], out[2 * g + 1]) for g in range(ng)]
    return sems, list(out[2 * ng:2 * ng + nw]), out[-1]


def _gather_wait(name, grp, lands, ssem, rsem, after):
    n = len(grp)

    def body(*refs):
        land, ssem_ref, rsem_ref = refs[:n], refs[n], refs[n + 1]
        x, y, c = _mesh_pos()
        k_me = 2 * x + y
        chips, ks = _other_chips(x, y)
        for wi, w in enumerate(grp):
            for j, chip in enumerate(chips):
                cp = _remote(w.half(land[wi], k_me, c), w.half(land[wi], ks[j], c), ssem_ref.at[3 * wi + j],
                             rsem_ref.at[3 * wi + j], (*chip, c))
                cp.wait_send()
                cp.wait_recv()

    return pl.pallas_call(
        body, name=name, in_specs=[HBM] * n + [SEM, SEM, ANY], out_specs=[HBM] * n,
        out_shape=[pltpu.HBM(a.shape, a.dtype) for a in lands], input_output_aliases={i: i for i in range(n)},
        compiler_params=pltpu.CompilerParams(has_side_effects=EFFECT),
    )(*lands, ssem, rsem, after)


def _split_start(name, arrays, n, copies, after=()):
    na = len(arrays)

    def body(*refs):
        ssem, rsem, token = refs[na + len(after):][0], refs[na + len(after):][1], refs[2 * na + len(after) + 2]
        for i, (src, dst, dev, _) in enumerate(copies(refs[:na], *_mesh_pos())):
            _remote(src, dst, ssem.at[i], rsem.at[i], dev).start()
        token[...] = jnp.zeros_like(token)

    out = pl.pallas_call(
        body, name=name, in_specs=[HBM] * na + [ANY] * len(after), out_specs=[SEM, SEM] + [HBM] * na + [VMEM],
        out_shape=[pltpu.SemaphoreType.DMA((n,))] * 2 + [pltpu.HBM(a.shape, a.dtype) for a in arrays]
        + [jax.ShapeDtypeStruct((8, LANES), F32)],
        input_output_aliases={i: 2 + i for i in range(na)},
        compiler_params=pltpu.CompilerParams(has_side_effects=EFFECT),
    )(*[_hbm(a) for a in arrays], *after)
    return out[0], out[1], list(out[2:2 + na]), out[-1]


def _split_wait(name, arrays, ssem, rsem, copies, after):
    na = len(arrays)

    def body(*refs):
        for i, (src, _, dev, dst) in enumerate(copies(refs[:na], *_mesh_pos())):
            cp = _remote(src, dst, refs[na].at[i], refs[na + 1].at[i], dev)
            cp.wait_send()
            cp.wait_recv()

    return list(pl.pallas_call(
        body, name=name, in_specs=[HBM] * na + [SEM, SEM] + [ANY] * len(after), out_specs=[HBM] * na,
        out_shape=[pltpu.HBM(a.shape, a.dtype) for a in arrays], input_output_aliases={i: i for i in range(na)},
        compiler_params=pltpu.CompilerParams(has_side_effects=EFFECT),
    )(*arrays, ssem, rsem, *after))


def _pass_copies(grp, rels=(0, 1, 2)):
    def copies(land, x, y, c):
        _, ks = _other_chips(x, y)
        return [(w.half(land[wi], ks[j], c), w.half(land[wi], ks[j], c), (x, y, 1 - c), w.half(land[wi], ks[j], 1 - c))
                for wi, w in enumerate(grp) for j in rels]
    copies.n = len(grp) * len(rels)
    return copies


def _near_copies(grp):
    def copies(land, x, y, c):
        chips, ks = _other_chips(x, y)
        out = []
        for wi, w in enumerate(grp):
            mine = w.half(land[wi], 2 * x + y, c)
            out += [(mine, mine, (*chips[j], c), w.half(land[wi], ks[j], c)) for j in (0, 1)]
        return out
    copies.n = 2 * len(grp)
    return copies


def _far_copies(grp):
    def copies(land, x, y, c):
        chips, ks = _other_chips(x, y)
        out = []
        for wi, w in enumerate(grp):
            for j in (0, 1):
                q = w.quarter(land[wi], ks[j], c, j)
                out.append((q, q, (*chips[1 - j], c), w.quarter(land[wi], ks[2], c, j)))
        return out
    copies.n = 2 * len(grp)
    return copies


def _pair_copies(n, whole=False):
    def copies(refs, x, y, c):
        return [(refs[i] if whole else refs[i].at[:, 1 - c], refs[n + i], (x, y, 1 - c), refs[n + i]) for i in range(n)]
    return copies


def _share_copies(n):
    def copies(refs, x, y, c):
        return [(refs[i].at[c], refs[i].at[c], (x, y, 1 - c), refs[i].at[1 - c]) for i in range(n)]
    return copies


def _gather_conv_w(cw, thru):
    ncw = cw.shape[1]

    def body(cw_ref, _, out_ref, __, ssem, rsem):
        x, y, c = _mesh_pos()
        k_me = 2 * x + y
        chips, ks = _other_chips(x, y)
        cols = lambda k: out_ref.at[:, pl.ds(pl.multiple_of(k * ncw, LANES), ncw)]
        cps = [_remote(cw_ref, cols(k_me), ssem.at[j], rsem.at[j], (*chip, c)) for j, chip in enumerate(chips)]
        for cp in cps:
            cp.start()
        for k in range(N_CHIPS):
            @pl.when(k_me == k)
            def _():
                out_ref[:, k * ncw:(k + 1) * ncw] = cw_ref[...]
        for j in range(3):
            _remote(cw_ref, cols(ks[j]), ssem.at[j], rsem.at[j], (*chips[j], c)).wait_recv()
        for cp in cps:
            cp.wait_send()

    return pl.pallas_call(
        body, name="gather_conv_w", in_specs=[VMEM, ANY], out_specs=[VMEM, ANY],
        out_shape=[jax.ShapeDtypeStruct((3, N_CHIPS * ncw), F32), jax.ShapeDtypeStruct(thru.shape, thru.dtype)],
        scratch_shapes=[pltpu.SemaphoreType.DMA((3,)), pltpu.SemaphoreType.DMA((3,))],
        input_output_aliases={1: 1},
    )(cw, thru)


def _grad_tiles(w, n):
    return _tile8(w.R, 512) if w.R <= 512 else w.R // 2, _tile(n, 2048)


def _pair_sum(name, w, pos, grad, got):
    tr, tn = _grad_tiles(w, w.N)

    def body(pos_ref, g_ref, r_ref, o_ref):
        o_ref[...] = (g_ref[...].astype(F32) + r_ref[...].astype(F32)).astype(BF16)

    blk = pl.BlockSpec((None, tr, tn), lambda p, i, j, pos: (p, i, j))
    grid_spec = pltpu.PrefetchScalarGridSpec(
        num_scalar_prefetch=1, grid=(w.P, w.R // tr, w.N // tn),
        in_specs=[pl.BlockSpec((None, None, tr, tn), lambda p, i, j, pos: (p, pos[0], i, j)), blk], out_specs=blk)
    return pl.pallas_call(body, name=name, grid_spec=grid_spec, out_shape=jax.ShapeDtypeStruct((w.P, w.R, w.N), BF16),
                          compiler_params=_cp(("parallel",) * 3))(pos, grad, got)


def _scatter_start(name, ws, pairs):
    nw = len(ws)

    def body(*refs):
        pr, land = refs[:nw], refs[nw:2 * nw]
        ssem, rsem = refs[2 * nw], refs[2 * nw + 1]
        token = refs[4 * nw + 2]
        x, y, c = _mesh_pos()
        chips, ks = _other_chips(x, y)
        for i, w in enumerate(ws):
            for j, chip in enumerate(chips):
                _remote(w.part(pr[i], ks[j]), land[i].at[j], ssem.at[3 * i + j], rsem.at[3 * i + j], (*chip, c)).start()
        token[...] = jnp.zeros_like(token)

    lands = [lax.empty((3, w.R, w.nn), BF16) for w in ws]
    out = pl.pallas_call(
        body, name=name, in_specs=[HBM] * (2 * nw),
        out_specs=[SEM, SEM] + [HBM] * (2 * nw) + [VMEM],
        out_shape=[pltpu.SemaphoreType.DMA((3 * nw,))] * 2 + [pltpu.HBM(a.shape, a.dtype) for a in pairs + lands]
        + [jax.ShapeDtypeStruct((8, LANES), F32)],
        input_output_aliases={i: 2 + i for i in range(2 * nw)},
        compiler_params=pltpu.CompilerParams(has_side_effects=EFFECT),
    )(*[_hbm(a) for a in pairs + lands])
    return out[0], out[1], list(out[2:2 + nw]), list(out[2 + nw:2 + 2 * nw]), out[-1]


def _scatter_wait(name, ws, pairs, lands, ssem, rsem, after):
    nw = len(ws)

    def body(*refs):
        pr, land = refs[:nw], refs[nw:2 * nw]
        ssem_ref, rsem_ref = refs[2 * nw], refs[2 * nw + 1]
        x, y, c = _mesh_pos()
        chips, ks = _other_chips(x, y)
        for i, w in enumerate(ws):
            for j, chip in enumerate(chips):
                cp = _remote(w.part(pr[i], ks[j]), land[i].at[j], ssem_ref.at[3 * i + j], rsem_ref.at[3 * i + j], (*chip, c))
                cp.wait_send()
                cp.wait_recv()

    out = pl.pallas_call(
        body, name=name, in_specs=[HBM] * (2 * nw) + [SEM, SEM] + [ANY] * len(after), out_specs=[HBM] * (2 * nw),
        out_shape=[pltpu.HBM(a.shape, a.dtype) for a in pairs + lands],
        input_output_aliases={i: i for i in range(2 * nw)},
        compiler_params=pltpu.CompilerParams(has_side_effects=EFFECT),
    )(*pairs, *lands, ssem, rsem, *after)
    return list(out[:nw]), list(out[nw:])


def _final_sum(name, w, pos, grad, got, parts):
    tr, tn = _grad_tiles(w, w.nn)
    nbc = w.nn // tn
    if got is None:
        return _final_sum_pair(name, w, pos, grad, parts, tr, tn)

    def body(pos_ref, g_ref, r_ref, p_ref, o_ref):
        acc = g_ref[...].astype(F32) + r_ref[...].astype(F32)
        for j in range(3):
            acc = acc + p_ref[j].astype(F32)
        o_ref[...] = acc

    if w.colshard:
        g_spec = pl.BlockSpec((None, None, tr, tn), lambda i, j, pos: (0, pos[0], i, pos[1] * nbc + j))
        r_spec = pl.BlockSpec((None, tr, tn), lambda i, j, pos: (0, i, pos[1] * nbc + j))
    else:
        g_spec = pl.BlockSpec((None, None, tr, tn), lambda i, j, pos: (pos[1], pos[0], i, j))
        r_spec = pl.BlockSpec((None, tr, tn), lambda i, j, pos: (pos[1], i, j))
    grid_spec = pltpu.PrefetchScalarGridSpec(
        num_scalar_prefetch=1, grid=(w.R // tr, nbc),
        in_specs=[g_spec, r_spec, pl.BlockSpec((3, tr, tn), lambda i, j, pos: (0, i, j))],
        out_specs=pl.BlockSpec((None, tr, tn), lambda i, j, pos: (pos[0], i, j)))
    return pl.pallas_call(body, name=name, grid_spec=grid_spec, out_shape=jax.ShapeDtypeStruct((2, w.R, w.nn), F32),
                          compiler_params=_cp(("parallel",) * 2))(pos, grad, got, parts)


def _final_sum_pair(name, w, pos, pair, parts, tr, tn):
    nbc = w.nn // tn

    def body(pos_ref, g_ref, p_ref, o_ref):
        acc = g_ref[...].astype(F32)
        for j in range(3):
            acc = acc + p_ref[j].astype(F32)
        o_ref[...] = acc

    if w.colshard:
        g_spec = pl.BlockSpec((None, tr, tn), lambda i, j, pos: (0, i, pos[1] * nbc + j))
    else:
        g_spec = pl.BlockSpec((None, tr, tn), lambda i, j, pos: (pos[1], i, j))
    grid_spec = pltpu.PrefetchScalarGridSpec(
        num_scalar_prefetch=1, grid=(w.R // tr, nbc),
        in_specs=[g_spec, pl.BlockSpec((3, tr, tn), lambda i, j, pos: (0, i, j))],
        out_specs=pl.BlockSpec((None, tr, tn), lambda i, j, pos: (pos[0], i, j)))
    return pl.pallas_call(body, name=name, grid_spec=grid_spec, out_shape=jax.ShapeDtypeStruct((2, w.R, w.nn), F32),
                          compiler_params=_cp(("parallel",) * 2))(pos, pair, parts)


VEC_ROWS = 16


def _vector_step(d, n_conv, parts, params, deps=()):
    ncw = params[2][0].shape[1]
    n_par = len(params)

    def body(*refs):
        dg1, dba, dbb, dcw, dcb, dps, dg2, dgf, lc = refs[:9]
        wmv = refs[9:9 + 3 * n_par]
        refs = refs[9 + 3 * n_par + len(deps):]
        outs = refs[:4 * n_par]
        loss_ref = refs[4 * n_par]
        snd, got, ssem, rsem = refs[4 * n_par + 1:]
        x, y, c = _mesh_pos()
        me = 4 * x + 2 * y + c
        snd[...] = jnp.zeros_like(snd)
        for row, ref in ((0, dg1), (1, dba), (2, dbb), (3, dps), (4, dg2), (5, dgf), (6, lc)):
            snd[row:row + 1, :] = ref[...]
        snd[7:8, :n_conv] = dcb[...]
        snd[8:11, :n_conv] = dcw[...]
        cps = []
        for r in range(1, N_DEV):
            peer = tuple(1 - p if (r >> b) & 1 else p for p, b in ((x, 2), (y, 1), (c, 0)))
            cps.append(_remote(snd, got.at[me], ssem.at[r - 1], rsem.at[r - 1], peer))
        for cp in cps:
            cp.start()
        got[me] = snd[...]
        for r in range(1, N_DEV):
            peer = tuple(1 - p if (r >> b) & 1 else p for p, b in ((x, 2), (y, 1), (c, 0)))
            _remote(snd, got.at[4 * peer[0] + 2 * peer[1] + peer[2]], ssem.at[r - 1], rsem.at[r - 1], peer).wait_recv()
        for cp in cps:
            cp.wait_send()
        tot = got[0]
        for dev in range(1, N_DEV):
            tot = tot + got[dev]
        loss_ref[...] = jnp.sum(tot[6:7, :], axis=1, keepdims=True)
        k_me = 2 * x + y
        g_cw = jnp.zeros((3, ncw), F32)
        for k in range(N_CHIPS):
            g_cw = g_cw + jnp.where(k_me == k, tot[8:11, k * ncw:(k + 1) * ncw], 0.0)
        grads = [tot[0:1, :], jnp.concatenate([tot[1:2, :], tot[2:3, :]], axis=1), g_cw, tot[7:8, :n_conv],
                 tot[3:4, :], tot[4:5, :], tot[5:6, :]]
        for i, g in enumerate(grads):
            w_ref, m_ref, v_ref = wmv[3 * i:3 * i + 3]
            delta, nm, nv = _adamw_math(w_ref[...], g, m_ref[...], v_ref[...])
            outs[4 * i][...] = g
            outs[4 * i + 1][...] = delta
            outs[4 * i + 2][...] = nm
            outs[4 * i + 3][...] = nv

    args = list(parts)
    out_shape = []
    for w, m, v in params:
        args += [w, m, v]
        out_shape += [jax.ShapeDtypeStruct(w.shape, F32)] * 4
    out_shape.append(jax.ShapeDtypeStruct((1, 1), F32))
    return pl.pallas_call(
        body, name="vector_params_step", in_specs=[VMEM] * len(args) + [ANY] * len(deps),
        out_specs=[VMEM] * len(out_shape), out_shape=out_shape,
        scratch_shapes=[pltpu.VMEM((VEC_ROWS, d), F32), pltpu.VMEM((N_DEV, VEC_ROWS, d), F32),
                        pltpu.SemaphoreType.DMA((N_DEV - 1,)), pltpu.SemaphoreType.DMA((N_DEV - 1,))],
        compiler_params=pltpu.CompilerParams(vmem_limit_bytes=VMEM_LIMIT),
    )(*args, *deps)


def kernel(x, norm1_g, w_in, b_gate, conv_w, conv_b, w_a_out, w_pool, pool_scale, w_o, norm2_g, w_ffn_gate, w_ffn_up, w_ffn_down, final_g, loss_target, m_norm1_g, m_w_in, m_b_gate, m_conv_w, m_conv_b, m_w_a_out, m_w_pool, m_pool_scale, m_w_o, m_norm2_g, m_w_ffn_gate, m_w_ffn_up, m_w_ffn_down, m_final_g, v_norm1_g, v_w_in, v_b_gate, v_conv_w, v_conv_b, v_w_a_out, v_w_pool, v_pool_scale, v_w_o, v_norm2_g, v_w_ffn_gate, v_w_ffn_up, v_w_ffn_down, v_final_g):
    t, d = x.shape[1], x.shape[2]
    n_conv = conv_b.shape[1]
    n_groups, pool_cg, pool_dg = w_pool.shape[1], w_pool.shape[2], N_CHIPS * w_pool.shape[3]
    d_ff = N_CHIPS * w_ffn_gate.shape[2]
    assert n_conv // n_groups == pool_cg and n_conv % (n_groups * MIX_COLS) == 0 and n_groups == len(POOL_WINDOWS)

    big = {"w_in": (w_in, m_w_in, v_w_in), "w_a_out": (w_a_out, m_w_a_out, v_w_a_out), "w_pool": (w_pool, m_w_pool, v_w_pool),
           "w_o": (w_o, m_w_o, v_w_o), "w_ffn_gate": (w_ffn_gate, m_w_ffn_gate, v_w_ffn_gate),
           "w_ffn_up": (w_ffn_up, m_w_ffn_up, v_w_ffn_up), "w_ffn_down": (w_ffn_down, m_w_ffn_down, v_w_ffn_down)}
    colshard = {"w_in": True, "w_a_out": True, "w_pool": True, "w_o": False, "w_ffn_gate": True, "w_ffn_up": True,
                "w_ffn_down": False}
    names = list(big)
    shard2d = {n: big[n][0].reshape(-1, big[n][0].shape[-1]) for n in names}
    ws = [_Weight(n, *shard2d[n].shape, colshard[n]) for n in names]

    xs, tgt = x[0], loss_target[0]
    cw_loc = conv_w[0]
    pos = jnp.stack([lax.axis_index("c"), 2 * lax.axis_index("x") + lax.axis_index("y")]).astype(jnp.int32)
    by_name = {w.name: w for w in ws}
    groups = [[by_name[n] for n in g] for g in (["w_in"], ["w_a_out", "w_pool", "w_o"], ["w_ffn_gate"], ["w_ffn_up"],
                                                 ["w_ffn_down"])]
    first = [sum(len(g) for g in groups[:i]) for i in range(len(groups))]
    rgroups = [groups[0], groups[1], groups[2] + groups[3], groups[4]]

    cast = lambda w, dep: _cast_place(f"cast_{w.name}", w, pos, shard2d[w.name].reshape(2, w.R, w.nn), deps=dep)
    chips, ks = _other_chips(lax.axis_index("x"), lax.axis_index("y"))
    kvec = jnp.stack([pos[1], *ks]).astype(jnp.int32)
    full = {}

    def start(name, arrays, copies, after=()):
        ssem, rsem, arrays, token = _split_start(name, arrays, copies.n, copies, after)
        return name, arrays, ssem, rsem, copies, token

    def wait(started, after):
        name, arrays, ssem, rsem, copies, _ = started
        return _split_wait(name + "_wait", arrays, ssem, rsem, copies, after)

    def pass_on(g, got, after=()):
        return start(f"pass_{g}", got, _pass_copies(groups[g]), after)

    def passed(g, st, after=None):
        got = wait(st, [st[5]] if after is None else after)
        full.update({w.name: a.reshape(w.P * 2 * w.R, w.N) for w, a in zip(groups[g], got)})

    near = start("near_0", [cast(w, []) for w in groups[0]], _near_copies(groups[0]))
    rest = [cast(w, [near[5]]) for grp in groups[1:] for w in grp]
    h1 = _rms_fwd("norm1_fwd", xs, norm1_g, deps=[near[5]])
    proj = _proj_piece("proj_own", h1, shard2d["w_in"], None, kvec, 0, 1, deps=rest)
    got = wait(near, [proj])
    far = start("far_0", got, _far_copies(groups[0]))
    sems_b, lands_b, tok_b = _gather_start("gather_start_b", groups[1:2], rest[:3], after=[far[5]])
    st = start("pass_near_0", far[1], _pass_copies(groups[0], (0, 1)), [tok_b])
    got = wait(st, [st[5]])
    proj = _proj_piece("proj_near", h1, got[0].reshape(-1, groups[0][0].N), proj, kvec, 1, 2)
    st = start("pass_far_0", wait((far[0], got) + far[2:], [proj]), _pass_copies(groups[0], (2,)))
    got = wait(st, [st[5]])
    w_in_full = got[0].reshape(-1, groups[0][0].N)
    proj = _proj_piece("proj_far", h1, w_in_full, proj, kvec, 3, 1)
    cw_full, proj = _gather_conv_w(cw_loc, proj)
    got = _gather_wait("gather_wait_1", groups[1], lands_b, *sems_b[0], proj)
    near_g = start("near_2", rest[3:4], _near_copies(groups[2]), got)
    st = pass_on(1, got, [near_g[5]])
    z, p = _mixer_fwd("mixer_fwd", proj, cw_full, conv_b, n_conv, n_groups, deps=[st[5]])
    passed(1, st, [z])
    wp_full = full["w_pool"].reshape(n_groups, pool_cg, pool_dg)
    ya = _mm_nn("conv_out", z, full["w_a_out"], BF16)
    yb = _gmm_nn("pool_out", p, wp_full, BF16)
    merged = _merge_fwd("merge_fwd", proj, b_gate, ya, yb, pool_scale)
    far_g = start("far_2", wait(near_g, [merged]), _far_copies(groups[2]))
    near_u = start("near_3", rest[4:5], _near_copies(groups[3]), [far_g[5]])
    x2 = _mm_nn("mix_out", merged, full["w_o"], F32, add=xs, deps=[near_u[5]])
    st = pass_on(2, wait(far_g, [x2]))
    h2 = _rms_fwd("norm2_fwd", x2, norm2_g, deps=[st[5]])
    passed(2, st, [h2])
    def far_and_pass(g, near_st, after):
        far_st = start(f"far_{g}", wait(near_st, after), _far_copies(groups[g]))
        return far_st, start(f"pass_near_{g}", far_st[1], _pass_copies(groups[g], (0, 1)), [far_st[5]])

    def finish(g, far_st, pass_st, after):
        got = wait(pass_st, after)
        st = start(f"pass_far_{g}", wait((far_st[0], got) + far_st[2:], after), _pass_copies(groups[g], (2,)))
        passed(g, st)

    gate = _mm_nn("ffn_gate_a", h2, full["w_ffn_gate"], BF16, part=(0, 2))
    far_u, pass_u = far_and_pass(3, near_u, [gate])
    near_d = start("near_4", rest[5:6], _near_copies(groups[4]), [pass_u[5]])
    gate = _mm_nn("ffn_gate_b", h2, full["w_ffn_gate"], BF16, part=(1, 2), prev=gate, deps=[near_d[5]])
    finish(3, far_u, pass_u, [gate])
    up_act = _ffn_up_act("ffn_up_act_a", h2, full["w_ffn_up"], gate, part=(0, 2))
    far_d, pass_d = far_and_pass(4, near_d, [up_act[0]])
    up, act = _ffn_up_act("ffn_up_act_b", h2, full["w_ffn_up"], gate, part=(1, 2), prev=up_act, deps=[pass_d[5]])
    finish(4, far_d, pass_d, [act])
    x3 = _mm_nn("ffn_down", act, full["w_ffn_down"], F32, add=x2, tiles=(512, 512))

    pending = {}

    def pair_start(g, grads):
        grp = rgroups[g]
        gcan = [grads[w.name].reshape(w.P, 2, w.R, w.N) for w in grp]
        slots = [lax.empty((w.P, w.R, w.N), BF16) for w in grp]
        pending[g] = _split_start(f"pair_start_{g}", gcan + slots, len(grp), _pair_copies(len(grp)))
        return pending[g][3]

    def scatter_start(g, after):
        grp = rgroups[g]
        n = len(grp)
        ssem, rsem, arrs, _ = pending[g]
        arrs = _split_wait(f"pair_wait_{g}", arrs, ssem, rsem, _pair_copies(n), after)
        gcan, sib = arrs[:n], arrs[n:]
        pairs = [_pair_sum(f"pair_sum_{w.name}", w, pos, a, s) for w, a, s in zip(grp, gcan, sib)]
        ssem, rsem, pairs, slots, token = _scatter_start(f"scatter_start_{g}", grp, pairs)
        pending[g] = (gcan, sib, pairs, slots, ssem, rsem)
        return token

    def pair_start_halves(g, ab, deps):
        grp = rgroups[g]
        sent = [_mm_tn_half(f"d{w.name}_sib", a, b, pos, False, deps=deps if i == 0 else ()) for i, (w, (a, b)) in enumerate(zip(grp, ab))]
        slots = [lax.empty((1, w.R, w.N), BF16) for w in grp]
        pending[g] = _split_start(f"pair_start_{g}", sent + slots, len(grp), _pair_copies(len(grp), whole=True))
        return pending[g][3]

    def scatter_start_halves(g, ab, after):
        grp = rgroups[g]
        n = len(grp)
        ssem, rsem, arrs, _ = pending[g]
        arrs = _split_wait(f"pair_wait_{g}", arrs, ssem, rsem, _pair_copies(n, whole=True), after)
        pairs = [_mm_tn_half(f"d{w.name}_own", a, b, pos, True, add=s) for w, (a, b), s in zip(grp, ab, arrs[n:])]
        ssem, rsem, pairs, slots, token = _scatter_start(f"scatter_start_{g}", grp, pairs)
        pending[g] = (None, None, pairs, slots, ssem, rsem)
        return token

    def reduce_finish(g, after):
        grp = rgroups[g]
        gcan, sib, pairs, slots, ssem, rsem = pending[g]
        pairs, parts = _scatter_wait(f"scatter_wait_{g}", grp, pairs, slots, ssem, rsem, after)
        if gcan is None:
            return [_final_sum(f"final_sum_{w.name}", w, pos, a, None, q) for w, a, q in zip(grp, pairs, parts)]
        return [_final_sum(f"final_sum_{w.name}", w, pos, a, s, q) for w, a, s, q in zip(grp, gcan, sib, parts)]

    grads = {}
    dx3, dx3b, d_gf, loss_cols = _final_bwd("final_bwd", x3, final_g.reshape(1, d), tgt)
    dgate, dup = _ffn_bwd("ffn_bwd", dx3b, full["w_ffn_down"], gate, up)
    grads["w_ffn_down"] = _mm_tn("dw_ffn_down", act, dx3b, BF16)
    tok = pair_start(3, grads)
    dh2 = _mm_nt("d_h2", [(dgate, full["w_ffn_gate"]), (dup, full["w_ffn_up"])], BF16, tk=d_ff // 4, deps=[tok])
    tok = scatter_start(3, [dh2])
    tok = pair_start_halves(2, [(h2, dgate), (h2, dup)], [tok])
    dx2, dx2b, d_g2 = _rms_bwd("norm2_bwd", x2, norm2_g, dh2, dx3, True, deps=[tok])
    dmerged = _mm_nt("d_merged", [(dx2b, full["w_o"])], BF16, tk=d)
    grads["w_o"] = _mm_tn("dw_o", merged, dx2b, BF16)
    tok = scatter_start_halves(2, [(h2, dgate), (h2, dup)], [grads["w_o"]])
    dya, dyb, dproj, d_bga, d_bgb, d_ps = _merge_bwd("merge_bwd", dmerged, proj, b_gate, ya, yb, pool_scale, deps=[tok])
    dz = _mm_nt("d_z", [(dya, full["w_a_out"])], BF16, tk=d)
    grads["w_a_out"] = _mm_tn("dw_a_out", z, dya, BF16)
    dp = _gmm_nt("d_pool", dyb, wp_full, BF16)
    grads["w_pool"] = _gmm_tn("dw_pool", p, dyb, n_groups, BF16)
    tok = pair_start(1, grads)
    dproj, d_cw, d_cb = _mixer_bwd("mixer_bwd", dz, dp, proj, cw_full, conv_b, dproj, n_conv, n_groups, deps=[tok])
    tok = scatter_start(1, [dproj])
    tok = pair_start_halves(0, [(h1, dproj)], [tok])
    dh1 = _mm_nt("d_h1", [(dproj, w_in_full)], BF16, tk=proj.shape[1] // 4, deps=[tok])
    tok = scatter_start_halves(0, [(h1, dproj)], [dh1])
    grad_x, d_g1 = _rms_bwd("norm1_bwd", xs, norm1_g, dh1, dx2, False, deps=[tok])

    g_big, d_big, m_big, v_big = {}, {}, {}, {}

    def update(wsub, shared):
        out = []
        for w, g in zip(wsub, shared):
            wt, mt, vt = big[w.name]
            g2 = g.reshape(2 * w.R, w.nn)
            go, dl, nm, nv = _adamw(f"adamw_{w.name}", shard2d[w.name], g2, mt.reshape(g2.shape), vt.reshape(g2.shape))
            g_big[w.name], d_big[w.name], m_big[w.name], v_big[w.name] = (a.reshape(wt.shape) for a in (go, dl, nm, nv))
            out.append(nv)
        return out

    after = [grad_x]
    started = []
    for g in (3, 2, 1):
        halves = reduce_finish(g, after)
        share = _share_copies(len(halves))
        ssem, rsem, halves, tok = _split_start(f"share_start_{g}", halves, len(halves), share)
        started.append((g, ssem, rsem, halves, share))
        after = [tok]
    for g, ssem, rsem, halves, share in started:
        after = update(rgroups[g], _split_wait(f"share_wait_{g}", halves, ssem, rsem, share, after))
    share = _share_copies(1)
    ssem, rsem, halves, tok = _split_start("share_start_0", reduce_finish(0, after), 1, share)

    vec_names = ["norm1_g", "b_gate", "conv_w", "conv_b", "pool_scale", "norm2_g", "final_g"]
    vec = {"norm1_g": (norm1_g, m_norm1_g, v_norm1_g), "b_gate": (b_gate, m_b_gate, v_b_gate),
           "conv_w": (cw_loc, m_conv_w[0], v_conv_w[0]), "conv_b": (conv_b, m_conv_b, v_conv_b),
           "pool_scale": (pool_scale, m_pool_scale, v_pool_scale), "norm2_g": (norm2_g, m_norm2_g, v_norm2_g),
           "final_g": tuple(a.reshape(1, d) for a in (final_g, m_final_g, v_final_g))}
    vout = _vector_step(d, n_conv, [d_g1, d_bga, d_bgb, d_cw, d_cb, d_ps, d_g2, d_gf, loss_cols],
                        [vec[n] for n in vec_names], deps=halves)
    update(rgroups[0], _split_wait("share_wait_0", halves, ssem, rsem, share, []))

    shapes = {"conv_w": conv_w.shape, "final_g": final_g.shape}
    g_vec, d_vec, m_vec, v_vec = ({n: vout[4 * i + q].reshape(shapes.get(n, vec[n][0].shape)) for i, n in enumerate(vec_names)}
                                  for q in range(4))
    loss = vout[-1].reshape(())

    order = ["norm1_g", "w_in", "b_gate", "conv_w", "conv_b", "w_a_out", "w_pool", "pool_scale", "w_o", "norm2_g",
             "w_ffn_gate", "w_ffn_up", "w_ffn_down", "final_g"]
    pick = lambda vecs, bigs: [vecs[n] if n in vecs else bigs[n] for n in order]
    return (loss, grad_x.reshape(x.shape), *pick(g_vec, g_big), *pick(d_vec, d_big), *pick(m_vec, m_big),
            *pick(v_vec, v_big))
```

```python
import functools

import jax
import jax.numpy as jnp
from jax import lax
from jax.experimental import pallas as pl
from jax.experimental.pallas import tpu as pltpu

F32, BF16 = jnp.float32, jnp.bfloat16
MESH = pl.DeviceIdType.MESH
ANY = pl.BlockSpec(memory_space=pl.ANY)
VMEM = pl.BlockSpec(memory_space=pltpu.VMEM)
HBM = pl.BlockSpec(memory_space=pltpu.HBM)
SEM = pl.BlockSpec(memory_space=pltpu.SEMAPHORE)
EFFECT = pltpu.SideEffectType.DATAFLOW_SIDE_EFFECTING

EPS = 1e-6
POOL_WINDOWS = (2, 4, 8, 16)
ADAM_LR, ADAM_B1, ADAM_B2, ADAM_EPS, ADAM_WD, ADAM_STEP = 0.001, 0.9, 0.999, 1e-08, 0.01, 10

V7X_VMEM_BYTES = 64 * 1024 * 1024
VMEM_LIMIT = V7X_VMEM_BYTES * 3 // 4
LANES = 128
COL_TILE = 8 * LANES
N_CHIPS = 4
N_DEV = 8

_DIMS = {
    "nn": (((1,), (0,)), ((), ())),
    "nt": (((1,), (1,)), ((), ())),
    "tn": (((0,), (0,)), ((), ())),
}


def _cp(sem):
    return pltpu.CompilerParams(dimension_semantics=sem, vmem_limit_bytes=VMEM_LIMIT)


def _mesh_pos():
    return lax.axis_index("x"), lax.axis_index("y"), lax.axis_index("c")


def _mm(name, pairs, *, mode, grid, out_shape, o_spec, nk=1, kaxis=None, add=None, deps=(), prev=None):
    npair = len(pairs)
    has_add = add is not None

    def body(*refs):
        ab = refs[: 2 * npair]
        pos = 2 * npair
        add_ref = refs[pos] if has_add else None
        pos += int(has_add) + len(deps) + (prev is not None)
        o_ref = refs[pos]
        acc_ref = refs[pos + 1] if nk > 1 else None
        d = None
        for p in range(npair):
            t = lax.dot_general(ab[2 * p][...], ab[2 * p + 1][...], _DIMS[mode], preferred_element_type=F32)
            d = t if d is None else d + t
        if nk == 1:
            if has_add:
                d = d + add_ref[...].astype(F32)
            o_ref[...] = d.astype(o_ref.dtype)
        else:
            k = pl.program_id(kaxis)

            @pl.when(k == 0)
            def _():
                acc_ref[...] = d

            @pl.when(k > 0)
            def _():
                acc_ref[...] += d

            @pl.when(k == nk - 1)
            def _():
                r = acc_ref[...]
                if has_add:
                    r = r + add_ref[...].astype(F32)
                o_ref[...] = r.astype(o_ref.dtype)

    args, specs = [], []
    for a, a_spec, b, b_spec in pairs:
        args += [a, b]
        specs += [a_spec, b_spec]
    if has_add:
        args.append(add[0])
        specs.append(add[1])
    args += list(deps)
    specs += [ANY] * len(deps)
    aliases = {}
    if prev is not None:
        aliases = {len(args): 0}
        args.append(prev)
        specs.append(ANY)
    scratch = []
    if nk > 1:
        blk = [d for d in o_spec.block_shape if d is not None]
        scratch = [pltpu.VMEM(tuple(blk), F32)]
    sem = tuple("arbitrary" if (nk > 1 and ax == kaxis) else "parallel" for ax in range(len(grid)))
    return pl.pallas_call(
        body, name=name, grid=grid, in_specs=specs, out_specs=o_spec, out_shape=out_shape,
        scratch_shapes=scratch, input_output_aliases=aliases, compiler_params=_cp(sem),
    )(*args)


def _tile_span(n_tiles, part):
    if part is None:
        return 0, n_tiles
    p, of = part
    return p * n_tiles // of, (p + 1) * n_tiles // of


def _tile(n, pref):
    if n <= pref:
        return n
    for t in range(pref, 0, -LANES):
        if t % LANES == 0 and n % t == 0:
            return t
    raise ValueError(f"no tile for {n}")


def _mm_nn(name, a, b, out_dtype, add=None, tk=None, deps=(), part=None, prev=None, tiles=None):
    m, kk = a.shape
    n = b.shape[1]
    tm, tn = _tile(m, 1024), _tile(n, COL_TILE)
    if tiles is not None:
        tm, tn = _tile(m, tiles[0]), _tile(n, tiles[1])
    out_shape = jax.ShapeDtypeStruct((m, n), out_dtype)
    if tk is None or tk == kk:
        j0, j1 = _tile_span(n // tn, part)
        grid = (m // tm, j1 - j0)
        pairs = [(a, pl.BlockSpec((tm, kk), lambda i, j: (i, 0)), b, pl.BlockSpec((kk, tn), lambda i, j: (0, j0 + j)))]
        o_spec = pl.BlockSpec((tm, tn), lambda i, j: (i, j0 + j))
        add_ = None if add is None else (add, pl.BlockSpec((tm, tn), lambda i, j: (i, j0 + j)))
        return _mm(name, pairs, mode="nn", grid=grid, out_shape=out_shape, o_spec=o_spec, add=add_, deps=deps, prev=prev)
    tn = _tile(n, 1024)
    nk = kk // tk
    grid = (m // tm, n // tn, nk)
    pairs = [(a, pl.BlockSpec((tm, tk), lambda i, j, k: (i, k)), b, pl.BlockSpec((tk, tn), lambda i, j, k: (k, j)))]
    o_spec = pl.BlockSpec((tm, tn), lambda i, j, k: (i, j))
    add_ = None if add is None else (add, pl.BlockSpec((tm, tn), lambda i, j, k: (i, j)))
    return _mm(name, pairs, mode="nn", grid=grid, out_shape=out_shape, o_spec=o_spec, nk=nk, kaxis=2, add=add_, deps=deps)


def _mm_nt(name, abs_, out_dtype, tk, deps=()):
    m, kk = abs_[0][0].shape
    n = abs_[0][1].shape[0]
    tm = _tile(m, 1024)
    nk = kk // tk
    tn = _tile(n, COL_TILE if nk == 1 else 1024)
    out_shape = jax.ShapeDtypeStruct((m, n), out_dtype)
    if nk == 1:
        grid = (m // tm, n // tn)
        pairs = [(a, pl.BlockSpec((tm, kk), lambda i, j: (i, 0)), b, pl.BlockSpec((tn, kk), lambda i, j: (j, 0)))
                 for a, b in abs_]
        o_spec = pl.BlockSpec((tm, tn), lambda i, j: (i, j))
        return _mm(name, pairs, mode="nt", grid=grid, out_shape=out_shape, o_spec=o_spec, deps=deps)
    grid = (m // tm, n // tn, nk)
    pairs = [(a, pl.BlockSpec((tm, tk), lambda i, j, k: (i, k)), b, pl.BlockSpec((tn, tk), lambda i, j, k: (j, k)))
             for a, b in abs_]
    o_spec = pl.BlockSpec((tm, tn), lambda i, j, k: (i, j))
    return _mm(name, pairs, mode="nt", grid=grid, out_shape=out_shape, o_spec=o_spec, nk=nk, kaxis=2, deps=deps)


def _mm_tn(name, a, b, out_dtype, deps=()):
    t, m = a.shape
    n = b.shape[1]
    tm, tn = _tile(m, 512), _tile(n, 2048)
    if n > m:
        grid = (n // tn, m // tm)
        a_map, b_map, o_map = (lambda j, i: (0, i)), (lambda j, i: (0, j)), (lambda j, i: (i, j))
    else:
        grid = (m // tm, n // tn)
        a_map, b_map, o_map = (lambda i, j: (0, i)), (lambda i, j: (0, j)), (lambda i, j: (i, j))
    pairs = [(a, pl.BlockSpec((t, tm), a_map), b, pl.BlockSpec((t, tn), b_map))]
    o_spec = pl.BlockSpec((tm, tn), o_map)
    return _mm(name, pairs, mode="tn", grid=grid, out_shape=jax.ShapeDtypeStruct((m, n), out_dtype), o_spec=o_spec,
               deps=deps)


def _mm_tn_half(name, a, b, pos, mine, add=None, deps=()):
    t, m = a.shape
    r, n = m // 2, b.shape[1]
    tm, tn = _tile(r, 512), _tile(n, 2048)
    nbi = r // tm
    half = (lambda pos: pos[0]) if mine else (lambda pos: 1 - pos[0])
    if n > r:
        grid, ij = (n // tn, nbi), (lambda g0, g1: (g1, g0))
    else:
        grid, ij = (nbi, n // tn), (lambda g0, g1: (g0, g1))
    has_add = add is not None

    def body(pos_ref, a_ref, b_ref, *rest):
        d = lax.dot_general(a_ref[...], b_ref[...], _DIMS["tn"], preferred_element_type=F32)
        if has_add:
            d = d + rest[0][...].astype(F32)
        rest[-1][...] = d.astype(BF16)

    o_spec = pl.BlockSpec((None, tm, tn), lambda g0, g1, pos: (0, *ij(g0, g1)))
    grid_spec = pltpu.PrefetchScalarGridSpec(
        num_scalar_prefetch=1, grid=grid,
        in_specs=[pl.BlockSpec((t, tm), lambda g0, g1, pos: (0, half(pos) * nbi + ij(g0, g1)[0])),
                  pl.BlockSpec((t, tn), lambda g0, g1, pos: (0, ij(g0, g1)[1]))]
        + ([o_spec] if has_add else []) + [ANY] * len(deps),
        out_specs=o_spec)
    return pl.pallas_call(body, name=name, grid_spec=grid_spec, out_shape=jax.ShapeDtypeStruct((1, r, n), BF16),
                          compiler_params=_cp(("parallel",) * 2))(pos, a, b, *([add] if has_add else []), *deps)


def _proj_piece(name, h, w, prev, kvec, base, count, deps=()):
    t, kk = h.shape
    own = w.dtype == F32
    nn = w.shape[1] if own else w.shape[1] // N_CHIPS
    tm, tn = _tile(t, 1024), _tile(nn, COL_TILE)
    nb = nn // tn

    def body(kv_ref, h_ref, w_ref, *rest):
        rest[-1][...] = lax.dot_general(h_ref[...], w_ref[...].astype(BF16), _DIMS["nn"],
                                        preferred_element_type=F32).astype(BF16)

    cols = lambda s, i, j, kv: (0, j) if own else (0, kv[base + s] * nb + j)
    extra = ([] if prev is None else [prev]) + list(deps)
    grid_spec = pltpu.PrefetchScalarGridSpec(
        num_scalar_prefetch=1, grid=(count, t // tm, nb),
        in_specs=[pl.BlockSpec((tm, kk), lambda s, i, j, kv: (i, 0)), pl.BlockSpec((kk, tn), cols)] + [ANY] * len(extra),
        out_specs=pl.BlockSpec((tm, tn), lambda s, i, j, kv: (i, kv[base + s] * nb + j)))
    return pl.pallas_call(body, name=name, grid_spec=grid_spec, out_shape=jax.ShapeDtypeStruct((t, N_CHIPS * nn), BF16),
                          input_output_aliases={} if prev is None else {3: 0},
                          compiler_params=_cp(("parallel",) * 3))(kvec, h, w, *extra)


def _gmm_nn(name, p, w, out_dtype):
    t = p.shape[0]
    g, cg, dg = w.shape
    tm = _tile(t, 1024)
    pairs = [(p, pl.BlockSpec((tm, cg), lambda i, j: (i, j)), w, pl.BlockSpec((None, cg, dg), lambda i, j: (j, 0, 0)))]
    o_spec = pl.BlockSpec((tm, dg), lambda i, j: (i, j))
    return _mm(name, pairs, mode="nn", grid=(t // tm, g), out_shape=jax.ShapeDtypeStruct((t, g * dg), out_dtype),
               o_spec=o_spec)


def _gmm_nt(name, dy, w, out_dtype):
    t = dy.shape[0]
    g, cg, dg = w.shape
    tm = _tile(t, 1024)
    pairs = [(dy, pl.BlockSpec((tm, dg), lambda i, j: (i, j)), w, pl.BlockSpec((None, cg, dg), lambda i, j: (j, 0, 0)))]
    o_spec = pl.BlockSpec((tm, cg), lambda i, j: (i, j))
    return _mm(name, pairs, mode="nt", grid=(t // tm, g), out_shape=jax.ShapeDtypeStruct((t, g * cg), out_dtype),
               o_spec=o_spec)


def _gmm_tn(name, p, dy, g, out_dtype):
    t = p.shape[0]
    cg, dg = p.shape[1] // g, dy.shape[1] // g
    pairs = [(p, pl.BlockSpec((t, cg), lambda j: (0, j)), dy, pl.BlockSpec((t, dg), lambda j: (0, j)))]
    o_spec = pl.BlockSpec((None, cg, dg), lambda j: (j, 0, 0))
    return _mm(name, pairs, mode="tn", grid=(g,), out_shape=jax.ShapeDtypeStruct((g, cg, dg), out_dtype), o_spec=o_spec)


ROW_TILE = 256


def _rows(t):
    return _tile8(t, ROW_TILE)


def _tile8(n, pref):
    if n <= pref:
        return n
    for t in range(pref, 0, -8):
        if n % t == 0:
            return t
    raise ValueError(f"no row tile for {n}")


def _cast_place(name, w, pos, shard, deps=()):
    tr = _tile8(w.R, 512)
    if w.colshard:
        o_map = lambda h, i, pos: (0, h, i, pos[1])
    else:
        o_map = lambda h, i, pos: (pos[1], h, i, 0)

    def body(pos_ref, w_ref, *rest):
        rest[-1][...] = w_ref[...].astype(BF16)

    grid_spec = pltpu.PrefetchScalarGridSpec(
        num_scalar_prefetch=1, grid=(2, w.R // tr),
        in_specs=[pl.BlockSpec((None, tr, w.nn), lambda h, i, pos: (h, i, 0))] + [ANY] * len(deps),
        out_specs=pl.BlockSpec((None, None, tr, w.nn), o_map))
    return pl.pallas_call(body, name=name, grid_spec=grid_spec, out_shape=jax.ShapeDtypeStruct((w.P, 2, w.R, w.N), BF16),
                          compiler_params=_cp(("parallel", "parallel")))(pos, shard, *deps)


def _rms_fwd(name, x, g, deps=()):
    t, d = x.shape
    tm = _rows(t)

    def body(x_ref, g_ref, *rest):
        xf = x_ref[...]
        r = lax.rsqrt(jnp.mean(xf * xf, axis=-1, keepdims=True) + EPS)
        rest[-1][...] = (xf * r * g_ref[...]).astype(BF16)

    return pl.pallas_call(
        body, name=name, grid=(t // tm,),
        in_specs=[pl.BlockSpec((tm, d), lambda i: (i, 0)), pl.BlockSpec((1, d), lambda i: (0, 0))] + [ANY] * len(deps),
        out_specs=pl.BlockSpec((tm, d), lambda i: (i, 0)), out_shape=jax.ShapeDtypeStruct((t, d), BF16),
        compiler_params=_cp(("parallel",)),
    )(x, g, *deps)


def _rms_bwd(name, x, g, dh, dres, want_bf16, deps=()):
    t, d = x.shape
    tm = _rows(t)

    def body(x_ref, g_ref, dh_ref, dres_ref, *rest):
        rest = rest[len(deps):]
        dx_ref, rest = rest[0], rest[1:]
        dg_ref = rest[-1]
        xf = x_ref[...]
        r = lax.rsqrt(jnp.mean(xf * xf, axis=-1, keepdims=True) + EPS)
        xh = xf * r
        dhf = dh_ref[...].astype(F32)
        dxh = dhf * g_ref[...]
        m = jnp.mean(dxh * xh, axis=-1, keepdims=True)
        dx = dres_ref[...] + r * (dxh - xh * m)
        dx_ref[...] = dx
        if want_bf16:
            rest[0][...] = dx.astype(BF16)

        @pl.when(pl.program_id(0) == 0)
        def _():
            dg_ref[...] = jnp.zeros_like(dg_ref)

        dg_ref[...] += jnp.sum(dhf * xh, axis=0, keepdims=True)

    row = pl.BlockSpec((tm, d), lambda i: (i, 0))
    vec = pl.BlockSpec((1, d), lambda i: (0, 0))
    out_specs = [row] + ([row] if want_bf16 else []) + [vec]
    out_shape = ([jax.ShapeDtypeStruct((t, d), F32)] + ([jax.ShapeDtypeStruct((t, d), BF16)] if want_bf16 else [])
                 + [jax.ShapeDtypeStruct((1, d), F32)])
    return pl.pallas_call(body, name=name, grid=(t // tm,), in_specs=[row, vec, row, row] + [ANY] * len(deps),
                          out_specs=out_specs, out_shape=out_shape, compiler_params=_cp(("arbitrary",)))(x, g, dh, dres, *deps)


def _final_bwd(name, x3, gf, tgt):
    t, d = x3.shape
    tm = _rows(t)

    def body(x_ref, g_ref, t_ref, dx_ref, dxb_ref, dg_ref, lc_ref):
        xf = x_ref[...]
        g = g_ref[...]
        r = lax.rsqrt(jnp.mean(xf * xf, axis=-1, keepdims=True) + EPS)
        xh = xf * r
        diff = xh * g - t_ref[...]
        dy = diff * (1.0 / d)
        dxh = dy * g
        m = jnp.mean(dxh * xh, axis=-1, keepdims=True)
        dx = r * (dxh - xh * m)
        dx_ref[...] = dx
        dxb_ref[...] = dx.astype(BF16)

        @pl.when(pl.program_id(0) == 0)
        def _():
            dg_ref[...] = jnp.zeros_like(dg_ref)
            lc_ref[...] = jnp.zeros_like(lc_ref)

        dg_ref[...] += jnp.sum(dy * xh, axis=0, keepdims=True)
        lc_ref[...] += jnp.sum(diff * diff, axis=0, keepdims=True) * (0.5 / d)

    row = pl.BlockSpec((tm, d), lambda i: (i, 0))
    vec = pl.BlockSpec((1, d), lambda i: (0, 0))
    return pl.pallas_call(
        body, name=name, grid=(t // tm,), in_specs=[row, vec, row], out_specs=[row, row, vec, vec],
        out_shape=[jax.ShapeDtypeStruct((t, d), F32), jax.ShapeDtypeStruct((t, d), BF16),
                   jax.ShapeDtypeStruct((1, d), F32), jax.ShapeDtypeStruct((1, d), F32)],
        compiler_params=_cp(("arbitrary",)),
    )(x3, gf, tgt)


def _shift_down(v, k, t_idx):
    return jnp.where(t_idx >= k, pltpu.roll(v, k, 0), 0.0)


def _shift_up(v, k, t_idx):
    n = v.shape[0]
    return jnp.where(t_idx < n - k, pltpu.roll(v, n - k, 0), 0.0)


def _window_sums(v, shift, t_idx, grp):
    s = v + shift(v, 1, t_idx)
    out = s
    for lvl in range(1, len(POOL_WINDOWS)):
        s = s + shift(s, 1 << lvl, t_idx)
        out = jnp.where(grp >= lvl, s, out)
    return out


def _window_weight(t_idx, grp):
    return 1.0 / jnp.minimum(t_idx[:, :1] + 1, jnp.left_shift(2, grp)).astype(F32)


MIX_COLS = 256


def _mixer_fwd(name, proj, cw, cb, n_conv, n_groups, deps=()):
    t = proj.shape[0]
    nb = n_conv // MIX_COLS
    per_group = n_conv // n_groups // MIX_COLS

    def body(ba_ref, ca_ref, va_ref, vb_ref, cw_ref, cb_ref, *rest):
        z_ref, p_ref = rest[len(deps):]
        t_idx = lax.broadcasted_iota(jnp.int32, (t, MIX_COLS), 0)
        q = ca_ref[...].astype(F32) * va_ref[...].astype(F32)
        w = cw_ref[...]
        u = cb_ref[...] + w[0:1] * _shift_down(q, 2, t_idx) + w[1:2] * _shift_down(q, 1, t_idx) + w[2:3] * q
        z_ref[...] = (ba_ref[...].astype(F32) * u).astype(BF16)
        grp = pl.program_id(0) // per_group
        v = vb_ref[...].astype(F32)
        p_ref[...] = (_window_sums(v, _shift_down, t_idx, grp) * _window_weight(t_idx, grp) - v).astype(BF16)

    col = lambda s: pl.BlockSpec((t, MIX_COLS), lambda j: (0, s * nb + j))
    return pl.pallas_call(
        body, name=name, grid=(nb,),
        in_specs=[col(0), col(1), col(2), col(3), pl.BlockSpec((3, MIX_COLS), lambda j: (0, j)),
                  pl.BlockSpec((1, MIX_COLS), lambda j: (0, j))] + [ANY] * len(deps),
        out_specs=[col(0), col(0)],
        out_shape=[jax.ShapeDtypeStruct((t, n_conv), BF16), jax.ShapeDtypeStruct((t, n_conv), BF16)],
        compiler_params=_cp(("parallel",)),
    )(proj, proj, proj, proj, cw, cb, *deps)


def _mixer_bwd(name, dz, dp, proj, cw, cb, dproj, n_conv, n_groups, deps=()):
    t = proj.shape[0]
    nb = n_conv // MIX_COLS
    per_group = n_conv // n_groups // MIX_COLS

    def body(dz_ref, dp_ref, ba_ref, ca_ref, va_ref, cw_ref, cb_ref, _, *rest):
        o_ref, dcw_ref, dcb_ref, scr = rest[len(deps):]
        s = pl.program_id(1)

        @pl.when(s == 0)
        def _():
            t_idx = lax.broadcasted_iota(jnp.int32, (t, MIX_COLS), 0)
            ca, va = ca_ref[...].astype(F32), va_ref[...].astype(F32)
            q = ca * va
            q1, q2 = _shift_down(q, 1, t_idx), _shift_down(q, 2, t_idx)
            w = cw_ref[...]
            u = cb_ref[...] + w[0:1] * q2 + w[1:2] * q1 + w[2:3] * q
            dzf = dz_ref[...].astype(F32)
            du = dzf * ba_ref[...].astype(F32)
            scr[0] = (dzf * u).astype(BF16)
            dq = w[2:3] * du + w[1:2] * _shift_up(du, 1, t_idx) + w[0:1] * _shift_up(du, 2, t_idx)
            scr[1] = (dq * va).astype(BF16)
            scr[2] = (dq * ca).astype(BF16)
            dcb_ref[...] = jnp.sum(du, axis=0, keepdims=True)
            dcw_ref[0:1, :] = jnp.sum(du * q2, axis=0, keepdims=True)
            dcw_ref[1:2, :] = jnp.sum(du * q1, axis=0, keepdims=True)
            dcw_ref[2:3, :] = jnp.sum(du * q, axis=0, keepdims=True)
            grp = pl.program_id(0) // per_group
            dpf = dp_ref[...].astype(F32)
            e = dpf * _window_weight(t_idx, grp)
            scr[3] = (_window_sums(e, _shift_up, t_idx, grp) - dpf).astype(BF16)

        o_ref[...] = scr[s]

    col = lambda c: pl.BlockSpec((t, MIX_COLS), lambda j, s: (0, c * nb + j))
    own = pl.BlockSpec((t, MIX_COLS), lambda j, s: (0, j))
    return pl.pallas_call(
        body, name=name, grid=(nb, 4),
        in_specs=[own, own, col(0), col(1), col(2), pl.BlockSpec((3, MIX_COLS), lambda j, s: (0, j)),
                  pl.BlockSpec((1, MIX_COLS), lambda j, s: (0, j)), ANY] + [ANY] * len(deps),
        out_specs=[pl.BlockSpec((t, MIX_COLS), lambda j, s: (0, s * nb + j)),
                   pl.BlockSpec((3, MIX_COLS), lambda j, s: (0, j)), pl.BlockSpec((1, MIX_COLS), lambda j, s: (0, j))],
        out_shape=[jax.ShapeDtypeStruct(dproj.shape, BF16), jax.ShapeDtypeStruct((3, n_conv), F32),
                   jax.ShapeDtypeStruct((1, n_conv), F32)],
        scratch_shapes=[pltpu.VMEM((4, t, MIX_COLS), BF16)],
        input_output_aliases={7: 0},
        compiler_params=_cp(("arbitrary", "arbitrary")),
    )(dz, dp, proj, proj, proj, cw, cb, dproj, *deps)


def _merge_fwd(name, proj, bg, ya, yb, ps):
    t, d = ya.shape
    tm = _rows(t)

    def body(gab_ref, bg_ref, ya_ref, yb_ref, ps_ref, o_ref):
        gab = gab_ref[...].astype(F32) + bg_ref[...]
        sa, sb = jax.nn.sigmoid(gab[:, :d]), jax.nn.sigmoid(gab[:, d:])
        o_ref[...] = (sa * ya_ref[...].astype(F32) + sb * (yb_ref[...].astype(F32) * ps_ref[...])).astype(BF16)

    row = pl.BlockSpec((tm, d), lambda i: (i, 0))
    return pl.pallas_call(
        body, name=name, grid=(t // tm,),
        in_specs=[pl.BlockSpec((tm, 2 * d), lambda i: (i, 1)), pl.BlockSpec((1, 2 * d), lambda i: (0, 0)), row, row,
                  pl.BlockSpec((1, d), lambda i: (0, 0))],
        out_specs=row, out_shape=jax.ShapeDtypeStruct((t, d), BF16), compiler_params=_cp(("parallel",)),
    )(proj, bg, ya, yb, ps)


def _merge_bwd(name, dm, proj, bg, ya, yb, ps, deps=()):
    t, d = ya.shape
    tm = _rows(t)

    def body(dm_ref, gab_ref, bg_ref, ya_ref, yb_ref, ps_ref, *rest):
        dya_ref, dyb_ref, dg_ref, dba_ref, dbb_ref, dps_ref = rest[len(deps):]
        gab = gab_ref[...].astype(F32) + bg_ref[...]
        sa, sb = jax.nn.sigmoid(gab[:, :d]), jax.nn.sigmoid(gab[:, d:])
        dmf = dm_ref[...].astype(F32)
        ybf, ps_ = yb_ref[...].astype(F32), ps_ref[...]
        dya_ref[...] = (dmf * sa).astype(BF16)
        dyb = dmf * sb
        dyb_ref[...] = (dyb * ps_).astype(BF16)
        dga = dmf * ya_ref[...].astype(F32) * sa * (1.0 - sa)
        dgb = dmf * (ybf * ps_) * sb * (1.0 - sb)
        dg_ref[:, :d] = dga.astype(BF16)
        dg_ref[:, d:] = dgb.astype(BF16)

        @pl.when(pl.program_id(0) == 0)
        def _():
            dba_ref[...] = jnp.zeros_like(dba_ref)
            dbb_ref[...] = jnp.zeros_like(dbb_ref)
            dps_ref[...] = jnp.zeros_like(dps_ref)

        dba_ref[...] += jnp.sum(dga, axis=0, keepdims=True)
        dbb_ref[...] += jnp.sum(dgb, axis=0, keepdims=True)
        dps_ref[...] += jnp.sum(dyb * ybf, axis=0, keepdims=True)

    row = pl.BlockSpec((tm, d), lambda i: (i, 0))
    vec = pl.BlockSpec((1, d), lambda i: (0, 0))
    gates = pl.BlockSpec((tm, 2 * d), lambda i: (i, 1))
    return pl.pallas_call(
        body, name=name, grid=(t // tm,),
        in_specs=[row, gates, pl.BlockSpec((1, 2 * d), lambda i: (0, 0)), row, row, vec] + [ANY] * len(deps),
        out_specs=[row, row, gates, vec, vec, vec],
        out_shape=[jax.ShapeDtypeStruct((t, d), BF16), jax.ShapeDtypeStruct((t, d), BF16),
                   jax.ShapeDtypeStruct(proj.shape, BF16), jax.ShapeDtypeStruct((1, d), F32),
                   jax.ShapeDtypeStruct((1, d), F32), jax.ShapeDtypeStruct((1, d), F32)],
        compiler_params=_cp(("arbitrary",)),
    )(dm, proj, bg, ya, yb, ps, *deps)


def _ffn_up_act(name, h, w_up, gate, part=None, prev=None, deps=()):
    t, d = h.shape
    f = w_up.shape[1]
    tm, tf = _tile(t, 1024), _tile(f, 512)
    j0, j1 = _tile_span(f // tf, part)
    n_prev = 0 if prev is None else 2
    extra = ([] if prev is None else list(prev)) + list(deps)

    def body(h_ref, w_ref, g_ref, *rest):
        u_ref, a_ref = rest[len(extra):]
        u = lax.dot_general(h_ref[...], w_ref[...], _DIMS["nn"], preferred_element_type=F32)
        g = g_ref[...].astype(F32)
        u_ref[...] = u.astype(BF16)
        a_ref[...] = (g * jax.nn.sigmoid(g) * u).astype(BF16)

    blk = pl.BlockSpec((tm, tf), lambda i, j: (i, j0 + j))
    shp = jax.ShapeDtypeStruct((t, f), BF16)
    return pl.pallas_call(
        body, name=name, grid=(t // tm, j1 - j0),
        in_specs=[pl.BlockSpec((tm, d), lambda i, j: (i, 0)), pl.BlockSpec((d, tf), lambda i, j: (0, j0 + j)), blk]
        + [ANY] * len(extra),
        out_specs=[blk, blk], out_shape=[shp, shp], input_output_aliases={3 + i: i for i in range(n_prev)},
        compiler_params=_cp(("parallel", "parallel")))(h, w_up, gate, *extra)


def _ffn_bwd(name, dy, w_down, gate, up):
    t, d = dy.shape
    f = w_down.shape[0]
    tm, tf = _tile(t, 1024), _tile(f, 512)

    def body(dy_ref, w_ref, g_ref, u_ref, dg_ref, du_ref):
        da = lax.dot_general(dy_ref[...], w_ref[...], _DIMS["nt"], preferred_element_type=F32)
        g = g_ref[...].astype(F32)
        s = jax.nn.sigmoid(g)
        du_ref[...] = (da * (g * s)).astype(BF16)
        dg_ref[...] = (da * u_ref[...].astype(F32) * (s * (1.0 + g * (1.0 - s)))).astype(BF16)

    blk = pl.BlockSpec((tm, tf), lambda i, j: (i, j))
    shp = jax.ShapeDtypeStruct((t, f), BF16)
    return pl.pallas_call(
        body, name=name, grid=(t // tm, f // tf),
        in_specs=[pl.BlockSpec((tm, d), lambda i, j: (i, 0)), pl.BlockSpec((tf, d), lambda i, j: (j, 0)), blk, blk],
        out_specs=[blk, blk], out_shape=[shp, shp], compiler_params=_cp(("parallel", "parallel")))(dy, w_down, gate, up)


def _adamw_math(w, g, m, v):
    m = ADAM_B1 * m + (1.0 - ADAM_B1) * g
    v = ADAM_B2 * v + (1.0 - ADAM_B2) * (g * g)
    m_hat = m / (1.0 - ADAM_B1 ** ADAM_STEP)
    v_hat = v / (1.0 - ADAM_B2 ** ADAM_STEP)
    delta = -ADAM_LR * (m_hat / (jnp.sqrt(v_hat) + ADAM_EPS) + ADAM_WD * w)
    return delta, m, v


def _adamw(name, w, g, m, v):
    r, c = w.shape
    tr = _tile8(r, 512 if c <= 1024 else 256)

    def body(w_ref, g_ref, m_ref, v_ref, go_ref, d_ref, nm_ref, nv_ref):
        g = g_ref[...]
        go_ref[...] = g
        d_ref[...], nm_ref[...], nv_ref[...] = _adamw_math(w_ref[...], g, m_ref[...], v_ref[...])

    blk = pl.BlockSpec((tr, c), lambda i: (i, 0))
    shp = jax.ShapeDtypeStruct((r, c), F32)
    return pl.pallas_call(body, name=name, grid=(r // tr,), in_specs=[blk] * 4, out_specs=[blk] * 4,
                          out_shape=[shp] * 4, compiler_params=_cp(("parallel",)))(w, g, m, v)


class _Weight:
    def __init__(self, name, rows, cols, colshard):
        self.name, self.colshard = name, colshard
        self.R, self.nn = rows // 2, cols
        self.P = 1 if colshard else N_CHIPS
        self.N = N_CHIPS * cols if colshard else cols

    def cols(self, k):
        return pl.ds(pl.multiple_of(k * self.nn, LANES), self.nn)

    def shard(self, ref, k):
        return ref.at[0, :, :, self.cols(k)] if self.colshard else ref.at[k]

    def half(self, ref, k, h):
        return ref.at[0, h, :, self.cols(k)] if self.colshard else ref.at[k, h]

    def quarter(self, ref, k, h, q):
        return self.half(ref, k, h).at[pl.ds(q * (self.R // 2), self.R // 2), :]

    def part(self, ref, k):
        return ref.at[0, :, self.cols(k)] if self.colshard else ref.at[k]


def _remote(src, dst, ssem, rsem, dev):
    return pltpu.make_async_remote_copy(src_ref=src, dst_ref=dst, send_sem=ssem, recv_sem=rsem, device_id=dev,
                                        device_id_type=MESH)


def _other_chips(x, y):
    chips = [(1 - x, y), (x, 1 - y), (1 - x, 1 - y)]
    return chips, [2 * cx + cy for cx, cy in chips]


def _hbm(a):
    return pltpu.with_memory_space_constraint(a, pltpu.HBM)


def _split_start(name, arrays, sets, after=()):
    na, ns = len(arrays), len(sets)

    def body(*refs):
        outs = refs[na + len(after):]
        for s_, (idx, copies) in enumerate(sets):
            for i, (src, dst, dev, _) in enumerate(copies([refs[k] for k in idx], *_mesh_pos())):
                _remote(src, dst, outs[2 * s_].at[i], outs[2 * s_ + 1].at[i], dev).start()
        outs[2 * ns + na][...] = jnp.zeros((8, LANES), F32)

    sems = []
    for _, copies in sets:
        sems += [pltpu.SemaphoreType.DMA((copies.n,))] * 2
    out = pl.pallas_call(
        body, name=name, in_specs=[HBM] * na + [ANY] * len(after), out_specs=[SEM] * (2 * ns) + [HBM] * na + [VMEM],
        out_shape=sems + [pltpu.HBM(a.shape, a.dtype) for a in arrays] + [jax.ShapeDtypeStruct((8, LANES), F32)],
        input_output_aliases={i: 2 * ns + i for i in range(na)},
        compiler_params=pltpu.CompilerParams(has_side_effects=EFFECT),
    )(*[_hbm(a) for a in arrays], *after)
    return [(out[2 * i], out[2 * i + 1]) for i in range(ns)], list(out[2 * ns:2 * ns + na]), out[-1]


def _split_wait(name, arrays, ssem, rsem, copies, after):
    na = len(arrays)

    def body(*refs):
        for i, (src, _, dev, dst) in enumerate(copies(refs[:na], *_mesh_pos())):
            cp = _remote(src, dst, refs[na].at[i], refs[na + 1].at[i], dev)
            cp.wait_send()
            cp.wait_recv()

    return list(pl.pallas_call(
        body, name=name, in_specs=[HBM] * na + [SEM, SEM] + [ANY] * len(after), out_specs=[HBM] * na,
        out_shape=[pltpu.HBM(a.shape, a.dtype) for a in arrays], input_output_aliases={i: i for i in range(na)},
        compiler_params=pltpu.CompilerParams(has_side_effects=EFFECT),
    )(*arrays, ssem, rsem, *after))


def _pass_copies(grp, rels=(0, 1, 2)):
    def copies(land, x, y, c):
        _, ks = _other_chips(x, y)
        return [(w.half(land[wi], ks[j], c), w.half(land[wi], ks[j], c), (x, y, 1 - c), w.half(land[wi], ks[j], 1 - c))
                for wi, w in enumerate(grp) for j in rels]
    copies.n = len(grp) * len(rels)
    return copies


def _direct_copies(grp):
    def copies(land, x, y, c):
        chips, ks = _other_chips(x, y)
        out = []
        for wi, w in enumerate(grp):
            mine = w.half(land[wi], 2 * x + y, c)
            out += [(mine, mine, (*chips[j], c), w.half(land[wi], ks[j], c)) for j in range(3)]
        return out
    copies.n = 3 * len(grp)
    return copies


def _near_copies(grp):
    def copies(land, x, y, c):
        chips, ks = _other_chips(x, y)
        out = []
        for wi, w in enumerate(grp):
            mine = w.half(land[wi], 2 * x + y, c)
            out += [(mine, mine, (*chips[j], c), w.half(land[wi], ks[j], c)) for j in (0, 1)]
        return out
    copies.n = 2 * len(grp)
    return copies


def _far_copies(grp):
    def copies(land, x, y, c):
        chips, ks = _other_chips(x, y)
        out = []
        for wi, w in enumerate(grp):
            for j in (0, 1):
                q = w.quarter(land[wi], ks[j], c, j)
                out.append((q, q, (*chips[1 - j], c), w.quarter(land[wi], ks[2], c, j)))
        return out
    copies.n = 2 * len(grp)
    return copies


def _pair_copies(n, whole=False):
    def copies(refs, x, y, c):
        return [(refs[i] if whole else refs[i].at[:, 1 - c], refs[n + i], (x, y, 1 - c), refs[n + i]) for i in range(n)]
    copies.n = n
    return copies


def _share_copies(n):
    def copies(refs, x, y, c):
        return [(refs[i].at[c], refs[i].at[c], (x, y, 1 - c), refs[i].at[1 - c]) for i in range(n)]
    copies.n = n
    return copies


def _gather_conv_w(cw, thru):
    ncw = cw.shape[1]

    def body(cw_ref, _, out_ref, __, ssem, rsem):
        x, y, c = _mesh_pos()
        k_me = 2 * x + y
        chips, ks = _other_chips(x, y)
        cols = lambda k: out_ref.at[:, pl.ds(pl.multiple_of(k * ncw, LANES), ncw)]
        cps = [_remote(cw_ref, cols(k_me), ssem.at[j], rsem.at[j], (*chip, c)) for j, chip in enumerate(chips)]
        for cp in cps:
            cp.start()
        for k in range(N_CHIPS):
            @pl.when(k_me == k)
            def _():
                out_ref[:, k * ncw:(k + 1) * ncw] = cw_ref[...]
        for j in range(3):
            _remote(cw_ref, cols(ks[j]), ssem.at[j], rsem.at[j], (*chips[j], c)).wait_recv()
        for cp in cps:
            cp.wait_send()

    return pl.pallas_call(
        body, name="gather_conv_w", in_specs=[VMEM, ANY], out_specs=[VMEM, ANY],
        out_shape=[jax.ShapeDtypeStruct((3, N_CHIPS * ncw), F32), jax.ShapeDtypeStruct(thru.shape, thru.dtype)],
        scratch_shapes=[pltpu.SemaphoreType.DMA((3,)), pltpu.SemaphoreType.DMA((3,))],
        input_output_aliases={1: 1},
    )(cw, thru)


def _grad_tiles(w, n):
    return _tile8(w.R, 512) if w.R <= 512 else w.R // 2, _tile(n, 2048)


def _pair_sum(name, w, pos, grad, got):
    tr, tn = _grad_tiles(w, w.N)

    def body(pos_ref, g_ref, r_ref, o_ref):
        o_ref[...] = (g_ref[...].astype(F32) + r_ref[...].astype(F32)).astype(BF16)

    blk = pl.BlockSpec((None, tr, tn), lambda p, i, j, pos: (p, i, j))
    grid_spec = pltpu.PrefetchScalarGridSpec(
        num_scalar_prefetch=1, grid=(w.P, w.R // tr, w.N // tn),
        in_specs=[pl.BlockSpec((None, None, tr, tn), lambda p, i, j, pos: (p, pos[0], i, j)), blk], out_specs=blk)
    return pl.pallas_call(body, name=name, grid_spec=grid_spec, out_shape=jax.ShapeDtypeStruct((w.P, w.R, w.N), BF16),
                          compiler_params=_cp(("parallel",) * 3))(pos, grad, got)


def _scatter_start(name, ws, pairs):
    nw = len(ws)

    def body(*refs):
        pr, land = refs[:nw], refs[nw:2 * nw]
        ssem, rsem = refs[2 * nw], refs[2 * nw + 1]
        token = refs[4 * nw + 2]
        x, y, c = _mesh_pos()
        chips, ks = _other_chips(x, y)
        for i, w in enumerate(ws):
            for j, chip in enumerate(chips):
                _remote(w.part(pr[i], ks[j]), land[i].at[j], ssem.at[3 * i + j], rsem.at[3 * i + j], (*chip, c)).start()
        token[...] = jnp.zeros_like(token)

    lands = [lax.empty((3, w.R, w.nn), BF16) for w in ws]
    out = pl.pallas_call(
        body, name=name, in_specs=[HBM] * (2 * nw),
        out_specs=[SEM, SEM] + [HBM] * (2 * nw) + [VMEM],
        out_shape=[pltpu.SemaphoreType.DMA((3 * nw,))] * 2 + [pltpu.HBM(a.shape, a.dtype) for a in pairs + lands]
        + [jax.ShapeDtypeStruct((8, LANES), F32)],
        input_output_aliases={i: 2 + i for i in range(2 * nw)},
        compiler_params=pltpu.CompilerParams(has_side_effects=EFFECT),
    )(*[_hbm(a) for a in pairs + lands])
    return out[0], out[1], list(out[2:2 + nw]), list(out[2 + nw:2 + 2 * nw]), out[-1]


def _scatter_wait(name, ws, pairs, lands, ssem, rsem, after):
    nw = len(ws)

    def body(*refs):
        pr, land = refs[:nw], refs[nw:2 * nw]
        ssem_ref, rsem_ref = refs[2 * nw], refs[2 * nw + 1]
        x, y, c = _mesh_pos()
        chips, ks = _other_chips(x, y)
        for i, w in enumerate(ws):
            for j, chip in enumerate(chips):
                cp = _remote(w.part(pr[i], ks[j]), land[i].at[j], ssem_ref.at[3 * i + j], rsem_ref.at[3 * i + j], (*chip, c))
                cp.wait_send()
                cp.wait_recv()

    out = pl.pallas_call(
        body, name=name, in_specs=[HBM] * (2 * nw) + [SEM, SEM] + [ANY] * len(after), out_specs=[HBM] * (2 * nw),
        out_shape=[pltpu.HBM(a.shape, a.dtype) for a in pairs + lands],
        input_output_aliases={i: i for i in range(2 * nw)},
        compiler_params=pltpu.CompilerParams(has_side_effects=EFFECT),
    )(*pairs, *lands, ssem, rsem, *after)
    return list(out[:nw]), list(out[nw:])


def _final_sum(name, w, pos, grad, got, parts):
    tr, tn = _grad_tiles(w, w.nn)
    nbc = w.nn // tn
    if got is None:
        return _final_sum_pair(name, w, pos, grad, parts, tr, tn)

    def body(pos_ref, g_ref, r_ref, p_ref, o_ref):
        acc = g_ref[...].astype(F32) + r_ref[...].astype(F32)
        for j in range(3):
            acc = acc + p_ref[j].astype(F32)
        o_ref[...] = acc

    if w.colshard:
        g_spec = pl.BlockSpec((None, None, tr, tn), lambda i, j, pos: (0, pos[0], i, pos[1] * nbc + j))
        r_spec = pl.BlockSpec((None, tr, tn), lambda i, j, pos: (0, i, pos[1] * nbc + j))
    else:
        g_spec = pl.BlockSpec((None, None, tr, tn), lambda i, j, pos: (pos[1], pos[0], i, j))
        r_spec = pl.BlockSpec((None, tr, tn), lambda i, j, pos: (pos[1], i, j))
    grid_spec = pltpu.PrefetchScalarGridSpec(
        num_scalar_prefetch=1, grid=(w.R // tr, nbc),
        in_specs=[g_spec, r_spec, pl.BlockSpec((3, tr, tn), lambda i, j, pos: (0, i, j))],
        out_specs=pl.BlockSpec((None, tr, tn), lambda i, j, pos: (pos[0], i, j)))
    return pl.pallas_call(body, name=name, grid_spec=grid_spec, out_shape=jax.ShapeDtypeStruct((2, w.R, w.nn), F32),
                          compiler_params=_cp(("parallel",) * 2))(pos, grad, got, parts)


def _final_sum_pair(name, w, pos, pair, parts, tr, tn):
    nbc = w.nn // tn

    def body(pos_ref, g_ref, p_ref, o_ref):
        acc = g_ref[...].astype(F32)
        for j in range(3):
            acc = acc + p_ref[j].astype(F32)
        o_ref[...] = acc

    if w.colshard:
        g_spec = pl.BlockSpec((None, tr, tn), lambda i, j, pos: (0, i, pos[1] * nbc + j))
    else:
        g_spec = pl.BlockSpec((None, tr, tn), lambda i, j, pos: (pos[1], i, j))
    grid_spec = pltpu.PrefetchScalarGridSpec(
        num_scalar_prefetch=1, grid=(w.R // tr, nbc),
        in_specs=[g_spec, pl.BlockSpec((3, tr, tn), lambda i, j, pos: (0, i, j))],
        out_specs=pl.BlockSpec((None, tr, tn), lambda i, j, pos: (pos[0], i, j)))
    return pl.pallas_call(body, name=name, grid_spec=grid_spec, out_shape=jax.ShapeDtypeStruct((2, w.R, w.nn), F32),
                          compiler_params=_cp(("parallel",) * 2))(pos, pair, parts)


VEC_ROWS = 16


def _vector_step(d, n_conv, parts, params, deps=()):
    ncw = params[2][0].shape[1]
    n_par = len(params)

    def body(*refs):
        dg1, dba, dbb, dcw, dcb, dps, dg2, dgf, lc = refs[:9]
        wmv = refs[9:9 + 3 * n_par]
        refs = refs[9 + 3 * n_par + len(deps):]
        outs = refs[:4 * n_par]
        loss_ref = refs[4 * n_par]
        snd, got, ssem, rsem = refs[4 * n_par + 1:]
        x, y, c = _mesh_pos()
        me = 4 * x + 2 * y + c
        snd[...] = jnp.zeros_like(snd)
        for row, ref in ((0, dg1), (1, dba), (2, dbb), (3, dps), (4, dg2), (5, dgf), (6, lc)):
            snd[row:row + 1, :] = ref[...]
        snd[7:8, :n_conv] = dcb[...]
        snd[8:11, :n_conv] = dcw[...]
        cps = []
        for r in range(1, N_DEV):
            peer = tuple(1 - p if (r >> b) & 1 else p for p, b in ((x, 2), (y, 1), (c, 0)))
            cps.append(_remote(snd, got.at[me], ssem.at[r - 1], rsem.at[r - 1], peer))
        for cp in cps:
            cp.start()
        got[me] = snd[...]
        for r in range(1, N_DEV):
            peer = tuple(1 - p if (r >> b) & 1 else p for p, b in ((x, 2), (y, 1), (c, 0)))
            _remote(snd, got.at[4 * peer[0] + 2 * peer[1] + peer[2]], ssem.at[r - 1], rsem.at[r - 1], peer).wait_recv()
        for cp in cps:
            cp.wait_send()
        tot = got[0]
        for dev in range(1, N_DEV):
            tot = tot + got[dev]
        loss_ref[...] = jnp.sum(tot[6:7, :], axis=1, keepdims=True)
        k_me = 2 * x + y
        g_cw = jnp.zeros((3, ncw), F32)
        for k in range(N_CHIPS):
            g_cw = g_cw + jnp.where(k_me == k, tot[8:11, k * ncw:(k + 1) * ncw], 0.0)
        grads = [tot[0:1, :], jnp.concatenate([tot[1:2, :], tot[2:3, :]], axis=1), g_cw, tot[7:8, :n_conv],
                 tot[3:4, :], tot[4:5, :], tot[5:6, :]]
        for i, g in enumerate(grads):
            w_ref, m_ref, v_ref = wmv[3 * i:3 * i + 3]
            delta, nm, nv = _adamw_math(w_ref[...], g, m_ref[...], v_ref[...])
            outs[4 * i][...] = g
            outs[4 * i + 1][...] = delta
            outs[4 * i + 2][...] = nm
            outs[4 * i + 3][...] = nv

    args = list(parts)
    out_shape = []
    for w, m, v in params:
        args += [w, m, v]
        out_shape += [jax.ShapeDtypeStruct(w.shape, F32)] * 4
    out_shape.append(jax.ShapeDtypeStruct((1, 1), F32))
    return pl.pallas_call(
        body, name="vector_params_step", in_specs=[VMEM] * len(args) + [ANY] * len(deps),
        out_specs=[VMEM] * len(out_shape), out_shape=out_shape,
        scratch_shapes=[pltpu.VMEM((VEC_ROWS, d), F32), pltpu.VMEM((N_DEV, VEC_ROWS, d), F32),
                        pltpu.SemaphoreType.DMA((N_DEV - 1,)), pltpu.SemaphoreType.DMA((N_DEV - 1,))],
        compiler_params=pltpu.CompilerParams(vmem_limit_bytes=VMEM_LIMIT),
    )(*args, *deps)


def kernel(x, norm1_g, w_in, b_gate, conv_w, conv_b, w_a_out, w_pool, pool_scale, w_o, norm2_g, w_ffn_gate, w_ffn_up, w_ffn_down, final_g, loss_target, m_norm1_g, m_w_in, m_b_gate, m_conv_w, m_conv_b, m_w_a_out, m_w_pool, m_pool_scale, m_w_o, m_norm2_g, m_w_ffn_gate, m_w_ffn_up, m_w_ffn_down, m_final_g, v_norm1_g, v_w_in, v_b_gate, v_conv_w, v_conv_b, v_w_a_out, v_w_pool, v_pool_scale, v_w_o, v_norm2_g, v_w_ffn_gate, v_w_ffn_up, v_w_ffn_down, v_final_g):
    t, d = x.shape[1], x.shape[2]
    n_conv = conv_b.shape[1]
    n_groups, pool_cg, pool_dg = w_pool.shape[1], w_pool.shape[2], N_CHIPS * w_pool.shape[3]
    d_ff = N_CHIPS * w_ffn_gate.shape[2]
    assert n_conv // n_groups == pool_cg and n_conv % (n_groups * MIX_COLS) == 0 and n_groups == len(POOL_WINDOWS)

    big = {"w_in": (w_in, m_w_in, v_w_in), "w_a_out": (w_a_out, m_w_a_out, v_w_a_out), "w_pool": (w_pool, m_w_pool, v_w_pool),
           "w_o": (w_o, m_w_o, v_w_o), "w_ffn_gate": (w_ffn_gate, m_w_ffn_gate, v_w_ffn_gate),
           "w_ffn_up": (w_ffn_up, m_w_ffn_up, v_w_ffn_up), "w_ffn_down": (w_ffn_down, m_w_ffn_down, v_w_ffn_down)}
    colshard = {"w_in": True, "w_a_out": True, "w_pool": True, "w_o": False, "w_ffn_gate": True, "w_ffn_up": True,
                "w_ffn_down": False}
    names = list(big)
    shard2d = {n: big[n][0].reshape(-1, big[n][0].shape[-1]) for n in names}
    ws = [_Weight(n, *shard2d[n].shape, colshard[n]) for n in names]

    xs, tgt = x[0], loss_target[0]
    cw_loc = conv_w[0]
    pos = jnp.stack([lax.axis_index("c"), 2 * lax.axis_index("x") + lax.axis_index("y")]).astype(jnp.int32)
    by_name = {w.name: w for w in ws}
    groups = [[by_name[n] for n in g] for g in (["w_in"], ["w_a_out", "w_pool", "w_o"], ["w_ffn_gate"], ["w_ffn_up"],
                                                 ["w_ffn_down"])]
    first = [sum(len(g) for g in groups[:i]) for i in range(len(groups))]
    rgroups = [groups[0], groups[1], groups[2] + groups[3], groups[4]]

    cast = lambda w, dep: _cast_place(f"cast_{w.name}", w, pos, shard2d[w.name].reshape(2, w.R, w.nn), deps=dep)
    chips, ks = _other_chips(lax.axis_index("x"), lax.axis_index("y"))
    kvec = jnp.stack([pos[1], *ks]).astype(jnp.int32)
    full = {}

    def start(name, arrays, copies, after=()):
        return start_many([(name, arrays, copies)], after)[0]

    def start_many(parts, after=()):
        arrays, sets = [], []
        for _, arrs, copies in parts:
            for a in arrs:
                if not any(a is b for b in arrays):
                    arrays.append(a)
            sets.append(([next(i for i, b in enumerate(arrays) if b is a) for a in arrs], copies))
        sems, thru, token = _split_start("_".join(p[0] for p in parts), arrays, sets, after)
        return [(name, [thru[i] for i in idx], ssem, rsem, copies, token)
                for (name, _, copies), (idx, _), (ssem, rsem) in zip(parts, sets, sems)]

    def wait(started, after):
        name, arrays, ssem, rsem, copies, _ = started
        return _split_wait(name + "_wait", arrays, ssem, rsem, copies, after)

    def pass_on(g, got, after=()):
        return start(f"pass_{g}", got, _pass_copies(groups[g]), after)

    def passed(g, st, after=None):
        got = wait(st, [st[5]] if after is None else after)
        full.update({w.name: a.reshape(w.P * 2 * w.R, w.N) for w, a in zip(groups[g], got)})

    near = start("near_0", [cast(w, []) for w in groups[0]], _near_copies(groups[0]))
    rest = [cast(w, [near[5]]) for grp in groups[1:] for w in grp]
    h1 = _rms_fwd("norm1_fwd", xs, norm1_g, deps=[near[5]])
    proj = _proj_piece("proj_own", h1, shard2d["w_in"], None, kvec, 0, 1, deps=rest)
    got = wait(near, [proj])
    far, small, st = start_many([("far_0", got, _far_copies(groups[0])), ("direct_1", rest[:3], _direct_copies(groups[1])),
                                 ("pass_near_0", got, _pass_copies(groups[0], (0, 1)))])
    got = wait(st, [st[5]])
    proj = _proj_piece("proj_near", h1, got[0].reshape(-1, groups[0][0].N), proj, kvec, 1, 2)
    st = start("pass_far_0", wait((far[0], got) + far[2:], [proj]), _pass_copies(groups[0], (2,)))
    got = wait(st, [st[5]])
    w_in_full = got[0].reshape(-1, groups[0][0].N)
    proj = _proj_piece("proj_far", h1, w_in_full, proj, kvec, 3, 1)
    cw_full, proj = _gather_conv_w(cw_loc, proj)
    got = wait(small, [proj])
    near_g = start("near_2", rest[3:4], _near_copies(groups[2]), got)
    st = pass_on(1, got, [near_g[5]])
    z, p = _mixer_fwd("mixer_fwd", proj, cw_full, conv_b, n_conv, n_groups, deps=[st[5]])
    passed(1, st, [z])
    wp_full = full["w_pool"].reshape(n_groups, pool_cg, pool_dg)
    ya = _mm_nn("conv_out", z, full["w_a_out"], BF16)
    yb = _gmm_nn("pool_out", p, wp_full, BF16)
    merged = _merge_fwd("merge_fwd", proj, b_gate, ya, yb, pool_scale)
    far_g, near_u = start_many([("far_2", wait(near_g, [merged]), _far_copies(groups[2])),
                                ("near_3", rest[4:5], _near_copies(groups[3]))])
    x2 = _mm_nn("mix_out", merged, full["w_o"], F32, add=xs, deps=[near_u[5]])
    st = pass_on(2, wait(far_g, [x2]))
    h2 = _rms_fwd("norm2_fwd", x2, norm2_g, deps=[st[5]])
    passed(2, st, [h2])
    def far_and_pass(g, near_st, after, more=()):
        got = wait(near_st, after)
        return start_many([(f"far_{g}", got, _far_copies(groups[g])), (f"pass_near_{g}", got, _pass_copies(groups[g], (0, 1))),
                           *more])

    def finish(g, far_st, pass_st, after):
        got = wait(pass_st, after)
        st = start(f"pass_far_{g}", wait((far_st[0], got) + far_st[2:], after), _pass_copies(groups[g], (2,)))
        passed(g, st)

    gate = _mm_nn("ffn_gate_a", h2, full["w_ffn_gate"], BF16, part=(0, 2))
    far_u, pass_u, near_d = far_and_pass(3, near_u, [gate], [("near_4", rest[5:6], _near_copies(groups[4]))])
    gate = _mm_nn("ffn_gate_b", h2, full["w_ffn_gate"], BF16, part=(1, 2), prev=gate, deps=[near_d[5]])
    finish(3, far_u, pass_u, [gate])
    up_act = _ffn_up_act("ffn_up_act_a", h2, full["w_ffn_up"], gate, part=(0, 2))
    far_d, pass_d = far_and_pass(4, near_d, [up_act[0]])
    up, act = _ffn_up_act("ffn_up_act_b", h2, full["w_ffn_up"], gate, part=(1, 2), prev=up_act, deps=[pass_d[5]])
    finish(4, far_d, pass_d, [act])
    x3 = _mm_nn("ffn_down", act, full["w_ffn_down"], F32, add=x2, tiles=(512, 512))

    pending = {}

    def pair_start(g, grads):
        grp = rgroups[g]
        gcan = [grads[w.name].reshape(w.P, 2, w.R, w.N) for w in grp]
        slots = [lax.empty((w.P, w.R, w.N), BF16) for w in grp]
        pending[g] = start(f"pair_start_{g}", gcan + slots, _pair_copies(len(grp)))
        return pending[g][5]

    def scatter_start(g, after):
        grp = rgroups[g]
        n = len(grp)
        arrs = wait(pending[g], after)
        gcan, sib = arrs[:n], arrs[n:]
        pairs = [_pair_sum(f"pair_sum_{w.name}", w, pos, a, s) for w, a, s in zip(grp, gcan, sib)]
        ssem, rsem, pairs, slots, token = _scatter_start(f"scatter_start_{g}", grp, pairs)
        pending[g] = (gcan, sib, pairs, slots, ssem, rsem)
        return token

    def pair_start_halves(g, ab, deps):
        grp = rgroups[g]
        sent = [_mm_tn_half(f"d{w.name}_sib", a, b, pos, False, deps=deps if i == 0 else ()) for i, (w, (a, b)) in enumerate(zip(grp, ab))]
        slots = [lax.empty((1, w.R, w.N), BF16) for w in grp]
        pending[g] = start(f"pair_start_{g}", sent + slots, _pair_copies(len(grp), whole=True))
        return pending[g][5]

    def scatter_start_halves(g, ab, after):
        grp = rgroups[g]
        n = len(grp)
        arrs = wait(pending[g], after)
        pairs = [_mm_tn_half(f"d{w.name}_own", a, b, pos, True, add=s) for w, (a, b), s in zip(grp, ab, arrs[n:])]
        ssem, rsem, pairs, slots, token = _scatter_start(f"scatter_start_{g}", grp, pairs)
        pending[g] = (None, None, pairs, slots, ssem, rsem)
        return token

    def reduce_finish(g, after):
        grp = rgroups[g]
        gcan, sib, pairs, slots, ssem, rsem = pending[g]
        pairs, parts = _scatter_wait(f"scatter_wait_{g}", grp, pairs, slots, ssem, rsem, after)
        if gcan is None:
            return [_final_sum(f"final_sum_{w.name}", w, pos, a, None, q) for w, a, q in zip(grp, pairs, parts)]
        return [_final_sum(f"final_sum_{w.name}", w, pos, a, s, q) for w, a, s, q in zip(grp, gcan, sib, parts)]

    grads = {}
    dx3, dx3b, d_gf, loss_cols = _final_bwd("final_bwd", x3, final_g.reshape(1, d), tgt)
    dgate, dup = _ffn_bwd("ffn_bwd", dx3b, full["w_ffn_down"], gate, up)
    grads["w_ffn_down"] = _mm_tn("dw_ffn_down", act, dx3b, BF16)
    tok = pair_start(3, grads)
    dh2 = _mm_nt("d_h2", [(dgate, full["w_ffn_gate"]), (dup, full["w_ffn_up"])], BF16, tk=d_ff // 4, deps=[tok])
    tok = scatter_start(3, [dh2])
    tok = pair_start_halves(2, [(h2, dgate), (h2, dup)], [tok])
    dx2, dx2b, d_g2 = _rms_bwd("norm2_bwd", x2, norm2_g, dh2, dx3, True, deps=[tok])
    dmerged = _mm_nt("d_merged", [(dx2b, full["w_o"])], BF16, tk=d)
    grads["w_o"] = _mm_tn("dw_o", merged, dx2b, BF16)
    tok = scatter_start_halves(2, [(h2, dgate), (h2, dup)], [grads["w_o"]])
    dya, dyb, dproj, d_bga, d_bgb, d_ps = _merge_bwd("merge_bwd", dmerged, proj, b_gate, ya, yb, pool_scale, deps=[tok])
    dz = _mm_nt("d_z", [(dya, full["w_a_out"])], BF16, tk=d)
    grads["w_a_out"] = _mm_tn("dw_a_out", z, dya, BF16)
    dp = _gmm_nt("d_pool", dyb, wp_full, BF16)
    grads["w_pool"] = _gmm_tn("dw_pool", p, dyb, n_groups, BF16)
    tok = pair_start(1, grads)
    dproj, d_cw, d_cb = _mixer_bwd("mixer_bwd", dz, dp, proj, cw_full, conv_b, dproj, n_conv, n_groups, deps=[tok])
    tok = scatter_start(1, [dproj])
    tok = pair_start_halves(0, [(h1, dproj)], [tok])
    dh1 = _mm_nt("d_h1", [(dproj, w_in_full)], BF16, tk=proj.shape[1] // 4, deps=[tok])
    tok = scatter_start_halves(0, [(h1, dproj)], [dh1])
    grad_x, d_g1 = _rms_bwd("norm1_bwd", xs, norm1_g, dh1, dx2, False, deps=[tok])

    g_big, d_big, m_big, v_big = {}, {}, {}, {}

    def update(wsub, shared):
        out = []
        for w, g in zip(wsub, shared):
            wt, mt, vt = big[w.name]
            g2 = g.reshape(2 * w.R, w.nn)
            go, dl, nm, nv = _adamw(f"adamw_{w.name}", shard2d[w.name], g2, mt.reshape(g2.shape), vt.reshape(g2.shape))
            g_big[w.name], d_big[w.name], m_big[w.name], v_big[w.name] = (a.reshape(wt.shape) for a in (go, dl, nm, nv))
            out.append(nv)
        return out

    after = [grad_x]
    started = []
    for g in (3, 2, 1):
        halves = reduce_finish(g, after)
        started.append((g, start(f"share_{g}", halves, _share_copies(len(halves)))))
        after = [started[-1][1][5]]
    for g, st in started:
        after = update(rgroups[g], wait(st, after))
    st = start("share_0", reduce_finish(0, after), _share_copies(1))

    vec_names = ["norm1_g", "b_gate", "conv_w", "conv_b", "pool_scale", "norm2_g", "final_g"]
    vec = {"norm1_g": (norm1_g, m_norm1_g, v_norm1_g), "b_gate": (b_gate, m_b_gate, v_b_gate),
           "conv_w": (cw_loc, m_conv_w[0], v_conv_w[0]), "conv_b": (conv_b, m_conv_b, v_conv_b),
           "pool_scale": (pool_scale, m_pool_scale, v_pool_scale), "norm2_g": (norm2_g, m_norm2_g, v_norm2_g),
           "final_g": tuple(a.reshape(1, d) for a in (final_g, m_final_g, v_final_g))}
    vout = _vector_step(d, n_conv, [d_g1, d_bga, d_bgb, d_cw, d_cb, d_ps, d_g2, d_gf, loss_cols],
                        [vec[n] for n in vec_names], deps=st[1])
    update(rgroups[0], wait(st, []))

    shapes = {"conv_w": conv_w.shape, "final_g": final_g.shape}
    g_vec, d_vec, m_vec, v_vec = ({n: vout[4 * i + q].reshape(shapes.get(n, vec[n][0].shape)) for i, n in enumerate(vec_names)}
                                  for q in range(4))
    loss = vout[-1].reshape(())

    order = ["norm1_g", "w_in", "b_gate", "conv_w", "conv_b", "w_a_out", "w_pool", "pool_scale", "w_o", "norm2_g",
             "w_ffn_gate", "w_ffn_up", "w_ffn_down", "final_g"]
    pick = lambda vecs, bigs: [vecs[n] if n in vecs else bigs[n] for n in order]
    return (loss, grad_x.reshape(x.shape), *pick(g_vec, g_big), *pick(d_vec, d_big), *pick(m_vec, m_big),
            *pick(v_vec, v_big))
```

```python
import functools

import jax
import jax.numpy as jnp
from jax import lax
from jax.experimental import pallas as pl
from jax.experimental.pallas import tpu as pltpu

F32, BF16 = jnp.float32, jnp.bfloat16
MESH = pl.DeviceIdType.MESH
ANY = pl.BlockSpec(memory_space=pl.ANY)
VMEM = pl.BlockSpec(memory_space=pltpu.VMEM)
HBM = pl.BlockSpec(memory_space=pltpu.HBM)
SEM = pl.BlockSpec(memory_space=pltpu.SEMAPHORE)
EFFECT = pltpu.SideEffectType.DATAFLOW_SIDE_EFFECTING

EPS = 1e-6
POOL_WINDOWS = (2, 4, 8, 16)
ADAM_LR, ADAM_B1, ADAM_B2, ADAM_EPS, ADAM_WD, ADAM_STEP = 0.001, 0.9, 0.999, 1e-08, 0.01, 10

V7X_VMEM_BYTES = 64 * 1024 * 1024
VMEM_LIMIT = V7X_VMEM_BYTES * 3 // 4
LANES = 128
COL_TILE = 8 * LANES
N_CHIPS = 4
N_DEV = 8

_DIMS = {
    "nn": (((1,), (0,)), ((), ())),
    "nt": (((1,), (1,)), ((), ())),
    "tn": (((0,), (0,)), ((), ())),
}


def _cp(sem):
    return pltpu.CompilerParams(dimension_semantics=sem, vmem_limit_bytes=VMEM_LIMIT)


def _mesh_pos():
    return lax.axis_index("x"), lax.axis_index("y"), lax.axis_index("c")


def _mm(name, pairs, *, mode, grid, out_shape, o_spec, nk=1, kaxis=None, add=None, deps=(), prev=None):
    npair = len(pairs)
    has_add = add is not None

    def body(*refs):
        ab = refs[: 2 * npair]
        pos = 2 * npair
        add_ref = refs[pos] if has_add else None
        pos += int(has_add) + len(deps) + (prev is not None)
        o_ref = refs[pos]
        acc_ref = refs[pos + 1] if nk > 1 else None
        d = None
        for p in range(npair):
            t = lax.dot_general(ab[2 * p][...], ab[2 * p + 1][...], _DIMS[mode], preferred_element_type=F32)
            d = t if d is None else d + t
        if nk == 1:
            if has_add:
                d = d + add_ref[...].astype(F32)
            o_ref[...] = d.astype(o_ref.dtype)
        else:
            k = pl.program_id(kaxis)

            @pl.when(k == 0)
            def _():
                acc_ref[...] = d

            @pl.when(k > 0)
            def _():
                acc_ref[...] += d

            @pl.when(k == nk - 1)
            def _():
                r = acc_ref[...]
                if has_add:
                    r = r + add_ref[...].astype(F32)
                o_ref[...] = r.astype(o_ref.dtype)

    args, specs = [], []
    for a, a_spec, b, b_spec in pairs:
        args += [a, b]
        specs += [a_spec, b_spec]
    if has_add:
        args.append(add[0])
        specs.append(add[1])
    args += list(deps)
    specs += [ANY] * len(deps)
    aliases = {}
    if prev is not None:
        aliases = {len(args): 0}
        args.append(prev)
        specs.append(ANY)
    scratch = []
    if nk > 1:
        blk = [d for d in o_spec.block_shape if d is not None]
        scratch = [pltpu.VMEM(tuple(blk), F32)]
    sem = tuple("arbitrary" if (nk > 1 and ax == kaxis) else "parallel" for ax in range(len(grid)))
    return pl.pallas_call(
        body, name=name, grid=grid, in_specs=specs, out_specs=o_spec, out_shape=out_shape,
        scratch_shapes=scratch, input_output_aliases=aliases, compiler_params=_cp(sem),
    )(*args)


def _tile_span(n_tiles, part):
    if part is None:
        return 0, n_tiles
    p, of = part
    return p * n_tiles // of, (p + 1) * n_tiles // of


def _tile(n, pref):
    if n <= pref:
        return n
    for t in range(pref, 0, -LANES):
        if t % LANES == 0 and n % t == 0:
            return t
    raise ValueError(f"no tile for {n}")


def _mm_nn(name, a, b, out_dtype, add=None, tk=None, deps=(), part=None, prev=None, tiles=None):
    m, kk = a.shape
    n = b.shape[1]
    tm, tn = _tile(m, 1024), _tile(n, COL_TILE)
    if tiles is not None:
        tm, tn = _tile(m, tiles[0]), _tile(n, tiles[1])
    out_shape = jax.ShapeDtypeStruct((m, n), out_dtype)
    if tk is None or tk == kk:
        j0, j1 = _tile_span(n // tn, part)
        grid = (m // tm, j1 - j0)
        pairs = [(a, pl.BlockSpec((tm, kk), lambda i, j: (i, 0)), b, pl.BlockSpec((kk, tn), lambda i, j: (0, j0 + j)))]
        o_spec = pl.BlockSpec((tm, tn), lambda i, j: (i, j0 + j))
        add_ = None if add is None else (add, pl.BlockSpec((tm, tn), lambda i, j: (i, j0 + j)))
        return _mm(name, pairs, mode="nn", grid=grid, out_shape=out_shape, o_spec=o_spec, add=add_, deps=deps, prev=prev)
    tn = _tile(n, 1024)
    nk = kk // tk
    grid = (m // tm, n // tn, nk)
    pairs = [(a, pl.BlockSpec((tm, tk), lambda i, j, k: (i, k)), b, pl.BlockSpec((tk, tn), lambda i, j, k: (k, j)))]
    o_spec = pl.BlockSpec((tm, tn), lambda i, j, k: (i, j))
    add_ = None if add is None else (add, pl.BlockSpec((tm, tn), lambda i, j, k: (i, j)))
    return _mm(name, pairs, mode="nn", grid=grid, out_shape=out_shape, o_spec=o_spec, nk=nk, kaxis=2, add=add_, deps=deps)


def _mm_nt(name, abs_, out_dtype, tk, deps=()):
    m, kk = abs_[0][0].shape
    n = abs_[0][1].shape[0]
    tm = _tile(m, 1024)
    nk = kk // tk
    tn = _tile(n, COL_TILE if nk == 1 else 1024)
    out_shape = jax.ShapeDtypeStruct((m, n), out_dtype)
    if nk == 1:
        grid = (m // tm, n // tn)
        pairs = [(a, pl.BlockSpec((tm, kk), lambda i, j: (i, 0)), b, pl.BlockSpec((tn, kk), lambda i, j: (j, 0)))
                 for a, b in abs_]
        o_spec = pl.BlockSpec((tm, tn), lambda i, j: (i, j))
        return _mm(name, pairs, mode="nt", grid=grid, out_shape=out_shape, o_spec=o_spec, deps=deps)
    grid = (m // tm, n // tn, nk)
    pairs = [(a, pl.BlockSpec((tm, tk), lambda i, j, k: (i, k)), b, pl.BlockSpec((tn, tk), lambda i, j, k: (j, k)))
             for a, b in abs_]
    o_spec = pl.BlockSpec((tm, tn), lambda i, j, k: (i, j))
    return _mm(name, pairs, mode="nt", grid=grid, out_shape=out_shape, o_spec=o_spec, nk=nk, kaxis=2, deps=deps)


def _mm_tn(name, a, b, out_dtype, deps=()):
    t, m = a.shape
    n = b.shape[1]
    tm, tn = _tile(m, 512), _tile(n, 2048)
    if n > m:
        grid = (n // tn, m // tm)
        a_map, b_map, o_map = (lambda j, i: (0, i)), (lambda j, i: (0, j)), (lambda j, i: (i, j))
    else:
        grid = (m // tm, n // tn)
        a_map, b_map, o_map = (lambda i, j: (0, i)), (lambda i, j: (0, j)), (lambda i, j: (i, j))
    pairs = [(a, pl.BlockSpec((t, tm), a_map), b, pl.BlockSpec((t, tn), b_map))]
    o_spec = pl.BlockSpec((tm, tn), o_map)
    return _mm(name, pairs, mode="tn", grid=grid, out_shape=jax.ShapeDtypeStruct((m, n), out_dtype), o_spec=o_spec,
               deps=deps)


def _mm_tn_half(name, a, b, pos, mine, add=None, deps=()):
    t, m = a.shape
    r, n = m // 2, b.shape[1]
    tm, tn = _tile(r, 512), _tile(n, 2048)
    nbi = r // tm
    half = (lambda pos: pos[0]) if mine else (lambda pos: 1 - pos[0])
    if n > r:
        grid, ij = (n // tn, nbi), (lambda g0, g1: (g1, g0))
    else:
        grid, ij = (nbi, n // tn), (lambda g0, g1: (g0, g1))
    has_add = add is not None

    def body(pos_ref, a_ref, b_ref, *rest):
        d = lax.dot_general(a_ref[...], b_ref[...], _DIMS["tn"], preferred_element_type=F32)
        if has_add:
            d = d + rest[0][...].astype(F32)
        rest[-1][...] = d.astype(BF16)

    o_spec = pl.BlockSpec((None, tm, tn), lambda g0, g1, pos: (0, *ij(g0, g1)))
    grid_spec = pltpu.PrefetchScalarGridSpec(
        num_scalar_prefetch=1, grid=grid,
        in_specs=[pl.BlockSpec((t, tm), lambda g0, g1, pos: (0, half(pos) * nbi + ij(g0, g1)[0])),
                  pl.BlockSpec((t, tn), lambda g0, g1, pos: (0, ij(g0, g1)[1]))]
        + ([o_spec] if has_add else []) + [ANY] * len(deps),
        out_specs=o_spec)
    return pl.pallas_call(body, name=name, grid_spec=grid_spec, out_shape=jax.ShapeDtypeStruct((1, r, n), BF16),
                          compiler_params=_cp(("parallel",) * 2))(pos, a, b, *([add] if has_add else []), *deps)


def _proj_piece(name, h, w, prev, kvec, base, count, deps=()):
    t, kk = h.shape
    own = w.dtype == F32
    nn = w.shape[1] if own else w.shape[1] // N_CHIPS
    tm, tn = _tile(t, 1024), _tile(nn, COL_TILE)
    nb = nn // tn

    def body(kv_ref, h_ref, w_ref, *rest):
        rest[-1][...] = lax.dot_general(h_ref[...], w_ref[...].astype(BF16), _DIMS["nn"],
                                        preferred_element_type=F32).astype(BF16)

    cols = lambda s, i, j, kv: (0, j) if own else (0, kv[base + s] * nb + j)
    extra = ([] if prev is None else [prev]) + list(deps)
    grid_spec = pltpu.PrefetchScalarGridSpec(
        num_scalar_prefetch=1, grid=(count, t // tm, nb),
        in_specs=[pl.BlockSpec((tm, kk), lambda s, i, j, kv: (i, 0)), pl.BlockSpec((kk, tn), cols)] + [ANY] * len(extra),
        out_specs=pl.BlockSpec((tm, tn), lambda s, i, j, kv: (i, kv[base + s] * nb + j)))
    return pl.pallas_call(body, name=name, grid_spec=grid_spec, out_shape=jax.ShapeDtypeStruct((t, N_CHIPS * nn), BF16),
                          input_output_aliases={} if prev is None else {3: 0},
                          compiler_params=_cp(("parallel",) * 3))(kvec, h, w, *extra)


def _gmm_nn(name, p, w, out_dtype):
    t = p.shape[0]
    g, cg, dg = w.shape
    tm = _tile(t, 1024)
    pairs = [(p, pl.BlockSpec((tm, cg), lambda i, j: (i, j)), w, pl.BlockSpec((None, cg, dg), lambda i, j: (j, 0, 0)))]
    o_spec = pl.BlockSpec((tm, dg), lambda i, j: (i, j))
    return _mm(name, pairs, mode="nn", grid=(t // tm, g), out_shape=jax.ShapeDtypeStruct((t, g * dg), out_dtype),
               o_spec=o_spec)


def _gmm_nt(name, dy, w, out_dtype):
    t = dy.shape[0]
    g, cg, dg = w.shape
    tm = _tile(t, 1024)
    pairs = [(dy, pl.BlockSpec((tm, dg), lambda i, j: (i, j)), w, pl.BlockSpec((None, cg, dg), lambda i, j: (j, 0, 0)))]
    o_spec = pl.BlockSpec((tm, cg), lambda i, j: (i, j))
    return _mm(name, pairs, mode="nt", grid=(t // tm, g), out_shape=jax.ShapeDtypeStruct((t, g * cg), out_dtype),
               o_spec=o_spec)


def _gmm_tn(name, p, dy, g, out_dtype):
    t = p.shape[0]
    cg, dg = p.shape[1] // g, dy.shape[1] // g
    pairs = [(p, pl.BlockSpec((t, cg), lambda j: (0, j)), dy, pl.BlockSpec((t, dg), lambda j: (0, j)))]
    o_spec = pl.BlockSpec((None, cg, dg), lambda j: (j, 0, 0))
    return _mm(name, pairs, mode="tn", grid=(g,), out_shape=jax.ShapeDtypeStruct((g, cg, dg), out_dtype), o_spec=o_spec)


ROW_TILE = 256


def _rows(t):
    return _tile8(t, ROW_TILE)


def _tile8(n, pref):
    if n <= pref:
        return n
    for t in range(pref, 0, -8):
        if n % t == 0:
            return t
    raise ValueError(f"no row tile for {n}")


def _cast_place(name, w, pos, shard, deps=()):
    tr = _tile8(w.R, 512)
    if w.colshard:
        o_map = lambda h, i, pos: (0, h, i, pos[1])
    else:
        o_map = lambda h, i, pos: (pos[1], h, i, 0)

    def body(pos_ref, w_ref, *rest):
        rest[-1][...] = w_ref[...].astype(BF16)

    grid_spec = pltpu.PrefetchScalarGridSpec(
        num_scalar_prefetch=1, grid=(2, w.R // tr),
        in_specs=[pl.BlockSpec((None, tr, w.nn), lambda h, i, pos: (h, i, 0))] + [ANY] * len(deps),
        out_specs=pl.BlockSpec((None, None, tr, w.nn), o_map))
    return pl.pallas_call(body, name=name, grid_spec=grid_spec, out_shape=jax.ShapeDtypeStruct((w.P, 2, w.R, w.N), BF16),
                          compiler_params=_cp(("parallel", "parallel")))(pos, shard, *deps)


def _rms_fwd(name, x, g, deps=()):
    t, d = x.shape
    tm = _rows(t)

    def body(x_ref, g_ref, *rest):
        xf = x_ref[...]
        r = lax.rsqrt(jnp.mean(xf * xf, axis=-1, keepdims=True) + EPS)
        rest[-1][...] = (xf * r * g_ref[...]).astype(BF16)

    return pl.pallas_call(
        body, name=name, grid=(t // tm,),
        in_specs=[pl.BlockSpec((tm, d), lambda i: (i, 0)), pl.BlockSpec((1, d), lambda i: (0, 0))] + [ANY] * len(deps),
        out_specs=pl.BlockSpec((tm, d), lambda i: (i, 0)), out_shape=jax.ShapeDtypeStruct((t, d), BF16),
        compiler_params=_cp(("parallel",)),
    )(x, g, *deps)


def _rms_bwd(name, x, g, dh, dres, want_bf16, deps=()):
    t, d = x.shape
    tm = _rows(t)

    def body(x_ref, g_ref, dh_ref, dres_ref, *rest):
        rest = rest[len(deps):]
        dx_ref, rest = rest[0], rest[1:]
        dg_ref = rest[-1]
        xf = x_ref[...]
        r = lax.rsqrt(jnp.mean(xf * xf, axis=-1, keepdims=True) + EPS)
        xh = xf * r
        dhf = dh_ref[...].astype(F32)
        dxh = dhf * g_ref[...]
        m = jnp.mean(dxh * xh, axis=-1, keepdims=True)
        dx = dres_ref[...] + r * (dxh - xh * m)
        dx_ref[...] = dx
        if want_bf16:
            rest[0][...] = dx.astype(BF16)

        @pl.when(pl.program_id(0) == 0)
        def _():
            dg_ref[...] = jnp.zeros_like(dg_ref)

        dg_ref[...] += jnp.sum(dhf * xh, axis=0, keepdims=True)

    row = pl.BlockSpec((tm, d), lambda i: (i, 0))
    vec = pl.BlockSpec((1, d), lambda i: (0, 0))
    out_specs = [row] + ([row] if want_bf16 else []) + [vec]
    out_shape = ([jax.ShapeDtypeStruct((t, d), F32)] + ([jax.ShapeDtypeStruct((t, d), BF16)] if want_bf16 else [])
                 + [jax.ShapeDtypeStruct((1, d), F32)])
    return pl.pallas_call(body, name=name, grid=(t // tm,), in_specs=[row, vec, row, row] + [ANY] * len(deps),
                          out_specs=out_specs, out_shape=out_shape, compiler_params=_cp(("arbitrary",)))(x, g, dh, dres, *deps)


def _final_bwd(name, x3, gf, tgt):
    t, d = x3.shape
    tm = _rows(t)

    def body(x_ref, g_ref, t_ref, dx_ref, dxb_ref, dg_ref, lc_ref):
        xf = x_ref[...]
        g = g_ref[...]
        r = lax.rsqrt(jnp.mean(xf * xf, axis=-1, keepdims=True) + EPS)
        xh = xf * r
        diff = xh * g - t_ref[...]
        dy = diff * (1.0 / d)
        dxh = dy * g
        m = jnp.mean(dxh * xh, axis=-1, keepdims=True)
        dx = r * (dxh - xh * m)
        dx_ref[...] = dx
        dxb_ref[...] = dx.astype(BF16)

        @pl.when(pl.program_id(0) == 0)
        def _():
            dg_ref[...] = jnp.zeros_like(dg_ref)
            lc_ref[...] = jnp.zeros_like(lc_ref)

        dg_ref[...] += jnp.sum(dy * xh, axis=0, keepdims=True)
        lc_ref[...] += jnp.sum(diff * diff, axis=0, keepdims=True) * (0.5 / d)

    row = pl.BlockSpec((tm, d), lambda i: (i, 0))
    vec = pl.BlockSpec((1, d), lambda i: (0, 0))
    return pl.pallas_call(
        body, name=name, grid=(t // tm,), in_specs=[row, vec, row], out_specs=[row, row, vec, vec],
        out_shape=[jax.ShapeDtypeStruct((t, d), F32), jax.ShapeDtypeStruct((t, d), BF16),
                   jax.ShapeDtypeStruct((1, d), F32), jax.ShapeDtypeStruct((1, d), F32)],
        compiler_params=_cp(("arbitrary",)),
    )(x3, gf, tgt)


def _shift_down(v, k, t_idx):
    return jnp.where(t_idx >= k, pltpu.roll(v, k, 0), 0.0)


def _shift_up(v, k, t_idx):
    n = v.shape[0]
    return jnp.where(t_idx < n - k, pltpu.roll(v, n - k, 0), 0.0)


def _window_sums(v, shift, t_idx, grp):
    s = v + shift(v, 1, t_idx)
    out = s
    for lvl in range(1, len(POOL_WINDOWS)):
        s = s + shift(s, 1 << lvl, t_idx)
        out = jnp.where(grp >= lvl, s, out)
    return out


def _window_weight(t_idx, grp):
    return 1.0 / jnp.minimum(t_idx[:, :1] + 1, jnp.left_shift(2, grp)).astype(F32)


MIX_COLS = 256


def _mixer_fwd(name, proj, cw, cb, n_conv, n_groups, deps=()):
    t = proj.shape[0]
    nb = n_conv // MIX_COLS
    per_group = n_conv // n_groups // MIX_COLS

    def body(ba_ref, ca_ref, va_ref, vb_ref, cw_ref, cb_ref, *rest):
        z_ref, p_ref = rest[len(deps):]
        t_idx = lax.broadcasted_iota(jnp.int32, (t, MIX_COLS), 0)
        q = ca_ref[...].astype(F32) * va_ref[...].astype(F32)
        w = cw_ref[...]
        u = cb_ref[...] + w[0:1] * _shift_down(q, 2, t_idx) + w[1:2] * _shift_down(q, 1, t_idx) + w[2:3] * q
        z_ref[...] = (ba_ref[...].astype(F32) * u).astype(BF16)
        grp = pl.program_id(0) // per_group
        v = vb_ref[...].astype(F32)
        p_ref[...] = (_window_sums(v, _shift_down, t_idx, grp) * _window_weight(t_idx, grp) - v).astype(BF16)

    col = lambda s: pl.BlockSpec((t, MIX_COLS), lambda j: (0, s * nb + j))
    return pl.pallas_call(
        body, name=name, grid=(nb,),
        in_specs=[col(0), col(1), col(2), col(3), pl.BlockSpec((3, MIX_COLS), lambda j: (0, j)),
                  pl.BlockSpec((1, MIX_COLS), lambda j: (0, j))] + [ANY] * len(deps),
        out_specs=[col(0), col(0)],
        out_shape=[jax.ShapeDtypeStruct((t, n_conv), BF16), jax.ShapeDtypeStruct((t, n_conv), BF16)],
        compiler_params=_cp(("parallel",)),
    )(proj, proj, proj, proj, cw, cb, *deps)


def _mixer_bwd(name, dz, dp, proj, cw, cb, dproj, n_conv, n_groups, deps=()):
    t = proj.shape[0]
    nb = n_conv // MIX_COLS
    per_group = n_conv // n_groups // MIX_COLS

    def body(dz_ref, dp_ref, ba_ref, ca_ref, va_ref, cw_ref, cb_ref, _, *rest):
        o_ref, dcw_ref, dcb_ref, scr = rest[len(deps):]
        s = pl.program_id(1)

        @pl.when(s == 0)
        def _():
            t_idx = lax.broadcasted_iota(jnp.int32, (t, MIX_COLS), 0)
            ca, va = ca_ref[...].astype(F32), va_ref[...].astype(F32)
            q = ca * va
            q1, q2 = _shift_down(q, 1, t_idx), _shift_down(q, 2, t_idx)
            w = cw_ref[...]
            u = cb_ref[...] + w[0:1] * q2 + w[1:2] * q1 + w[2:3] * q
            dzf = dz_ref[...].astype(F32)
            du = dzf * ba_ref[...].astype(F32)
            scr[0] = (dzf * u).astype(BF16)
            dq = w[2:3] * du + w[1:2] * _shift_up(du, 1, t_idx) + w[0:1] * _shift_up(du, 2, t_idx)
            scr[1] = (dq * va).astype(BF16)
            scr[2] = (dq * ca).astype(BF16)
            dcb_ref[...] = jnp.sum(du, axis=0, keepdims=True)
            dcw_ref[0:1, :] = jnp.sum(du * q2, axis=0, keepdims=True)
            dcw_ref[1:2, :] = jnp.sum(du * q1, axis=0, keepdims=True)
            dcw_ref[2:3, :] = jnp.sum(du * q, axis=0, keepdims=True)
            grp = pl.program_id(0) // per_group
            dpf = dp_ref[...].astype(F32)
            e = dpf * _window_weight(t_idx, grp)
            scr[3] = (_window_sums(e, _shift_up, t_idx, grp) - dpf).astype(BF16)

        o_ref[...] = scr[s]

    col = lambda c: pl.BlockSpec((t, MIX_COLS), lambda j, s: (0, c * nb + j))
    own = pl.BlockSpec((t, MIX_COLS), lambda j, s: (0, j))
    return pl.pallas_call(
        body, name=name, grid=(nb, 4),
        in_specs=[own, own, col(0), col(1), col(2), pl.BlockSpec((3, MIX_COLS), lambda j, s: (0, j)),
                  pl.BlockSpec((1, MIX_COLS), lambda j, s: (0, j)), ANY] + [ANY] * len(deps),
        out_specs=[pl.BlockSpec((t, MIX_COLS), lambda j, s: (0, s * nb + j)),
                   pl.BlockSpec((3, MIX_COLS), lambda j, s: (0, j)), pl.BlockSpec((1, MIX_COLS), lambda j, s: (0, j))],
        out_shape=[jax.ShapeDtypeStruct(dproj.shape, BF16), jax.ShapeDtypeStruct((3, n_conv), F32),
                   jax.ShapeDtypeStruct((1, n_conv), F32)],
        scratch_shapes=[pltpu.VMEM((4, t, MIX_COLS), BF16)],
        input_output_aliases={7: 0},
        compiler_params=_cp(("arbitrary", "arbitrary")),
    )(dz, dp, proj, proj, proj, cw, cb, dproj, *deps)


def _merge_fwd(name, proj, bg, ya, yb, ps):
    t, d = ya.shape
    tm = _rows(t)

    def body(gab_ref, bg_ref, ya_ref, yb_ref, ps_ref, o_ref):
        gab = gab_ref[...].astype(F32) + bg_ref[...]
        sa, sb = jax.nn.sigmoid(gab[:, :d]), jax.nn.sigmoid(gab[:, d:])
        o_ref[...] = (sa * ya_ref[...].astype(F32) + sb * (yb_ref[...].astype(F32) * ps_ref[...])).astype(BF16)

    row = pl.BlockSpec((tm, d), lambda i: (i, 0))
    return pl.pallas_call(
        body, name=name, grid=(t // tm,),
        in_specs=[pl.BlockSpec((tm, 2 * d), lambda i: (i, 1)), pl.BlockSpec((1, 2 * d), lambda i: (0, 0)), row, row,
                  pl.BlockSpec((1, d), lambda i: (0, 0))],
        out_specs=row, out_shape=jax.ShapeDtypeStruct((t, d), BF16), compiler_params=_cp(("parallel",)),
    )(proj, bg, ya, yb, ps)


def _merge_bwd(name, dm, proj, bg, ya, yb, ps, deps=()):
    t, d = ya.shape
    tm = _rows(t)

    def body(dm_ref, gab_ref, bg_ref, ya_ref, yb_ref, ps_ref, *rest):
        dya_ref, dyb_ref, dg_ref, dba_ref, dbb_ref, dps_ref = rest[len(deps):]
        gab = gab_ref[...].astype(F32) + bg_ref[...]
        sa, sb = jax.nn.sigmoid(gab[:, :d]), jax.nn.sigmoid(gab[:, d:])
        dmf = dm_ref[...].astype(F32)
        ybf, ps_ = yb_ref[...].astype(F32), ps_ref[...]
        dya_ref[...] = (dmf * sa).astype(BF16)
        dyb = dmf * sb
        dyb_ref[...] = (dyb * ps_).astype(BF16)
        dga = dmf * ya_ref[...].astype(F32) * sa * (1.0 - sa)
        dgb = dmf * (ybf * ps_) * sb * (1.0 - sb)
        dg_ref[:, :d] = dga.astype(BF16)
        dg_ref[:, d:] = dgb.astype(BF16)

        @pl.when(pl.program_id(0) == 0)
        def _():
            dba_ref[...] = jnp.zeros_like(dba_ref)
            dbb_ref[...] = jnp.zeros_like(dbb_ref)
            dps_ref[...] = jnp.zeros_like(dps_ref)

        dba_ref[...] += jnp.sum(dga, axis=0, keepdims=True)
        dbb_ref[...] += jnp.sum(dgb, axis=0, keepdims=True)
        dps_ref[...] += jnp.sum(dyb * ybf, axis=0, keepdims=True)

    row = pl.BlockSpec((tm, d), lambda i: (i, 0))
    vec = pl.BlockSpec((1, d), lambda i: (0, 0))
    gates = pl.BlockSpec((tm, 2 * d), lambda i: (i, 1))
    return pl.pallas_call(
        body, name=name, grid=(t // tm,),
        in_specs=[row, gates, pl.BlockSpec((1, 2 * d), lambda i: (0, 0)), row, row, vec] + [ANY] * len(deps),
        out_specs=[row, row, gates, vec, vec, vec],
        out_shape=[jax.ShapeDtypeStruct((t, d), BF16), jax.ShapeDtypeStruct((t, d), BF16),
                   jax.ShapeDtypeStruct(proj.shape, BF16), jax.ShapeDtypeStruct((1, d), F32),
                   jax.ShapeDtypeStruct((1, d), F32), jax.ShapeDtypeStruct((1, d), F32)],
        compiler_params=_cp(("arbitrary",)),
    )(dm, proj, bg, ya, yb, ps, *deps)


def _ffn_up_act(name, h, w_up, gate, part=None, prev=None, deps=()):
    t, d = h.shape
    f = w_up.shape[1]
    tm, tf = _tile(t, 1024), _tile(f, 512)
    j0, j1 = _tile_span(f // tf, part)
    n_prev = 0 if prev is None else 2
    extra = ([] if prev is None else list(prev)) + list(deps)

    def body(h_ref, w_ref, g_ref, *rest):
        u_ref, a_ref = rest[len(extra):]
        u = lax.dot_general(h_ref[...], w_ref[...], _DIMS["nn"], preferred_element_type=F32)
        g = g_ref[...].astype(F32)
        u_ref[...] = u.astype(BF16)
        a_ref[...] = (g * jax.nn.sigmoid(g) * u).astype(BF16)

    blk = pl.BlockSpec((tm, tf), lambda i, j: (i, j0 + j))
    shp = jax.ShapeDtypeStruct((t, f), BF16)
    return pl.pallas_call(
        body, name=name, grid=(t // tm, j1 - j0),
        in_specs=[pl.BlockSpec((tm, d), lambda i, j: (i, 0)), pl.BlockSpec((d, tf), lambda i, j: (0, j0 + j)), blk]
        + [ANY] * len(extra),
        out_specs=[blk, blk], out_shape=[shp, shp], input_output_aliases={3 + i: i for i in range(n_prev)},
        compiler_params=_cp(("parallel", "parallel")))(h, w_up, gate, *extra)


def _ffn_bwd(name, dy, w_down, gate, up):
    t, d = dy.shape
    f = w_down.shape[0]
    tm, tf = _tile(t, 1024), _tile(f, 512)

    def body(dy_ref, w_ref, g_ref, u_ref, dg_ref, du_ref):
        da = lax.dot_general(dy_ref[...], w_ref[...], _DIMS["nt"], preferred_element_type=F32)
        g = g_ref[...].astype(F32)
        s = jax.nn.sigmoid(g)
        du_ref[...] = (da * (g * s)).astype(BF16)
        dg_ref[...] = (da * u_ref[...].astype(F32) * (s * (1.0 + g * (1.0 - s)))).astype(BF16)

    blk = pl.BlockSpec((tm, tf), lambda i, j: (i, j))
    shp = jax.ShapeDtypeStruct((t, f), BF16)
    return pl.pallas_call(
        body, name=name, grid=(t // tm, f // tf),
        in_specs=[pl.BlockSpec((tm, d), lambda i, j: (i, 0)), pl.BlockSpec((tf, d), lambda i, j: (j, 0)), blk, blk],
        out_specs=[blk, blk], out_shape=[shp, shp], compiler_params=_cp(("parallel", "parallel")))(dy, w_down, gate, up)


def _adamw_math(w, g, m, v):
    m = ADAM_B1 * m + (1.0 - ADAM_B1) * g
    v = ADAM_B2 * v + (1.0 - ADAM_B2) * (g * g)
    m_hat = m / (1.0 - ADAM_B1 ** ADAM_STEP)
    v_hat = v / (1.0 - ADAM_B2 ** ADAM_STEP)
    delta = -ADAM_LR * (m_hat / (jnp.sqrt(v_hat) + ADAM_EPS) + ADAM_WD * w)
    return delta, m, v


def _adamw(name, w, g, m, v):
    r, c = w.shape
    tr = _tile8(r, 512 if c <= 1024 else 256)

    def body(w_ref, g_ref, m_ref, v_ref, go_ref, d_ref, nm_ref, nv_ref):
        g = g_ref[...]
        go_ref[...] = g
        d_ref[...], nm_ref[...], nv_ref[...] = _adamw_math(w_ref[...], g, m_ref[...], v_ref[...])

    blk = pl.BlockSpec((tr, c), lambda i: (i, 0))
    shp = jax.ShapeDtypeStruct((r, c), F32)
    return pl.pallas_call(body, name=name, grid=(r // tr,), in_specs=[blk] * 4, out_specs=[blk] * 4,
                          out_shape=[shp] * 4, compiler_params=_cp(("parallel",)))(w, g, m, v)


class _Weight:
    def __init__(self, name, rows, cols, colshard):
        self.name, self.colshard = name, colshard
        self.R, self.nn = rows // 2, cols
        self.P = 1 if colshard else N_CHIPS
        self.N = N_CHIPS * cols if colshard else cols

    def cols(self, k):
        return pl.ds(pl.multiple_of(k * self.nn, LANES), self.nn)

    def shard(self, ref, k):
        return ref.at[0, :, :, self.cols(k)] if self.colshard else ref.at[k]

    def half(self, ref, k, h):
        return ref.at[0, h, :, self.cols(k)] if self.colshard else ref.at[k, h]

    def quarter(self, ref, k, h, q):
        return self.half(ref, k, h).at[pl.ds(q * (self.R // 2), self.R // 2), :]

    def part(self, ref, k):
        return ref.at[0, :, self.cols(k)] if self.colshard else ref.at[k]


def _remote(src, dst, ssem, rsem, dev):
    return pltpu.make_async_remote_copy(src_ref=src, dst_ref=dst, send_sem=ssem, recv_sem=rsem, device_id=dev,
                                        device_id_type=MESH)


def _other_chips(x, y):
    chips = [(1 - x, y), (x, 1 - y), (1 - x, 1 - y)]
    return chips, [2 * cx + cy for cx, cy in chips]


def _hbm(a):
    return pltpu.with_memory_space_constraint(a, pltpu.HBM)


def _split_start(name, arrays, sets, after=()):
    na, ns = len(arrays), len(sets)

    def body(*refs):
        outs = refs[na + len(after):]
        for s_, (idx, copies) in enumerate(sets):
            for i, (src, dst, dev, _) in enumerate(copies([refs[k] for k in idx], *_mesh_pos())):
                _remote(src, dst, outs[2 * s_].at[i], outs[2 * s_ + 1].at[i], dev).start()
        outs[2 * ns + na][...] = jnp.zeros((8, LANES), F32)

    sems = []
    for _, copies in sets:
        sems += [pltpu.SemaphoreType.DMA((copies.n,))] * 2
    out = pl.pallas_call(
        body, name=name, in_specs=[HBM] * na + [ANY] * len(after), out_specs=[SEM] * (2 * ns) + [HBM] * na + [VMEM],
        out_shape=sems + [pltpu.HBM(a.shape, a.dtype) for a in arrays] + [jax.ShapeDtypeStruct((8, LANES), F32)],
        input_output_aliases={i: 2 * ns + i for i in range(na)},
        compiler_params=pltpu.CompilerParams(has_side_effects=EFFECT),
    )(*[_hbm(a) for a in arrays], *after)
    return [(out[2 * i], out[2 * i + 1]) for i in range(ns)], list(out[2 * ns:2 * ns + na]), out[-1]


def _split_wait(name, arrays, ssem, rsem, copies, after):
    na = len(arrays)

    def body(*refs):
        for i, (src, _, dev, dst) in enumerate(copies(refs[:na], *_mesh_pos())):
            cp = _remote(src, dst, refs[na].at[i], refs[na + 1].at[i], dev)
            cp.wait_send()
            cp.wait_recv()

    return list(pl.pallas_call(
        body, name=name, in_specs=[HBM] * na + [SEM, SEM] + [ANY] * len(after), out_specs=[HBM] * na,
        out_shape=[pltpu.HBM(a.shape, a.dtype) for a in arrays], input_output_aliases={i: i for i in range(na)},
        compiler_params=pltpu.CompilerParams(has_side_effects=EFFECT),
    )(*arrays, ssem, rsem, *after))


def _pass_copies(grp, rels=(0, 1, 2)):
    def copies(land, x, y, c):
        _, ks = _other_chips(x, y)
        return [(w.half(land[wi], ks[j], c), w.half(land[wi], ks[j], c), (x, y, 1 - c), w.half(land[wi], ks[j], 1 - c))
                for wi, w in enumerate(grp) for j in rels]
    copies.n = len(grp) * len(rels)
    return copies


def _direct_copies(grp):
    def copies(land, x, y, c):
        chips, ks = _other_chips(x, y)
        out = []
        for wi, w in enumerate(grp):
            mine = w.half(land[wi], 2 * x + y, c)
            out += [(mine, mine, (*chips[j], c), w.half(land[wi], ks[j], c)) for j in range(3)]
        return out
    copies.n = 3 * len(grp)
    return copies


def _near_copies(grp):
    def copies(land, x, y, c):
        chips, ks = _other_chips(x, y)
        out = []
        for wi, w in enumerate(grp):
            mine = w.half(land[wi], 2 * x + y, c)
            out += [(mine, mine, (*chips[j], c), w.half(land[wi], ks[j], c)) for j in (0, 1)]
        return out
    copies.n = 2 * len(grp)
    return copies


def _far_copies(grp):
    def copies(land, x, y, c):
        chips, ks = _other_chips(x, y)
        out = []
        for wi, w in enumerate(grp):
            for j in (0, 1):
                q = w.quarter(land[wi], ks[j], c, j)
                out.append((q, q, (*chips[1 - j], c), w.quarter(land[wi], ks[2], c, j)))
        return out
    copies.n = 2 * len(grp)
    return copies


def _pair_copies(n, whole=False):
    def copies(refs, x, y, c):
        return [(refs[i] if whole else refs[i].at[:, 1 - c], refs[n + i], (x, y, 1 - c), refs[n + i]) for i in range(n)]
    copies.n = n
    return copies


def _share_copies(n):
    def copies(refs, x, y, c):
        return [(refs[i].at[c], refs[i].at[c], (x, y, 1 - c), refs[i].at[1 - c]) for i in range(n)]
    copies.n = n
    return copies


def _gather_conv_w(cw, thru):
    ncw = cw.shape[1]

    def body(cw_ref, _, out_ref, __, ssem, rsem):
        x, y, c = _mesh_pos()
        k_me = 2 * x + y
        chips, ks = _other_chips(x, y)
        cols = lambda k: out_ref.at[:, pl.ds(pl.multiple_of(k * ncw, LANES), ncw)]
        cps = [_remote(cw_ref, cols(k_me), ssem.at[j], rsem.at[j], (*chip, c)) for j, chip in enumerate(chips)]
        for cp in cps:
            cp.start()
        for k in range(N_CHIPS):
            @pl.when(k_me == k)
            def _():
                out_ref[:, k * ncw:(k + 1) * ncw] = cw_ref[...]
        for j in range(3):
            _remote(cw_ref, cols(ks[j]), ssem.at[j], rsem.at[j], (*chips[j], c)).wait_recv()
        for cp in cps:
            cp.wait_send()

    return pl.pallas_call(
        body, name="gather_conv_w", in_specs=[VMEM, ANY], out_specs=[VMEM, ANY],
        out_shape=[jax.ShapeDtypeStruct((3, N_CHIPS * ncw), F32), jax.ShapeDtypeStruct(thru.shape, thru.dtype)],
        scratch_shapes=[pltpu.SemaphoreType.DMA((3,)), pltpu.SemaphoreType.DMA((3,))],
        input_output_aliases={1: 1},
    )(cw, thru)


def _grad_tiles(w, n):
    return _tile8(w.R, 512) if w.R <= 512 else w.R // 2, _tile(n, 2048)


def _pair_sum(name, w, pos, grad, got):
    tr, tn = _grad_tiles(w, w.N)

    def body(pos_ref, g_ref, r_ref, o_ref):
        o_ref[...] = (g_ref[...].astype(F32) + r_ref[...].astype(F32)).astype(BF16)

    blk = pl.BlockSpec((None, tr, tn), lambda p, i, j, pos: (p, i, j))
    grid_spec = pltpu.PrefetchScalarGridSpec(
        num_scalar_prefetch=1, grid=(w.P, w.R // tr, w.N // tn),
        in_specs=[pl.BlockSpec((None, None, tr, tn), lambda p, i, j, pos: (p, pos[0], i, j)), blk], out_specs=blk)
    return pl.pallas_call(body, name=name, grid_spec=grid_spec, out_shape=jax.ShapeDtypeStruct((w.P, w.R, w.N), BF16),
                          compiler_params=_cp(("parallel",) * 3))(pos, grad, got)


def _scatter_start(name, ws, pairs):
    nw = len(ws)

    def body(*refs):
        pr, land = refs[:nw], refs[nw:2 * nw]
        ssem, rsem = refs[2 * nw], refs[2 * nw + 1]
        token = refs[4 * nw + 2]
        x, y, c = _mesh_pos()
        chips, ks = _other_chips(x, y)
        for i, w in enumerate(ws):
            for j, chip in enumerate(chips):
                _remote(w.part(pr[i], ks[j]), land[i].at[j], ssem.at[3 * i + j], rsem.at[3 * i + j], (*chip, c)).start()
        token[...] = jnp.zeros_like(token)

    lands = [lax.empty((3, w.R, w.nn), BF16) for w in ws]
    out = pl.pallas_call(
        body, name=name, in_specs=[HBM] * (2 * nw),
        out_specs=[SEM, SEM] + [HBM] * (2 * nw) + [VMEM],
        out_shape=[pltpu.SemaphoreType.DMA((3 * nw,))] * 2 + [pltpu.HBM(a.shape, a.dtype) for a in pairs + lands]
        + [jax.ShapeDtypeStruct((8, LANES), F32)],
        input_output_aliases={i: 2 + i for i in range(2 * nw)},
        compiler_params=pltpu.CompilerParams(has_side_effects=EFFECT),
    )(*[_hbm(a) for a in pairs + lands])
    return out[0], out[1], list(out[2:2 + nw]), list(out[2 + nw:2 + 2 * nw]), out[-1]


def _scatter_wait(name, ws, pairs, lands, ssem, rsem, after):
    nw = len(ws)

    def body(*refs):
        pr, land = refs[:nw], refs[nw:2 * nw]
        ssem_ref, rsem_ref = refs[2 * nw], refs[2 * nw + 1]
        x, y, c = _mesh_pos()
        chips, ks = _other_chips(x, y)
        for i, w in enumerate(ws):
            for j, chip in enumerate(chips):
                cp = _remote(w.part(pr[i], ks[j]), land[i].at[j], ssem_ref.at[3 * i + j], rsem_ref.at[3 * i + j], (*chip, c))
                cp.wait_send()
                cp.wait_recv()

    out = pl.pallas_call(
        body, name=name, in_specs=[HBM] * (2 * nw) + [SEM, SEM] + [ANY] * len(after), out_specs=[HBM] * (2 * nw),
        out_shape=[pltpu.HBM(a.shape, a.dtype) for a in pairs + lands],
        input_output_aliases={i: i for i in range(2 * nw)},
        compiler_params=pltpu.CompilerParams(has_side_effects=EFFECT),
    )(*pairs, *lands, ssem, rsem, *after)
    return list(out[:nw]), list(out[nw:])


def _final_sum(name, w, pos, grad, got, parts):
    tr, tn = _grad_tiles(w, w.nn)
    nbc = w.nn // tn
    if got is None:
        return _final_sum_pair(name, w, pos, grad, parts, tr, tn)

    def body(pos_ref, g_ref, r_ref, p_ref, o_ref):
        acc = g_ref[...].astype(F32) + r_ref[...].astype(F32)
        for j in range(3):
            acc = acc + p_ref[j].astype(F32)
        o_ref[...] = acc

    if w.colshard:
        g_spec = pl.BlockSpec((None, None, tr, tn), lambda i, j, pos: (0, pos[0], i, pos[1] * nbc + j))
        r_spec = pl.BlockSpec((None, tr, tn), lambda i, j, pos: (0, i, pos[1] * nbc + j))
    else:
        g_spec = pl.BlockSpec((None, None, tr, tn), lambda i, j, pos: (pos[1], pos[0], i, j))
        r_spec = pl.BlockSpec((None, tr, tn), lambda i, j, pos: (pos[1], i, j))
    grid_spec = pltpu.PrefetchScalarGridSpec(
        num_scalar_prefetch=1, grid=(w.R // tr, nbc),
        in_specs=[g_spec, r_spec, pl.BlockSpec((3, tr, tn), lambda i, j, pos: (0, i, j))],
        out_specs=pl.BlockSpec((None, tr, tn), lambda i, j, pos: (pos[0], i, j)))
    return pl.pallas_call(body, name=name, grid_spec=grid_spec, out_shape=jax.ShapeDtypeStruct((2, w.R, w.nn), F32),
                          compiler_params=_cp(("parallel",) * 2))(pos, grad, got, parts)


def _final_sum_pair(name, w, pos, pair, parts, tr, tn):
    nbc = w.nn // tn

    def body(pos_ref, g_ref, p_ref, o_ref):
        acc = g_ref[...].astype(F32)
        for j in range(3):
            acc = acc + p_ref[j].astype(F32)
        o_ref[...] = acc

    if w.colshard:
        g_spec = pl.BlockSpec((None, tr, tn), lambda i, j, pos: (0, i, pos[1] * nbc + j))
    else:
        g_spec = pl.BlockSpec((None, tr, tn), lambda i, j, pos: (pos[1], i, j))
    grid_spec = pltpu.PrefetchScalarGridSpec(
        num_scalar_prefetch=1, grid=(w.R // tr, nbc),
        in_specs=[g_spec, pl.BlockSpec((3, tr, tn), lambda i, j, pos: (0, i, j))],
        out_specs=pl.BlockSpec((None, tr, tn), lambda i, j, pos: (pos[0], i, j)))
    return pl.pallas_call(body, name=name, grid_spec=grid_spec, out_shape=jax.ShapeDtypeStruct((2, w.R, w.nn), F32),
                          compiler_params=_cp(("parallel",) * 2))(pos, pair, parts)


VEC_ROWS = 16


def _vector_sum(d, n_conv, parts, deps=()):
    def body(*refs):
        dg1, dba, dbb, dcw, dcb, dps, dg2, dgf, lc = refs[:9]
        tot_ref, snd, got, ssem, rsem = refs[9 + len(deps):]
        x, y, c = _mesh_pos()
        me = 4 * x + 2 * y + c
        snd[...] = jnp.zeros_like(snd)
        for row, ref in ((0, dg1), (1, dba), (2, dbb), (3, dps), (4, dg2), (5, dgf), (6, lc)):
            snd[row:row + 1, :] = ref[...]
        snd[7:8, :n_conv] = dcb[...]
        snd[8:11, :n_conv] = dcw[...]
        cps = []
        for r in range(1, N_DEV):
            peer = tuple(1 - p if (r >> b) & 1 else p for p, b in ((x, 2), (y, 1), (c, 0)))
            cps.append(_remote(snd, got.at[me], ssem.at[r - 1], rsem.at[r - 1], peer))
        for cp in cps:
            cp.start()
        got[me] = snd[...]
        for r in range(1, N_DEV):
            peer = tuple(1 - p if (r >> b) & 1 else p for p, b in ((x, 2), (y, 1), (c, 0)))
            _remote(snd, got.at[4 * peer[0] + 2 * peer[1] + peer[2]], ssem.at[r - 1], rsem.at[r - 1], peer).wait_recv()
        for cp in cps:
            cp.wait_send()
        tot = got[0]
        for dev in range(1, N_DEV):
            tot = tot + got[dev]
        tot_ref[...] = tot

    return pl.pallas_call(
        body, name="vector_params_sum", in_specs=[VMEM] * len(parts) + [ANY] * len(deps), out_specs=VMEM,
        out_shape=jax.ShapeDtypeStruct((VEC_ROWS, d), F32),
        scratch_shapes=[pltpu.VMEM((VEC_ROWS, d), F32), pltpu.VMEM((N_DEV, VEC_ROWS, d), F32),
                        pltpu.SemaphoreType.DMA((N_DEV - 1,)), pltpu.SemaphoreType.DMA((N_DEV - 1,))],
        compiler_params=pltpu.CompilerParams(vmem_limit_bytes=VMEM_LIMIT),
    )(*parts, *deps)


def _vector_update(tot, n_conv, params):
    ncw = params[2][0].shape[1]
    n_par = len(params)

    def body(*refs):
        tot = refs[0][...]
        wmv = refs[1:1 + 3 * n_par]
        outs = refs[1 + 3 * n_par:1 + 7 * n_par]
        refs[1 + 7 * n_par][...] = jnp.sum(tot[6:7, :], axis=1, keepdims=True)
        k_me = 2 * lax.axis_index("x") + lax.axis_index("y")
        g_cw = jnp.zeros((3, ncw), F32)
        for k in range(N_CHIPS):
            g_cw = g_cw + jnp.where(k_me == k, tot[8:11, k * ncw:(k + 1) * ncw], 0.0)
        grads = [tot[0:1, :], jnp.concatenate([tot[1:2, :], tot[2:3, :]], axis=1), g_cw, tot[7:8, :n_conv],
                 tot[3:4, :], tot[4:5, :], tot[5:6, :]]
        for i, g in enumerate(grads):
            w_ref, m_ref, v_ref = wmv[3 * i:3 * i + 3]
            delta, nm, nv = _adamw_math(w_ref[...], g, m_ref[...], v_ref[...])
            outs[4 * i][...] = g
            outs[4 * i + 1][...] = delta
            outs[4 * i + 2][...] = nm
            outs[4 * i + 3][...] = nv

    args = [tot]
    out_shape = []
    for w, m, v in params:
        args += [w, m, v]
        out_shape += [jax.ShapeDtypeStruct(w.shape, F32)] * 4
    out_shape.append(jax.ShapeDtypeStruct((1, 1), F32))
    return pl.pallas_call(body, name="vector_params_update", in_specs=[VMEM] * len(args), out_specs=[VMEM] * len(out_shape),
                          out_shape=out_shape, compiler_params=pltpu.CompilerParams(vmem_limit_bytes=VMEM_LIMIT))(*args)


def kernel(x, norm1_g, w_in, b_gate, conv_w, conv_b, w_a_out, w_pool, pool_scale, w_o, norm2_g, w_ffn_gate, w_ffn_up, w_ffn_down, final_g, loss_target, m_norm1_g, m_w_in, m_b_gate, m_conv_w, m_conv_b, m_w_a_out, m_w_pool, m_pool_scale, m_w_o, m_norm2_g, m_w_ffn_gate, m_w_ffn_up, m_w_ffn_down, m_final_g, v_norm1_g, v_w_in, v_b_gate, v_conv_w, v_conv_b, v_w_a_out, v_w_pool, v_pool_scale, v_w_o, v_norm2_g, v_w_ffn_gate, v_w_ffn_up, v_w_ffn_down, v_final_g):
    t, d = x.shape[1], x.shape[2]
    n_conv = conv_b.shape[1]
    n_groups, pool_cg, pool_dg = w_pool.shape[1], w_pool.shape[2], N_CHIPS * w_pool.shape[3]
    d_ff = N_CHIPS * w_ffn_gate.shape[2]
    assert n_conv // n_groups == pool_cg and n_conv % (n_groups * MIX_COLS) == 0 and n_groups == len(POOL_WINDOWS)

    big = {"w_in": (w_in, m_w_in, v_w_in), "w_a_out": (w_a_out, m_w_a_out, v_w_a_out), "w_pool": (w_pool, m_w_pool, v_w_pool),
           "w_o": (w_o, m_w_o, v_w_o), "w_ffn_gate": (w_ffn_gate, m_w_ffn_gate, v_w_ffn_gate),
           "w_ffn_up": (w_ffn_up, m_w_ffn_up, v_w_ffn_up), "w_ffn_down": (w_ffn_down, m_w_ffn_down, v_w_ffn_down)}
    colshard = {"w_in": True, "w_a_out": True, "w_pool": True, "w_o": False, "w_ffn_gate": True, "w_ffn_up": True,
                "w_ffn_down": False}
    names = list(big)
    shard2d = {n: big[n][0].reshape(-1, big[n][0].shape[-1]) for n in names}
    ws = [_Weight(n, *shard2d[n].shape, colshard[n]) for n in names]

    xs, tgt = x[0], loss_target[0]
    cw_loc = conv_w[0]
    pos = jnp.stack([lax.axis_index("c"), 2 * lax.axis_index("x") + lax.axis_index("y")]).astype(jnp.int32)
    by_name = {w.name: w for w in ws}
    groups = [[by_name[n] for n in g] for g in (["w_in"], ["w_a_out", "w_pool", "w_o"], ["w_ffn_gate"], ["w_ffn_up"],
                                                 ["w_ffn_down"])]
    first = [sum(len(g) for g in groups[:i]) for i in range(len(groups))]
    rgroups = [groups[0], groups[1], groups[2] + groups[3], groups[4]]

    cast = lambda w, dep: _cast_place(f"cast_{w.name}", w, pos, shard2d[w.name].reshape(2, w.R, w.nn), deps=dep)
    chips, ks = _other_chips(lax.axis_index("x"), lax.axis_index("y"))
    kvec = jnp.stack([pos[1], *ks]).astype(jnp.int32)
    full = {}

    def start(name, arrays, copies, after=()):
        return start_many([(name, arrays, copies)], after)[0]

    def start_many(parts, after=()):
        arrays, sets = [], []
        for _, arrs, copies in parts:
            for a in arrs:
                if not any(a is b for b in arrays):
                    arrays.append(a)
            sets.append(([next(i for i, b in enumerate(arrays) if b is a) for a in arrs], copies))
        sems, thru, token = _split_start("_".join(p[0] for p in parts), arrays, sets, after)
        return [(name, [thru[i] for i in idx], ssem, rsem, copies, token)
                for (name, _, copies), (idx, _), (ssem, rsem) in zip(parts, sets, sems)]

    def wait(started, after):
        name, arrays, ssem, rsem, copies, _ = started
        return _split_wait(name + "_wait", arrays, ssem, rsem, copies, after)

    def pass_on(g, got, after=()):
        return start(f"pass_{g}", got, _pass_copies(groups[g]), after)

    def passed(g, st, after=None):
        got = wait(st, [st[5]] if after is None else after)
        full.update({w.name: a.reshape(w.P * 2 * w.R, w.N) for w, a in zip(groups[g], got)})

    near = start("near_0", [cast(w, []) for w in groups[0]], _near_copies(groups[0]))
    rest = [cast(w, [near[5]]) for grp in groups[1:] for w in grp]
    h1 = _rms_fwd("norm1_fwd", xs, norm1_g, deps=[near[5]])
    proj = _proj_piece("proj_own", h1, shard2d["w_in"], None, kvec, 0, 1, deps=rest)
    got = wait(near, [proj])
    far, small, st = start_many([("far_0", got, _far_copies(groups[0])), ("direct_1", rest[:3], _direct_copies(groups[1])),
                                 ("pass_near_0", got, _pass_copies(groups[0], (0, 1)))])
    got = wait(st, [st[5]])
    proj = _proj_piece("proj_near", h1, got[0].reshape(-1, groups[0][0].N), proj, kvec, 1, 2)
    st = start("pass_far_0", wait((far[0], got) + far[2:], [proj]), _pass_copies(groups[0], (2,)))
    got = wait(st, [st[5]])
    w_in_full = got[0].reshape(-1, groups[0][0].N)
    proj = _proj_piece("proj_far", h1, w_in_full, proj, kvec, 3, 1)
    cw_full, proj = _gather_conv_w(cw_loc, proj)
    got = wait(small, [proj])
    near_g = start("near_2", rest[3:4], _near_copies(groups[2]), got)
    st = pass_on(1, got, [near_g[5]])
    z, p = _mixer_fwd("mixer_fwd", proj, cw_full, conv_b, n_conv, n_groups, deps=[st[5]])
    passed(1, st, [z])
    wp_full = full["w_pool"].reshape(n_groups, pool_cg, pool_dg)
    ya = _mm_nn("conv_out", z, full["w_a_out"], BF16)
    yb = _gmm_nn("pool_out", p, wp_full, BF16)
    merged = _merge_fwd("merge_fwd", proj, b_gate, ya, yb, pool_scale)
    far_g, near_u = start_many([("far_2", wait(near_g, [merged]), _far_copies(groups[2])),
                                ("near_3", rest[4:5], _near_copies(groups[3]))])
    x2 = _mm_nn("mix_out", merged, full["w_o"], F32, add=xs, deps=[near_u[5]])
    st = pass_on(2, wait(far_g, [x2]))
    h2 = _rms_fwd("norm2_fwd", x2, norm2_g, deps=[st[5]])
    passed(2, st, [h2])
    def far_and_pass(g, near_st, after, more=()):
        got = wait(near_st, after)
        return start_many([(f"far_{g}", got, _far_copies(groups[g])), (f"pass_near_{g}", got, _pass_copies(groups[g], (0, 1))),
                           *more])

    def finish(g, far_st, pass_st, after):
        got = wait(pass_st, after)
        st = start(f"pass_far_{g}", wait((far_st[0], got) + far_st[2:], after), _pass_copies(groups[g], (2,)))
        passed(g, st)

    gate = _mm_nn("ffn_gate_a", h2, full["w_ffn_gate"], BF16, part=(0, 2))
    far_u, pass_u, near_d = far_and_pass(3, near_u, [gate], [("near_4", rest[5:6], _near_copies(groups[4]))])
    gate = _mm_nn("ffn_gate_b", h2, full["w_ffn_gate"], BF16, part=(1, 2), prev=gate, deps=[near_d[5]])
    finish(3, far_u, pass_u, [gate])
    up_act = _ffn_up_act("ffn_up_act_a", h2, full["w_ffn_up"], gate, part=(0, 2))
    far_d, pass_d = far_and_pass(4, near_d, [up_act[0]])
    up, act = _ffn_up_act("ffn_up_act_b", h2, full["w_ffn_up"], gate, part=(1, 2), prev=up_act, deps=[pass_d[5]])
    finish(4, far_d, pass_d, [act])
    x3 = _mm_nn("ffn_down", act, full["w_ffn_down"], F32, add=x2, tiles=(512, 512))

    pending = {}

    def pair_start(g, grads):
        grp = rgroups[g]
        gcan = [grads[w.name].reshape(w.P, 2, w.R, w.N) for w in grp]
        slots = [lax.empty((w.P, w.R, w.N), BF16) for w in grp]
        pending[g] = start(f"pair_start_{g}", gcan + slots, _pair_copies(len(grp)))
        return pending[g][5]

    def scatter_start(g, after):
        grp = rgroups[g]
        n = len(grp)
        arrs = wait(pending[g], after)
        gcan, sib = arrs[:n], arrs[n:]
        pairs = [_pair_sum(f"pair_sum_{w.name}", w, pos, a, s) for w, a, s in zip(grp, gcan, sib)]
        ssem, rsem, pairs, slots, token = _scatter_start(f"scatter_start_{g}", grp, pairs)
        pending[g] = (gcan, sib, pairs, slots, ssem, rsem)
        return token

    def pair_start_halves(g, ab, deps):
        grp = rgroups[g]
        sent = [_mm_tn_half(f"d{w.name}_sib", a, b, pos, False, deps=deps if i == 0 else ()) for i, (w, (a, b)) in enumerate(zip(grp, ab))]
        slots = [lax.empty((1, w.R, w.N), BF16) for w in grp]
        pending[g] = start(f"pair_start_{g}", sent + slots, _pair_copies(len(grp), whole=True))
        return pending[g][5]

    def scatter_start_halves(g, ab, after):
        grp = rgroups[g]
        n = len(grp)
        arrs = wait(pending[g], after)
        pairs = [_mm_tn_half(f"d{w.name}_own", a, b, pos, True, add=s) for w, (a, b), s in zip(grp, ab, arrs[n:])]
        ssem, rsem, pairs, slots, token = _scatter_start(f"scatter_start_{g}", grp, pairs)
        pending[g] = (None, None, pairs, slots, ssem, rsem)
        return token

    def reduce_finish(g, after):
        grp = rgroups[g]
        gcan, sib, pairs, slots, ssem, rsem = pending[g]
        pairs, parts = _scatter_wait(f"scatter_wait_{g}", grp, pairs, slots, ssem, rsem, after)
        if gcan is None:
            return [_final_sum(f"final_sum_{w.name}", w, pos, a, None, q) for w, a, q in zip(grp, pairs, parts)]
        return [_final_sum(f"final_sum_{w.name}", w, pos, a, s, q) for w, a, s, q in zip(grp, gcan, sib, parts)]

    grads = {}
    dx3, dx3b, d_gf, loss_cols = _final_bwd("final_bwd", x3, final_g.reshape(1, d), tgt)
    dgate, dup = _ffn_bwd("ffn_bwd", dx3b, full["w_ffn_down"], gate, up)
    grads["w_ffn_down"] = _mm_tn("dw_ffn_down", act, dx3b, BF16)
    tok = pair_start(3, grads)
    dh2 = _mm_nt("d_h2", [(dgate, full["w_ffn_gate"]), (dup, full["w_ffn_up"])], BF16, tk=d_ff // 4, deps=[tok])
    tok = scatter_start(3, [dh2])
    tok = pair_start_halves(2, [(h2, dgate), (h2, dup)], [tok])
    dx2, dx2b, d_g2 = _rms_bwd("norm2_bwd", x2, norm2_g, dh2, dx3, True, deps=[tok])
    dmerged = _mm_nt("d_merged", [(dx2b, full["w_o"])], BF16, tk=d)
    grads["w_o"] = _mm_tn("dw_o", merged, dx2b, BF16)
    tok = scatter_start_halves(2, [(h2, dgate), (h2, dup)], [grads["w_o"]])
    dya, dyb, dproj, d_bga, d_bgb, d_ps = _merge_bwd("merge_bwd", dmerged, proj, b_gate, ya, yb, pool_scale, deps=[tok])
    dz = _mm_nt("d_z", [(dya, full["w_a_out"])], BF16, tk=d)
    grads["w_a_out"] = _mm_tn("dw_a_out", z, dya, BF16)
    dp = _gmm_nt("d_pool", dyb, wp_full, BF16)
    grads["w_pool"] = _gmm_tn("dw_pool", p, dyb, n_groups, BF16)
    tok = pair_start(1, grads)
    dproj, d_cw, d_cb = _mixer_bwd("mixer_bwd", dz, dp, proj, cw_full, conv_b, dproj, n_conv, n_groups, deps=[tok])
    tok = scatter_start(1, [dproj])
    tok = pair_start_halves(0, [(h1, dproj)], [tok])
    dh1 = _mm_nt("d_h1", [(dproj, w_in_full)], BF16, tk=proj.shape[1] // 4, deps=[tok])
    tok = scatter_start_halves(0, [(h1, dproj)], [dh1])
    grad_x, d_g1 = _rms_bwd("norm1_bwd", xs, norm1_g, dh1, dx2, False, deps=[tok])

    g_big, d_big, m_big, v_big = {}, {}, {}, {}

    def update(wsub, shared):
        out = []
        for w, g in zip(wsub, shared):
            wt, mt, vt = big[w.name]
            g2 = g.reshape(2 * w.R, w.nn)
            go, dl, nm, nv = _adamw(f"adamw_{w.name}", shard2d[w.name], g2, mt.reshape(g2.shape), vt.reshape(g2.shape))
            g_big[w.name], d_big[w.name], m_big[w.name], v_big[w.name] = (a.reshape(wt.shape) for a in (go, dl, nm, nv))
            out.append(nv)
        return out

    after = [grad_x]
    started = []
    for g in (3, 2, 1):
        halves = reduce_finish(g, after)
        started.append((g, start(f"share_{g}", halves, _share_copies(len(halves)))))
        after = [started[-1][1][5]]
    for g, st in started:
        after = update(rgroups[g], wait(st, after))
    st = start("share_0", reduce_finish(0, after), _share_copies(1))

    vec_names = ["norm1_g", "b_gate", "conv_w", "conv_b", "pool_scale", "norm2_g", "final_g"]
    vec = {"norm1_g": (norm1_g, m_norm1_g, v_norm1_g), "b_gate": (b_gate, m_b_gate, v_b_gate),
           "conv_w": (cw_loc, m_conv_w[0], v_conv_w[0]), "conv_b": (conv_b, m_conv_b, v_conv_b),
           "pool_scale": (pool_scale, m_pool_scale, v_pool_scale), "norm2_g": (norm2_g, m_norm2_g, v_norm2_g),
           "final_g": tuple(a.reshape(1, d) for a in (final_g, m_final_g, v_final_g))}
    tot = _vector_sum(d, n_conv, [d_g1, d_bga, d_bgb, d_cw, d_cb, d_ps, d_g2, d_gf, loss_cols], deps=st[1])
    vout = _vector_update(tot, n_conv, [vec[n] for n in vec_names])
    update(rgroups[0], wait(st, []))

    shapes = {"conv_w": conv_w.shape, "final_g": final_g.shape}
    g_vec, d_vec, m_vec, v_vec = ({n: vout[4 * i + q].reshape(shapes.get(n, vec[n][0].shape)) for i, n in enumerate(vec_names)}
                                  for q in range(4))
    loss = vout[-1].reshape(())

    order = ["norm1_g", "w_in", "b_gate", "conv_w", "conv_b", "w_a_out", "w_pool", "pool_scale", "w_o", "norm2_g",
             "w_ffn_gate", "w_ffn_up", "w_ffn_down", "final_g"]
    pick = lambda vecs, bigs: [vecs[n] if n in vecs else bigs[n] for n in order]
    return (loss, grad_x.reshape(x.shape), *pick(g_vec, g_big), *pick(d_vec, d_big), *pick(m_vec, m_big),
            *pick(v_vec, v_big))
```

```python
import jax
import jax.numpy as jnp
from jax import lax
from jax.experimental import pallas as pl
from jax.experimental.pallas import tpu as pltpu

F32, BF16 = jnp.float32, jnp.bfloat16
MESH = pl.DeviceIdType.MESH
ANY = pl.BlockSpec(memory_space=pl.ANY)
VMEM = pl.BlockSpec(memory_space=pltpu.VMEM)
HBM = pl.BlockSpec(memory_space=pltpu.HBM)
SEM = pl.BlockSpec(memory_space=pltpu.SEMAPHORE)
EFFECT = pltpu.SideEffectType.DATAFLOW_SIDE_EFFECTING

EPS = 1e-6
POOL_WINDOWS = (2, 4, 8, 16)
ADAM_LR, ADAM_B1, ADAM_B2, ADAM_EPS, ADAM_WD, ADAM_STEP = 0.001, 0.9, 0.999, 1e-08, 0.01, 10

V7X_VMEM_BYTES = 64 * 1024 * 1024
VMEM_LIMIT = V7X_VMEM_BYTES * 3 // 4
LANES = 128
COL_TILE = 8 * LANES
N_CHIPS = 4
N_DEV = 8

_DIMS = {
    "nn": (((1,), (0,)), ((), ())),
    "nt": (((1,), (1,)), ((), ())),
    "tn": (((0,), (0,)), ((), ())),
}


def _cp(sem):
    return pltpu.CompilerParams(dimension_semantics=sem, vmem_limit_bytes=VMEM_LIMIT)


def _mesh_pos():
    return lax.axis_index("x"), lax.axis_index("y"), lax.axis_index("c")


def _mm(name, pairs, *, mode, grid, out_shape, o_spec, nk=1, kaxis=None, add=None, deps=(), prev=None):
    npair = len(pairs)
    has_add = add is not None

    def body(*refs):
        ab = refs[: 2 * npair]
        pos = 2 * npair
        add_ref = refs[pos] if has_add else None
        pos += int(has_add) + len(deps) + (prev is not None)
        o_ref = refs[pos]
        acc_ref = refs[pos + 1] if nk > 1 else None
        d = None
        for p in range(npair):
            t = lax.dot_general(ab[2 * p][...], ab[2 * p + 1][...], _DIMS[mode], preferred_element_type=F32)
            d = t if d is None else d + t
        if nk == 1:
            if has_add:
                d = d + add_ref[...].astype(F32)
            o_ref[...] = d.astype(o_ref.dtype)
        else:
            k = pl.program_id(kaxis)

            @pl.when(k == 0)
            def _():
                acc_ref[...] = d

            @pl.when(k > 0)
            def _():
                acc_ref[...] += d

            @pl.when(k == nk - 1)
            def _():
                r = acc_ref[...]
                if has_add:
                    r = r + add_ref[...].astype(F32)
                o_ref[...] = r.astype(o_ref.dtype)

    args, specs = [], []
    for a, a_spec, b, b_spec in pairs:
        args += [a, b]
        specs += [a_spec, b_spec]
    if has_add:
        args.append(add[0])
        specs.append(add[1])
    args += list(deps)
    specs += [ANY] * len(deps)
    aliases = {}
    if prev is not None:
        aliases = {len(args): 0}
        args.append(prev)
        specs.append(ANY)
    scratch = []
    if nk > 1:
        blk = [d for d in o_spec.block_shape if d is not None]
        scratch = [pltpu.VMEM(tuple(blk), F32)]
    sem = tuple("arbitrary" if (nk > 1 and ax == kaxis) else "parallel" for ax in range(len(grid)))
    return pl.pallas_call(
        body, name=name, grid=grid, in_specs=specs, out_specs=o_spec, out_shape=out_shape,
        scratch_shapes=scratch, input_output_aliases=aliases, compiler_params=_cp(sem),
    )(*args)


def _tile_span(n_tiles, part):
    if part is None:
        return 0, n_tiles
    p, of = part
    return p * n_tiles // of, (p + 1) * n_tiles // of


def _tile(n, pref):
    if n <= pref:
        return n
    for t in range(pref, 0, -LANES):
        if t % LANES == 0 and n % t == 0:
            return t
    raise ValueError(f"no tile for {n}")


def _mm_nn(name, a, b, out_dtype, add=None, tk=None, deps=(), part=None, prev=None, tiles=None):
    m, kk = a.shape
    n = b.shape[1]
    tm, tn = _tile(m, 1024), _tile(n, COL_TILE)
    if tiles is not None:
        tm, tn = _tile(m, tiles[0]), _tile(n, tiles[1])
    out_shape = jax.ShapeDtypeStruct((m, n), out_dtype)
    if tk is None or tk == kk:
        j0, j1 = _tile_span(n // tn, part)
        grid = (m // tm, j1 - j0)
        pairs = [(a, pl.BlockSpec((tm, kk), lambda i, j: (i, 0)), b, pl.BlockSpec((kk, tn), lambda i, j: (0, j0 + j)))]
        o_spec = pl.BlockSpec((tm, tn), lambda i, j: (i, j0 + j))
        add_ = None if add is None else (add, pl.BlockSpec((tm, tn), lambda i, j: (i, j0 + j)))
        return _mm(name, pairs, mode="nn", grid=grid, out_shape=out_shape, o_spec=o_spec, add=add_, deps=deps, prev=prev)
    tn = _tile(n, 1024)
    nk = kk // tk
    grid = (m // tm, n // tn, nk)
    pairs = [(a, pl.BlockSpec((tm, tk), lambda i, j, k: (i, k)), b, pl.BlockSpec((tk, tn), lambda i, j, k: (k, j)))]
    o_spec = pl.BlockSpec((tm, tn), lambda i, j, k: (i, j))
    add_ = None if add is None else (add, pl.BlockSpec((tm, tn), lambda i, j, k: (i, j)))
    return _mm(name, pairs, mode="nn", grid=grid, out_shape=out_shape, o_spec=o_spec, nk=nk, kaxis=2, add=add_, deps=deps)


def _mm_nt(name, abs_, out_dtype, tk, deps=()):
    m, kk = abs_[0][0].shape
    n = abs_[0][1].shape[0]
    tm = _tile(m, 1024)
    nk = kk // tk
    tn = _tile(n, COL_TILE if nk == 1 else 1024)
    out_shape = jax.ShapeDtypeStruct((m, n), out_dtype)
    if nk == 1:
        grid = (m // tm, n // tn)
        pairs = [(a, pl.BlockSpec((tm, kk), lambda i, j: (i, 0)), b, pl.BlockSpec((tn, kk), lambda i, j: (j, 0)))
                 for a, b in abs_]
        o_spec = pl.BlockSpec((tm, tn), lambda i, j: (i, j))
        return _mm(name, pairs, mode="nt", grid=grid, out_shape=out_shape, o_spec=o_spec, deps=deps)
    grid = (m // tm, n // tn, nk)
    pairs = [(a, pl.BlockSpec((tm, tk), lambda i, j, k: (i, k)), b, pl.BlockSpec((tn, tk), lambda i, j, k: (j, k)))
             for a, b in abs_]
    o_spec = pl.BlockSpec((tm, tn), lambda i, j, k: (i, j))
    return _mm(name, pairs, mode="nt", grid=grid, out_shape=out_shape, o_spec=o_spec, nk=nk, kaxis=2, deps=deps)


def _mm_tn(name, a, b, out_dtype, deps=()):
    t, m = a.shape
    n = b.shape[1]
    tm, tn = _tile(m, 512), _tile(n, 2048)
    if n > m:
        grid = (n // tn, m // tm)
        a_map, b_map, o_map = (lambda j, i: (0, i)), (lambda j, i: (0, j)), (lambda j, i: (i, j))
    else:
        grid = (m // tm, n // tn)
        a_map, b_map, o_map = (lambda i, j: (0, i)), (lambda i, j: (0, j)), (lambda i, j: (i, j))
    pairs = [(a, pl.BlockSpec((t, tm), a_map), b, pl.BlockSpec((t, tn), b_map))]
    o_spec = pl.BlockSpec((tm, tn), o_map)
    return _mm(name, pairs, mode="tn", grid=grid, out_shape=jax.ShapeDtypeStruct((m, n), out_dtype), o_spec=o_spec,
               deps=deps)


def _mm_tn_half(name, a, b, pos, mine, add=None, deps=()):
    t, m = a.shape
    r, n = m // 2, b.shape[1]
    tm, tn = _tile(r, 512), _tile(n, 2048)
    nbi = r // tm
    half = (lambda pos: pos[0]) if mine else (lambda pos: 1 - pos[0])
    if n > r:
        grid, ij = (n // tn, nbi), (lambda g0, g1: (g1, g0))
    else:
        grid, ij = (nbi, n // tn), (lambda g0, g1: (g0, g1))
    has_add = add is not None

    def body(pos_ref, a_ref, b_ref, *rest):
        d = lax.dot_general(a_ref[...], b_ref[...], _DIMS["tn"], preferred_element_type=F32)
        if has_add:
            d = d + rest[0][...].astype(F32)
        rest[-1][...] = d.astype(BF16)

    o_spec = pl.BlockSpec((None, tm, tn), lambda g0, g1, pos: (0, *ij(g0, g1)))
    grid_spec = pltpu.PrefetchScalarGridSpec(
        num_scalar_prefetch=1, grid=grid,
        in_specs=[pl.BlockSpec((t, tm), lambda g0, g1, pos: (0, half(pos) * nbi + ij(g0, g1)[0])),
                  pl.BlockSpec((t, tn), lambda g0, g1, pos: (0, ij(g0, g1)[1]))]
        + ([o_spec] if has_add else []) + [ANY] * len(deps),
        out_specs=o_spec)
    return pl.pallas_call(body, name=name, grid_spec=grid_spec, out_shape=jax.ShapeDtypeStruct((1, r, n), BF16),
                          compiler_params=_cp(("parallel",) * 2))(pos, a, b, *([add] if has_add else []), *deps)


def _proj_piece(name, h, w, prev, kvec, base, count, deps=()):
    t, kk = h.shape
    own = w.dtype == F32
    nn = w.shape[1] if own else w.shape[1] // N_CHIPS
    tm, tn = _tile(t, 1024), _tile(nn, COL_TILE)
    nb = nn // tn

    def body(kv_ref, h_ref, w_ref, *rest):
        rest[-1][...] = lax.dot_general(h_ref[...], w_ref[...].astype(BF16), _DIMS["nn"],
                                        preferred_element_type=F32).astype(BF16)

    cols = lambda s, i, j, kv: (0, j) if own else (0, kv[base + s] * nb + j)
    extra = ([] if prev is None else [prev]) + list(deps)
    grid_spec = pltpu.PrefetchScalarGridSpec(
        num_scalar_prefetch=1, grid=(count, t // tm, nb),
        in_specs=[pl.BlockSpec((tm, kk), lambda s, i, j, kv: (i, 0)), pl.BlockSpec((kk, tn), cols)] + [ANY] * len(extra),
        out_specs=pl.BlockSpec((tm, tn), lambda s, i, j, kv: (i, kv[base + s] * nb + j)))
    return pl.pallas_call(body, name=name, grid_spec=grid_spec, out_shape=jax.ShapeDtypeStruct((t, N_CHIPS * nn), BF16),
                          input_output_aliases={} if prev is None else {3: 0},
                          compiler_params=_cp(("parallel",) * 3))(kvec, h, w, *extra)


def _gmm_nn(name, p, w, out_dtype):
    t = p.shape[0]
    g, cg, dg = w.shape
    tm = _tile(t, 1024)
    pairs = [(p, pl.BlockSpec((tm, cg), lambda i, j: (i, j)), w, pl.BlockSpec((None, cg, dg), lambda i, j: (j, 0, 0)))]
    o_spec = pl.BlockSpec((tm, dg), lambda i, j: (i, j))
    return _mm(name, pairs, mode="nn", grid=(t // tm, g), out_shape=jax.ShapeDtypeStruct((t, g * dg), out_dtype),
               o_spec=o_spec)


def _gmm_nt(name, dy, w, out_dtype):
    t = dy.shape[0]
    g, cg, dg = w.shape
    tm = _tile(t, 1024)
    pairs = [(dy, pl.BlockSpec((tm, dg), lambda i, j: (i, j)), w, pl.BlockSpec((None, cg, dg), lambda i, j: (j, 0, 0)))]
    o_spec = pl.BlockSpec((tm, cg), lambda i, j: (i, j))
    return _mm(name, pairs, mode="nt", grid=(t // tm, g), out_shape=jax.ShapeDtypeStruct((t, g * cg), out_dtype),
               o_spec=o_spec)


def _gmm_tn(name, p, dy, g, out_dtype):
    t = p.shape[0]
    cg, dg = p.shape[1] // g, dy.shape[1] // g
    pairs = [(p, pl.BlockSpec((t, cg), lambda j: (0, j)), dy, pl.BlockSpec((t, dg), lambda j: (0, j)))]
    o_spec = pl.BlockSpec((None, cg, dg), lambda j: (j, 0, 0))
    return _mm(name, pairs, mode="tn", grid=(g,), out_shape=jax.ShapeDtypeStruct((g, cg, dg), out_dtype), o_spec=o_spec)


ROW_TILE = 256


def _rows(t):
    return _tile8(t, ROW_TILE)


def _tile8(n, pref):
    if n <= pref:
        return n
    for t in range(pref, 0, -8):
        if n % t == 0:
            return t
    raise ValueError(f"no row tile for {n}")


def _cast_place(name, w, pos, shard, deps=()):
    tr = _tile8(w.R, 512)
    if w.colshard:
        o_map = lambda h, i, pos: (0, h, i, pos[1])
    else:
        o_map = lambda h, i, pos: (pos[1], h, i, 0)

    def body(pos_ref, w_ref, *rest):
        rest[-1][...] = w_ref[...].astype(BF16)

    grid_spec = pltpu.PrefetchScalarGridSpec(
        num_scalar_prefetch=1, grid=(2, w.R // tr),
        in_specs=[pl.BlockSpec((None, tr, w.nn), lambda h, i, pos: (h, i, 0))] + [ANY] * len(deps),
        out_specs=pl.BlockSpec((None, None, tr, w.nn), o_map))
    return pl.pallas_call(body, name=name, grid_spec=grid_spec, out_shape=jax.ShapeDtypeStruct((w.P, 2, w.R, w.N), BF16),
                          compiler_params=_cp(("parallel", "parallel")))(pos, shard, *deps)


def _rms_fwd(name, x, g, deps=()):
    t, d = x.shape
    tm = _rows(t)

    def body(x_ref, g_ref, *rest):
        xf = x_ref[...]
        r = lax.rsqrt(jnp.mean(xf * xf, axis=-1, keepdims=True) + EPS)
        rest[-1][...] = (xf * r * g_ref[...]).astype(BF16)

    return pl.pallas_call(
        body, name=name, grid=(t // tm,),
        in_specs=[pl.BlockSpec((tm, d), lambda i: (i, 0)), pl.BlockSpec((1, d), lambda i: (0, 0))] + [ANY] * len(deps),
        out_specs=pl.BlockSpec((tm, d), lambda i: (i, 0)), out_shape=jax.ShapeDtypeStruct((t, d), BF16),
        compiler_params=_cp(("parallel",)),
    )(x, g, *deps)


def _rms_bwd(name, x, g, dh, dres, want_bf16, deps=()):
    t, d = x.shape
    tm = _rows(t)

    def body(x_ref, g_ref, dh_ref, dres_ref, *rest):
        rest = rest[len(deps):]
        dx_ref, rest = rest[0], rest[1:]
        dg_ref = rest[-1]
        xf = x_ref[...]
        r = lax.rsqrt(jnp.mean(xf * xf, axis=-1, keepdims=True) + EPS)
        xh = xf * r
        dhf = dh_ref[...].astype(F32)
        dxh = dhf * g_ref[...]
        m = jnp.mean(dxh * xh, axis=-1, keepdims=True)
        dx = dres_ref[...] + r * (dxh - xh * m)
        dx_ref[...] = dx
        if want_bf16:
            rest[0][...] = dx.astype(BF16)

        @pl.when(pl.program_id(0) == 0)
        def _():
            dg_ref[...] = jnp.zeros_like(dg_ref)

        dg_ref[...] += jnp.sum(dhf * xh, axis=0, keepdims=True)

    row = pl.BlockSpec((tm, d), lambda i: (i, 0))
    vec = pl.BlockSpec((1, d), lambda i: (0, 0))
    out_specs = [row] + ([row] if want_bf16 else []) + [vec]
    out_shape = ([jax.ShapeDtypeStruct((t, d), F32)] + ([jax.ShapeDtypeStruct((t, d), BF16)] if want_bf16 else [])
                 + [jax.ShapeDtypeStruct((1, d), F32)])
    return pl.pallas_call(body, name=name, grid=(t // tm,), in_specs=[row, vec, row, row] + [ANY] * len(deps),
                          out_specs=out_specs, out_shape=out_shape, compiler_params=_cp(("arbitrary",)))(x, g, dh, dres, *deps)


def _final_bwd(name, x3, gf, tgt):
    t, d = x3.shape
    tm = _rows(t)

    def body(x_ref, g_ref, t_ref, dx_ref, dxb_ref, dg_ref, lc_ref):
        xf = x_ref[...]
        g = g_ref[...]
        r = lax.rsqrt(jnp.mean(xf * xf, axis=-1, keepdims=True) + EPS)
        xh = xf * r
        diff = xh * g - t_ref[...]
        dy = diff * (1.0 / d)
        dxh = dy * g
        m = jnp.mean(dxh * xh, axis=-1, keepdims=True)
        dx = r * (dxh - xh * m)
        dx_ref[...] = dx
        dxb_ref[...] = dx.astype(BF16)

        @pl.when(pl.program_id(0) == 0)
        def _():
            dg_ref[...] = jnp.zeros_like(dg_ref)
            lc_ref[...] = jnp.zeros_like(lc_ref)

        dg_ref[...] += jnp.sum(dy * xh, axis=0, keepdims=True)
        lc_ref[...] += jnp.sum(diff * diff, axis=0, keepdims=True) * (0.5 / d)

    row = pl.BlockSpec((tm, d), lambda i: (i, 0))
    vec = pl.BlockSpec((1, d), lambda i: (0, 0))
    return pl.pallas_call(
        body, name=name, grid=(t // tm,), in_specs=[row, vec, row], out_specs=[row, row, vec, vec],
        out_shape=[jax.ShapeDtypeStruct((t, d), F32), jax.ShapeDtypeStruct((t, d), BF16),
                   jax.ShapeDtypeStruct((1, d), F32), jax.ShapeDtypeStruct((1, d), F32)],
        compiler_params=_cp(("arbitrary",)),
    )(x3, gf, tgt)


def _shift_down(v, k, t_idx):
    return jnp.where(t_idx >= k, pltpu.roll(v, k, 0), 0.0)


def _shift_up(v, k, t_idx):
    n = v.shape[0]
    return jnp.where(t_idx < n - k, pltpu.roll(v, n - k, 0), 0.0)


def _window_sums(v, shift, t_idx, grp):
    s = v + shift(v, 1, t_idx)
    out = s
    for lvl in range(1, len(POOL_WINDOWS)):
        s = s + shift(s, 1 << lvl, t_idx)
        out = jnp.where(grp >= lvl, s, out)
    return out


def _window_weight(t_idx, grp):
    return 1.0 / jnp.minimum(t_idx[:, :1] + 1, jnp.left_shift(2, grp)).astype(F32)


MIX_COLS = 256


def _mixer_fwd(name, proj, cw, cb, n_conv, n_groups, deps=()):
    t = proj.shape[0]
    nb = n_conv // MIX_COLS
    per_group = n_conv // n_groups // MIX_COLS

    def body(ba_ref, ca_ref, va_ref, vb_ref, cw_ref, cb_ref, *rest):
        z_ref, p_ref = rest[len(deps):]
        t_idx = lax.broadcasted_iota(jnp.int32, (t, MIX_COLS), 0)
        q = ca_ref[...].astype(F32) * va_ref[...].astype(F32)
        w = cw_ref[...]
        u = cb_ref[...] + w[0:1] * _shift_down(q, 2, t_idx) + w[1:2] * _shift_down(q, 1, t_idx) + w[2:3] * q
        z_ref[...] = (ba_ref[...].astype(F32) * u).astype(BF16)
        grp = pl.program_id(0) // per_group
        v = vb_ref[...].astype(F32)
        p_ref[...] = (_window_sums(v, _shift_down, t_idx, grp) * _window_weight(t_idx, grp) - v).astype(BF16)

    col = lambda s: pl.BlockSpec((t, MIX_COLS), lambda j: (0, s * nb + j))
    return pl.pallas_call(
        body, name=name, grid=(nb,),
        in_specs=[col(0), col(1), col(2), col(3), pl.BlockSpec((3, MIX_COLS), lambda j: (0, j)),
                  pl.BlockSpec((1, MIX_COLS), lambda j: (0, j))] + [ANY] * len(deps),
        out_specs=[col(0), col(0)],
        out_shape=[jax.ShapeDtypeStruct((t, n_conv), BF16), jax.ShapeDtypeStruct((t, n_conv), BF16)],
        compiler_params=_cp(("parallel",)),
    )(proj, proj, proj, proj, cw, cb, *deps)


def _mixer_bwd(name, dz, dp, proj, cw, cb, dproj, n_conv, n_groups, deps=()):
    t = proj.shape[0]
    nb = n_conv // MIX_COLS
    per_group = n_conv // n_groups // MIX_COLS

    def body(dz_ref, dp_ref, ba_ref, ca_ref, va_ref, cw_ref, cb_ref, _, *rest):
        o_ref, dcw_ref, dcb_ref, scr = rest[len(deps):]
        s = pl.program_id(1)

        @pl.when(s == 0)
        def _():
            t_idx = lax.broadcasted_iota(jnp.int32, (t, MIX_COLS), 0)
            ca, va = ca_ref[...].astype(F32), va_ref[...].astype(F32)
            q = ca * va
            q1, q2 = _shift_down(q, 1, t_idx), _shift_down(q, 2, t_idx)
            w = cw_ref[...]
            u = cb_ref[...] + w[0:1] * q2 + w[1:2] * q1 + w[2:3] * q
            dzf = dz_ref[...].astype(F32)
            du = dzf * ba_ref[...].astype(F32)
            scr[0] = (dzf * u).astype(BF16)
            dq = w[2:3] * du + w[1:2] * _shift_up(du, 1, t_idx) + w[0:1] * _shift_up(du, 2, t_idx)
            scr[1] = (dq * va).astype(BF16)
            scr[2] = (dq * ca).astype(BF16)
            dcb_ref[...] = jnp.sum(du, axis=0, keepdims=True)
            dcw_ref[0:1, :] = jnp.sum(du * q2, axis=0, keepdims=True)
            dcw_ref[1:2, :] = jnp.sum(du * q1, axis=0, keepdims=True)
            dcw_ref[2:3, :] = jnp.sum(du * q, axis=0, keepdims=True)
            grp = pl.program_id(0) // per_group
            dpf = dp_ref[...].astype(F32)
            e = dpf * _window_weight(t_idx, grp)
            scr[3] = (_window_sums(e, _shift_up, t_idx, grp) - dpf).astype(BF16)

        o_ref[...] = scr[s]

    col = lambda c: pl.BlockSpec((t, MIX_COLS), lambda j, s: (0, c * nb + j))
    own = pl.BlockSpec((t, MIX_COLS), lambda j, s: (0, j))
    return pl.pallas_call(
        body, name=name, grid=(nb, 4),
        in_specs=[own, own, col(0), col(1), col(2), pl.BlockSpec((3, MIX_COLS), lambda j, s: (0, j)),
                  pl.BlockSpec((1, MIX_COLS), lambda j, s: (0, j)), ANY] + [ANY] * len(deps),
        out_specs=[pl.BlockSpec((t, MIX_COLS), lambda j, s: (0, s * nb + j)),
                   pl.BlockSpec((3, MIX_COLS), lambda j, s: (0, j)), pl.BlockSpec((1, MIX_COLS), lambda j, s: (0, j))],
        out_shape=[jax.ShapeDtypeStruct(dproj.shape, BF16), jax.ShapeDtypeStruct((3, n_conv), F32),
                   jax.ShapeDtypeStruct((1, n_conv), F32)],
        scratch_shapes=[pltpu.VMEM((4, t, MIX_COLS), BF16)],
        input_output_aliases={7: 0},
        compiler_params=_cp(("arbitrary", "arbitrary")),
    )(dz, dp, proj, proj, proj, cw, cb, dproj, *deps)


def _merge_fwd(name, proj, bg, ya, yb, ps):
    t, d = ya.shape
    tm = _rows(t)

    def body(gab_ref, bg_ref, ya_ref, yb_ref, ps_ref, o_ref):
        gab = gab_ref[...].astype(F32) + bg_ref[...]
        sa, sb = jax.nn.sigmoid(gab[:, :d]), jax.nn.sigmoid(gab[:, d:])
        o_ref[...] = (sa * ya_ref[...].astype(F32) + sb * (yb_ref[...].astype(F32) * ps_ref[...])).astype(BF16)

    row = pl.BlockSpec((tm, d), lambda i: (i, 0))
    return pl.pallas_call(
        body, name=name, grid=(t // tm,),
        in_specs=[pl.BlockSpec((tm, 2 * d), lambda i: (i, 1)), pl.BlockSpec((1, 2 * d), lambda i: (0, 0)), row, row,
                  pl.BlockSpec((1, d), lambda i: (0, 0))],
        out_specs=row, out_shape=jax.ShapeDtypeStruct((t, d), BF16), compiler_params=_cp(("parallel",)),
    )(proj, bg, ya, yb, ps)


def _merge_bwd(name, dm, proj, bg, ya, yb, ps, deps=()):
    t, d = ya.shape
    tm = _rows(t)

    def body(dm_ref, gab_ref, bg_ref, ya_ref, yb_ref, ps_ref, *rest):
        dya_ref, dyb_ref, dg_ref, dba_ref, dbb_ref, dps_ref = rest[len(deps):]
        gab = gab_ref[...].astype(F32) + bg_ref[...]
        sa, sb = jax.nn.sigmoid(gab[:, :d]), jax.nn.sigmoid(gab[:, d:])
        dmf = dm_ref[...].astype(F32)
        ybf, ps_ = yb_ref[...].astype(F32), ps_ref[...]
        dya_ref[...] = (dmf * sa).astype(BF16)
        dyb = dmf * sb
        dyb_ref[...] = (dyb * ps_).astype(BF16)
        dga = dmf * ya_ref[...].astype(F32) * sa * (1.0 - sa)
        dgb = dmf * (ybf * ps_) * sb * (1.0 - sb)
        dg_ref[:, :d] = dga.astype(BF16)
        dg_ref[:, d:] = dgb.astype(BF16)

        @pl.when(pl.program_id(0) == 0)
        def _():
            dba_ref[...] = jnp.zeros_like(dba_ref)
            dbb_ref[...] = jnp.zeros_like(dbb_ref)
            dps_ref[...] = jnp.zeros_like(dps_ref)

        dba_ref[...] += jnp.sum(dga, axis=0, keepdims=True)
        dbb_ref[...] += jnp.sum(dgb, axis=0, keepdims=True)
        dps_ref[...] += jnp.sum(dyb * ybf, axis=0, keepdims=True)

    row = pl.BlockSpec((tm, d), lambda i: (i, 0))
    vec = pl.BlockSpec((1, d), lambda i: (0, 0))
    gates = pl.BlockSpec((tm, 2 * d), lambda i: (i, 1))
    return pl.pallas_call(
        body, name=name, grid=(t // tm,),
        in_specs=[row, gates, pl.BlockSpec((1, 2 * d), lambda i: (0, 0)), row, row, vec] + [ANY] * len(deps),
        out_specs=[row, row, gates, vec, vec, vec],
        out_shape=[jax.ShapeDtypeStruct((t, d), BF16), jax.ShapeDtypeStruct((t, d), BF16),
                   jax.ShapeDtypeStruct(proj.shape, BF16), jax.ShapeDtypeStruct((1, d), F32),
                   jax.ShapeDtypeStruct((1, d), F32), jax.ShapeDtypeStruct((1, d), F32)],
        compiler_params=_cp(("arbitrary",)),
    )(dm, proj, bg, ya, yb, ps, *deps)


def _ffn_up_act(name, h, w_up, gate, part=None, prev=None, deps=()):
    t, d = h.shape
    f = w_up.shape[1]
    tm, tf = _tile(t, 1024), _tile(f, 512)
    j0, j1 = _tile_span(f // tf, part)
    n_prev = 0 if prev is None else 2
    extra = ([] if prev is None else list(prev)) + list(deps)

    def body(h_ref, w_ref, g_ref, *rest):
        u_ref, a_ref = rest[len(extra):]
        u = lax.dot_general(h_ref[...], w_ref[...], _DIMS["nn"], preferred_element_type=F32)
        g = g_ref[...].astype(F32)
        u_ref[...] = u.astype(BF16)
        a_ref[...] = (g * jax.nn.sigmoid(g) * u).astype(BF16)

    blk = pl.BlockSpec((tm, tf), lambda i, j: (i, j0 + j))
    shp = jax.ShapeDtypeStruct((t, f), BF16)
    return pl.pallas_call(
        body, name=name, grid=(t // tm, j1 - j0),
        in_specs=[pl.BlockSpec((tm, d), lambda i, j: (i, 0)), pl.BlockSpec((d, tf), lambda i, j: (0, j0 + j)), blk]
        + [ANY] * len(extra),
        out_specs=[blk, blk], out_shape=[shp, shp], input_output_aliases={3 + i: i for i in range(n_prev)},
        compiler_params=_cp(("parallel", "parallel")))(h, w_up, gate, *extra)


def _ffn_bwd(name, dy, w_down, gate, up):
    t, d = dy.shape
    f = w_down.shape[0]
    tm, tf = _tile(t, 1024), _tile(f, 512)

    def body(dy_ref, w_ref, g_ref, u_ref, dg_ref, du_ref):
        da = lax.dot_general(dy_ref[...], w_ref[...], _DIMS["nt"], preferred_element_type=F32)
        g = g_ref[...].astype(F32)
        s = jax.nn.sigmoid(g)
        du_ref[...] = (da * (g * s)).astype(BF16)
        dg_ref[...] = (da * u_ref[...].astype(F32) * (s * (1.0 + g * (1.0 - s)))).astype(BF16)

    blk = pl.BlockSpec((tm, tf), lambda i, j: (i, j))
    shp = jax.ShapeDtypeStruct((t, f), BF16)
    return pl.pallas_call(
        body, name=name, grid=(t // tm, f // tf),
        in_specs=[pl.BlockSpec((tm, d), lambda i, j: (i, 0)), pl.BlockSpec((tf, d), lambda i, j: (j, 0)), blk, blk],
        out_specs=[blk, blk], out_shape=[shp, shp], compiler_params=_cp(("parallel", "parallel")))(dy, w_down, gate, up)


def _adamw_math(w, g, m, v):
    m = ADAM_B1 * m + (1.0 - ADAM_B1) * g
    v = ADAM_B2 * v + (1.0 - ADAM_B2) * (g * g)
    m_hat = m / (1.0 - ADAM_B1 ** ADAM_STEP)
    v_hat = v / (1.0 - ADAM_B2 ** ADAM_STEP)
    delta = -ADAM_LR * (m_hat / (jnp.sqrt(v_hat) + ADAM_EPS) + ADAM_WD * w)
    return delta, m, v


def _adamw(name, w, g, m, v):
    r, c = w.shape
    tr = _tile8(r, 512 if c <= 1024 else 256)

    def body(w_ref, g_ref, m_ref, v_ref, go_ref, d_ref, nm_ref, nv_ref):
        g = g_ref[...]
        go_ref[...] = g
        d_ref[...], nm_ref[...], nv_ref[...] = _adamw_math(w_ref[...], g, m_ref[...], v_ref[...])

    blk = pl.BlockSpec((tr, c), lambda i: (i, 0))
    shp = jax.ShapeDtypeStruct((r, c), F32)
    return pl.pallas_call(body, name=name, grid=(r // tr,), in_specs=[blk] * 4, out_specs=[blk] * 4,
                          out_shape=[shp] * 4, compiler_params=_cp(("parallel",)))(w, g, m, v)


class _Weight:
    def __init__(self, name, rows, cols, colshard):
        self.name, self.colshard = name, colshard
        self.R, self.nn = rows // 2, cols
        self.P = 1 if colshard else N_CHIPS
        self.N = N_CHIPS * cols if colshard else cols

    def cols(self, k):
        return pl.ds(pl.multiple_of(k * self.nn, LANES), self.nn)

    def half(self, ref, k, h):
        return ref.at[0, h, :, self.cols(k)] if self.colshard else ref.at[k, h]

    def quarter(self, ref, k, h, q):
        return self.half(ref, k, h).at[pl.ds(q * (self.R // 2), self.R // 2), :]

    def part(self, ref, k):
        return ref.at[0, :, self.cols(k)] if self.colshard else ref.at[k]


def _remote(src, dst, ssem, rsem, dev):
    return pltpu.make_async_remote_copy(src_ref=src, dst_ref=dst, send_sem=ssem, recv_sem=rsem, device_id=dev,
                                        device_id_type=MESH)


def _other_chips(x, y):
    chips = [(1 - x, y), (x, 1 - y), (1 - x, 1 - y)]
    return chips, [2 * cx + cy for cx, cy in chips]


def _hbm(a):
    return pltpu.with_memory_space_constraint(a, pltpu.HBM)


def _split_start(name, arrays, sets, after=()):
    na, ns = len(arrays), len(sets)

    def body(*refs):
        outs = refs[na + len(after):]
        for s_, (idx, copies) in enumerate(sets):
            for i, (src, dst, dev, _) in enumerate(copies([refs[k] for k in idx], *_mesh_pos())):
                _remote(src, dst, outs[2 * s_].at[i], outs[2 * s_ + 1].at[i], dev).start()
        outs[2 * ns + na][...] = jnp.zeros((8, LANES), F32)

    sems = []
    for _, copies in sets:
        sems += [pltpu.SemaphoreType.DMA((copies.n,))] * 2
    out = pl.pallas_call(
        body, name=name, in_specs=[HBM] * na + [ANY] * len(after), out_specs=[SEM] * (2 * ns) + [HBM] * na + [VMEM],
        out_shape=sems + [pltpu.HBM(a.shape, a.dtype) for a in arrays] + [jax.ShapeDtypeStruct((8, LANES), F32)],
        input_output_aliases={i: 2 * ns + i for i in range(na)},
        compiler_params=pltpu.CompilerParams(has_side_effects=EFFECT),
    )(*[_hbm(a) for a in arrays], *after)
    return [(out[2 * i], out[2 * i + 1]) for i in range(ns)], list(out[2 * ns:2 * ns + na]), out[-1]


def _split_wait(name, arrays, ssem, rsem, copies, after):
    na = len(arrays)

    def body(*refs):
        for i, (src, _, dev, dst) in enumerate(copies(refs[:na], *_mesh_pos())):
            cp = _remote(src, dst, refs[na].at[i], refs[na + 1].at[i], dev)
            cp.wait_send()
            cp.wait_recv()

    return list(pl.pallas_call(
        body, name=name, in_specs=[HBM] * na + [SEM, SEM] + [ANY] * len(after), out_specs=[HBM] * na,
        out_shape=[pltpu.HBM(a.shape, a.dtype) for a in arrays], input_output_aliases={i: i for i in range(na)},
        compiler_params=pltpu.CompilerParams(has_side_effects=EFFECT),
    )(*arrays, ssem, rsem, *after))


def _pass_copies(grp, rels=(0, 1, 2)):
    def copies(land, x, y, c):
        _, ks = _other_chips(x, y)
        return [(w.half(land[wi], ks[j], c), w.half(land[wi], ks[j], c), (x, y, 1 - c), w.half(land[wi], ks[j], 1 - c))
                for wi, w in enumerate(grp) for j in rels]
    copies.n = len(grp) * len(rels)
    return copies


def _direct_copies(grp):
    def copies(land, x, y, c):
        chips, ks = _other_chips(x, y)
        out = []
        for wi, w in enumerate(grp):
            mine = w.half(land[wi], 2 * x + y, c)
            out += [(mine, mine, (*chips[j], c), w.half(land[wi], ks[j], c)) for j in range(3)]
        return out
    copies.n = 3 * len(grp)
    return copies


def _near_copies(grp):
    def copies(land, x, y, c):
        chips, ks = _other_chips(x, y)
        out = []
        for wi, w in enumerate(grp):
            mine = w.half(land[wi], 2 * x + y, c)
            out += [(mine, mine, (*chips[j], c), w.half(land[wi], ks[j], c)) for j in (0, 1)]
        return out
    copies.n = 2 * len(grp)
    return copies


def _far_copies(grp):
    def copies(land, x, y, c):
        chips, ks = _other_chips(x, y)
        out = []
        for wi, w in enumerate(grp):
            for j in (0, 1):
                q = w.quarter(land[wi], ks[j], c, j)
                out.append((q, q, (*chips[1 - j], c), w.quarter(land[wi], ks[2], c, j)))
        return out
    copies.n = 2 * len(grp)
    return copies


def _pair_copies(n, whole=False):
    def copies(refs, x, y, c):
        return [(refs[i] if whole else refs[i].at[:, 1 - c], refs[n + i], (x, y, 1 - c), refs[n + i]) for i in range(n)]
    copies.n = n
    return copies


def _share_copies(n):
    def copies(refs, x, y, c):
        return [(refs[i].at[c], refs[i].at[c], (x, y, 1 - c), refs[i].at[1 - c]) for i in range(n)]
    copies.n = n
    return copies


def _gather_conv_w(cw, thru):
    ncw = cw.shape[1]

    def body(cw_ref, _, out_ref, __, ssem, rsem):
        x, y, c = _mesh_pos()
        k_me = 2 * x + y
        chips, ks = _other_chips(x, y)
        cols = lambda k: out_ref.at[:, pl.ds(pl.multiple_of(k * ncw, LANES), ncw)]
        cps = [_remote(cw_ref, cols(k_me), ssem.at[j], rsem.at[j], (*chip, c)) for j, chip in enumerate(chips)]
        for cp in cps:
            cp.start()
        for k in range(N_CHIPS):
            @pl.when(k_me == k)
            def _():
                out_ref[:, k * ncw:(k + 1) * ncw] = cw_ref[...]
        for j in range(3):
            _remote(cw_ref, cols(ks[j]), ssem.at[j], rsem.at[j], (*chips[j], c)).wait_recv()
        for cp in cps:
            cp.wait_send()

    return pl.pallas_call(
        body, name="gather_conv_w", in_specs=[VMEM, ANY], out_specs=[VMEM, ANY],
        out_shape=[jax.ShapeDtypeStruct((3, N_CHIPS * ncw), F32), jax.ShapeDtypeStruct(thru.shape, thru.dtype)],
        scratch_shapes=[pltpu.SemaphoreType.DMA((3,)), pltpu.SemaphoreType.DMA((3,))],
        input_output_aliases={1: 1},
    )(cw, thru)


def _grad_tiles(w, n):
    return _tile8(w.R, 512) if w.R <= 512 else w.R // 2, _tile(n, 2048)


def _pair_sum(name, w, pos, grad, got):
    tr, tn = _grad_tiles(w, w.N)

    def body(pos_ref, g_ref, r_ref, o_ref):
        o_ref[...] = (g_ref[...].astype(F32) + r_ref[...].astype(F32)).astype(BF16)

    blk = pl.BlockSpec((None, tr, tn), lambda p, i, j, pos: (p, i, j))
    grid_spec = pltpu.PrefetchScalarGridSpec(
        num_scalar_prefetch=1, grid=(w.P, w.R // tr, w.N // tn),
        in_specs=[pl.BlockSpec((None, None, tr, tn), lambda p, i, j, pos: (p, pos[0], i, j)), blk], out_specs=blk)
    return pl.pallas_call(body, name=name, grid_spec=grid_spec, out_shape=jax.ShapeDtypeStruct((w.P, w.R, w.N), BF16),
                          compiler_params=_cp(("parallel",) * 3))(pos, grad, got)


def _scatter_start(name, ws, pairs):
    nw = len(ws)

    def body(*refs):
        pr, land = refs[:nw], refs[nw:2 * nw]
        ssem, rsem = refs[2 * nw], refs[2 * nw + 1]
        token = refs[4 * nw + 2]
        x, y, c = _mesh_pos()
        chips, ks = _other_chips(x, y)
        for i, w in enumerate(ws):
            for j, chip in enumerate(chips):
                _remote(w.part(pr[i], ks[j]), land[i].at[j], ssem.at[3 * i + j], rsem.at[3 * i + j], (*chip, c)).start()
        token[...] = jnp.zeros_like(token)

    lands = [lax.empty((3, w.R, w.nn), BF16) for w in ws]
    out = pl.pallas_call(
        body, name=name, in_specs=[HBM] * (2 * nw),
        out_specs=[SEM, SEM] + [HBM] * (2 * nw) + [VMEM],
        out_shape=[pltpu.SemaphoreType.DMA((3 * nw,))] * 2 + [pltpu.HBM(a.shape, a.dtype) for a in pairs + lands]
        + [jax.ShapeDtypeStruct((8, LANES), F32)],
        input_output_aliases={i: 2 + i for i in range(2 * nw)},
        compiler_params=pltpu.CompilerParams(has_side_effects=EFFECT),
    )(*[_hbm(a) for a in pairs + lands])
    return out[0], out[1], list(out[2:2 + nw]), list(out[2 + nw:2 + 2 * nw]), out[-1]


def _scatter_wait(name, ws, pairs, lands, ssem, rsem, after):
    nw = len(ws)

    def body(*refs):
        pr, land = refs[:nw], refs[nw:2 * nw]
        ssem_ref, rsem_ref = refs[2 * nw], refs[2 * nw + 1]
        x, y, c = _mesh_pos()
        chips, ks = _other_chips(x, y)
        for i, w in enumerate(ws):
            for j, chip in enumerate(chips):
                cp = _remote(w.part(pr[i], ks[j]), land[i].at[j], ssem_ref.at[3 * i + j], rsem_ref.at[3 * i + j], (*chip, c))
                cp.wait_send()
                cp.wait_recv()

    out = pl.pallas_call(
        body, name=name, in_specs=[HBM] * (2 * nw) + [SEM, SEM] + [ANY] * len(after), out_specs=[HBM] * (2 * nw),
        out_shape=[pltpu.HBM(a.shape, a.dtype) for a in pairs + lands],
        input_output_aliases={i: i for i in range(2 * nw)},
        compiler_params=pltpu.CompilerParams(has_side_effects=EFFECT),
    )(*pairs, *lands, ssem, rsem, *after)
    return list(out[:nw]), list(out[nw:])


def _final_sum(name, w, pos, grad, got, parts):
    tr, tn = _grad_tiles(w, w.nn)
    nbc = w.nn // tn
    if got is None:
        return _final_sum_pair(name, w, pos, grad, parts, tr, tn)

    def body(pos_ref, g_ref, r_ref, p_ref, o_ref):
        acc = g_ref[...].astype(F32) + r_ref[...].astype(F32)
        for j in range(3):
            acc = acc + p_ref[j].astype(F32)
        o_ref[...] = acc

    if w.colshard:
        g_spec = pl.BlockSpec((None, None, tr, tn), lambda i, j, pos: (0, pos[0], i, pos[1] * nbc + j))
        r_spec = pl.BlockSpec((None, tr, tn), lambda i, j, pos: (0, i, pos[1] * nbc + j))
    else:
        g_spec = pl.BlockSpec((None, None, tr, tn), lambda i, j, pos: (pos[1], pos[0], i, j))
        r_spec = pl.BlockSpec((None, tr, tn), lambda i, j, pos: (pos[1], i, j))
    grid_spec = pltpu.PrefetchScalarGridSpec(
        num_scalar_prefetch=1, grid=(w.R // tr, nbc),
        in_specs=[g_spec, r_spec, pl.BlockSpec((3, tr, tn), lambda i, j, pos: (0, i, j))],
        out_specs=pl.BlockSpec((None, tr, tn), lambda i, j, pos: (pos[0], i, j)))
    return pl.pallas_call(body, name=name, grid_spec=grid_spec, out_shape=jax.ShapeDtypeStruct((2, w.R, w.nn), F32),
                          compiler_params=_cp(("parallel",) * 2))(pos, grad, got, parts)


def _final_sum_pair(name, w, pos, pair, parts, tr, tn):
    nbc = w.nn // tn

    def body(pos_ref, g_ref, p_ref, o_ref):
        acc = g_ref[...].astype(F32)
        for j in range(3):
            acc = acc + p_ref[j].astype(F32)
        o_ref[...] = acc

    if w.colshard:
        g_spec = pl.BlockSpec((None, tr, tn), lambda i, j, pos: (0, i, pos[1] * nbc + j))
    else:
        g_spec = pl.BlockSpec((None, tr, tn), lambda i, j, pos: (pos[1], i, j))
    grid_spec = pltpu.PrefetchScalarGridSpec(
        num_scalar_prefetch=1, grid=(w.R // tr, nbc),
        in_specs=[g_spec, pl.BlockSpec((3, tr, tn), lambda i, j, pos: (0, i, j))],
        out_specs=pl.BlockSpec((None, tr, tn), lambda i, j, pos: (pos[0], i, j)))
    return pl.pallas_call(body, name=name, grid_spec=grid_spec, out_shape=jax.ShapeDtypeStruct((2, w.R, w.nn), F32),
                          compiler_params=_cp(("parallel",) * 2))(pos, pair, parts)


VEC_ROWS = 16


def _vector_sum(d, n_conv, parts, deps=()):
    def body(*refs):
        dg1, dba, dbb, dcw, dcb, dps, dg2, dgf, lc = refs[:9]
        tot_ref, snd, got, ssem, rsem = refs[9 + len(deps):]
        x, y, c = _mesh_pos()
        me = 4 * x + 2 * y + c
        snd[...] = jnp.zeros_like(snd)
        for row, ref in ((0, dg1), (1, dba), (2, dbb), (3, dps), (4, dg2), (5, dgf), (6, lc)):
            snd[row:row + 1, :] = ref[...]
        snd[7:8, :n_conv] = dcb[...]
        snd[8:11, :n_conv] = dcw[...]
        cps = []
        for r in range(1, N_DEV):
            peer = tuple(1 - p if (r >> b) & 1 else p for p, b in ((x, 2), (y, 1), (c, 0)))
            cps.append(_remote(snd, got.at[me], ssem.at[r - 1], rsem.at[r - 1], peer))
        for cp in cps:
            cp.start()
        got[me] = snd[...]
        for r in range(1, N_DEV):
            peer = tuple(1 - p if (r >> b) & 1 else p for p, b in ((x, 2), (y, 1), (c, 0)))
            _remote(snd, got.at[4 * peer[0] + 2 * peer[1] + peer[2]], ssem.at[r - 1], rsem.at[r - 1], peer).wait_recv()
        for cp in cps:
            cp.wait_send()
        tot = got[0]
        for dev in range(1, N_DEV):
            tot = tot + got[dev]
        tot_ref[...] = tot

    return pl.pallas_call(
        body, name="vector_params_sum", in_specs=[VMEM] * len(parts) + [ANY] * len(deps), out_specs=VMEM,
        out_shape=jax.ShapeDtypeStruct((VEC_ROWS, d), F32),
        scratch_shapes=[pltpu.VMEM((VEC_ROWS, d), F32), pltpu.VMEM((N_DEV, VEC_ROWS, d), F32),
                        pltpu.SemaphoreType.DMA((N_DEV - 1,)), pltpu.SemaphoreType.DMA((N_DEV - 1,))],
        compiler_params=pltpu.CompilerParams(vmem_limit_bytes=VMEM_LIMIT),
    )(*parts, *deps)


def _vector_update(tot, n_conv, params):
    ncw = params[2][0].shape[1]
    n_par = len(params)

    def body(*refs):
        tot = refs[0][...]
        wmv = refs[1:1 + 3 * n_par]
        outs = refs[1 + 3 * n_par:1 + 7 * n_par]
        refs[1 + 7 * n_par][...] = jnp.sum(tot[6:7, :], axis=1, keepdims=True)
        k_me = 2 * lax.axis_index("x") + lax.axis_index("y")
        g_cw = jnp.zeros((3, ncw), F32)
        for k in range(N_CHIPS):
            g_cw = g_cw + jnp.where(k_me == k, tot[8:11, k * ncw:(k + 1) * ncw], 0.0)
        grads = [tot[0:1, :], jnp.concatenate([tot[1:2, :], tot[2:3, :]], axis=1), g_cw, tot[7:8, :n_conv],
                 tot[3:4, :], tot[4:5, :], tot[5:6, :]]
        for i, g in enumerate(grads):
            w_ref, m_ref, v_ref = wmv[3 * i:3 * i + 3]
            delta, nm, nv = _adamw_math(w_ref[...], g, m_ref[...], v_ref[...])
            outs[4 * i][...] = g
            outs[4 * i + 1][...] = delta
            outs[4 * i + 2][...] = nm
            outs[4 * i + 3][...] = nv

    args = [tot]
    out_shape = []
    for w, m, v in params:
        args += [w, m, v]
        out_shape += [jax.ShapeDtypeStruct(w.shape, F32)] * 4
    out_shape.append(jax.ShapeDtypeStruct((1, 1), F32))
    return pl.pallas_call(body, name="vector_params_update", in_specs=[VMEM] * len(args), out_specs=[VMEM] * len(out_shape),
                          out_shape=out_shape, compiler_params=pltpu.CompilerParams(vmem_limit_bytes=VMEM_LIMIT))(*args)


def kernel(x, norm1_g, w_in, b_gate, conv_w, conv_b, w_a_out, w_pool, pool_scale, w_o, norm2_g, w_ffn_gate, w_ffn_up, w_ffn_down, final_g, loss_target, m_norm1_g, m_w_in, m_b_gate, m_conv_w, m_conv_b, m_w_a_out, m_w_pool, m_pool_scale, m_w_o, m_norm2_g, m_w_ffn_gate, m_w_ffn_up, m_w_ffn_down, m_final_g, v_norm1_g, v_w_in, v_b_gate, v_conv_w, v_conv_b, v_w_a_out, v_w_pool, v_pool_scale, v_w_o, v_norm2_g, v_w_ffn_gate, v_w_ffn_up, v_w_ffn_down, v_final_g):
    t, d = x.shape[1], x.shape[2]
    n_conv = conv_b.shape[1]
    n_groups, pool_cg, pool_dg = w_pool.shape[1], w_pool.shape[2], N_CHIPS * w_pool.shape[3]
    d_ff = N_CHIPS * w_ffn_gate.shape[2]
    assert n_conv // n_groups == pool_cg and n_conv % (n_groups * MIX_COLS) == 0 and n_groups == len(POOL_WINDOWS)

    big = {"w_in": (w_in, m_w_in, v_w_in), "w_a_out": (w_a_out, m_w_a_out, v_w_a_out), "w_pool": (w_pool, m_w_pool, v_w_pool),
           "w_o": (w_o, m_w_o, v_w_o), "w_ffn_gate": (w_ffn_gate, m_w_ffn_gate, v_w_ffn_gate),
           "w_ffn_up": (w_ffn_up, m_w_ffn_up, v_w_ffn_up), "w_ffn_down": (w_ffn_down, m_w_ffn_down, v_w_ffn_down)}
    colshard = {"w_in": True, "w_a_out": True, "w_pool": True, "w_o": False, "w_ffn_gate": True, "w_ffn_up": True,
                "w_ffn_down": False}
    names = list(big)
    shard2d = {n: big[n][0].reshape(-1, big[n][0].shape[-1]) for n in names}
    ws = [_Weight(n, *shard2d[n].shape, colshard[n]) for n in names]

    xs, tgt = x[0], loss_target[0]
    cw_loc = conv_w[0]
    pos = jnp.stack([lax.axis_index("c"), 2 * lax.axis_index("x") + lax.axis_index("y")]).astype(jnp.int32)
    by_name = {w.name: w for w in ws}
    groups = [[by_name[n] for n in g] for g in (["w_in"], ["w_a_out", "w_pool", "w_o"], ["w_ffn_gate"], ["w_ffn_up"],
                                                 ["w_ffn_down"])]
    rgroups = [groups[0], groups[1], groups[2] + groups[3], groups[4]]

    cast = lambda w, dep: _cast_place(f"cast_{w.name}", w, pos, shard2d[w.name].reshape(2, w.R, w.nn), deps=dep)
    chips, ks = _other_chips(lax.axis_index("x"), lax.axis_index("y"))
    kvec = jnp.stack([pos[1], *ks]).astype(jnp.int32)
    full = {}

    def start(name, arrays, copies, after=()):
        return start_many([(name, arrays, copies)], after)[0]

    def start_many(parts, after=()):
        arrays, sets = [], []
        for _, arrs, copies in parts:
            for a in arrs:
                if not any(a is b for b in arrays):
                    arrays.append(a)
            sets.append(([next(i for i, b in enumerate(arrays) if b is a) for a in arrs], copies))
        sems, thru, token = _split_start("_".join(p[0] for p in parts), arrays, sets, after)
        return [(name, [thru[i] for i in idx], ssem, rsem, copies, token)
                for (name, _, copies), (idx, _), (ssem, rsem) in zip(parts, sets, sems)]

    def wait(started, after):
        name, arrays, ssem, rsem, copies, _ = started
        return _split_wait(name + "_wait", arrays, ssem, rsem, copies, after)

    def pass_on(g, got, after=()):
        return start(f"pass_{g}", got, _pass_copies(groups[g]), after)

    def passed(g, st, after=None):
        got = wait(st, [st[5]] if after is None else after)
        full.update({w.name: a.reshape(w.P * 2 * w.R, w.N) for w, a in zip(groups[g], got)})

    near = start("near_0", [cast(w, []) for w in groups[0]], _near_copies(groups[0]))
    rest = [cast(w, [near[5]]) for grp in groups[1:] for w in grp]
    h1 = _rms_fwd("norm1_fwd", xs, norm1_g, deps=[near[5]])
    proj = _proj_piece("proj_own", h1, shard2d["w_in"], None, kvec, 0, 1, deps=rest)
    got = wait(near, [proj])
    far, small, st = start_many([("far_0", got, _far_copies(groups[0])), ("direct_1", rest[:3], _direct_copies(groups[1])),
                                 ("pass_near_0", got, _pass_copies(groups[0], (0, 1)))])
    got = wait(st, [st[5]])
    proj = _proj_piece("proj_near", h1, got[0].reshape(-1, groups[0][0].N), proj, kvec, 1, 2)
    st = start("pass_far_0", wait((far[0], got) + far[2:], [proj]), _pass_copies(groups[0], (2,)))
    got = wait(st, [st[5]])
    w_in_full = got[0].reshape(-1, groups[0][0].N)
    proj = _proj_piece("proj_far", h1, w_in_full, proj, kvec, 3, 1)
    cw_full, proj = _gather_conv_w(cw_loc, proj)
    got = wait(small, [proj])
    near_g = start("near_2", rest[3:4], _near_copies(groups[2]), got)
    st = pass_on(1, got, [near_g[5]])
    z, p = _mixer_fwd("mixer_fwd", proj, cw_full, conv_b, n_conv, n_groups, deps=[st[5]])
    passed(1, st, [z])
    wp_full = full["w_pool"].reshape(n_groups, pool_cg, pool_dg)
    ya = _mm_nn("conv_out", z, full["w_a_out"], BF16)
    yb = _gmm_nn("pool_out", p, wp_full, BF16)
    merged = _merge_fwd("merge_fwd", proj, b_gate, ya, yb, pool_scale)
    far_g, near_u = start_many([("far_2", wait(near_g, [merged]), _far_copies(groups[2])),
                                ("near_3", rest[4:5], _near_copies(groups[3]))])
    x2 = _mm_nn("mix_out", merged, full["w_o"], F32, add=xs, deps=[near_u[5]])
    st = pass_on(2, wait(far_g, [x2]))
    h2 = _rms_fwd("norm2_fwd", x2, norm2_g, deps=[st[5]])
    passed(2, st, [h2])
    def far_and_pass(g, near_st, after, more=()):
        got = wait(near_st, after)
        return start_many([(f"far_{g}", got, _far_copies(groups[g])), (f"pass_near_{g}", got, _pass_copies(groups[g], (0, 1))),
                           *more])

    def finish(g, far_st, pass_st, after):
        got = wait(pass_st, after)
        st = start(f"pass_far_{g}", wait((far_st[0], got) + far_st[2:], after), _pass_copies(groups[g], (2,)))
        passed(g, st)

    gate = _mm_nn("ffn_gate_a", h2, full["w_ffn_gate"], BF16, part=(0, 2))
    far_u, pass_u, near_d = far_and_pass(3, near_u, [gate], [("near_4", rest[5:6], _near_copies(groups[4]))])
    gate = _mm_nn("ffn_gate_b", h2, full["w_ffn_gate"], BF16, part=(1, 2), prev=gate, deps=[near_d[5]])
    finish(3, far_u, pass_u, [gate])
    up_act = _ffn_up_act("ffn_up_act_a", h2, full["w_ffn_up"], gate, part=(0, 2))
    far_d, pass_d = far_and_pass(4, near_d, [up_act[0]])
    up, act = _ffn_up_act("ffn_up_act_b", h2, full["w_ffn_up"], gate, part=(1, 2), prev=up_act, deps=[pass_d[5]])
    finish(4, far_d, pass_d, [act])
    x3 = _mm_nn("ffn_down", act, full["w_ffn_down"], F32, add=x2, tiles=(1024, 512))

    pending = {}

    def pair_start(g, grads):
        grp = rgroups[g]
        gcan = [grads[w.name].reshape(w.P, 2, w.R, w.N) for w in grp]
        slots = [lax.empty((w.P, w.R, w.N), BF16) for w in grp]
        pending[g] = start(f"pair_start_{g}", gcan + slots, _pair_copies(len(grp)))
        return pending[g][5]

    def scatter_start(g, after):
        grp = rgroups[g]
        n = len(grp)
        arrs = wait(pending[g], after)
        gcan, sib = arrs[:n], arrs[n:]
        pairs = [_pair_sum(f"pair_sum_{w.name}", w, pos, a, s) for w, a, s in zip(grp, gcan, sib)]
        ssem, rsem, pairs, slots, token = _scatter_start(f"scatter_start_{g}", grp, pairs)
        pending[g] = (gcan, sib, pairs, slots, ssem, rsem)
        return token

    def pair_start_halves(g, ab, deps):
        grp = rgroups[g]
        sent = [_mm_tn_half(f"d{w.name}_sib", a, b, pos, False, deps=deps if i == 0 else ()) for i, (w, (a, b)) in enumerate(zip(grp, ab))]
        slots = [lax.empty((1, w.R, w.N), BF16) for w in grp]
        pending[g] = start(f"pair_start_{g}", sent + slots, _pair_copies(len(grp), whole=True))
        return pending[g][5]

    def scatter_start_halves(g, ab, after):
        grp = rgroups[g]
        n = len(grp)
        arrs = wait(pending[g], after)
        pairs = [_mm_tn_half(f"d{w.name}_own", a, b, pos, True, add=s) for w, (a, b), s in zip(grp, ab, arrs[n:])]
        ssem, rsem, pairs, slots, token = _scatter_start(f"scatter_start_{g}", grp, pairs)
        pending[g] = (None, None, pairs, slots, ssem, rsem)
        return token

    def reduce_finish(g, after):
        grp = rgroups[g]
        gcan, sib, pairs, slots, ssem, rsem = pending[g]
        pairs, parts = _scatter_wait(f"scatter_wait_{g}", grp, pairs, slots, ssem, rsem, after)
        if gcan is None:
            return [_final_sum(f"final_sum_{w.name}", w, pos, a, None, q) for w, a, q in zip(grp, pairs, parts)]
        return [_final_sum(f"final_sum_{w.name}", w, pos, a, s, q) for w, a, s, q in zip(grp, gcan, sib, parts)]

    grads = {}
    dx3, dx3b, d_gf, loss_cols = _final_bwd("final_bwd", x3, final_g.reshape(1, d), tgt)
    dgate, dup = _ffn_bwd("ffn_bwd", dx3b, full["w_ffn_down"], gate, up)
    grads["w_ffn_down"] = _mm_tn("dw_ffn_down", act, dx3b, BF16)
    tok = pair_start(3, grads)
    dh2 = _mm_nt("d_h2", [(dgate, full["w_ffn_gate"]), (dup, full["w_ffn_up"])], BF16, tk=d_ff // 4, deps=[tok])
    tok = scatter_start(3, [dh2])
    tok = pair_start_halves(2, [(h2, dgate), (h2, dup)], [tok])
    dx2, dx2b, d_g2 = _rms_bwd("norm2_bwd", x2, norm2_g, dh2, dx3, True, deps=[tok])
    dmerged = _mm_nt("d_merged", [(dx2b, full["w_o"])], BF16, tk=d)
    grads["w_o"] = _mm_tn("dw_o", merged, dx2b, BF16)
    tok = scatter_start_halves(2, [(h2, dgate), (h2, dup)], [grads["w_o"]])
    dya, dyb, dproj, d_bga, d_bgb, d_ps = _merge_bwd("merge_bwd", dmerged, proj, b_gate, ya, yb, pool_scale, deps=[tok])
    dz = _mm_nt("d_z", [(dya, full["w_a_out"])], BF16, tk=d)
    grads["w_a_out"] = _mm_tn("dw_a_out", z, dya, BF16)
    dp = _gmm_nt("d_pool", dyb, wp_full, BF16)
    grads["w_pool"] = _gmm_tn("dw_pool", p, dyb, n_groups, BF16)
    tok = pair_start(1, grads)
    dproj, d_cw, d_cb = _mixer_bwd("mixer_bwd", dz, dp, proj, cw_full, conv_b, dproj, n_conv, n_groups, deps=[tok])
    tok = scatter_start(1, [dproj])
    tok = pair_start_halves(0, [(h1, dproj)], [tok])
    dh1 = _mm_nt("d_h1", [(dproj, w_in_full)], BF16, tk=proj.shape[1] // 4, deps=[tok])
    tok = scatter_start_halves(0, [(h1, dproj)], [dh1])
    grad_x, d_g1 = _rms_bwd("norm1_bwd", xs, norm1_g, dh1, dx2, False, deps=[tok])

    g_big, d_big, m_big, v_big = {}, {}, {}, {}

    def update(wsub, shared):
        out = []
        for w, g in zip(wsub, shared):
            wt, mt, vt = big[w.name]
            g2 = g.reshape(2 * w.R, w.nn)
            go, dl, nm, nv = _adamw(f"adamw_{w.name}", shard2d[w.name], g2, mt.reshape(g2.shape), vt.reshape(g2.shape))
            g_big[w.name], d_big[w.name], m_big[w.name], v_big[w.name] = (a.reshape(wt.shape) for a in (go, dl, nm, nv))
            out.append(nv)
        return out

    after = [grad_x]
    started = []
    for g in (3, 2, 1):
        halves = reduce_finish(g, after)
        started.append((g, start(f"share_{g}", halves, _share_copies(len(halves)))))
        after = [started[-1][1][5]]
    for g, st in started:
        after = update(rgroups[g], wait(st, after))
    st = start("share_0", reduce_finish(0, after), _share_copies(1))

    vec_names = ["norm1_g", "b_gate", "conv_w", "conv_b", "pool_scale", "norm2_g", "final_g"]
    vec = {"norm1_g": (norm1_g, m_norm1_g, v_norm1_g), "b_gate": (b_gate, m_b_gate, v_b_gate),
           "conv_w": (cw_loc, m_conv_w[0], v_conv_w[0]), "conv_b": (conv_b, m_conv_b, v_conv_b),
           "pool_scale": (pool_scale, m_pool_scale, v_pool_scale), "norm2_g": (norm2_g, m_norm2_g, v_norm2_g),
           "final_g": tuple(a.reshape(1, d) for a in (final_g, m_final_g, v_final_g))}
    tot = _vector_sum(d, n_conv, [d_g1, d_bga, d_bgb, d_cw, d_cb, d_ps, d_g2, d_gf, loss_cols], deps=st[1])
    vout = _vector_update(tot, n_conv, [vec[n] for n in vec_names])
    update(rgroups[0], wait(st, []))

    shapes = {"conv_w": conv_w.shape, "final_g": final_g.shape}
    g_vec, d_vec, m_vec, v_vec = ({n: vout[4 * i + q].reshape(shapes.get(n, vec[n][0].shape)) for i, n in enumerate(vec_names)}
                                  for q in range(4))
    loss = vout[-1].reshape(())

    order = ["norm1_g", "w_in", "b_gate", "conv_w", "conv_b", "w_a_out", "w_pool", "pool_scale", "w_o", "norm2_g",
             "w_ffn_gate", "w_ffn_up", "w_ffn_down", "final_g"]
    pick = lambda vecs, bigs: [vecs[n] if n in vecs else bigs[n] for n in order]
    return (loss, grad_x.reshape(x.shape), *pick(g_vec, g_big), *pick(d_vec, d_big), *pick(m_vec, m_big),
            *pick(v_vec, v_big))
```

```python
import jax
import jax.numpy as jnp
from jax import lax
from jax.experimental import pallas as pl
from jax.experimental.pallas import tpu as pltpu

F32, BF16 = jnp.float32, jnp.bfloat16
MESH = pl.DeviceIdType.MESH
ANY = pl.BlockSpec(memory_space=pl.ANY)
VMEM = pl.BlockSpec(memory_space=pltpu.VMEM)
HBM = pl.BlockSpec(memory_space=pltpu.HBM)
SEM = pl.BlockSpec(memory_space=pltpu.SEMAPHORE)
EFFECT = pltpu.SideEffectType.DATAFLOW_SIDE_EFFECTING

EPS = 1e-6
POOL_WINDOWS = (2, 4, 8, 16)
ADAM_LR, ADAM_B1, ADAM_B2, ADAM_EPS, ADAM_WD, ADAM_STEP = 0.001, 0.9, 0.999, 1e-08, 0.01, 10

V7X_VMEM_BYTES = 64 * 1024 * 1024
VMEM_LIMIT = V7X_VMEM_BYTES * 3 // 4
LANES = 128
COL_TILE = 8 * LANES
N_CHIPS = 4
N_DEV = 8

_DIMS = {
    "nn": (((1,), (0,)), ((), ())),
    "nt": (((1,), (1,)), ((), ())),
    "tn": (((0,), (0,)), ((), ())),
}


def _cp(sem):
    return pltpu.CompilerParams(dimension_semantics=sem, vmem_limit_bytes=VMEM_LIMIT)


def _mesh_pos():
    return lax.axis_index("x"), lax.axis_index("y"), lax.axis_index("c")


def _mm(name, pairs, *, mode, grid, out_shape, o_spec, nk=1, kaxis=None, add=None, deps=(), prev=None):
    npair = len(pairs)
    has_add = add is not None

    def body(*refs):
        ab = refs[: 2 * npair]
        pos = 2 * npair
        add_ref = refs[pos] if has_add else None
        pos += int(has_add) + len(deps) + (prev is not None)
        o_ref = refs[pos]
        acc_ref = refs[pos + 1] if nk > 1 else None
        d = None
        for p in range(npair):
            t = lax.dot_general(ab[2 * p][...], ab[2 * p + 1][...], _DIMS[mode], preferred_element_type=F32)
            d = t if d is None else d + t
        if nk == 1:
            if has_add:
                d = d + add_ref[...].astype(F32)
            o_ref[...] = d.astype(o_ref.dtype)
        else:
            k = pl.program_id(kaxis)

            @pl.when(k == 0)
            def _():
                acc_ref[...] = d

            @pl.when(k > 0)
            def _():
                acc_ref[...] += d

            @pl.when(k == nk - 1)
            def _():
                r = acc_ref[...]
                if has_add:
                    r = r + add_ref[...].astype(F32)
                o_ref[...] = r.astype(o_ref.dtype)

    args, specs = [], []
    for a, a_spec, b, b_spec in pairs:
        args += [a, b]
        specs += [a_spec, b_spec]
    if has_add:
        args.append(add[0])
        specs.append(add[1])
    args += list(deps)
    specs += [ANY] * len(deps)
    aliases = {}
    if prev is not None:
        aliases = {len(args): 0}
        args.append(prev)
        specs.append(ANY)
    scratch = []
    if nk > 1:
        blk = [d for d in o_spec.block_shape if d is not None]
        scratch = [pltpu.VMEM(tuple(blk), F32)]
    sem = tuple("arbitrary" if (nk > 1 and ax == kaxis) else "parallel" for ax in range(len(grid)))
    return pl.pallas_call(
        body, name=name, grid=grid, in_specs=specs, out_specs=o_spec, out_shape=out_shape,
        scratch_shapes=scratch, input_output_aliases=aliases, compiler_params=_cp(sem),
    )(*args)


def _tile_span(n_tiles, part):
    if part is None:
        return 0, n_tiles
    p, of = part
    return p * n_tiles // of, (p + 1) * n_tiles // of


def _tile(n, pref):
    if n <= pref:
        return n
    for t in range(pref, 0, -LANES):
        if t % LANES == 0 and n % t == 0:
            return t
    raise ValueError(f"no tile for {n}")


def _mm_nn(name, a, b, out_dtype, add=None, tk=None, deps=(), part=None, prev=None, tiles=None):
    m, kk = a.shape
    n = b.shape[1]
    tm, tn = _tile(m, 1024), _tile(n, COL_TILE)
    if tiles is not None:
        tm, tn = _tile(m, tiles[0]), _tile(n, tiles[1])
    out_shape = jax.ShapeDtypeStruct((m, n), out_dtype)
    if tk is None or tk == kk:
        j0, j1 = _tile_span(n // tn, part)
        grid = (m // tm, j1 - j0)
        pairs = [(a, pl.BlockSpec((tm, kk), lambda i, j: (i, 0)), b, pl.BlockSpec((kk, tn), lambda i, j: (0, j0 + j)))]
        o_spec = pl.BlockSpec((tm, tn), lambda i, j: (i, j0 + j))
        add_ = None if add is None else (add, pl.BlockSpec((tm, tn), lambda i, j: (i, j0 + j)))
        return _mm(name, pairs, mode="nn", grid=grid, out_shape=out_shape, o_spec=o_spec, add=add_, deps=deps, prev=prev)
    tn = _tile(n, 1024)
    nk = kk // tk
    grid = (m // tm, n // tn, nk)
    pairs = [(a, pl.BlockSpec((tm, tk), lambda i, j, k: (i, k)), b, pl.BlockSpec((tk, tn), lambda i, j, k: (k, j)))]
    o_spec = pl.BlockSpec((tm, tn), lambda i, j, k: (i, j))
    add_ = None if add is None else (add, pl.BlockSpec((tm, tn), lambda i, j, k: (i, j)))
    return _mm(name, pairs, mode="nn", grid=grid, out_shape=out_shape, o_spec=o_spec, nk=nk, kaxis=2, add=add_, deps=deps)


def _mm_nt(name, abs_, out_dtype, tk, deps=()):
    m, kk = abs_[0][0].shape
    n = abs_[0][1].shape[0]
    tm = _tile(m, 1024)
    nk = kk // tk
    tn = _tile(n, COL_TILE if nk == 1 else 1024)
    out_shape = jax.ShapeDtypeStruct((m, n), out_dtype)
    if nk == 1:
        grid = (m // tm, n // tn)
        pairs = [(a, pl.BlockSpec((tm, kk), lambda i, j: (i, 0)), b, pl.BlockSpec((tn, kk), lambda i, j: (j, 0)))
                 for a, b in abs_]
        o_spec = pl.BlockSpec((tm, tn), lambda i, j: (i, j))
        return _mm(name, pairs, mode="nt", grid=grid, out_shape=out_shape, o_spec=o_spec, deps=deps)
    grid = (m // tm, n // tn, nk)
    pairs = [(a, pl.BlockSpec((tm, tk), lambda i, j, k: (i, k)), b, pl.BlockSpec((tn, tk), lambda i, j, k: (j, k)))
             for a, b in abs_]
    o_spec = pl.BlockSpec((tm, tn), lambda i, j, k: (i, j))
    return _mm(name, pairs, mode="nt", grid=grid, out_shape=out_shape, o_spec=o_spec, nk=nk, kaxis=2, deps=deps)


def _mm_tn(name, a, b, out_dtype, deps=()):
    t, m = a.shape
    n = b.shape[1]
    tm, tn = _tile(m, 512), _tile(n, 2048)
    if n > m:
        grid = (n // tn, m // tm)
        a_map, b_map, o_map = (lambda j, i: (0, i)), (lambda j, i: (0, j)), (lambda j, i: (i, j))
    else:
        grid = (m // tm, n // tn)
        a_map, b_map, o_map = (lambda i, j: (0, i)), (lambda i, j: (0, j)), (lambda i, j: (i, j))
    pairs = [(a, pl.BlockSpec((t, tm), a_map), b, pl.BlockSpec((t, tn), b_map))]
    o_spec = pl.BlockSpec((tm, tn), o_map)
    return _mm(name, pairs, mode="tn", grid=grid, out_shape=jax.ShapeDtypeStruct((m, n), out_dtype), o_spec=o_spec,
               deps=deps)


def _mm_tn_half(name, a, b, pos, mine, add=None, deps=()):
    t, m = a.shape
    r, n = m // 2, b.shape[1]
    tm, tn = _tile(r, 512), _tile(n, 2048)
    nbi = r // tm
    half = (lambda pos: pos[0]) if mine else (lambda pos: 1 - pos[0])
    if n > r:
        grid, ij = (n // tn, nbi), (lambda g0, g1: (g1, g0))
    else:
        grid, ij = (nbi, n // tn), (lambda g0, g1: (g0, g1))
    has_add = add is not None

    def body(pos_ref, a_ref, b_ref, *rest):
        d = lax.dot_general(a_ref[...], b_ref[...], _DIMS["tn"], preferred_element_type=F32)
        if has_add:
            d = d + rest[0][...].astype(F32)
        rest[-1][...] = d.astype(BF16)

    o_spec = pl.BlockSpec((None, tm, tn), lambda g0, g1, pos: (0, *ij(g0, g1)))
    grid_spec = pltpu.PrefetchScalarGridSpec(
        num_scalar_prefetch=1, grid=grid,
        in_specs=[pl.BlockSpec((t, tm), lambda g0, g1, pos: (0, half(pos) * nbi + ij(g0, g1)[0])),
                  pl.BlockSpec((t, tn), lambda g0, g1, pos: (0, ij(g0, g1)[1]))]
        + ([o_spec] if has_add else []) + [ANY] * len(deps),
        out_specs=o_spec)
    return pl.pallas_call(body, name=name, grid_spec=grid_spec, out_shape=jax.ShapeDtypeStruct((1, r, n), BF16),
                          compiler_params=_cp(("parallel",) * 2))(pos, a, b, *([add] if has_add else []), *deps)


def _proj_piece(name, h, w, prev, kvec, base, count, deps=()):
    t, kk = h.shape
    own = w.dtype == F32
    nn = w.shape[1] if own else w.shape[1] // N_CHIPS
    tm, tn = _tile(t, 1024), _tile(nn, COL_TILE)
    nb = nn // tn

    def body(kv_ref, h_ref, w_ref, *rest):
        rest[-1][...] = lax.dot_general(h_ref[...], w_ref[...].astype(BF16), _DIMS["nn"],
                                        preferred_element_type=F32).astype(BF16)

    cols = lambda s, i, j, kv: (0, j) if own else (0, kv[base + s] * nb + j)
    extra = ([] if prev is None else [prev]) + list(deps)
    grid_spec = pltpu.PrefetchScalarGridSpec(
        num_scalar_prefetch=1, grid=(count, t // tm, nb),
        in_specs=[pl.BlockSpec((tm, kk), lambda s, i, j, kv: (i, 0)), pl.BlockSpec((kk, tn), cols)] + [ANY] * len(extra),
        out_specs=pl.BlockSpec((tm, tn), lambda s, i, j, kv: (i, kv[base + s] * nb + j)))
    return pl.pallas_call(body, name=name, grid_spec=grid_spec, out_shape=jax.ShapeDtypeStruct((t, N_CHIPS * nn), BF16),
                          input_output_aliases={} if prev is None else {3: 0},
                          compiler_params=_cp(("parallel",) * 3))(kvec, h, w, *extra)


def _gmm_nn(name, p, w, out_dtype):
    t = p.shape[0]
    g, cg, dg = w.shape
    tm = _tile(t, 1024)
    pairs = [(p, pl.BlockSpec((tm, cg), lambda i, j: (i, j)), w, pl.BlockSpec((None, cg, dg), lambda i, j: (j, 0, 0)))]
    o_spec = pl.BlockSpec((tm, dg), lambda i, j: (i, j))
    return _mm(name, pairs, mode="nn", grid=(t // tm, g), out_shape=jax.ShapeDtypeStruct((t, g * dg), out_dtype),
               o_spec=o_spec)


def _gmm_nt(name, dy, w, out_dtype):
    t = dy.shape[0]
    g, cg, dg = w.shape
    tm = _tile(t, 1024)
    pairs = [(dy, pl.BlockSpec((tm, dg), lambda i, j: (i, j)), w, pl.BlockSpec((None, cg, dg), lambda i, j: (j, 0, 0)))]
    o_spec = pl.BlockSpec((tm, cg), lambda i, j: (i, j))
    return _mm(name, pairs, mode="nt", grid=(t // tm, g), out_shape=jax.ShapeDtypeStruct((t, g * cg), out_dtype),
               o_spec=o_spec)


def _gmm_tn(name, p, dy, g, out_dtype):
    t = p.shape[0]
    cg, dg = p.shape[1] // g, dy.shape[1] // g
    pairs = [(p, pl.BlockSpec((t, cg), lambda j: (0, j)), dy, pl.BlockSpec((t, dg), lambda j: (0, j)))]
    o_spec = pl.BlockSpec((None, cg, dg), lambda j: (j, 0, 0))
    return _mm(name, pairs, mode="tn", grid=(g,), out_shape=jax.ShapeDtypeStruct((g, cg, dg), out_dtype), o_spec=o_spec)


ROW_TILE = 256


def _rows(t):
    return _tile8(t, ROW_TILE)


def _tile8(n, pref):
    if n <= pref:
        return n
    for t in range(pref, 0, -8):
        if n % t == 0:
            return t
    raise ValueError(f"no row tile for {n}")


def _cast_place(name, w, pos, shard, deps=()):
    tr = _tile8(w.R, 512)
    if w.colshard:
        o_map = lambda h, i, pos: (0, h, i, pos[1])
    else:
        o_map = lambda h, i, pos: (pos[1], h, i, 0)

    def body(pos_ref, w_ref, *rest):
        rest[-1][...] = w_ref[...].astype(BF16)

    grid_spec = pltpu.PrefetchScalarGridSpec(
        num_scalar_prefetch=1, grid=(2, w.R // tr),
        in_specs=[pl.BlockSpec((None, tr, w.nn), lambda h, i, pos: (h, i, 0))] + [ANY] * len(deps),
        out_specs=pl.BlockSpec((None, None, tr, w.nn), o_map))
    return pl.pallas_call(body, name=name, grid_spec=grid_spec, out_shape=jax.ShapeDtypeStruct((w.P, 2, w.R, w.N), BF16),
                          compiler_params=_cp(("parallel", "parallel")))(pos, shard, *deps)


def _rms_fwd(name, x, g, deps=()):
    t, d = x.shape
    tm = _rows(t)

    def body(x_ref, g_ref, *rest):
        xf = x_ref[...]
        r = lax.rsqrt(jnp.mean(xf * xf, axis=-1, keepdims=True) + EPS)
        rest[-1][...] = (xf * r * g_ref[...]).astype(BF16)

    return pl.pallas_call(
        body, name=name, grid=(t // tm,),
        in_specs=[pl.BlockSpec((tm, d), lambda i: (i, 0)), pl.BlockSpec((1, d), lambda i: (0, 0))] + [ANY] * len(deps),
        out_specs=pl.BlockSpec((tm, d), lambda i: (i, 0)), out_shape=jax.ShapeDtypeStruct((t, d), BF16),
        compiler_params=_cp(("parallel",)),
    )(x, g, *deps)


def _rms_bwd(name, x, g, dh, dres, want_bf16, deps=()):
    t, d = x.shape
    tm = _rows(t)

    def body(x_ref, g_ref, dh_ref, dres_ref, *rest):
        rest = rest[len(deps):]
        dx_ref, rest = rest[0], rest[1:]
        dg_ref = rest[-1]
        xf = x_ref[...]
        r = lax.rsqrt(jnp.mean(xf * xf, axis=-1, keepdims=True) + EPS)
        xh = xf * r
        dhf = dh_ref[...].astype(F32)
        dxh = dhf * g_ref[...]
        m = jnp.mean(dxh * xh, axis=-1, keepdims=True)
        dx = dres_ref[...] + r * (dxh - xh * m)
        dx_ref[...] = dx
        if want_bf16:
            rest[0][...] = dx.astype(BF16)

        @pl.when(pl.program_id(0) == 0)
        def _():
            dg_ref[...] = jnp.zeros_like(dg_ref)

        dg_ref[...] += jnp.sum(dhf * xh, axis=0, keepdims=True)

    row = pl.BlockSpec((tm, d), lambda i: (i, 0))
    vec = pl.BlockSpec((1, d), lambda i: (0, 0))
    out_specs = [row] + ([row] if want_bf16 else []) + [vec]
    out_shape = ([jax.ShapeDtypeStruct((t, d), F32)] + ([jax.ShapeDtypeStruct((t, d), BF16)] if want_bf16 else [])
                 + [jax.ShapeDtypeStruct((1, d), F32)])
    return pl.pallas_call(body, name=name, grid=(t // tm,), in_specs=[row, vec, row, row] + [ANY] * len(deps),
                          out_specs=out_specs, out_shape=out_shape, compiler_params=_cp(("arbitrary",)))(x, g, dh, dres, *deps)


def _final_bwd(name, x3, gf, tgt):
    t, d = x3.shape
    tm = _rows(t)

    def body(x_ref, g_ref, t_ref, dx_ref, dxb_ref, dg_ref, lc_ref):
        xf = x_ref[...]
        g = g_ref[...]
        r = lax.rsqrt(jnp.mean(xf * xf, axis=-1, keepdims=True) + EPS)
        xh = xf * r
        diff = xh * g - t_ref[...]
        dy = diff * (1.0 / d)
        dxh = dy * g
        m = jnp.mean(dxh * xh, axis=-1, keepdims=True)
        dx = r * (dxh - xh * m)
        dx_ref[...] = dx
        dxb_ref[...] = dx.astype(BF16)

        @pl.when(pl.program_id(0) == 0)
        def _():
            dg_ref[...] = jnp.zeros_like(dg_ref)
            lc_ref[...] = jnp.zeros_like(lc_ref)

        dg_ref[...] += jnp.sum(dy * xh, axis=0, keepdims=True)
        lc_ref[...] += jnp.sum(diff * diff, axis=0, keepdims=True) * (0.5 / d)

    row = pl.BlockSpec((tm, d), lambda i: (i, 0))
    vec = pl.BlockSpec((1, d), lambda i: (0, 0))
    return pl.pallas_call(
        body, name=name, grid=(t // tm,), in_specs=[row, vec, row], out_specs=[row, row, vec, vec],
        out_shape=[jax.ShapeDtypeStruct((t, d), F32), jax.ShapeDtypeStruct((t, d), BF16),
                   jax.ShapeDtypeStruct((1, d), F32), jax.ShapeDtypeStruct((1, d), F32)],
        compiler_params=_cp(("arbitrary",)),
    )(x3, gf, tgt)


def _shift_down(v, k, t_idx):
    return jnp.where(t_idx >= k, pltpu.roll(v, k, 0), 0.0)


def _shift_up(v, k, t_idx):
    n = v.shape[0]
    return jnp.where(t_idx < n - k, pltpu.roll(v, n - k, 0), 0.0)


def _window_sums(v, shift, t_idx, grp):
    s = v + shift(v, 1, t_idx)
    out = s
    for lvl in range(1, len(POOL_WINDOWS)):
        s = s + shift(s, 1 << lvl, t_idx)
        out = jnp.where(grp >= lvl, s, out)
    return out


def _window_weight(t_idx, grp):
    return 1.0 / jnp.minimum(t_idx[:, :1] + 1, jnp.left_shift(2, grp)).astype(F32)


MIX_COLS = 256


def _mixer_fwd(name, proj, cw, cb, n_conv, n_groups, deps=()):
    t = proj.shape[0]
    nb = n_conv // MIX_COLS
    per_group = n_conv // n_groups // MIX_COLS

    def body(ba_ref, ca_ref, va_ref, vb_ref, cw_ref, cb_ref, *rest):
        z_ref, p_ref = rest[len(deps):]
        t_idx = lax.broadcasted_iota(jnp.int32, (t, MIX_COLS), 0)
        q = ca_ref[...].astype(F32) * va_ref[...].astype(F32)
        w = cw_ref[...]
        u = cb_ref[...] + w[0:1] * _shift_down(q, 2, t_idx) + w[1:2] * _shift_down(q, 1, t_idx) + w[2:3] * q
        z_ref[...] = (ba_ref[...].astype(F32) * u).astype(BF16)
        grp = pl.program_id(0) // per_group
        v = vb_ref[...].astype(F32)
        p_ref[...] = (_window_sums(v, _shift_down, t_idx, grp) * _window_weight(t_idx, grp) - v).astype(BF16)

    col = lambda s: pl.BlockSpec((t, MIX_COLS), lambda j: (0, s * nb + j))
    return pl.pallas_call(
        body, name=name, grid=(nb,),
        in_specs=[col(0), col(1), col(2), col(3), pl.BlockSpec((3, MIX_COLS), lambda j: (0, j)),
                  pl.BlockSpec((1, MIX_COLS), lambda j: (0, j))] + [ANY] * len(deps),
        out_specs=[col(0), col(0)],
        out_shape=[jax.ShapeDtypeStruct((t, n_conv), BF16), jax.ShapeDtypeStruct((t, n_conv), BF16)],
        compiler_params=_cp(("parallel",)),
    )(proj, proj, proj, proj, cw, cb, *deps)


def _mixer_bwd(name, dz, dp, proj, cw, cb, dproj, n_conv, n_groups, deps=()):
    t = proj.shape[0]
    nb = n_conv // MIX_COLS
    per_group = n_conv // n_groups // MIX_COLS

    def body(dz_ref, dp_ref, ba_ref, ca_ref, va_ref, cw_ref, cb_ref, _, *rest):
        o_ref, dcw_ref, dcb_ref, scr = rest[len(deps):]
        s = pl.program_id(1)

        @pl.when(s == 0)
        def _():
            t_idx = lax.broadcasted_iota(jnp.int32, (t, MIX_COLS), 0)
            ca, va = ca_ref[...].astype(F32), va_ref[...].astype(F32)
            q = ca * va
            q1, q2 = _shift_down(q, 1, t_idx), _shift_down(q, 2, t_idx)
            w = cw_ref[...]
            u = cb_ref[...] + w[0:1] * q2 + w[1:2] * q1 + w[2:3] * q
            dzf = dz_ref[...].astype(F32)
            du = dzf * ba_ref[...].astype(F32)
            scr[0] = (dzf * u).astype(BF16)
            dq = w[2:3] * du + w[1:2] * _shift_up(du, 1, t_idx) + w[0:1] * _shift_up(du, 2, t_idx)
            scr[1] = (dq * va).astype(BF16)
            scr[2] = (dq * ca).astype(BF16)
            dcb_ref[...] = jnp.sum(du, axis=0, keepdims=True)
            dcw_ref[0:1, :] = jnp.sum(du * q2, axis=0, keepdims=True)
            dcw_ref[1:2, :] = jnp.sum(du * q1, axis=0, keepdims=True)
            dcw_ref[2:3, :] = jnp.sum(du * q, axis=0, keepdims=True)
            grp = pl.program_id(0) // per_group
            dpf = dp_ref[...].astype(F32)
            e = dpf * _window_weight(t_idx, grp)
            scr[3] = (_window_sums(e, _shift_up, t_idx, grp) - dpf).astype(BF16)

        o_ref[...] = scr[s]

    col = lambda c: pl.BlockSpec((t, MIX_COLS), lambda j, s: (0, c * nb + j))
    own = pl.BlockSpec((t, MIX_COLS), lambda j, s: (0, j))
    return pl.pallas_call(
        body, name=name, grid=(nb, 4),
        in_specs=[own, own, col(0), col(1), col(2), pl.BlockSpec((3, MIX_COLS), lambda j, s: (0, j)),
                  pl.BlockSpec((1, MIX_COLS), lambda j, s: (0, j)), ANY] + [ANY] * len(deps),
        out_specs=[pl.BlockSpec((t, MIX_COLS), lambda j, s: (0, s * nb + j)),
                   pl.BlockSpec((3, MIX_COLS), lambda j, s: (0, j)), pl.BlockSpec((1, MIX_COLS), lambda j, s: (0, j))],
        out_shape=[jax.ShapeDtypeStruct(dproj.shape, BF16), jax.ShapeDtypeStruct((3, n_conv), F32),
                   jax.ShapeDtypeStruct((1, n_conv), F32)],
        scratch_shapes=[pltpu.VMEM((4, t, MIX_COLS), BF16)],
        input_output_aliases={7: 0},
        compiler_params=_cp(("arbitrary", "arbitrary")),
    )(dz, dp, proj, proj, proj, cw, cb, dproj, *deps)


def _merge_fwd(name, proj, bg, ya, yb, ps):
    t, d = ya.shape
    tm = _rows(t)

    def body(gab_ref, bg_ref, ya_ref, yb_ref, ps_ref, o_ref):
        gab = gab_ref[...].astype(F32) + bg_ref[...]
        sa, sb = jax.nn.sigmoid(gab[:, :d]), jax.nn.sigmoid(gab[:, d:])
        o_ref[...] = (sa * ya_ref[...].astype(F32) + sb * (yb_ref[...].astype(F32) * ps_ref[...])).astype(BF16)

    row = pl.BlockSpec((tm, d), lambda i: (i, 0))
    return pl.pallas_call(
        body, name=name, grid=(t // tm,),
        in_specs=[pl.BlockSpec((tm, 2 * d), lambda i: (i, 1)), pl.BlockSpec((1, 2 * d), lambda i: (0, 0)), row, row,
                  pl.BlockSpec((1, d), lambda i: (0, 0))],
        out_specs=row, out_shape=jax.ShapeDtypeStruct((t, d), BF16), compiler_params=_cp(("parallel",)),
    )(proj, bg, ya, yb, ps)


def _merge_bwd(name, dm, proj, bg, ya, yb, ps, deps=()):
    t, d = ya.shape
    tm = _rows(t)

    def body(dm_ref, gab_ref, bg_ref, ya_ref, yb_ref, ps_ref, *rest):
        dya_ref, dyb_ref, dg_ref, dba_ref, dbb_ref, dps_ref = rest[len(deps):]
        gab = gab_ref[...].astype(F32) + bg_ref[...]
        sa, sb = jax.nn.sigmoid(gab[:, :d]), jax.nn.sigmoid(gab[:, d:])
        dmf = dm_ref[...].astype(F32)
        ybf, ps_ = yb_ref[...].astype(F32), ps_ref[...]
        dya_ref[...] = (dmf * sa).astype(BF16)
        dyb = dmf * sb
        dyb_ref[...] = (dyb * ps_).astype(BF16)
        dga = dmf * ya_ref[...].astype(F32) * sa * (1.0 - sa)
        dgb = dmf * (ybf * ps_) * sb * (1.0 - sb)
        dg_ref[:, :d] = dga.astype(BF16)
        dg_ref[:, d:] = dgb.astype(BF16)

        @pl.when(pl.program_id(0) == 0)
        def _():
            dba_ref[...] = jnp.zeros_like(dba_ref)
            dbb_ref[...] = jnp.zeros_like(dbb_ref)
            dps_ref[...] = jnp.zeros_like(dps_ref)

        dba_ref[...] += jnp.sum(dga, axis=0, keepdims=True)
        dbb_ref[...] += jnp.sum(dgb, axis=0, keepdims=True)
        dps_ref[...] += jnp.sum(dyb * ybf, axis=0, keepdims=True)

    row = pl.BlockSpec((tm, d), lambda i: (i, 0))
    vec = pl.BlockSpec((1, d), lambda i: (0, 0))
    gates = pl.BlockSpec((tm, 2 * d), lambda i: (i, 1))
    return pl.pallas_call(
        body, name=name, grid=(t // tm,),
        in_specs=[row, gates, pl.BlockSpec((1, 2 * d), lambda i: (0, 0)), row, row, vec] + [ANY] * len(deps),
        out_specs=[row, row, gates, vec, vec, vec],
        out_shape=[jax.ShapeDtypeStruct((t, d), BF16), jax.ShapeDtypeStruct((t, d), BF16),
                   jax.ShapeDtypeStruct(proj.shape, BF16), jax.ShapeDtypeStruct((1, d), F32),
                   jax.ShapeDtypeStruct((1, d), F32), jax.ShapeDtypeStruct((1, d), F32)],
        compiler_params=_cp(("arbitrary",)),
    )(dm, proj, bg, ya, yb, ps, *deps)


def _ffn_up_act(name, h, w_up, gate, part=None, prev=None, deps=()):
    t, d = h.shape
    f = w_up.shape[1]
    tm, tf = _tile(t, 1024), _tile(f, 512)
    j0, j1 = _tile_span(f // tf, part)
    n_prev = 0 if prev is None else 2
    extra = ([] if prev is None else list(prev)) + list(deps)

    def body(h_ref, w_ref, g_ref, *rest):
        u_ref, a_ref = rest[len(extra):]
        u = lax.dot_general(h_ref[...], w_ref[...], _DIMS["nn"], preferred_element_type=F32)
        g = g_ref[...].astype(F32)
        u_ref[...] = u.astype(BF16)
        a_ref[...] = (g * jax.nn.sigmoid(g) * u).astype(BF16)

    blk = pl.BlockSpec((tm, tf), lambda i, j: (i, j0 + j))
    shp = jax.ShapeDtypeStruct((t, f), BF16)
    return pl.pallas_call(
        body, name=name, grid=(t // tm, j1 - j0),
        in_specs=[pl.BlockSpec((tm, d), lambda i, j: (i, 0)), pl.BlockSpec((d, tf), lambda i, j: (0, j0 + j)), blk]
        + [ANY] * len(extra),
        out_specs=[blk, blk], out_shape=[shp, shp], input_output_aliases={3 + i: i for i in range(n_prev)},
        compiler_params=_cp(("parallel", "parallel")))(h, w_up, gate, *extra)


WEIGHT_SLOTS = 3


def _ffn_bwd(name, dy, w_down, gate, up):
    t, d = dy.shape
    f = w_down.shape[0]
    tm, tf = _tile(t, 1024), _tile(f, 512)
    nj = f // tf
    steps = (t // tm) * nj

    def body(dy_ref, w_hbm, g_ref, u_ref, dg_ref, du_ref, w_buf, w_sem):
        step = pl.program_id(0) * nj + pl.program_id(1)

        def fetch(at):
            rows = pl.ds(pl.multiple_of((at % nj) * tf, tf), tf)
            return pltpu.make_async_copy(w_hbm.at[rows, :], w_buf.at[at % WEIGHT_SLOTS], w_sem.at[at % WEIGHT_SLOTS])

        @pl.when(step == 0)
        def _():
            for ahead in range(min(WEIGHT_SLOTS - 1, steps)):
                fetch(ahead).start()

        @pl.when(step + WEIGHT_SLOTS - 1 < steps)
        def _():
            fetch(step + WEIGHT_SLOTS - 1).start()

        fetch(step).wait()
        da = lax.dot_general(dy_ref[...], w_buf[step % WEIGHT_SLOTS], _DIMS["nt"], preferred_element_type=F32)
        g = g_ref[...].astype(F32)
        s = jax.nn.sigmoid(g)
        du_ref[...] = (da * (g * s)).astype(BF16)
        dg_ref[...] = (da * u_ref[...].astype(F32) * (s * (1.0 + g * (1.0 - s)))).astype(BF16)

    blk = pl.BlockSpec((tm, tf), lambda i, j: (i, j))
    shp = jax.ShapeDtypeStruct((t, f), BF16)
    return pl.pallas_call(
        body, name=name, grid=(t // tm, nj),
        in_specs=[pl.BlockSpec((tm, d), lambda i, j: (i, 0)), ANY, blk, blk],
        out_specs=[blk, blk], out_shape=[shp, shp],
        scratch_shapes=[pltpu.VMEM((WEIGHT_SLOTS, tf, d), BF16), pltpu.SemaphoreType.DMA((WEIGHT_SLOTS,))],
        compiler_params=_cp(("arbitrary", "arbitrary")))(dy, w_down, gate, up)


def _adamw_math(w, g, m, v):
    m = ADAM_B1 * m + (1.0 - ADAM_B1) * g
    v = ADAM_B2 * v + (1.0 - ADAM_B2) * (g * g)
    m_hat = m / (1.0 - ADAM_B1 ** ADAM_STEP)
    v_hat = v / (1.0 - ADAM_B2 ** ADAM_STEP)
    delta = -ADAM_LR * (m_hat / (jnp.sqrt(v_hat) + ADAM_EPS) + ADAM_WD * w)
    return delta, m, v


def _adamw(name, w, g, m, v):
    r, c = w.shape
    tr = _tile8(r, 512 if c <= 1024 else 256)

    def body(w_ref, g_ref, m_ref, v_ref, go_ref, d_ref, nm_ref, nv_ref):
        g = g_ref[...]
        go_ref[...] = g
        d_ref[...], nm_ref[...], nv_ref[...] = _adamw_math(w_ref[...], g, m_ref[...], v_ref[...])

    blk = pl.BlockSpec((tr, c), lambda i: (i, 0))
    shp = jax.ShapeDtypeStruct((r, c), F32)
    return pl.pallas_call(body, name=name, grid=(r // tr,), in_specs=[blk] * 4, out_specs=[blk] * 4,
                          out_shape=[shp] * 4, compiler_params=_cp(("parallel",)))(w, g, m, v)


class _Weight:
    def __init__(self, name, rows, cols, colshard):
        self.name, self.colshard = name, colshard
        self.R, self.nn = rows // 2, cols
        self.P = 1 if colshard else N_CHIPS
        self.N = N_CHIPS * cols if colshard else cols

    def cols(self, k):
        return pl.ds(pl.multiple_of(k * self.nn, LANES), self.nn)

    def half(self, ref, k, h):
        return ref.at[0, h, :, self.cols(k)] if self.colshard else ref.at[k, h]

    def quarter(self, ref, k, h, q):
        return self.half(ref, k, h).at[pl.ds(q * (self.R // 2), self.R // 2), :]

    def part(self, ref, k):
        return ref.at[0, :, self.cols(k)] if self.colshard else ref.at[k]


def _remote(src, dst, ssem, rsem, dev):
    return pltpu.make_async_remote_copy(src_ref=src, dst_ref=dst, send_sem=ssem, recv_sem=rsem, device_id=dev,
                                        device_id_type=MESH)


def _other_chips(x, y):
    chips = [(1 - x, y), (x, 1 - y), (1 - x, 1 - y)]
    return chips, [2 * cx + cy for cx, cy in chips]


def _hbm(a):
    return pltpu.with_memory_space_constraint(a, pltpu.HBM)


def _split_start(name, arrays, sets, after=()):
    na, ns = len(arrays), len(sets)

    def body(*refs):
        outs = refs[na + len(after):]
        for s_, (idx, copies) in enumerate(sets):
            for i, (src, dst, dev, _) in enumerate(copies([refs[k] for k in idx], *_mesh_pos())):
                _remote(src, dst, outs[2 * s_].at[i], outs[2 * s_ + 1].at[i], dev).start()
        outs[2 * ns + na][...] = jnp.zeros((8, LANES), F32)

    sems = []
    for _, copies in sets:
        sems += [pltpu.SemaphoreType.DMA((copies.n,))] * 2
    out = pl.pallas_call(
        body, name=name, in_specs=[HBM] * na + [ANY] * len(after), out_specs=[SEM] * (2 * ns) + [HBM] * na + [VMEM],
        out_shape=sems + [pltpu.HBM(a.shape, a.dtype) for a in arrays] + [jax.ShapeDtypeStruct((8, LANES), F32)],
        input_output_aliases={i: 2 * ns + i for i in range(na)},
        compiler_params=pltpu.CompilerParams(has_side_effects=EFFECT),
    )(*[_hbm(a) for a in arrays], *after)
    return [(out[2 * i], out[2 * i + 1]) for i in range(ns)], list(out[2 * ns:2 * ns + na]), out[-1]


def _split_wait(name, arrays, ssem, rsem, copies, after):
    na = len(arrays)

    def body(*refs):
        for i, (src, _, dev, dst) in enumerate(copies(refs[:na], *_mesh_pos())):
            cp = _remote(src, dst, refs[na].at[i], refs[na + 1].at[i], dev)
            cp.wait_send()
            cp.wait_recv()

    return list(pl.pallas_call(
        body, name=name, in_specs=[HBM] * na + [SEM, SEM] + [ANY] * len(after), out_specs=[HBM] * na,
        out_shape=[pltpu.HBM(a.shape, a.dtype) for a in arrays], input_output_aliases={i: i for i in range(na)},
        compiler_params=pltpu.CompilerParams(has_side_effects=EFFECT),
    )(*arrays, ssem, rsem, *after))


def _pass_copies(grp, rels=(0, 1, 2)):
    def copies(land, x, y, c):
        _, ks = _other_chips(x, y)
        return [(w.half(land[wi], ks[j], c), w.half(land[wi], ks[j], c), (x, y, 1 - c), w.half(land[wi], ks[j], 1 - c))
                for wi, w in enumerate(grp) for j in rels]
    copies.n = len(grp) * len(rels)
    return copies


def _direct_copies(grp):
    def copies(land, x, y, c):
        chips, ks = _other_chips(x, y)
        out = []
        for wi, w in enumerate(grp):
            mine = w.half(land[wi], 2 * x + y, c)
            out += [(mine, mine, (*chips[j], c), w.half(land[wi], ks[j], c)) for j in range(3)]
        return out
    copies.n = 3 * len(grp)
    return copies


def _near_copies(grp):
    def copies(land, x, y, c):
        chips, ks = _other_chips(x, y)
        out = []
        for wi, w in enumerate(grp):
            mine = w.half(land[wi], 2 * x + y, c)
            out += [(mine, mine, (*chips[j], c), w.half(land[wi], ks[j], c)) for j in (0, 1)]
        return out
    copies.n = 2 * len(grp)
    return copies


def _far_copies(grp):
    def copies(land, x, y, c):
        chips, ks = _other_chips(x, y)
        out = []
        for wi, w in enumerate(grp):
            for j in (0, 1):
                q = w.quarter(land[wi], ks[j], c, j)
                out.append((q, q, (*chips[1 - j], c), w.quarter(land[wi], ks[2], c, j)))
        return out
    copies.n = 2 * len(grp)
    return copies


def _pair_copies(n, whole=False):
    def copies(refs, x, y, c):
        return [(refs[i] if whole else refs[i].at[:, 1 - c], refs[n + i], (x, y, 1 - c), refs[n + i]) for i in range(n)]
    copies.n = n
    return copies


def _share_copies(n):
    def copies(refs, x, y, c):
        return [(refs[i].at[c], refs[i].at[c], (x, y, 1 - c), refs[i].at[1 - c]) for i in range(n)]
    copies.n = n
    return copies


def _gather_conv_w(cw, thru):
    ncw = cw.shape[1]

    def body(cw_ref, _, out_ref, __, ssem, rsem):
        x, y, c = _mesh_pos()
        k_me = 2 * x + y
        chips, ks = _other_chips(x, y)
        cols = lambda k: out_ref.at[:, pl.ds(pl.multiple_of(k * ncw, LANES), ncw)]
        cps = [_remote(cw_ref, cols(k_me), ssem.at[j], rsem.at[j], (*chip, c)) for j, chip in enumerate(chips)]
        for cp in cps:
            cp.start()
        for k in range(N_CHIPS):
            @pl.when(k_me == k)
            def _():
                out_ref[:, k * ncw:(k + 1) * ncw] = cw_ref[...]
        for j in range(3):
            _remote(cw_ref, cols(ks[j]), ssem.at[j], rsem.at[j], (*chips[j], c)).wait_recv()
        for cp in cps:
            cp.wait_send()

    return pl.pallas_call(
        body, name="gather_conv_w", in_specs=[VMEM, ANY], out_specs=[VMEM, ANY],
        out_shape=[jax.ShapeDtypeStruct((3, N_CHIPS * ncw), F32), jax.ShapeDtypeStruct(thru.shape, thru.dtype)],
        scratch_shapes=[pltpu.SemaphoreType.DMA((3,)), pltpu.SemaphoreType.DMA((3,))],
        input_output_aliases={1: 1},
    )(cw, thru)


def _grad_tiles(w, n):
    return _tile8(w.R, 512) if w.R <= 512 else w.R // 2, _tile(n, 2048)


def _pair_sum(name, w, pos, grad, got):
    tr, tn = _grad_tiles(w, w.N)

    def body(pos_ref, g_ref, r_ref, o_ref):
        o_ref[...] = (g_ref[...].astype(F32) + r_ref[...].astype(F32)).astype(BF16)

    blk = pl.BlockSpec((None, tr, tn), lambda p, i, j, pos: (p, i, j))
    grid_spec = pltpu.PrefetchScalarGridSpec(
        num_scalar_prefetch=1, grid=(w.P, w.R // tr, w.N // tn),
        in_specs=[pl.BlockSpec((None, None, tr, tn), lambda p, i, j, pos: (p, pos[0], i, j)), blk], out_specs=blk)
    return pl.pallas_call(body, name=name, grid_spec=grid_spec, out_shape=jax.ShapeDtypeStruct((w.P, w.R, w.N), BF16),
                          compiler_params=_cp(("parallel",) * 3))(pos, grad, got)


def _scatter_start(name, ws, pairs):
    nw = len(ws)

    def body(*refs):
        pr, land = refs[:nw], refs[nw:2 * nw]
        ssem, rsem = refs[2 * nw], refs[2 * nw + 1]
        token = refs[4 * nw + 2]
        x, y, c = _mesh_pos()
        chips, ks = _other_chips(x, y)
        for i, w in enumerate(ws):
            for j, chip in enumerate(chips):
                _remote(w.part(pr[i], ks[j]), land[i].at[j], ssem.at[3 * i + j], rsem.at[3 * i + j], (*chip, c)).start()
        token[...] = jnp.zeros_like(token)

    lands = [lax.empty((3, w.R, w.nn), BF16) for w in ws]
    out = pl.pallas_call(
        body, name=name, in_specs=[HBM] * (2 * nw),
        out_specs=[SEM, SEM] + [HBM] * (2 * nw) + [VMEM],
        out_shape=[pltpu.SemaphoreType.DMA((3 * nw,))] * 2 + [pltpu.HBM(a.shape, a.dtype) for a in pairs + lands]
        + [jax.ShapeDtypeStruct((8, LANES), F32)],
        input_output_aliases={i: 2 + i for i in range(2 * nw)},
        compiler_params=pltpu.CompilerParams(has_side_effects=EFFECT),
    )(*[_hbm(a) for a in pairs + lands])
    return out[0], out[1], list(out[2:2 + nw]), list(out[2 + nw:2 + 2 * nw]), out[-1]


def _scatter_wait(name, ws, pairs, lands, ssem, rsem, after):
    nw = len(ws)

    def body(*refs):
        pr, land = refs[:nw], refs[nw:2 * nw]
        ssem_ref, rsem_ref = refs[2 * nw], refs[2 * nw + 1]
        x, y, c = _mesh_pos()
        chips, ks = _other_chips(x, y)
        for i, w in enumerate(ws):
            for j, chip in enumerate(chips):
                cp = _remote(w.part(pr[i], ks[j]), land[i].at[j], ssem_ref.at[3 * i + j], rsem_ref.at[3 * i + j], (*chip, c))
                cp.wait_send()
                cp.wait_recv()

    out = pl.pallas_call(
        body, name=name, in_specs=[HBM] * (2 * nw) + [SEM, SEM] + [ANY] * len(after), out_specs=[HBM] * (2 * nw),
        out_shape=[pltpu.HBM(a.shape, a.dtype) for a in pairs + lands],
        input_output_aliases={i: i for i in range(2 * nw)},
        compiler_params=pltpu.CompilerParams(has_side_effects=EFFECT),
    )(*pairs, *lands, ssem, rsem, *after)
    return list(out[:nw]), list(out[nw:])


def _final_sum(name, w, pos, grad, got, parts):
    tr, tn = _grad_tiles(w, w.nn)
    nbc = w.nn // tn
    if got is None:
        return _final_sum_pair(name, w, pos, grad, parts, tr, tn)

    def body(pos_ref, g_ref, r_ref, p_ref, o_ref):
        acc = g_ref[...].astype(F32) + r_ref[...].astype(F32)
        for j in range(3):
            acc = acc + p_ref[j].astype(F32)
        o_ref[...] = acc

    if w.colshard:
        g_spec = pl.BlockSpec((None, None, tr, tn), lambda i, j, pos: (0, pos[0], i, pos[1] * nbc + j))
        r_spec = pl.BlockSpec((None, tr, tn), lambda i, j, pos: (0, i, pos[1] * nbc + j))
    else:
        g_spec = pl.BlockSpec((None, None, tr, tn), lambda i, j, pos: (pos[1], pos[0], i, j))
        r_spec = pl.BlockSpec((None, tr, tn), lambda i, j, pos: (pos[1], i, j))
    grid_spec = pltpu.PrefetchScalarGridSpec(
        num_scalar_prefetch=1, grid=(w.R // tr, nbc),
        in_specs=[g_spec, r_spec, pl.BlockSpec((3, tr, tn), lambda i, j, pos: (0, i, j))],
        out_specs=pl.BlockSpec((None, tr, tn), lambda i, j, pos: (pos[0], i, j)))
    return pl.pallas_call(body, name=name, grid_spec=grid_spec, out_shape=jax.ShapeDtypeStruct((2, w.R, w.nn), F32),
                          compiler_params=_cp(("parallel",) * 2))(pos, grad, got, parts)


def _final_sum_pair(name, w, pos, pair, parts, tr, tn):
    nbc = w.nn // tn

    def body(pos_ref, g_ref, p_ref, o_ref):
        acc = g_ref[...].astype(F32)
        for j in range(3):
            acc = acc + p_ref[j].astype(F32)
        o_ref[...] = acc

    if w.colshard:
        g_spec = pl.BlockSpec((None, tr, tn), lambda i, j, pos: (0, i, pos[1] * nbc + j))
    else:
        g_spec = pl.BlockSpec((None, tr, tn), lambda i, j, pos: (pos[1], i, j))
    grid_spec = pltpu.PrefetchScalarGridSpec(
        num_scalar_prefetch=1, grid=(w.R // tr, nbc),
        in_specs=[g_spec, pl.BlockSpec((3, tr, tn), lambda i, j, pos: (0, i, j))],
        out_specs=pl.BlockSpec((None, tr, tn), lambda i, j, pos: (pos[0], i, j)))
    return pl.pallas_call(body, name=name, grid_spec=grid_spec, out_shape=jax.ShapeDtypeStruct((2, w.R, w.nn), F32),
                          compiler_params=_cp(("parallel",) * 2))(pos, pair, parts)


VEC_ROWS = 16


def _vector_sum(d, n_conv, parts, deps=()):
    def body(*refs):
        dg1, dba, dbb, dcw, dcb, dps, dg2, dgf, lc = refs[:9]
        tot_ref, snd, got, ssem, rsem = refs[9 + len(deps):]
        x, y, c = _mesh_pos()
        me = 4 * x + 2 * y + c
        snd[...] = jnp.zeros_like(snd)
        for row, ref in ((0, dg1), (1, dba), (2, dbb), (3, dps), (4, dg2), (5, dgf), (6, lc)):
            snd[row:row + 1, :] = ref[...]
        snd[7:8, :n_conv] = dcb[...]
        snd[8:11, :n_conv] = dcw[...]
        cps = []
        for r in range(1, N_DEV):
            peer = tuple(1 - p if (r >> b) & 1 else p for p, b in ((x, 2), (y, 1), (c, 0)))
            cps.append(_remote(snd, got.at[me], ssem.at[r - 1], rsem.at[r - 1], peer))
        for cp in cps:
            cp.start()
        got[me] = snd[...]
        for r in range(1, N_DEV):
            peer = tuple(1 - p if (r >> b) & 1 else p for p, b in ((x, 2), (y, 1), (c, 0)))
            _remote(snd, got.at[4 * peer[0] + 2 * peer[1] + peer[2]], ssem.at[r - 1], rsem.at[r - 1], peer).wait_recv()
        for cp in cps:
            cp.wait_send()
        tot = got[0]
        for dev in range(1, N_DEV):
            tot = tot + got[dev]
        tot_ref[...] = tot

    return pl.pallas_call(
        body, name="vector_params_sum", in_specs=[VMEM] * len(parts) + [ANY] * len(deps), out_specs=VMEM,
        out_shape=jax.ShapeDtypeStruct((VEC_ROWS, d), F32),
        scratch_shapes=[pltpu.VMEM((VEC_ROWS, d), F32), pltpu.VMEM((N_DEV, VEC_ROWS, d), F32),
                        pltpu.SemaphoreType.DMA((N_DEV - 1,)), pltpu.SemaphoreType.DMA((N_DEV - 1,))],
        compiler_params=pltpu.CompilerParams(vmem_limit_bytes=VMEM_LIMIT),
    )(*parts, *deps)


def _vector_update(tot, n_conv, params):
    ncw = params[2][0].shape[1]
    n_par = len(params)

    def body(*refs):
        tot = refs[0][...]
        wmv = refs[1:1 + 3 * n_par]
        outs = refs[1 + 3 * n_par:1 + 7 * n_par]
        refs[1 + 7 * n_par][...] = jnp.sum(tot[6:7, :], axis=1, keepdims=True)
        k_me = 2 * lax.axis_index("x") + lax.axis_index("y")
        g_cw = jnp.zeros((3, ncw), F32)
        for k in range(N_CHIPS):
            g_cw = g_cw + jnp.where(k_me == k, tot[8:11, k * ncw:(k + 1) * ncw], 0.0)
        grads = [tot[0:1, :], jnp.concatenate([tot[1:2, :], tot[2:3, :]], axis=1), g_cw, tot[7:8, :n_conv],
                 tot[3:4, :], tot[4:5, :], tot[5:6, :]]
        for i, g in enumerate(grads):
            w_ref, m_ref, v_ref = wmv[3 * i:3 * i + 3]
            delta, nm, nv = _adamw_math(w_ref[...], g, m_ref[...], v_ref[...])
            outs[4 * i][...] = g
            outs[4 * i + 1][...] = delta
            outs[4 * i + 2][...] = nm
            outs[4 * i + 3][...] = nv

    args = [tot]
    out_shape = []
    for w, m, v in params:
        args += [w, m, v]
        out_shape += [jax.ShapeDtypeStruct(w.shape, F32)] * 4
    out_shape.append(jax.ShapeDtypeStruct((1, 1), F32))
    return pl.pallas_call(body, name="vector_params_update", in_specs=[VMEM] * len(args), out_specs=[VMEM] * len(out_shape),
                          out_shape=out_shape, compiler_params=pltpu.CompilerParams(vmem_limit_bytes=VMEM_LIMIT))(*args)


def kernel(x, norm1_g, w_in, b_gate, conv_w, conv_b, w_a_out, w_pool, pool_scale, w_o, norm2_g, w_ffn_gate, w_ffn_up, w_ffn_down, final_g, loss_target, m_norm1_g, m_w_in, m_b_gate, m_conv_w, m_conv_b, m_w_a_out, m_w_pool, m_pool_scale, m_w_o, m_norm2_g, m_w_ffn_gate, m_w_ffn_up, m_w_ffn_down, m_final_g, v_norm1_g, v_w_in, v_b_gate, v_conv_w, v_conv_b, v_w_a_out, v_w_pool, v_pool_scale, v_w_o, v_norm2_g, v_w_ffn_gate, v_w_ffn_up, v_w_ffn_down, v_final_g):
    t, d = x.shape[1], x.shape[2]
    n_conv = conv_b.shape[1]
    n_groups, pool_cg, pool_dg = w_pool.shape[1], w_pool.shape[2], N_CHIPS * w_pool.shape[3]
    d_ff = N_CHIPS * w_ffn_gate.shape[2]
    assert n_conv // n_groups == pool_cg and n_conv % (n_groups * MIX_COLS) == 0 and n_groups == len(POOL_WINDOWS)

    big = {"w_in": (w_in, m_w_in, v_w_in), "w_a_out": (w_a_out, m_w_a_out, v_w_a_out), "w_pool": (w_pool, m_w_pool, v_w_pool),
           "w_o": (w_o, m_w_o, v_w_o), "w_ffn_gate": (w_ffn_gate, m_w_ffn_gate, v_w_ffn_gate),
           "w_ffn_up": (w_ffn_up, m_w_ffn_up, v_w_ffn_up), "w_ffn_down": (w_ffn_down, m_w_ffn_down, v_w_ffn_down)}
    colshard = {"w_in": True, "w_a_out": True, "w_pool": True, "w_o": False, "w_ffn_gate": True, "w_ffn_up": True,
                "w_ffn_down": False}
    names = list(big)
    shard2d = {n: big[n][0].reshape(-1, big[n][0].shape[-1]) for n in names}
    ws = [_Weight(n, *shard2d[n].shape, colshard[n]) for n in names]

    xs, tgt = x[0], loss_target[0]
    cw_loc = conv_w[0]
    pos = jnp.stack([lax.axis_index("c"), 2 * lax.axis_index("x") + lax.axis_index("y")]).astype(jnp.int32)
    by_name = {w.name: w for w in ws}
    groups = [[by_name[n] for n in g] for g in (["w_in"], ["w_a_out", "w_pool", "w_o"], ["w_ffn_gate"], ["w_ffn_up"],
                                                 ["w_ffn_down"])]
    rgroups = [groups[0], groups[1], groups[2] + groups[3], groups[4]]

    cast = lambda w, dep: _cast_place(f"cast_{w.name}", w, pos, shard2d[w.name].reshape(2, w.R, w.nn), deps=dep)
    chips, ks = _other_chips(lax.axis_index("x"), lax.axis_index("y"))
    kvec = jnp.stack([pos[1], *ks]).astype(jnp.int32)
    full = {}

    def start(name, arrays, copies, after=()):
        return start_many([(name, arrays, copies)], after)[0]

    def start_many(parts, after=()):
        arrays, sets = [], []
        for _, arrs, copies in parts:
            for a in arrs:
                if not any(a is b for b in arrays):
                    arrays.append(a)
            sets.append(([next(i for i, b in enumerate(arrays) if b is a) for a in arrs], copies))
        sems, thru, token = _split_start("_".join(p[0] for p in parts), arrays, sets, after)
        return [(name, [thru[i] for i in idx], ssem, rsem, copies, token)
                for (name, _, copies), (idx, _), (ssem, rsem) in zip(parts, sets, sems)]

    def wait(started, after):
        name, arrays, ssem, rsem, copies, _ = started
        return _split_wait(name + "_wait", arrays, ssem, rsem, copies, after)

    def pass_on(g, got, after=()):
        return start(f"pass_{g}", got, _pass_copies(groups[g]), after)

    def passed(g, st, after=None):
        got = wait(st, [st[5]] if after is None else after)
        full.update({w.name: a.reshape(w.P * 2 * w.R, w.N) for w, a in zip(groups[g], got)})

    near = start("near_0", [cast(w, []) for w in groups[0]], _near_copies(groups[0]))
    rest = [cast(w, [near[5]]) for grp in groups[1:] for w in grp]
    h1 = _rms_fwd("norm1_fwd", xs, norm1_g, deps=[near[5]])
    proj = _proj_piece("proj_own", h1, shard2d["w_in"], None, kvec, 0, 1, deps=rest)
    got = wait(near, [proj])
    far, small, st = start_many([("far_0", got, _far_copies(groups[0])), ("direct_1", rest[:3], _direct_copies(groups[1])),
                                 ("pass_near_0", got, _pass_copies(groups[0], (0, 1)))])
    got = wait(st, [st[5]])
    proj = _proj_piece("proj_near", h1, got[0].reshape(-1, groups[0][0].N), proj, kvec, 1, 2)
    st = start("pass_far_0", wait((far[0], got) + far[2:], [proj]), _pass_copies(groups[0], (2,)))
    got = wait(st, [st[5]])
    w_in_full = got[0].reshape(-1, groups[0][0].N)
    proj = _proj_piece("proj_far", h1, w_in_full, proj, kvec, 3, 1)
    cw_full, proj = _gather_conv_w(cw_loc, proj)
    got = wait(small, [proj])
    near_g = start("near_2", rest[3:4], _near_copies(groups[2]), got)
    st = pass_on(1, got, [near_g[5]])
    z, p = _mixer_fwd("mixer_fwd", proj, cw_full, conv_b, n_conv, n_groups, deps=[st[5]])
    passed(1, st, [z])
    wp_full = full["w_pool"].reshape(n_groups, pool_cg, pool_dg)
    ya = _mm_nn("conv_out", z, full["w_a_out"], BF16)
    yb = _gmm_nn("pool_out", p, wp_full, BF16)
    merged = _merge_fwd("merge_fwd", proj, b_gate, ya, yb, pool_scale)
    far_g, near_u = start_many([("far_2", wait(near_g, [merged]), _far_copies(groups[2])),
                                ("near_3", rest[4:5], _near_copies(groups[3]))])
    x2 = _mm_nn("mix_out", merged, full["w_o"], F32, add=xs, deps=[near_u[5]])
    st = pass_on(2, wait(far_g, [x2]))
    h2 = _rms_fwd("norm2_fwd", x2, norm2_g, deps=[st[5]])
    passed(2, st, [h2])
    def far_and_pass(g, near_st, after, more=()):
        got = wait(near_st, after)
        return start_many([(f"far_{g}", got, _far_copies(groups[g])), (f"pass_near_{g}", got, _pass_copies(groups[g], (0, 1))),
                           *more])

    def finish(g, far_st, pass_st, after):
        got = wait(pass_st, after)
        st = start(f"pass_far_{g}", wait((far_st[0], got) + far_st[2:], after), _pass_copies(groups[g], (2,)))
        passed(g, st)

    gate = _mm_nn("ffn_gate_a", h2, full["w_ffn_gate"], BF16, part=(0, 2))
    far_u, pass_u, near_d = far_and_pass(3, near_u, [gate], [("near_4", rest[5:6], _near_copies(groups[4]))])
    gate = _mm_nn("ffn_gate_b", h2, full["w_ffn_gate"], BF16, part=(1, 2), prev=gate, deps=[near_d[5]])
    finish(3, far_u, pass_u, [gate])
    up_act = _ffn_up_act("ffn_up_act_a", h2, full["w_ffn_up"], gate, part=(0, 2))
    far_d, pass_d = far_and_pass(4, near_d, [up_act[0]])
    up, act = _ffn_up_act("ffn_up_act_b", h2, full["w_ffn_up"], gate, part=(1, 2), prev=up_act, deps=[pass_d[5]])
    finish(4, far_d, pass_d, [act])
    x3 = _mm_nn("ffn_down", act, full["w_ffn_down"], F32, add=x2, tiles=(1024, 512))

    pending = {}

    def pair_start(g, grads):
        grp = rgroups[g]
        gcan = [grads[w.name].reshape(w.P, 2, w.R, w.N) for w in grp]
        slots = [lax.empty((w.P, w.R, w.N), BF16) for w in grp]
        pending[g] = start(f"pair_start_{g}", gcan + slots, _pair_copies(len(grp)))
        return pending[g][5]

    def scatter_start(g, after):
        grp = rgroups[g]
        n = len(grp)
        arrs = wait(pending[g], after)
        gcan, sib = arrs[:n], arrs[n:]
        pairs = [_pair_sum(f"pair_sum_{w.name}", w, pos, a, s) for w, a, s in zip(grp, gcan, sib)]
        ssem, rsem, pairs, slots, token = _scatter_start(f"scatter_start_{g}", grp, pairs)
        pending[g] = (gcan, sib, pairs, slots, ssem, rsem)
        return token

    def pair_start_halves(g, ab, deps):
        grp = rgroups[g]
        sent = [_mm_tn_half(f"d{w.name}_sib", a, b, pos, False, deps=deps if i == 0 else ()) for i, (w, (a, b)) in enumerate(zip(grp, ab))]
        slots = [lax.empty((1, w.R, w.N), BF16) for w in grp]
        pending[g] = start(f"pair_start_{g}", sent + slots, _pair_copies(len(grp), whole=True))
        return pending[g][5]

    def scatter_start_halves(g, ab, after):
        grp = rgroups[g]
        n = len(grp)
        arrs = wait(pending[g], after)
        pairs = [_mm_tn_half(f"d{w.name}_own", a, b, pos, True, add=s) for w, (a, b), s in zip(grp, ab, arrs[n:])]
        ssem, rsem, pairs, slots, token = _scatter_start(f"scatter_start_{g}", grp, pairs)
        pending[g] = (None, None, pairs, slots, ssem, rsem)
        return token

    def reduce_finish(g, after):
        grp = rgroups[g]
        gcan, sib, pairs, slots, ssem, rsem = pending[g]
        pairs, parts = _scatter_wait(f"scatter_wait_{g}", grp, pairs, slots, ssem, rsem, after)
        if gcan is None:
            return [_final_sum(f"final_sum_{w.name}", w, pos, a, None, q) for w, a, q in zip(grp, pairs, parts)]
        return [_final_sum(f"final_sum_{w.name}", w, pos, a, s, q) for w, a, s, q in zip(grp, gcan, sib, parts)]

    grads = {}
    dx3, dx3b, d_gf, loss_cols = _final_bwd("final_bwd", x3, final_g.reshape(1, d), tgt)
    dgate, dup = _ffn_bwd("ffn_bwd", dx3b, full["w_ffn_down"], gate, up)
    grads["w_ffn_down"] = _mm_tn("dw_ffn_down", act, dx3b, BF16)
    tok = pair_start(3, grads)
    dh2 = _mm_nt("d_h2", [(dgate, full["w_ffn_gate"]), (dup, full["w_ffn_up"])], BF16, tk=d_ff // 4, deps=[tok])
    tok = scatter_start(3, [dh2])
    tok = pair_start_halves(2, [(h2, dgate), (h2, dup)], [tok])
    dx2, dx2b, d_g2 = _rms_bwd("norm2_bwd", x2, norm2_g, dh2, dx3, True, deps=[tok])
    dmerged = _mm_nt("d_merged", [(dx2b, full["w_o"])], BF16, tk=d)
    grads["w_o"] = _mm_tn("dw_o", merged, dx2b, BF16)
    tok = scatter_start_halves(2, [(h2, dgate), (h2, dup)], [grads["w_o"]])
    dya, dyb, dproj, d_bga, d_bgb, d_ps = _merge_bwd("merge_bwd", dmerged, proj, b_gate, ya, yb, pool_scale, deps=[tok])
    dz = _mm_nt("d_z", [(dya, full["w_a_out"])], BF16, tk=d)
    grads["w_a_out"] = _mm_tn("dw_a_out", z, dya, BF16)
    dp = _gmm_nt("d_pool", dyb, wp_full, BF16)
    grads["w_pool"] = _gmm_tn("dw_pool", p, dyb, n_groups, BF16)
    tok = pair_start(1, grads)
    dproj, d_cw, d_cb = _mixer_bwd("mixer_bwd", dz, dp, proj, cw_full, conv_b, dproj, n_conv, n_groups, deps=[tok])
    tok = scatter_start(1, [dproj])
    tok = pair_start_halves(0, [(h1, dproj)], [tok])
    dh1 = _mm_nt("d_h1", [(dproj, w_in_full)], BF16, tk=proj.shape[1] // 4, deps=[tok])
    tok = scatter_start_halves(0, [(h1, dproj)], [dh1])
    grad_x, d_g1 = _rms_bwd("norm1_bwd", xs, norm1_g, dh1, dx2, False, deps=[tok])

    g_big, d_big, m_big, v_big = {}, {}, {}, {}

    def update(wsub, shared):
        out = []
        for w, g in zip(wsub, shared):
            wt, mt, vt = big[w.name]
            g2 = g.reshape(2 * w.R, w.nn)
            go, dl, nm, nv = _adamw(f"adamw_{w.name}", shard2d[w.name], g2, mt.reshape(g2.shape), vt.reshape(g2.shape))
            g_big[w.name], d_big[w.name], m_big[w.name], v_big[w.name] = (a.reshape(wt.shape) for a in (go, dl, nm, nv))
            out.append(nv)
        return out

    after = [grad_x]
    started = []
    for g in (3, 2, 1):
        halves = reduce_finish(g, after)
        started.append((g, start(f"share_{g}", halves, _share_copies(len(halves)))))
        after = [started[-1][1][5]]
    for g, st in started:
        after = update(rgroups[g], wait(st, after))
    st = start("share_0", reduce_finish(0, after), _share_copies(1))

    vec_names = ["norm1_g", "b_gate", "conv_w", "conv_b", "pool_scale", "norm2_g", "final_g"]
    vec = {"norm1_g": (norm1_g, m_norm1_g, v_norm1_g), "b_gate": (b_gate, m_b_gate, v_b_gate),
           "conv_w": (cw_loc, m_conv_w[0], v_conv_w[0]), "conv_b": (conv_b, m_conv_b, v_conv_b),
           "pool_scale": (pool_scale, m_pool_scale, v_pool_scale), "norm2_g": (norm2_g, m_norm2_g, v_norm2_g),
           "final_g": tuple(a.reshape(1, d) for a in (final_g, m_final_g, v_final_g))}
    tot = _vector_sum(d, n_conv, [d_g1, d_bga, d_bgb, d_cw, d_cb, d_ps, d_g2, d_gf, loss_cols], deps=st[1])
    vout = _vector_update(tot, n_conv, [vec[n] for n in vec_names])
    update(rgroups[0], wait(st, []))

    shapes = {"conv_w": conv_w.shape, "final_g": final_g.shape}
    g_vec, d_vec, m_vec, v_vec = ({n: vout[4 * i + q].reshape(shapes.get(n, vec[n][0].shape)) for i, n in enumerate(vec_names)}
                                  for q in range(4))
    loss = vout[-1].reshape(())

    order = ["norm1_g", "w_in", "b_gate", "conv_w", "conv_b", "w_a_out", "w_pool", "pool_scale", "w_o", "norm2_g",
             "w_ffn_gate", "w_ffn_up", "w_ffn_down", "final_g"]
    pick = lambda vecs, bigs: [vecs[n] if n in vecs else bigs[n] for n in order]
    return (loss, grad_x.reshape(x.shape), *pick(g_vec, g_big), *pick(d_vec, d_big), *pick(m_vec, m_big),
            *pick(v_vec, v_big))
```

```python
import jax
import jax.numpy as jnp
from jax import lax
from jax.experimental import pallas as pl
from jax.experimental.pallas import tpu as pltpu

F32, BF16 = jnp.float32, jnp.bfloat16
MESH = pl.DeviceIdType.MESH
ANY = pl.BlockSpec(memory_space=pl.ANY)
VMEM = pl.BlockSpec(memory_space=pltpu.VMEM)
HBM = pl.BlockSpec(memory_space=pltpu.HBM)
SEM = pl.BlockSpec(memory_space=pltpu.SEMAPHORE)
EFFECT = pltpu.SideEffectType.DATAFLOW_SIDE_EFFECTING

EPS = 1e-6
POOL_WINDOWS = (2, 4, 8, 16)
ADAM_LR, ADAM_B1, ADAM_B2, ADAM_EPS, ADAM_WD, ADAM_STEP = 0.001, 0.9, 0.999, 1e-08, 0.01, 10

V7X_VMEM_BYTES = 64 * 1024 * 1024
VMEM_LIMIT = V7X_VMEM_BYTES * 3 // 4
LANES = 128
COL_TILE = 8 * LANES
N_CHIPS = 4
N_DEV = 8

_DIMS = {
    "nn": (((1,), (0,)), ((), ())),
    "nt": (((1,), (1,)), ((), ())),
    "tn": (((0,), (0,)), ((), ())),
}


def _cp(sem):
    return pltpu.CompilerParams(dimension_semantics=sem, vmem_limit_bytes=VMEM_LIMIT)


def _mesh_pos():
    return lax.axis_index("x"), lax.axis_index("y"), lax.axis_index("c")


def _mm(name, pairs, *, mode, grid, out_shape, o_spec, nk=1, kaxis=None, add=None, deps=(), prev=None):
    npair = len(pairs)
    has_add = add is not None

    def body(*refs):
        ab = refs[: 2 * npair]
        pos = 2 * npair
        add_ref = refs[pos] if has_add else None
        pos += int(has_add) + len(deps) + (prev is not None)
        o_ref = refs[pos]
        acc_ref = refs[pos + 1] if nk > 1 else None
        d = None
        for p in range(npair):
            t = lax.dot_general(ab[2 * p][...], ab[2 * p + 1][...], _DIMS[mode], preferred_element_type=F32)
            d = t if d is None else d + t
        if nk == 1:
            if has_add:
                d = d + add_ref[...].astype(F32)
            o_ref[...] = d.astype(o_ref.dtype)
        else:
            k = pl.program_id(kaxis)

            @pl.when(k == 0)
            def _():
                acc_ref[...] = d

            @pl.when(k > 0)
            def _():
                acc_ref[...] += d

            @pl.when(k == nk - 1)
            def _():
                r = acc_ref[...]
                if has_add:
                    r = r + add_ref[...].astype(F32)
                o_ref[...] = r.astype(o_ref.dtype)

    args, specs = [], []
    for a, a_spec, b, b_spec in pairs:
        args += [a, b]
        specs += [a_spec, b_spec]
    if has_add:
        args.append(add[0])
        specs.append(add[1])
    args += list(deps)
    specs += [ANY] * len(deps)
    aliases = {}
    if prev is not None:
        aliases = {len(args): 0}
        args.append(prev)
        specs.append(ANY)
    scratch = []
    if nk > 1:
        blk = [d for d in o_spec.block_shape if d is not None]
        scratch = [pltpu.VMEM(tuple(blk), F32)]
    sem = tuple("arbitrary" if (nk > 1 and ax == kaxis) else "parallel" for ax in range(len(grid)))
    return pl.pallas_call(
        body, name=name, grid=grid, in_specs=specs, out_specs=o_spec, out_shape=out_shape,
        scratch_shapes=scratch, input_output_aliases=aliases, compiler_params=_cp(sem),
    )(*args)


def _tile_span(n_tiles, part):
    if part is None:
        return 0, n_tiles
    p, of = part
    return p * n_tiles // of, (p + 1) * n_tiles // of


def _tile(n, pref):
    if n <= pref:
        return n
    for t in range(pref, 0, -LANES):
        if t % LANES == 0 and n % t == 0:
            return t
    raise ValueError(f"no tile for {n}")


def _mm_nn(name, a, b, out_dtype, add=None, tk=None, deps=(), part=None, prev=None, tiles=None):
    m, kk = a.shape
    n = b.shape[1]
    tm, tn = _tile(m, 1024), _tile(n, COL_TILE)
    if tiles is not None:
        tm, tn = _tile(m, tiles[0]), _tile(n, tiles[1])
    out_shape = jax.ShapeDtypeStruct((m, n), out_dtype)
    if tk is None or tk == kk:
        j0, j1 = _tile_span(n // tn, part)
        grid = (m // tm, j1 - j0)
        pairs = [(a, pl.BlockSpec((tm, kk), lambda i, j: (i, 0)), b, pl.BlockSpec((kk, tn), lambda i, j: (0, j0 + j)))]
        o_spec = pl.BlockSpec((tm, tn), lambda i, j: (i, j0 + j))
        add_ = None if add is None else (add, pl.BlockSpec((tm, tn), lambda i, j: (i, j0 + j)))
        return _mm(name, pairs, mode="nn", grid=grid, out_shape=out_shape, o_spec=o_spec, add=add_, deps=deps, prev=prev)
    tn = _tile(n, 1024)
    nk = kk // tk
    grid = (m // tm, n // tn, nk)
    pairs = [(a, pl.BlockSpec((tm, tk), lambda i, j, k: (i, k)), b, pl.BlockSpec((tk, tn), lambda i, j, k: (k, j)))]
    o_spec = pl.BlockSpec((tm, tn), lambda i, j, k: (i, j))
    add_ = None if add is None else (add, pl.BlockSpec((tm, tn), lambda i, j, k: (i, j)))
    return _mm(name, pairs, mode="nn", grid=grid, out_shape=out_shape, o_spec=o_spec, nk=nk, kaxis=2, add=add_, deps=deps)


def _mm_nt(name, abs_, out_dtype, tk, deps=()):
    m, kk = abs_[0][0].shape
    n = abs_[0][1].shape[0]
    tm = _tile(m, 1024)
    nk = kk // tk
    tn = _tile(n, COL_TILE if nk == 1 else 1024)
    out_shape = jax.ShapeDtypeStruct((m, n), out_dtype)
    if nk == 1:
        grid = (m // tm, n // tn)
        pairs = [(a, pl.BlockSpec((tm, kk), lambda i, j: (i, 0)), b, pl.BlockSpec((tn, kk), lambda i, j: (j, 0)))
                 for a, b in abs_]
        o_spec = pl.BlockSpec((tm, tn), lambda i, j: (i, j))
        return _mm(name, pairs, mode="nt", grid=grid, out_shape=out_shape, o_spec=o_spec, deps=deps)
    grid = (m // tm, n // tn, nk)
    pairs = [(a, pl.BlockSpec((tm, tk), lambda i, j, k: (i, k)), b, pl.BlockSpec((tn, tk), lambda i, j, k: (j, k)))
             for a, b in abs_]
    o_spec = pl.BlockSpec((tm, tn), lambda i, j, k: (i, j))
    return _mm(name, pairs, mode="nt", grid=grid, out_shape=out_shape, o_spec=o_spec, nk=nk, kaxis=2, deps=deps)


def _mm_tn(name, a, b, out_dtype, deps=()):
    t, m = a.shape
    n = b.shape[1]
    tm, tn = _tile(m, 512), _tile(n, 2048)
    if n > m:
        grid = (n // tn, m // tm)
        a_map, b_map, o_map = (lambda j, i: (0, i)), (lambda j, i: (0, j)), (lambda j, i: (i, j))
    else:
        grid = (m // tm, n // tn)
        a_map, b_map, o_map = (lambda i, j: (0, i)), (lambda i, j: (0, j)), (lambda i, j: (i, j))
    pairs = [(a, pl.BlockSpec((t, tm), a_map), b, pl.BlockSpec((t, tn), b_map))]
    o_spec = pl.BlockSpec((tm, tn), o_map)
    return _mm(name, pairs, mode="tn", grid=grid, out_shape=jax.ShapeDtypeStruct((m, n), out_dtype), o_spec=o_spec,
               deps=deps)


def _mm_tn_half(name, a, b, pos, mine, add=None, deps=()):
    t, m = a.shape
    r, n = m // 2, b.shape[1]
    tm, tn = _tile(r, 512), _tile(n, 2048)
    nbi = r // tm
    half = (lambda pos: pos[0]) if mine else (lambda pos: 1 - pos[0])
    if n > r:
        grid, ij = (n // tn, nbi), (lambda g0, g1: (g1, g0))
    else:
        grid, ij = (nbi, n // tn), (lambda g0, g1: (g0, g1))
    has_add = add is not None

    def body(pos_ref, a_ref, b_ref, *rest):
        d = lax.dot_general(a_ref[...], b_ref[...], _DIMS["tn"], preferred_element_type=F32)
        if has_add:
            d = d + rest[0][...].astype(F32)
        rest[-1][...] = d.astype(BF16)

    o_spec = pl.BlockSpec((None, tm, tn), lambda g0, g1, pos: (0, *ij(g0, g1)))
    grid_spec = pltpu.PrefetchScalarGridSpec(
        num_scalar_prefetch=1, grid=grid,
        in_specs=[pl.BlockSpec((t, tm), lambda g0, g1, pos: (0, half(pos) * nbi + ij(g0, g1)[0])),
                  pl.BlockSpec((t, tn), lambda g0, g1, pos: (0, ij(g0, g1)[1]))]
        + ([o_spec] if has_add else []) + [ANY] * len(deps),
        out_specs=o_spec)
    return pl.pallas_call(body, name=name, grid_spec=grid_spec, out_shape=jax.ShapeDtypeStruct((1, r, n), BF16),
                          compiler_params=_cp(("parallel",) * 2))(pos, a, b, *([add] if has_add else []), *deps)


def _proj_piece(name, h, w, prev, kvec, base, count, deps=()):
    t, kk = h.shape
    own = w.dtype == F32
    nn = w.shape[1] if own else w.shape[1] // N_CHIPS
    tm, tn = _tile(t, 1024), _tile(nn, COL_TILE)
    nb = nn // tn

    def body(kv_ref, h_ref, w_ref, *rest):
        rest[-1][...] = lax.dot_general(h_ref[...], w_ref[...].astype(BF16), _DIMS["nn"],
                                        preferred_element_type=F32).astype(BF16)

    cols = lambda s, i, j, kv: (0, j) if own else (0, kv[base + s] * nb + j)
    extra = ([] if prev is None else [prev]) + list(deps)
    grid_spec = pltpu.PrefetchScalarGridSpec(
        num_scalar_prefetch=1, grid=(count, t // tm, nb),
        in_specs=[pl.BlockSpec((tm, kk), lambda s, i, j, kv: (i, 0)), pl.BlockSpec((kk, tn), cols)] + [ANY] * len(extra),
        out_specs=pl.BlockSpec((tm, tn), lambda s, i, j, kv: (i, kv[base + s] * nb + j)))
    return pl.pallas_call(body, name=name, grid_spec=grid_spec, out_shape=jax.ShapeDtypeStruct((t, N_CHIPS * nn), BF16),
                          input_output_aliases={} if prev is None else {3: 0},
                          compiler_params=_cp(("parallel",) * 3))(kvec, h, w, *extra)


def _gmm_nn(name, p, w, out_dtype):
    t = p.shape[0]
    g, cg, dg = w.shape
    tm = _tile(t, 1024)
    pairs = [(p, pl.BlockSpec((tm, cg), lambda i, j: (i, j)), w, pl.BlockSpec((None, cg, dg), lambda i, j: (j, 0, 0)))]
    o_spec = pl.BlockSpec((tm, dg), lambda i, j: (i, j))
    return _mm(name, pairs, mode="nn", grid=(t // tm, g), out_shape=jax.ShapeDtypeStruct((t, g * dg), out_dtype),
               o_spec=o_spec)


def _gmm_nt(name, dy, w, out_dtype):
    t = dy.shape[0]
    g, cg, dg = w.shape
    tm = _tile(t, 1024)
    pairs = [(dy, pl.BlockSpec((tm, dg), lambda i, j: (i, j)), w, pl.BlockSpec((None, cg, dg), lambda i, j: (j, 0, 0)))]
    o_spec = pl.BlockSpec((tm, cg), lambda i, j: (i, j))
    return _mm(name, pairs, mode="nt", grid=(t // tm, g), out_shape=jax.ShapeDtypeStruct((t, g * cg), out_dtype),
               o_spec=o_spec)


def _gmm_tn(name, p, dy, g, out_dtype):
    t = p.shape[0]
    cg, dg = p.shape[1] // g, dy.shape[1] // g
    pairs = [(p, pl.BlockSpec((t, cg), lambda j: (0, j)), dy, pl.BlockSpec((t, dg), lambda j: (0, j)))]
    o_spec = pl.BlockSpec((None, cg, dg), lambda j: (j, 0, 0))
    return _mm(name, pairs, mode="tn", grid=(g,), out_shape=jax.ShapeDtypeStruct((g, cg, dg), out_dtype), o_spec=o_spec)


ROW_TILE = 256


def _rows(t):
    return _tile8(t, ROW_TILE)


def _tile8(n, pref):
    if n <= pref:
        return n
    for t in range(pref, 0, -8):
        if n % t == 0:
            return t
    raise ValueError(f"no row tile for {n}")


def _cast_place(name, w, pos, shard, deps=()):
    tr = _tile8(w.R, 512)
    if w.colshard:
        o_map = lambda h, i, pos: (0, h, i, pos[1])
    else:
        o_map = lambda h, i, pos: (pos[1], h, i, 0)

    def body(pos_ref, w_ref, *rest):
        rest[-1][...] = w_ref[...].astype(BF16)

    grid_spec = pltpu.PrefetchScalarGridSpec(
        num_scalar_prefetch=1, grid=(2, w.R // tr),
        in_specs=[pl.BlockSpec((None, tr, w.nn), lambda h, i, pos: (h, i, 0))] + [ANY] * len(deps),
        out_specs=pl.BlockSpec((None, None, tr, w.nn), o_map))
    return pl.pallas_call(body, name=name, grid_spec=grid_spec, out_shape=jax.ShapeDtypeStruct((w.P, 2, w.R, w.N), BF16),
                          compiler_params=_cp(("parallel", "parallel")))(pos, shard, *deps)


def _rms_fwd(name, x, g, deps=()):
    t, d = x.shape
    tm = _rows(t)

    def body(x_ref, g_ref, *rest):
        xf = x_ref[...]
        r = lax.rsqrt(jnp.mean(xf * xf, axis=-1, keepdims=True) + EPS)
        rest[-1][...] = (xf * r * g_ref[...]).astype(BF16)

    return pl.pallas_call(
        body, name=name, grid=(t // tm,),
        in_specs=[pl.BlockSpec((tm, d), lambda i: (i, 0)), pl.BlockSpec((1, d), lambda i: (0, 0))] + [ANY] * len(deps),
        out_specs=pl.BlockSpec((tm, d), lambda i: (i, 0)), out_shape=jax.ShapeDtypeStruct((t, d), BF16),
        compiler_params=_cp(("parallel",)),
    )(x, g, *deps)


def _rms_bwd(name, x, g, dh, dres, want_bf16, deps=()):
    t, d = x.shape
    tm = _rows(t)

    def body(x_ref, g_ref, dh_ref, dres_ref, *rest):
        rest = rest[len(deps):]
        dx_ref, rest = rest[0], rest[1:]
        dg_ref = rest[-1]
        xf = x_ref[...]
        r = lax.rsqrt(jnp.mean(xf * xf, axis=-1, keepdims=True) + EPS)
        xh = xf * r
        dhf = dh_ref[...].astype(F32)
        dxh = dhf * g_ref[...]
        m = jnp.mean(dxh * xh, axis=-1, keepdims=True)
        dx = dres_ref[...] + r * (dxh - xh * m)
        dx_ref[...] = dx
        if want_bf16:
            rest[0][...] = dx.astype(BF16)

        @pl.when(pl.program_id(0) == 0)
        def _():
            dg_ref[...] = jnp.zeros_like(dg_ref)

        dg_ref[...] += jnp.sum(dhf * xh, axis=0, keepdims=True)

    row = pl.BlockSpec((tm, d), lambda i: (i, 0))
    vec = pl.BlockSpec((1, d), lambda i: (0, 0))
    out_specs = [row] + ([row] if want_bf16 else []) + [vec]
    out_shape = ([jax.ShapeDtypeStruct((t, d), F32)] + ([jax.ShapeDtypeStruct((t, d), BF16)] if want_bf16 else [])
                 + [jax.ShapeDtypeStruct((1, d), F32)])
    return pl.pallas_call(body, name=name, grid=(t // tm,), in_specs=[row, vec, row, row] + [ANY] * len(deps),
                          out_specs=out_specs, out_shape=out_shape, compiler_params=_cp(("arbitrary",)))(x, g, dh, dres, *deps)


def _final_bwd(name, x3, gf, tgt):
    t, d = x3.shape
    tm = _rows(t)

    def body(x_ref, g_ref, t_ref, dx_ref, dxb_ref, dg_ref, lc_ref):
        xf = x_ref[...]
        g = g_ref[...]
        r = lax.rsqrt(jnp.mean(xf * xf, axis=-1, keepdims=True) + EPS)
        xh = xf * r
        diff = xh * g - t_ref[...]
        dy = diff * (1.0 / d)
        dxh = dy * g
        m = jnp.mean(dxh * xh, axis=-1, keepdims=True)
        dx = r * (dxh - xh * m)
        dx_ref[...] = dx
        dxb_ref[...] = dx.astype(BF16)

        @pl.when(pl.program_id(0) == 0)
        def _():
            dg_ref[...] = jnp.zeros_like(dg_ref)
            lc_ref[...] = jnp.zeros_like(lc_ref)

        dg_ref[...] += jnp.sum(dy * xh, axis=0, keepdims=True)
        lc_ref[...] += jnp.sum(diff * diff, axis=0, keepdims=True) * (0.5 / d)

    row = pl.BlockSpec((tm, d), lambda i: (i, 0))
    vec = pl.BlockSpec((1, d), lambda i: (0, 0))
    return pl.pallas_call(
        body, name=name, grid=(t // tm,), in_specs=[row, vec, row], out_specs=[row, row, vec, vec],
        out_shape=[jax.ShapeDtypeStruct((t, d), F32), jax.ShapeDtypeStruct((t, d), BF16),
                   jax.ShapeDtypeStruct((1, d), F32), jax.ShapeDtypeStruct((1, d), F32)],
        compiler_params=_cp(("arbitrary",)),
    )(x3, gf, tgt)


def _shift_down(v, k, t_idx):
    return jnp.where(t_idx >= k, pltpu.roll(v, k, 0), 0.0)


def _shift_up(v, k, t_idx):
    n = v.shape[0]
    return jnp.where(t_idx < n - k, pltpu.roll(v, n - k, 0), 0.0)


def _window_sums(v, shift, t_idx, grp):
    s = v + shift(v, 1, t_idx)
    out = s
    for lvl in range(1, len(POOL_WINDOWS)):
        s = s + shift(s, 1 << lvl, t_idx)
        out = jnp.where(grp >= lvl, s, out)
    return out


def _window_weight(t_idx, grp):
    return 1.0 / jnp.minimum(t_idx[:, :1] + 1, jnp.left_shift(2, grp)).astype(F32)


MIX_COLS = 256


def _mixer_fwd(name, proj, cw, cb, n_conv, n_groups, deps=()):
    t = proj.shape[0]
    nb = n_conv // MIX_COLS
    per_group = n_conv // n_groups // MIX_COLS

    def body(ba_ref, ca_ref, va_ref, vb_ref, cw_ref, cb_ref, *rest):
        z_ref, p_ref = rest[len(deps):]
        t_idx = lax.broadcasted_iota(jnp.int32, (t, MIX_COLS), 0)
        q = ca_ref[...].astype(F32) * va_ref[...].astype(F32)
        w = cw_ref[...]
        u = cb_ref[...] + w[0:1] * _shift_down(q, 2, t_idx) + w[1:2] * _shift_down(q, 1, t_idx) + w[2:3] * q
        z_ref[...] = (ba_ref[...].astype(F32) * u).astype(BF16)
        grp = pl.program_id(0) // per_group
        v = vb_ref[...].astype(F32)
        p_ref[...] = (_window_sums(v, _shift_down, t_idx, grp) * _window_weight(t_idx, grp) - v).astype(BF16)

    col = lambda s: pl.BlockSpec((t, MIX_COLS), lambda j: (0, s * nb + j))
    return pl.pallas_call(
        body, name=name, grid=(nb,),
        in_specs=[col(0), col(1), col(2), col(3), pl.BlockSpec((3, MIX_COLS), lambda j: (0, j)),
                  pl.BlockSpec((1, MIX_COLS), lambda j: (0, j))] + [ANY] * len(deps),
        out_specs=[col(0), col(0)],
        out_shape=[jax.ShapeDtypeStruct((t, n_conv), BF16), jax.ShapeDtypeStruct((t, n_conv), BF16)],
        compiler_params=_cp(("parallel",)),
    )(proj, proj, proj, proj, cw, cb, *deps)


def _mixer_bwd(name, dz, dp, proj, cw, cb, dproj, n_conv, n_groups, deps=()):
    t = proj.shape[0]
    nb = n_conv // MIX_COLS
    per_group = n_conv // n_groups // MIX_COLS

    def body(dz_ref, dp_ref, ba_ref, ca_ref, va_ref, cw_ref, cb_ref, _, *rest):
        o_ref, dcw_ref, dcb_ref, scr = rest[len(deps):]
        s = pl.program_id(1)

        @pl.when(s == 0)
        def _():
            t_idx = lax.broadcasted_iota(jnp.int32, (t, MIX_COLS), 0)
            ca, va = ca_ref[...].astype(F32), va_ref[...].astype(F32)
            q = ca * va
            q1, q2 = _shift_down(q, 1, t_idx), _shift_down(q, 2, t_idx)
            w = cw_ref[...]
            u = cb_ref[...] + w[0:1] * q2 + w[1:2] * q1 + w[2:3] * q
            dzf = dz_ref[...].astype(F32)
            du = dzf * ba_ref[...].astype(F32)
            scr[0] = (dzf * u).astype(BF16)
            dq = w[2:3] * du + w[1:2] * _shift_up(du, 1, t_idx) + w[0:1] * _shift_up(du, 2, t_idx)
            scr[1] = (dq * va).astype(BF16)
            scr[2] = (dq * ca).astype(BF16)
            dcb_ref[...] = jnp.sum(du, axis=0, keepdims=True)
            dcw_ref[0:1, :] = jnp.sum(du * q2, axis=0, keepdims=True)
            dcw_ref[1:2, :] = jnp.sum(du * q1, axis=0, keepdims=True)
            dcw_ref[2:3, :] = jnp.sum(du * q, axis=0, keepdims=True)
            grp = pl.program_id(0) // per_group
            dpf = dp_ref[...].astype(F32)
            e = dpf * _window_weight(t_idx, grp)
            scr[3] = (_window_sums(e, _shift_up, t_idx, grp) - dpf).astype(BF16)

        o_ref[...] = scr[s]

    col = lambda c: pl.BlockSpec((t, MIX_COLS), lambda j, s: (0, c * nb + j))
    own = pl.BlockSpec((t, MIX_COLS), lambda j, s: (0, j))
    return pl.pallas_call(
        body, name=name, grid=(nb, 4),
        in_specs=[own, own, col(0), col(1), col(2), pl.BlockSpec((3, MIX_COLS), lambda j, s: (0, j)),
                  pl.BlockSpec((1, MIX_COLS), lambda j, s: (0, j)), ANY] + [ANY] * len(deps),
        out_specs=[pl.BlockSpec((t, MIX_COLS), lambda j, s: (0, s * nb + j)),
                   pl.BlockSpec((3, MIX_COLS), lambda j, s: (0, j)), pl.BlockSpec((1, MIX_COLS), lambda j, s: (0, j))],
        out_shape=[jax.ShapeDtypeStruct(dproj.shape, BF16), jax.ShapeDtypeStruct((3, n_conv), F32),
                   jax.ShapeDtypeStruct((1, n_conv), F32)],
        scratch_shapes=[pltpu.VMEM((4, t, MIX_COLS), BF16)],
        input_output_aliases={7: 0},
        compiler_params=_cp(("arbitrary", "arbitrary")),
    )(dz, dp, proj, proj, proj, cw, cb, dproj, *deps)


def _conv_out_merge(name, z, w_a, proj, bg, yb, ps):
    t, c = z.shape
    d = w_a.shape[1]
    tm, tn = _tile(t, 512), _tile(d, COL_TILE)
    nj = d // tn
    ga0 = (proj.shape[1] - 2 * d) // tn

    def body(z_ref, w_ref, ga_ref, gb_ref, ba_ref, bb_ref, yb_ref, ps_ref, ya_ref, m_ref):
        ya = lax.dot_general(z_ref[...], w_ref[...], _DIMS["nn"], preferred_element_type=F32)
        sa = jax.nn.sigmoid(ga_ref[...].astype(F32) + ba_ref[...])
        sb = jax.nn.sigmoid(gb_ref[...].astype(F32) + bb_ref[...])
        ya_ref[...] = ya.astype(BF16)
        m_ref[...] = (sa * ya + sb * (yb_ref[...].astype(F32) * ps_ref[...])).astype(BF16)

    blk = pl.BlockSpec((tm, tn), lambda i, j: (i, j))
    vec = lambda off: pl.BlockSpec((1, tn), lambda i, j: (0, off + j))
    shp = jax.ShapeDtypeStruct((t, d), BF16)
    return pl.pallas_call(
        body, name=name, grid=(t // tm, nj),
        in_specs=[pl.BlockSpec((tm, c), lambda i, j: (i, 0)), pl.BlockSpec((c, tn), lambda i, j: (0, j)),
                  pl.BlockSpec((tm, tn), lambda i, j: (i, ga0 + j)), pl.BlockSpec((tm, tn), lambda i, j: (i, ga0 + nj + j)),
                  vec(0), vec(nj), blk, vec(0)],
        out_specs=[blk, blk], out_shape=[shp, shp], compiler_params=_cp(("parallel", "parallel")))(
            z, w_a, proj, proj, bg, bg, yb, ps)


def _merge_bwd(name, dm, proj, bg, ya, yb, ps, deps=()):
    t, d = ya.shape
    tm = _rows(t)

    def body(dm_ref, gab_ref, bg_ref, ya_ref, yb_ref, ps_ref, *rest):
        dya_ref, dyb_ref, dg_ref, dba_ref, dbb_ref, dps_ref = rest[len(deps):]
        gab = gab_ref[...].astype(F32) + bg_ref[...]
        sa, sb = jax.nn.sigmoid(gab[:, :d]), jax.nn.sigmoid(gab[:, d:])
        dmf = dm_ref[...].astype(F32)
        ybf, ps_ = yb_ref[...].astype(F32), ps_ref[...]
        dya_ref[...] = (dmf * sa).astype(BF16)
        dyb = dmf * sb
        dyb_ref[...] = (dyb * ps_).astype(BF16)
        dga = dmf * ya_ref[...].astype(F32) * sa * (1.0 - sa)
        dgb = dmf * (ybf * ps_) * sb * (1.0 - sb)
        dg_ref[:, :d] = dga.astype(BF16)
        dg_ref[:, d:] = dgb.astype(BF16)

        @pl.when(pl.program_id(0) == 0)
        def _():
            dba_ref[...] = jnp.zeros_like(dba_ref)
            dbb_ref[...] = jnp.zeros_like(dbb_ref)
            dps_ref[...] = jnp.zeros_like(dps_ref)

        dba_ref[...] += jnp.sum(dga, axis=0, keepdims=True)
        dbb_ref[...] += jnp.sum(dgb, axis=0, keepdims=True)
        dps_ref[...] += jnp.sum(dyb * ybf, axis=0, keepdims=True)

    row = pl.BlockSpec((tm, d), lambda i: (i, 0))
    vec = pl.BlockSpec((1, d), lambda i: (0, 0))
    gates = pl.BlockSpec((tm, 2 * d), lambda i: (i, 1))
    return pl.pallas_call(
        body, name=name, grid=(t // tm,),
        in_specs=[row, gates, pl.BlockSpec((1, 2 * d), lambda i: (0, 0)), row, row, vec] + [ANY] * len(deps),
        out_specs=[row, row, gates, vec, vec, vec],
        out_shape=[jax.ShapeDtypeStruct((t, d), BF16), jax.ShapeDtypeStruct((t, d), BF16),
                   jax.ShapeDtypeStruct(proj.shape, BF16), jax.ShapeDtypeStruct((1, d), F32),
                   jax.ShapeDtypeStruct((1, d), F32), jax.ShapeDtypeStruct((1, d), F32)],
        compiler_params=_cp(("arbitrary",)),
    )(dm, proj, bg, ya, yb, ps, *deps)


def _ffn_up_act(name, h, w_up, gate, part=None, prev=None, deps=()):
    t, d = h.shape
    f = w_up.shape[1]
    tm, tf = _tile(t, 1024), _tile(f, 512)
    j0, j1 = _tile_span(f // tf, part)
    n_prev = 0 if prev is None else 2
    extra = ([] if prev is None else list(prev)) + list(deps)

    def body(h_ref, w_ref, g_ref, *rest):
        u_ref, a_ref = rest[len(extra):]
        u = lax.dot_general(h_ref[...], w_ref[...], _DIMS["nn"], preferred_element_type=F32)
        g = g_ref[...].astype(F32)
        u_ref[...] = u.astype(BF16)
        a_ref[...] = (g * jax.nn.sigmoid(g) * u).astype(BF16)

    blk = pl.BlockSpec((tm, tf), lambda i, j: (i, j0 + j))
    shp = jax.ShapeDtypeStruct((t, f), BF16)
    return pl.pallas_call(
        body, name=name, grid=(t // tm, j1 - j0),
        in_specs=[pl.BlockSpec((tm, d), lambda i, j: (i, 0)), pl.BlockSpec((d, tf), lambda i, j: (0, j0 + j)), blk]
        + [ANY] * len(extra),
        out_specs=[blk, blk], out_shape=[shp, shp], input_output_aliases={3 + i: i for i in range(n_prev)},
        compiler_params=_cp(("parallel", "parallel")))(h, w_up, gate, *extra)


def _ffn_bwd(name, dy, w_down, gate, up):
    t, d = dy.shape
    f = w_down.shape[0]
    tm, tf = _tile(t, 1024), _tile(f, 512)

    def body(dy_ref, w_ref, g_ref, u_ref, dg_ref, du_ref):
        da = lax.dot_general(dy_ref[...], w_ref[...], _DIMS["nt"], preferred_element_type=F32)
        g = g_ref[...].astype(F32)
        s = jax.nn.sigmoid(g)
        du_ref[...] = (da * (g * s)).astype(BF16)
        dg_ref[...] = (da * u_ref[...].astype(F32) * (s * (1.0 + g * (1.0 - s)))).astype(BF16)

    blk = pl.BlockSpec((tm, tf), lambda i, j: (i, j))
    shp = jax.ShapeDtypeStruct((t, f), BF16)
    return pl.pallas_call(
        body, name=name, grid=(t // tm, f // tf),
        in_specs=[pl.BlockSpec((tm, d), lambda i, j: (i, 0)), pl.BlockSpec((tf, d), lambda i, j: (j, 0)), blk, blk],
        out_specs=[blk, blk], out_shape=[shp, shp], compiler_params=_cp(("parallel", "parallel")))(dy, w_down, gate, up)


def _adamw_math(w, g, m, v):
    m = ADAM_B1 * m + (1.0 - ADAM_B1) * g
    v = ADAM_B2 * v + (1.0 - ADAM_B2) * (g * g)
    m_hat = m / (1.0 - ADAM_B1 ** ADAM_STEP)
    v_hat = v / (1.0 - ADAM_B2 ** ADAM_STEP)
    delta = -ADAM_LR * (m_hat / (jnp.sqrt(v_hat) + ADAM_EPS) + ADAM_WD * w)
    return delta, m, v


def _adamw(name, w, g, m, v):
    r, c = w.shape
    tr = _tile8(r, 512 if c <= 1024 else 256)

    def body(w_ref, g_ref, m_ref, v_ref, go_ref, d_ref, nm_ref, nv_ref):
        g = g_ref[...]
        go_ref[...] = g
        d_ref[...], nm_ref[...], nv_ref[...] = _adamw_math(w_ref[...], g, m_ref[...], v_ref[...])

    blk = pl.BlockSpec((tr, c), lambda i: (i, 0))
    shp = jax.ShapeDtypeStruct((r, c), F32)
    return pl.pallas_call(body, name=name, grid=(r // tr,), in_specs=[blk] * 4, out_specs=[blk] * 4,
                          out_shape=[shp] * 4, compiler_params=_cp(("parallel",)))(w, g, m, v)


class _Weight:
    def __init__(self, name, rows, cols, colshard):
        self.name, self.colshard = name, colshard
        self.R, self.nn = rows // 2, cols
        self.P = 1 if colshard else N_CHIPS
        self.N = N_CHIPS * cols if colshard else cols

    def cols(self, k):
        return pl.ds(pl.multiple_of(k * self.nn, LANES), self.nn)

    def half(self, ref, k, h):
        return ref.at[0, h, :, self.cols(k)] if self.colshard else ref.at[k, h]

    def quarter(self, ref, k, h, q):
        return self.half(ref, k, h).at[pl.ds(q * (self.R // 2), self.R // 2), :]

    def part(self, ref, k):
        return ref.at[0, :, self.cols(k)] if self.colshard else ref.at[k]


def _remote(src, dst, ssem, rsem, dev):
    return pltpu.make_async_remote_copy(src_ref=src, dst_ref=dst, send_sem=ssem, recv_sem=rsem, device_id=dev,
                                        device_id_type=MESH)


def _other_chips(x, y):
    chips = [(1 - x, y), (x, 1 - y), (1 - x, 1 - y)]
    return chips, [2 * cx + cy for cx, cy in chips]


def _hbm(a):
    return pltpu.with_memory_space_constraint(a, pltpu.HBM)


def _split_start(name, arrays, sets, after=()):
    na, ns = len(arrays), len(sets)

    def body(*refs):
        outs = refs[na + len(after):]
        for s_, (idx, copies) in enumerate(sets):
            for i, (src, dst, dev, _) in enumerate(copies([refs[k] for k in idx], *_mesh_pos())):
                _remote(src, dst, outs[2 * s_].at[i], outs[2 * s_ + 1].at[i], dev).start()
        outs[2 * ns + na][...] = jnp.zeros((8, LANES), F32)

    sems = []
    for _, copies in sets:
        sems += [pltpu.SemaphoreType.DMA((copies.n,))] * 2
    out = pl.pallas_call(
        body, name=name, in_specs=[HBM] * na + [ANY] * len(after), out_specs=[SEM] * (2 * ns) + [HBM] * na + [VMEM],
        out_shape=sems + [pltpu.HBM(a.shape, a.dtype) for a in arrays] + [jax.ShapeDtypeStruct((8, LANES), F32)],
        input_output_aliases={i: 2 * ns + i for i in range(na)},
        compiler_params=pltpu.CompilerParams(has_side_effects=EFFECT),
    )(*[_hbm(a) for a in arrays], *after)
    return [(out[2 * i], out[2 * i + 1]) for i in range(ns)], list(out[2 * ns:2 * ns + na]), out[-1]


def _split_wait(name, arrays, ssem, rsem, copies, after):
    na = len(arrays)

    def body(*refs):
        for i, (src, _, dev, dst) in enumerate(copies(refs[:na], *_mesh_pos())):
            cp = _remote(src, dst, refs[na].at[i], refs[na + 1].at[i], dev)
            cp.wait_send()
            cp.wait_recv()

    return list(pl.pallas_call(
        body, name=name, in_specs=[HBM] * na + [SEM, SEM] + [ANY] * len(after), out_specs=[HBM] * na,
        out_shape=[pltpu.HBM(a.shape, a.dtype) for a in arrays], input_output_aliases={i: i for i in range(na)},
        compiler_params=pltpu.CompilerParams(has_side_effects=EFFECT),
    )(*arrays, ssem, rsem, *after))


def _pass_copies(grp, rels=(0, 1, 2)):
    def copies(land, x, y, c):
        _, ks = _other_chips(x, y)
        return [(w.half(land[wi], ks[j], c), w.half(land[wi], ks[j], c), (x, y, 1 - c), w.half(land[wi], ks[j], 1 - c))
                for wi, w in enumerate(grp) for j in rels]
    copies.n = len(grp) * len(rels)
    return copies


def _direct_copies(grp):
    def copies(land, x, y, c):
        chips, ks = _other_chips(x, y)
        out = []
        for wi, w in enumerate(grp):
            mine = w.half(land[wi], 2 * x + y, c)
            out += [(mine, mine, (*chips[j], c), w.half(land[wi], ks[j], c)) for j in range(3)]
        return out
    copies.n = 3 * len(grp)
    return copies


def _near_copies(grp):
    def copies(land, x, y, c):
        chips, ks = _other_chips(x, y)
        out = []
        for wi, w in enumerate(grp):
            mine = w.half(land[wi], 2 * x + y, c)
            out += [(mine, mine, (*chips[j], c), w.half(land[wi], ks[j], c)) for j in (0, 1)]
        return out
    copies.n = 2 * len(grp)
    return copies


def _far_copies(grp):
    def copies(land, x, y, c):
        chips, ks = _other_chips(x, y)
        out = []
        for wi, w in enumerate(grp):
            for j in (0, 1):
                q = w.quarter(land[wi], ks[j], c, j)
                out.append((q, q, (*chips[1 - j], c), w.quarter(land[wi], ks[2], c, j)))
        return out
    copies.n = 2 * len(grp)
    return copies


def _pair_copies(n, whole=False):
    def copies(refs, x, y, c):
        return [(refs[i] if whole else refs[i].at[:, 1 - c], refs[n + i], (x, y, 1 - c), refs[n + i]) for i in range(n)]
    copies.n = n
    return copies


def _share_copies(n):
    def copies(refs, x, y, c):
        return [(refs[i].at[c], refs[i].at[c], (x, y, 1 - c), refs[i].at[1 - c]) for i in range(n)]
    copies.n = n
    return copies


def _gather_conv_w(cw, thru):
    ncw = cw.shape[1]

    def body(cw_ref, _, out_ref, __, ssem, rsem):
        x, y, c = _mesh_pos()
        k_me = 2 * x + y
        chips, ks = _other_chips(x, y)
        cols = lambda k: out_ref.at[:, pl.ds(pl.multiple_of(k * ncw, LANES), ncw)]
        cps = [_remote(cw_ref, cols(k_me), ssem.at[j], rsem.at[j], (*chip, c)) for j, chip in enumerate(chips)]
        for cp in cps:
            cp.start()
        for k in range(N_CHIPS):
            @pl.when(k_me == k)
            def _():
                out_ref[:, k * ncw:(k + 1) * ncw] = cw_ref[...]
        for j in range(3):
            _remote(cw_ref, cols(ks[j]), ssem.at[j], rsem.at[j], (*chips[j], c)).wait_recv()
        for cp in cps:
            cp.wait_send()

    return pl.pallas_call(
        body, name="gather_conv_w", in_specs=[VMEM, ANY], out_specs=[VMEM, ANY],
        out_shape=[jax.ShapeDtypeStruct((3, N_CHIPS * ncw), F32), jax.ShapeDtypeStruct(thru.shape, thru.dtype)],
        scratch_shapes=[pltpu.SemaphoreType.DMA((3,)), pltpu.SemaphoreType.DMA((3,))],
        input_output_aliases={1: 1},
    )(cw, thru)


def _grad_tiles(w, n):
    return _tile8(w.R, 512) if w.R <= 512 else w.R // 2, _tile(n, 2048)


def _pair_sum(name, w, pos, grad, got):
    tr, tn = _grad_tiles(w, w.N)

    def body(pos_ref, g_ref, r_ref, o_ref):
        o_ref[...] = (g_ref[...].astype(F32) + r_ref[...].astype(F32)).astype(BF16)

    blk = pl.BlockSpec((None, tr, tn), lambda p, i, j, pos: (p, i, j))
    grid_spec = pltpu.PrefetchScalarGridSpec(
        num_scalar_prefetch=1, grid=(w.P, w.R // tr, w.N // tn),
        in_specs=[pl.BlockSpec((None, None, tr, tn), lambda p, i, j, pos: (p, pos[0], i, j)), blk], out_specs=blk)
    return pl.pallas_call(body, name=name, grid_spec=grid_spec, out_shape=jax.ShapeDtypeStruct((w.P, w.R, w.N), BF16),
                          compiler_params=_cp(("parallel",) * 3))(pos, grad, got)


def _scatter_start(name, ws, pairs):
    nw = len(ws)

    def body(*refs):
        pr, land = refs[:nw], refs[nw:2 * nw]
        ssem, rsem = refs[2 * nw], refs[2 * nw + 1]
        token = refs[4 * nw + 2]
        x, y, c = _mesh_pos()
        chips, ks = _other_chips(x, y)
        for i, w in enumerate(ws):
            for j, chip in enumerate(chips):
                _remote(w.part(pr[i], ks[j]), land[i].at[j], ssem.at[3 * i + j], rsem.at[3 * i + j], (*chip, c)).start()
        token[...] = jnp.zeros_like(token)

    lands = [lax.empty((3, w.R, w.nn), BF16) for w in ws]
    out = pl.pallas_call(
        body, name=name, in_specs=[HBM] * (2 * nw),
        out_specs=[SEM, SEM] + [HBM] * (2 * nw) + [VMEM],
        out_shape=[pltpu.SemaphoreType.DMA((3 * nw,))] * 2 + [pltpu.HBM(a.shape, a.dtype) for a in pairs + lands]
        + [jax.ShapeDtypeStruct((8, LANES), F32)],
        input_output_aliases={i: 2 + i for i in range(2 * nw)},
        compiler_params=pltpu.CompilerParams(has_side_effects=EFFECT),
    )(*[_hbm(a) for a in pairs + lands])
    return out[0], out[1], list(out[2:2 + nw]), list(out[2 + nw:2 + 2 * nw]), out[-1]


def _scatter_wait(name, ws, pairs, lands, ssem, rsem, after):
    nw = len(ws)

    def body(*refs):
        pr, land = refs[:nw], refs[nw:2 * nw]
        ssem_ref, rsem_ref = refs[2 * nw], refs[2 * nw + 1]
        x, y, c = _mesh_pos()
        chips, ks = _other_chips(x, y)
        for i, w in enumerate(ws):
            for j, chip in enumerate(chips):
                cp = _remote(w.part(pr[i], ks[j]), land[i].at[j], ssem_ref.at[3 * i + j], rsem_ref.at[3 * i + j], (*chip, c))
                cp.wait_send()
                cp.wait_recv()

    out = pl.pallas_call(
        body, name=name, in_specs=[HBM] * (2 * nw) + [SEM, SEM] + [ANY] * len(after), out_specs=[HBM] * (2 * nw),
        out_shape=[pltpu.HBM(a.shape, a.dtype) for a in pairs + lands],
        input_output_aliases={i: i for i in range(2 * nw)},
        compiler_params=pltpu.CompilerParams(has_side_effects=EFFECT),
    )(*pairs, *lands, ssem, rsem, *after)
    return list(out[:nw]), list(out[nw:])


def _final_sum(name, w, pos, grad, got, parts):
    tr, tn = _grad_tiles(w, w.nn)
    nbc = w.nn // tn
    if got is None:
        return _final_sum_pair(name, w, pos, grad, parts, tr, tn)

    def body(pos_ref, g_ref, r_ref, p_ref, o_ref):
        acc = g_ref[...].astype(F32) + r_ref[...].astype(F32)
        for j in range(3):
            acc = acc + p_ref[j].astype(F32)
        o_ref[...] = acc

    if w.colshard:
        g_spec = pl.BlockSpec((None, None, tr, tn), lambda i, j, pos: (0, pos[0], i, pos[1] * nbc + j))
        r_spec = pl.BlockSpec((None, tr, tn), lambda i, j, pos: (0, i, pos[1] * nbc + j))
    else:
        g_spec = pl.BlockSpec((None, None, tr, tn), lambda i, j, pos: (pos[1], pos[0], i, j))
        r_spec = pl.BlockSpec((None, tr, tn), lambda i, j, pos: (pos[1], i, j))
    grid_spec = pltpu.PrefetchScalarGridSpec(
        num_scalar_prefetch=1, grid=(w.R // tr, nbc),
        in_specs=[g_spec, r_spec, pl.BlockSpec((3, tr, tn), lambda i, j, pos: (0, i, j))],
        out_specs=pl.BlockSpec((None, tr, tn), lambda i, j, pos: (pos[0], i, j)))
    return pl.pallas_call(body, name=name, grid_spec=grid_spec, out_shape=jax.ShapeDtypeStruct((2, w.R, w.nn), F32),
                          compiler_params=_cp(("parallel",) * 2))(pos, grad, got, parts)


def _final_sum_pair(name, w, pos, pair, parts, tr, tn):
    nbc = w.nn // tn

    def body(pos_ref, g_ref, p_ref, o_ref):
        acc = g_ref[...].astype(F32)
        for j in range(3):
            acc = acc + p_ref[j].astype(F32)
        o_ref[...] = acc

    if w.colshard:
        g_spec = pl.BlockSpec((None, tr, tn), lambda i, j, pos: (0, i, pos[1] * nbc + j))
    else:
        g_spec = pl.BlockSpec((None, tr, tn), lambda i, j, pos: (pos[1], i, j))
    grid_spec = pltpu.PrefetchScalarGridSpec(
        num_scalar_prefetch=1, grid=(w.R // tr, nbc),
        in_specs=[g_spec, pl.BlockSpec((3, tr, tn), lambda i, j, pos: (0, i, j))],
        out_specs=pl.BlockSpec((None, tr, tn), lambda i, j, pos: (pos[0], i, j)))
    return pl.pallas_call(body, name=name, grid_spec=grid_spec, out_shape=jax.ShapeDtypeStruct((2, w.R, w.nn), F32),
                          compiler_params=_cp(("parallel",) * 2))(pos, pair, parts)


VEC_ROWS = 16


def _vector_sum(d, n_conv, parts, deps=()):
    def body(*refs):
        dg1, dba, dbb, dcw, dcb, dps, dg2, dgf, lc = refs[:9]
        tot_ref, snd, got, ssem, rsem = refs[9 + len(deps):]
        x, y, c = _mesh_pos()
        me = 4 * x + 2 * y + c
        snd[...] = jnp.zeros_like(snd)
        for row, ref in ((0, dg1), (1, dba), (2, dbb), (3, dps), (4, dg2), (5, dgf), (6, lc)):
            snd[row:row + 1, :] = ref[...]
        snd[7:8, :n_conv] = dcb[...]
        snd[8:11, :n_conv] = dcw[...]
        cps = []
        for r in range(1, N_DEV):
            peer = tuple(1 - p if (r >> b) & 1 else p for p, b in ((x, 2), (y, 1), (c, 0)))
            cps.append(_remote(snd, got.at[me], ssem.at[r - 1], rsem.at[r - 1], peer))
        for cp in cps:
            cp.start()
        got[me] = snd[...]
        for r in range(1, N_DEV):
            peer = tuple(1 - p if (r >> b) & 1 else p for p, b in ((x, 2), (y, 1), (c, 0)))
            _remote(snd, got.at[4 * peer[0] + 2 * peer[1] + peer[2]], ssem.at[r - 1], rsem.at[r - 1], peer).wait_recv()
        for cp in cps:
            cp.wait_send()
        tot = got[0]
        for dev in range(1, N_DEV):
            tot = tot + got[dev]
        tot_ref[...] = tot

    return pl.pallas_call(
        body, name="vector_params_sum", in_specs=[VMEM] * len(parts) + [ANY] * len(deps), out_specs=VMEM,
        out_shape=jax.ShapeDtypeStruct((VEC_ROWS, d), F32),
        scratch_shapes=[pltpu.VMEM((VEC_ROWS, d), F32), pltpu.VMEM((N_DEV, VEC_ROWS, d), F32),
                        pltpu.SemaphoreType.DMA((N_DEV - 1,)), pltpu.SemaphoreType.DMA((N_DEV - 1,))],
        compiler_params=pltpu.CompilerParams(vmem_limit_bytes=VMEM_LIMIT),
    )(*parts, *deps)


def _vector_update(tot, n_conv, params):
    ncw = params[2][0].shape[1]
    n_par = len(params)

    def body(*refs):
        tot = refs[0][...]
        wmv = refs[1:1 + 3 * n_par]
        outs = refs[1 + 3 * n_par:1 + 7 * n_par]
        refs[1 + 7 * n_par][...] = jnp.sum(tot[6:7, :], axis=1, keepdims=True)
        k_me = 2 * lax.axis_index("x") + lax.axis_index("y")
        g_cw = jnp.zeros((3, ncw), F32)
        for k in range(N_CHIPS):
            g_cw = g_cw + jnp.where(k_me == k, tot[8:11, k * ncw:(k + 1) * ncw], 0.0)
        grads = [tot[0:1, :], jnp.concatenate([tot[1:2, :], tot[2:3, :]], axis=1), g_cw, tot[7:8, :n_conv],
                 tot[3:4, :], tot[4:5, :], tot[5:6, :]]
        for i, g in enumerate(grads):
            w_ref, m_ref, v_ref = wmv[3 * i:3 * i + 3]
            delta, nm, nv = _adamw_math(w_ref[...], g, m_ref[...], v_ref[...])
            outs[4 * i][...] = g
            outs[4 * i + 1][...] = delta
            outs[4 * i + 2][...] = nm
            outs[4 * i + 3][...] = nv

    args = [tot]
    out_shape = []
    for w, m, v in params:
        args += [w, m, v]
        out_shape += [jax.ShapeDtypeStruct(w.shape, F32)] * 4
    out_shape.append(jax.ShapeDtypeStruct((1, 1), F32))
    return pl.pallas_call(body, name="vector_params_update", in_specs=[VMEM] * len(args), out_specs=[VMEM] * len(out_shape),
                          out_shape=out_shape, compiler_params=pltpu.CompilerParams(vmem_limit_bytes=VMEM_LIMIT))(*args)


def kernel(x, norm1_g, w_in, b_gate, conv_w, conv_b, w_a_out, w_pool, pool_scale, w_o, norm2_g, w_ffn_gate, w_ffn_up, w_ffn_down, final_g, loss_target, m_norm1_g, m_w_in, m_b_gate, m_conv_w, m_conv_b, m_w_a_out, m_w_pool, m_pool_scale, m_w_o, m_norm2_g, m_w_ffn_gate, m_w_ffn_up, m_w_ffn_down, m_final_g, v_norm1_g, v_w_in, v_b_gate, v_conv_w, v_conv_b, v_w_a_out, v_w_pool, v_pool_scale, v_w_o, v_norm2_g, v_w_ffn_gate, v_w_ffn_up, v_w_ffn_down, v_final_g):
    t, d = x.shape[1], x.shape[2]
    n_conv = conv_b.shape[1]
    n_groups, pool_cg, pool_dg = w_pool.shape[1], w_pool.shape[2], N_CHIPS * w_pool.shape[3]
    d_ff = N_CHIPS * w_ffn_gate.shape[2]
    assert n_conv // n_groups == pool_cg and n_conv % (n_groups * MIX_COLS) == 0 and n_groups == len(POOL_WINDOWS)

    big = {"w_in": (w_in, m_w_in, v_w_in), "w_a_out": (w_a_out, m_w_a_out, v_w_a_out), "w_pool": (w_pool, m_w_pool, v_w_pool),
           "w_o": (w_o, m_w_o, v_w_o), "w_ffn_gate": (w_ffn_gate, m_w_ffn_gate, v_w_ffn_gate),
           "w_ffn_up": (w_ffn_up, m_w_ffn_up, v_w_ffn_up), "w_ffn_down": (w_ffn_down, m_w_ffn_down, v_w_ffn_down)}
    colshard = {"w_in": True, "w_a_out": True, "w_pool": True, "w_o": False, "w_ffn_gate": True, "w_ffn_up": True,
                "w_ffn_down": False}
    names = list(big)
    shard2d = {n: big[n][0].reshape(-1, big[n][0].shape[-1]) for n in names}
    ws = [_Weight(n, *shard2d[n].shape, colshard[n]) for n in names]

    xs, tgt = x[0], loss_target[0]
    cw_loc = conv_w[0]
    pos = jnp.stack([lax.axis_index("c"), 2 * lax.axis_index("x") + lax.axis_index("y")]).astype(jnp.int32)
    by_name = {w.name: w for w in ws}
    groups = [[by_name[n] for n in g] for g in (["w_in"], ["w_a_out", "w_pool", "w_o"], ["w_ffn_gate"], ["w_ffn_up"],
                                                 ["w_ffn_down"])]
    rgroups = [groups[0], groups[1], groups[2] + groups[3], groups[4]]

    cast = lambda w, dep: _cast_place(f"cast_{w.name}", w, pos, shard2d[w.name].reshape(2, w.R, w.nn), deps=dep)
    chips, ks = _other_chips(lax.axis_index("x"), lax.axis_index("y"))
    kvec = jnp.stack([pos[1], *ks]).astype(jnp.int32)
    full = {}

    def start(name, arrays, copies, after=()):
        return start_many([(name, arrays, copies)], after)[0]

    def start_many(parts, after=()):
        arrays, sets = [], []
        for _, arrs, copies in parts:
            for a in arrs:
                if not any(a is b for b in arrays):
                    arrays.append(a)
            sets.append(([next(i for i, b in enumerate(arrays) if b is a) for a in arrs], copies))
        sems, thru, token = _split_start("_".join(p[0] for p in parts), arrays, sets, after)
        return [(name, [thru[i] for i in idx], ssem, rsem, copies, token)
                for (name, _, copies), (idx, _), (ssem, rsem) in zip(parts, sets, sems)]

    def wait(started, after):
        name, arrays, ssem, rsem, copies, _ = started
        return _split_wait(name + "_wait", arrays, ssem, rsem, copies, after)

    def pass_on(g, got, after=()):
        return start(f"pass_{g}", got, _pass_copies(groups[g]), after)

    def passed(g, st, after=None):
        got = wait(st, [st[5]] if after is None else after)
        full.update({w.name: a.reshape(w.P * 2 * w.R, w.N) for w, a in zip(groups[g], got)})

    near = start("near_0", [cast(w, []) for w in groups[0]], _near_copies(groups[0]))
    rest = [cast(w, [near[5]]) for grp in groups[1:] for w in grp]
    h1 = _rms_fwd("norm1_fwd", xs, norm1_g, deps=[near[5]])
    proj = _proj_piece("proj_own", h1, shard2d["w_in"], None, kvec, 0, 1, deps=rest)
    got = wait(near, [proj])
    far, small, st = start_many([("far_0", got, _far_copies(groups[0])), ("direct_1", rest[:3], _direct_copies(groups[1])),
                                 ("pass_near_0", got, _pass_copies(groups[0], (0, 1)))])
    got = wait(st, [st[5]])
    proj = _proj_piece("proj_near", h1, got[0].reshape(-1, groups[0][0].N), proj, kvec, 1, 2)
    st = start("pass_far_0", wait((far[0], got) + far[2:], [proj]), _pass_copies(groups[0], (2,)))
    got = wait(st, [st[5]])
    w_in_full = got[0].reshape(-1, groups[0][0].N)
    proj = _proj_piece("proj_far", h1, w_in_full, proj, kvec, 3, 1)
    cw_full, proj = _gather_conv_w(cw_loc, proj)
    got = wait(small, [proj])
    near_g = start("near_2", rest[3:4], _near_copies(groups[2]), got)
    st = pass_on(1, got, [near_g[5]])
    z, p = _mixer_fwd("mixer_fwd", proj, cw_full, conv_b, n_conv, n_groups, deps=[st[5]])
    passed(1, st, [z])
    wp_full = full["w_pool"].reshape(n_groups, pool_cg, pool_dg)
    yb = _gmm_nn("pool_out", p, wp_full, BF16)
    ya, merged = _conv_out_merge("conv_out_merge", z, full["w_a_out"], proj, b_gate, yb, pool_scale)
    far_g, near_u = start_many([("far_2", wait(near_g, [merged]), _far_copies(groups[2])),
                                ("near_3", rest[4:5], _near_copies(groups[3]))])
    x2 = _mm_nn("mix_out", merged, full["w_o"], F32, add=xs, deps=[near_u[5]])
    st = pass_on(2, wait(far_g, [x2]))
    h2 = _rms_fwd("norm2_fwd", x2, norm2_g, deps=[st[5]])
    passed(2, st, [h2])
    def far_and_pass(g, near_st, after, more=()):
        got = wait(near_st, after)
        return start_many([(f"far_{g}", got, _far_copies(groups[g])), (f"pass_near_{g}", got, _pass_copies(groups[g], (0, 1))),
                           *more])

    def finish(g, far_st, pass_st, after):
        got = wait(pass_st, after)
        st = start(f"pass_far_{g}", wait((far_st[0], got) + far_st[2:], after), _pass_copies(groups[g], (2,)))
        passed(g, st)

    gate = _mm_nn("ffn_gate_a", h2, full["w_ffn_gate"], BF16, part=(0, 2))
    far_u, pass_u, near_d = far_and_pass(3, near_u, [gate], [("near_4", rest[5:6], _near_copies(groups[4]))])
    gate = _mm_nn("ffn_gate_b", h2, full["w_ffn_gate"], BF16, part=(1, 2), prev=gate, deps=[near_d[5]])
    finish(3, far_u, pass_u, [gate])
    up_act = _ffn_up_act("ffn_up_act_a", h2, full["w_ffn_up"], gate, part=(0, 2))
    far_d, pass_d = far_and_pass(4, near_d, [up_act[0]])
    up, act = _ffn_up_act("ffn_up_act_b", h2, full["w_ffn_up"], gate, part=(1, 2), prev=up_act, deps=[pass_d[5]])
    finish(4, far_d, pass_d, [act])
    x3 = _mm_nn("ffn_down", act, full["w_ffn_down"], F32, add=x2, tiles=(1024, 512))

    pending = {}

    def pair_start(g, grads):
        grp = rgroups[g]
        gcan = [grads[w.name].reshape(w.P, 2, w.R, w.N) for w in grp]
        slots = [lax.empty((w.P, w.R, w.N), BF16) for w in grp]
        pending[g] = start(f"pair_start_{g}", gcan + slots, _pair_copies(len(grp)))
        return pending[g][5]

    def scatter_start(g, after):
        grp = rgroups[g]
        n = len(grp)
        arrs = wait(pending[g], after)
        gcan, sib = arrs[:n], arrs[n:]
        pairs = [_pair_sum(f"pair_sum_{w.name}", w, pos, a, s) for w, a, s in zip(grp, gcan, sib)]
        ssem, rsem, pairs, slots, token = _scatter_start(f"scatter_start_{g}", grp, pairs)
        pending[g] = (gcan, sib, pairs, slots, ssem, rsem)
        return token

    def pair_start_halves(g, ab, deps):
        grp = rgroups[g]
        sent = [_mm_tn_half(f"d{w.name}_sib", a, b, pos, False, deps=deps if i == 0 else ()) for i, (w, (a, b)) in enumerate(zip(grp, ab))]
        slots = [lax.empty((1, w.R, w.N), BF16) for w in grp]
        pending[g] = start(f"pair_start_{g}", sent + slots, _pair_copies(len(grp), whole=True))
        return pending[g][5]

    def scatter_start_halves(g, ab, after):
        grp = rgroups[g]
        n = len(grp)
        arrs = wait(pending[g], after)
        pairs = [_mm_tn_half(f"d{w.name}_own", a, b, pos, True, add=s) for w, (a, b), s in zip(grp, ab, arrs[n:])]
        ssem, rsem, pairs, slots, token = _scatter_start(f"scatter_start_{g}", grp, pairs)
        pending[g] = (None, None, pairs, slots, ssem, rsem)
        return token

    def reduce_finish(g, after):
        grp = rgroups[g]
        gcan, sib, pairs, slots, ssem, rsem = pending[g]
        pairs, parts = _scatter_wait(f"scatter_wait_{g}", grp, pairs, slots, ssem, rsem, after)
        if gcan is None:
            return [_final_sum(f"final_sum_{w.name}", w, pos, a, None, q) for w, a, q in zip(grp, pairs, parts)]
        return [_final_sum(f"final_sum_{w.name}", w, pos, a, s, q) for w, a, s, q in zip(grp, gcan, sib, parts)]

    grads = {}
    dx3, dx3b, d_gf, loss_cols = _final_bwd("final_bwd", x3, final_g.reshape(1, d), tgt)
    dgate, dup = _ffn_bwd("ffn_bwd", dx3b, full["w_ffn_down"], gate, up)
    grads["w_ffn_down"] = _mm_tn("dw_ffn_down", act, dx3b, BF16)
    tok = pair_start(3, grads)
    dh2 = _mm_nt("d_h2", [(dgate, full["w_ffn_gate"]), (dup, full["w_ffn_up"])], BF16, tk=d_ff // 4, deps=[tok])
    tok = scatter_start(3, [dh2])
    tok = pair_start_halves(2, [(h2, dgate), (h2, dup)], [tok])
    dx2, dx2b, d_g2 = _rms_bwd("norm2_bwd", x2, norm2_g, dh2, dx3, True, deps=[tok])
    dmerged = _mm_nt("d_merged", [(dx2b, full["w_o"])], BF16, tk=d)
    grads["w_o"] = _mm_tn("dw_o", merged, dx2b, BF16)
    tok = scatter_start_halves(2, [(h2, dgate), (h2, dup)], [grads["w_o"]])
    dya, dyb, dproj, d_bga, d_bgb, d_ps = _merge_bwd("merge_bwd", dmerged, proj, b_gate, ya, yb, pool_scale, deps=[tok])
    dz = _mm_nt("d_z", [(dya, full["w_a_out"])], BF16, tk=d)
    grads["w_a_out"] = _mm_tn("dw_a_out", z, dya, BF16)
    dp = _gmm_nt("d_pool", dyb, wp_full, BF16)
    grads["w_pool"] = _gmm_tn("dw_pool", p, dyb, n_groups, BF16)
    tok = pair_start(1, grads)
    dproj, d_cw, d_cb = _mixer_bwd("mixer_bwd", dz, dp, proj, cw_full, conv_b, dproj, n_conv, n_groups, deps=[tok])
    tok = scatter_start(1, [dproj])
    tok = pair_start_halves(0, [(h1, dproj)], [tok])
    dh1 = _mm_nt("d_h1", [(dproj, w_in_full)], BF16, tk=proj.shape[1] // 4, deps=[tok])
    tok = scatter_start_halves(0, [(h1, dproj)], [dh1])
    grad_x, d_g1 = _rms_bwd("norm1_bwd", xs, norm1_g, dh1, dx2, False, deps=[tok])

    g_big, d_big, m_big, v_big = {}, {}, {}, {}

    def update(wsub, shared):
        out = []
        for w, g in zip(wsub, shared):
            wt, mt, vt = big[w.name]
            g2 = g.reshape(2 * w.R, w.nn)
            go, dl, nm, nv = _adamw(f"adamw_{w.name}", shard2d[w.name], g2, mt.reshape(g2.shape), vt.reshape(g2.shape))
            g_big[w.name], d_big[w.name], m_big[w.name], v_big[w.name] = (a.reshape(wt.shape) for a in (go, dl, nm, nv))
            out.append(nv)
        return out

    after = [grad_x]
    started = []
    for g in (3, 2, 1):
        halves = reduce_finish(g, after)
        started.append((g, start(f"share_{g}", halves, _share_copies(len(halves)))))
        after = [started[-1][1][5]]
    for g, st in started:
        after = update(rgroups[g], wait(st, after))
    st = start("share_0", reduce_finish(0, after), _share_copies(1))

    vec_names = ["norm1_g", "b_gate", "conv_w", "conv_b", "pool_scale", "norm2_g", "final_g"]
    vec = {"norm1_g": (norm1_g, m_norm1_g, v_norm1_g), "b_gate": (b_gate, m_b_gate, v_b_gate),
           "conv_w": (cw_loc, m_conv_w[0], v_conv_w[0]), "conv_b": (conv_b, m_conv_b, v_conv_b),
           "pool_scale": (pool_scale, m_pool_scale, v_pool_scale), "norm2_g": (norm2_g, m_norm2_g, v_norm2_g),
           "final_g": tuple(a.reshape(1, d) for a in (final_g, m_final_g, v_final_g))}
    tot = _vector_sum(d, n_conv, [d_g1, d_bga, d_bgb, d_cw, d_cb, d_ps, d_g2, d_gf, loss_cols], deps=st[1])
    vout = _vector_update(tot, n_conv, [vec[n] for n in vec_names])
    update(rgroups[0], wait(st, []))

    shapes = {"conv_w": conv_w.shape, "final_g": final_g.shape}
    g_vec, d_vec, m_vec, v_vec = ({n: vout[4 * i + q].reshape(shapes.get(n, vec[n][0].shape)) for i, n in enumerate(vec_names)}
                                  for q in range(4))
    loss = vout[-1].reshape(())

    order = ["norm1_g", "w_in", "b_gate", "conv_w", "conv_b", "w_a_out", "w_pool", "pool_scale", "w_o", "norm2_g",
             "w_ffn_gate", "w_ffn_up", "w_ffn_down", "final_g"]
    pick = lambda vecs, bigs: [vecs[n] if n in vecs else bigs[n] for n in order]
    return (loss, grad_x.reshape(x.shape), *pick(g_vec, g_big), *pick(d_vec, d_big), *pick(m_vec, m_big),
            *pick(v_vec, v_big))
```
